```python
import jax, jax.numpy as jnp
from jax import lax
import numpy as np

D_MODEL = 1024
BATCH = 16
SEQ = 2048
DEPTH = 1

CHUNK = 64
Q_BLOCK = 128
EPS = 1e-6
GLA_HEADS = 4
GLA_DK = 128
GLA_DV = 256
GLA_LOWRANK = 16
GLA_TAU = 16.0
MLA_HEADS = 16
MLA_Q_RANK = 256
MLA_KV_RANK = 128
MLA_NOPE = 64
MLA_ROPE = 32
MLA_V = 64
ROPE_THETA = 10000.0
D_FF = 4 * D_MODEL
N_BRANCH = 2
IN_SPLITS = (GLA_HEADS * GLA_DK, GLA_HEADS * GLA_DK, GLA_HEADS * GLA_DV, GLA_HEADS * GLA_DV,
             GLA_LOWRANK, MLA_Q_RANK, MLA_KV_RANK, MLA_ROPE, N_BRANCH * D_MODEL)
IN_WIDTH = sum(IN_SPLITS)

kernel_name = "hybrid_gla_mla_sqrelu_adaln_block"


def rms_norm(x, g):
    xf = x.astype(jnp.float32)
    y = xf * lax.rsqrt(jnp.mean(xf * xf, axis=-1, keepdims=True) + EPS)
    return (y * g.astype(jnp.float32)).astype(x.dtype)


def modulate(h, shift, scale):
    return h * (1.0 + scale[:, None, :]) + shift[:, None, :]


def rope(x, positions):
    r = x.shape[-1]
    freqs = ROPE_THETA ** (-jnp.arange(0, r, 2, dtype=jnp.float32) / r)
    ang = positions.astype(jnp.float32)[..., None] * freqs
    cos = jnp.cos(ang)[:, :, None, :]
    sin = jnp.sin(ang)[:, :, None, :]
    xf = x.astype(jnp.float32)
    x1, x2 = xf[..., : r // 2], xf[..., r // 2:]
    return jnp.concatenate([x1 * cos - x2 * sin, x2 * cos + x1 * sin], axis=-1).astype(x.dtype)


def gla_branch(q, k, v, g, a_lr, w_alpha, b_alpha, out_norm_g, w_o):
    b, s, _ = q.shape
    nc = s // CHUNK
    qc = q.reshape(b, nc, CHUNK, GLA_HEADS, GLA_DK) * (GLA_DK ** -0.5)
    kc = k.reshape(b, nc, CHUNK, GLA_HEADS, GLA_DK)
    vc = v.reshape(b, nc, CHUNK, GLA_HEADS, GLA_DV)
    log_a = jax.nn.log_sigmoid((a_lr @ w_alpha + b_alpha).astype(jnp.float32)) / GLA_TAU
    log_a = log_a.reshape(b, nc, CHUNK, GLA_HEADS, GLA_DK)
    cum = jnp.cumsum(log_a, axis=2)
    cum_end = cum[:, :, -1]
    k_dec = kc.astype(jnp.float32) * jnp.exp(cum_end[:, :, None] - cum)
    u = jnp.einsum('bnchk,bnchv->nbhkv', k_dec, vc.astype(jnp.float32))
    decay = jnp.transpose(jnp.exp(cum_end), (1, 0, 2, 3))

    def step(state, inp):
        d, uc = inp
        state = d[..., None] * state + uc
        return state, state

    s0 = jnp.zeros((b, GLA_HEADS, GLA_DK, GLA_DV), jnp.float32)
    _, states = lax.scan(step, s0, (decay, u))
    o = jnp.einsum('bnchk,nbhkv->bnchv', qc.astype(jnp.float32), states).astype(q.dtype)
    o = o.reshape(b, s, GLA_HEADS, GLA_DV)
    o = rms_norm(o, out_norm_g) * jax.nn.silu(g.reshape(b, s, GLA_HEADS, GLA_DV))
    return o.reshape(b, s, GLA_HEADS * GLA_DV) @ w_o


def chunk_causal_attention(q, k, v):
    b, s, h, dqk = q.shape
    dv = v.shape[-1]
    nb = s // Q_BLOCK
    scale = dqk ** -0.5
    qb = jnp.transpose(q.reshape(b, nb, Q_BLOCK, h, dqk), (1, 0, 3, 2, 4))
    key_chunk = jnp.arange(s) // CHUNK

    def one_block(args):
        qi, bi = args
        sc = jnp.einsum('bhqd,bkhd->bhqk', qi, k).astype(jnp.float32) * scale
        q_chunk = (bi * Q_BLOCK + jnp.arange(Q_BLOCK)) // CHUNK
        mask = key_chunk[None, :] <= q_chunk[:, None]
        sc = jnp.where(mask[None, None], sc, -jnp.inf)
        p = jax.nn.softmax(sc, axis=-1).astype(v.dtype)
        return jnp.einsum('bhqk,bkhd->bqhd', p, v)

    out = lax.map(one_block, (qb, jnp.arange(nb)))
    return jnp.transpose(out, (1, 0, 2, 3, 4)).reshape(b, s, h, dv)


def mla_branch(cq, ckv, kpe, positions, q_lat_g, w_uq, kv_lat_g, w_ukv, qn_g, kn_g, w_o):
    b, s, _ = cq.shape
    q = (rms_norm(cq, q_lat_g) @ w_uq).reshape(b, s, MLA_HEADS, MLA_NOPE + MLA_ROPE)
    kv = (rms_norm(ckv, kv_lat_g) @ w_ukv).reshape(b, s, MLA_HEADS, MLA_NOPE + MLA_V)
    k_nope, v = kv[..., :MLA_NOPE], kv[..., MLA_NOPE:]
    k_rope = jnp.broadcast_to(kpe[:, :, None, :], (b, s, MLA_HEADS, MLA_ROPE))
    k = jnp.concatenate([k_nope, k_rope], axis=-1)
    q = rms_norm(q, qn_g)
    k = rms_norm(k, kn_g)
    q = jnp.concatenate([q[..., :MLA_NOPE], rope(q[..., MLA_NOPE:], positions)], axis=-1)
    k = jnp.concatenate([k[..., :MLA_NOPE], rope(k[..., MLA_NOPE:], positions)], axis=-1)
    o = chunk_causal_attention(q, k, v)
    return o.reshape(b, s, MLA_HEADS * MLA_V) @ w_o


def _fwd_setup_inputs(seed: int = 0) -> dict:
    key = jax.random.key(seed)
    ks = jax.random.split(key, 24)
    f32 = jnp.float32

    def nrm(k, shape, scale):
        return jax.random.normal(k, shape, f32) * scale

    def gain(k, dim):
        return 1.0 + 0.02 * jax.random.normal(k, (DEPTH, dim), f32)

    L = DEPTH
    offsets = jax.random.randint(ks[2], (BATCH, 1), 0, 4096, dtype=jnp.int32)
    positions = offsets + jnp.arange(SEQ, dtype=jnp.int32)[None, :]
    return {
        "x": nrm(ks[0], (BATCH, SEQ, D_MODEL), 1.0),
        "c": nrm(ks[1], (BATCH, D_MODEL), 1.0),
        "positions": positions,
        "w_ada": nrm(ks[3], (L, D_MODEL, 6 * D_MODEL), 0.5 * D_MODEL ** -0.5),
        "b_ada": nrm(ks[4], (L, 6 * D_MODEL), 0.02),
        "norm1_g": gain(ks[5], D_MODEL),
        "w_in": nrm(ks[6], (L, D_MODEL, IN_WIDTH), D_MODEL ** -0.5),
        "b_merge": nrm(ks[7], (L, N_BRANCH * D_MODEL), 0.02),
        "gla_w_alpha": nrm(ks[8], (L, GLA_LOWRANK, GLA_HEADS * GLA_DK), GLA_LOWRANK ** -0.5),
        "gla_b_alpha": nrm(ks[9], (L, GLA_HEADS * GLA_DK), 0.1),
        "gla_out_norm_g": gain(ks[10], GLA_DV),
        "gla_w_o": nrm(ks[11], (L, GLA_HEADS * GLA_DV, D_MODEL), (GLA_HEADS * GLA_DV) ** -0.5),
        "mla_q_lat_g": gain(ks[12], MLA_Q_RANK),
        "mla_w_uq": nrm(ks[13], (L, MLA_Q_RANK, MLA_HEADS * (MLA_NOPE + MLA_ROPE)), MLA_Q_RANK ** -0.5),
        "mla_kv_lat_g": gain(ks[14], MLA_KV_RANK),
        "mla_w_ukv": nrm(ks[15], (L, MLA_KV_RANK, MLA_HEADS * (MLA_NOPE + MLA_V)), MLA_KV_RANK ** -0.5),
        "mla_qn_g": gain(ks[16], MLA_NOPE + MLA_ROPE),
        "mla_kn_g": gain(ks[17], MLA_NOPE + MLA_ROPE),
        "mla_w_o": nrm(ks[18], (L, MLA_HEADS * MLA_V, D_MODEL), (MLA_HEADS * MLA_V) ** -0.5),
        "w_out": nrm(ks[19], (L, D_MODEL, D_MODEL), D_MODEL ** -0.5),
        "norm2_g": gain(ks[20], D_MODEL),
        "mlp_w1": nrm(ks[21], (L, D_MODEL, D_FF), D_MODEL ** -0.5),
        "mlp_w2": nrm(ks[22], (L, D_FF, D_MODEL), D_FF ** -0.5),
    }


def _fwd_reference(x, c, positions, w_ada, b_ada, norm1_g, w_in, b_merge, gla_w_alpha, gla_b_alpha,
              gla_out_norm_g, gla_w_o, mla_q_lat_g, mla_w_uq, mla_kv_lat_g, mla_w_ukv,
              mla_qn_g, mla_kn_g, mla_w_o, w_out, norm2_g, mlp_w1, mlp_w2):
    split_at = np.cumsum(IN_SPLITS)[:-1].tolist()
    c_act = jax.nn.silu(c)
    for l in range(DEPTH):
        mod = c_act @ w_ada[l] + b_ada[l]
        shift1, scale1, gate1, shift2, scale2, gate2 = jnp.split(mod, 6, axis=-1)

        h = modulate(rms_norm(x, norm1_g[l]), shift1, scale1)
        proj = h @ w_in[l]
        g_q, g_k, g_v, g_g, g_a, m_cq, m_ckv, m_kpe, merge_logits = jnp.split(proj, split_at, axis=-1)
        y_a = gla_branch(g_q, g_k, g_v, g_g, g_a, gla_w_alpha[l], gla_b_alpha[l],
                         gla_out_norm_g[l], gla_w_o[l])
        y_b = mla_branch(m_cq, m_ckv, m_kpe, positions, mla_q_lat_g[l], mla_w_uq[l],
                         mla_kv_lat_g[l], mla_w_ukv[l], mla_qn_g[l], mla_kn_g[l], mla_w_o[l])
        gates = jax.nn.sigmoid(merge_logits + b_merge[l])
        gate_a, gate_b = gates[..., :D_MODEL], gates[..., D_MODEL:]
        mixed = (gate_a * y_a + gate_b * y_b) @ w_out[l]
        x = x + gate1[:, None, :] * mixed

        h2 = modulate(rms_norm(x, norm2_g[l]), shift2, scale2)
        ff = jnp.square(jax.nn.relu(h2 @ mlp_w1[l])) @ mlp_w2[l]
        x = x + gate2[:, None, :] * ff
    return x


import jax as _jax
import jax.numpy as _jnp

TWIN_FORMAT = 'train_step'
FWD_PARAMS = ['x', 'c', 'positions', 'w_ada', 'b_ada', 'norm1_g', 'w_in', 'b_merge', 'gla_w_alpha', 'gla_b_alpha', 'gla_out_norm_g', 'gla_w_o', 'mla_q_lat_g', 'mla_w_uq', 'mla_kv_lat_g', 'mla_w_ukv', 'mla_qn_g', 'mla_kn_g', 'mla_w_o', 'w_out', 'norm2_g', 'mlp_w1', 'mlp_w2']
TWIN_WEIGHTS = ['w_ada', 'b_ada', 'norm1_g', 'w_in', 'b_merge', 'gla_w_alpha', 'gla_b_alpha', 'gla_out_norm_g', 'gla_w_o', 'mla_q_lat_g', 'mla_w_uq', 'mla_kv_lat_g', 'mla_w_ukv', 'mla_qn_g', 'mla_kn_g', 'mla_w_o', 'w_out', 'norm2_g', 'mlp_w1', 'mlp_w2']
TWIN_DIFF_INPUT = 'x'
TWIN_INPUTS = ['x', 'c', 'positions', 'w_ada', 'b_ada', 'norm1_g', 'w_in', 'b_merge', 'gla_w_alpha', 'gla_b_alpha', 'gla_out_norm_g', 'gla_w_o', 'mla_q_lat_g', 'mla_w_uq', 'mla_kv_lat_g', 'mla_w_ukv', 'mla_qn_g', 'mla_kn_g', 'mla_w_o', 'w_out', 'norm2_g', 'mlp_w1', 'mlp_w2', 'loss_target', 'm_w_ada', 'm_b_ada', 'm_norm1_g', 'm_w_in', 'm_b_merge', 'm_gla_w_alpha', 'm_gla_b_alpha', 'm_gla_out_norm_g', 'm_gla_w_o', 'm_mla_q_lat_g', 'm_mla_w_uq', 'm_mla_kv_lat_g', 'm_mla_w_ukv', 'm_mla_qn_g', 'm_mla_kn_g', 'm_mla_w_o', 'm_w_out', 'm_norm2_g', 'm_mlp_w1', 'm_mlp_w2', 'v_w_ada', 'v_b_ada', 'v_norm1_g', 'v_w_in', 'v_b_merge', 'v_gla_w_alpha', 'v_gla_b_alpha', 'v_gla_out_norm_g', 'v_gla_w_o', 'v_mla_q_lat_g', 'v_mla_w_uq', 'v_mla_kv_lat_g', 'v_mla_w_ukv', 'v_mla_qn_g', 'v_mla_kn_g', 'v_mla_w_o', 'v_w_out', 'v_norm2_g', 'v_mlp_w1', 'v_mlp_w2']
TWIN_OUTPUTS = ['loss', 'grad_x', 'grad_w_ada', 'grad_b_ada', 'grad_norm1_g', 'grad_w_in', 'grad_b_merge', 'grad_gla_w_alpha', 'grad_gla_b_alpha', 'grad_gla_out_norm_g', 'grad_gla_w_o', 'grad_mla_q_lat_g', 'grad_mla_w_uq', 'grad_mla_kv_lat_g', 'grad_mla_w_ukv', 'grad_mla_qn_g', 'grad_mla_kn_g', 'grad_mla_w_o', 'grad_w_out', 'grad_norm2_g', 'grad_mlp_w1', 'grad_mlp_w2', 'delta_w_ada', 'delta_b_ada', 'delta_norm1_g', 'delta_w_in', 'delta_b_merge', 'delta_gla_w_alpha', 'delta_gla_b_alpha', 'delta_gla_out_norm_g', 'delta_gla_w_o', 'delta_mla_q_lat_g', 'delta_mla_w_uq', 'delta_mla_kv_lat_g', 'delta_mla_w_ukv', 'delta_mla_qn_g', 'delta_mla_kn_g', 'delta_mla_w_o', 'delta_w_out', 'delta_norm2_g', 'delta_mlp_w1', 'delta_mlp_w2', 'new_m_w_ada', 'new_m_b_ada', 'new_m_norm1_g', 'new_m_w_in', 'new_m_b_merge', 'new_m_gla_w_alpha', 'new_m_gla_b_alpha', 'new_m_gla_out_norm_g', 'new_m_gla_w_o', 'new_m_mla_q_lat_g', 'new_m_mla_w_uq', 'new_m_mla_kv_lat_g', 'new_m_mla_w_ukv', 'new_m_mla_qn_g', 'new_m_mla_kn_g', 'new_m_mla_w_o', 'new_m_w_out', 'new_m_norm2_g', 'new_m_mlp_w1', 'new_m_mlp_w2', 'new_v_w_ada', 'new_v_b_ada', 'new_v_norm1_g', 'new_v_w_in', 'new_v_b_merge', 'new_v_gla_w_alpha', 'new_v_gla_b_alpha', 'new_v_gla_out_norm_g', 'new_v_gla_w_o', 'new_v_mla_q_lat_g', 'new_v_mla_w_uq', 'new_v_mla_kv_lat_g', 'new_v_mla_w_ukv', 'new_v_mla_qn_g', 'new_v_mla_kn_g', 'new_v_mla_w_o', 'new_v_w_out', 'new_v_norm2_g', 'new_v_mlp_w1', 'new_v_mlp_w2']
TWIN_LEAF_KINDS = {'loss': 'loss', 'grad_x': 'grad_x', 'grad_w_ada': 'grad_w', 'grad_b_ada': 'grad_w', 'grad_norm1_g': 'grad_w', 'grad_w_in': 'grad_w', 'grad_b_merge': 'grad_w', 'grad_gla_w_alpha': 'grad_w', 'grad_gla_b_alpha': 'grad_w', 'grad_gla_out_norm_g': 'grad_w', 'grad_gla_w_o': 'grad_w', 'grad_mla_q_lat_g': 'grad_w', 'grad_mla_w_uq': 'grad_w', 'grad_mla_kv_lat_g': 'grad_w', 'grad_mla_w_ukv': 'grad_w', 'grad_mla_qn_g': 'grad_w', 'grad_mla_kn_g': 'grad_w', 'grad_mla_w_o': 'grad_w', 'grad_w_out': 'grad_w', 'grad_norm2_g': 'grad_w', 'grad_mlp_w1': 'grad_w', 'grad_mlp_w2': 'grad_w', 'delta_w_ada': 'delta_w', 'delta_b_ada': 'delta_w', 'delta_norm1_g': 'delta_w', 'delta_w_in': 'delta_w', 'delta_b_merge': 'delta_w', 'delta_gla_w_alpha': 'delta_w', 'delta_gla_b_alpha': 'delta_w', 'delta_gla_out_norm_g': 'delta_w', 'delta_gla_w_o': 'delta_w', 'delta_mla_q_lat_g': 'delta_w', 'delta_mla_w_uq': 'delta_w', 'delta_mla_kv_lat_g': 'delta_w', 'delta_mla_w_ukv': 'delta_w', 'delta_mla_qn_g': 'delta_w', 'delta_mla_kn_g': 'delta_w', 'delta_mla_w_o': 'delta_w', 'delta_w_out': 'delta_w', 'delta_norm2_g': 'delta_w', 'delta_mlp_w1': 'delta_w', 'delta_mlp_w2': 'delta_w', 'new_m_w_ada': 'new_m', 'new_m_b_ada': 'new_m', 'new_m_norm1_g': 'new_m', 'new_m_w_in': 'new_m', 'new_m_b_merge': 'new_m', 'new_m_gla_w_alpha': 'new_m', 'new_m_gla_b_alpha': 'new_m', 'new_m_gla_out_norm_g': 'new_m', 'new_m_gla_w_o': 'new_m', 'new_m_mla_q_lat_g': 'new_m', 'new_m_mla_w_uq': 'new_m', 'new_m_mla_kv_lat_g': 'new_m', 'new_m_mla_w_ukv': 'new_m', 'new_m_mla_qn_g': 'new_m', 'new_m_mla_kn_g': 'new_m', 'new_m_mla_w_o': 'new_m', 'new_m_w_out': 'new_m', 'new_m_norm2_g': 'new_m', 'new_m_mlp_w1': 'new_m', 'new_m_mlp_w2': 'new_m', 'new_v_w_ada': 'new_v', 'new_v_b_ada': 'new_v', 'new_v_norm1_g': 'new_v', 'new_v_w_in': 'new_v', 'new_v_b_merge': 'new_v', 'new_v_gla_w_alpha': 'new_v', 'new_v_gla_b_alpha': 'new_v', 'new_v_gla_out_norm_g': 'new_v', 'new_v_gla_w_o': 'new_v', 'new_v_mla_q_lat_g': 'new_v', 'new_v_mla_w_uq': 'new_v', 'new_v_mla_kv_lat_g': 'new_v', 'new_v_mla_w_ukv': 'new_v', 'new_v_mla_qn_g': 'new_v', 'new_v_mla_kn_g': 'new_v', 'new_v_mla_w_o': 'new_v', 'new_v_w_out': 'new_v', 'new_v_norm2_g': 'new_v', 'new_v_mlp_w1': 'new_v', 'new_v_mlp_w2': 'new_v'}


def _forward(args):
    return _fwd_reference(*[args[k] for k in FWD_PARAMS])


def _output_shape():
    out = _jax.eval_shape(lambda: _forward(_fwd_setup_inputs(0)))
    return out.shape, out.dtype

N_MICROBATCH = 1
ADAM_LR = 0.001
ADAM_B1 = 0.9
ADAM_B2 = 0.999
ADAM_EPS = 1e-08
ADAM_WD = 0.01
ADAM_STEP = 10
PER_EXAMPLE_BATCH_AXIS = {'x': 0, 'c': 0, 'positions': 0, 'loss_target': 0}
SHARED_INPUTS = []
_WEIGHT_DTYPES = {'w_ada': _jnp.float32, 'b_ada': _jnp.float32, 'norm1_g': _jnp.float32, 'w_in': _jnp.float32, 'b_merge': _jnp.float32, 'gla_w_alpha': _jnp.float32, 'gla_b_alpha': _jnp.float32, 'gla_out_norm_g': _jnp.float32, 'gla_w_o': _jnp.float32, 'mla_q_lat_g': _jnp.float32, 'mla_w_uq': _jnp.float32, 'mla_kv_lat_g': _jnp.float32, 'mla_w_ukv': _jnp.float32, 'mla_qn_g': _jnp.float32, 'mla_kn_g': _jnp.float32, 'mla_w_o': _jnp.float32, 'w_out': _jnp.float32, 'norm2_g': _jnp.float32, 'mlp_w1': _jnp.float32, 'mlp_w2': _jnp.float32}
MOMENT_SCALE = {'w_ada': 3.431970e+00, 'b_ada': 7.334199e+00, 'norm1_g': 3.604590e-01, 'w_in': 8.212571e-02, 'b_merge': 1.103429e-01, 'gla_w_alpha': 1.051063e-02, 'gla_b_alpha': 2.830717e-02, 'gla_out_norm_g': 1.698089e+00, 'gla_w_o': 4.086497e-02, 'mla_q_lat_g': 1.852204e-02, 'mla_w_uq': 7.008715e-03, 'mla_kv_lat_g': 1.194329e+00, 'mla_w_ukv': 1.468251e-01, 'mla_qn_g': 4.674819e-02, 'mla_kn_g': 4.762609e-02, 'mla_w_o': 1.989667e-01, 'w_out': 1.645254e-01, 'norm2_g': 1.284853e+01, 'mlp_w1': 3.563250e-01, 'mlp_w2': 1.415754e+00}


def _to_microbatches(a, axis):
    t = _jnp.moveaxis(a, axis, 0)
    t = t.reshape((N_MICROBATCH, t.shape[0] // N_MICROBATCH) + t.shape[1:])
    return _jnp.moveaxis(t, 1, axis + 1)


def setup_inputs(seed: int = 0) -> dict:
    inp = _fwd_setup_inputs(seed)
    key = _jax.random.fold_in(_jax.random.key(seed), 7919)
    shape, _ = _output_shape()
    out = dict(inp)
    out["loss_target"] = _jax.random.normal(_jax.random.fold_in(key, 0), shape, _jnp.float32)
    for i, name in enumerate(TWIN_WEIGHTS):
        w = inp[name].astype(_jnp.float32)
        if MOMENT_SCALE is None:
            s = _jnp.sqrt(_jnp.mean(_jnp.square(w)) + 1e-30)
        else:
            s = MOMENT_SCALE[name]
        km, kv = _jax.random.split(_jax.random.fold_in(key, i + 1))
        out[name] = w
        out["m_" + name] = s * _jax.random.normal(km, w.shape, _jnp.float32)
        out["v_" + name] = (s * s) * _jax.random.uniform(kv, w.shape, _jnp.float32, 0.5, 1.5)
    if N_MICROBATCH > 1:
        for name, axis in PER_EXAMPLE_BATCH_AXIS.items():
            out[name] = _to_microbatches(out[name], axis)
    return {'x': out['x'], 'c': out['c'], 'positions': out['positions'], 'w_ada': out['w_ada'], 'b_ada': out['b_ada'], 'norm1_g': out['norm1_g'], 'w_in': out['w_in'], 'b_merge': out['b_merge'], 'gla_w_alpha': out['gla_w_alpha'], 'gla_b_alpha': out['gla_b_alpha'], 'gla_out_norm_g': out['gla_out_norm_g'], 'gla_w_o': out['gla_w_o'], 'mla_q_lat_g': out['mla_q_lat_g'], 'mla_w_uq': out['mla_w_uq'], 'mla_kv_lat_g': out['mla_kv_lat_g'], 'mla_w_ukv': out['mla_w_ukv'], 'mla_qn_g': out['mla_qn_g'], 'mla_kn_g': out['mla_kn_g'], 'mla_w_o': out['mla_w_o'], 'w_out': out['w_out'], 'norm2_g': out['norm2_g'], 'mlp_w1': out['mlp_w1'], 'mlp_w2': out['mlp_w2'], 'loss_target': out['loss_target'], 'm_w_ada': out['m_w_ada'], 'm_b_ada': out['m_b_ada'], 'm_norm1_g': out['m_norm1_g'], 'm_w_in': out['m_w_in'], 'm_b_merge': out['m_b_merge'], 'm_gla_w_alpha': out['m_gla_w_alpha'], 'm_gla_b_alpha': out['m_gla_b_alpha'], 'm_gla_out_norm_g': out['m_gla_out_norm_g'], 'm_gla_w_o': out['m_gla_w_o'], 'm_mla_q_lat_g': out['m_mla_q_lat_g'], 'm_mla_w_uq': out['m_mla_w_uq'], 'm_mla_kv_lat_g': out['m_mla_kv_lat_g'], 'm_mla_w_ukv': out['m_mla_w_ukv'], 'm_mla_qn_g': out['m_mla_qn_g'], 'm_mla_kn_g': out['m_mla_kn_g'], 'm_mla_w_o': out['m_mla_w_o'], 'm_w_out': out['m_w_out'], 'm_norm2_g': out['m_norm2_g'], 'm_mlp_w1': out['m_mlp_w1'], 'm_mlp_w2': out['m_mlp_w2'], 'v_w_ada': out['v_w_ada'], 'v_b_ada': out['v_b_ada'], 'v_norm1_g': out['v_norm1_g'], 'v_w_in': out['v_w_in'], 'v_b_merge': out['v_b_merge'], 'v_gla_w_alpha': out['v_gla_w_alpha'], 'v_gla_b_alpha': out['v_gla_b_alpha'], 'v_gla_out_norm_g': out['v_gla_out_norm_g'], 'v_gla_w_o': out['v_gla_w_o'], 'v_mla_q_lat_g': out['v_mla_q_lat_g'], 'v_mla_w_uq': out['v_mla_w_uq'], 'v_mla_kv_lat_g': out['v_mla_kv_lat_g'], 'v_mla_w_ukv': out['v_mla_w_ukv'], 'v_mla_qn_g': out['v_mla_qn_g'], 'v_mla_kn_g': out['v_mla_kn_g'], 'v_mla_w_o': out['v_mla_w_o'], 'v_w_out': out['v_w_out'], 'v_norm2_g': out['v_norm2_g'], 'v_mlp_w1': out['v_mlp_w1'], 'v_mlp_w2': out['v_mlp_w2']}


def _loss(weights, diff, rest, loss_target):
    with _jax.named_scope("forward"):
        args = {**rest, TWIN_DIFF_INPUT: diff, **{k: w.astype(_WEIGHT_DTYPES[k]) for k, w in weights.items()}}
        y = _forward(args)
    with _jax.named_scope("loss_head"):
        err = _jnp.square(y.astype(_jnp.float32) - loss_target)
        return 0.5 * _jnp.sum(_jnp.mean(err, axis=-1)) if err.ndim else 0.5 * err


def _adamw(w, g, m, v):
    m = ADAM_B1 * m + (1.0 - ADAM_B1) * g
    v = ADAM_B2 * v + (1.0 - ADAM_B2) * _jnp.square(g)
    m_hat = m / (1.0 - ADAM_B1 ** ADAM_STEP)
    v_hat = v / (1.0 - ADAM_B2 ** ADAM_STEP)
    delta = -ADAM_LR * (m_hat / (_jnp.sqrt(v_hat) + ADAM_EPS) + ADAM_WD * w)
    return delta, m, v


def reference(x, c, positions, w_ada, b_ada, norm1_g, w_in, b_merge, gla_w_alpha, gla_b_alpha, gla_out_norm_g, gla_w_o, mla_q_lat_g, mla_w_uq, mla_kv_lat_g, mla_w_ukv, mla_qn_g, mla_kn_g, mla_w_o, w_out, norm2_g, mlp_w1, mlp_w2, loss_target, m_w_ada, m_b_ada, m_norm1_g, m_w_in, m_b_merge, m_gla_w_alpha, m_gla_b_alpha, m_gla_out_norm_g, m_gla_w_o, m_mla_q_lat_g, m_mla_w_uq, m_mla_kv_lat_g, m_mla_w_ukv, m_mla_qn_g, m_mla_kn_g, m_mla_w_o, m_w_out, m_norm2_g, m_mlp_w1, m_mlp_w2, v_w_ada, v_b_ada, v_norm1_g, v_w_in, v_b_merge, v_gla_w_alpha, v_gla_b_alpha, v_gla_out_norm_g, v_gla_w_o, v_mla_q_lat_g, v_mla_w_uq, v_mla_kv_lat_g, v_mla_w_ukv, v_mla_qn_g, v_mla_kn_g, v_mla_w_o, v_w_out, v_norm2_g, v_mlp_w1, v_mlp_w2):
    given = dict(x=x, c=c, positions=positions, w_ada=w_ada, b_ada=b_ada, norm1_g=norm1_g, w_in=w_in, b_merge=b_merge, gla_w_alpha=gla_w_alpha, gla_b_alpha=gla_b_alpha, gla_out_norm_g=gla_out_norm_g, gla_w_o=gla_w_o, mla_q_lat_g=mla_q_lat_g, mla_w_uq=mla_w_uq, mla_kv_lat_g=mla_kv_lat_g, mla_w_ukv=mla_w_ukv, mla_qn_g=mla_qn_g, mla_kn_g=mla_kn_g, mla_w_o=mla_w_o, w_out=w_out, norm2_g=norm2_g, mlp_w1=mlp_w1, mlp_w2=mlp_w2, loss_target=loss_target, m_w_ada=m_w_ada, m_b_ada=m_b_ada, m_norm1_g=m_norm1_g, m_w_in=m_w_in, m_b_merge=m_b_merge, m_gla_w_alpha=m_gla_w_alpha, m_gla_b_alpha=m_gla_b_alpha, m_gla_out_norm_g=m_gla_out_norm_g, m_gla_w_o=m_gla_w_o, m_mla_q_lat_g=m_mla_q_lat_g, m_mla_w_uq=m_mla_w_uq, m_mla_kv_lat_g=m_mla_kv_lat_g, m_mla_w_ukv=m_mla_w_ukv, m_mla_qn_g=m_mla_qn_g, m_mla_kn_g=m_mla_kn_g, m_mla_w_o=m_mla_w_o, m_w_out=m_w_out, m_norm2_g=m_norm2_g, m_mlp_w1=m_mlp_w1, m_mlp_w2=m_mlp_w2, v_w_ada=v_w_ada, v_b_ada=v_b_ada, v_norm1_g=v_norm1_g, v_w_in=v_w_in, v_b_merge=v_b_merge, v_gla_w_alpha=v_gla_w_alpha, v_gla_b_alpha=v_gla_b_alpha, v_gla_out_norm_g=v_gla_out_norm_g, v_gla_w_o=v_gla_w_o, v_mla_q_lat_g=v_mla_q_lat_g, v_mla_w_uq=v_mla_w_uq, v_mla_kv_lat_g=v_mla_kv_lat_g, v_mla_w_ukv=v_mla_w_ukv, v_mla_qn_g=v_mla_qn_g, v_mla_kn_g=v_mla_kn_g, v_mla_w_o=v_mla_w_o, v_w_out=v_w_out, v_norm2_g=v_norm2_g, v_mlp_w1=v_mlp_w1, v_mlp_w2=v_mlp_w2)
    weights = {n: given[n] for n in TWIN_WEIGHTS}
    shared = {n: given[n] for n in SHARED_INPUTS}
    per_example = {n: given[n] for n in ['x', 'c', 'positions']}
    grad_fn = _jax.value_and_grad(_loss, argnums=(0, 1))

    def one_microbatch(ex, loss_target):
        ex = dict(ex)
        diff = ex.pop(TWIN_DIFF_INPUT)
        return grad_fn(weights, diff, {**shared, **ex}, loss_target)

    if N_MICROBATCH == 1:
        loss, (grad_w, grad_x) = one_microbatch(per_example, given["loss_target"])
    else:
        def body(carry, xs):
            loss_sum, grad_sum = carry
            l_k, (gw_k, gx_k) = one_microbatch(xs[0], xs[1])
            with _jax.named_scope("update"):
                return (loss_sum + l_k, _jax.tree.map(_jnp.add, grad_sum, gw_k)), gx_k

        init = (_jnp.zeros((), _jnp.float32), _jax.tree.map(_jnp.zeros_like, weights))
        (loss, grad_w), grad_x = _jax.lax.scan(body, init, (per_example, given["loss_target"]))
    with _jax.named_scope("update"):
        delta_w, new_m, new_v = {}, {}, {}
        for n in TWIN_WEIGHTS:
            delta_w[n], new_m[n], new_v[n] = _adamw(weights[n], grad_w[n], given["m_" + n], given["v_" + n])
    return (loss, grad_x, *[grad_w[n] for n in TWIN_WEIGHTS], *[delta_w[n] for n in TWIN_WEIGHTS],
            *[new_m[n] for n in TWIN_WEIGHTS], *[new_v[n] for n in TWIN_WEIGHTS])
```

```python
import functools

import jax
import jax.numpy as jnp
import numpy as np
from jax import lax
from jax.experimental import pallas as pl
from jax.experimental.pallas import tpu as pltpu

F32 = jnp.float32
BF16 = jnp.bfloat16
MESH = pl.DeviceIdType.MESH

D = 1024
CHUNK = 64
EPS = 1e-6
GH, GDK, GDV, GLR, GTAU = 4, 128, 256, 16, 16.0
MH, MQR, MKVR, MNOPE, MROPE, MVD = 16, 256, 128, 64, 32, 64
MQK = MNOPE + MROPE
DFF = 4 * D
ROPE_THETA = 10000.0
IN_WIDTH = 5552
LANE = 128
OFF_Q, OFF_K, OFF_V, OFF_G, OFF_MA, OFF_MB, OFF_CQ, OFF_CKV, OFF_A, OFF_KPE, PW = (
    0, 512, 1024, 2048, 3072, 4096, 5120, 5376, 5504, 5632, 5760)
ADAM_LR, ADAM_B1, ADAM_B2, ADAM_EPS, ADAM_WD, ADAM_STEP = 0.001, 0.9, 0.999, 1e-08, 0.01, 10
VMEM_LIMIT = 48 * 1024 * 1024


def _params(n_axes):
    return pltpu.CompilerParams(dimension_semantics=("arbitrary",) * n_axes, vmem_limit_bytes=VMEM_LIMIT)


def _tile(n, target):
    if n <= target:
        return n
    best = None
    for t in range(LANE, target + 1, LANE):
        if n % t == 0:
            best = t
    assert best is not None, (n, target)
    return best


def _sigmoid(x):
    return 1.0 / (1.0 + jnp.exp(-x))


def _mm(a, b, *, name, ta=False, tb=False, out_dtype=F32, tm=512, tn=512, tk=1024,
        epilogue=None, extras=(), extra_specs=(), out_shape=None, out_specs=None, a_fn=None):
    if ta:
        kdim, m = a.shape
    else:
        m, kdim = a.shape
    if tb:
        n, k2 = b.shape
    else:
        k2, n = b.shape
    assert kdim == k2, (a.shape, b.shape)
    tm, tn, tk = _tile(m, tm), _tile(n, tn), _tile(kdim, tk)
    nk = kdim // tk
    a_spec = pl.BlockSpec((tk, tm), lambda i, j, k: (k, i)) if ta else pl.BlockSpec((tm, tk), lambda i, j, k: (i, k))
    b_spec = pl.BlockSpec((tn, tk), lambda i, j, k: (j, k)) if tb else pl.BlockSpec((tk, tn), lambda i, j, k: (k, j))
    dims = (((0 if ta else 1,), (1 if tb else 0,)), ((), ()))
    ne = len(extras)
    if out_shape is None:
        out_shape = jax.ShapeDtypeStruct((m, n), out_dtype)
        out_specs = pl.BlockSpec((tm, tn), lambda i, j, k: (i, j))

    def body(a_ref, b_ref, *rest):
        ex, outs, acc = rest[:ne], rest[ne:-1], rest[-1]
        k = pl.program_id(2)

        @pl.when(k == 0)
        def _():
            acc[...] = jnp.zeros_like(acc)

        av = a_ref[...] if a_fn is None else a_fn(a_ref[...])
        acc[...] += lax.dot_general(av.astype(BF16), b_ref[...].astype(BF16), dims, preferred_element_type=F32)

        @pl.when(k == nk - 1)
        def _():
            if epilogue is None:
                outs[0][...] = acc[...].astype(outs[0].dtype)
            else:
                epilogue(acc[...], ex, outs)

    return pl.pallas_call(
        body, name=name, grid=(m // tm, n // tn, nk),
        in_specs=[a_spec, b_spec, *extra_specs], out_specs=out_specs, out_shape=out_shape,
        scratch_shapes=[pltpu.VMEM((tm, tn), F32)], compiler_params=_params(3),
    )(a, b, *extras)


def _tile_spec(tm, tn):
    return pl.BlockSpec((tm, tn), lambda i, j, k: (i, j))


def _rms(x, g):
    r = lax.rsqrt(jnp.mean(x * x, axis=-1, keepdims=True) + EPS)
    return x * r, r


def _row_spec(ts, width, col=0):
    return pl.BlockSpec((None, ts, width), lambda b, i: (b, i, col))


def _vec_spec(width):
    return pl.BlockSpec((None, 1, width), lambda b, i: (b, 0, 0))


def _gain_spec(width):
    return pl.BlockSpec((1, width), lambda b, i: (0, 0))


def _norm_mod(x, g, scale, shift, *, name, ts=256):
    bsz, s, d = x.shape
    ts = min(ts, s)

    def body(x_ref, g_ref, sc_ref, sh_ref, h_ref):
        xh, _ = _rms(x_ref[...], None)
        h_ref[...] = ((xh * g_ref[...]) * (1.0 + sc_ref[...]) + sh_ref[...]).astype(BF16)

    return pl.pallas_call(
        body, name=name, grid=(bsz, s // ts),
        in_specs=[_row_spec(ts, d), _gain_spec(d), _vec_spec(d), _vec_spec(d)],
        out_specs=_row_spec(ts, d), out_shape=jax.ShapeDtypeStruct((bsz, s, d), BF16),
        compiler_params=_params(2),
    )(x, g, scale, shift)


def _resid_norm_mod(x, mixed, gate, g, scale, shift, *, name, ts=256):
    bsz, s, d = x.shape
    ts = min(ts, s)

    def body(x_ref, mx_ref, gt_ref, g_ref, sc_ref, sh_ref, x1_ref, h_ref):
        x1 = x_ref[...] + gt_ref[...] * mx_ref[...]
        x1_ref[...] = x1
        xh, _ = _rms(x1, None)
        h_ref[...] = ((xh * g_ref[...]) * (1.0 + sc_ref[...]) + sh_ref[...]).astype(BF16)

    return pl.pallas_call(
        body, name=name, grid=(bsz, s // ts),
        in_specs=[_row_spec(ts, d), _row_spec(ts, d), _vec_spec(d), _gain_spec(d), _vec_spec(d), _vec_spec(d)],
        out_specs=[_row_spec(ts, d), _row_spec(ts, d)],
        out_shape=[jax.ShapeDtypeStruct((bsz, s, d), F32), jax.ShapeDtypeStruct((bsz, s, d), BF16)],
        compiler_params=_params(2),
    )(x, mixed, gate, g, scale, shift)


def _norm_mod_bwd(dh, xin, resid, g, scale, gate=None, mixed=None, *, name, ts=256):
    bsz, s, d = xin.shape
    ts = min(ts, s)
    gated = gate is not None

    def body(*refs):
        if gated:
            dh_ref, x_ref, rs_ref, g_ref, sc_ref, gt_ref, mx_ref, dx_ref, dsc_ref, dsh_ref, dg_ref, dgt_ref, dmx_ref = refs
        else:
            dh_ref, x_ref, rs_ref, g_ref, sc_ref, dx_ref, dsc_ref, dsh_ref, dg_ref = refs
        b, i = pl.program_id(0), pl.program_id(1)

        @pl.when(i == 0)
        def _():
            dsc_ref[...] = jnp.zeros_like(dsc_ref)
            dsh_ref[...] = jnp.zeros_like(dsh_ref)
            if gated:
                dgt_ref[...] = jnp.zeros_like(dgt_ref)

        @pl.when((i == 0) & (b == 0))
        def _():
            dg_ref[...] = jnp.zeros_like(dg_ref)

        dh_v, gv = dh_ref[...], g_ref[...]
        xh, r = _rms(x_ref[...], None)
        dsc_ref[...] += jnp.sum(dh_v * (xh * gv), axis=0, keepdims=True)
        dsh_ref[...] += jnp.sum(dh_v, axis=0, keepdims=True)
        dn = dh_v * (1.0 + sc_ref[...])
        dg_ref[...] += jnp.sum(dn * xh, axis=0, keepdims=True)
        dxh = dn * gv
        dx = rs_ref[...] + r * (dxh - xh * jnp.mean(dxh * xh, axis=-1, keepdims=True))
        dx_ref[...] = dx
        if gated:
            dgt_ref[...] += jnp.sum(dx * mx_ref[...], axis=0, keepdims=True)
            dmx_ref[...] = (dx * gt_ref[...]).astype(BF16)

    ins = [dh, xin, resid, g, scale]
    in_specs = [_row_spec(ts, d), _row_spec(ts, d), _row_spec(ts, d), _gain_spec(d), _vec_spec(d)]
    out_specs = [_row_spec(ts, d), _vec_spec(d), _vec_spec(d), _gain_spec(d)]
    out_shape = [jax.ShapeDtypeStruct((bsz, s, d), F32), jax.ShapeDtypeStruct((bsz, 1, d), F32),
                 jax.ShapeDtypeStruct((bsz, 1, d), F32), jax.ShapeDtypeStruct((1, d), F32)]
    if gated:
        ins += [gate, mixed]
        in_specs += [_vec_spec(d), _row_spec(ts, d)]
        out_specs += [_vec_spec(d), _row_spec(ts, d)]
        out_shape += [jax.ShapeDtypeStruct((bsz, 1, d), F32), jax.ShapeDtypeStruct((bsz, s, d), BF16)]
    return pl.pallas_call(
        body, name=name, grid=(bsz, s // ts), in_specs=in_specs, out_specs=out_specs, out_shape=out_shape,
        compiler_params=_params(2),
    )(*ins)


def _loss_head(x1, ff, gate2, target, *, name, ts=256):
    bsz, s, d = x1.shape
    ts = min(ts, s)

    def body(x1_ref, ff_ref, gt_ref, t_ref, dy_ref, dff_ref, dgt_ref, loss_ref, acc):
        b, i = pl.program_id(0), pl.program_id(1)

        @pl.when(i == 0)
        def _():
            dgt_ref[...] = jnp.zeros_like(dgt_ref)

        @pl.when((i == 0) & (b == 0))
        def _():
            acc[...] = jnp.zeros_like(acc)

        ffv, gt = ff_ref[...], gt_ref[...]
        diff = (x1_ref[...] + gt * ffv) - t_ref[...]
        acc[...] += jnp.sum((diff * diff).reshape(ts // 8, 8, d), axis=0)
        dy = diff * (1.0 / d)
        dy_ref[...] = dy
        dgt_ref[...] += jnp.sum(dy * ffv, axis=0, keepdims=True)
        dff_ref[...] = (dy * gt).astype(BF16)

        @pl.when((i == pl.num_programs(1) - 1) & (b == pl.num_programs(0) - 1))
        def _():
            loss_ref[...] = jnp.full(loss_ref.shape, jnp.sum(acc[...]), F32)

    return pl.pallas_call(
        body, name=name, grid=(bsz, s // ts),
        in_specs=[_row_spec(ts, d), _row_spec(ts, d), _vec_spec(d), _row_spec(ts, d)],
        out_specs=[_row_spec(ts, d), _row_spec(ts, d), _vec_spec(d), pl.BlockSpec((8, LANE), lambda b, i: (0, 0))],
        out_shape=[jax.ShapeDtypeStruct((bsz, s, d), F32), jax.ShapeDtypeStruct((bsz, s, d), BF16),
                   jax.ShapeDtypeStruct((bsz, 1, d), F32), jax.ShapeDtypeStruct((8, LANE), F32)],
        scratch_shapes=[pltpu.VMEM((8, d), F32)], compiler_params=_params(2),
    )(x1, ff, gate2, target)


def _merge_fwd(proj, b_merge, y_a, y_b, *, name, ts=256):
    bsz, s, _ = proj.shape
    ts = min(ts, s)

    def body(la_ref, lb_ref, ba_ref, bb_ref, ya_ref, yb_ref, out_ref):
        ga = _sigmoid(la_ref[...] + ba_ref[...])
        gb = _sigmoid(lb_ref[...] + bb_ref[...])
        out_ref[...] = (ga * ya_ref[...] + gb * yb_ref[...]).astype(BF16)

    return pl.pallas_call(
        body, name=name, grid=(bsz, s // ts),
        in_specs=[_row_spec(ts, D, OFF_MA // D), _row_spec(ts, D, OFF_MB // D),
                  pl.BlockSpec((1, D), lambda b, i: (0, 0)), pl.BlockSpec((1, D), lambda b, i: (0, 1)),
                  _row_spec(ts, D), _row_spec(ts, D)],
        out_specs=_row_spec(ts, D), out_shape=jax.ShapeDtypeStruct((bsz, s, D), BF16),
        compiler_params=_params(2),
    )(proj, proj, b_merge, b_merge, y_a, y_b)


def _merge_bwd(dmi, proj, b_merge, y_a, y_b, *, name, ts=256):
    bsz, s, _ = proj.shape
    ts = min(ts, s)

    def body(d_ref, la_ref, lb_ref, ba_ref, bb_ref, ya_ref, yb_ref, dya_ref, dyb_ref, dla_ref, dlb_ref, dba_ref, dbb_ref):
        @pl.when((pl.program_id(0) == 0) & (pl.program_id(1) == 0))
        def _():
            dba_ref[...] = jnp.zeros_like(dba_ref)
            dbb_ref[...] = jnp.zeros_like(dbb_ref)

        dv = d_ref[...]
        ga = _sigmoid(la_ref[...] + ba_ref[...])
        gb = _sigmoid(lb_ref[...] + bb_ref[...])
        dya_ref[...] = (dv * ga).astype(BF16)
        dyb_ref[...] = (dv * gb).astype(BF16)
        dla = (dv * ya_ref[...]) * (ga * (1.0 - ga))
        dlb = (dv * yb_ref[...]) * (gb * (1.0 - gb))
        dla_ref[...] = dla.astype(BF16)
        dlb_ref[...] = dlb.astype(BF16)
        dba_ref[...] += jnp.sum(dla, axis=0, keepdims=True)
        dbb_ref[...] += jnp.sum(dlb, axis=0, keepdims=True)

    act = jax.ShapeDtypeStruct((bsz, s, D), BF16)
    return pl.pallas_call(
        body, name=name, grid=(bsz, s // ts),
        in_specs=[_row_spec(ts, D), _row_spec(ts, D, OFF_MA // D), _row_spec(ts, D, OFF_MB // D),
                  pl.BlockSpec((1, D), lambda b, i: (0, 0)), pl.BlockSpec((1, D), lambda b, i: (0, 1)),
                  _row_spec(ts, D), _row_spec(ts, D)],
        out_specs=[_row_spec(ts, D)] * 4 + [_gain_spec(D)] * 2,
        out_shape=[act, act, act, act, jax.ShapeDtypeStruct((1, D), F32), jax.ShapeDtypeStruct((1, D), F32)],
        compiler_params=_params(2),
    )(dmi, proj, proj, b_merge, b_merge, y_a, y_b)


def _tri(lower):
    r = lax.broadcasted_iota(jnp.int32, (CHUNK, CHUNK), 0)
    c = lax.broadcasted_iota(jnp.int32, (CHUNK, CHUNK), 1)
    return jnp.where((c <= r) if lower else (c >= r), 1.0, 0.0).astype(F32)


def _gla_decay(a_ref, wal_ref, bal_ref):
    logits = jnp.dot(a_ref[...].astype(BF16), wal_ref[...].astype(BF16), preferred_element_type=F32) + bal_ref[...]
    la = (jnp.minimum(logits, 0.0) - jnp.log(1.0 + jnp.exp(-jnp.abs(logits)))) * (1.0 / GTAU)
    cum = jnp.dot(_tri(True), la, preferred_element_type=F32, precision=lax.Precision.HIGHEST)
    cum_end = jnp.sum(la, axis=0, keepdims=True)
    return logits, cum, cum_end


def _gla_fwd(proj, w_alpha_p, b_alpha, out_norm_g, *, name):
    bsz, s, _ = proj.shape
    nc = s // CHUNK
    scale = GDK ** -0.5

    def body(q_ref, k_ref, v_ref, g_ref, a_ref, wal_ref, bal_ref, ong_ref, o_ref, og_ref, st_ref, st):
        @pl.when(pl.program_id(2) == 0)
        def _():
            st[...] = jnp.zeros_like(st)

        _, cum, cum_end = _gla_decay(a_ref, wal_ref, bal_ref)
        kd = k_ref[...] * jnp.exp(cum_end - cum)
        ut = lax.dot_general(v_ref[...].astype(BF16), kd.astype(BF16), (((0,), (0,)), ((), ())),
                             preferred_element_type=F32)
        new = st[...] * jnp.exp(cum_end) + ut
        st[...] = new
        st_ref[...] = new
        o = lax.dot_general((q_ref[...] * scale).astype(BF16), new.astype(BF16), (((1,), (1,)), ((), ())),
                            preferred_element_type=F32)
        o_ref[...] = o
        oh, _ = _rms(o, None)
        gv = g_ref[...]
        og_ref[...] = ((oh * ong_ref[...]) * (gv * _sigmoid(gv))).astype(BF16)

    def blk(width, off):
        return pl.BlockSpec((None, CHUNK, width), lambda h, b, n: (b, n, off // width + h))

    return pl.pallas_call(
        body, name=name, grid=(GH, bsz, nc),
        in_specs=[blk(GDK, OFF_Q), blk(GDK, OFF_K), blk(GDV, OFF_V), blk(GDV, OFF_G),
                  pl.BlockSpec((None, CHUNK, LANE), lambda h, b, n: (b, n, OFF_A // LANE)),
                  pl.BlockSpec((LANE, GDK), lambda h, b, n: (0, h)), pl.BlockSpec((1, GDK), lambda h, b, n: (0, h)),
                  pl.BlockSpec((1, GDV), lambda h, b, n: (0, 0))],
        out_specs=[pl.BlockSpec((None, CHUNK, GDV), lambda h, b, n: (b, n, h)),
                   pl.BlockSpec((None, CHUNK, GDV), lambda h, b, n: (b, n, h)),
                   pl.BlockSpec((None, None, None, GDV, GDK), lambda h, b, n: (b, h, n, 0, 0))],
        out_shape=[jax.ShapeDtypeStruct((bsz, s, GH * GDV), F32), jax.ShapeDtypeStruct((bsz, s, GH * GDV), BF16),
                   jax.ShapeDtypeStruct((bsz, GH, nc, GDV, GDK), F32)],
        scratch_shapes=[pltpu.VMEM((GDV, GDK), F32)], compiler_params=_params(3),
    )(proj, proj, proj, proj, proj, w_alpha_p, b_alpha, out_norm_g)


def _gla_bwd(dog, o, states, proj, w_alpha_p, b_alpha, out_norm_g, *, name):
    bsz, s, _ = proj.shape
    nc = s // CHUNK
    scale = GDK ** -0.5

    def body(dog_ref, o_ref, st_ref, sp_ref, q_ref, k_ref, v_ref, g_ref, a_ref, wal_ref, bal_ref, ong_ref,
             dq_ref, dk_ref, dv_ref, dg_ref, dl_ref, dbal_ref, dong_ref, carry):
        h, b, t = pl.program_id(0), pl.program_id(1), pl.program_id(2)
        n = nc - 1 - t

        @pl.when(t == 0)
        def _():
            carry[...] = jnp.zeros_like(carry)

        @pl.when((t == 0) & (b == 0))
        def _():
            dbal_ref[...] = jnp.zeros_like(dbal_ref)

        @pl.when((t == 0) & (b == 0) & (h == 0))
        def _():
            dong_ref[...] = jnp.zeros_like(dong_ref)

        gv, ov, dogv, ong = g_ref[...], o_ref[...], dog_ref[...], ong_ref[...]
        sg = _sigmoid(gv)
        oh, r = _rms(ov, None)
        don = dogv * (gv * sg)
        dg_ref[...] = (dogv * (oh * ong) * (sg * (1.0 + gv * (1.0 - sg)))).astype(BF16)
        dong_ref[...] += jnp.sum(don * oh, axis=0, keepdims=True)
        doh = don * ong
        do = r * (doh - oh * jnp.mean(doh * oh, axis=-1, keepdims=True))
        do_b = do.astype(BF16)

        logits, cum, cum_end = _gla_decay(a_ref, wal_ref, bal_ref)
        decay = jnp.exp(cum_end)
        w = jnp.exp(cum_end - cum)
        kv_ = k_ref[...]
        kd = kv_ * w
        stv = st_ref[...]
        qs_b = (q_ref[...] * scale).astype(BF16)
        dq_ref[...] = (jnp.dot(do_b, stv.astype(BF16), preferred_element_type=F32) * scale).astype(BF16)
        dsn = lax.dot_general(do_b, qs_b, (((0,), (0,)), ((), ())), preferred_element_type=F32) + carry[...]
        sprev = jnp.where(n > 0, sp_ref[...], 0.0)
        ddecay = jnp.sum(dsn * sprev, axis=0, keepdims=True)
        carry[...] = dsn * decay
        dsn_b = dsn.astype(BF16)
        dv_ref[...] = lax.dot_general(kd.astype(BF16), dsn_b, (((1,), (1,)), ((), ())),
                                      preferred_element_type=F32).astype(BF16)
        dkd = jnp.dot(v_ref[...].astype(BF16), dsn_b, preferred_element_type=F32)
        dk_ref[...] = (dkd * w).astype(BF16)
        e = dkd * kd
        dcum_end = jnp.sum(e, axis=0, keepdims=True) + ddecay * decay
        dla = dcum_end - jnp.dot(_tri(False), e, preferred_element_type=F32, precision=lax.Precision.HIGHEST)
        dlog = dla * (1.0 / GTAU) * (1.0 - _sigmoid(logits))
        dl_ref[...] = dlog.astype(BF16)
        dbal_ref[...] += jnp.sum(dlog, axis=0, keepdims=True)

    def blk(width, off):
        return pl.BlockSpec((None, CHUNK, width), lambda h, b, t: (b, nc - 1 - t, off // width + h))

    def stblk(prev):
        def im(h, b, t):
            n = nc - 1 - t
            return (b, h, jnp.maximum(n - 1, 0) if prev else n, 0, 0)
        return pl.BlockSpec((None, None, None, GDV, GDK), im)

    act = lambda wd: jax.ShapeDtypeStruct((bsz, s, wd), BF16)
    return pl.pallas_call(
        body, name=name, grid=(GH, bsz, nc),
        in_specs=[blk(GDV, 0), blk(GDV, 0), stblk(False), stblk(True),
                  blk(GDK, OFF_Q), blk(GDK, OFF_K), blk(GDV, OFF_V), blk(GDV, OFF_G),
                  pl.BlockSpec((None, CHUNK, LANE), lambda h, b, t: (b, nc - 1 - t, OFF_A // LANE)),
                  pl.BlockSpec((LANE, GDK), lambda h, b, t: (0, h)), pl.BlockSpec((1, GDK), lambda h, b, t: (0, h)),
                  pl.BlockSpec((1, GDV), lambda h, b, t: (0, 0))],
        out_specs=[blk(GDK, 0), blk(GDK, 0), blk(GDV, 0), blk(GDV, 0), blk(GDK, 0),
                   pl.BlockSpec((1, GDK), lambda h, b, t: (0, h)), pl.BlockSpec((1, GDV), lambda h, b, t: (0, 0))],
        out_shape=[act(GH * GDK), act(GH * GDK), act(GH * GDV), act(GH * GDV), act(GH * GDK),
                   jax.ShapeDtypeStruct((1, GH * GDK), F32), jax.ShapeDtypeStruct((1, GDV), F32)],
        scratch_shapes=[pltpu.VMEM((GDV, GDK), F32)], compiler_params=_params(3),
    )(dog, o, states, states, proj, proj, proj, proj, proj, w_alpha_p, b_alpha, out_norm_g)


def _lane():
    return lax.broadcasted_iota(jnp.int32, (1, LANE), 1)


def _swap_halves(x):
    lane = _lane()
    half = MROPE // 2
    lo = (lane >= MNOPE) & (lane < MNOPE + half)
    hi = (lane >= MNOPE + half) & (lane < MQK)
    return jnp.where(lo, pltpu.roll(x, LANE - half, 1), jnp.where(hi, pltpu.roll(x, half, 1), 0.0))


def _norm96(x, g):
    r = lax.rsqrt(jnp.sum(x * x, axis=-1, keepdims=True) * (1.0 / MQK) + EPS)
    return x * r, r


def _lat_norm(proj, q_lat_g, kv_lat_g, *, name, ts=512):
    t = proj.shape[0]
    ts = min(ts, t)

    def body(cq_ref, ckv_ref, gq_ref, gk_ref, oq_ref, ok_ref):
        xq, _ = _rms(cq_ref[...], None)
        oq_ref[...] = (xq * gq_ref[...]).astype(BF16)
        xk, _ = _rms(ckv_ref[...], None)
        ok_ref[...] = (xk * gk_ref[...]).astype(BF16)

    return pl.pallas_call(
        body, name=name, grid=(t // ts,),
        in_specs=[pl.BlockSpec((ts, MQR), lambda i: (i, OFF_CQ // MQR)), pl.BlockSpec((ts, MKVR), lambda i: (i, OFF_CKV // MKVR)),
                  pl.BlockSpec((1, MQR), lambda i: (0, 0)), pl.BlockSpec((1, MKVR), lambda i: (0, 0))],
        out_specs=[pl.BlockSpec((ts, MQR), lambda i: (i, 0)), pl.BlockSpec((ts, MKVR), lambda i: (i, 0))],
        out_shape=[jax.ShapeDtypeStruct((t, MQR), BF16), jax.ShapeDtypeStruct((t, MKVR), BF16)],
        compiler_params=_params(1),
    )(proj, proj, q_lat_g, kv_lat_g)


def _lat_norm_bwd(dcqn, dckvn, proj, q_lat_g, kv_lat_g, *, name, ts=512):
    t = proj.shape[0]
    ts = min(ts, t)

    def one(d_ref, x_ref, g_ref, dx_ref, dg_ref):
        xh, r = _rms(x_ref[...], None)
        dn = d_ref[...]
        dg_ref[...] += jnp.sum(dn * xh, axis=0, keepdims=True)
        dxh = dn * g_ref[...]
        dx_ref[...] = (r * (dxh - xh * jnp.mean(dxh * xh, axis=-1, keepdims=True))).astype(BF16)

    def body(dq_ref, dk_ref, cq_ref, ckv_ref, gq_ref, gk_ref, dxq_ref, dxk_ref, dgq_ref, dgk_ref):
        @pl.when(pl.program_id(0) == 0)
        def _():
            dgq_ref[...] = jnp.zeros_like(dgq_ref)
            dgk_ref[...] = jnp.zeros_like(dgk_ref)

        one(dq_ref, cq_ref, gq_ref, dxq_ref, dgq_ref)
        one(dk_ref, ckv_ref, gk_ref, dxk_ref, dgk_ref)

    return pl.pallas_call(
        body, name=name, grid=(t // ts,),
        in_specs=[pl.BlockSpec((ts, MQR), lambda i: (i, 0)), pl.BlockSpec((ts, MKVR), lambda i: (i, 0)),
                  pl.BlockSpec((ts, MQR), lambda i: (i, OFF_CQ // MQR)), pl.BlockSpec((ts, MKVR), lambda i: (i, OFF_CKV // MKVR)),
                  pl.BlockSpec((1, MQR), lambda i: (0, 0)), pl.BlockSpec((1, MKVR), lambda i: (0, 0))],
        out_specs=[pl.BlockSpec((ts, MQR), lambda i: (i, 0)), pl.BlockSpec((ts, MKVR), lambda i: (i, 0)),
                   pl.BlockSpec((1, MQR), lambda i: (0, 0)), pl.BlockSpec((1, MKVR), lambda i: (0, 0))],
        out_shape=[jax.ShapeDtypeStruct((t, MQR), BF16), jax.ShapeDtypeStruct((t, MKVR), BF16),
                   jax.ShapeDtypeStruct((1, MQR), F32), jax.ShapeDtypeStruct((1, MKVR), F32)],
        compiler_params=_params(1),
    )(dcqn, dckvn, proj, proj, q_lat_g, kv_lat_g)


def _qk_prep(q_raw, kv, proj, cos_t, sin_t, gq, gk, *, name, ts=512):
    t = q_raw.shape[0]
    ts = min(ts, t)

    def body(q_ref, kv_ref, kpe_ref, c_ref, s_ref, gq_ref, gk_ref, qo_ref, ko_ref, vo_ref):
        cs, sn = c_ref[...], s_ref[...]
        nope = _lane() < MNOPE
        qn, _ = _norm96(q_ref[...], None)
        qn = qn * gq_ref[...]
        qo_ref[...] = (qn * cs + _swap_halves(qn) * sn).astype(BF16)
        kvv = kv_ref[...]
        kn, _ = _norm96(jnp.where(nope, kvv, kpe_ref[...]), None)
        kn = kn * gk_ref[...]
        ko_ref[...] = (kn * cs + _swap_halves(kn) * sn).astype(BF16)
        vo_ref[...] = jnp.where(nope, pltpu.roll(kvv, MNOPE, 1), 0.0).astype(BF16)

    hd = pl.BlockSpec((ts, LANE), lambda i, h: (i, h))
    shared = lambda col: pl.BlockSpec((ts, LANE), lambda i, h: (i, col))
    gain = pl.BlockSpec((1, LANE), lambda i, h: (0, 0))
    out = jax.ShapeDtypeStruct((t, MH * LANE), BF16)
    return pl.pallas_call(
        body, name=name, grid=(t // ts, MH),
        in_specs=[hd, hd, shared(OFF_KPE // LANE), shared(0), shared(0), gain, gain],
        out_specs=[hd, hd, hd], out_shape=[out, out, out], compiler_params=_params(2),
    )(q_raw, kv, proj, cos_t, sin_t, gq, gk)


def _qk_prep_bwd(dq, dk, dv, q_raw, kv, proj, cos_t, sin_t, gq, gk, *, name, ts=512):
    t = q_raw.shape[0]
    ts = min(ts, t)

    def norm_bwd(dy, x, g, dg_ref):
        xh, r = _norm96(x, None)
        dg_ref[...] += jnp.sum(dy * xh, axis=0, keepdims=True)
        dxh = dy * g
        return r * (dxh - xh * (jnp.sum(dxh * xh, axis=-1, keepdims=True) * (1.0 / MQK)))

    def body(dq_ref, dk_ref, dv_ref, q_ref, kv_ref, kpe_ref, c_ref, s_ref, gq_ref, gk_ref,
             dqr_ref, dkv_ref, dkpe_ref, dgq_ref, dgk_ref):
        i, h = pl.program_id(0), pl.program_id(1)

        @pl.when(h == 0)
        def _():
            dkpe_ref[...] = jnp.zeros_like(dkpe_ref)

        @pl.when((h == 0) & (i == 0))
        def _():
            dgq_ref[...] = jnp.zeros_like(dgq_ref)
            dgk_ref[...] = jnp.zeros_like(dgk_ref)

        cs, sn = c_ref[...], s_ref[...]
        lane = _lane()
        nope = lane < MNOPE
        dqv = dq_ref[...]
        dqn = dqv * cs + _swap_halves(dqv * sn)
        dqr_ref[...] = norm_bwd(dqn, q_ref[...], gq_ref[...], dgq_ref).astype(BF16)
        dkv_ = dk_ref[...]
        dkn = dkv_ * cs + _swap_halves(dkv_ * sn)
        kvv = kv_ref[...]
        dkr = norm_bwd(dkn, jnp.where(nope, kvv, kpe_ref[...]), gk_ref[...], dgk_ref)
        dkv_ref[...] = jnp.where(nope, dkr, pltpu.roll(dv_ref[...], MNOPE, 1)).astype(BF16)
        dkpe_ref[...] += jnp.where((lane >= MNOPE) & (lane < MQK), dkr, 0.0)

    hd = pl.BlockSpec((ts, LANE), lambda i, h: (i, h))
    shared = lambda col: pl.BlockSpec((ts, LANE), lambda i, h: (i, col))
    gain = pl.BlockSpec((1, LANE), lambda i, h: (0, 0))
    out = jax.ShapeDtypeStruct((t, MH * LANE), BF16)
    return pl.pallas_call(
        body, name=name, grid=(t // ts, MH),
        in_specs=[hd, hd, hd, hd, hd, shared(OFF_KPE // LANE), shared(0), shared(0), gain, gain],
        out_specs=[hd, hd, shared(0), gain, gain],
        out_shape=[out, out, jax.ShapeDtypeStruct((t, LANE), F32), jax.ShapeDtypeStruct((1, LANE), F32),
                   jax.ShapeDtypeStruct((1, LANE), F32)],
        compiler_params=_params(2),
    )(dq, dk, dv, q_raw, kv, proj, cos_t, sin_t, gq, gk)


def _attn_probs(q_ref, k_ref, i, tq, s):
    sc = lax.dot_general(q_ref[...], k_ref[...], (((1,), (1,)), ((), ())), preferred_element_type=F32) * (MQK ** -0.5)
    qc = (i * tq + lax.broadcasted_iota(jnp.int32, (tq, s), 0)) // CHUNK
    kc = lax.broadcasted_iota(jnp.int32, (tq, s), 1) // CHUNK
    sc = jnp.where(kc <= qc, sc, -1e30)
    p = jnp.exp(sc - jnp.max(sc, axis=-1, keepdims=True))
    return p / jnp.sum(p, axis=-1, keepdims=True)


def _attn_fwd(q, k, v, *, name, tq=256):
    bsz, s, _ = q.shape
    tq = min(tq, s)

    def body(q_ref, k_ref, v_ref, o_ref):
        p = _attn_probs(q_ref, k_ref, pl.program_id(2), tq, s)
        o_ref[...] = jnp.dot(p.astype(BF16), v_ref[...], preferred_element_type=F32).astype(BF16)

    qs = pl.BlockSpec((None, tq, LANE), lambda b, h, i: (b, i, h))
    ks = pl.BlockSpec((None, s, LANE), lambda b, h, i: (b, 0, h))
    return pl.pallas_call(
        body, name=name, grid=(bsz, MH, s // tq), in_specs=[qs, ks, ks], out_specs=qs,
        out_shape=jax.ShapeDtypeStruct((bsz, s, MH * LANE), BF16), compiler_params=_params(3),
    )(q, k, v)


def _attn_bwd(q, k, v, do, *, name, tq=256):
    bsz, s, _ = q.shape
    tq = min(tq, s)
    nq = s // tq

    def body(q_ref, k_ref, v_ref, do_ref, dq_ref, dk_ref, dv_ref, dk_acc, dv_acc):
        i = pl.program_id(2)

        @pl.when(i == 0)
        def _():
            dk_acc[...] = jnp.zeros_like(dk_acc)
            dv_acc[...] = jnp.zeros_like(dv_acc)

        p = _attn_probs(q_ref, k_ref, i, tq, s)
        dov = do_ref[...]
        dv_acc[...] += lax.dot_general(p.astype(BF16), dov, (((0,), (0,)), ((), ())), preferred_element_type=F32)
        dp = lax.dot_general(dov, v_ref[...], (((1,), (1,)), ((), ())), preferred_element_type=F32)
        ds = (p * (dp - jnp.sum(dp * p, axis=-1, keepdims=True)) * (MQK ** -0.5)).astype(BF16)
        dq_ref[...] = jnp.dot(ds, k_ref[...], preferred_element_type=F32)
        dk_acc[...] += lax.dot_general(ds, q_ref[...], (((0,), (0,)), ((), ())), preferred_element_type=F32)

        @pl.when(i == nq - 1)
        def _():
            dk_ref[...] = dk_acc[...]
            dv_ref[...] = dv_acc[...]

    qs = pl.BlockSpec((None, tq, LANE), lambda b, h, i: (b, i, h))
    ks = pl.BlockSpec((None, s, LANE), lambda b, h, i: (b, 0, h))
    out = jax.ShapeDtypeStruct((bsz, s, MH * LANE), F32)
    return pl.pallas_call(
        body, name=name, grid=(bsz, MH, nq), in_specs=[qs, ks, ks, qs], out_specs=[qs, ks, ks],
        out_shape=[out, out, out], scratch_shapes=[pltpu.VMEM((s, LANE), F32), pltpu.VMEM((s, LANE), F32)],
        compiler_params=_params(3),
    )(q, k, v, do)


def _adamw(w, g, m, v, *, name, tr=1024):
    rows = w.shape[0]
    tr = _tile_rows(rows, tr)

    def body(w_ref, g_ref, m_ref, v_ref, d_ref, nm_ref, nv_ref):
        gv = g_ref[...]
        nm = ADAM_B1 * m_ref[...] + (1.0 - ADAM_B1) * gv
        nv = ADAM_B2 * v_ref[...] + (1.0 - ADAM_B2) * (gv * gv)
        m_hat = nm / (1.0 - ADAM_B1 ** ADAM_STEP)
        v_hat = nv / (1.0 - ADAM_B2 ** ADAM_STEP)
        d_ref[...] = -ADAM_LR * (m_hat / (jnp.sqrt(v_hat) + ADAM_EPS) + ADAM_WD * w_ref[...])
        nm_ref[...] = nm
        nv_ref[...] = nv

    spec = pl.BlockSpec((tr, LANE), lambda i: (i, 0))
    out = jax.ShapeDtypeStruct((rows, LANE), F32)
    return pl.pallas_call(body, name=name, grid=(rows // tr,), in_specs=[spec] * 4, out_specs=[spec] * 3,
                          out_shape=[out, out, out], compiler_params=_params(1))(w, g, m, v)


def _tile_rows(rows, target):
    if rows <= target:
        return rows
    best = 8
    for t in range(8, target + 1, 8):
        if rows % t == 0:
            best = t
    return best


def _sum_slots(x, *, name, tr=1024):
    n, rows, _ = x.shape
    tr = _tile_rows(rows, tr)

    def body(x_ref, o_ref):
        acc = x_ref[0]
        for j in range(1, n):
            acc = acc + x_ref[j]
        o_ref[...] = acc

    return pl.pallas_call(
        body, name=name, grid=(rows // tr,), in_specs=[pl.BlockSpec((n, tr, LANE), lambda i: (0, i, 0))],
        out_specs=pl.BlockSpec((tr, LANE), lambda i: (i, 0)), out_shape=jax.ShapeDtypeStruct((rows, LANE), F32),
        compiler_params=_params(1))(x)


def _add2(a, b, *, name, tr=1024):
    n, rows, _ = a.shape
    tr = _tile_rows(rows, tr)

    def body(a_ref, b_ref, o_ref):
        o_ref[...] = a_ref[...] + b_ref[...]

    spec = pl.BlockSpec((None, tr, LANE), lambda j, i: (j, i, 0))
    return pl.pallas_call(body, name=name, grid=(n, rows // tr), in_specs=[spec, spec], out_specs=spec,
                          out_shape=jax.ShapeDtypeStruct(a.shape, F32), compiler_params=_params(2))(a, b)


def _me():
    return lax.axis_index("x"), lax.axis_index("y"), lax.axis_index("c")


def _flip(pos, bits):
    x, y, c = pos
    return (x ^ bits[0] if bits[0] else x, y ^ bits[1] if bits[1] else y, c ^ bits[2] if bits[2] else c)


ANY = pl.BlockSpec(memory_space=pl.ANY)


def _all_gather8(x, *, name):
    flips = [((k >> 2) & 1, (k >> 1) & 1, k & 1) for k in range(1, 8)]

    def body(x_ref, out_ref, send_sems, recv_sems, local_sem):
        me = _me()
        slot = lambda p: 4 * p[0] + 2 * p[1] + p[2]
        mine = pltpu.make_async_copy(x_ref, out_ref.at[slot(me)], local_sem)
        mine.start()
        sends = []
        for k, f in enumerate(flips):
            cp = pltpu.make_async_remote_copy(src_ref=x_ref, dst_ref=out_ref.at[slot(me)], send_sem=send_sems.at[k],
                                              recv_sem=recv_sems.at[k], device_id=_flip(me, f), device_id_type=MESH)
            cp.start()
            sends.append(cp)
        for k, f in enumerate(flips):
            peer = _flip(me, f)
            pltpu.make_async_remote_copy(src_ref=x_ref, dst_ref=out_ref.at[slot(peer)], send_sem=send_sems.at[k],
                                         recv_sem=recv_sems.at[k], device_id=peer, device_id_type=MESH).wait_recv()
        for cp in sends:
            cp.wait_send()
        mine.wait()

    return pl.pallas_call(
        body, name=name, in_specs=[ANY], out_specs=ANY, out_shape=jax.ShapeDtypeStruct((8, *x.shape), x.dtype),
        scratch_shapes=[pltpu.SemaphoreType.DMA((7,)), pltpu.SemaphoreType.DMA((7,)), pltpu.SemaphoreType.DMA])(x)


CHIP_FLIPS = [(1, 0, 0), (0, 1, 0), (1, 1, 0)]


def _all_gather_chips(x, *, name):
    def body(x_ref, out_ref, send_sems, recv_sems, local_sem):
        me = _me()
        slot = lambda p: 2 * p[0] + p[1]
        mine = pltpu.make_async_copy(x_ref, out_ref.at[slot(me)], local_sem)
        mine.start()
        sends = []
        for k, f in enumerate(CHIP_FLIPS):
            cp = pltpu.make_async_remote_copy(src_ref=x_ref, dst_ref=out_ref.at[slot(me)], send_sem=send_sems.at[k],
                                              recv_sem=recv_sems.at[k], device_id=_flip(me, f), device_id_type=MESH)
            cp.start()
            sends.append(cp)
        for k, f in enumerate(CHIP_FLIPS):
            peer = _flip(me, f)
            pltpu.make_async_remote_copy(src_ref=x_ref, dst_ref=out_ref.at[slot(peer)], send_sem=send_sems.at[k],
                                         recv_sem=recv_sems.at[k], device_id=peer, device_id_type=MESH).wait_recv()
        for cp in sends:
            cp.wait_send()
        mine.wait()

    return pl.pallas_call(
        body, name=name, in_specs=[ANY], out_specs=ANY, out_shape=jax.ShapeDtypeStruct((4, *x.shape), x.dtype),
        scratch_shapes=[pltpu.SemaphoreType.DMA((3,)), pltpu.SemaphoreType.DMA((3,)), pltpu.SemaphoreType.DMA])(x)


def _pair_swap_halves(x, *, name):
    n, rows, cols = x.shape
    hr = rows // 2

    def body(x_ref, out_ref, send_sem, recv_sem):
        me = _me()
        sib = _flip(me, (0, 0, 1))
        theirs = pl.multiple_of((1 - me[2]) * hr, 8)
        cp = pltpu.make_async_remote_copy(src_ref=x_ref.at[:, pl.ds(theirs, hr), :], dst_ref=out_ref, send_sem=send_sem,
                                          recv_sem=recv_sem, device_id=sib, device_id_type=MESH)
        cp.start()
        cp.wait()

    return pl.pallas_call(
        body, name=name, in_specs=[ANY], out_specs=ANY, out_shape=jax.ShapeDtypeStruct((n, hr, cols), x.dtype),
        scratch_shapes=[pltpu.SemaphoreType.DMA, pltpu.SemaphoreType.DMA])(x)


def _scatter_chips(p, *, name):
    n, rows, cols = p.shape

    def body(p_ref, out_ref, send_sems, recv_sems, local_sem):
        me = _me()
        slot = lambda q: 2 * q[0] + q[1]
        mine = pltpu.make_async_copy(p_ref.at[slot(me)], out_ref.at[slot(me)], local_sem)
        mine.start()
        sends = []
        for k, f in enumerate(CHIP_FLIPS):
            peer = _flip(me, f)
            cp = pltpu.make_async_remote_copy(src_ref=p_ref.at[slot(peer)], dst_ref=out_ref.at[slot(me)],
                                              send_sem=send_sems.at[k], recv_sem=recv_sems.at[k], device_id=peer,
                                              device_id_type=MESH)
            cp.start()
            sends.append(cp)
        for k, f in enumerate(CHIP_FLIPS):
            peer = _flip(me, f)
            pltpu.make_async_remote_copy(src_ref=p_ref.at[slot(me)], dst_ref=out_ref.at[slot(peer)],
                                         send_sem=send_sems.at[k], recv_sem=recv_sems.at[k], device_id=peer,
                                         device_id_type=MESH).wait_recv()
        for cp in sends:
            cp.wait_send()
        mine.wait()

    return pl.pallas_call(
        body, name=name, in_specs=[ANY], out_specs=ANY, out_shape=jax.ShapeDtypeStruct(p.shape, p.dtype),
        scratch_shapes=[pltpu.SemaphoreType.DMA((3,)), pltpu.SemaphoreType.DMA((3,)), pltpu.SemaphoreType.DMA])(p)


def _pair_join_halves(h, *, name):
    hr, cols = h.shape

    def body(h_ref, out_ref, send_sem, recv_sem, local_sem):
        me = _me()
        sib = _flip(me, (0, 0, 1))
        my_rows = out_ref.at[pl.ds(pl.multiple_of(me[2] * hr, 8), hr), :]
        their_rows = out_ref.at[pl.ds(pl.multiple_of((1 - me[2]) * hr, 8), hr), :]
        mine = pltpu.make_async_copy(h_ref, my_rows, local_sem)
        mine.start()
        cp = pltpu.make_async_remote_copy(src_ref=h_ref, dst_ref=my_rows, send_sem=send_sem, recv_sem=recv_sem,
                                          device_id=sib, device_id_type=MESH)
        cp.start()
        pltpu.make_async_remote_copy(src_ref=h_ref, dst_ref=their_rows, send_sem=send_sem, recv_sem=recv_sem,
                                     device_id=sib, device_id_type=MESH).wait_recv()
        cp.wait_send()
        mine.wait()

    return pl.pallas_call(
        body, name=name, in_specs=[ANY], out_specs=ANY, out_shape=jax.ShapeDtypeStruct((2 * hr, cols), h.dtype),
        scratch_shapes=[pltpu.SemaphoreType.DMA, pltpu.SemaphoreType.DMA, pltpu.SemaphoreType.DMA])(h)


BIG = (("w_in", (D, IN_WIDTH // 4), 1), ("gla_w_o", (D // 4, D), 0), ("mla_w_uq", (MQR, MH * MQK // 4), 1),
       ("mla_w_ukv", (MKVR, MH * (MNOPE + MVD) // 4), 1), ("mla_w_o", (D // 4, D), 0), ("w_out", (D // 4, D), 0),
       ("mlp_w1", (D, DFF // 4), 1), ("mlp_w2", (DFF // 4, D), 0))
ADA_SHARD = (D, 6 * D // 4)
SMALL = (("b_ada", 6 * D), ("norm1_g", D), ("b_merge", 2 * D), ("gla_b_alpha", GH * GDK), ("gla_out_norm_g", GDV),
         ("mla_q_lat_g", MQR), ("mla_kv_lat_g", MKVR), ("mla_qn_g", MQK), ("mla_kn_g", MQK), ("norm2_g", D))


PACK_ROWS = 2048


def _rows_of(n):
    return -(-n // LANE)


def _flat(a):
    v = a.reshape(-1)
    rows = _rows_of(v.shape[0])
    if rows * LANE != v.shape[0]:
        v = jnp.pad(v, (0, rows * LANE - v.shape[0]))
    return v.reshape(rows, LANE)


def _pack(arrs, multiple=1):
    parts = [_flat(a) for a in arrs]
    rows = sum(p.shape[0] for p in parts)
    if rows % multiple:
        parts.append(jnp.zeros((multiple - rows % multiple, LANE), parts[0].dtype))
    return jnp.concatenate(parts, axis=0)


def _unpack(flat, shapes):
    out, r = [], 0
    for shp in shapes:
        n = int(np.prod(shp))
        rows = _rows_of(n)
        out.append(flat[r:r + rows].reshape(-1)[:n].reshape(shp))
        r += rows
    return out


def _full_weights(slots):
    per_chip = [_unpack(slots[j], [shp for _, shp, _ in BIG]) for j in range(4)]
    w = {name: jnp.concatenate([per_chip[j][i] for j in range(4)], axis=axis) for i, (name, _, axis) in enumerate(BIG)}
    wi = w["w_in"]
    zeros = lambda n: jnp.zeros((D, n), wi.dtype)
    w["w_in"] = jnp.concatenate(
        [wi[:, :3072], wi[:, 3504:5552], wi[:, 3088:3344], wi[:, 3344:3472], wi[:, 3072:3088], zeros(LANE - GLR),
         zeros(MNOPE), wi[:, 3472:3504], zeros(LANE - MQK)], axis=1)
    w["mla_w_uq"] = jnp.pad(w["mla_w_uq"].reshape(MQR, MH, MQK), ((0, 0), (0, 0), (0, LANE - MQK))).reshape(MQR, MH * LANE)
    w["mla_w_o"] = jnp.pad(w["mla_w_o"].reshape(MH, MVD, D), ((0, 0), (0, LANE - MVD), (0, 0))).reshape(MH * LANE, D)
    return w


def _grad_slots(g):
    gi = g["w_in"]
    g = dict(g)
    g["w_in"] = jnp.concatenate(
        [gi[:, :3072], gi[:, OFF_A:OFF_A + GLR], gi[:, OFF_CQ:OFF_CQ + MQR], gi[:, OFF_CKV:OFF_CKV + MKVR],
         gi[:, OFF_KPE + MNOPE:OFF_KPE + MQK], gi[:, OFF_MA:OFF_MA + 2 * D]], axis=1)
    g["mla_w_uq"] = g["mla_w_uq"].reshape(MQR, MH, LANE)[:, :, :MQK].reshape(MQR, MH * MQK)
    g["mla_w_o"] = g["mla_w_o"].reshape(MH, LANE, D)[:, :MVD].reshape(MH * MVD, D)
    slots = []
    for j in range(4):
        parts = []
        for name, shp, axis in BIG:
            n = shp[axis]
            parts.append(lax.slice_in_dim(g[name], j * n, (j + 1) * n, axis=axis))
        slots.append(_pack(parts, multiple=PACK_ROWS))
    return jnp.stack(slots)


def _rope_tables(positions):
    freqs = ROPE_THETA ** (-jnp.arange(0, MROPE, 2, dtype=F32) / MROPE)
    ang = positions.astype(F32)[..., None] * freqs
    cos, sin = jnp.cos(ang), jnp.sin(ang)
    shape = ang.shape[:-1]
    cos_t = jnp.concatenate([jnp.ones(shape + (MNOPE,), F32), cos, cos, jnp.ones(shape + (LANE - MQK,), F32)], axis=-1)
    sin_t = jnp.concatenate([jnp.zeros(shape + (MNOPE,), F32), -sin, sin, jnp.zeros(shape + (LANE - MQK,), F32)], axis=-1)
    return cos_t.reshape(-1, LANE), sin_t.reshape(-1, LANE)


def _local_step(x, positions, mod, target, w, small):
    bsz, s, _ = x.shape
    t = bsz * s
    tt = _tile(t, 512)
    shift1, scale1, gate1, shift2, scale2, gate2 = [mod[:, None, i * D:(i + 1) * D] for i in range(6)]
    cos_t, sin_t = _rope_tables(positions)
    w_alpha_p = jnp.pad(small["gla_w_alpha"], ((0, LANE - GLR), (0, 0)))
    gq = jnp.pad(small["mla_qn_g"], ((0, 0), (0, LANE - MQK)))
    gk = jnp.pad(small["mla_kn_g"], ((0, 0), (0, LANE - MQK)))
    flat2 = lambda a: a.reshape(t, a.shape[-1])
    bsd = lambda a: a.reshape(bsz, s, a.shape[-1])

    h = _norm_mod(x, small["norm1_g"], scale1, shift1, name="norm1")
    proj = _mm(flat2(h), w["w_in"], name="proj", tn=640)
    proj3 = bsd(proj)
    o, o_gated, states = _gla_fwd(proj3, w_alpha_p, small["gla_b_alpha"], small["gla_out_norm_g"], name="gla_fwd")
    y_a = _mm(flat2(o_gated), w["gla_w_o"], name="gla_out")
    cq_n, ckv_n = _lat_norm(proj, small["mla_q_lat_g"], small["mla_kv_lat_g"], name="lat_norm")
    q_raw = _mm(cq_n, w["mla_w_uq"], name="mla_uq")
    kv = _mm(ckv_n, w["mla_w_ukv"], name="mla_ukv")
    qf, kf, vf = _qk_prep(q_raw, kv, proj, cos_t, sin_t, gq, gk, name="qk_prep")
    o_attn = _attn_fwd(bsd(qf), bsd(kf), bsd(vf), name="attn_fwd")
    y_b = _mm(flat2(o_attn), w["mla_w_o"], name="mla_out")
    mixed_in = _merge_fwd(proj3, small["b_merge"], bsd(y_a), bsd(y_b), name="merge_fwd")
    mixed = _mm(flat2(mixed_in), w["w_out"], name="w_out")
    x1, h2 = _resid_norm_mod(x, bsd(mixed), gate1, small["norm2_g"], scale2, shift2, name="norm2")

    def sqrelu(acc, ex, outs):
        outs[0][...] = acc
        r = jnp.maximum(acc, 0.0)
        outs[1][...] = (r * r).astype(BF16)

    a1, r = _mm(flat2(h2), w["mlp_w1"], name="mlp1", epilogue=sqrelu,
                out_shape=[jax.ShapeDtypeStruct((t, DFF), F32), jax.ShapeDtypeStruct((t, DFF), BF16)],
                out_specs=[_tile_spec(tt, 512), _tile_spec(tt, 512)])
    ff = _mm(r, w["mlp_w2"], name="mlp2")
    dy, dff, dgate2, loss_part = _loss_head(x1, bsd(ff), gate2, target, name="loss_head")

    g = {}

    def relu2_bwd(acc, ex, outs):
        outs[0][...] = (acc * (2.0 * jnp.maximum(ex[0][...], 0.0))).astype(BF16)

    dff2 = flat2(dff)
    da1 = _mm(dff2, w["mlp_w2"], tb=True, name="mlp2_dx", epilogue=relu2_bwd, extras=(a1,),
              extra_specs=(_tile_spec(tt, 512),), out_shape=jax.ShapeDtypeStruct((t, DFF), BF16),
              out_specs=_tile_spec(tt, 512))
    g["mlp_w2"] = _mm(r, dff2, ta=True, name="mlp2_dw")
    dh2 = _mm(da1, w["mlp_w1"], tb=True, name="mlp1_dx")
    g["mlp_w1"] = _mm(flat2(h2), da1, ta=True, name="mlp1_dw")
    dx1, dscale2, dshift2, dg2, dgate1, dmixed = _norm_mod_bwd(
        bsd(dh2), x1, dy, small["norm2_g"], scale2, gate1, bsd(mixed), name="norm2_bwd")
    dmixed2 = flat2(dmixed)
    dmi = _mm(dmixed2, w["w_out"], tb=True, name="w_out_dx")
    g["w_out"] = _mm(flat2(mixed_in), dmixed2, ta=True, name="w_out_dw")
    dy_a, dy_b, dl_a, dl_b, db_a, db_b = _merge_bwd(bsd(dmi), proj3, small["b_merge"], bsd(y_a), bsd(y_b), name="merge_bwd")
    dy_a2, dy_b2 = flat2(dy_a), flat2(dy_b)
    dog = _mm(dy_a2, w["gla_w_o"], tb=True, name="gla_out_dx")
    g["gla_w_o"] = _mm(flat2(o_gated), dy_a2, ta=True, name="gla_out_dw")
    dq_g, dk_g, dv_g, dg_g, dlog, db_alpha, d_ong = _gla_bwd(
        bsd(dog), o, states, proj3, w_alpha_p, small["gla_b_alpha"], small["gla_out_norm_g"], name="gla_bwd")
    dlog2 = flat2(dlog)
    da_p = _mm(dlog2, w_alpha_p, tb=True, out_dtype=BF16, name="alpha_dx")
    d_w_alpha = _mm(proj[:, OFF_A:OFF_A + LANE], dlog2, ta=True, name="alpha_dw")[:GLR]
    do_attn = _mm(dy_b2, w["mla_w_o"], tb=True, out_dtype=BF16, name="mla_out_dx")
    g["mla_w_o"] = _mm(flat2(o_attn), dy_b2, ta=True, name="mla_out_dw")
    dqf, dkf, dvf = _attn_bwd(bsd(qf), bsd(kf), bsd(vf), bsd(do_attn), name="attn_bwd")
    dq_raw, dkv, dkpe, dgq, dgk = _qk_prep_bwd(flat2(dqf), flat2(dkf), flat2(dvf), q_raw, kv, proj, cos_t, sin_t, gq, gk,
                                                name="qk_prep_bwd")
    dcq_n = _mm(dq_raw, w["mla_w_uq"], tb=True, name="mla_uq_dx")
    g["mla_w_uq"] = _mm(cq_n, dq_raw, ta=True, name="mla_uq_dw")
    dckv_n = _mm(dkv, w["mla_w_ukv"], tb=True, name="mla_ukv_dx")
    g["mla_w_ukv"] = _mm(ckv_n, dkv, ta=True, name="mla_ukv_dw")
    dcq, dckv, dg_qlat, dg_kvlat = _lat_norm_bwd(dcq_n, dckv_n, proj, small["mla_q_lat_g"], small["mla_kv_lat_g"],
                                                  name="lat_norm_bwd")
    dproj = jnp.concatenate([flat2(dq_g), flat2(dk_g), flat2(dv_g), flat2(dg_g), flat2(dl_a), flat2(dl_b), dcq, dckv,
                             da_p, dkpe.astype(BF16)], axis=1)
    dh = _mm(dproj, w["w_in"], tb=True, name="proj_dx", tk=640)
    g["w_in"] = _mm(flat2(h), dproj, ta=True, name="proj_dw", tn=640)
    grad_x, dscale1, dshift1, dg1 = _norm_mod_bwd(bsd(dh), x, dx1, small["norm1_g"], scale1, name="norm1_bwd")

    dmod = jnp.concatenate([dshift1, dscale1, dgate1, dshift2, dscale2, dgate2], axis=-1).reshape(bsz, 6 * D)
    gs = {"norm1_g": dg1, "b_merge": jnp.concatenate([db_a, db_b], axis=1), "gla_b_alpha": db_alpha,
          "gla_out_norm_g": d_ong, "mla_q_lat_g": dg_qlat, "mla_kv_lat_g": dg_kvlat, "mla_qn_g": dgq[:, :MQK],
          "mla_kn_g": dgk[:, :MQK], "norm2_g": dg2}
    return loss_part[0, 0], grad_x, dmod, g, gs, d_w_alpha


def kernel(x, c, positions, w_ada, b_ada, norm1_g, w_in, b_merge, gla_w_alpha, gla_b_alpha, gla_out_norm_g, gla_w_o, mla_q_lat_g, mla_w_uq, mla_kv_lat_g, mla_w_ukv, mla_qn_g, mla_kn_g, mla_w_o, w_out, norm2_g, mlp_w1, mlp_w2, loss_target, m_w_ada, m_b_ada, m_norm1_g, m_w_in, m_b_merge, m_gla_w_alpha, m_gla_b_alpha, m_gla_out_norm_g, m_gla_w_o, m_mla_q_lat_g, m_mla_w_uq, m_mla_kv_lat_g, m_mla_w_ukv, m_mla_qn_g, m_mla_kn_g, m_mla_w_o, m_w_out, m_norm2_g, m_mlp_w1, m_mlp_w2, v_w_ada, v_b_ada, v_norm1_g, v_w_in, v_b_merge, v_gla_w_alpha, v_gla_b_alpha, v_gla_out_norm_g, v_gla_w_o, v_mla_q_lat_g, v_mla_w_uq, v_mla_kv_lat_g, v_mla_w_ukv, v_mla_qn_g, v_mla_kn_g, v_mla_w_o, v_w_out, v_norm2_g, v_mlp_w1, v_mlp_w2):
    args = dict(locals())
    names_big = [n for n, _, _ in BIG]
    names_small = [n for n, _ in SMALL]
    bsz = x.shape[0]
    ax, ay, ac = lax.axis_index("x"), lax.axis_index("y"), lax.axis_index("c")
    chip = 2 * ax + ay
    dev = 2 * chip + ac

    wp = _pack([args[n][0] for n in names_big], multiple=PACK_ROWS)
    slots = _all_gather_chips(wp.astype(BF16), name="comm_weights")
    w = _full_weights(slots)
    small = {n: args[n] for n in names_small}
    w_alpha_all = _all_gather8(gla_w_alpha[0], name="comm_w_alpha")
    small["gla_w_alpha"] = jnp.concatenate([w_alpha_all[2 * j] for j in range(4)], axis=1)

    c_all = _all_gather8(c, name="comm_c").reshape(8 * bsz, D)

    def add_bias(acc, ex, outs):
        outs[0][...] = acc + ex[0][...]

    silu = lambda v: v * _sigmoid(v)
    b_ada_mine = lax.dynamic_slice(b_ada, (0, chip * ADA_SHARD[1]), (1, ADA_SHARD[1]))
    mod_part = _mm(c_all, w_ada[0], name="ada", a_fn=silu, epilogue=add_bias, extras=(b_ada_mine,),
                   extra_specs=(pl.BlockSpec((1, 512), lambda i, j, k: (0, j)),),
                   out_shape=jax.ShapeDtypeStruct((8 * bsz, ADA_SHARD[1]), F32), out_specs=_tile_spec(8 * bsz, 512))
    mod_all = _all_gather8(mod_part, name="comm_mod")
    mod_rows = lax.dynamic_slice(mod_all, (0, dev * bsz, 0), (8, bsz, ADA_SHARD[1]))
    mod = jnp.concatenate([mod_rows[2 * j] for j in range(4)], axis=1)

    loss_part, grad_x, dmod, g, gs, d_w_alpha = _local_step(x, positions, mod, loss_target, w, small)
    loss = lax.psum(loss_part * (0.5 / D), ("x", "y", "c"))

    dmod_all = _all_gather8(dmod, name="comm_dmod").reshape(8 * bsz, 6 * D)
    dmod_mine = lax.dynamic_slice(dmod_all, (0, chip * ADA_SHARD[1]), (8 * bsz, ADA_SHARD[1]))
    g_w_ada = _mm(c_all, dmod_mine, ta=True, a_fn=silu, name="ada_dw")
    g_b_ada = _sum_slots(dmod_all.reshape(8 * bsz, 6 * D // LANE, LANE), name="sum_b_ada", tr=6 * D // LANE)

    names_red = [n for n in names_small if n != "b_ada"]
    red_shapes = [(1, n) for name, n in SMALL if name != "b_ada"] + [(GLR, GH * GDK)]
    gs_packed = _pack([gs[n] for n in names_red] + [d_w_alpha], multiple=8)
    gs_sum = _sum_slots(_all_gather8(gs_packed, name="comm_small"), name="sum_small")
    gs_full = _unpack(gs_sum, red_shapes)
    g_small = dict(zip(names_red, gs_full[:-1]))
    g_small["b_ada"] = g_b_ada.reshape(1, 6 * D)
    g_w_alpha = lax.dynamic_slice(gs_full[-1], (0, chip * GDK), (GLR, GDK))

    gslots = _grad_slots(g)
    rows = gslots.shape[1]
    hr = rows // 2
    sib_half = _pair_swap_halves(gslots, name="comm_pair_sum")
    my_half = lax.dynamic_slice(gslots, (0, ac * hr, 0), (4, hr, LANE))
    pair = _add2(my_half, sib_half, name="pair_add")
    from_chips = _scatter_chips(pair, name="comm_scatter")
    g_half = _sum_slots(from_chips, name="chip_sum")
    g_big = _pair_join_halves(g_half, name="comm_pair_join")

    big_shapes = [shp for _, shp, _ in BIG]
    all_names = names_big + ["w_ada"] + names_small + ["gla_w_alpha"]
    all_shapes = big_shapes + [ADA_SHARD] + [(1, n) for _, n in SMALL] + [(GLR, GDK)]
    g_list = _unpack(g_big, big_shapes) + [g_w_ada] + [g_small[n] for n in names_small] + [g_w_alpha]
    pack_all = lambda prefix: _pack([args[prefix + n].reshape(shp) for n, shp in zip(all_names, all_shapes)], multiple=1024)
    delta, new_m, new_v = _adamw(pack_all(""), _pack(g_list, multiple=1024), pack_all("m_"), pack_all("v_"), name="adamw")

    order = ["w_ada", "b_ada", "norm1_g", "w_in", "b_merge", "gla_w_alpha", "gla_b_alpha", "gla_out_norm_g", "gla_w_o",
             "mla_q_lat_g", "mla_w_uq", "mla_kv_lat_g", "mla_w_ukv", "mla_qn_g", "mla_kn_g", "mla_w_o", "w_out",
             "norm2_g", "mlp_w1", "mlp_w2"]

    def named(flat_or_list):
        vals = flat_or_list if isinstance(flat_or_list, list) else _unpack(flat_or_list, all_shapes)
        d = dict(zip(all_names, vals))
        return [d[n].reshape(args[n].shape) for n in order]

    return (loss, grad_x, *named(g_list), *named(delta), *named(new_m), *named(new_v))
```

```python
import functools

import jax
import jax.numpy as jnp
import numpy as np
from jax import lax
from jax.experimental import pallas as pl
from jax.experimental.pallas import tpu as pltpu

F32 = jnp.float32
BF16 = jnp.bfloat16
MESH = pl.DeviceIdType.MESH

D = 1024
CHUNK = 64
EPS = 1e-6
GH, GDK, GDV, GLR, GTAU = 4, 128, 256, 16, 16.0
MH, MQR, MKVR, MNOPE, MROPE, MVD = 16, 256, 128, 64, 32, 64
MQK = MNOPE + MROPE
DFF = 4 * D
ROPE_THETA = 10000.0
IN_WIDTH = 5552
LANE = 128
OFF_Q, OFF_K, OFF_V, OFF_G, OFF_MA, OFF_MB, OFF_CQ, OFF_CKV, OFF_A, OFF_KPE, PW = (
    0, 512, 1024, 2048, 3072, 4096, 5120, 5376, 5504, 5632, 5760)
ADAM_LR, ADAM_B1, ADAM_B2, ADAM_EPS, ADAM_WD, ADAM_STEP = 0.001, 0.9, 0.999, 1e-08, 0.01, 10
VMEM_LIMIT = 48 * 1024 * 1024


def _params(n_axes):
    return pltpu.CompilerParams(dimension_semantics=("arbitrary",) * n_axes, vmem_limit_bytes=VMEM_LIMIT)


def _tile(n, target):
    if n <= target:
        return n
    best = None
    for t in range(LANE, target + 1, LANE):
        if n % t == 0:
            best = t
    assert best is not None, (n, target)
    return best


def _sigmoid(x):
    return 1.0 / (1.0 + jnp.exp(-x))


def _mm(a, b, *, name, ta=False, tb=False, out_dtype=F32, tm=1024, tn=1024, tk=1024,
        epilogue=None, extras=(), extra_specs=(), out_shape=None, out_specs=None, a_fn=None):
    if ta:
        kdim, m = a.shape
    else:
        m, kdim = a.shape
    if tb:
        n, k2 = b.shape
    else:
        k2, n = b.shape
    assert kdim == k2, (a.shape, b.shape)
    tm, tn, tk = _tile(m, tm), _tile(n, tn), _tile(kdim, tk)
    nk = kdim // tk
    a_spec = pl.BlockSpec((tk, tm), lambda i, j, k: (k, i)) if ta else pl.BlockSpec((tm, tk), lambda i, j, k: (i, k))
    b_spec = pl.BlockSpec((tn, tk), lambda i, j, k: (j, k)) if tb else pl.BlockSpec((tk, tn), lambda i, j, k: (k, j))
    dims = (((0 if ta else 1,), (1 if tb else 0,)), ((), ()))
    ne = len(extras)
    if out_shape is None:
        out_shape = jax.ShapeDtypeStruct((m, n), out_dtype)
        out_specs = pl.BlockSpec((tm, tn), lambda i, j, k: (i, j))

    def body(a_ref, b_ref, *rest):
        ex, outs, acc = rest[:ne], rest[ne:-1], rest[-1]
        k = pl.program_id(2)

        @pl.when(k == 0)
        def _():
            acc[...] = jnp.zeros_like(acc)

        av = a_ref[...] if a_fn is None else a_fn(a_ref[...])
        acc[...] += lax.dot_general(av.astype(BF16), b_ref[...].astype(BF16), dims, preferred_element_type=F32)

        @pl.when(k == nk - 1)
        def _():
            if epilogue is None:
                outs[0][...] = acc[...].astype(outs[0].dtype)
            else:
                epilogue(acc[...], ex, outs)

    return pl.pallas_call(
        body, name=name, grid=(m // tm, n // tn, nk),
        in_specs=[a_spec, b_spec, *extra_specs], out_specs=out_specs, out_shape=out_shape,
        scratch_shapes=[pltpu.VMEM((tm, tn), F32)], compiler_params=_params(3),
    )(a, b, *extras)


def _tile_spec(tm, tn):
    return pl.BlockSpec((tm, tn), lambda i, j, k: (i, j))


def _rms(x, g):
    r = lax.rsqrt(jnp.mean(x * x, axis=-1, keepdims=True) + EPS)
    return x * r, r


def _row_spec(ts, width, col=0):
    return pl.BlockSpec((None, ts, width), lambda b, i: (b, i, col))


def _vec_spec(width):
    return pl.BlockSpec((None, 1, width), lambda b, i: (b, 0, 0))


def _gain_spec(width):
    return pl.BlockSpec((1, width), lambda b, i: (0, 0))


def _norm_mod(x, g, scale, shift, *, name, ts=256):
    bsz, s, d = x.shape
    ts = min(ts, s)

    def body(x_ref, g_ref, sc_ref, sh_ref, h_ref):
        xh, _ = _rms(x_ref[...], None)
        h_ref[...] = ((xh * g_ref[...]) * (1.0 + sc_ref[...]) + sh_ref[...]).astype(BF16)

    return pl.pallas_call(
        body, name=name, grid=(bsz, s // ts),
        in_specs=[_row_spec(ts, d), _gain_spec(d), _vec_spec(d), _vec_spec(d)],
        out_specs=_row_spec(ts, d), out_shape=jax.ShapeDtypeStruct((bsz, s, d), BF16),
        compiler_params=_params(2),
    )(x, g, scale, shift)


def _resid_norm_mod(x, mixed, gate, g, scale, shift, *, name, ts=256):
    bsz, s, d = x.shape
    ts = min(ts, s)

    def body(x_ref, mx_ref, gt_ref, g_ref, sc_ref, sh_ref, x1_ref, h_ref):
        x1 = x_ref[...] + gt_ref[...] * mx_ref[...]
        x1_ref[...] = x1
        xh, _ = _rms(x1, None)
        h_ref[...] = ((xh * g_ref[...]) * (1.0 + sc_ref[...]) + sh_ref[...]).astype(BF16)

    return pl.pallas_call(
        body, name=name, grid=(bsz, s // ts),
        in_specs=[_row_spec(ts, d), _row_spec(ts, d), _vec_spec(d), _gain_spec(d), _vec_spec(d), _vec_spec(d)],
        out_specs=[_row_spec(ts, d), _row_spec(ts, d)],
        out_shape=[jax.ShapeDtypeStruct((bsz, s, d), F32), jax.ShapeDtypeStruct((bsz, s, d), BF16)],
        compiler_params=_params(2),
    )(x, mixed, gate, g, scale, shift)


def _norm_mod_bwd(dh, xin, resid, g, scale, gate=None, mixed=None, *, name, ts=256):
    bsz, s, d = xin.shape
    ts = min(ts, s)
    gated = gate is not None

    def body(*refs):
        if gated:
            dh_ref, x_ref, rs_ref, g_ref, sc_ref, gt_ref, mx_ref, dx_ref, dsc_ref, dsh_ref, dg_ref, dgt_ref, dmx_ref = refs
        else:
            dh_ref, x_ref, rs_ref, g_ref, sc_ref, dx_ref, dsc_ref, dsh_ref, dg_ref = refs
        b, i = pl.program_id(0), pl.program_id(1)

        @pl.when(i == 0)
        def _():
            dsc_ref[...] = jnp.zeros_like(dsc_ref)
            dsh_ref[...] = jnp.zeros_like(dsh_ref)
            if gated:
                dgt_ref[...] = jnp.zeros_like(dgt_ref)

        @pl.when((i == 0) & (b == 0))
        def _():
            dg_ref[...] = jnp.zeros_like(dg_ref)

        dh_v, gv = dh_ref[...], g_ref[...]
        xh, r = _rms(x_ref[...], None)
        dsc_ref[...] += jnp.sum(dh_v * (xh * gv), axis=0, keepdims=True)
        dsh_ref[...] += jnp.sum(dh_v, axis=0, keepdims=True)
        dn = dh_v * (1.0 + sc_ref[...])
        dg_ref[...] += jnp.sum(dn * xh, axis=0, keepdims=True)
        dxh = dn * gv
        dx = rs_ref[...] + r * (dxh - xh * jnp.mean(dxh * xh, axis=-1, keepdims=True))
        dx_ref[...] = dx
        if gated:
            dgt_ref[...] += jnp.sum(dx * mx_ref[...], axis=0, keepdims=True)
            dmx_ref[...] = (dx * gt_ref[...]).astype(BF16)

    ins = [dh, xin, resid, g, scale]
    in_specs = [_row_spec(ts, d), _row_spec(ts, d), _row_spec(ts, d), _gain_spec(d), _vec_spec(d)]
    out_specs = [_row_spec(ts, d), _vec_spec(d), _vec_spec(d), _gain_spec(d)]
    out_shape = [jax.ShapeDtypeStruct((bsz, s, d), F32), jax.ShapeDtypeStruct((bsz, 1, d), F32),
                 jax.ShapeDtypeStruct((bsz, 1, d), F32), jax.ShapeDtypeStruct((1, d), F32)]
    if gated:
        ins += [gate, mixed]
        in_specs += [_vec_spec(d), _row_spec(ts, d)]
        out_specs += [_vec_spec(d), _row_spec(ts, d)]
        out_shape += [jax.ShapeDtypeStruct((bsz, 1, d), F32), jax.ShapeDtypeStruct((bsz, s, d), BF16)]
    return pl.pallas_call(
        body, name=name, grid=(bsz, s // ts), in_specs=in_specs, out_specs=out_specs, out_shape=out_shape,
        compiler_params=_params(2),
    )(*ins)


def _loss_head(x1, ff, gate2, target, *, name, ts=256):
    bsz, s, d = x1.shape
    ts = min(ts, s)

    def body(x1_ref, ff_ref, gt_ref, t_ref, dy_ref, dff_ref, dgt_ref, loss_ref, acc):
        b, i = pl.program_id(0), pl.program_id(1)

        @pl.when(i == 0)
        def _():
            dgt_ref[...] = jnp.zeros_like(dgt_ref)

        @pl.when((i == 0) & (b == 0))
        def _():
            acc[...] = jnp.zeros_like(acc)

        ffv, gt = ff_ref[...], gt_ref[...]
        diff = (x1_ref[...] + gt * ffv) - t_ref[...]
        acc[...] += jnp.sum((diff * diff).reshape(ts // 8, 8, d), axis=0)
        dy = diff * (1.0 / d)
        dy_ref[...] = dy
        dgt_ref[...] += jnp.sum(dy * ffv, axis=0, keepdims=True)
        dff_ref[...] = (dy * gt).astype(BF16)

        @pl.when((i == pl.num_programs(1) - 1) & (b == pl.num_programs(0) - 1))
        def _():
            loss_ref[...] = jnp.full(loss_ref.shape, jnp.sum(acc[...]), F32)

    return pl.pallas_call(
        body, name=name, grid=(bsz, s // ts),
        in_specs=[_row_spec(ts, d), _row_spec(ts, d), _vec_spec(d), _row_spec(ts, d)],
        out_specs=[_row_spec(ts, d), _row_spec(ts, d), _vec_spec(d), pl.BlockSpec((8, LANE), lambda b, i: (0, 0))],
        out_shape=[jax.ShapeDtypeStruct((bsz, s, d), F32), jax.ShapeDtypeStruct((bsz, s, d), BF16),
                   jax.ShapeDtypeStruct((bsz, 1, d), F32), jax.ShapeDtypeStruct((8, LANE), F32)],
        scratch_shapes=[pltpu.VMEM((8, d), F32)], compiler_params=_params(2),
    )(x1, ff, gate2, target)


def _merge_fwd(proj, b_merge, y_a, y_b, *, name, ts=256):
    bsz, s, _ = proj.shape
    ts = min(ts, s)

    def body(la_ref, lb_ref, ba_ref, bb_ref, ya_ref, yb_ref, out_ref):
        ga = _sigmoid(la_ref[...] + ba_ref[...])
        gb = _sigmoid(lb_ref[...] + bb_ref[...])
        out_ref[...] = (ga * ya_ref[...] + gb * yb_ref[...]).astype(BF16)

    return pl.pallas_call(
        body, name=name, grid=(bsz, s // ts),
        in_specs=[_row_spec(ts, D, OFF_MA // D), _row_spec(ts, D, OFF_MB // D),
                  pl.BlockSpec((1, D), lambda b, i: (0, 0)), pl.BlockSpec((1, D), lambda b, i: (0, 1)),
                  _row_spec(ts, D), _row_spec(ts, D)],
        out_specs=_row_spec(ts, D), out_shape=jax.ShapeDtypeStruct((bsz, s, D), BF16),
        compiler_params=_params(2),
    )(proj, proj, b_merge, b_merge, y_a, y_b)


def _merge_bwd(dmi, proj, b_merge, y_a, y_b, *, name, ts=256):
    bsz, s, _ = proj.shape
    ts = min(ts, s)

    def body(d_ref, la_ref, lb_ref, ba_ref, bb_ref, ya_ref, yb_ref, dya_ref, dyb_ref, dla_ref, dlb_ref, dba_ref, dbb_ref):
        @pl.when((pl.program_id(0) == 0) & (pl.program_id(1) == 0))
        def _():
            dba_ref[...] = jnp.zeros_like(dba_ref)
            dbb_ref[...] = jnp.zeros_like(dbb_ref)

        dv = d_ref[...]
        ga = _sigmoid(la_ref[...] + ba_ref[...])
        gb = _sigmoid(lb_ref[...] + bb_ref[...])
        dya_ref[...] = (dv * ga).astype(BF16)
        dyb_ref[...] = (dv * gb).astype(BF16)
        dla = (dv * ya_ref[...]) * (ga * (1.0 - ga))
        dlb = (dv * yb_ref[...]) * (gb * (1.0 - gb))
        dla_ref[...] = dla.astype(BF16)
        dlb_ref[...] = dlb.astype(BF16)
        dba_ref[...] += jnp.sum(dla, axis=0, keepdims=True)
        dbb_ref[...] += jnp.sum(dlb, axis=0, keepdims=True)

    act = jax.ShapeDtypeStruct((bsz, s, D), BF16)
    return pl.pallas_call(
        body, name=name, grid=(bsz, s // ts),
        in_specs=[_row_spec(ts, D), _row_spec(ts, D, OFF_MA // D), _row_spec(ts, D, OFF_MB // D),
                  pl.BlockSpec((1, D), lambda b, i: (0, 0)), pl.BlockSpec((1, D), lambda b, i: (0, 1)),
                  _row_spec(ts, D), _row_spec(ts, D)],
        out_specs=[_row_spec(ts, D)] * 4 + [_gain_spec(D)] * 2,
        out_shape=[act, act, act, act, jax.ShapeDtypeStruct((1, D), F32), jax.ShapeDtypeStruct((1, D), F32)],
        compiler_params=_params(2),
    )(dmi, proj, proj, b_merge, b_merge, y_a, y_b)


def _tri(lower):
    r = lax.broadcasted_iota(jnp.int32, (CHUNK, CHUNK), 0)
    c = lax.broadcasted_iota(jnp.int32, (CHUNK, CHUNK), 1)
    return jnp.where((c <= r) if lower else (c >= r), 1.0, 0.0).astype(F32)


def _gla_decay(a_ref, wal_ref, bal_ref):
    logits = jnp.dot(a_ref[...].astype(BF16), wal_ref[...].astype(BF16), preferred_element_type=F32) + bal_ref[...]
    la = (jnp.minimum(logits, 0.0) - jnp.log(1.0 + jnp.exp(-jnp.abs(logits)))) * (1.0 / GTAU)
    cum = jnp.dot(_tri(True), la, preferred_element_type=F32, precision=lax.Precision.HIGHEST)
    cum_end = jnp.sum(la, axis=0, keepdims=True)
    return logits, cum, cum_end


def _gla_fwd(proj, w_alpha_p, b_alpha, out_norm_g, *, name):
    bsz, s, _ = proj.shape
    nc = s // CHUNK
    scale = GDK ** -0.5

    def body(q_ref, k_ref, v_ref, g_ref, a_ref, wal_ref, bal_ref, ong_ref, o_ref, og_ref, st_ref, st):
        @pl.when(pl.program_id(2) == 0)
        def _():
            st[...] = jnp.zeros_like(st)

        _, cum, cum_end = _gla_decay(a_ref, wal_ref, bal_ref)
        kd = k_ref[...] * jnp.exp(cum_end - cum)
        ut = lax.dot_general(v_ref[...].astype(BF16), kd.astype(BF16), (((0,), (0,)), ((), ())),
                             preferred_element_type=F32)
        new = st[...] * jnp.exp(cum_end) + ut
        st[...] = new
        st_ref[...] = new
        o = lax.dot_general((q_ref[...] * scale).astype(BF16), new.astype(BF16), (((1,), (1,)), ((), ())),
                            preferred_element_type=F32)
        o_ref[...] = o
        oh, _ = _rms(o, None)
        gv = g_ref[...]
        og_ref[...] = ((oh * ong_ref[...]) * (gv * _sigmoid(gv))).astype(BF16)

    def blk(width, off):
        return pl.BlockSpec((None, CHUNK, width), lambda h, b, n: (b, n, off // width + h))

    return pl.pallas_call(
        body, name=name, grid=(GH, bsz, nc),
        in_specs=[blk(GDK, OFF_Q), blk(GDK, OFF_K), blk(GDV, OFF_V), blk(GDV, OFF_G),
                  pl.BlockSpec((None, CHUNK, LANE), lambda h, b, n: (b, n, OFF_A // LANE)),
                  pl.BlockSpec((LANE, GDK), lambda h, b, n: (0, h)), pl.BlockSpec((1, GDK), lambda h, b, n: (0, h)),
                  pl.BlockSpec((1, GDV), lambda h, b, n: (0, 0))],
        out_specs=[pl.BlockSpec((None, CHUNK, GDV), lambda h, b, n: (b, n, h)),
                   pl.BlockSpec((None, CHUNK, GDV), lambda h, b, n: (b, n, h)),
                   pl.BlockSpec((None, None, None, GDV, GDK), lambda h, b, n: (b, h, n, 0, 0))],
        out_shape=[jax.ShapeDtypeStruct((bsz, s, GH * GDV), F32), jax.ShapeDtypeStruct((bsz, s, GH * GDV), BF16),
                   jax.ShapeDtypeStruct((bsz, GH, nc, GDV, GDK), F32)],
        scratch_shapes=[pltpu.VMEM((GDV, GDK), F32)], compiler_params=_params(3),
    )(proj, proj, proj, proj, proj, w_alpha_p, b_alpha, out_norm_g)


def _gla_bwd(dog, o, states, proj, w_alpha_p, b_alpha, out_norm_g, *, name):
    bsz, s, _ = proj.shape
    nc = s // CHUNK
    scale = GDK ** -0.5

    def body(dog_ref, o_ref, st_ref, sp_ref, q_ref, k_ref, v_ref, g_ref, a_ref, wal_ref, bal_ref, ong_ref,
             dq_ref, dk_ref, dv_ref, dg_ref, dl_ref, dbal_ref, dong_ref, carry):
        h, b, t = pl.program_id(0), pl.program_id(1), pl.program_id(2)
        n = nc - 1 - t

        @pl.when(t == 0)
        def _():
            carry[...] = jnp.zeros_like(carry)

        @pl.when((t == 0) & (b == 0))
        def _():
            dbal_ref[...] = jnp.zeros_like(dbal_ref)

        @pl.when((t == 0) & (b == 0) & (h == 0))
        def _():
            dong_ref[...] = jnp.zeros_like(dong_ref)

        gv, ov, dogv, ong = g_ref[...], o_ref[...], dog_ref[...], ong_ref[...]
        sg = _sigmoid(gv)
        oh, r = _rms(ov, None)
        don = dogv * (gv * sg)
        dg_ref[...] = (dogv * (oh * ong) * (sg * (1.0 + gv * (1.0 - sg)))).astype(BF16)
        dong_ref[...] += jnp.sum(don * oh, axis=0, keepdims=True)
        doh = don * ong
        do = r * (doh - oh * jnp.mean(doh * oh, axis=-1, keepdims=True))
        do_b = do.astype(BF16)

        logits, cum, cum_end = _gla_decay(a_ref, wal_ref, bal_ref)
        decay = jnp.exp(cum_end)
        w = jnp.exp(cum_end - cum)
        kv_ = k_ref[...]
        kd = kv_ * w
        stv = st_ref[...]
        qs_b = (q_ref[...] * scale).astype(BF16)
        dq_ref[...] = (jnp.dot(do_b, stv.astype(BF16), preferred_element_type=F32) * scale).astype(BF16)
        dsn = lax.dot_general(do_b, qs_b, (((0,), (0,)), ((), ())), preferred_element_type=F32) + carry[...]
        sprev = jnp.where(n > 0, sp_ref[...], 0.0)
        ddecay = jnp.sum(dsn * sprev, axis=0, keepdims=True)
        carry[...] = dsn * decay
        dsn_b = dsn.astype(BF16)
        dv_ref[...] = lax.dot_general(kd.astype(BF16), dsn_b, (((1,), (1,)), ((), ())),
                                      preferred_element_type=F32).astype(BF16)
        dkd = jnp.dot(v_ref[...].astype(BF16), dsn_b, preferred_element_type=F32)
        dk_ref[...] = (dkd * w).astype(BF16)
        e = dkd * kd
        dcum_end = jnp.sum(e, axis=0, keepdims=True) + ddecay * decay
        dla = dcum_end - jnp.dot(_tri(False), e, preferred_element_type=F32, precision=lax.Precision.HIGHEST)
        dlog = dla * (1.0 / GTAU) * (1.0 - _sigmoid(logits))
        dl_ref[...] = dlog.astype(BF16)
        dbal_ref[...] += jnp.sum(dlog, axis=0, keepdims=True)

    def blk(width, off):
        return pl.BlockSpec((None, CHUNK, width), lambda h, b, t: (b, nc - 1 - t, off // width + h))

    def stblk(prev):
        def im(h, b, t):
            n = nc - 1 - t
            return (b, h, jnp.maximum(n - 1, 0) if prev else n, 0, 0)
        return pl.BlockSpec((None, None, None, GDV, GDK), im)

    act = lambda wd: jax.ShapeDtypeStruct((bsz, s, wd), BF16)
    return pl.pallas_call(
        body, name=name, grid=(GH, bsz, nc),
        in_specs=[blk(GDV, 0), blk(GDV, 0), stblk(False), stblk(True),
                  blk(GDK, OFF_Q), blk(GDK, OFF_K), blk(GDV, OFF_V), blk(GDV, OFF_G),
                  pl.BlockSpec((None, CHUNK, LANE), lambda h, b, t: (b, nc - 1 - t, OFF_A // LANE)),
                  pl.BlockSpec((LANE, GDK), lambda h, b, t: (0, h)), pl.BlockSpec((1, GDK), lambda h, b, t: (0, h)),
                  pl.BlockSpec((1, GDV), lambda h, b, t: (0, 0))],
        out_specs=[blk(GDK, 0), blk(GDK, 0), blk(GDV, 0), blk(GDV, 0), blk(GDK, 0),
                   pl.BlockSpec((1, GDK), lambda h, b, t: (0, h)), pl.BlockSpec((1, GDV), lambda h, b, t: (0, 0))],
        out_shape=[act(GH * GDK), act(GH * GDK), act(GH * GDV), act(GH * GDV), act(GH * GDK),
                   jax.ShapeDtypeStruct((1, GH * GDK), F32), jax.ShapeDtypeStruct((1, GDV), F32)],
        scratch_shapes=[pltpu.VMEM((GDV, GDK), F32)], compiler_params=_params(3),
    )(dog, o, states, states, proj, proj, proj, proj, proj, w_alpha_p, b_alpha, out_norm_g)


def _lane():
    return lax.broadcasted_iota(jnp.int32, (1, LANE), 1)


def _swap_halves(x):
    lane = _lane()
    half = MROPE // 2
    lo = (lane >= MNOPE) & (lane < MNOPE + half)
    hi = (lane >= MNOPE + half) & (lane < MQK)
    return jnp.where(lo, pltpu.roll(x, LANE - half, 1), jnp.where(hi, pltpu.roll(x, half, 1), 0.0))


def _norm96(x, g):
    r = lax.rsqrt(jnp.sum(x * x, axis=-1, keepdims=True) * (1.0 / MQK) + EPS)
    return x * r, r


def _lat_norm(proj, q_lat_g, kv_lat_g, *, name, ts=512):
    t = proj.shape[0]
    ts = min(ts, t)

    def body(cq_ref, ckv_ref, gq_ref, gk_ref, oq_ref, ok_ref):
        xq, _ = _rms(cq_ref[...], None)
        oq_ref[...] = (xq * gq_ref[...]).astype(BF16)
        xk, _ = _rms(ckv_ref[...], None)
        ok_ref[...] = (xk * gk_ref[...]).astype(BF16)

    return pl.pallas_call(
        body, name=name, grid=(t // ts,),
        in_specs=[pl.BlockSpec((ts, MQR), lambda i: (i, OFF_CQ // MQR)), pl.BlockSpec((ts, MKVR), lambda i: (i, OFF_CKV // MKVR)),
                  pl.BlockSpec((1, MQR), lambda i: (0, 0)), pl.BlockSpec((1, MKVR), lambda i: (0, 0))],
        out_specs=[pl.BlockSpec((ts, MQR), lambda i: (i, 0)), pl.BlockSpec((ts, MKVR), lambda i: (i, 0))],
        out_shape=[jax.ShapeDtypeStruct((t, MQR), BF16), jax.ShapeDtypeStruct((t, MKVR), BF16)],
        compiler_params=_params(1),
    )(proj, proj, q_lat_g, kv_lat_g)


def _lat_norm_bwd(dcqn, dckvn, proj, q_lat_g, kv_lat_g, *, name, ts=512):
    t = proj.shape[0]
    ts = min(ts, t)

    def one(d_ref, x_ref, g_ref, dx_ref, dg_ref):
        xh, r = _rms(x_ref[...], None)
        dn = d_ref[...]
        dg_ref[...] += jnp.sum(dn * xh, axis=0, keepdims=True)
        dxh = dn * g_ref[...]
        dx_ref[...] = (r * (dxh - xh * jnp.mean(dxh * xh, axis=-1, keepdims=True))).astype(BF16)

    def body(dq_ref, dk_ref, cq_ref, ckv_ref, gq_ref, gk_ref, dxq_ref, dxk_ref, dgq_ref, dgk_ref):
        @pl.when(pl.program_id(0) == 0)
        def _():
            dgq_ref[...] = jnp.zeros_like(dgq_ref)
            dgk_ref[...] = jnp.zeros_like(dgk_ref)

        one(dq_ref, cq_ref, gq_ref, dxq_ref, dgq_ref)
        one(dk_ref, ckv_ref, gk_ref, dxk_ref, dgk_ref)

    return pl.pallas_call(
        body, name=name, grid=(t // ts,),
        in_specs=[pl.BlockSpec((ts, MQR), lambda i: (i, 0)), pl.BlockSpec((ts, MKVR), lambda i: (i, 0)),
                  pl.BlockSpec((ts, MQR), lambda i: (i, OFF_CQ // MQR)), pl.BlockSpec((ts, MKVR), lambda i: (i, OFF_CKV // MKVR)),
                  pl.BlockSpec((1, MQR), lambda i: (0, 0)), pl.BlockSpec((1, MKVR), lambda i: (0, 0))],
        out_specs=[pl.BlockSpec((ts, MQR), lambda i: (i, 0)), pl.BlockSpec((ts, MKVR), lambda i: (i, 0)),
                   pl.BlockSpec((1, MQR), lambda i: (0, 0)), pl.BlockSpec((1, MKVR), lambda i: (0, 0))],
        out_shape=[jax.ShapeDtypeStruct((t, MQR), BF16), jax.ShapeDtypeStruct((t, MKVR), BF16),
                   jax.ShapeDtypeStruct((1, MQR), F32), jax.ShapeDtypeStruct((1, MKVR), F32)],
        compiler_params=_params(1),
    )(dcqn, dckvn, proj, proj, q_lat_g, kv_lat_g)


def _qk_prep(q_raw, kv, proj, cos_t, sin_t, gq, gk, *, name, ts=512):
    t = q_raw.shape[0]
    ts = min(ts, t)

    def body(q_ref, kv_ref, kpe_ref, c_ref, s_ref, gq_ref, gk_ref, qo_ref, ko_ref, vo_ref):
        cs, sn = c_ref[...], s_ref[...]
        nope = _lane() < MNOPE
        qn, _ = _norm96(q_ref[...], None)
        qn = qn * gq_ref[...]
        qo_ref[...] = (qn * cs + _swap_halves(qn) * sn).astype(BF16)
        kvv = kv_ref[...]
        kn, _ = _norm96(jnp.where(nope, kvv, kpe_ref[...]), None)
        kn = kn * gk_ref[...]
        ko_ref[...] = (kn * cs + _swap_halves(kn) * sn).astype(BF16)
        vo_ref[...] = jnp.where(nope, pltpu.roll(kvv, MNOPE, 1), 0.0).astype(BF16)

    hd = pl.BlockSpec((ts, LANE), lambda i, h: (i, h))
    shared = lambda col: pl.BlockSpec((ts, LANE), lambda i, h: (i, col))
    gain = pl.BlockSpec((1, LANE), lambda i, h: (0, 0))
    out = jax.ShapeDtypeStruct((t, MH * LANE), BF16)
    return pl.pallas_call(
        body, name=name, grid=(t // ts, MH),
        in_specs=[hd, hd, shared(OFF_KPE // LANE), shared(0), shared(0), gain, gain],
        out_specs=[hd, hd, hd], out_shape=[out, out, out], compiler_params=_params(2),
    )(q_raw, kv, proj, cos_t, sin_t, gq, gk)


def _qk_prep_bwd(dq, dk, dv, q_raw, kv, proj, cos_t, sin_t, gq, gk, *, name, ts=512):
    t = q_raw.shape[0]
    ts = min(ts, t)

    def norm_bwd(dy, x, g, dg_ref):
        xh, r = _norm96(x, None)
        dg_ref[...] += jnp.sum(dy * xh, axis=0, keepdims=True)
        dxh = dy * g
        return r * (dxh - xh * (jnp.sum(dxh * xh, axis=-1, keepdims=True) * (1.0 / MQK)))

    def body(dq_ref, dk_ref, dv_ref, q_ref, kv_ref, kpe_ref, c_ref, s_ref, gq_ref, gk_ref,
             dqr_ref, dkv_ref, dkpe_ref, dgq_ref, dgk_ref):
        i, h = pl.program_id(0), pl.program_id(1)

        @pl.when(h == 0)
        def _():
            dkpe_ref[...] = jnp.zeros_like(dkpe_ref)

        @pl.when((h == 0) & (i == 0))
        def _():
            dgq_ref[...] = jnp.zeros_like(dgq_ref)
            dgk_ref[...] = jnp.zeros_like(dgk_ref)

        cs, sn = c_ref[...], s_ref[...]
        lane = _lane()
        nope = lane < MNOPE
        dqv = dq_ref[...]
        dqn = dqv * cs + _swap_halves(dqv * sn)
        dqr_ref[...] = norm_bwd(dqn, q_ref[...], gq_ref[...], dgq_ref).astype(BF16)
        dkv_ = dk_ref[...]
        dkn = dkv_ * cs + _swap_halves(dkv_ * sn)
        kvv = kv_ref[...]
        dkr = norm_bwd(dkn, jnp.where(nope, kvv, kpe_ref[...]), gk_ref[...], dgk_ref)
        dkv_ref[...] = jnp.where(nope, dkr, pltpu.roll(dv_ref[...], MNOPE, 1)).astype(BF16)
        dkpe_ref[...] += jnp.where((lane >= MNOPE) & (lane < MQK), dkr, 0.0)

    hd = pl.BlockSpec((ts, LANE), lambda i, h: (i, h))
    shared = lambda col: pl.BlockSpec((ts, LANE), lambda i, h: (i, col))
    gain = pl.BlockSpec((1, LANE), lambda i, h: (0, 0))
    out = jax.ShapeDtypeStruct((t, MH * LANE), BF16)
    return pl.pallas_call(
        body, name=name, grid=(t // ts, MH),
        in_specs=[hd, hd, hd, hd, hd, shared(OFF_KPE // LANE), shared(0), shared(0), gain, gain],
        out_specs=[hd, hd, shared(0), gain, gain],
        out_shape=[out, out, jax.ShapeDtypeStruct((t, LANE), F32), jax.ShapeDtypeStruct((1, LANE), F32),
                   jax.ShapeDtypeStruct((1, LANE), F32)],
        compiler_params=_params(2),
    )(dq, dk, dv, q_raw, kv, proj, cos_t, sin_t, gq, gk)


_NT = (((1,), (1,)), ((), ()))
_TN = (((0,), (0,)), ((), ()))


def _attn_probs(q, k_ref, lo, tq):
    scale = MQK ** -0.5
    row = lax.broadcasted_iota(jnp.int32, (tq, tq), 0) // CHUNK
    col = lax.broadcasted_iota(jnp.int32, (tq, tq), 1) // CHUNK
    sd = lax.dot_general(q, k_ref[pl.ds(lo, tq), :], _NT, preferred_element_type=F32) * scale
    sd = jnp.where(col <= row, sd, -1e30)
    m = jnp.max(sd, axis=-1, keepdims=True)
    if lo:
        so = lax.dot_general(q, k_ref[pl.ds(0, lo), :], _NT, preferred_element_type=F32) * scale
        m = jnp.maximum(m, jnp.max(so, axis=-1, keepdims=True))
        po = jnp.exp(so - m)
        pd = jnp.exp(sd - m)
        inv = 1.0 / (jnp.sum(po, axis=-1, keepdims=True) + jnp.sum(pd, axis=-1, keepdims=True))
        return po * inv, pd * inv
    pd = jnp.exp(sd - m)
    return None, pd * (1.0 / jnp.sum(pd, axis=-1, keepdims=True))


def _attn_fwd(q, k, v, *, name, tq=256):
    bsz, s, _ = q.shape
    tq = min(tq, s)

    def body(q_ref, k_ref, v_ref, o_ref):
        for i in range(s // tq):
            lo = i * tq
            po, pd = _attn_probs(q_ref[pl.ds(lo, tq), :], k_ref, lo, tq)
            o = jnp.dot(pd.astype(BF16), v_ref[pl.ds(lo, tq), :], preferred_element_type=F32)
            if lo:
                o += jnp.dot(po.astype(BF16), v_ref[pl.ds(0, lo), :], preferred_element_type=F32)
            o_ref[pl.ds(lo, tq), :] = o.astype(BF16)

    spec = pl.BlockSpec((None, s, LANE), lambda b, h: (b, 0, h))
    return pl.pallas_call(
        body, name=name, grid=(bsz, MH), in_specs=[spec, spec, spec], out_specs=spec,
        out_shape=jax.ShapeDtypeStruct((bsz, s, MH * LANE), BF16), compiler_params=_params(2),
    )(q, k, v)


def _attn_bwd(q, k, v, do, *, name, tq=256):
    bsz, s, _ = q.shape
    tq = min(tq, s)
    scale = MQK ** -0.5

    def body(q_ref, k_ref, v_ref, do_ref, dq_ref, dk_ref, dv_ref):
        dk_ref[...] = jnp.zeros_like(dk_ref)
        dv_ref[...] = jnp.zeros_like(dv_ref)
        for i in range(s // tq):
            lo = i * tq
            here, before = pl.ds(lo, tq), pl.ds(0, lo)
            qv, dov = q_ref[here, :], do_ref[here, :]
            po, pd = _attn_probs(qv, k_ref, lo, tq)
            dv_ref[here, :] += lax.dot_general(pd.astype(BF16), dov, _TN, preferred_element_type=F32)
            dpd = lax.dot_general(dov, v_ref[here, :], _NT, preferred_element_type=F32)
            delta = jnp.sum(dpd * pd, axis=-1, keepdims=True)
            if lo:
                dv_ref[before, :] += lax.dot_general(po.astype(BF16), dov, _TN, preferred_element_type=F32)
                dpo = lax.dot_general(dov, v_ref[before, :], _NT, preferred_element_type=F32)
                delta += jnp.sum(dpo * po, axis=-1, keepdims=True)
            dsd = (pd * (dpd - delta) * scale).astype(BF16)
            dq = jnp.dot(dsd, k_ref[here, :], preferred_element_type=F32)
            dk_ref[here, :] += lax.dot_general(dsd, qv, _TN, preferred_element_type=F32)
            if lo:
                dso = (po * (dpo - delta) * scale).astype(BF16)
                dq += jnp.dot(dso, k_ref[before, :], preferred_element_type=F32)
                dk_ref[before, :] += lax.dot_general(dso, qv, _TN, preferred_element_type=F32)
            dq_ref[here, :] = dq

    spec = pl.BlockSpec((None, s, LANE), lambda b, h: (b, 0, h))
    out = jax.ShapeDtypeStruct((bsz, s, MH * LANE), F32)
    return pl.pallas_call(
        body, name=name, grid=(bsz, MH), in_specs=[spec] * 4, out_specs=[spec] * 3, out_shape=[out, out, out],
        compiler_params=_params(2),
    )(q, k, v, do)


def _adamw(w, g, m, v, *, name, tr=256):
    rows, cols = w.shape
    tr = _tile_rows(rows, tr)

    def body(w_ref, g_ref, m_ref, v_ref, d_ref, nm_ref, nv_ref):
        gv = g_ref[...]
        nm = ADAM_B1 * m_ref[...] + (1.0 - ADAM_B1) * gv
        nv = ADAM_B2 * v_ref[...] + (1.0 - ADAM_B2) * (gv * gv)
        m_hat = nm / (1.0 - ADAM_B1 ** ADAM_STEP)
        v_hat = nv / (1.0 - ADAM_B2 ** ADAM_STEP)
        d_ref[...] = -ADAM_LR * (m_hat / (jnp.sqrt(v_hat) + ADAM_EPS) + ADAM_WD * w_ref[...])
        nm_ref[...] = nm
        nv_ref[...] = nv

    spec = pl.BlockSpec((tr, cols), lambda i: (i, 0))
    out = jax.ShapeDtypeStruct((rows, cols), F32)
    return pl.pallas_call(body, name=name, grid=(rows // tr,), in_specs=[spec] * 4, out_specs=[spec] * 3,
                          out_shape=[out, out, out], compiler_params=_params(1))(w, g, m, v)


def _tile_rows(rows, target):
    if rows <= target:
        return rows
    best = 8
    for t in range(8, target + 1, 8):
        if rows % t == 0:
            best = t
    return best


def _sum_slots(x, *, name, tr=1024):
    n, rows, _ = x.shape
    tr = _tile_rows(rows, tr)

    def body(x_ref, o_ref):
        acc = x_ref[0].astype(F32)
        for j in range(1, n):
            acc = acc + x_ref[j].astype(F32)
        o_ref[...] = acc

    return pl.pallas_call(
        body, name=name, grid=(rows // tr,), in_specs=[pl.BlockSpec((n, tr, LANE), lambda i: (0, i, 0))],
        out_specs=pl.BlockSpec((tr, LANE), lambda i: (i, 0)), out_shape=jax.ShapeDtypeStruct((rows, LANE), F32),
        compiler_params=_params(1))(x)


def _add2(a, b, *, name, out_dtype=F32, tr=1024):
    n, rows, _ = a.shape
    tr = _tile_rows(rows, tr)

    def body(a_ref, b_ref, o_ref):
        o_ref[...] = (a_ref[...] + b_ref[...]).astype(out_dtype)

    spec = pl.BlockSpec((None, tr, LANE), lambda j, i: (j, i, 0))
    return pl.pallas_call(body, name=name, grid=(n, rows // tr), in_specs=[spec, spec], out_specs=spec,
                          out_shape=jax.ShapeDtypeStruct(a.shape, out_dtype), compiler_params=_params(2))(a, b)


def _me():
    return lax.axis_index("x"), lax.axis_index("y"), lax.axis_index("c")


def _flip(pos, bits):
    x, y, c = pos
    return (x ^ bits[0] if bits[0] else x, y ^ bits[1] if bits[1] else y, c ^ bits[2] if bits[2] else c)


ANY = pl.BlockSpec(memory_space=pl.ANY)


def _all_gather8(x, *, name):
    flips = [((k >> 2) & 1, (k >> 1) & 1, k & 1) for k in range(1, 8)]

    def body(x_ref, out_ref, send_sems, recv_sems, local_sem):
        me = _me()
        slot = lambda p: 4 * p[0] + 2 * p[1] + p[2]
        mine = pltpu.make_async_copy(x_ref, out_ref.at[slot(me)], local_sem)
        mine.start()
        sends = []
        for k, f in enumerate(flips):
            cp = pltpu.make_async_remote_copy(src_ref=x_ref, dst_ref=out_ref.at[slot(me)], send_sem=send_sems.at[k],
                                              recv_sem=recv_sems.at[k], device_id=_flip(me, f), device_id_type=MESH)
            cp.start()
            sends.append(cp)
        for k, f in enumerate(flips):
            peer = _flip(me, f)
            pltpu.make_async_remote_copy(src_ref=x_ref, dst_ref=out_ref.at[slot(peer)], send_sem=send_sems.at[k],
                                         recv_sem=recv_sems.at[k], device_id=peer, device_id_type=MESH).wait_recv()
        for cp in sends:
            cp.wait_send()
        mine.wait()

    return pl.pallas_call(
        body, name=name, in_specs=[ANY], out_specs=ANY, out_shape=jax.ShapeDtypeStruct((8, *x.shape), x.dtype),
        scratch_shapes=[pltpu.SemaphoreType.DMA((7,)), pltpu.SemaphoreType.DMA((7,)), pltpu.SemaphoreType.DMA])(x)


CHIP_FLIPS = [(1, 0, 0), (0, 1, 0), (1, 1, 0)]


def _all_gather_chips(x, *, name):
    rows = x.shape[0]
    hr = rows // 2

    def body(x_ref, out_ref, send_sems, recv_sems, local_sem):
        me = _me()
        sib = _flip(me, (0, 0, 1))
        slot = lambda p: 2 * p[0] + p[1]
        mine_rows = pl.ds(pl.multiple_of(me[2] * hr, 16), hr)
        their_rows = pl.ds(pl.multiple_of((1 - me[2]) * hr, 16), hr)

        def copy(k, src, dst, to):
            return pltpu.make_async_remote_copy(src_ref=src, dst_ref=dst, send_sem=send_sems.at[k],
                                                recv_sem=recv_sems.at[k], device_id=to, device_id_type=MESH)

        mine = pltpu.make_async_copy(x_ref, out_ref.at[slot(me)], local_sem)
        mine.start()
        sends = []
        for k, f in enumerate(CHIP_FLIPS):
            cp = copy(k, x_ref.at[mine_rows, :], out_ref.at[slot(me), mine_rows, :], _flip(me, f))
            cp.start()
            sends.append(cp)
        for k, f in enumerate(CHIP_FLIPS):
            landed = out_ref.at[slot(_flip(me, f)), mine_rows, :]
            copy(k, landed, landed, me).wait_recv()
            cp = copy(3 + k, landed, landed, sib)
            cp.start()
            sends.append(cp)
        for k, f in enumerate(CHIP_FLIPS):
            from_sib = out_ref.at[slot(_flip(me, f)), their_rows, :]
            copy(3 + k, from_sib, from_sib, sib).wait_recv()
        for cp in sends:
            cp.wait_send()
        mine.wait()

    return pl.pallas_call(
        body, name=name, in_specs=[ANY], out_specs=ANY, out_shape=jax.ShapeDtypeStruct((4, *x.shape), x.dtype),
        scratch_shapes=[pltpu.SemaphoreType.DMA((6,)), pltpu.SemaphoreType.DMA((6,)), pltpu.SemaphoreType.DMA])(x)


def _pair_swap_halves(x, *, name):
    n, rows, cols = x.shape
    hr = rows // 2

    def body(x_ref, out_ref, send_sem, recv_sem):
        me = _me()
        sib = _flip(me, (0, 0, 1))
        theirs = pl.multiple_of((1 - me[2]) * hr, 8)
        cp = pltpu.make_async_remote_copy(src_ref=x_ref.at[:, pl.ds(theirs, hr), :], dst_ref=out_ref, send_sem=send_sem,
                                          recv_sem=recv_sem, device_id=sib, device_id_type=MESH)
        cp.start()
        cp.wait()

    return pl.pallas_call(
        body, name=name, in_specs=[ANY], out_specs=ANY, out_shape=jax.ShapeDtypeStruct((n, hr, cols), x.dtype),
        scratch_shapes=[pltpu.SemaphoreType.DMA, pltpu.SemaphoreType.DMA])(x)


def _scatter_chips(p, *, name):
    n, rows, cols = p.shape

    def body(p_ref, out_ref, send_sems, recv_sems, local_sem):
        me = _me()
        slot = lambda q: 2 * q[0] + q[1]
        mine = pltpu.make_async_copy(p_ref.at[slot(me)], out_ref.at[slot(me)], local_sem)
        mine.start()
        sends = []
        for k, f in enumerate(CHIP_FLIPS):
            peer = _flip(me, f)
            cp = pltpu.make_async_remote_copy(src_ref=p_ref.at[slot(peer)], dst_ref=out_ref.at[slot(me)],
                                              send_sem=send_sems.at[k], recv_sem=recv_sems.at[k], device_id=peer,
                                              device_id_type=MESH)
            cp.start()
            sends.append(cp)
        for k, f in enumerate(CHIP_FLIPS):
            peer = _flip(me, f)
            pltpu.make_async_remote_copy(src_ref=p_ref.at[slot(me)], dst_ref=out_ref.at[slot(peer)],
                                         send_sem=send_sems.at[k], recv_sem=recv_sems.at[k], device_id=peer,
                                         device_id_type=MESH).wait_recv()
        for cp in sends:
            cp.wait_send()
        mine.wait()

    return pl.pallas_call(
        body, name=name, in_specs=[ANY], out_specs=ANY, out_shape=jax.ShapeDtypeStruct(p.shape, p.dtype),
        scratch_shapes=[pltpu.SemaphoreType.DMA((3,)), pltpu.SemaphoreType.DMA((3,)), pltpu.SemaphoreType.DMA])(p)


def _pair_join_halves(h, *, name):
    hr, cols = h.shape

    def body(h_ref, out_ref, send_sem, recv_sem, local_sem):
        me = _me()
        sib = _flip(me, (0, 0, 1))
        my_rows = out_ref.at[pl.ds(pl.multiple_of(me[2] * hr, 8), hr), :]
        their_rows = out_ref.at[pl.ds(pl.multiple_of((1 - me[2]) * hr, 8), hr), :]
        mine = pltpu.make_async_copy(h_ref, my_rows, local_sem)
        mine.start()
        cp = pltpu.make_async_remote_copy(src_ref=h_ref, dst_ref=my_rows, send_sem=send_sem, recv_sem=recv_sem,
                                          device_id=sib, device_id_type=MESH)
        cp.start()
        pltpu.make_async_remote_copy(src_ref=h_ref, dst_ref=their_rows, send_sem=send_sem, recv_sem=recv_sem,
                                     device_id=sib, device_id_type=MESH).wait_recv()
        cp.wait_send()
        mine.wait()

    return pl.pallas_call(
        body, name=name, in_specs=[ANY], out_specs=ANY, out_shape=jax.ShapeDtypeStruct((2 * hr, cols), h.dtype),
        scratch_shapes=[pltpu.SemaphoreType.DMA, pltpu.SemaphoreType.DMA, pltpu.SemaphoreType.DMA])(h)


BIG = (("w_in", (D, IN_WIDTH // 4), 1), ("gla_w_o", (D // 4, D), 0), ("mla_w_uq", (MQR, MH * MQK // 4), 1),
       ("mla_w_ukv", (MKVR, MH * (MNOPE + MVD) // 4), 1), ("mla_w_o", (D // 4, D), 0), ("w_out", (D // 4, D), 0),
       ("mlp_w1", (D, DFF // 4), 1), ("mlp_w2", (DFF // 4, D), 0))
ADA_SHARD = (D, 6 * D // 4)
SMALL = (("b_ada", 6 * D), ("norm1_g", D), ("b_merge", 2 * D), ("gla_b_alpha", GH * GDK), ("gla_out_norm_g", GDV),
         ("mla_q_lat_g", MQR), ("mla_kv_lat_g", MKVR), ("mla_qn_g", MQK), ("mla_kn_g", MQK), ("norm2_g", D))


PACK_ROWS = 2048


def _rows_of(n):
    return -(-n // (8 * LANE)) * 8


def _flat(a):
    v = a.reshape(-1)
    rows = _rows_of(v.shape[0])
    if rows * LANE != v.shape[0]:
        v = jnp.pad(v, (0, rows * LANE - v.shape[0]))
    return v.reshape(rows, LANE)


def _pack(arrs, multiple=1):
    parts = [_flat(a) for a in arrs]
    rows = sum(p.shape[0] for p in parts)
    if rows % multiple:
        parts.append(jnp.zeros((multiple - rows % multiple, LANE), parts[0].dtype))
    return jnp.concatenate(parts, axis=0)


def _unpack(flat, shapes):
    out, r = [], 0
    for shp in shapes:
        n = int(np.prod(shp))
        rows = _rows_of(n)
        out.append(flat[r:r + rows].reshape(-1)[:n].reshape(shp))
        r += rows
    return out


def _full_weights(slots):
    per_chip = [_unpack(slots[j], [shp for _, shp, _ in BIG]) for j in range(4)]
    w = {name: jnp.concatenate([per_chip[j][i] for j in range(4)], axis=axis) for i, (name, _, axis) in enumerate(BIG)}
    wi = w["w_in"]
    zeros = lambda n: jnp.zeros((D, n), wi.dtype)
    w["w_in"] = jnp.concatenate(
        [wi[:, :3072], wi[:, 3504:5552], wi[:, 3088:3344], wi[:, 3344:3472], wi[:, 3072:3088], zeros(LANE - GLR),
         zeros(MNOPE), wi[:, 3472:3504], zeros(LANE - MQK)], axis=1)
    w["mla_w_uq"] = jnp.pad(w["mla_w_uq"].reshape(MQR, MH, MQK), ((0, 0), (0, 0), (0, LANE - MQK))).reshape(MQR, MH * LANE)
    w["mla_w_o"] = jnp.pad(w["mla_w_o"].reshape(MH, MVD, D), ((0, 0), (0, LANE - MVD), (0, 0))).reshape(MH * LANE, D)
    return w


def _grad_slots(g):
    gi = g["w_in"]
    g = dict(g)
    g["w_in"] = jnp.concatenate(
        [gi[:, :3072], gi[:, OFF_A:OFF_A + GLR], gi[:, OFF_CQ:OFF_CQ + MQR], gi[:, OFF_CKV:OFF_CKV + MKVR],
         gi[:, OFF_KPE + MNOPE:OFF_KPE + MQK], gi[:, OFF_MA:OFF_MA + 2 * D]], axis=1)
    g["mla_w_uq"] = g["mla_w_uq"].reshape(MQR, MH, LANE)[:, :, :MQK].reshape(MQR, MH * MQK)
    g["mla_w_o"] = g["mla_w_o"].reshape(MH, LANE, D)[:, :MVD].reshape(MH * MVD, D)
    slots = []
    for j in range(4):
        parts = []
        for name, shp, axis in BIG:
            n = shp[axis]
            parts.append(lax.slice_in_dim(g[name], j * n, (j + 1) * n, axis=axis))
        slots.append(_pack(parts, multiple=PACK_ROWS))
    return jnp.stack(slots)


def _rope_tables(positions):
    freqs = ROPE_THETA ** (-jnp.arange(0, MROPE, 2, dtype=F32) / MROPE)
    ang = positions.astype(F32)[..., None] * freqs
    cos, sin = jnp.cos(ang), jnp.sin(ang)
    shape = ang.shape[:-1]
    cos_t = jnp.concatenate([jnp.ones(shape + (MNOPE,), F32), cos, cos, jnp.ones(shape + (LANE - MQK,), F32)], axis=-1)
    sin_t = jnp.concatenate([jnp.zeros(shape + (MNOPE,), F32), -sin, sin, jnp.zeros(shape + (LANE - MQK,), F32)], axis=-1)
    return cos_t.reshape(-1, LANE), sin_t.reshape(-1, LANE)


def _local_step(x, positions, mod, target, w, small):
    bsz, s, _ = x.shape
    t = bsz * s
    tt = _tile(t, 1024)
    shift1, scale1, gate1, shift2, scale2, gate2 = [mod[:, None, i * D:(i + 1) * D] for i in range(6)]
    cos_t, sin_t = _rope_tables(positions)
    w_alpha_p = jnp.pad(small["gla_w_alpha"], ((0, LANE - GLR), (0, 0)))
    gq = jnp.pad(small["mla_qn_g"], ((0, 0), (0, LANE - MQK)))
    gk = jnp.pad(small["mla_kn_g"], ((0, 0), (0, LANE - MQK)))
    flat2 = lambda a: a.reshape(t, a.shape[-1])
    bsd = lambda a: a.reshape(bsz, s, a.shape[-1])

    h = _norm_mod(x, small["norm1_g"], scale1, shift1, name="norm1")
    proj = _mm(flat2(h), w["w_in"], name="proj", tn=1152)
    proj3 = bsd(proj)
    o, o_gated, states = _gla_fwd(proj3, w_alpha_p, small["gla_b_alpha"], small["gla_out_norm_g"], name="gla_fwd")
    y_a = _mm(flat2(o_gated), w["gla_w_o"], name="gla_out")
    cq_n, ckv_n = _lat_norm(proj, small["mla_q_lat_g"], small["mla_kv_lat_g"], name="lat_norm")
    q_raw = _mm(cq_n, w["mla_w_uq"], name="mla_uq")
    kv = _mm(ckv_n, w["mla_w_ukv"], name="mla_ukv")
    qf, kf, vf = _qk_prep(q_raw, kv, proj, cos_t, sin_t, gq, gk, name="qk_prep")
    o_attn = _attn_fwd(bsd(qf), bsd(kf), bsd(vf), name="attn_fwd")
    y_b = _mm(flat2(o_attn), w["mla_w_o"], name="mla_out")
    mixed_in = _merge_fwd(proj3, small["b_merge"], bsd(y_a), bsd(y_b), name="merge_fwd")
    mixed = _mm(flat2(mixed_in), w["w_out"], name="w_out")
    x1, h2 = _resid_norm_mod(x, bsd(mixed), gate1, small["norm2_g"], scale2, shift2, name="norm2")

    def sqrelu(acc, ex, outs):
        outs[0][...] = acc
        r = jnp.maximum(acc, 0.0)
        outs[1][...] = (r * r).astype(BF16)

    a1, r = _mm(flat2(h2), w["mlp_w1"], name="mlp1", epilogue=sqrelu,
                out_shape=[jax.ShapeDtypeStruct((t, DFF), F32), jax.ShapeDtypeStruct((t, DFF), BF16)],
                out_specs=[_tile_spec(tt, 1024), _tile_spec(tt, 1024)])
    ff = _mm(r, w["mlp_w2"], name="mlp2")
    dy, dff, dgate2, loss_part = _loss_head(x1, bsd(ff), gate2, target, name="loss_head")

    g = {}

    def relu2_bwd(acc, ex, outs):
        outs[0][...] = (acc * (2.0 * jnp.maximum(ex[0][...], 0.0))).astype(BF16)

    dff2 = flat2(dff)
    da1 = _mm(dff2, w["mlp_w2"], tb=True, name="mlp2_dx", epilogue=relu2_bwd, extras=(a1,),
              extra_specs=(_tile_spec(tt, 1024),), out_shape=jax.ShapeDtypeStruct((t, DFF), BF16),
              out_specs=_tile_spec(tt, 1024))
    g["mlp_w2"] = _mm(r, dff2, ta=True, name="mlp2_dw")
    dh2 = _mm(da1, w["mlp_w1"], tb=True, name="mlp1_dx")
    g["mlp_w1"] = _mm(flat2(h2), da1, ta=True, name="mlp1_dw")
    dx1, dscale2, dshift2, dg2, dgate1, dmixed = _norm_mod_bwd(
        bsd(dh2), x1, dy, small["norm2_g"], scale2, gate1, bsd(mixed), name="norm2_bwd")
    dmixed2 = flat2(dmixed)
    dmi = _mm(dmixed2, w["w_out"], tb=True, name="w_out_dx")
    g["w_out"] = _mm(flat2(mixed_in), dmixed2, ta=True, name="w_out_dw")
    dy_a, dy_b, dl_a, dl_b, db_a, db_b = _merge_bwd(bsd(dmi), proj3, small["b_merge"], bsd(y_a), bsd(y_b), name="merge_bwd")
    dy_a2, dy_b2 = flat2(dy_a), flat2(dy_b)
    dog = _mm(dy_a2, w["gla_w_o"], tb=True, name="gla_out_dx")
    g["gla_w_o"] = _mm(flat2(o_gated), dy_a2, ta=True, name="gla_out_dw")
    dq_g, dk_g, dv_g, dg_g, dlog, db_alpha, d_ong = _gla_bwd(
        bsd(dog), o, states, proj3, w_alpha_p, small["gla_b_alpha"], small["gla_out_norm_g"], name="gla_bwd")
    dlog2 = flat2(dlog)
    da_p = _mm(dlog2, w_alpha_p, tb=True, out_dtype=BF16, name="alpha_dx")
    d_w_alpha = _mm(proj[:, OFF_A:OFF_A + LANE], dlog2, ta=True, name="alpha_dw")[:GLR]
    do_attn = _mm(dy_b2, w["mla_w_o"], tb=True, out_dtype=BF16, name="mla_out_dx")
    g["mla_w_o"] = _mm(flat2(o_attn), dy_b2, ta=True, name="mla_out_dw")
    dqf, dkf, dvf = _attn_bwd(bsd(qf), bsd(kf), bsd(vf), bsd(do_attn), name="attn_bwd")
    dq_raw, dkv, dkpe, dgq, dgk = _qk_prep_bwd(flat2(dqf), flat2(dkf), flat2(dvf), q_raw, kv, proj, cos_t, sin_t, gq, gk,
                                                name="qk_prep_bwd")
    dcq_n = _mm(dq_raw, w["mla_w_uq"], tb=True, name="mla_uq_dx")
    g["mla_w_uq"] = _mm(cq_n, dq_raw, ta=True, name="mla_uq_dw")
    dckv_n = _mm(dkv, w["mla_w_ukv"], tb=True, name="mla_ukv_dx")
    g["mla_w_ukv"] = _mm(ckv_n, dkv, ta=True, name="mla_ukv_dw")
    dcq, dckv, dg_qlat, dg_kvlat = _lat_norm_bwd(dcq_n, dckv_n, proj, small["mla_q_lat_g"], small["mla_kv_lat_g"],
                                                  name="lat_norm_bwd")
    dproj = jnp.concatenate([flat2(dq_g), flat2(dk_g), flat2(dv_g), flat2(dg_g), flat2(dl_a), flat2(dl_b), dcq, dckv,
                             da_p, dkpe.astype(BF16)], axis=1)
    dh = _mm(dproj, w["w_in"], tb=True, name="proj_dx", tk=1152)
    g["w_in"] = _mm(flat2(h), dproj, ta=True, name="proj_dw", tn=1152)
    grad_x, dscale1, dshift1, dg1 = _norm_mod_bwd(bsd(dh), x, dx1, small["norm1_g"], scale1, name="norm1_bwd")

    dmod = jnp.concatenate([dshift1, dscale1, dgate1, dshift2, dscale2, dgate2], axis=-1).reshape(bsz, 6 * D)
    gs = {"norm1_g": dg1, "b_merge": jnp.concatenate([db_a, db_b], axis=1), "gla_b_alpha": db_alpha,
          "gla_out_norm_g": d_ong, "mla_q_lat_g": dg_qlat, "mla_kv_lat_g": dg_kvlat, "mla_qn_g": dgq[:, :MQK],
          "mla_kn_g": dgk[:, :MQK], "norm2_g": dg2}
    return loss_part[0, 0], grad_x, dmod, g, gs, d_w_alpha


def kernel(x, c, positions, w_ada, b_ada, norm1_g, w_in, b_merge, gla_w_alpha, gla_b_alpha, gla_out_norm_g, gla_w_o, mla_q_lat_g, mla_w_uq, mla_kv_lat_g, mla_w_ukv, mla_qn_g, mla_kn_g, mla_w_o, w_out, norm2_g, mlp_w1, mlp_w2, loss_target, m_w_ada, m_b_ada, m_norm1_g, m_w_in, m_b_merge, m_gla_w_alpha, m_gla_b_alpha, m_gla_out_norm_g, m_gla_w_o, m_mla_q_lat_g, m_mla_w_uq, m_mla_kv_lat_g, m_mla_w_ukv, m_mla_qn_g, m_mla_kn_g, m_mla_w_o, m_w_out, m_norm2_g, m_mlp_w1, m_mlp_w2, v_w_ada, v_b_ada, v_norm1_g, v_w_in, v_b_merge, v_gla_w_alpha, v_gla_b_alpha, v_gla_out_norm_g, v_gla_w_o, v_mla_q_lat_g, v_mla_w_uq, v_mla_kv_lat_g, v_mla_w_ukv, v_mla_qn_g, v_mla_kn_g, v_mla_w_o, v_w_out, v_norm2_g, v_mlp_w1, v_mlp_w2):
    args = dict(locals())
    names_big = [n for n, _, _ in BIG]
    names_small = [n for n, _ in SMALL]
    bsz = x.shape[0]
    ax, ay, ac = lax.axis_index("x"), lax.axis_index("y"), lax.axis_index("c")
    chip = 2 * ax + ay
    dev = 2 * chip + ac

    wp = _pack([args[n][0] for n in names_big], multiple=PACK_ROWS)
    slots = _all_gather_chips(wp.astype(BF16), name="comm_weights")
    w = _full_weights(slots)
    small = {n: args[n] for n in names_small}
    w_alpha_all = _all_gather8(gla_w_alpha[0], name="comm_w_alpha")
    small["gla_w_alpha"] = jnp.concatenate([w_alpha_all[2 * j] for j in range(4)], axis=1)

    c_all = _all_gather8(c, name="comm_c").reshape(8 * bsz, D)

    def add_bias(acc, ex, outs):
        outs[0][...] = acc + ex[0][...]

    silu = lambda v: v * _sigmoid(v)
    b_ada_mine = lax.dynamic_slice(b_ada, (0, chip * ADA_SHARD[1]), (1, ADA_SHARD[1]))
    mod_part = _mm(c_all, w_ada[0], name="ada", tn=512, a_fn=silu, epilogue=add_bias, extras=(b_ada_mine,),
                   extra_specs=(pl.BlockSpec((1, 512), lambda i, j, k: (0, j)),),
                   out_shape=jax.ShapeDtypeStruct((8 * bsz, ADA_SHARD[1]), F32), out_specs=_tile_spec(8 * bsz, 512))
    mod_all = _all_gather8(mod_part, name="comm_mod")
    mod_rows = lax.dynamic_slice(mod_all, (0, dev * bsz, 0), (8, bsz, ADA_SHARD[1]))
    mod = jnp.concatenate([mod_rows[2 * j] for j in range(4)], axis=1)

    loss_part, grad_x, dmod, g, gs, d_w_alpha = _local_step(x, positions, mod, loss_target, w, small)
    loss = lax.psum(loss_part * (0.5 / D), ("x", "y", "c"))

    dmod_all = _all_gather8(dmod, name="comm_dmod").reshape(8 * bsz, 6 * D)
    dmod_mine = lax.dynamic_slice(dmod_all, (0, chip * ADA_SHARD[1]), (8 * bsz, ADA_SHARD[1]))
    g_w_ada = _mm(c_all, dmod_mine, ta=True, a_fn=silu, name="ada_dw")
    g_b_ada = _sum_slots(dmod_all.reshape(8 * bsz, 6 * D // LANE, LANE), name="sum_b_ada", tr=6 * D // LANE)

    names_red = [n for n in names_small if n != "b_ada"]
    red_shapes = [(1, n) for name, n in SMALL if name != "b_ada"] + [(GLR, GH * GDK)]
    gs_packed = _pack([gs[n] for n in names_red] + [d_w_alpha], multiple=8)
    gs_sum = _sum_slots(_all_gather8(gs_packed, name="comm_small"), name="sum_small")
    gs_full = _unpack(gs_sum, red_shapes)
    g_small = dict(zip(names_red, gs_full[:-1]))
    g_small["b_ada"] = g_b_ada.reshape(1, 6 * D)
    g_w_alpha = lax.dynamic_slice(gs_full[-1], (0, chip * GDK), (GLR, GDK))

    gslots = _grad_slots(g)
    rows = gslots.shape[1]
    hr = rows // 2
    sib_half = _pair_swap_halves(gslots, name="comm_pair_sum")
    my_half = lax.dynamic_slice(gslots, (0, ac * hr, 0), (4, hr, LANE))
    pair = _add2(my_half, sib_half, name="pair_add", out_dtype=BF16)
    from_chips = _scatter_chips(pair, name="comm_scatter")
    g_half = _sum_slots(from_chips, name="chip_sum")
    g_big = _pair_join_halves(g_half, name="comm_pair_join")

    res = {}
    g_mats = dict(zip(names_big, _unpack(g_big, [shp for _, shp, _ in BIG])))
    g_mats["w_ada"] = g_w_ada
    for n, gm in g_mats.items():
        res[n] = (gm, *_adamw(args[n][0], gm, args["m_" + n][0], args["v_" + n][0], name="adamw_" + n))
    sm_names = names_small + ["gla_w_alpha"]
    sm_shapes = [(1, n) for _, n in SMALL] + [(GLR, GDK)]
    g_sm = [g_small[n] for n in names_small] + [g_w_alpha]
    pack_sm = lambda prefix: _pack([args[prefix + n].reshape(shp) for n, shp in zip(sm_names, sm_shapes)], multiple=8)
    outs_sm = _adamw(pack_sm(""), _pack(g_sm, multiple=8), pack_sm("m_"), pack_sm("v_"), name="adamw_small")
    for n, gm, *rest in zip(sm_names, g_sm, *[_unpack(o, sm_shapes) for o in outs_sm]):
        res[n] = (gm, *rest)

    order = ["w_ada", "b_ada", "norm1_g", "w_in", "b_merge", "gla_w_alpha", "gla_b_alpha", "gla_out_norm_g", "gla_w_o",
             "mla_q_lat_g", "mla_w_uq", "mla_kv_lat_g", "mla_w_ukv", "mla_qn_g", "mla_kn_g", "mla_w_o", "w_out",
             "norm2_g", "mlp_w1", "mlp_w2"]
    named = lambda k: [res[n][k].reshape(args[n].shape) for n in order]
    return (loss, grad_x, *named(0), *named(1), *named(2), *named(3))
```

```python
import functools

import jax
import jax.numpy as jnp
import numpy as np
from jax import lax
from jax.experimental import pallas as pl
from jax.experimental.pallas import tpu as pltpu

F32 = jnp.float32
BF16 = jnp.bfloat16
MESH = pl.DeviceIdType.MESH

D = 1024
CHUNK = 64
EPS = 1e-6
GH, GDK, GDV, GLR, GTAU = 4, 128, 256, 16, 16.0
MH, MQR, MKVR, MNOPE, MROPE, MVD = 16, 256, 128, 64, 32, 64
MQK = MNOPE + MROPE
DFF = 4 * D
ROPE_THETA = 10000.0
IN_WIDTH = 5552
LANE = 128
OFF_Q, OFF_K, OFF_V, OFF_G, OFF_MA, OFF_MB, OFF_CQ, OFF_CKV, OFF_A, OFF_KPE, PW = (
    0, 512, 1024, 2048, 3072, 4096, 5120, 5376, 5504, 5632, 5760)
ADAM_LR, ADAM_B1, ADAM_B2, ADAM_EPS, ADAM_WD, ADAM_STEP = 0.001, 0.9, 0.999, 1e-08, 0.01, 10
VMEM_LIMIT = 48 * 1024 * 1024


def _params(n_axes):
    return pltpu.CompilerParams(dimension_semantics=("arbitrary",) * n_axes, vmem_limit_bytes=VMEM_LIMIT)


def _tile(n, target):
    if n <= target:
        return n
    best = None
    for t in range(LANE, target + 1, LANE):
        if n % t == 0:
            best = t
    assert best is not None, (n, target)
    return best


def _sigmoid(x):
    return 1.0 / (1.0 + jnp.exp(-x))


def _mm(a, b, *, name, ta=False, tb=False, out_dtype=F32, tm=1024, tn=1024, tk=1024,
        epilogue=None, extras=(), extra_specs=(), out_shape=None, out_specs=None, a_fn=None):
    if ta:
        kdim, m = a.shape
    else:
        m, kdim = a.shape
    if tb:
        n, k2 = b.shape
    else:
        k2, n = b.shape
    assert kdim == k2, (a.shape, b.shape)
    tm, tn, tk = _tile(m, tm), _tile(n, tn), _tile(kdim, tk)
    nk = kdim // tk
    a_spec = pl.BlockSpec((tk, tm), lambda i, j, k: (k, i)) if ta else pl.BlockSpec((tm, tk), lambda i, j, k: (i, k))
    b_spec = pl.BlockSpec((tn, tk), lambda i, j, k: (j, k)) if tb else pl.BlockSpec((tk, tn), lambda i, j, k: (k, j))
    dims = (((0 if ta else 1,), (1 if tb else 0,)), ((), ()))
    ne = len(extras)
    if out_shape is None:
        out_shape = jax.ShapeDtypeStruct((m, n), out_dtype)
        out_specs = pl.BlockSpec((tm, tn), lambda i, j, k: (i, j))

    def body(a_ref, b_ref, *rest):
        ex, outs, acc = rest[:ne], rest[ne:-1], rest[-1]
        k = pl.program_id(2)

        @pl.when(k == 0)
        def _():
            acc[...] = jnp.zeros_like(acc)

        av = a_ref[...] if a_fn is None else a_fn(a_ref[...])
        acc[...] += lax.dot_general(av.astype(BF16), b_ref[...].astype(BF16), dims, preferred_element_type=F32)

        @pl.when(k == nk - 1)
        def _():
            if epilogue is None:
                outs[0][...] = acc[...].astype(outs[0].dtype)
            else:
                epilogue(acc[...], ex, outs)

    return pl.pallas_call(
        body, name=name, grid=(m // tm, n // tn, nk),
        in_specs=[a_spec, b_spec, *extra_specs], out_specs=out_specs, out_shape=out_shape,
        scratch_shapes=[pltpu.VMEM((tm, tn), F32)], compiler_params=_params(3),
    )(a, b, *extras)


def _tile_spec(tm, tn):
    return pl.BlockSpec((tm, tn), lambda i, j, k: (i, j))


def _rms(x, g):
    r = lax.rsqrt(jnp.mean(x * x, axis=-1, keepdims=True) + EPS)
    return x * r, r


def _row_spec(ts, width, col=0):
    return pl.BlockSpec((None, ts, width), lambda b, i: (b, i, col))


def _vec_spec(width):
    return pl.BlockSpec((None, 1, width), lambda b, i: (b, 0, 0))


def _gain_spec(width):
    return pl.BlockSpec((1, width), lambda b, i: (0, 0))


def _norm_mod(x, g, scale, shift, *, name, ts=256):
    bsz, s, d = x.shape
    ts = min(ts, s)

    def body(x_ref, g_ref, sc_ref, sh_ref, h_ref):
        xh, _ = _rms(x_ref[...], None)
        h_ref[...] = ((xh * g_ref[...]) * (1.0 + sc_ref[...]) + sh_ref[...]).astype(BF16)

    return pl.pallas_call(
        body, name=name, grid=(bsz, s // ts),
        in_specs=[_row_spec(ts, d), _gain_spec(d), _vec_spec(d), _vec_spec(d)],
        out_specs=_row_spec(ts, d), out_shape=jax.ShapeDtypeStruct((bsz, s, d), BF16),
        compiler_params=_params(2),
    )(x, g, scale, shift)


def _resid_norm_mod(x, mixed, gate, g, scale, shift, *, name, ts=256):
    bsz, s, d = x.shape
    ts = min(ts, s)

    def body(x_ref, mx_ref, gt_ref, g_ref, sc_ref, sh_ref, x1_ref, h_ref):
        x1 = x_ref[...] + gt_ref[...] * mx_ref[...]
        x1_ref[...] = x1
        xh, _ = _rms(x1, None)
        h_ref[...] = ((xh * g_ref[...]) * (1.0 + sc_ref[...]) + sh_ref[...]).astype(BF16)

    return pl.pallas_call(
        body, name=name, grid=(bsz, s // ts),
        in_specs=[_row_spec(ts, d), _row_spec(ts, d), _vec_spec(d), _gain_spec(d), _vec_spec(d), _vec_spec(d)],
        out_specs=[_row_spec(ts, d), _row_spec(ts, d)],
        out_shape=[jax.ShapeDtypeStruct((bsz, s, d), F32), jax.ShapeDtypeStruct((bsz, s, d), BF16)],
        compiler_params=_params(2),
    )(x, mixed, gate, g, scale, shift)


def _norm_mod_bwd(dh, xin, resid, g, scale, gate=None, mixed=None, *, name, ts=256):
    bsz, s, d = xin.shape
    ts = min(ts, s)
    gated = gate is not None

    def body(*refs):
        if gated:
            dh_ref, x_ref, rs_ref, g_ref, sc_ref, gt_ref, mx_ref, dx_ref, dsc_ref, dsh_ref, dg_ref, dgt_ref, dmx_ref = refs
        else:
            dh_ref, x_ref, rs_ref, g_ref, sc_ref, dx_ref, dsc_ref, dsh_ref, dg_ref = refs
        b, i = pl.program_id(0), pl.program_id(1)

        @pl.when(i == 0)
        def _():
            dsc_ref[...] = jnp.zeros_like(dsc_ref)
            dsh_ref[...] = jnp.zeros_like(dsh_ref)
            if gated:
                dgt_ref[...] = jnp.zeros_like(dgt_ref)

        @pl.when((i == 0) & (b == 0))
        def _():
            dg_ref[...] = jnp.zeros_like(dg_ref)

        dh_v, gv = dh_ref[...], g_ref[...]
        xh, r = _rms(x_ref[...], None)
        dsc_ref[...] += jnp.sum(dh_v * (xh * gv), axis=0, keepdims=True)
        dsh_ref[...] += jnp.sum(dh_v, axis=0, keepdims=True)
        dn = dh_v * (1.0 + sc_ref[...])
        dg_ref[...] += jnp.sum(dn * xh, axis=0, keepdims=True)
        dxh = dn * gv
        dx = rs_ref[...] + r * (dxh - xh * jnp.mean(dxh * xh, axis=-1, keepdims=True))
        dx_ref[...] = dx
        if gated:
            dgt_ref[...] += jnp.sum(dx * mx_ref[...], axis=0, keepdims=True)
            dmx_ref[...] = (dx * gt_ref[...]).astype(BF16)

    ins = [dh, xin, resid, g, scale]
    in_specs = [_row_spec(ts, d), _row_spec(ts, d), _row_spec(ts, d), _gain_spec(d), _vec_spec(d)]
    out_specs = [_row_spec(ts, d), _vec_spec(d), _vec_spec(d), _gain_spec(d)]
    out_shape = [jax.ShapeDtypeStruct((bsz, s, d), F32), jax.ShapeDtypeStruct((bsz, 1, d), F32),
                 jax.ShapeDtypeStruct((bsz, 1, d), F32), jax.ShapeDtypeStruct((1, d), F32)]
    if gated:
        ins += [gate, mixed]
        in_specs += [_vec_spec(d), _row_spec(ts, d)]
        out_specs += [_vec_spec(d), _row_spec(ts, d)]
        out_shape += [jax.ShapeDtypeStruct((bsz, 1, d), F32), jax.ShapeDtypeStruct((bsz, s, d), BF16)]
    return pl.pallas_call(
        body, name=name, grid=(bsz, s // ts), in_specs=in_specs, out_specs=out_specs, out_shape=out_shape,
        compiler_params=_params(2),
    )(*ins)


def _loss_head(x1, ff, gate2, target, *, name, ts=256):
    bsz, s, d = x1.shape
    ts = min(ts, s)

    def body(x1_ref, ff_ref, gt_ref, t_ref, dy_ref, dff_ref, dgt_ref, loss_ref, acc):
        b, i = pl.program_id(0), pl.program_id(1)

        @pl.when(i == 0)
        def _():
            dgt_ref[...] = jnp.zeros_like(dgt_ref)

        @pl.when((i == 0) & (b == 0))
        def _():
            acc[...] = jnp.zeros_like(acc)

        ffv, gt = ff_ref[...], gt_ref[...]
        diff = (x1_ref[...] + gt * ffv) - t_ref[...]
        acc[...] += jnp.sum((diff * diff).reshape(ts // 8, 8, d), axis=0)
        dy = diff * (1.0 / d)
        dy_ref[...] = dy
        dgt_ref[...] += jnp.sum(dy * ffv, axis=0, keepdims=True)
        dff_ref[...] = (dy * gt).astype(BF16)

        @pl.when((i == pl.num_programs(1) - 1) & (b == pl.num_programs(0) - 1))
        def _():
            loss_ref[...] = jnp.full(loss_ref.shape, jnp.sum(acc[...]), F32)

    return pl.pallas_call(
        body, name=name, grid=(bsz, s // ts),
        in_specs=[_row_spec(ts, d), _row_spec(ts, d), _vec_spec(d), _row_spec(ts, d)],
        out_specs=[_row_spec(ts, d), _row_spec(ts, d), _vec_spec(d), pl.BlockSpec((8, LANE), lambda b, i: (0, 0))],
        out_shape=[jax.ShapeDtypeStruct((bsz, s, d), F32), jax.ShapeDtypeStruct((bsz, s, d), BF16),
                   jax.ShapeDtypeStruct((bsz, 1, d), F32), jax.ShapeDtypeStruct((8, LANE), F32)],
        scratch_shapes=[pltpu.VMEM((8, d), F32)], compiler_params=_params(2),
    )(x1, ff, gate2, target)


def _merge_fwd(proj, b_merge, y_a, y_b, *, name, ts=256):
    bsz, s, _ = proj.shape
    ts = min(ts, s)

    def body(la_ref, lb_ref, ba_ref, bb_ref, ya_ref, yb_ref, out_ref):
        ga = _sigmoid(la_ref[...] + ba_ref[...])
        gb = _sigmoid(lb_ref[...] + bb_ref[...])
        out_ref[...] = (ga * ya_ref[...] + gb * yb_ref[...]).astype(BF16)

    return pl.pallas_call(
        body, name=name, grid=(bsz, s // ts),
        in_specs=[_row_spec(ts, D, OFF_MA // D), _row_spec(ts, D, OFF_MB // D),
                  pl.BlockSpec((1, D), lambda b, i: (0, 0)), pl.BlockSpec((1, D), lambda b, i: (0, 1)),
                  _row_spec(ts, D), _row_spec(ts, D)],
        out_specs=_row_spec(ts, D), out_shape=jax.ShapeDtypeStruct((bsz, s, D), BF16),
        compiler_params=_params(2),
    )(proj, proj, b_merge, b_merge, y_a, y_b)


def _merge_bwd(dmi, proj, b_merge, y_a, y_b, *, name, ts=256):
    bsz, s, _ = proj.shape
    ts = min(ts, s)

    def body(d_ref, la_ref, lb_ref, ba_ref, bb_ref, ya_ref, yb_ref, dya_ref, dyb_ref, dla_ref, dlb_ref, dba_ref, dbb_ref):
        @pl.when((pl.program_id(0) == 0) & (pl.program_id(1) == 0))
        def _():
            dba_ref[...] = jnp.zeros_like(dba_ref)
            dbb_ref[...] = jnp.zeros_like(dbb_ref)

        dv = d_ref[...]
        ga = _sigmoid(la_ref[...] + ba_ref[...])
        gb = _sigmoid(lb_ref[...] + bb_ref[...])
        dya_ref[...] = (dv * ga).astype(BF16)
        dyb_ref[...] = (dv * gb).astype(BF16)
        dla = (dv * ya_ref[...]) * (ga * (1.0 - ga))
        dlb = (dv * yb_ref[...]) * (gb * (1.0 - gb))
        dla_ref[...] = dla.astype(BF16)
        dlb_ref[...] = dlb.astype(BF16)
        dba_ref[...] += jnp.sum(dla, axis=0, keepdims=True)
        dbb_ref[...] += jnp.sum(dlb, axis=0, keepdims=True)

    act = jax.ShapeDtypeStruct((bsz, s, D), BF16)
    return pl.pallas_call(
        body, name=name, grid=(bsz, s // ts),
        in_specs=[_row_spec(ts, D), _row_spec(ts, D, OFF_MA // D), _row_spec(ts, D, OFF_MB // D),
                  pl.BlockSpec((1, D), lambda b, i: (0, 0)), pl.BlockSpec((1, D), lambda b, i: (0, 1)),
                  _row_spec(ts, D), _row_spec(ts, D)],
        out_specs=[_row_spec(ts, D)] * 4 + [_gain_spec(D)] * 2,
        out_shape=[act, act, act, act, jax.ShapeDtypeStruct((1, D), F32), jax.ShapeDtypeStruct((1, D), F32)],
        compiler_params=_params(2),
    )(dmi, proj, proj, b_merge, b_merge, y_a, y_b)


def _tri(lower):
    r = lax.broadcasted_iota(jnp.int32, (CHUNK, CHUNK), 0)
    c = lax.broadcasted_iota(jnp.int32, (CHUNK, CHUNK), 1)
    return jnp.where((c <= r) if lower else (c >= r), 1.0, 0.0).astype(F32)


def _gla_logits(a_ref, wal_ref, bal_ref):
    logits = jnp.dot(a_ref[...].astype(BF16), wal_ref[...].astype(BF16), preferred_element_type=F32) + bal_ref[...]
    la = (jnp.minimum(logits, 0.0) - jnp.log(1.0 + jnp.exp(-jnp.abs(logits)))) * (1.0 / GTAU)
    return logits, la


def _chunk_cumsum(la_n, tri):
    cum = jnp.dot(tri, la_n, preferred_element_type=F32, precision=lax.Precision.HIGHEST)
    return cum, jnp.sum(la_n, axis=0, keepdims=True)


def _gla_specs(s, nc):
    def blk(width, off):
        return pl.BlockSpec((None, s, width), lambda h, b: (b, 0, off // width + h))

    proj_specs = [blk(GDK, OFF_Q), blk(GDK, OFF_K), blk(GDV, OFF_V), blk(GDV, OFF_G),
                  pl.BlockSpec((None, s, LANE), lambda h, b: (b, 0, OFF_A // LANE)),
                  pl.BlockSpec((LANE, GDK), lambda h, b: (0, h)), pl.BlockSpec((1, GDK), lambda h, b: (0, h)),
                  pl.BlockSpec((1, GDV), lambda h, b: (0, 0))]
    st_spec = pl.BlockSpec((None, None, nc, GDV, GDK), lambda h, b: (b, h, 0, 0, 0))
    return blk, proj_specs, st_spec


def _gla_fwd(proj, w_alpha_p, b_alpha, out_norm_g, *, name):
    bsz, s, _ = proj.shape
    nc = s // CHUNK
    scale = GDK ** -0.5

    rb = min(512, s)

    def body(q_ref, k_ref, v_ref, g_ref, a_ref, wal_ref, bal_ref, ong_ref, o_ref, og_ref, st_ref):
        _, la = _gla_logits(a_ref, wal_ref, bal_ref)
        tri = _tri(True)
        st = jnp.zeros((GDV, GDK), F32)
        for n in range(nc):
            rows = pl.ds(n * CHUNK, CHUNK)
            cum, cum_end = _chunk_cumsum(la[n * CHUNK:(n + 1) * CHUNK], tri)
            kd = k_ref[rows, :] * jnp.exp(cum_end - cum)
            ut = lax.dot_general(v_ref[rows, :].astype(BF16), kd.astype(BF16), _TN, preferred_element_type=F32)
            st = st * jnp.exp(cum_end) + ut
            st_ref[n] = st
            o_ref[rows, :] = lax.dot_general((q_ref[rows, :] * scale).astype(BF16), st.astype(BF16), _NT,
                                             preferred_element_type=F32)
        for j in range(0, s, rb):
            blk_rows = pl.ds(j, rb)
            oh, _ = _rms(o_ref[blk_rows, :], None)
            gv = g_ref[blk_rows, :]
            og_ref[blk_rows, :] = ((oh * ong_ref[...]) * (gv * _sigmoid(gv))).astype(BF16)

    blk, proj_specs, st_spec = _gla_specs(s, nc)
    return pl.pallas_call(
        body, name=name, grid=(GH, bsz), in_specs=proj_specs, out_specs=[blk(GDV, 0), blk(GDV, 0), st_spec],
        out_shape=[jax.ShapeDtypeStruct((bsz, s, GH * GDV), F32), jax.ShapeDtypeStruct((bsz, s, GH * GDV), BF16),
                   jax.ShapeDtypeStruct((bsz, GH, nc, GDV, GDK), F32)],
        compiler_params=_params(2),
    )(proj, proj, proj, proj, proj, w_alpha_p, b_alpha, out_norm_g)


def _gla_bwd(dog, o, states, proj, w_alpha_p, b_alpha, out_norm_g, *, name):
    bsz, s, _ = proj.shape
    nc = s // CHUNK
    scale = GDK ** -0.5

    def body(dog_ref, o_ref, st_ref, q_ref, k_ref, v_ref, g_ref, a_ref, wal_ref, bal_ref, ong_ref,
             dq_ref, dk_ref, dv_ref, dg_ref, dl_ref, dbal_ref, dong_ref, do_scr, dlog_scr):
        h, b = pl.program_id(0), pl.program_id(1)

        @pl.when(b == 0)
        def _():
            dbal_ref[...] = jnp.zeros_like(dbal_ref)

        @pl.when((b == 0) & (h == 0))
        def _():
            dong_ref[...] = jnp.zeros_like(dong_ref)

        ong = ong_ref[...]
        for j in range(0, s, rb):
            blk_rows = pl.ds(j, rb)
            gv, dogv = g_ref[blk_rows, :], dog_ref[blk_rows, :]
            sg = _sigmoid(gv)
            oh, r = _rms(o_ref[blk_rows, :], None)
            don = dogv * (gv * sg)
            dg_ref[blk_rows, :] = (dogv * (oh * ong) * (sg * (1.0 + gv * (1.0 - sg)))).astype(BF16)
            dong_ref[...] += jnp.sum(don * oh, axis=0, keepdims=True)
            doh = don * ong
            do_scr[blk_rows, :] = (r * (doh - oh * jnp.mean(doh * oh, axis=-1, keepdims=True))).astype(BF16)

        logits, la = _gla_logits(a_ref, wal_ref, bal_ref)
        tri_lo, tri_up = _tri(True), _tri(False)
        carry = jnp.zeros((GDV, GDK), F32)
        for n in range(nc - 1, -1, -1):
            rows = pl.ds(n * CHUNK, CHUNK)
            cum, cum_end = _chunk_cumsum(la[n * CHUNK:(n + 1) * CHUNK], tri_lo)
            decay = jnp.exp(cum_end)
            w = jnp.exp(cum_end - cum)
            kd = k_ref[rows, :] * w
            do_b = do_scr[rows, :]
            qs_b = (q_ref[rows, :] * scale).astype(BF16)
            dq_ref[rows, :] = (jnp.dot(do_b, st_ref[n].astype(BF16), preferred_element_type=F32) * scale).astype(BF16)
            dsn = lax.dot_general(do_b, qs_b, _TN, preferred_element_type=F32) + carry
            carry = dsn * decay
            dsn_b = dsn.astype(BF16)
            dv_ref[rows, :] = lax.dot_general(kd.astype(BF16), dsn_b, _NT, preferred_element_type=F32).astype(BF16)
            dkd = jnp.dot(v_ref[rows, :].astype(BF16), dsn_b, preferred_element_type=F32)
            dk_ref[rows, :] = (dkd * w).astype(BF16)
            e = dkd * kd
            dcum_end = jnp.sum(e, axis=0, keepdims=True)
            if n > 0:
                dcum_end += jnp.sum(dsn * st_ref[n - 1], axis=0, keepdims=True) * decay
            dlog_scr[rows, :] = dcum_end - jnp.dot(tri_up, e, preferred_element_type=F32,
                                                  precision=lax.Precision.HIGHEST)
        dlog = dlog_scr[...] * (1.0 / GTAU) * (1.0 - _sigmoid(logits))
        dl_ref[...] = dlog.astype(BF16)
        dbal_ref[...] += jnp.sum(dlog, axis=0, keepdims=True)

    rb = min(512, s)

    blk, proj_specs, st_spec = _gla_specs(s, nc)
    act = lambda wd: jax.ShapeDtypeStruct((bsz, s, wd), BF16)
    return pl.pallas_call(
        body, name=name, grid=(GH, bsz), in_specs=[blk(GDV, 0), blk(GDV, 0), st_spec, *proj_specs],
        out_specs=[blk(GDK, 0), blk(GDK, 0), blk(GDV, 0), blk(GDV, 0), blk(GDK, 0),
                   pl.BlockSpec((1, GDK), lambda h, b: (0, h)), pl.BlockSpec((1, GDV), lambda h, b: (0, 0))],
        out_shape=[act(GH * GDK), act(GH * GDK), act(GH * GDV), act(GH * GDV), act(GH * GDK),
                   jax.ShapeDtypeStruct((1, GH * GDK), F32), jax.ShapeDtypeStruct((1, GDV), F32)],
        scratch_shapes=[pltpu.VMEM((s, GDV), BF16), pltpu.VMEM((s, GDK), F32)], compiler_params=_params(2),
    )(dog, o, states, proj, proj, proj, proj, proj, w_alpha_p, b_alpha, out_norm_g)


def _lane():
    return lax.broadcasted_iota(jnp.int32, (1, LANE), 1)


def _swap_halves(x):
    lane = _lane()
    half = MROPE // 2
    lo = (lane >= MNOPE) & (lane < MNOPE + half)
    hi = (lane >= MNOPE + half) & (lane < MQK)
    return jnp.where(lo, pltpu.roll(x, LANE - half, 1), jnp.where(hi, pltpu.roll(x, half, 1), 0.0))


def _norm96(x, g):
    r = lax.rsqrt(jnp.sum(x * x, axis=-1, keepdims=True) * (1.0 / MQK) + EPS)
    return x * r, r


def _lat_norm(proj, q_lat_g, kv_lat_g, *, name, ts=512):
    t = proj.shape[0]
    ts = min(ts, t)

    def body(cq_ref, ckv_ref, gq_ref, gk_ref, oq_ref, ok_ref):
        xq, _ = _rms(cq_ref[...], None)
        oq_ref[...] = (xq * gq_ref[...]).astype(BF16)
        xk, _ = _rms(ckv_ref[...], None)
        ok_ref[...] = (xk * gk_ref[...]).astype(BF16)

    return pl.pallas_call(
        body, name=name, grid=(t // ts,),
        in_specs=[pl.BlockSpec((ts, MQR), lambda i: (i, OFF_CQ // MQR)), pl.BlockSpec((ts, MKVR), lambda i: (i, OFF_CKV // MKVR)),
                  pl.BlockSpec((1, MQR), lambda i: (0, 0)), pl.BlockSpec((1, MKVR), lambda i: (0, 0))],
        out_specs=[pl.BlockSpec((ts, MQR), lambda i: (i, 0)), pl.BlockSpec((ts, MKVR), lambda i: (i, 0))],
        out_shape=[jax.ShapeDtypeStruct((t, MQR), BF16), jax.ShapeDtypeStruct((t, MKVR), BF16)],
        compiler_params=_params(1),
    )(proj, proj, q_lat_g, kv_lat_g)


def _lat_norm_bwd(dcqn, dckvn, proj, q_lat_g, kv_lat_g, *, name, ts=512):
    t = proj.shape[0]
    ts = min(ts, t)

    def one(d_ref, x_ref, g_ref, dx_ref, dg_ref):
        xh, r = _rms(x_ref[...], None)
        dn = d_ref[...]
        dg_ref[...] += jnp.sum(dn * xh, axis=0, keepdims=True)
        dxh = dn * g_ref[...]
        dx_ref[...] = (r * (dxh - xh * jnp.mean(dxh * xh, axis=-1, keepdims=True))).astype(BF16)

    def body(dq_ref, dk_ref, cq_ref, ckv_ref, gq_ref, gk_ref, dxq_ref, dxk_ref, dgq_ref, dgk_ref):
        @pl.when(pl.program_id(0) == 0)
        def _():
            dgq_ref[...] = jnp.zeros_like(dgq_ref)
            dgk_ref[...] = jnp.zeros_like(dgk_ref)

        one(dq_ref, cq_ref, gq_ref, dxq_ref, dgq_ref)
        one(dk_ref, ckv_ref, gk_ref, dxk_ref, dgk_ref)

    return pl.pallas_call(
        body, name=name, grid=(t // ts,),
        in_specs=[pl.BlockSpec((ts, MQR), lambda i: (i, 0)), pl.BlockSpec((ts, MKVR), lambda i: (i, 0)),
                  pl.BlockSpec((ts, MQR), lambda i: (i, OFF_CQ // MQR)), pl.BlockSpec((ts, MKVR), lambda i: (i, OFF_CKV // MKVR)),
                  pl.BlockSpec((1, MQR), lambda i: (0, 0)), pl.BlockSpec((1, MKVR), lambda i: (0, 0))],
        out_specs=[pl.BlockSpec((ts, MQR), lambda i: (i, 0)), pl.BlockSpec((ts, MKVR), lambda i: (i, 0)),
                   pl.BlockSpec((1, MQR), lambda i: (0, 0)), pl.BlockSpec((1, MKVR), lambda i: (0, 0))],
        out_shape=[jax.ShapeDtypeStruct((t, MQR), BF16), jax.ShapeDtypeStruct((t, MKVR), BF16),
                   jax.ShapeDtypeStruct((1, MQR), F32), jax.ShapeDtypeStruct((1, MKVR), F32)],
        compiler_params=_params(1),
    )(dcqn, dckvn, proj, proj, q_lat_g, kv_lat_g)


def _qk_prep(q_raw, kv, proj, cos_t, sin_t, gq, gk, *, name, ts=512):
    t = q_raw.shape[0]
    ts = min(ts, t)

    def body(q_ref, kv_ref, kpe_ref, c_ref, s_ref, gq_ref, gk_ref, qo_ref, ko_ref, vo_ref):
        cs, sn = c_ref[...], s_ref[...]
        nope = _lane() < MNOPE
        qn, _ = _norm96(q_ref[...], None)
        qn = qn * gq_ref[...]
        qo_ref[...] = (qn * cs + _swap_halves(qn) * sn).astype(BF16)
        kvv = kv_ref[...]
        kn, _ = _norm96(jnp.where(nope, kvv, kpe_ref[...]), None)
        kn = kn * gk_ref[...]
        ko_ref[...] = (kn * cs + _swap_halves(kn) * sn).astype(BF16)
        vo_ref[...] = jnp.where(nope, pltpu.roll(kvv, MNOPE, 1), 0.0).astype(BF16)

    hd = pl.BlockSpec((ts, LANE), lambda i, h: (i, h))
    shared = lambda col: pl.BlockSpec((ts, LANE), lambda i, h: (i, col))
    gain = pl.BlockSpec((1, LANE), lambda i, h: (0, 0))
    out = jax.ShapeDtypeStruct((t, MH * LANE), BF16)
    return pl.pallas_call(
        body, name=name, grid=(t // ts, MH),
        in_specs=[hd, hd, shared(OFF_KPE // LANE), shared(0), shared(0), gain, gain],
        out_specs=[hd, hd, hd], out_shape=[out, out, out], compiler_params=_params(2),
    )(q_raw, kv, proj, cos_t, sin_t, gq, gk)


def _qk_prep_bwd(dq, dk, dv, q_raw, kv, proj, cos_t, sin_t, gq, gk, *, name, ts=512):
    t = q_raw.shape[0]
    ts = min(ts, t)

    def norm_bwd(dy, x, g, dg_ref):
        xh, r = _norm96(x, None)
        dg_ref[...] += jnp.sum(dy * xh, axis=0, keepdims=True)
        dxh = dy * g
        return r * (dxh - xh * (jnp.sum(dxh * xh, axis=-1, keepdims=True) * (1.0 / MQK)))

    def body(dq_ref, dk_ref, dv_ref, q_ref, kv_ref, kpe_ref, c_ref, s_ref, gq_ref, gk_ref,
             dqr_ref, dkv_ref, dkpe_ref, dgq_ref, dgk_ref):
        i, h = pl.program_id(0), pl.program_id(1)

        @pl.when(h == 0)
        def _():
            dkpe_ref[...] = jnp.zeros_like(dkpe_ref)

        @pl.when((h == 0) & (i == 0))
        def _():
            dgq_ref[...] = jnp.zeros_like(dgq_ref)
            dgk_ref[...] = jnp.zeros_like(dgk_ref)

        cs, sn = c_ref[...], s_ref[...]
        lane = _lane()
        nope = lane < MNOPE
        dqv = dq_ref[...]
        dqn = dqv * cs + _swap_halves(dqv * sn)
        dqr_ref[...] = norm_bwd(dqn, q_ref[...], gq_ref[...], dgq_ref).astype(BF16)
        dkv_ = dk_ref[...]
        dkn = dkv_ * cs + _swap_halves(dkv_ * sn)
        kvv = kv_ref[...]
        dkr = norm_bwd(dkn, jnp.where(nope, kvv, kpe_ref[...]), gk_ref[...], dgk_ref)
        dkv_ref[...] = jnp.where(nope, dkr, pltpu.roll(dv_ref[...], MNOPE, 1)).astype(BF16)
        dkpe_ref[...] += jnp.where((lane >= MNOPE) & (lane < MQK), dkr, 0.0)

    hd = pl.BlockSpec((ts, LANE), lambda i, h: (i, h))
    shared = lambda col: pl.BlockSpec((ts, LANE), lambda i, h: (i, col))
    gain = pl.BlockSpec((1, LANE), lambda i, h: (0, 0))
    out = jax.ShapeDtypeStruct((t, MH * LANE), BF16)
    return pl.pallas_call(
        body, name=name, grid=(t // ts, MH),
        in_specs=[hd, hd, hd, hd, hd, shared(OFF_KPE // LANE), shared(0), shared(0), gain, gain],
        out_specs=[hd, hd, shared(0), gain, gain],
        out_shape=[out, out, jax.ShapeDtypeStruct((t, LANE), F32), jax.ShapeDtypeStruct((1, LANE), F32),
                   jax.ShapeDtypeStruct((1, LANE), F32)],
        compiler_params=_params(2),
    )(dq, dk, dv, q_raw, kv, proj, cos_t, sin_t, gq, gk)


_NT = (((1,), (1,)), ((), ()))
_TN = (((0,), (0,)), ((), ()))


def _attn_probs(q, k_ref, lo, tq):
    scale = MQK ** -0.5
    row = lax.broadcasted_iota(jnp.int32, (tq, tq), 0) // CHUNK
    col = lax.broadcasted_iota(jnp.int32, (tq, tq), 1) // CHUNK
    sd = lax.dot_general(q, k_ref[pl.ds(lo, tq), :], _NT, preferred_element_type=F32) * scale
    sd = jnp.where(col <= row, sd, -1e30)
    m = jnp.max(sd, axis=-1, keepdims=True)
    if lo:
        so = lax.dot_general(q, k_ref[pl.ds(0, lo), :], _NT, preferred_element_type=F32) * scale
        m = jnp.maximum(m, jnp.max(so, axis=-1, keepdims=True))
        po = jnp.exp(so - m)
        pd = jnp.exp(sd - m)
        inv = 1.0 / (jnp.sum(po, axis=-1, keepdims=True) + jnp.sum(pd, axis=-1, keepdims=True))
        return po * inv, pd * inv
    pd = jnp.exp(sd - m)
    return None, pd * (1.0 / jnp.sum(pd, axis=-1, keepdims=True))


def _attn_fwd(q, k, v, *, name, tq=256):
    bsz, s, _ = q.shape
    tq = min(tq, s)

    def body(q_ref, k_ref, v_ref, o_ref):
        for i in range(s // tq):
            lo = i * tq
            po, pd = _attn_probs(q_ref[pl.ds(lo, tq), :], k_ref, lo, tq)
            o = jnp.dot(pd.astype(BF16), v_ref[pl.ds(lo, tq), :], preferred_element_type=F32)
            if lo:
                o += jnp.dot(po.astype(BF16), v_ref[pl.ds(0, lo), :], preferred_element_type=F32)
            o_ref[pl.ds(lo, tq), :] = o.astype(BF16)

    spec = pl.BlockSpec((None, s, LANE), lambda b, h: (b, 0, h))
    return pl.pallas_call(
        body, name=name, grid=(bsz, MH), in_specs=[spec, spec, spec], out_specs=spec,
        out_shape=jax.ShapeDtypeStruct((bsz, s, MH * LANE), BF16), compiler_params=_params(2),
    )(q, k, v)


def _attn_bwd(q, k, v, do, *, name, tq=256):
    bsz, s, _ = q.shape
    tq = min(tq, s)
    scale = MQK ** -0.5

    def body(q_ref, k_ref, v_ref, do_ref, dq_ref, dk_ref, dv_ref):
        dk_ref[...] = jnp.zeros_like(dk_ref)
        dv_ref[...] = jnp.zeros_like(dv_ref)
        for i in range(s // tq):
            lo = i * tq
            here, before = pl.ds(lo, tq), pl.ds(0, lo)
            qv, dov = q_ref[here, :], do_ref[here, :]
            po, pd = _attn_probs(qv, k_ref, lo, tq)
            dv_ref[here, :] += lax.dot_general(pd.astype(BF16), dov, _TN, preferred_element_type=F32)
            dpd = lax.dot_general(dov, v_ref[here, :], _NT, preferred_element_type=F32)
            delta = jnp.sum(dpd * pd, axis=-1, keepdims=True)
            if lo:
                dv_ref[before, :] += lax.dot_general(po.astype(BF16), dov, _TN, preferred_element_type=F32)
                dpo = lax.dot_general(dov, v_ref[before, :], _NT, preferred_element_type=F32)
                delta += jnp.sum(dpo * po, axis=-1, keepdims=True)
            dsd = (pd * (dpd - delta) * scale).astype(BF16)
            dq = jnp.dot(dsd, k_ref[here, :], preferred_element_type=F32)
            dk_ref[here, :] += lax.dot_general(dsd, qv, _TN, preferred_element_type=F32)
            if lo:
                dso = (po * (dpo - delta) * scale).astype(BF16)
                dq += jnp.dot(dso, k_ref[before, :], preferred_element_type=F32)
                dk_ref[before, :] += lax.dot_general(dso, qv, _TN, preferred_element_type=F32)
            dq_ref[here, :] = dq

    spec = pl.BlockSpec((None, s, LANE), lambda b, h: (b, 0, h))
    out = jax.ShapeDtypeStruct((bsz, s, MH * LANE), F32)
    return pl.pallas_call(
        body, name=name, grid=(bsz, MH), in_specs=[spec] * 4, out_specs=[spec] * 3, out_shape=[out, out, out],
        compiler_params=_params(2),
    )(q, k, v, do)


def _adamw(w, g, m, v, *, name, tr=256):
    rows, cols = w.shape
    tr = _tile_rows(rows, tr)

    def body(w_ref, g_ref, m_ref, v_ref, d_ref, nm_ref, nv_ref):
        gv = g_ref[...]
        nm = ADAM_B1 * m_ref[...] + (1.0 - ADAM_B1) * gv
        nv = ADAM_B2 * v_ref[...] + (1.0 - ADAM_B2) * (gv * gv)
        m_hat = nm / (1.0 - ADAM_B1 ** ADAM_STEP)
        v_hat = nv / (1.0 - ADAM_B2 ** ADAM_STEP)
        d_ref[...] = -ADAM_LR * (m_hat / (jnp.sqrt(v_hat) + ADAM_EPS) + ADAM_WD * w_ref[...])
        nm_ref[...] = nm
        nv_ref[...] = nv

    spec = pl.BlockSpec((tr, cols), lambda i: (i, 0))
    out = jax.ShapeDtypeStruct((rows, cols), F32)
    return pl.pallas_call(body, name=name, grid=(rows // tr,), in_specs=[spec] * 4, out_specs=[spec] * 3,
                          out_shape=[out, out, out], compiler_params=_params(1))(w, g, m, v)


def _tile_rows(rows, target):
    if rows <= target:
        return rows
    best = 8
    for t in range(8, target + 1, 8):
        if rows % t == 0:
            best = t
    return best


def _sum_slots(x, *, name, tr=1024):
    n, rows, _ = x.shape
    tr = _tile_rows(rows, tr)

    def body(x_ref, o_ref):
        acc = x_ref[0].astype(F32)
        for j in range(1, n):
            acc = acc + x_ref[j].astype(F32)
        o_ref[...] = acc

    return pl.pallas_call(
        body, name=name, grid=(rows // tr,), in_specs=[pl.BlockSpec((n, tr, LANE), lambda i: (0, i, 0))],
        out_specs=pl.BlockSpec((tr, LANE), lambda i: (i, 0)), out_shape=jax.ShapeDtypeStruct((rows, LANE), F32),
        compiler_params=_params(1))(x)


def _add2(a, b, *, name, out_dtype=F32, tr=1024):
    n, rows, _ = a.shape
    tr = _tile_rows(rows, tr)

    def body(a_ref, b_ref, o_ref):
        o_ref[...] = (a_ref[...] + b_ref[...]).astype(out_dtype)

    spec = pl.BlockSpec((None, tr, LANE), lambda j, i: (j, i, 0))
    return pl.pallas_call(body, name=name, grid=(n, rows // tr), in_specs=[spec, spec], out_specs=spec,
                          out_shape=jax.ShapeDtypeStruct(a.shape, out_dtype), compiler_params=_params(2))(a, b)


def _me():
    return lax.axis_index("x"), lax.axis_index("y"), lax.axis_index("c")


def _flip(pos, bits):
    x, y, c = pos
    return (x ^ bits[0] if bits[0] else x, y ^ bits[1] if bits[1] else y, c ^ bits[2] if bits[2] else c)


ANY = pl.BlockSpec(memory_space=pl.ANY)


def _all_gather8(x, *, name):
    flips = [((k >> 2) & 1, (k >> 1) & 1, k & 1) for k in range(1, 8)]

    def body(x_ref, out_ref, send_sems, recv_sems, local_sem):
        me = _me()
        slot = lambda p: 4 * p[0] + 2 * p[1] + p[2]
        mine = pltpu.make_async_copy(x_ref, out_ref.at[slot(me)], local_sem)
        mine.start()
        sends = []
        for k, f in enumerate(flips):
            cp = pltpu.make_async_remote_copy(src_ref=x_ref, dst_ref=out_ref.at[slot(me)], send_sem=send_sems.at[k],
                                              recv_sem=recv_sems.at[k], device_id=_flip(me, f), device_id_type=MESH)
            cp.start()
            sends.append(cp)
        for k, f in enumerate(flips):
            peer = _flip(me, f)
            pltpu.make_async_remote_copy(src_ref=x_ref, dst_ref=out_ref.at[slot(peer)], send_sem=send_sems.at[k],
                                         recv_sem=recv_sems.at[k], device_id=peer, device_id_type=MESH).wait_recv()
        for cp in sends:
            cp.wait_send()
        mine.wait()

    return pl.pallas_call(
        body, name=name, in_specs=[ANY], out_specs=ANY, out_shape=jax.ShapeDtypeStruct((8, *x.shape), x.dtype),
        scratch_shapes=[pltpu.SemaphoreType.DMA((7,)), pltpu.SemaphoreType.DMA((7,)), pltpu.SemaphoreType.DMA])(x)


CHIP_FLIPS = [(1, 0, 0), (0, 1, 0), (1, 1, 0)]


def _all_gather_chips(x, *, name):
    rows, cols = x.shape
    hr = rows // 2

    def body(x_ref, out_ref, send_sems, recv_sems, local_sem):
        me = _me()
        sib = _flip(me, (0, 0, 1))
        slot = lambda p: 2 * p[0] + p[1]
        my_half, their_half = me[2], 1 - me[2]

        def copy(k, src, dst, to):
            return pltpu.make_async_remote_copy(src_ref=src, dst_ref=dst, send_sem=send_sems.at[k],
                                                recv_sem=recv_sems.at[k], device_id=to, device_id_type=MESH)

        mine = pltpu.make_async_copy(x_ref, out_ref.at[slot(me)], local_sem)
        mine.start()
        sends = []
        for k, f in enumerate(CHIP_FLIPS):
            cp = copy(k, x_ref.at[my_half], out_ref.at[slot(me), my_half], _flip(me, f))
            cp.start()
            sends.append(cp)
        for k, f in enumerate(CHIP_FLIPS):
            landed = out_ref.at[slot(_flip(me, f)), my_half]
            copy(k, landed, landed, me).wait_recv()
            cp = copy(3 + k, landed, landed, sib)
            cp.start()
            sends.append(cp)
        for k, f in enumerate(CHIP_FLIPS):
            from_sib = out_ref.at[slot(_flip(me, f)), their_half]
            copy(3 + k, from_sib, from_sib, sib).wait_recv()
        for cp in sends:
            cp.wait_send()
        mine.wait()

    return pl.pallas_call(
        body, name=name, in_specs=[ANY], out_specs=ANY, out_shape=jax.ShapeDtypeStruct((4, 2, hr, cols), x.dtype),
        scratch_shapes=[pltpu.SemaphoreType.DMA((6,)), pltpu.SemaphoreType.DMA((6,)), pltpu.SemaphoreType.DMA],
    )(x.reshape(2, hr, cols)).reshape(4, rows, cols)


def _pair_swap_halves(x, *, name):
    n, rows, cols = x.shape
    hr = rows // 2

    def body(x_ref, out_ref, send_sem, recv_sem):
        me = _me()
        sib = _flip(me, (0, 0, 1))
        cp = pltpu.make_async_remote_copy(src_ref=x_ref.at[:, 1 - me[2]], dst_ref=out_ref, send_sem=send_sem,
                                          recv_sem=recv_sem, device_id=sib, device_id_type=MESH)
        cp.start()
        cp.wait()

    return pl.pallas_call(
        body, name=name, in_specs=[ANY], out_specs=ANY, out_shape=jax.ShapeDtypeStruct((n, hr, cols), x.dtype),
        scratch_shapes=[pltpu.SemaphoreType.DMA, pltpu.SemaphoreType.DMA])(x.reshape(n, 2, hr, cols))


def _scatter_chips(p, *, name):
    n, rows, cols = p.shape

    def body(p_ref, out_ref, send_sems, recv_sems, local_sem):
        me = _me()
        slot = lambda q: 2 * q[0] + q[1]
        mine = pltpu.make_async_copy(p_ref.at[slot(me)], out_ref.at[slot(me)], local_sem)
        mine.start()
        sends = []
        for k, f in enumerate(CHIP_FLIPS):
            peer = _flip(me, f)
            cp = pltpu.make_async_remote_copy(src_ref=p_ref.at[slot(peer)], dst_ref=out_ref.at[slot(me)],
                                              send_sem=send_sems.at[k], recv_sem=recv_sems.at[k], device_id=peer,
                                              device_id_type=MESH)
            cp.start()
            sends.append(cp)
        for k, f in enumerate(CHIP_FLIPS):
            peer = _flip(me, f)
            pltpu.make_async_remote_copy(src_ref=p_ref.at[slot(me)], dst_ref=out_ref.at[slot(peer)],
                                         send_sem=send_sems.at[k], recv_sem=recv_sems.at[k], device_id=peer,
                                         device_id_type=MESH).wait_recv()
        for cp in sends:
            cp.wait_send()
        mine.wait()

    return pl.pallas_call(
        body, name=name, in_specs=[ANY], out_specs=ANY, out_shape=jax.ShapeDtypeStruct(p.shape, p.dtype),
        scratch_shapes=[pltpu.SemaphoreType.DMA((3,)), pltpu.SemaphoreType.DMA((3,)), pltpu.SemaphoreType.DMA])(p)


def _pair_join_halves(h, *, name):
    hr, cols = h.shape

    def body(h_ref, out_ref, send_sem, recv_sem, local_sem):
        me = _me()
        sib = _flip(me, (0, 0, 1))
        my_rows, their_rows = out_ref.at[me[2]], out_ref.at[1 - me[2]]
        mine = pltpu.make_async_copy(h_ref, my_rows, local_sem)
        mine.start()
        cp = pltpu.make_async_remote_copy(src_ref=h_ref, dst_ref=my_rows, send_sem=send_sem, recv_sem=recv_sem,
                                          device_id=sib, device_id_type=MESH)
        cp.start()
        pltpu.make_async_remote_copy(src_ref=h_ref, dst_ref=their_rows, send_sem=send_sem, recv_sem=recv_sem,
                                     device_id=sib, device_id_type=MESH).wait_recv()
        cp.wait_send()
        mine.wait()

    return pl.pallas_call(
        body, name=name, in_specs=[ANY], out_specs=ANY, out_shape=jax.ShapeDtypeStruct((2, hr, cols), h.dtype),
        scratch_shapes=[pltpu.SemaphoreType.DMA, pltpu.SemaphoreType.DMA, pltpu.SemaphoreType.DMA])(h).reshape(2 * hr, cols)


BIG = (("w_in", (D, IN_WIDTH // 4), 1), ("gla_w_o", (D // 4, D), 0), ("mla_w_uq", (MQR, MH * MQK // 4), 1),
       ("mla_w_ukv", (MKVR, MH * (MNOPE + MVD) // 4), 1), ("mla_w_o", (D // 4, D), 0), ("w_out", (D // 4, D), 0),
       ("mlp_w1", (D, DFF // 4), 1), ("mlp_w2", (DFF // 4, D), 0))
ADA_SHARD = (D, 6 * D // 4)
SMALL = (("b_ada", 6 * D), ("norm1_g", D), ("b_merge", 2 * D), ("gla_b_alpha", GH * GDK), ("gla_out_norm_g", GDV),
         ("mla_q_lat_g", MQR), ("mla_kv_lat_g", MKVR), ("mla_qn_g", MQK), ("mla_kn_g", MQK), ("norm2_g", D))


PACK_ROWS = 2048


def _rows_of(n):
    return -(-n // (8 * LANE)) * 8


def _flat(a):
    v = a.reshape(-1)
    rows = _rows_of(v.shape[0])
    if rows * LANE != v.shape[0]:
        v = jnp.pad(v, (0, rows * LANE - v.shape[0]))
    return v.reshape(rows, LANE)


def _pack(arrs, multiple=1):
    parts = [_flat(a) for a in arrs]
    rows = sum(p.shape[0] for p in parts)
    if rows % multiple:
        parts.append(jnp.zeros((multiple - rows % multiple, LANE), parts[0].dtype))
    return jnp.concatenate(parts, axis=0)


def _unpack(flat, shapes):
    out, r = [], 0
    for shp in shapes:
        n = int(np.prod(shp))
        rows = _rows_of(n)
        out.append(flat[r:r + rows].reshape(-1)[:n].reshape(shp))
        r += rows
    return out


def _full_weights(slots):
    per_chip = [_unpack(slots[j], [shp for _, shp, _ in BIG]) for j in range(4)]
    w = {name: jnp.concatenate([per_chip[j][i] for j in range(4)], axis=axis) for i, (name, _, axis) in enumerate(BIG)}
    wi = w["w_in"]
    zeros = lambda n: jnp.zeros((D, n), wi.dtype)
    w["w_in"] = jnp.concatenate(
        [wi[:, :3072], wi[:, 3504:5552], wi[:, 3088:3344], wi[:, 3344:3472], wi[:, 3072:3088], zeros(LANE - GLR),
         zeros(MNOPE), wi[:, 3472:3504], zeros(LANE - MQK)], axis=1)
    w["mla_w_uq"] = jnp.pad(w["mla_w_uq"].reshape(MQR, MH, MQK), ((0, 0), (0, 0), (0, LANE - MQK))).reshape(MQR, MH * LANE)
    w["mla_w_o"] = jnp.pad(w["mla_w_o"].reshape(MH, MVD, D), ((0, 0), (0, LANE - MVD), (0, 0))).reshape(MH * LANE, D)
    return w


def _grad_slots(g):
    gi = g["w_in"]
    g = dict(g)
    g["w_in"] = jnp.concatenate(
        [gi[:, :3072], gi[:, OFF_A:OFF_A + GLR], gi[:, OFF_CQ:OFF_CQ + MQR], gi[:, OFF_CKV:OFF_CKV + MKVR],
         gi[:, OFF_KPE + MNOPE:OFF_KPE + MQK], gi[:, OFF_MA:OFF_MA + 2 * D]], axis=1)
    g["mla_w_uq"] = g["mla_w_uq"].reshape(MQR, MH, LANE)[:, :, :MQK].reshape(MQR, MH * MQK)
    g["mla_w_o"] = g["mla_w_o"].reshape(MH, LANE, D)[:, :MVD].reshape(MH * MVD, D)
    slots = []
    for j in range(4):
        parts = []
        for name, shp, axis in BIG:
            n = shp[axis]
            parts.append(lax.slice_in_dim(g[name], j * n, (j + 1) * n, axis=axis))
        slots.append(_pack(parts, multiple=PACK_ROWS))
    return jnp.stack(slots)


def _rope_tables(positions):
    freqs = ROPE_THETA ** (-jnp.arange(0, MROPE, 2, dtype=F32) / MROPE)
    ang = positions.astype(F32)[..., None] * freqs
    cos, sin = jnp.cos(ang), jnp.sin(ang)
    shape = ang.shape[:-1]
    cos_t = jnp.concatenate([jnp.ones(shape + (MNOPE,), F32), cos, cos, jnp.ones(shape + (LANE - MQK,), F32)], axis=-1)
    sin_t = jnp.concatenate([jnp.zeros(shape + (MNOPE,), F32), -sin, sin, jnp.zeros(shape + (LANE - MQK,), F32)], axis=-1)
    return cos_t.reshape(-1, LANE), sin_t.reshape(-1, LANE)


def _local_step(x, positions, mod, target, w, small):
    bsz, s, _ = x.shape
    t = bsz * s
    tt = _tile(t, 1024)
    shift1, scale1, gate1, shift2, scale2, gate2 = [mod[:, None, i * D:(i + 1) * D] for i in range(6)]
    cos_t, sin_t = _rope_tables(positions)
    w_alpha_p = jnp.pad(small["gla_w_alpha"], ((0, LANE - GLR), (0, 0)))
    gq = jnp.pad(small["mla_qn_g"], ((0, 0), (0, LANE - MQK)))
    gk = jnp.pad(small["mla_kn_g"], ((0, 0), (0, LANE - MQK)))
    flat2 = lambda a: a.reshape(t, a.shape[-1])
    bsd = lambda a: a.reshape(bsz, s, a.shape[-1])

    h = _norm_mod(x, small["norm1_g"], scale1, shift1, name="norm1")
    proj = _mm(flat2(h), w["w_in"], name="proj", tn=1152)
    proj3 = bsd(proj)
    o, o_gated, states = _gla_fwd(proj3, w_alpha_p, small["gla_b_alpha"], small["gla_out_norm_g"], name="gla_fwd")
    y_a = _mm(flat2(o_gated), w["gla_w_o"], name="gla_out")
    cq_n, ckv_n = _lat_norm(proj, small["mla_q_lat_g"], small["mla_kv_lat_g"], name="lat_norm")
    q_raw = _mm(cq_n, w["mla_w_uq"], name="mla_uq")
    kv = _mm(ckv_n, w["mla_w_ukv"], name="mla_ukv")
    qf, kf, vf = _qk_prep(q_raw, kv, proj, cos_t, sin_t, gq, gk, name="qk_prep")
    o_attn = _attn_fwd(bsd(qf), bsd(kf), bsd(vf), name="attn_fwd")
    y_b = _mm(flat2(o_attn), w["mla_w_o"], name="mla_out")
    mixed_in = _merge_fwd(proj3, small["b_merge"], bsd(y_a), bsd(y_b), name="merge_fwd")
    mixed = _mm(flat2(mixed_in), w["w_out"], name="w_out")
    x1, h2 = _resid_norm_mod(x, bsd(mixed), gate1, small["norm2_g"], scale2, shift2, name="norm2")

    def sqrelu(acc, ex, outs):
        outs[0][...] = acc
        r = jnp.maximum(acc, 0.0)
        outs[1][...] = (r * r).astype(BF16)

    a1, r = _mm(flat2(h2), w["mlp_w1"], name="mlp1", epilogue=sqrelu,
                out_shape=[jax.ShapeDtypeStruct((t, DFF), F32), jax.ShapeDtypeStruct((t, DFF), BF16)],
                out_specs=[_tile_spec(tt, 1024), _tile_spec(tt, 1024)])
    ff = _mm(r, w["mlp_w2"], name="mlp2")
    dy, dff, dgate2, loss_part = _loss_head(x1, bsd(ff), gate2, target, name="loss_head")

    g = {}

    def relu2_bwd(acc, ex, outs):
        outs[0][...] = (acc * (2.0 * jnp.maximum(ex[0][...], 0.0))).astype(BF16)

    dff2 = flat2(dff)
    da1 = _mm(dff2, w["mlp_w2"], tb=True, name="mlp2_dx", epilogue=relu2_bwd, extras=(a1,),
              extra_specs=(_tile_spec(tt, 1024),), out_shape=jax.ShapeDtypeStruct((t, DFF), BF16),
              out_specs=_tile_spec(tt, 1024))
    g["mlp_w2"] = _mm(r, dff2, ta=True, name="mlp2_dw")
    dh2 = _mm(da1, w["mlp_w1"], tb=True, name="mlp1_dx")
    g["mlp_w1"] = _mm(flat2(h2), da1, ta=True, name="mlp1_dw")
    dx1, dscale2, dshift2, dg2, dgate1, dmixed = _norm_mod_bwd(
        bsd(dh2), x1, dy, small["norm2_g"], scale2, gate1, bsd(mixed), name="norm2_bwd")
    dmixed2 = flat2(dmixed)
    dmi = _mm(dmixed2, w["w_out"], tb=True, name="w_out_dx")
    g["w_out"] = _mm(flat2(mixed_in), dmixed2, ta=True, name="w_out_dw")
    dy_a, dy_b, dl_a, dl_b, db_a, db_b = _merge_bwd(bsd(dmi), proj3, small["b_merge"], bsd(y_a), bsd(y_b), name="merge_bwd")
    dy_a2, dy_b2 = flat2(dy_a), flat2(dy_b)
    dog = _mm(dy_a2, w["gla_w_o"], tb=True, name="gla_out_dx")
    g["gla_w_o"] = _mm(flat2(o_gated), dy_a2, ta=True, name="gla_out_dw")
    dq_g, dk_g, dv_g, dg_g, dlog, db_alpha, d_ong = _gla_bwd(
        bsd(dog), o, states, proj3, w_alpha_p, small["gla_b_alpha"], small["gla_out_norm_g"], name="gla_bwd")
    dlog2 = flat2(dlog)
    da_p = _mm(dlog2, w_alpha_p, tb=True, out_dtype=BF16, name="alpha_dx")
    d_w_alpha = _mm(proj[:, OFF_A:OFF_A + LANE], dlog2, ta=True, name="alpha_dw")[:GLR]
    do_attn = _mm(dy_b2, w["mla_w_o"], tb=True, out_dtype=BF16, name="mla_out_dx")
    g["mla_w_o"] = _mm(flat2(o_attn), dy_b2, ta=True, name="mla_out_dw")
    dqf, dkf, dvf = _attn_bwd(bsd(qf), bsd(kf), bsd(vf), bsd(do_attn), name="attn_bwd")
    dq_raw, dkv, dkpe, dgq, dgk = _qk_prep_bwd(flat2(dqf), flat2(dkf), flat2(dvf), q_raw, kv, proj, cos_t, sin_t, gq, gk,
                                                name="qk_prep_bwd")
    dcq_n = _mm(dq_raw, w["mla_w_uq"], tb=True, name="mla_uq_dx")
    g["mla_w_uq"] = _mm(cq_n, dq_raw, ta=True, name="mla_uq_dw")
    dckv_n = _mm(dkv, w["mla_w_ukv"], tb=True, name="mla_ukv_dx")
    g["mla_w_ukv"] = _mm(ckv_n, dkv, ta=True, name="mla_ukv_dw")
    dcq, dckv, dg_qlat, dg_kvlat = _lat_norm_bwd(dcq_n, dckv_n, proj, small["mla_q_lat_g"], small["mla_kv_lat_g"],
                                                  name="lat_norm_bwd")
    dproj = jnp.concatenate([flat2(dq_g), flat2(dk_g), flat2(dv_g), flat2(dg_g), flat2(dl_a), flat2(dl_b), dcq, dckv,
                             da_p, dkpe.astype(BF16)], axis=1)
    dh = _mm(dproj, w["w_in"], tb=True, name="proj_dx", tk=1152)
    g["w_in"] = _mm(flat2(h), dproj, ta=True, name="proj_dw", tn=1152)
    grad_x, dscale1, dshift1, dg1 = _norm_mod_bwd(bsd(dh), x, dx1, small["norm1_g"], scale1, name="norm1_bwd")

    dmod = jnp.concatenate([dshift1, dscale1, dgate1, dshift2, dscale2, dgate2], axis=-1).reshape(bsz, 6 * D)
    gs = {"norm1_g": dg1, "b_merge": jnp.concatenate([db_a, db_b], axis=1), "gla_b_alpha": db_alpha,
          "gla_out_norm_g": d_ong, "mla_q_lat_g": dg_qlat, "mla_kv_lat_g": dg_kvlat, "mla_qn_g": dgq[:, :MQK],
          "mla_kn_g": dgk[:, :MQK], "norm2_g": dg2}
    return loss_part[0, 0], grad_x, dmod, g, gs, d_w_alpha


def kernel(x, c, positions, w_ada, b_ada, norm1_g, w_in, b_merge, gla_w_alpha, gla_b_alpha, gla_out_norm_g, gla_w_o, mla_q_lat_g, mla_w_uq, mla_kv_lat_g, mla_w_ukv, mla_qn_g, mla_kn_g, mla_w_o, w_out, norm2_g, mlp_w1, mlp_w2, loss_target, m_w_ada, m_b_ada, m_norm1_g, m_w_in, m_b_merge, m_gla_w_alpha, m_gla_b_alpha, m_gla_out_norm_g, m_gla_w_o, m_mla_q_lat_g, m_mla_w_uq, m_mla_kv_lat_g, m_mla_w_ukv, m_mla_qn_g, m_mla_kn_g, m_mla_w_o, m_w_out, m_norm2_g, m_mlp_w1, m_mlp_w2, v_w_ada, v_b_ada, v_norm1_g, v_w_in, v_b_merge, v_gla_w_alpha, v_gla_b_alpha, v_gla_out_norm_g, v_gla_w_o, v_mla_q_lat_g, v_mla_w_uq, v_mla_kv_lat_g, v_mla_w_ukv, v_mla_qn_g, v_mla_kn_g, v_mla_w_o, v_w_out, v_norm2_g, v_mlp_w1, v_mlp_w2):
    args = dict(locals())
    names_big = [n for n, _, _ in BIG]
    names_small = [n for n, _ in SMALL]
    bsz = x.shape[0]
    ax, ay, ac = lax.axis_index("x"), lax.axis_index("y"), lax.axis_index("c")
    chip = 2 * ax + ay
    dev = 2 * chip + ac

    wp = _pack([args[n][0] for n in names_big], multiple=PACK_ROWS)
    slots = _all_gather_chips(wp.astype(BF16), name="comm_weights")
    w = _full_weights(slots)
    small = {n: args[n] for n in names_small}
    w_alpha_all = _all_gather8(gla_w_alpha[0], name="comm_w_alpha")
    small["gla_w_alpha"] = jnp.concatenate([w_alpha_all[2 * j] for j in range(4)], axis=1)

    c_all = _all_gather8(c, name="comm_c").reshape(8 * bsz, D)

    def add_bias(acc, ex, outs):
        outs[0][...] = acc + ex[0][...]

    silu = lambda v: v * _sigmoid(v)
    b_ada_mine = lax.dynamic_slice(b_ada, (0, chip * ADA_SHARD[1]), (1, ADA_SHARD[1]))
    mod_part = _mm(c_all, w_ada[0], name="ada", tn=512, a_fn=silu, epilogue=add_bias, extras=(b_ada_mine,),
                   extra_specs=(pl.BlockSpec((1, 512), lambda i, j, k: (0, j)),),
                   out_shape=jax.ShapeDtypeStruct((8 * bsz, ADA_SHARD[1]), F32), out_specs=_tile_spec(8 * bsz, 512))
    mod_all = _all_gather8(mod_part, name="comm_mod")
    mod_rows = lax.dynamic_slice(mod_all, (0, dev * bsz, 0), (8, bsz, ADA_SHARD[1]))
    mod = jnp.concatenate([mod_rows[2 * j] for j in range(4)], axis=1)

    loss_part, grad_x, dmod, g, gs, d_w_alpha = _local_step(x, positions, mod, loss_target, w, small)
    loss = lax.psum(loss_part * (0.5 / D), ("x", "y", "c"))

    dmod_all = _all_gather8(dmod, name="comm_dmod").reshape(8 * bsz, 6 * D)
    dmod_mine = lax.dynamic_slice(dmod_all, (0, chip * ADA_SHARD[1]), (8 * bsz, ADA_SHARD[1]))
    g_w_ada = _mm(c_all, dmod_mine, ta=True, a_fn=silu, name="ada_dw")
    g_b_ada = _sum_slots(dmod_all.reshape(8 * bsz, 6 * D // LANE, LANE), name="sum_b_ada", tr=6 * D // LANE)

    names_red = [n for n in names_small if n != "b_ada"]
    red_shapes = [(1, n) for name, n in SMALL if name != "b_ada"] + [(GLR, GH * GDK)]
    gs_packed = _pack([gs[n] for n in names_red] + [d_w_alpha], multiple=8)
    gs_sum = _sum_slots(_all_gather8(gs_packed, name="comm_small"), name="sum_small")
    gs_full = _unpack(gs_sum, red_shapes)
    g_small = dict(zip(names_red, gs_full[:-1]))
    g_small["b_ada"] = g_b_ada.reshape(1, 6 * D)
    g_w_alpha = lax.dynamic_slice(gs_full[-1], (0, chip * GDK), (GLR, GDK))

    gslots = _grad_slots(g)
    rows = gslots.shape[1]
    hr = rows // 2
    sib_half = _pair_swap_halves(gslots, name="comm_pair_sum")
    my_half = lax.dynamic_slice(gslots, (0, ac * hr, 0), (4, hr, LANE))
    pair = _add2(my_half, sib_half, name="pair_add", out_dtype=BF16)
    from_chips = _scatter_chips(pair, name="comm_scatter")
    g_half = _sum_slots(from_chips, name="chip_sum")
    g_big = _pair_join_halves(g_half, name="comm_pair_join")

    res = {}
    g_mats = dict(zip(names_big, _unpack(g_big, [shp for _, shp, _ in BIG])))
    g_mats["w_ada"] = g_w_ada
    for n, gm in g_mats.items():
        res[n] = (gm, *_adamw(args[n][0], gm, args["m_" + n][0], args["v_" + n][0], name="adamw_" + n))
    sm_names = names_small + ["gla_w_alpha"]
    sm_shapes = [(1, n) for _, n in SMALL] + [(GLR, GDK)]
    g_sm = [g_small[n] for n in names_small] + [g_w_alpha]
    pack_sm = lambda prefix: _pack([args[prefix + n].reshape(shp) for n, shp in zip(sm_names, sm_shapes)], multiple=8)
    outs_sm = _adamw(pack_sm(""), _pack(g_sm, multiple=8), pack_sm("m_"), pack_sm("v_"), name="adamw_small")
    for n, gm, *rest in zip(sm_names, g_sm, *[_unpack(o, sm_shapes) for o in outs_sm]):
        res[n] = (gm, *rest)

    order = ["w_ada", "b_ada", "norm1_g", "w_in", "b_merge", "gla_w_alpha", "gla_b_alpha", "gla_out_norm_g", "gla_w_o",
             "mla_q_lat_g", "mla_w_uq", "mla_kv_lat_g", "mla_w_ukv", "mla_qn_g", "mla_kn_g", "mla_w_o", "w_out",
             "norm2_g", "mlp_w1", "mlp_w2"]
    named = lambda k: [res[n][k].reshape(args[n].shape) for n in order]
    return (loss, grad_x, *named(0), *named(1), *named(2), *named(3))
```

```python
import functools

import jax
import jax.numpy as jnp
import numpy as np
from jax import lax
from jax.experimental import pallas as pl
from jax.experimental.pallas import tpu as pltpu

F32 = jnp.float32
BF16 = jnp.bfloat16
MESH = pl.DeviceIdType.MESH

D = 1024
CHUNK = 64
EPS = 1e-6
GH, GDK, GDV, GLR, GTAU = 4, 128, 256, 16, 16.0
MH, MQR, MKVR, MNOPE, MROPE, MVD = 16, 256, 128, 64, 32, 64
MQK = MNOPE + MROPE
DFF = 4 * D
ROPE_THETA = 10000.0
IN_WIDTH = 5552
LANE = 128
OFF_Q, OFF_K, OFF_V, OFF_G, OFF_MA, OFF_MB, OFF_CQ, OFF_CKV, OFF_A, OFF_KPE, PW = (
    0, 512, 1024, 2048, 3072, 4096, 5120, 5376, 5504, 5632, 5760)
ADAM_LR, ADAM_B1, ADAM_B2, ADAM_EPS, ADAM_WD, ADAM_STEP = 0.001, 0.9, 0.999, 1e-08, 0.01, 10
VMEM_LIMIT = 48 * 1024 * 1024


def _params(n_axes):
    return pltpu.CompilerParams(dimension_semantics=("arbitrary",) * n_axes, vmem_limit_bytes=VMEM_LIMIT)


def _tile(n, target):
    if n <= target:
        return n
    best = None
    for t in range(LANE, target + 1, LANE):
        if n % t == 0:
            best = t
    assert best is not None, (n, target)
    return best


def _sigmoid(x):
    return 1.0 / (1.0 + jnp.exp(-x))


def _mm(a, b, *, name, ta=False, tb=False, out_dtype=F32, tm=1024, tn=1024, tk=1024,
        epilogue=None, extras=(), extra_specs=(), out_shape=None, out_specs=None, a_fn=None):
    if ta:
        kdim, m = a.shape
    else:
        m, kdim = a.shape
    if tb:
        n, k2 = b.shape
    else:
        k2, n = b.shape
    assert kdim == k2, (a.shape, b.shape)
    tm, tn, tk = _tile(m, tm), _tile(n, tn), _tile(kdim, tk)
    nk = kdim // tk
    a_spec = pl.BlockSpec((tk, tm), lambda i, j, k: (k, i)) if ta else pl.BlockSpec((tm, tk), lambda i, j, k: (i, k))
    b_spec = pl.BlockSpec((tn, tk), lambda i, j, k: (j, k)) if tb else pl.BlockSpec((tk, tn), lambda i, j, k: (k, j))
    dims = (((0 if ta else 1,), (1 if tb else 0,)), ((), ()))
    ne = len(extras)
    if out_shape is None:
        out_shape = jax.ShapeDtypeStruct((m, n), out_dtype)
        out_specs = pl.BlockSpec((tm, tn), lambda i, j, k: (i, j))

    def body(a_ref, b_ref, *rest):
        ex, outs, acc = rest[:ne], rest[ne:-1], rest[-1]
        k = pl.program_id(2)

        @pl.when(k == 0)
        def _():
            acc[...] = jnp.zeros_like(acc)

        av = a_ref[...] if a_fn is None else a_fn(a_ref[...])
        acc[...] += lax.dot_general(av.astype(BF16), b_ref[...].astype(BF16), dims, preferred_element_type=F32)

        @pl.when(k == nk - 1)
        def _():
            if epilogue is None:
                outs[0][...] = acc[...].astype(outs[0].dtype)
            else:
                epilogue(acc[...], ex, outs)

    return pl.pallas_call(
        body, name=name, grid=(m // tm, n // tn, nk),
        in_specs=[a_spec, b_spec, *extra_specs], out_specs=out_specs, out_shape=out_shape,
        scratch_shapes=[pltpu.VMEM((tm, tn), F32)], compiler_params=_params(3),
    )(a, b, *extras)


def _tile_spec(tm, tn):
    return pl.BlockSpec((tm, tn), lambda i, j, k: (i, j))


def _rms(x, g):
    r = lax.rsqrt(jnp.mean(x * x, axis=-1, keepdims=True) + EPS)
    return x * r, r


def _row_spec(ts, width, col=0):
    return pl.BlockSpec((None, ts, width), lambda b, i: (b, i, col))


def _vec_spec(width):
    return pl.BlockSpec((None, 1, width), lambda b, i: (b, 0, 0))


def _gain_spec(width):
    return pl.BlockSpec((1, width), lambda b, i: (0, 0))


def _norm_mod(x, g, scale, shift, *, name, ts=256):
    bsz, s, d = x.shape
    ts = min(ts, s)

    def body(x_ref, g_ref, sc_ref, sh_ref, h_ref):
        xh, _ = _rms(x_ref[...], None)
        h_ref[...] = ((xh * g_ref[...]) * (1.0 + sc_ref[...]) + sh_ref[...]).astype(BF16)

    return pl.pallas_call(
        body, name=name, grid=(bsz, s // ts),
        in_specs=[_row_spec(ts, d), _gain_spec(d), _vec_spec(d), _vec_spec(d)],
        out_specs=_row_spec(ts, d), out_shape=jax.ShapeDtypeStruct((bsz, s, d), BF16),
        compiler_params=_params(2),
    )(x, g, scale, shift)


def _resid_norm_mod(x, mixed, gate, g, scale, shift, *, name, ts=256):
    bsz, s, d = x.shape
    ts = min(ts, s)

    def body(x_ref, mx_ref, gt_ref, g_ref, sc_ref, sh_ref, x1_ref, h_ref):
        x1 = x_ref[...] + gt_ref[...] * mx_ref[...]
        x1_ref[...] = x1
        xh, _ = _rms(x1, None)
        h_ref[...] = ((xh * g_ref[...]) * (1.0 + sc_ref[...]) + sh_ref[...]).astype(BF16)

    return pl.pallas_call(
        body, name=name, grid=(bsz, s // ts),
        in_specs=[_row_spec(ts, d), _row_spec(ts, d), _vec_spec(d), _gain_spec(d), _vec_spec(d), _vec_spec(d)],
        out_specs=[_row_spec(ts, d), _row_spec(ts, d)],
        out_shape=[jax.ShapeDtypeStruct((bsz, s, d), F32), jax.ShapeDtypeStruct((bsz, s, d), BF16)],
        compiler_params=_params(2),
    )(x, mixed, gate, g, scale, shift)


def _norm_mod_bwd(dh, xin, resid, g, scale, gate=None, mixed=None, *, name, ts=256):
    bsz, s, d = xin.shape
    ts = min(ts, s)
    gated = gate is not None

    def body(*refs):
        if gated:
            dh_ref, x_ref, rs_ref, g_ref, sc_ref, gt_ref, mx_ref, dx_ref, dsc_ref, dsh_ref, dg_ref, dgt_ref, dmx_ref = refs
        else:
            dh_ref, x_ref, rs_ref, g_ref, sc_ref, dx_ref, dsc_ref, dsh_ref, dg_ref = refs
        b, i = pl.program_id(0), pl.program_id(1)

        @pl.when(i == 0)
        def _():
            dsc_ref[...] = jnp.zeros_like(dsc_ref)
            dsh_ref[...] = jnp.zeros_like(dsh_ref)
            if gated:
                dgt_ref[...] = jnp.zeros_like(dgt_ref)

        @pl.when((i == 0) & (b == 0))
        def _():
            dg_ref[...] = jnp.zeros_like(dg_ref)

        dh_v, gv = dh_ref[...], g_ref[...]
        xh, r = _rms(x_ref[...], None)
        dsc_ref[...] += jnp.sum(dh_v * (xh * gv), axis=0, keepdims=True)
        dsh_ref[...] += jnp.sum(dh_v, axis=0, keepdims=True)
        dn = dh_v * (1.0 + sc_ref[...])
        dg_ref[...] += jnp.sum(dn * xh, axis=0, keepdims=True)
        dxh = dn * gv
        dx = rs_ref[...] + r * (dxh - xh * jnp.mean(dxh * xh, axis=-1, keepdims=True))
        dx_ref[...] = dx
        if gated:
            dgt_ref[...] += jnp.sum(dx * mx_ref[...], axis=0, keepdims=True)
            dmx_ref[...] = (dx * gt_ref[...]).astype(BF16)

    ins = [dh, xin, resid, g, scale]
    in_specs = [_row_spec(ts, d), _row_spec(ts, d), _row_spec(ts, d), _gain_spec(d), _vec_spec(d)]
    out_specs = [_row_spec(ts, d), _vec_spec(d), _vec_spec(d), _gain_spec(d)]
    out_shape = [jax.ShapeDtypeStruct((bsz, s, d), F32), jax.ShapeDtypeStruct((bsz, 1, d), F32),
                 jax.ShapeDtypeStruct((bsz, 1, d), F32), jax.ShapeDtypeStruct((1, d), F32)]
    if gated:
        ins += [gate, mixed]
        in_specs += [_vec_spec(d), _row_spec(ts, d)]
        out_specs += [_vec_spec(d), _row_spec(ts, d)]
        out_shape += [jax.ShapeDtypeStruct((bsz, 1, d), F32), jax.ShapeDtypeStruct((bsz, s, d), BF16)]
    return pl.pallas_call(
        body, name=name, grid=(bsz, s // ts), in_specs=in_specs, out_specs=out_specs, out_shape=out_shape,
        compiler_params=_params(2),
    )(*ins)


def _loss_head(x1, ff, gate2, target, *, name, ts=256):
    bsz, s, d = x1.shape
    ts = min(ts, s)

    def body(x1_ref, ff_ref, gt_ref, t_ref, dy_ref, dff_ref, dgt_ref, loss_ref, acc):
        b, i = pl.program_id(0), pl.program_id(1)

        @pl.when(i == 0)
        def _():
            dgt_ref[...] = jnp.zeros_like(dgt_ref)

        @pl.when((i == 0) & (b == 0))
        def _():
            acc[...] = jnp.zeros_like(acc)

        ffv, gt = ff_ref[...], gt_ref[...]
        diff = (x1_ref[...] + gt * ffv) - t_ref[...]
        acc[...] += jnp.sum((diff * diff).reshape(ts // 8, 8, d), axis=0)
        dy = diff * (1.0 / d)
        dy_ref[...] = dy
        dgt_ref[...] += jnp.sum(dy * ffv, axis=0, keepdims=True)
        dff_ref[...] = (dy * gt).astype(BF16)

        @pl.when((i == pl.num_programs(1) - 1) & (b == pl.num_programs(0) - 1))
        def _():
            loss_ref[...] = jnp.full(loss_ref.shape, jnp.sum(acc[...]), F32)

    return pl.pallas_call(
        body, name=name, grid=(bsz, s // ts),
        in_specs=[_row_spec(ts, d), _row_spec(ts, d), _vec_spec(d), _row_spec(ts, d)],
        out_specs=[_row_spec(ts, d), _row_spec(ts, d), _vec_spec(d), pl.BlockSpec((8, LANE), lambda b, i: (0, 0))],
        out_shape=[jax.ShapeDtypeStruct((bsz, s, d), F32), jax.ShapeDtypeStruct((bsz, s, d), BF16),
                   jax.ShapeDtypeStruct((bsz, 1, d), F32), jax.ShapeDtypeStruct((8, LANE), F32)],
        scratch_shapes=[pltpu.VMEM((8, d), F32)], compiler_params=_params(2),
    )(x1, ff, gate2, target)


def _merge_fwd(proj, b_merge, y_a, y_b, *, name, ts=256):
    bsz, s, _ = proj.shape
    ts = min(ts, s)

    def body(la_ref, lb_ref, ba_ref, bb_ref, ya_ref, yb_ref, out_ref):
        ga = _sigmoid(la_ref[...] + ba_ref[...])
        gb = _sigmoid(lb_ref[...] + bb_ref[...])
        out_ref[...] = (ga * ya_ref[...] + gb * yb_ref[...]).astype(BF16)

    return pl.pallas_call(
        body, name=name, grid=(bsz, s // ts),
        in_specs=[_row_spec(ts, D, OFF_MA // D), _row_spec(ts, D, OFF_MB // D),
                  pl.BlockSpec((1, D), lambda b, i: (0, 0)), pl.BlockSpec((1, D), lambda b, i: (0, 1)),
                  _row_spec(ts, D), _row_spec(ts, D)],
        out_specs=_row_spec(ts, D), out_shape=jax.ShapeDtypeStruct((bsz, s, D), BF16),
        compiler_params=_params(2),
    )(proj, proj, b_merge, b_merge, y_a, y_b)


def _merge_bwd(dmi, proj, b_merge, y_a, y_b, *, name, ts=256):
    bsz, s, _ = proj.shape
    ts = min(ts, s)

    def body(d_ref, la_ref, lb_ref, ba_ref, bb_ref, ya_ref, yb_ref, dya_ref, dyb_ref, dla_ref, dlb_ref, dba_ref, dbb_ref):
        @pl.when((pl.program_id(0) == 0) & (pl.program_id(1) == 0))
        def _():
            dba_ref[...] = jnp.zeros_like(dba_ref)
            dbb_ref[...] = jnp.zeros_like(dbb_ref)

        dv = d_ref[...]
        ga = _sigmoid(la_ref[...] + ba_ref[...])
        gb = _sigmoid(lb_ref[...] + bb_ref[...])
        dya_ref[...] = (dv * ga).astype(BF16)
        dyb_ref[...] = (dv * gb).astype(BF16)
        dla = (dv * ya_ref[...]) * (ga * (1.0 - ga))
        dlb = (dv * yb_ref[...]) * (gb * (1.0 - gb))
        dla_ref[...] = dla.astype(BF16)
        dlb_ref[...] = dlb.astype(BF16)
        dba_ref[...] += jnp.sum(dla, axis=0, keepdims=True)
        dbb_ref[...] += jnp.sum(dlb, axis=0, keepdims=True)

    act = jax.ShapeDtypeStruct((bsz, s, D), BF16)
    return pl.pallas_call(
        body, name=name, grid=(bsz, s // ts),
        in_specs=[_row_spec(ts, D), _row_spec(ts, D, OFF_MA // D), _row_spec(ts, D, OFF_MB // D),
                  pl.BlockSpec((1, D), lambda b, i: (0, 0)), pl.BlockSpec((1, D), lambda b, i: (0, 1)),
                  _row_spec(ts, D), _row_spec(ts, D)],
        out_specs=[_row_spec(ts, D)] * 4 + [_gain_spec(D)] * 2,
        out_shape=[act, act, act, act, jax.ShapeDtypeStruct((1, D), F32), jax.ShapeDtypeStruct((1, D), F32)],
        compiler_params=_params(2),
    )(dmi, proj, proj, b_merge, b_merge, y_a, y_b)


def _tri(lower):
    r = lax.broadcasted_iota(jnp.int32, (CHUNK, CHUNK), 0)
    c = lax.broadcasted_iota(jnp.int32, (CHUNK, CHUNK), 1)
    return jnp.where((c <= r) if lower else (c >= r), 1.0, 0.0).astype(F32)


def _gla_logits(a_ref, wal_ref, bal_ref):
    logits = jnp.dot(a_ref[...].astype(BF16), wal_ref[...].astype(BF16), preferred_element_type=F32) + bal_ref[...]
    la = (jnp.minimum(logits, 0.0) - jnp.log(1.0 + jnp.exp(-jnp.abs(logits)))) * (1.0 / GTAU)
    return logits, la


def _chunk_cumsum(la_n, tri):
    cum = jnp.dot(tri, la_n, preferred_element_type=F32, precision=lax.Precision.HIGHEST)
    return cum, jnp.sum(la_n, axis=0, keepdims=True)


def _gla_specs(s, nc):
    def blk(width, off):
        return pl.BlockSpec((None, s, width), lambda h, b: (b, 0, off // width + h))

    proj_specs = [blk(GDK, OFF_Q), blk(GDK, OFF_K), blk(GDV, OFF_V), blk(GDV, OFF_G),
                  pl.BlockSpec((None, s, LANE), lambda h, b: (b, 0, OFF_A // LANE)),
                  pl.BlockSpec((LANE, GDK), lambda h, b: (0, h)), pl.BlockSpec((1, GDK), lambda h, b: (0, h)),
                  pl.BlockSpec((1, GDV), lambda h, b: (0, 0))]
    st_spec = pl.BlockSpec((None, None, nc, GDV, GDK), lambda h, b: (b, h, 0, 0, 0))
    return blk, proj_specs, st_spec


def _gla_fwd(proj, w_alpha_p, b_alpha, out_norm_g, *, name):
    bsz, s, _ = proj.shape
    nc = s // CHUNK
    scale = GDK ** -0.5

    rb = min(512, s)

    def body(q_ref, k_ref, v_ref, g_ref, a_ref, wal_ref, bal_ref, ong_ref, o_ref, og_ref, st_ref):
        _, la = _gla_logits(a_ref, wal_ref, bal_ref)
        tri = _tri(True)
        st = jnp.zeros((GDV, GDK), F32)
        for n in range(nc):
            rows = pl.ds(n * CHUNK, CHUNK)
            cum, cum_end = _chunk_cumsum(la[n * CHUNK:(n + 1) * CHUNK], tri)
            kd = k_ref[rows, :] * jnp.exp(cum_end - cum)
            ut = lax.dot_general(v_ref[rows, :].astype(BF16), kd.astype(BF16), _TN, preferred_element_type=F32)
            st = st * jnp.exp(cum_end) + ut
            st_ref[n] = st
            o_ref[rows, :] = lax.dot_general((q_ref[rows, :] * scale).astype(BF16), st.astype(BF16), _NT,
                                             preferred_element_type=F32)
        for j in range(0, s, rb):
            blk_rows = pl.ds(j, rb)
            oh, _ = _rms(o_ref[blk_rows, :], None)
            gv = g_ref[blk_rows, :]
            og_ref[blk_rows, :] = ((oh * ong_ref[...]) * (gv * _sigmoid(gv))).astype(BF16)

    blk, proj_specs, st_spec = _gla_specs(s, nc)
    return pl.pallas_call(
        body, name=name, grid=(GH, bsz), in_specs=proj_specs, out_specs=[blk(GDV, 0), blk(GDV, 0), st_spec],
        out_shape=[jax.ShapeDtypeStruct((bsz, s, GH * GDV), F32), jax.ShapeDtypeStruct((bsz, s, GH * GDV), BF16),
                   jax.ShapeDtypeStruct((bsz, GH, nc, GDV, GDK), F32)],
        compiler_params=_params(2),
    )(proj, proj, proj, proj, proj, w_alpha_p, b_alpha, out_norm_g)


def _gla_bwd(dog, o, states, proj, w_alpha_p, b_alpha, out_norm_g, *, name):
    bsz, s, _ = proj.shape
    nc = s // CHUNK
    scale = GDK ** -0.5

    def body(dog_ref, o_ref, st_ref, q_ref, k_ref, v_ref, g_ref, a_ref, wal_ref, bal_ref, ong_ref,
             dq_ref, dk_ref, dv_ref, dg_ref, dl_ref, dbal_ref, dong_ref, do_scr, dlog_scr):
        h, b = pl.program_id(0), pl.program_id(1)

        @pl.when(b == 0)
        def _():
            dbal_ref[...] = jnp.zeros_like(dbal_ref)

        @pl.when((b == 0) & (h == 0))
        def _():
            dong_ref[...] = jnp.zeros_like(dong_ref)

        ong = ong_ref[...]
        for j in range(0, s, rb):
            blk_rows = pl.ds(j, rb)
            gv, dogv = g_ref[blk_rows, :], dog_ref[blk_rows, :]
            sg = _sigmoid(gv)
            oh, r = _rms(o_ref[blk_rows, :], None)
            don = dogv * (gv * sg)
            dg_ref[blk_rows, :] = (dogv * (oh * ong) * (sg * (1.0 + gv * (1.0 - sg)))).astype(BF16)
            dong_ref[...] += jnp.sum(don * oh, axis=0, keepdims=True)
            doh = don * ong
            do_scr[blk_rows, :] = (r * (doh - oh * jnp.mean(doh * oh, axis=-1, keepdims=True))).astype(BF16)

        logits, la = _gla_logits(a_ref, wal_ref, bal_ref)
        tri_lo, tri_up = _tri(True), _tri(False)
        carry = jnp.zeros((GDV, GDK), F32)
        for n in range(nc - 1, -1, -1):
            rows = pl.ds(n * CHUNK, CHUNK)
            cum, cum_end = _chunk_cumsum(la[n * CHUNK:(n + 1) * CHUNK], tri_lo)
            decay = jnp.exp(cum_end)
            w = jnp.exp(cum_end - cum)
            kd = k_ref[rows, :] * w
            do_b = do_scr[rows, :]
            qs_b = (q_ref[rows, :] * scale).astype(BF16)
            dq_ref[rows, :] = (jnp.dot(do_b, st_ref[n].astype(BF16), preferred_element_type=F32) * scale).astype(BF16)
            dsn = lax.dot_general(do_b, qs_b, _TN, preferred_element_type=F32) + carry
            carry = dsn * decay
            dsn_b = dsn.astype(BF16)
            dv_ref[rows, :] = lax.dot_general(kd.astype(BF16), dsn_b, _NT, preferred_element_type=F32).astype(BF16)
            dkd = jnp.dot(v_ref[rows, :].astype(BF16), dsn_b, preferred_element_type=F32)
            dk_ref[rows, :] = (dkd * w).astype(BF16)
            e = dkd * kd
            dcum_end = jnp.sum(e, axis=0, keepdims=True)
            if n > 0:
                dcum_end += jnp.sum(dsn * st_ref[n - 1], axis=0, keepdims=True) * decay
            dlog_scr[rows, :] = dcum_end - jnp.dot(tri_up, e, preferred_element_type=F32,
                                                  precision=lax.Precision.HIGHEST)
        dlog = dlog_scr[...] * (1.0 / GTAU) * (1.0 - _sigmoid(logits))
        dl_ref[...] = dlog.astype(BF16)
        dbal_ref[...] += jnp.sum(dlog, axis=0, keepdims=True)

    rb = min(512, s)

    blk, proj_specs, st_spec = _gla_specs(s, nc)
    act = lambda wd: jax.ShapeDtypeStruct((bsz, s, wd), BF16)
    return pl.pallas_call(
        body, name=name, grid=(GH, bsz), in_specs=[blk(GDV, 0), blk(GDV, 0), st_spec, *proj_specs],
        out_specs=[blk(GDK, 0), blk(GDK, 0), blk(GDV, 0), blk(GDV, 0), blk(GDK, 0),
                   pl.BlockSpec((1, GDK), lambda h, b: (0, h)), pl.BlockSpec((1, GDV), lambda h, b: (0, 0))],
        out_shape=[act(GH * GDK), act(GH * GDK), act(GH * GDV), act(GH * GDV), act(GH * GDK),
                   jax.ShapeDtypeStruct((1, GH * GDK), F32), jax.ShapeDtypeStruct((1, GDV), F32)],
        scratch_shapes=[pltpu.VMEM((s, GDV), BF16), pltpu.VMEM((s, GDK), F32)], compiler_params=_params(2),
    )(dog, o, states, proj, proj, proj, proj, proj, w_alpha_p, b_alpha, out_norm_g)


def _lane():
    return lax.broadcasted_iota(jnp.int32, (1, LANE), 1)


def _swap_halves(x):
    lane = _lane()
    half = MROPE // 2
    lo = (lane >= MNOPE) & (lane < MNOPE + half)
    hi = (lane >= MNOPE + half) & (lane < MQK)
    return jnp.where(lo, pltpu.roll(x, LANE - half, 1), jnp.where(hi, pltpu.roll(x, half, 1), 0.0))


def _norm96(x, g):
    r = lax.rsqrt(jnp.sum(x * x, axis=-1, keepdims=True) * (1.0 / MQK) + EPS)
    return x * r, r


def _lat_norm(proj, q_lat_g, kv_lat_g, *, name, ts=512):
    t = proj.shape[0]
    ts = min(ts, t)

    def body(cq_ref, ckv_ref, gq_ref, gk_ref, oq_ref, ok_ref):
        xq, _ = _rms(cq_ref[...], None)
        oq_ref[...] = (xq * gq_ref[...]).astype(BF16)
        xk, _ = _rms(ckv_ref[...], None)
        ok_ref[...] = (xk * gk_ref[...]).astype(BF16)

    return pl.pallas_call(
        body, name=name, grid=(t // ts,),
        in_specs=[pl.BlockSpec((ts, MQR), lambda i: (i, OFF_CQ // MQR)), pl.BlockSpec((ts, MKVR), lambda i: (i, OFF_CKV // MKVR)),
                  pl.BlockSpec((1, MQR), lambda i: (0, 0)), pl.BlockSpec((1, MKVR), lambda i: (0, 0))],
        out_specs=[pl.BlockSpec((ts, MQR), lambda i: (i, 0)), pl.BlockSpec((ts, MKVR), lambda i: (i, 0))],
        out_shape=[jax.ShapeDtypeStruct((t, MQR), BF16), jax.ShapeDtypeStruct((t, MKVR), BF16)],
        compiler_params=_params(1),
    )(proj, proj, q_lat_g, kv_lat_g)


def _lat_norm_bwd(dcqn, dckvn, proj, q_lat_g, kv_lat_g, *, name, ts=512):
    t = proj.shape[0]
    ts = min(ts, t)

    def one(d_ref, x_ref, g_ref, dx_ref, dg_ref):
        xh, r = _rms(x_ref[...], None)
        dn = d_ref[...]
        dg_ref[...] += jnp.sum(dn * xh, axis=0, keepdims=True)
        dxh = dn * g_ref[...]
        dx_ref[...] = (r * (dxh - xh * jnp.mean(dxh * xh, axis=-1, keepdims=True))).astype(BF16)

    def body(dq_ref, dk_ref, cq_ref, ckv_ref, gq_ref, gk_ref, dxq_ref, dxk_ref, dgq_ref, dgk_ref):
        @pl.when(pl.program_id(0) == 0)
        def _():
            dgq_ref[...] = jnp.zeros_like(dgq_ref)
            dgk_ref[...] = jnp.zeros_like(dgk_ref)

        one(dq_ref, cq_ref, gq_ref, dxq_ref, dgq_ref)
        one(dk_ref, ckv_ref, gk_ref, dxk_ref, dgk_ref)

    return pl.pallas_call(
        body, name=name, grid=(t // ts,),
        in_specs=[pl.BlockSpec((ts, MQR), lambda i: (i, 0)), pl.BlockSpec((ts, MKVR), lambda i: (i, 0)),
                  pl.BlockSpec((ts, MQR), lambda i: (i, OFF_CQ // MQR)), pl.BlockSpec((ts, MKVR), lambda i: (i, OFF_CKV // MKVR)),
                  pl.BlockSpec((1, MQR), lambda i: (0, 0)), pl.BlockSpec((1, MKVR), lambda i: (0, 0))],
        out_specs=[pl.BlockSpec((ts, MQR), lambda i: (i, 0)), pl.BlockSpec((ts, MKVR), lambda i: (i, 0)),
                   pl.BlockSpec((1, MQR), lambda i: (0, 0)), pl.BlockSpec((1, MKVR), lambda i: (0, 0))],
        out_shape=[jax.ShapeDtypeStruct((t, MQR), BF16), jax.ShapeDtypeStruct((t, MKVR), BF16),
                   jax.ShapeDtypeStruct((1, MQR), F32), jax.ShapeDtypeStruct((1, MKVR), F32)],
        compiler_params=_params(1),
    )(dcqn, dckvn, proj, proj, q_lat_g, kv_lat_g)


def _qk_prep(q_raw, kv, proj, cos_t, sin_t, gq, gk, *, name, ts=512):
    t = q_raw.shape[0]
    ts = min(ts, t)

    def body(q_ref, kv_ref, kpe_ref, c_ref, s_ref, gq_ref, gk_ref, qo_ref, ko_ref, vo_ref):
        cs, sn = c_ref[...], s_ref[...]
        nope = _lane() < MNOPE
        qn, _ = _norm96(q_ref[...], None)
        qn = qn * gq_ref[...]
        qo_ref[...] = (qn * cs + _swap_halves(qn) * sn).astype(BF16)
        kvv = kv_ref[...]
        kn, _ = _norm96(jnp.where(nope, kvv, kpe_ref[...]), None)
        kn = kn * gk_ref[...]
        ko_ref[...] = (kn * cs + _swap_halves(kn) * sn).astype(BF16)
        vo_ref[...] = jnp.where(nope, pltpu.roll(kvv, MNOPE, 1), 0.0).astype(BF16)

    hd = pl.BlockSpec((ts, LANE), lambda i, h: (i, h))
    shared = lambda col: pl.BlockSpec((ts, LANE), lambda i, h: (i, col))
    gain = pl.BlockSpec((1, LANE), lambda i, h: (0, 0))
    out = jax.ShapeDtypeStruct((t, MH * LANE), BF16)
    return pl.pallas_call(
        body, name=name, grid=(t // ts, MH),
        in_specs=[hd, hd, shared(OFF_KPE // LANE), shared(0), shared(0), gain, gain],
        out_specs=[hd, hd, hd], out_shape=[out, out, out], compiler_params=_params(2),
    )(q_raw, kv, proj, cos_t, sin_t, gq, gk)


def _qk_prep_bwd(dq, dk, dv, q_raw, kv, proj, cos_t, sin_t, gq, gk, *, name, ts=512):
    t = q_raw.shape[0]
    ts = min(ts, t)

    def norm_bwd(dy, x, g, dg_ref):
        xh, r = _norm96(x, None)
        dg_ref[...] += jnp.sum(dy * xh, axis=0, keepdims=True)
        dxh = dy * g
        return r * (dxh - xh * (jnp.sum(dxh * xh, axis=-1, keepdims=True) * (1.0 / MQK)))

    def body(dq_ref, dk_ref, dv_ref, q_ref, kv_ref, kpe_ref, c_ref, s_ref, gq_ref, gk_ref,
             dqr_ref, dkv_ref, dkpe_ref, dgq_ref, dgk_ref):
        i, h = pl.program_id(0), pl.program_id(1)

        @pl.when(h == 0)
        def _():
            dkpe_ref[...] = jnp.zeros_like(dkpe_ref)

        @pl.when((h == 0) & (i == 0))
        def _():
            dgq_ref[...] = jnp.zeros_like(dgq_ref)
            dgk_ref[...] = jnp.zeros_like(dgk_ref)

        cs, sn = c_ref[...], s_ref[...]
        lane = _lane()
        nope = lane < MNOPE
        dqv = dq_ref[...]
        dqn = dqv * cs + _swap_halves(dqv * sn)
        dqr_ref[...] = norm_bwd(dqn, q_ref[...], gq_ref[...], dgq_ref).astype(BF16)
        dkv_ = dk_ref[...]
        dkn = dkv_ * cs + _swap_halves(dkv_ * sn)
        kvv = kv_ref[...]
        dkr = norm_bwd(dkn, jnp.where(nope, kvv, kpe_ref[...]), gk_ref[...], dgk_ref)
        dkv_ref[...] = jnp.where(nope, dkr, pltpu.roll(dv_ref[...], MNOPE, 1)).astype(BF16)
        dkpe_ref[...] += jnp.where((lane >= MNOPE) & (lane < MQK), dkr, 0.0)

    hd = pl.BlockSpec((ts, LANE), lambda i, h: (i, h))
    shared = lambda col: pl.BlockSpec((ts, LANE), lambda i, h: (i, col))
    gain = pl.BlockSpec((1, LANE), lambda i, h: (0, 0))
    out = jax.ShapeDtypeStruct((t, MH * LANE), BF16)
    return pl.pallas_call(
        body, name=name, grid=(t // ts, MH),
        in_specs=[hd, hd, hd, hd, hd, shared(OFF_KPE // LANE), shared(0), shared(0), gain, gain],
        out_specs=[hd, hd, shared(0), gain, gain],
        out_shape=[out, out, jax.ShapeDtypeStruct((t, LANE), F32), jax.ShapeDtypeStruct((1, LANE), F32),
                   jax.ShapeDtypeStruct((1, LANE), F32)],
        compiler_params=_params(2),
    )(dq, dk, dv, q_raw, kv, proj, cos_t, sin_t, gq, gk)


_NT = (((1,), (1,)), ((), ()))
_TN = (((0,), (0,)), ((), ()))


def _attn_probs(q, k_ref, lo, tq):
    scale = MQK ** -0.5
    row = lax.broadcasted_iota(jnp.int32, (tq, tq), 0) // CHUNK
    col = lax.broadcasted_iota(jnp.int32, (tq, tq), 1) // CHUNK
    sd = lax.dot_general(q, k_ref[pl.ds(lo, tq), :], _NT, preferred_element_type=F32) * scale
    sd = jnp.where(col <= row, sd, -1e30)
    m = jnp.max(sd, axis=-1, keepdims=True)
    if lo:
        so = lax.dot_general(q, k_ref[pl.ds(0, lo), :], _NT, preferred_element_type=F32) * scale
        m = jnp.maximum(m, jnp.max(so, axis=-1, keepdims=True))
        po = jnp.exp(so - m)
        pd = jnp.exp(sd - m)
        inv = 1.0 / (jnp.sum(po, axis=-1, keepdims=True) + jnp.sum(pd, axis=-1, keepdims=True))
        return po * inv, pd * inv
    pd = jnp.exp(sd - m)
    return None, pd * (1.0 / jnp.sum(pd, axis=-1, keepdims=True))


def _attn_fwd(q, k, v, *, name, tq=256):
    bsz, s, _ = q.shape
    tq = min(tq, s)

    def body(q_ref, k_ref, v_ref, o_ref):
        for i in range(s // tq):
            lo = i * tq
            po, pd = _attn_probs(q_ref[pl.ds(lo, tq), :], k_ref, lo, tq)
            o = jnp.dot(pd.astype(BF16), v_ref[pl.ds(lo, tq), :], preferred_element_type=F32)
            if lo:
                o += jnp.dot(po.astype(BF16), v_ref[pl.ds(0, lo), :], preferred_element_type=F32)
            o_ref[pl.ds(lo, tq), :] = o.astype(BF16)

    spec = pl.BlockSpec((None, s, LANE), lambda b, h: (b, 0, h))
    return pl.pallas_call(
        body, name=name, grid=(bsz, MH), in_specs=[spec, spec, spec], out_specs=spec,
        out_shape=jax.ShapeDtypeStruct((bsz, s, MH * LANE), BF16), compiler_params=_params(2),
    )(q, k, v)


def _attn_bwd(q, k, v, do, *, name, tq=256):
    bsz, s, _ = q.shape
    tq = min(tq, s)
    scale = MQK ** -0.5

    def body(q_ref, k_ref, v_ref, do_ref, dq_ref, dk_ref, dv_ref):
        dk_ref[...] = jnp.zeros_like(dk_ref)
        dv_ref[...] = jnp.zeros_like(dv_ref)
        for i in range(s // tq):
            lo = i * tq
            here, before = pl.ds(lo, tq), pl.ds(0, lo)
            qv, dov = q_ref[here, :], do_ref[here, :]
            po, pd = _attn_probs(qv, k_ref, lo, tq)
            dv_ref[here, :] += lax.dot_general(pd.astype(BF16), dov, _TN, preferred_element_type=F32)
            dpd = lax.dot_general(dov, v_ref[here, :], _NT, preferred_element_type=F32)
            delta = jnp.sum(dpd * pd, axis=-1, keepdims=True)
            if lo:
                dv_ref[before, :] += lax.dot_general(po.astype(BF16), dov, _TN, preferred_element_type=F32)
                dpo = lax.dot_general(dov, v_ref[before, :], _NT, preferred_element_type=F32)
                delta += jnp.sum(dpo * po, axis=-1, keepdims=True)
            dsd = (pd * (dpd - delta) * scale).astype(BF16)
            dq = jnp.dot(dsd, k_ref[here, :], preferred_element_type=F32)
            dk_ref[here, :] += lax.dot_general(dsd, qv, _TN, preferred_element_type=F32)
            if lo:
                dso = (po * (dpo - delta) * scale).astype(BF16)
                dq += jnp.dot(dso, k_ref[before, :], preferred_element_type=F32)
                dk_ref[before, :] += lax.dot_general(dso, qv, _TN, preferred_element_type=F32)
            dq_ref[here, :] = dq

    spec = pl.BlockSpec((None, s, LANE), lambda b, h: (b, 0, h))
    out = jax.ShapeDtypeStruct((bsz, s, MH * LANE), F32)
    return pl.pallas_call(
        body, name=name, grid=(bsz, MH), in_specs=[spec] * 4, out_specs=[spec] * 3, out_shape=[out, out, out],
        compiler_params=_params(2),
    )(q, k, v, do)


def _adamw(w, g, m, v, *, name, tr=256):
    rows, cols = w.shape
    tr = _tile_rows(rows, tr)

    def body(w_ref, g_ref, m_ref, v_ref, d_ref, nm_ref, nv_ref):
        gv = g_ref[...]
        nm = ADAM_B1 * m_ref[...] + (1.0 - ADAM_B1) * gv
        nv = ADAM_B2 * v_ref[...] + (1.0 - ADAM_B2) * (gv * gv)
        m_hat = nm / (1.0 - ADAM_B1 ** ADAM_STEP)
        v_hat = nv / (1.0 - ADAM_B2 ** ADAM_STEP)
        d_ref[...] = -ADAM_LR * (m_hat / (jnp.sqrt(v_hat) + ADAM_EPS) + ADAM_WD * w_ref[...])
        nm_ref[...] = nm
        nv_ref[...] = nv

    spec = pl.BlockSpec((tr, cols), lambda i: (i, 0))
    out = jax.ShapeDtypeStruct((rows, cols), F32)
    return pl.pallas_call(body, name=name, grid=(rows // tr,), in_specs=[spec] * 4, out_specs=[spec] * 3,
                          out_shape=[out, out, out], compiler_params=_params(1))(w, g, m, v)


def _tile_rows(rows, target):
    if rows <= target:
        return rows
    best = 8
    for t in range(8, target + 1, 8):
        if rows % t == 0:
            best = t
    return best


def _sum_slots(x, *, name, tr=1024):
    n, rows, _ = x.shape
    tr = _tile_rows(rows, tr)

    def body(x_ref, o_ref):
        acc = x_ref[0].astype(F32)
        for j in range(1, n):
            acc = acc + x_ref[j].astype(F32)
        o_ref[...] = acc

    return pl.pallas_call(
        body, name=name, grid=(rows // tr,), in_specs=[pl.BlockSpec((n, tr, LANE), lambda i: (0, i, 0))],
        out_specs=pl.BlockSpec((tr, LANE), lambda i: (i, 0)), out_shape=jax.ShapeDtypeStruct((rows, LANE), F32),
        compiler_params=_params(1))(x)


def _add2(a, b, *, name, out_dtype=F32, tr=1024):
    n, rows, _ = a.shape
    tr = _tile_rows(rows, tr)

    def body(a_ref, b_ref, o_ref):
        o_ref[...] = (a_ref[...] + b_ref[...]).astype(out_dtype)

    spec = pl.BlockSpec((None, tr, LANE), lambda j, i: (j, i, 0))
    return pl.pallas_call(body, name=name, grid=(n, rows // tr), in_specs=[spec, spec], out_specs=spec,
                          out_shape=jax.ShapeDtypeStruct(a.shape, out_dtype), compiler_params=_params(2))(a, b)


def _me():
    return lax.axis_index("x"), lax.axis_index("y"), lax.axis_index("c")


def _flip(pos, bits):
    x, y, c = pos
    return (x ^ bits[0] if bits[0] else x, y ^ bits[1] if bits[1] else y, c ^ bits[2] if bits[2] else c)


ANY = pl.BlockSpec(memory_space=pl.ANY)


def _all_gather8(x, *, name):
    flips = [((k >> 2) & 1, (k >> 1) & 1, k & 1) for k in range(1, 8)]

    def body(x_ref, out_ref, send_sems, recv_sems, local_sem):
        me = _me()
        slot = lambda p: 4 * p[0] + 2 * p[1] + p[2]
        mine = pltpu.make_async_copy(x_ref, out_ref.at[slot(me)], local_sem)
        mine.start()
        sends = []
        for k, f in enumerate(flips):
            cp = pltpu.make_async_remote_copy(src_ref=x_ref, dst_ref=out_ref.at[slot(me)], send_sem=send_sems.at[k],
                                              recv_sem=recv_sems.at[k], device_id=_flip(me, f), device_id_type=MESH)
            cp.start()
            sends.append(cp)
        for k, f in enumerate(flips):
            peer = _flip(me, f)
            pltpu.make_async_remote_copy(src_ref=x_ref, dst_ref=out_ref.at[slot(peer)], send_sem=send_sems.at[k],
                                         recv_sem=recv_sems.at[k], device_id=peer, device_id_type=MESH).wait_recv()
        for cp in sends:
            cp.wait_send()
        mine.wait()

    return pl.pallas_call(
        body, name=name, in_specs=[ANY], out_specs=ANY, out_shape=jax.ShapeDtypeStruct((8, *x.shape), x.dtype),
        scratch_shapes=[pltpu.SemaphoreType.DMA((7,)), pltpu.SemaphoreType.DMA((7,)), pltpu.SemaphoreType.DMA])(x)


CHIP_FLIPS = [(1, 0, 0), (0, 1, 0), (1, 1, 0)]


def _all_gather_chips(x, *, name):
    rows, cols = x.shape
    hr = rows // 2

    def body(x_ref, out_ref, send_sems, recv_sems):
        me = _me()
        sib = _flip(me, (0, 0, 1))
        slot = lambda p: 2 * p[0] + p[1]
        my_half, their_half = me[2], 1 - me[2]

        def copy(k, src, dst, to):
            return pltpu.make_async_remote_copy(src_ref=src, dst_ref=dst, send_sem=send_sems.at[k],
                                                recv_sem=recv_sems.at[k], device_id=to, device_id_type=MESH)

        sends = []
        for k, f in enumerate(CHIP_FLIPS):
            cp = copy(k, x_ref.at[my_half], out_ref.at[slot(me), my_half], _flip(me, f))
            cp.start()
            sends.append(cp)
        for k, f in enumerate(CHIP_FLIPS):
            landed = out_ref.at[slot(_flip(me, f)), my_half]
            copy(k, landed, landed, me).wait_recv()
            cp = copy(3 + k, landed, landed, sib)
            cp.start()
            sends.append(cp)
        for k, f in enumerate(CHIP_FLIPS):
            from_sib = out_ref.at[slot(_flip(me, f)), their_half]
            copy(3 + k, from_sib, from_sib, sib).wait_recv()
        for cp in sends:
            cp.wait_send()

    out = pl.pallas_call(
        body, name=name, in_specs=[ANY], out_specs=ANY, out_shape=jax.ShapeDtypeStruct((4, 2, hr, cols), x.dtype),
        scratch_shapes=[pltpu.SemaphoreType.DMA((6,)), pltpu.SemaphoreType.DMA((6,))],
    )(x.reshape(2, hr, cols)).reshape(4, rows, cols)
    chip = 2 * lax.axis_index("x") + lax.axis_index("y")
    return lax.dynamic_update_slice(out, x[None], (chip, 0, 0))


def _pair_swap_halves(x, *, name):
    n, rows, cols = x.shape
    hr = rows // 2

    def body(x_ref, out_ref, send_sem, recv_sem):
        me = _me()
        sib = _flip(me, (0, 0, 1))
        cp = pltpu.make_async_remote_copy(src_ref=x_ref.at[:, 1 - me[2]], dst_ref=out_ref, send_sem=send_sem,
                                          recv_sem=recv_sem, device_id=sib, device_id_type=MESH)
        cp.start()
        cp.wait()

    return pl.pallas_call(
        body, name=name, in_specs=[ANY], out_specs=ANY, out_shape=jax.ShapeDtypeStruct((n, hr, cols), x.dtype),
        scratch_shapes=[pltpu.SemaphoreType.DMA, pltpu.SemaphoreType.DMA])(x.reshape(n, 2, hr, cols))


def _scatter_chips(p, *, name):
    n, rows, cols = p.shape

    def body(p_ref, out_ref, send_sems, recv_sems):
        me = _me()
        slot = lambda q: 2 * q[0] + q[1]
        sends = []
        for k, f in enumerate(CHIP_FLIPS):
            peer = _flip(me, f)
            cp = pltpu.make_async_remote_copy(src_ref=p_ref.at[slot(peer)], dst_ref=out_ref.at[slot(me)],
                                              send_sem=send_sems.at[k], recv_sem=recv_sems.at[k], device_id=peer,
                                              device_id_type=MESH)
            cp.start()
            sends.append(cp)
        for k, f in enumerate(CHIP_FLIPS):
            peer = _flip(me, f)
            pltpu.make_async_remote_copy(src_ref=p_ref.at[slot(me)], dst_ref=out_ref.at[slot(peer)],
                                         send_sem=send_sems.at[k], recv_sem=recv_sems.at[k], device_id=peer,
                                         device_id_type=MESH).wait_recv()
        for cp in sends:
            cp.wait_send()

    out = pl.pallas_call(
        body, name=name, in_specs=[ANY], out_specs=ANY, out_shape=jax.ShapeDtypeStruct(p.shape, p.dtype),
        scratch_shapes=[pltpu.SemaphoreType.DMA((3,)), pltpu.SemaphoreType.DMA((3,))])(p)
    chip = 2 * lax.axis_index("x") + lax.axis_index("y")
    return lax.dynamic_update_slice(out, lax.dynamic_slice(p, (chip, 0, 0), (1, rows, cols)), (chip, 0, 0))


def _pair_join_halves(h, *, name):
    hr, cols = h.shape

    def body(h_ref, out_ref, send_sem, recv_sem):
        cp = pltpu.make_async_remote_copy(src_ref=h_ref, dst_ref=out_ref, send_sem=send_sem, recv_sem=recv_sem,
                                          device_id=_flip(_me(), (0, 0, 1)), device_id_type=MESH)
        cp.start()
        cp.wait()

    theirs = pl.pallas_call(
        body, name=name, in_specs=[ANY], out_specs=ANY, out_shape=jax.ShapeDtypeStruct((hr, cols), h.dtype),
        scratch_shapes=[pltpu.SemaphoreType.DMA, pltpu.SemaphoreType.DMA])(h)
    south = lax.axis_index("c") == 0
    return jnp.concatenate([jnp.where(south, h, theirs), jnp.where(south, theirs, h)], axis=0)


BIG = (("w_in", (D, IN_WIDTH // 4), 1), ("gla_w_o", (D // 4, D), 0), ("mla_w_uq", (MQR, MH * MQK // 4), 1),
       ("mla_w_ukv", (MKVR, MH * (MNOPE + MVD) // 4), 1), ("mla_w_o", (D // 4, D), 0), ("w_out", (D // 4, D), 0),
       ("mlp_w1", (D, DFF // 4), 1), ("mlp_w2", (DFF // 4, D), 0))
ADA_SHARD = (D, 6 * D // 4)
SMALL = (("b_ada", 6 * D), ("norm1_g", D), ("b_merge", 2 * D), ("gla_b_alpha", GH * GDK), ("gla_out_norm_g", GDV),
         ("mla_q_lat_g", MQR), ("mla_kv_lat_g", MKVR), ("mla_qn_g", MQK), ("mla_kn_g", MQK), ("norm2_g", D))


PACK_ROWS = 2048


def _rows_of(n):
    return -(-n // (8 * LANE)) * 8


def _flat(a):
    v = a.reshape(-1)
    rows = _rows_of(v.shape[0])
    if rows * LANE != v.shape[0]:
        v = jnp.pad(v, (0, rows * LANE - v.shape[0]))
    return v.reshape(rows, LANE)


def _pack(arrs, multiple=1):
    parts = [_flat(a) for a in arrs]
    rows = sum(p.shape[0] for p in parts)
    if rows % multiple:
        parts.append(jnp.zeros((multiple - rows % multiple, LANE), parts[0].dtype))
    return jnp.concatenate(parts, axis=0)


def _unpack(flat, shapes):
    out, r = [], 0
    for shp in shapes:
        n = int(np.prod(shp))
        rows = _rows_of(n)
        out.append(flat[r:r + rows].reshape(-1)[:n].reshape(shp))
        r += rows
    return out


def _full_weights(slots):
    per_chip = [_unpack(slots[j], [shp for _, shp, _ in BIG]) for j in range(4)]
    w = {name: jnp.concatenate([per_chip[j][i] for j in range(4)], axis=axis) for i, (name, _, axis) in enumerate(BIG)}
    wi = w["w_in"]
    zeros = lambda n: jnp.zeros((D, n), wi.dtype)
    w["w_in"] = jnp.concatenate(
        [wi[:, :3072], wi[:, 3504:5552], wi[:, 3088:3344], wi[:, 3344:3472], wi[:, 3072:3088], zeros(LANE - GLR),
         zeros(MNOPE), wi[:, 3472:3504], zeros(LANE - MQK)], axis=1)
    w["mla_w_uq"] = jnp.pad(w["mla_w_uq"].reshape(MQR, MH, MQK), ((0, 0), (0, 0), (0, LANE - MQK))).reshape(MQR, MH * LANE)
    w["mla_w_o"] = jnp.pad(w["mla_w_o"].reshape(MH, MVD, D), ((0, 0), (0, LANE - MVD), (0, 0))).reshape(MH * LANE, D)
    return w


def _grad_slots(g):
    gi = g["w_in"]
    g = dict(g)
    g["w_in"] = jnp.concatenate(
        [gi[:, :3072], gi[:, OFF_A:OFF_A + GLR], gi[:, OFF_CQ:OFF_CQ + MQR], gi[:, OFF_CKV:OFF_CKV + MKVR],
         gi[:, OFF_KPE + MNOPE:OFF_KPE + MQK], gi[:, OFF_MA:OFF_MA + 2 * D]], axis=1)
    g["mla_w_uq"] = g["mla_w_uq"].reshape(MQR, MH, LANE)[:, :, :MQK].reshape(MQR, MH * MQK)
    g["mla_w_o"] = g["mla_w_o"].reshape(MH, LANE, D)[:, :MVD].reshape(MH * MVD, D)
    slots = []
    for j in range(4):
        parts = []
        for name, shp, axis in BIG:
            n = shp[axis]
            parts.append(lax.slice_in_dim(g[name], j * n, (j + 1) * n, axis=axis))
        slots.append(_pack(parts, multiple=PACK_ROWS))
    return jnp.stack(slots)


def _rope_tables(positions):
    freqs = ROPE_THETA ** (-jnp.arange(0, MROPE, 2, dtype=F32) / MROPE)
    ang = positions.astype(F32)[..., None] * freqs
    cos, sin = jnp.cos(ang), jnp.sin(ang)
    shape = ang.shape[:-1]
    cos_t = jnp.concatenate([jnp.ones(shape + (MNOPE,), F32), cos, cos, jnp.ones(shape + (LANE - MQK,), F32)], axis=-1)
    sin_t = jnp.concatenate([jnp.zeros(shape + (MNOPE,), F32), -sin, sin, jnp.zeros(shape + (LANE - MQK,), F32)], axis=-1)
    return cos_t.reshape(-1, LANE), sin_t.reshape(-1, LANE)


def _local_step(x, positions, mod, target, w, small):
    bsz, s, _ = x.shape
    t = bsz * s
    tt = _tile(t, 1024)
    shift1, scale1, gate1, shift2, scale2, gate2 = [mod[:, None, i * D:(i + 1) * D] for i in range(6)]
    cos_t, sin_t = _rope_tables(positions)
    w_alpha_p = jnp.pad(small["gla_w_alpha"], ((0, LANE - GLR), (0, 0)))
    gq = jnp.pad(small["mla_qn_g"], ((0, 0), (0, LANE - MQK)))
    gk = jnp.pad(small["mla_kn_g"], ((0, 0), (0, LANE - MQK)))
    flat2 = lambda a: a.reshape(t, a.shape[-1])
    bsd = lambda a: a.reshape(bsz, s, a.shape[-1])

    h = _norm_mod(x, small["norm1_g"], scale1, shift1, name="norm1")
    proj = _mm(flat2(h), w["w_in"], name="proj", tn=1152)
    proj3 = bsd(proj)
    o, o_gated, states = _gla_fwd(proj3, w_alpha_p, small["gla_b_alpha"], small["gla_out_norm_g"], name="gla_fwd")
    y_a = _mm(flat2(o_gated), w["gla_w_o"], name="gla_out")
    cq_n, ckv_n = _lat_norm(proj, small["mla_q_lat_g"], small["mla_kv_lat_g"], name="lat_norm")
    q_raw = _mm(cq_n, w["mla_w_uq"], name="mla_uq")
    kv = _mm(ckv_n, w["mla_w_ukv"], name="mla_ukv")
    qf, kf, vf = _qk_prep(q_raw, kv, proj, cos_t, sin_t, gq, gk, name="qk_prep")
    o_attn = _attn_fwd(bsd(qf), bsd(kf), bsd(vf), name="attn_fwd")
    y_b = _mm(flat2(o_attn), w["mla_w_o"], name="mla_out")
    mixed_in = _merge_fwd(proj3, small["b_merge"], bsd(y_a), bsd(y_b), name="merge_fwd")
    mixed = _mm(flat2(mixed_in), w["w_out"], name="w_out")
    x1, h2 = _resid_norm_mod(x, bsd(mixed), gate1, small["norm2_g"], scale2, shift2, name="norm2")

    def sqrelu(acc, ex, outs):
        outs[0][...] = acc
        r = jnp.maximum(acc, 0.0)
        outs[1][...] = (r * r).astype(BF16)

    a1, r = _mm(flat2(h2), w["mlp_w1"], name="mlp1", epilogue=sqrelu,
                out_shape=[jax.ShapeDtypeStruct((t, DFF), F32), jax.ShapeDtypeStruct((t, DFF), BF16)],
                out_specs=[_tile_spec(tt, 1024), _tile_spec(tt, 1024)])
    ff = _mm(r, w["mlp_w2"], name="mlp2")
    dy, dff, dgate2, loss_part = _loss_head(x1, bsd(ff), gate2, target, name="loss_head")

    g = {}

    def relu2_bwd(acc, ex, outs):
        outs[0][...] = (acc * (2.0 * jnp.maximum(ex[0][...], 0.0))).astype(BF16)

    dff2 = flat2(dff)
    da1 = _mm(dff2, w["mlp_w2"], tb=True, name="mlp2_dx", epilogue=relu2_bwd, extras=(a1,),
              extra_specs=(_tile_spec(tt, 1024),), out_shape=jax.ShapeDtypeStruct((t, DFF), BF16),
              out_specs=_tile_spec(tt, 1024))
    g["mlp_w2"] = _mm(r, dff2, ta=True, name="mlp2_dw")
    dh2 = _mm(da1, w["mlp_w1"], tb=True, name="mlp1_dx")
    g["mlp_w1"] = _mm(flat2(h2), da1, ta=True, name="mlp1_dw")
    dx1, dscale2, dshift2, dg2, dgate1, dmixed = _norm_mod_bwd(
        bsd(dh2), x1, dy, small["norm2_g"], scale2, gate1, bsd(mixed), name="norm2_bwd")
    dmixed2 = flat2(dmixed)
    dmi = _mm(dmixed2, w["w_out"], tb=True, name="w_out_dx")
    g["w_out"] = _mm(flat2(mixed_in), dmixed2, ta=True, name="w_out_dw")
    dy_a, dy_b, dl_a, dl_b, db_a, db_b = _merge_bwd(bsd(dmi), proj3, small["b_merge"], bsd(y_a), bsd(y_b), name="merge_bwd")
    dy_a2, dy_b2 = flat2(dy_a), flat2(dy_b)
    dog = _mm(dy_a2, w["gla_w_o"], tb=True, name="gla_out_dx")
    g["gla_w_o"] = _mm(flat2(o_gated), dy_a2, ta=True, name="gla_out_dw")
    dq_g, dk_g, dv_g, dg_g, dlog, db_alpha, d_ong = _gla_bwd(
        bsd(dog), o, states, proj3, w_alpha_p, small["gla_b_alpha"], small["gla_out_norm_g"], name="gla_bwd")
    dlog2 = flat2(dlog)
    da_p = _mm(dlog2, w_alpha_p, tb=True, out_dtype=BF16, name="alpha_dx")
    d_w_alpha = _mm(proj[:, OFF_A:OFF_A + LANE], dlog2, ta=True, name="alpha_dw")[:GLR]
    do_attn = _mm(dy_b2, w["mla_w_o"], tb=True, out_dtype=BF16, name="mla_out_dx")
    g["mla_w_o"] = _mm(flat2(o_attn), dy_b2, ta=True, name="mla_out_dw")
    dqf, dkf, dvf = _attn_bwd(bsd(qf), bsd(kf), bsd(vf), bsd(do_attn), name="attn_bwd")
    dq_raw, dkv, dkpe, dgq, dgk = _qk_prep_bwd(flat2(dqf), flat2(dkf), flat2(dvf), q_raw, kv, proj, cos_t, sin_t, gq, gk,
                                                name="qk_prep_bwd")
    dcq_n = _mm(dq_raw, w["mla_w_uq"], tb=True, name="mla_uq_dx")
    g["mla_w_uq"] = _mm(cq_n, dq_raw, ta=True, name="mla_uq_dw")
    dckv_n = _mm(dkv, w["mla_w_ukv"], tb=True, name="mla_ukv_dx")
    g["mla_w_ukv"] = _mm(ckv_n, dkv, ta=True, name="mla_ukv_dw")
    dcq, dckv, dg_qlat, dg_kvlat = _lat_norm_bwd(dcq_n, dckv_n, proj, small["mla_q_lat_g"], small["mla_kv_lat_g"],
                                                  name="lat_norm_bwd")
    dproj = jnp.concatenate([flat2(dq_g), flat2(dk_g), flat2(dv_g), flat2(dg_g), flat2(dl_a), flat2(dl_b), dcq, dckv,
                             da_p, dkpe.astype(BF16)], axis=1)
    dh = _mm(dproj, w["w_in"], tb=True, name="proj_dx", tk=1152)
    g["w_in"] = _mm(flat2(h), dproj, ta=True, name="proj_dw", tn=1152)
    grad_x, dscale1, dshift1, dg1 = _norm_mod_bwd(bsd(dh), x, dx1, small["norm1_g"], scale1, name="norm1_bwd")

    dmod = jnp.concatenate([dshift1, dscale1, dgate1, dshift2, dscale2, dgate2], axis=-1).reshape(bsz, 6 * D)
    gs = {"norm1_g": dg1, "b_merge": jnp.concatenate([db_a, db_b], axis=1), "gla_b_alpha": db_alpha,
          "gla_out_norm_g": d_ong, "mla_q_lat_g": dg_qlat, "mla_kv_lat_g": dg_kvlat, "mla_qn_g": dgq[:, :MQK],
          "mla_kn_g": dgk[:, :MQK], "norm2_g": dg2}
    return loss_part[0, 0], grad_x, dmod, g, gs, d_w_alpha


def kernel(x, c, positions, w_ada, b_ada, norm1_g, w_in, b_merge, gla_w_alpha, gla_b_alpha, gla_out_norm_g, gla_w_o, mla_q_lat_g, mla_w_uq, mla_kv_lat_g, mla_w_ukv, mla_qn_g, mla_kn_g, mla_w_o, w_out, norm2_g, mlp_w1, mlp_w2, loss_target, m_w_ada, m_b_ada, m_norm1_g, m_w_in, m_b_merge, m_gla_w_alpha, m_gla_b_alpha, m_gla_out_norm_g, m_gla_w_o, m_mla_q_lat_g, m_mla_w_uq, m_mla_kv_lat_g, m_mla_w_ukv, m_mla_qn_g, m_mla_kn_g, m_mla_w_o, m_w_out, m_norm2_g, m_mlp_w1, m_mlp_w2, v_w_ada, v_b_ada, v_norm1_g, v_w_in, v_b_merge, v_gla_w_alpha, v_gla_b_alpha, v_gla_out_norm_g, v_gla_w_o, v_mla_q_lat_g, v_mla_w_uq, v_mla_kv_lat_g, v_mla_w_ukv, v_mla_qn_g, v_mla_kn_g, v_mla_w_o, v_w_out, v_norm2_g, v_mlp_w1, v_mlp_w2):
    args = dict(locals())
    names_big = [n for n, _, _ in BIG]
    names_small = [n for n, _ in SMALL]
    bsz = x.shape[0]
    ax, ay, ac = lax.axis_index("x"), lax.axis_index("y"), lax.axis_index("c")
    chip = 2 * ax + ay
    dev = 2 * chip + ac

    wp = _pack([args[n][0] for n in names_big], multiple=PACK_ROWS)
    slots = _all_gather_chips(wp.astype(BF16), name="comm_weights")
    w = _full_weights(slots)
    small = {n: args[n] for n in names_small}
    w_alpha_all = _all_gather8(gla_w_alpha[0], name="comm_w_alpha")
    small["gla_w_alpha"] = jnp.concatenate([w_alpha_all[2 * j] for j in range(4)], axis=1)

    c_all = _all_gather8(c, name="comm_c").reshape(8 * bsz, D)

    def add_bias(acc, ex, outs):
        outs[0][...] = acc + ex[0][...]

    silu = lambda v: v * _sigmoid(v)
    b_ada_mine = lax.dynamic_slice(b_ada, (0, chip * ADA_SHARD[1]), (1, ADA_SHARD[1]))
    mod_part = _mm(c_all, w_ada[0], name="ada", tn=512, a_fn=silu, epilogue=add_bias, extras=(b_ada_mine,),
                   extra_specs=(pl.BlockSpec((1, 512), lambda i, j, k: (0, j)),),
                   out_shape=jax.ShapeDtypeStruct((8 * bsz, ADA_SHARD[1]), F32), out_specs=_tile_spec(8 * bsz, 512))
    mod_all = _all_gather8(mod_part, name="comm_mod")
    mod_rows = lax.dynamic_slice(mod_all, (0, dev * bsz, 0), (8, bsz, ADA_SHARD[1]))
    mod = jnp.concatenate([mod_rows[2 * j] for j in range(4)], axis=1)

    loss_part, grad_x, dmod, g, gs, d_w_alpha = _local_step(x, positions, mod, loss_target, w, small)
    loss = lax.psum(loss_part * (0.5 / D), ("x", "y", "c"))

    dmod_all = _all_gather8(dmod, name="comm_dmod").reshape(8 * bsz, 6 * D)
    dmod_mine = lax.dynamic_slice(dmod_all, (0, chip * ADA_SHARD[1]), (8 * bsz, ADA_SHARD[1]))
    g_w_ada = _mm(c_all, dmod_mine, ta=True, a_fn=silu, name="ada_dw")
    g_b_ada = _sum_slots(dmod_all.reshape(8 * bsz, 6 * D // LANE, LANE), name="sum_b_ada", tr=6 * D // LANE)

    names_red = [n for n in names_small if n != "b_ada"]
    red_shapes = [(1, n) for name, n in SMALL if name != "b_ada"] + [(GLR, GH * GDK)]
    gs_packed = _pack([gs[n] for n in names_red] + [d_w_alpha], multiple=8)
    gs_sum = _sum_slots(_all_gather8(gs_packed, name="comm_small"), name="sum_small")
    gs_full = _unpack(gs_sum, red_shapes)
    g_small = dict(zip(names_red, gs_full[:-1]))
    g_small["b_ada"] = g_b_ada.reshape(1, 6 * D)
    g_w_alpha = lax.dynamic_slice(gs_full[-1], (0, chip * GDK), (GLR, GDK))

    gslots = _grad_slots(g)
    rows = gslots.shape[1]
    hr = rows // 2
    sib_half = _pair_swap_halves(gslots, name="comm_pair_sum")
    my_half = lax.dynamic_slice(gslots, (0, ac * hr, 0), (4, hr, LANE))
    pair = _add2(my_half, sib_half, name="pair_add", out_dtype=BF16)
    from_chips = _scatter_chips(pair, name="comm_scatter")
    g_half = _sum_slots(from_chips, name="chip_sum")
    g_big = _pair_join_halves(g_half, name="comm_pair_join")

    res = {}
    g_mats = dict(zip(names_big, _unpack(g_big, [shp for _, shp, _ in BIG])))
    g_mats["w_ada"] = g_w_ada
    for n, gm in g_mats.items():
        res[n] = (gm, *_adamw(args[n][0], gm, args["m_" + n][0], args["v_" + n][0], name="adamw_" + n))
    sm_names = names_small + ["gla_w_alpha"]
    sm_shapes = [(1, n) for _, n in SMALL] + [(GLR, GDK)]
    g_sm = [g_small[n] for n in names_small] + [g_w_alpha]
    pack_sm = lambda prefix: _pack([args[prefix + n].reshape(shp) for n, shp in zip(sm_names, sm_shapes)], multiple=8)
    outs_sm = _adamw(pack_sm(""), _pack(g_sm, multiple=8), pack_sm("m_"), pack_sm("v_"), name="adamw_small")
    for n, gm, *rest in zip(sm_names, g_sm, *[_unpack(o, sm_shapes) for o in outs_sm]):
        res[n] = (gm, *rest)

    order = ["w_ada", "b_ada", "norm1_g", "w_in", "b_merge", "gla_w_alpha", "gla_b_alpha", "gla_out_norm_g", "gla_w_o",
             "mla_q_lat_g", "mla_w_uq", "mla_kv_lat_g", "mla_w_ukv", "mla_qn_g", "mla_kn_g", "mla_w_o", "w_out",
             "norm2_g", "mlp_w1", "mlp_w2"]
    named = lambda k: [res[n][k].reshape(args[n].shape) for n in order]
    return (loss, grad_x, *named(0), *named(1), *named(2), *named(3))
```

```python
import functools

import jax
import jax.numpy as jnp
import numpy as np
from jax import lax
from jax.experimental import pallas as pl
from jax.experimental.pallas import tpu as pltpu

F32 = jnp.float32
BF16 = jnp.bfloat16
MESH = pl.DeviceIdType.MESH

D = 1024
CHUNK = 64
EPS = 1e-6
GH, GDK, GDV, GLR, GTAU = 4, 128, 256, 16, 16.0
MH, MQR, MKVR, MNOPE, MROPE, MVD = 16, 256, 128, 64, 32, 64
MQK = MNOPE + MROPE
DFF = 4 * D
ROPE_THETA = 10000.0
IN_WIDTH = 5552
LANE = 128
OFF_Q, OFF_K, OFF_V, OFF_G, OFF_MA, OFF_MB, OFF_CQ, OFF_CKV, OFF_A, OFF_KPE, PW = (
    0, 512, 1024, 2048, 3072, 4096, 5120, 5376, 5504, 5632, 5760)
ADAM_LR, ADAM_B1, ADAM_B2, ADAM_EPS, ADAM_WD, ADAM_STEP = 0.001, 0.9, 0.999, 1e-08, 0.01, 10
VMEM_LIMIT = 48 * 1024 * 1024


def _params(n_axes):
    return pltpu.CompilerParams(dimension_semantics=("arbitrary",) * n_axes, vmem_limit_bytes=VMEM_LIMIT)


def _tile(n, target):
    if n <= target:
        return n
    best = None
    for t in range(LANE, target + 1, LANE):
        if n % t == 0:
            best = t
    assert best is not None, (n, target)
    return best


def _sigmoid(x):
    return 1.0 / (1.0 + jnp.exp(-x))


def _mm(a, b, *, name, ta=False, tb=False, out_dtype=F32, tm=1024, tn=1024, tk=1024,
        epilogue=None, extras=(), extra_specs=(), out_shape=None, out_specs=None, a_fn=None):
    if ta:
        kdim, m = a.shape
    else:
        m, kdim = a.shape
    if tb:
        n, k2 = b.shape
    else:
        k2, n = b.shape
    assert kdim == k2, (a.shape, b.shape)
    tm, tn, tk = _tile(m, tm), _tile(n, tn), _tile(kdim, tk)
    nk = kdim // tk
    a_spec = pl.BlockSpec((tk, tm), lambda i, j, k: (k, i)) if ta else pl.BlockSpec((tm, tk), lambda i, j, k: (i, k))
    b_spec = pl.BlockSpec((tn, tk), lambda i, j, k: (j, k)) if tb else pl.BlockSpec((tk, tn), lambda i, j, k: (k, j))
    dims = (((0 if ta else 1,), (1 if tb else 0,)), ((), ()))
    ne = len(extras)
    if out_shape is None:
        out_shape = jax.ShapeDtypeStruct((m, n), out_dtype)
        out_specs = pl.BlockSpec((tm, tn), lambda i, j, k: (i, j))

    def body(a_ref, b_ref, *rest):
        ex, outs, acc = rest[:ne], rest[ne:-1], rest[-1]
        k = pl.program_id(2)

        @pl.when(k == 0)
        def _():
            acc[...] = jnp.zeros_like(acc)

        av = a_ref[...] if a_fn is None else a_fn(a_ref[...])
        acc[...] += lax.dot_general(av.astype(BF16), b_ref[...].astype(BF16), dims, preferred_element_type=F32)

        @pl.when(k == nk - 1)
        def _():
            if epilogue is None:
                outs[0][...] = acc[...].astype(outs[0].dtype)
            else:
                epilogue(acc[...], ex, outs)

    return pl.pallas_call(
        body, name=name, grid=(m // tm, n // tn, nk),
        in_specs=[a_spec, b_spec, *extra_specs], out_specs=out_specs, out_shape=out_shape,
        scratch_shapes=[pltpu.VMEM((tm, tn), F32)], compiler_params=_params(3),
    )(a, b, *extras)


def _tile_spec(tm, tn):
    return pl.BlockSpec((tm, tn), lambda i, j, k: (i, j))


def _rms(x, g):
    r = lax.rsqrt(jnp.mean(x * x, axis=-1, keepdims=True) + EPS)
    return x * r, r


def _row_spec(ts, width, col=0):
    return pl.BlockSpec((None, ts, width), lambda b, i: (b, i, col))


def _vec_spec(width):
    return pl.BlockSpec((None, 1, width), lambda b, i: (b, 0, 0))


def _gain_spec(width):
    return pl.BlockSpec((1, width), lambda b, i: (0, 0))


def _norm_mod(x, g, scale, shift, *, name, ts=256):
    bsz, s, d = x.shape
    ts = min(ts, s)

    def body(x_ref, g_ref, sc_ref, sh_ref, h_ref):
        xh, _ = _rms(x_ref[...], None)
        h_ref[...] = ((xh * g_ref[...]) * (1.0 + sc_ref[...]) + sh_ref[...]).astype(BF16)

    return pl.pallas_call(
        body, name=name, grid=(bsz, s // ts),
        in_specs=[_row_spec(ts, d), _gain_spec(d), _vec_spec(d), _vec_spec(d)],
        out_specs=_row_spec(ts, d), out_shape=jax.ShapeDtypeStruct((bsz, s, d), BF16),
        compiler_params=_params(2),
    )(x, g, scale, shift)


def _resid_norm_mod(x, mixed, gate, g, scale, shift, *, name, ts=256):
    bsz, s, d = x.shape
    ts = min(ts, s)

    def body(x_ref, mx_ref, gt_ref, g_ref, sc_ref, sh_ref, x1_ref, h_ref):
        x1 = x_ref[...] + gt_ref[...] * mx_ref[...]
        x1_ref[...] = x1
        xh, _ = _rms(x1, None)
        h_ref[...] = ((xh * g_ref[...]) * (1.0 + sc_ref[...]) + sh_ref[...]).astype(BF16)

    return pl.pallas_call(
        body, name=name, grid=(bsz, s // ts),
        in_specs=[_row_spec(ts, d), _row_spec(ts, d), _vec_spec(d), _gain_spec(d), _vec_spec(d), _vec_spec(d)],
        out_specs=[_row_spec(ts, d), _row_spec(ts, d)],
        out_shape=[jax.ShapeDtypeStruct((bsz, s, d), F32), jax.ShapeDtypeStruct((bsz, s, d), BF16)],
        compiler_params=_params(2),
    )(x, mixed, gate, g, scale, shift)


def _norm_mod_bwd(dh, xin, resid, g, scale, gate=None, mixed=None, *, name, ts=256):
    bsz, s, d = xin.shape
    ts = min(ts, s)
    gated = gate is not None

    def body(*refs):
        if gated:
            dh_ref, x_ref, rs_ref, g_ref, sc_ref, gt_ref, mx_ref, dx_ref, dsc_ref, dsh_ref, dg_ref, dgt_ref, dmx_ref = refs
        else:
            dh_ref, x_ref, rs_ref, g_ref, sc_ref, dx_ref, dsc_ref, dsh_ref, dg_ref = refs
        b, i = pl.program_id(0), pl.program_id(1)

        @pl.when(i == 0)
        def _():
            dsc_ref[...] = jnp.zeros_like(dsc_ref)
            dsh_ref[...] = jnp.zeros_like(dsh_ref)
            if gated:
                dgt_ref[...] = jnp.zeros_like(dgt_ref)

        @pl.when((i == 0) & (b == 0))
        def _():
            dg_ref[...] = jnp.zeros_like(dg_ref)

        dh_v, gv = dh_ref[...], g_ref[...]
        xh, r = _rms(x_ref[...], None)
        dsc_ref[...] += jnp.sum(dh_v * (xh * gv), axis=0, keepdims=True)
        dsh_ref[...] += jnp.sum(dh_v, axis=0, keepdims=True)
        dn = dh_v * (1.0 + sc_ref[...])
        dg_ref[...] += jnp.sum(dn * xh, axis=0, keepdims=True)
        dxh = dn * gv
        dx = rs_ref[...] + r * (dxh - xh * jnp.mean(dxh * xh, axis=-1, keepdims=True))
        dx_ref[...] = dx
        if gated:
            dgt_ref[...] += jnp.sum(dx * mx_ref[...], axis=0, keepdims=True)
            dmx_ref[...] = (dx * gt_ref[...]).astype(BF16)

    ins = [dh, xin, resid, g, scale]
    in_specs = [_row_spec(ts, d), _row_spec(ts, d), _row_spec(ts, d), _gain_spec(d), _vec_spec(d)]
    out_specs = [_row_spec(ts, d), _vec_spec(d), _vec_spec(d), _gain_spec(d)]
    out_shape = [jax.ShapeDtypeStruct((bsz, s, d), F32), jax.ShapeDtypeStruct((bsz, 1, d), F32),
                 jax.ShapeDtypeStruct((bsz, 1, d), F32), jax.ShapeDtypeStruct((1, d), F32)]
    if gated:
        ins += [gate, mixed]
        in_specs += [_vec_spec(d), _row_spec(ts, d)]
        out_specs += [_vec_spec(d), _row_spec(ts, d)]
        out_shape += [jax.ShapeDtypeStruct((bsz, 1, d), F32), jax.ShapeDtypeStruct((bsz, s, d), BF16)]
    return pl.pallas_call(
        body, name=name, grid=(bsz, s // ts), in_specs=in_specs, out_specs=out_specs, out_shape=out_shape,
        compiler_params=_params(2),
    )(*ins)


def _loss_head(x1, ff, gate2, target, *, name, ts=256):
    bsz, s, d = x1.shape
    ts = min(ts, s)

    def body(x1_ref, ff_ref, gt_ref, t_ref, dy_ref, dff_ref, dgt_ref, loss_ref, acc):
        b, i = pl.program_id(0), pl.program_id(1)

        @pl.when(i == 0)
        def _():
            dgt_ref[...] = jnp.zeros_like(dgt_ref)

        @pl.when((i == 0) & (b == 0))
        def _():
            acc[...] = jnp.zeros_like(acc)

        ffv, gt = ff_ref[...], gt_ref[...]
        diff = (x1_ref[...] + gt * ffv) - t_ref[...]
        acc[...] += jnp.sum((diff * diff).reshape(ts // 8, 8, d), axis=0)
        dy = diff * (1.0 / d)
        dy_ref[...] = dy
        dgt_ref[...] += jnp.sum(dy * ffv, axis=0, keepdims=True)
        dff_ref[...] = (dy * gt).astype(BF16)

        @pl.when((i == pl.num_programs(1) - 1) & (b == pl.num_programs(0) - 1))
        def _():
            loss_ref[...] = jnp.full(loss_ref.shape, jnp.sum(acc[...]), F32)

    return pl.pallas_call(
        body, name=name, grid=(bsz, s // ts),
        in_specs=[_row_spec(ts, d), _row_spec(ts, d), _vec_spec(d), _row_spec(ts, d)],
        out_specs=[_row_spec(ts, d), _row_spec(ts, d), _vec_spec(d), pl.BlockSpec((8, LANE), lambda b, i: (0, 0))],
        out_shape=[jax.ShapeDtypeStruct((bsz, s, d), F32), jax.ShapeDtypeStruct((bsz, s, d), BF16),
                   jax.ShapeDtypeStruct((bsz, 1, d), F32), jax.ShapeDtypeStruct((8, LANE), F32)],
        scratch_shapes=[pltpu.VMEM((8, d), F32)], compiler_params=_params(2),
    )(x1, ff, gate2, target)


def _merge_fwd(proj, b_merge, y_a, y_b, *, name, ts=256):
    bsz, s, _ = proj.shape
    ts = min(ts, s)

    def body(la_ref, lb_ref, ba_ref, bb_ref, ya_ref, yb_ref, out_ref):
        ga = _sigmoid(la_ref[...] + ba_ref[...])
        gb = _sigmoid(lb_ref[...] + bb_ref[...])
        out_ref[...] = (ga * ya_ref[...] + gb * yb_ref[...]).astype(BF16)

    return pl.pallas_call(
        body, name=name, grid=(bsz, s // ts),
        in_specs=[_row_spec(ts, D, OFF_MA // D), _row_spec(ts, D, OFF_MB // D),
                  pl.BlockSpec((1, D), lambda b, i: (0, 0)), pl.BlockSpec((1, D), lambda b, i: (0, 1)),
                  _row_spec(ts, D), _row_spec(ts, D)],
        out_specs=_row_spec(ts, D), out_shape=jax.ShapeDtypeStruct((bsz, s, D), BF16),
        compiler_params=_params(2),
    )(proj, proj, b_merge, b_merge, y_a, y_b)


def _merge_bwd(dmi, proj, b_merge, y_a, y_b, *, name, ts=256):
    bsz, s, _ = proj.shape
    ts = min(ts, s)

    def body(d_ref, la_ref, lb_ref, ba_ref, bb_ref, ya_ref, yb_ref, dya_ref, dyb_ref, dla_ref, dlb_ref, dba_ref, dbb_ref):
        @pl.when((pl.program_id(0) == 0) & (pl.program_id(1) == 0))
        def _():
            dba_ref[...] = jnp.zeros_like(dba_ref)
            dbb_ref[...] = jnp.zeros_like(dbb_ref)

        dv = d_ref[...]
        ga = _sigmoid(la_ref[...] + ba_ref[...])
        gb = _sigmoid(lb_ref[...] + bb_ref[...])
        dya_ref[...] = (dv * ga).astype(BF16)
        dyb_ref[...] = (dv * gb).astype(BF16)
        dla = (dv * ya_ref[...]) * (ga * (1.0 - ga))
        dlb = (dv * yb_ref[...]) * (gb * (1.0 - gb))
        dla_ref[...] = dla.astype(BF16)
        dlb_ref[...] = dlb.astype(BF16)
        dba_ref[...] += jnp.sum(dla, axis=0, keepdims=True)
        dbb_ref[...] += jnp.sum(dlb, axis=0, keepdims=True)

    act = jax.ShapeDtypeStruct((bsz, s, D), BF16)
    return pl.pallas_call(
        body, name=name, grid=(bsz, s // ts),
        in_specs=[_row_spec(ts, D), _row_spec(ts, D, OFF_MA // D), _row_spec(ts, D, OFF_MB // D),
                  pl.BlockSpec((1, D), lambda b, i: (0, 0)), pl.BlockSpec((1, D), lambda b, i: (0, 1)),
                  _row_spec(ts, D), _row_spec(ts, D)],
        out_specs=[_row_spec(ts, D)] * 4 + [_gain_spec(D)] * 2,
        out_shape=[act, act, act, act, jax.ShapeDtypeStruct((1, D), F32), jax.ShapeDtypeStruct((1, D), F32)],
        compiler_params=_params(2),
    )(dmi, proj, proj, b_merge, b_merge, y_a, y_b)


def _tri(lower):
    r = lax.broadcasted_iota(jnp.int32, (CHUNK, CHUNK), 0)
    c = lax.broadcasted_iota(jnp.int32, (CHUNK, CHUNK), 1)
    return jnp.where((c <= r) if lower else (c >= r), 1.0, 0.0).astype(F32)


def _gla_logits(a_ref, wal_ref, bal_ref):
    logits = jnp.dot(a_ref[...].astype(BF16), wal_ref[...].astype(BF16), preferred_element_type=F32) + bal_ref[...]
    la = (jnp.minimum(logits, 0.0) - jnp.log(1.0 + jnp.exp(-jnp.abs(logits)))) * (1.0 / GTAU)
    return logits, la


def _chunk_cumsum(la_n, tri):
    cum = jnp.dot(tri, la_n, preferred_element_type=F32, precision=lax.Precision.HIGHEST)
    return cum, jnp.sum(la_n, axis=0, keepdims=True)


def _gla_specs(s, nc):
    def blk(width, off):
        return pl.BlockSpec((None, s, width), lambda h, b: (b, 0, off // width + h))

    proj_specs = [blk(GDK, OFF_Q), blk(GDK, OFF_K), blk(GDV, OFF_V), blk(GDV, OFF_G),
                  pl.BlockSpec((None, s, LANE), lambda h, b: (b, 0, OFF_A // LANE)),
                  pl.BlockSpec((LANE, GDK), lambda h, b: (0, h)), pl.BlockSpec((1, GDK), lambda h, b: (0, h)),
                  pl.BlockSpec((1, GDV), lambda h, b: (0, 0))]
    st_spec = pl.BlockSpec((None, None, nc, GDV, GDK), lambda h, b: (b, h, 0, 0, 0))
    return blk, proj_specs, st_spec


def _gla_fwd(proj, w_alpha_p, b_alpha, out_norm_g, *, name):
    bsz, s, _ = proj.shape
    nc = s // CHUNK
    scale = GDK ** -0.5

    rb = min(512, s)

    def body(q_ref, k_ref, v_ref, g_ref, a_ref, wal_ref, bal_ref, ong_ref, o_ref, og_ref, st_ref):
        _, la = _gla_logits(a_ref, wal_ref, bal_ref)
        tri = _tri(True)
        st = jnp.zeros((GDV, GDK), F32)
        for n in range(nc):
            rows = pl.ds(n * CHUNK, CHUNK)
            cum, cum_end = _chunk_cumsum(la[n * CHUNK:(n + 1) * CHUNK], tri)
            kd = k_ref[rows, :] * jnp.exp(cum_end - cum)
            ut = lax.dot_general(v_ref[rows, :].astype(BF16), kd.astype(BF16), _TN, preferred_element_type=F32)
            st = st * jnp.exp(cum_end) + ut
            st_ref[n] = st
            o_ref[rows, :] = lax.dot_general((q_ref[rows, :] * scale).astype(BF16), st.astype(BF16), _NT,
                                             preferred_element_type=F32)
        for j in range(0, s, rb):
            blk_rows = pl.ds(j, rb)
            oh, _ = _rms(o_ref[blk_rows, :], None)
            gv = g_ref[blk_rows, :]
            og_ref[blk_rows, :] = ((oh * ong_ref[...]) * (gv * _sigmoid(gv))).astype(BF16)

    blk, proj_specs, st_spec = _gla_specs(s, nc)
    return pl.pallas_call(
        body, name=name, grid=(GH, bsz), in_specs=proj_specs, out_specs=[blk(GDV, 0), blk(GDV, 0), st_spec],
        out_shape=[jax.ShapeDtypeStruct((bsz, s, GH * GDV), F32), jax.ShapeDtypeStruct((bsz, s, GH * GDV), BF16),
                   jax.ShapeDtypeStruct((bsz, GH, nc, GDV, GDK), F32)],
        compiler_params=_params(2),
    )(proj, proj, proj, proj, proj, w_alpha_p, b_alpha, out_norm_g)


def _gla_bwd(dog, o, states, proj, w_alpha_p, b_alpha, out_norm_g, *, name):
    bsz, s, _ = proj.shape
    nc = s // CHUNK
    scale = GDK ** -0.5

    def body(dog_ref, o_ref, st_ref, q_ref, k_ref, v_ref, g_ref, a_ref, wal_ref, bal_ref, ong_ref,
             dq_ref, dk_ref, dv_ref, dg_ref, dl_ref, dbal_ref, dong_ref, do_scr, dlog_scr):
        h, b = pl.program_id(0), pl.program_id(1)

        @pl.when(b == 0)
        def _():
            dbal_ref[...] = jnp.zeros_like(dbal_ref)

        @pl.when((b == 0) & (h == 0))
        def _():
            dong_ref[...] = jnp.zeros_like(dong_ref)

        ong = ong_ref[...]
        for j in range(0, s, rb):
            blk_rows = pl.ds(j, rb)
            gv, dogv = g_ref[blk_rows, :], dog_ref[blk_rows, :]
            sg = _sigmoid(gv)
            oh, r = _rms(o_ref[blk_rows, :], None)
            don = dogv * (gv * sg)
            dg_ref[blk_rows, :] = (dogv * (oh * ong) * (sg * (1.0 + gv * (1.0 - sg)))).astype(BF16)
            dong_ref[...] += jnp.sum(don * oh, axis=0, keepdims=True)
            doh = don * ong
            do_scr[blk_rows, :] = (r * (doh - oh * jnp.mean(doh * oh, axis=-1, keepdims=True))).astype(BF16)

        logits, la = _gla_logits(a_ref, wal_ref, bal_ref)
        tri_lo, tri_up = _tri(True), _tri(False)
        carry = jnp.zeros((GDV, GDK), F32)
        for n in range(nc - 1, -1, -1):
            rows = pl.ds(n * CHUNK, CHUNK)
            cum, cum_end = _chunk_cumsum(la[n * CHUNK:(n + 1) * CHUNK], tri_lo)
            decay = jnp.exp(cum_end)
            w = jnp.exp(cum_end - cum)
            kd = k_ref[rows, :] * w
            do_b = do_scr[rows, :]
            qs_b = (q_ref[rows, :] * scale).astype(BF16)
            dq_ref[rows, :] = (jnp.dot(do_b, st_ref[n].astype(BF16), preferred_element_type=F32) * scale).astype(BF16)
            dsn = lax.dot_general(do_b, qs_b, _TN, preferred_element_type=F32) + carry
            carry = dsn * decay
            dsn_b = dsn.astype(BF16)
            dv_ref[rows, :] = lax.dot_general(kd.astype(BF16), dsn_b, _NT, preferred_element_type=F32).astype(BF16)
            dkd = jnp.dot(v_ref[rows, :].astype(BF16), dsn_b, preferred_element_type=F32)
            dk_ref[rows, :] = (dkd * w).astype(BF16)
            e = dkd * kd
            dcum_end = jnp.sum(e, axis=0, keepdims=True)
            if n > 0:
                dcum_end += jnp.sum(dsn * st_ref[n - 1], axis=0, keepdims=True) * decay
            dlog_scr[rows, :] = dcum_end - jnp.dot(tri_up, e, preferred_element_type=F32,
                                                  precision=lax.Precision.HIGHEST)
        dlog = dlog_scr[...] * (1.0 / GTAU) * (1.0 - _sigmoid(logits))
        dl_ref[...] = dlog.astype(BF16)
        dbal_ref[...] += jnp.sum(dlog, axis=0, keepdims=True)

    rb = min(512, s)

    blk, proj_specs, st_spec = _gla_specs(s, nc)
    act = lambda wd: jax.ShapeDtypeStruct((bsz, s, wd), BF16)
    return pl.pallas_call(
        body, name=name, grid=(GH, bsz), in_specs=[blk(GDV, 0), blk(GDV, 0), st_spec, *proj_specs],
        out_specs=[blk(GDK, 0), blk(GDK, 0), blk(GDV, 0), blk(GDV, 0), blk(GDK, 0),
                   pl.BlockSpec((1, GDK), lambda h, b: (0, h)), pl.BlockSpec((1, GDV), lambda h, b: (0, 0))],
        out_shape=[act(GH * GDK), act(GH * GDK), act(GH * GDV), act(GH * GDV), act(GH * GDK),
                   jax.ShapeDtypeStruct((1, GH * GDK), F32), jax.ShapeDtypeStruct((1, GDV), F32)],
        scratch_shapes=[pltpu.VMEM((s, GDV), BF16), pltpu.VMEM((s, GDK), F32)], compiler_params=_params(2),
    )(dog, o, states, proj, proj, proj, proj, proj, w_alpha_p, b_alpha, out_norm_g)


def _lane():
    return lax.broadcasted_iota(jnp.int32, (1, LANE), 1)


def _swap_halves(x):
    lane = _lane()
    half = MROPE // 2
    lo = (lane >= MNOPE) & (lane < MNOPE + half)
    hi = (lane >= MNOPE + half) & (lane < MQK)
    return jnp.where(lo, pltpu.roll(x, LANE - half, 1), jnp.where(hi, pltpu.roll(x, half, 1), 0.0))


def _norm96(x, g):
    r = lax.rsqrt(jnp.sum(x * x, axis=-1, keepdims=True) * (1.0 / MQK) + EPS)
    return x * r, r


def _lat_norm(proj, q_lat_g, kv_lat_g, *, name, ts=512):
    t = proj.shape[0]
    ts = min(ts, t)

    def body(cq_ref, ckv_ref, gq_ref, gk_ref, oq_ref, ok_ref):
        xq, _ = _rms(cq_ref[...], None)
        oq_ref[...] = (xq * gq_ref[...]).astype(BF16)
        xk, _ = _rms(ckv_ref[...], None)
        ok_ref[...] = (xk * gk_ref[...]).astype(BF16)

    return pl.pallas_call(
        body, name=name, grid=(t // ts,),
        in_specs=[pl.BlockSpec((ts, MQR), lambda i: (i, OFF_CQ // MQR)), pl.BlockSpec((ts, MKVR), lambda i: (i, OFF_CKV // MKVR)),
                  pl.BlockSpec((1, MQR), lambda i: (0, 0)), pl.BlockSpec((1, MKVR), lambda i: (0, 0))],
        out_specs=[pl.BlockSpec((ts, MQR), lambda i: (i, 0)), pl.BlockSpec((ts, MKVR), lambda i: (i, 0))],
        out_shape=[jax.ShapeDtypeStruct((t, MQR), BF16), jax.ShapeDtypeStruct((t, MKVR), BF16)],
        compiler_params=_params(1),
    )(proj, proj, q_lat_g, kv_lat_g)


def _lat_norm_bwd(dcqn, dckvn, proj, q_lat_g, kv_lat_g, *, name, ts=512):
    t = proj.shape[0]
    ts = min(ts, t)

    def one(d_ref, x_ref, g_ref, dx_ref, dg_ref):
        xh, r = _rms(x_ref[...], None)
        dn = d_ref[...]
        dg_ref[...] += jnp.sum(dn * xh, axis=0, keepdims=True)
        dxh = dn * g_ref[...]
        dx_ref[...] = (r * (dxh - xh * jnp.mean(dxh * xh, axis=-1, keepdims=True))).astype(BF16)

    def body(dq_ref, dk_ref, cq_ref, ckv_ref, gq_ref, gk_ref, dxq_ref, dxk_ref, dgq_ref, dgk_ref):
        @pl.when(pl.program_id(0) == 0)
        def _():
            dgq_ref[...] = jnp.zeros_like(dgq_ref)
            dgk_ref[...] = jnp.zeros_like(dgk_ref)

        one(dq_ref, cq_ref, gq_ref, dxq_ref, dgq_ref)
        one(dk_ref, ckv_ref, gk_ref, dxk_ref, dgk_ref)

    return pl.pallas_call(
        body, name=name, grid=(t // ts,),
        in_specs=[pl.BlockSpec((ts, MQR), lambda i: (i, 0)), pl.BlockSpec((ts, MKVR), lambda i: (i, 0)),
                  pl.BlockSpec((ts, MQR), lambda i: (i, OFF_CQ // MQR)), pl.BlockSpec((ts, MKVR), lambda i: (i, OFF_CKV // MKVR)),
                  pl.BlockSpec((1, MQR), lambda i: (0, 0)), pl.BlockSpec((1, MKVR), lambda i: (0, 0))],
        out_specs=[pl.BlockSpec((ts, MQR), lambda i: (i, 0)), pl.BlockSpec((ts, MKVR), lambda i: (i, 0)),
                   pl.BlockSpec((1, MQR), lambda i: (0, 0)), pl.BlockSpec((1, MKVR), lambda i: (0, 0))],
        out_shape=[jax.ShapeDtypeStruct((t, MQR), BF16), jax.ShapeDtypeStruct((t, MKVR), BF16),
                   jax.ShapeDtypeStruct((1, MQR), F32), jax.ShapeDtypeStruct((1, MKVR), F32)],
        compiler_params=_params(1),
    )(dcqn, dckvn, proj, proj, q_lat_g, kv_lat_g)


def _qk_prep(q_raw, kv, proj, cos_t, sin_t, gq, gk, *, name, ts=512):
    t = q_raw.shape[0]
    ts = min(ts, t)

    def body(q_ref, kv_ref, kpe_ref, c_ref, s_ref, gq_ref, gk_ref, qo_ref, ko_ref, vo_ref):
        cs, sn = c_ref[...], s_ref[...]
        nope = _lane() < MNOPE
        qn, _ = _norm96(q_ref[...], None)
        qn = qn * gq_ref[...]
        qo_ref[...] = (qn * cs + _swap_halves(qn) * sn).astype(BF16)
        kvv = kv_ref[...]
        kn, _ = _norm96(jnp.where(nope, kvv, kpe_ref[...]), None)
        kn = kn * gk_ref[...]
        ko_ref[...] = (kn * cs + _swap_halves(kn) * sn).astype(BF16)
        vo_ref[...] = jnp.where(nope, pltpu.roll(kvv, MNOPE, 1), 0.0).astype(BF16)

    hd = pl.BlockSpec((ts, LANE), lambda i, h: (i, h))
    shared = lambda col: pl.BlockSpec((ts, LANE), lambda i, h: (i, col))
    gain = pl.BlockSpec((1, LANE), lambda i, h: (0, 0))
    out = jax.ShapeDtypeStruct((t, MH * LANE), BF16)
    return pl.pallas_call(
        body, name=name, grid=(t // ts, MH),
        in_specs=[hd, hd, shared(OFF_KPE // LANE), shared(0), shared(0), gain, gain],
        out_specs=[hd, hd, hd], out_shape=[out, out, out], compiler_params=_params(2),
    )(q_raw, kv, proj, cos_t, sin_t, gq, gk)


def _qk_prep_bwd(dq, dk, dv, q_raw, kv, proj, cos_t, sin_t, gq, gk, *, name, ts=512):
    t = q_raw.shape[0]
    ts = min(ts, t)

    def norm_bwd(dy, x, g, dg_ref):
        xh, r = _norm96(x, None)
        dg_ref[...] += jnp.sum(dy * xh, axis=0, keepdims=True)
        dxh = dy * g
        return r * (dxh - xh * (jnp.sum(dxh * xh, axis=-1, keepdims=True) * (1.0 / MQK)))

    def body(dq_ref, dk_ref, dv_ref, q_ref, kv_ref, kpe_ref, c_ref, s_ref, gq_ref, gk_ref,
             dqr_ref, dkv_ref, dkpe_ref, dgq_ref, dgk_ref):
        i, h = pl.program_id(0), pl.program_id(1)

        @pl.when(h == 0)
        def _():
            dkpe_ref[...] = jnp.zeros_like(dkpe_ref)

        @pl.when((h == 0) & (i == 0))
        def _():
            dgq_ref[...] = jnp.zeros_like(dgq_ref)
            dgk_ref[...] = jnp.zeros_like(dgk_ref)

        cs, sn = c_ref[...], s_ref[...]
        lane = _lane()
        nope = lane < MNOPE
        dqv = dq_ref[...]
        dqn = dqv * cs + _swap_halves(dqv * sn)
        dqr_ref[...] = norm_bwd(dqn, q_ref[...], gq_ref[...], dgq_ref).astype(BF16)
        dkv_ = dk_ref[...]
        dkn = dkv_ * cs + _swap_halves(dkv_ * sn)
        kvv = kv_ref[...]
        dkr = norm_bwd(dkn, jnp.where(nope, kvv, kpe_ref[...]), gk_ref[...], dgk_ref)
        dkv_ref[...] = jnp.where(nope, dkr, pltpu.roll(dv_ref[...], MNOPE, 1)).astype(BF16)
        dkpe_ref[...] += jnp.where((lane >= MNOPE) & (lane < MQK), dkr, 0.0)

    hd = pl.BlockSpec((ts, LANE), lambda i, h: (i, h))
    shared = lambda col: pl.BlockSpec((ts, LANE), lambda i, h: (i, col))
    gain = pl.BlockSpec((1, LANE), lambda i, h: (0, 0))
    out = jax.ShapeDtypeStruct((t, MH * LANE), BF16)
    return pl.pallas_call(
        body, name=name, grid=(t // ts, MH),
        in_specs=[hd, hd, hd, hd, hd, shared(OFF_KPE // LANE), shared(0), shared(0), gain, gain],
        out_specs=[hd, hd, shared(0), gain, gain],
        out_shape=[out, out, jax.ShapeDtypeStruct((t, LANE), F32), jax.ShapeDtypeStruct((1, LANE), F32),
                   jax.ShapeDtypeStruct((1, LANE), F32)],
        compiler_params=_params(2),
    )(dq, dk, dv, q_raw, kv, proj, cos_t, sin_t, gq, gk)


_NT = (((1,), (1,)), ((), ()))
_TN = (((0,), (0,)), ((), ()))


def _attn_probs(q, k_ref, lo, tq):
    scale = MQK ** -0.5
    row = lax.broadcasted_iota(jnp.int32, (tq, tq), 0) // CHUNK
    col = lax.broadcasted_iota(jnp.int32, (tq, tq), 1) // CHUNK
    sd = lax.dot_general(q, k_ref[pl.ds(lo, tq), :], _NT, preferred_element_type=F32) * scale
    sd = jnp.where(col <= row, sd, -1e30)
    m = jnp.max(sd, axis=-1, keepdims=True)
    if lo:
        so = lax.dot_general(q, k_ref[pl.ds(0, lo), :], _NT, preferred_element_type=F32) * scale
        m = jnp.maximum(m, jnp.max(so, axis=-1, keepdims=True))
        po = jnp.exp(so - m)
        pd = jnp.exp(sd - m)
        inv = 1.0 / (jnp.sum(po, axis=-1, keepdims=True) + jnp.sum(pd, axis=-1, keepdims=True))
        return po * inv, pd * inv
    pd = jnp.exp(sd - m)
    return None, pd * (1.0 / jnp.sum(pd, axis=-1, keepdims=True))


def _attn_fwd(q, k, v, *, name, tq=256):
    bsz, s, _ = q.shape
    tq = min(tq, s)

    def body(q_ref, k_ref, v_ref, o_ref):
        for i in range(s // tq):
            lo = i * tq
            po, pd = _attn_probs(q_ref[pl.ds(lo, tq), :], k_ref, lo, tq)
            o = jnp.dot(pd.astype(BF16), v_ref[pl.ds(lo, tq), :], preferred_element_type=F32)
            if lo:
                o += jnp.dot(po.astype(BF16), v_ref[pl.ds(0, lo), :], preferred_element_type=F32)
            o_ref[pl.ds(lo, tq), :] = o.astype(BF16)

    spec = pl.BlockSpec((None, s, LANE), lambda b, h: (b, 0, h))
    return pl.pallas_call(
        body, name=name, grid=(bsz, MH), in_specs=[spec, spec, spec], out_specs=spec,
        out_shape=jax.ShapeDtypeStruct((bsz, s, MH * LANE), BF16), compiler_params=_params(2),
    )(q, k, v)


def _attn_bwd(q, k, v, do, *, name, tq=256):
    bsz, s, _ = q.shape
    tq = min(tq, s)
    scale = MQK ** -0.5

    def body(q_ref, k_ref, v_ref, do_ref, dq_ref, dk_ref, dv_ref):
        dk_ref[...] = jnp.zeros_like(dk_ref)
        dv_ref[...] = jnp.zeros_like(dv_ref)
        for i in range(s // tq):
            lo = i * tq
            here, before = pl.ds(lo, tq), pl.ds(0, lo)
            qv, dov = q_ref[here, :], do_ref[here, :]
            po, pd = _attn_probs(qv, k_ref, lo, tq)
            dv_ref[here, :] += lax.dot_general(pd.astype(BF16), dov, _TN, preferred_element_type=F32)
            dpd = lax.dot_general(dov, v_ref[here, :], _NT, preferred_element_type=F32)
            delta = jnp.sum(dpd * pd, axis=-1, keepdims=True)
            if lo:
                dv_ref[before, :] += lax.dot_general(po.astype(BF16), dov, _TN, preferred_element_type=F32)
                dpo = lax.dot_general(dov, v_ref[before, :], _NT, preferred_element_type=F32)
                delta += jnp.sum(dpo * po, axis=-1, keepdims=True)
            dsd = (pd * (dpd - delta) * scale).astype(BF16)
            dq = jnp.dot(dsd, k_ref[here, :], preferred_element_type=F32)
            dk_ref[here, :] += lax.dot_general(dsd, qv, _TN, preferred_element_type=F32)
            if lo:
                dso = (po * (dpo - delta) * scale).astype(BF16)
                dq += jnp.dot(dso, k_ref[before, :], preferred_element_type=F32)
                dk_ref[before, :] += lax.dot_general(dso, qv, _TN, preferred_element_type=F32)
            dq_ref[here, :] = dq

    spec = pl.BlockSpec((None, s, LANE), lambda b, h: (b, 0, h))
    out = jax.ShapeDtypeStruct((bsz, s, MH * LANE), F32)
    return pl.pallas_call(
        body, name=name, grid=(bsz, MH), in_specs=[spec] * 4, out_specs=[spec] * 3, out_shape=[out, out, out],
        compiler_params=_params(2),
    )(q, k, v, do)


def _adamw(w, g, m, v, *, name, tr=256):
    rows, cols = w.shape
    tr = _tile_rows(rows, tr)

    def body(w_ref, g_ref, m_ref, v_ref, d_ref, nm_ref, nv_ref):
        d_ref[...], nm_ref[...], nv_ref[...] = _adamw_update(w_ref[...], g_ref[...], m_ref[...], v_ref[...])

    spec = pl.BlockSpec((tr, cols), lambda i: (i, 0))
    out = jax.ShapeDtypeStruct((rows, cols), F32)
    return pl.pallas_call(body, name=name, grid=(rows // tr,), in_specs=[spec] * 4, out_specs=[spec] * 3,
                          out_shape=[out, out, out], compiler_params=_params(1))(w, g, m, v)


def _tile_rows(rows, target):
    if rows <= target:
        return rows
    best = 8
    for t in range(8, target + 1, 8):
        if rows % t == 0:
            best = t
    return best


def _adamw_update(w, g, m, v):
    nm = ADAM_B1 * m + (1.0 - ADAM_B1) * g
    nv = ADAM_B2 * v + (1.0 - ADAM_B2) * (g * g)
    m_hat = nm / (1.0 - ADAM_B1 ** ADAM_STEP)
    v_hat = nv / (1.0 - ADAM_B2 ** ADAM_STEP)
    return -ADAM_LR * (m_hat / (jnp.sqrt(v_hat) + ADAM_EPS) + ADAM_WD * w), nm, nv


def _adamw_halves(w, m, v, mine, theirs, sel, *, name, tr=256):
    rows, cols = w.shape
    tr = _tile_rows(rows // 2, tr)
    nh = rows // 2 // tr

    def body(sel_ref, w_ref, m_ref, v_ref, mine_ref, theirs_ref, g_ref, d_ref, nm_ref, nv_ref):
        lower = pl.program_id(0) < nh
        south = sel_ref[0] == 0
        gv = jnp.where(lower == south, mine_ref[...], theirs_ref[...])
        g_ref[...] = gv
        d_ref[...], nm_ref[...], nv_ref[...] = _adamw_update(w_ref[...], gv, m_ref[...], v_ref[...])

    full = pl.BlockSpec((tr, cols), lambda i, sel_ref: (i, 0))
    half = pl.BlockSpec((tr, cols), lambda i, sel_ref: (i % nh, 0))
    out = jax.ShapeDtypeStruct((rows, cols), F32)
    return pl.pallas_call(
        body, name=name, out_shape=[out] * 4, compiler_params=_params(1),
        grid_spec=pltpu.PrefetchScalarGridSpec(num_scalar_prefetch=1, grid=(rows // tr,),
                                               in_specs=[full, full, full, half, half], out_specs=[full] * 4),
    )(sel, w, m, v, mine, theirs)


def _pair_add(x, sib, sel, *, name, tr=256):
    n, _, rows, cols = x.shape
    tr = _tile_rows(rows, tr)

    def body(sel_ref, x_ref, s_ref, o_ref):
        o_ref[...] = (x_ref[...] + s_ref[...]).astype(BF16)

    spec = pl.BlockSpec((None, tr, cols), lambda j, i, sel_ref: (j, i, 0))
    return pl.pallas_call(
        body, name=name, out_shape=jax.ShapeDtypeStruct((n, rows, cols), BF16), compiler_params=_params(2),
        grid_spec=pltpu.PrefetchScalarGridSpec(
            num_scalar_prefetch=1, grid=(n, rows // tr),
            in_specs=[pl.BlockSpec((None, None, tr, cols), lambda j, i, sel_ref: (j, sel_ref[0], i, 0)), spec],
            out_specs=spec),
    )(sel, x, sib)


def _chip_sum(pair, recv, sel, *, name, tr=256):
    _, rows, cols = pair.shape
    tr = _tile_rows(rows, tr)

    def body(sel_ref, p_ref, r_ref, o_ref):
        acc = p_ref[...].astype(F32)
        for k in range(3):
            acc = acc + r_ref[k].astype(F32)
        o_ref[...] = acc

    return pl.pallas_call(
        body, name=name, out_shape=jax.ShapeDtypeStruct((rows, cols), F32), compiler_params=_params(1),
        grid_spec=pltpu.PrefetchScalarGridSpec(
            num_scalar_prefetch=1, grid=(rows // tr,),
            in_specs=[pl.BlockSpec((None, tr, cols), lambda i, sel_ref: (sel_ref[0], i, 0)),
                      pl.BlockSpec((3, tr, cols), lambda i, sel_ref: (0, i, 0))],
            out_specs=pl.BlockSpec((tr, cols), lambda i, sel_ref: (i, 0))),
    )(sel, pair, recv)


def _me():
    return lax.axis_index("x"), lax.axis_index("y"), lax.axis_index("c")


def _flip(pos, bits):
    x, y, c = pos
    return (x ^ bits[0] if bits[0] else x, y ^ bits[1] if bits[1] else y, c ^ bits[2] if bits[2] else c)


ANY = pl.BlockSpec(memory_space=pl.ANY)


def _all_gather8(x, *, name):
    flips = [((k >> 2) & 1, (k >> 1) & 1, k & 1) for k in range(1, 8)]

    def body(x_ref, out_ref, send_sems, recv_sems, local_sem):
        me = _me()
        slot = lambda p: 4 * p[0] + 2 * p[1] + p[2]
        mine = pltpu.make_async_copy(x_ref, out_ref.at[slot(me)], local_sem)
        mine.start()
        sends = []
        for k, f in enumerate(flips):
            cp = pltpu.make_async_remote_copy(src_ref=x_ref, dst_ref=out_ref.at[slot(me)], send_sem=send_sems.at[k],
                                              recv_sem=recv_sems.at[k], device_id=_flip(me, f), device_id_type=MESH)
            cp.start()
            sends.append(cp)
        for k, f in enumerate(flips):
            peer = _flip(me, f)
            pltpu.make_async_remote_copy(src_ref=x_ref, dst_ref=out_ref.at[slot(peer)], send_sem=send_sems.at[k],
                                         recv_sem=recv_sems.at[k], device_id=peer, device_id_type=MESH).wait_recv()
        for cp in sends:
            cp.wait_send()
        mine.wait()

    return pl.pallas_call(
        body, name=name, in_specs=[ANY], out_specs=ANY, out_shape=jax.ShapeDtypeStruct((8, *x.shape), x.dtype),
        scratch_shapes=[pltpu.SemaphoreType.DMA((7,)), pltpu.SemaphoreType.DMA((7,)), pltpu.SemaphoreType.DMA])(x)


CHIP_FLIPS = [(1, 0, 0), (0, 1, 0), (1, 1, 0)]


def _chip():
    return 2 * lax.axis_index("x") + lax.axis_index("y")


def _gather_weights(xs, *, name):
    n = len(xs)
    halves = [x.reshape(2, x.shape[0] // 2, x.shape[1]) for x in xs]

    def body(*refs):
        x_refs, out_refs, (send_sems, recv_sems) = refs[:n], refs[n:2 * n], refs[2 * n:]
        me = _me()
        sib = _flip(me, (0, 0, 1))
        slot = lambda p: 2 * p[0] + p[1]
        my_half, their_half = me[2], 1 - me[2]

        def copy(k, src, dst, to):
            return pltpu.make_async_remote_copy(src_ref=src, dst_ref=dst, send_sem=send_sems.at[k],
                                                recv_sem=recv_sems.at[k], device_id=to, device_id_type=MESH)

        sends = []
        for i in range(n):
            for k, f in enumerate(CHIP_FLIPS):
                cp = copy(6 * i + k, x_refs[i].at[my_half], out_refs[i].at[slot(me), my_half], _flip(me, f))
                cp.start()
                sends.append(cp)
        for i in range(n):
            for k, f in enumerate(CHIP_FLIPS):
                landed = out_refs[i].at[slot(_flip(me, f)), my_half]
                copy(6 * i + k, landed, landed, me).wait_recv()
                cp = copy(6 * i + 3 + k, landed, landed, sib)
                cp.start()
                sends.append(cp)
        for i in range(n):
            for k, f in enumerate(CHIP_FLIPS):
                from_sib = out_refs[i].at[slot(_flip(me, f)), their_half]
                copy(6 * i + 3 + k, from_sib, from_sib, sib).wait_recv()
        for cp in sends:
            cp.wait_send()

    outs = pl.pallas_call(
        body, name=name, in_specs=[ANY] * n, out_specs=[ANY] * n,
        out_shape=[jax.ShapeDtypeStruct((4, *h.shape), h.dtype) for h in halves],
        scratch_shapes=[pltpu.SemaphoreType.DMA((6 * n,)), pltpu.SemaphoreType.DMA((6 * n,))])(*halves)
    return [lax.dynamic_update_slice(o.reshape(4, *x.shape), x[None], (_chip(), 0, 0)) for o, x in zip(outs, xs)]


def _pair_swap_halves(xs, *, name):
    n = len(xs)

    def body(*refs):
        x_refs, out_refs, (send_sems, recv_sems) = refs[:n], refs[n:2 * n], refs[2 * n:]
        me = _me()
        sib = _flip(me, (0, 0, 1))
        copies = [pltpu.make_async_remote_copy(src_ref=x_refs[i].at[:, 1 - me[2]], dst_ref=out_refs[i],
                                               send_sem=send_sems.at[i], recv_sem=recv_sems.at[i], device_id=sib,
                                               device_id_type=MESH) for i in range(n)]
        for cp in copies:
            cp.start()
        for cp in copies:
            cp.wait()

    return pl.pallas_call(
        body, name=name, in_specs=[ANY] * n, out_specs=[ANY] * n,
        out_shape=[jax.ShapeDtypeStruct((x.shape[0], *x.shape[2:]), x.dtype) for x in xs],
        scratch_shapes=[pltpu.SemaphoreType.DMA((n,)), pltpu.SemaphoreType.DMA((n,))])(*xs)


def _scatter_chips(ps, *, name):
    n = len(ps)

    def body(*refs):
        p_refs, out_refs, (send_sems, recv_sems) = refs[:n], refs[n:2 * n], refs[2 * n:]
        me = _me()
        slot = lambda q: 2 * q[0] + q[1]
        sends = []
        for i in range(n):
            for k, f in enumerate(CHIP_FLIPS):
                peer = _flip(me, f)
                cp = pltpu.make_async_remote_copy(src_ref=p_refs[i].at[slot(peer)], dst_ref=out_refs[i].at[k],
                                                  send_sem=send_sems.at[3 * i + k], recv_sem=recv_sems.at[3 * i + k],
                                                  device_id=peer, device_id_type=MESH)
                cp.start()
                sends.append(cp)
        for cp in sends:
            cp.wait()

    return pl.pallas_call(
        body, name=name, in_specs=[ANY] * n, out_specs=[ANY] * n,
        out_shape=[jax.ShapeDtypeStruct((3, *p.shape[1:]), p.dtype) for p in ps],
        scratch_shapes=[pltpu.SemaphoreType.DMA((3 * n,)), pltpu.SemaphoreType.DMA((3 * n,))])(*ps)


def _pair_swap(hs, *, name):
    n = len(hs)

    def body(*refs):
        h_refs, out_refs, (send_sems, recv_sems) = refs[:n], refs[n:2 * n], refs[2 * n:]
        sib = _flip(_me(), (0, 0, 1))
        copies = [pltpu.make_async_remote_copy(src_ref=h_refs[i], dst_ref=out_refs[i], send_sem=send_sems.at[i],
                                               recv_sem=recv_sems.at[i], device_id=sib, device_id_type=MESH)
                  for i in range(n)]
        for cp in copies:
            cp.start()
        for cp in copies:
            cp.wait()

    return pl.pallas_call(
        body, name=name, in_specs=[ANY] * n, out_specs=[ANY] * n,
        out_shape=[jax.ShapeDtypeStruct(h.shape, h.dtype) for h in hs],
        scratch_shapes=[pltpu.SemaphoreType.DMA((n,)), pltpu.SemaphoreType.DMA((n,))])(*hs)


BIG = (("w_in", (D, IN_WIDTH // 4), 1), ("gla_w_o", (D // 4, D), 0), ("mla_w_uq", (MQR, MH * MQK // 4), 1),
       ("mla_w_ukv", (MKVR, MH * (MNOPE + MVD) // 4), 1), ("mla_w_o", (D // 4, D), 0), ("w_out", (D // 4, D), 0),
       ("mlp_w1", (D, DFF // 4), 1), ("mlp_w2", (DFF // 4, D), 0))
ADA_SHARD = (D, 6 * D // 4)
SMALL = (("b_ada", 6 * D), ("norm1_g", D), ("b_merge", 2 * D), ("gla_b_alpha", GH * GDK), ("gla_out_norm_g", GDV),
         ("mla_q_lat_g", MQR), ("mla_kv_lat_g", MKVR), ("mla_qn_g", MQK), ("mla_kn_g", MQK), ("norm2_g", D))


SMALL_ROWS, SMALL_COLS = 32, 2 * D
W_ALPHA_ROW = 16
SMALL_RED = tuple((n, k) for n, k in SMALL if n != "b_ada")


def _pack_small(grads, d_w_alpha, *, name):
    def body(*refs):
        g_refs, wa_ref, out_ref = refs[:-2], refs[-2], refs[-1]
        out_ref[...] = jnp.zeros_like(out_ref)
        for i, ((_, k), g_ref) in enumerate(zip(SMALL_RED, g_refs)):
            out_ref[i:i + 1, 0:k] = g_ref[...]
        out_ref[W_ALPHA_ROW:W_ALPHA_ROW + GLR, 0:GH * GDK] = wa_ref[...]

    return pl.pallas_call(body, name=name, out_shape=jax.ShapeDtypeStruct((SMALL_ROWS, SMALL_COLS), F32))(*grads, d_w_alpha)


def _small_update(gathered, dmod_all, sel, wmv, *, name):
    names = [n for n, _ in SMALL] + ["gla_w_alpha"]
    n_par = len(names)

    def body(sel_ref, g_ref, dmod_ref, *refs):
        in_refs, out_refs, acc = refs[:3 * n_par], refs[3 * n_par:-1], refs[-1]
        total = g_ref[0]
        for j in range(1, 8):
            total = total + g_ref[j]
        acc[...] = total
        row = {n: i for i, (n, _) in enumerate(SMALL_RED)}
        for p, name_p in enumerate(names):
            w_ref, m_ref, v_ref = in_refs[3 * p:3 * p + 3]
            if name_p == "b_ada":
                gv = jnp.sum(dmod_ref[...], axis=0, keepdims=True)
            elif name_p == "gla_w_alpha":
                gv = jnp.zeros((GLR, GDK), F32)
                for j in range(4):
                    blk = acc[W_ALPHA_ROW:W_ALPHA_ROW + GLR, j * GDK:(j + 1) * GDK]
                    gv = gv + jnp.where(sel_ref[0] == j, blk, 0.0)
            else:
                gv = acc[row[name_p]:row[name_p] + 1, 0:w_ref.shape[1]]
            o = out_refs[4 * p:4 * p + 4]
            o[0][...] = gv
            o[1][...], o[2][...], o[3][...] = _adamw_update(w_ref[...], gv, m_ref[...], v_ref[...])

    flat = [a for t in wmv for a in t]
    out_shape = [jax.ShapeDtypeStruct(t[0].shape, F32) for t in wmv for _ in range(4)]
    vmem = pl.BlockSpec(memory_space=pltpu.VMEM)
    outs = pl.pallas_call(
        body, name=name, out_shape=out_shape, in_specs=[pl.BlockSpec(memory_space=pltpu.SMEM), vmem, vmem] + [vmem] * len(flat),
        out_specs=[vmem] * len(out_shape), scratch_shapes=[pltpu.VMEM((SMALL_ROWS, SMALL_COLS), F32)],
    )(sel, gathered, dmod_all, *flat)
    return {n: tuple(outs[4 * p:4 * p + 4]) for p, n in enumerate(names)}


def _full_weights(gathered):
    w = {}
    for name, _, axis in BIG:
        a = gathered[name]
        w[name] = a.reshape(-1, a.shape[2]) if axis == 0 else jnp.transpose(a, (1, 0, 2)).reshape(a.shape[1], -1)
    wi = w["w_in"]
    zeros = lambda n: jnp.zeros((D, n), wi.dtype)
    w["w_in"] = jnp.concatenate(
        [wi[:, :3072], wi[:, 3504:5552], wi[:, 3088:3344], wi[:, 3344:3472], wi[:, 3072:3088], zeros(LANE - GLR),
         zeros(MNOPE), wi[:, 3472:3504], zeros(LANE - MQK)], axis=1)
    w["mla_w_uq"] = jnp.pad(w["mla_w_uq"].reshape(MQR, MH, MQK), ((0, 0), (0, 0), (0, LANE - MQK))).reshape(MQR, MH * LANE)
    w["mla_w_o"] = jnp.pad(w["mla_w_o"].reshape(MH, MVD, D), ((0, 0), (0, LANE - MVD), (0, 0))).reshape(MH * LANE, D)
    return w


def _grad_slots(g):
    gi = g["w_in"]
    g = dict(g)
    g["w_in"] = jnp.concatenate(
        [gi[:, :3072], gi[:, OFF_A:OFF_A + GLR], gi[:, OFF_CQ:OFF_CQ + MQR], gi[:, OFF_CKV:OFF_CKV + MKVR],
         gi[:, OFF_KPE + MNOPE:OFF_KPE + MQK], gi[:, OFF_MA:OFF_MA + 2 * D]], axis=1)
    g["mla_w_uq"] = g["mla_w_uq"].reshape(MQR, MH, LANE)[:, :, :MQK].reshape(MQR, MH * MQK)
    g["mla_w_o"] = g["mla_w_o"].reshape(MH, LANE, D)[:, :MVD].reshape(MH * MVD, D)
    out = {}
    for name, (rows, cols), axis in BIG:
        a = g[name]
        a = a.reshape(4, rows, cols) if axis == 0 else jnp.transpose(a.reshape(rows, 4, cols), (1, 0, 2))
        out[name] = a.reshape(4, 2, rows // 2, cols)
    return out


def _rope_tables(positions):
    freqs = ROPE_THETA ** (-jnp.arange(0, MROPE, 2, dtype=F32) / MROPE)
    lane = np.arange(LANE)
    in_rope = (lane >= MNOPE) & (lane < MQK)
    freq_lane = jnp.where(in_rope, freqs[(lane - MNOPE) % (MROPE // 2)], 0.0)
    sign = np.where(in_rope, np.where(lane < MNOPE + MROPE // 2, -1.0, 1.0), 0.0).astype(np.float32)
    ang = positions.astype(F32).reshape(-1, 1) * freq_lane[None, :]
    return jnp.cos(ang), jnp.sin(ang) * sign[None, :]


def _local_step(x, positions, mod, target, w, small):
    bsz, s, _ = x.shape
    t = bsz * s
    tt = _tile(t, 1024)
    shift1, scale1, gate1, shift2, scale2, gate2 = [mod[:, None, i * D:(i + 1) * D] for i in range(6)]
    cos_t, sin_t = _rope_tables(positions)
    w_alpha_p = jnp.pad(small["gla_w_alpha"], ((0, LANE - GLR), (0, 0)))
    gq = jnp.pad(small["mla_qn_g"], ((0, 0), (0, LANE - MQK)))
    gk = jnp.pad(small["mla_kn_g"], ((0, 0), (0, LANE - MQK)))
    flat2 = lambda a: a.reshape(t, a.shape[-1])
    bsd = lambda a: a.reshape(bsz, s, a.shape[-1])

    h = _norm_mod(x, small["norm1_g"], scale1, shift1, name="norm1")
    proj = _mm(flat2(h), w["w_in"], name="proj", tn=1152)
    proj3 = bsd(proj)
    o, o_gated, states = _gla_fwd(proj3, w_alpha_p, small["gla_b_alpha"], small["gla_out_norm_g"], name="gla_fwd")
    y_a = _mm(flat2(o_gated), w["gla_w_o"], name="gla_out")
    cq_n, ckv_n = _lat_norm(proj, small["mla_q_lat_g"], small["mla_kv_lat_g"], name="lat_norm")
    q_raw = _mm(cq_n, w["mla_w_uq"], name="mla_uq")
    kv = _mm(ckv_n, w["mla_w_ukv"], name="mla_ukv")
    qf, kf, vf = _qk_prep(q_raw, kv, proj, cos_t, sin_t, gq, gk, name="qk_prep")
    o_attn = _attn_fwd(bsd(qf), bsd(kf), bsd(vf), name="attn_fwd")
    y_b = _mm(flat2(o_attn), w["mla_w_o"], name="mla_out")
    mixed_in = _merge_fwd(proj3, small["b_merge"], bsd(y_a), bsd(y_b), name="merge_fwd")
    mixed = _mm(flat2(mixed_in), w["w_out"], name="w_out")
    x1, h2 = _resid_norm_mod(x, bsd(mixed), gate1, small["norm2_g"], scale2, shift2, name="norm2")

    def sqrelu(acc, ex, outs):
        outs[0][...] = acc
        r = jnp.maximum(acc, 0.0)
        outs[1][...] = (r * r).astype(BF16)

    a1, r = _mm(flat2(h2), w["mlp_w1"], name="mlp1", epilogue=sqrelu,
                out_shape=[jax.ShapeDtypeStruct((t, DFF), F32), jax.ShapeDtypeStruct((t, DFF), BF16)],
                out_specs=[_tile_spec(tt, 1024), _tile_spec(tt, 1024)])
    ff = _mm(r, w["mlp_w2"], name="mlp2")
    dy, dff, dgate2, loss_part = _loss_head(x1, bsd(ff), gate2, target, name="loss_head")

    g = {}

    def relu2_bwd(acc, ex, outs):
        outs[0][...] = (acc * (2.0 * jnp.maximum(ex[0][...], 0.0))).astype(BF16)

    dff2 = flat2(dff)
    da1 = _mm(dff2, w["mlp_w2"], tb=True, name="mlp2_dx", epilogue=relu2_bwd, extras=(a1,),
              extra_specs=(_tile_spec(tt, 1024),), out_shape=jax.ShapeDtypeStruct((t, DFF), BF16),
              out_specs=_tile_spec(tt, 1024))
    g["mlp_w2"] = _mm(r, dff2, ta=True, name="mlp2_dw")
    dh2 = _mm(da1, w["mlp_w1"], tb=True, name="mlp1_dx")
    g["mlp_w1"] = _mm(flat2(h2), da1, ta=True, name="mlp1_dw")
    dx1, dscale2, dshift2, dg2, dgate1, dmixed = _norm_mod_bwd(
        bsd(dh2), x1, dy, small["norm2_g"], scale2, gate1, bsd(mixed), name="norm2_bwd")
    dmixed2 = flat2(dmixed)
    dmi = _mm(dmixed2, w["w_out"], tb=True, name="w_out_dx")
    g["w_out"] = _mm(flat2(mixed_in), dmixed2, ta=True, name="w_out_dw")
    dy_a, dy_b, dl_a, dl_b, db_a, db_b = _merge_bwd(bsd(dmi), proj3, small["b_merge"], bsd(y_a), bsd(y_b), name="merge_bwd")
    dy_a2, dy_b2 = flat2(dy_a), flat2(dy_b)
    dog = _mm(dy_a2, w["gla_w_o"], tb=True, name="gla_out_dx")
    g["gla_w_o"] = _mm(flat2(o_gated), dy_a2, ta=True, name="gla_out_dw")
    dq_g, dk_g, dv_g, dg_g, dlog, db_alpha, d_ong = _gla_bwd(
        bsd(dog), o, states, proj3, w_alpha_p, small["gla_b_alpha"], small["gla_out_norm_g"], name="gla_bwd")
    dlog2 = flat2(dlog)
    da_p = _mm(dlog2, w_alpha_p, tb=True, out_dtype=BF16, name="alpha_dx")
    d_w_alpha = _mm(proj[:, OFF_A:OFF_A + LANE], dlog2, ta=True, name="alpha_dw")[:GLR]
    do_attn = _mm(dy_b2, w["mla_w_o"], tb=True, out_dtype=BF16, name="mla_out_dx")
    g["mla_w_o"] = _mm(flat2(o_attn), dy_b2, ta=True, name="mla_out_dw")
    dqf, dkf, dvf = _attn_bwd(bsd(qf), bsd(kf), bsd(vf), bsd(do_attn), name="attn_bwd")
    dq_raw, dkv, dkpe, dgq, dgk = _qk_prep_bwd(flat2(dqf), flat2(dkf), flat2(dvf), q_raw, kv, proj, cos_t, sin_t, gq, gk,
                                                name="qk_prep_bwd")
    dcq_n = _mm(dq_raw, w["mla_w_uq"], tb=True, name="mla_uq_dx")
    g["mla_w_uq"] = _mm(cq_n, dq_raw, ta=True, name="mla_uq_dw")
    dckv_n = _mm(dkv, w["mla_w_ukv"], tb=True, name="mla_ukv_dx")
    g["mla_w_ukv"] = _mm(ckv_n, dkv, ta=True, name="mla_ukv_dw")
    dcq, dckv, dg_qlat, dg_kvlat = _lat_norm_bwd(dcq_n, dckv_n, proj, small["mla_q_lat_g"], small["mla_kv_lat_g"],
                                                  name="lat_norm_bwd")
    dproj = jnp.concatenate([flat2(dq_g), flat2(dk_g), flat2(dv_g), flat2(dg_g), flat2(dl_a), flat2(dl_b), dcq, dckv,
                             da_p, dkpe.astype(BF16)], axis=1)
    dh = _mm(dproj, w["w_in"], tb=True, name="proj_dx", tk=1152)
    g["w_in"] = _mm(flat2(h), dproj, ta=True, name="proj_dw", tn=1152)
    grad_x, dscale1, dshift1, dg1 = _norm_mod_bwd(bsd(dh), x, dx1, small["norm1_g"], scale1, name="norm1_bwd")

    dmod = jnp.concatenate([dshift1, dscale1, dgate1, dshift2, dscale2, dgate2], axis=-1).reshape(bsz, 6 * D)
    gs = {"norm1_g": dg1, "b_merge": jnp.concatenate([db_a, db_b], axis=1), "gla_b_alpha": db_alpha,
          "gla_out_norm_g": d_ong, "mla_q_lat_g": dg_qlat, "mla_kv_lat_g": dg_kvlat, "mla_qn_g": dgq[:, :MQK],
          "mla_kn_g": dgk[:, :MQK], "norm2_g": dg2}
    return loss_part[0, 0], grad_x, dmod, g, gs, d_w_alpha


def kernel(x, c, positions, w_ada, b_ada, norm1_g, w_in, b_merge, gla_w_alpha, gla_b_alpha, gla_out_norm_g, gla_w_o, mla_q_lat_g, mla_w_uq, mla_kv_lat_g, mla_w_ukv, mla_qn_g, mla_kn_g, mla_w_o, w_out, norm2_g, mlp_w1, mlp_w2, loss_target, m_w_ada, m_b_ada, m_norm1_g, m_w_in, m_b_merge, m_gla_w_alpha, m_gla_b_alpha, m_gla_out_norm_g, m_gla_w_o, m_mla_q_lat_g, m_mla_w_uq, m_mla_kv_lat_g, m_mla_w_ukv, m_mla_qn_g, m_mla_kn_g, m_mla_w_o, m_w_out, m_norm2_g, m_mlp_w1, m_mlp_w2, v_w_ada, v_b_ada, v_norm1_g, v_w_in, v_b_merge, v_gla_w_alpha, v_gla_b_alpha, v_gla_out_norm_g, v_gla_w_o, v_mla_q_lat_g, v_mla_w_uq, v_mla_kv_lat_g, v_mla_w_ukv, v_mla_qn_g, v_mla_kn_g, v_mla_w_o, v_w_out, v_norm2_g, v_mlp_w1, v_mlp_w2):
    args = dict(locals())
    names_big = [n for n, _, _ in BIG]
    names_small = [n for n, _ in SMALL]
    bsz = x.shape[0]
    ax, ay, ac = lax.axis_index("x"), lax.axis_index("y"), lax.axis_index("c")
    chip = 2 * ax + ay
    dev = 2 * chip + ac

    gathered = _gather_weights([args[n][0].astype(BF16) for n in names_big], name="comm_weights")
    w = _full_weights(dict(zip(names_big, gathered)))
    small = {n: args[n] for n in names_small}
    sel_c = jnp.reshape(ac, (1,)).astype(jnp.int32)
    sel_chip = jnp.reshape(chip, (1,)).astype(jnp.int32)
    w_alpha_all = _all_gather8(gla_w_alpha[0], name="comm_w_alpha")
    small["gla_w_alpha"] = jnp.concatenate([w_alpha_all[2 * j] for j in range(4)], axis=1)

    c_all = _all_gather8(c, name="comm_c").reshape(8 * bsz, D)

    def add_bias(acc, ex, outs):
        outs[0][...] = acc + ex[0][...]

    silu = lambda v: v * _sigmoid(v)
    b_ada_mine = lax.dynamic_slice(b_ada, (0, chip * ADA_SHARD[1]), (1, ADA_SHARD[1]))
    mod_part = _mm(c_all, w_ada[0], name="ada", tn=512, a_fn=silu, epilogue=add_bias, extras=(b_ada_mine,),
                   extra_specs=(pl.BlockSpec((1, 512), lambda i, j, k: (0, j)),),
                   out_shape=jax.ShapeDtypeStruct((8 * bsz, ADA_SHARD[1]), F32), out_specs=_tile_spec(8 * bsz, 512))
    mod_all = _all_gather8(mod_part, name="comm_mod")
    mod_rows = lax.dynamic_slice(mod_all, (0, dev * bsz, 0), (8, bsz, ADA_SHARD[1]))
    mod = jnp.concatenate([mod_rows[2 * j] for j in range(4)], axis=1)

    loss_part, grad_x, dmod, g, gs, d_w_alpha = _local_step(x, positions, mod, loss_target, w, small)
    loss = lax.psum(loss_part * (0.5 / D), ("x", "y", "c"))

    dmod_all = _all_gather8(dmod, name="comm_dmod").reshape(8 * bsz, 6 * D)
    dmod_mine = lax.dynamic_slice(dmod_all, (0, chip * ADA_SHARD[1]), (8 * bsz, ADA_SHARD[1]))
    g_w_ada = _mm(c_all, dmod_mine, ta=True, a_fn=silu, name="ada_dw")

    gs_packed = _pack_small([gs[n] for n, _ in SMALL_RED], d_w_alpha, name="pack_small")
    gs_all = _all_gather8(gs_packed, name="comm_small")
    wmv = [(args[n], args["m_" + n], args["v_" + n]) for n in names_small]
    wmv.append((gla_w_alpha[0], m_gla_w_alpha[0], v_gla_w_alpha[0]))
    res = _small_update(gs_all, dmod_all, sel_chip, wmv, name="small_update")

    gslots = _grad_slots(g)
    parts = [gslots[n] for n in names_big]
    sib_halves = _pair_swap_halves(parts, name="comm_pair_sum")
    pairs = [_pair_add(p, s, sel_c, name="pair_add_" + n) for n, p, s in zip(names_big, parts, sib_halves)]
    from_chips = _scatter_chips(pairs, name="comm_scatter")
    halves = [_chip_sum(p, r, sel_chip, name="chip_sum_" + n) for n, p, r in zip(names_big, pairs, from_chips)]
    theirs = _pair_swap(halves, name="comm_pair_join")
    for n, mine, other in zip(names_big, halves, theirs):
        res[n] = _adamw_halves(args[n][0], args["m_" + n][0], args["v_" + n][0], mine, other, sel_c, name="adamw_" + n)
    res["w_ada"] = (g_w_ada, *_adamw(w_ada[0], g_w_ada, m_w_ada[0], v_w_ada[0], name="adamw_w_ada"))

    order = ["w_ada", "b_ada", "norm1_g", "w_in", "b_merge", "gla_w_alpha", "gla_b_alpha", "gla_out_norm_g", "gla_w_o",
             "mla_q_lat_g", "mla_w_uq", "mla_kv_lat_g", "mla_w_ukv", "mla_qn_g", "mla_kn_g", "mla_w_o", "w_out",
             "norm2_g", "mlp_w1", "mlp_w2"]
    named = lambda k: [res[n][k].reshape(args[n].shape) for n in order]
    return (loss, grad_x, *named(0), *named(1), *named(2), *named(3))
```

```python
import functools

import jax
import jax.numpy as jnp
import numpy as np
from jax import lax
from jax.experimental import pallas as pl
from jax.experimental.pallas import tpu as pltpu

F32 = jnp.float32
BF16 = jnp.bfloat16
MESH = pl.DeviceIdType.MESH

D = 1024
CHUNK = 64
EPS = 1e-6
GH, GDK, GDV, GLR, GTAU = 4, 128, 256, 16, 16.0
MH, MQR, MKVR, MNOPE, MROPE, MVD = 16, 256, 128, 64, 32, 64
MQK = MNOPE + MROPE
DFF = 4 * D
ROPE_THETA = 10000.0
IN_WIDTH = 5552
LANE = 128
OFF_Q, OFF_K, OFF_V, OFF_G, OFF_MA, OFF_MB, OFF_CQ, OFF_CKV, OFF_A, OFF_KPE, PW = (
    0, 512, 1024, 2048, 3072, 4096, 5120, 5376, 5504, 5632, 5760)
ADAM_LR, ADAM_B1, ADAM_B2, ADAM_EPS, ADAM_WD, ADAM_STEP = 0.001, 0.9, 0.999, 1e-08, 0.01, 10
VMEM_LIMIT = 48 * 1024 * 1024


def _params(n_axes):
    return pltpu.CompilerParams(dimension_semantics=("arbitrary",) * n_axes, vmem_limit_bytes=VMEM_LIMIT)


def _tile(n, target):
    if n <= target:
        return n
    best = None
    for t in range(LANE, target + 1, LANE):
        if n % t == 0:
            best = t
    assert best is not None, (n, target)
    return best


def _sigmoid(x):
    return 1.0 / (1.0 + jnp.exp(-x))


def _mm(a, b, *, name, ta=False, tb=False, out_dtype=F32, tm=1024, tn=1024, tk=1024,
        epilogue=None, extras=(), extra_specs=(), out_shape=None, out_specs=None, a_fn=None):
    if ta:
        kdim, m = a.shape
    else:
        m, kdim = a.shape
    if tb:
        n, k2 = b.shape
    else:
        k2, n = b.shape
    assert kdim == k2, (a.shape, b.shape)
    tm, tn, tk = _tile(m, tm), _tile(n, tn), _tile(kdim, tk)
    nk = kdim // tk
    a_spec = pl.BlockSpec((tk, tm), lambda i, j, k: (k, i)) if ta else pl.BlockSpec((tm, tk), lambda i, j, k: (i, k))
    b_spec = pl.BlockSpec((tn, tk), lambda i, j, k: (j, k)) if tb else pl.BlockSpec((tk, tn), lambda i, j, k: (k, j))
    dims = (((0 if ta else 1,), (1 if tb else 0,)), ((), ()))
    ne = len(extras)
    if out_shape is None:
        out_shape = jax.ShapeDtypeStruct((m, n), out_dtype)
        out_specs = pl.BlockSpec((tm, tn), lambda i, j, k: (i, j))

    def body(a_ref, b_ref, *rest):
        ex, outs, acc = rest[:ne], rest[ne:-1], rest[-1]
        k = pl.program_id(2)

        @pl.when(k == 0)
        def _():
            acc[...] = jnp.zeros_like(acc)

        av = a_ref[...] if a_fn is None else a_fn(a_ref[...])
        acc[...] += lax.dot_general(av.astype(BF16), b_ref[...].astype(BF16), dims, preferred_element_type=F32)

        @pl.when(k == nk - 1)
        def _():
            if epilogue is None:
                outs[0][...] = acc[...].astype(outs[0].dtype)
            else:
                epilogue(acc[...], ex, outs)

    return pl.pallas_call(
        body, name=name, grid=(m // tm, n // tn, nk),
        in_specs=[a_spec, b_spec, *extra_specs], out_specs=out_specs, out_shape=out_shape,
        scratch_shapes=[pltpu.VMEM((tm, tn), F32)], compiler_params=_params(3),
    )(a, b, *extras)


def _tile_spec(tm, tn):
    return pl.BlockSpec((tm, tn), lambda i, j, k: (i, j))


def _rms(x, g):
    r = lax.rsqrt(jnp.mean(x * x, axis=-1, keepdims=True) + EPS)
    return x * r, r


def _row_spec(ts, width, col=0):
    return pl.BlockSpec((None, ts, width), lambda b, i: (b, i, col))


def _vec_spec(width):
    return pl.BlockSpec((None, 1, width), lambda b, i: (b, 0, 0))


def _gain_spec(width):
    return pl.BlockSpec((1, width), lambda b, i: (0, 0))


def _norm_mod(x, g, scale, shift, *, name, ts=256):
    bsz, s, d = x.shape
    ts = min(ts, s)

    def body(x_ref, g_ref, sc_ref, sh_ref, h_ref):
        xh, _ = _rms(x_ref[...], None)
        h_ref[...] = ((xh * g_ref[...]) * (1.0 + sc_ref[...]) + sh_ref[...]).astype(BF16)

    return pl.pallas_call(
        body, name=name, grid=(bsz, s // ts),
        in_specs=[_row_spec(ts, d), _gain_spec(d), _vec_spec(d), _vec_spec(d)],
        out_specs=_row_spec(ts, d), out_shape=jax.ShapeDtypeStruct((bsz, s, d), BF16),
        compiler_params=_params(2),
    )(x, g, scale, shift)


def _resid_norm_mod(x, mixed, gate, g, scale, shift, *, name, ts=256):
    bsz, s, d = x.shape
    ts = min(ts, s)

    def body(x_ref, mx_ref, gt_ref, g_ref, sc_ref, sh_ref, x1_ref, h_ref):
        x1 = x_ref[...] + gt_ref[...] * mx_ref[...]
        x1_ref[...] = x1
        xh, _ = _rms(x1, None)
        h_ref[...] = ((xh * g_ref[...]) * (1.0 + sc_ref[...]) + sh_ref[...]).astype(BF16)

    return pl.pallas_call(
        body, name=name, grid=(bsz, s // ts),
        in_specs=[_row_spec(ts, d), _row_spec(ts, d), _vec_spec(d), _gain_spec(d), _vec_spec(d), _vec_spec(d)],
        out_specs=[_row_spec(ts, d), _row_spec(ts, d)],
        out_shape=[jax.ShapeDtypeStruct((bsz, s, d), F32), jax.ShapeDtypeStruct((bsz, s, d), BF16)],
        compiler_params=_params(2),
    )(x, mixed, gate, g, scale, shift)


def _norm_mod_bwd(dh, xin, resid, g, scale, gate=None, mixed=None, *, name, ts=256):
    bsz, s, d = xin.shape
    ts = min(ts, s)
    gated = gate is not None

    def body(*refs):
        if gated:
            dh_ref, x_ref, rs_ref, g_ref, sc_ref, gt_ref, mx_ref, dx_ref, dsc_ref, dsh_ref, dg_ref, dgt_ref, dmx_ref = refs
        else:
            dh_ref, x_ref, rs_ref, g_ref, sc_ref, dx_ref, dsc_ref, dsh_ref, dg_ref = refs
        b, i = pl.program_id(0), pl.program_id(1)

        @pl.when(i == 0)
        def _():
            dsc_ref[...] = jnp.zeros_like(dsc_ref)
            dsh_ref[...] = jnp.zeros_like(dsh_ref)
            if gated:
                dgt_ref[...] = jnp.zeros_like(dgt_ref)

        @pl.when((i == 0) & (b == 0))
        def _():
            dg_ref[...] = jnp.zeros_like(dg_ref)

        dh_v, gv = dh_ref[...], g_ref[...]
        xh, r = _rms(x_ref[...], None)
        dsc_ref[...] += jnp.sum(dh_v * (xh * gv), axis=0, keepdims=True)
        dsh_ref[...] += jnp.sum(dh_v, axis=0, keepdims=True)
        dn = dh_v * (1.0 + sc_ref[...])
        dg_ref[...] += jnp.sum(dn * xh, axis=0, keepdims=True)
        dxh = dn * gv
        dx = rs_ref[...] + r * (dxh - xh * jnp.mean(dxh * xh, axis=-1, keepdims=True))
        dx_ref[...] = dx
        if gated:
            dgt_ref[...] += jnp.sum(dx * mx_ref[...], axis=0, keepdims=True)
            dmx_ref[...] = (dx * gt_ref[...]).astype(BF16)

    ins = [dh, xin, resid, g, scale]
    in_specs = [_row_spec(ts, d), _row_spec(ts, d), _row_spec(ts, d), _gain_spec(d), _vec_spec(d)]
    out_specs = [_row_spec(ts, d), _vec_spec(d), _vec_spec(d), _gain_spec(d)]
    out_shape = [jax.ShapeDtypeStruct((bsz, s, d), F32), jax.ShapeDtypeStruct((bsz, 1, d), F32),
                 jax.ShapeDtypeStruct((bsz, 1, d), F32), jax.ShapeDtypeStruct((1, d), F32)]
    if gated:
        ins += [gate, mixed]
        in_specs += [_vec_spec(d), _row_spec(ts, d)]
        out_specs += [_vec_spec(d), _row_spec(ts, d)]
        out_shape += [jax.ShapeDtypeStruct((bsz, 1, d), F32), jax.ShapeDtypeStruct((bsz, s, d), BF16)]
    return pl.pallas_call(
        body, name=name, grid=(bsz, s // ts), in_specs=in_specs, out_specs=out_specs, out_shape=out_shape,
        compiler_params=_params(2),
    )(*ins)


def _loss_head(x1, ff, gate2, target, *, name, ts=256):
    bsz, s, d = x1.shape
    ts = min(ts, s)

    def body(x1_ref, ff_ref, gt_ref, t_ref, dy_ref, dff_ref, dgt_ref, loss_ref, acc):
        b, i = pl.program_id(0), pl.program_id(1)

        @pl.when(i == 0)
        def _():
            dgt_ref[...] = jnp.zeros_like(dgt_ref)

        @pl.when((i == 0) & (b == 0))
        def _():
            acc[...] = jnp.zeros_like(acc)

        ffv, gt = ff_ref[...], gt_ref[...]
        diff = (x1_ref[...] + gt * ffv) - t_ref[...]
        acc[...] += jnp.sum((diff * diff).reshape(ts // 8, 8, d), axis=0)
        dy = diff * (1.0 / d)
        dy_ref[...] = dy
        dgt_ref[...] += jnp.sum(dy * ffv, axis=0, keepdims=True)
        dff_ref[...] = (dy * gt).astype(BF16)

        @pl.when((i == pl.num_programs(1) - 1) & (b == pl.num_programs(0) - 1))
        def _():
            loss_ref[...] = jnp.full(loss_ref.shape, jnp.sum(acc[...]), F32)

    return pl.pallas_call(
        body, name=name, grid=(bsz, s // ts),
        in_specs=[_row_spec(ts, d), _row_spec(ts, d), _vec_spec(d), _row_spec(ts, d)],
        out_specs=[_row_spec(ts, d), _row_spec(ts, d), _vec_spec(d), pl.BlockSpec((8, LANE), lambda b, i: (0, 0))],
        out_shape=[jax.ShapeDtypeStruct((bsz, s, d), F32), jax.ShapeDtypeStruct((bsz, s, d), BF16),
                   jax.ShapeDtypeStruct((bsz, 1, d), F32), jax.ShapeDtypeStruct((8, LANE), F32)],
        scratch_shapes=[pltpu.VMEM((8, d), F32)], compiler_params=_params(2),
    )(x1, ff, gate2, target)


def _merge_fwd(proj, b_merge, y_a, y_b, *, name, ts=256):
    bsz, s, _ = proj.shape
    ts = min(ts, s)

    def body(la_ref, lb_ref, ba_ref, bb_ref, ya_ref, yb_ref, out_ref):
        ga = _sigmoid(la_ref[...] + ba_ref[...])
        gb = _sigmoid(lb_ref[...] + bb_ref[...])
        out_ref[...] = (ga * ya_ref[...] + gb * yb_ref[...]).astype(BF16)

    return pl.pallas_call(
        body, name=name, grid=(bsz, s // ts),
        in_specs=[_row_spec(ts, D, OFF_MA // D), _row_spec(ts, D, OFF_MB // D),
                  pl.BlockSpec((1, D), lambda b, i: (0, 0)), pl.BlockSpec((1, D), lambda b, i: (0, 1)),
                  _row_spec(ts, D), _row_spec(ts, D)],
        out_specs=_row_spec(ts, D), out_shape=jax.ShapeDtypeStruct((bsz, s, D), BF16),
        compiler_params=_params(2),
    )(proj, proj, b_merge, b_merge, y_a, y_b)


def _merge_bwd(dmi, proj, b_merge, y_a, y_b, *, name, ts=256):
    bsz, s, _ = proj.shape
    ts = min(ts, s)

    def body(d_ref, la_ref, lb_ref, ba_ref, bb_ref, ya_ref, yb_ref, dya_ref, dyb_ref, dla_ref, dlb_ref, dba_ref, dbb_ref):
        @pl.when((pl.program_id(0) == 0) & (pl.program_id(1) == 0))
        def _():
            dba_ref[...] = jnp.zeros_like(dba_ref)
            dbb_ref[...] = jnp.zeros_like(dbb_ref)

        dv = d_ref[...]
        ga = _sigmoid(la_ref[...] + ba_ref[...])
        gb = _sigmoid(lb_ref[...] + bb_ref[...])
        dya_ref[...] = (dv * ga).astype(BF16)
        dyb_ref[...] = (dv * gb).astype(BF16)
        dla = (dv * ya_ref[...]) * (ga * (1.0 - ga))
        dlb = (dv * yb_ref[...]) * (gb * (1.0 - gb))
        dla_ref[...] = dla.astype(BF16)
        dlb_ref[...] = dlb.astype(BF16)
        dba_ref[...] += jnp.sum(dla, axis=0, keepdims=True)
        dbb_ref[...] += jnp.sum(dlb, axis=0, keepdims=True)

    act = jax.ShapeDtypeStruct((bsz, s, D), BF16)
    return pl.pallas_call(
        body, name=name, grid=(bsz, s // ts),
        in_specs=[_row_spec(ts, D), _row_spec(ts, D, OFF_MA // D), _row_spec(ts, D, OFF_MB // D),
                  pl.BlockSpec((1, D), lambda b, i: (0, 0)), pl.BlockSpec((1, D), lambda b, i: (0, 1)),
                  _row_spec(ts, D), _row_spec(ts, D)],
        out_specs=[_row_spec(ts, D)] * 4 + [_gain_spec(D)] * 2,
        out_shape=[act, act, act, act, jax.ShapeDtypeStruct((1, D), F32), jax.ShapeDtypeStruct((1, D), F32)],
        compiler_params=_params(2),
    )(dmi, proj, proj, b_merge, b_merge, y_a, y_b)


def _tri(lower):
    r = lax.broadcasted_iota(jnp.int32, (CHUNK, CHUNK), 0)
    c = lax.broadcasted_iota(jnp.int32, (CHUNK, CHUNK), 1)
    return jnp.where((c <= r) if lower else (c >= r), 1.0, 0.0).astype(F32)


def _gla_logits(a_ref, wal_ref, bal_ref):
    logits = jnp.dot(a_ref[...].astype(BF16), wal_ref[...].astype(BF16), preferred_element_type=F32) + bal_ref[...]
    la = (jnp.minimum(logits, 0.0) - jnp.log(1.0 + jnp.exp(-jnp.abs(logits)))) * (1.0 / GTAU)
    return logits, la


def _chunk_cumsum(la_n, tri):
    cum = jnp.dot(tri, la_n, preferred_element_type=F32, precision=lax.Precision.HIGHEST)
    return cum, jnp.sum(la_n, axis=0, keepdims=True)


def _gla_specs(s, nc):
    def blk(width, off):
        return pl.BlockSpec((None, s, width), lambda h, b: (b, 0, off // width + h))

    proj_specs = [blk(GDK, OFF_Q), blk(GDK, OFF_K), blk(GDV, OFF_V), blk(GDV, OFF_G),
                  pl.BlockSpec((None, s, LANE), lambda h, b: (b, 0, OFF_A // LANE)),
                  pl.BlockSpec((LANE, GDK), lambda h, b: (0, h)), pl.BlockSpec((1, GDK), lambda h, b: (0, h)),
                  pl.BlockSpec((1, GDV), lambda h, b: (0, 0))]
    st_spec = pl.BlockSpec((None, None, nc, GDV, GDK), lambda h, b: (b, h, 0, 0, 0))
    return blk, proj_specs, st_spec


def _gla_fwd(proj, w_alpha_p, b_alpha, out_norm_g, *, name):
    bsz, s, _ = proj.shape
    nc = s // CHUNK
    scale = GDK ** -0.5

    rb = min(512, s)

    def body(q_ref, k_ref, v_ref, g_ref, a_ref, wal_ref, bal_ref, ong_ref, o_ref, og_ref, st_ref):
        _, la = _gla_logits(a_ref, wal_ref, bal_ref)
        tri = _tri(True)
        st = jnp.zeros((GDV, GDK), F32)
        for n in range(nc):
            rows = pl.ds(n * CHUNK, CHUNK)
            cum, cum_end = _chunk_cumsum(la[n * CHUNK:(n + 1) * CHUNK], tri)
            kd = k_ref[rows, :] * jnp.exp(cum_end - cum)
            ut = lax.dot_general(v_ref[rows, :].astype(BF16), kd.astype(BF16), _TN, preferred_element_type=F32)
            st = st * jnp.exp(cum_end) + ut
            st_ref[n] = st
            o_ref[rows, :] = lax.dot_general((q_ref[rows, :] * scale).astype(BF16), st.astype(BF16), _NT,
                                             preferred_element_type=F32)
        for j in range(0, s, rb):
            blk_rows = pl.ds(j, rb)
            oh, _ = _rms(o_ref[blk_rows, :], None)
            gv = g_ref[blk_rows, :]
            og_ref[blk_rows, :] = ((oh * ong_ref[...]) * (gv * _sigmoid(gv))).astype(BF16)

    blk, proj_specs, st_spec = _gla_specs(s, nc)
    return pl.pallas_call(
        body, name=name, grid=(GH, bsz), in_specs=proj_specs, out_specs=[blk(GDV, 0), blk(GDV, 0), st_spec],
        out_shape=[jax.ShapeDtypeStruct((bsz, s, GH * GDV), F32), jax.ShapeDtypeStruct((bsz, s, GH * GDV), BF16),
                   jax.ShapeDtypeStruct((bsz, GH, nc, GDV, GDK), F32)],
        compiler_params=_params(2),
    )(proj, proj, proj, proj, proj, w_alpha_p, b_alpha, out_norm_g)


def _gla_bwd(dog, o, states, proj, w_alpha_p, b_alpha, out_norm_g, *, name):
    bsz, s, _ = proj.shape
    nc = s // CHUNK
    scale = GDK ** -0.5

    def body(dog_ref, o_ref, st_ref, q_ref, k_ref, v_ref, g_ref, a_ref, wal_ref, bal_ref, ong_ref,
             dq_ref, dk_ref, dv_ref, dg_ref, dl_ref, dbal_ref, dong_ref, do_scr, dlog_scr):
        h, b = pl.program_id(0), pl.program_id(1)

        @pl.when(b == 0)
        def _():
            dbal_ref[...] = jnp.zeros_like(dbal_ref)

        @pl.when((b == 0) & (h == 0))
        def _():
            dong_ref[...] = jnp.zeros_like(dong_ref)

        ong = ong_ref[...]
        for j in range(0, s, rb):
            blk_rows = pl.ds(j, rb)
            gv, dogv = g_ref[blk_rows, :], dog_ref[blk_rows, :]
            sg = _sigmoid(gv)
            oh, r = _rms(o_ref[blk_rows, :], None)
            don = dogv * (gv * sg)
            dg_ref[blk_rows, :] = (dogv * (oh * ong) * (sg * (1.0 + gv * (1.0 - sg)))).astype(BF16)
            dong_ref[...] += jnp.sum(don * oh, axis=0, keepdims=True)
            doh = don * ong
            do_scr[blk_rows, :] = (r * (doh - oh * jnp.mean(doh * oh, axis=-1, keepdims=True))).astype(BF16)

        logits, la = _gla_logits(a_ref, wal_ref, bal_ref)
        tri_lo, tri_up = _tri(True), _tri(False)
        carry = jnp.zeros((GDV, GDK), F32)
        for n in range(nc - 1, -1, -1):
            rows = pl.ds(n * CHUNK, CHUNK)
            cum, cum_end = _chunk_cumsum(la[n * CHUNK:(n + 1) * CHUNK], tri_lo)
            decay = jnp.exp(cum_end)
            w = jnp.exp(cum_end - cum)
            kd = k_ref[rows, :] * w
            do_b = do_scr[rows, :]
            qs_b = (q_ref[rows, :] * scale).astype(BF16)
            dq_ref[rows, :] = (jnp.dot(do_b, st_ref[n].astype(BF16), preferred_element_type=F32) * scale).astype(BF16)
            dsn = lax.dot_general(do_b, qs_b, _TN, preferred_element_type=F32) + carry
            carry = dsn * decay
            dsn_b = dsn.astype(BF16)
            dv_ref[rows, :] = lax.dot_general(kd.astype(BF16), dsn_b, _NT, preferred_element_type=F32).astype(BF16)
            dkd = jnp.dot(v_ref[rows, :].astype(BF16), dsn_b, preferred_element_type=F32)
            dk_ref[rows, :] = (dkd * w).astype(BF16)
            e = dkd * kd
            dcum_end = jnp.sum(e, axis=0, keepdims=True)
            if n > 0:
                dcum_end += jnp.sum(dsn * st_ref[n - 1], axis=0, keepdims=True) * decay
            dlog_scr[rows, :] = dcum_end - jnp.dot(tri_up, e, preferred_element_type=F32,
                                                  precision=lax.Precision.HIGHEST)
        dlog = dlog_scr[...] * (1.0 / GTAU) * (1.0 - _sigmoid(logits))
        dl_ref[...] = dlog.astype(BF16)
        dbal_ref[...] += jnp.sum(dlog, axis=0, keepdims=True)

    rb = min(512, s)

    blk, proj_specs, st_spec = _gla_specs(s, nc)
    act = lambda wd: jax.ShapeDtypeStruct((bsz, s, wd), BF16)
    return pl.pallas_call(
        body, name=name, grid=(GH, bsz), in_specs=[blk(GDV, 0), blk(GDV, 0), st_spec, *proj_specs],
        out_specs=[blk(GDK, 0), blk(GDK, 0), blk(GDV, 0), blk(GDV, 0), blk(GDK, 0),
                   pl.BlockSpec((1, GDK), lambda h, b: (0, h)), pl.BlockSpec((1, GDV), lambda h, b: (0, 0))],
        out_shape=[act(GH * GDK), act(GH * GDK), act(GH * GDV), act(GH * GDV), act(GH * GDK),
                   jax.ShapeDtypeStruct((1, GH * GDK), F32), jax.ShapeDtypeStruct((1, GDV), F32)],
        scratch_shapes=[pltpu.VMEM((s, GDV), BF16), pltpu.VMEM((s, GDK), F32)], compiler_params=_params(2),
    )(dog, o, states, proj, proj, proj, proj, proj, w_alpha_p, b_alpha, out_norm_g)


def _lane():
    return lax.broadcasted_iota(jnp.int32, (1, LANE), 1)


def _swap_halves(x):
    lane = _lane()
    half = MROPE // 2
    lo = (lane >= MNOPE) & (lane < MNOPE + half)
    hi = (lane >= MNOPE + half) & (lane < MQK)
    return jnp.where(lo, pltpu.roll(x, LANE - half, 1), jnp.where(hi, pltpu.roll(x, half, 1), 0.0))


def _norm96(x, g):
    r = lax.rsqrt(jnp.sum(x * x, axis=-1, keepdims=True) * (1.0 / MQK) + EPS)
    return x * r, r


def _lat_norm(proj, q_lat_g, kv_lat_g, *, name, ts=512):
    t = proj.shape[0]
    ts = min(ts, t)

    def body(cq_ref, ckv_ref, gq_ref, gk_ref, oq_ref, ok_ref):
        xq, _ = _rms(cq_ref[...], None)
        oq_ref[...] = (xq * gq_ref[...]).astype(BF16)
        xk, _ = _rms(ckv_ref[...], None)
        ok_ref[...] = (xk * gk_ref[...]).astype(BF16)

    return pl.pallas_call(
        body, name=name, grid=(t // ts,),
        in_specs=[pl.BlockSpec((ts, MQR), lambda i: (i, OFF_CQ // MQR)), pl.BlockSpec((ts, MKVR), lambda i: (i, OFF_CKV // MKVR)),
                  pl.BlockSpec((1, MQR), lambda i: (0, 0)), pl.BlockSpec((1, MKVR), lambda i: (0, 0))],
        out_specs=[pl.BlockSpec((ts, MQR), lambda i: (i, 0)), pl.BlockSpec((ts, MKVR), lambda i: (i, 0))],
        out_shape=[jax.ShapeDtypeStruct((t, MQR), BF16), jax.ShapeDtypeStruct((t, MKVR), BF16)],
        compiler_params=_params(1),
    )(proj, proj, q_lat_g, kv_lat_g)


def _lat_norm_bwd(dcqn, dckvn, proj, q_lat_g, kv_lat_g, *, name, ts=512):
    t = proj.shape[0]
    ts = min(ts, t)

    def one(d_ref, x_ref, g_ref, dx_ref, dg_ref):
        xh, r = _rms(x_ref[...], None)
        dn = d_ref[...]
        dg_ref[...] += jnp.sum(dn * xh, axis=0, keepdims=True)
        dxh = dn * g_ref[...]
        dx_ref[...] = (r * (dxh - xh * jnp.mean(dxh * xh, axis=-1, keepdims=True))).astype(BF16)

    def body(dq_ref, dk_ref, cq_ref, ckv_ref, gq_ref, gk_ref, dxq_ref, dxk_ref, dgq_ref, dgk_ref):
        @pl.when(pl.program_id(0) == 0)
        def _():
            dgq_ref[...] = jnp.zeros_like(dgq_ref)
            dgk_ref[...] = jnp.zeros_like(dgk_ref)

        one(dq_ref, cq_ref, gq_ref, dxq_ref, dgq_ref)
        one(dk_ref, ckv_ref, gk_ref, dxk_ref, dgk_ref)

    return pl.pallas_call(
        body, name=name, grid=(t // ts,),
        in_specs=[pl.BlockSpec((ts, MQR), lambda i: (i, 0)), pl.BlockSpec((ts, MKVR), lambda i: (i, 0)),
                  pl.BlockSpec((ts, MQR), lambda i: (i, OFF_CQ // MQR)), pl.BlockSpec((ts, MKVR), lambda i: (i, OFF_CKV // MKVR)),
                  pl.BlockSpec((1, MQR), lambda i: (0, 0)), pl.BlockSpec((1, MKVR), lambda i: (0, 0))],
        out_specs=[pl.BlockSpec((ts, MQR), lambda i: (i, 0)), pl.BlockSpec((ts, MKVR), lambda i: (i, 0)),
                   pl.BlockSpec((1, MQR), lambda i: (0, 0)), pl.BlockSpec((1, MKVR), lambda i: (0, 0))],
        out_shape=[jax.ShapeDtypeStruct((t, MQR), BF16), jax.ShapeDtypeStruct((t, MKVR), BF16),
                   jax.ShapeDtypeStruct((1, MQR), F32), jax.ShapeDtypeStruct((1, MKVR), F32)],
        compiler_params=_params(1),
    )(dcqn, dckvn, proj, proj, q_lat_g, kv_lat_g)


def _qk_prep(q_raw, kv, proj, cos_t, sin_t, gq, gk, *, name, ts=512):
    t = q_raw.shape[0]
    ts = min(ts, t)

    def body(q_ref, kv_ref, kpe_ref, c_ref, s_ref, gq_ref, gk_ref, qo_ref, ko_ref, vo_ref):
        cs, sn = c_ref[...], s_ref[...]
        nope = _lane() < MNOPE
        qn, _ = _norm96(q_ref[...], None)
        qn = qn * gq_ref[...]
        qo_ref[...] = (qn * cs + _swap_halves(qn) * sn).astype(BF16)
        kvv = kv_ref[...]
        kn, _ = _norm96(jnp.where(nope, kvv, kpe_ref[...]), None)
        kn = kn * gk_ref[...]
        ko_ref[...] = (kn * cs + _swap_halves(kn) * sn).astype(BF16)
        vo_ref[...] = jnp.where(nope, pltpu.roll(kvv, MNOPE, 1), 0.0).astype(BF16)

    hd = pl.BlockSpec((ts, LANE), lambda i, h: (i, h))
    shared = lambda col: pl.BlockSpec((ts, LANE), lambda i, h: (i, col))
    gain = pl.BlockSpec((1, LANE), lambda i, h: (0, 0))
    out = jax.ShapeDtypeStruct((t, MH * LANE), BF16)
    return pl.pallas_call(
        body, name=name, grid=(t // ts, MH),
        in_specs=[hd, hd, shared(OFF_KPE // LANE), shared(0), shared(0), gain, gain],
        out_specs=[hd, hd, hd], out_shape=[out, out, out], compiler_params=_params(2),
    )(q_raw, kv, proj, cos_t, sin_t, gq, gk)


def _qk_prep_bwd(dq, dk, dv, q_raw, kv, proj, cos_t, sin_t, gq, gk, *, name, ts=512):
    t = q_raw.shape[0]
    ts = min(ts, t)

    def norm_bwd(dy, x, g, dg_ref):
        xh, r = _norm96(x, None)
        dg_ref[...] += jnp.sum(dy * xh, axis=0, keepdims=True)
        dxh = dy * g
        return r * (dxh - xh * (jnp.sum(dxh * xh, axis=-1, keepdims=True) * (1.0 / MQK)))

    def body(dq_ref, dk_ref, dv_ref, q_ref, kv_ref, kpe_ref, c_ref, s_ref, gq_ref, gk_ref,
             dqr_ref, dkv_ref, dkpe_ref, dgq_ref, dgk_ref):
        i, h = pl.program_id(0), pl.program_id(1)

        @pl.when(h == 0)
        def _():
            dkpe_ref[...] = jnp.zeros_like(dkpe_ref)

        @pl.when((h == 0) & (i == 0))
        def _():
            dgq_ref[...] = jnp.zeros_like(dgq_ref)
            dgk_ref[...] = jnp.zeros_like(dgk_ref)

        cs, sn = c_ref[...], s_ref[...]
        lane = _lane()
        nope = lane < MNOPE
        dqv = dq_ref[...]
        dqn = dqv * cs + _swap_halves(dqv * sn)
        dqr_ref[...] = norm_bwd(dqn, q_ref[...], gq_ref[...], dgq_ref).astype(BF16)
        dkv_ = dk_ref[...]
        dkn = dkv_ * cs + _swap_halves(dkv_ * sn)
        kvv = kv_ref[...]
        dkr = norm_bwd(dkn, jnp.where(nope, kvv, kpe_ref[...]), gk_ref[...], dgk_ref)
        dkv_ref[...] = jnp.where(nope, dkr, pltpu.roll(dv_ref[...], MNOPE, 1)).astype(BF16)
        dkpe_ref[...] += jnp.where((lane >= MNOPE) & (lane < MQK), dkr, 0.0)

    hd = pl.BlockSpec((ts, LANE), lambda i, h: (i, h))
    shared = lambda col: pl.BlockSpec((ts, LANE), lambda i, h: (i, col))
    gain = pl.BlockSpec((1, LANE), lambda i, h: (0, 0))
    out = jax.ShapeDtypeStruct((t, MH * LANE), BF16)
    return pl.pallas_call(
        body, name=name, grid=(t // ts, MH),
        in_specs=[hd, hd, hd, hd, hd, shared(OFF_KPE // LANE), shared(0), shared(0), gain, gain],
        out_specs=[hd, hd, shared(0), gain, gain],
        out_shape=[out, out, jax.ShapeDtypeStruct((t, LANE), F32), jax.ShapeDtypeStruct((1, LANE), F32),
                   jax.ShapeDtypeStruct((1, LANE), F32)],
        compiler_params=_params(2),
    )(dq, dk, dv, q_raw, kv, proj, cos_t, sin_t, gq, gk)


_NT = (((1,), (1,)), ((), ()))
_TN = (((0,), (0,)), ((), ()))


SOFTMAX_SCALE = MQK ** -0.5
Q_PRESCALE = SOFTMAX_SCALE * float(np.log2(np.e))


def _attn_weights(q, k_ref, lo, tq):
    row = lax.broadcasted_iota(jnp.int32, (tq, tq), 0) // CHUNK
    col = lax.broadcasted_iota(jnp.int32, (tq, tq), 1) // CHUNK
    sd = lax.dot_general(q, k_ref[pl.ds(lo, tq), :], _NT, preferred_element_type=F32)
    sd = jnp.where(col <= row, sd, -1e30)
    m = jnp.max(sd, axis=-1, keepdims=True)
    if lo:
        so = lax.dot_general(q, k_ref[pl.ds(0, lo), :], _NT, preferred_element_type=F32)
        m = jnp.maximum(m, jnp.max(so, axis=-1, keepdims=True))
        eo = jnp.exp2(so - m)
        ed = jnp.exp2(sd - m)
        return eo, ed, 1.0 / (jnp.sum(eo, axis=-1, keepdims=True) + jnp.sum(ed, axis=-1, keepdims=True))
    ed = jnp.exp2(sd - m)
    return None, ed, 1.0 / jnp.sum(ed, axis=-1, keepdims=True)


def _attn_fwd(q, k, v, *, name, tq=256):
    bsz, s, _ = q.shape
    tq = min(tq, s)

    def body(q_ref, k_ref, v_ref, o_ref):
        for i in range(s // tq):
            lo = i * tq
            eo, ed, inv = _attn_weights(q_ref[pl.ds(lo, tq), :], k_ref, lo, tq)
            o = jnp.dot(ed.astype(BF16), v_ref[pl.ds(lo, tq), :], preferred_element_type=F32)
            if lo:
                o += jnp.dot(eo.astype(BF16), v_ref[pl.ds(0, lo), :], preferred_element_type=F32)
            o_ref[pl.ds(lo, tq), :] = (o * inv).astype(BF16)

    spec = pl.BlockSpec((None, s, LANE), lambda b, h: (b, 0, h))
    return pl.pallas_call(
        body, name=name, grid=(bsz, MH), in_specs=[spec, spec, spec], out_specs=spec,
        out_shape=jax.ShapeDtypeStruct((bsz, s, MH * LANE), BF16), compiler_params=_params(2),
    )(q, k, v)


def _attn_bwd(q, k, v, do, *, name, tq=256):
    bsz, s, _ = q.shape
    tq = min(tq, s)

    def body(q_ref, k_ref, v_ref, do_ref, dq_ref, dk_ref, dv_ref):
        dk_ref[...] = jnp.zeros_like(dk_ref)
        dv_ref[...] = jnp.zeros_like(dv_ref)
        for i in range(s // tq):
            lo = i * tq
            here, before = pl.ds(lo, tq), pl.ds(0, lo)
            qv, dov = q_ref[here, :], do_ref[here, :]
            eo, ed, inv = _attn_weights(qv, k_ref, lo, tq)
            do_n = (dov.astype(F32) * inv).astype(BF16)
            dv_ref[here, :] += lax.dot_general(ed.astype(BF16), do_n, _TN, preferred_element_type=F32)
            dpd = lax.dot_general(dov, v_ref[here, :], _NT, preferred_element_type=F32)
            delta = jnp.sum(dpd * ed, axis=-1, keepdims=True)
            if lo:
                dv_ref[before, :] += lax.dot_general(eo.astype(BF16), do_n, _TN, preferred_element_type=F32)
                dpo = lax.dot_general(dov, v_ref[before, :], _NT, preferred_element_type=F32)
                delta += jnp.sum(dpo * eo, axis=-1, keepdims=True)
            delta = delta * inv
            r = inv * SOFTMAX_SCALE
            dsd = (ed * (dpd - delta) * r).astype(BF16)
            dq = jnp.dot(dsd, k_ref[here, :], preferred_element_type=F32)
            dk_ref[here, :] += lax.dot_general(dsd, qv, _TN, preferred_element_type=F32)
            if lo:
                dso = (eo * (dpo - delta) * r).astype(BF16)
                dq += jnp.dot(dso, k_ref[before, :], preferred_element_type=F32)
                dk_ref[before, :] += lax.dot_general(dso, qv, _TN, preferred_element_type=F32)
            dq_ref[here, :] = dq
        dk_ref[...] = dk_ref[...] * (1.0 / Q_PRESCALE)

    spec = pl.BlockSpec((None, s, LANE), lambda b, h: (b, 0, h))
    out = jax.ShapeDtypeStruct((bsz, s, MH * LANE), F32)
    return pl.pallas_call(
        body, name=name, grid=(bsz, MH), in_specs=[spec] * 4, out_specs=[spec] * 3, out_shape=[out, out, out],
        compiler_params=_params(2),
    )(q, k, v, do)


def _adamw(w, g, m, v, *, name, tr=256):
    rows, cols = w.shape
    tr = _tile_rows(rows, tr)

    def body(w_ref, g_ref, m_ref, v_ref, d_ref, nm_ref, nv_ref):
        d_ref[...], nm_ref[...], nv_ref[...] = _adamw_update(w_ref[...], g_ref[...], m_ref[...], v_ref[...])

    spec = pl.BlockSpec((tr, cols), lambda i: (i, 0))
    out = jax.ShapeDtypeStruct((rows, cols), F32)
    return pl.pallas_call(body, name=name, grid=(rows // tr,), in_specs=[spec] * 4, out_specs=[spec] * 3,
                          out_shape=[out, out, out], compiler_params=_params(1))(w, g, m, v)


def _tile_rows(rows, target):
    if rows <= target:
        return rows
    best = 8
    for t in range(8, target + 1, 8):
        if rows % t == 0:
            best = t
    return best


def _adamw_update(w, g, m, v):
    nm = ADAM_B1 * m + (1.0 - ADAM_B1) * g
    nv = ADAM_B2 * v + (1.0 - ADAM_B2) * (g * g)
    m_hat = nm / (1.0 - ADAM_B1 ** ADAM_STEP)
    v_hat = nv / (1.0 - ADAM_B2 ** ADAM_STEP)
    return -ADAM_LR * (m_hat / (jnp.sqrt(v_hat) + ADAM_EPS) + ADAM_WD * w), nm, nv


def _adamw_halves(w, m, v, mine, theirs, sel, *, name, tr=256):
    rows, cols = w.shape
    tr = _tile_rows(rows // 2, tr)
    nh = rows // 2 // tr

    def body(sel_ref, w_ref, m_ref, v_ref, mine_ref, theirs_ref, g_ref, d_ref, nm_ref, nv_ref):
        lower = pl.program_id(0) < nh
        south = sel_ref[0] == 0
        gv = jnp.where(lower == south, mine_ref[...], theirs_ref[...])
        g_ref[...] = gv
        d_ref[...], nm_ref[...], nv_ref[...] = _adamw_update(w_ref[...], gv, m_ref[...], v_ref[...])

    full = pl.BlockSpec((tr, cols), lambda i, sel_ref: (i, 0))
    half = pl.BlockSpec((tr, cols), lambda i, sel_ref: (i % nh, 0))
    out = jax.ShapeDtypeStruct((rows, cols), F32)
    return pl.pallas_call(
        body, name=name, out_shape=[out] * 4, compiler_params=_params(1),
        grid_spec=pltpu.PrefetchScalarGridSpec(num_scalar_prefetch=1, grid=(rows // tr,),
                                               in_specs=[full, full, full, half, half], out_specs=[full] * 4),
    )(sel, w, m, v, mine, theirs)


def _pair_add(x, sib, sel, *, name, tr=256):
    n, _, rows, cols = x.shape
    tr = _tile_rows(rows, tr)

    def body(sel_ref, x_ref, s_ref, o_ref):
        o_ref[...] = (x_ref[...] + s_ref[...]).astype(BF16)

    spec = pl.BlockSpec((None, tr, cols), lambda j, i, sel_ref: (j, i, 0))
    return pl.pallas_call(
        body, name=name, out_shape=jax.ShapeDtypeStruct((n, rows, cols), BF16), compiler_params=_params(2),
        grid_spec=pltpu.PrefetchScalarGridSpec(
            num_scalar_prefetch=1, grid=(n, rows // tr),
            in_specs=[pl.BlockSpec((None, None, tr, cols), lambda j, i, sel_ref: (j, sel_ref[0], i, 0)), spec],
            out_specs=spec),
    )(sel, x, sib)


def _chip_sum(pair, recv, sel, *, name, tr=256):
    _, rows, cols = pair.shape
    tr = _tile_rows(rows, tr)

    def body(sel_ref, p_ref, r_ref, o_ref):
        acc = p_ref[...].astype(F32)
        for k in range(3):
            acc = acc + r_ref[k].astype(F32)
        o_ref[...] = acc

    return pl.pallas_call(
        body, name=name, out_shape=jax.ShapeDtypeStruct((rows, cols), F32), compiler_params=_params(1),
        grid_spec=pltpu.PrefetchScalarGridSpec(
            num_scalar_prefetch=1, grid=(rows // tr,),
            in_specs=[pl.BlockSpec((None, tr, cols), lambda i, sel_ref: (sel_ref[0], i, 0)),
                      pl.BlockSpec((3, tr, cols), lambda i, sel_ref: (0, i, 0))],
            out_specs=pl.BlockSpec((tr, cols), lambda i, sel_ref: (i, 0))),
    )(sel, pair, recv)


def _me():
    return lax.axis_index("x"), lax.axis_index("y"), lax.axis_index("c")


def _flip(pos, bits):
    x, y, c = pos
    return (x ^ bits[0] if bits[0] else x, y ^ bits[1] if bits[1] else y, c ^ bits[2] if bits[2] else c)


ANY = pl.BlockSpec(memory_space=pl.ANY)


def _all_gather8(x, *, name):
    flips = [((k >> 2) & 1, (k >> 1) & 1, k & 1) for k in range(1, 8)]

    def body(x_ref, out_ref, send_sems, recv_sems, local_sem):
        me = _me()
        slot = lambda p: 4 * p[0] + 2 * p[1] + p[2]
        mine = pltpu.make_async_copy(x_ref, out_ref.at[slot(me)], local_sem)
        mine.start()
        sends = []
        for k, f in enumerate(flips):
            cp = pltpu.make_async_remote_copy(src_ref=x_ref, dst_ref=out_ref.at[slot(me)], send_sem=send_sems.at[k],
                                              recv_sem=recv_sems.at[k], device_id=_flip(me, f), device_id_type=MESH)
            cp.start()
            sends.append(cp)
        for k, f in enumerate(flips):
            peer = _flip(me, f)
            pltpu.make_async_remote_copy(src_ref=x_ref, dst_ref=out_ref.at[slot(peer)], send_sem=send_sems.at[k],
                                         recv_sem=recv_sems.at[k], device_id=peer, device_id_type=MESH).wait_recv()
        for cp in sends:
            cp.wait_send()
        mine.wait()

    return pl.pallas_call(
        body, name=name, in_specs=[ANY], out_specs=ANY, out_shape=jax.ShapeDtypeStruct((8, *x.shape), x.dtype),
        scratch_shapes=[pltpu.SemaphoreType.DMA((7,)), pltpu.SemaphoreType.DMA((7,)), pltpu.SemaphoreType.DMA])(x)


CHIP_FLIPS = [(1, 0, 0), (0, 1, 0), (1, 1, 0)]


def _chip():
    return 2 * lax.axis_index("x") + lax.axis_index("y")


def _gather_weights(xs, *, name):
    n = len(xs)
    halves = [x.reshape(2, x.shape[0] // 2, x.shape[1]) for x in xs]

    def body(*refs):
        x_refs, out_refs, (send_sems, recv_sems) = refs[:n], refs[n:2 * n], refs[2 * n:]
        me = _me()
        sib = _flip(me, (0, 0, 1))
        slot = lambda p: 2 * p[0] + p[1]
        my_half, their_half = me[2], 1 - me[2]

        def copy(k, src, dst, to):
            return pltpu.make_async_remote_copy(src_ref=src, dst_ref=dst, send_sem=send_sems.at[k],
                                                recv_sem=recv_sems.at[k], device_id=to, device_id_type=MESH)

        sends = []
        for i in range(n):
            for k, f in enumerate(CHIP_FLIPS):
                cp = copy(6 * i + k, x_refs[i].at[my_half], out_refs[i].at[slot(me), my_half], _flip(me, f))
                cp.start()
                sends.append(cp)
        for i in range(n):
            for k, f in enumerate(CHIP_FLIPS):
                landed = out_refs[i].at[slot(_flip(me, f)), my_half]
                copy(6 * i + k, landed, landed, me).wait_recv()
                cp = copy(6 * i + 3 + k, landed, landed, sib)
                cp.start()
                sends.append(cp)
        for i in range(n):
            for k, f in enumerate(CHIP_FLIPS):
                from_sib = out_refs[i].at[slot(_flip(me, f)), their_half]
                copy(6 * i + 3 + k, from_sib, from_sib, sib).wait_recv()
        for cp in sends:
            cp.wait_send()

    outs = pl.pallas_call(
        body, name=name, in_specs=[ANY] * n, out_specs=[ANY] * n,
        out_shape=[jax.ShapeDtypeStruct((4, *h.shape), h.dtype) for h in halves],
        scratch_shapes=[pltpu.SemaphoreType.DMA((6 * n,)), pltpu.SemaphoreType.DMA((6 * n,))])(*halves)
    chip = _chip()
    return [[jnp.where(chip == j, x, o.reshape(4, *x.shape)[j]) for j in range(4)] for o, x in zip(outs, xs)]


def _pair_swap_halves(xs, *, name):
    n = len(xs)

    def body(*refs):
        x_refs, out_refs, (send_sems, recv_sems) = refs[:n], refs[n:2 * n], refs[2 * n:]
        me = _me()
        sib = _flip(me, (0, 0, 1))
        copies = [pltpu.make_async_remote_copy(src_ref=x_refs[i].at[:, 1 - me[2]], dst_ref=out_refs[i],
                                               send_sem=send_sems.at[i], recv_sem=recv_sems.at[i], device_id=sib,
                                               device_id_type=MESH) for i in range(n)]
        for cp in copies:
            cp.start()
        for cp in copies:
            cp.wait()

    return pl.pallas_call(
        body, name=name, in_specs=[ANY] * n, out_specs=[ANY] * n,
        out_shape=[jax.ShapeDtypeStruct((x.shape[0], *x.shape[2:]), x.dtype) for x in xs],
        scratch_shapes=[pltpu.SemaphoreType.DMA((n,)), pltpu.SemaphoreType.DMA((n,))])(*xs)


def _scatter_chips(ps, *, name):
    n = len(ps)

    def body(*refs):
        p_refs, out_refs, (send_sems, recv_sems) = refs[:n], refs[n:2 * n], refs[2 * n:]
        me = _me()
        slot = lambda q: 2 * q[0] + q[1]
        sends = []
        for i in range(n):
            for k, f in enumerate(CHIP_FLIPS):
                peer = _flip(me, f)
                cp = pltpu.make_async_remote_copy(src_ref=p_refs[i].at[slot(peer)], dst_ref=out_refs[i].at[k],
                                                  send_sem=send_sems.at[3 * i + k], recv_sem=recv_sems.at[3 * i + k],
                                                  device_id=peer, device_id_type=MESH)
                cp.start()
                sends.append(cp)
        for cp in sends:
            cp.wait()

    return pl.pallas_call(
        body, name=name, in_specs=[ANY] * n, out_specs=[ANY] * n,
        out_shape=[jax.ShapeDtypeStruct((3, *p.shape[1:]), p.dtype) for p in ps],
        scratch_shapes=[pltpu.SemaphoreType.DMA((3 * n,)), pltpu.SemaphoreType.DMA((3 * n,))])(*ps)


def _pair_swap(hs, *, name):
    n = len(hs)

    def body(*refs):
        h_refs, out_refs, (send_sems, recv_sems) = refs[:n], refs[n:2 * n], refs[2 * n:]
        sib = _flip(_me(), (0, 0, 1))
        copies = [pltpu.make_async_remote_copy(src_ref=h_refs[i], dst_ref=out_refs[i], send_sem=send_sems.at[i],
                                               recv_sem=recv_sems.at[i], device_id=sib, device_id_type=MESH)
                  for i in range(n)]
        for cp in copies:
            cp.start()
        for cp in copies:
            cp.wait()

    return pl.pallas_call(
        body, name=name, in_specs=[ANY] * n, out_specs=[ANY] * n,
        out_shape=[jax.ShapeDtypeStruct(h.shape, h.dtype) for h in hs],
        scratch_shapes=[pltpu.SemaphoreType.DMA((n,)), pltpu.SemaphoreType.DMA((n,))])(*hs)


BIG = (("w_in", (D, IN_WIDTH // 4), 1), ("gla_w_o", (D // 4, D), 0), ("mla_w_uq", (MQR, MH * MQK // 4), 1),
       ("mla_w_ukv", (MKVR, MH * (MNOPE + MVD) // 4), 1), ("mla_w_o", (D // 4, D), 0), ("w_out", (D // 4, D), 0),
       ("mlp_w1", (D, DFF // 4), 1), ("mlp_w2", (DFF // 4, D), 0))
ADA_SHARD = (D, 6 * D // 4)
SMALL = (("b_ada", 6 * D), ("norm1_g", D), ("b_merge", 2 * D), ("gla_b_alpha", GH * GDK), ("gla_out_norm_g", GDV),
         ("mla_q_lat_g", MQR), ("mla_kv_lat_g", MKVR), ("mla_qn_g", MQK), ("mla_kn_g", MQK), ("norm2_g", D))


SMALL_ROWS, SMALL_COLS = 32, 2 * D
W_ALPHA_ROW = 16
SMALL_RED = tuple((n, k) for n, k in SMALL if n != "b_ada")


def _pack_small(grads, d_w_alpha, *, name):
    def body(*refs):
        g_refs, wa_ref, out_ref = refs[:-2], refs[-2], refs[-1]
        out_ref[...] = jnp.zeros_like(out_ref)
        for i, ((_, k), g_ref) in enumerate(zip(SMALL_RED, g_refs)):
            out_ref[i:i + 1, 0:k] = g_ref[...]
        out_ref[W_ALPHA_ROW:W_ALPHA_ROW + GLR, 0:GH * GDK] = wa_ref[...]

    return pl.pallas_call(body, name=name, out_shape=jax.ShapeDtypeStruct((SMALL_ROWS, SMALL_COLS), F32))(*grads, d_w_alpha)


def _small_update(gathered, dmod_all, sel, wmv, *, name):
    names = [n for n, _ in SMALL] + ["gla_w_alpha"]
    n_par = len(names)

    def body(sel_ref, g_ref, dmod_ref, *refs):
        in_refs, out_refs, acc = refs[:3 * n_par], refs[3 * n_par:-1], refs[-1]
        total = g_ref[0]
        for j in range(1, 8):
            total = total + g_ref[j]
        acc[...] = total
        row = {n: i for i, (n, _) in enumerate(SMALL_RED)}
        for p, name_p in enumerate(names):
            w_ref, m_ref, v_ref = in_refs[3 * p:3 * p + 3]
            if name_p == "b_ada":
                gv = jnp.sum(dmod_ref[...], axis=0, keepdims=True)
            elif name_p == "gla_w_alpha":
                gv = jnp.zeros((GLR, GDK), F32)
                for j in range(4):
                    blk = acc[W_ALPHA_ROW:W_ALPHA_ROW + GLR, j * GDK:(j + 1) * GDK]
                    gv = gv + jnp.where(sel_ref[0] == j, blk, 0.0)
            else:
                gv = acc[row[name_p]:row[name_p] + 1, 0:w_ref.shape[1]]
            o = out_refs[4 * p:4 * p + 4]
            o[0][...] = gv
            o[1][...], o[2][...], o[3][...] = _adamw_update(w_ref[...], gv, m_ref[...], v_ref[...])

    flat = [a for t in wmv for a in t]
    out_shape = [jax.ShapeDtypeStruct(t[0].shape, F32) for t in wmv for _ in range(4)]
    vmem = pl.BlockSpec(memory_space=pltpu.VMEM)
    outs = pl.pallas_call(
        body, name=name, out_shape=out_shape, in_specs=[pl.BlockSpec(memory_space=pltpu.SMEM), vmem, vmem] + [vmem] * len(flat),
        out_specs=[vmem] * len(out_shape), scratch_shapes=[pltpu.VMEM((SMALL_ROWS, SMALL_COLS), F32)],
    )(sel, gathered, dmod_all, *flat)
    return {n: tuple(outs[4 * p:4 * p + 4]) for p, n in enumerate(names)}


def _full_weights(gathered):
    w = {name: jnp.concatenate(gathered[name], axis=axis) for name, _, axis in BIG}
    wi = w["w_in"]
    zeros = lambda n: jnp.zeros((D, n), wi.dtype)
    w["w_in"] = jnp.concatenate(
        [wi[:, :3072], wi[:, 3504:5552], wi[:, 3088:3344], wi[:, 3344:3472], wi[:, 3072:3088], zeros(LANE - GLR),
         zeros(MNOPE), wi[:, 3472:3504], zeros(LANE - MQK)], axis=1)
    w["mla_w_uq"] = jnp.pad(w["mla_w_uq"].reshape(MQR, MH, MQK), ((0, 0), (0, 0), (0, LANE - MQK))).reshape(MQR, MH * LANE)
    w["mla_w_o"] = jnp.pad(w["mla_w_o"].reshape(MH, MVD, D), ((0, 0), (0, LANE - MVD), (0, 0))).reshape(MH * LANE, D)
    return w


def _grad_slots(g):
    gi = g["w_in"]
    g = dict(g)
    g["w_in"] = jnp.concatenate(
        [gi[:, :3072], gi[:, OFF_A:OFF_A + GLR], gi[:, OFF_CQ:OFF_CQ + MQR], gi[:, OFF_CKV:OFF_CKV + MKVR],
         gi[:, OFF_KPE + MNOPE:OFF_KPE + MQK], gi[:, OFF_MA:OFF_MA + 2 * D]], axis=1)
    g["mla_w_uq"] = g["mla_w_uq"].reshape(MQR, MH, LANE)[:, :, :MQK].reshape(MQR, MH * MQK)
    g["mla_w_o"] = g["mla_w_o"].reshape(MH, LANE, D)[:, :MVD].reshape(MH * MVD, D)
    out = {}
    for name, (rows, cols), axis in BIG:
        a = g[name]
        a = a.reshape(4, rows, cols) if axis == 0 else jnp.transpose(a.reshape(rows, 4, cols), (1, 0, 2))
        out[name] = a.reshape(4, 2, rows // 2, cols)
    return out


def _rope_tables(positions):
    freqs = ROPE_THETA ** (-jnp.arange(0, MROPE, 2, dtype=F32) / MROPE)
    lane = np.arange(LANE)
    in_rope = (lane >= MNOPE) & (lane < MQK)
    freq_lane = jnp.where(in_rope, freqs[(lane - MNOPE) % (MROPE // 2)], 0.0)
    sign = np.where(in_rope, np.where(lane < MNOPE + MROPE // 2, -1.0, 1.0), 0.0).astype(np.float32)
    ang = positions.astype(F32).reshape(-1, 1) * freq_lane[None, :]
    return jnp.cos(ang), jnp.sin(ang) * sign[None, :]


def _local_step(x, positions, mod, target, w, small):
    bsz, s, _ = x.shape
    t = bsz * s
    tt = _tile(t, 1024)
    shift1, scale1, gate1, shift2, scale2, gate2 = [mod[:, None, i * D:(i + 1) * D] for i in range(6)]
    cos_t, sin_t = _rope_tables(positions)
    w_alpha_p = jnp.pad(small["gla_w_alpha"], ((0, LANE - GLR), (0, 0)))
    gq = jnp.pad(small["mla_qn_g"], ((0, 0), (0, LANE - MQK)))
    gk = jnp.pad(small["mla_kn_g"], ((0, 0), (0, LANE - MQK)))
    flat2 = lambda a: a.reshape(t, a.shape[-1])
    bsd = lambda a: a.reshape(bsz, s, a.shape[-1])

    h = _norm_mod(x, small["norm1_g"], scale1, shift1, name="norm1")
    proj = _mm(flat2(h), w["w_in"], name="proj", tn=1152)
    proj3 = bsd(proj)
    o, o_gated, states = _gla_fwd(proj3, w_alpha_p, small["gla_b_alpha"], small["gla_out_norm_g"], name="gla_fwd")
    y_a = _mm(flat2(o_gated), w["gla_w_o"], name="gla_out")
    cq_n, ckv_n = _lat_norm(proj, small["mla_q_lat_g"], small["mla_kv_lat_g"], name="lat_norm")
    q_raw = _mm(cq_n, w["mla_w_uq"], name="mla_uq")
    kv = _mm(ckv_n, w["mla_w_ukv"], name="mla_ukv")
    qf, kf, vf = _qk_prep(q_raw, kv, proj, cos_t, sin_t, gq * Q_PRESCALE, gk, name="qk_prep")
    o_attn = _attn_fwd(bsd(qf), bsd(kf), bsd(vf), name="attn_fwd")
    y_b = _mm(flat2(o_attn), w["mla_w_o"], name="mla_out")
    mixed_in = _merge_fwd(proj3, small["b_merge"], bsd(y_a), bsd(y_b), name="merge_fwd")
    mixed = _mm(flat2(mixed_in), w["w_out"], name="w_out")
    x1, h2 = _resid_norm_mod(x, bsd(mixed), gate1, small["norm2_g"], scale2, shift2, name="norm2")

    def sqrelu(acc, ex, outs):
        outs[0][...] = acc
        r = jnp.maximum(acc, 0.0)
        outs[1][...] = (r * r).astype(BF16)

    a1, r = _mm(flat2(h2), w["mlp_w1"], name="mlp1", epilogue=sqrelu,
                out_shape=[jax.ShapeDtypeStruct((t, DFF), F32), jax.ShapeDtypeStruct((t, DFF), BF16)],
                out_specs=[_tile_spec(tt, 1024), _tile_spec(tt, 1024)])
    ff = _mm(r, w["mlp_w2"], name="mlp2")
    dy, dff, dgate2, loss_part = _loss_head(x1, bsd(ff), gate2, target, name="loss_head")

    g = {}

    def relu2_bwd(acc, ex, outs):
        outs[0][...] = (acc * (2.0 * jnp.maximum(ex[0][...], 0.0))).astype(BF16)

    dff2 = flat2(dff)
    da1 = _mm(dff2, w["mlp_w2"], tb=True, name="mlp2_dx", epilogue=relu2_bwd, extras=(a1,),
              extra_specs=(_tile_spec(tt, 1024),), out_shape=jax.ShapeDtypeStruct((t, DFF), BF16),
              out_specs=_tile_spec(tt, 1024))
    g["mlp_w2"] = _mm(r, dff2, ta=True, name="mlp2_dw")
    dh2 = _mm(da1, w["mlp_w1"], tb=True, name="mlp1_dx")
    g["mlp_w1"] = _mm(flat2(h2), da1, ta=True, name="mlp1_dw")
    dx1, dscale2, dshift2, dg2, dgate1, dmixed = _norm_mod_bwd(
        bsd(dh2), x1, dy, small["norm2_g"], scale2, gate1, bsd(mixed), name="norm2_bwd")
    dmixed2 = flat2(dmixed)
    dmi = _mm(dmixed2, w["w_out"], tb=True, name="w_out_dx")
    g["w_out"] = _mm(flat2(mixed_in), dmixed2, ta=True, name="w_out_dw")
    dy_a, dy_b, dl_a, dl_b, db_a, db_b = _merge_bwd(bsd(dmi), proj3, small["b_merge"], bsd(y_a), bsd(y_b), name="merge_bwd")
    dy_a2, dy_b2 = flat2(dy_a), flat2(dy_b)
    dog = _mm(dy_a2, w["gla_w_o"], tb=True, name="gla_out_dx")
    g["gla_w_o"] = _mm(flat2(o_gated), dy_a2, ta=True, name="gla_out_dw")
    dq_g, dk_g, dv_g, dg_g, dlog, db_alpha, d_ong = _gla_bwd(
        bsd(dog), o, states, proj3, w_alpha_p, small["gla_b_alpha"], small["gla_out_norm_g"], name="gla_bwd")
    dlog2 = flat2(dlog)
    da_p = _mm(dlog2, w_alpha_p, tb=True, out_dtype=BF16, name="alpha_dx")
    d_w_alpha = _mm(proj[:, OFF_A:OFF_A + LANE], dlog2, ta=True, name="alpha_dw")[:GLR]
    do_attn = _mm(dy_b2, w["mla_w_o"], tb=True, out_dtype=BF16, name="mla_out_dx")
    g["mla_w_o"] = _mm(flat2(o_attn), dy_b2, ta=True, name="mla_out_dw")
    dqf, dkf, dvf = _attn_bwd(bsd(qf), bsd(kf), bsd(vf), bsd(do_attn), name="attn_bwd")
    dq_raw, dkv, dkpe, dgq, dgk = _qk_prep_bwd(flat2(dqf), flat2(dkf), flat2(dvf), q_raw, kv, proj, cos_t, sin_t, gq, gk,
                                                name="qk_prep_bwd")
    dcq_n = _mm(dq_raw, w["mla_w_uq"], tb=True, name="mla_uq_dx")
    g["mla_w_uq"] = _mm(cq_n, dq_raw, ta=True, name="mla_uq_dw")
    dckv_n = _mm(dkv, w["mla_w_ukv"], tb=True, name="mla_ukv_dx")
    g["mla_w_ukv"] = _mm(ckv_n, dkv, ta=True, name="mla_ukv_dw")
    dcq, dckv, dg_qlat, dg_kvlat = _lat_norm_bwd(dcq_n, dckv_n, proj, small["mla_q_lat_g"], small["mla_kv_lat_g"],
                                                  name="lat_norm_bwd")
    dproj = jnp.concatenate([flat2(dq_g), flat2(dk_g), flat2(dv_g), flat2(dg_g), flat2(dl_a), flat2(dl_b), dcq, dckv,
                             da_p, dkpe.astype(BF16)], axis=1)
    dh = _mm(dproj, w["w_in"], tb=True, name="proj_dx", tk=1152)
    g["w_in"] = _mm(flat2(h), dproj, ta=True, name="proj_dw", tn=1152)
    grad_x, dscale1, dshift1, dg1 = _norm_mod_bwd(bsd(dh), x, dx1, small["norm1_g"], scale1, name="norm1_bwd")

    dmod = jnp.concatenate([dshift1, dscale1, dgate1, dshift2, dscale2, dgate2], axis=-1).reshape(bsz, 6 * D)
    gs = {"norm1_g": dg1, "b_merge": jnp.concatenate([db_a, db_b], axis=1), "gla_b_alpha": db_alpha,
          "gla_out_norm_g": d_ong, "mla_q_lat_g": dg_qlat, "mla_kv_lat_g": dg_kvlat, "mla_qn_g": dgq[:, :MQK],
          "mla_kn_g": dgk[:, :MQK], "norm2_g": dg2}
    return loss_part[0, 0], grad_x, dmod, g, gs, d_w_alpha


def kernel(x, c, positions, w_ada, b_ada, norm1_g, w_in, b_merge, gla_w_alpha, gla_b_alpha, gla_out_norm_g, gla_w_o, mla_q_lat_g, mla_w_uq, mla_kv_lat_g, mla_w_ukv, mla_qn_g, mla_kn_g, mla_w_o, w_out, norm2_g, mlp_w1, mlp_w2, loss_target, m_w_ada, m_b_ada, m_norm1_g, m_w_in, m_b_merge, m_gla_w_alpha, m_gla_b_alpha, m_gla_out_norm_g, m_gla_w_o, m_mla_q_lat_g, m_mla_w_uq, m_mla_kv_lat_g, m_mla_w_ukv, m_mla_qn_g, m_mla_kn_g, m_mla_w_o, m_w_out, m_norm2_g, m_mlp_w1, m_mlp_w2, v_w_ada, v_b_ada, v_norm1_g, v_w_in, v_b_merge, v_gla_w_alpha, v_gla_b_alpha, v_gla_out_norm_g, v_gla_w_o, v_mla_q_lat_g, v_mla_w_uq, v_mla_kv_lat_g, v_mla_w_ukv, v_mla_qn_g, v_mla_kn_g, v_mla_w_o, v_w_out, v_norm2_g, v_mlp_w1, v_mlp_w2):
    args = dict(locals())
    names_big = [n for n, _, _ in BIG]
    names_small = [n for n, _ in SMALL]
    bsz = x.shape[0]
    ax, ay, ac = lax.axis_index("x"), lax.axis_index("y"), lax.axis_index("c")
    chip = 2 * ax + ay
    dev = 2 * chip + ac

    gathered = _gather_weights([args[n][0].astype(BF16) for n in names_big], name="comm_weights")
    w = _full_weights(dict(zip(names_big, gathered)))
    small = {n: args[n] for n in names_small}
    sel_c = jnp.reshape(ac, (1,)).astype(jnp.int32)
    sel_chip = jnp.reshape(chip, (1,)).astype(jnp.int32)
    w_alpha_all = _all_gather8(gla_w_alpha[0], name="comm_w_alpha")
    small["gla_w_alpha"] = jnp.concatenate([w_alpha_all[2 * j] for j in range(4)], axis=1)

    c_all = _all_gather8(c, name="comm_c").reshape(8 * bsz, D)

    def add_bias(acc, ex, outs):
        outs[0][...] = acc + ex[0][...]

    silu = lambda v: v * _sigmoid(v)
    b_ada_mine = lax.dynamic_slice(b_ada, (0, chip * ADA_SHARD[1]), (1, ADA_SHARD[1]))
    mod_part = _mm(c_all, w_ada[0], name="ada", tn=512, a_fn=silu, epilogue=add_bias, extras=(b_ada_mine,),
                   extra_specs=(pl.BlockSpec((1, 512), lambda i, j, k: (0, j)),),
                   out_shape=jax.ShapeDtypeStruct((8 * bsz, ADA_SHARD[1]), F32), out_specs=_tile_spec(8 * bsz, 512))
    mod_all = _all_gather8(mod_part, name="comm_mod")
    mod_rows = lax.dynamic_slice(mod_all, (0, dev * bsz, 0), (8, bsz, ADA_SHARD[1]))
    mod = jnp.concatenate([mod_rows[2 * j] for j in range(4)], axis=1)

    loss_part, grad_x, dmod, g, gs, d_w_alpha = _local_step(x, positions, mod, loss_target, w, small)
    loss = lax.psum(loss_part * (0.5 / D), ("x", "y", "c"))

    dmod_all = _all_gather8(dmod, name="comm_dmod").reshape(8 * bsz, 6 * D)
    dmod_mine = lax.dynamic_slice(dmod_all, (0, chip * ADA_SHARD[1]), (8 * bsz, ADA_SHARD[1]))
    g_w_ada = _mm(c_all, dmod_mine, ta=True, a_fn=silu, name="ada_dw")

    gs_packed = _pack_small([gs[n] for n, _ in SMALL_RED], d_w_alpha, name="pack_small")
    gs_all = _all_gather8(gs_packed, name="comm_small")
    wmv = [(args[n], args["m_" + n], args["v_" + n]) for n in names_small]
    wmv.append((gla_w_alpha[0], m_gla_w_alpha[0], v_gla_w_alpha[0]))
    res = _small_update(gs_all, dmod_all, sel_chip, wmv, name="small_update")

    gslots = _grad_slots(g)
    parts = [gslots[n] for n in names_big]
    sib_halves = _pair_swap_halves(parts, name="comm_pair_sum")
    pairs = [_pair_add(p, s, sel_c, name="pair_add_" + n) for n, p, s in zip(names_big, parts, sib_halves)]
    from_chips = _scatter_chips(pairs, name="comm_scatter")
    halves = [_chip_sum(p, r, sel_chip, name="chip_sum_" + n) for n, p, r in zip(names_big, pairs, from_chips)]
    theirs = _pair_swap(halves, name="comm_pair_join")
    for n, mine, other in zip(names_big, halves, theirs):
        res[n] = _adamw_halves(args[n][0], args["m_" + n][0], args["v_" + n][0], mine, other, sel_c, name="adamw_" + n)
    res["w_ada"] = (g_w_ada, *_adamw(w_ada[0], g_w_ada, m_w_ada[0], v_w_ada[0], name="adamw_w_ada"))

    order = ["w_ada", "b_ada", "norm1_g", "w_in", "b_merge", "gla_w_alpha", "gla_b_alpha", "gla_out_norm_g", "gla_w_o",
             "mla_q_lat_g", "mla_w_uq", "mla_kv_lat_g", "mla_w_ukv", "mla_qn_g", "mla_kn_g", "mla_w_o", "w_out",
             "norm2_g", "mlp_w1", "mlp_w2"]
    named = lambda k: [res[n][k].reshape(args[n].shape) for n in order]
    return (loss, grad_x, *named(0), *named(1), *named(2), *named(3))
```

```python
import functools

import jax
import jax.numpy as jnp
import numpy as np
from jax import lax
from jax.experimental import pallas as pl
from jax.experimental.pallas import tpu as pltpu

F32 = jnp.float32
BF16 = jnp.bfloat16
MESH = pl.DeviceIdType.MESH

D = 1024
CHUNK = 64
EPS = 1e-6
GH, GDK, GDV, GLR, GTAU = 4, 128, 256, 16, 16.0
MH, MQR, MKVR, MNOPE, MROPE, MVD = 16, 256, 128, 64, 32, 64
MQK = MNOPE + MROPE
DFF = 4 * D
ROPE_THETA = 10000.0
IN_WIDTH = 5552
LANE = 128
OFF_Q, OFF_K, OFF_V, OFF_G, OFF_MA, OFF_MB, OFF_CQ, OFF_CKV, OFF_A, OFF_KPE, PW = (
    0, 512, 1024, 2048, 3072, 4096, 5120, 5376, 5504, 5632, 5760)
ADAM_LR, ADAM_B1, ADAM_B2, ADAM_EPS, ADAM_WD, ADAM_STEP = 0.001, 0.9, 0.999, 1e-08, 0.01, 10
VMEM_LIMIT = 48 * 1024 * 1024


def _params(n_axes):
    return pltpu.CompilerParams(dimension_semantics=("arbitrary",) * n_axes, vmem_limit_bytes=VMEM_LIMIT)


def _tile(n, target):
    if n <= target:
        return n
    best = None
    for t in range(LANE, target + 1, LANE):
        if n % t == 0:
            best = t
    assert best is not None, (n, target)
    return best


def _sigmoid(x):
    return 1.0 / (1.0 + jnp.exp(-x))


def _mm(a, b, *, name, ta=False, tb=False, out_dtype=F32, tm=1024, tn=1024, tk=1024,
        epilogue=None, extras=(), extra_specs=(), out_shape=None, out_specs=None, a_fn=None):
    if ta:
        kdim, m = a.shape
    else:
        m, kdim = a.shape
    if tb:
        n, k2 = b.shape
    else:
        k2, n = b.shape
    assert kdim == k2, (a.shape, b.shape)
    tm, tn, tk = _tile(m, tm), _tile(n, tn), _tile(kdim, tk)
    nk = kdim // tk
    a_spec = pl.BlockSpec((tk, tm), lambda i, j, k: (k, i)) if ta else pl.BlockSpec((tm, tk), lambda i, j, k: (i, k))
    b_spec = pl.BlockSpec((tn, tk), lambda i, j, k: (j, k)) if tb else pl.BlockSpec((tk, tn), lambda i, j, k: (k, j))
    dims = (((0 if ta else 1,), (1 if tb else 0,)), ((), ()))
    ne = len(extras)
    if out_shape is None:
        out_shape = jax.ShapeDtypeStruct((m, n), out_dtype)
        out_specs = pl.BlockSpec((tm, tn), lambda i, j, k: (i, j))

    def body(a_ref, b_ref, *rest):
        ex, outs, acc = rest[:ne], rest[ne:-1], rest[-1]
        k = pl.program_id(2)

        @pl.when(k == 0)
        def _():
            acc[...] = jnp.zeros_like(acc)

        av = a_ref[...] if a_fn is None else a_fn(a_ref[...])
        acc[...] += lax.dot_general(av.astype(BF16), b_ref[...].astype(BF16), dims, preferred_element_type=F32)

        @pl.when(k == nk - 1)
        def _():
            if epilogue is None:
                outs[0][...] = acc[...].astype(outs[0].dtype)
            else:
                epilogue(acc[...], ex, outs)

    return pl.pallas_call(
        body, name=name, grid=(m // tm, n // tn, nk),
        in_specs=[a_spec, b_spec, *extra_specs], out_specs=out_specs, out_shape=out_shape,
        scratch_shapes=[pltpu.VMEM((tm, tn), F32)], compiler_params=_params(3),
    )(a, b, *extras)


def _tile_spec(tm, tn):
    return pl.BlockSpec((tm, tn), lambda i, j, k: (i, j))


def _rms(x, g):
    r = lax.rsqrt(jnp.mean(x * x, axis=-1, keepdims=True) + EPS)
    return x * r, r


def _row_spec(ts, width, col=0):
    return pl.BlockSpec((None, ts, width), lambda b, i: (b, i, col))


def _vec_spec(width):
    return pl.BlockSpec((None, 1, width), lambda b, i: (b, 0, 0))


def _gain_spec(width):
    return pl.BlockSpec((1, width), lambda b, i: (0, 0))


def _norm_mod(x, g, scale, shift, *, name, ts=256):
    bsz, s, d = x.shape
    ts = min(ts, s)

    def body(x_ref, g_ref, sc_ref, sh_ref, h_ref):
        xh, _ = _rms(x_ref[...], None)
        h_ref[...] = ((xh * g_ref[...]) * (1.0 + sc_ref[...]) + sh_ref[...]).astype(BF16)

    return pl.pallas_call(
        body, name=name, grid=(bsz, s // ts),
        in_specs=[_row_spec(ts, d), _gain_spec(d), _vec_spec(d), _vec_spec(d)],
        out_specs=_row_spec(ts, d), out_shape=jax.ShapeDtypeStruct((bsz, s, d), BF16),
        compiler_params=_params(2),
    )(x, g, scale, shift)


def _resid_norm_mod(x, mixed, gate, g, scale, shift, *, name, ts=256):
    bsz, s, d = x.shape
    ts = min(ts, s)

    def body(x_ref, mx_ref, gt_ref, g_ref, sc_ref, sh_ref, x1_ref, h_ref):
        x1 = x_ref[...] + gt_ref[...] * mx_ref[...]
        x1_ref[...] = x1
        xh, _ = _rms(x1, None)
        h_ref[...] = ((xh * g_ref[...]) * (1.0 + sc_ref[...]) + sh_ref[...]).astype(BF16)

    return pl.pallas_call(
        body, name=name, grid=(bsz, s // ts),
        in_specs=[_row_spec(ts, d), _row_spec(ts, d), _vec_spec(d), _gain_spec(d), _vec_spec(d), _vec_spec(d)],
        out_specs=[_row_spec(ts, d), _row_spec(ts, d)],
        out_shape=[jax.ShapeDtypeStruct((bsz, s, d), F32), jax.ShapeDtypeStruct((bsz, s, d), BF16)],
        compiler_params=_params(2),
    )(x, mixed, gate, g, scale, shift)


def _norm_mod_bwd(dh, xin, resid, g, scale, gate=None, mixed=None, *, name, ts=256):
    bsz, s, d = xin.shape
    ts = min(ts, s)
    gated = gate is not None

    def body(*refs):
        if gated:
            dh_ref, x_ref, rs_ref, g_ref, sc_ref, gt_ref, mx_ref, dx_ref, dsc_ref, dsh_ref, dg_ref, dgt_ref, dmx_ref = refs
        else:
            dh_ref, x_ref, rs_ref, g_ref, sc_ref, dx_ref, dsc_ref, dsh_ref, dg_ref = refs
        b, i = pl.program_id(0), pl.program_id(1)

        @pl.when(i == 0)
        def _():
            dsc_ref[...] = jnp.zeros_like(dsc_ref)
            dsh_ref[...] = jnp.zeros_like(dsh_ref)
            if gated:
                dgt_ref[...] = jnp.zeros_like(dgt_ref)

        @pl.when((i == 0) & (b == 0))
        def _():
            dg_ref[...] = jnp.zeros_like(dg_ref)

        dh_v, gv = dh_ref[...], g_ref[...]
        xh, r = _rms(x_ref[...], None)
        dsc_ref[...] += jnp.sum(dh_v * (xh * gv), axis=0, keepdims=True)
        dsh_ref[...] += jnp.sum(dh_v, axis=0, keepdims=True)
        dn = dh_v * (1.0 + sc_ref[...])
        dg_ref[...] += jnp.sum(dn * xh, axis=0, keepdims=True)
        dxh = dn * gv
        dx = rs_ref[...] + r * (dxh - xh * jnp.mean(dxh * xh, axis=-1, keepdims=True))
        dx_ref[...] = dx
        if gated:
            dgt_ref[...] += jnp.sum(dx * mx_ref[...], axis=0, keepdims=True)
            dmx_ref[...] = (dx * gt_ref[...]).astype(BF16)

    ins = [dh, xin, resid, g, scale]
    in_specs = [_row_spec(ts, d), _row_spec(ts, d), _row_spec(ts, d), _gain_spec(d), _vec_spec(d)]
    out_specs = [_row_spec(ts, d), _vec_spec(d), _vec_spec(d), _gain_spec(d)]
    out_shape = [jax.ShapeDtypeStruct((bsz, s, d), F32), jax.ShapeDtypeStruct((bsz, 1, d), F32),
                 jax.ShapeDtypeStruct((bsz, 1, d), F32), jax.ShapeDtypeStruct((1, d), F32)]
    if gated:
        ins += [gate, mixed]
        in_specs += [_vec_spec(d), _row_spec(ts, d)]
        out_specs += [_vec_spec(d), _row_spec(ts, d)]
        out_shape += [jax.ShapeDtypeStruct((bsz, 1, d), F32), jax.ShapeDtypeStruct((bsz, s, d), BF16)]
    return pl.pallas_call(
        body, name=name, grid=(bsz, s // ts), in_specs=in_specs, out_specs=out_specs, out_shape=out_shape,
        compiler_params=_params(2),
    )(*ins)


def _loss_head(x1, ff, gate2, target, *, name, ts=256):
    bsz, s, d = x1.shape
    ts = min(ts, s)

    def body(x1_ref, ff_ref, gt_ref, t_ref, dy_ref, dff_ref, dgt_ref, loss_ref, acc):
        b, i = pl.program_id(0), pl.program_id(1)

        @pl.when(i == 0)
        def _():
            dgt_ref[...] = jnp.zeros_like(dgt_ref)

        @pl.when((i == 0) & (b == 0))
        def _():
            acc[...] = jnp.zeros_like(acc)

        ffv, gt = ff_ref[...], gt_ref[...]
        diff = (x1_ref[...] + gt * ffv) - t_ref[...]
        acc[...] += jnp.sum((diff * diff).reshape(ts // 8, 8, d), axis=0)
        dy = diff * (1.0 / d)
        dy_ref[...] = dy
        dgt_ref[...] += jnp.sum(dy * ffv, axis=0, keepdims=True)
        dff_ref[...] = (dy * gt).astype(BF16)

        @pl.when((i == pl.num_programs(1) - 1) & (b == pl.num_programs(0) - 1))
        def _():
            loss_ref[...] = jnp.full(loss_ref.shape, jnp.sum(acc[...]), F32)

    return pl.pallas_call(
        body, name=name, grid=(bsz, s // ts),
        in_specs=[_row_spec(ts, d), _row_spec(ts, d), _vec_spec(d), _row_spec(ts, d)],
        out_specs=[_row_spec(ts, d), _row_spec(ts, d), _vec_spec(d), pl.BlockSpec((8, LANE), lambda b, i: (0, 0))],
        out_shape=[jax.ShapeDtypeStruct((bsz, s, d), F32), jax.ShapeDtypeStruct((bsz, s, d), BF16),
                   jax.ShapeDtypeStruct((bsz, 1, d), F32), jax.ShapeDtypeStruct((8, LANE), F32)],
        scratch_shapes=[pltpu.VMEM((8, d), F32)], compiler_params=_params(2),
    )(x1, ff, gate2, target)


def _merge_fwd(proj, b_merge, y_a, y_b, *, name, ts=256):
    bsz, s, _ = proj.shape
    ts = min(ts, s)

    def body(la_ref, lb_ref, ba_ref, bb_ref, ya_ref, yb_ref, out_ref):
        ga = _sigmoid(la_ref[...] + ba_ref[...])
        gb = _sigmoid(lb_ref[...] + bb_ref[...])
        out_ref[...] = (ga * ya_ref[...] + gb * yb_ref[...]).astype(BF16)

    return pl.pallas_call(
        body, name=name, grid=(bsz, s // ts),
        in_specs=[_row_spec(ts, D, OFF_MA // D), _row_spec(ts, D, OFF_MB // D),
                  pl.BlockSpec((1, D), lambda b, i: (0, 0)), pl.BlockSpec((1, D), lambda b, i: (0, 1)),
                  _row_spec(ts, D), _row_spec(ts, D)],
        out_specs=_row_spec(ts, D), out_shape=jax.ShapeDtypeStruct((bsz, s, D), BF16),
        compiler_params=_params(2),
    )(proj, proj, b_merge, b_merge, y_a, y_b)


def _merge_bwd(dmi, proj, b_merge, y_a, y_b, *, name, ts=256):
    bsz, s, _ = proj.shape
    ts = min(ts, s)

    def body(d_ref, la_ref, lb_ref, ba_ref, bb_ref, ya_ref, yb_ref, dya_ref, dyb_ref, dla_ref, dlb_ref, dba_ref, dbb_ref):
        @pl.when((pl.program_id(0) == 0) & (pl.program_id(1) == 0))
        def _():
            dba_ref[...] = jnp.zeros_like(dba_ref)
            dbb_ref[...] = jnp.zeros_like(dbb_ref)

        dv = d_ref[...]
        ga = _sigmoid(la_ref[...] + ba_ref[...])
        gb = _sigmoid(lb_ref[...] + bb_ref[...])
        dya_ref[...] = (dv * ga).astype(BF16)
        dyb_ref[...] = (dv * gb).astype(BF16)
        dla = (dv * ya_ref[...]) * (ga * (1.0 - ga))
        dlb = (dv * yb_ref[...]) * (gb * (1.0 - gb))
        dla_ref[...] = dla.astype(BF16)
        dlb_ref[...] = dlb.astype(BF16)
        dba_ref[...] += jnp.sum(dla, axis=0, keepdims=True)
        dbb_ref[...] += jnp.sum(dlb, axis=0, keepdims=True)

    act = jax.ShapeDtypeStruct((bsz, s, D), BF16)
    return pl.pallas_call(
        body, name=name, grid=(bsz, s // ts),
        in_specs=[_row_spec(ts, D), _row_spec(ts, D, OFF_MA // D), _row_spec(ts, D, OFF_MB // D),
                  pl.BlockSpec((1, D), lambda b, i: (0, 0)), pl.BlockSpec((1, D), lambda b, i: (0, 1)),
                  _row_spec(ts, D), _row_spec(ts, D)],
        out_specs=[_row_spec(ts, D)] * 4 + [_gain_spec(D)] * 2,
        out_shape=[act, act, act, act, jax.ShapeDtypeStruct((1, D), F32), jax.ShapeDtypeStruct((1, D), F32)],
        compiler_params=_params(2),
    )(dmi, proj, proj, b_merge, b_merge, y_a, y_b)


def _tri(lower):
    r = lax.broadcasted_iota(jnp.int32, (CHUNK, CHUNK), 0)
    c = lax.broadcasted_iota(jnp.int32, (CHUNK, CHUNK), 1)
    return jnp.where((c <= r) if lower else (c >= r), 1.0, 0.0).astype(F32)


def _gla_logits(a_ref, wal_ref, bal_ref):
    logits = jnp.dot(a_ref[...].astype(BF16), wal_ref[...].astype(BF16), preferred_element_type=F32) + bal_ref[...]
    la = (jnp.minimum(logits, 0.0) - jnp.log(1.0 + jnp.exp(-jnp.abs(logits)))) * (1.0 / GTAU)
    return logits, la


def _chunk_cumsum(la_n, tri):
    cum = jnp.dot(tri, la_n, preferred_element_type=F32, precision=lax.Precision.HIGHEST)
    return cum, jnp.sum(la_n, axis=0, keepdims=True)


def _gla_specs(s, nc):
    def blk(width, off):
        return pl.BlockSpec((None, s, width), lambda h, b: (b, 0, off // width + h))

    proj_specs = [blk(GDK, OFF_Q), blk(GDK, OFF_K), blk(GDV, OFF_V), blk(GDV, OFF_G),
                  pl.BlockSpec((None, s, LANE), lambda h, b: (b, 0, OFF_A // LANE)),
                  pl.BlockSpec((LANE, GDK), lambda h, b: (0, h)), pl.BlockSpec((1, GDK), lambda h, b: (0, h)),
                  pl.BlockSpec((1, GDV), lambda h, b: (0, 0))]
    st_spec = pl.BlockSpec((None, None, nc, GDV, GDK), lambda h, b: (b, h, 0, 0, 0))
    return blk, proj_specs, st_spec


def _gla_fwd(proj, w_alpha_p, b_alpha, out_norm_g, *, name):
    bsz, s, _ = proj.shape
    nc = s // CHUNK
    scale = GDK ** -0.5

    rb = min(512, s)

    def body(q_ref, k_ref, v_ref, g_ref, a_ref, wal_ref, bal_ref, ong_ref, o_ref, og_ref, st_ref):
        _, la = _gla_logits(a_ref, wal_ref, bal_ref)
        tri = _tri(True)
        st = jnp.zeros((GDV, GDK), F32)
        for n in range(nc):
            rows = pl.ds(n * CHUNK, CHUNK)
            cum, cum_end = _chunk_cumsum(la[n * CHUNK:(n + 1) * CHUNK], tri)
            kd = k_ref[rows, :] * jnp.exp(cum_end - cum)
            ut = lax.dot_general(v_ref[rows, :].astype(BF16), kd.astype(BF16), _TN, preferred_element_type=F32)
            st = st * jnp.exp(cum_end) + ut
            st_ref[n] = st
            o_ref[rows, :] = lax.dot_general((q_ref[rows, :] * scale).astype(BF16), st.astype(BF16), _NT,
                                             preferred_element_type=F32)
        for j in range(0, s, rb):
            blk_rows = pl.ds(j, rb)
            oh, _ = _rms(o_ref[blk_rows, :], None)
            gv = g_ref[blk_rows, :]
            og_ref[blk_rows, :] = ((oh * ong_ref[...]) * (gv * _sigmoid(gv))).astype(BF16)

    blk, proj_specs, st_spec = _gla_specs(s, nc)
    return pl.pallas_call(
        body, name=name, grid=(GH, bsz), in_specs=proj_specs, out_specs=[blk(GDV, 0), blk(GDV, 0), st_spec],
        out_shape=[jax.ShapeDtypeStruct((bsz, s, GH * GDV), F32), jax.ShapeDtypeStruct((bsz, s, GH * GDV), BF16),
                   jax.ShapeDtypeStruct((bsz, GH, nc, GDV, GDK), F32)],
        compiler_params=_params(2),
    )(proj, proj, proj, proj, proj, w_alpha_p, b_alpha, out_norm_g)


def _gla_bwd(dog, o, states, proj, w_alpha_p, b_alpha, out_norm_g, *, name):
    bsz, s, _ = proj.shape
    nc = s // CHUNK
    scale = GDK ** -0.5

    def body(dog_ref, o_ref, st_ref, q_ref, k_ref, v_ref, g_ref, a_ref, wal_ref, bal_ref, ong_ref,
             dq_ref, dk_ref, dv_ref, dg_ref, dl_ref, dbal_ref, dong_ref, do_scr, dlog_scr):
        h, b = pl.program_id(0), pl.program_id(1)

        @pl.when(b == 0)
        def _():
            dbal_ref[...] = jnp.zeros_like(dbal_ref)

        @pl.when((b == 0) & (h == 0))
        def _():
            dong_ref[...] = jnp.zeros_like(dong_ref)

        ong = ong_ref[...]
        for j in range(0, s, rb):
            blk_rows = pl.ds(j, rb)
            gv, dogv = g_ref[blk_rows, :], dog_ref[blk_rows, :]
            sg = _sigmoid(gv)
            oh, r = _rms(o_ref[blk_rows, :], None)
            don = dogv * (gv * sg)
            dg_ref[blk_rows, :] = (dogv * (oh * ong) * (sg * (1.0 + gv * (1.0 - sg)))).astype(BF16)
            dong_ref[...] += jnp.sum(don * oh, axis=0, keepdims=True)
            doh = don * ong
            do_scr[blk_rows, :] = (r * (doh - oh * jnp.mean(doh * oh, axis=-1, keepdims=True))).astype(BF16)

        logits, la = _gla_logits(a_ref, wal_ref, bal_ref)
        tri_lo, tri_up = _tri(True), _tri(False)
        carry = jnp.zeros((GDV, GDK), F32)
        for n in range(nc - 1, -1, -1):
            rows = pl.ds(n * CHUNK, CHUNK)
            cum, cum_end = _chunk_cumsum(la[n * CHUNK:(n + 1) * CHUNK], tri_lo)
            decay = jnp.exp(cum_end)
            w = jnp.exp(cum_end - cum)
            kd = k_ref[rows, :] * w
            do_b = do_scr[rows, :]
            qs_b = (q_ref[rows, :] * scale).astype(BF16)
            dq_ref[rows, :] = (jnp.dot(do_b, st_ref[n].astype(BF16), preferred_element_type=F32) * scale).astype(BF16)
            dsn = lax.dot_general(do_b, qs_b, _TN, preferred_element_type=F32) + carry
            carry = dsn * decay
            dsn_b = dsn.astype(BF16)
            dv_ref[rows, :] = lax.dot_general(kd.astype(BF16), dsn_b, _NT, preferred_element_type=F32).astype(BF16)
            dkd = jnp.dot(v_ref[rows, :].astype(BF16), dsn_b, preferred_element_type=F32)
            dk_ref[rows, :] = (dkd * w).astype(BF16)
            e = dkd * kd
            dcum_end = jnp.sum(e, axis=0, keepdims=True)
            if n > 0:
                dcum_end += jnp.sum(dsn * st_ref[n - 1], axis=0, keepdims=True) * decay
            dlog_scr[rows, :] = dcum_end - jnp.dot(tri_up, e, preferred_element_type=F32,
                                                  precision=lax.Precision.HIGHEST)
        dlog = dlog_scr[...] * (1.0 / GTAU) * (1.0 - _sigmoid(logits))
        dl_ref[...] = dlog.astype(BF16)
        dbal_ref[...] += jnp.sum(dlog, axis=0, keepdims=True)

    rb = min(512, s)

    blk, proj_specs, st_spec = _gla_specs(s, nc)
    act = lambda wd: jax.ShapeDtypeStruct((bsz, s, wd), BF16)
    return pl.pallas_call(
        body, name=name, grid=(GH, bsz), in_specs=[blk(GDV, 0), blk(GDV, 0), st_spec, *proj_specs],
        out_specs=[blk(GDK, 0), blk(GDK, 0), blk(GDV, 0), blk(GDV, 0), blk(GDK, 0),
                   pl.BlockSpec((1, GDK), lambda h, b: (0, h)), pl.BlockSpec((1, GDV), lambda h, b: (0, 0))],
        out_shape=[act(GH * GDK), act(GH * GDK), act(GH * GDV), act(GH * GDV), act(GH * GDK),
                   jax.ShapeDtypeStruct((1, GH * GDK), F32), jax.ShapeDtypeStruct((1, GDV), F32)],
        scratch_shapes=[pltpu.VMEM((s, GDV), BF16), pltpu.VMEM((s, GDK), F32)], compiler_params=_params(2),
    )(dog, o, states, proj, proj, proj, proj, proj, w_alpha_p, b_alpha, out_norm_g)


def _lane():
    return lax.broadcasted_iota(jnp.int32, (1, LANE), 1)


def _swap_halves(x):
    lane = _lane()
    half = MROPE // 2
    lo = (lane >= MNOPE) & (lane < MNOPE + half)
    hi = (lane >= MNOPE + half) & (lane < MQK)
    return jnp.where(lo, pltpu.roll(x, LANE - half, 1), jnp.where(hi, pltpu.roll(x, half, 1), 0.0))


def _norm96(x, g):
    r = lax.rsqrt(jnp.sum(x * x, axis=-1, keepdims=True) * (1.0 / MQK) + EPS)
    return x * r, r


def _lat_norm(proj, q_lat_g, kv_lat_g, *, name, ts=512):
    t = proj.shape[0]
    ts = min(ts, t)

    def body(cq_ref, ckv_ref, gq_ref, gk_ref, oq_ref, ok_ref):
        xq, _ = _rms(cq_ref[...], None)
        oq_ref[...] = (xq * gq_ref[...]).astype(BF16)
        xk, _ = _rms(ckv_ref[...], None)
        ok_ref[...] = (xk * gk_ref[...]).astype(BF16)

    return pl.pallas_call(
        body, name=name, grid=(t // ts,),
        in_specs=[pl.BlockSpec((ts, MQR), lambda i: (i, OFF_CQ // MQR)), pl.BlockSpec((ts, MKVR), lambda i: (i, OFF_CKV // MKVR)),
                  pl.BlockSpec((1, MQR), lambda i: (0, 0)), pl.BlockSpec((1, MKVR), lambda i: (0, 0))],
        out_specs=[pl.BlockSpec((ts, MQR), lambda i: (i, 0)), pl.BlockSpec((ts, MKVR), lambda i: (i, 0))],
        out_shape=[jax.ShapeDtypeStruct((t, MQR), BF16), jax.ShapeDtypeStruct((t, MKVR), BF16)],
        compiler_params=_params(1),
    )(proj, proj, q_lat_g, kv_lat_g)


def _lat_norm_bwd(dcqn, dckvn, proj, q_lat_g, kv_lat_g, *, name, ts=512):
    t = proj.shape[0]
    ts = min(ts, t)

    def one(d_ref, x_ref, g_ref, dx_ref, dg_ref):
        xh, r = _rms(x_ref[...], None)
        dn = d_ref[...]
        dg_ref[...] += jnp.sum(dn * xh, axis=0, keepdims=True)
        dxh = dn * g_ref[...]
        dx_ref[...] = (r * (dxh - xh * jnp.mean(dxh * xh, axis=-1, keepdims=True))).astype(BF16)

    def body(dq_ref, dk_ref, cq_ref, ckv_ref, gq_ref, gk_ref, dxq_ref, dxk_ref, dgq_ref, dgk_ref):
        @pl.when(pl.program_id(0) == 0)
        def _():
            dgq_ref[...] = jnp.zeros_like(dgq_ref)
            dgk_ref[...] = jnp.zeros_like(dgk_ref)

        one(dq_ref, cq_ref, gq_ref, dxq_ref, dgq_ref)
        one(dk_ref, ckv_ref, gk_ref, dxk_ref, dgk_ref)

    return pl.pallas_call(
        body, name=name, grid=(t // ts,),
        in_specs=[pl.BlockSpec((ts, MQR), lambda i: (i, 0)), pl.BlockSpec((ts, MKVR), lambda i: (i, 0)),
                  pl.BlockSpec((ts, MQR), lambda i: (i, OFF_CQ // MQR)), pl.BlockSpec((ts, MKVR), lambda i: (i, OFF_CKV // MKVR)),
                  pl.BlockSpec((1, MQR), lambda i: (0, 0)), pl.BlockSpec((1, MKVR), lambda i: (0, 0))],
        out_specs=[pl.BlockSpec((ts, MQR), lambda i: (i, 0)), pl.BlockSpec((ts, MKVR), lambda i: (i, 0)),
                   pl.BlockSpec((1, MQR), lambda i: (0, 0)), pl.BlockSpec((1, MKVR), lambda i: (0, 0))],
        out_shape=[jax.ShapeDtypeStruct((t, MQR), BF16), jax.ShapeDtypeStruct((t, MKVR), BF16),
                   jax.ShapeDtypeStruct((1, MQR), F32), jax.ShapeDtypeStruct((1, MKVR), F32)],
        compiler_params=_params(1),
    )(dcqn, dckvn, proj, proj, q_lat_g, kv_lat_g)


def _qk_prep(q_raw, kv, proj, cos_t, sin_t, gq, gk, *, name, ts=512):
    t = q_raw.shape[0]
    ts = min(ts, t)

    def body(q_ref, kv_ref, kpe_ref, c_ref, s_ref, gq_ref, gk_ref, qo_ref, ko_ref, vo_ref):
        cs, sn = c_ref[...], s_ref[...]
        nope = _lane() < MNOPE
        qn, _ = _norm96(q_ref[...], None)
        qn = qn * gq_ref[...]
        qo_ref[...] = (qn * cs + _swap_halves(qn) * sn).astype(BF16)
        kvv = kv_ref[...]
        kn, _ = _norm96(jnp.where(nope, kvv, kpe_ref[...]), None)
        kn = kn * gk_ref[...]
        ko_ref[...] = (kn * cs + _swap_halves(kn) * sn).astype(BF16)
        vo_ref[...] = jnp.where(nope, pltpu.roll(kvv, MNOPE, 1), 0.0).astype(BF16)

    hd = pl.BlockSpec((ts, LANE), lambda i, h: (i, h))
    shared = lambda col: pl.BlockSpec((ts, LANE), lambda i, h: (i, col))
    gain = pl.BlockSpec((1, LANE), lambda i, h: (0, 0))
    out = jax.ShapeDtypeStruct((t, MH * LANE), BF16)
    return pl.pallas_call(
        body, name=name, grid=(t // ts, MH),
        in_specs=[hd, hd, shared(OFF_KPE // LANE), shared(0), shared(0), gain, gain],
        out_specs=[hd, hd, hd], out_shape=[out, out, out], compiler_params=_params(2),
    )(q_raw, kv, proj, cos_t, sin_t, gq, gk)


def _qk_prep_bwd(dq, dk, dv, q_raw, kv, proj, cos_t, sin_t, gq, gk, *, name, ts=512):
    t = q_raw.shape[0]
    ts = min(ts, t)

    def norm_bwd(dy, x, g, dg_ref):
        xh, r = _norm96(x, None)
        dg_ref[...] += jnp.sum(dy * xh, axis=0, keepdims=True)
        dxh = dy * g
        return r * (dxh - xh * (jnp.sum(dxh * xh, axis=-1, keepdims=True) * (1.0 / MQK)))

    def body(dq_ref, dk_ref, dv_ref, q_ref, kv_ref, kpe_ref, c_ref, s_ref, gq_ref, gk_ref,
             dqr_ref, dkv_ref, dkpe_ref, dgq_ref, dgk_ref):
        i, h = pl.program_id(0), pl.program_id(1)

        @pl.when(h == 0)
        def _():
            dkpe_ref[...] = jnp.zeros_like(dkpe_ref)

        @pl.when((h == 0) & (i == 0))
        def _():
            dgq_ref[...] = jnp.zeros_like(dgq_ref)
            dgk_ref[...] = jnp.zeros_like(dgk_ref)

        cs, sn = c_ref[...], s_ref[...]
        lane = _lane()
        nope = lane < MNOPE
        dqv = dq_ref[...]
        dqn = dqv * cs + _swap_halves(dqv * sn)
        dqr_ref[...] = norm_bwd(dqn, q_ref[...], gq_ref[...], dgq_ref).astype(BF16)
        dkv_ = dk_ref[...]
        dkn = dkv_ * cs + _swap_halves(dkv_ * sn)
        kvv = kv_ref[...]
        dkr = norm_bwd(dkn, jnp.where(nope, kvv, kpe_ref[...]), gk_ref[...], dgk_ref)
        dkv_ref[...] = jnp.where(nope, dkr, pltpu.roll(dv_ref[...], MNOPE, 1)).astype(BF16)
        dkpe_ref[...] += jnp.where((lane >= MNOPE) & (lane < MQK), dkr, 0.0)

    hd = pl.BlockSpec((ts, LANE), lambda i, h: (i, h))
    shared = lambda col: pl.BlockSpec((ts, LANE), lambda i, h: (i, col))
    gain = pl.BlockSpec((1, LANE), lambda i, h: (0, 0))
    out = jax.ShapeDtypeStruct((t, MH * LANE), BF16)
    return pl.pallas_call(
        body, name=name, grid=(t // ts, MH),
        in_specs=[hd, hd, hd, hd, hd, shared(OFF_KPE // LANE), shared(0), shared(0), gain, gain],
        out_specs=[hd, hd, shared(0), gain, gain],
        out_shape=[out, out, jax.ShapeDtypeStruct((t, LANE), F32), jax.ShapeDtypeStruct((1, LANE), F32),
                   jax.ShapeDtypeStruct((1, LANE), F32)],
        compiler_params=_params(2),
    )(dq, dk, dv, q_raw, kv, proj, cos_t, sin_t, gq, gk)


_NT = (((1,), (1,)), ((), ()))
_TN = (((0,), (0,)), ((), ()))


SOFTMAX_SCALE = MQK ** -0.5
Q_PRESCALE = SOFTMAX_SCALE * float(np.log2(np.e))


def _attn_weights(q, k_ref, lo, tq):
    row = lax.broadcasted_iota(jnp.int32, (tq, tq), 0) // CHUNK
    col = lax.broadcasted_iota(jnp.int32, (tq, tq), 1) // CHUNK
    sd = lax.dot_general(q, k_ref[pl.ds(lo, tq), :], _NT, preferred_element_type=F32)
    sd = jnp.where(col <= row, sd, -1e30)
    m = jnp.max(sd, axis=-1, keepdims=True)
    if lo:
        so = lax.dot_general(q, k_ref[pl.ds(0, lo), :], _NT, preferred_element_type=F32)
        m = jnp.maximum(m, jnp.max(so, axis=-1, keepdims=True))
        eo = jnp.exp2(so - m)
        ed = jnp.exp2(sd - m)
        return eo, ed, 1.0 / (jnp.sum(eo, axis=-1, keepdims=True) + jnp.sum(ed, axis=-1, keepdims=True))
    ed = jnp.exp2(sd - m)
    return None, ed, 1.0 / jnp.sum(ed, axis=-1, keepdims=True)


def _attn_fwd(q, k, v, *, name, tq=256):
    bsz, s, _ = q.shape
    tq = min(tq, s)

    def body(q_ref, k_ref, v_ref, o_ref):
        for i in range(s // tq):
            lo = i * tq
            eo, ed, inv = _attn_weights(q_ref[pl.ds(lo, tq), :], k_ref, lo, tq)
            o = jnp.dot(ed.astype(BF16), v_ref[pl.ds(lo, tq), :], preferred_element_type=F32)
            if lo:
                o += jnp.dot(eo.astype(BF16), v_ref[pl.ds(0, lo), :], preferred_element_type=F32)
            o_ref[pl.ds(lo, tq), :] = (o * inv).astype(BF16)

    spec = pl.BlockSpec((None, s, LANE), lambda b, h: (b, 0, h))
    return pl.pallas_call(
        body, name=name, grid=(bsz, MH), in_specs=[spec, spec, spec], out_specs=spec,
        out_shape=jax.ShapeDtypeStruct((bsz, s, MH * LANE), BF16), compiler_params=_params(2),
    )(q, k, v)


def _attn_bwd(q, k, v, do, *, name, tq=256):
    bsz, s, _ = q.shape
    tq = min(tq, s)

    def body(q_ref, k_ref, v_ref, do_ref, dq_ref, dk_ref, dv_ref):
        dk_ref[...] = jnp.zeros_like(dk_ref)
        dv_ref[...] = jnp.zeros_like(dv_ref)
        for i in range(s // tq):
            lo = i * tq
            here, before = pl.ds(lo, tq), pl.ds(0, lo)
            qv, dov = q_ref[here, :], do_ref[here, :]
            eo, ed, inv = _attn_weights(qv, k_ref, lo, tq)
            do_n = (dov.astype(F32) * inv).astype(BF16)
            dv_ref[here, :] += lax.dot_general(ed.astype(BF16), do_n, _TN, preferred_element_type=F32)
            dpd = lax.dot_general(dov, v_ref[here, :], _NT, preferred_element_type=F32)
            delta = jnp.sum(dpd * ed, axis=-1, keepdims=True)
            if lo:
                dv_ref[before, :] += lax.dot_general(eo.astype(BF16), do_n, _TN, preferred_element_type=F32)
                dpo = lax.dot_general(dov, v_ref[before, :], _NT, preferred_element_type=F32)
                delta += jnp.sum(dpo * eo, axis=-1, keepdims=True)
            delta = delta * inv
            r = inv * SOFTMAX_SCALE
            dsd = (ed * (dpd - delta) * r).astype(BF16)
            dq = jnp.dot(dsd, k_ref[here, :], preferred_element_type=F32)
            dk_ref[here, :] += lax.dot_general(dsd, qv, _TN, preferred_element_type=F32)
            if lo:
                dso = (eo * (dpo - delta) * r).astype(BF16)
                dq += jnp.dot(dso, k_ref[before, :], preferred_element_type=F32)
                dk_ref[before, :] += lax.dot_general(dso, qv, _TN, preferred_element_type=F32)
            dq_ref[here, :] = dq
        dk_ref[...] = dk_ref[...] * (1.0 / Q_PRESCALE)

    spec = pl.BlockSpec((None, s, LANE), lambda b, h: (b, 0, h))
    out = jax.ShapeDtypeStruct((bsz, s, MH * LANE), F32)
    return pl.pallas_call(
        body, name=name, grid=(bsz, MH), in_specs=[spec] * 4, out_specs=[spec] * 3, out_shape=[out, out, out],
        compiler_params=_params(2),
    )(q, k, v, do)


def _adamw(w, g, m, v, *, name, tr=256):
    rows, cols = w.shape
    tr = _tile_rows(rows, tr)

    def body(w_ref, g_ref, m_ref, v_ref, d_ref, nm_ref, nv_ref):
        d_ref[...], nm_ref[...], nv_ref[...] = _adamw_update(w_ref[...], g_ref[...], m_ref[...], v_ref[...])

    spec = pl.BlockSpec((tr, cols), lambda i: (i, 0))
    out = jax.ShapeDtypeStruct((rows, cols), F32)
    return pl.pallas_call(body, name=name, grid=(rows // tr,), in_specs=[spec] * 4, out_specs=[spec] * 3,
                          out_shape=[out, out, out], compiler_params=_params(1))(w, g, m, v)


def _tile_rows(rows, target):
    if rows <= target:
        return rows
    best = 8
    for t in range(8, target + 1, 8):
        if rows % t == 0:
            best = t
    return best


def _adamw_update(w, g, m, v):
    nm = ADAM_B1 * m + (1.0 - ADAM_B1) * g
    nv = ADAM_B2 * v + (1.0 - ADAM_B2) * (g * g)
    m_hat = nm / (1.0 - ADAM_B1 ** ADAM_STEP)
    v_hat = nv / (1.0 - ADAM_B2 ** ADAM_STEP)
    return -ADAM_LR * (m_hat / (jnp.sqrt(v_hat) + ADAM_EPS) + ADAM_WD * w), nm, nv


def _adamw_halves(w, m, v, mine, theirs, sel, *, name, tr=256):
    rows, cols = w.shape
    tr = _tile_rows(rows // 2, tr)
    nh = rows // 2 // tr

    def body(sel_ref, w_ref, m_ref, v_ref, mine_ref, theirs_ref, g_ref, d_ref, nm_ref, nv_ref):
        lower = pl.program_id(0) < nh
        south = sel_ref[0] == 0
        gv = jnp.where(lower == south, mine_ref[...], theirs_ref[...])
        g_ref[...] = gv
        d_ref[...], nm_ref[...], nv_ref[...] = _adamw_update(w_ref[...], gv, m_ref[...], v_ref[...])

    full = pl.BlockSpec((tr, cols), lambda i, sel_ref: (i, 0))
    half = pl.BlockSpec((tr, cols), lambda i, sel_ref: (i % nh, 0))
    out = jax.ShapeDtypeStruct((rows, cols), F32)
    return pl.pallas_call(
        body, name=name, out_shape=[out] * 4, compiler_params=_params(1),
        grid_spec=pltpu.PrefetchScalarGridSpec(num_scalar_prefetch=1, grid=(rows // tr,),
                                               in_specs=[full, full, full, half, half], out_specs=[full] * 4),
    )(sel, w, m, v, mine, theirs)


def _pair_add(x, sib, sel, *, name, tr=256):
    n, _, rows, cols = x.shape
    tr = _tile_rows(rows, tr)

    def body(sel_ref, x_ref, s_ref, o_ref):
        o_ref[...] = (x_ref[...] + s_ref[...]).astype(BF16)

    spec = pl.BlockSpec((None, tr, cols), lambda j, i, sel_ref: (j, i, 0))
    return pl.pallas_call(
        body, name=name, out_shape=jax.ShapeDtypeStruct((n, rows, cols), BF16), compiler_params=_params(2),
        grid_spec=pltpu.PrefetchScalarGridSpec(
            num_scalar_prefetch=1, grid=(n, rows // tr),
            in_specs=[pl.BlockSpec((None, None, tr, cols), lambda j, i, sel_ref: (j, sel_ref[0], i, 0)), spec],
            out_specs=spec),
    )(sel, x, sib)


def _chip_sum(pair, recv, sel, *, name, tr=256):
    _, rows, cols = pair.shape
    tr = _tile_rows(rows, tr)

    def body(sel_ref, p_ref, r_ref, o_ref):
        acc = p_ref[...].astype(F32)
        for k in range(3):
            acc = acc + r_ref[k].astype(F32)
        o_ref[...] = acc

    return pl.pallas_call(
        body, name=name, out_shape=jax.ShapeDtypeStruct((rows, cols), F32), compiler_params=_params(1),
        grid_spec=pltpu.PrefetchScalarGridSpec(
            num_scalar_prefetch=1, grid=(rows // tr,),
            in_specs=[pl.BlockSpec((None, tr, cols), lambda i, sel_ref: (sel_ref[0], i, 0)),
                      pl.BlockSpec((3, tr, cols), lambda i, sel_ref: (0, i, 0))],
            out_specs=pl.BlockSpec((tr, cols), lambda i, sel_ref: (i, 0))),
    )(sel, pair, recv)


def _me():
    return lax.axis_index("x"), lax.axis_index("y"), lax.axis_index("c")


def _flip(pos, bits):
    x, y, c = pos
    return (x ^ bits[0] if bits[0] else x, y ^ bits[1] if bits[1] else y, c ^ bits[2] if bits[2] else c)


ANY = pl.BlockSpec(memory_space=pl.ANY)


def _all_gather8(x, *, name):
    flips = [((k >> 2) & 1, (k >> 1) & 1, k & 1) for k in range(1, 8)]

    def body(x_ref, out_ref, send_sems, recv_sems, local_sem):
        me = _me()
        slot = lambda p: 4 * p[0] + 2 * p[1] + p[2]
        mine = pltpu.make_async_copy(x_ref, out_ref.at[slot(me)], local_sem)
        mine.start()
        sends = []
        for k, f in enumerate(flips):
            cp = pltpu.make_async_remote_copy(src_ref=x_ref, dst_ref=out_ref.at[slot(me)], send_sem=send_sems.at[k],
                                              recv_sem=recv_sems.at[k], device_id=_flip(me, f), device_id_type=MESH)
            cp.start()
            sends.append(cp)
        for k, f in enumerate(flips):
            peer = _flip(me, f)
            pltpu.make_async_remote_copy(src_ref=x_ref, dst_ref=out_ref.at[slot(peer)], send_sem=send_sems.at[k],
                                         recv_sem=recv_sems.at[k], device_id=peer, device_id_type=MESH).wait_recv()
        for cp in sends:
            cp.wait_send()
        mine.wait()

    return pl.pallas_call(
        body, name=name, in_specs=[ANY], out_specs=ANY, out_shape=jax.ShapeDtypeStruct((8, *x.shape), x.dtype),
        scratch_shapes=[pltpu.SemaphoreType.DMA((7,)), pltpu.SemaphoreType.DMA((7,)), pltpu.SemaphoreType.DMA])(x)


CHIP_FLIPS = [(1, 0, 0), (0, 1, 0), (1, 1, 0)]


def _chip():
    return 2 * lax.axis_index("x") + lax.axis_index("y")


def _gather_weights(xs, *, name):
    n = len(xs)
    halves = [x.reshape(2, x.shape[0] // 2, x.shape[1]) for x in xs]

    def body(*refs):
        x_refs, out_refs, (send_sems, recv_sems) = refs[:n], refs[n:2 * n], refs[2 * n:]
        me = _me()
        sib = _flip(me, (0, 0, 1))
        slot = lambda p: 2 * p[0] + p[1]
        my_half, their_half = me[2], 1 - me[2]

        def copy(k, src, dst, to):
            return pltpu.make_async_remote_copy(src_ref=src, dst_ref=dst, send_sem=send_sems.at[k],
                                                recv_sem=recv_sems.at[k], device_id=to, device_id_type=MESH)

        sends = []
        for i in range(n):
            for k, f in enumerate(CHIP_FLIPS):
                cp = copy(6 * i + k, x_refs[i].at[my_half], out_refs[i].at[slot(me), my_half], _flip(me, f))
                cp.start()
                sends.append(cp)
        for i in range(n):
            for k, f in enumerate(CHIP_FLIPS):
                landed = out_refs[i].at[slot(_flip(me, f)), my_half]
                copy(6 * i + k, landed, landed, me).wait_recv()
                cp = copy(6 * i + 3 + k, landed, landed, sib)
                cp.start()
                sends.append(cp)
        for i in range(n):
            for k, f in enumerate(CHIP_FLIPS):
                from_sib = out_refs[i].at[slot(_flip(me, f)), their_half]
                copy(6 * i + 3 + k, from_sib, from_sib, sib).wait_recv()
        for cp in sends:
            cp.wait_send()

    outs = pl.pallas_call(
        body, name=name, in_specs=[ANY] * n, out_specs=[ANY] * n,
        out_shape=[jax.ShapeDtypeStruct((4, *h.shape), h.dtype) for h in halves],
        scratch_shapes=[pltpu.SemaphoreType.DMA((6 * n,)), pltpu.SemaphoreType.DMA((6 * n,))])(*halves)
    chip = _chip()
    return [[jnp.where(chip == j, x, o.reshape(4, *x.shape)[j]) for j in range(4)] for o, x in zip(outs, xs)]


def _pair_swap_halves(xs, *, name):
    n = len(xs)

    def body(*refs):
        x_refs, out_refs, (send_sems, recv_sems) = refs[:n], refs[n:2 * n], refs[2 * n:]
        me = _me()
        sib = _flip(me, (0, 0, 1))
        copies = [pltpu.make_async_remote_copy(src_ref=x_refs[i].at[:, 1 - me[2]], dst_ref=out_refs[i],
                                               send_sem=send_sems.at[i], recv_sem=recv_sems.at[i], device_id=sib,
                                               device_id_type=MESH) for i in range(n)]
        for cp in copies:
            cp.start()
        for cp in copies:
            cp.wait()

    return pl.pallas_call(
        body, name=name, in_specs=[ANY] * n, out_specs=[ANY] * n,
        out_shape=[jax.ShapeDtypeStruct((x.shape[0], *x.shape[2:]), x.dtype) for x in xs],
        scratch_shapes=[pltpu.SemaphoreType.DMA((n,)), pltpu.SemaphoreType.DMA((n,))])(*xs)


def _scatter_chips(ps, *, name):
    n = len(ps)

    def body(*refs):
        p_refs, out_refs, (send_sems, recv_sems) = refs[:n], refs[n:2 * n], refs[2 * n:]
        me = _me()
        slot = lambda q: 2 * q[0] + q[1]
        sends = []
        for i in range(n):
            for k, f in enumerate(CHIP_FLIPS):
                peer = _flip(me, f)
                cp = pltpu.make_async_remote_copy(src_ref=p_refs[i].at[slot(peer)], dst_ref=out_refs[i].at[k],
                                                  send_sem=send_sems.at[3 * i + k], recv_sem=recv_sems.at[3 * i + k],
                                                  device_id=peer, device_id_type=MESH)
                cp.start()
                sends.append(cp)
        for cp in sends:
            cp.wait()

    return pl.pallas_call(
        body, name=name, in_specs=[ANY] * n, out_specs=[ANY] * n,
        out_shape=[jax.ShapeDtypeStruct((3, *p.shape[1:]), p.dtype) for p in ps],
        scratch_shapes=[pltpu.SemaphoreType.DMA((3 * n,)), pltpu.SemaphoreType.DMA((3 * n,))])(*ps)


def _pair_swap(hs, *, name):
    n = len(hs)

    def body(*refs):
        h_refs, out_refs, (send_sems, recv_sems) = refs[:n], refs[n:2 * n], refs[2 * n:]
        sib = _flip(_me(), (0, 0, 1))
        copies = [pltpu.make_async_remote_copy(src_ref=h_refs[i], dst_ref=out_refs[i], send_sem=send_sems.at[i],
                                               recv_sem=recv_sems.at[i], device_id=sib, device_id_type=MESH)
                  for i in range(n)]
        for cp in copies:
            cp.start()
        for cp in copies:
            cp.wait()

    return pl.pallas_call(
        body, name=name, in_specs=[ANY] * n, out_specs=[ANY] * n,
        out_shape=[jax.ShapeDtypeStruct(h.shape, h.dtype) for h in hs],
        scratch_shapes=[pltpu.SemaphoreType.DMA((n,)), pltpu.SemaphoreType.DMA((n,))])(*hs)


HBM = pl.BlockSpec(memory_space=pltpu.HBM)
SEM = pl.BlockSpec(memory_space=pltpu.SEMAPHORE)
EFFECT = pltpu.SideEffectType.DATAFLOW_SIDE_EFFECTING


def _plan_copies(plan, refs, send_sems, recv_sems):
    return [pltpu.make_async_remote_copy(src_ref=src, dst_ref=dst, send_sem=send_sems.at[k], recv_sem=recv_sems.at[k],
                                         device_id=to, device_id_type=MESH) for k, (src, dst, to) in enumerate(plan(refs))]


def _rdma_start(arrays, n_copies, plan, deps, *, name):
    n, nd = len(arrays), len(deps)

    def body(*refs):
        for cp in _plan_copies(plan, refs[:n], refs[n + nd], refs[n + nd + 1]):
            cp.start()
        refs[-1][...] = jnp.zeros_like(refs[-1])

    outs = pl.pallas_call(
        body, name=name,
        out_shape=(pltpu.SemaphoreType.DMA((n_copies,)), pltpu.SemaphoreType.DMA((n_copies,)),
                   *[pltpu.HBM(a.shape, a.dtype) for a in arrays], jax.ShapeDtypeStruct((8, LANE), F32)),
        in_specs=[HBM] * n + [ANY] * nd, out_specs=(SEM, SEM, *[HBM] * n, pl.BlockSpec(memory_space=pltpu.VMEM)),
        input_output_aliases={i: i + 2 for i in range(n)}, compiler_params=pltpu.CompilerParams(has_side_effects=EFFECT),
    )(*[pltpu.with_memory_space_constraint(a, pltpu.HBM) for a in arrays], *deps)
    return outs[0], outs[1], list(outs[2:2 + n]), outs[-1]


def _rdma_wait(send_sems, recv_sems, arrays, plan, after, *, name):
    n = len(arrays)

    def body(*refs):
        for cp in _plan_copies(plan, refs[:n], refs[n], refs[n + 1]):
            cp.wait_send()
            cp.wait_recv()

    return list(pl.pallas_call(
        body, name=name, out_shape=tuple(pltpu.HBM(a.shape, a.dtype) for a in arrays),
        in_specs=[HBM] * n + [SEM, SEM, ANY], out_specs=tuple([HBM] * n), input_output_aliases={i: i for i in range(n)},
        compiler_params=pltpu.CompilerParams(has_side_effects=EFFECT),
    )(*arrays, send_sems, recv_sems, after))


def _gather_plan(n):
    def plan(refs):
        me = _me()
        slot = 2 * me[0] + me[1]
        return [(refs[i].at[me[2]], refs[n + i].at[slot, me[2]], _flip(me, f)) for i in range(n) for f in CHIP_FLIPS]
    return plan


def _scatter_plan(n):
    def plan(refs):
        me = _me()
        out = []
        for i in range(n):
            for k, f in enumerate(CHIP_FLIPS):
                peer = _flip(me, f)
                out.append((refs[i].at[2 * peer[0] + peer[1]], refs[n + i].at[k], peer))
        return out
    return plan


def _pair_fill(lands, *, name):
    n = len(lands)

    def body(*refs):
        in_refs, (send_sems, recv_sems) = refs[:n], refs[2 * n:]
        me = _me()
        sib = _flip(me, (0, 0, 1))
        copies = []
        for i in range(n):
            for k, f in enumerate(CHIP_FLIPS):
                peer = _flip(me, f)
                slot = 2 * peer[0] + peer[1]
                mine, theirs = in_refs[i].at[slot, me[2]], in_refs[i].at[slot, 1 - me[2]]
                cp = pltpu.make_async_remote_copy(src_ref=mine, dst_ref=mine, send_sem=send_sems.at[3 * i + k],
                                                  recv_sem=recv_sems.at[3 * i + k], device_id=sib, device_id_type=MESH)
                cp.start()
                copies.append((cp, pltpu.make_async_remote_copy(
                    src_ref=mine, dst_ref=theirs, send_sem=send_sems.at[3 * i + k], recv_sem=recv_sems.at[3 * i + k],
                    device_id=sib, device_id_type=MESH)))
        for cp, arrival in copies:
            arrival.wait_recv()
            cp.wait_send()

    return list(pl.pallas_call(
        body, name=name, in_specs=[ANY] * n, out_specs=[ANY] * n,
        out_shape=[jax.ShapeDtypeStruct(a.shape, a.dtype) for a in lands], input_output_aliases={i: i for i in range(n)},
        scratch_shapes=[pltpu.SemaphoreType.DMA((3 * n,)), pltpu.SemaphoreType.DMA((3 * n,))])(*lands))


def _own_and_landed(lands, xs):
    chip = _chip()
    return [[jnp.where(chip == j, x, o.reshape(4, *x.shape)[j]) for j in range(4)] for o, x in zip(lands, xs)]


BIG = (("w_in", (D, IN_WIDTH // 4), 1), ("gla_w_o", (D // 4, D), 0), ("mla_w_uq", (MQR, MH * MQK // 4), 1),
       ("mla_w_ukv", (MKVR, MH * (MNOPE + MVD) // 4), 1), ("mla_w_o", (D // 4, D), 0), ("w_out", (D // 4, D), 0),
       ("mlp_w1", (D, DFF // 4), 1), ("mlp_w2", (DFF // 4, D), 0))
ADA_SHARD = (D, 6 * D // 4)
SMALL = (("b_ada", 6 * D), ("norm1_g", D), ("b_merge", 2 * D), ("gla_b_alpha", GH * GDK), ("gla_out_norm_g", GDV),
         ("mla_q_lat_g", MQR), ("mla_kv_lat_g", MKVR), ("mla_qn_g", MQK), ("mla_kn_g", MQK), ("norm2_g", D))


SMALL_ROWS, SMALL_COLS = 32, 2 * D
W_ALPHA_ROW = 16
SMALL_RED = tuple((n, k) for n, k in SMALL if n != "b_ada")


def _pack_small(grads, d_w_alpha, *, name):
    def body(*refs):
        g_refs, wa_ref, out_ref = refs[:-2], refs[-2], refs[-1]
        out_ref[...] = jnp.zeros_like(out_ref)
        for i, ((_, k), g_ref) in enumerate(zip(SMALL_RED, g_refs)):
            out_ref[i:i + 1, 0:k] = g_ref[...]
        out_ref[W_ALPHA_ROW:W_ALPHA_ROW + GLR, 0:GH * GDK] = wa_ref[...]

    return pl.pallas_call(body, name=name, out_shape=jax.ShapeDtypeStruct((SMALL_ROWS, SMALL_COLS), F32))(*grads, d_w_alpha)


def _small_update(gathered, dmod_all, sel, wmv, *, name):
    names = [n for n, _ in SMALL] + ["gla_w_alpha"]
    n_par = len(names)

    def body(sel_ref, g_ref, dmod_ref, *refs):
        in_refs, out_refs, acc = refs[:3 * n_par], refs[3 * n_par:-1], refs[-1]
        total = g_ref[0]
        for j in range(1, 8):
            total = total + g_ref[j]
        acc[...] = total
        row = {n: i for i, (n, _) in enumerate(SMALL_RED)}
        for p, name_p in enumerate(names):
            w_ref, m_ref, v_ref = in_refs[3 * p:3 * p + 3]
            if name_p == "b_ada":
                gv = jnp.sum(dmod_ref[...], axis=0, keepdims=True)
            elif name_p == "gla_w_alpha":
                gv = jnp.zeros((GLR, GDK), F32)
                for j in range(4):
                    blk = acc[W_ALPHA_ROW:W_ALPHA_ROW + GLR, j * GDK:(j + 1) * GDK]
                    gv = gv + jnp.where(sel_ref[0] == j, blk, 0.0)
            else:
                gv = acc[row[name_p]:row[name_p] + 1, 0:w_ref.shape[1]]
            o = out_refs[4 * p:4 * p + 4]
            o[0][...] = gv
            o[1][...], o[2][...], o[3][...] = _adamw_update(w_ref[...], gv, m_ref[...], v_ref[...])

    flat = [a for t in wmv for a in t]
    out_shape = [jax.ShapeDtypeStruct(t[0].shape, F32) for t in wmv for _ in range(4)]
    vmem = pl.BlockSpec(memory_space=pltpu.VMEM)
    outs = pl.pallas_call(
        body, name=name, out_shape=out_shape, in_specs=[pl.BlockSpec(memory_space=pltpu.SMEM), vmem, vmem] + [vmem] * len(flat),
        out_specs=[vmem] * len(out_shape), scratch_shapes=[pltpu.VMEM((SMALL_ROWS, SMALL_COLS), F32)],
    )(sel, gathered, dmod_all, *flat)
    return {n: tuple(outs[4 * p:4 * p + 4]) for p, n in enumerate(names)}


def _full_weights(gathered):
    w = {name: jnp.concatenate(gathered[name], axis=axis) for name, _, axis in BIG if name in gathered}
    if "w_in" in w:
        wi = w["w_in"]
        zeros = lambda n: jnp.zeros((D, n), wi.dtype)
        w["w_in"] = jnp.concatenate(
            [wi[:, :3072], wi[:, 3504:5552], wi[:, 3088:3344], wi[:, 3344:3472], wi[:, 3072:3088], zeros(LANE - GLR),
             zeros(MNOPE), wi[:, 3472:3504], zeros(LANE - MQK)], axis=1)
    if "mla_w_uq" in w:
        w["mla_w_uq"] = jnp.pad(w["mla_w_uq"].reshape(MQR, MH, MQK), ((0, 0), (0, 0), (0, LANE - MQK))).reshape(MQR, MH * LANE)
    if "mla_w_o" in w:
        w["mla_w_o"] = jnp.pad(w["mla_w_o"].reshape(MH, MVD, D), ((0, 0), (0, LANE - MVD), (0, 0))).reshape(MH * LANE, D)
    return w


def _grad_slots(g):
    g = dict(g)
    if "w_in" in g:
        gi = g["w_in"]
        g["w_in"] = jnp.concatenate(
            [gi[:, :3072], gi[:, OFF_A:OFF_A + GLR], gi[:, OFF_CQ:OFF_CQ + MQR], gi[:, OFF_CKV:OFF_CKV + MKVR],
             gi[:, OFF_KPE + MNOPE:OFF_KPE + MQK], gi[:, OFF_MA:OFF_MA + 2 * D]], axis=1)
    if "mla_w_uq" in g:
        g["mla_w_uq"] = g["mla_w_uq"].reshape(MQR, MH, LANE)[:, :, :MQK].reshape(MQR, MH * MQK)
    if "mla_w_o" in g:
        g["mla_w_o"] = g["mla_w_o"].reshape(MH, LANE, D)[:, :MVD].reshape(MH * MVD, D)
    out = {}
    for name, (rows, cols), axis in BIG:
        if name not in g:
            continue
        a = g[name]
        a = a.reshape(4, rows, cols) if axis == 0 else jnp.transpose(a.reshape(rows, 4, cols), (1, 0, 2))
        out[name] = a.reshape(4, 2, rows // 2, cols)
    return out


def _rope_tables(positions):
    freqs = ROPE_THETA ** (-jnp.arange(0, MROPE, 2, dtype=F32) / MROPE)
    lane = np.arange(LANE)
    in_rope = (lane >= MNOPE) & (lane < MQK)
    freq_lane = jnp.where(in_rope, freqs[(lane - MNOPE) % (MROPE // 2)], 0.0)
    sign = np.where(in_rope, np.where(lane < MNOPE + MROPE // 2, -1.0, 1.0), 0.0).astype(np.float32)
    ang = positions.astype(F32).reshape(-1, 1) * freq_lane[None, :]
    return jnp.cos(ang), jnp.sin(ang) * sign[None, :]


def _local_step(x, positions, mod, target, w, small, more_weights=None, on_grads=None):
    kept = {}
    if on_grads is None:
        on_grads = lambda tag, grads: kept.update(grads)
    bsz, s, _ = x.shape
    t = bsz * s
    tt = _tile(t, 1024)
    shift1, scale1, gate1, shift2, scale2, gate2 = [mod[:, None, i * D:(i + 1) * D] for i in range(6)]
    cos_t, sin_t = _rope_tables(positions)
    w_alpha_p = jnp.pad(small["gla_w_alpha"], ((0, LANE - GLR), (0, 0)))
    gq = jnp.pad(small["mla_qn_g"], ((0, 0), (0, LANE - MQK)))
    gk = jnp.pad(small["mla_kn_g"], ((0, 0), (0, LANE - MQK)))
    flat2 = lambda a: a.reshape(t, a.shape[-1])
    bsd = lambda a: a.reshape(bsz, s, a.shape[-1])

    h = _norm_mod(x, small["norm1_g"], scale1, shift1, name="norm1")
    proj = _mm(flat2(h), w["w_in"], name="proj", tn=1152)
    proj3 = bsd(proj)
    o, o_gated, states = _gla_fwd(proj3, w_alpha_p, small["gla_b_alpha"], small["gla_out_norm_g"], name="gla_fwd")
    if more_weights is not None:
        w = {**w, **more_weights(o_gated)}
    y_a = _mm(flat2(o_gated), w["gla_w_o"], name="gla_out")
    cq_n, ckv_n = _lat_norm(proj, small["mla_q_lat_g"], small["mla_kv_lat_g"], name="lat_norm")
    q_raw = _mm(cq_n, w["mla_w_uq"], name="mla_uq")
    kv = _mm(ckv_n, w["mla_w_ukv"], name="mla_ukv")
    qf, kf, vf = _qk_prep(q_raw, kv, proj, cos_t, sin_t, gq * Q_PRESCALE, gk, name="qk_prep")
    o_attn = _attn_fwd(bsd(qf), bsd(kf), bsd(vf), name="attn_fwd")
    y_b = _mm(flat2(o_attn), w["mla_w_o"], name="mla_out")
    mixed_in = _merge_fwd(proj3, small["b_merge"], bsd(y_a), bsd(y_b), name="merge_fwd")
    mixed = _mm(flat2(mixed_in), w["w_out"], name="w_out")
    x1, h2 = _resid_norm_mod(x, bsd(mixed), gate1, small["norm2_g"], scale2, shift2, name="norm2")

    def sqrelu(acc, ex, outs):
        outs[0][...] = acc
        r = jnp.maximum(acc, 0.0)
        outs[1][...] = (r * r).astype(BF16)

    a1, r = _mm(flat2(h2), w["mlp_w1"], name="mlp1", epilogue=sqrelu,
                out_shape=[jax.ShapeDtypeStruct((t, DFF), F32), jax.ShapeDtypeStruct((t, DFF), BF16)],
                out_specs=[_tile_spec(tt, 1024), _tile_spec(tt, 1024)])
    ff = _mm(r, w["mlp_w2"], name="mlp2")
    dy, dff, dgate2, loss_part = _loss_head(x1, bsd(ff), gate2, target, name="loss_head")

    g = {}

    def relu2_bwd(acc, ex, outs):
        outs[0][...] = (acc * (2.0 * jnp.maximum(ex[0][...], 0.0))).astype(BF16)

    dff2 = flat2(dff)
    da1 = _mm(dff2, w["mlp_w2"], tb=True, name="mlp2_dx", epilogue=relu2_bwd, extras=(a1,),
              extra_specs=(_tile_spec(tt, 1024),), out_shape=jax.ShapeDtypeStruct((t, DFF), BF16),
              out_specs=_tile_spec(tt, 1024))
    g["mlp_w2"] = _mm(r, dff2, ta=True, name="mlp2_dw")
    dh2 = _mm(da1, w["mlp_w1"], tb=True, name="mlp1_dx")
    g["mlp_w1"] = _mm(flat2(h2), da1, ta=True, name="mlp1_dw")
    token = on_grads("mlp", {n: g.pop(n) for n in ("mlp_w2", "mlp_w1")})
    if token is not None:
        gate1 = gate1 + token[0, 0]
    dx1, dscale2, dshift2, dg2, dgate1, dmixed = _norm_mod_bwd(
        bsd(dh2), x1, dy, small["norm2_g"], scale2, gate1, bsd(mixed), name="norm2_bwd")
    dmixed2 = flat2(dmixed)
    dmi = _mm(dmixed2, w["w_out"], tb=True, name="w_out_dx")
    g["w_out"] = _mm(flat2(mixed_in), dmixed2, ta=True, name="w_out_dw")
    dy_a, dy_b, dl_a, dl_b, db_a, db_b = _merge_bwd(bsd(dmi), proj3, small["b_merge"], bsd(y_a), bsd(y_b), name="merge_bwd")
    dy_a2, dy_b2 = flat2(dy_a), flat2(dy_b)
    dog = _mm(dy_a2, w["gla_w_o"], tb=True, name="gla_out_dx")
    g["gla_w_o"] = _mm(flat2(o_gated), dy_a2, ta=True, name="gla_out_dw")
    dq_g, dk_g, dv_g, dg_g, dlog, db_alpha, d_ong = _gla_bwd(
        bsd(dog), o, states, proj3, w_alpha_p, small["gla_b_alpha"], small["gla_out_norm_g"], name="gla_bwd")
    dlog2 = flat2(dlog)
    da_p = _mm(dlog2, w_alpha_p, tb=True, out_dtype=BF16, name="alpha_dx")
    d_w_alpha = _mm(proj[:, OFF_A:OFF_A + LANE], dlog2, ta=True, name="alpha_dw")[:GLR]
    do_attn = _mm(dy_b2, w["mla_w_o"], tb=True, out_dtype=BF16, name="mla_out_dx")
    g["mla_w_o"] = _mm(flat2(o_attn), dy_b2, ta=True, name="mla_out_dw")
    dqf, dkf, dvf = _attn_bwd(bsd(qf), bsd(kf), bsd(vf), bsd(do_attn), name="attn_bwd")
    dq_raw, dkv, dkpe, dgq, dgk = _qk_prep_bwd(flat2(dqf), flat2(dkf), flat2(dvf), q_raw, kv, proj, cos_t, sin_t, gq, gk,
                                                name="qk_prep_bwd")
    dcq_n = _mm(dq_raw, w["mla_w_uq"], tb=True, name="mla_uq_dx")
    g["mla_w_uq"] = _mm(cq_n, dq_raw, ta=True, name="mla_uq_dw")
    dckv_n = _mm(dkv, w["mla_w_ukv"], tb=True, name="mla_ukv_dx")
    g["mla_w_ukv"] = _mm(ckv_n, dkv, ta=True, name="mla_ukv_dw")
    token = on_grads("mix", {n: g.pop(n) for n in ("w_out", "gla_w_o", "mla_w_o", "mla_w_uq", "mla_w_ukv")})
    q_lat_g = small["mla_q_lat_g"] if token is None else small["mla_q_lat_g"] + token[0:1, 0:1]
    dcq, dckv, dg_qlat, dg_kvlat = _lat_norm_bwd(dcq_n, dckv_n, proj, q_lat_g, small["mla_kv_lat_g"],
                                                  name="lat_norm_bwd")
    dproj = jnp.concatenate([flat2(dq_g), flat2(dk_g), flat2(dv_g), flat2(dg_g), flat2(dl_a), flat2(dl_b), dcq, dckv,
                             da_p, dkpe.astype(BF16)], axis=1)
    dh = _mm(dproj, w["w_in"], tb=True, name="proj_dx", tk=1152)
    g["w_in"] = _mm(flat2(h), dproj, ta=True, name="proj_dw", tn=1152)
    grad_x, dscale1, dshift1, dg1 = _norm_mod_bwd(bsd(dh), x, dx1, small["norm1_g"], scale1, name="norm1_bwd")

    dmod = jnp.concatenate([dshift1, dscale1, dgate1, dshift2, dscale2, dgate2], axis=-1).reshape(bsz, 6 * D)
    gs = {"norm1_g": dg1, "b_merge": jnp.concatenate([db_a, db_b], axis=1), "gla_b_alpha": db_alpha,
          "gla_out_norm_g": d_ong, "mla_q_lat_g": dg_qlat, "mla_kv_lat_g": dg_kvlat, "mla_qn_g": dgq[:, :MQK],
          "mla_kn_g": dgk[:, :MQK], "norm2_g": dg2}
    return loss_part[0, 0], grad_x, dmod, {**kept, **g}, gs, d_w_alpha


def kernel(x, c, positions, w_ada, b_ada, norm1_g, w_in, b_merge, gla_w_alpha, gla_b_alpha, gla_out_norm_g, gla_w_o, mla_q_lat_g, mla_w_uq, mla_kv_lat_g, mla_w_ukv, mla_qn_g, mla_kn_g, mla_w_o, w_out, norm2_g, mlp_w1, mlp_w2, loss_target, m_w_ada, m_b_ada, m_norm1_g, m_w_in, m_b_merge, m_gla_w_alpha, m_gla_b_alpha, m_gla_out_norm_g, m_gla_w_o, m_mla_q_lat_g, m_mla_w_uq, m_mla_kv_lat_g, m_mla_w_ukv, m_mla_qn_g, m_mla_kn_g, m_mla_w_o, m_w_out, m_norm2_g, m_mlp_w1, m_mlp_w2, v_w_ada, v_b_ada, v_norm1_g, v_w_in, v_b_merge, v_gla_w_alpha, v_gla_b_alpha, v_gla_out_norm_g, v_gla_w_o, v_mla_q_lat_g, v_mla_w_uq, v_mla_kv_lat_g, v_mla_w_ukv, v_mla_qn_g, v_mla_kn_g, v_mla_w_o, v_w_out, v_norm2_g, v_mlp_w1, v_mlp_w2):
    args = dict(locals())
    names_big = [n for n, _, _ in BIG]
    names_small = [n for n, _ in SMALL]
    bsz = x.shape[0]
    ax, ay, ac = lax.axis_index("x"), lax.axis_index("y"), lax.axis_index("c")
    chip = 2 * ax + ay
    dev = 2 * chip + ac

    small = {n: args[n] for n in names_small}
    sel_c = jnp.reshape(ac, (1,)).astype(jnp.int32)
    sel_chip = jnp.reshape(chip, (1,)).astype(jnp.int32)
    w_alpha_all = _all_gather8(gla_w_alpha[0], name="comm_w_alpha")
    small["gla_w_alpha"] = jnp.concatenate([w_alpha_all[2 * j] for j in range(4)], axis=1)

    c_all = _all_gather8(c, name="comm_c").reshape(8 * bsz, D)

    def add_bias(acc, ex, outs):
        outs[0][...] = acc + ex[0][...]

    silu = lambda v: v * _sigmoid(v)
    b_ada_mine = lax.dynamic_slice(b_ada, (0, chip * ADA_SHARD[1]), (1, ADA_SHARD[1]))
    mod_part = _mm(c_all, w_ada[0], name="ada", tn=512, a_fn=silu, epilogue=add_bias, extras=(b_ada_mine,),
                   extra_specs=(pl.BlockSpec((1, 512), lambda i, j, k: (0, j)),),
                   out_shape=jax.ShapeDtypeStruct((8 * bsz, ADA_SHARD[1]), F32), out_specs=_tile_spec(8 * bsz, 512))
    mod_all = _all_gather8(mod_part, name="comm_mod")
    mod_rows = lax.dynamic_slice(mod_all, (0, dev * bsz, 0), (8, bsz, ADA_SHARD[1]))
    mod = jnp.concatenate([mod_rows[2 * j] for j in range(4)], axis=1)

    shards = {n: args[n][0].astype(BF16) for n in names_big}
    later = [n for n in names_big if n != "w_in"]
    w = _full_weights({"w_in": _gather_weights([shards["w_in"]], name="comm_w_in")[0]})
    xs = [shards[n].reshape(2, shards[n].shape[0] // 2, shards[n].shape[1]) for n in later]
    lands = [lax.empty((4, *xh.shape), BF16) for xh in xs]
    gather_plan = _gather_plan(len(later))
    w_sems = _rdma_start(xs + lands, 3 * len(later), gather_plan, (w["w_in"],), name="comm_weights_start")
    mod = mod + w_sems[3][0, 0]

    def more_weights(after):
        arrs = _rdma_wait(w_sems[0], w_sems[1], w_sems[2], gather_plan, after, name="comm_weights_wait")
        filled = _pair_fill(arrs[len(later):], name="comm_weights_pair")
        return _full_weights(dict(zip(later, _own_and_landed(filled, [shards[n] for n in later]))))

    in_flight = []

    def reduce_start(tag, grads):
        names = list(grads)
        parts = [_grad_slots(grads)[n] for n in names]
        sib_halves = _pair_swap_halves(parts, name="comm_pair_sum_" + tag)
        pairs = [_pair_add(p, s, sel_c, name="pair_add_" + n) for n, p, s in zip(names, parts, sib_halves)]
        recvs = [lax.empty((3, *p.shape[1:]), BF16) for p in pairs]
        plan = _scatter_plan(len(names))
        sems = _rdma_start(pairs + recvs, 3 * len(names), plan, (), name="comm_scatter_start_" + tag)
        in_flight.append((tag, names, plan, sems))
        return sems[3]

    loss_part, grad_x, dmod, g, gs, d_w_alpha = _local_step(x, positions, mod, loss_target, w, small,
                                                            more_weights, reduce_start)
    loss = lax.psum(loss_part * (0.5 / D), ("x", "y", "c"))

    dmod_all = _all_gather8(dmod, name="comm_dmod").reshape(8 * bsz, 6 * D)
    dmod_mine = lax.dynamic_slice(dmod_all, (0, chip * ADA_SHARD[1]), (8 * bsz, ADA_SHARD[1]))
    g_w_ada = _mm(c_all, dmod_mine, ta=True, a_fn=silu, name="ada_dw")

    gs_packed = _pack_small([gs[n] for n, _ in SMALL_RED], d_w_alpha, name="pack_small")
    gs_all = _all_gather8(gs_packed, name="comm_small")
    wmv = [(args[n], args["m_" + n], args["v_" + n]) for n in names_small]
    wmv.append((gla_w_alpha[0], m_gla_w_alpha[0], v_gla_w_alpha[0]))
    res = _small_update(gs_all, dmod_all, sel_chip, wmv, name="small_update")

    part = _grad_slots(g)["w_in"]
    sib_half = _pair_swap_halves([part], name="comm_pair_sum_in")[0]
    pair = _pair_add(part, sib_half, sel_c, name="pair_add_w_in")
    from_chips = _scatter_chips([pair], name="comm_scatter_in")[0]
    half_of = {"w_in": _chip_sum(pair, from_chips, sel_chip, name="chip_sum_w_in")}
    for tag, names, plan, sems in in_flight:
        arrs = _rdma_wait(sems[0], sems[1], sems[2], plan, grad_x, name="comm_scatter_wait_" + tag)
        for n, p, r in zip(names, arrs[:len(names)], arrs[len(names):]):
            half_of[n] = _chip_sum(p, r, sel_chip, name="chip_sum_" + n)
    halves = [half_of[n] for n in names_big]
    theirs = _pair_swap(halves, name="comm_pair_join")
    for n, mine, other in zip(names_big, halves, theirs):
        res[n] = _adamw_halves(args[n][0], args["m_" + n][0], args["v_" + n][0], mine, other, sel_c, name="adamw_" + n)
    res["w_ada"] = (g_w_ada, *_adamw(w_ada[0], g_w_ada, m_w_ada[0], v_w_ada[0], name="adamw_w_ada"))

    order = ["w_ada", "b_ada", "norm1_g", "w_in", "b_merge", "gla_w_alpha", "gla_b_alpha", "gla_out_norm_g", "gla_w_o",
             "mla_q_lat_g", "mla_w_uq", "mla_kv_lat_g", "mla_w_ukv", "mla_qn_g", "mla_kn_g", "mla_w_o", "w_out",
             "norm2_g", "mlp_w1", "mlp_w2"]
    named = lambda k: [res[n][k].reshape(args[n].shape) for n in order]
    return (loss, grad_x, *named(0), *named(1), *named(2), *named(3))
```

```python
import functools

import jax
import jax.numpy as jnp
import numpy as np
from jax import lax
from jax.experimental import pallas as pl
from jax.experimental.pallas import tpu as pltpu

F32 = jnp.float32
BF16 = jnp.bfloat16
MESH = pl.DeviceIdType.MESH

D = 1024
CHUNK = 64
EPS = 1e-6
GH, GDK, GDV, GLR, GTAU = 4, 128, 256, 16, 16.0
MH, MQR, MKVR, MNOPE, MROPE, MVD = 16, 256, 128, 64, 32, 64
MQK = MNOPE + MROPE
DFF = 4 * D
ROPE_THETA = 10000.0
IN_WIDTH = 5552
LANE = 128
OFF_Q, OFF_K, OFF_V, OFF_G, OFF_MA, OFF_MB, OFF_CQ, OFF_CKV, OFF_A, OFF_KPE, PW = (
    0, 512, 1024, 2048, 3072, 4096, 5120, 5376, 5504, 5632, 5760)
ADAM_LR, ADAM_B1, ADAM_B2, ADAM_EPS, ADAM_WD, ADAM_STEP = 0.001, 0.9, 0.999, 1e-08, 0.01, 10
VMEM_LIMIT = 48 * 1024 * 1024


def _params(n_axes):
    return pltpu.CompilerParams(dimension_semantics=("arbitrary",) * n_axes, vmem_limit_bytes=VMEM_LIMIT)


def _tile(n, target):
    if n <= target:
        return n
    best = None
    for t in range(LANE, target + 1, LANE):
        if n % t == 0:
            best = t
    assert best is not None, (n, target)
    return best


def _sigmoid(x):
    return 1.0 / (1.0 + jnp.exp(-x))


def _mm(a, b, *, name, ta=False, tb=False, out_dtype=F32, tm=1024, tn=1024, tk=1024,
        epilogue=None, extras=(), extra_specs=(), out_shape=None, out_specs=None, a_fn=None):
    if ta:
        kdim, m = a.shape
    else:
        m, kdim = a.shape
    if tb:
        n, k2 = b.shape
    else:
        k2, n = b.shape
    assert kdim == k2, (a.shape, b.shape)
    tm, tn, tk = _tile(m, tm), _tile(n, tn), _tile(kdim, tk)
    nk = kdim // tk
    a_spec = pl.BlockSpec((tk, tm), lambda i, j, k: (k, i)) if ta else pl.BlockSpec((tm, tk), lambda i, j, k: (i, k))
    b_spec = pl.BlockSpec((tn, tk), lambda i, j, k: (j, k)) if tb else pl.BlockSpec((tk, tn), lambda i, j, k: (k, j))
    dims = (((0 if ta else 1,), (1 if tb else 0,)), ((), ()))
    ne = len(extras)
    if out_shape is None:
        out_shape = jax.ShapeDtypeStruct((m, n), out_dtype)
        out_specs = pl.BlockSpec((tm, tn), lambda i, j, k: (i, j))

    def body(a_ref, b_ref, *rest):
        ex, outs, acc = rest[:ne], rest[ne:-1], rest[-1]
        k = pl.program_id(2)

        @pl.when(k == 0)
        def _():
            acc[...] = jnp.zeros_like(acc)

        av = a_ref[...] if a_fn is None else a_fn(a_ref[...])
        acc[...] += lax.dot_general(av.astype(BF16), b_ref[...].astype(BF16), dims, preferred_element_type=F32)

        @pl.when(k == nk - 1)
        def _():
            if epilogue is None:
                outs[0][...] = acc[...].astype(outs[0].dtype)
            else:
                epilogue(acc[...], ex, outs)

    return pl.pallas_call(
        body, name=name, grid=(m // tm, n // tn, nk),
        in_specs=[a_spec, b_spec, *extra_specs], out_specs=out_specs, out_shape=out_shape,
        scratch_shapes=[pltpu.VMEM((tm, tn), F32)], compiler_params=_params(3),
    )(a, b, *extras)


def _tile_spec(tm, tn):
    return pl.BlockSpec((tm, tn), lambda i, j, k: (i, j))


def _rms(x, g):
    r = lax.rsqrt(jnp.mean(x * x, axis=-1, keepdims=True) + EPS)
    return x * r, r


def _row_spec(ts, width, col=0):
    return pl.BlockSpec((None, ts, width), lambda b, i: (b, i, col))


def _vec_spec(width):
    return pl.BlockSpec((None, 1, width), lambda b, i: (b, 0, 0))


def _gain_spec(width):
    return pl.BlockSpec((1, width), lambda b, i: (0, 0))


def _norm_mod(x, g, scale, shift, *, name, ts=256):
    bsz, s, d = x.shape
    ts = min(ts, s)

    def body(x_ref, g_ref, sc_ref, sh_ref, h_ref):
        xh, _ = _rms(x_ref[...], None)
        h_ref[...] = ((xh * g_ref[...]) * (1.0 + sc_ref[...]) + sh_ref[...]).astype(BF16)

    return pl.pallas_call(
        body, name=name, grid=(bsz, s // ts),
        in_specs=[_row_spec(ts, d), _gain_spec(d), _vec_spec(d), _vec_spec(d)],
        out_specs=_row_spec(ts, d), out_shape=jax.ShapeDtypeStruct((bsz, s, d), BF16),
        compiler_params=_params(2),
    )(x, g, scale, shift)


def _resid_norm_mod(x, mixed, gate, g, scale, shift, *, name, ts=256):
    bsz, s, d = x.shape
    ts = min(ts, s)

    def body(x_ref, mx_ref, gt_ref, g_ref, sc_ref, sh_ref, x1_ref, h_ref):
        x1 = x_ref[...] + gt_ref[...] * mx_ref[...]
        x1_ref[...] = x1
        xh, _ = _rms(x1, None)
        h_ref[...] = ((xh * g_ref[...]) * (1.0 + sc_ref[...]) + sh_ref[...]).astype(BF16)

    return pl.pallas_call(
        body, name=name, grid=(bsz, s // ts),
        in_specs=[_row_spec(ts, d), _row_spec(ts, d), _vec_spec(d), _gain_spec(d), _vec_spec(d), _vec_spec(d)],
        out_specs=[_row_spec(ts, d), _row_spec(ts, d)],
        out_shape=[jax.ShapeDtypeStruct((bsz, s, d), F32), jax.ShapeDtypeStruct((bsz, s, d), BF16)],
        compiler_params=_params(2),
    )(x, mixed, gate, g, scale, shift)


def _norm_mod_bwd(dh, xin, resid, g, scale, gate=None, mixed=None, *, name, ts=256):
    bsz, s, d = xin.shape
    ts = min(ts, s)
    gated = gate is not None

    def body(*refs):
        if gated:
            dh_ref, x_ref, rs_ref, g_ref, sc_ref, gt_ref, mx_ref, dx_ref, dsc_ref, dsh_ref, dg_ref, dgt_ref, dmx_ref = refs
        else:
            dh_ref, x_ref, rs_ref, g_ref, sc_ref, dx_ref, dsc_ref, dsh_ref, dg_ref = refs
        b, i = pl.program_id(0), pl.program_id(1)

        @pl.when(i == 0)
        def _():
            dsc_ref[...] = jnp.zeros_like(dsc_ref)
            dsh_ref[...] = jnp.zeros_like(dsh_ref)
            if gated:
                dgt_ref[...] = jnp.zeros_like(dgt_ref)

        @pl.when((i == 0) & (b == 0))
        def _():
            dg_ref[...] = jnp.zeros_like(dg_ref)

        dh_v, gv = dh_ref[...], g_ref[...]
        xh, r = _rms(x_ref[...], None)
        dsc_ref[...] += jnp.sum(dh_v * (xh * gv), axis=0, keepdims=True)
        dsh_ref[...] += jnp.sum(dh_v, axis=0, keepdims=True)
        dn = dh_v * (1.0 + sc_ref[...])
        dg_ref[...] += jnp.sum(dn * xh, axis=0, keepdims=True)
        dxh = dn * gv
        dx = rs_ref[...] + r * (dxh - xh * jnp.mean(dxh * xh, axis=-1, keepdims=True))
        dx_ref[...] = dx
        if gated:
            dgt_ref[...] += jnp.sum(dx * mx_ref[...], axis=0, keepdims=True)
            dmx_ref[...] = (dx * gt_ref[...]).astype(BF16)

    ins = [dh, xin, resid, g, scale]
    in_specs = [_row_spec(ts, d), _row_spec(ts, d), _row_spec(ts, d), _gain_spec(d), _vec_spec(d)]
    out_specs = [_row_spec(ts, d), _vec_spec(d), _vec_spec(d), _gain_spec(d)]
    out_shape = [jax.ShapeDtypeStruct((bsz, s, d), F32), jax.ShapeDtypeStruct((bsz, 1, d), F32),
                 jax.ShapeDtypeStruct((bsz, 1, d), F32), jax.ShapeDtypeStruct((1, d), F32)]
    if gated:
        ins += [gate, mixed]
        in_specs += [_vec_spec(d), _row_spec(ts, d)]
        out_specs += [_vec_spec(d), _row_spec(ts, d)]
        out_shape += [jax.ShapeDtypeStruct((bsz, 1, d), F32), jax.ShapeDtypeStruct((bsz, s, d), BF16)]
    return pl.pallas_call(
        body, name=name, grid=(bsz, s // ts), in_specs=in_specs, out_specs=out_specs, out_shape=out_shape,
        compiler_params=_params(2),
    )(*ins)


def _loss_head(x1, ff, gate2, target, *, name, ts=256):
    bsz, s, d = x1.shape
    ts = min(ts, s)

    def body(x1_ref, ff_ref, gt_ref, t_ref, dy_ref, dff_ref, dgt_ref, loss_ref, acc):
        b, i = pl.program_id(0), pl.program_id(1)

        @pl.when(i == 0)
        def _():
            dgt_ref[...] = jnp.zeros_like(dgt_ref)

        @pl.when((i == 0) & (b == 0))
        def _():
            acc[...] = jnp.zeros_like(acc)

        ffv, gt = ff_ref[...], gt_ref[...]
        diff = (x1_ref[...] + gt * ffv) - t_ref[...]
        acc[...] += jnp.sum((diff * diff).reshape(ts // 8, 8, d), axis=0)
        dy = diff * (1.0 / d)
        dy_ref[...] = dy
        dgt_ref[...] += jnp.sum(dy * ffv, axis=0, keepdims=True)
        dff_ref[...] = (dy * gt).astype(BF16)

        @pl.when((i == pl.num_programs(1) - 1) & (b == pl.num_programs(0) - 1))
        def _():
            loss_ref[...] = jnp.full(loss_ref.shape, jnp.sum(acc[...]), F32)

    return pl.pallas_call(
        body, name=name, grid=(bsz, s // ts),
        in_specs=[_row_spec(ts, d), _row_spec(ts, d), _vec_spec(d), _row_spec(ts, d)],
        out_specs=[_row_spec(ts, d), _row_spec(ts, d), _vec_spec(d), pl.BlockSpec((8, LANE), lambda b, i: (0, 0))],
        out_shape=[jax.ShapeDtypeStruct((bsz, s, d), F32), jax.ShapeDtypeStruct((bsz, s, d), BF16),
                   jax.ShapeDtypeStruct((bsz, 1, d), F32), jax.ShapeDtypeStruct((8, LANE), F32)],
        scratch_shapes=[pltpu.VMEM((8, d), F32)], compiler_params=_params(2),
    )(x1, ff, gate2, target)


def _merge_fwd(proj, b_merge, y_a, y_b, *, name, ts=256):
    bsz, s, _ = proj.shape
    ts = min(ts, s)

    def body(la_ref, lb_ref, ba_ref, bb_ref, ya_ref, yb_ref, out_ref):
        ga = _sigmoid(la_ref[...] + ba_ref[...])
        gb = _sigmoid(lb_ref[...] + bb_ref[...])
        out_ref[...] = (ga * ya_ref[...] + gb * yb_ref[...]).astype(BF16)

    return pl.pallas_call(
        body, name=name, grid=(bsz, s // ts),
        in_specs=[_row_spec(ts, D, OFF_MA // D), _row_spec(ts, D, OFF_MB // D),
                  pl.BlockSpec((1, D), lambda b, i: (0, 0)), pl.BlockSpec((1, D), lambda b, i: (0, 1)),
                  _row_spec(ts, D), _row_spec(ts, D)],
        out_specs=_row_spec(ts, D), out_shape=jax.ShapeDtypeStruct((bsz, s, D), BF16),
        compiler_params=_params(2),
    )(proj, proj, b_merge, b_merge, y_a, y_b)


def _merge_bwd(dmi, proj, b_merge, y_a, y_b, *, name, ts=256):
    bsz, s, _ = proj.shape
    ts = min(ts, s)

    def body(d_ref, la_ref, lb_ref, ba_ref, bb_ref, ya_ref, yb_ref, dya_ref, dyb_ref, dla_ref, dlb_ref, dba_ref, dbb_ref):
        @pl.when((pl.program_id(0) == 0) & (pl.program_id(1) == 0))
        def _():
            dba_ref[...] = jnp.zeros_like(dba_ref)
            dbb_ref[...] = jnp.zeros_like(dbb_ref)

        dv = d_ref[...]
        ga = _sigmoid(la_ref[...] + ba_ref[...])
        gb = _sigmoid(lb_ref[...] + bb_ref[...])
        dya_ref[...] = (dv * ga).astype(BF16)
        dyb_ref[...] = (dv * gb).astype(BF16)
        dla = (dv * ya_ref[...]) * (ga * (1.0 - ga))
        dlb = (dv * yb_ref[...]) * (gb * (1.0 - gb))
        dla_ref[...] = dla.astype(BF16)
        dlb_ref[...] = dlb.astype(BF16)
        dba_ref[...] += jnp.sum(dla, axis=0, keepdims=True)
        dbb_ref[...] += jnp.sum(dlb, axis=0, keepdims=True)

    act = jax.ShapeDtypeStruct((bsz, s, D), BF16)
    return pl.pallas_call(
        body, name=name, grid=(bsz, s // ts),
        in_specs=[_row_spec(ts, D), _row_spec(ts, D, OFF_MA // D), _row_spec(ts, D, OFF_MB // D),
                  pl.BlockSpec((1, D), lambda b, i: (0, 0)), pl.BlockSpec((1, D), lambda b, i: (0, 1)),
                  _row_spec(ts, D), _row_spec(ts, D)],
        out_specs=[_row_spec(ts, D)] * 4 + [_gain_spec(D)] * 2,
        out_shape=[act, act, act, act, jax.ShapeDtypeStruct((1, D), F32), jax.ShapeDtypeStruct((1, D), F32)],
        compiler_params=_params(2),
    )(dmi, proj, proj, b_merge, b_merge, y_a, y_b)


def _tri(lower):
    r = lax.broadcasted_iota(jnp.int32, (CHUNK, CHUNK), 0)
    c = lax.broadcasted_iota(jnp.int32, (CHUNK, CHUNK), 1)
    return jnp.where((c <= r) if lower else (c >= r), 1.0, 0.0).astype(F32)


def _gla_logits(a_ref, wal_ref, bal_ref):
    logits = jnp.dot(a_ref[...].astype(BF16), wal_ref[...].astype(BF16), preferred_element_type=F32) + bal_ref[...]
    la = (jnp.minimum(logits, 0.0) - jnp.log(1.0 + jnp.exp(-jnp.abs(logits)))) * (1.0 / GTAU)
    return logits, la


def _chunk_cumsum(la_n, tri):
    cum = jnp.dot(tri, la_n, preferred_element_type=F32, precision=lax.Precision.HIGHEST)
    return cum, jnp.sum(la_n, axis=0, keepdims=True)


def _gla_specs(s, nc):
    def blk(width, off):
        return pl.BlockSpec((None, s, width), lambda h, b: (b, 0, off // width + h))

    proj_specs = [blk(GDK, OFF_Q), blk(GDK, OFF_K), blk(GDV, OFF_V), blk(GDV, OFF_G),
                  pl.BlockSpec((None, s, LANE), lambda h, b: (b, 0, OFF_A // LANE)),
                  pl.BlockSpec((LANE, GDK), lambda h, b: (0, h)), pl.BlockSpec((1, GDK), lambda h, b: (0, h)),
                  pl.BlockSpec((1, GDV), lambda h, b: (0, 0))]
    st_spec = pl.BlockSpec((None, None, nc, GDV, GDK), lambda h, b: (b, h, 0, 0, 0))
    return blk, proj_specs, st_spec


def _gla_fwd(proj, w_alpha_p, b_alpha, out_norm_g, *, name):
    bsz, s, _ = proj.shape
    nc = s // CHUNK
    scale = GDK ** -0.5

    rb = min(512, s)

    def body(q_ref, k_ref, v_ref, g_ref, a_ref, wal_ref, bal_ref, ong_ref, o_ref, og_ref, st_ref):
        _, la = _gla_logits(a_ref, wal_ref, bal_ref)
        tri = _tri(True)
        st = jnp.zeros((GDV, GDK), F32)
        for n in range(nc):
            rows = pl.ds(n * CHUNK, CHUNK)
            cum, cum_end = _chunk_cumsum(la[n * CHUNK:(n + 1) * CHUNK], tri)
            kd = k_ref[rows, :] * jnp.exp(cum_end - cum)
            ut = lax.dot_general(v_ref[rows, :].astype(BF16), kd.astype(BF16), _TN, preferred_element_type=F32)
            st = st * jnp.exp(cum_end) + ut
            st_ref[n] = st
            o_ref[rows, :] = lax.dot_general((q_ref[rows, :] * scale).astype(BF16), st.astype(BF16), _NT,
                                             preferred_element_type=F32)
        for j in range(0, s, rb):
            blk_rows = pl.ds(j, rb)
            oh, _ = _rms(o_ref[blk_rows, :], None)
            gv = g_ref[blk_rows, :]
            og_ref[blk_rows, :] = ((oh * ong_ref[...]) * (gv * _sigmoid(gv))).astype(BF16)

    blk, proj_specs, st_spec = _gla_specs(s, nc)
    return pl.pallas_call(
        body, name=name, grid=(GH, bsz), in_specs=proj_specs, out_specs=[blk(GDV, 0), blk(GDV, 0), st_spec],
        out_shape=[jax.ShapeDtypeStruct((bsz, s, GH * GDV), F32), jax.ShapeDtypeStruct((bsz, s, GH * GDV), BF16),
                   jax.ShapeDtypeStruct((bsz, GH, nc, GDV, GDK), F32)],
        compiler_params=_params(2),
    )(proj, proj, proj, proj, proj, w_alpha_p, b_alpha, out_norm_g)


def _gla_bwd(dog, o, states, proj, w_alpha_p, b_alpha, out_norm_g, *, name):
    bsz, s, _ = proj.shape
    nc = s // CHUNK
    scale = GDK ** -0.5

    def body(dog_ref, o_ref, st_ref, q_ref, k_ref, v_ref, g_ref, a_ref, wal_ref, bal_ref, ong_ref,
             dq_ref, dk_ref, dv_ref, dg_ref, dl_ref, dbal_ref, dong_ref, do_scr, dlog_scr):
        h, b = pl.program_id(0), pl.program_id(1)

        @pl.when(b == 0)
        def _():
            dbal_ref[...] = jnp.zeros_like(dbal_ref)

        @pl.when((b == 0) & (h == 0))
        def _():
            dong_ref[...] = jnp.zeros_like(dong_ref)

        ong = ong_ref[...]
        for j in range(0, s, rb):
            blk_rows = pl.ds(j, rb)
            gv, dogv = g_ref[blk_rows, :], dog_ref[blk_rows, :]
            sg = _sigmoid(gv)
            oh, r = _rms(o_ref[blk_rows, :], None)
            don = dogv * (gv * sg)
            dg_ref[blk_rows, :] = (dogv * (oh * ong) * (sg * (1.0 + gv * (1.0 - sg)))).astype(BF16)
            dong_ref[...] += jnp.sum(don * oh, axis=0, keepdims=True)
            doh = don * ong
            do_scr[blk_rows, :] = (r * (doh - oh * jnp.mean(doh * oh, axis=-1, keepdims=True))).astype(BF16)

        logits, la = _gla_logits(a_ref, wal_ref, bal_ref)
        tri_lo, tri_up = _tri(True), _tri(False)
        carry = jnp.zeros((GDV, GDK), F32)
        for n in range(nc - 1, -1, -1):
            rows = pl.ds(n * CHUNK, CHUNK)
            cum, cum_end = _chunk_cumsum(la[n * CHUNK:(n + 1) * CHUNK], tri_lo)
            decay = jnp.exp(cum_end)
            w = jnp.exp(cum_end - cum)
            kd = k_ref[rows, :] * w
            do_b = do_scr[rows, :]
            qs_b = (q_ref[rows, :] * scale).astype(BF16)
            dq_ref[rows, :] = (jnp.dot(do_b, st_ref[n].astype(BF16), preferred_element_type=F32) * scale).astype(BF16)
            dsn = lax.dot_general(do_b, qs_b, _TN, preferred_element_type=F32) + carry
            carry = dsn * decay
            dsn_b = dsn.astype(BF16)
            dv_ref[rows, :] = lax.dot_general(kd.astype(BF16), dsn_b, _NT, preferred_element_type=F32).astype(BF16)
            dkd = jnp.dot(v_ref[rows, :].astype(BF16), dsn_b, preferred_element_type=F32)
            dk_ref[rows, :] = (dkd * w).astype(BF16)
            e = dkd * kd
            dcum_end = jnp.sum(e, axis=0, keepdims=True)
            if n > 0:
                dcum_end += jnp.sum(dsn * st_ref[n - 1], axis=0, keepdims=True) * decay
            dlog_scr[rows, :] = dcum_end - jnp.dot(tri_up, e, preferred_element_type=F32,
                                                  precision=lax.Precision.HIGHEST)
        dlog = dlog_scr[...] * (1.0 / GTAU) * (1.0 - _sigmoid(logits))
        dl_ref[...] = dlog.astype(BF16)
        dbal_ref[...] += jnp.sum(dlog, axis=0, keepdims=True)

    rb = min(512, s)

    blk, proj_specs, st_spec = _gla_specs(s, nc)
    act = lambda wd: jax.ShapeDtypeStruct((bsz, s, wd), BF16)
    return pl.pallas_call(
        body, name=name, grid=(GH, bsz), in_specs=[blk(GDV, 0), blk(GDV, 0), st_spec, *proj_specs],
        out_specs=[blk(GDK, 0), blk(GDK, 0), blk(GDV, 0), blk(GDV, 0), blk(GDK, 0),
                   pl.BlockSpec((1, GDK), lambda h, b: (0, h)), pl.BlockSpec((1, GDV), lambda h, b: (0, 0))],
        out_shape=[act(GH * GDK), act(GH * GDK), act(GH * GDV), act(GH * GDV), act(GH * GDK),
                   jax.ShapeDtypeStruct((1, GH * GDK), F32), jax.ShapeDtypeStruct((1, GDV), F32)],
        scratch_shapes=[pltpu.VMEM((s, GDV), BF16), pltpu.VMEM((s, GDK), F32)], compiler_params=_params(2),
    )(dog, o, states, proj, proj, proj, proj, proj, w_alpha_p, b_alpha, out_norm_g)


def _lane():
    return lax.broadcasted_iota(jnp.int32, (1, LANE), 1)


def _swap_halves(x):
    lane = _lane()
    half = MROPE // 2
    lo = (lane >= MNOPE) & (lane < MNOPE + half)
    hi = (lane >= MNOPE + half) & (lane < MQK)
    return jnp.where(lo, pltpu.roll(x, LANE - half, 1), jnp.where(hi, pltpu.roll(x, half, 1), 0.0))


def _norm96(x, g):
    r = lax.rsqrt(jnp.sum(x * x, axis=-1, keepdims=True) * (1.0 / MQK) + EPS)
    return x * r, r


def _lat_norm(proj, q_lat_g, kv_lat_g, *, name, ts=512):
    t = proj.shape[0]
    ts = min(ts, t)

    def body(cq_ref, ckv_ref, gq_ref, gk_ref, oq_ref, ok_ref):
        xq, _ = _rms(cq_ref[...], None)
        oq_ref[...] = (xq * gq_ref[...]).astype(BF16)
        xk, _ = _rms(ckv_ref[...], None)
        ok_ref[...] = (xk * gk_ref[...]).astype(BF16)

    return pl.pallas_call(
        body, name=name, grid=(t // ts,),
        in_specs=[pl.BlockSpec((ts, MQR), lambda i: (i, OFF_CQ // MQR)), pl.BlockSpec((ts, MKVR), lambda i: (i, OFF_CKV // MKVR)),
                  pl.BlockSpec((1, MQR), lambda i: (0, 0)), pl.BlockSpec((1, MKVR), lambda i: (0, 0))],
        out_specs=[pl.BlockSpec((ts, MQR), lambda i: (i, 0)), pl.BlockSpec((ts, MKVR), lambda i: (i, 0))],
        out_shape=[jax.ShapeDtypeStruct((t, MQR), BF16), jax.ShapeDtypeStruct((t, MKVR), BF16)],
        compiler_params=_params(1),
    )(proj, proj, q_lat_g, kv_lat_g)


def _lat_norm_bwd(dcqn, dckvn, proj, q_lat_g, kv_lat_g, *, name, ts=512):
    t = proj.shape[0]
    ts = min(ts, t)

    def one(d_ref, x_ref, g_ref, dx_ref, dg_ref):
        xh, r = _rms(x_ref[...], None)
        dn = d_ref[...]
        dg_ref[...] += jnp.sum(dn * xh, axis=0, keepdims=True)
        dxh = dn * g_ref[...]
        dx_ref[...] = (r * (dxh - xh * jnp.mean(dxh * xh, axis=-1, keepdims=True))).astype(BF16)

    def body(dq_ref, dk_ref, cq_ref, ckv_ref, gq_ref, gk_ref, dxq_ref, dxk_ref, dgq_ref, dgk_ref):
        @pl.when(pl.program_id(0) == 0)
        def _():
            dgq_ref[...] = jnp.zeros_like(dgq_ref)
            dgk_ref[...] = jnp.zeros_like(dgk_ref)

        one(dq_ref, cq_ref, gq_ref, dxq_ref, dgq_ref)
        one(dk_ref, ckv_ref, gk_ref, dxk_ref, dgk_ref)

    return pl.pallas_call(
        body, name=name, grid=(t // ts,),
        in_specs=[pl.BlockSpec((ts, MQR), lambda i: (i, 0)), pl.BlockSpec((ts, MKVR), lambda i: (i, 0)),
                  pl.BlockSpec((ts, MQR), lambda i: (i, OFF_CQ // MQR)), pl.BlockSpec((ts, MKVR), lambda i: (i, OFF_CKV // MKVR)),
                  pl.BlockSpec((1, MQR), lambda i: (0, 0)), pl.BlockSpec((1, MKVR), lambda i: (0, 0))],
        out_specs=[pl.BlockSpec((ts, MQR), lambda i: (i, 0)), pl.BlockSpec((ts, MKVR), lambda i: (i, 0)),
                   pl.BlockSpec((1, MQR), lambda i: (0, 0)), pl.BlockSpec((1, MKVR), lambda i: (0, 0))],
        out_shape=[jax.ShapeDtypeStruct((t, MQR), BF16), jax.ShapeDtypeStruct((t, MKVR), BF16),
                   jax.ShapeDtypeStruct((1, MQR), F32), jax.ShapeDtypeStruct((1, MKVR), F32)],
        compiler_params=_params(1),
    )(dcqn, dckvn, proj, proj, q_lat_g, kv_lat_g)


def _qk_prep(q_raw, kv, proj, cos_t, sin_t, gq, gk, *, name, ts=512):
    t = q_raw.shape[0]
    ts = min(ts, t)

    def body(q_ref, kv_ref, kpe_ref, c_ref, s_ref, gq_ref, gk_ref, qo_ref, ko_ref, vo_ref):
        cs, sn = c_ref[...], s_ref[...]
        nope = _lane() < MNOPE
        qn, _ = _norm96(q_ref[...], None)
        qn = qn * gq_ref[...]
        qo_ref[...] = (qn * cs + _swap_halves(qn) * sn).astype(BF16)
        kvv = kv_ref[...]
        kn, _ = _norm96(jnp.where(nope, kvv, kpe_ref[...]), None)
        kn = kn * gk_ref[...]
        ko_ref[...] = (kn * cs + _swap_halves(kn) * sn).astype(BF16)
        vo_ref[...] = jnp.where(nope, pltpu.roll(kvv, MNOPE, 1), 0.0).astype(BF16)

    hd = pl.BlockSpec((ts, LANE), lambda i, h: (i, h))
    shared = lambda col: pl.BlockSpec((ts, LANE), lambda i, h: (i, col))
    gain = pl.BlockSpec((1, LANE), lambda i, h: (0, 0))
    out = jax.ShapeDtypeStruct((t, MH * LANE), BF16)
    return pl.pallas_call(
        body, name=name, grid=(t // ts, MH),
        in_specs=[hd, hd, shared(OFF_KPE // LANE), shared(0), shared(0), gain, gain],
        out_specs=[hd, hd, hd], out_shape=[out, out, out], compiler_params=_params(2),
    )(q_raw, kv, proj, cos_t, sin_t, gq, gk)


def _qk_prep_bwd(dq, dk, dv, q_raw, kv, proj, cos_t, sin_t, gq, gk, *, name, ts=512):
    t = q_raw.shape[0]
    ts = min(ts, t)

    def norm_bwd(dy, x, g, dg_ref):
        xh, r = _norm96(x, None)
        dg_ref[...] += jnp.sum(dy * xh, axis=0, keepdims=True)
        dxh = dy * g
        return r * (dxh - xh * (jnp.sum(dxh * xh, axis=-1, keepdims=True) * (1.0 / MQK)))

    def body(dq_ref, dk_ref, dv_ref, q_ref, kv_ref, kpe_ref, c_ref, s_ref, gq_ref, gk_ref,
             dqr_ref, dkv_ref, dkpe_ref, dgq_ref, dgk_ref):
        i, h = pl.program_id(0), pl.program_id(1)

        @pl.when(h == 0)
        def _():
            dkpe_ref[...] = jnp.zeros_like(dkpe_ref)

        @pl.when((h == 0) & (i == 0))
        def _():
            dgq_ref[...] = jnp.zeros_like(dgq_ref)
            dgk_ref[...] = jnp.zeros_like(dgk_ref)

        cs, sn = c_ref[...], s_ref[...]
        lane = _lane()
        nope = lane < MNOPE
        dqv = dq_ref[...]
        dqn = dqv * cs + _swap_halves(dqv * sn)
        dqr_ref[...] = norm_bwd(dqn, q_ref[...], gq_ref[...], dgq_ref).astype(BF16)
        dkv_ = dk_ref[...]
        dkn = dkv_ * cs + _swap_halves(dkv_ * sn)
        kvv = kv_ref[...]
        dkr = norm_bwd(dkn, jnp.where(nope, kvv, kpe_ref[...]), gk_ref[...], dgk_ref)
        dkv_ref[...] = jnp.where(nope, dkr, pltpu.roll(dv_ref[...], MNOPE, 1)).astype(BF16)
        dkpe_ref[...] += jnp.where((lane >= MNOPE) & (lane < MQK), dkr, 0.0)

    hd = pl.BlockSpec((ts, LANE), lambda i, h: (i, h))
    shared = lambda col: pl.BlockSpec((ts, LANE), lambda i, h: (i, col))
    gain = pl.BlockSpec((1, LANE), lambda i, h: (0, 0))
    out = jax.ShapeDtypeStruct((t, MH * LANE), BF16)
    return pl.pallas_call(
        body, name=name, grid=(t // ts, MH),
        in_specs=[hd, hd, hd, hd, hd, shared(OFF_KPE // LANE), shared(0), shared(0), gain, gain],
        out_specs=[hd, hd, shared(0), gain, gain],
        out_shape=[out, out, jax.ShapeDtypeStruct((t, LANE), F32), jax.ShapeDtypeStruct((1, LANE), F32),
                   jax.ShapeDtypeStruct((1, LANE), F32)],
        compiler_params=_params(2),
    )(dq, dk, dv, q_raw, kv, proj, cos_t, sin_t, gq, gk)


_NT = (((1,), (1,)), ((), ()))
_TN = (((0,), (0,)), ((), ()))


SOFTMAX_SCALE = MQK ** -0.5
Q_PRESCALE = SOFTMAX_SCALE * float(np.log2(np.e))


def _attn_weights(q, k_ref, lo, tq):
    row = lax.broadcasted_iota(jnp.int32, (tq, tq), 0) // CHUNK
    col = lax.broadcasted_iota(jnp.int32, (tq, tq), 1) // CHUNK
    sd = lax.dot_general(q, k_ref[pl.ds(lo, tq), :], _NT, preferred_element_type=F32)
    sd = jnp.where(col <= row, sd, -1e30)
    m = jnp.max(sd, axis=-1, keepdims=True)
    if lo:
        so = lax.dot_general(q, k_ref[pl.ds(0, lo), :], _NT, preferred_element_type=F32)
        m = jnp.maximum(m, jnp.max(so, axis=-1, keepdims=True))
        eo = jnp.exp2(so - m)
        ed = jnp.exp2(sd - m)
        return eo, ed, 1.0 / (jnp.sum(eo, axis=-1, keepdims=True) + jnp.sum(ed, axis=-1, keepdims=True))
    ed = jnp.exp2(sd - m)
    return None, ed, 1.0 / jnp.sum(ed, axis=-1, keepdims=True)


def _attn_fwd(q, k, v, *, name, tq=256):
    bsz, s, _ = q.shape
    tq = min(tq, s)

    def body(q_ref, k_ref, v_ref, o_ref):
        for i in range(s // tq):
            lo = i * tq
            eo, ed, inv = _attn_weights(q_ref[pl.ds(lo, tq), :], k_ref, lo, tq)
            o = jnp.dot(ed.astype(BF16), v_ref[pl.ds(lo, tq), :], preferred_element_type=F32)
            if lo:
                o += jnp.dot(eo.astype(BF16), v_ref[pl.ds(0, lo), :], preferred_element_type=F32)
            o_ref[pl.ds(lo, tq), :] = (o * inv).astype(BF16)

    spec = pl.BlockSpec((None, s, LANE), lambda b, h: (b, 0, h))
    return pl.pallas_call(
        body, name=name, grid=(bsz, MH), in_specs=[spec, spec, spec], out_specs=spec,
        out_shape=jax.ShapeDtypeStruct((bsz, s, MH * LANE), BF16), compiler_params=_params(2),
    )(q, k, v)


def _attn_bwd(q, k, v, do, *, name, tq=256):
    bsz, s, _ = q.shape
    tq = min(tq, s)

    def body(q_ref, k_ref, v_ref, do_ref, dq_ref, dk_ref, dv_ref):
        dk_ref[...] = jnp.zeros_like(dk_ref)
        dv_ref[...] = jnp.zeros_like(dv_ref)
        for i in range(s // tq):
            lo = i * tq
            here, before = pl.ds(lo, tq), pl.ds(0, lo)
            qv, dov = q_ref[here, :], do_ref[here, :]
            eo, ed, inv = _attn_weights(qv, k_ref, lo, tq)
            do_n = (dov.astype(F32) * inv).astype(BF16)
            dv_ref[here, :] += lax.dot_general(ed.astype(BF16), do_n, _TN, preferred_element_type=F32)
            dpd = lax.dot_general(dov, v_ref[here, :], _NT, preferred_element_type=F32)
            delta = jnp.sum(dpd * ed, axis=-1, keepdims=True)
            if lo:
                dv_ref[before, :] += lax.dot_general(eo.astype(BF16), do_n, _TN, preferred_element_type=F32)
                dpo = lax.dot_general(dov, v_ref[before, :], _NT, preferred_element_type=F32)
                delta += jnp.sum(dpo * eo, axis=-1, keepdims=True)
            delta = delta * inv
            r = inv * SOFTMAX_SCALE
            dsd = (ed * (dpd - delta) * r).astype(BF16)
            dq = jnp.dot(dsd, k_ref[here, :], preferred_element_type=F32)
            dk_ref[here, :] += lax.dot_general(dsd, qv, _TN, preferred_element_type=F32)
            if lo:
                dso = (eo * (dpo - delta) * r).astype(BF16)
                dq += jnp.dot(dso, k_ref[before, :], preferred_element_type=F32)
                dk_ref[before, :] += lax.dot_general(dso, qv, _TN, preferred_element_type=F32)
            dq_ref[here, :] = dq
        dk_ref[...] = dk_ref[...] * (1.0 / Q_PRESCALE)

    spec = pl.BlockSpec((None, s, LANE), lambda b, h: (b, 0, h))
    out = jax.ShapeDtypeStruct((bsz, s, MH * LANE), F32)
    return pl.pallas_call(
        body, name=name, grid=(bsz, MH), in_specs=[spec] * 4, out_specs=[spec] * 3, out_shape=[out, out, out],
        compiler_params=_params(2),
    )(q, k, v, do)


def _adamw(w, g, m, v, *, name, tr=256):
    rows, cols = w.shape
    tr = _tile_rows(rows, tr)

    def body(w_ref, g_ref, m_ref, v_ref, d_ref, nm_ref, nv_ref):
        d_ref[...], nm_ref[...], nv_ref[...] = _adamw_update(w_ref[...], g_ref[...], m_ref[...], v_ref[...])

    spec = pl.BlockSpec((tr, cols), lambda i: (i, 0))
    out = jax.ShapeDtypeStruct((rows, cols), F32)
    return pl.pallas_call(body, name=name, grid=(rows // tr,), in_specs=[spec] * 4, out_specs=[spec] * 3,
                          out_shape=[out, out, out], compiler_params=_params(1))(w, g, m, v)


def _tile_rows(rows, target):
    if rows <= target:
        return rows
    best = 8
    for t in range(8, target + 1, 8):
        if rows % t == 0:
            best = t
    return best


def _adamw_update(w, g, m, v):
    nm = ADAM_B1 * m + (1.0 - ADAM_B1) * g
    nv = ADAM_B2 * v + (1.0 - ADAM_B2) * (g * g)
    m_hat = nm / (1.0 - ADAM_B1 ** ADAM_STEP)
    v_hat = nv / (1.0 - ADAM_B2 ** ADAM_STEP)
    return -ADAM_LR * (m_hat / (jnp.sqrt(v_hat) + ADAM_EPS) + ADAM_WD * w), nm, nv


def _adamw_halves(w, m, v, mine, theirs, sel, *, name, tr=256):
    rows, cols = w.shape
    tr = _tile_rows(rows // 2, tr)
    nh = rows // 2 // tr

    def body(sel_ref, w_ref, m_ref, v_ref, mine_ref, theirs_ref, g_ref, d_ref, nm_ref, nv_ref):
        lower = pl.program_id(0) < nh
        south = sel_ref[0] == 0
        gv = jnp.where(lower == south, mine_ref[...], theirs_ref[...])
        g_ref[...] = gv
        d_ref[...], nm_ref[...], nv_ref[...] = _adamw_update(w_ref[...], gv, m_ref[...], v_ref[...])

    full = pl.BlockSpec((tr, cols), lambda i, sel_ref: (i, 0))
    half = pl.BlockSpec((tr, cols), lambda i, sel_ref: (i % nh, 0))
    out = jax.ShapeDtypeStruct((rows, cols), F32)
    return pl.pallas_call(
        body, name=name, out_shape=[out] * 4, compiler_params=_params(1),
        grid_spec=pltpu.PrefetchScalarGridSpec(num_scalar_prefetch=1, grid=(rows // tr,),
                                               in_specs=[full, full, full, half, half], out_specs=[full] * 4),
    )(sel, w, m, v, mine, theirs)


def _pair_add(x, sib, sel, *, name, tr=256):
    n, _, rows, cols = x.shape
    tr = _tile_rows(rows, tr)

    def body(sel_ref, x_ref, s_ref, o_ref):
        o_ref[...] = (x_ref[...] + s_ref[...]).astype(BF16)

    spec = pl.BlockSpec((None, tr, cols), lambda j, i, sel_ref: (j, i, 0))
    return pl.pallas_call(
        body, name=name, out_shape=jax.ShapeDtypeStruct((n, rows, cols), BF16), compiler_params=_params(2),
        grid_spec=pltpu.PrefetchScalarGridSpec(
            num_scalar_prefetch=1, grid=(n, rows // tr),
            in_specs=[pl.BlockSpec((None, None, tr, cols), lambda j, i, sel_ref: (j, sel_ref[0], i, 0)), spec],
            out_specs=spec),
    )(sel, x, sib)


def _chip_sum(pair, recv, sel, *, name, tr=256):
    _, rows, cols = pair.shape
    tr = _tile_rows(rows, tr)

    def body(sel_ref, p_ref, r_ref, o_ref):
        acc = p_ref[...].astype(F32)
        for k in range(3):
            acc = acc + r_ref[k].astype(F32)
        o_ref[...] = acc

    return pl.pallas_call(
        body, name=name, out_shape=jax.ShapeDtypeStruct((rows, cols), F32), compiler_params=_params(1),
        grid_spec=pltpu.PrefetchScalarGridSpec(
            num_scalar_prefetch=1, grid=(rows // tr,),
            in_specs=[pl.BlockSpec((None, tr, cols), lambda i, sel_ref: (sel_ref[0], i, 0)),
                      pl.BlockSpec((3, tr, cols), lambda i, sel_ref: (0, i, 0))],
            out_specs=pl.BlockSpec((tr, cols), lambda i, sel_ref: (i, 0))),
    )(sel, pair, recv)


def _me():
    return lax.axis_index("x"), lax.axis_index("y"), lax.axis_index("c")


def _flip(pos, bits):
    x, y, c = pos
    return (x ^ bits[0] if bits[0] else x, y ^ bits[1] if bits[1] else y, c ^ bits[2] if bits[2] else c)


ANY = pl.BlockSpec(memory_space=pl.ANY)


def _all_gather8(xs, *, name):
    n = len(xs)
    flips = [((k >> 2) & 1, (k >> 1) & 1, k & 1) for k in range(1, 8)]

    def body(*refs):
        x_refs, out_refs, (send_sems, recv_sems, local_sems) = refs[:n], refs[n:2 * n], refs[2 * n:]
        me = _me()
        slot = lambda p: 4 * p[0] + 2 * p[1] + p[2]
        copies = []
        for i in range(n):
            mine = pltpu.make_async_copy(x_refs[i], out_refs[i].at[slot(me)], local_sems.at[i])
            mine.start()
            copies.append(mine)
            for k, f in enumerate(flips):
                peer = _flip(me, f)
                sems = dict(send_sem=send_sems.at[7 * i + k], recv_sem=recv_sems.at[7 * i + k], device_id=peer,
                            device_id_type=MESH)
                cp = pltpu.make_async_remote_copy(src_ref=x_refs[i], dst_ref=out_refs[i].at[slot(me)], **sems)
                cp.start()
                copies.append(cp)
                copies.append(pltpu.make_async_remote_copy(src_ref=x_refs[i], dst_ref=out_refs[i].at[slot(peer)], **sems))
        for i in range(n):
            base = i * 15
            copies[base].wait()
            for k in range(7):
                copies[base + 1 + 2 * k].wait_send()
                copies[base + 2 + 2 * k].wait_recv()

    outs = pl.pallas_call(
        body, name=name, in_specs=[ANY] * n, out_specs=[ANY] * n,
        out_shape=[jax.ShapeDtypeStruct((8, *x.shape), x.dtype) for x in xs],
        scratch_shapes=[pltpu.SemaphoreType.DMA((7 * n,)), pltpu.SemaphoreType.DMA((7 * n,)),
                        pltpu.SemaphoreType.DMA((n,))])(*xs)
    return list(outs)


CHIP_FLIPS = [(1, 0, 0), (0, 1, 0), (1, 1, 0)]


def _chip():
    return 2 * lax.axis_index("x") + lax.axis_index("y")


def _pair_swap_halves(xs, *, name):
    n = len(xs)

    def body(*refs):
        x_refs, out_refs, (send_sems, recv_sems) = refs[:n], refs[n:2 * n], refs[2 * n:]
        me = _me()
        sib = _flip(me, (0, 0, 1))
        copies = [pltpu.make_async_remote_copy(src_ref=x_refs[i].at[:, 1 - me[2]], dst_ref=out_refs[i],
                                               send_sem=send_sems.at[i], recv_sem=recv_sems.at[i], device_id=sib,
                                               device_id_type=MESH) for i in range(n)]
        for cp in copies:
            cp.start()
        for cp in copies:
            cp.wait()

    return pl.pallas_call(
        body, name=name, in_specs=[ANY] * n, out_specs=[ANY] * n,
        out_shape=[jax.ShapeDtypeStruct((x.shape[0], *x.shape[2:]), x.dtype) for x in xs],
        scratch_shapes=[pltpu.SemaphoreType.DMA((n,)), pltpu.SemaphoreType.DMA((n,))])(*xs)


def _pair_swap(hs, *, name):
    n = len(hs)

    def body(*refs):
        h_refs, out_refs, (send_sems, recv_sems) = refs[:n], refs[n:2 * n], refs[2 * n:]
        sib = _flip(_me(), (0, 0, 1))
        copies = [pltpu.make_async_remote_copy(src_ref=h_refs[i], dst_ref=out_refs[i], send_sem=send_sems.at[i],
                                               recv_sem=recv_sems.at[i], device_id=sib, device_id_type=MESH)
                  for i in range(n)]
        for cp in copies:
            cp.start()
        for cp in copies:
            cp.wait()

    return pl.pallas_call(
        body, name=name, in_specs=[ANY] * n, out_specs=[ANY] * n,
        out_shape=[jax.ShapeDtypeStruct(h.shape, h.dtype) for h in hs],
        scratch_shapes=[pltpu.SemaphoreType.DMA((n,)), pltpu.SemaphoreType.DMA((n,))])(*hs)


HBM = pl.BlockSpec(memory_space=pltpu.HBM)
SEM = pl.BlockSpec(memory_space=pltpu.SEMAPHORE)
EFFECT = pltpu.SideEffectType.DATAFLOW_SIDE_EFFECTING


def _plan_copies(plan, refs, send_sems, recv_sems):
    return [pltpu.make_async_remote_copy(src_ref=src, dst_ref=dst, send_sem=send_sems.at[k], recv_sem=recv_sems.at[k],
                                         device_id=to, device_id_type=MESH) for k, (src, dst, to) in enumerate(plan(refs))]


def _rdma_start(arrays, n_copies, plan, deps, *, name):
    n, nd = len(arrays), len(deps)

    def body(*refs):
        for cp in _plan_copies(plan, refs[:n], refs[n + nd], refs[n + nd + 1]):
            cp.start()
        refs[-1][...] = jnp.zeros_like(refs[-1])

    outs = pl.pallas_call(
        body, name=name,
        out_shape=(pltpu.SemaphoreType.DMA((n_copies,)), pltpu.SemaphoreType.DMA((n_copies,)),
                   *[pltpu.HBM(a.shape, a.dtype) for a in arrays], jax.ShapeDtypeStruct((8, LANE), F32)),
        in_specs=[HBM] * n + [ANY] * nd, out_specs=(SEM, SEM, *[HBM] * n, pl.BlockSpec(memory_space=pltpu.VMEM)),
        input_output_aliases={i: i + 2 for i in range(n)}, compiler_params=pltpu.CompilerParams(has_side_effects=EFFECT),
    )(*[pltpu.with_memory_space_constraint(a, pltpu.HBM) for a in arrays], *deps)
    return outs[0], outs[1], list(outs[2:2 + n]), outs[-1]


def _rdma_wait(send_sems, recv_sems, arrays, plan, after, *, name):
    n = len(arrays)

    def body(*refs):
        for cp in _plan_copies(plan, refs[:n], refs[n], refs[n + 1]):
            cp.wait_send()
            cp.wait_recv()

    return list(pl.pallas_call(
        body, name=name, out_shape=tuple(pltpu.HBM(a.shape, a.dtype) for a in arrays),
        in_specs=[HBM] * n + [SEM, SEM, ANY], out_specs=tuple([HBM] * n), input_output_aliases={i: i for i in range(n)},
        compiler_params=pltpu.CompilerParams(has_side_effects=EFFECT),
    )(*arrays, send_sems, recv_sems, after))


def _gather_plan(n):
    def plan(refs):
        me = _me()
        slot = 2 * me[0] + me[1]
        return [(refs[i].at[me[2]], refs[n + i].at[slot, me[2]], _flip(me, f)) for i in range(n) for f in CHIP_FLIPS]
    return plan


def _scatter_plan(n):
    def plan(refs):
        me = _me()
        out = []
        for i in range(n):
            for k, f in enumerate(CHIP_FLIPS):
                peer = _flip(me, f)
                out.append((refs[i].at[2 * peer[0] + peer[1]], refs[n + i].at[k], peer))
        return out
    return plan


def _pair_fill(lands, *, name):
    n = len(lands)

    def body(*refs):
        in_refs, (send_sems, recv_sems) = refs[:n], refs[2 * n:]
        me = _me()
        sib = _flip(me, (0, 0, 1))
        copies = []
        for i in range(n):
            for k, f in enumerate(CHIP_FLIPS):
                peer = _flip(me, f)
                slot = 2 * peer[0] + peer[1]
                mine, theirs = in_refs[i].at[slot, me[2]], in_refs[i].at[slot, 1 - me[2]]
                cp = pltpu.make_async_remote_copy(src_ref=mine, dst_ref=mine, send_sem=send_sems.at[3 * i + k],
                                                  recv_sem=recv_sems.at[3 * i + k], device_id=sib, device_id_type=MESH)
                cp.start()
                copies.append((cp, pltpu.make_async_remote_copy(
                    src_ref=mine, dst_ref=theirs, send_sem=send_sems.at[3 * i + k], recv_sem=recv_sems.at[3 * i + k],
                    device_id=sib, device_id_type=MESH)))
        for cp, arrival in copies:
            arrival.wait_recv()
            cp.wait_send()

    return list(pl.pallas_call(
        body, name=name, in_specs=[ANY] * n, out_specs=[ANY] * n,
        out_shape=[jax.ShapeDtypeStruct(a.shape, a.dtype) for a in lands], input_output_aliases={i: i for i in range(n)},
        scratch_shapes=[pltpu.SemaphoreType.DMA((3 * n,)), pltpu.SemaphoreType.DMA((3 * n,))])(*lands))


def _own_and_landed(lands, xs):
    chip = _chip()
    return [[jnp.where(chip == j, x, o.reshape(4, *x.shape)[j]) for j in range(4)] for o, x in zip(lands, xs)]


BIG = (("w_in", (D, IN_WIDTH // 4), 1), ("gla_w_o", (D // 4, D), 0), ("mla_w_uq", (MQR, MH * MQK // 4), 1),
       ("mla_w_ukv", (MKVR, MH * (MNOPE + MVD) // 4), 1), ("mla_w_o", (D // 4, D), 0), ("w_out", (D // 4, D), 0),
       ("mlp_w1", (D, DFF // 4), 1), ("mlp_w2", (DFF // 4, D), 0))
ADA_SHARD = (D, 6 * D // 4)
SMALL = (("b_ada", 6 * D), ("norm1_g", D), ("b_merge", 2 * D), ("gla_b_alpha", GH * GDK), ("gla_out_norm_g", GDV),
         ("mla_q_lat_g", MQR), ("mla_kv_lat_g", MKVR), ("mla_qn_g", MQK), ("mla_kn_g", MQK), ("norm2_g", D))


SMALL_ROWS, SMALL_COLS = 32, 2 * D
W_ALPHA_ROW = 16
SMALL_RED = tuple((n, k) for n, k in SMALL if n != "b_ada")


def _pack_small(grads, d_w_alpha, *, name):
    def body(*refs):
        g_refs, wa_ref, out_ref = refs[:-2], refs[-2], refs[-1]
        out_ref[...] = jnp.zeros_like(out_ref)
        for i, ((_, k), g_ref) in enumerate(zip(SMALL_RED, g_refs)):
            out_ref[i:i + 1, 0:k] = g_ref[...]
        out_ref[W_ALPHA_ROW:W_ALPHA_ROW + GLR, 0:GH * GDK] = wa_ref[...]

    return pl.pallas_call(body, name=name, out_shape=jax.ShapeDtypeStruct((SMALL_ROWS, SMALL_COLS), F32))(*grads, d_w_alpha)


def _small_update(gathered, dmod_all, sel, wmv, *, name):
    names = [n for n, _ in SMALL] + ["gla_w_alpha"]
    n_par = len(names)

    def body(sel_ref, g_ref, dmod_ref, *refs):
        in_refs, out_refs, acc = refs[:3 * n_par], refs[3 * n_par:-1], refs[-1]
        total = g_ref[0]
        for j in range(1, 8):
            total = total + g_ref[j]
        acc[...] = total
        row = {n: i for i, (n, _) in enumerate(SMALL_RED)}
        for p, name_p in enumerate(names):
            w_ref, m_ref, v_ref = in_refs[3 * p:3 * p + 3]
            if name_p == "b_ada":
                gv = jnp.sum(dmod_ref[...], axis=0, keepdims=True)
            elif name_p == "gla_w_alpha":
                gv = jnp.zeros((GLR, GDK), F32)
                for j in range(4):
                    blk = acc[W_ALPHA_ROW:W_ALPHA_ROW + GLR, j * GDK:(j + 1) * GDK]
                    gv = gv + jnp.where(sel_ref[0] == j, blk, 0.0)
            else:
                gv = acc[row[name_p]:row[name_p] + 1, 0:w_ref.shape[1]]
            o = out_refs[4 * p:4 * p + 4]
            o[0][...] = gv
            o[1][...], o[2][...], o[3][...] = _adamw_update(w_ref[...], gv, m_ref[...], v_ref[...])

    flat = [a for t in wmv for a in t]
    out_shape = [jax.ShapeDtypeStruct(t[0].shape, F32) for t in wmv for _ in range(4)]
    vmem = pl.BlockSpec(memory_space=pltpu.VMEM)
    outs = pl.pallas_call(
        body, name=name, out_shape=out_shape, in_specs=[pl.BlockSpec(memory_space=pltpu.SMEM), vmem, vmem] + [vmem] * len(flat),
        out_specs=[vmem] * len(out_shape), scratch_shapes=[pltpu.VMEM((SMALL_ROWS, SMALL_COLS), F32)],
    )(sel, gathered, dmod_all, *flat)
    return {n: tuple(outs[4 * p:4 * p + 4]) for p, n in enumerate(names)}


def _full_weights(gathered):
    w = {name: jnp.concatenate(gathered[name], axis=axis) for name, _, axis in BIG if name in gathered}
    if "w_in" in w:
        wi = w["w_in"]
        zeros = lambda n: jnp.zeros((D, n), wi.dtype)
        w["w_in"] = jnp.concatenate(
            [wi[:, :3072], wi[:, 3504:5552], wi[:, 3088:3344], wi[:, 3344:3472], wi[:, 3072:3088], zeros(LANE - GLR),
             zeros(MNOPE), wi[:, 3472:3504], zeros(LANE - MQK)], axis=1)
    if "mla_w_uq" in w:
        w["mla_w_uq"] = jnp.pad(w["mla_w_uq"].reshape(MQR, MH, MQK), ((0, 0), (0, 0), (0, LANE - MQK))).reshape(MQR, MH * LANE)
    if "mla_w_o" in w:
        w["mla_w_o"] = jnp.pad(w["mla_w_o"].reshape(MH, MVD, D), ((0, 0), (0, LANE - MVD), (0, 0))).reshape(MH * LANE, D)
    return w


def _grad_slots(g):
    g = dict(g)
    if "w_in" in g:
        gi = g["w_in"]
        g["w_in"] = jnp.concatenate(
            [gi[:, :3072], gi[:, OFF_A:OFF_A + GLR], gi[:, OFF_CQ:OFF_CQ + MQR], gi[:, OFF_CKV:OFF_CKV + MKVR],
             gi[:, OFF_KPE + MNOPE:OFF_KPE + MQK], gi[:, OFF_MA:OFF_MA + 2 * D]], axis=1)
    if "mla_w_uq" in g:
        g["mla_w_uq"] = g["mla_w_uq"].reshape(MQR, MH, LANE)[:, :, :MQK].reshape(MQR, MH * MQK)
    if "mla_w_o" in g:
        g["mla_w_o"] = g["mla_w_o"].reshape(MH, LANE, D)[:, :MVD].reshape(MH * MVD, D)
    out = {}
    for name, (rows, cols), axis in BIG:
        if name not in g:
            continue
        a = g[name]
        a = a.reshape(4, rows, cols) if axis == 0 else jnp.transpose(a.reshape(rows, 4, cols), (1, 0, 2))
        out[name] = a.reshape(4, 2, rows // 2, cols)
    return out


def _rope_tables(positions):
    freqs = ROPE_THETA ** (-jnp.arange(0, MROPE, 2, dtype=F32) / MROPE)
    lane = np.arange(LANE)
    in_rope = (lane >= MNOPE) & (lane < MQK)
    freq_lane = jnp.where(in_rope, freqs[(lane - MNOPE) % (MROPE // 2)], 0.0)
    sign = np.where(in_rope, np.where(lane < MNOPE + MROPE // 2, -1.0, 1.0), 0.0).astype(np.float32)
    ang = positions.astype(F32).reshape(-1, 1) * freq_lane[None, :]
    return jnp.cos(ang), jnp.sin(ang) * sign[None, :]


def _local_step(x, positions, mod, target, w, small, more_weights=None, on_grads=None):
    kept = {}
    if on_grads is None:
        on_grads = lambda tag, grads: kept.update(grads)
    bsz, s, _ = x.shape
    t = bsz * s
    tt = _tile(t, 1024)
    shift1, scale1, gate1, shift2, scale2, gate2 = [mod[:, None, i * D:(i + 1) * D] for i in range(6)]
    cos_t, sin_t = _rope_tables(positions)
    w_alpha_p = jnp.pad(small["gla_w_alpha"], ((0, LANE - GLR), (0, 0)))
    gq = jnp.pad(small["mla_qn_g"], ((0, 0), (0, LANE - MQK)))
    gk = jnp.pad(small["mla_kn_g"], ((0, 0), (0, LANE - MQK)))
    flat2 = lambda a: a.reshape(t, a.shape[-1])
    bsd = lambda a: a.reshape(bsz, s, a.shape[-1])

    h = _norm_mod(x, small["norm1_g"], scale1, shift1, name="norm1")
    if callable(w):
        w = w(h)
    proj = _mm(flat2(h), w["w_in"], name="proj", tn=1152)
    proj3 = bsd(proj)
    o, o_gated, states = _gla_fwd(proj3, w_alpha_p, small["gla_b_alpha"], small["gla_out_norm_g"], name="gla_fwd")
    if more_weights is not None:
        w = {**w, **more_weights(o_gated)}
    y_a = _mm(flat2(o_gated), w["gla_w_o"], name="gla_out")
    cq_n, ckv_n = _lat_norm(proj, small["mla_q_lat_g"], small["mla_kv_lat_g"], name="lat_norm")
    q_raw = _mm(cq_n, w["mla_w_uq"], name="mla_uq")
    kv = _mm(ckv_n, w["mla_w_ukv"], name="mla_ukv")
    qf, kf, vf = _qk_prep(q_raw, kv, proj, cos_t, sin_t, gq * Q_PRESCALE, gk, name="qk_prep")
    o_attn = _attn_fwd(bsd(qf), bsd(kf), bsd(vf), name="attn_fwd")
    y_b = _mm(flat2(o_attn), w["mla_w_o"], name="mla_out")
    mixed_in = _merge_fwd(proj3, small["b_merge"], bsd(y_a), bsd(y_b), name="merge_fwd")
    mixed = _mm(flat2(mixed_in), w["w_out"], name="w_out")
    x1, h2 = _resid_norm_mod(x, bsd(mixed), gate1, small["norm2_g"], scale2, shift2, name="norm2")

    def sqrelu(acc, ex, outs):
        outs[0][...] = acc
        r = jnp.maximum(acc, 0.0)
        outs[1][...] = (r * r).astype(BF16)

    a1, r = _mm(flat2(h2), w["mlp_w1"], name="mlp1", epilogue=sqrelu,
                out_shape=[jax.ShapeDtypeStruct((t, DFF), F32), jax.ShapeDtypeStruct((t, DFF), BF16)],
                out_specs=[_tile_spec(tt, 1024), _tile_spec(tt, 1024)])
    ff = _mm(r, w["mlp_w2"], name="mlp2")
    dy, dff, dgate2, loss_part = _loss_head(x1, bsd(ff), gate2, target, name="loss_head")

    g = {}

    def relu2_bwd(acc, ex, outs):
        outs[0][...] = (acc * (2.0 * jnp.maximum(ex[0][...], 0.0))).astype(BF16)

    dff2 = flat2(dff)
    da1 = _mm(dff2, w["mlp_w2"], tb=True, name="mlp2_dx", epilogue=relu2_bwd, extras=(a1,),
              extra_specs=(_tile_spec(tt, 1024),), out_shape=jax.ShapeDtypeStruct((t, DFF), BF16),
              out_specs=_tile_spec(tt, 1024))
    g["mlp_w2"] = _mm(r, dff2, ta=True, name="mlp2_dw")
    dh2 = _mm(da1, w["mlp_w1"], tb=True, name="mlp1_dx")
    g["mlp_w1"] = _mm(flat2(h2), da1, ta=True, name="mlp1_dw")
    token = on_grads("mlp", {n: g.pop(n) for n in ("mlp_w2", "mlp_w1")})
    if token is not None:
        gate1 = gate1 + token[0, 0]
    dx1, dscale2, dshift2, dg2, dgate1, dmixed = _norm_mod_bwd(
        bsd(dh2), x1, dy, small["norm2_g"], scale2, gate1, bsd(mixed), name="norm2_bwd")
    dmixed2 = flat2(dmixed)
    dmi = _mm(dmixed2, w["w_out"], tb=True, name="w_out_dx")
    g["w_out"] = _mm(flat2(mixed_in), dmixed2, ta=True, name="w_out_dw")
    dy_a, dy_b, dl_a, dl_b, db_a, db_b = _merge_bwd(bsd(dmi), proj3, small["b_merge"], bsd(y_a), bsd(y_b), name="merge_bwd")
    dy_a2, dy_b2 = flat2(dy_a), flat2(dy_b)
    dog = _mm(dy_a2, w["gla_w_o"], tb=True, name="gla_out_dx")
    g["gla_w_o"] = _mm(flat2(o_gated), dy_a2, ta=True, name="gla_out_dw")
    dq_g, dk_g, dv_g, dg_g, dlog, db_alpha, d_ong = _gla_bwd(
        bsd(dog), o, states, proj3, w_alpha_p, small["gla_b_alpha"], small["gla_out_norm_g"], name="gla_bwd")
    dlog2 = flat2(dlog)
    da_p = _mm(dlog2, w_alpha_p, tb=True, out_dtype=BF16, name="alpha_dx")
    d_w_alpha = _mm(proj[:, OFF_A:OFF_A + LANE], dlog2, ta=True, name="alpha_dw")[:GLR]
    do_attn = _mm(dy_b2, w["mla_w_o"], tb=True, out_dtype=BF16, name="mla_out_dx")
    g["mla_w_o"] = _mm(flat2(o_attn), dy_b2, ta=True, name="mla_out_dw")
    dqf, dkf, dvf = _attn_bwd(bsd(qf), bsd(kf), bsd(vf), bsd(do_attn), name="attn_bwd")
    dq_raw, dkv, dkpe, dgq, dgk = _qk_prep_bwd(flat2(dqf), flat2(dkf), flat2(dvf), q_raw, kv, proj, cos_t, sin_t, gq, gk,
                                                name="qk_prep_bwd")
    dcq_n = _mm(dq_raw, w["mla_w_uq"], tb=True, name="mla_uq_dx")
    g["mla_w_uq"] = _mm(cq_n, dq_raw, ta=True, name="mla_uq_dw")
    dckv_n = _mm(dkv, w["mla_w_ukv"], tb=True, name="mla_ukv_dx")
    g["mla_w_ukv"] = _mm(ckv_n, dkv, ta=True, name="mla_ukv_dw")
    token = on_grads("mix", {n: g.pop(n) for n in ("w_out", "gla_w_o", "mla_w_o", "mla_w_uq", "mla_w_ukv")})
    q_lat_g = small["mla_q_lat_g"] if token is None else small["mla_q_lat_g"] + token[0:1, 0:1]
    dcq, dckv, dg_qlat, dg_kvlat = _lat_norm_bwd(dcq_n, dckv_n, proj, q_lat_g, small["mla_kv_lat_g"],
                                                  name="lat_norm_bwd")
    dproj = jnp.concatenate([flat2(dq_g), flat2(dk_g), flat2(dv_g), flat2(dg_g), flat2(dl_a), flat2(dl_b), dcq, dckv,
                             da_p, dkpe.astype(BF16)], axis=1)
    g["w_in"] = _mm(flat2(h), dproj, ta=True, name="proj_dw", tn=1152)
    token = on_grads("in", {"w_in": g.pop("w_in")})
    after = {} if token is None else dict(extras=(token,), extra_specs=(pl.BlockSpec((8, LANE), lambda i, j, k: (0, 0)),))
    dh = _mm(dproj, w["w_in"], tb=True, name="proj_dx", tk=1152, **after)
    grad_x, dscale1, dshift1, dg1 = _norm_mod_bwd(bsd(dh), x, dx1, small["norm1_g"], scale1, name="norm1_bwd")

    dmod = jnp.concatenate([dshift1, dscale1, dgate1, dshift2, dscale2, dgate2], axis=-1).reshape(bsz, 6 * D)
    gs = {"norm1_g": dg1, "b_merge": jnp.concatenate([db_a, db_b], axis=1), "gla_b_alpha": db_alpha,
          "gla_out_norm_g": d_ong, "mla_q_lat_g": dg_qlat, "mla_kv_lat_g": dg_kvlat, "mla_qn_g": dgq[:, :MQK],
          "mla_kn_g": dgk[:, :MQK], "norm2_g": dg2}
    return loss_part[0, 0], grad_x, dmod, {**kept, **g}, gs, d_w_alpha


def kernel(x, c, positions, w_ada, b_ada, norm1_g, w_in, b_merge, gla_w_alpha, gla_b_alpha, gla_out_norm_g, gla_w_o, mla_q_lat_g, mla_w_uq, mla_kv_lat_g, mla_w_ukv, mla_qn_g, mla_kn_g, mla_w_o, w_out, norm2_g, mlp_w1, mlp_w2, loss_target, m_w_ada, m_b_ada, m_norm1_g, m_w_in, m_b_merge, m_gla_w_alpha, m_gla_b_alpha, m_gla_out_norm_g, m_gla_w_o, m_mla_q_lat_g, m_mla_w_uq, m_mla_kv_lat_g, m_mla_w_ukv, m_mla_qn_g, m_mla_kn_g, m_mla_w_o, m_w_out, m_norm2_g, m_mlp_w1, m_mlp_w2, v_w_ada, v_b_ada, v_norm1_g, v_w_in, v_b_merge, v_gla_w_alpha, v_gla_b_alpha, v_gla_out_norm_g, v_gla_w_o, v_mla_q_lat_g, v_mla_w_uq, v_mla_kv_lat_g, v_mla_w_ukv, v_mla_qn_g, v_mla_kn_g, v_mla_w_o, v_w_out, v_norm2_g, v_mlp_w1, v_mlp_w2):
    args = dict(locals())
    names_big = [n for n, _, _ in BIG]
    names_small = [n for n, _ in SMALL]
    bsz = x.shape[0]
    ax, ay, ac = lax.axis_index("x"), lax.axis_index("y"), lax.axis_index("c")
    chip = 2 * ax + ay
    dev = 2 * chip + ac

    small = {n: args[n] for n in names_small}
    sel_c = jnp.reshape(ac, (1,)).astype(jnp.int32)
    sel_chip = jnp.reshape(chip, (1,)).astype(jnp.int32)
    c_all, w_alpha_all = _all_gather8([c, gla_w_alpha[0]], name="comm_c_alpha")
    small["gla_w_alpha"] = jnp.concatenate([w_alpha_all[2 * j] for j in range(4)], axis=1)
    c_all = c_all.reshape(8 * bsz, D)

    shards = {n: args[n][0].astype(BF16) for n in names_big}
    halves_of = lambda names: [shards[n].reshape(2, shards[n].shape[0] // 2, shards[n].shape[1]) for n in names]

    def gather_start(names, deps, tag):
        xs = halves_of(names)
        lands = [lax.empty((4, *xh.shape), BF16) for xh in xs]
        plan = _gather_plan(len(names))
        return names, plan, _rdma_start(xs + lands, 3 * len(names), plan, deps, name="comm_weights_start_" + tag)

    def gather_finish(started, after, tag):
        names, plan, sems = started
        arrs = _rdma_wait(sems[0], sems[1], sems[2], plan, after, name="comm_weights_wait_" + tag)
        filled = _pair_fill(arrs[len(names):], name="comm_weights_pair_" + tag)
        return _full_weights(dict(zip(names, _own_and_landed(filled, [shards[n] for n in names]))))

    first = gather_start(["w_in"], (c_all,), "in")
    c_all = c_all + first[2][3][0, 0]


    def add_bias(acc, ex, outs):
        outs[0][...] = acc + ex[0][...]

    silu = lambda v: v * _sigmoid(v)
    b_ada_mine = lax.dynamic_slice(b_ada, (0, chip * ADA_SHARD[1]), (1, ADA_SHARD[1]))
    mod_part = _mm(c_all, w_ada[0], name="ada", tn=512, a_fn=silu, epilogue=add_bias, extras=(b_ada_mine,),
                   extra_specs=(pl.BlockSpec((1, 512), lambda i, j, k: (0, j)),),
                   out_shape=jax.ShapeDtypeStruct((8 * bsz, ADA_SHARD[1]), F32), out_specs=_tile_spec(8 * bsz, 512))
    mod_all = _all_gather8([mod_part], name="comm_mod")[0]
    mod_rows = lax.dynamic_slice(mod_all, (0, dev * bsz, 0), (8, bsz, ADA_SHARD[1]))
    mod = jnp.concatenate([mod_rows[2 * j] for j in range(4)], axis=1)
    rest = gather_start([n for n in names_big if n != "w_in"], (mod,), "rest")
    mod = mod + rest[2][3][0, 0]
    w_in_after = lambda after: gather_finish(first, after, "in")
    more_weights = lambda after: gather_finish(rest, after, "rest")

    in_flight = []

    def reduce_start(tag, grads):
        names = list(grads)
        parts = [_grad_slots(grads)[n] for n in names]
        sib_halves = _pair_swap_halves(parts, name="comm_pair_sum_" + tag)
        pairs = [_pair_add(p, s, sel_c, name="pair_add_" + n) for n, p, s in zip(names, parts, sib_halves)]
        recvs = [lax.empty((3, *p.shape[1:]), BF16) for p in pairs]
        plan = _scatter_plan(len(names))
        sems = _rdma_start(pairs + recvs, 3 * len(names), plan, (), name="comm_scatter_start_" + tag)
        in_flight.append((tag, names, plan, sems))
        return sems[3]

    loss_part, grad_x, dmod, g, gs, d_w_alpha = _local_step(x, positions, mod, loss_target, w_in_after, small,
                                                            more_weights, reduce_start)
    loss = lax.psum(loss_part * (0.5 / D), ("x", "y", "c"))

    gs_packed = _pack_small([gs[n] for n, _ in SMALL_RED], d_w_alpha, name="pack_small")
    dmod_all, gs_all = _all_gather8([dmod, gs_packed], name="comm_dmod_small")
    dmod_all = dmod_all.reshape(8 * bsz, 6 * D)
    dmod_mine = lax.dynamic_slice(dmod_all, (0, chip * ADA_SHARD[1]), (8 * bsz, ADA_SHARD[1]))
    g_w_ada = _mm(c_all, dmod_mine, ta=True, a_fn=silu, name="ada_dw")

    wmv = [(args[n], args["m_" + n], args["v_" + n]) for n in names_small]
    wmv.append((gla_w_alpha[0], m_gla_w_alpha[0], v_gla_w_alpha[0]))
    res = _small_update(gs_all, dmod_all, sel_chip, wmv, name="small_update")

    assert not g, list(g)
    half_of = {}
    for tag, names, plan, sems in in_flight:
        arrs = _rdma_wait(sems[0], sems[1], sems[2], plan, grad_x, name="comm_scatter_wait_" + tag)
        for n, p, r in zip(names, arrs[:len(names)], arrs[len(names):]):
            half_of[n] = _chip_sum(p, r, sel_chip, name="chip_sum_" + n)
    halves = [half_of[n] for n in names_big]
    theirs = _pair_swap(halves, name="comm_pair_join")
    for n, mine, other in zip(names_big, halves, theirs):
        res[n] = _adamw_halves(args[n][0], args["m_" + n][0], args["v_" + n][0], mine, other, sel_c, name="adamw_" + n)
    res["w_ada"] = (g_w_ada, *_adamw(w_ada[0], g_w_ada, m_w_ada[0], v_w_ada[0], name="adamw_w_ada"))

    order = ["w_ada", "b_ada", "norm1_g", "w_in", "b_merge", "gla_w_alpha", "gla_b_alpha", "gla_out_norm_g", "gla_w_o",
             "mla_q_lat_g", "mla_w_uq", "mla_kv_lat_g", "mla_w_ukv", "mla_qn_g", "mla_kn_g", "mla_w_o", "w_out",
             "norm2_g", "mlp_w1", "mlp_w2"]
    named = lambda k: [res[n][k].reshape(args[n].shape) for n in order]
    return (loss, grad_x, *named(0), *named(1), *named(2), *named(3))
```

```python
import functools

import jax
import jax.numpy as jnp
import numpy as np
from jax import lax
from jax.experimental import pallas as pl
from jax.experimental.pallas import tpu as pltpu

F32 = jnp.float32
BF16 = jnp.bfloat16
MESH = pl.DeviceIdType.MESH

D = 1024
CHUNK = 64
EPS = 1e-6
GH, GDK, GDV, GLR, GTAU = 4, 128, 256, 16, 16.0
MH, MQR, MKVR, MNOPE, MROPE, MVD = 16, 256, 128, 64, 32, 64
MQK = MNOPE + MROPE
DFF = 4 * D
ROPE_THETA = 10000.0
IN_WIDTH = 5552
LANE = 128
OFF_Q, OFF_K, OFF_V, OFF_G, OFF_MA, OFF_MB, OFF_CQ, OFF_CKV, OFF_A, OFF_KPE, PW = (
    0, 512, 1024, 2048, 3072, 4096, 5120, 5376, 5504, 5632, 5760)
ADAM_LR, ADAM_B1, ADAM_B2, ADAM_EPS, ADAM_WD, ADAM_STEP = 0.001, 0.9, 0.999, 1e-08, 0.01, 10
VMEM_LIMIT = 48 * 1024 * 1024


def _params(n_axes):
    return pltpu.CompilerParams(dimension_semantics=("arbitrary",) * n_axes, vmem_limit_bytes=VMEM_LIMIT)


def _tile(n, target):
    if n <= target:
        return n
    best = None
    for t in range(LANE, target + 1, LANE):
        if n % t == 0:
            best = t
    assert best is not None, (n, target)
    return best


def _sigmoid(x):
    return 1.0 / (1.0 + jnp.exp(-x))


def _mm(a, b, *, name, ta=False, tb=False, out_dtype=F32, tm=1024, tn=1024, tk=1024,
        epilogue=None, extras=(), extra_specs=(), out_shape=None, out_specs=None, a_fn=None):
    if ta:
        kdim, m = a.shape
    else:
        m, kdim = a.shape
    if tb:
        n, k2 = b.shape
    else:
        k2, n = b.shape
    assert kdim == k2, (a.shape, b.shape)
    tm, tn, tk = _tile(m, tm), _tile(n, tn), _tile(kdim, tk)
    nk = kdim // tk
    a_spec = pl.BlockSpec((tk, tm), lambda i, j, k: (k, i)) if ta else pl.BlockSpec((tm, tk), lambda i, j, k: (i, k))
    b_spec = pl.BlockSpec((tn, tk), lambda i, j, k: (j, k)) if tb else pl.BlockSpec((tk, tn), lambda i, j, k: (k, j))
    dims = (((0 if ta else 1,), (1 if tb else 0,)), ((), ()))
    ne = len(extras)
    if out_shape is None:
        out_shape = jax.ShapeDtypeStruct((m, n), out_dtype)
        out_specs = pl.BlockSpec((tm, tn), lambda i, j, k: (i, j))

    def body(a_ref, b_ref, *rest):
        ex, outs, acc = rest[:ne], rest[ne:-1], rest[-1]
        k = pl.program_id(2)

        @pl.when(k == 0)
        def _():
            acc[...] = jnp.zeros_like(acc)

        av = a_ref[...] if a_fn is None else a_fn(a_ref[...])
        acc[...] += lax.dot_general(av.astype(BF16), b_ref[...].astype(BF16), dims, preferred_element_type=F32)

        @pl.when(k == nk - 1)
        def _():
            if epilogue is None:
                outs[0][...] = acc[...].astype(outs[0].dtype)
            else:
                epilogue(acc[...], ex, outs)

    return pl.pallas_call(
        body, name=name, grid=(m // tm, n // tn, nk),
        in_specs=[a_spec, b_spec, *extra_specs], out_specs=out_specs, out_shape=out_shape,
        scratch_shapes=[pltpu.VMEM((tm, tn), F32)], compiler_params=_params(3),
    )(a, b, *extras)


def _tile_spec(tm, tn):
    return pl.BlockSpec((tm, tn), lambda i, j, k: (i, j))


def _rms(x, g):
    r = lax.rsqrt(jnp.mean(x * x, axis=-1, keepdims=True) + EPS)
    return x * r, r


def _row_spec(ts, width, col=0):
    return pl.BlockSpec((None, ts, width), lambda b, i: (b, i, col))


def _vec_spec(width):
    return pl.BlockSpec((None, 1, width), lambda b, i: (b, 0, 0))


def _gain_spec(width):
    return pl.BlockSpec((1, width), lambda b, i: (0, 0))


def _norm_mod(x, g, scale, shift, *, name, ts=256):
    bsz, s, d = x.shape
    ts = min(ts, s)

    def body(x_ref, g_ref, sc_ref, sh_ref, h_ref):
        xh, _ = _rms(x_ref[...], None)
        h_ref[...] = ((xh * g_ref[...]) * (1.0 + sc_ref[...]) + sh_ref[...]).astype(BF16)

    return pl.pallas_call(
        body, name=name, grid=(bsz, s // ts),
        in_specs=[_row_spec(ts, d), _gain_spec(d), _vec_spec(d), _vec_spec(d)],
        out_specs=_row_spec(ts, d), out_shape=jax.ShapeDtypeStruct((bsz, s, d), BF16),
        compiler_params=_params(2),
    )(x, g, scale, shift)


def _resid_norm_mod(x, mixed, gate, g, scale, shift, *, name, ts=256):
    bsz, s, d = x.shape
    ts = min(ts, s)

    def body(x_ref, mx_ref, gt_ref, g_ref, sc_ref, sh_ref, x1_ref, h_ref):
        x1 = x_ref[...] + gt_ref[...] * mx_ref[...]
        x1_ref[...] = x1
        xh, _ = _rms(x1, None)
        h_ref[...] = ((xh * g_ref[...]) * (1.0 + sc_ref[...]) + sh_ref[...]).astype(BF16)

    return pl.pallas_call(
        body, name=name, grid=(bsz, s // ts),
        in_specs=[_row_spec(ts, d), _row_spec(ts, d), _vec_spec(d), _gain_spec(d), _vec_spec(d), _vec_spec(d)],
        out_specs=[_row_spec(ts, d), _row_spec(ts, d)],
        out_shape=[jax.ShapeDtypeStruct((bsz, s, d), F32), jax.ShapeDtypeStruct((bsz, s, d), BF16)],
        compiler_params=_params(2),
    )(x, mixed, gate, g, scale, shift)


def _norm_mod_bwd(dh, xin, resid, g, scale, gate=None, mixed=None, *, name, ts=256):
    bsz, s, d = xin.shape
    ts = min(ts, s)
    gated = gate is not None

    def body(*refs):
        if gated:
            dh_ref, x_ref, rs_ref, g_ref, sc_ref, gt_ref, mx_ref, dx_ref, dsc_ref, dsh_ref, dg_ref, dgt_ref, dmx_ref = refs
        else:
            dh_ref, x_ref, rs_ref, g_ref, sc_ref, dx_ref, dsc_ref, dsh_ref, dg_ref = refs
        b, i = pl.program_id(0), pl.program_id(1)

        @pl.when(i == 0)
        def _():
            dsc_ref[...] = jnp.zeros_like(dsc_ref)
            dsh_ref[...] = jnp.zeros_like(dsh_ref)
            if gated:
                dgt_ref[...] = jnp.zeros_like(dgt_ref)

        @pl.when((i == 0) & (b == 0))
        def _():
            dg_ref[...] = jnp.zeros_like(dg_ref)

        dh_v, gv = dh_ref[...], g_ref[...]
        xh, r = _rms(x_ref[...], None)
        dsc_ref[...] += jnp.sum(dh_v * (xh * gv), axis=0, keepdims=True)
        dsh_ref[...] += jnp.sum(dh_v, axis=0, keepdims=True)
        dn = dh_v * (1.0 + sc_ref[...])
        dg_ref[...] += jnp.sum(dn * xh, axis=0, keepdims=True)
        dxh = dn * gv
        dx = rs_ref[...] + r * (dxh - xh * jnp.mean(dxh * xh, axis=-1, keepdims=True))
        dx_ref[...] = dx
        if gated:
            dgt_ref[...] += jnp.sum(dx * mx_ref[...], axis=0, keepdims=True)
            dmx_ref[...] = (dx * gt_ref[...]).astype(BF16)

    ins = [dh, xin, resid, g, scale]
    in_specs = [_row_spec(ts, d), _row_spec(ts, d), _row_spec(ts, d), _gain_spec(d), _vec_spec(d)]
    out_specs = [_row_spec(ts, d), _vec_spec(d), _vec_spec(d), _gain_spec(d)]
    out_shape = [jax.ShapeDtypeStruct((bsz, s, d), F32), jax.ShapeDtypeStruct((bsz, 1, d), F32),
                 jax.ShapeDtypeStruct((bsz, 1, d), F32), jax.ShapeDtypeStruct((1, d), F32)]
    if gated:
        ins += [gate, mixed]
        in_specs += [_vec_spec(d), _row_spec(ts, d)]
        out_specs += [_vec_spec(d), _row_spec(ts, d)]
        out_shape += [jax.ShapeDtypeStruct((bsz, 1, d), F32), jax.ShapeDtypeStruct((bsz, s, d), BF16)]
    return pl.pallas_call(
        body, name=name, grid=(bsz, s // ts), in_specs=in_specs, out_specs=out_specs, out_shape=out_shape,
        compiler_params=_params(2),
    )(*ins)


def _loss_head(x1, ff, gate2, target, *, name, ts=256):
    bsz, s, d = x1.shape
    ts = min(ts, s)

    def body(x1_ref, ff_ref, gt_ref, t_ref, dy_ref, dff_ref, dgt_ref, loss_ref, acc):
        b, i = pl.program_id(0), pl.program_id(1)

        @pl.when(i == 0)
        def _():
            dgt_ref[...] = jnp.zeros_like(dgt_ref)

        @pl.when((i == 0) & (b == 0))
        def _():
            acc[...] = jnp.zeros_like(acc)

        ffv, gt = ff_ref[...], gt_ref[...]
        diff = (x1_ref[...] + gt * ffv) - t_ref[...]
        acc[...] += jnp.sum((diff * diff).reshape(ts // 8, 8, d), axis=0)
        dy = diff * (1.0 / d)
        dy_ref[...] = dy
        dgt_ref[...] += jnp.sum(dy * ffv, axis=0, keepdims=True)
        dff_ref[...] = (dy * gt).astype(BF16)

        @pl.when((i == pl.num_programs(1) - 1) & (b == pl.num_programs(0) - 1))
        def _():
            loss_ref[...] = jnp.full(loss_ref.shape, jnp.sum(acc[...]), F32)

    return pl.pallas_call(
        body, name=name, grid=(bsz, s // ts),
        in_specs=[_row_spec(ts, d), _row_spec(ts, d), _vec_spec(d), _row_spec(ts, d)],
        out_specs=[_row_spec(ts, d), _row_spec(ts, d), _vec_spec(d), pl.BlockSpec((8, LANE), lambda b, i: (0, 0))],
        out_shape=[jax.ShapeDtypeStruct((bsz, s, d), F32), jax.ShapeDtypeStruct((bsz, s, d), BF16),
                   jax.ShapeDtypeStruct((bsz, 1, d), F32), jax.ShapeDtypeStruct((8, LANE), F32)],
        scratch_shapes=[pltpu.VMEM((8, d), F32)], compiler_params=_params(2),
    )(x1, ff, gate2, target)


def _merge_fwd(proj, b_merge, y_a, y_b, *, name, ts=256):
    bsz, s, _ = proj.shape
    ts = min(ts, s)

    def body(la_ref, lb_ref, ba_ref, bb_ref, ya_ref, yb_ref, out_ref):
        ga = _sigmoid(la_ref[...] + ba_ref[...])
        gb = _sigmoid(lb_ref[...] + bb_ref[...])
        out_ref[...] = (ga * ya_ref[...] + gb * yb_ref[...]).astype(BF16)

    return pl.pallas_call(
        body, name=name, grid=(bsz, s // ts),
        in_specs=[_row_spec(ts, D, OFF_MA // D), _row_spec(ts, D, OFF_MB // D),
                  pl.BlockSpec((1, D), lambda b, i: (0, 0)), pl.BlockSpec((1, D), lambda b, i: (0, 1)),
                  _row_spec(ts, D), _row_spec(ts, D)],
        out_specs=_row_spec(ts, D), out_shape=jax.ShapeDtypeStruct((bsz, s, D), BF16),
        compiler_params=_params(2),
    )(proj, proj, b_merge, b_merge, y_a, y_b)


def _merge_bwd(dmi, proj, b_merge, y_a, y_b, *, name, ts=256):
    bsz, s, _ = proj.shape
    ts = min(ts, s)

    def body(d_ref, la_ref, lb_ref, ba_ref, bb_ref, ya_ref, yb_ref, dya_ref, dyb_ref, dla_ref, dlb_ref, dba_ref, dbb_ref):
        @pl.when((pl.program_id(0) == 0) & (pl.program_id(1) == 0))
        def _():
            dba_ref[...] = jnp.zeros_like(dba_ref)
            dbb_ref[...] = jnp.zeros_like(dbb_ref)

        dv = d_ref[...]
        ga = _sigmoid(la_ref[...] + ba_ref[...])
        gb = _sigmoid(lb_ref[...] + bb_ref[...])
        dya_ref[...] = (dv * ga).astype(BF16)
        dyb_ref[...] = (dv * gb).astype(BF16)
        dla = (dv * ya_ref[...]) * (ga * (1.0 - ga))
        dlb = (dv * yb_ref[...]) * (gb * (1.0 - gb))
        dla_ref[...] = dla.astype(BF16)
        dlb_ref[...] = dlb.astype(BF16)
        dba_ref[...] += jnp.sum(dla, axis=0, keepdims=True)
        dbb_ref[...] += jnp.sum(dlb, axis=0, keepdims=True)

    act = jax.ShapeDtypeStruct((bsz, s, D), BF16)
    return pl.pallas_call(
        body, name=name, grid=(bsz, s // ts),
        in_specs=[_row_spec(ts, D), _row_spec(ts, D, OFF_MA // D), _row_spec(ts, D, OFF_MB // D),
                  pl.BlockSpec((1, D), lambda b, i: (0, 0)), pl.BlockSpec((1, D), lambda b, i: (0, 1)),
                  _row_spec(ts, D), _row_spec(ts, D)],
        out_specs=[_row_spec(ts, D)] * 4 + [_gain_spec(D)] * 2,
        out_shape=[act, act, act, act, jax.ShapeDtypeStruct((1, D), F32), jax.ShapeDtypeStruct((1, D), F32)],
        compiler_params=_params(2),
    )(dmi, proj, proj, b_merge, b_merge, y_a, y_b)


def _tri(lower):
    r = lax.broadcasted_iota(jnp.int32, (CHUNK, CHUNK), 0)
    c = lax.broadcasted_iota(jnp.int32, (CHUNK, CHUNK), 1)
    return jnp.where((c <= r) if lower else (c >= r), 1.0, 0.0).astype(F32)


def _gla_logits(a_ref, wal_ref, bal_ref):
    logits = jnp.dot(a_ref[...].astype(BF16), wal_ref[...].astype(BF16), preferred_element_type=F32) + bal_ref[...]
    la = (jnp.minimum(logits, 0.0) - jnp.log(1.0 + jnp.exp(-jnp.abs(logits)))) * (1.0 / GTAU)
    return logits, la


def _chunk_cumsum(la_n, tri):
    cum = jnp.dot(tri, la_n, preferred_element_type=F32, precision=lax.Precision.HIGHEST)
    return cum, jnp.sum(la_n, axis=0, keepdims=True)


def _gla_specs(s, nc):
    def blk(width, off):
        return pl.BlockSpec((None, s, width), lambda h, b: (b, 0, off // width + h))

    proj_specs = [blk(GDK, OFF_Q), blk(GDK, OFF_K), blk(GDV, OFF_V), blk(GDV, OFF_G),
                  pl.BlockSpec((None, s, LANE), lambda h, b: (b, 0, OFF_A // LANE)),
                  pl.BlockSpec((LANE, GDK), lambda h, b: (0, h)), pl.BlockSpec((1, GDK), lambda h, b: (0, h)),
                  pl.BlockSpec((1, GDV), lambda h, b: (0, 0))]
    st_spec = pl.BlockSpec((None, None, nc, GDV, GDK), lambda h, b: (b, h, 0, 0, 0))
    return blk, proj_specs, st_spec


def _gla_fwd(proj, w_alpha_p, b_alpha, out_norm_g, *, name):
    bsz, s, _ = proj.shape
    nc = s // CHUNK
    scale = GDK ** -0.5

    rb = min(512, s)

    def body(q_ref, k_ref, v_ref, g_ref, a_ref, wal_ref, bal_ref, ong_ref, o_ref, og_ref, st_ref):
        _, la = _gla_logits(a_ref, wal_ref, bal_ref)
        tri = _tri(True)
        st = jnp.zeros((GDV, GDK), F32)
        for n in range(nc):
            rows = pl.ds(n * CHUNK, CHUNK)
            cum, cum_end = _chunk_cumsum(la[n * CHUNK:(n + 1) * CHUNK], tri)
            kd = k_ref[rows, :] * jnp.exp(cum_end - cum)
            ut = lax.dot_general(v_ref[rows, :].astype(BF16), kd.astype(BF16), _TN, preferred_element_type=F32)
            st = st * jnp.exp(cum_end) + ut
            st_ref[n] = st
            o_ref[rows, :] = lax.dot_general((q_ref[rows, :] * scale).astype(BF16), st.astype(BF16), _NT,
                                             preferred_element_type=F32)
        for j in range(0, s, rb):
            blk_rows = pl.ds(j, rb)
            oh, _ = _rms(o_ref[blk_rows, :], None)
            gv = g_ref[blk_rows, :]
            og_ref[blk_rows, :] = ((oh * ong_ref[...]) * (gv * _sigmoid(gv))).astype(BF16)

    blk, proj_specs, st_spec = _gla_specs(s, nc)
    return pl.pallas_call(
        body, name=name, grid=(GH, bsz), in_specs=proj_specs, out_specs=[blk(GDV, 0), blk(GDV, 0), st_spec],
        out_shape=[jax.ShapeDtypeStruct((bsz, s, GH * GDV), F32), jax.ShapeDtypeStruct((bsz, s, GH * GDV), BF16),
                   jax.ShapeDtypeStruct((bsz, GH, nc, GDV, GDK), F32)],
        compiler_params=_params(2),
    )(proj, proj, proj, proj, proj, w_alpha_p, b_alpha, out_norm_g)


def _gla_bwd(dog, o, states, proj, w_alpha_p, b_alpha, out_norm_g, *, name):
    bsz, s, _ = proj.shape
    nc = s // CHUNK
    scale = GDK ** -0.5

    def body(dog_ref, o_ref, st_ref, q_ref, k_ref, v_ref, g_ref, a_ref, wal_ref, bal_ref, ong_ref,
             dq_ref, dk_ref, dv_ref, dg_ref, dl_ref, dbal_ref, dong_ref, do_scr, dlog_scr):
        h, b = pl.program_id(0), pl.program_id(1)

        @pl.when(b == 0)
        def _():
            dbal_ref[...] = jnp.zeros_like(dbal_ref)

        @pl.when((b == 0) & (h == 0))
        def _():
            dong_ref[...] = jnp.zeros_like(dong_ref)

        ong = ong_ref[...]
        for j in range(0, s, rb):
            blk_rows = pl.ds(j, rb)
            gv, dogv = g_ref[blk_rows, :], dog_ref[blk_rows, :]
            sg = _sigmoid(gv)
            oh, r = _rms(o_ref[blk_rows, :], None)
            don = dogv * (gv * sg)
            dg_ref[blk_rows, :] = (dogv * (oh * ong) * (sg * (1.0 + gv * (1.0 - sg)))).astype(BF16)
            dong_ref[...] += jnp.sum(don * oh, axis=0, keepdims=True)
            doh = don * ong
            do_scr[blk_rows, :] = (r * (doh - oh * jnp.mean(doh * oh, axis=-1, keepdims=True))).astype(BF16)

        logits, la = _gla_logits(a_ref, wal_ref, bal_ref)
        tri_lo, tri_up = _tri(True), _tri(False)
        carry = jnp.zeros((GDV, GDK), F32)
        for n in range(nc - 1, -1, -1):
            rows = pl.ds(n * CHUNK, CHUNK)
            cum, cum_end = _chunk_cumsum(la[n * CHUNK:(n + 1) * CHUNK], tri_lo)
            decay = jnp.exp(cum_end)
            w = jnp.exp(cum_end - cum)
            kd = k_ref[rows, :] * w
            do_b = do_scr[rows, :]
            qs_b = (q_ref[rows, :] * scale).astype(BF16)
            dq_ref[rows, :] = (jnp.dot(do_b, st_ref[n].astype(BF16), preferred_element_type=F32) * scale).astype(BF16)
            dsn = lax.dot_general(do_b, qs_b, _TN, preferred_element_type=F32) + carry
            carry = dsn * decay
            dsn_b = dsn.astype(BF16)
            dv_ref[rows, :] = lax.dot_general(kd.astype(BF16), dsn_b, _NT, preferred_element_type=F32).astype(BF16)
            dkd = jnp.dot(v_ref[rows, :].astype(BF16), dsn_b, preferred_element_type=F32)
            dk_ref[rows, :] = (dkd * w).astype(BF16)
            e = dkd * kd
            dcum_end = jnp.sum(e, axis=0, keepdims=True)
            if n > 0:
                dcum_end += jnp.sum(dsn * st_ref[n - 1], axis=0, keepdims=True) * decay
            dlog_scr[rows, :] = dcum_end - jnp.dot(tri_up, e, preferred_element_type=F32,
                                                  precision=lax.Precision.HIGHEST)
        dlog = dlog_scr[...] * (1.0 / GTAU) * (1.0 - _sigmoid(logits))
        dl_ref[...] = dlog.astype(BF16)
        dbal_ref[...] += jnp.sum(dlog, axis=0, keepdims=True)

    rb = min(512, s)

    blk, proj_specs, st_spec = _gla_specs(s, nc)
    act = lambda wd: jax.ShapeDtypeStruct((bsz, s, wd), BF16)
    return pl.pallas_call(
        body, name=name, grid=(GH, bsz), in_specs=[blk(GDV, 0), blk(GDV, 0), st_spec, *proj_specs],
        out_specs=[blk(GDK, 0), blk(GDK, 0), blk(GDV, 0), blk(GDV, 0), blk(GDK, 0),
                   pl.BlockSpec((1, GDK), lambda h, b: (0, h)), pl.BlockSpec((1, GDV), lambda h, b: (0, 0))],
        out_shape=[act(GH * GDK), act(GH * GDK), act(GH * GDV), act(GH * GDV), act(GH * GDK),
                   jax.ShapeDtypeStruct((1, GH * GDK), F32), jax.ShapeDtypeStruct((1, GDV), F32)],
        scratch_shapes=[pltpu.VMEM((s, GDV), BF16), pltpu.VMEM((s, GDK), F32)], compiler_params=_params(2),
    )(dog, o, states, proj, proj, proj, proj, proj, w_alpha_p, b_alpha, out_norm_g)


def _lane():
    return lax.broadcasted_iota(jnp.int32, (1, LANE), 1)


def _swap_halves(x):
    lane = _lane()
    half = MROPE // 2
    lo = (lane >= MNOPE) & (lane < MNOPE + half)
    hi = (lane >= MNOPE + half) & (lane < MQK)
    return jnp.where(lo, pltpu.roll(x, LANE - half, 1), jnp.where(hi, pltpu.roll(x, half, 1), 0.0))


def _norm96(x, g):
    r = lax.rsqrt(jnp.sum(x * x, axis=-1, keepdims=True) * (1.0 / MQK) + EPS)
    return x * r, r


def _lat_norm(proj, q_lat_g, kv_lat_g, *, name, ts=512):
    t = proj.shape[0]
    ts = min(ts, t)

    def body(cq_ref, ckv_ref, gq_ref, gk_ref, oq_ref, ok_ref):
        xq, _ = _rms(cq_ref[...], None)
        oq_ref[...] = (xq * gq_ref[...]).astype(BF16)
        xk, _ = _rms(ckv_ref[...], None)
        ok_ref[...] = (xk * gk_ref[...]).astype(BF16)

    return pl.pallas_call(
        body, name=name, grid=(t // ts,),
        in_specs=[pl.BlockSpec((ts, MQR), lambda i: (i, OFF_CQ // MQR)), pl.BlockSpec((ts, MKVR), lambda i: (i, OFF_CKV // MKVR)),
                  pl.BlockSpec((1, MQR), lambda i: (0, 0)), pl.BlockSpec((1, MKVR), lambda i: (0, 0))],
        out_specs=[pl.BlockSpec((ts, MQR), lambda i: (i, 0)), pl.BlockSpec((ts, MKVR), lambda i: (i, 0))],
        out_shape=[jax.ShapeDtypeStruct((t, MQR), BF16), jax.ShapeDtypeStruct((t, MKVR), BF16)],
        compiler_params=_params(1),
    )(proj, proj, q_lat_g, kv_lat_g)


def _lat_norm_bwd(dcqn, dckvn, proj, q_lat_g, kv_lat_g, *, name, ts=512):
    t = proj.shape[0]
    ts = min(ts, t)

    def one(d_ref, x_ref, g_ref, dx_ref, dg_ref):
        xh, r = _rms(x_ref[...], None)
        dn = d_ref[...]
        dg_ref[...] += jnp.sum(dn * xh, axis=0, keepdims=True)
        dxh = dn * g_ref[...]
        dx_ref[...] = (r * (dxh - xh * jnp.mean(dxh * xh, axis=-1, keepdims=True))).astype(BF16)

    def body(dq_ref, dk_ref, cq_ref, ckv_ref, gq_ref, gk_ref, dxq_ref, dxk_ref, dgq_ref, dgk_ref):
        @pl.when(pl.program_id(0) == 0)
        def _():
            dgq_ref[...] = jnp.zeros_like(dgq_ref)
            dgk_ref[...] = jnp.zeros_like(dgk_ref)

        one(dq_ref, cq_ref, gq_ref, dxq_ref, dgq_ref)
        one(dk_ref, ckv_ref, gk_ref, dxk_ref, dgk_ref)

    return pl.pallas_call(
        body, name=name, grid=(t // ts,),
        in_specs=[pl.BlockSpec((ts, MQR), lambda i: (i, 0)), pl.BlockSpec((ts, MKVR), lambda i: (i, 0)),
                  pl.BlockSpec((ts, MQR), lambda i: (i, OFF_CQ // MQR)), pl.BlockSpec((ts, MKVR), lambda i: (i, OFF_CKV // MKVR)),
                  pl.BlockSpec((1, MQR), lambda i: (0, 0)), pl.BlockSpec((1, MKVR), lambda i: (0, 0))],
        out_specs=[pl.BlockSpec((ts, MQR), lambda i: (i, 0)), pl.BlockSpec((ts, MKVR), lambda i: (i, 0)),
                   pl.BlockSpec((1, MQR), lambda i: (0, 0)), pl.BlockSpec((1, MKVR), lambda i: (0, 0))],
        out_shape=[jax.ShapeDtypeStruct((t, MQR), BF16), jax.ShapeDtypeStruct((t, MKVR), BF16),
                   jax.ShapeDtypeStruct((1, MQR), F32), jax.ShapeDtypeStruct((1, MKVR), F32)],
        compiler_params=_params(1),
    )(dcqn, dckvn, proj, proj, q_lat_g, kv_lat_g)


def _qk_prep(q_raw, kv, proj, cos_t, sin_t, gq, gk, *, name, ts=2048):
    t = q_raw.shape[0]
    ts = min(ts, t)

    def body(q_ref, kv_ref, kpe_ref, c_ref, s_ref, gq_ref, gk_ref, qo_ref, ko_ref, vo_ref):
        cs, sn = c_ref[...], s_ref[...]
        nope = _lane() < MNOPE
        qn, _ = _norm96(q_ref[...], None)
        qn = qn * gq_ref[...]
        qo_ref[...] = (qn * cs + _swap_halves(qn) * sn).astype(BF16)
        kvv = kv_ref[...]
        kn, _ = _norm96(jnp.where(nope, kvv, kpe_ref[...]), None)
        kn = kn * gk_ref[...]
        ko_ref[...] = (kn * cs + _swap_halves(kn) * sn).astype(BF16)
        vo_ref[...] = jnp.where(nope, pltpu.roll(kvv, MNOPE, 1), 0.0).astype(BF16)

    hd = pl.BlockSpec((ts, LANE), lambda i, h: (i, h))
    shared = lambda col: pl.BlockSpec((ts, LANE), lambda i, h: (i, col))
    gain = pl.BlockSpec((1, LANE), lambda i, h: (0, 0))
    out = jax.ShapeDtypeStruct((t, MH * LANE), BF16)
    return pl.pallas_call(
        body, name=name, grid=(t // ts, MH),
        in_specs=[hd, hd, shared(OFF_KPE // LANE), shared(0), shared(0), gain, gain],
        out_specs=[hd, hd, hd], out_shape=[out, out, out], compiler_params=_params(2),
    )(q_raw, kv, proj, cos_t, sin_t, gq, gk)


def _qk_prep_bwd(dq, dk, dv, q_raw, kv, proj, cos_t, sin_t, gq, gk, *, name, ts=2048):
    t = q_raw.shape[0]
    ts = min(ts, t)

    def norm_bwd(dy, x, g, dg_ref):
        xh, r = _norm96(x, None)
        dg_ref[...] += jnp.sum(dy * xh, axis=0, keepdims=True)
        dxh = dy * g
        return r * (dxh - xh * (jnp.sum(dxh * xh, axis=-1, keepdims=True) * (1.0 / MQK)))

    def body(dq_ref, dk_ref, dv_ref, q_ref, kv_ref, kpe_ref, c_ref, s_ref, gq_ref, gk_ref,
             dqr_ref, dkv_ref, dkpe_ref, dgq_ref, dgk_ref):
        i, h = pl.program_id(0), pl.program_id(1)

        @pl.when(h == 0)
        def _():
            dkpe_ref[...] = jnp.zeros_like(dkpe_ref)

        @pl.when((h == 0) & (i == 0))
        def _():
            dgq_ref[...] = jnp.zeros_like(dgq_ref)
            dgk_ref[...] = jnp.zeros_like(dgk_ref)

        cs, sn = c_ref[...], s_ref[...]
        lane = _lane()
        nope = lane < MNOPE
        dqv = dq_ref[...]
        dqn = dqv * cs + _swap_halves(dqv * sn)
        dqr_ref[...] = norm_bwd(dqn, q_ref[...], gq_ref[...], dgq_ref).astype(BF16)
        dkv_ = dk_ref[...]
        dkn = dkv_ * cs + _swap_halves(dkv_ * sn)
        kvv = kv_ref[...]
        dkr = norm_bwd(dkn, jnp.where(nope, kvv, kpe_ref[...]), gk_ref[...], dgk_ref)
        dkv_ref[...] = jnp.where(nope, dkr, pltpu.roll(dv_ref[...], MNOPE, 1)).astype(BF16)
        dkpe_ref[...] += jnp.where((lane >= MNOPE) & (lane < MQK), dkr, 0.0)

    hd = pl.BlockSpec((ts, LANE), lambda i, h: (i, h))
    shared = lambda col: pl.BlockSpec((ts, LANE), lambda i, h: (i, col))
    gain = pl.BlockSpec((1, LANE), lambda i, h: (0, 0))
    out = jax.ShapeDtypeStruct((t, MH * LANE), BF16)
    return pl.pallas_call(
        body, name=name, grid=(t // ts, MH),
        in_specs=[hd, hd, hd, hd, hd, shared(OFF_KPE // LANE), shared(0), shared(0), gain, gain],
        out_specs=[hd, hd, shared(0), gain, gain],
        out_shape=[out, out, jax.ShapeDtypeStruct((t, LANE), F32), jax.ShapeDtypeStruct((1, LANE), F32),
                   jax.ShapeDtypeStruct((1, LANE), F32)],
        compiler_params=_params(2),
    )(dq, dk, dv, q_raw, kv, proj, cos_t, sin_t, gq, gk)


_NT = (((1,), (1,)), ((), ()))
_TN = (((0,), (0,)), ((), ()))


SOFTMAX_SCALE = MQK ** -0.5
Q_PRESCALE = SOFTMAX_SCALE * float(np.log2(np.e))


def _attn_weights(q, k_ref, lo, tq):
    row = lax.broadcasted_iota(jnp.int32, (tq, tq), 0) // CHUNK
    col = lax.broadcasted_iota(jnp.int32, (tq, tq), 1) // CHUNK
    sd = lax.dot_general(q, k_ref[pl.ds(lo, tq), :], _NT, preferred_element_type=F32)
    sd = jnp.where(col <= row, sd, -1e30)
    m = jnp.max(sd, axis=-1, keepdims=True)
    if lo:
        so = lax.dot_general(q, k_ref[pl.ds(0, lo), :], _NT, preferred_element_type=F32)
        m = jnp.maximum(m, jnp.max(so, axis=-1, keepdims=True))
        eo = jnp.exp2(so - m)
        ed = jnp.exp2(sd - m)
        return eo, ed, 1.0 / (jnp.sum(eo, axis=-1, keepdims=True) + jnp.sum(ed, axis=-1, keepdims=True))
    ed = jnp.exp2(sd - m)
    return None, ed, 1.0 / jnp.sum(ed, axis=-1, keepdims=True)


def _attn_fwd(q, k, v, *, name, tq=256):
    bsz, s, _ = q.shape
    tq = min(tq, s)

    def body(q_ref, k_ref, v_ref, o_ref):
        for i in range(s // tq):
            lo = i * tq
            eo, ed, inv = _attn_weights(q_ref[pl.ds(lo, tq), :], k_ref, lo, tq)
            o = jnp.dot(ed.astype(BF16), v_ref[pl.ds(lo, tq), :], preferred_element_type=F32)
            if lo:
                o += jnp.dot(eo.astype(BF16), v_ref[pl.ds(0, lo), :], preferred_element_type=F32)
            o_ref[pl.ds(lo, tq), :] = (o * inv).astype(BF16)

    spec = pl.BlockSpec((None, s, LANE), lambda b, h: (b, 0, h))
    return pl.pallas_call(
        body, name=name, grid=(bsz, MH), in_specs=[spec, spec, spec], out_specs=spec,
        out_shape=jax.ShapeDtypeStruct((bsz, s, MH * LANE), BF16), compiler_params=_params(2),
    )(q, k, v)


def _attn_bwd(q, k, v, do, *, name, tq=256):
    bsz, s, _ = q.shape
    tq = min(tq, s)

    def body(q_ref, k_ref, v_ref, do_ref, dq_ref, dk_ref, dv_ref):
        dk_ref[...] = jnp.zeros_like(dk_ref)
        dv_ref[...] = jnp.zeros_like(dv_ref)
        for i in range(s // tq):
            lo = i * tq
            here, before = pl.ds(lo, tq), pl.ds(0, lo)
            qv, dov = q_ref[here, :], do_ref[here, :]
            eo, ed, inv = _attn_weights(qv, k_ref, lo, tq)
            do_n = (dov.astype(F32) * inv).astype(BF16)
            dv_ref[here, :] += lax.dot_general(ed.astype(BF16), do_n, _TN, preferred_element_type=F32)
            dpd = lax.dot_general(dov, v_ref[here, :], _NT, preferred_element_type=F32)
            delta = jnp.sum(dpd * ed, axis=-1, keepdims=True)
            if lo:
                dv_ref[before, :] += lax.dot_general(eo.astype(BF16), do_n, _TN, preferred_element_type=F32)
                dpo = lax.dot_general(dov, v_ref[before, :], _NT, preferred_element_type=F32)
                delta += jnp.sum(dpo * eo, axis=-1, keepdims=True)
            delta = delta * inv
            r = inv * SOFTMAX_SCALE
            dsd = (ed * (dpd - delta) * r).astype(BF16)
            dq = jnp.dot(dsd, k_ref[here, :], preferred_element_type=F32)
            dk_ref[here, :] += lax.dot_general(dsd, qv, _TN, preferred_element_type=F32)
            if lo:
                dso = (eo * (dpo - delta) * r).astype(BF16)
                dq += jnp.dot(dso, k_ref[before, :], preferred_element_type=F32)
                dk_ref[before, :] += lax.dot_general(dso, qv, _TN, preferred_element_type=F32)
            dq_ref[here, :] = dq
        dk_ref[...] = dk_ref[...] * (1.0 / Q_PRESCALE)

    spec = pl.BlockSpec((None, s, LANE), lambda b, h: (b, 0, h))
    out = jax.ShapeDtypeStruct((bsz, s, MH * LANE), F32)
    return pl.pallas_call(
        body, name=name, grid=(bsz, MH), in_specs=[spec] * 4, out_specs=[spec] * 3, out_shape=[out, out, out],
        compiler_params=_params(2),
    )(q, k, v, do)


def _adamw(w, g, m, v, *, name, tr=256, by_cols=False):
    rows, cols = w.shape
    tr = _tile_rows(rows, tr)

    def body(w_ref, g_ref, m_ref, v_ref, d_ref, nm_ref, nv_ref):
        d_ref[...], nm_ref[...], nv_ref[...] = _adamw_update(w_ref[...], g_ref[...], m_ref[...], v_ref[...])

    spec = pl.BlockSpec((rows, LANE), lambda i: (0, i)) if by_cols else pl.BlockSpec((tr, cols), lambda i: (i, 0))
    out = jax.ShapeDtypeStruct((rows, cols), F32)
    return pl.pallas_call(body, name=name, grid=(cols // LANE if by_cols else rows // tr,), in_specs=[spec] * 4,
                          out_specs=[spec] * 3, out_shape=[out, out, out], compiler_params=_params(1))(w, g, m, v)


def _tile_rows(rows, target):
    if rows <= target:
        return rows
    best = 8
    for t in range(8, target + 1, 8):
        if rows % t == 0:
            best = t
    return best


def _adamw_update(w, g, m, v):
    nm = ADAM_B1 * m + (1.0 - ADAM_B1) * g
    nv = ADAM_B2 * v + (1.0 - ADAM_B2) * (g * g)
    m_hat = nm / (1.0 - ADAM_B1 ** ADAM_STEP)
    v_hat = nv / (1.0 - ADAM_B2 ** ADAM_STEP)
    return -ADAM_LR * (m_hat / (jnp.sqrt(v_hat) + ADAM_EPS) + ADAM_WD * w), nm, nv


def _adamw_halves(w, m, v, mine, theirs, sel, *, name, tr=256):
    rows, cols = w.shape
    tr = _tile_rows(rows // 2, tr)
    nh = rows // 2 // tr

    def body(sel_ref, w_ref, m_ref, v_ref, mine_ref, theirs_ref, g_ref, d_ref, nm_ref, nv_ref):
        lower = pl.program_id(0) < nh
        south = sel_ref[0] == 0
        gv = jnp.where(lower == south, mine_ref[...], theirs_ref[...])
        g_ref[...] = gv
        d_ref[...], nm_ref[...], nv_ref[...] = _adamw_update(w_ref[...], gv, m_ref[...], v_ref[...])

    full = pl.BlockSpec((tr, cols), lambda i, sel_ref: (i, 0))
    half = pl.BlockSpec((tr, cols), lambda i, sel_ref: (i % nh, 0))
    out = jax.ShapeDtypeStruct((rows, cols), F32)
    return pl.pallas_call(
        body, name=name, out_shape=[out] * 4, compiler_params=_params(1),
        grid_spec=pltpu.PrefetchScalarGridSpec(num_scalar_prefetch=1, grid=(rows // tr,),
                                               in_specs=[full, full, full, half, half], out_specs=[full] * 4),
    )(sel, w, m, v, mine, theirs)


def _pair_add(x, sib, sel, *, name, tr=256):
    n, _, rows, cols = x.shape
    tr = _tile_rows(rows, tr)

    def body(sel_ref, x_ref, s_ref, o_ref):
        o_ref[...] = (x_ref[...] + s_ref[...]).astype(BF16)

    spec = pl.BlockSpec((None, tr, cols), lambda j, i, sel_ref: (j, i, 0))
    return pl.pallas_call(
        body, name=name, out_shape=jax.ShapeDtypeStruct((n, rows, cols), BF16), compiler_params=_params(2),
        grid_spec=pltpu.PrefetchScalarGridSpec(
            num_scalar_prefetch=1, grid=(n, rows // tr),
            in_specs=[pl.BlockSpec((None, None, tr, cols), lambda j, i, sel_ref: (j, sel_ref[0], i, 0)), spec],
            out_specs=spec),
    )(sel, x, sib)


def _chip_sum(pair, recv, sel, *, name, tr=256):
    _, rows, cols = pair.shape
    tr = _tile_rows(rows, tr)

    def body(sel_ref, p_ref, r_ref, o_ref):
        acc = p_ref[...].astype(F32)
        for k in range(3):
            acc = acc + r_ref[k].astype(F32)
        o_ref[...] = acc

    return pl.pallas_call(
        body, name=name, out_shape=jax.ShapeDtypeStruct((rows, cols), F32), compiler_params=_params(1),
        grid_spec=pltpu.PrefetchScalarGridSpec(
            num_scalar_prefetch=1, grid=(rows // tr,),
            in_specs=[pl.BlockSpec((None, tr, cols), lambda i, sel_ref: (sel_ref[0], i, 0)),
                      pl.BlockSpec((3, tr, cols), lambda i, sel_ref: (0, i, 0))],
            out_specs=pl.BlockSpec((tr, cols), lambda i, sel_ref: (i, 0))),
    )(sel, pair, recv)


def _me():
    return lax.axis_index("x"), lax.axis_index("y"), lax.axis_index("c")


def _flip(pos, bits):
    x, y, c = pos
    return (x ^ bits[0] if bits[0] else x, y ^ bits[1] if bits[1] else y, c ^ bits[2] if bits[2] else c)


ANY = pl.BlockSpec(memory_space=pl.ANY)


def _all_gather8(xs, *, name):
    n = len(xs)
    flips = [((k >> 2) & 1, (k >> 1) & 1, k & 1) for k in range(1, 8)]

    def body(*refs):
        x_refs, out_refs, (send_sems, recv_sems, local_sems) = refs[:n], refs[n:2 * n], refs[2 * n:]
        me = _me()
        slot = lambda p: 4 * p[0] + 2 * p[1] + p[2]
        copies = []
        for i in range(n):
            mine = pltpu.make_async_copy(x_refs[i], out_refs[i].at[slot(me)], local_sems.at[i])
            mine.start()
            copies.append(mine)
            for k, f in enumerate(flips):
                peer = _flip(me, f)
                sems = dict(send_sem=send_sems.at[7 * i + k], recv_sem=recv_sems.at[7 * i + k], device_id=peer,
                            device_id_type=MESH)
                cp = pltpu.make_async_remote_copy(src_ref=x_refs[i], dst_ref=out_refs[i].at[slot(me)], **sems)
                cp.start()
                copies.append(cp)
                copies.append(pltpu.make_async_remote_copy(src_ref=x_refs[i], dst_ref=out_refs[i].at[slot(peer)], **sems))
        for i in range(n):
            base = i * 15
            copies[base].wait()
            for k in range(7):
                copies[base + 1 + 2 * k].wait_send()
                copies[base + 2 + 2 * k].wait_recv()

    outs = pl.pallas_call(
        body, name=name, in_specs=[ANY] * n, out_specs=[ANY] * n,
        out_shape=[jax.ShapeDtypeStruct((8, *x.shape), x.dtype) for x in xs],
        scratch_shapes=[pltpu.SemaphoreType.DMA((7 * n,)), pltpu.SemaphoreType.DMA((7 * n,)),
                        pltpu.SemaphoreType.DMA((n,))])(*xs)
    return list(outs)


CHIP_FLIPS = [(1, 0, 0), (0, 1, 0), (1, 1, 0)]


def _chip():
    return 2 * lax.axis_index("x") + lax.axis_index("y")


def _pair_swap_halves(xs, *, name):
    n = len(xs)

    def body(*refs):
        x_refs, out_refs, (send_sems, recv_sems) = refs[:n], refs[n:2 * n], refs[2 * n:]
        me = _me()
        sib = _flip(me, (0, 0, 1))
        copies = [pltpu.make_async_remote_copy(src_ref=x_refs[i].at[:, 1 - me[2]], dst_ref=out_refs[i],
                                               send_sem=send_sems.at[i], recv_sem=recv_sems.at[i], device_id=sib,
                                               device_id_type=MESH) for i in range(n)]
        for cp in copies:
            cp.start()
        for cp in copies:
            cp.wait()

    return pl.pallas_call(
        body, name=name, in_specs=[ANY] * n, out_specs=[ANY] * n,
        out_shape=[jax.ShapeDtypeStruct((x.shape[0], *x.shape[2:]), x.dtype) for x in xs],
        scratch_shapes=[pltpu.SemaphoreType.DMA((n,)), pltpu.SemaphoreType.DMA((n,))])(*xs)


def _pair_swap(hs, *, name):
    n = len(hs)

    def body(*refs):
        h_refs, out_refs, (send_sems, recv_sems) = refs[:n], refs[n:2 * n], refs[2 * n:]
        sib = _flip(_me(), (0, 0, 1))
        copies = [pltpu.make_async_remote_copy(src_ref=h_refs[i], dst_ref=out_refs[i], send_sem=send_sems.at[i],
                                               recv_sem=recv_sems.at[i], device_id=sib, device_id_type=MESH)
                  for i in range(n)]
        for cp in copies:
            cp.start()
        for cp in copies:
            cp.wait()

    return pl.pallas_call(
        body, name=name, in_specs=[ANY] * n, out_specs=[ANY] * n,
        out_shape=[jax.ShapeDtypeStruct(h.shape, h.dtype) for h in hs],
        scratch_shapes=[pltpu.SemaphoreType.DMA((n,)), pltpu.SemaphoreType.DMA((n,))])(*hs)


HBM = pl.BlockSpec(memory_space=pltpu.HBM)
SEM = pl.BlockSpec(memory_space=pltpu.SEMAPHORE)
EFFECT = pltpu.SideEffectType.DATAFLOW_SIDE_EFFECTING


def _plan_copies(plan, refs, send_sems, recv_sems):
    return [pltpu.make_async_remote_copy(src_ref=src, dst_ref=dst, send_sem=send_sems.at[k], recv_sem=recv_sems.at[k],
                                         device_id=to, device_id_type=MESH) for k, (src, dst, to) in enumerate(plan(refs))]


def _rdma_start(arrays, n_copies, plan, deps, *, name):
    n, nd = len(arrays), len(deps)

    def body(*refs):
        for cp in _plan_copies(plan, refs[:n], refs[n + nd], refs[n + nd + 1]):
            cp.start()
        refs[-1][...] = jnp.zeros_like(refs[-1])

    outs = pl.pallas_call(
        body, name=name,
        out_shape=(pltpu.SemaphoreType.DMA((n_copies,)), pltpu.SemaphoreType.DMA((n_copies,)),
                   *[pltpu.HBM(a.shape, a.dtype) for a in arrays], jax.ShapeDtypeStruct((8, LANE), F32)),
        in_specs=[HBM] * n + [ANY] * nd, out_specs=(SEM, SEM, *[HBM] * n, pl.BlockSpec(memory_space=pltpu.VMEM)),
        input_output_aliases={i: i + 2 for i in range(n)}, compiler_params=pltpu.CompilerParams(has_side_effects=EFFECT),
    )(*[pltpu.with_memory_space_constraint(a, pltpu.HBM) for a in arrays], *deps)
    return outs[0], outs[1], list(outs[2:2 + n]), outs[-1]


def _rdma_wait(send_sems, recv_sems, arrays, plan, after, *, name):
    n = len(arrays)

    def body(*refs):
        for cp in _plan_copies(plan, refs[:n], refs[n], refs[n + 1]):
            cp.wait_send()
            cp.wait_recv()

    return list(pl.pallas_call(
        body, name=name, out_shape=tuple(pltpu.HBM(a.shape, a.dtype) for a in arrays),
        in_specs=[HBM] * n + [SEM, SEM, ANY], out_specs=tuple([HBM] * n), input_output_aliases={i: i for i in range(n)},
        compiler_params=pltpu.CompilerParams(has_side_effects=EFFECT),
    )(*arrays, send_sems, recv_sems, after))


def _gather_plan(n):
    def plan(refs):
        me = _me()
        slot = 2 * me[0] + me[1]
        return [(refs[i].at[me[2]], refs[n + i].at[slot, me[2]], _flip(me, f)) for i in range(n) for f in CHIP_FLIPS]
    return plan


def _scatter_plan(n):
    def plan(refs):
        me = _me()
        out = []
        for i in range(n):
            for k, f in enumerate(CHIP_FLIPS):
                peer = _flip(me, f)
                out.append((refs[i].at[2 * peer[0] + peer[1]], refs[n + i].at[k], peer))
        return out
    return plan


def _pair_fill(lands, *, name):
    n = len(lands)

    def body(*refs):
        in_refs, (send_sems, recv_sems) = refs[:n], refs[2 * n:]
        me = _me()
        sib = _flip(me, (0, 0, 1))
        copies = []
        for i in range(n):
            for k, f in enumerate(CHIP_FLIPS):
                peer = _flip(me, f)
                slot = 2 * peer[0] + peer[1]
                mine, theirs = in_refs[i].at[slot, me[2]], in_refs[i].at[slot, 1 - me[2]]
                cp = pltpu.make_async_remote_copy(src_ref=mine, dst_ref=mine, send_sem=send_sems.at[3 * i + k],
                                                  recv_sem=recv_sems.at[3 * i + k], device_id=sib, device_id_type=MESH)
                cp.start()
                copies.append((cp, pltpu.make_async_remote_copy(
                    src_ref=mine, dst_ref=theirs, send_sem=send_sems.at[3 * i + k], recv_sem=recv_sems.at[3 * i + k],
                    device_id=sib, device_id_type=MESH)))
        for cp, arrival in copies:
            arrival.wait_recv()
            cp.wait_send()

    return list(pl.pallas_call(
        body, name=name, in_specs=[ANY] * n, out_specs=[ANY] * n,
        out_shape=[jax.ShapeDtypeStruct(a.shape, a.dtype) for a in lands], input_output_aliases={i: i for i in range(n)},
        scratch_shapes=[pltpu.SemaphoreType.DMA((3 * n,)), pltpu.SemaphoreType.DMA((3 * n,))])(*lands))


def _own_and_landed(lands, xs):
    chip = _chip()
    return [[jnp.where(chip == j, x, o.reshape(4, *x.shape)[j]) for j in range(4)] for o, x in zip(lands, xs)]


BIG = (("w_in", (D, IN_WIDTH // 4), 1), ("gla_w_o", (D // 4, D), 0), ("mla_w_uq", (MQR, MH * MQK // 4), 1),
       ("mla_w_ukv", (MKVR, MH * (MNOPE + MVD) // 4), 1), ("mla_w_o", (D // 4, D), 0), ("w_out", (D // 4, D), 0),
       ("mlp_w1", (D, DFF // 4), 1), ("mlp_w2", (DFF // 4, D), 0))
ADA_SHARD = (D, 6 * D // 4)
SMALL = (("b_ada", 6 * D), ("norm1_g", D), ("b_merge", 2 * D), ("gla_b_alpha", GH * GDK), ("gla_out_norm_g", GDV),
         ("mla_q_lat_g", MQR), ("mla_kv_lat_g", MKVR), ("mla_qn_g", MQK), ("mla_kn_g", MQK), ("norm2_g", D))


W_IN_SEGMENTS = ((0, 3072, OFF_Q), (3072, 3088, OFF_A), (3088, 3344, OFF_CQ), (3344, 3472, OFF_CKV),
                 (3472, 3504, OFF_KPE + MNOPE), (3504, 5552, OFF_MA))
SMALL_ROWS, SMALL_COLS = 32, 2 * D
W_ALPHA_ROW = 16
SMALL_RED = tuple((n, k) for n, k in SMALL if n != "b_ada")


def _pack_small(grads, d_w_alpha, *, name):
    def body(*refs):
        g_refs, wa_ref, out_ref = refs[:-2], refs[-2], refs[-1]
        out_ref[...] = jnp.zeros_like(out_ref)
        for i, ((_, k), g_ref) in enumerate(zip(SMALL_RED, g_refs)):
            out_ref[i:i + 1, 0:k] = g_ref[...]
        out_ref[W_ALPHA_ROW:W_ALPHA_ROW + GLR, 0:GH * GDK] = wa_ref[...]

    return pl.pallas_call(body, name=name, out_shape=jax.ShapeDtypeStruct((SMALL_ROWS, SMALL_COLS), F32))(*grads, d_w_alpha)


def _small_update(gathered, dmod_all, sel, wmv, *, name):
    names = [n for n, _ in SMALL] + ["gla_w_alpha"]
    n_par = len(names)

    def body(sel_ref, g_ref, dmod_ref, *refs):
        in_refs, out_refs, acc = refs[:3 * n_par], refs[3 * n_par:-1], refs[-1]
        total = g_ref[0]
        for j in range(1, 8):
            total = total + g_ref[j]
        acc[...] = total
        row = {n: i for i, (n, _) in enumerate(SMALL_RED)}
        for p, name_p in enumerate(names):
            w_ref, m_ref, v_ref = in_refs[3 * p:3 * p + 3]
            if name_p == "b_ada":
                gv = jnp.sum(dmod_ref[...], axis=0, keepdims=True)
            elif name_p == "gla_w_alpha":
                gv = jnp.zeros((GLR, GDK), F32)
                for j in range(4):
                    blk = acc[W_ALPHA_ROW:W_ALPHA_ROW + GLR, j * GDK:(j + 1) * GDK]
                    gv = gv + jnp.where(sel_ref[0] == j, blk, 0.0)
            else:
                gv = acc[row[name_p]:row[name_p] + 1, 0:w_ref.shape[1]]
            o = out_refs[4 * p:4 * p + 4]
            o[0][...] = gv
            o[1][...], o[2][...], o[3][...] = _adamw_update(w_ref[...], gv, m_ref[...], v_ref[...])

    flat = [a for t in wmv for a in t]
    out_shape = [jax.ShapeDtypeStruct(t[0].shape, F32) for t in wmv for _ in range(4)]
    vmem = pl.BlockSpec(memory_space=pltpu.VMEM)
    outs = pl.pallas_call(
        body, name=name, out_shape=out_shape, in_specs=[pl.BlockSpec(memory_space=pltpu.SMEM), vmem, vmem] + [vmem] * len(flat),
        out_specs=[vmem] * len(out_shape), scratch_shapes=[pltpu.VMEM((SMALL_ROWS, SMALL_COLS), F32)],
    )(sel, gathered, dmod_all, *flat)
    return {n: tuple(outs[4 * p:4 * p + 4]) for p, n in enumerate(names)}


def _full_weights(gathered):
    w = {name: jnp.concatenate(gathered[name], axis=axis) for name, _, axis in BIG if name in gathered and name != "w_in"}
    if "w_in" in gathered:
        shards = gathered["w_in"]
        zeros = lambda n: [jnp.zeros((D, n), shards[0].dtype)]

        def cols(a, b):
            width = IN_WIDTH // 4
            return [shards[j][:, max(a, j * width) - j * width:min(b, (j + 1) * width) - j * width]
                    for j in range(4) if max(a, j * width) < min(b, (j + 1) * width)]

        parts = []
        for a, b, at in sorted(W_IN_SEGMENTS, key=lambda seg: seg[2]):
            have = sum(p.shape[1] for p in parts)
            parts += (zeros(at - have) if at > have else []) + cols(a, b)
        w["w_in"] = jnp.concatenate(parts + zeros(PW - sum(p.shape[1] for p in parts)), axis=1)
    if "mla_w_uq" in w:
        w["mla_w_uq"] = jnp.pad(w["mla_w_uq"].reshape(MQR, MH, MQK), ((0, 0), (0, 0), (0, LANE - MQK))).reshape(MQR, MH * LANE)
    if "mla_w_o" in w:
        w["mla_w_o"] = jnp.pad(w["mla_w_o"].reshape(MH, MVD, D), ((0, 0), (0, LANE - MVD), (0, 0))).reshape(MH * LANE, D)
    return w


def _grad_slots(g):
    g = dict(g)
    out = {}
    if "w_in" in g:
        gi = g.pop("w_in")
        width = IN_WIDTH // 4
        slots = []
        for j in range(4):
            lo, hi = j * width, (j + 1) * width
            slots.append(jnp.concatenate([gi[:, at + max(lo, a) - a:at + min(hi, b) - a]
                                          for a, b, at in W_IN_SEGMENTS if max(lo, a) < min(hi, b)], axis=1))
        out["w_in"] = jnp.stack(slots).reshape(4, 2, D // 2, width)
    if "mla_w_uq" in g:
        g["mla_w_uq"] = g["mla_w_uq"].reshape(MQR, MH, LANE)[:, :, :MQK].reshape(MQR, MH * MQK)
    if "mla_w_o" in g:
        g["mla_w_o"] = g["mla_w_o"].reshape(MH, LANE, D)[:, :MVD].reshape(MH * MVD, D)
    for name, (rows, cols), axis in BIG:
        if name not in g:
            continue
        a = g[name]
        a = a.reshape(4, rows, cols) if axis == 0 else jnp.transpose(a.reshape(rows, 4, cols), (1, 0, 2))
        out[name] = a.reshape(4, 2, rows // 2, cols)
    return out


def _rope_tables(positions):
    freqs = ROPE_THETA ** (-jnp.arange(0, MROPE, 2, dtype=F32) / MROPE)
    lane = np.arange(LANE)
    in_rope = (lane >= MNOPE) & (lane < MQK)
    freq_lane = jnp.where(in_rope, freqs[(lane - MNOPE) % (MROPE // 2)], 0.0)
    sign = np.where(in_rope, np.where(lane < MNOPE + MROPE // 2, -1.0, 1.0), 0.0).astype(np.float32)
    ang = positions.astype(F32).reshape(-1, 1) * freq_lane[None, :]
    return jnp.cos(ang), jnp.sin(ang) * sign[None, :]


def _local_step(x, positions, mod, target, w, small, more_weights=None, on_grads=None):
    kept = {}
    if on_grads is None:
        on_grads = lambda tag, grads: kept.update(grads)
    bsz, s, _ = x.shape
    t = bsz * s
    tt = _tile(t, 1024)
    shift1, scale1, gate1, shift2, scale2, gate2 = [mod[:, None, i * D:(i + 1) * D] for i in range(6)]
    cos_t, sin_t = _rope_tables(positions)
    w_alpha_p = jnp.pad(small["gla_w_alpha"], ((0, LANE - GLR), (0, 0)))
    gq = jnp.pad(small["mla_qn_g"], ((0, 0), (0, LANE - MQK)))
    gk = jnp.pad(small["mla_kn_g"], ((0, 0), (0, LANE - MQK)))
    flat2 = lambda a: a.reshape(t, a.shape[-1])
    bsd = lambda a: a.reshape(bsz, s, a.shape[-1])

    h = _norm_mod(x, small["norm1_g"], scale1, shift1, name="norm1")
    if callable(w):
        w = w(h)
    proj = _mm(flat2(h), w["w_in"], name="proj", tn=1152)
    proj3 = bsd(proj)
    o, o_gated, states = _gla_fwd(proj3, w_alpha_p, small["gla_b_alpha"], small["gla_out_norm_g"], name="gla_fwd")
    if more_weights is not None:
        w = {**w, **more_weights(o_gated)}
    y_a = _mm(flat2(o_gated), w["gla_w_o"], name="gla_out")
    cq_n, ckv_n = _lat_norm(proj, small["mla_q_lat_g"], small["mla_kv_lat_g"], name="lat_norm")
    q_raw = _mm(cq_n, w["mla_w_uq"], name="mla_uq")
    kv = _mm(ckv_n, w["mla_w_ukv"], name="mla_ukv")
    qf, kf, vf = _qk_prep(q_raw, kv, proj, cos_t, sin_t, gq * Q_PRESCALE, gk, name="qk_prep")
    o_attn = _attn_fwd(bsd(qf), bsd(kf), bsd(vf), name="attn_fwd")
    y_b = _mm(flat2(o_attn), w["mla_w_o"], name="mla_out")
    mixed_in = _merge_fwd(proj3, small["b_merge"], bsd(y_a), bsd(y_b), name="merge_fwd")
    mixed = _mm(flat2(mixed_in), w["w_out"], name="w_out")
    x1, h2 = _resid_norm_mod(x, bsd(mixed), gate1, small["norm2_g"], scale2, shift2, name="norm2")

    def sqrelu(acc, ex, outs):
        outs[0][...] = acc
        r = jnp.maximum(acc, 0.0)
        outs[1][...] = (r * r).astype(BF16)

    a1, r = _mm(flat2(h2), w["mlp_w1"], name="mlp1", epilogue=sqrelu,
                out_shape=[jax.ShapeDtypeStruct((t, DFF), F32), jax.ShapeDtypeStruct((t, DFF), BF16)],
                out_specs=[_tile_spec(tt, 1024), _tile_spec(tt, 1024)])
    ff = _mm(r, w["mlp_w2"], name="mlp2")
    dy, dff, dgate2, loss_part = _loss_head(x1, bsd(ff), gate2, target, name="loss_head")

    g = {}

    def relu2_bwd(acc, ex, outs):
        outs[0][...] = (acc * (2.0 * jnp.maximum(ex[0][...], 0.0))).astype(BF16)

    dff2 = flat2(dff)
    da1 = _mm(dff2, w["mlp_w2"], tb=True, name="mlp2_dx", epilogue=relu2_bwd, extras=(a1,),
              extra_specs=(_tile_spec(tt, 1024),), out_shape=jax.ShapeDtypeStruct((t, DFF), BF16),
              out_specs=_tile_spec(tt, 1024))
    g["mlp_w2"] = _mm(r, dff2, ta=True, name="mlp2_dw")
    dh2 = _mm(da1, w["mlp_w1"], tb=True, name="mlp1_dx")
    g["mlp_w1"] = _mm(flat2(h2), da1, ta=True, name="mlp1_dw")
    token = on_grads("mlp", {n: g.pop(n) for n in ("mlp_w2", "mlp_w1")})
    if token is not None:
        gate1 = gate1 + token[0, 0]
    dx1, dscale2, dshift2, dg2, dgate1, dmixed = _norm_mod_bwd(
        bsd(dh2), x1, dy, small["norm2_g"], scale2, gate1, bsd(mixed), name="norm2_bwd")
    dmixed2 = flat2(dmixed)
    dmi = _mm(dmixed2, w["w_out"], tb=True, name="w_out_dx")
    g["w_out"] = _mm(flat2(mixed_in), dmixed2, ta=True, name="w_out_dw")
    dy_a, dy_b, dl_a, dl_b, db_a, db_b = _merge_bwd(bsd(dmi), proj3, small["b_merge"], bsd(y_a), bsd(y_b), name="merge_bwd")
    dy_a2, dy_b2 = flat2(dy_a), flat2(dy_b)
    dog = _mm(dy_a2, w["gla_w_o"], tb=True, name="gla_out_dx")
    g["gla_w_o"] = _mm(flat2(o_gated), dy_a2, ta=True, name="gla_out_dw")
    dq_g, dk_g, dv_g, dg_g, dlog, db_alpha, d_ong = _gla_bwd(
        bsd(dog), o, states, proj3, w_alpha_p, small["gla_b_alpha"], small["gla_out_norm_g"], name="gla_bwd")
    dlog2 = flat2(dlog)
    da_p = _mm(dlog2, w_alpha_p, tb=True, out_dtype=BF16, name="alpha_dx")
    d_w_alpha = _mm(proj[:, OFF_A:OFF_A + LANE], dlog2, ta=True, name="alpha_dw")[:GLR]
    do_attn = _mm(dy_b2, w["mla_w_o"], tb=True, out_dtype=BF16, name="mla_out_dx")
    g["mla_w_o"] = _mm(flat2(o_attn), dy_b2, ta=True, name="mla_out_dw")
    dqf, dkf, dvf = _attn_bwd(bsd(qf), bsd(kf), bsd(vf), bsd(do_attn), name="attn_bwd")
    dq_raw, dkv, dkpe, dgq, dgk = _qk_prep_bwd(flat2(dqf), flat2(dkf), flat2(dvf), q_raw, kv, proj, cos_t, sin_t, gq, gk,
                                                name="qk_prep_bwd")
    dcq_n = _mm(dq_raw, w["mla_w_uq"], tb=True, name="mla_uq_dx")
    g["mla_w_uq"] = _mm(cq_n, dq_raw, ta=True, name="mla_uq_dw")
    dckv_n = _mm(dkv, w["mla_w_ukv"], tb=True, name="mla_ukv_dx")
    g["mla_w_ukv"] = _mm(ckv_n, dkv, ta=True, name="mla_ukv_dw")
    token = on_grads("mix", {n: g.pop(n) for n in ("w_out", "gla_w_o", "mla_w_o", "mla_w_uq", "mla_w_ukv")})
    q_lat_g = small["mla_q_lat_g"] if token is None else small["mla_q_lat_g"] + token[0:1, 0:1]
    dcq, dckv, dg_qlat, dg_kvlat = _lat_norm_bwd(dcq_n, dckv_n, proj, q_lat_g, small["mla_kv_lat_g"],
                                                  name="lat_norm_bwd")
    dproj = jnp.concatenate([flat2(dq_g), flat2(dk_g), flat2(dv_g), flat2(dg_g), flat2(dl_a), flat2(dl_b), dcq, dckv,
                             da_p, dkpe.astype(BF16)], axis=1)
    g["w_in"] = _mm(flat2(h), dproj, ta=True, name="proj_dw", tn=1152)
    token = on_grads("in", {"w_in": g.pop("w_in")})
    after = {} if token is None else dict(extras=(token,), extra_specs=(pl.BlockSpec((8, LANE), lambda i, j, k: (0, 0)),))
    dh = _mm(dproj, w["w_in"], tb=True, name="proj_dx", tk=1152, **after)
    grad_x, dscale1, dshift1, dg1 = _norm_mod_bwd(bsd(dh), x, dx1, small["norm1_g"], scale1, name="norm1_bwd")

    dmod = jnp.concatenate([dshift1, dscale1, dgate1, dshift2, dscale2, dgate2], axis=-1).reshape(bsz, 6 * D)
    gs = {"norm1_g": dg1, "b_merge": jnp.concatenate([db_a, db_b], axis=1), "gla_b_alpha": db_alpha,
          "gla_out_norm_g": d_ong, "mla_q_lat_g": dg_qlat, "mla_kv_lat_g": dg_kvlat, "mla_qn_g": dgq[:, :MQK],
          "mla_kn_g": dgk[:, :MQK], "norm2_g": dg2}
    return loss_part[0, 0], grad_x, dmod, {**kept, **g}, gs, d_w_alpha


def kernel(x, c, positions, w_ada, b_ada, norm1_g, w_in, b_merge, gla_w_alpha, gla_b_alpha, gla_out_norm_g, gla_w_o, mla_q_lat_g, mla_w_uq, mla_kv_lat_g, mla_w_ukv, mla_qn_g, mla_kn_g, mla_w_o, w_out, norm2_g, mlp_w1, mlp_w2, loss_target, m_w_ada, m_b_ada, m_norm1_g, m_w_in, m_b_merge, m_gla_w_alpha, m_gla_b_alpha, m_gla_out_norm_g, m_gla_w_o, m_mla_q_lat_g, m_mla_w_uq, m_mla_kv_lat_g, m_mla_w_ukv, m_mla_qn_g, m_mla_kn_g, m_mla_w_o, m_w_out, m_norm2_g, m_mlp_w1, m_mlp_w2, v_w_ada, v_b_ada, v_norm1_g, v_w_in, v_b_merge, v_gla_w_alpha, v_gla_b_alpha, v_gla_out_norm_g, v_gla_w_o, v_mla_q_lat_g, v_mla_w_uq, v_mla_kv_lat_g, v_mla_w_ukv, v_mla_qn_g, v_mla_kn_g, v_mla_w_o, v_w_out, v_norm2_g, v_mlp_w1, v_mlp_w2):
    args = dict(locals())
    names_big = [n for n, _, _ in BIG]
    names_small = [n for n, _ in SMALL]
    bsz = x.shape[0]
    ax, ay, ac = lax.axis_index("x"), lax.axis_index("y"), lax.axis_index("c")
    chip = 2 * ax + ay
    dev = 2 * chip + ac

    small = {n: args[n] for n in names_small}
    sel_c = jnp.reshape(ac, (1,)).astype(jnp.int32)
    sel_chip = jnp.reshape(chip, (1,)).astype(jnp.int32)
    c_all, w_alpha_all = _all_gather8([c, gla_w_alpha[0]], name="comm_c_alpha")
    small["gla_w_alpha"] = jnp.concatenate([w_alpha_all[2 * j] for j in range(4)], axis=1)
    c_all = c_all.reshape(8 * bsz, D)

    shards = {n: args[n][0].astype(BF16) for n in names_big}
    halves_of = lambda names: [shards[n].reshape(2, shards[n].shape[0] // 2, shards[n].shape[1]) for n in names]

    def gather_start(names, deps, tag):
        xs = halves_of(names)
        lands = [lax.empty((4, *xh.shape), BF16) for xh in xs]
        plan = _gather_plan(len(names))
        return names, plan, _rdma_start(xs + lands, 3 * len(names), plan, deps, name="comm_weights_start_" + tag)

    def gather_finish(started, after, tag):
        names, plan, sems = started
        arrs = _rdma_wait(sems[0], sems[1], sems[2], plan, after, name="comm_weights_wait_" + tag)
        filled = _pair_fill(arrs[len(names):], name="comm_weights_pair_" + tag)
        return _full_weights(dict(zip(names, _own_and_landed(filled, [shards[n] for n in names]))))

    first = gather_start(["w_in"], (c_all,), "in")
    c_all = c_all + first[2][3][0, 0]


    def add_bias(acc, ex, outs):
        outs[0][...] = acc + ex[0][...]

    silu = lambda v: v * _sigmoid(v)
    b_ada_mine = lax.dynamic_slice(b_ada, (0, chip * ADA_SHARD[1]), (1, ADA_SHARD[1]))
    mod_part = _mm(c_all, w_ada[0], name="ada", tn=512, a_fn=silu, epilogue=add_bias, extras=(b_ada_mine,),
                   extra_specs=(pl.BlockSpec((1, 512), lambda i, j, k: (0, j)),),
                   out_shape=jax.ShapeDtypeStruct((8 * bsz, ADA_SHARD[1]), F32), out_specs=_tile_spec(8 * bsz, 512))
    mod_all = _all_gather8([mod_part], name="comm_mod")[0]
    mod_rows = lax.dynamic_slice(mod_all, (0, dev * bsz, 0), (8, bsz, ADA_SHARD[1]))
    mod = jnp.concatenate([mod_rows[2 * j] for j in range(4)], axis=1)
    rest = gather_start([n for n in names_big if n != "w_in"], (mod,), "rest")
    mod = mod + rest[2][3][0, 0]
    w_in_after = lambda after: gather_finish(first, after, "in")
    more_weights = lambda after: gather_finish(rest, after, "rest")

    in_flight = []

    def reduce_start(tag, grads):
        names = list(grads)
        parts = [_grad_slots(grads)[n] for n in names]
        sib_halves = _pair_swap_halves(parts, name="comm_pair_sum_" + tag)
        pairs = [_pair_add(p, s, sel_c, name="pair_add_" + n) for n, p, s in zip(names, parts, sib_halves)]
        recvs = [lax.empty((3, *p.shape[1:]), BF16) for p in pairs]
        plan = _scatter_plan(len(names))
        sems = _rdma_start(pairs + recvs, 3 * len(names), plan, (), name="comm_scatter_start_" + tag)
        in_flight.append((tag, names, plan, sems))
        return sems[3]

    loss_part, grad_x, dmod, g, gs, d_w_alpha = _local_step(x, positions, mod, loss_target, w_in_after, small,
                                                            more_weights, reduce_start)
    loss = lax.psum(loss_part * (0.5 / D), ("x", "y", "c"))

    gs_packed = _pack_small([gs[n] for n, _ in SMALL_RED], d_w_alpha, name="pack_small")
    dmod_all, gs_all = _all_gather8([dmod, gs_packed], name="comm_dmod_small")
    dmod_all = dmod_all.reshape(8 * bsz, 6 * D)
    dmod_mine = lax.dynamic_slice(dmod_all, (0, chip * ADA_SHARD[1]), (8 * bsz, ADA_SHARD[1]))
    g_w_ada = _mm(c_all, dmod_mine, ta=True, a_fn=silu, name="ada_dw")

    wmv = [(args[n], args["m_" + n], args["v_" + n]) for n in names_small]
    wmv.append((gla_w_alpha[0], m_gla_w_alpha[0], v_gla_w_alpha[0]))
    res = _small_update(gs_all, dmod_all, sel_chip, wmv, name="small_update")

    assert not g, list(g)
    half_of = {}
    for tag, names, plan, sems in in_flight:
        arrs = _rdma_wait(sems[0], sems[1], sems[2], plan, grad_x, name="comm_scatter_wait_" + tag)
        for n, p, r in zip(names, arrs[:len(names)], arrs[len(names):]):
            half_of[n] = _chip_sum(p, r, sel_chip, name="chip_sum_" + n)
    halves = [half_of[n] for n in names_big]
    theirs = _pair_swap(halves, name="comm_pair_join")
    for n, mine, other in zip(names_big, halves, theirs):
        if n == "w_in":
            south = ac == 0
            g_t = jnp.concatenate([jnp.where(south, mine, other), jnp.where(south, other, mine)], axis=0).T
            outs = _adamw(w_in[0].T, g_t, m_w_in[0].T, v_w_in[0].T, name="adamw_w_in", by_cols=True)
            res[n] = tuple(a.T for a in (g_t, *outs))
            continue
        res[n] = _adamw_halves(args[n][0], args["m_" + n][0], args["v_" + n][0], mine, other, sel_c, name="adamw_" + n)
    res["w_ada"] = (g_w_ada, *_adamw(w_ada[0], g_w_ada, m_w_ada[0], v_w_ada[0], name="adamw_w_ada"))

    order = ["w_ada", "b_ada", "norm1_g", "w_in", "b_merge", "gla_w_alpha", "gla_b_alpha", "gla_out_norm_g", "gla_w_o",
             "mla_q_lat_g", "mla_w_uq", "mla_kv_lat_g", "mla_w_ukv", "mla_qn_g", "mla_kn_g", "mla_w_o", "w_out",
             "norm2_g", "mlp_w1", "mlp_w2"]
    named = lambda k: [res[n][k].reshape(args[n].shape) for n in order]
    return (loss, grad_x, *named(0), *named(1), *named(2), *named(3))
```

```python
import functools

import jax
import jax.numpy as jnp
import numpy as np
from jax import lax
from jax.experimental import pallas as pl
from jax.experimental.pallas import tpu as pltpu

F32 = jnp.float32
BF16 = jnp.bfloat16
MESH = pl.DeviceIdType.MESH

D = 1024
CHUNK = 64
EPS = 1e-6
GH, GDK, GDV, GLR, GTAU = 4, 128, 256, 16, 16.0
MH, MQR, MKVR, MNOPE, MROPE, MVD = 16, 256, 128, 64, 32, 64
MQK = MNOPE + MROPE
DFF = 4 * D
ROPE_THETA = 10000.0
IN_WIDTH = 5552
LANE = 128
OFF_Q, OFF_K, OFF_V, OFF_G, OFF_MA, OFF_MB, OFF_CQ, OFF_CKV, OFF_A, OFF_KPE, PW = (
    0, 512, 1024, 2048, 3072, 4096, 5120, 5376, 5504, 5632, 5760)
ADAM_LR, ADAM_B1, ADAM_B2, ADAM_EPS, ADAM_WD, ADAM_STEP = 0.001, 0.9, 0.999, 1e-08, 0.01, 10
VMEM_LIMIT = 48 * 1024 * 1024


def _params(n_axes):
    return pltpu.CompilerParams(dimension_semantics=("arbitrary",) * n_axes, vmem_limit_bytes=VMEM_LIMIT)


def _tile(n, target):
    if n <= target:
        return n
    best = None
    for t in range(LANE, target + 1, LANE):
        if n % t == 0:
            best = t
    assert best is not None, (n, target)
    return best


def _sigmoid(x):
    return 1.0 / (1.0 + jnp.exp(-x))


def _mm(a, b, *, name, ta=False, tb=False, out_dtype=F32, tm=1024, tn=1024, tk=1024,
        epilogue=None, extras=(), extra_specs=(), out_shape=None, out_specs=None, a_fn=None):
    if ta:
        kdim, m = a.shape
    else:
        m, kdim = a.shape
    if tb:
        n, k2 = b.shape
    else:
        k2, n = b.shape
    assert kdim == k2, (a.shape, b.shape)
    tm, tn, tk = _tile(m, tm), _tile(n, tn), _tile(kdim, tk)
    nk = kdim // tk
    a_spec = pl.BlockSpec((tk, tm), lambda i, j, k: (k, i)) if ta else pl.BlockSpec((tm, tk), lambda i, j, k: (i, k))
    b_spec = pl.BlockSpec((tn, tk), lambda i, j, k: (j, k)) if tb else pl.BlockSpec((tk, tn), lambda i, j, k: (k, j))
    dims = (((0 if ta else 1,), (1 if tb else 0,)), ((), ()))
    ne = len(extras)
    if out_shape is None:
        out_shape = jax.ShapeDtypeStruct((m, n), out_dtype)
        out_specs = pl.BlockSpec((tm, tn), lambda i, j, k: (i, j))
    n_out = len(out_shape) if isinstance(out_shape, (list, tuple)) else 1
    in_place = epilogue is None and n_out == 1 and out_shape.dtype == F32
    scratch = [] if (nk == 1 or in_place) else [pltpu.VMEM((tm, tn), F32)]

    def body(a_ref, b_ref, *rest):
        ex, outs = rest[:ne], rest[ne:ne + n_out]
        av = a_ref[...] if a_fn is None else a_fn(a_ref[...])
        prod = lax.dot_general(av.astype(BF16), b_ref[...].astype(BF16), dims, preferred_element_type=F32)

        def finish(val):
            if epilogue is None:
                outs[0][...] = val.astype(outs[0].dtype)
            else:
                epilogue(val, ex, outs)

        if nk == 1:
            finish(prod)
            return
        k = pl.program_id(2)
        acc = outs[0] if in_place else rest[-1]

        @pl.when(k == 0)
        def _():
            acc[...] = prod

        @pl.when(k > 0)
        def _():
            acc[...] += prod

        if not in_place:
            @pl.when(k == nk - 1)
            def _():
                finish(acc[...])

    return pl.pallas_call(
        body, name=name, grid=(m // tm, n // tn, nk),
        in_specs=[a_spec, b_spec, *extra_specs], out_specs=out_specs, out_shape=out_shape,
        scratch_shapes=scratch, compiler_params=_params(3),
    )(a, b, *extras)


def _tile_spec(tm, tn):
    return pl.BlockSpec((tm, tn), lambda i, j, k: (i, j))


def _rms(x, g):
    r = lax.rsqrt(jnp.mean(x * x, axis=-1, keepdims=True) + EPS)
    return x * r, r


def _row_spec(ts, width, col=0):
    return pl.BlockSpec((None, ts, width), lambda b, i: (b, i, col))


def _vec_spec(width):
    return pl.BlockSpec((None, 1, width), lambda b, i: (b, 0, 0))


def _gain_spec(width):
    return pl.BlockSpec((1, width), lambda b, i: (0, 0))


def _norm_mod(x, g, scale, shift, *, name, ts=256):
    bsz, s, d = x.shape
    ts = min(ts, s)

    def body(x_ref, g_ref, sc_ref, sh_ref, h_ref):
        xh, _ = _rms(x_ref[...], None)
        h_ref[...] = ((xh * g_ref[...]) * (1.0 + sc_ref[...]) + sh_ref[...]).astype(BF16)

    return pl.pallas_call(
        body, name=name, grid=(bsz, s // ts),
        in_specs=[_row_spec(ts, d), _gain_spec(d), _vec_spec(d), _vec_spec(d)],
        out_specs=_row_spec(ts, d), out_shape=jax.ShapeDtypeStruct((bsz, s, d), BF16),
        compiler_params=_params(2),
    )(x, g, scale, shift)


def _resid_norm_mod(x, mixed, gate, g, scale, shift, *, name, ts=256):
    bsz, s, d = x.shape
    ts = min(ts, s)

    def body(x_ref, mx_ref, gt_ref, g_ref, sc_ref, sh_ref, x1_ref, h_ref):
        x1 = x_ref[...] + gt_ref[...] * mx_ref[...]
        x1_ref[...] = x1
        xh, _ = _rms(x1, None)
        h_ref[...] = ((xh * g_ref[...]) * (1.0 + sc_ref[...]) + sh_ref[...]).astype(BF16)

    return pl.pallas_call(
        body, name=name, grid=(bsz, s // ts),
        in_specs=[_row_spec(ts, d), _row_spec(ts, d), _vec_spec(d), _gain_spec(d), _vec_spec(d), _vec_spec(d)],
        out_specs=[_row_spec(ts, d), _row_spec(ts, d)],
        out_shape=[jax.ShapeDtypeStruct((bsz, s, d), F32), jax.ShapeDtypeStruct((bsz, s, d), BF16)],
        compiler_params=_params(2),
    )(x, mixed, gate, g, scale, shift)


def _norm_mod_bwd(dh, xin, resid, g, scale, gate=None, mixed=None, *, name, ts=256):
    bsz, s, d = xin.shape
    ts = min(ts, s)
    gated = gate is not None

    def body(*refs):
        if gated:
            dh_ref, x_ref, rs_ref, g_ref, sc_ref, gt_ref, mx_ref, dx_ref, dsc_ref, dsh_ref, dg_ref, dgt_ref, dmx_ref = refs
        else:
            dh_ref, x_ref, rs_ref, g_ref, sc_ref, dx_ref, dsc_ref, dsh_ref, dg_ref = refs
        b, i = pl.program_id(0), pl.program_id(1)

        @pl.when(i == 0)
        def _():
            dsc_ref[...] = jnp.zeros_like(dsc_ref)
            dsh_ref[...] = jnp.zeros_like(dsh_ref)
            if gated:
                dgt_ref[...] = jnp.zeros_like(dgt_ref)

        @pl.when((i == 0) & (b == 0))
        def _():
            dg_ref[...] = jnp.zeros_like(dg_ref)

        dh_v, gv = dh_ref[...], g_ref[...]
        xh, r = _rms(x_ref[...], None)
        dsc_ref[...] += jnp.sum(dh_v * (xh * gv), axis=0, keepdims=True)
        dsh_ref[...] += jnp.sum(dh_v, axis=0, keepdims=True)
        dn = dh_v * (1.0 + sc_ref[...])
        dg_ref[...] += jnp.sum(dn * xh, axis=0, keepdims=True)
        dxh = dn * gv
        dx = rs_ref[...] + r * (dxh - xh * jnp.mean(dxh * xh, axis=-1, keepdims=True))
        dx_ref[...] = dx
        if gated:
            dgt_ref[...] += jnp.sum(dx * mx_ref[...], axis=0, keepdims=True)
            dmx_ref[...] = (dx * gt_ref[...]).astype(BF16)

    ins = [dh, xin, resid, g, scale]
    in_specs = [_row_spec(ts, d), _row_spec(ts, d), _row_spec(ts, d), _gain_spec(d), _vec_spec(d)]
    out_specs = [_row_spec(ts, d), _vec_spec(d), _vec_spec(d), _gain_spec(d)]
    out_shape = [jax.ShapeDtypeStruct((bsz, s, d), F32), jax.ShapeDtypeStruct((bsz, 1, d), F32),
                 jax.ShapeDtypeStruct((bsz, 1, d), F32), jax.ShapeDtypeStruct((1, d), F32)]
    if gated:
        ins += [gate, mixed]
        in_specs += [_vec_spec(d), _row_spec(ts, d)]
        out_specs += [_vec_spec(d), _row_spec(ts, d)]
        out_shape += [jax.ShapeDtypeStruct((bsz, 1, d), F32), jax.ShapeDtypeStruct((bsz, s, d), BF16)]
    return pl.pallas_call(
        body, name=name, grid=(bsz, s // ts), in_specs=in_specs, out_specs=out_specs, out_shape=out_shape,
        compiler_params=_params(2),
    )(*ins)


def _loss_head(x1, ff, gate2, target, *, name, ts=256):
    bsz, s, d = x1.shape
    ts = min(ts, s)

    def body(x1_ref, ff_ref, gt_ref, t_ref, dy_ref, dff_ref, dgt_ref, loss_ref, acc):
        b, i = pl.program_id(0), pl.program_id(1)

        @pl.when(i == 0)
        def _():
            dgt_ref[...] = jnp.zeros_like(dgt_ref)

        @pl.when((i == 0) & (b == 0))
        def _():
            acc[...] = jnp.zeros_like(acc)

        ffv, gt = ff_ref[...], gt_ref[...]
        diff = (x1_ref[...] + gt * ffv) - t_ref[...]
        acc[...] += jnp.sum((diff * diff).reshape(ts // 8, 8, d), axis=0)
        dy = diff * (1.0 / d)
        dy_ref[...] = dy
        dgt_ref[...] += jnp.sum(dy * ffv, axis=0, keepdims=True)
        dff_ref[...] = (dy * gt).astype(BF16)

        @pl.when((i == pl.num_programs(1) - 1) & (b == pl.num_programs(0) - 1))
        def _():
            loss_ref[...] = jnp.full(loss_ref.shape, jnp.sum(acc[...]), F32)

    return pl.pallas_call(
        body, name=name, grid=(bsz, s // ts),
        in_specs=[_row_spec(ts, d), _row_spec(ts, d), _vec_spec(d), _row_spec(ts, d)],
        out_specs=[_row_spec(ts, d), _row_spec(ts, d), _vec_spec(d), pl.BlockSpec((8, LANE), lambda b, i: (0, 0))],
        out_shape=[jax.ShapeDtypeStruct((bsz, s, d), F32), jax.ShapeDtypeStruct((bsz, s, d), BF16),
                   jax.ShapeDtypeStruct((bsz, 1, d), F32), jax.ShapeDtypeStruct((8, LANE), F32)],
        scratch_shapes=[pltpu.VMEM((8, d), F32)], compiler_params=_params(2),
    )(x1, ff, gate2, target)


def _merge_fwd(proj, b_merge, y_a, y_b, *, name, ts=256):
    bsz, s, _ = proj.shape
    ts = min(ts, s)

    def body(la_ref, lb_ref, ba_ref, bb_ref, ya_ref, yb_ref, out_ref):
        ga = _sigmoid(la_ref[...] + ba_ref[...])
        gb = _sigmoid(lb_ref[...] + bb_ref[...])
        out_ref[...] = (ga * ya_ref[...] + gb * yb_ref[...]).astype(BF16)

    return pl.pallas_call(
        body, name=name, grid=(bsz, s // ts),
        in_specs=[_row_spec(ts, D, OFF_MA // D), _row_spec(ts, D, OFF_MB // D),
                  pl.BlockSpec((1, D), lambda b, i: (0, 0)), pl.BlockSpec((1, D), lambda b, i: (0, 1)),
                  _row_spec(ts, D), _row_spec(ts, D)],
        out_specs=_row_spec(ts, D), out_shape=jax.ShapeDtypeStruct((bsz, s, D), BF16),
        compiler_params=_params(2),
    )(proj, proj, b_merge, b_merge, y_a, y_b)


def _merge_bwd(dmi, proj, b_merge, y_a, y_b, *, name, ts=256):
    bsz, s, _ = proj.shape
    ts = min(ts, s)

    def body(d_ref, la_ref, lb_ref, ba_ref, bb_ref, ya_ref, yb_ref, dya_ref, dyb_ref, dla_ref, dlb_ref, dba_ref, dbb_ref):
        @pl.when((pl.program_id(0) == 0) & (pl.program_id(1) == 0))
        def _():
            dba_ref[...] = jnp.zeros_like(dba_ref)
            dbb_ref[...] = jnp.zeros_like(dbb_ref)

        dv = d_ref[...]
        ga = _sigmoid(la_ref[...] + ba_ref[...])
        gb = _sigmoid(lb_ref[...] + bb_ref[...])
        dya_ref[...] = (dv * ga).astype(BF16)
        dyb_ref[...] = (dv * gb).astype(BF16)
        dla = (dv * ya_ref[...]) * (ga * (1.0 - ga))
        dlb = (dv * yb_ref[...]) * (gb * (1.0 - gb))
        dla_ref[...] = dla.astype(BF16)
        dlb_ref[...] = dlb.astype(BF16)
        dba_ref[...] += jnp.sum(dla, axis=0, keepdims=True)
        dbb_ref[...] += jnp.sum(dlb, axis=0, keepdims=True)

    act = jax.ShapeDtypeStruct((bsz, s, D), BF16)
    return pl.pallas_call(
        body, name=name, grid=(bsz, s // ts),
        in_specs=[_row_spec(ts, D), _row_spec(ts, D, OFF_MA // D), _row_spec(ts, D, OFF_MB // D),
                  pl.BlockSpec((1, D), lambda b, i: (0, 0)), pl.BlockSpec((1, D), lambda b, i: (0, 1)),
                  _row_spec(ts, D), _row_spec(ts, D)],
        out_specs=[_row_spec(ts, D)] * 4 + [_gain_spec(D)] * 2,
        out_shape=[act, act, act, act, jax.ShapeDtypeStruct((1, D), F32), jax.ShapeDtypeStruct((1, D), F32)],
        compiler_params=_params(2),
    )(dmi, proj, proj, b_merge, b_merge, y_a, y_b)


def _tri(lower):
    r = lax.broadcasted_iota(jnp.int32, (CHUNK, CHUNK), 0)
    c = lax.broadcasted_iota(jnp.int32, (CHUNK, CHUNK), 1)
    return jnp.where((c <= r) if lower else (c >= r), 1.0, 0.0).astype(F32)


def _gla_logits(a_ref, wal_ref, bal_ref):
    logits = jnp.dot(a_ref[...].astype(BF16), wal_ref[...].astype(BF16), preferred_element_type=F32) + bal_ref[...]
    la = (jnp.minimum(logits, 0.0) - jnp.log(1.0 + jnp.exp(-jnp.abs(logits)))) * (1.0 / GTAU)
    return logits, la


def _chunk_cumsum(la_n, tri):
    cum = jnp.dot(tri, la_n, preferred_element_type=F32, precision=lax.Precision.HIGHEST)
    return cum, jnp.sum(la_n, axis=0, keepdims=True)


def _gla_specs(s, nc):
    def blk(width, off):
        return pl.BlockSpec((None, s, width), lambda h, b: (b, 0, off // width + h))

    proj_specs = [blk(GDK, OFF_Q), blk(GDK, OFF_K), blk(GDV, OFF_V), blk(GDV, OFF_G),
                  pl.BlockSpec((None, s, LANE), lambda h, b: (b, 0, OFF_A // LANE)),
                  pl.BlockSpec((LANE, GDK), lambda h, b: (0, h)), pl.BlockSpec((1, GDK), lambda h, b: (0, h)),
                  pl.BlockSpec((1, GDV), lambda h, b: (0, 0))]
    st_spec = pl.BlockSpec((None, None, nc, GDV, GDK), lambda h, b: (b, h, 0, 0, 0))
    return blk, proj_specs, st_spec


def _gla_fwd(proj, w_alpha_p, b_alpha, out_norm_g, *, name):
    bsz, s, _ = proj.shape
    nc = s // CHUNK
    scale = GDK ** -0.5

    rb = min(512, s)

    def body(q_ref, k_ref, v_ref, g_ref, a_ref, wal_ref, bal_ref, ong_ref, o_ref, og_ref, st_ref):
        _, la = _gla_logits(a_ref, wal_ref, bal_ref)
        tri = _tri(True)
        st = jnp.zeros((GDV, GDK), F32)
        for n in range(nc):
            rows = pl.ds(n * CHUNK, CHUNK)
            cum, cum_end = _chunk_cumsum(la[n * CHUNK:(n + 1) * CHUNK], tri)
            kd = k_ref[rows, :] * jnp.exp(cum_end - cum)
            ut = lax.dot_general(v_ref[rows, :].astype(BF16), kd.astype(BF16), _TN, preferred_element_type=F32)
            st = st * jnp.exp(cum_end) + ut
            st_ref[n] = st
            o_ref[rows, :] = lax.dot_general((q_ref[rows, :] * scale).astype(BF16), st.astype(BF16), _NT,
                                             preferred_element_type=F32)
        for j in range(0, s, rb):
            blk_rows = pl.ds(j, rb)
            oh, _ = _rms(o_ref[blk_rows, :], None)
            gv = g_ref[blk_rows, :]
            og_ref[blk_rows, :] = ((oh * ong_ref[...]) * (gv * _sigmoid(gv))).astype(BF16)

    blk, proj_specs, st_spec = _gla_specs(s, nc)
    return pl.pallas_call(
        body, name=name, grid=(GH, bsz), in_specs=proj_specs, out_specs=[blk(GDV, 0), blk(GDV, 0), st_spec],
        out_shape=[jax.ShapeDtypeStruct((bsz, s, GH * GDV), F32), jax.ShapeDtypeStruct((bsz, s, GH * GDV), BF16),
                   jax.ShapeDtypeStruct((bsz, GH, nc, GDV, GDK), F32)],
        compiler_params=_params(2),
    )(proj, proj, proj, proj, proj, w_alpha_p, b_alpha, out_norm_g)


def _gla_bwd(dog, o, states, proj, w_alpha_p, b_alpha, out_norm_g, *, name):
    bsz, s, _ = proj.shape
    nc = s // CHUNK
    scale = GDK ** -0.5

    def body(dog_ref, o_ref, st_ref, q_ref, k_ref, v_ref, g_ref, a_ref, wal_ref, bal_ref, ong_ref,
             dq_ref, dk_ref, dv_ref, dg_ref, dl_ref, dbal_ref, dong_ref, do_scr, dlog_scr):
        h, b = pl.program_id(0), pl.program_id(1)

        @pl.when(b == 0)
        def _():
            dbal_ref[...] = jnp.zeros_like(dbal_ref)

        @pl.when((b == 0) & (h == 0))
        def _():
            dong_ref[...] = jnp.zeros_like(dong_ref)

        ong = ong_ref[...]
        for j in range(0, s, rb):
            blk_rows = pl.ds(j, rb)
            gv, dogv = g_ref[blk_rows, :], dog_ref[blk_rows, :]
            sg = _sigmoid(gv)
            oh, r = _rms(o_ref[blk_rows, :], None)
            don = dogv * (gv * sg)
            dg_ref[blk_rows, :] = (dogv * (oh * ong) * (sg * (1.0 + gv * (1.0 - sg)))).astype(BF16)
            dong_ref[...] += jnp.sum(don * oh, axis=0, keepdims=True)
            doh = don * ong
            do_scr[blk_rows, :] = (r * (doh - oh * jnp.mean(doh * oh, axis=-1, keepdims=True))).astype(BF16)

        logits, la = _gla_logits(a_ref, wal_ref, bal_ref)
        tri_lo, tri_up = _tri(True), _tri(False)
        carry = jnp.zeros((GDV, GDK), F32)
        for n in range(nc - 1, -1, -1):
            rows = pl.ds(n * CHUNK, CHUNK)
            cum, cum_end = _chunk_cumsum(la[n * CHUNK:(n + 1) * CHUNK], tri_lo)
            decay = jnp.exp(cum_end)
            w = jnp.exp(cum_end - cum)
            kd = k_ref[rows, :] * w
            do_b = do_scr[rows, :]
            qs_b = (q_ref[rows, :] * scale).astype(BF16)
            dq_ref[rows, :] = (jnp.dot(do_b, st_ref[n].astype(BF16), preferred_element_type=F32) * scale).astype(BF16)
            dsn = lax.dot_general(do_b, qs_b, _TN, preferred_element_type=F32) + carry
            carry = dsn * decay
            dsn_b = dsn.astype(BF16)
            dv_ref[rows, :] = lax.dot_general(kd.astype(BF16), dsn_b, _NT, preferred_element_type=F32).astype(BF16)
            dkd = jnp.dot(v_ref[rows, :].astype(BF16), dsn_b, preferred_element_type=F32)
            dk_ref[rows, :] = (dkd * w).astype(BF16)
            e = dkd * kd
            dcum_end = jnp.sum(e, axis=0, keepdims=True)
            if n > 0:
                dcum_end += jnp.sum(dsn * st_ref[n - 1], axis=0, keepdims=True) * decay
            dlog_scr[rows, :] = dcum_end - jnp.dot(tri_up, e, preferred_element_type=F32,
                                                  precision=lax.Precision.HIGHEST)
        dlog = dlog_scr[...] * (1.0 / GTAU) * (1.0 - _sigmoid(logits))
        dl_ref[...] = dlog.astype(BF16)
        dbal_ref[...] += jnp.sum(dlog, axis=0, keepdims=True)

    rb = min(512, s)

    blk, proj_specs, st_spec = _gla_specs(s, nc)
    act = lambda wd: jax.ShapeDtypeStruct((bsz, s, wd), BF16)
    return pl.pallas_call(
        body, name=name, grid=(GH, bsz), in_specs=[blk(GDV, 0), blk(GDV, 0), st_spec, *proj_specs],
        out_specs=[blk(GDK, 0), blk(GDK, 0), blk(GDV, 0), blk(GDV, 0), blk(GDK, 0),
                   pl.BlockSpec((1, GDK), lambda h, b: (0, h)), pl.BlockSpec((1, GDV), lambda h, b: (0, 0))],
        out_shape=[act(GH * GDK), act(GH * GDK), act(GH * GDV), act(GH * GDV), act(GH * GDK),
                   jax.ShapeDtypeStruct((1, GH * GDK), F32), jax.ShapeDtypeStruct((1, GDV), F32)],
        scratch_shapes=[pltpu.VMEM((s, GDV), BF16), pltpu.VMEM((s, GDK), F32)], compiler_params=_params(2),
    )(dog, o, states, proj, proj, proj, proj, proj, w_alpha_p, b_alpha, out_norm_g)


def _lane():
    return lax.broadcasted_iota(jnp.int32, (1, LANE), 1)


def _swap_halves(x):
    lane = _lane()
    half = MROPE // 2
    lo = (lane >= MNOPE) & (lane < MNOPE + half)
    hi = (lane >= MNOPE + half) & (lane < MQK)
    return jnp.where(lo, pltpu.roll(x, LANE - half, 1), jnp.where(hi, pltpu.roll(x, half, 1), 0.0))


def _norm96(x, g):
    r = lax.rsqrt(jnp.sum(x * x, axis=-1, keepdims=True) * (1.0 / MQK) + EPS)
    return x * r, r


def _lat_norm(proj, q_lat_g, kv_lat_g, *, name, ts=512):
    t = proj.shape[0]
    ts = min(ts, t)

    def body(cq_ref, ckv_ref, gq_ref, gk_ref, oq_ref, ok_ref):
        xq, _ = _rms(cq_ref[...], None)
        oq_ref[...] = (xq * gq_ref[...]).astype(BF16)
        xk, _ = _rms(ckv_ref[...], None)
        ok_ref[...] = (xk * gk_ref[...]).astype(BF16)

    return pl.pallas_call(
        body, name=name, grid=(t // ts,),
        in_specs=[pl.BlockSpec((ts, MQR), lambda i: (i, OFF_CQ // MQR)), pl.BlockSpec((ts, MKVR), lambda i: (i, OFF_CKV // MKVR)),
                  pl.BlockSpec((1, MQR), lambda i: (0, 0)), pl.BlockSpec((1, MKVR), lambda i: (0, 0))],
        out_specs=[pl.BlockSpec((ts, MQR), lambda i: (i, 0)), pl.BlockSpec((ts, MKVR), lambda i: (i, 0))],
        out_shape=[jax.ShapeDtypeStruct((t, MQR), BF16), jax.ShapeDtypeStruct((t, MKVR), BF16)],
        compiler_params=_params(1),
    )(proj, proj, q_lat_g, kv_lat_g)


def _lat_norm_bwd(dcqn, dckvn, proj, q_lat_g, kv_lat_g, *, name, ts=512):
    t = proj.shape[0]
    ts = min(ts, t)

    def one(d_ref, x_ref, g_ref, dx_ref, dg_ref):
        xh, r = _rms(x_ref[...], None)
        dn = d_ref[...]
        dg_ref[...] += jnp.sum(dn * xh, axis=0, keepdims=True)
        dxh = dn * g_ref[...]
        dx_ref[...] = (r * (dxh - xh * jnp.mean(dxh * xh, axis=-1, keepdims=True))).astype(BF16)

    def body(dq_ref, dk_ref, cq_ref, ckv_ref, gq_ref, gk_ref, dxq_ref, dxk_ref, dgq_ref, dgk_ref):
        @pl.when(pl.program_id(0) == 0)
        def _():
            dgq_ref[...] = jnp.zeros_like(dgq_ref)
            dgk_ref[...] = jnp.zeros_like(dgk_ref)

        one(dq_ref, cq_ref, gq_ref, dxq_ref, dgq_ref)
        one(dk_ref, ckv_ref, gk_ref, dxk_ref, dgk_ref)

    return pl.pallas_call(
        body, name=name, grid=(t // ts,),
        in_specs=[pl.BlockSpec((ts, MQR), lambda i: (i, 0)), pl.BlockSpec((ts, MKVR), lambda i: (i, 0)),
                  pl.BlockSpec((ts, MQR), lambda i: (i, OFF_CQ // MQR)), pl.BlockSpec((ts, MKVR), lambda i: (i, OFF_CKV // MKVR)),
                  pl.BlockSpec((1, MQR), lambda i: (0, 0)), pl.BlockSpec((1, MKVR), lambda i: (0, 0))],
        out_specs=[pl.BlockSpec((ts, MQR), lambda i: (i, 0)), pl.BlockSpec((ts, MKVR), lambda i: (i, 0)),
                   pl.BlockSpec((1, MQR), lambda i: (0, 0)), pl.BlockSpec((1, MKVR), lambda i: (0, 0))],
        out_shape=[jax.ShapeDtypeStruct((t, MQR), BF16), jax.ShapeDtypeStruct((t, MKVR), BF16),
                   jax.ShapeDtypeStruct((1, MQR), F32), jax.ShapeDtypeStruct((1, MKVR), F32)],
        compiler_params=_params(1),
    )(dcqn, dckvn, proj, proj, q_lat_g, kv_lat_g)


def _qk_prep(q_raw, kv, proj, cos_t, sin_t, gq, gk, *, name, ts=2048):
    t = q_raw.shape[0]
    ts = min(ts, t)

    def body(q_ref, kv_ref, kpe_ref, c_ref, s_ref, gq_ref, gk_ref, qo_ref, ko_ref, vo_ref):
        cs, sn = c_ref[...], s_ref[...]
        nope = _lane() < MNOPE
        qn, _ = _norm96(q_ref[...], None)
        qn = qn * gq_ref[...]
        qo_ref[...] = (qn * cs + _swap_halves(qn) * sn).astype(BF16)
        kvv = kv_ref[...]
        kn, _ = _norm96(jnp.where(nope, kvv, kpe_ref[...]), None)
        kn = kn * gk_ref[...]
        ko_ref[...] = (kn * cs + _swap_halves(kn) * sn).astype(BF16)
        vo_ref[...] = jnp.where(nope, pltpu.roll(kvv, MNOPE, 1), 0.0).astype(BF16)

    hd = pl.BlockSpec((ts, LANE), lambda i, h: (i, h))
    shared = lambda col: pl.BlockSpec((ts, LANE), lambda i, h: (i, col))
    gain = pl.BlockSpec((1, LANE), lambda i, h: (0, 0))
    out = jax.ShapeDtypeStruct((t, MH * LANE), BF16)
    return pl.pallas_call(
        body, name=name, grid=(t // ts, MH),
        in_specs=[hd, hd, shared(OFF_KPE // LANE), shared(0), shared(0), gain, gain],
        out_specs=[hd, hd, hd], out_shape=[out, out, out], compiler_params=_params(2),
    )(q_raw, kv, proj, cos_t, sin_t, gq, gk)


def _qk_prep_bwd(dq, dk, dv, q_raw, kv, proj, cos_t, sin_t, gq, gk, *, name, ts=2048):
    t = q_raw.shape[0]
    ts = min(ts, t)

    def norm_bwd(dy, x, g, dg_ref):
        xh, r = _norm96(x, None)
        dg_ref[...] += jnp.sum(dy * xh, axis=0, keepdims=True)
        dxh = dy * g
        return r * (dxh - xh * (jnp.sum(dxh * xh, axis=-1, keepdims=True) * (1.0 / MQK)))

    def body(dq_ref, dk_ref, dv_ref, q_ref, kv_ref, kpe_ref, c_ref, s_ref, gq_ref, gk_ref,
             dqr_ref, dkv_ref, dkpe_ref, dgq_ref, dgk_ref):
        i, h = pl.program_id(0), pl.program_id(1)

        @pl.when(h == 0)
        def _():
            dkpe_ref[...] = jnp.zeros_like(dkpe_ref)

        @pl.when((h == 0) & (i == 0))
        def _():
            dgq_ref[...] = jnp.zeros_like(dgq_ref)
            dgk_ref[...] = jnp.zeros_like(dgk_ref)

        cs, sn = c_ref[...], s_ref[...]
        lane = _lane()
        nope = lane < MNOPE
        dqv = dq_ref[...]
        dqn = dqv * cs + _swap_halves(dqv * sn)
        dqr_ref[...] = norm_bwd(dqn, q_ref[...], gq_ref[...], dgq_ref).astype(BF16)
        dkv_ = dk_ref[...]
        dkn = dkv_ * cs + _swap_halves(dkv_ * sn)
        kvv = kv_ref[...]
        dkr = norm_bwd(dkn, jnp.where(nope, kvv, kpe_ref[...]), gk_ref[...], dgk_ref)
        dkv_ref[...] = jnp.where(nope, dkr, pltpu.roll(dv_ref[...], MNOPE, 1)).astype(BF16)
        dkpe_ref[...] += jnp.where((lane >= MNOPE) & (lane < MQK), dkr, 0.0)

    hd = pl.BlockSpec((ts, LANE), lambda i, h: (i, h))
    shared = lambda col: pl.BlockSpec((ts, LANE), lambda i, h: (i, col))
    gain = pl.BlockSpec((1, LANE), lambda i, h: (0, 0))
    out = jax.ShapeDtypeStruct((t, MH * LANE), BF16)
    return pl.pallas_call(
        body, name=name, grid=(t // ts, MH),
        in_specs=[hd, hd, hd, hd, hd, shared(OFF_KPE // LANE), shared(0), shared(0), gain, gain],
        out_specs=[hd, hd, shared(0), gain, gain],
        out_shape=[out, out, jax.ShapeDtypeStruct((t, LANE), F32), jax.ShapeDtypeStruct((1, LANE), F32),
                   jax.ShapeDtypeStruct((1, LANE), F32)],
        compiler_params=_params(2),
    )(dq, dk, dv, q_raw, kv, proj, cos_t, sin_t, gq, gk)


_NT = (((1,), (1,)), ((), ()))
_TN = (((0,), (0,)), ((), ()))


SOFTMAX_SCALE = MQK ** -0.5
Q_PRESCALE = SOFTMAX_SCALE * float(np.log2(np.e))


def _attn_weights(q, k_ref, lo, tq):
    row = lax.broadcasted_iota(jnp.int32, (tq, tq), 0) // CHUNK
    col = lax.broadcasted_iota(jnp.int32, (tq, tq), 1) // CHUNK
    sd = lax.dot_general(q, k_ref[pl.ds(lo, tq), :], _NT, preferred_element_type=F32)
    sd = jnp.where(col <= row, sd, -1e30)
    m = jnp.max(sd, axis=-1, keepdims=True)
    if lo:
        so = lax.dot_general(q, k_ref[pl.ds(0, lo), :], _NT, preferred_element_type=F32)
        m = jnp.maximum(m, jnp.max(so, axis=-1, keepdims=True))
        eo = jnp.exp2(so - m)
        ed = jnp.exp2(sd - m)
        return eo, ed, 1.0 / (jnp.sum(eo, axis=-1, keepdims=True) + jnp.sum(ed, axis=-1, keepdims=True))
    ed = jnp.exp2(sd - m)
    return None, ed, 1.0 / jnp.sum(ed, axis=-1, keepdims=True)


def _attn_fwd(q, k, v, *, name, tq=256):
    bsz, s, _ = q.shape
    tq = min(tq, s)

    def body(q_ref, k_ref, v_ref, o_ref):
        for i in range(s // tq):
            lo = i * tq
            eo, ed, inv = _attn_weights(q_ref[pl.ds(lo, tq), :], k_ref, lo, tq)
            o = jnp.dot(ed.astype(BF16), v_ref[pl.ds(lo, tq), :], preferred_element_type=F32)
            if lo:
                o += jnp.dot(eo.astype(BF16), v_ref[pl.ds(0, lo), :], preferred_element_type=F32)
            o_ref[pl.ds(lo, tq), :] = (o * inv).astype(BF16)

    spec = pl.BlockSpec((None, s, LANE), lambda b, h: (b, 0, h))
    return pl.pallas_call(
        body, name=name, grid=(bsz, MH), in_specs=[spec, spec, spec], out_specs=spec,
        out_shape=jax.ShapeDtypeStruct((bsz, s, MH * LANE), BF16), compiler_params=_params(2),
    )(q, k, v)


def _attn_bwd(q, k, v, do, *, name, tq=256):
    bsz, s, _ = q.shape
    tq = min(tq, s)

    def body(q_ref, k_ref, v_ref, do_ref, dq_ref, dk_ref, dv_ref):
        dk_ref[...] = jnp.zeros_like(dk_ref)
        dv_ref[...] = jnp.zeros_like(dv_ref)
        for i in range(s // tq):
            lo = i * tq
            here, before = pl.ds(lo, tq), pl.ds(0, lo)
            qv, dov = q_ref[here, :], do_ref[here, :]
            eo, ed, inv = _attn_weights(qv, k_ref, lo, tq)
            do_n = (dov.astype(F32) * inv).astype(BF16)
            dv_ref[here, :] += lax.dot_general(ed.astype(BF16), do_n, _TN, preferred_element_type=F32)
            dpd = lax.dot_general(dov, v_ref[here, :], _NT, preferred_element_type=F32)
            delta = jnp.sum(dpd * ed, axis=-1, keepdims=True)
            if lo:
                dv_ref[before, :] += lax.dot_general(eo.astype(BF16), do_n, _TN, preferred_element_type=F32)
                dpo = lax.dot_general(dov, v_ref[before, :], _NT, preferred_element_type=F32)
                delta += jnp.sum(dpo * eo, axis=-1, keepdims=True)
            delta = delta * inv
            r = inv * SOFTMAX_SCALE
            dsd = (ed * (dpd - delta) * r).astype(BF16)
            dq = jnp.dot(dsd, k_ref[here, :], preferred_element_type=F32)
            dk_ref[here, :] += lax.dot_general(dsd, qv, _TN, preferred_element_type=F32)
            if lo:
                dso = (eo * (dpo - delta) * r).astype(BF16)
                dq += jnp.dot(dso, k_ref[before, :], preferred_element_type=F32)
                dk_ref[before, :] += lax.dot_general(dso, qv, _TN, preferred_element_type=F32)
            dq_ref[here, :] = dq
        dk_ref[...] = dk_ref[...] * (1.0 / Q_PRESCALE)

    spec = pl.BlockSpec((None, s, LANE), lambda b, h: (b, 0, h))
    out = jax.ShapeDtypeStruct((bsz, s, MH * LANE), F32)
    return pl.pallas_call(
        body, name=name, grid=(bsz, MH), in_specs=[spec] * 4, out_specs=[spec] * 3, out_shape=[out, out, out],
        compiler_params=_params(2),
    )(q, k, v, do)


def _adamw(w, g, m, v, *, name, tr=256, by_cols=False):
    rows, cols = w.shape
    tr = _tile_rows(rows, tr)

    def body(w_ref, g_ref, m_ref, v_ref, d_ref, nm_ref, nv_ref):
        d_ref[...], nm_ref[...], nv_ref[...] = _adamw_update(w_ref[...], g_ref[...], m_ref[...], v_ref[...])

    spec = pl.BlockSpec((rows, LANE), lambda i: (0, i)) if by_cols else pl.BlockSpec((tr, cols), lambda i: (i, 0))
    out = jax.ShapeDtypeStruct((rows, cols), F32)
    return pl.pallas_call(body, name=name, grid=(cols // LANE if by_cols else rows // tr,), in_specs=[spec] * 4,
                          out_specs=[spec] * 3, out_shape=[out, out, out], compiler_params=_params(1))(w, g, m, v)


def _tile_rows(rows, target):
    if rows <= target:
        return rows
    best = 8
    for t in range(8, target + 1, 8):
        if rows % t == 0:
            best = t
    return best


def _adamw_update(w, g, m, v):
    nm = ADAM_B1 * m + (1.0 - ADAM_B1) * g
    nv = ADAM_B2 * v + (1.0 - ADAM_B2) * (g * g)
    m_hat = nm / (1.0 - ADAM_B1 ** ADAM_STEP)
    v_hat = nv / (1.0 - ADAM_B2 ** ADAM_STEP)
    return -ADAM_LR * (m_hat / (jnp.sqrt(v_hat) + ADAM_EPS) + ADAM_WD * w), nm, nv


def _adamw_halves(w, m, v, mine, theirs, sel, *, name, tr=256):
    rows, cols = w.shape
    tr = _tile_rows(rows // 2, tr)
    nh = rows // 2 // tr

    def body(sel_ref, w_ref, m_ref, v_ref, mine_ref, theirs_ref, g_ref, d_ref, nm_ref, nv_ref):
        lower = pl.program_id(0) < nh
        south = sel_ref[0] == 0
        gv = jnp.where(lower == south, mine_ref[...], theirs_ref[...])
        g_ref[...] = gv
        d_ref[...], nm_ref[...], nv_ref[...] = _adamw_update(w_ref[...], gv, m_ref[...], v_ref[...])

    full = pl.BlockSpec((tr, cols), lambda i, sel_ref: (i, 0))
    half = pl.BlockSpec((tr, cols), lambda i, sel_ref: (i % nh, 0))
    out = jax.ShapeDtypeStruct((rows, cols), F32)
    return pl.pallas_call(
        body, name=name, out_shape=[out] * 4, compiler_params=_params(1),
        grid_spec=pltpu.PrefetchScalarGridSpec(num_scalar_prefetch=1, grid=(rows // tr,),
                                               in_specs=[full, full, full, half, half], out_specs=[full] * 4),
    )(sel, w, m, v, mine, theirs)


def _pair_add(x, sib, sel, *, name, tr=256):
    n, _, rows, cols = x.shape
    tr = _tile_rows(rows, tr)

    def body(sel_ref, x_ref, s_ref, o_ref):
        o_ref[...] = (x_ref[...] + s_ref[...]).astype(BF16)

    spec = pl.BlockSpec((None, tr, cols), lambda j, i, sel_ref: (j, i, 0))
    return pl.pallas_call(
        body, name=name, out_shape=jax.ShapeDtypeStruct((n, rows, cols), BF16), compiler_params=_params(2),
        grid_spec=pltpu.PrefetchScalarGridSpec(
            num_scalar_prefetch=1, grid=(n, rows // tr),
            in_specs=[pl.BlockSpec((None, None, tr, cols), lambda j, i, sel_ref: (j, sel_ref[0], i, 0)), spec],
            out_specs=spec),
    )(sel, x, sib)


def _chip_sum(pair, recv, sel, *, name, tr=256):
    _, rows, cols = pair.shape
    tr = _tile_rows(rows, tr)

    def body(sel_ref, p_ref, r_ref, o_ref):
        acc = p_ref[...].astype(F32)
        for k in range(3):
            acc = acc + r_ref[k].astype(F32)
        o_ref[...] = acc

    return pl.pallas_call(
        body, name=name, out_shape=jax.ShapeDtypeStruct((rows, cols), F32), compiler_params=_params(1),
        grid_spec=pltpu.PrefetchScalarGridSpec(
            num_scalar_prefetch=1, grid=(rows // tr,),
            in_specs=[pl.BlockSpec((None, tr, cols), lambda i, sel_ref: (sel_ref[0], i, 0)),
                      pl.BlockSpec((3, tr, cols), lambda i, sel_ref: (0, i, 0))],
            out_specs=pl.BlockSpec((tr, cols), lambda i, sel_ref: (i, 0))),
    )(sel, pair, recv)


def _me():
    return lax.axis_index("x"), lax.axis_index("y"), lax.axis_index("c")


def _flip(pos, bits):
    x, y, c = pos
    return (x ^ bits[0] if bits[0] else x, y ^ bits[1] if bits[1] else y, c ^ bits[2] if bits[2] else c)


ANY = pl.BlockSpec(memory_space=pl.ANY)


def _all_gather8(xs, *, name):
    n = len(xs)
    flips = [((k >> 2) & 1, (k >> 1) & 1, k & 1) for k in range(1, 8)]

    def body(*refs):
        x_refs, out_refs, (send_sems, recv_sems, local_sems) = refs[:n], refs[n:2 * n], refs[2 * n:]
        me = _me()
        slot = lambda p: 4 * p[0] + 2 * p[1] + p[2]
        copies = []
        for i in range(n):
            mine = pltpu.make_async_copy(x_refs[i], out_refs[i].at[slot(me)], local_sems.at[i])
            mine.start()
            copies.append(mine)
            for k, f in enumerate(flips):
                peer = _flip(me, f)
                sems = dict(send_sem=send_sems.at[7 * i + k], recv_sem=recv_sems.at[7 * i + k], device_id=peer,
                            device_id_type=MESH)
                cp = pltpu.make_async_remote_copy(src_ref=x_refs[i], dst_ref=out_refs[i].at[slot(me)], **sems)
                cp.start()
                copies.append(cp)
                copies.append(pltpu.make_async_remote_copy(src_ref=x_refs[i], dst_ref=out_refs[i].at[slot(peer)], **sems))
        for i in range(n):
            base = i * 15
            copies[base].wait()
            for k in range(7):
                copies[base + 1 + 2 * k].wait_send()
                copies[base + 2 + 2 * k].wait_recv()

    outs = pl.pallas_call(
        body, name=name, in_specs=[ANY] * n, out_specs=[ANY] * n,
        out_shape=[jax.ShapeDtypeStruct((8, *x.shape), x.dtype) for x in xs],
        scratch_shapes=[pltpu.SemaphoreType.DMA((7 * n,)), pltpu.SemaphoreType.DMA((7 * n,)),
                        pltpu.SemaphoreType.DMA((n,))])(*xs)
    return list(outs)


CHIP_FLIPS = [(1, 0, 0), (0, 1, 0), (1, 1, 0)]


def _chip():
    return 2 * lax.axis_index("x") + lax.axis_index("y")


def _pair_swap_halves(xs, *, name):
    n = len(xs)

    def body(*refs):
        x_refs, out_refs, (send_sems, recv_sems) = refs[:n], refs[n:2 * n], refs[2 * n:]
        me = _me()
        sib = _flip(me, (0, 0, 1))
        copies = [pltpu.make_async_remote_copy(src_ref=x_refs[i].at[:, 1 - me[2]], dst_ref=out_refs[i],
                                               send_sem=send_sems.at[i], recv_sem=recv_sems.at[i], device_id=sib,
                                               device_id_type=MESH) for i in range(n)]
        for cp in copies:
            cp.start()
        for cp in copies:
            cp.wait()

    return pl.pallas_call(
        body, name=name, in_specs=[ANY] * n, out_specs=[ANY] * n,
        out_shape=[jax.ShapeDtypeStruct((x.shape[0], *x.shape[2:]), x.dtype) for x in xs],
        scratch_shapes=[pltpu.SemaphoreType.DMA((n,)), pltpu.SemaphoreType.DMA((n,))])(*xs)


def _pair_swap(hs, *, name):
    n = len(hs)

    def body(*refs):
        h_refs, out_refs, (send_sems, recv_sems) = refs[:n], refs[n:2 * n], refs[2 * n:]
        sib = _flip(_me(), (0, 0, 1))
        copies = [pltpu.make_async_remote_copy(src_ref=h_refs[i], dst_ref=out_refs[i], send_sem=send_sems.at[i],
                                               recv_sem=recv_sems.at[i], device_id=sib, device_id_type=MESH)
                  for i in range(n)]
        for cp in copies:
            cp.start()
        for cp in copies:
            cp.wait()

    return pl.pallas_call(
        body, name=name, in_specs=[ANY] * n, out_specs=[ANY] * n,
        out_shape=[jax.ShapeDtypeStruct(h.shape, h.dtype) for h in hs],
        scratch_shapes=[pltpu.SemaphoreType.DMA((n,)), pltpu.SemaphoreType.DMA((n,))])(*hs)


HBM = pl.BlockSpec(memory_space=pltpu.HBM)
SEM = pl.BlockSpec(memory_space=pltpu.SEMAPHORE)
EFFECT = pltpu.SideEffectType.DATAFLOW_SIDE_EFFECTING


def _plan_copies(plan, refs, send_sems, recv_sems):
    return [pltpu.make_async_remote_copy(src_ref=src, dst_ref=dst, send_sem=send_sems.at[k], recv_sem=recv_sems.at[k],
                                         device_id=to, device_id_type=MESH) for k, (src, dst, to) in enumerate(plan(refs))]


def _rdma_start(arrays, n_copies, plan, deps, *, name):
    n, nd = len(arrays), len(deps)

    def body(*refs):
        for cp in _plan_copies(plan, refs[:n], refs[n + nd], refs[n + nd + 1]):
            cp.start()
        refs[-1][...] = jnp.zeros_like(refs[-1])

    outs = pl.pallas_call(
        body, name=name,
        out_shape=(pltpu.SemaphoreType.DMA((n_copies,)), pltpu.SemaphoreType.DMA((n_copies,)),
                   *[pltpu.HBM(a.shape, a.dtype) for a in arrays], jax.ShapeDtypeStruct((8, LANE), F32)),
        in_specs=[HBM] * n + [ANY] * nd, out_specs=(SEM, SEM, *[HBM] * n, pl.BlockSpec(memory_space=pltpu.VMEM)),
        input_output_aliases={i: i + 2 for i in range(n)}, compiler_params=pltpu.CompilerParams(has_side_effects=EFFECT),
    )(*[pltpu.with_memory_space_constraint(a, pltpu.HBM) for a in arrays], *deps)
    return outs[0], outs[1], list(outs[2:2 + n]), outs[-1]


def _rdma_wait(send_sems, recv_sems, arrays, plan, after, *, name):
    n = len(arrays)

    def body(*refs):
        for cp in _plan_copies(plan, refs[:n], refs[n], refs[n + 1]):
            cp.wait_send()
            cp.wait_recv()

    return list(pl.pallas_call(
        body, name=name, out_shape=tuple(pltpu.HBM(a.shape, a.dtype) for a in arrays),
        in_specs=[HBM] * n + [SEM, SEM, ANY], out_specs=tuple([HBM] * n), input_output_aliases={i: i for i in range(n)},
        compiler_params=pltpu.CompilerParams(has_side_effects=EFFECT),
    )(*arrays, send_sems, recv_sems, after))


def _gather_plan(n):
    def plan(refs):
        me = _me()
        slot = 2 * me[0] + me[1]
        return [(refs[i].at[me[2]], refs[n + i].at[slot, me[2]], _flip(me, f)) for i in range(n) for f in CHIP_FLIPS]
    return plan


def _scatter_plan(n):
    def plan(refs):
        me = _me()
        out = []
        for i in range(n):
            for k, f in enumerate(CHIP_FLIPS):
                peer = _flip(me, f)
                out.append((refs[i].at[2 * peer[0] + peer[1]], refs[n + i].at[k], peer))
        return out
    return plan


def _pair_fill(lands, *, name):
    n = len(lands)

    def body(*refs):
        in_refs, (send_sems, recv_sems) = refs[:n], refs[2 * n:]
        me = _me()
        sib = _flip(me, (0, 0, 1))
        copies = []
        for i in range(n):
            for k, f in enumerate(CHIP_FLIPS):
                peer = _flip(me, f)
                slot = 2 * peer[0] + peer[1]
                mine, theirs = in_refs[i].at[slot, me[2]], in_refs[i].at[slot, 1 - me[2]]
                cp = pltpu.make_async_remote_copy(src_ref=mine, dst_ref=mine, send_sem=send_sems.at[3 * i + k],
                                                  recv_sem=recv_sems.at[3 * i + k], device_id=sib, device_id_type=MESH)
                cp.start()
                copies.append((cp, pltpu.make_async_remote_copy(
                    src_ref=mine, dst_ref=theirs, send_sem=send_sems.at[3 * i + k], recv_sem=recv_sems.at[3 * i + k],
                    device_id=sib, device_id_type=MESH)))
        for cp, arrival in copies:
            arrival.wait_recv()
            cp.wait_send()

    return list(pl.pallas_call(
        body, name=name, in_specs=[ANY] * n, out_specs=[ANY] * n,
        out_shape=[jax.ShapeDtypeStruct(a.shape, a.dtype) for a in lands], input_output_aliases={i: i for i in range(n)},
        scratch_shapes=[pltpu.SemaphoreType.DMA((3 * n,)), pltpu.SemaphoreType.DMA((3 * n,))])(*lands))


def _own_and_landed(lands, xs):
    chip = _chip()
    return [[jnp.where(chip == j, x, o.reshape(4, *x.shape)[j]) for j in range(4)] for o, x in zip(lands, xs)]


BIG = (("w_in", (D, IN_WIDTH // 4), 1), ("gla_w_o", (D // 4, D), 0), ("mla_w_uq", (MQR, MH * MQK // 4), 1),
       ("mla_w_ukv", (MKVR, MH * (MNOPE + MVD) // 4), 1), ("mla_w_o", (D // 4, D), 0), ("w_out", (D // 4, D), 0),
       ("mlp_w1", (D, DFF // 4), 1), ("mlp_w2", (DFF // 4, D), 0))
ADA_SHARD = (D, 6 * D // 4)
SMALL = (("b_ada", 6 * D), ("norm1_g", D), ("b_merge", 2 * D), ("gla_b_alpha", GH * GDK), ("gla_out_norm_g", GDV),
         ("mla_q_lat_g", MQR), ("mla_kv_lat_g", MKVR), ("mla_qn_g", MQK), ("mla_kn_g", MQK), ("norm2_g", D))


W_IN_SEGMENTS = ((0, 3072, OFF_Q), (3072, 3088, OFF_A), (3088, 3344, OFF_CQ), (3344, 3472, OFF_CKV),
                 (3472, 3504, OFF_KPE + MNOPE), (3504, 5552, OFF_MA))
SMALL_ROWS, SMALL_COLS = 32, 2 * D
W_ALPHA_ROW = 16
SMALL_RED = tuple((n, k) for n, k in SMALL if n != "b_ada")


def _pack_small(grads, d_w_alpha, *, name):
    def body(*refs):
        g_refs, wa_ref, out_ref = refs[:-2], refs[-2], refs[-1]
        out_ref[...] = jnp.zeros_like(out_ref)
        for i, ((_, k), g_ref) in enumerate(zip(SMALL_RED, g_refs)):
            out_ref[i:i + 1, 0:k] = g_ref[...]
        out_ref[W_ALPHA_ROW:W_ALPHA_ROW + GLR, 0:GH * GDK] = wa_ref[...]

    return pl.pallas_call(body, name=name, out_shape=jax.ShapeDtypeStruct((SMALL_ROWS, SMALL_COLS), F32))(*grads, d_w_alpha)


def _small_update(gathered, dmod_all, sel, wmv, *, name):
    names = [n for n, _ in SMALL] + ["gla_w_alpha"]
    n_par = len(names)

    def body(sel_ref, g_ref, dmod_ref, *refs):
        in_refs, out_refs, acc = refs[:3 * n_par], refs[3 * n_par:-1], refs[-1]
        total = g_ref[0]
        for j in range(1, 8):
            total = total + g_ref[j]
        acc[...] = total
        row = {n: i for i, (n, _) in enumerate(SMALL_RED)}
        for p, name_p in enumerate(names):
            w_ref, m_ref, v_ref = in_refs[3 * p:3 * p + 3]
            if name_p == "b_ada":
                gv = jnp.sum(dmod_ref[...], axis=0, keepdims=True)
            elif name_p == "gla_w_alpha":
                gv = jnp.zeros((GLR, GDK), F32)
                for j in range(4):
                    blk = acc[W_ALPHA_ROW:W_ALPHA_ROW + GLR, j * GDK:(j + 1) * GDK]
                    gv = gv + jnp.where(sel_ref[0] == j, blk, 0.0)
            else:
                gv = acc[row[name_p]:row[name_p] + 1, 0:w_ref.shape[1]]
            o = out_refs[4 * p:4 * p + 4]
            o[0][...] = gv
            o[1][...], o[2][...], o[3][...] = _adamw_update(w_ref[...], gv, m_ref[...], v_ref[...])

    flat = [a for t in wmv for a in t]
    out_shape = [jax.ShapeDtypeStruct(t[0].shape, F32) for t in wmv for _ in range(4)]
    vmem = pl.BlockSpec(memory_space=pltpu.VMEM)
    outs = pl.pallas_call(
        body, name=name, out_shape=out_shape, in_specs=[pl.BlockSpec(memory_space=pltpu.SMEM), vmem, vmem] + [vmem] * len(flat),
        out_specs=[vmem] * len(out_shape), scratch_shapes=[pltpu.VMEM((SMALL_ROWS, SMALL_COLS), F32)],
    )(sel, gathered, dmod_all, *flat)
    return {n: tuple(outs[4 * p:4 * p + 4]) for p, n in enumerate(names)}


def _full_weights(gathered):
    w = {name: jnp.concatenate(gathered[name], axis=axis) for name, _, axis in BIG if name in gathered and name != "w_in"}
    if "w_in" in gathered:
        shards = gathered["w_in"]
        zeros = lambda n: [jnp.zeros((D, n), shards[0].dtype)]

        def cols(a, b):
            width = IN_WIDTH // 4
            return [shards[j][:, max(a, j * width) - j * width:min(b, (j + 1) * width) - j * width]
                    for j in range(4) if max(a, j * width) < min(b, (j + 1) * width)]

        parts = []
        for a, b, at in sorted(W_IN_SEGMENTS, key=lambda seg: seg[2]):
            have = sum(p.shape[1] for p in parts)
            parts += (zeros(at - have) if at > have else []) + cols(a, b)
        w["w_in"] = jnp.concatenate(parts + zeros(PW - sum(p.shape[1] for p in parts)), axis=1)
    if "mla_w_uq" in w:
        w["mla_w_uq"] = jnp.pad(w["mla_w_uq"].reshape(MQR, MH, MQK), ((0, 0), (0, 0), (0, LANE - MQK))).reshape(MQR, MH * LANE)
    if "mla_w_o" in w:
        w["mla_w_o"] = jnp.pad(w["mla_w_o"].reshape(MH, MVD, D), ((0, 0), (0, LANE - MVD), (0, 0))).reshape(MH * LANE, D)
    return w


def _grad_slots(g):
    g = dict(g)
    out = {}
    if "w_in" in g:
        gi = g.pop("w_in")
        width = IN_WIDTH // 4
        slots = []
        for j in range(4):
            lo, hi = j * width, (j + 1) * width
            slots.append(jnp.concatenate([gi[:, at + max(lo, a) - a:at + min(hi, b) - a]
                                          for a, b, at in W_IN_SEGMENTS if max(lo, a) < min(hi, b)], axis=1))
        out["w_in"] = jnp.stack(slots).reshape(4, 2, D // 2, width)
    if "mla_w_uq" in g:
        g["mla_w_uq"] = g["mla_w_uq"].reshape(MQR, MH, LANE)[:, :, :MQK].reshape(MQR, MH * MQK)
    if "mla_w_o" in g:
        g["mla_w_o"] = g["mla_w_o"].reshape(MH, LANE, D)[:, :MVD].reshape(MH * MVD, D)
    for name, (rows, cols), axis in BIG:
        if name not in g:
            continue
        a = g[name]
        a = a.reshape(4, rows, cols) if axis == 0 else jnp.transpose(a.reshape(rows, 4, cols), (1, 0, 2))
        out[name] = a.reshape(4, 2, rows // 2, cols)
    return out


def _rope_tables(positions):
    freqs = ROPE_THETA ** (-jnp.arange(0, MROPE, 2, dtype=F32) / MROPE)
    lane = np.arange(LANE)
    in_rope = (lane >= MNOPE) & (lane < MQK)
    freq_lane = jnp.where(in_rope, freqs[(lane - MNOPE) % (MROPE // 2)], 0.0)
    sign = np.where(in_rope, np.where(lane < MNOPE + MROPE // 2, -1.0, 1.0), 0.0).astype(np.float32)
    ang = positions.astype(F32).reshape(-1, 1) * freq_lane[None, :]
    return jnp.cos(ang), jnp.sin(ang) * sign[None, :]


def _local_step(x, positions, mod, target, w, small, more_weights=None, on_grads=None):
    kept = {}
    if on_grads is None:
        on_grads = lambda tag, grads: kept.update(grads)
    bsz, s, _ = x.shape
    t = bsz * s
    tt = _tile(t, 1024)
    shift1, scale1, gate1, shift2, scale2, gate2 = [mod[:, None, i * D:(i + 1) * D] for i in range(6)]
    cos_t, sin_t = _rope_tables(positions)
    w_alpha_p = jnp.pad(small["gla_w_alpha"], ((0, LANE - GLR), (0, 0)))
    gq = jnp.pad(small["mla_qn_g"], ((0, 0), (0, LANE - MQK)))
    gk = jnp.pad(small["mla_kn_g"], ((0, 0), (0, LANE - MQK)))
    flat2 = lambda a: a.reshape(t, a.shape[-1])
    bsd = lambda a: a.reshape(bsz, s, a.shape[-1])

    h = _norm_mod(x, small["norm1_g"], scale1, shift1, name="norm1")
    if callable(w):
        w = w(h)
    proj = _mm(flat2(h), w["w_in"], name="proj", tn=1152)
    proj3 = bsd(proj)
    o, o_gated, states = _gla_fwd(proj3, w_alpha_p, small["gla_b_alpha"], small["gla_out_norm_g"], name="gla_fwd")
    if more_weights is not None:
        w = {**w, **more_weights(o_gated)}
    y_a = _mm(flat2(o_gated), w["gla_w_o"], name="gla_out")
    cq_n, ckv_n = _lat_norm(proj, small["mla_q_lat_g"], small["mla_kv_lat_g"], name="lat_norm")
    q_raw = _mm(cq_n, w["mla_w_uq"], name="mla_uq")
    kv = _mm(ckv_n, w["mla_w_ukv"], name="mla_ukv")
    qf, kf, vf = _qk_prep(q_raw, kv, proj, cos_t, sin_t, gq * Q_PRESCALE, gk, name="qk_prep")
    o_attn = _attn_fwd(bsd(qf), bsd(kf), bsd(vf), name="attn_fwd")
    y_b = _mm(flat2(o_attn), w["mla_w_o"], name="mla_out")
    mixed_in = _merge_fwd(proj3, small["b_merge"], bsd(y_a), bsd(y_b), name="merge_fwd")
    mixed = _mm(flat2(mixed_in), w["w_out"], name="w_out")
    x1, h2 = _resid_norm_mod(x, bsd(mixed), gate1, small["norm2_g"], scale2, shift2, name="norm2")

    def sqrelu(acc, ex, outs):
        outs[0][...] = acc
        r = jnp.maximum(acc, 0.0)
        outs[1][...] = (r * r).astype(BF16)

    a1, r = _mm(flat2(h2), w["mlp_w1"], name="mlp1", epilogue=sqrelu,
                out_shape=[jax.ShapeDtypeStruct((t, DFF), F32), jax.ShapeDtypeStruct((t, DFF), BF16)],
                out_specs=[_tile_spec(tt, 1024), _tile_spec(tt, 1024)])
    ff = _mm(r, w["mlp_w2"], name="mlp2")
    dy, dff, dgate2, loss_part = _loss_head(x1, bsd(ff), gate2, target, name="loss_head")

    g = {}

    def relu2_bwd(acc, ex, outs):
        outs[0][...] = (acc * (2.0 * jnp.maximum(ex[0][...], 0.0))).astype(BF16)

    dff2 = flat2(dff)
    da1 = _mm(dff2, w["mlp_w2"], tb=True, name="mlp2_dx", epilogue=relu2_bwd, extras=(a1,),
              extra_specs=(_tile_spec(tt, 1024),), out_shape=jax.ShapeDtypeStruct((t, DFF), BF16),
              out_specs=_tile_spec(tt, 1024))
    g["mlp_w2"] = _mm(r, dff2, ta=True, name="mlp2_dw")
    dh2 = _mm(da1, w["mlp_w1"], tb=True, name="mlp1_dx")
    g["mlp_w1"] = _mm(flat2(h2), da1, ta=True, name="mlp1_dw")
    token = on_grads("mlp", {n: g.pop(n) for n in ("mlp_w2", "mlp_w1")})
    if token is not None:
        gate1 = gate1 + token[0, 0]
    dx1, dscale2, dshift2, dg2, dgate1, dmixed = _norm_mod_bwd(
        bsd(dh2), x1, dy, small["norm2_g"], scale2, gate1, bsd(mixed), name="norm2_bwd")
    dmixed2 = flat2(dmixed)
    dmi = _mm(dmixed2, w["w_out"], tb=True, name="w_out_dx")
    g["w_out"] = _mm(flat2(mixed_in), dmixed2, ta=True, name="w_out_dw")
    dy_a, dy_b, dl_a, dl_b, db_a, db_b = _merge_bwd(bsd(dmi), proj3, small["b_merge"], bsd(y_a), bsd(y_b), name="merge_bwd")
    dy_a2, dy_b2 = flat2(dy_a), flat2(dy_b)
    dog = _mm(dy_a2, w["gla_w_o"], tb=True, name="gla_out_dx")
    g["gla_w_o"] = _mm(flat2(o_gated), dy_a2, ta=True, name="gla_out_dw")
    dq_g, dk_g, dv_g, dg_g, dlog, db_alpha, d_ong = _gla_bwd(
        bsd(dog), o, states, proj3, w_alpha_p, small["gla_b_alpha"], small["gla_out_norm_g"], name="gla_bwd")
    dlog2 = flat2(dlog)
    da_p = _mm(dlog2, w_alpha_p, tb=True, out_dtype=BF16, name="alpha_dx")
    d_w_alpha = _mm(proj[:, OFF_A:OFF_A + LANE], dlog2, ta=True, name="alpha_dw")[:GLR]
    do_attn = _mm(dy_b2, w["mla_w_o"], tb=True, out_dtype=BF16, name="mla_out_dx")
    g["mla_w_o"] = _mm(flat2(o_attn), dy_b2, ta=True, name="mla_out_dw")
    dqf, dkf, dvf = _attn_bwd(bsd(qf), bsd(kf), bsd(vf), bsd(do_attn), name="attn_bwd")
    dq_raw, dkv, dkpe, dgq, dgk = _qk_prep_bwd(flat2(dqf), flat2(dkf), flat2(dvf), q_raw, kv, proj, cos_t, sin_t, gq, gk,
                                                name="qk_prep_bwd")
    dcq_n = _mm(dq_raw, w["mla_w_uq"], tb=True, name="mla_uq_dx")
    g["mla_w_uq"] = _mm(cq_n, dq_raw, ta=True, name="mla_uq_dw")
    dckv_n = _mm(dkv, w["mla_w_ukv"], tb=True, name="mla_ukv_dx")
    g["mla_w_ukv"] = _mm(ckv_n, dkv, ta=True, name="mla_ukv_dw")
    token = on_grads("mix", {n: g.pop(n) for n in ("w_out", "gla_w_o", "mla_w_o", "mla_w_uq", "mla_w_ukv")})
    q_lat_g = small["mla_q_lat_g"] if token is None else small["mla_q_lat_g"] + token[0:1, 0:1]
    dcq, dckv, dg_qlat, dg_kvlat = _lat_norm_bwd(dcq_n, dckv_n, proj, q_lat_g, small["mla_kv_lat_g"],
                                                  name="lat_norm_bwd")
    dproj = jnp.concatenate([flat2(dq_g), flat2(dk_g), flat2(dv_g), flat2(dg_g), flat2(dl_a), flat2(dl_b), dcq, dckv,
                             da_p, dkpe.astype(BF16)], axis=1)
    g["w_in"] = _mm(flat2(h), dproj, ta=True, name="proj_dw", tn=1152)
    token = on_grads("in", {"w_in": g.pop("w_in")})
    after = {} if token is None else dict(extras=(token,), extra_specs=(pl.BlockSpec((8, LANE), lambda i, j, k: (0, 0)),))
    dh = _mm(dproj, w["w_in"], tb=True, name="proj_dx", tk=1152, **after)
    grad_x, dscale1, dshift1, dg1 = _norm_mod_bwd(bsd(dh), x, dx1, small["norm1_g"], scale1, name="norm1_bwd")

    dmod = jnp.concatenate([dshift1, dscale1, dgate1, dshift2, dscale2, dgate2], axis=-1).reshape(bsz, 6 * D)
    gs = {"norm1_g": dg1, "b_merge": jnp.concatenate([db_a, db_b], axis=1), "gla_b_alpha": db_alpha,
          "gla_out_norm_g": d_ong, "mla_q_lat_g": dg_qlat, "mla_kv_lat_g": dg_kvlat, "mla_qn_g": dgq[:, :MQK],
          "mla_kn_g": dgk[:, :MQK], "norm2_g": dg2}
    return loss_part[0, 0], grad_x, dmod, {**kept, **g}, gs, d_w_alpha


def kernel(x, c, positions, w_ada, b_ada, norm1_g, w_in, b_merge, gla_w_alpha, gla_b_alpha, gla_out_norm_g, gla_w_o, mla_q_lat_g, mla_w_uq, mla_kv_lat_g, mla_w_ukv, mla_qn_g, mla_kn_g, mla_w_o, w_out, norm2_g, mlp_w1, mlp_w2, loss_target, m_w_ada, m_b_ada, m_norm1_g, m_w_in, m_b_merge, m_gla_w_alpha, m_gla_b_alpha, m_gla_out_norm_g, m_gla_w_o, m_mla_q_lat_g, m_mla_w_uq, m_mla_kv_lat_g, m_mla_w_ukv, m_mla_qn_g, m_mla_kn_g, m_mla_w_o, m_w_out, m_norm2_g, m_mlp_w1, m_mlp_w2, v_w_ada, v_b_ada, v_norm1_g, v_w_in, v_b_merge, v_gla_w_alpha, v_gla_b_alpha, v_gla_out_norm_g, v_gla_w_o, v_mla_q_lat_g, v_mla_w_uq, v_mla_kv_lat_g, v_mla_w_ukv, v_mla_qn_g, v_mla_kn_g, v_mla_w_o, v_w_out, v_norm2_g, v_mlp_w1, v_mlp_w2):
    args = dict(locals())
    names_big = [n for n, _, _ in BIG]
    names_small = [n for n, _ in SMALL]
    bsz = x.shape[0]
    ax, ay, ac = lax.axis_index("x"), lax.axis_index("y"), lax.axis_index("c")
    chip = 2 * ax + ay
    dev = 2 * chip + ac

    small = {n: args[n] for n in names_small}
    sel_c = jnp.reshape(ac, (1,)).astype(jnp.int32)
    sel_chip = jnp.reshape(chip, (1,)).astype(jnp.int32)
    c_all, w_alpha_all = _all_gather8([c, gla_w_alpha[0]], name="comm_c_alpha")
    small["gla_w_alpha"] = jnp.concatenate([w_alpha_all[2 * j] for j in range(4)], axis=1)
    c_all = c_all.reshape(8 * bsz, D)

    shards = {n: args[n][0].astype(BF16) for n in names_big}
    halves_of = lambda names: [shards[n].reshape(2, shards[n].shape[0] // 2, shards[n].shape[1]) for n in names]

    def gather_start(names, deps, tag):
        xs = halves_of(names)
        lands = [lax.empty((4, *xh.shape), BF16) for xh in xs]
        plan = _gather_plan(len(names))
        return names, plan, _rdma_start(xs + lands, 3 * len(names), plan, deps, name="comm_weights_start_" + tag)

    def gather_finish(started, after, tag):
        names, plan, sems = started
        arrs = _rdma_wait(sems[0], sems[1], sems[2], plan, after, name="comm_weights_wait_" + tag)
        filled = _pair_fill(arrs[len(names):], name="comm_weights_pair_" + tag)
        own = [a.reshape(shards[n].shape) for n, a in zip(names, arrs)]
        return _full_weights(dict(zip(names, _own_and_landed(filled, own))))

    first = gather_start(["w_in"], (c_all,), "in")
    c_all = c_all + first[2][3][0, 0]


    def add_bias(acc, ex, outs):
        outs[0][...] = acc + ex[0][...]

    silu = lambda v: v * _sigmoid(v)
    b_ada_mine = lax.dynamic_slice(b_ada, (0, chip * ADA_SHARD[1]), (1, ADA_SHARD[1]))
    mod_part = _mm(c_all, w_ada[0], name="ada", tn=512, a_fn=silu, epilogue=add_bias, extras=(b_ada_mine,),
                   extra_specs=(pl.BlockSpec((1, 512), lambda i, j, k: (0, j)),),
                   out_shape=jax.ShapeDtypeStruct((8 * bsz, ADA_SHARD[1]), F32), out_specs=_tile_spec(8 * bsz, 512))
    mod_all = _all_gather8([mod_part], name="comm_mod")[0]
    mod_rows = lax.dynamic_slice(mod_all, (0, dev * bsz, 0), (8, bsz, ADA_SHARD[1]))
    mod = jnp.concatenate([mod_rows[2 * j] for j in range(4)], axis=1)
    rest = gather_start([n for n in names_big if n != "w_in"], (mod,), "rest")
    mod = mod + rest[2][3][0, 0]
    w_in_after = lambda after: gather_finish(first, after, "in")
    more_weights = lambda after: gather_finish(rest, after, "rest")

    in_flight = []

    def reduce_start(tag, grads):
        names = list(grads)
        parts = [_grad_slots(grads)[n] for n in names]
        sib_halves = _pair_swap_halves(parts, name="comm_pair_sum_" + tag)
        pairs = [_pair_add(p, s, sel_c, name="pair_add_" + n) for n, p, s in zip(names, parts, sib_halves)]
        recvs = [lax.empty((3, *p.shape[1:]), BF16) for p in pairs]
        plan = _scatter_plan(len(names))
        sems = _rdma_start(pairs + recvs, 3 * len(names), plan, (), name="comm_scatter_start_" + tag)
        in_flight.append((tag, names, plan, sems))
        return sems[3]

    loss_part, grad_x, dmod, g, gs, d_w_alpha = _local_step(x, positions, mod, loss_target, w_in_after, small,
                                                            more_weights, reduce_start)
    loss = lax.psum(loss_part * (0.5 / D), ("x", "y", "c"))

    gs_packed = _pack_small([gs[n] for n, _ in SMALL_RED], d_w_alpha, name="pack_small")
    dmod_all, gs_all = _all_gather8([dmod, gs_packed], name="comm_dmod_small")
    dmod_all = dmod_all.reshape(8 * bsz, 6 * D)
    dmod_mine = lax.dynamic_slice(dmod_all, (0, chip * ADA_SHARD[1]), (8 * bsz, ADA_SHARD[1]))
    g_w_ada = _mm(c_all, dmod_mine, ta=True, a_fn=silu, name="ada_dw")

    wmv = [(args[n], args["m_" + n], args["v_" + n]) for n in names_small]
    wmv.append((gla_w_alpha[0], m_gla_w_alpha[0], v_gla_w_alpha[0]))
    res = _small_update(gs_all, dmod_all, sel_chip, wmv, name="small_update")

    assert not g, list(g)
    half_of = {}
    for tag, names, plan, sems in in_flight:
        arrs = _rdma_wait(sems[0], sems[1], sems[2], plan, grad_x, name="comm_scatter_wait_" + tag)
        for n, p, r in zip(names, arrs[:len(names)], arrs[len(names):]):
            half_of[n] = _chip_sum(p, r, sel_chip, name="chip_sum_" + n)
    halves = [half_of[n] for n in names_big]
    theirs = _pair_swap(halves, name="comm_pair_join")
    for n, mine, other in zip(names_big, halves, theirs):
        if n == "w_in":
            south = ac == 0
            g_t = jnp.concatenate([jnp.where(south, mine, other), jnp.where(south, other, mine)], axis=0).T
            outs = _adamw(w_in[0].T, g_t, m_w_in[0].T, v_w_in[0].T, name="adamw_w_in", by_cols=True)
            res[n] = tuple(a.T for a in (g_t, *outs))
            continue
        res[n] = _adamw_halves(args[n][0], args["m_" + n][0], args["v_" + n][0], mine, other, sel_c, name="adamw_" + n)
    res["w_ada"] = (g_w_ada, *_adamw(w_ada[0], g_w_ada, m_w_ada[0], v_w_ada[0], name="adamw_w_ada"))

    order = ["w_ada", "b_ada", "norm1_g", "w_in", "b_merge", "gla_w_alpha", "gla_b_alpha", "gla_out_norm_g", "gla_w_o",
             "mla_q_lat_g", "mla_w_uq", "mla_kv_lat_g", "mla_w_ukv", "mla_qn_g", "mla_kn_g", "mla_w_o", "w_out",
             "norm2_g", "mlp_w1", "mlp_w2"]
    named = lambda k: [res[n][k].reshape(args[n].shape) for n in order]
    return (loss, grad_x, *named(0), *named(1), *named(2), *named(3))
```

```python
import functools

import jax
import jax.numpy as jnp
import numpy as np
from jax import lax
from jax.experimental import pallas as pl
from jax.experimental.pallas import tpu as pltpu

F32 = jnp.float32
BF16 = jnp.bfloat16
MESH = pl.DeviceIdType.MESH

D = 1024
CHUNK = 64
EPS = 1e-6
GH, GDK, GDV, GLR, GTAU = 4, 128, 256, 16, 16.0
MH, MQR, MKVR, MNOPE, MROPE, MVD = 16, 256, 128, 64, 32, 64
MQK = MNOPE + MROPE
DFF = 4 * D
ROPE_THETA = 10000.0
IN_WIDTH = 5552
LANE = 128
OFF_Q, OFF_K, OFF_V, OFF_G, OFF_MA, OFF_MB, OFF_CQ, OFF_CKV, OFF_A, OFF_KPE, PW = (
    0, 512, 1024, 2048, 3072, 4096, 5120, 5376, 5504, 5632, 5760)
ADAM_LR, ADAM_B1, ADAM_B2, ADAM_EPS, ADAM_WD, ADAM_STEP = 0.001, 0.9, 0.999, 1e-08, 0.01, 10
VMEM_LIMIT = 48 * 1024 * 1024


def _params(n_axes):
    return pltpu.CompilerParams(dimension_semantics=("arbitrary",) * n_axes, vmem_limit_bytes=VMEM_LIMIT)


def _tile(n, target):
    if n <= target:
        return n
    best = None
    for t in range(LANE, target + 1, LANE):
        if n % t == 0:
            best = t
    assert best is not None, (n, target)
    return best


def _sigmoid(x):
    return 1.0 / (1.0 + jnp.exp(-x))


def _mm(a, b, *, name, ta=False, tb=False, out_dtype=F32, tm=1024, tn=1024, tk=1024,
        epilogue=None, extras=(), extra_specs=(), out_shape=None, out_specs=None, a_fn=None):
    if ta:
        kdim, m = a.shape
    else:
        m, kdim = a.shape
    if tb:
        n, k2 = b.shape
    else:
        k2, n = b.shape
    assert kdim == k2, (a.shape, b.shape)
    tm, tn, tk = _tile(m, tm), _tile(n, tn), _tile(kdim, tk)
    nk = kdim // tk
    a_spec = pl.BlockSpec((tk, tm), lambda i, j, k: (k, i)) if ta else pl.BlockSpec((tm, tk), lambda i, j, k: (i, k))
    b_spec = pl.BlockSpec((tn, tk), lambda i, j, k: (j, k)) if tb else pl.BlockSpec((tk, tn), lambda i, j, k: (k, j))
    dims = (((0 if ta else 1,), (1 if tb else 0,)), ((), ()))
    ne = len(extras)
    if out_shape is None:
        out_shape = jax.ShapeDtypeStruct((m, n), out_dtype)
        out_specs = pl.BlockSpec((tm, tn), lambda i, j, k: (i, j))
    n_out = len(out_shape) if isinstance(out_shape, (list, tuple)) else 1
    in_place = epilogue is None and n_out == 1 and out_shape.dtype == F32
    scratch = [] if (nk == 1 or in_place) else [pltpu.VMEM((tm, tn), F32)]

    def body(a_ref, b_ref, *rest):
        ex, outs = rest[:ne], rest[ne:ne + n_out]
        av = a_ref[...] if a_fn is None else a_fn(a_ref[...])
        prod = lax.dot_general(av.astype(BF16), b_ref[...].astype(BF16), dims, preferred_element_type=F32)

        def finish(val):
            if epilogue is None:
                outs[0][...] = val.astype(outs[0].dtype)
            else:
                epilogue(val, ex, outs)

        if nk == 1:
            finish(prod)
            return
        k = pl.program_id(2)
        acc = outs[0] if in_place else rest[-1]

        @pl.when(k == 0)
        def _():
            acc[...] = prod

        @pl.when(k > 0)
        def _():
            acc[...] += prod

        if not in_place:
            @pl.when(k == nk - 1)
            def _():
                finish(acc[...])

    return pl.pallas_call(
        body, name=name, grid=(m // tm, n // tn, nk),
        in_specs=[a_spec, b_spec, *extra_specs], out_specs=out_specs, out_shape=out_shape,
        scratch_shapes=scratch, compiler_params=_params(3),
    )(a, b, *extras)


def _tile_spec(tm, tn):
    return pl.BlockSpec((tm, tn), lambda i, j, k: (i, j))


def _pieces_dx(pieces, w, after, *, name, tm=256):
    t = pieces[0][0].shape[0]
    tm = _tile(t, tm)
    npc = len(pieces)

    def body(*refs):
        p_refs, w_ref, out_ref = refs[:npc], refs[npc], refs[-1]
        acc = None
        for (arr, off), p_ref in zip(pieces, p_refs):
            part = lax.dot_general(p_ref[...].astype(BF16), w_ref[:, off:off + arr.shape[1]], _NT,
                                   preferred_element_type=F32)
            acc = part if acc is None else acc + part
        out_ref[...] = acc

    return pl.pallas_call(
        body, name=name, grid=(t // tm,),
        in_specs=[pl.BlockSpec((tm, arr.shape[1]), lambda i: (i, 0)) for arr, _ in pieces]
        + [pl.BlockSpec(w.shape, lambda i: (0, 0)), pl.BlockSpec((8, LANE), lambda i: (0, 0))],
        out_specs=pl.BlockSpec((tm, w.shape[0]), lambda i: (i, 0)),
        out_shape=jax.ShapeDtypeStruct((t, w.shape[0]), F32), compiler_params=_params(1),
    )(*[arr for arr, _ in pieces], w, after)


def _pieces_dw(h, pieces, *, name, tk=512):
    t, d = h.shape
    tk = _tile(t, tk)
    widths = [p.shape[1] for p in pieces]
    starts = [sum(widths[:i]) for i in range(len(pieces))]

    def body(h_ref, *refs):
        p_refs, out_ref = refs[:-1], refs[-1]
        first = pl.program_id(0) == 0
        hv = h_ref[...]
        for p_ref, start, width in zip(p_refs, starts, widths):
            part = lax.dot_general(hv, p_ref[...].astype(BF16), _TN, preferred_element_type=F32)
            cols = slice(start, start + width)

            @pl.when(first)
            def _():
                out_ref[:, cols] = part

            @pl.when(jnp.logical_not(first))
            def _():
                out_ref[:, cols] += part

    return pl.pallas_call(
        body, name=name, grid=(t // tk,),
        in_specs=[pl.BlockSpec((tk, d), lambda k: (k, 0))] + [pl.BlockSpec((tk, wd), lambda k: (k, 0)) for wd in widths],
        out_specs=pl.BlockSpec((d, sum(widths)), lambda k: (0, 0)),
        out_shape=jax.ShapeDtypeStruct((d, sum(widths)), F32), compiler_params=_params(1),
    )(h, *pieces)


def _rms(x, g):
    r = lax.rsqrt(jnp.mean(x * x, axis=-1, keepdims=True) + EPS)
    return x * r, r


def _row_spec(ts, width, col=0):
    return pl.BlockSpec((None, ts, width), lambda b, i: (b, i, col))


def _vec_spec(width):
    return pl.BlockSpec((None, 1, width), lambda b, i: (b, 0, 0))


def _gain_spec(width):
    return pl.BlockSpec((1, width), lambda b, i: (0, 0))


def _norm_mod(x, g, scale, shift, *, name, ts=256):
    bsz, s, d = x.shape
    ts = min(ts, s)

    def body(x_ref, g_ref, sc_ref, sh_ref, h_ref):
        xh, _ = _rms(x_ref[...], None)
        h_ref[...] = ((xh * g_ref[...]) * (1.0 + sc_ref[...]) + sh_ref[...]).astype(BF16)

    return pl.pallas_call(
        body, name=name, grid=(bsz, s // ts),
        in_specs=[_row_spec(ts, d), _gain_spec(d), _vec_spec(d), _vec_spec(d)],
        out_specs=_row_spec(ts, d), out_shape=jax.ShapeDtypeStruct((bsz, s, d), BF16),
        compiler_params=_params(2),
    )(x, g, scale, shift)


def _resid_norm_mod(x, mixed, gate, g, scale, shift, *, name, ts=256):
    bsz, s, d = x.shape
    ts = min(ts, s)

    def body(x_ref, mx_ref, gt_ref, g_ref, sc_ref, sh_ref, x1_ref, h_ref):
        x1 = x_ref[...] + gt_ref[...] * mx_ref[...]
        x1_ref[...] = x1
        xh, _ = _rms(x1, None)
        h_ref[...] = ((xh * g_ref[...]) * (1.0 + sc_ref[...]) + sh_ref[...]).astype(BF16)

    return pl.pallas_call(
        body, name=name, grid=(bsz, s // ts),
        in_specs=[_row_spec(ts, d), _row_spec(ts, d), _vec_spec(d), _gain_spec(d), _vec_spec(d), _vec_spec(d)],
        out_specs=[_row_spec(ts, d), _row_spec(ts, d)],
        out_shape=[jax.ShapeDtypeStruct((bsz, s, d), F32), jax.ShapeDtypeStruct((bsz, s, d), BF16)],
        compiler_params=_params(2),
    )(x, mixed, gate, g, scale, shift)


def _norm_mod_bwd(dh, xin, resid, g, scale, gate=None, mixed=None, *, name, ts=256):
    bsz, s, d = xin.shape
    ts = min(ts, s)
    gated = gate is not None

    def body(*refs):
        if gated:
            dh_ref, x_ref, rs_ref, g_ref, sc_ref, gt_ref, mx_ref, dx_ref, dsc_ref, dsh_ref, dg_ref, dgt_ref, dmx_ref = refs
        else:
            dh_ref, x_ref, rs_ref, g_ref, sc_ref, dx_ref, dsc_ref, dsh_ref, dg_ref = refs
        b, i = pl.program_id(0), pl.program_id(1)

        @pl.when(i == 0)
        def _():
            dsc_ref[...] = jnp.zeros_like(dsc_ref)
            dsh_ref[...] = jnp.zeros_like(dsh_ref)
            if gated:
                dgt_ref[...] = jnp.zeros_like(dgt_ref)

        @pl.when((i == 0) & (b == 0))
        def _():
            dg_ref[...] = jnp.zeros_like(dg_ref)

        dh_v, gv = dh_ref[...], g_ref[...]
        xh, r = _rms(x_ref[...], None)
        dsc_ref[...] += jnp.sum(dh_v * (xh * gv), axis=0, keepdims=True)
        dsh_ref[...] += jnp.sum(dh_v, axis=0, keepdims=True)
        dn = dh_v * (1.0 + sc_ref[...])
        dg_ref[...] += jnp.sum(dn * xh, axis=0, keepdims=True)
        dxh = dn * gv
        dx = rs_ref[...] + r * (dxh - xh * jnp.mean(dxh * xh, axis=-1, keepdims=True))
        dx_ref[...] = dx
        if gated:
            dgt_ref[...] += jnp.sum(dx * mx_ref[...], axis=0, keepdims=True)
            dmx_ref[...] = (dx * gt_ref[...]).astype(BF16)

    ins = [dh, xin, resid, g, scale]
    in_specs = [_row_spec(ts, d), _row_spec(ts, d), _row_spec(ts, d), _gain_spec(d), _vec_spec(d)]
    out_specs = [_row_spec(ts, d), _vec_spec(d), _vec_spec(d), _gain_spec(d)]
    out_shape = [jax.ShapeDtypeStruct((bsz, s, d), F32), jax.ShapeDtypeStruct((bsz, 1, d), F32),
                 jax.ShapeDtypeStruct((bsz, 1, d), F32), jax.ShapeDtypeStruct((1, d), F32)]
    if gated:
        ins += [gate, mixed]
        in_specs += [_vec_spec(d), _row_spec(ts, d)]
        out_specs += [_vec_spec(d), _row_spec(ts, d)]
        out_shape += [jax.ShapeDtypeStruct((bsz, 1, d), F32), jax.ShapeDtypeStruct((bsz, s, d), BF16)]
    return pl.pallas_call(
        body, name=name, grid=(bsz, s // ts), in_specs=in_specs, out_specs=out_specs, out_shape=out_shape,
        compiler_params=_params(2),
    )(*ins)


def _loss_head(x1, ff, gate2, target, *, name, ts=256):
    bsz, s, d = x1.shape
    ts = min(ts, s)

    def body(x1_ref, ff_ref, gt_ref, t_ref, dy_ref, dff_ref, dgt_ref, loss_ref, acc):
        b, i = pl.program_id(0), pl.program_id(1)

        @pl.when(i == 0)
        def _():
            dgt_ref[...] = jnp.zeros_like(dgt_ref)

        @pl.when((i == 0) & (b == 0))
        def _():
            acc[...] = jnp.zeros_like(acc)

        ffv, gt = ff_ref[...], gt_ref[...]
        diff = (x1_ref[...] + gt * ffv) - t_ref[...]
        acc[...] += jnp.sum((diff * diff).reshape(ts // 8, 8, d), axis=0)
        dy = diff * (1.0 / d)
        dy_ref[...] = dy
        dgt_ref[...] += jnp.sum(dy * ffv, axis=0, keepdims=True)
        dff_ref[...] = (dy * gt).astype(BF16)

        @pl.when((i == pl.num_programs(1) - 1) & (b == pl.num_programs(0) - 1))
        def _():
            loss_ref[...] = jnp.full(loss_ref.shape, jnp.sum(acc[...]), F32)

    return pl.pallas_call(
        body, name=name, grid=(bsz, s // ts),
        in_specs=[_row_spec(ts, d), _row_spec(ts, d), _vec_spec(d), _row_spec(ts, d)],
        out_specs=[_row_spec(ts, d), _row_spec(ts, d), _vec_spec(d), pl.BlockSpec((8, LANE), lambda b, i: (0, 0))],
        out_shape=[jax.ShapeDtypeStruct((bsz, s, d), F32), jax.ShapeDtypeStruct((bsz, s, d), BF16),
                   jax.ShapeDtypeStruct((bsz, 1, d), F32), jax.ShapeDtypeStruct((8, LANE), F32)],
        scratch_shapes=[pltpu.VMEM((8, d), F32)], compiler_params=_params(2),
    )(x1, ff, gate2, target)


def _merge_fwd(proj, b_merge, y_a, y_b, *, name, ts=256):
    bsz, s, _ = proj.shape
    ts = min(ts, s)

    def body(la_ref, lb_ref, ba_ref, bb_ref, ya_ref, yb_ref, out_ref):
        ga = _sigmoid(la_ref[...] + ba_ref[...])
        gb = _sigmoid(lb_ref[...] + bb_ref[...])
        out_ref[...] = (ga * ya_ref[...] + gb * yb_ref[...]).astype(BF16)

    return pl.pallas_call(
        body, name=name, grid=(bsz, s // ts),
        in_specs=[_row_spec(ts, D, OFF_MA // D), _row_spec(ts, D, OFF_MB // D),
                  pl.BlockSpec((1, D), lambda b, i: (0, 0)), pl.BlockSpec((1, D), lambda b, i: (0, 1)),
                  _row_spec(ts, D), _row_spec(ts, D)],
        out_specs=_row_spec(ts, D), out_shape=jax.ShapeDtypeStruct((bsz, s, D), BF16),
        compiler_params=_params(2),
    )(proj, proj, b_merge, b_merge, y_a, y_b)


def _merge_bwd(dmi, proj, b_merge, y_a, y_b, *, name, ts=256):
    bsz, s, _ = proj.shape
    ts = min(ts, s)

    def body(d_ref, la_ref, lb_ref, ba_ref, bb_ref, ya_ref, yb_ref, dya_ref, dyb_ref, dla_ref, dlb_ref, dba_ref, dbb_ref):
        @pl.when((pl.program_id(0) == 0) & (pl.program_id(1) == 0))
        def _():
            dba_ref[...] = jnp.zeros_like(dba_ref)
            dbb_ref[...] = jnp.zeros_like(dbb_ref)

        dv = d_ref[...]
        ga = _sigmoid(la_ref[...] + ba_ref[...])
        gb = _sigmoid(lb_ref[...] + bb_ref[...])
        dya_ref[...] = (dv * ga).astype(BF16)
        dyb_ref[...] = (dv * gb).astype(BF16)
        dla = (dv * ya_ref[...]) * (ga * (1.0 - ga))
        dlb = (dv * yb_ref[...]) * (gb * (1.0 - gb))
        dla_ref[...] = dla.astype(BF16)
        dlb_ref[...] = dlb.astype(BF16)
        dba_ref[...] += jnp.sum(dla, axis=0, keepdims=True)
        dbb_ref[...] += jnp.sum(dlb, axis=0, keepdims=True)

    act = jax.ShapeDtypeStruct((bsz, s, D), BF16)
    return pl.pallas_call(
        body, name=name, grid=(bsz, s // ts),
        in_specs=[_row_spec(ts, D), _row_spec(ts, D, OFF_MA // D), _row_spec(ts, D, OFF_MB // D),
                  pl.BlockSpec((1, D), lambda b, i: (0, 0)), pl.BlockSpec((1, D), lambda b, i: (0, 1)),
                  _row_spec(ts, D), _row_spec(ts, D)],
        out_specs=[_row_spec(ts, D)] * 4 + [_gain_spec(D)] * 2,
        out_shape=[act, act, act, act, jax.ShapeDtypeStruct((1, D), F32), jax.ShapeDtypeStruct((1, D), F32)],
        compiler_params=_params(2),
    )(dmi, proj, proj, b_merge, b_merge, y_a, y_b)


def _tri(lower):
    r = lax.broadcasted_iota(jnp.int32, (CHUNK, CHUNK), 0)
    c = lax.broadcasted_iota(jnp.int32, (CHUNK, CHUNK), 1)
    return jnp.where((c <= r) if lower else (c >= r), 1.0, 0.0).astype(F32)


def _gla_logits(a_ref, wal_ref, bal_ref):
    logits = jnp.dot(a_ref[...].astype(BF16), wal_ref[...].astype(BF16), preferred_element_type=F32) + bal_ref[...]
    la = (jnp.minimum(logits, 0.0) - jnp.log(1.0 + jnp.exp(-jnp.abs(logits)))) * (1.0 / GTAU)
    return logits, la


def _chunk_cumsum(la_n, tri):
    cum = jnp.dot(tri, la_n, preferred_element_type=F32, precision=lax.Precision.HIGHEST)
    return cum, jnp.sum(la_n, axis=0, keepdims=True)


def _gla_specs(s, nc):
    def blk(width, off):
        return pl.BlockSpec((None, s, width), lambda h, b: (b, 0, off // width + h))

    proj_specs = [blk(GDK, OFF_Q), blk(GDK, OFF_K), blk(GDV, OFF_V), blk(GDV, OFF_G),
                  pl.BlockSpec((None, s, LANE), lambda h, b: (b, 0, OFF_A // LANE)),
                  pl.BlockSpec((LANE, GDK), lambda h, b: (0, h)), pl.BlockSpec((1, GDK), lambda h, b: (0, h)),
                  pl.BlockSpec((1, GDV), lambda h, b: (0, 0))]
    st_spec = pl.BlockSpec((None, None, nc, GDV, GDK), lambda h, b: (b, h, 0, 0, 0))
    return blk, proj_specs, st_spec


def _gla_fwd(proj, w_alpha_p, b_alpha, out_norm_g, *, name):
    bsz, s, _ = proj.shape
    nc = s // CHUNK
    scale = GDK ** -0.5

    rb = min(512, s)

    def body(q_ref, k_ref, v_ref, g_ref, a_ref, wal_ref, bal_ref, ong_ref, o_ref, og_ref, st_ref):
        _, la = _gla_logits(a_ref, wal_ref, bal_ref)
        tri = _tri(True)
        st = jnp.zeros((GDV, GDK), F32)
        for n in range(nc):
            rows = pl.ds(n * CHUNK, CHUNK)
            cum, cum_end = _chunk_cumsum(la[n * CHUNK:(n + 1) * CHUNK], tri)
            kd = k_ref[rows, :] * jnp.exp(cum_end - cum)
            ut = lax.dot_general(v_ref[rows, :].astype(BF16), kd.astype(BF16), _TN, preferred_element_type=F32)
            st = st * jnp.exp(cum_end) + ut
            st_ref[n] = st
            o_ref[rows, :] = lax.dot_general((q_ref[rows, :] * scale).astype(BF16), st.astype(BF16), _NT,
                                             preferred_element_type=F32)
        for j in range(0, s, rb):
            blk_rows = pl.ds(j, rb)
            oh, _ = _rms(o_ref[blk_rows, :], None)
            gv = g_ref[blk_rows, :]
            og_ref[blk_rows, :] = ((oh * ong_ref[...]) * (gv * _sigmoid(gv))).astype(BF16)

    blk, proj_specs, st_spec = _gla_specs(s, nc)
    return pl.pallas_call(
        body, name=name, grid=(GH, bsz), in_specs=proj_specs, out_specs=[blk(GDV, 0), blk(GDV, 0), st_spec],
        out_shape=[jax.ShapeDtypeStruct((bsz, s, GH * GDV), F32), jax.ShapeDtypeStruct((bsz, s, GH * GDV), BF16),
                   jax.ShapeDtypeStruct((bsz, GH, nc, GDV, GDK), F32)],
        compiler_params=_params(2),
    )(proj, proj, proj, proj, proj, w_alpha_p, b_alpha, out_norm_g)


def _gla_bwd(dog, o, states, proj, w_alpha_p, b_alpha, out_norm_g, *, name):
    bsz, s, _ = proj.shape
    nc = s // CHUNK
    scale = GDK ** -0.5

    def body(dog_ref, o_ref, st_ref, q_ref, k_ref, v_ref, g_ref, a_ref, wal_ref, bal_ref, ong_ref,
             dq_ref, dk_ref, dv_ref, dg_ref, dl_ref, dbal_ref, dong_ref, do_scr, dlog_scr):
        h, b = pl.program_id(0), pl.program_id(1)

        @pl.when(b == 0)
        def _():
            dbal_ref[...] = jnp.zeros_like(dbal_ref)

        @pl.when((b == 0) & (h == 0))
        def _():
            dong_ref[...] = jnp.zeros_like(dong_ref)

        ong = ong_ref[...]
        for j in range(0, s, rb):
            blk_rows = pl.ds(j, rb)
            gv, dogv = g_ref[blk_rows, :], dog_ref[blk_rows, :]
            sg = _sigmoid(gv)
            oh, r = _rms(o_ref[blk_rows, :], None)
            don = dogv * (gv * sg)
            dg_ref[blk_rows, :] = (dogv * (oh * ong) * (sg * (1.0 + gv * (1.0 - sg)))).astype(BF16)
            dong_ref[...] += jnp.sum(don * oh, axis=0, keepdims=True)
            doh = don * ong
            do_scr[blk_rows, :] = (r * (doh - oh * jnp.mean(doh * oh, axis=-1, keepdims=True))).astype(BF16)

        logits, la = _gla_logits(a_ref, wal_ref, bal_ref)
        tri_lo, tri_up = _tri(True), _tri(False)
        carry = jnp.zeros((GDV, GDK), F32)
        for n in range(nc - 1, -1, -1):
            rows = pl.ds(n * CHUNK, CHUNK)
            cum, cum_end = _chunk_cumsum(la[n * CHUNK:(n + 1) * CHUNK], tri_lo)
            decay = jnp.exp(cum_end)
            w = jnp.exp(cum_end - cum)
            kd = k_ref[rows, :] * w
            do_b = do_scr[rows, :]
            qs_b = (q_ref[rows, :] * scale).astype(BF16)
            dq_ref[rows, :] = (jnp.dot(do_b, st_ref[n].astype(BF16), preferred_element_type=F32) * scale).astype(BF16)
            dsn = lax.dot_general(do_b, qs_b, _TN, preferred_element_type=F32) + carry
            carry = dsn * decay
            dsn_b = dsn.astype(BF16)
            dv_ref[rows, :] = lax.dot_general(kd.astype(BF16), dsn_b, _NT, preferred_element_type=F32).astype(BF16)
            dkd = jnp.dot(v_ref[rows, :].astype(BF16), dsn_b, preferred_element_type=F32)
            dk_ref[rows, :] = (dkd * w).astype(BF16)
            e = dkd * kd
            dcum_end = jnp.sum(e, axis=0, keepdims=True)
            if n > 0:
                dcum_end += jnp.sum(dsn * st_ref[n - 1], axis=0, keepdims=True) * decay
            dlog_scr[rows, :] = dcum_end - jnp.dot(tri_up, e, preferred_element_type=F32,
                                                  precision=lax.Precision.HIGHEST)
        dlog = dlog_scr[...] * (1.0 / GTAU) * (1.0 - _sigmoid(logits))
        dl_ref[...] = dlog.astype(BF16)
        dbal_ref[...] += jnp.sum(dlog, axis=0, keepdims=True)

    rb = min(512, s)

    blk, proj_specs, st_spec = _gla_specs(s, nc)
    act = lambda wd: jax.ShapeDtypeStruct((bsz, s, wd), BF16)
    return pl.pallas_call(
        body, name=name, grid=(GH, bsz), in_specs=[blk(GDV, 0), blk(GDV, 0), st_spec, *proj_specs],
        out_specs=[blk(GDK, 0), blk(GDK, 0), blk(GDV, 0), blk(GDV, 0), blk(GDK, 0),
                   pl.BlockSpec((1, GDK), lambda h, b: (0, h)), pl.BlockSpec((1, GDV), lambda h, b: (0, 0))],
        out_shape=[act(GH * GDK), act(GH * GDK), act(GH * GDV), act(GH * GDV), act(GH * GDK),
                   jax.ShapeDtypeStruct((1, GH * GDK), F32), jax.ShapeDtypeStruct((1, GDV), F32)],
        scratch_shapes=[pltpu.VMEM((s, GDV), BF16), pltpu.VMEM((s, GDK), F32)], compiler_params=_params(2),
    )(dog, o, states, proj, proj, proj, proj, proj, w_alpha_p, b_alpha, out_norm_g)


def _lane():
    return lax.broadcasted_iota(jnp.int32, (1, LANE), 1)


def _swap_halves(x):
    lane = _lane()
    half = MROPE // 2
    lo = (lane >= MNOPE) & (lane < MNOPE + half)
    hi = (lane >= MNOPE + half) & (lane < MQK)
    return jnp.where(lo, pltpu.roll(x, LANE - half, 1), jnp.where(hi, pltpu.roll(x, half, 1), 0.0))


def _norm96(x, g):
    r = lax.rsqrt(jnp.sum(x * x, axis=-1, keepdims=True) * (1.0 / MQK) + EPS)
    return x * r, r


def _lat_norm(proj, q_lat_g, kv_lat_g, *, name, ts=512):
    t = proj.shape[0]
    ts = min(ts, t)

    def body(cq_ref, ckv_ref, gq_ref, gk_ref, oq_ref, ok_ref):
        xq, _ = _rms(cq_ref[...], None)
        oq_ref[...] = (xq * gq_ref[...]).astype(BF16)
        xk, _ = _rms(ckv_ref[...], None)
        ok_ref[...] = (xk * gk_ref[...]).astype(BF16)

    return pl.pallas_call(
        body, name=name, grid=(t // ts,),
        in_specs=[pl.BlockSpec((ts, MQR), lambda i: (i, OFF_CQ // MQR)), pl.BlockSpec((ts, MKVR), lambda i: (i, OFF_CKV // MKVR)),
                  pl.BlockSpec((1, MQR), lambda i: (0, 0)), pl.BlockSpec((1, MKVR), lambda i: (0, 0))],
        out_specs=[pl.BlockSpec((ts, MQR), lambda i: (i, 0)), pl.BlockSpec((ts, MKVR), lambda i: (i, 0))],
        out_shape=[jax.ShapeDtypeStruct((t, MQR), BF16), jax.ShapeDtypeStruct((t, MKVR), BF16)],
        compiler_params=_params(1),
    )(proj, proj, q_lat_g, kv_lat_g)


def _lat_norm_bwd(dcqn, dckvn, proj, q_lat_g, kv_lat_g, *, name, ts=512):
    t = proj.shape[0]
    ts = min(ts, t)

    def one(d_ref, x_ref, g_ref, dx_ref, dg_ref):
        xh, r = _rms(x_ref[...], None)
        dn = d_ref[...]
        dg_ref[...] += jnp.sum(dn * xh, axis=0, keepdims=True)
        dxh = dn * g_ref[...]
        dx_ref[...] = (r * (dxh - xh * jnp.mean(dxh * xh, axis=-1, keepdims=True))).astype(BF16)

    def body(dq_ref, dk_ref, cq_ref, ckv_ref, gq_ref, gk_ref, dxq_ref, dxk_ref, dgq_ref, dgk_ref):
        @pl.when(pl.program_id(0) == 0)
        def _():
            dgq_ref[...] = jnp.zeros_like(dgq_ref)
            dgk_ref[...] = jnp.zeros_like(dgk_ref)

        one(dq_ref, cq_ref, gq_ref, dxq_ref, dgq_ref)
        one(dk_ref, ckv_ref, gk_ref, dxk_ref, dgk_ref)

    return pl.pallas_call(
        body, name=name, grid=(t // ts,),
        in_specs=[pl.BlockSpec((ts, MQR), lambda i: (i, 0)), pl.BlockSpec((ts, MKVR), lambda i: (i, 0)),
                  pl.BlockSpec((ts, MQR), lambda i: (i, OFF_CQ // MQR)), pl.BlockSpec((ts, MKVR), lambda i: (i, OFF_CKV // MKVR)),
                  pl.BlockSpec((1, MQR), lambda i: (0, 0)), pl.BlockSpec((1, MKVR), lambda i: (0, 0))],
        out_specs=[pl.BlockSpec((ts, MQR), lambda i: (i, 0)), pl.BlockSpec((ts, MKVR), lambda i: (i, 0)),
                   pl.BlockSpec((1, MQR), lambda i: (0, 0)), pl.BlockSpec((1, MKVR), lambda i: (0, 0))],
        out_shape=[jax.ShapeDtypeStruct((t, MQR), BF16), jax.ShapeDtypeStruct((t, MKVR), BF16),
                   jax.ShapeDtypeStruct((1, MQR), F32), jax.ShapeDtypeStruct((1, MKVR), F32)],
        compiler_params=_params(1),
    )(dcqn, dckvn, proj, proj, q_lat_g, kv_lat_g)


def _qk_prep(q_raw, kv, proj, cos_t, sin_t, gq, gk, *, name, ts=2048):
    t = q_raw.shape[0]
    ts = min(ts, t)

    def body(q_ref, kv_ref, kpe_ref, c_ref, s_ref, gq_ref, gk_ref, qo_ref, ko_ref, vo_ref):
        cs, sn = c_ref[...], s_ref[...]
        nope = _lane() < MNOPE
        qn, _ = _norm96(q_ref[...], None)
        qn = qn * gq_ref[...]
        qo_ref[...] = (qn * cs + _swap_halves(qn) * sn).astype(BF16)
        kvv = kv_ref[...]
        kn, _ = _norm96(jnp.where(nope, kvv, kpe_ref[...]), None)
        kn = kn * gk_ref[...]
        ko_ref[...] = (kn * cs + _swap_halves(kn) * sn).astype(BF16)
        vo_ref[...] = jnp.where(nope, pltpu.roll(kvv, MNOPE, 1), 0.0).astype(BF16)

    hd = pl.BlockSpec((ts, LANE), lambda i, h: (i, h))
    shared = lambda col: pl.BlockSpec((ts, LANE), lambda i, h: (i, col))
    gain = pl.BlockSpec((1, LANE), lambda i, h: (0, 0))
    out = jax.ShapeDtypeStruct((t, MH * LANE), BF16)
    return pl.pallas_call(
        body, name=name, grid=(t // ts, MH),
        in_specs=[hd, hd, shared(OFF_KPE // LANE), shared(0), shared(0), gain, gain],
        out_specs=[hd, hd, hd], out_shape=[out, out, out], compiler_params=_params(2),
    )(q_raw, kv, proj, cos_t, sin_t, gq, gk)


def _qk_prep_bwd(dq, dk, dv, q_raw, kv, proj, cos_t, sin_t, gq, gk, *, name, ts=2048):
    t = q_raw.shape[0]
    ts = min(ts, t)

    def norm_bwd(dy, x, g, dg_ref):
        xh, r = _norm96(x, None)
        dg_ref[...] += jnp.sum(dy * xh, axis=0, keepdims=True)
        dxh = dy * g
        return r * (dxh - xh * (jnp.sum(dxh * xh, axis=-1, keepdims=True) * (1.0 / MQK)))

    def body(dq_ref, dk_ref, dv_ref, q_ref, kv_ref, kpe_ref, c_ref, s_ref, gq_ref, gk_ref,
             dqr_ref, dkv_ref, dkpe_ref, dgq_ref, dgk_ref):
        i, h = pl.program_id(0), pl.program_id(1)

        @pl.when(h == 0)
        def _():
            dkpe_ref[...] = jnp.zeros_like(dkpe_ref)

        @pl.when((h == 0) & (i == 0))
        def _():
            dgq_ref[...] = jnp.zeros_like(dgq_ref)
            dgk_ref[...] = jnp.zeros_like(dgk_ref)

        cs, sn = c_ref[...], s_ref[...]
        lane = _lane()
        nope = lane < MNOPE
        dqv = dq_ref[...]
        dqn = dqv * cs + _swap_halves(dqv * sn)
        dqr_ref[...] = norm_bwd(dqn, q_ref[...], gq_ref[...], dgq_ref).astype(BF16)
        dkv_ = dk_ref[...]
        dkn = dkv_ * cs + _swap_halves(dkv_ * sn)
        kvv = kv_ref[...]
        dkr = norm_bwd(dkn, jnp.where(nope, kvv, kpe_ref[...]), gk_ref[...], dgk_ref)
        dkv_ref[...] = jnp.where(nope, dkr, pltpu.roll(dv_ref[...], MNOPE, 1)).astype(BF16)
        dkpe_ref[...] += jnp.where((lane >= MNOPE) & (lane < MQK), dkr, 0.0)

    hd = pl.BlockSpec((ts, LANE), lambda i, h: (i, h))
    shared = lambda col: pl.BlockSpec((ts, LANE), lambda i, h: (i, col))
    gain = pl.BlockSpec((1, LANE), lambda i, h: (0, 0))
    out = jax.ShapeDtypeStruct((t, MH * LANE), BF16)
    return pl.pallas_call(
        body, name=name, grid=(t // ts, MH),
        in_specs=[hd, hd, hd, hd, hd, shared(OFF_KPE // LANE), shared(0), shared(0), gain, gain],
        out_specs=[hd, hd, shared(0), gain, gain],
        out_shape=[out, out, jax.ShapeDtypeStruct((t, LANE), F32), jax.ShapeDtypeStruct((1, LANE), F32),
                   jax.ShapeDtypeStruct((1, LANE), F32)],
        compiler_params=_params(2),
    )(dq, dk, dv, q_raw, kv, proj, cos_t, sin_t, gq, gk)


_NT = (((1,), (1,)), ((), ()))
_TN = (((0,), (0,)), ((), ()))


SOFTMAX_SCALE = MQK ** -0.5
Q_PRESCALE = SOFTMAX_SCALE * float(np.log2(np.e))


def _attn_weights(q, k_ref, lo, tq):
    row = lax.broadcasted_iota(jnp.int32, (tq, tq), 0) // CHUNK
    col = lax.broadcasted_iota(jnp.int32, (tq, tq), 1) // CHUNK
    sd = lax.dot_general(q, k_ref[pl.ds(lo, tq), :], _NT, preferred_element_type=F32)
    sd = jnp.where(col <= row, sd, -1e30)
    m = jnp.max(sd, axis=-1, keepdims=True)
    if lo:
        so = lax.dot_general(q, k_ref[pl.ds(0, lo), :], _NT, preferred_element_type=F32)
        m = jnp.maximum(m, jnp.max(so, axis=-1, keepdims=True))
        eo = jnp.exp2(so - m)
        ed = jnp.exp2(sd - m)
        return eo, ed, 1.0 / (jnp.sum(eo, axis=-1, keepdims=True) + jnp.sum(ed, axis=-1, keepdims=True))
    ed = jnp.exp2(sd - m)
    return None, ed, 1.0 / jnp.sum(ed, axis=-1, keepdims=True)


def _attn_fwd(q, k, v, *, name, tq=256):
    bsz, s, _ = q.shape
    tq = min(tq, s)

    def body(q_ref, k_ref, v_ref, o_ref):
        for i in range(s // tq):
            lo = i * tq
            eo, ed, inv = _attn_weights(q_ref[pl.ds(lo, tq), :], k_ref, lo, tq)
            o = jnp.dot(ed.astype(BF16), v_ref[pl.ds(lo, tq), :], preferred_element_type=F32)
            if lo:
                o += jnp.dot(eo.astype(BF16), v_ref[pl.ds(0, lo), :], preferred_element_type=F32)
            o_ref[pl.ds(lo, tq), :] = (o * inv).astype(BF16)

    spec = pl.BlockSpec((None, s, LANE), lambda b, h: (b, 0, h))
    return pl.pallas_call(
        body, name=name, grid=(bsz, MH), in_specs=[spec, spec, spec], out_specs=spec,
        out_shape=jax.ShapeDtypeStruct((bsz, s, MH * LANE), BF16), compiler_params=_params(2),
    )(q, k, v)


def _attn_bwd(q, k, v, do, *, name, tq=256):
    bsz, s, _ = q.shape
    tq = min(tq, s)

    def body(q_ref, k_ref, v_ref, do_ref, dq_ref, dk_ref, dv_ref):
        dk_ref[...] = jnp.zeros_like(dk_ref)
        dv_ref[...] = jnp.zeros_like(dv_ref)
        for i in range(s // tq):
            lo = i * tq
            here, before = pl.ds(lo, tq), pl.ds(0, lo)
            qv, dov = q_ref[here, :], do_ref[here, :]
            eo, ed, inv = _attn_weights(qv, k_ref, lo, tq)
            do_n = (dov.astype(F32) * inv).astype(BF16)
            dv_ref[here, :] += lax.dot_general(ed.astype(BF16), do_n, _TN, preferred_element_type=F32)
            dpd = lax.dot_general(dov, v_ref[here, :], _NT, preferred_element_type=F32)
            delta = jnp.sum(dpd * ed, axis=-1, keepdims=True)
            if lo:
                dv_ref[before, :] += lax.dot_general(eo.astype(BF16), do_n, _TN, preferred_element_type=F32)
                dpo = lax.dot_general(dov, v_ref[before, :], _NT, preferred_element_type=F32)
                delta += jnp.sum(dpo * eo, axis=-1, keepdims=True)
            delta = delta * inv
            r = inv * SOFTMAX_SCALE
            dsd = (ed * (dpd - delta) * r).astype(BF16)
            dq = jnp.dot(dsd, k_ref[here, :], preferred_element_type=F32)
            dk_ref[here, :] += lax.dot_general(dsd, qv, _TN, preferred_element_type=F32)
            if lo:
                dso = (eo * (dpo - delta) * r).astype(BF16)
                dq += jnp.dot(dso, k_ref[before, :], preferred_element_type=F32)
                dk_ref[before, :] += lax.dot_general(dso, qv, _TN, preferred_element_type=F32)
            dq_ref[here, :] = dq
        dk_ref[...] = dk_ref[...] * (1.0 / Q_PRESCALE)

    spec = pl.BlockSpec((None, s, LANE), lambda b, h: (b, 0, h))
    out = jax.ShapeDtypeStruct((bsz, s, MH * LANE), F32)
    return pl.pallas_call(
        body, name=name, grid=(bsz, MH), in_specs=[spec] * 4, out_specs=[spec] * 3, out_shape=[out, out, out],
        compiler_params=_params(2),
    )(q, k, v, do)


def _adamw(w, g, m, v, *, name, tr=256, by_cols=False):
    rows, cols = w.shape
    tr = _tile_rows(rows, tr)

    def body(w_ref, g_ref, m_ref, v_ref, d_ref, nm_ref, nv_ref):
        d_ref[...], nm_ref[...], nv_ref[...] = _adamw_update(w_ref[...], g_ref[...], m_ref[...], v_ref[...])

    spec = pl.BlockSpec((rows, LANE), lambda i: (0, i)) if by_cols else pl.BlockSpec((tr, cols), lambda i: (i, 0))
    out = jax.ShapeDtypeStruct((rows, cols), F32)
    return pl.pallas_call(body, name=name, grid=(cols // LANE if by_cols else rows // tr,), in_specs=[spec] * 4,
                          out_specs=[spec] * 3, out_shape=[out, out, out], compiler_params=_params(1))(w, g, m, v)


def _tile_rows(rows, target):
    if rows <= target:
        return rows
    best = 8
    for t in range(8, target + 1, 8):
        if rows % t == 0:
            best = t
    return best


def _adamw_update(w, g, m, v):
    nm = ADAM_B1 * m + (1.0 - ADAM_B1) * g
    nv = ADAM_B2 * v + (1.0 - ADAM_B2) * (g * g)
    m_hat = nm / (1.0 - ADAM_B1 ** ADAM_STEP)
    v_hat = nv / (1.0 - ADAM_B2 ** ADAM_STEP)
    return -ADAM_LR * (m_hat / (jnp.sqrt(v_hat) + ADAM_EPS) + ADAM_WD * w), nm, nv


def _adamw_halves(w, m, v, mine, theirs, sel, *, name, tr=256):
    rows, cols = w.shape
    tr = _tile_rows(rows // 2, tr)
    nh = rows // 2 // tr

    def body(sel_ref, w_ref, m_ref, v_ref, mine_ref, theirs_ref, g_ref, d_ref, nm_ref, nv_ref):
        lower = pl.program_id(0) < nh
        south = sel_ref[0] == 0
        gv = jnp.where(lower == south, mine_ref[...], theirs_ref[...])
        g_ref[...] = gv
        d_ref[...], nm_ref[...], nv_ref[...] = _adamw_update(w_ref[...], gv, m_ref[...], v_ref[...])

    full = pl.BlockSpec((tr, cols), lambda i, sel_ref: (i, 0))
    half = pl.BlockSpec((tr, cols), lambda i, sel_ref: (i % nh, 0))
    out = jax.ShapeDtypeStruct((rows, cols), F32)
    return pl.pallas_call(
        body, name=name, out_shape=[out] * 4, compiler_params=_params(1),
        grid_spec=pltpu.PrefetchScalarGridSpec(num_scalar_prefetch=1, grid=(rows // tr,),
                                               in_specs=[full, full, full, half, half], out_specs=[full] * 4),
    )(sel, w, m, v, mine, theirs)


def _pair_add(x, sib, sel, *, name, tr=256):
    n, _, rows, cols = x.shape
    tr = _tile_rows(rows, tr)

    def body(sel_ref, x_ref, s_ref, o_ref):
        o_ref[...] = (x_ref[...] + s_ref[...]).astype(BF16)

    spec = pl.BlockSpec((None, tr, cols), lambda j, i, sel_ref: (j, i, 0))
    return pl.pallas_call(
        body, name=name, out_shape=jax.ShapeDtypeStruct((n, rows, cols), BF16), compiler_params=_params(2),
        grid_spec=pltpu.PrefetchScalarGridSpec(
            num_scalar_prefetch=1, grid=(n, rows // tr),
            in_specs=[pl.BlockSpec((None, None, tr, cols), lambda j, i, sel_ref: (j, sel_ref[0], i, 0)), spec],
            out_specs=spec),
    )(sel, x, sib)


def _chip_sum(pair, recv, sel, *, name, tr=256):
    _, rows, cols = pair.shape
    tr = _tile_rows(rows, tr)

    def body(sel_ref, p_ref, r_ref, o_ref):
        acc = p_ref[...].astype(F32)
        for k in range(3):
            acc = acc + r_ref[k].astype(F32)
        o_ref[...] = acc

    return pl.pallas_call(
        body, name=name, out_shape=jax.ShapeDtypeStruct((rows, cols), F32), compiler_params=_params(1),
        grid_spec=pltpu.PrefetchScalarGridSpec(
            num_scalar_prefetch=1, grid=(rows // tr,),
            in_specs=[pl.BlockSpec((None, tr, cols), lambda i, sel_ref: (sel_ref[0], i, 0)),
                      pl.BlockSpec((3, tr, cols), lambda i, sel_ref: (0, i, 0))],
            out_specs=pl.BlockSpec((tr, cols), lambda i, sel_ref: (i, 0))),
    )(sel, pair, recv)


def _me():
    return lax.axis_index("x"), lax.axis_index("y"), lax.axis_index("c")


def _flip(pos, bits):
    x, y, c = pos
    return (x ^ bits[0] if bits[0] else x, y ^ bits[1] if bits[1] else y, c ^ bits[2] if bits[2] else c)


ANY = pl.BlockSpec(memory_space=pl.ANY)


def _all_gather8(xs, *, name):
    n = len(xs)
    flips = [((k >> 2) & 1, (k >> 1) & 1, k & 1) for k in range(1, 8)]

    def body(*refs):
        x_refs, out_refs, (send_sems, recv_sems, local_sems) = refs[:n], refs[n:2 * n], refs[2 * n:]
        me = _me()
        slot = lambda p: 4 * p[0] + 2 * p[1] + p[2]
        copies = []
        for i in range(n):
            mine = pltpu.make_async_copy(x_refs[i], out_refs[i].at[slot(me)], local_sems.at[i])
            mine.start()
            copies.append(mine)
            for k, f in enumerate(flips):
                peer = _flip(me, f)
                sems = dict(send_sem=send_sems.at[7 * i + k], recv_sem=recv_sems.at[7 * i + k], device_id=peer,
                            device_id_type=MESH)
                cp = pltpu.make_async_remote_copy(src_ref=x_refs[i], dst_ref=out_refs[i].at[slot(me)], **sems)
                cp.start()
                copies.append(cp)
                copies.append(pltpu.make_async_remote_copy(src_ref=x_refs[i], dst_ref=out_refs[i].at[slot(peer)], **sems))
        for i in range(n):
            base = i * 15
            copies[base].wait()
            for k in range(7):
                copies[base + 1 + 2 * k].wait_send()
                copies[base + 2 + 2 * k].wait_recv()

    outs = pl.pallas_call(
        body, name=name, in_specs=[ANY] * n, out_specs=[ANY] * n,
        out_shape=[jax.ShapeDtypeStruct((8, *x.shape), x.dtype) for x in xs],
        scratch_shapes=[pltpu.SemaphoreType.DMA((7 * n,)), pltpu.SemaphoreType.DMA((7 * n,)),
                        pltpu.SemaphoreType.DMA((n,))])(*xs)
    return list(outs)


CHIP_FLIPS = [(1, 0, 0), (0, 1, 0), (1, 1, 0)]


def _chip():
    return 2 * lax.axis_index("x") + lax.axis_index("y")


def _pair_swap_halves(xs, *, name):
    n = len(xs)

    def body(*refs):
        x_refs, out_refs, (send_sems, recv_sems) = refs[:n], refs[n:2 * n], refs[2 * n:]
        me = _me()
        sib = _flip(me, (0, 0, 1))
        copies = [pltpu.make_async_remote_copy(src_ref=x_refs[i].at[:, 1 - me[2]], dst_ref=out_refs[i],
                                               send_sem=send_sems.at[i], recv_sem=recv_sems.at[i], device_id=sib,
                                               device_id_type=MESH) for i in range(n)]
        for cp in copies:
            cp.start()
        for cp in copies:
            cp.wait()

    return pl.pallas_call(
        body, name=name, in_specs=[ANY] * n, out_specs=[ANY] * n,
        out_shape=[jax.ShapeDtypeStruct((x.shape[0], *x.shape[2:]), x.dtype) for x in xs],
        scratch_shapes=[pltpu.SemaphoreType.DMA((n,)), pltpu.SemaphoreType.DMA((n,))])(*xs)


def _pair_swap(hs, *, name):
    n = len(hs)

    def body(*refs):
        h_refs, out_refs, (send_sems, recv_sems) = refs[:n], refs[n:2 * n], refs[2 * n:]
        sib = _flip(_me(), (0, 0, 1))
        copies = [pltpu.make_async_remote_copy(src_ref=h_refs[i], dst_ref=out_refs[i], send_sem=send_sems.at[i],
                                               recv_sem=recv_sems.at[i], device_id=sib, device_id_type=MESH)
                  for i in range(n)]
        for cp in copies:
            cp.start()
        for cp in copies:
            cp.wait()

    return pl.pallas_call(
        body, name=name, in_specs=[ANY] * n, out_specs=[ANY] * n,
        out_shape=[jax.ShapeDtypeStruct(h.shape, h.dtype) for h in hs],
        scratch_shapes=[pltpu.SemaphoreType.DMA((n,)), pltpu.SemaphoreType.DMA((n,))])(*hs)


HBM = pl.BlockSpec(memory_space=pltpu.HBM)
SEM = pl.BlockSpec(memory_space=pltpu.SEMAPHORE)
EFFECT = pltpu.SideEffectType.DATAFLOW_SIDE_EFFECTING


def _plan_copies(plan, refs, send_sems, recv_sems):
    return [pltpu.make_async_remote_copy(src_ref=src, dst_ref=dst, send_sem=send_sems.at[k], recv_sem=recv_sems.at[k],
                                         device_id=to, device_id_type=MESH) for k, (src, dst, to) in enumerate(plan(refs))]


def _rdma_start(arrays, n_copies, plan, deps, *, name):
    n, nd = len(arrays), len(deps)

    def body(*refs):
        for cp in _plan_copies(plan, refs[:n], refs[n + nd], refs[n + nd + 1]):
            cp.start()
        refs[-1][...] = jnp.zeros_like(refs[-1])

    outs = pl.pallas_call(
        body, name=name,
        out_shape=(pltpu.SemaphoreType.DMA((n_copies,)), pltpu.SemaphoreType.DMA((n_copies,)),
                   *[pltpu.HBM(a.shape, a.dtype) for a in arrays], jax.ShapeDtypeStruct((8, LANE), F32)),
        in_specs=[HBM] * n + [ANY] * nd, out_specs=(SEM, SEM, *[HBM] * n, pl.BlockSpec(memory_space=pltpu.VMEM)),
        input_output_aliases={i: i + 2 for i in range(n)}, compiler_params=pltpu.CompilerParams(has_side_effects=EFFECT),
    )(*[pltpu.with_memory_space_constraint(a, pltpu.HBM) for a in arrays], *deps)
    return outs[0], outs[1], list(outs[2:2 + n]), outs[-1]


def _rdma_wait(send_sems, recv_sems, arrays, plan, after, *, name):
    n = len(arrays)

    def body(*refs):
        for cp in _plan_copies(plan, refs[:n], refs[n], refs[n + 1]):
            cp.wait_send()
            cp.wait_recv()

    return list(pl.pallas_call(
        body, name=name, out_shape=tuple(pltpu.HBM(a.shape, a.dtype) for a in arrays),
        in_specs=[HBM] * n + [SEM, SEM, ANY], out_specs=tuple([HBM] * n), input_output_aliases={i: i for i in range(n)},
        compiler_params=pltpu.CompilerParams(has_side_effects=EFFECT),
    )(*arrays, send_sems, recv_sems, after))


def _gather_plan(n):
    def plan(refs):
        me = _me()
        slot = 2 * me[0] + me[1]
        return [(refs[i].at[me[2]], refs[n + i].at[slot, me[2]], _flip(me, f)) for i in range(n) for f in CHIP_FLIPS]
    return plan


def _scatter_plan(n):
    def plan(refs):
        me = _me()
        out = []
        for i in range(n):
            for k, f in enumerate(CHIP_FLIPS):
                peer = _flip(me, f)
                out.append((refs[i].at[2 * peer[0] + peer[1]], refs[n + i].at[k], peer))
        return out
    return plan


def _pair_fill(lands, *, name):
    n = len(lands)

    def body(*refs):
        in_refs, (send_sems, recv_sems) = refs[:n], refs[2 * n:]
        me = _me()
        sib = _flip(me, (0, 0, 1))
        copies = []
        for i in range(n):
            for k, f in enumerate(CHIP_FLIPS):
                peer = _flip(me, f)
                slot = 2 * peer[0] + peer[1]
                mine, theirs = in_refs[i].at[slot, me[2]], in_refs[i].at[slot, 1 - me[2]]
                cp = pltpu.make_async_remote_copy(src_ref=mine, dst_ref=mine, send_sem=send_sems.at[3 * i + k],
                                                  recv_sem=recv_sems.at[3 * i + k], device_id=sib, device_id_type=MESH)
                cp.start()
                copies.append((cp, pltpu.make_async_remote_copy(
                    src_ref=mine, dst_ref=theirs, send_sem=send_sems.at[3 * i + k], recv_sem=recv_sems.at[3 * i + k],
                    device_id=sib, device_id_type=MESH)))
        for cp, arrival in copies:
            arrival.wait_recv()
            cp.wait_send()

    return list(pl.pallas_call(
        body, name=name, in_specs=[ANY] * n, out_specs=[ANY] * n,
        out_shape=[jax.ShapeDtypeStruct(a.shape, a.dtype) for a in lands], input_output_aliases={i: i for i in range(n)},
        scratch_shapes=[pltpu.SemaphoreType.DMA((3 * n,)), pltpu.SemaphoreType.DMA((3 * n,))])(*lands))


def _own_and_landed(lands, xs):
    chip = _chip()
    return [[jnp.where(chip == j, x, o.reshape(4, *x.shape)[j]) for j in range(4)] for o, x in zip(lands, xs)]


BIG = (("w_in", (D, IN_WIDTH // 4), 1), ("gla_w_o", (D // 4, D), 0), ("mla_w_uq", (MQR, MH * MQK // 4), 1),
       ("mla_w_ukv", (MKVR, MH * (MNOPE + MVD) // 4), 1), ("mla_w_o", (D // 4, D), 0), ("w_out", (D // 4, D), 0),
       ("mlp_w1", (D, DFF // 4), 1), ("mlp_w2", (DFF // 4, D), 0))
ADA_SHARD = (D, 6 * D // 4)
SMALL = (("b_ada", 6 * D), ("norm1_g", D), ("b_merge", 2 * D), ("gla_b_alpha", GH * GDK), ("gla_out_norm_g", GDV),
         ("mla_q_lat_g", MQR), ("mla_kv_lat_g", MKVR), ("mla_qn_g", MQK), ("mla_kn_g", MQK), ("norm2_g", D))


W_IN_SEGMENTS = ((0, 3072, OFF_Q), (3072, 3088, OFF_A), (3088, 3344, OFF_CQ), (3344, 3472, OFF_CKV),
                 (3472, 3504, OFF_KPE + MNOPE), (3504, 5552, OFF_MA))
W_IN_SPLIT = OFF_MA
SMALL_ROWS, SMALL_COLS = 32, 2 * D
W_ALPHA_ROW = 16
SMALL_RED = tuple((n, k) for n, k in SMALL if n != "b_ada")


def _pack_small(grads, d_w_alpha, *, name):
    def body(*refs):
        g_refs, wa_ref, out_ref = refs[:-2], refs[-2], refs[-1]
        out_ref[...] = jnp.zeros_like(out_ref)
        for i, ((_, k), g_ref) in enumerate(zip(SMALL_RED, g_refs)):
            out_ref[i:i + 1, 0:k] = g_ref[...]
        out_ref[W_ALPHA_ROW:W_ALPHA_ROW + GLR, 0:GH * GDK] = wa_ref[...]

    return pl.pallas_call(body, name=name, out_shape=jax.ShapeDtypeStruct((SMALL_ROWS, SMALL_COLS), F32))(*grads, d_w_alpha)


def _small_update(gathered, dmod_all, sel, wmv, *, name):
    names = [n for n, _ in SMALL] + ["gla_w_alpha"]
    n_par = len(names)

    def body(sel_ref, g_ref, dmod_ref, *refs):
        in_refs, out_refs, acc = refs[:3 * n_par], refs[3 * n_par:-1], refs[-1]
        total = g_ref[0]
        for j in range(1, 8):
            total = total + g_ref[j]
        acc[...] = total
        row = {n: i for i, (n, _) in enumerate(SMALL_RED)}
        for p, name_p in enumerate(names):
            w_ref, m_ref, v_ref = in_refs[3 * p:3 * p + 3]
            if name_p == "b_ada":
                gv = jnp.sum(dmod_ref[...], axis=0, keepdims=True)
            elif name_p == "gla_w_alpha":
                gv = jnp.zeros((GLR, GDK), F32)
                for j in range(4):
                    blk = acc[W_ALPHA_ROW:W_ALPHA_ROW + GLR, j * GDK:(j + 1) * GDK]
                    gv = gv + jnp.where(sel_ref[0] == j, blk, 0.0)
            else:
                gv = acc[row[name_p]:row[name_p] + 1, 0:w_ref.shape[1]]
            o = out_refs[4 * p:4 * p + 4]
            o[0][...] = gv
            o[1][...], o[2][...], o[3][...] = _adamw_update(w_ref[...], gv, m_ref[...], v_ref[...])

    flat = [a for t in wmv for a in t]
    out_shape = [jax.ShapeDtypeStruct(t[0].shape, F32) for t in wmv for _ in range(4)]
    vmem = pl.BlockSpec(memory_space=pltpu.VMEM)
    outs = pl.pallas_call(
        body, name=name, out_shape=out_shape, in_specs=[pl.BlockSpec(memory_space=pltpu.SMEM), vmem, vmem] + [vmem] * len(flat),
        out_specs=[vmem] * len(out_shape), scratch_shapes=[pltpu.VMEM((SMALL_ROWS, SMALL_COLS), F32)],
    )(sel, gathered, dmod_all, *flat)
    return {n: tuple(outs[4 * p:4 * p + 4]) for p, n in enumerate(names)}


def _full_weights(gathered):
    w = {name: jnp.concatenate(gathered[name], axis=axis) for name, _, axis in BIG if name in gathered and name != "w_in"}
    if "w_in" in gathered:
        shards = gathered["w_in"]
        zeros = lambda n: [jnp.zeros((D, n), shards[0].dtype)]

        def cols(a, b):
            width = IN_WIDTH // 4
            return [shards[j][:, max(a, j * width) - j * width:min(b, (j + 1) * width) - j * width]
                    for j in range(4) if max(a, j * width) < min(b, (j + 1) * width)]

        parts = []
        for a, b, at in sorted(W_IN_SEGMENTS, key=lambda seg: seg[2]):
            have = sum(p.shape[1] for p in parts)
            parts += (zeros(at - have) if at > have else []) + cols(a, b)
        w["w_in"] = jnp.concatenate(parts + zeros(PW - sum(p.shape[1] for p in parts)), axis=1)
    if "mla_w_uq" in w:
        w["mla_w_uq"] = jnp.pad(w["mla_w_uq"].reshape(MQR, MH, MQK), ((0, 0), (0, 0), (0, LANE - MQK))).reshape(MQR, MH * LANE)
    if "mla_w_o" in w:
        w["mla_w_o"] = jnp.pad(w["mla_w_o"].reshape(MH, MVD, D), ((0, 0), (0, LANE - MVD), (0, 0))).reshape(MH * LANE, D)
    return w


def _grad_slots(g):
    g = dict(g)
    out = {}
    if "w_in" in g:
        g_lo, g_hi = g.pop("w_in")
        take = lambda at, lo, hi: g_lo[:, at + lo:at + hi] if at < W_IN_SPLIT else g_hi[:, at - W_IN_SPLIT + lo:at - W_IN_SPLIT + hi]
        width = IN_WIDTH // 4
        slots = []
        for j in range(4):
            lo, hi = j * width, (j + 1) * width
            slots.append(jnp.concatenate([take(at, max(lo, a) - a, min(hi, b) - a)
                                          for a, b, at in W_IN_SEGMENTS if max(lo, a) < min(hi, b)], axis=1))
        out["w_in"] = jnp.stack(slots).reshape(4, 2, D // 2, width)
    if "mla_w_uq" in g:
        g["mla_w_uq"] = g["mla_w_uq"].reshape(MQR, MH, LANE)[:, :, :MQK].reshape(MQR, MH * MQK)
    if "mla_w_o" in g:
        g["mla_w_o"] = g["mla_w_o"].reshape(MH, LANE, D)[:, :MVD].reshape(MH * MVD, D)
    for name, (rows, cols), axis in BIG:
        if name not in g:
            continue
        a = g[name]
        a = a.reshape(4, rows, cols) if axis == 0 else jnp.transpose(a.reshape(rows, 4, cols), (1, 0, 2))
        out[name] = a.reshape(4, 2, rows // 2, cols)
    return out


def _rope_tables(positions):
    freqs = ROPE_THETA ** (-jnp.arange(0, MROPE, 2, dtype=F32) / MROPE)
    lane = np.arange(LANE)
    in_rope = (lane >= MNOPE) & (lane < MQK)
    freq_lane = jnp.where(in_rope, freqs[(lane - MNOPE) % (MROPE // 2)], 0.0)
    sign = np.where(in_rope, np.where(lane < MNOPE + MROPE // 2, -1.0, 1.0), 0.0).astype(np.float32)
    ang = positions.astype(F32).reshape(-1, 1) * freq_lane[None, :]
    return jnp.cos(ang), jnp.sin(ang) * sign[None, :]


def _local_step(x, positions, mod, target, w, small, more_weights=None, on_grads=None):
    kept = {}
    if on_grads is None:
        on_grads = lambda tag, grads: kept.update(grads)
    bsz, s, _ = x.shape
    t = bsz * s
    tt = _tile(t, 1024)
    shift1, scale1, gate1, shift2, scale2, gate2 = [mod[:, None, i * D:(i + 1) * D] for i in range(6)]
    cos_t, sin_t = _rope_tables(positions)
    w_alpha_p = jnp.pad(small["gla_w_alpha"], ((0, LANE - GLR), (0, 0)))
    gq = jnp.pad(small["mla_qn_g"], ((0, 0), (0, LANE - MQK)))
    gk = jnp.pad(small["mla_kn_g"], ((0, 0), (0, LANE - MQK)))
    flat2 = lambda a: a.reshape(t, a.shape[-1])
    bsd = lambda a: a.reshape(bsz, s, a.shape[-1])

    h = _norm_mod(x, small["norm1_g"], scale1, shift1, name="norm1")
    if callable(w):
        w = w(h)
    proj = _mm(flat2(h), w["w_in"], name="proj", tn=1152)
    proj3 = bsd(proj)
    o, o_gated, states = _gla_fwd(proj3, w_alpha_p, small["gla_b_alpha"], small["gla_out_norm_g"], name="gla_fwd")
    if more_weights is not None:
        w = {**w, **more_weights(o_gated)}
    y_a = _mm(flat2(o_gated), w["gla_w_o"], name="gla_out")
    cq_n, ckv_n = _lat_norm(proj, small["mla_q_lat_g"], small["mla_kv_lat_g"], name="lat_norm")
    q_raw = _mm(cq_n, w["mla_w_uq"], name="mla_uq")
    kv = _mm(ckv_n, w["mla_w_ukv"], name="mla_ukv")
    qf, kf, vf = _qk_prep(q_raw, kv, proj, cos_t, sin_t, gq * Q_PRESCALE, gk, name="qk_prep")
    o_attn = _attn_fwd(bsd(qf), bsd(kf), bsd(vf), name="attn_fwd")
    y_b = _mm(flat2(o_attn), w["mla_w_o"], name="mla_out")
    mixed_in = _merge_fwd(proj3, small["b_merge"], bsd(y_a), bsd(y_b), name="merge_fwd")
    mixed = _mm(flat2(mixed_in), w["w_out"], name="w_out")
    x1, h2 = _resid_norm_mod(x, bsd(mixed), gate1, small["norm2_g"], scale2, shift2, name="norm2")

    def sqrelu(acc, ex, outs):
        r = jnp.maximum(acc, 0.0)
        outs[0][...] = (r * r).astype(BF16)

    r = _mm(flat2(h2), w["mlp_w1"], name="mlp1", epilogue=sqrelu, out_shape=jax.ShapeDtypeStruct((t, DFF), BF16),
            out_specs=_tile_spec(tt, 1024))
    ff = _mm(r, w["mlp_w2"], name="mlp2")
    dy, dff, dgate2, loss_part = _loss_head(x1, bsd(ff), gate2, target, name="loss_head")

    g = {}

    def relu2_bwd(acc, ex, outs):
        outs[0][...] = (acc * (2.0 * jnp.sqrt(ex[0][...].astype(F32)))).astype(BF16)

    dff2 = flat2(dff)
    da1 = _mm(dff2, w["mlp_w2"], tb=True, name="mlp2_dx", epilogue=relu2_bwd, extras=(r,),
              extra_specs=(_tile_spec(tt, 1024),), out_shape=jax.ShapeDtypeStruct((t, DFF), BF16),
              out_specs=_tile_spec(tt, 1024))
    g["mlp_w2"] = _mm(r, dff2, ta=True, name="mlp2_dw")
    dh2 = _mm(da1, w["mlp_w1"], tb=True, name="mlp1_dx")
    g["mlp_w1"] = _mm(flat2(h2), da1, ta=True, name="mlp1_dw")
    token = on_grads("mlp", {n: g.pop(n) for n in ("mlp_w2", "mlp_w1")})
    if token is not None:
        gate1 = gate1 + token[0, 0]
    dx1, dscale2, dshift2, dg2, dgate1, dmixed = _norm_mod_bwd(
        bsd(dh2), x1, dy, small["norm2_g"], scale2, gate1, bsd(mixed), name="norm2_bwd")
    dmixed2 = flat2(dmixed)
    dmi = _mm(dmixed2, w["w_out"], tb=True, name="w_out_dx")
    g["w_out"] = _mm(flat2(mixed_in), dmixed2, ta=True, name="w_out_dw")
    dy_a, dy_b, dl_a, dl_b, db_a, db_b = _merge_bwd(bsd(dmi), proj3, small["b_merge"], bsd(y_a), bsd(y_b), name="merge_bwd")
    dy_a2, dy_b2 = flat2(dy_a), flat2(dy_b)
    dog = _mm(dy_a2, w["gla_w_o"], tb=True, name="gla_out_dx")
    g["gla_w_o"] = _mm(flat2(o_gated), dy_a2, ta=True, name="gla_out_dw")
    dq_g, dk_g, dv_g, dg_g, dlog, db_alpha, d_ong = _gla_bwd(
        bsd(dog), o, states, proj3, w_alpha_p, small["gla_b_alpha"], small["gla_out_norm_g"], name="gla_bwd")
    dlog2 = flat2(dlog)
    da_p = _mm(dlog2, w_alpha_p, tb=True, out_dtype=BF16, name="alpha_dx")
    d_w_alpha = _mm(proj[:, OFF_A:OFF_A + LANE], dlog2, ta=True, name="alpha_dw")[:GLR]
    do_attn = _mm(dy_b2, w["mla_w_o"], tb=True, out_dtype=BF16, name="mla_out_dx")
    g["mla_w_o"] = _mm(flat2(o_attn), dy_b2, ta=True, name="mla_out_dw")
    dqf, dkf, dvf = _attn_bwd(bsd(qf), bsd(kf), bsd(vf), bsd(do_attn), name="attn_bwd")
    dq_raw, dkv, dkpe, dgq, dgk = _qk_prep_bwd(flat2(dqf), flat2(dkf), flat2(dvf), q_raw, kv, proj, cos_t, sin_t, gq, gk,
                                                name="qk_prep_bwd")
    dcq_n = _mm(dq_raw, w["mla_w_uq"], tb=True, name="mla_uq_dx")
    g["mla_w_uq"] = _mm(cq_n, dq_raw, ta=True, name="mla_uq_dw")
    dckv_n = _mm(dkv, w["mla_w_ukv"], tb=True, name="mla_ukv_dx")
    g["mla_w_ukv"] = _mm(ckv_n, dkv, ta=True, name="mla_ukv_dw")
    token = on_grads("mix", {n: g.pop(n) for n in ("w_out", "gla_w_o", "mla_w_o", "mla_w_uq", "mla_w_ukv")})
    q_lat_g = small["mla_q_lat_g"] if token is None else small["mla_q_lat_g"] + token[0:1, 0:1]
    dcq, dckv, dg_qlat, dg_kvlat = _lat_norm_bwd(dcq_n, dckv_n, proj, q_lat_g, small["mla_kv_lat_g"],
                                                  name="lat_norm_bwd")
    pieces = [(flat2(dq_g), OFF_Q), (flat2(dk_g), OFF_K), (flat2(dv_g), OFF_V), (flat2(dg_g), OFF_G),
              (flat2(dl_a), OFF_MA), (flat2(dl_b), OFF_MB), (dcq, OFF_CQ), (dckv, OFF_CKV), (da_p, OFF_A), (dkpe, OFF_KPE)]
    hb = flat2(h)
    g_w_in = (_pieces_dw(hb, [p for p, off in pieces if off < W_IN_SPLIT], name="proj_dw_a"),
              _pieces_dw(hb, [p for p, off in pieces if off >= W_IN_SPLIT], name="proj_dw_b"))
    token = on_grads("in", {"w_in": g_w_in})
    after = jnp.zeros((8, LANE), F32) if token is None else token
    dh = _pieces_dx(pieces, w["w_in"], after, name="proj_dx")
    grad_x, dscale1, dshift1, dg1 = _norm_mod_bwd(bsd(dh), x, dx1, small["norm1_g"], scale1, name="norm1_bwd")

    dmod = jnp.concatenate([dshift1, dscale1, dgate1, dshift2, dscale2, dgate2], axis=-1).reshape(bsz, 6 * D)
    gs = {"norm1_g": dg1, "b_merge": jnp.concatenate([db_a, db_b], axis=1), "gla_b_alpha": db_alpha,
          "gla_out_norm_g": d_ong, "mla_q_lat_g": dg_qlat, "mla_kv_lat_g": dg_kvlat, "mla_qn_g": dgq[:, :MQK],
          "mla_kn_g": dgk[:, :MQK], "norm2_g": dg2}
    return loss_part[0, 0], grad_x, dmod, {**kept, **g}, gs, d_w_alpha


def kernel(x, c, positions, w_ada, b_ada, norm1_g, w_in, b_merge, gla_w_alpha, gla_b_alpha, gla_out_norm_g, gla_w_o, mla_q_lat_g, mla_w_uq, mla_kv_lat_g, mla_w_ukv, mla_qn_g, mla_kn_g, mla_w_o, w_out, norm2_g, mlp_w1, mlp_w2, loss_target, m_w_ada, m_b_ada, m_norm1_g, m_w_in, m_b_merge, m_gla_w_alpha, m_gla_b_alpha, m_gla_out_norm_g, m_gla_w_o, m_mla_q_lat_g, m_mla_w_uq, m_mla_kv_lat_g, m_mla_w_ukv, m_mla_qn_g, m_mla_kn_g, m_mla_w_o, m_w_out, m_norm2_g, m_mlp_w1, m_mlp_w2, v_w_ada, v_b_ada, v_norm1_g, v_w_in, v_b_merge, v_gla_w_alpha, v_gla_b_alpha, v_gla_out_norm_g, v_gla_w_o, v_mla_q_lat_g, v_mla_w_uq, v_mla_kv_lat_g, v_mla_w_ukv, v_mla_qn_g, v_mla_kn_g, v_mla_w_o, v_w_out, v_norm2_g, v_mlp_w1, v_mlp_w2):
    args = dict(locals())
    names_big = [n for n, _, _ in BIG]
    names_small = [n for n, _ in SMALL]
    bsz = x.shape[0]
    ax, ay, ac = lax.axis_index("x"), lax.axis_index("y"), lax.axis_index("c")
    chip = 2 * ax + ay
    dev = 2 * chip + ac

    small = {n: args[n] for n in names_small}
    sel_c = jnp.reshape(ac, (1,)).astype(jnp.int32)
    sel_chip = jnp.reshape(chip, (1,)).astype(jnp.int32)
    c_all, w_alpha_all = _all_gather8([c, gla_w_alpha[0]], name="comm_c_alpha")
    small["gla_w_alpha"] = jnp.concatenate([w_alpha_all[2 * j] for j in range(4)], axis=1)
    c_all = c_all.reshape(8 * bsz, D)

    shards = {n: args[n][0].astype(BF16) for n in names_big}
    halves_of = lambda names: [shards[n].reshape(2, shards[n].shape[0] // 2, shards[n].shape[1]) for n in names]

    def gather_start(names, deps, tag):
        xs = halves_of(names)
        lands = [lax.empty((4, *xh.shape), BF16) for xh in xs]
        plan = _gather_plan(len(names))
        return names, plan, _rdma_start(xs + lands, 3 * len(names), plan, deps, name="comm_weights_start_" + tag)

    def gather_finish(started, after, tag):
        names, plan, sems = started
        arrs = _rdma_wait(sems[0], sems[1], sems[2], plan, after, name="comm_weights_wait_" + tag)
        filled = _pair_fill(arrs[len(names):], name="comm_weights_pair_" + tag)
        own = [a.reshape(shards[n].shape) for n, a in zip(names, arrs)]
        return _full_weights(dict(zip(names, _own_and_landed(filled, own))))

    first = gather_start(["w_in"], (c_all,), "in")
    c_all = c_all + first[2][3][0, 0]


    def add_bias(acc, ex, outs):
        outs[0][...] = acc + ex[0][...]

    silu = lambda v: v * _sigmoid(v)
    b_ada_mine = lax.dynamic_slice(b_ada, (0, chip * ADA_SHARD[1]), (1, ADA_SHARD[1]))
    mod_part = _mm(c_all, w_ada[0], name="ada", tn=512, a_fn=silu, epilogue=add_bias, extras=(b_ada_mine,),
                   extra_specs=(pl.BlockSpec((1, 512), lambda i, j, k: (0, j)),),
                   out_shape=jax.ShapeDtypeStruct((8 * bsz, ADA_SHARD[1]), F32), out_specs=_tile_spec(8 * bsz, 512))
    mod_all = _all_gather8([mod_part], name="comm_mod")[0]
    mod_rows = lax.dynamic_slice(mod_all, (0, dev * bsz, 0), (8, bsz, ADA_SHARD[1]))
    mod = jnp.concatenate([mod_rows[2 * j] for j in range(4)], axis=1)
    rest = gather_start([n for n in names_big if n != "w_in"], (mod,), "rest")
    mod = mod + rest[2][3][0, 0]
    w_in_after = lambda after: gather_finish(first, after, "in")
    more_weights = lambda after: gather_finish(rest, after, "rest")

    in_flight = []

    def reduce_start(tag, grads):
        names = list(grads)
        parts = [_grad_slots(grads)[n] for n in names]
        sib_halves = _pair_swap_halves(parts, name="comm_pair_sum_" + tag)
        pairs = [_pair_add(p, s, sel_c, name="pair_add_" + n) for n, p, s in zip(names, parts, sib_halves)]
        recvs = [lax.empty((3, *p.shape[1:]), BF16) for p in pairs]
        plan = _scatter_plan(len(names))
        sems = _rdma_start(pairs + recvs, 3 * len(names), plan, (), name="comm_scatter_start_" + tag)
        in_flight.append((tag, names, plan, sems))
        return sems[3]

    loss_part, grad_x, dmod, g, gs, d_w_alpha = _local_step(x, positions, mod, loss_target, w_in_after, small,
                                                            more_weights, reduce_start)
    loss = lax.psum(loss_part * (0.5 / D), ("x", "y", "c"))

    gs_packed = _pack_small([gs[n] for n, _ in SMALL_RED], d_w_alpha, name="pack_small")
    dmod_all, gs_all = _all_gather8([dmod, gs_packed], name="comm_dmod_small")
    dmod_all = dmod_all.reshape(8 * bsz, 6 * D)
    dmod_mine = lax.dynamic_slice(dmod_all, (0, chip * ADA_SHARD[1]), (8 * bsz, ADA_SHARD[1]))
    g_w_ada = _mm(c_all, dmod_mine, ta=True, a_fn=silu, name="ada_dw")

    wmv = [(args[n], args["m_" + n], args["v_" + n]) for n in names_small]
    wmv.append((gla_w_alpha[0], m_gla_w_alpha[0], v_gla_w_alpha[0]))
    res = _small_update(gs_all, dmod_all, sel_chip, wmv, name="small_update")

    assert not g, list(g)
    half_of = {}
    for tag, names, plan, sems in in_flight:
        arrs = _rdma_wait(sems[0], sems[1], sems[2], plan, grad_x, name="comm_scatter_wait_" + tag)
        for n, p, r in zip(names, arrs[:len(names)], arrs[len(names):]):
            half_of[n] = _chip_sum(p, r, sel_chip, name="chip_sum_" + n)
    halves = [half_of[n] for n in names_big]
    theirs = _pair_swap(halves, name="comm_pair_join")
    for n, mine, other in zip(names_big, halves, theirs):
        if n == "w_in":
            south = ac == 0
            g_t = jnp.concatenate([jnp.where(south, mine, other), jnp.where(south, other, mine)], axis=0).T
            outs = _adamw(w_in[0].T, g_t, m_w_in[0].T, v_w_in[0].T, name="adamw_w_in", by_cols=True)
            res[n] = tuple(a.T for a in (g_t, *outs))
            continue
        res[n] = _adamw_halves(args[n][0], args["m_" + n][0], args["v_" + n][0], mine, other, sel_c, name="adamw_" + n)
    res["w_ada"] = (g_w_ada, *_adamw(w_ada[0], g_w_ada, m_w_ada[0], v_w_ada[0], name="adamw_w_ada"))

    order = ["w_ada", "b_ada", "norm1_g", "w_in", "b_merge", "gla_w_alpha", "gla_b_alpha", "gla_out_norm_g", "gla_w_o",
             "mla_q_lat_g", "mla_w_uq", "mla_kv_lat_g", "mla_w_ukv", "mla_qn_g", "mla_kn_g", "mla_w_o", "w_out",
             "norm2_g", "mlp_w1", "mlp_w2"]
    named = lambda k: [res[n][k].reshape(args[n].shape) for n in order]
    return (loss, grad_x, *named(0), *named(1), *named(2), *named(3))
```

```python
import functools

import jax
import jax.numpy as jnp
import numpy as np
from jax import lax
from jax.experimental import pallas as pl
from jax.experimental.pallas import tpu as pltpu

F32 = jnp.float32
BF16 = jnp.bfloat16
MESH = pl.DeviceIdType.MESH

D = 1024
CHUNK = 64
EPS = 1e-6
GH, GDK, GDV, GLR, GTAU = 4, 128, 256, 16, 16.0
MH, MQR, MKVR, MNOPE, MROPE, MVD = 16, 256, 128, 64, 32, 64
MQK = MNOPE + MROPE
DFF = 4 * D
ROPE_THETA = 10000.0
IN_WIDTH = 5552
LANE = 128
OFF_Q, OFF_K, OFF_V, OFF_G, OFF_MA, OFF_MB, OFF_CQ, OFF_CKV, OFF_A, OFF_KPE, PW = (
    0, 512, 1024, 2048, 3072, 4096, 5120, 5376, 5504, 5632, 5760)
ADAM_LR, ADAM_B1, ADAM_B2, ADAM_EPS, ADAM_WD, ADAM_STEP = 0.001, 0.9, 0.999, 1e-08, 0.01, 10
VMEM_LIMIT = 48 * 1024 * 1024


def _params(n_axes):
    return pltpu.CompilerParams(dimension_semantics=("arbitrary",) * n_axes, vmem_limit_bytes=VMEM_LIMIT)


def _tile(n, target):
    if n <= target:
        return n
    best = None
    for t in range(LANE, target + 1, LANE):
        if n % t == 0:
            best = t
    assert best is not None, (n, target)
    return best


def _sigmoid(x):
    return 1.0 / (1.0 + jnp.exp(-x))


def _mm(a, b, *, name, ta=False, tb=False, out_dtype=F32, tm=1024, tn=1024, tk=1024,
        epilogue=None, extras=(), extra_specs=(), out_shape=None, out_specs=None, a_fn=None):
    if ta:
        kdim, m = a.shape
    else:
        m, kdim = a.shape
    if tb:
        n, k2 = b.shape
    else:
        k2, n = b.shape
    assert kdim == k2, (a.shape, b.shape)
    tm, tn, tk = _tile(m, tm), _tile(n, tn), _tile(kdim, tk)
    nk = kdim // tk
    a_spec = pl.BlockSpec((tk, tm), lambda i, j, k: (k, i)) if ta else pl.BlockSpec((tm, tk), lambda i, j, k: (i, k))
    b_spec = pl.BlockSpec((tn, tk), lambda i, j, k: (j, k)) if tb else pl.BlockSpec((tk, tn), lambda i, j, k: (k, j))
    dims = (((0 if ta else 1,), (1 if tb else 0,)), ((), ()))
    ne = len(extras)
    if out_shape is None:
        out_shape = jax.ShapeDtypeStruct((m, n), out_dtype)
        out_specs = pl.BlockSpec((tm, tn), lambda i, j, k: (i, j))
    n_out = len(out_shape) if isinstance(out_shape, (list, tuple)) else 1
    in_place = epilogue is None and n_out == 1 and out_shape.dtype == F32
    scratch = [] if (nk == 1 or in_place) else [pltpu.VMEM((tm, tn), F32)]

    def body(a_ref, b_ref, *rest):
        ex, outs = rest[:ne], rest[ne:ne + n_out]
        av = a_ref[...] if a_fn is None else a_fn(a_ref[...])
        prod = lax.dot_general(av.astype(BF16), b_ref[...].astype(BF16), dims, preferred_element_type=F32)

        def finish(val):
            if epilogue is None:
                outs[0][...] = val.astype(outs[0].dtype)
            else:
                epilogue(val, ex, outs)

        if nk == 1:
            finish(prod)
            return
        k = pl.program_id(2)
        acc = outs[0] if in_place else rest[-1]

        @pl.when(k == 0)
        def _():
            acc[...] = prod

        @pl.when(k > 0)
        def _():
            acc[...] += prod

        if not in_place:
            @pl.when(k == nk - 1)
            def _():
                finish(acc[...])

    return pl.pallas_call(
        body, name=name, grid=(m // tm, n // tn, nk),
        in_specs=[a_spec, b_spec, *extra_specs], out_specs=out_specs, out_shape=out_shape,
        scratch_shapes=scratch, compiler_params=_params(3),
    )(a, b, *extras)


def _tile_spec(tm, tn):
    return pl.BlockSpec((tm, tn), lambda i, j, k: (i, j))


def _pieces_dx(pieces, w, after, *, name, tm=256):
    t = pieces[0][0].shape[0]
    tm = _tile(t, tm)
    npc = len(pieces)

    def body(*refs):
        p_refs, w_ref, out_ref = refs[:npc], refs[npc], refs[-1]
        acc = None
        for (arr, off), p_ref in zip(pieces, p_refs):
            part = lax.dot_general(p_ref[...].astype(BF16), w_ref[:, off:off + arr.shape[1]], _NT,
                                   preferred_element_type=F32)
            acc = part if acc is None else acc + part
        out_ref[...] = acc

    return pl.pallas_call(
        body, name=name, grid=(t // tm,),
        in_specs=[pl.BlockSpec((tm, arr.shape[1]), lambda i: (i, 0)) for arr, _ in pieces]
        + [pl.BlockSpec(w.shape, lambda i: (0, 0)), pl.BlockSpec((8, LANE), lambda i: (0, 0))],
        out_specs=pl.BlockSpec((tm, w.shape[0]), lambda i: (i, 0)),
        out_shape=jax.ShapeDtypeStruct((t, w.shape[0]), F32), compiler_params=_params(1),
    )(*[arr for arr, _ in pieces], w, after)


def _pieces_dw(h, pieces, *, name, tk=1024):
    t, d = h.shape
    tk = _tile(t, tk)
    widths = [p.shape[1] for p in pieces]
    starts = [sum(widths[:i]) for i in range(len(pieces))]

    def body(h_ref, *refs):
        p_refs, out_ref = refs[:-1], refs[-1]
        first = pl.program_id(0) == 0
        hv = h_ref[...]
        for p_ref, start, width in zip(p_refs, starts, widths):
            part = lax.dot_general(hv, p_ref[...].astype(BF16), _TN, preferred_element_type=F32)
            cols = slice(start, start + width)

            @pl.when(first)
            def _():
                out_ref[:, cols] = part

            @pl.when(jnp.logical_not(first))
            def _():
                out_ref[:, cols] += part

    return pl.pallas_call(
        body, name=name, grid=(t // tk,),
        in_specs=[pl.BlockSpec((tk, d), lambda k: (k, 0))] + [pl.BlockSpec((tk, wd), lambda k: (k, 0)) for wd in widths],
        out_specs=pl.BlockSpec((d, sum(widths)), lambda k: (0, 0)),
        out_shape=jax.ShapeDtypeStruct((d, sum(widths)), F32), compiler_params=_params(1),
    )(h, *pieces)


def _rms(x, g):
    r = lax.rsqrt(jnp.mean(x * x, axis=-1, keepdims=True) + EPS)
    return x * r, r


def _row_spec(ts, width, col=0):
    return pl.BlockSpec((None, ts, width), lambda b, i: (b, i, col))


def _vec_spec(width):
    return pl.BlockSpec((None, 1, width), lambda b, i: (b, 0, 0))


def _gain_spec(width):
    return pl.BlockSpec((1, width), lambda b, i: (0, 0))


def _norm_mod(x, g, scale, shift, *, name, ts=256):
    bsz, s, d = x.shape
    ts = min(ts, s)

    def body(x_ref, g_ref, sc_ref, sh_ref, h_ref):
        xh, _ = _rms(x_ref[...], None)
        h_ref[...] = ((xh * g_ref[...]) * (1.0 + sc_ref[...]) + sh_ref[...]).astype(BF16)

    return pl.pallas_call(
        body, name=name, grid=(bsz, s // ts),
        in_specs=[_row_spec(ts, d), _gain_spec(d), _vec_spec(d), _vec_spec(d)],
        out_specs=_row_spec(ts, d), out_shape=jax.ShapeDtypeStruct((bsz, s, d), BF16),
        compiler_params=_params(2),
    )(x, g, scale, shift)


def _resid_norm_mod(x, mixed, gate, g, scale, shift, *, name, ts=256):
    bsz, s, d = x.shape
    ts = min(ts, s)

    def body(x_ref, mx_ref, gt_ref, g_ref, sc_ref, sh_ref, x1_ref, h_ref):
        x1 = x_ref[...] + gt_ref[...] * mx_ref[...]
        x1_ref[...] = x1
        xh, _ = _rms(x1, None)
        h_ref[...] = ((xh * g_ref[...]) * (1.0 + sc_ref[...]) + sh_ref[...]).astype(BF16)

    return pl.pallas_call(
        body, name=name, grid=(bsz, s // ts),
        in_specs=[_row_spec(ts, d), _row_spec(ts, d), _vec_spec(d), _gain_spec(d), _vec_spec(d), _vec_spec(d)],
        out_specs=[_row_spec(ts, d), _row_spec(ts, d)],
        out_shape=[jax.ShapeDtypeStruct((bsz, s, d), F32), jax.ShapeDtypeStruct((bsz, s, d), BF16)],
        compiler_params=_params(2),
    )(x, mixed, gate, g, scale, shift)


def _norm_mod_bwd(dh, xin, resid, g, scale, gate=None, mixed=None, *, name, ts=256):
    bsz, s, d = xin.shape
    ts = min(ts, s)
    gated = gate is not None

    def body(*refs):
        if gated:
            dh_ref, x_ref, rs_ref, g_ref, sc_ref, gt_ref, mx_ref, dx_ref, dsc_ref, dsh_ref, dg_ref, dgt_ref, dmx_ref = refs
        else:
            dh_ref, x_ref, rs_ref, g_ref, sc_ref, dx_ref, dsc_ref, dsh_ref, dg_ref = refs
        b, i = pl.program_id(0), pl.program_id(1)

        @pl.when(i == 0)
        def _():
            dsc_ref[...] = jnp.zeros_like(dsc_ref)
            dsh_ref[...] = jnp.zeros_like(dsh_ref)
            if gated:
                dgt_ref[...] = jnp.zeros_like(dgt_ref)

        @pl.when((i == 0) & (b == 0))
        def _():
            dg_ref[...] = jnp.zeros_like(dg_ref)

        dh_v, gv = dh_ref[...], g_ref[...]
        xh, r = _rms(x_ref[...], None)
        dsc_ref[...] += jnp.sum(dh_v * (xh * gv), axis=0, keepdims=True)
        dsh_ref[...] += jnp.sum(dh_v, axis=0, keepdims=True)
        dn = dh_v * (1.0 + sc_ref[...])
        dg_ref[...] += jnp.sum(dn * xh, axis=0, keepdims=True)
        dxh = dn * gv
        dx = rs_ref[...] + r * (dxh - xh * jnp.mean(dxh * xh, axis=-1, keepdims=True))
        dx_ref[...] = dx
        if gated:
            dgt_ref[...] += jnp.sum(dx * mx_ref[...], axis=0, keepdims=True)
            dmx_ref[...] = (dx * gt_ref[...]).astype(BF16)

    ins = [dh, xin, resid, g, scale]
    in_specs = [_row_spec(ts, d), _row_spec(ts, d), _row_spec(ts, d), _gain_spec(d), _vec_spec(d)]
    out_specs = [_row_spec(ts, d), _vec_spec(d), _vec_spec(d), _gain_spec(d)]
    out_shape = [jax.ShapeDtypeStruct((bsz, s, d), F32), jax.ShapeDtypeStruct((bsz, 1, d), F32),
                 jax.ShapeDtypeStruct((bsz, 1, d), F32), jax.ShapeDtypeStruct((1, d), F32)]
    if gated:
        ins += [gate, mixed]
        in_specs += [_vec_spec(d), _row_spec(ts, d)]
        out_specs += [_vec_spec(d), _row_spec(ts, d)]
        out_shape += [jax.ShapeDtypeStruct((bsz, 1, d), F32), jax.ShapeDtypeStruct((bsz, s, d), BF16)]
    return pl.pallas_call(
        body, name=name, grid=(bsz, s // ts), in_specs=in_specs, out_specs=out_specs, out_shape=out_shape,
        compiler_params=_params(2),
    )(*ins)


def _loss_head(x1, ff, gate2, target, *, name, ts=256):
    bsz, s, d = x1.shape
    ts = min(ts, s)

    def body(x1_ref, ff_ref, gt_ref, t_ref, dy_ref, dff_ref, dgt_ref, loss_ref, acc):
        b, i = pl.program_id(0), pl.program_id(1)

        @pl.when(i == 0)
        def _():
            dgt_ref[...] = jnp.zeros_like(dgt_ref)

        @pl.when((i == 0) & (b == 0))
        def _():
            acc[...] = jnp.zeros_like(acc)

        ffv, gt = ff_ref[...], gt_ref[...]
        diff = (x1_ref[...] + gt * ffv) - t_ref[...]
        acc[...] += jnp.sum((diff * diff).reshape(ts // 8, 8, d), axis=0)
        dy = diff * (1.0 / d)
        dy_ref[...] = dy
        dgt_ref[...] += jnp.sum(dy * ffv, axis=0, keepdims=True)
        dff_ref[...] = (dy * gt).astype(BF16)

        @pl.when((i == pl.num_programs(1) - 1) & (b == pl.num_programs(0) - 1))
        def _():
            loss_ref[...] = jnp.full(loss_ref.shape, jnp.sum(acc[...]), F32)

    return pl.pallas_call(
        body, name=name, grid=(bsz, s // ts),
        in_specs=[_row_spec(ts, d), _row_spec(ts, d), _vec_spec(d), _row_spec(ts, d)],
        out_specs=[_row_spec(ts, d), _row_spec(ts, d), _vec_spec(d), pl.BlockSpec((8, LANE), lambda b, i: (0, 0))],
        out_shape=[jax.ShapeDtypeStruct((bsz, s, d), F32), jax.ShapeDtypeStruct((bsz, s, d), BF16),
                   jax.ShapeDtypeStruct((bsz, 1, d), F32), jax.ShapeDtypeStruct((8, LANE), F32)],
        scratch_shapes=[pltpu.VMEM((8, d), F32)], compiler_params=_params(2),
    )(x1, ff, gate2, target)


def _merge_fwd(proj, b_merge, y_a, y_b, *, name, ts=256):
    bsz, s, _ = proj.shape
    ts = min(ts, s)

    def body(la_ref, lb_ref, ba_ref, bb_ref, ya_ref, yb_ref, out_ref):
        ga = _sigmoid(la_ref[...] + ba_ref[...])
        gb = _sigmoid(lb_ref[...] + bb_ref[...])
        out_ref[...] = (ga * ya_ref[...] + gb * yb_ref[...]).astype(BF16)

    return pl.pallas_call(
        body, name=name, grid=(bsz, s // ts),
        in_specs=[_row_spec(ts, D, OFF_MA // D), _row_spec(ts, D, OFF_MB // D),
                  pl.BlockSpec((1, D), lambda b, i: (0, 0)), pl.BlockSpec((1, D), lambda b, i: (0, 1)),
                  _row_spec(ts, D), _row_spec(ts, D)],
        out_specs=_row_spec(ts, D), out_shape=jax.ShapeDtypeStruct((bsz, s, D), BF16),
        compiler_params=_params(2),
    )(proj, proj, b_merge, b_merge, y_a, y_b)


def _merge_bwd(dmi, proj, b_merge, y_a, y_b, *, name, ts=256):
    bsz, s, _ = proj.shape
    ts = min(ts, s)

    def body(d_ref, la_ref, lb_ref, ba_ref, bb_ref, ya_ref, yb_ref, dya_ref, dyb_ref, dla_ref, dlb_ref, dba_ref, dbb_ref):
        @pl.when((pl.program_id(0) == 0) & (pl.program_id(1) == 0))
        def _():
            dba_ref[...] = jnp.zeros_like(dba_ref)
            dbb_ref[...] = jnp.zeros_like(dbb_ref)

        dv = d_ref[...]
        ga = _sigmoid(la_ref[...] + ba_ref[...])
        gb = _sigmoid(lb_ref[...] + bb_ref[...])
        dya_ref[...] = (dv * ga).astype(BF16)
        dyb_ref[...] = (dv * gb).astype(BF16)
        dla = (dv * ya_ref[...]) * (ga * (1.0 - ga))
        dlb = (dv * yb_ref[...]) * (gb * (1.0 - gb))
        dla_ref[...] = dla.astype(BF16)
        dlb_ref[...] = dlb.astype(BF16)
        dba_ref[...] += jnp.sum(dla, axis=0, keepdims=True)
        dbb_ref[...] += jnp.sum(dlb, axis=0, keepdims=True)

    act = jax.ShapeDtypeStruct((bsz, s, D), BF16)
    return pl.pallas_call(
        body, name=name, grid=(bsz, s // ts),
        in_specs=[_row_spec(ts, D), _row_spec(ts, D, OFF_MA // D), _row_spec(ts, D, OFF_MB // D),
                  pl.BlockSpec((1, D), lambda b, i: (0, 0)), pl.BlockSpec((1, D), lambda b, i: (0, 1)),
                  _row_spec(ts, D), _row_spec(ts, D)],
        out_specs=[_row_spec(ts, D)] * 4 + [_gain_spec(D)] * 2,
        out_shape=[act, act, act, act, jax.ShapeDtypeStruct((1, D), F32), jax.ShapeDtypeStruct((1, D), F32)],
        compiler_params=_params(2),
    )(dmi, proj, proj, b_merge, b_merge, y_a, y_b)


def _tri(lower):
    r = lax.broadcasted_iota(jnp.int32, (CHUNK, CHUNK), 0)
    c = lax.broadcasted_iota(jnp.int32, (CHUNK, CHUNK), 1)
    return jnp.where((c <= r) if lower else (c >= r), 1.0, 0.0).astype(F32)


def _gla_logits(a_ref, wal_ref, bal_ref):
    logits = jnp.dot(a_ref[...].astype(BF16), wal_ref[...].astype(BF16), preferred_element_type=F32) + bal_ref[...]
    la = (jnp.minimum(logits, 0.0) - jnp.log(1.0 + jnp.exp(-jnp.abs(logits)))) * (1.0 / GTAU)
    return logits, la


def _chunk_cumsum(la_n, tri):
    cum = jnp.dot(tri, la_n, preferred_element_type=F32, precision=lax.Precision.HIGHEST)
    return cum, jnp.sum(la_n, axis=0, keepdims=True)


def _gla_specs(s, nc):
    def blk(width, off):
        return pl.BlockSpec((None, s, width), lambda h, b: (b, 0, off // width + h))

    proj_specs = [blk(GDK, OFF_Q), blk(GDK, OFF_K), blk(GDV, OFF_V), blk(GDV, OFF_G),
                  pl.BlockSpec((None, s, LANE), lambda h, b: (b, 0, OFF_A // LANE)),
                  pl.BlockSpec((LANE, GDK), lambda h, b: (0, h)), pl.BlockSpec((1, GDK), lambda h, b: (0, h)),
                  pl.BlockSpec((1, GDV), lambda h, b: (0, 0))]
    st_spec = pl.BlockSpec((None, None, nc, GDV, GDK), lambda h, b: (b, h, 0, 0, 0))
    return blk, proj_specs, st_spec


def _gla_fwd(proj, w_alpha_p, b_alpha, out_norm_g, *, name):
    bsz, s, _ = proj.shape
    nc = s // CHUNK
    scale = GDK ** -0.5

    rb = min(512, s)

    def body(q_ref, k_ref, v_ref, g_ref, a_ref, wal_ref, bal_ref, ong_ref, o_ref, og_ref, st_ref):
        _, la = _gla_logits(a_ref, wal_ref, bal_ref)
        tri = _tri(True)
        st = jnp.zeros((GDV, GDK), F32)
        for n in range(nc):
            rows = pl.ds(n * CHUNK, CHUNK)
            cum, cum_end = _chunk_cumsum(la[n * CHUNK:(n + 1) * CHUNK], tri)
            kd = k_ref[rows, :] * jnp.exp(cum_end - cum)
            ut = lax.dot_general(v_ref[rows, :].astype(BF16), kd.astype(BF16), _TN, preferred_element_type=F32)
            st = st * jnp.exp(cum_end) + ut
            st_ref[n] = st
            o_ref[rows, :] = lax.dot_general((q_ref[rows, :] * scale).astype(BF16), st.astype(BF16), _NT,
                                             preferred_element_type=F32)
        for j in range(0, s, rb):
            blk_rows = pl.ds(j, rb)
            oh, _ = _rms(o_ref[blk_rows, :], None)
            gv = g_ref[blk_rows, :]
            og_ref[blk_rows, :] = ((oh * ong_ref[...]) * (gv * _sigmoid(gv))).astype(BF16)

    blk, proj_specs, st_spec = _gla_specs(s, nc)
    return pl.pallas_call(
        body, name=name, grid=(GH, bsz), in_specs=proj_specs, out_specs=[blk(GDV, 0), blk(GDV, 0), st_spec],
        out_shape=[jax.ShapeDtypeStruct((bsz, s, GH * GDV), F32), jax.ShapeDtypeStruct((bsz, s, GH * GDV), BF16),
                   jax.ShapeDtypeStruct((bsz, GH, nc, GDV, GDK), F32)],
        compiler_params=_params(2),
    )(proj, proj, proj, proj, proj, w_alpha_p, b_alpha, out_norm_g)


def _gla_bwd(dog, o, states, proj, w_alpha_p, b_alpha, out_norm_g, *, name):
    bsz, s, _ = proj.shape
    nc = s // CHUNK
    scale = GDK ** -0.5

    def body(dog_ref, o_ref, st_ref, q_ref, k_ref, v_ref, g_ref, a_ref, wal_ref, bal_ref, ong_ref,
             dq_ref, dk_ref, dv_ref, dg_ref, dl_ref, dbal_ref, dong_ref, do_scr, dlog_scr):
        h, b = pl.program_id(0), pl.program_id(1)

        @pl.when(b == 0)
        def _():
            dbal_ref[...] = jnp.zeros_like(dbal_ref)

        @pl.when((b == 0) & (h == 0))
        def _():
            dong_ref[...] = jnp.zeros_like(dong_ref)

        ong = ong_ref[...]
        for j in range(0, s, rb):
            blk_rows = pl.ds(j, rb)
            gv, dogv = g_ref[blk_rows, :], dog_ref[blk_rows, :]
            sg = _sigmoid(gv)
            oh, r = _rms(o_ref[blk_rows, :], None)
            don = dogv * (gv * sg)
            dg_ref[blk_rows, :] = (dogv * (oh * ong) * (sg * (1.0 + gv * (1.0 - sg)))).astype(BF16)
            dong_ref[...] += jnp.sum(don * oh, axis=0, keepdims=True)
            doh = don * ong
            do_scr[blk_rows, :] = (r * (doh - oh * jnp.mean(doh * oh, axis=-1, keepdims=True))).astype(BF16)

        logits, la = _gla_logits(a_ref, wal_ref, bal_ref)
        tri_lo, tri_up = _tri(True), _tri(False)
        carry = jnp.zeros((GDV, GDK), F32)
        for n in range(nc - 1, -1, -1):
            rows = pl.ds(n * CHUNK, CHUNK)
            cum, cum_end = _chunk_cumsum(la[n * CHUNK:(n + 1) * CHUNK], tri_lo)
            decay = jnp.exp(cum_end)
            w = jnp.exp(cum_end - cum)
            kd = k_ref[rows, :] * w
            do_b = do_scr[rows, :]
            qs_b = (q_ref[rows, :] * scale).astype(BF16)
            dq_ref[rows, :] = (jnp.dot(do_b, st_ref[n].astype(BF16), preferred_element_type=F32) * scale).astype(BF16)
            dsn = lax.dot_general(do_b, qs_b, _TN, preferred_element_type=F32) + carry
            carry = dsn * decay
            dsn_b = dsn.astype(BF16)
            dv_ref[rows, :] = lax.dot_general(kd.astype(BF16), dsn_b, _NT, preferred_element_type=F32).astype(BF16)
            dkd = jnp.dot(v_ref[rows, :].astype(BF16), dsn_b, preferred_element_type=F32)
            dk_ref[rows, :] = (dkd * w).astype(BF16)
            e = dkd * kd
            dcum_end = jnp.sum(e, axis=0, keepdims=True)
            if n > 0:
                dcum_end += jnp.sum(dsn * st_ref[n - 1], axis=0, keepdims=True) * decay
            dlog_scr[rows, :] = dcum_end - jnp.dot(tri_up, e, preferred_element_type=F32,
                                                  precision=lax.Precision.HIGHEST)
        dlog = dlog_scr[...] * (1.0 / GTAU) * (1.0 - _sigmoid(logits))
        dl_ref[...] = dlog.astype(BF16)
        dbal_ref[...] += jnp.sum(dlog, axis=0, keepdims=True)

    rb = min(512, s)

    blk, proj_specs, st_spec = _gla_specs(s, nc)
    act = lambda wd: jax.ShapeDtypeStruct((bsz, s, wd), BF16)
    return pl.pallas_call(
        body, name=name, grid=(GH, bsz), in_specs=[blk(GDV, 0), blk(GDV, 0), st_spec, *proj_specs],
        out_specs=[blk(GDK, 0), blk(GDK, 0), blk(GDV, 0), blk(GDV, 0), blk(GDK, 0),
                   pl.BlockSpec((1, GDK), lambda h, b: (0, h)), pl.BlockSpec((1, GDV), lambda h, b: (0, 0))],
        out_shape=[act(GH * GDK), act(GH * GDK), act(GH * GDV), act(GH * GDV), act(GH * GDK),
                   jax.ShapeDtypeStruct((1, GH * GDK), F32), jax.ShapeDtypeStruct((1, GDV), F32)],
        scratch_shapes=[pltpu.VMEM((s, GDV), BF16), pltpu.VMEM((s, GDK), F32)], compiler_params=_params(2),
    )(dog, o, states, proj, proj, proj, proj, proj, w_alpha_p, b_alpha, out_norm_g)


def _lane():
    return lax.broadcasted_iota(jnp.int32, (1, LANE), 1)


def _swap_halves(x):
    lane = _lane()
    half = MROPE // 2
    lo = (lane >= MNOPE) & (lane < MNOPE + half)
    hi = (lane >= MNOPE + half) & (lane < MQK)
    return jnp.where(lo, pltpu.roll(x, LANE - half, 1), jnp.where(hi, pltpu.roll(x, half, 1), 0.0))


def _norm96(x, g):
    r = lax.rsqrt(jnp.sum(x * x, axis=-1, keepdims=True) * (1.0 / MQK) + EPS)
    return x * r, r


def _lat_norm(proj, q_lat_g, kv_lat_g, *, name, ts=512):
    t = proj.shape[0]
    ts = min(ts, t)

    def body(cq_ref, ckv_ref, gq_ref, gk_ref, oq_ref, ok_ref):
        xq, _ = _rms(cq_ref[...], None)
        oq_ref[...] = (xq * gq_ref[...]).astype(BF16)
        xk, _ = _rms(ckv_ref[...], None)
        ok_ref[...] = (xk * gk_ref[...]).astype(BF16)

    return pl.pallas_call(
        body, name=name, grid=(t // ts,),
        in_specs=[pl.BlockSpec((ts, MQR), lambda i: (i, OFF_CQ // MQR)), pl.BlockSpec((ts, MKVR), lambda i: (i, OFF_CKV // MKVR)),
                  pl.BlockSpec((1, MQR), lambda i: (0, 0)), pl.BlockSpec((1, MKVR), lambda i: (0, 0))],
        out_specs=[pl.BlockSpec((ts, MQR), lambda i: (i, 0)), pl.BlockSpec((ts, MKVR), lambda i: (i, 0))],
        out_shape=[jax.ShapeDtypeStruct((t, MQR), BF16), jax.ShapeDtypeStruct((t, MKVR), BF16)],
        compiler_params=_params(1),
    )(proj, proj, q_lat_g, kv_lat_g)


def _lat_norm_bwd(dcqn, dckvn, proj, q_lat_g, kv_lat_g, *, name, ts=512):
    t = proj.shape[0]
    ts = min(ts, t)

    def one(d_ref, x_ref, g_ref, dx_ref, dg_ref):
        xh, r = _rms(x_ref[...], None)
        dn = d_ref[...]
        dg_ref[...] += jnp.sum(dn * xh, axis=0, keepdims=True)
        dxh = dn * g_ref[...]
        dx_ref[...] = (r * (dxh - xh * jnp.mean(dxh * xh, axis=-1, keepdims=True))).astype(BF16)

    def body(dq_ref, dk_ref, cq_ref, ckv_ref, gq_ref, gk_ref, dxq_ref, dxk_ref, dgq_ref, dgk_ref):
        @pl.when(pl.program_id(0) == 0)
        def _():
            dgq_ref[...] = jnp.zeros_like(dgq_ref)
            dgk_ref[...] = jnp.zeros_like(dgk_ref)

        one(dq_ref, cq_ref, gq_ref, dxq_ref, dgq_ref)
        one(dk_ref, ckv_ref, gk_ref, dxk_ref, dgk_ref)

    return pl.pallas_call(
        body, name=name, grid=(t // ts,),
        in_specs=[pl.BlockSpec((ts, MQR), lambda i: (i, 0)), pl.BlockSpec((ts, MKVR), lambda i: (i, 0)),
                  pl.BlockSpec((ts, MQR), lambda i: (i, OFF_CQ // MQR)), pl.BlockSpec((ts, MKVR), lambda i: (i, OFF_CKV // MKVR)),
                  pl.BlockSpec((1, MQR), lambda i: (0, 0)), pl.BlockSpec((1, MKVR), lambda i: (0, 0))],
        out_specs=[pl.BlockSpec((ts, MQR), lambda i: (i, 0)), pl.BlockSpec((ts, MKVR), lambda i: (i, 0)),
                   pl.BlockSpec((1, MQR), lambda i: (0, 0)), pl.BlockSpec((1, MKVR), lambda i: (0, 0))],
        out_shape=[jax.ShapeDtypeStruct((t, MQR), BF16), jax.ShapeDtypeStruct((t, MKVR), BF16),
                   jax.ShapeDtypeStruct((1, MQR), F32), jax.ShapeDtypeStruct((1, MKVR), F32)],
        compiler_params=_params(1),
    )(dcqn, dckvn, proj, proj, q_lat_g, kv_lat_g)


def _qk_prep(q_raw, kv, proj, cos_t, sin_t, gq, gk, *, name, ts=2048):
    t = q_raw.shape[0]
    ts = min(ts, t)

    def body(q_ref, kv_ref, kpe_ref, c_ref, s_ref, gq_ref, gk_ref, qo_ref, ko_ref, vo_ref):
        cs, sn = c_ref[...], s_ref[...]
        nope = _lane() < MNOPE
        qn, _ = _norm96(q_ref[...], None)
        qn = qn * gq_ref[...]
        qo_ref[...] = (qn * cs + _swap_halves(qn) * sn).astype(BF16)
        kvv = kv_ref[...]
        kn, _ = _norm96(jnp.where(nope, kvv, kpe_ref[...]), None)
        kn = kn * gk_ref[...]
        ko_ref[...] = (kn * cs + _swap_halves(kn) * sn).astype(BF16)
        vo_ref[...] = jnp.where(nope, pltpu.roll(kvv, MNOPE, 1), 0.0).astype(BF16)

    hd = pl.BlockSpec((ts, LANE), lambda i, h: (i, h))
    shared = lambda col: pl.BlockSpec((ts, LANE), lambda i, h: (i, col))
    gain = pl.BlockSpec((1, LANE), lambda i, h: (0, 0))
    out = jax.ShapeDtypeStruct((t, MH * LANE), BF16)
    return pl.pallas_call(
        body, name=name, grid=(t // ts, MH),
        in_specs=[hd, hd, shared(OFF_KPE // LANE), shared(0), shared(0), gain, gain],
        out_specs=[hd, hd, hd], out_shape=[out, out, out], compiler_params=_params(2),
    )(q_raw, kv, proj, cos_t, sin_t, gq, gk)


def _qk_prep_bwd(dq, dk, dv, q_raw, kv, proj, cos_t, sin_t, gq, gk, *, name, ts=2048):
    t = q_raw.shape[0]
    ts = min(ts, t)

    def norm_bwd(dy, x, g, dg_ref):
        xh, r = _norm96(x, None)
        dg_ref[...] += jnp.sum(dy * xh, axis=0, keepdims=True)
        dxh = dy * g
        return r * (dxh - xh * (jnp.sum(dxh * xh, axis=-1, keepdims=True) * (1.0 / MQK)))

    def body(dq_ref, dk_ref, dv_ref, q_ref, kv_ref, kpe_ref, c_ref, s_ref, gq_ref, gk_ref,
             dqr_ref, dkv_ref, dkpe_ref, dgq_ref, dgk_ref):
        i, h = pl.program_id(0), pl.program_id(1)

        @pl.when(h == 0)
        def _():
            dkpe_ref[...] = jnp.zeros_like(dkpe_ref)

        @pl.when((h == 0) & (i == 0))
        def _():
            dgq_ref[...] = jnp.zeros_like(dgq_ref)
            dgk_ref[...] = jnp.zeros_like(dgk_ref)

        cs, sn = c_ref[...], s_ref[...]
        lane = _lane()
        nope = lane < MNOPE
        dqv = dq_ref[...]
        dqn = dqv * cs + _swap_halves(dqv * sn)
        dqr_ref[...] = norm_bwd(dqn, q_ref[...], gq_ref[...], dgq_ref).astype(BF16)
        dkv_ = dk_ref[...]
        dkn = dkv_ * cs + _swap_halves(dkv_ * sn)
        kvv = kv_ref[...]
        dkr = norm_bwd(dkn, jnp.where(nope, kvv, kpe_ref[...]), gk_ref[...], dgk_ref)
        dkv_ref[...] = jnp.where(nope, dkr, pltpu.roll(dv_ref[...], MNOPE, 1)).astype(BF16)
        dkpe_ref[...] += jnp.where((lane >= MNOPE) & (lane < MQK), dkr, 0.0)

    hd = pl.BlockSpec((ts, LANE), lambda i, h: (i, h))
    shared = lambda col: pl.BlockSpec((ts, LANE), lambda i, h: (i, col))
    gain = pl.BlockSpec((1, LANE), lambda i, h: (0, 0))
    out = jax.ShapeDtypeStruct((t, MH * LANE), BF16)
    return pl.pallas_call(
        body, name=name, grid=(t // ts, MH),
        in_specs=[hd, hd, hd, hd, hd, shared(OFF_KPE // LANE), shared(0), shared(0), gain, gain],
        out_specs=[hd, hd, shared(0), gain, gain],
        out_shape=[out, out, jax.ShapeDtypeStruct((t, LANE), F32), jax.ShapeDtypeStruct((1, LANE), F32),
                   jax.ShapeDtypeStruct((1, LANE), F32)],
        compiler_params=_params(2),
    )(dq, dk, dv, q_raw, kv, proj, cos_t, sin_t, gq, gk)


_NT = (((1,), (1,)), ((), ()))
_TN = (((0,), (0,)), ((), ()))


SOFTMAX_SCALE = MQK ** -0.5
Q_PRESCALE = SOFTMAX_SCALE * float(np.log2(np.e))


def _attn_weights(q, k_ref, lo, tq):
    row = lax.broadcasted_iota(jnp.int32, (tq, tq), 0) // CHUNK
    col = lax.broadcasted_iota(jnp.int32, (tq, tq), 1) // CHUNK
    sd = lax.dot_general(q, k_ref[pl.ds(lo, tq), :], _NT, preferred_element_type=F32)
    sd = jnp.where(col <= row, sd, -1e30)
    m = jnp.max(sd, axis=-1, keepdims=True)
    if lo:
        so = lax.dot_general(q, k_ref[pl.ds(0, lo), :], _NT, preferred_element_type=F32)
        m = jnp.maximum(m, jnp.max(so, axis=-1, keepdims=True))
        eo = jnp.exp2(so - m)
        ed = jnp.exp2(sd - m)
        return eo, ed, 1.0 / (jnp.sum(eo, axis=-1, keepdims=True) + jnp.sum(ed, axis=-1, keepdims=True))
    ed = jnp.exp2(sd - m)
    return None, ed, 1.0 / jnp.sum(ed, axis=-1, keepdims=True)


def _attn_fwd(q, k, v, *, name, tq=256):
    bsz, s, _ = q.shape
    tq = min(tq, s)

    def body(q_ref, k_ref, v_ref, o_ref):
        for i in range(s // tq):
            lo = i * tq
            eo, ed, inv = _attn_weights(q_ref[pl.ds(lo, tq), :], k_ref, lo, tq)
            o = jnp.dot(ed.astype(BF16), v_ref[pl.ds(lo, tq), :], preferred_element_type=F32)
            if lo:
                o += jnp.dot(eo.astype(BF16), v_ref[pl.ds(0, lo), :], preferred_element_type=F32)
            o_ref[pl.ds(lo, tq), :] = (o * inv).astype(BF16)

    spec = pl.BlockSpec((None, s, LANE), lambda b, h: (b, 0, h))
    return pl.pallas_call(
        body, name=name, grid=(bsz, MH), in_specs=[spec, spec, spec], out_specs=spec,
        out_shape=jax.ShapeDtypeStruct((bsz, s, MH * LANE), BF16), compiler_params=_params(2),
    )(q, k, v)


def _attn_bwd(q, k, v, do, *, name, tq=256):
    bsz, s, _ = q.shape
    tq = min(tq, s)

    def body(q_ref, k_ref, v_ref, do_ref, dq_ref, dk_ref, dv_ref):
        dk_ref[...] = jnp.zeros_like(dk_ref)
        dv_ref[...] = jnp.zeros_like(dv_ref)
        for i in range(s // tq):
            lo = i * tq
            here, before = pl.ds(lo, tq), pl.ds(0, lo)
            qv, dov = q_ref[here, :], do_ref[here, :]
            eo, ed, inv = _attn_weights(qv, k_ref, lo, tq)
            do_n = (dov.astype(F32) * inv).astype(BF16)
            dv_ref[here, :] += lax.dot_general(ed.astype(BF16), do_n, _TN, preferred_element_type=F32)
            dpd = lax.dot_general(dov, v_ref[here, :], _NT, preferred_element_type=F32)
            delta = jnp.sum(dpd * ed, axis=-1, keepdims=True)
            if lo:
                dv_ref[before, :] += lax.dot_general(eo.astype(BF16), do_n, _TN, preferred_element_type=F32)
                dpo = lax.dot_general(dov, v_ref[before, :], _NT, preferred_element_type=F32)
                delta += jnp.sum(dpo * eo, axis=-1, keepdims=True)
            delta = delta * inv
            r = inv * SOFTMAX_SCALE
            dsd = (ed * (dpd - delta) * r).astype(BF16)
            dq = jnp.dot(dsd, k_ref[here, :], preferred_element_type=F32)
            dk_ref[here, :] += lax.dot_general(dsd, qv, _TN, preferred_element_type=F32)
            if lo:
                dso = (eo * (dpo - delta) * r).astype(BF16)
                dq += jnp.dot(dso, k_ref[before, :], preferred_element_type=F32)
                dk_ref[before, :] += lax.dot_general(dso, qv, _TN, preferred_element_type=F32)
            dq_ref[here, :] = dq
        dk_ref[...] = dk_ref[...] * (1.0 / Q_PRESCALE)

    spec = pl.BlockSpec((None, s, LANE), lambda b, h: (b, 0, h))
    out = jax.ShapeDtypeStruct((bsz, s, MH * LANE), F32)
    return pl.pallas_call(
        body, name=name, grid=(bsz, MH), in_specs=[spec] * 4, out_specs=[spec] * 3, out_shape=[out, out, out],
        compiler_params=_params(2),
    )(q, k, v, do)


def _adamw(w, g, m, v, *, name, tr=256, by_cols=False):
    rows, cols = w.shape
    tr = _tile_rows(rows, tr)

    def body(w_ref, g_ref, m_ref, v_ref, d_ref, nm_ref, nv_ref):
        d_ref[...], nm_ref[...], nv_ref[...] = _adamw_update(w_ref[...], g_ref[...], m_ref[...], v_ref[...])

    spec = pl.BlockSpec((rows, LANE), lambda i: (0, i)) if by_cols else pl.BlockSpec((tr, cols), lambda i: (i, 0))
    out = jax.ShapeDtypeStruct((rows, cols), F32)
    return pl.pallas_call(body, name=name, grid=(cols // LANE if by_cols else rows // tr,), in_specs=[spec] * 4,
                          out_specs=[spec] * 3, out_shape=[out, out, out], compiler_params=_params(1))(w, g, m, v)


def _tile_rows(rows, target):
    if rows <= target:
        return rows
    best = 8
    for t in range(8, target + 1, 8):
        if rows % t == 0:
            best = t
    return best


def _adamw_update(w, g, m, v):
    nm = ADAM_B1 * m + (1.0 - ADAM_B1) * g
    nv = ADAM_B2 * v + (1.0 - ADAM_B2) * (g * g)
    m_hat = nm / (1.0 - ADAM_B1 ** ADAM_STEP)
    v_hat = nv / (1.0 - ADAM_B2 ** ADAM_STEP)
    return -ADAM_LR * (m_hat / (jnp.sqrt(v_hat) + ADAM_EPS) + ADAM_WD * w), nm, nv


def _adamw_halves(w, m, v, mine, theirs, sel, *, name, tr=256):
    rows, cols = w.shape
    tr = _tile_rows(rows // 2, tr)
    nh = rows // 2 // tr

    def body(sel_ref, w_ref, m_ref, v_ref, mine_ref, theirs_ref, g_ref, d_ref, nm_ref, nv_ref):
        lower = pl.program_id(0) < nh
        south = sel_ref[0] == 0
        gv = jnp.where(lower == south, mine_ref[...], theirs_ref[...])
        g_ref[...] = gv
        d_ref[...], nm_ref[...], nv_ref[...] = _adamw_update(w_ref[...], gv, m_ref[...], v_ref[...])

    full = pl.BlockSpec((tr, cols), lambda i, sel_ref: (i, 0))
    half = pl.BlockSpec((tr, cols), lambda i, sel_ref: (i % nh, 0))
    out = jax.ShapeDtypeStruct((rows, cols), F32)
    return pl.pallas_call(
        body, name=name, out_shape=[out] * 4, compiler_params=_params(1),
        grid_spec=pltpu.PrefetchScalarGridSpec(num_scalar_prefetch=1, grid=(rows // tr,),
                                               in_specs=[full, full, full, half, half], out_specs=[full] * 4),
    )(sel, w, m, v, mine, theirs)


def _pair_add(x, sib, sel, *, name, tr=256):
    n, _, rows, cols = x.shape
    tr = _tile_rows(rows, tr)

    def body(sel_ref, x_ref, s_ref, o_ref):
        o_ref[...] = (x_ref[...] + s_ref[...]).astype(BF16)

    spec = pl.BlockSpec((None, tr, cols), lambda j, i, sel_ref: (j, i, 0))
    return pl.pallas_call(
        body, name=name, out_shape=jax.ShapeDtypeStruct((n, rows, cols), BF16), compiler_params=_params(2),
        grid_spec=pltpu.PrefetchScalarGridSpec(
            num_scalar_prefetch=1, grid=(n, rows // tr),
            in_specs=[pl.BlockSpec((None, None, tr, cols), lambda j, i, sel_ref: (j, sel_ref[0], i, 0)), spec],
            out_specs=spec),
    )(sel, x, sib)


def _chip_sum(pair, recv, sel, *, name, tr=256):
    _, rows, cols = pair.shape
    tr = _tile_rows(rows, tr)

    def body(sel_ref, p_ref, r_ref, o_ref):
        acc = p_ref[...].astype(F32)
        for k in range(3):
            acc = acc + r_ref[k].astype(F32)
        o_ref[...] = acc

    return pl.pallas_call(
        body, name=name, out_shape=jax.ShapeDtypeStruct((rows, cols), F32), compiler_params=_params(1),
        grid_spec=pltpu.PrefetchScalarGridSpec(
            num_scalar_prefetch=1, grid=(rows // tr,),
            in_specs=[pl.BlockSpec((None, tr, cols), lambda i, sel_ref: (sel_ref[0], i, 0)),
                      pl.BlockSpec((3, tr, cols), lambda i, sel_ref: (0, i, 0))],
            out_specs=pl.BlockSpec((tr, cols), lambda i, sel_ref: (i, 0))),
    )(sel, pair, recv)


def _me():
    return lax.axis_index("x"), lax.axis_index("y"), lax.axis_index("c")


def _flip(pos, bits):
    x, y, c = pos
    return (x ^ bits[0] if bits[0] else x, y ^ bits[1] if bits[1] else y, c ^ bits[2] if bits[2] else c)


ANY = pl.BlockSpec(memory_space=pl.ANY)


def _all_gather8(xs, *, name):
    n = len(xs)
    flips = [((k >> 2) & 1, (k >> 1) & 1, k & 1) for k in range(1, 8)]

    def body(*refs):
        x_refs, out_refs, (send_sems, recv_sems, local_sems) = refs[:n], refs[n:2 * n], refs[2 * n:]
        me = _me()
        slot = lambda p: 4 * p[0] + 2 * p[1] + p[2]
        copies = []
        for i in range(n):
            mine = pltpu.make_async_copy(x_refs[i], out_refs[i].at[slot(me)], local_sems.at[i])
            mine.start()
            copies.append(mine)
            for k, f in enumerate(flips):
                peer = _flip(me, f)
                sems = dict(send_sem=send_sems.at[7 * i + k], recv_sem=recv_sems.at[7 * i + k], device_id=peer,
                            device_id_type=MESH)
                cp = pltpu.make_async_remote_copy(src_ref=x_refs[i], dst_ref=out_refs[i].at[slot(me)], **sems)
                cp.start()
                copies.append(cp)
                copies.append(pltpu.make_async_remote_copy(src_ref=x_refs[i], dst_ref=out_refs[i].at[slot(peer)], **sems))
        for i in range(n):
            base = i * 15
            copies[base].wait()
            for k in range(7):
                copies[base + 1 + 2 * k].wait_send()
                copies[base + 2 + 2 * k].wait_recv()

    outs = pl.pallas_call(
        body, name=name, in_specs=[ANY] * n, out_specs=[ANY] * n,
        out_shape=[jax.ShapeDtypeStruct((8, *x.shape), x.dtype) for x in xs],
        scratch_shapes=[pltpu.SemaphoreType.DMA((7 * n,)), pltpu.SemaphoreType.DMA((7 * n,)),
                        pltpu.SemaphoreType.DMA((n,))])(*xs)
    return list(outs)


CHIP_FLIPS = [(1, 0, 0), (0, 1, 0), (1, 1, 0)]


def _chip():
    return 2 * lax.axis_index("x") + lax.axis_index("y")


HBM = pl.BlockSpec(memory_space=pltpu.HBM)
SEM = pl.BlockSpec(memory_space=pltpu.SEMAPHORE)
EFFECT = pltpu.SideEffectType.DATAFLOW_SIDE_EFFECTING


def _plan_copies(plan, refs, send_sems, recv_sems):
    return [pltpu.make_async_remote_copy(src_ref=src, dst_ref=dst, send_sem=send_sems.at[k], recv_sem=recv_sems.at[k],
                                         device_id=to, device_id_type=MESH) for k, (src, dst, to) in enumerate(plan(refs))]


def _rdma_start(arrays, n_copies, plan, deps, *, name):
    n, nd = len(arrays), len(deps)

    def body(*refs):
        for cp in _plan_copies(plan, refs[:n], refs[n + nd], refs[n + nd + 1]):
            cp.start()
        refs[-1][...] = jnp.zeros_like(refs[-1])

    outs = pl.pallas_call(
        body, name=name,
        out_shape=(pltpu.SemaphoreType.DMA((n_copies,)), pltpu.SemaphoreType.DMA((n_copies,)),
                   *[pltpu.HBM(a.shape, a.dtype) for a in arrays], jax.ShapeDtypeStruct((8, LANE), F32)),
        in_specs=[HBM] * n + [ANY] * nd, out_specs=(SEM, SEM, *[HBM] * n, pl.BlockSpec(memory_space=pltpu.VMEM)),
        input_output_aliases={i: i + 2 for i in range(n)}, compiler_params=pltpu.CompilerParams(has_side_effects=EFFECT),
    )(*[pltpu.with_memory_space_constraint(a, pltpu.HBM) for a in arrays], *deps)
    return outs[0], outs[1], list(outs[2:2 + n]), outs[-1]


def _rdma_wait(send_sems, recv_sems, arrays, plan, after, *, name):
    n = len(arrays)

    def body(*refs):
        for cp in _plan_copies(plan, refs[:n], refs[n], refs[n + 1]):
            cp.wait_send()
            cp.wait_recv()

    return list(pl.pallas_call(
        body, name=name, out_shape=tuple(pltpu.HBM(a.shape, a.dtype) for a in arrays),
        in_specs=[HBM] * n + [SEM, SEM, ANY], out_specs=tuple([HBM] * n), input_output_aliases={i: i for i in range(n)},
        compiler_params=pltpu.CompilerParams(has_side_effects=EFFECT),
    )(*arrays, send_sems, recv_sems, after))


def _gather_plan(n):
    def plan(refs):
        me = _me()
        slot = 2 * me[0] + me[1]
        return [(refs[i].at[me[2]], refs[n + i].at[slot, me[2]], _flip(me, f)) for i in range(n) for f in CHIP_FLIPS]
    return plan


def _scatter_plan(n):
    def plan(refs):
        me = _me()
        out = []
        for i in range(n):
            for k, f in enumerate(CHIP_FLIPS):
                peer = _flip(me, f)
                out.append((refs[i].at[2 * peer[0] + peer[1]], refs[n + i].at[k], peer))
        return out
    return plan


def _sibling_plan(n, src_of):
    def plan(refs):
        me = _me()
        return [(src_of(refs[i], me[2]), refs[n + i], _flip(me, (0, 0, 1))) for i in range(n)]
    return plan


def _gather8_plan(n):
    def plan(refs):
        me = _me()
        slot = 4 * me[0] + 2 * me[1] + me[2]
        return [(refs[i], refs[n + i].at[slot], _flip(me, ((k >> 2) & 1, (k >> 1) & 1, k & 1)))
                for i in range(n) for k in range(1, 8)]
    return plan


def _pair_fill(lands, *, name):
    n = len(lands)

    def body(*refs):
        in_refs, (send_sems, recv_sems) = refs[:n], refs[2 * n:]
        me = _me()
        sib = _flip(me, (0, 0, 1))
        copies = []
        for i in range(n):
            for k, f in enumerate(CHIP_FLIPS):
                peer = _flip(me, f)
                slot = 2 * peer[0] + peer[1]
                mine, theirs = in_refs[i].at[slot, me[2]], in_refs[i].at[slot, 1 - me[2]]
                cp = pltpu.make_async_remote_copy(src_ref=mine, dst_ref=mine, send_sem=send_sems.at[3 * i + k],
                                                  recv_sem=recv_sems.at[3 * i + k], device_id=sib, device_id_type=MESH)
                cp.start()
                copies.append((cp, pltpu.make_async_remote_copy(
                    src_ref=mine, dst_ref=theirs, send_sem=send_sems.at[3 * i + k], recv_sem=recv_sems.at[3 * i + k],
                    device_id=sib, device_id_type=MESH)))
        for cp, arrival in copies:
            arrival.wait_recv()
            cp.wait_send()

    return list(pl.pallas_call(
        body, name=name, in_specs=[ANY] * n, out_specs=[ANY] * n,
        out_shape=[jax.ShapeDtypeStruct(a.shape, a.dtype) for a in lands], input_output_aliases={i: i for i in range(n)},
        scratch_shapes=[pltpu.SemaphoreType.DMA((3 * n,)), pltpu.SemaphoreType.DMA((3 * n,))])(*lands))


def _own_and_landed(lands, xs):
    chip = _chip()
    return [[jnp.where(chip == j, x, o.reshape(4, *x.shape)[j]) for j in range(4)] for o, x in zip(lands, xs)]


BIG = (("w_in", (D, IN_WIDTH // 4), 1), ("gla_w_o", (D // 4, D), 0), ("mla_w_uq", (MQR, MH * MQK // 4), 1),
       ("mla_w_ukv", (MKVR, MH * (MNOPE + MVD) // 4), 1), ("mla_w_o", (D // 4, D), 0), ("w_out", (D // 4, D), 0),
       ("mlp_w1", (D, DFF // 4), 1), ("mlp_w2", (DFF // 4, D), 0))
ADA_SHARD = (D, 6 * D // 4)
SMALL = (("b_ada", 6 * D), ("norm1_g", D), ("b_merge", 2 * D), ("gla_b_alpha", GH * GDK), ("gla_out_norm_g", GDV),
         ("mla_q_lat_g", MQR), ("mla_kv_lat_g", MKVR), ("mla_qn_g", MQK), ("mla_kn_g", MQK), ("norm2_g", D))


W_IN_SEGMENTS = ((0, 3072, OFF_Q), (3072, 3088, OFF_A), (3088, 3344, OFF_CQ), (3344, 3472, OFF_CKV),
                 (3472, 3504, OFF_KPE + MNOPE), (3504, 5552, OFF_MA))
W_IN_SPLIT = OFF_MA
SMALL_ROWS, SMALL_COLS = 32, 2 * D
W_ALPHA_ROW = 16
SMALL_RED = tuple((n, k) for n, k in SMALL if n != "b_ada")


def _pack_small(grads, d_w_alpha, *, name):
    def body(*refs):
        g_refs, wa_ref, out_ref = refs[:-2], refs[-2], refs[-1]
        out_ref[...] = jnp.zeros_like(out_ref)
        for i, ((_, k), g_ref) in enumerate(zip(SMALL_RED, g_refs)):
            out_ref[i:i + 1, 0:k] = g_ref[...]
        out_ref[W_ALPHA_ROW:W_ALPHA_ROW + GLR, 0:GH * GDK] = wa_ref[...]

    return pl.pallas_call(body, name=name, out_shape=jax.ShapeDtypeStruct((SMALL_ROWS, SMALL_COLS), F32))(*grads, d_w_alpha)


def _small_update(gathered, dmod_all, sel, wmv, *, name):
    names = [n for n, _ in SMALL] + ["gla_w_alpha"]
    n_par = len(names)

    def body(sel_ref, g_ref, dmod_ref, *refs):
        in_refs, out_refs, acc = refs[:3 * n_par], refs[3 * n_par:-1], refs[-1]
        total = g_ref[0]
        for j in range(1, 8):
            total = total + g_ref[j]
        acc[...] = total
        row = {n: i for i, (n, _) in enumerate(SMALL_RED)}
        for p, name_p in enumerate(names):
            w_ref, m_ref, v_ref = in_refs[3 * p:3 * p + 3]
            if name_p == "b_ada":
                gv = jnp.sum(dmod_ref[...], axis=0, keepdims=True)
            elif name_p == "gla_w_alpha":
                gv = jnp.zeros((GLR, GDK), F32)
                for j in range(4):
                    blk = acc[W_ALPHA_ROW:W_ALPHA_ROW + GLR, j * GDK:(j + 1) * GDK]
                    gv = gv + jnp.where(sel_ref[0] == j, blk, 0.0)
            else:
                gv = acc[row[name_p]:row[name_p] + 1, 0:w_ref.shape[1]]
            o = out_refs[4 * p:4 * p + 4]
            o[0][...] = gv
            o[1][...], o[2][...], o[3][...] = _adamw_update(w_ref[...], gv, m_ref[...], v_ref[...])

    flat = [a for t in wmv for a in t]
    out_shape = [jax.ShapeDtypeStruct(t[0].shape, F32) for t in wmv for _ in range(4)]
    vmem = pl.BlockSpec(memory_space=pltpu.VMEM)
    outs = pl.pallas_call(
        body, name=name, out_shape=out_shape, in_specs=[pl.BlockSpec(memory_space=pltpu.SMEM), vmem, vmem] + [vmem] * len(flat),
        out_specs=[vmem] * len(out_shape), scratch_shapes=[pltpu.VMEM((SMALL_ROWS, SMALL_COLS), F32)],
    )(sel, gathered, dmod_all, *flat)
    return {n: tuple(outs[4 * p:4 * p + 4]) for p, n in enumerate(names)}


def _full_weights(gathered):
    w = {name: jnp.concatenate(gathered[name], axis=axis) for name, _, axis in BIG if name in gathered and name != "w_in"}
    if "w_in" in gathered:
        shards = gathered["w_in"]
        zeros = lambda n: [jnp.zeros((D, n), shards[0].dtype)]

        def cols(a, b):
            width = IN_WIDTH // 4
            return [shards[j][:, max(a, j * width) - j * width:min(b, (j + 1) * width) - j * width]
                    for j in range(4) if max(a, j * width) < min(b, (j + 1) * width)]

        parts = []
        for a, b, at in sorted(W_IN_SEGMENTS, key=lambda seg: seg[2]):
            have = sum(p.shape[1] for p in parts)
            parts += (zeros(at - have) if at > have else []) + cols(a, b)
        w["w_in"] = jnp.concatenate(parts + zeros(PW - sum(p.shape[1] for p in parts)), axis=1)
    if "mla_w_uq" in w:
        w["mla_w_uq"] = jnp.pad(w["mla_w_uq"].reshape(MQR, MH, MQK), ((0, 0), (0, 0), (0, LANE - MQK))).reshape(MQR, MH * LANE)
    if "mla_w_o" in w:
        w["mla_w_o"] = jnp.pad(w["mla_w_o"].reshape(MH, MVD, D), ((0, 0), (0, LANE - MVD), (0, 0))).reshape(MH * LANE, D)
    return w


def _grad_slots(g):
    g = dict(g)
    out = {}
    if "w_in" in g:
        g_lo, g_hi = g.pop("w_in")
        take = lambda at, lo, hi: g_lo[:, at + lo:at + hi] if at < W_IN_SPLIT else g_hi[:, at - W_IN_SPLIT + lo:at - W_IN_SPLIT + hi]
        width = IN_WIDTH // 4
        slots = []
        for j in range(4):
            lo, hi = j * width, (j + 1) * width
            slots.append(jnp.concatenate([take(at, max(lo, a) - a, min(hi, b) - a)
                                          for a, b, at in W_IN_SEGMENTS if max(lo, a) < min(hi, b)], axis=1))
        out["w_in"] = jnp.stack(slots).reshape(4, 2, D // 2, width)
    if "mla_w_uq" in g:
        g["mla_w_uq"] = g["mla_w_uq"].reshape(MQR, MH, LANE)[:, :, :MQK].reshape(MQR, MH * MQK)
    if "mla_w_o" in g:
        g["mla_w_o"] = g["mla_w_o"].reshape(MH, LANE, D)[:, :MVD].reshape(MH * MVD, D)
    for name, (rows, cols), axis in BIG:
        if name not in g:
            continue
        a = g[name]
        a = a.reshape(4, rows, cols) if axis == 0 else jnp.transpose(a.reshape(rows, 4, cols), (1, 0, 2))
        out[name] = a.reshape(4, 2, rows // 2, cols)
    return out


def _rope_tables(positions):
    freqs = ROPE_THETA ** (-jnp.arange(0, MROPE, 2, dtype=F32) / MROPE)
    lane = np.arange(LANE)
    in_rope = (lane >= MNOPE) & (lane < MQK)
    freq_lane = jnp.where(in_rope, freqs[(lane - MNOPE) % (MROPE // 2)], 0.0)
    sign = np.where(in_rope, np.where(lane < MNOPE + MROPE // 2, -1.0, 1.0), 0.0).astype(np.float32)
    ang = positions.astype(F32).reshape(-1, 1) * freq_lane[None, :]
    return jnp.cos(ang), jnp.sin(ang) * sign[None, :]


def _local_step(x, positions, mod, target, w, small, more_weights=None, on_grads=None):
    kept = {}
    if on_grads is None:
        on_grads = lambda tag, grads, after: kept.update(grads)
    bsz, s, _ = x.shape
    t = bsz * s
    tt = _tile(t, 1024)
    shift1, scale1, gate1, shift2, scale2, gate2 = [mod[:, None, i * D:(i + 1) * D] for i in range(6)]
    cos_t, sin_t = _rope_tables(positions)
    w_alpha_p = jnp.pad(small["gla_w_alpha"], ((0, LANE - GLR), (0, 0)))
    gq = jnp.pad(small["mla_qn_g"], ((0, 0), (0, LANE - MQK)))
    gk = jnp.pad(small["mla_kn_g"], ((0, 0), (0, LANE - MQK)))
    flat2 = lambda a: a.reshape(t, a.shape[-1])
    bsd = lambda a: a.reshape(bsz, s, a.shape[-1])

    h = _norm_mod(x, small["norm1_g"], scale1, shift1, name="norm1")
    if callable(w):
        w = w(h)
    proj = _mm(flat2(h), w["w_in"], name="proj", tn=1152)
    proj3 = bsd(proj)
    o, o_gated, states = _gla_fwd(proj3, w_alpha_p, small["gla_b_alpha"], small["gla_out_norm_g"], name="gla_fwd")
    if more_weights is not None:
        w = {**w, **more_weights(o_gated)}
    y_a = _mm(flat2(o_gated), w["gla_w_o"], name="gla_out")
    cq_n, ckv_n = _lat_norm(proj, small["mla_q_lat_g"], small["mla_kv_lat_g"], name="lat_norm")
    q_raw = _mm(cq_n, w["mla_w_uq"], name="mla_uq")
    kv = _mm(ckv_n, w["mla_w_ukv"], name="mla_ukv")
    qf, kf, vf = _qk_prep(q_raw, kv, proj, cos_t, sin_t, gq * Q_PRESCALE, gk, name="qk_prep")
    o_attn = _attn_fwd(bsd(qf), bsd(kf), bsd(vf), name="attn_fwd")
    y_b = _mm(flat2(o_attn), w["mla_w_o"], name="mla_out")
    mixed_in = _merge_fwd(proj3, small["b_merge"], bsd(y_a), bsd(y_b), name="merge_fwd")
    mixed = _mm(flat2(mixed_in), w["w_out"], name="w_out")
    x1, h2 = _resid_norm_mod(x, bsd(mixed), gate1, small["norm2_g"], scale2, shift2, name="norm2")

    def sqrelu(acc, ex, outs):
        r = jnp.maximum(acc, 0.0)
        outs[0][...] = (r * r).astype(BF16)

    r = _mm(flat2(h2), w["mlp_w1"], name="mlp1", epilogue=sqrelu, out_shape=jax.ShapeDtypeStruct((t, DFF), BF16),
            out_specs=_tile_spec(tt, 1024))
    ff = _mm(r, w["mlp_w2"], name="mlp2")
    dy, dff, dgate2, loss_part = _loss_head(x1, bsd(ff), gate2, target, name="loss_head")

    g = {}

    def relu2_bwd(acc, ex, outs):
        outs[0][...] = (acc * (2.0 * jnp.sqrt(ex[0][...].astype(F32)))).astype(BF16)

    dff2 = flat2(dff)
    da1 = _mm(dff2, w["mlp_w2"], tb=True, name="mlp2_dx", epilogue=relu2_bwd, extras=(r,),
              extra_specs=(_tile_spec(tt, 1024),), out_shape=jax.ShapeDtypeStruct((t, DFF), BF16),
              out_specs=_tile_spec(tt, 1024))
    g["mlp_w2"] = _mm(r, dff2, ta=True, name="mlp2_dw")
    dh2 = _mm(da1, w["mlp_w1"], tb=True, name="mlp1_dx")
    g["mlp_w1"] = _mm(flat2(h2), da1, ta=True, name="mlp1_dw")
    token = on_grads("mlp", {n: g.pop(n) for n in ("mlp_w2", "mlp_w1")}, dh2)
    if token is not None:
        gate1 = gate1 + token[0, 0]
    dx1, dscale2, dshift2, dg2, dgate1, dmixed = _norm_mod_bwd(
        bsd(dh2), x1, dy, small["norm2_g"], scale2, gate1, bsd(mixed), name="norm2_bwd")
    dmixed2 = flat2(dmixed)
    dmi = _mm(dmixed2, w["w_out"], tb=True, name="w_out_dx")
    g["w_out"] = _mm(flat2(mixed_in), dmixed2, ta=True, name="w_out_dw")
    dy_a, dy_b, dl_a, dl_b, db_a, db_b = _merge_bwd(bsd(dmi), proj3, small["b_merge"], bsd(y_a), bsd(y_b), name="merge_bwd")
    dy_a2, dy_b2 = flat2(dy_a), flat2(dy_b)
    dog = _mm(dy_a2, w["gla_w_o"], tb=True, name="gla_out_dx")
    g["gla_w_o"] = _mm(flat2(o_gated), dy_a2, ta=True, name="gla_out_dw")
    dq_g, dk_g, dv_g, dg_g, dlog, db_alpha, d_ong = _gla_bwd(
        bsd(dog), o, states, proj3, w_alpha_p, small["gla_b_alpha"], small["gla_out_norm_g"], name="gla_bwd")
    dlog2 = flat2(dlog)
    da_p = _mm(dlog2, w_alpha_p, tb=True, out_dtype=BF16, name="alpha_dx")
    d_w_alpha = _mm(proj[:, OFF_A:OFF_A + LANE], dlog2, ta=True, name="alpha_dw")[:GLR]
    do_attn = _mm(dy_b2, w["mla_w_o"], tb=True, out_dtype=BF16, name="mla_out_dx")
    g["mla_w_o"] = _mm(flat2(o_attn), dy_b2, ta=True, name="mla_out_dw")
    dqf, dkf, dvf = _attn_bwd(bsd(qf), bsd(kf), bsd(vf), bsd(do_attn), name="attn_bwd")
    dq_raw, dkv, dkpe, dgq, dgk = _qk_prep_bwd(flat2(dqf), flat2(dkf), flat2(dvf), q_raw, kv, proj, cos_t, sin_t, gq, gk,
                                                name="qk_prep_bwd")
    dcq_n = _mm(dq_raw, w["mla_w_uq"], tb=True, name="mla_uq_dx")
    g["mla_w_uq"] = _mm(cq_n, dq_raw, ta=True, name="mla_uq_dw")
    dckv_n = _mm(dkv, w["mla_w_ukv"], tb=True, name="mla_ukv_dx")
    g["mla_w_ukv"] = _mm(ckv_n, dkv, ta=True, name="mla_ukv_dw")
    token = on_grads("mix", {n: g.pop(n) for n in ("w_out", "gla_w_o", "mla_w_o", "mla_w_uq", "mla_w_ukv")}, dckv_n)
    q_lat_g = small["mla_q_lat_g"] if token is None else small["mla_q_lat_g"] + token[0:1, 0:1]
    dcq, dckv, dg_qlat, dg_kvlat = _lat_norm_bwd(dcq_n, dckv_n, proj, q_lat_g, small["mla_kv_lat_g"],
                                                  name="lat_norm_bwd")
    pieces = [(flat2(dq_g), OFF_Q), (flat2(dk_g), OFF_K), (flat2(dv_g), OFF_V), (flat2(dg_g), OFF_G),
              (flat2(dl_a), OFF_MA), (flat2(dl_b), OFF_MB), (dcq, OFF_CQ), (dckv, OFF_CKV), (da_p, OFF_A), (dkpe, OFF_KPE)]
    hb = flat2(h)
    g_w_in = (_pieces_dw(hb, [p for p, off in pieces if off < W_IN_SPLIT], name="proj_dw_a"),
              _pieces_dw(hb, [p for p, off in pieces if off >= W_IN_SPLIT], name="proj_dw_b"))
    token = on_grads("in", {"w_in": g_w_in}, g_w_in[1])
    after = jnp.zeros((8, LANE), F32) if token is None else token
    dh = _pieces_dx(pieces, w["w_in"], after, name="proj_dx")
    token = on_grads("dx", {}, dh)
    if token is not None:
        scale1 = scale1 + token[0, 0]
    grad_x, dscale1, dshift1, dg1 = _norm_mod_bwd(bsd(dh), x, dx1, small["norm1_g"], scale1, name="norm1_bwd")

    dmod = jnp.concatenate([dshift1, dscale1, dgate1, dshift2, dscale2, dgate2], axis=-1).reshape(bsz, 6 * D)
    gs = {"norm1_g": dg1, "b_merge": jnp.concatenate([db_a, db_b], axis=1), "gla_b_alpha": db_alpha,
          "gla_out_norm_g": d_ong, "mla_q_lat_g": dg_qlat, "mla_kv_lat_g": dg_kvlat, "mla_qn_g": dgq[:, :MQK],
          "mla_kn_g": dgk[:, :MQK], "norm2_g": dg2}
    return loss_part[0, 0], grad_x, dmod, {**kept, **g}, gs, d_w_alpha


def kernel(x, c, positions, w_ada, b_ada, norm1_g, w_in, b_merge, gla_w_alpha, gla_b_alpha, gla_out_norm_g, gla_w_o, mla_q_lat_g, mla_w_uq, mla_kv_lat_g, mla_w_ukv, mla_qn_g, mla_kn_g, mla_w_o, w_out, norm2_g, mlp_w1, mlp_w2, loss_target, m_w_ada, m_b_ada, m_norm1_g, m_w_in, m_b_merge, m_gla_w_alpha, m_gla_b_alpha, m_gla_out_norm_g, m_gla_w_o, m_mla_q_lat_g, m_mla_w_uq, m_mla_kv_lat_g, m_mla_w_ukv, m_mla_qn_g, m_mla_kn_g, m_mla_w_o, m_w_out, m_norm2_g, m_mlp_w1, m_mlp_w2, v_w_ada, v_b_ada, v_norm1_g, v_w_in, v_b_merge, v_gla_w_alpha, v_gla_b_alpha, v_gla_out_norm_g, v_gla_w_o, v_mla_q_lat_g, v_mla_w_uq, v_mla_kv_lat_g, v_mla_w_ukv, v_mla_qn_g, v_mla_kn_g, v_mla_w_o, v_w_out, v_norm2_g, v_mlp_w1, v_mlp_w2):
    args = dict(locals())
    names_big = [n for n, _, _ in BIG]
    names_small = [n for n, _ in SMALL]
    bsz = x.shape[0]
    ax, ay, ac = lax.axis_index("x"), lax.axis_index("y"), lax.axis_index("c")
    chip = 2 * ax + ay
    dev = 2 * chip + ac

    small = {n: args[n] for n in names_small}
    sel_c = jnp.reshape(ac, (1,)).astype(jnp.int32)
    sel_chip = jnp.reshape(chip, (1,)).astype(jnp.int32)
    c_all, w_alpha_all = _all_gather8([c, gla_w_alpha[0]], name="comm_c_alpha")
    small["gla_w_alpha"] = jnp.concatenate([w_alpha_all[2 * j] for j in range(4)], axis=1)
    c_all = c_all.reshape(8 * bsz, D)

    shards = {n: args[n][0].astype(BF16) for n in names_big}
    halves_of = lambda names: [shards[n].reshape(2, shards[n].shape[0] // 2, shards[n].shape[1]) for n in names]

    def gather_start(names, deps, tag):
        xs = halves_of(names)
        lands = [lax.empty((4, *xh.shape), BF16) for xh in xs]
        plan = _gather_plan(len(names))
        return names, plan, _rdma_start(xs + lands, 3 * len(names), plan, deps, name="comm_weights_start_" + tag)

    def gather_finish(started, after, tag):
        names, plan, sems = started
        arrs = _rdma_wait(sems[0], sems[1], sems[2], plan, after, name="comm_weights_wait_" + tag)
        filled = _pair_fill(arrs[len(names):], name="comm_weights_pair_" + tag)
        own = [a.reshape(shards[n].shape) for n, a in zip(names, arrs)]
        return _full_weights(dict(zip(names, _own_and_landed(filled, own))))

    first = gather_start(["w_in"], (c_all,), "in")
    c_all = c_all + first[2][3][0, 0]


    def add_bias(acc, ex, outs):
        outs[0][...] = acc + ex[0][...]

    silu = lambda v: v * _sigmoid(v)
    b_ada_mine = lax.dynamic_slice(b_ada, (0, chip * ADA_SHARD[1]), (1, ADA_SHARD[1]))
    mod_part = _mm(c_all, w_ada[0], name="ada", tn=512, a_fn=silu, epilogue=add_bias, extras=(b_ada_mine,),
                   extra_specs=(pl.BlockSpec((1, 512), lambda i, j, k: (0, j)),),
                   out_shape=jax.ShapeDtypeStruct((8 * bsz, ADA_SHARD[1]), F32), out_specs=_tile_spec(8 * bsz, 512))
    mod_all = _all_gather8([mod_part], name="comm_mod")[0]
    mod_rows = lax.dynamic_slice(mod_all, (0, dev * bsz, 0), (8, bsz, ADA_SHARD[1]))
    mod = jnp.concatenate([mod_rows[2 * j] for j in range(4)], axis=1)
    rest = gather_start([n for n in names_big if n != "w_in"], (mod,), "rest")
    mod = mod + rest[2][3][0, 0]
    w_in_after = lambda after: gather_finish(first, after, "in")
    more_weights = lambda after: gather_finish(rest, after, "rest")

    stage = {}

    def begin(tag, names, arrays, lands, n_copies, plan, what):
        stage[tag] = (names, plan, _rdma_start(arrays + lands, n_copies, plan, (), name=f"comm_{what}_start_{tag}"))
        return stage[tag][2][3]

    def landed(tag, after, what):
        names, plan, sems = stage[tag]
        arrs = _rdma_wait(sems[0], sems[1], sems[2], plan, after, name=f"comm_{what}_wait_{tag}")
        return names, arrs[:len(arrs) // 2], arrs[len(arrs) // 2:]

    def swap_start(tag, grads):
        names = list(grads)
        parts = [_grad_slots(grads)[n] for n in names]
        lands = [lax.empty((4, *p.shape[2:]), F32) for p in parts]
        return begin(tag, names, parts, lands, len(names), _sibling_plan(len(names), lambda r, c: r.at[:, 1 - c]), "pair_sum")

    def scatter_start(tag, after):
        names, parts, sib_halves = landed(tag, after, "pair_sum")
        pairs = [_pair_add(p, s, sel_c, name="pair_add_" + n) for n, p, s in zip(names, parts, sib_halves)]
        recvs = [lax.empty((3, *p.shape[1:]), BF16) for p in pairs]
        return begin(tag, names, pairs, recvs, 3 * len(names), _scatter_plan(len(names)), "scatter")

    def join_start(tag, after):
        names, pairs, recvs = landed(tag, after, "scatter")
        halves = [_chip_sum(p, r, sel_chip, name="chip_sum_" + n) for n, p, r in zip(names, pairs, recvs)]
        lands = [lax.empty(h.shape, F32) for h in halves]
        return begin(tag, names, halves, lands, len(names), _sibling_plan(len(names), lambda r, c: r), "pair_join")

    def reduce_step(tag, grads, after):
        if tag == "mlp":
            return swap_start("mlp", grads)
        if tag == "mix":
            return scatter_start("mlp", after) + swap_start("mix", grads)
        if tag == "in":
            return scatter_start("mix", after) + swap_start("in", grads)
        return scatter_start("in", after)

    loss_part, grad_x, dmod, g, gs, d_w_alpha = _local_step(x, positions, mod, loss_target, w_in_after, small,
                                                            more_weights, reduce_step)
    loss = lax.psum(loss_part * (0.5 / D), ("x", "y", "c"))

    assert not g, list(g)
    gs_packed = _pack_small([gs[n] for n, _ in SMALL_RED], d_w_alpha, name="pack_small")
    small_lands = [lax.empty((8, *a.shape), F32) for a in (dmod, gs_packed)]
    begin("small", ["dmod", "small"], [dmod, gs_packed], small_lands, 14, _gather8_plan(2), "gather8")

    res = {}

    def finish(tag, after):
        names, halves, theirs = landed(tag, after, "pair_join")
        for n, mine, other in zip(names, halves, theirs):
            if n == "w_in":
                south = ac == 0
                g_t = jnp.concatenate([jnp.where(south, mine, other), jnp.where(south, other, mine)], axis=0).T
                outs = _adamw(w_in[0].T, g_t, m_w_in[0].T, v_w_in[0].T, name="adamw_w_in", by_cols=True)
                res[n] = tuple(a.T for a in (g_t, *outs))
            else:
                res[n] = _adamw_halves(args[n][0], args["m_" + n][0], args["v_" + n][0], mine, other, sel_c,
                                       name="adamw_" + n)
        return res[names[-1]][1]

    join_start("mlp", grad_x)
    join_start("mix", grad_x)
    done = finish("mix", finish("mlp", grad_x))

    _, (dmod_own, gs_own), (dmod_all, gs_all) = landed("small", done, "gather8")
    dmod_all = lax.dynamic_update_slice(dmod_all, dmod_own[None], (dev, 0, 0)).reshape(8 * bsz, 6 * D)
    gs_all = lax.dynamic_update_slice(gs_all, gs_own[None], (dev, 0, 0))
    dmod_mine = lax.dynamic_slice(dmod_all, (0, chip * ADA_SHARD[1]), (8 * bsz, ADA_SHARD[1]))
    g_w_ada = _mm(c_all, dmod_mine, ta=True, a_fn=silu, name="ada_dw")
    wmv = [(args[n], args["m_" + n], args["v_" + n]) for n in names_small]
    wmv.append((gla_w_alpha[0], m_gla_w_alpha[0], v_gla_w_alpha[0]))
    res.update(_small_update(gs_all, dmod_all, sel_chip, wmv, name="small_update"))
    join_start("in", g_w_ada)
    res["w_ada"] = (g_w_ada, *_adamw(w_ada[0], g_w_ada, m_w_ada[0], v_w_ada[0], name="adamw_w_ada"))
    finish("in", res["w_ada"][1])

    order = ["w_ada", "b_ada", "norm1_g", "w_in", "b_merge", "gla_w_alpha", "gla_b_alpha", "gla_out_norm_g", "gla_w_o",
             "mla_q_lat_g", "mla_w_uq", "mla_kv_lat_g", "mla_w_ukv", "mla_qn_g", "mla_kn_g", "mla_w_o", "w_out",
             "norm2_g", "mlp_w1", "mlp_w2"]
    named = lambda k: [res[n][k].reshape(args[n].shape) for n in order]
    return (loss, grad_x, *named(0), *named(1), *named(2), *named(3))
```

```python
import functools

import jax
import jax.numpy as jnp
import numpy as np
from jax import lax
from jax.experimental import pallas as pl
from jax.experimental.pallas import tpu as pltpu

F32 = jnp.float32
BF16 = jnp.bfloat16
MESH = pl.DeviceIdType.MESH

D = 1024
CHUNK = 64
EPS = 1e-6
GH, GDK, GDV, GLR, GTAU = 4, 128, 256, 16, 16.0
MH, MQR, MKVR, MNOPE, MROPE, MVD = 16, 256, 128, 64, 32, 64
MQK = MNOPE + MROPE
DFF = 4 * D
ROPE_THETA = 10000.0
IN_WIDTH = 5552
LANE = 128
OFF_Q, OFF_K, OFF_V, OFF_G, OFF_MA, OFF_MB, OFF_CQ, OFF_CKV, OFF_A, OFF_KPE, PW = (
    0, 512, 1024, 2048, 3072, 4096, 5120, 5376, 5504, 5632, 5760)
ADAM_LR, ADAM_B1, ADAM_B2, ADAM_EPS, ADAM_WD, ADAM_STEP = 0.001, 0.9, 0.999, 1e-08, 0.01, 10
VMEM_LIMIT = 48 * 1024 * 1024


def _params(n_axes):
    return pltpu.CompilerParams(dimension_semantics=("arbitrary",) * n_axes, vmem_limit_bytes=VMEM_LIMIT)


def _tile(n, target):
    if n <= target:
        return n
    best = None
    for t in range(LANE, target + 1, LANE):
        if n % t == 0:
            best = t
    assert best is not None, (n, target)
    return best


def _sigmoid(x):
    return 1.0 / (1.0 + jnp.exp(-x))


def _mm(a, b, *, name, ta=False, tb=False, out_dtype=F32, tm=1024, tn=1024, tk=1024,
        epilogue=None, extras=(), extra_specs=(), out_shape=None, out_specs=None, a_fn=None):
    if ta:
        kdim, m = a.shape
    else:
        m, kdim = a.shape
    if tb:
        n, k2 = b.shape
    else:
        k2, n = b.shape
    assert kdim == k2, (a.shape, b.shape)
    tm, tn, tk = _tile(m, tm), _tile(n, tn), _tile(kdim, tk)
    nk = kdim // tk
    a_spec = pl.BlockSpec((tk, tm), lambda i, j, k: (k, i)) if ta else pl.BlockSpec((tm, tk), lambda i, j, k: (i, k))
    b_spec = pl.BlockSpec((tn, tk), lambda i, j, k: (j, k)) if tb else pl.BlockSpec((tk, tn), lambda i, j, k: (k, j))
    dims = (((0 if ta else 1,), (1 if tb else 0,)), ((), ()))
    ne = len(extras)
    if out_shape is None:
        out_shape = jax.ShapeDtypeStruct((m, n), out_dtype)
        out_specs = pl.BlockSpec((tm, tn), lambda i, j, k: (i, j))
    n_out = len(out_shape) if isinstance(out_shape, (list, tuple)) else 1
    in_place = epilogue is None and n_out == 1 and out_shape.dtype == F32
    scratch = [] if (nk == 1 or in_place) else [pltpu.VMEM((tm, tn), F32)]

    def body(a_ref, b_ref, *rest):
        ex, outs = rest[:ne], rest[ne:ne + n_out]
        av = a_ref[...] if a_fn is None else a_fn(a_ref[...])
        prod = lax.dot_general(av.astype(BF16), b_ref[...].astype(BF16), dims, preferred_element_type=F32)

        def finish(val):
            if epilogue is None:
                outs[0][...] = val.astype(outs[0].dtype)
            else:
                epilogue(val, ex, outs)

        if nk == 1:
            finish(prod)
            return
        k = pl.program_id(2)
        acc = outs[0] if in_place else rest[-1]

        @pl.when(k == 0)
        def _():
            acc[...] = prod

        @pl.when(k > 0)
        def _():
            acc[...] += prod

        if not in_place:
            @pl.when(k == nk - 1)
            def _():
                finish(acc[...])

    return pl.pallas_call(
        body, name=name, grid=(m // tm, n // tn, nk),
        in_specs=[a_spec, b_spec, *extra_specs], out_specs=out_specs, out_shape=out_shape,
        scratch_shapes=scratch, compiler_params=_params(3),
    )(a, b, *extras)


def _tile_spec(tm, tn):
    return pl.BlockSpec((tm, tn), lambda i, j, k: (i, j))


def _pieces_dx(pieces, w, after, *, name, tm=256):
    t = pieces[0][0].shape[0]
    tm = _tile(t, tm)
    npc = len(pieces)

    def body(*refs):
        p_refs, w_ref, out_ref = refs[:npc], refs[npc], refs[-1]
        acc = None
        for (arr, off), p_ref in zip(pieces, p_refs):
            part = lax.dot_general(p_ref[...].astype(BF16), w_ref[:, off:off + arr.shape[1]], _NT,
                                   preferred_element_type=F32)
            acc = part if acc is None else acc + part
        out_ref[...] = acc

    return pl.pallas_call(
        body, name=name, grid=(t // tm,),
        in_specs=[pl.BlockSpec((tm, arr.shape[1]), lambda i: (i, 0)) for arr, _ in pieces]
        + [pl.BlockSpec(w.shape, lambda i: (0, 0)), pl.BlockSpec((8, LANE), lambda i: (0, 0))],
        out_specs=pl.BlockSpec((tm, w.shape[0]), lambda i: (i, 0)),
        out_shape=jax.ShapeDtypeStruct((t, w.shape[0]), F32), compiler_params=_params(1),
    )(*[arr for arr, _ in pieces], w, after)


def _pieces_dw(h, pieces, *, name, tk=1024):
    t, d = h.shape
    tk = _tile(t, tk)
    widths = [p.shape[1] for p in pieces]
    starts = [sum(widths[:i]) for i in range(len(pieces))]

    def body(h_ref, *refs):
        p_refs, out_ref = refs[:-1], refs[-1]
        first = pl.program_id(0) == 0
        hv = h_ref[...]
        for p_ref, start, width in zip(p_refs, starts, widths):
            part = lax.dot_general(hv, p_ref[...].astype(BF16), _TN, preferred_element_type=F32)
            cols = slice(start, start + width)

            @pl.when(first)
            def _():
                out_ref[:, cols] = part

            @pl.when(jnp.logical_not(first))
            def _():
                out_ref[:, cols] += part

    return pl.pallas_call(
        body, name=name, grid=(t // tk,),
        in_specs=[pl.BlockSpec((tk, d), lambda k: (k, 0))] + [pl.BlockSpec((tk, wd), lambda k: (k, 0)) for wd in widths],
        out_specs=pl.BlockSpec((d, sum(widths)), lambda k: (0, 0)),
        out_shape=jax.ShapeDtypeStruct((d, sum(widths)), F32), compiler_params=_params(1),
    )(h, *pieces)


def _rms(x, g):
    r = lax.rsqrt(jnp.mean(x * x, axis=-1, keepdims=True) + EPS)
    return x * r, r


def _row_spec(ts, width, col=0):
    return pl.BlockSpec((None, ts, width), lambda b, i: (b, i, col))


def _vec_spec(width):
    return pl.BlockSpec((None, 1, width), lambda b, i: (b, 0, 0))


def _gain_spec(width):
    return pl.BlockSpec((1, width), lambda b, i: (0, 0))


def _norm_mod(x, g, scale, shift, *, name, ts=256):
    bsz, s, d = x.shape
    ts = min(ts, s)

    def body(x_ref, g_ref, sc_ref, sh_ref, h_ref):
        xh, _ = _rms(x_ref[...], None)
        h_ref[...] = ((xh * g_ref[...]) * (1.0 + sc_ref[...]) + sh_ref[...]).astype(BF16)

    return pl.pallas_call(
        body, name=name, grid=(bsz, s // ts),
        in_specs=[_row_spec(ts, d), _gain_spec(d), _vec_spec(d), _vec_spec(d)],
        out_specs=_row_spec(ts, d), out_shape=jax.ShapeDtypeStruct((bsz, s, d), BF16),
        compiler_params=_params(2),
    )(x, g, scale, shift)


def _resid_norm_mod(x, mixed, gate, g, scale, shift, *, name, ts=256):
    bsz, s, d = x.shape
    ts = min(ts, s)

    def body(x_ref, mx_ref, gt_ref, g_ref, sc_ref, sh_ref, x1_ref, h_ref):
        x1 = x_ref[...] + gt_ref[...] * mx_ref[...]
        x1_ref[...] = x1
        xh, _ = _rms(x1, None)
        h_ref[...] = ((xh * g_ref[...]) * (1.0 + sc_ref[...]) + sh_ref[...]).astype(BF16)

    return pl.pallas_call(
        body, name=name, grid=(bsz, s // ts),
        in_specs=[_row_spec(ts, d), _row_spec(ts, d), _vec_spec(d), _gain_spec(d), _vec_spec(d), _vec_spec(d)],
        out_specs=[_row_spec(ts, d), _row_spec(ts, d)],
        out_shape=[jax.ShapeDtypeStruct((bsz, s, d), F32), jax.ShapeDtypeStruct((bsz, s, d), BF16)],
        compiler_params=_params(2),
    )(x, mixed, gate, g, scale, shift)


def _norm_mod_bwd(dh, xin, resid, g, scale, gate=None, mixed=None, *, name, ts=256):
    bsz, s, d = xin.shape
    ts = min(ts, s)
    gated = gate is not None

    def body(*refs):
        if gated:
            dh_ref, x_ref, rs_ref, g_ref, sc_ref, gt_ref, mx_ref, dx_ref, dsc_ref, dsh_ref, dg_ref, dgt_ref, dmx_ref = refs
        else:
            dh_ref, x_ref, rs_ref, g_ref, sc_ref, dx_ref, dsc_ref, dsh_ref, dg_ref = refs
        b, i = pl.program_id(0), pl.program_id(1)

        @pl.when(i == 0)
        def _():
            dsc_ref[...] = jnp.zeros_like(dsc_ref)
            dsh_ref[...] = jnp.zeros_like(dsh_ref)
            if gated:
                dgt_ref[...] = jnp.zeros_like(dgt_ref)

        @pl.when((i == 0) & (b == 0))
        def _():
            dg_ref[...] = jnp.zeros_like(dg_ref)

        dh_v, gv = dh_ref[...], g_ref[...]
        xh, r = _rms(x_ref[...], None)
        dsc_ref[...] += jnp.sum(dh_v * (xh * gv), axis=0, keepdims=True)
        dsh_ref[...] += jnp.sum(dh_v, axis=0, keepdims=True)
        dn = dh_v * (1.0 + sc_ref[...])
        dg_ref[...] += jnp.sum(dn * xh, axis=0, keepdims=True)
        dxh = dn * gv
        dx = rs_ref[...] + r * (dxh - xh * jnp.mean(dxh * xh, axis=-1, keepdims=True))
        dx_ref[...] = dx
        if gated:
            dgt_ref[...] += jnp.sum(dx * mx_ref[...], axis=0, keepdims=True)
            dmx_ref[...] = (dx * gt_ref[...]).astype(BF16)

    ins = [dh, xin, resid, g, scale]
    in_specs = [_row_spec(ts, d), _row_spec(ts, d), _row_spec(ts, d), _gain_spec(d), _vec_spec(d)]
    out_specs = [_row_spec(ts, d), _vec_spec(d), _vec_spec(d), _gain_spec(d)]
    out_shape = [jax.ShapeDtypeStruct((bsz, s, d), F32), jax.ShapeDtypeStruct((bsz, 1, d), F32),
                 jax.ShapeDtypeStruct((bsz, 1, d), F32), jax.ShapeDtypeStruct((1, d), F32)]
    if gated:
        ins += [gate, mixed]
        in_specs += [_vec_spec(d), _row_spec(ts, d)]
        out_specs += [_vec_spec(d), _row_spec(ts, d)]
        out_shape += [jax.ShapeDtypeStruct((bsz, 1, d), F32), jax.ShapeDtypeStruct((bsz, s, d), BF16)]
    return pl.pallas_call(
        body, name=name, grid=(bsz, s // ts), in_specs=in_specs, out_specs=out_specs, out_shape=out_shape,
        compiler_params=_params(2),
    )(*ins)


def _loss_head(x1, ff, gate2, target, *, name, ts=256):
    bsz, s, d = x1.shape
    ts = min(ts, s)

    def body(x1_ref, ff_ref, gt_ref, t_ref, dy_ref, dff_ref, dgt_ref, loss_ref, acc):
        b, i = pl.program_id(0), pl.program_id(1)

        @pl.when(i == 0)
        def _():
            dgt_ref[...] = jnp.zeros_like(dgt_ref)

        @pl.when((i == 0) & (b == 0))
        def _():
            acc[...] = jnp.zeros_like(acc)

        ffv, gt = ff_ref[...], gt_ref[...]
        diff = (x1_ref[...] + gt * ffv) - t_ref[...]
        acc[...] += jnp.sum((diff * diff).reshape(ts // 8, 8, d), axis=0)
        dy = diff * (1.0 / d)
        dy_ref[...] = dy
        dgt_ref[...] += jnp.sum(dy * ffv, axis=0, keepdims=True)
        dff_ref[...] = (dy * gt).astype(BF16)

        @pl.when((i == pl.num_programs(1) - 1) & (b == pl.num_programs(0) - 1))
        def _():
            loss_ref[...] = jnp.full(loss_ref.shape, jnp.sum(acc[...]), F32)

    return pl.pallas_call(
        body, name=name, grid=(bsz, s // ts),
        in_specs=[_row_spec(ts, d), _row_spec(ts, d), _vec_spec(d), _row_spec(ts, d)],
        out_specs=[_row_spec(ts, d), _row_spec(ts, d), _vec_spec(d), pl.BlockSpec((8, LANE), lambda b, i: (0, 0))],
        out_shape=[jax.ShapeDtypeStruct((bsz, s, d), F32), jax.ShapeDtypeStruct((bsz, s, d), BF16),
                   jax.ShapeDtypeStruct((bsz, 1, d), F32), jax.ShapeDtypeStruct((8, LANE), F32)],
        scratch_shapes=[pltpu.VMEM((8, d), F32)], compiler_params=_params(2),
    )(x1, ff, gate2, target)


def _merge_fwd(proj, b_merge, y_a, y_b, *, name, ts=256):
    bsz, s, _ = proj.shape
    ts = min(ts, s)

    def body(la_ref, lb_ref, ba_ref, bb_ref, ya_ref, yb_ref, out_ref):
        ga = _sigmoid(la_ref[...] + ba_ref[...])
        gb = _sigmoid(lb_ref[...] + bb_ref[...])
        out_ref[...] = (ga * ya_ref[...] + gb * yb_ref[...]).astype(BF16)

    return pl.pallas_call(
        body, name=name, grid=(bsz, s // ts),
        in_specs=[_row_spec(ts, D, OFF_MA // D), _row_spec(ts, D, OFF_MB // D),
                  pl.BlockSpec((1, D), lambda b, i: (0, 0)), pl.BlockSpec((1, D), lambda b, i: (0, 1)),
                  _row_spec(ts, D), _row_spec(ts, D)],
        out_specs=_row_spec(ts, D), out_shape=jax.ShapeDtypeStruct((bsz, s, D), BF16),
        compiler_params=_params(2),
    )(proj, proj, b_merge, b_merge, y_a, y_b)


def _merge_bwd(dmi, proj, b_merge, y_a, y_b, *, name, ts=256):
    bsz, s, _ = proj.shape
    ts = min(ts, s)

    def body(d_ref, la_ref, lb_ref, ba_ref, bb_ref, ya_ref, yb_ref, dya_ref, dyb_ref, dla_ref, dlb_ref, dba_ref, dbb_ref):
        @pl.when((pl.program_id(0) == 0) & (pl.program_id(1) == 0))
        def _():
            dba_ref[...] = jnp.zeros_like(dba_ref)
            dbb_ref[...] = jnp.zeros_like(dbb_ref)

        dv = d_ref[...]
        ga = _sigmoid(la_ref[...] + ba_ref[...])
        gb = _sigmoid(lb_ref[...] + bb_ref[...])
        dya_ref[...] = (dv * ga).astype(BF16)
        dyb_ref[...] = (dv * gb).astype(BF16)
        dla = (dv * ya_ref[...]) * (ga * (1.0 - ga))
        dlb = (dv * yb_ref[...]) * (gb * (1.0 - gb))
        dla_ref[...] = dla.astype(BF16)
        dlb_ref[...] = dlb.astype(BF16)
        dba_ref[...] += jnp.sum(dla, axis=0, keepdims=True)
        dbb_ref[...] += jnp.sum(dlb, axis=0, keepdims=True)

    act = jax.ShapeDtypeStruct((bsz, s, D), BF16)
    return pl.pallas_call(
        body, name=name, grid=(bsz, s // ts),
        in_specs=[_row_spec(ts, D), _row_spec(ts, D, OFF_MA // D), _row_spec(ts, D, OFF_MB // D),
                  pl.BlockSpec((1, D), lambda b, i: (0, 0)), pl.BlockSpec((1, D), lambda b, i: (0, 1)),
                  _row_spec(ts, D), _row_spec(ts, D)],
        out_specs=[_row_spec(ts, D)] * 4 + [_gain_spec(D)] * 2,
        out_shape=[act, act, act, act, jax.ShapeDtypeStruct((1, D), F32), jax.ShapeDtypeStruct((1, D), F32)],
        compiler_params=_params(2),
    )(dmi, proj, proj, b_merge, b_merge, y_a, y_b)


def _tri(lower):
    r = lax.broadcasted_iota(jnp.int32, (CHUNK, CHUNK), 0)
    c = lax.broadcasted_iota(jnp.int32, (CHUNK, CHUNK), 1)
    return jnp.where((c <= r) if lower else (c >= r), 1.0, 0.0).astype(F32)


def _gla_logits(a_ref, wal_ref, bal_ref):
    logits = jnp.dot(a_ref[...].astype(BF16), wal_ref[...].astype(BF16), preferred_element_type=F32) + bal_ref[...]
    la = (jnp.minimum(logits, 0.0) - jnp.log(1.0 + jnp.exp(-jnp.abs(logits)))) * (1.0 / GTAU)
    return logits, la


def _chunk_cumsum(la_n, tri):
    cum = jnp.dot(tri, la_n, preferred_element_type=F32, precision=lax.Precision.HIGHEST)
    return cum, jnp.sum(la_n, axis=0, keepdims=True)


def _gla_specs(s, nc):
    def blk(width, off):
        return pl.BlockSpec((None, s, width), lambda h, b: (b, 0, off // width + h))

    proj_specs = [blk(GDK, OFF_Q), blk(GDK, OFF_K), blk(GDV, OFF_V), blk(GDV, OFF_G),
                  pl.BlockSpec((None, s, LANE), lambda h, b: (b, 0, OFF_A // LANE)),
                  pl.BlockSpec((LANE, GDK), lambda h, b: (0, h)), pl.BlockSpec((1, GDK), lambda h, b: (0, h)),
                  pl.BlockSpec((1, GDV), lambda h, b: (0, 0))]
    st_spec = pl.BlockSpec((None, None, nc, GDV, GDK), lambda h, b: (b, h, 0, 0, 0))
    return blk, proj_specs, st_spec


def _gla_fwd(proj, w_alpha_p, b_alpha, out_norm_g, *, name):
    bsz, s, _ = proj.shape
    nc = s // CHUNK
    scale = GDK ** -0.5

    rb = min(512, s)

    def body(q_ref, k_ref, v_ref, g_ref, a_ref, wal_ref, bal_ref, ong_ref, o_ref, og_ref, st_ref):
        _, la = _gla_logits(a_ref, wal_ref, bal_ref)
        tri = _tri(True)
        st = jnp.zeros((GDV, GDK), F32)
        for n in range(nc):
            rows = pl.ds(n * CHUNK, CHUNK)
            cum, cum_end = _chunk_cumsum(la[n * CHUNK:(n + 1) * CHUNK], tri)
            kd = k_ref[rows, :] * jnp.exp(cum_end - cum)
            ut = lax.dot_general(v_ref[rows, :].astype(BF16), kd.astype(BF16), _TN, preferred_element_type=F32)
            st = st * jnp.exp(cum_end) + ut
            st_ref[n] = st
            o_ref[rows, :] = lax.dot_general((q_ref[rows, :] * scale).astype(BF16), st.astype(BF16), _NT,
                                             preferred_element_type=F32)
        for j in range(0, s, rb):
            blk_rows = pl.ds(j, rb)
            oh, _ = _rms(o_ref[blk_rows, :], None)
            gv = g_ref[blk_rows, :]
            og_ref[blk_rows, :] = ((oh * ong_ref[...]) * (gv * _sigmoid(gv))).astype(BF16)

    blk, proj_specs, st_spec = _gla_specs(s, nc)
    return pl.pallas_call(
        body, name=name, grid=(GH, bsz), in_specs=proj_specs, out_specs=[blk(GDV, 0), blk(GDV, 0), st_spec],
        out_shape=[jax.ShapeDtypeStruct((bsz, s, GH * GDV), F32), jax.ShapeDtypeStruct((bsz, s, GH * GDV), BF16),
                   jax.ShapeDtypeStruct((bsz, GH, nc, GDV, GDK), F32)],
        compiler_params=_params(2),
    )(proj, proj, proj, proj, proj, w_alpha_p, b_alpha, out_norm_g)


def _gla_bwd(dog, o, states, proj, w_alpha_p, b_alpha, out_norm_g, *, name):
    bsz, s, _ = proj.shape
    nc = s // CHUNK
    scale = GDK ** -0.5

    def body(dog_ref, o_ref, st_ref, q_ref, k_ref, v_ref, g_ref, a_ref, wal_ref, bal_ref, ong_ref,
             dq_ref, dk_ref, dv_ref, dg_ref, dl_ref, dbal_ref, dong_ref, do_scr, dlog_scr):
        h, b = pl.program_id(0), pl.program_id(1)

        @pl.when(b == 0)
        def _():
            dbal_ref[...] = jnp.zeros_like(dbal_ref)

        @pl.when((b == 0) & (h == 0))
        def _():
            dong_ref[...] = jnp.zeros_like(dong_ref)

        ong = ong_ref[...]
        for j in range(0, s, rb):
            blk_rows = pl.ds(j, rb)
            gv, dogv = g_ref[blk_rows, :], dog_ref[blk_rows, :]
            sg = _sigmoid(gv)
            oh, r = _rms(o_ref[blk_rows, :], None)
            don = dogv * (gv * sg)
            dg_ref[blk_rows, :] = (dogv * (oh * ong) * (sg * (1.0 + gv * (1.0 - sg)))).astype(BF16)
            dong_ref[...] += jnp.sum(don * oh, axis=0, keepdims=True)
            doh = don * ong
            do_scr[blk_rows, :] = (r * (doh - oh * jnp.mean(doh * oh, axis=-1, keepdims=True))).astype(BF16)

        logits, la = _gla_logits(a_ref, wal_ref, bal_ref)
        tri_lo, tri_up = _tri(True), _tri(False)
        carry = jnp.zeros((GDV, GDK), F32)
        for n in range(nc - 1, -1, -1):
            rows = pl.ds(n * CHUNK, CHUNK)
            cum, cum_end = _chunk_cumsum(la[n * CHUNK:(n + 1) * CHUNK], tri_lo)
            decay = jnp.exp(cum_end)
            w = jnp.exp(cum_end - cum)
            kd = k_ref[rows, :] * w
            do_b = do_scr[rows, :]
            qs_b = (q_ref[rows, :] * scale).astype(BF16)
            dq_ref[rows, :] = (jnp.dot(do_b, st_ref[n].astype(BF16), preferred_element_type=F32) * scale).astype(BF16)
            dsn = lax.dot_general(do_b, qs_b, _TN, preferred_element_type=F32) + carry
            carry = dsn * decay
            dsn_b = dsn.astype(BF16)
            dv_ref[rows, :] = lax.dot_general(kd.astype(BF16), dsn_b, _NT, preferred_element_type=F32).astype(BF16)
            dkd = jnp.dot(v_ref[rows, :].astype(BF16), dsn_b, preferred_element_type=F32)
            dk_ref[rows, :] = (dkd * w).astype(BF16)
            e = dkd * kd
            dcum_end = jnp.sum(e, axis=0, keepdims=True)
            if n > 0:
                dcum_end += jnp.sum(dsn * st_ref[n - 1], axis=0, keepdims=True) * decay
            dlog_scr[rows, :] = dcum_end - jnp.dot(tri_up, e, preferred_element_type=F32,
                                                  precision=lax.Precision.HIGHEST)
        dlog = dlog_scr[...] * (1.0 / GTAU) * (1.0 - _sigmoid(logits))
        dl_ref[...] = dlog.astype(BF16)
        dbal_ref[...] += jnp.sum(dlog, axis=0, keepdims=True)

    rb = min(512, s)

    blk, proj_specs, st_spec = _gla_specs(s, nc)
    act = lambda wd: jax.ShapeDtypeStruct((bsz, s, wd), BF16)
    return pl.pallas_call(
        body, name=name, grid=(GH, bsz), in_specs=[blk(GDV, 0), blk(GDV, 0), st_spec, *proj_specs],
        out_specs=[blk(GDK, 0), blk(GDK, 0), blk(GDV, 0), blk(GDV, 0), blk(GDK, 0),
                   pl.BlockSpec((1, GDK), lambda h, b: (0, h)), pl.BlockSpec((1, GDV), lambda h, b: (0, 0))],
        out_shape=[act(GH * GDK), act(GH * GDK), act(GH * GDV), act(GH * GDV), act(GH * GDK),
                   jax.ShapeDtypeStruct((1, GH * GDK), F32), jax.ShapeDtypeStruct((1, GDV), F32)],
        scratch_shapes=[pltpu.VMEM((s, GDV), BF16), pltpu.VMEM((s, GDK), F32)], compiler_params=_params(2),
    )(dog, o, states, proj, proj, proj, proj, proj, w_alpha_p, b_alpha, out_norm_g)


def _lane():
    return lax.broadcasted_iota(jnp.int32, (1, LANE), 1)


def _swap_halves(x):
    lane = _lane()
    half = MROPE // 2
    lo = (lane >= MNOPE) & (lane < MNOPE + half)
    hi = (lane >= MNOPE + half) & (lane < MQK)
    return jnp.where(lo, pltpu.roll(x, LANE - half, 1), jnp.where(hi, pltpu.roll(x, half, 1), 0.0))


def _norm96(x, g):
    r = lax.rsqrt(jnp.sum(x * x, axis=-1, keepdims=True) * (1.0 / MQK) + EPS)
    return x * r, r


def _lat_norm(proj, q_lat_g, kv_lat_g, *, name, ts=512):
    t = proj.shape[0]
    ts = min(ts, t)

    def body(cq_ref, ckv_ref, gq_ref, gk_ref, oq_ref, ok_ref):
        xq, _ = _rms(cq_ref[...], None)
        oq_ref[...] = (xq * gq_ref[...]).astype(BF16)
        xk, _ = _rms(ckv_ref[...], None)
        ok_ref[...] = (xk * gk_ref[...]).astype(BF16)

    return pl.pallas_call(
        body, name=name, grid=(t // ts,),
        in_specs=[pl.BlockSpec((ts, MQR), lambda i: (i, OFF_CQ // MQR)), pl.BlockSpec((ts, MKVR), lambda i: (i, OFF_CKV // MKVR)),
                  pl.BlockSpec((1, MQR), lambda i: (0, 0)), pl.BlockSpec((1, MKVR), lambda i: (0, 0))],
        out_specs=[pl.BlockSpec((ts, MQR), lambda i: (i, 0)), pl.BlockSpec((ts, MKVR), lambda i: (i, 0))],
        out_shape=[jax.ShapeDtypeStruct((t, MQR), BF16), jax.ShapeDtypeStruct((t, MKVR), BF16)],
        compiler_params=_params(1),
    )(proj, proj, q_lat_g, kv_lat_g)


def _lat_norm_bwd(dcqn, dckvn, proj, q_lat_g, kv_lat_g, *, name, ts=512):
    t = proj.shape[0]
    ts = min(ts, t)

    def one(d_ref, x_ref, g_ref, dx_ref, dg_ref):
        xh, r = _rms(x_ref[...], None)
        dn = d_ref[...]
        dg_ref[...] += jnp.sum(dn * xh, axis=0, keepdims=True)
        dxh = dn * g_ref[...]
        dx_ref[...] = (r * (dxh - xh * jnp.mean(dxh * xh, axis=-1, keepdims=True))).astype(BF16)

    def body(dq_ref, dk_ref, cq_ref, ckv_ref, gq_ref, gk_ref, dxq_ref, dxk_ref, dgq_ref, dgk_ref):
        @pl.when(pl.program_id(0) == 0)
        def _():
            dgq_ref[...] = jnp.zeros_like(dgq_ref)
            dgk_ref[...] = jnp.zeros_like(dgk_ref)

        one(dq_ref, cq_ref, gq_ref, dxq_ref, dgq_ref)
        one(dk_ref, ckv_ref, gk_ref, dxk_ref, dgk_ref)

    return pl.pallas_call(
        body, name=name, grid=(t // ts,),
        in_specs=[pl.BlockSpec((ts, MQR), lambda i: (i, 0)), pl.BlockSpec((ts, MKVR), lambda i: (i, 0)),
                  pl.BlockSpec((ts, MQR), lambda i: (i, OFF_CQ // MQR)), pl.BlockSpec((ts, MKVR), lambda i: (i, OFF_CKV // MKVR)),
                  pl.BlockSpec((1, MQR), lambda i: (0, 0)), pl.BlockSpec((1, MKVR), lambda i: (0, 0))],
        out_specs=[pl.BlockSpec((ts, MQR), lambda i: (i, 0)), pl.BlockSpec((ts, MKVR), lambda i: (i, 0)),
                   pl.BlockSpec((1, MQR), lambda i: (0, 0)), pl.BlockSpec((1, MKVR), lambda i: (0, 0))],
        out_shape=[jax.ShapeDtypeStruct((t, MQR), BF16), jax.ShapeDtypeStruct((t, MKVR), BF16),
                   jax.ShapeDtypeStruct((1, MQR), F32), jax.ShapeDtypeStruct((1, MKVR), F32)],
        compiler_params=_params(1),
    )(dcqn, dckvn, proj, proj, q_lat_g, kv_lat_g)


def _qk_prep(q_raw, kv, proj, cos_t, sin_t, gq, gk, *, name, ts=2048):
    t = q_raw.shape[0]
    ts = min(ts, t)

    def body(q_ref, kv_ref, kpe_ref, c_ref, s_ref, gq_ref, gk_ref, qo_ref, ko_ref, vo_ref):
        cs, sn = c_ref[...], s_ref[...]
        nope = _lane() < MNOPE
        qn, _ = _norm96(q_ref[...], None)
        qn = qn * gq_ref[...]
        qo_ref[...] = (qn * cs + _swap_halves(qn) * sn).astype(BF16)
        kvv = kv_ref[...]
        kn, _ = _norm96(jnp.where(nope, kvv, kpe_ref[...]), None)
        kn = kn * gk_ref[...]
        ko_ref[...] = (kn * cs + _swap_halves(kn) * sn).astype(BF16)
        vo_ref[...] = jnp.where(nope, pltpu.roll(kvv, MNOPE, 1), 0.0).astype(BF16)

    hd = pl.BlockSpec((ts, LANE), lambda i, h: (i, h))
    shared = lambda col: pl.BlockSpec((ts, LANE), lambda i, h: (i, col))
    gain = pl.BlockSpec((1, LANE), lambda i, h: (0, 0))
    out = jax.ShapeDtypeStruct((t, MH * LANE), BF16)
    return pl.pallas_call(
        body, name=name, grid=(t // ts, MH),
        in_specs=[hd, hd, shared(OFF_KPE // LANE), shared(0), shared(0), gain, gain],
        out_specs=[hd, hd, hd], out_shape=[out, out, out], compiler_params=_params(2),
    )(q_raw, kv, proj, cos_t, sin_t, gq, gk)


def _qk_prep_bwd(dq, dk, dv, q_raw, kv, proj, cos_t, sin_t, gq, gk, *, name, ts=2048):
    t = q_raw.shape[0]
    ts = min(ts, t)

    def norm_bwd(dy, x, g, dg_ref):
        xh, r = _norm96(x, None)
        dg_ref[...] += jnp.sum(dy * xh, axis=0, keepdims=True)
        dxh = dy * g
        return r * (dxh - xh * (jnp.sum(dxh * xh, axis=-1, keepdims=True) * (1.0 / MQK)))

    def body(dq_ref, dk_ref, dv_ref, q_ref, kv_ref, kpe_ref, c_ref, s_ref, gq_ref, gk_ref,
             dqr_ref, dkv_ref, dkpe_ref, dgq_ref, dgk_ref):
        i, h = pl.program_id(0), pl.program_id(1)

        @pl.when(h == 0)
        def _():
            dkpe_ref[...] = jnp.zeros_like(dkpe_ref)

        @pl.when((h == 0) & (i == 0))
        def _():
            dgq_ref[...] = jnp.zeros_like(dgq_ref)
            dgk_ref[...] = jnp.zeros_like(dgk_ref)

        cs, sn = c_ref[...], s_ref[...]
        lane = _lane()
        nope = lane < MNOPE
        dqv = dq_ref[...]
        dqn = dqv * cs + _swap_halves(dqv * sn)
        dqr_ref[...] = norm_bwd(dqn, q_ref[...], gq_ref[...], dgq_ref).astype(BF16)
        dkv_ = dk_ref[...]
        dkn = dkv_ * cs + _swap_halves(dkv_ * sn)
        kvv = kv_ref[...]
        dkr = norm_bwd(dkn, jnp.where(nope, kvv, kpe_ref[...]), gk_ref[...], dgk_ref)
        dkv_ref[...] = jnp.where(nope, dkr, pltpu.roll(dv_ref[...], MNOPE, 1)).astype(BF16)
        dkpe_ref[...] += jnp.where((lane >= MNOPE) & (lane < MQK), dkr, 0.0)

    hd = pl.BlockSpec((ts, LANE), lambda i, h: (i, h))
    shared = lambda col: pl.BlockSpec((ts, LANE), lambda i, h: (i, col))
    gain = pl.BlockSpec((1, LANE), lambda i, h: (0, 0))
    out = jax.ShapeDtypeStruct((t, MH * LANE), BF16)
    return pl.pallas_call(
        body, name=name, grid=(t // ts, MH),
        in_specs=[hd, hd, hd, hd, hd, shared(OFF_KPE // LANE), shared(0), shared(0), gain, gain],
        out_specs=[hd, hd, shared(0), gain, gain],
        out_shape=[out, out, jax.ShapeDtypeStruct((t, LANE), F32), jax.ShapeDtypeStruct((1, LANE), F32),
                   jax.ShapeDtypeStruct((1, LANE), F32)],
        compiler_params=_params(2),
    )(dq, dk, dv, q_raw, kv, proj, cos_t, sin_t, gq, gk)


_NT = (((1,), (1,)), ((), ()))
_TN = (((0,), (0,)), ((), ()))


SOFTMAX_SCALE = MQK ** -0.5
Q_PRESCALE = SOFTMAX_SCALE * float(np.log2(np.e))


def _attn_weights(q, k_ref, lo, tq):
    row = lax.broadcasted_iota(jnp.int32, (tq, tq), 0) // CHUNK
    col = lax.broadcasted_iota(jnp.int32, (tq, tq), 1) // CHUNK
    sd = lax.dot_general(q, k_ref[pl.ds(lo, tq), :], _NT, preferred_element_type=F32)
    sd = jnp.where(col <= row, sd, -1e30)
    m = jnp.max(sd, axis=-1, keepdims=True)
    if lo:
        so = lax.dot_general(q, k_ref[pl.ds(0, lo), :], _NT, preferred_element_type=F32)
        m = jnp.maximum(m, jnp.max(so, axis=-1, keepdims=True))
        eo = jnp.exp2(so - m)
        ed = jnp.exp2(sd - m)
        return eo, ed, 1.0 / (jnp.sum(eo, axis=-1, keepdims=True) + jnp.sum(ed, axis=-1, keepdims=True))
    ed = jnp.exp2(sd - m)
    return None, ed, 1.0 / jnp.sum(ed, axis=-1, keepdims=True)


def _attn_fwd(q, k, v, *, name, tq=256):
    bsz, s, _ = q.shape
    tq = min(tq, s)

    def body(q_ref, k_ref, v_ref, o_ref):
        for i in range(s // tq):
            lo = i * tq
            eo, ed, inv = _attn_weights(q_ref[pl.ds(lo, tq), :], k_ref, lo, tq)
            o = jnp.dot(ed.astype(BF16), v_ref[pl.ds(lo, tq), :], preferred_element_type=F32)
            if lo:
                o += jnp.dot(eo.astype(BF16), v_ref[pl.ds(0, lo), :], preferred_element_type=F32)
            o_ref[pl.ds(lo, tq), :] = (o * inv).astype(BF16)

    spec = pl.BlockSpec((None, s, LANE), lambda b, h: (b, 0, h))
    return pl.pallas_call(
        body, name=name, grid=(bsz, MH), in_specs=[spec, spec, spec], out_specs=spec,
        out_shape=jax.ShapeDtypeStruct((bsz, s, MH * LANE), BF16), compiler_params=_params(2),
    )(q, k, v)


def _attn_bwd(q, k, v, do, *, name, tq=256):
    bsz, s, _ = q.shape
    tq = min(tq, s)

    def body(q_ref, k_ref, v_ref, do_ref, dq_ref, dk_ref, dv_ref):
        dk_ref[...] = jnp.zeros_like(dk_ref)
        dv_ref[...] = jnp.zeros_like(dv_ref)
        for i in range(s // tq):
            lo = i * tq
            here, before = pl.ds(lo, tq), pl.ds(0, lo)
            qv, dov = q_ref[here, :], do_ref[here, :]
            eo, ed, inv = _attn_weights(qv, k_ref, lo, tq)
            do_n = (dov.astype(F32) * inv).astype(BF16)
            dv_ref[here, :] += lax.dot_general(ed.astype(BF16), do_n, _TN, preferred_element_type=F32)
            dpd = lax.dot_general(dov, v_ref[here, :], _NT, preferred_element_type=F32)
            delta = jnp.sum(dpd * ed, axis=-1, keepdims=True)
            if lo:
                dv_ref[before, :] += lax.dot_general(eo.astype(BF16), do_n, _TN, preferred_element_type=F32)
                dpo = lax.dot_general(dov, v_ref[before, :], _NT, preferred_element_type=F32)
                delta += jnp.sum(dpo * eo, axis=-1, keepdims=True)
            delta = delta * inv
            r = inv * SOFTMAX_SCALE
            dsd = (ed * (dpd - delta) * r).astype(BF16)
            dq = jnp.dot(dsd, k_ref[here, :], preferred_element_type=F32)
            dk_ref[here, :] += lax.dot_general(dsd, qv, _TN, preferred_element_type=F32)
            if lo:
                dso = (eo * (dpo - delta) * r).astype(BF16)
                dq += jnp.dot(dso, k_ref[before, :], preferred_element_type=F32)
                dk_ref[before, :] += lax.dot_general(dso, qv, _TN, preferred_element_type=F32)
            dq_ref[here, :] = dq
        dk_ref[...] = dk_ref[...] * (1.0 / Q_PRESCALE)

    spec = pl.BlockSpec((None, s, LANE), lambda b, h: (b, 0, h))
    out = jax.ShapeDtypeStruct((bsz, s, MH * LANE), F32)
    return pl.pallas_call(
        body, name=name, grid=(bsz, MH), in_specs=[spec] * 4, out_specs=[spec] * 3, out_shape=[out, out, out],
        compiler_params=_params(2),
    )(q, k, v, do)


def _adamw(w, g, m, v, *, name, tr=256, by_cols=False):
    rows, cols = w.shape
    tr = _tile_rows(rows, tr)

    def body(w_ref, g_ref, m_ref, v_ref, d_ref, nm_ref, nv_ref):
        d_ref[...], nm_ref[...], nv_ref[...] = _adamw_update(w_ref[...], g_ref[...], m_ref[...], v_ref[...])

    spec = pl.BlockSpec((rows, LANE), lambda i: (0, i)) if by_cols else pl.BlockSpec((tr, cols), lambda i: (i, 0))
    out = jax.ShapeDtypeStruct((rows, cols), F32)
    return pl.pallas_call(body, name=name, grid=(cols // LANE if by_cols else rows // tr,), in_specs=[spec] * 4,
                          out_specs=[spec] * 3, out_shape=[out, out, out], compiler_params=_params(1))(w, g, m, v)


def _tile_rows(rows, target):
    if rows <= target:
        return rows
    best = 8
    for t in range(8, target + 1, 8):
        if rows % t == 0:
            best = t
    return best


def _adamw_update(w, g, m, v):
    nm = ADAM_B1 * m + (1.0 - ADAM_B1) * g
    nv = ADAM_B2 * v + (1.0 - ADAM_B2) * (g * g)
    m_hat = nm / (1.0 - ADAM_B1 ** ADAM_STEP)
    v_hat = nv / (1.0 - ADAM_B2 ** ADAM_STEP)
    return -ADAM_LR * (m_hat / (jnp.sqrt(v_hat) + ADAM_EPS) + ADAM_WD * w), nm, nv


def _adamw_halves(w, m, v, mine, theirs, sel, *, name, tr=256):
    rows, cols = w.shape
    tr = _tile_rows(rows // 2, tr)
    nh = rows // 2 // tr

    def body(sel_ref, w_ref, m_ref, v_ref, mine_ref, theirs_ref, g_ref, d_ref, nm_ref, nv_ref):
        lower = pl.program_id(0) < nh
        south = sel_ref[0] == 0
        gv = jnp.where(lower == south, mine_ref[...], theirs_ref[...])
        g_ref[...] = gv
        d_ref[...], nm_ref[...], nv_ref[...] = _adamw_update(w_ref[...], gv, m_ref[...], v_ref[...])

    full = pl.BlockSpec((tr, cols), lambda i, sel_ref: (i, 0))
    half = pl.BlockSpec((tr, cols), lambda i, sel_ref: (i % nh, 0))
    out = jax.ShapeDtypeStruct((rows, cols), F32)
    return pl.pallas_call(
        body, name=name, out_shape=[out] * 4, compiler_params=_params(1),
        grid_spec=pltpu.PrefetchScalarGridSpec(num_scalar_prefetch=1, grid=(rows // tr,),
                                               in_specs=[full, full, full, half, half], out_specs=[full] * 4),
    )(sel, w, m, v, mine, theirs)


def _pair_add(x, sib, sel, *, name, tr=256):
    n, _, rows, cols = x.shape
    tr = _tile_rows(rows, tr)

    def body(sel_ref, x_ref, s_ref, o_ref):
        o_ref[...] = (x_ref[...] + s_ref[...]).astype(BF16)

    spec = pl.BlockSpec((None, tr, cols), lambda j, i, sel_ref: (j, i, 0))
    return pl.pallas_call(
        body, name=name, out_shape=jax.ShapeDtypeStruct((n, rows, cols), BF16), compiler_params=_params(2),
        grid_spec=pltpu.PrefetchScalarGridSpec(
            num_scalar_prefetch=1, grid=(n, rows // tr),
            in_specs=[pl.BlockSpec((None, None, tr, cols), lambda j, i, sel_ref: (j, sel_ref[0], i, 0)), spec],
            out_specs=spec),
    )(sel, x, sib)


def _chip_sum(pair, recv, sel, *, name, tr=256):
    _, rows, cols = pair.shape
    tr = _tile_rows(rows, tr)

    def body(sel_ref, p_ref, r_ref, o_ref):
        acc = p_ref[...].astype(F32)
        for k in range(3):
            acc = acc + r_ref[k].astype(F32)
        o_ref[...] = acc

    return pl.pallas_call(
        body, name=name, out_shape=jax.ShapeDtypeStruct((rows, cols), F32), compiler_params=_params(1),
        grid_spec=pltpu.PrefetchScalarGridSpec(
            num_scalar_prefetch=1, grid=(rows // tr,),
            in_specs=[pl.BlockSpec((None, tr, cols), lambda i, sel_ref: (sel_ref[0], i, 0)),
                      pl.BlockSpec((3, tr, cols), lambda i, sel_ref: (0, i, 0))],
            out_specs=pl.BlockSpec((tr, cols), lambda i, sel_ref: (i, 0))),
    )(sel, pair, recv)


def _me():
    return lax.axis_index("x"), lax.axis_index("y"), lax.axis_index("c")


def _flip(pos, bits):
    x, y, c = pos
    return (x ^ bits[0] if bits[0] else x, y ^ bits[1] if bits[1] else y, c ^ bits[2] if bits[2] else c)


ANY = pl.BlockSpec(memory_space=pl.ANY)


def _all_gather8(xs, *, name):
    n = len(xs)
    flips = [((k >> 2) & 1, (k >> 1) & 1, k & 1) for k in range(1, 8)]

    def body(*refs):
        x_refs, out_refs, (send_sems, recv_sems, local_sems) = refs[:n], refs[n:2 * n], refs[2 * n:]
        me = _me()
        slot = lambda p: 4 * p[0] + 2 * p[1] + p[2]
        copies = []
        for i in range(n):
            mine = pltpu.make_async_copy(x_refs[i], out_refs[i].at[slot(me)], local_sems.at[i])
            mine.start()
            copies.append(mine)
            for k, f in enumerate(flips):
                peer = _flip(me, f)
                sems = dict(send_sem=send_sems.at[7 * i + k], recv_sem=recv_sems.at[7 * i + k], device_id=peer,
                            device_id_type=MESH)
                cp = pltpu.make_async_remote_copy(src_ref=x_refs[i], dst_ref=out_refs[i].at[slot(me)], **sems)
                cp.start()
                copies.append(cp)
                copies.append(pltpu.make_async_remote_copy(src_ref=x_refs[i], dst_ref=out_refs[i].at[slot(peer)], **sems))
        for i in range(n):
            base = i * 15
            copies[base].wait()
            for k in range(7):
                copies[base + 1 + 2 * k].wait_send()
                copies[base + 2 + 2 * k].wait_recv()

    outs = pl.pallas_call(
        body, name=name, in_specs=[ANY] * n, out_specs=[ANY] * n,
        out_shape=[jax.ShapeDtypeStruct((8, *x.shape), x.dtype) for x in xs],
        scratch_shapes=[pltpu.SemaphoreType.DMA((7 * n,)), pltpu.SemaphoreType.DMA((7 * n,)),
                        pltpu.SemaphoreType.DMA((n,))])(*xs)
    return list(outs)


CHIP_FLIPS = [(1, 0, 0), (0, 1, 0), (1, 1, 0)]


def _chip():
    return 2 * lax.axis_index("x") + lax.axis_index("y")


HBM = pl.BlockSpec(memory_space=pltpu.HBM)
SEM = pl.BlockSpec(memory_space=pltpu.SEMAPHORE)
EFFECT = pltpu.SideEffectType.DATAFLOW_SIDE_EFFECTING


def _plan_copies(plan, refs, send_sems, recv_sems):
    return [pltpu.make_async_remote_copy(src_ref=src, dst_ref=dst, send_sem=send_sems.at[k], recv_sem=recv_sems.at[k],
                                         device_id=to, device_id_type=MESH) for k, (src, dst, to) in enumerate(plan(refs))]


def _rdma_start(arrays, n_copies, plan, deps, *, name):
    n, nd = len(arrays), len(deps)

    def body(*refs):
        for cp in _plan_copies(plan, refs[:n], refs[n + nd], refs[n + nd + 1]):
            cp.start()
        refs[-1][...] = jnp.zeros_like(refs[-1])

    outs = pl.pallas_call(
        body, name=name,
        out_shape=(pltpu.SemaphoreType.DMA((n_copies,)), pltpu.SemaphoreType.DMA((n_copies,)),
                   *[pltpu.HBM(a.shape, a.dtype) for a in arrays], jax.ShapeDtypeStruct((8, LANE), F32)),
        in_specs=[HBM] * n + [ANY] * nd, out_specs=(SEM, SEM, *[HBM] * n, pl.BlockSpec(memory_space=pltpu.VMEM)),
        input_output_aliases={i: i + 2 for i in range(n)}, compiler_params=pltpu.CompilerParams(has_side_effects=EFFECT),
    )(*[pltpu.with_memory_space_constraint(a, pltpu.HBM) for a in arrays], *deps)
    return outs[0], outs[1], list(outs[2:2 + n]), outs[-1]


def _rdma_wait(send_sems, recv_sems, arrays, plan, after, *, name):
    n = len(arrays)

    def body(*refs):
        for cp in _plan_copies(plan, refs[:n], refs[n], refs[n + 1]):
            cp.wait_send()
            cp.wait_recv()

    return list(pl.pallas_call(
        body, name=name, out_shape=tuple(pltpu.HBM(a.shape, a.dtype) for a in arrays),
        in_specs=[HBM] * n + [SEM, SEM, ANY], out_specs=tuple([HBM] * n), input_output_aliases={i: i for i in range(n)},
        compiler_params=pltpu.CompilerParams(has_side_effects=EFFECT),
    )(*arrays, send_sems, recv_sems, after))


def _gather_plan(n):
    def plan(refs):
        me = _me()
        slot = 2 * me[0] + me[1]
        return [(refs[i].at[me[2]], refs[n + i].at[slot, me[2]], _flip(me, f)) for i in range(n) for f in CHIP_FLIPS]
    return plan


def _scatter_plan(n):
    def plan(refs):
        me = _me()
        out = []
        for i in range(n):
            for k, f in enumerate(CHIP_FLIPS):
                peer = _flip(me, f)
                out.append((refs[i].at[2 * peer[0] + peer[1]], refs[n + i].at[k], peer))
        return out
    return plan


def _sibling_plan(n, src_of):
    def plan(refs):
        me = _me()
        return [(src_of(refs[i], me[2]), refs[n + i], _flip(me, (0, 0, 1))) for i in range(n)]
    return plan


def _gather8_plan(n):
    def plan(refs):
        me = _me()
        slot = 4 * me[0] + 2 * me[1] + me[2]
        return [(refs[i], refs[n + i].at[slot], _flip(me, ((k >> 2) & 1, (k >> 1) & 1, k & 1)))
                for i in range(n) for k in range(1, 8)]
    return plan


def _pair_fill(lands, *, name):
    n = len(lands)

    def body(*refs):
        in_refs, (send_sems, recv_sems) = refs[:n], refs[2 * n:]
        me = _me()
        sib = _flip(me, (0, 0, 1))
        copies = []
        for i in range(n):
            for k, f in enumerate(CHIP_FLIPS):
                peer = _flip(me, f)
                slot = 2 * peer[0] + peer[1]
                mine, theirs = in_refs[i].at[slot, me[2]], in_refs[i].at[slot, 1 - me[2]]
                cp = pltpu.make_async_remote_copy(src_ref=mine, dst_ref=mine, send_sem=send_sems.at[3 * i + k],
                                                  recv_sem=recv_sems.at[3 * i + k], device_id=sib, device_id_type=MESH)
                cp.start()
                copies.append((cp, pltpu.make_async_remote_copy(
                    src_ref=mine, dst_ref=theirs, send_sem=send_sems.at[3 * i + k], recv_sem=recv_sems.at[3 * i + k],
                    device_id=sib, device_id_type=MESH)))
        for cp, arrival in copies:
            arrival.wait_recv()
            cp.wait_send()

    return list(pl.pallas_call(
        body, name=name, in_specs=[ANY] * n, out_specs=[ANY] * n,
        out_shape=[jax.ShapeDtypeStruct(a.shape, a.dtype) for a in lands], input_output_aliases={i: i for i in range(n)},
        scratch_shapes=[pltpu.SemaphoreType.DMA((3 * n,)), pltpu.SemaphoreType.DMA((3 * n,))])(*lands))


def _own_and_landed(lands, xs):
    chip = _chip()
    return [[jnp.where(chip == j, x, o.reshape(4, *x.shape)[j]) for j in range(4)] for o, x in zip(lands, xs)]


BIG = (("w_in", (D, IN_WIDTH // 4), 1), ("gla_w_o", (D // 4, D), 0), ("mla_w_uq", (MQR, MH * MQK // 4), 1),
       ("mla_w_ukv", (MKVR, MH * (MNOPE + MVD) // 4), 1), ("mla_w_o", (D // 4, D), 0), ("w_out", (D // 4, D), 0),
       ("mlp_w1", (D, DFF // 4), 1), ("mlp_w2", (DFF // 4, D), 0))
ADA_SHARD = (D, 6 * D // 4)
SMALL = (("b_ada", 6 * D), ("norm1_g", D), ("b_merge", 2 * D), ("gla_b_alpha", GH * GDK), ("gla_out_norm_g", GDV),
         ("mla_q_lat_g", MQR), ("mla_kv_lat_g", MKVR), ("mla_qn_g", MQK), ("mla_kn_g", MQK), ("norm2_g", D))


W_IN_SEGMENTS = ((0, 3072, OFF_Q), (3072, 3088, OFF_A), (3088, 3344, OFF_CQ), (3344, 3472, OFF_CKV),
                 (3472, 3504, OFF_KPE + MNOPE), (3504, 5552, OFF_MA))
W_IN_SPLIT = OFF_MA
SMALL_ROWS, SMALL_COLS = 32, 2 * D
W_ALPHA_ROW = 16
SMALL_RED = tuple((n, k) for n, k in SMALL if n != "b_ada")


def _pack_small(grads, d_w_alpha, *, name):
    def body(*refs):
        g_refs, wa_ref, out_ref = refs[:-2], refs[-2], refs[-1]
        out_ref[...] = jnp.zeros_like(out_ref)
        for i, ((_, k), g_ref) in enumerate(zip(SMALL_RED, g_refs)):
            out_ref[i:i + 1, 0:k] = g_ref[...]
        out_ref[W_ALPHA_ROW:W_ALPHA_ROW + GLR, 0:GH * GDK] = wa_ref[...]

    return pl.pallas_call(body, name=name, out_shape=jax.ShapeDtypeStruct((SMALL_ROWS, SMALL_COLS), F32))(*grads, d_w_alpha)


def _small_update(gathered, dmod_all, sel, wmv, *, name):
    names = [n for n, _ in SMALL] + ["gla_w_alpha"]
    n_par = len(names)

    def body(sel_ref, g_ref, dmod_ref, *refs):
        in_refs, out_refs, acc = refs[:3 * n_par], refs[3 * n_par:-1], refs[-1]
        total = g_ref[0]
        for j in range(1, 8):
            total = total + g_ref[j]
        acc[...] = total
        row = {n: i for i, (n, _) in enumerate(SMALL_RED)}
        for p, name_p in enumerate(names):
            w_ref, m_ref, v_ref = in_refs[3 * p:3 * p + 3]
            if name_p == "b_ada":
                gv = jnp.sum(dmod_ref[...], axis=0, keepdims=True)
            elif name_p == "gla_w_alpha":
                gv = jnp.zeros((GLR, GDK), F32)
                for j in range(4):
                    blk = acc[W_ALPHA_ROW:W_ALPHA_ROW + GLR, j * GDK:(j + 1) * GDK]
                    gv = gv + jnp.where(sel_ref[0] == j, blk, 0.0)
            else:
                gv = acc[row[name_p]:row[name_p] + 1, 0:w_ref.shape[1]]
            o = out_refs[4 * p:4 * p + 4]
            o[0][...] = gv
            o[1][...], o[2][...], o[3][...] = _adamw_update(w_ref[...], gv, m_ref[...], v_ref[...])

    flat = [a for t in wmv for a in t]
    out_shape = [jax.ShapeDtypeStruct(t[0].shape, F32) for t in wmv for _ in range(4)]
    vmem = pl.BlockSpec(memory_space=pltpu.VMEM)
    outs = pl.pallas_call(
        body, name=name, out_shape=out_shape, in_specs=[pl.BlockSpec(memory_space=pltpu.SMEM), vmem, vmem] + [vmem] * len(flat),
        out_specs=[vmem] * len(out_shape), scratch_shapes=[pltpu.VMEM((SMALL_ROWS, SMALL_COLS), F32)],
    )(sel, gathered, dmod_all, *flat)
    return {n: tuple(outs[4 * p:4 * p + 4]) for p, n in enumerate(names)}


def _full_weights(gathered):
    w = {name: jnp.concatenate(gathered[name], axis=axis) for name, _, axis in BIG if name in gathered and name != "w_in"}
    if "w_in" in gathered:
        shards = gathered["w_in"]
        zeros = lambda n: [jnp.zeros((D, n), shards[0].dtype)]

        def cols(a, b):
            width = IN_WIDTH // 4
            return [shards[j][:, max(a, j * width) - j * width:min(b, (j + 1) * width) - j * width]
                    for j in range(4) if max(a, j * width) < min(b, (j + 1) * width)]

        parts = []
        for a, b, at in sorted(W_IN_SEGMENTS, key=lambda seg: seg[2]):
            have = sum(p.shape[1] for p in parts)
            parts += (zeros(at - have) if at > have else []) + cols(a, b)
        w["w_in"] = jnp.concatenate(parts + zeros(PW - sum(p.shape[1] for p in parts)), axis=1)
    if "mla_w_uq" in w:
        w["mla_w_uq"] = jnp.pad(w["mla_w_uq"].reshape(MQR, MH, MQK), ((0, 0), (0, 0), (0, LANE - MQK))).reshape(MQR, MH * LANE)
    if "mla_w_o" in w:
        w["mla_w_o"] = jnp.pad(w["mla_w_o"].reshape(MH, MVD, D), ((0, 0), (0, LANE - MVD), (0, 0))).reshape(MH * LANE, D)
    return w


def _grad_slots(g):
    g = dict(g)
    out = {}
    if "w_in" in g:
        g_lo, g_hi = g.pop("w_in")
        take = lambda at, lo, hi: g_lo[:, at + lo:at + hi] if at < W_IN_SPLIT else g_hi[:, at - W_IN_SPLIT + lo:at - W_IN_SPLIT + hi]
        width = IN_WIDTH // 4
        slots = []
        for j in range(4):
            lo, hi = j * width, (j + 1) * width
            slots.append(jnp.concatenate([take(at, max(lo, a) - a, min(hi, b) - a)
                                          for a, b, at in W_IN_SEGMENTS if max(lo, a) < min(hi, b)], axis=1))
        out["w_in"] = jnp.stack(slots).reshape(4, 2, D // 2, width)
    if "mla_w_uq" in g:
        g["mla_w_uq"] = g["mla_w_uq"].reshape(MQR, MH, LANE)[:, :, :MQK].reshape(MQR, MH * MQK)
    if "mla_w_o" in g:
        g["mla_w_o"] = g["mla_w_o"].reshape(MH, LANE, D)[:, :MVD].reshape(MH * MVD, D)
    for name, (rows, cols), axis in BIG:
        if name not in g:
            continue
        a = g[name]
        a = a.reshape(4, rows, cols) if axis == 0 else jnp.transpose(a.reshape(rows, 4, cols), (1, 0, 2))
        out[name] = a.reshape(4, 2, rows // 2, cols)
    return out


def _rope_tables(positions):
    freqs = ROPE_THETA ** (-jnp.arange(0, MROPE, 2, dtype=F32) / MROPE)
    lane = np.arange(LANE)
    in_rope = (lane >= MNOPE) & (lane < MQK)
    freq_lane = jnp.where(in_rope, freqs[(lane - MNOPE) % (MROPE // 2)], 0.0)
    sign = np.where(in_rope, np.where(lane < MNOPE + MROPE // 2, -1.0, 1.0), 0.0).astype(np.float32)
    ang = positions.astype(F32).reshape(-1, 1) * freq_lane[None, :]
    return jnp.cos(ang), jnp.sin(ang) * sign[None, :]


def _local_step(x, positions, mod, target, w, small, more_weights=None, on_grads=None):
    kept = {}
    if on_grads is None:
        on_grads = lambda tag, grads, after: kept.update(grads)
    bsz, s, _ = x.shape
    t = bsz * s
    tt = _tile(t, 1024)
    shift1, scale1, gate1, shift2, scale2, gate2 = [mod[:, None, i * D:(i + 1) * D] for i in range(6)]
    cos_t, sin_t = _rope_tables(positions)
    w_alpha_p = jnp.pad(small["gla_w_alpha"], ((0, LANE - GLR), (0, 0)))
    gq = jnp.pad(small["mla_qn_g"], ((0, 0), (0, LANE - MQK)))
    gk = jnp.pad(small["mla_kn_g"], ((0, 0), (0, LANE - MQK)))
    flat2 = lambda a: a.reshape(t, a.shape[-1])
    bsd = lambda a: a.reshape(bsz, s, a.shape[-1])

    h = _norm_mod(x, small["norm1_g"], scale1, shift1, name="norm1")
    if callable(w):
        w = w(h)
    proj = _mm(flat2(h), w["w_in"], name="proj", tn=1152)
    proj3 = bsd(proj)
    o, o_gated, states = _gla_fwd(proj3, w_alpha_p, small["gla_b_alpha"], small["gla_out_norm_g"], name="gla_fwd")
    if more_weights is not None:
        w = {**w, **more_weights(o_gated)}
    y_a = _mm(flat2(o_gated), w["gla_w_o"], name="gla_out")
    cq_n, ckv_n = _lat_norm(proj, small["mla_q_lat_g"], small["mla_kv_lat_g"], name="lat_norm")
    q_raw = _mm(cq_n, w["mla_w_uq"], name="mla_uq")
    kv = _mm(ckv_n, w["mla_w_ukv"], name="mla_ukv")
    qf, kf, vf = _qk_prep(q_raw, kv, proj, cos_t, sin_t, gq * Q_PRESCALE, gk, name="qk_prep")
    o_attn = _attn_fwd(bsd(qf), bsd(kf), bsd(vf), name="attn_fwd")
    y_b = _mm(flat2(o_attn), w["mla_w_o"], name="mla_out")
    mixed_in = _merge_fwd(proj3, small["b_merge"], bsd(y_a), bsd(y_b), name="merge_fwd")
    mixed = _mm(flat2(mixed_in), w["w_out"], name="w_out")
    x1, h2 = _resid_norm_mod(x, bsd(mixed), gate1, small["norm2_g"], scale2, shift2, name="norm2")

    def sqrelu(acc, ex, outs):
        r = jnp.maximum(acc, 0.0)
        outs[0][...] = (r * r).astype(BF16)

    r = _mm(flat2(h2), w["mlp_w1"], name="mlp1", epilogue=sqrelu, out_shape=jax.ShapeDtypeStruct((t, DFF), BF16),
            out_specs=_tile_spec(tt, 1024))
    ff = _mm(r, w["mlp_w2"], name="mlp2")
    dy, dff, dgate2, loss_part = _loss_head(x1, bsd(ff), gate2, target, name="loss_head")

    g = {}

    def relu2_bwd(acc, ex, outs):
        outs[0][...] = (acc * (2.0 * jnp.sqrt(ex[0][...].astype(F32)))).astype(BF16)

    dff2 = flat2(dff)
    da1 = _mm(dff2, w["mlp_w2"], tb=True, name="mlp2_dx", epilogue=relu2_bwd, extras=(r,),
              extra_specs=(_tile_spec(tt, 1024),), out_shape=jax.ShapeDtypeStruct((t, DFF), BF16),
              out_specs=_tile_spec(tt, 1024))
    g["mlp_w2"] = _mm(r, dff2, ta=True, name="mlp2_dw")
    dh2 = _mm(da1, w["mlp_w1"], tb=True, name="mlp1_dx")
    g["mlp_w1"] = _mm(flat2(h2), da1, ta=True, name="mlp1_dw")
    token = on_grads("mlp", {n: g.pop(n) for n in ("mlp_w2", "mlp_w1")}, dh2)
    if token is not None:
        gate1 = gate1 + token[0, 0]
    dx1, dscale2, dshift2, dg2, dgate1, dmixed = _norm_mod_bwd(
        bsd(dh2), x1, dy, small["norm2_g"], scale2, gate1, bsd(mixed), name="norm2_bwd")
    dmixed2 = flat2(dmixed)
    dmi = _mm(dmixed2, w["w_out"], tb=True, name="w_out_dx")
    g["w_out"] = _mm(flat2(mixed_in), dmixed2, ta=True, name="w_out_dw")
    dy_a, dy_b, dl_a, dl_b, db_a, db_b = _merge_bwd(bsd(dmi), proj3, small["b_merge"], bsd(y_a), bsd(y_b), name="merge_bwd")
    dy_a2, dy_b2 = flat2(dy_a), flat2(dy_b)
    dog = _mm(dy_a2, w["gla_w_o"], tb=True, name="gla_out_dx")
    g["gla_w_o"] = _mm(flat2(o_gated), dy_a2, ta=True, name="gla_out_dw")
    dq_g, dk_g, dv_g, dg_g, dlog, db_alpha, d_ong = _gla_bwd(
        bsd(dog), o, states, proj3, w_alpha_p, small["gla_b_alpha"], small["gla_out_norm_g"], name="gla_bwd")
    dlog2 = flat2(dlog)
    da_p = _mm(dlog2, w_alpha_p, tb=True, out_dtype=BF16, name="alpha_dx")
    d_w_alpha = _mm(proj[:, OFF_A:OFF_A + LANE], dlog2, ta=True, name="alpha_dw")[:GLR]
    do_attn = _mm(dy_b2, w["mla_w_o"], tb=True, out_dtype=BF16, name="mla_out_dx")
    g["mla_w_o"] = _mm(flat2(o_attn), dy_b2, ta=True, name="mla_out_dw")
    dqf, dkf, dvf = _attn_bwd(bsd(qf), bsd(kf), bsd(vf), bsd(do_attn), name="attn_bwd")
    dq_raw, dkv, dkpe, dgq, dgk = _qk_prep_bwd(flat2(dqf), flat2(dkf), flat2(dvf), q_raw, kv, proj, cos_t, sin_t, gq, gk,
                                                name="qk_prep_bwd")
    dcq_n = _mm(dq_raw, w["mla_w_uq"], tb=True, name="mla_uq_dx")
    g["mla_w_uq"] = _mm(cq_n, dq_raw, ta=True, name="mla_uq_dw")
    dckv_n = _mm(dkv, w["mla_w_ukv"], tb=True, name="mla_ukv_dx")
    g["mla_w_ukv"] = _mm(ckv_n, dkv, ta=True, name="mla_ukv_dw")
    token = on_grads("mix", {n: g.pop(n) for n in ("w_out", "gla_w_o", "mla_w_o", "mla_w_uq", "mla_w_ukv")}, dckv_n)
    q_lat_g = small["mla_q_lat_g"] if token is None else small["mla_q_lat_g"] + token[0:1, 0:1]
    dcq, dckv, dg_qlat, dg_kvlat = _lat_norm_bwd(dcq_n, dckv_n, proj, q_lat_g, small["mla_kv_lat_g"],
                                                  name="lat_norm_bwd")
    pieces = [(flat2(dq_g), OFF_Q), (flat2(dk_g), OFF_K), (flat2(dv_g), OFF_V), (flat2(dg_g), OFF_G),
              (flat2(dl_a), OFF_MA), (flat2(dl_b), OFF_MB), (dcq, OFF_CQ), (dckv, OFF_CKV), (da_p, OFF_A), (dkpe, OFF_KPE)]
    hb = flat2(h)
    g_w_in = (_pieces_dw(hb, [p for p, off in pieces if off < W_IN_SPLIT], name="proj_dw_a"),
              _pieces_dw(hb, [p for p, off in pieces if off >= W_IN_SPLIT], name="proj_dw_b"))
    token = on_grads("in", {"w_in": g_w_in}, g_w_in[1])
    after = jnp.zeros((8, LANE), F32) if token is None else token
    dh = _pieces_dx(pieces, w["w_in"], after, name="proj_dx")
    token = on_grads("dx", {}, dh)
    if token is not None:
        scale1 = scale1 + token[0, 0]
    grad_x, dscale1, dshift1, dg1 = _norm_mod_bwd(bsd(dh), x, dx1, small["norm1_g"], scale1, name="norm1_bwd")

    dmod = jnp.concatenate([dshift1, dscale1, dgate1, dshift2, dscale2, dgate2], axis=-1).reshape(bsz, 6 * D)
    gs = {"norm1_g": dg1, "b_merge": jnp.concatenate([db_a, db_b], axis=1), "gla_b_alpha": db_alpha,
          "gla_out_norm_g": d_ong, "mla_q_lat_g": dg_qlat, "mla_kv_lat_g": dg_kvlat, "mla_qn_g": dgq[:, :MQK],
          "mla_kn_g": dgk[:, :MQK], "norm2_g": dg2}
    return loss_part[0, 0], grad_x, dmod, {**kept, **g}, gs, d_w_alpha


def kernel(x, c, positions, w_ada, b_ada, norm1_g, w_in, b_merge, gla_w_alpha, gla_b_alpha, gla_out_norm_g, gla_w_o, mla_q_lat_g, mla_w_uq, mla_kv_lat_g, mla_w_ukv, mla_qn_g, mla_kn_g, mla_w_o, w_out, norm2_g, mlp_w1, mlp_w2, loss_target, m_w_ada, m_b_ada, m_norm1_g, m_w_in, m_b_merge, m_gla_w_alpha, m_gla_b_alpha, m_gla_out_norm_g, m_gla_w_o, m_mla_q_lat_g, m_mla_w_uq, m_mla_kv_lat_g, m_mla_w_ukv, m_mla_qn_g, m_mla_kn_g, m_mla_w_o, m_w_out, m_norm2_g, m_mlp_w1, m_mlp_w2, v_w_ada, v_b_ada, v_norm1_g, v_w_in, v_b_merge, v_gla_w_alpha, v_gla_b_alpha, v_gla_out_norm_g, v_gla_w_o, v_mla_q_lat_g, v_mla_w_uq, v_mla_kv_lat_g, v_mla_w_ukv, v_mla_qn_g, v_mla_kn_g, v_mla_w_o, v_w_out, v_norm2_g, v_mlp_w1, v_mlp_w2):
    args = dict(locals())
    names_big = [n for n, _, _ in BIG]
    names_small = [n for n, _ in SMALL]
    bsz = x.shape[0]
    ax, ay, ac = lax.axis_index("x"), lax.axis_index("y"), lax.axis_index("c")
    chip = 2 * ax + ay
    dev = 2 * chip + ac

    small = {n: args[n] for n in names_small}
    sel_c = jnp.reshape(ac, (1,)).astype(jnp.int32)
    sel_chip = jnp.reshape(chip, (1,)).astype(jnp.int32)
    c_all, w_alpha_all = _all_gather8([c, gla_w_alpha[0]], name="comm_c_alpha")
    small["gla_w_alpha"] = jnp.concatenate([w_alpha_all[2 * j] for j in range(4)], axis=1)
    c_all = c_all.reshape(8 * bsz, D)

    shards = {n: args[n][0].astype(BF16) for n in names_big}
    halves_of = lambda names: [shards[n].reshape(2, shards[n].shape[0] // 2, shards[n].shape[1]) for n in names]

    def gather_start(names, deps, tag):
        xs = halves_of(names)
        lands = [lax.empty((4, *xh.shape), BF16) for xh in xs]
        plan = _gather_plan(len(names))
        return names, plan, _rdma_start(xs + lands, 3 * len(names), plan, deps, name="comm_weights_start_" + tag)

    def gather_finish(started, after, tag):
        names, plan, sems = started
        arrs = _rdma_wait(sems[0], sems[1], sems[2], plan, after, name="comm_weights_wait_" + tag)
        filled = _pair_fill(arrs[len(names):], name="comm_weights_pair_" + tag)
        own = [a.reshape(shards[n].shape) for n, a in zip(names, arrs)]
        return _full_weights(dict(zip(names, _own_and_landed(filled, own))))


    def add_bias(acc, ex, outs):
        outs[0][...] = acc + ex[0][...]

    silu = lambda v: v * _sigmoid(v)
    b_ada_mine = lax.dynamic_slice(b_ada, (0, chip * ADA_SHARD[1]), (1, ADA_SHARD[1]))
    mod_part = _mm(c_all, w_ada[0], name="ada", tn=512, a_fn=silu, epilogue=add_bias, extras=(b_ada_mine,),
                   extra_specs=(pl.BlockSpec((1, 512), lambda i, j, k: (0, j)),),
                   out_shape=jax.ShapeDtypeStruct((8 * bsz, ADA_SHARD[1]), F32), out_specs=_tile_spec(8 * bsz, 512))
    mod_all = _all_gather8([mod_part], name="comm_mod")[0]
    mod_rows = lax.dynamic_slice(mod_all, (0, dev * bsz, 0), (8, bsz, ADA_SHARD[1]))
    mod = jnp.concatenate([mod_rows[2 * j] for j in range(4)], axis=1)
    first = gather_start(["w_in"], (mod,), "in")
    rest = gather_start([n for n in names_big if n != "w_in"], (mod, first[2][3]), "rest")
    mod = mod + rest[2][3][0, 0]
    w_in_after = lambda after: gather_finish(first, after, "in")
    more_weights = lambda after: gather_finish(rest, after, "rest")

    stage = {}

    def begin(tag, names, arrays, lands, n_copies, plan, what):
        stage[tag] = (names, plan, _rdma_start(arrays + lands, n_copies, plan, (), name=f"comm_{what}_start_{tag}"))
        return stage[tag][2][3]

    def landed(tag, after, what):
        names, plan, sems = stage[tag]
        arrs = _rdma_wait(sems[0], sems[1], sems[2], plan, after, name=f"comm_{what}_wait_{tag}")
        return names, arrs[:len(arrs) // 2], arrs[len(arrs) // 2:]

    def swap_start(tag, grads):
        names = list(grads)
        parts = [_grad_slots(grads)[n] for n in names]
        lands = [lax.empty((4, *p.shape[2:]), F32) for p in parts]
        return begin(tag, names, parts, lands, len(names), _sibling_plan(len(names), lambda r, c: r.at[:, 1 - c]), "pair_sum")

    def scatter_start(tag, after):
        names, parts, sib_halves = landed(tag, after, "pair_sum")
        pairs = [_pair_add(p, s, sel_c, name="pair_add_" + n) for n, p, s in zip(names, parts, sib_halves)]
        recvs = [lax.empty((3, *p.shape[1:]), BF16) for p in pairs]
        return begin(tag, names, pairs, recvs, 3 * len(names), _scatter_plan(len(names)), "scatter")

    def join_start(tag, after):
        names, pairs, recvs = landed(tag, after, "scatter")
        halves = [_chip_sum(p, r, sel_chip, name="chip_sum_" + n) for n, p, r in zip(names, pairs, recvs)]
        lands = [lax.empty(h.shape, F32) for h in halves]
        return begin(tag, names, halves, lands, len(names), _sibling_plan(len(names), lambda r, c: r), "pair_join")

    def reduce_step(tag, grads, after):
        if tag == "mlp":
            return swap_start("mlp", grads)
        if tag == "mix":
            return scatter_start("mlp", after) + swap_start("mix", grads)
        if tag == "in":
            return scatter_start("mix", after) + swap_start("in", grads)
        return scatter_start("in", after)

    loss_part, grad_x, dmod, g, gs, d_w_alpha = _local_step(x, positions, mod, loss_target, w_in_after, small,
                                                            more_weights, reduce_step)
    loss = lax.psum(loss_part * (0.5 / D), ("x", "y", "c"))

    assert not g, list(g)
    gs_packed = _pack_small([gs[n] for n, _ in SMALL_RED], d_w_alpha, name="pack_small")
    small_lands = [lax.empty((8, *a.shape), F32) for a in (dmod, gs_packed)]
    begin("small", ["dmod", "small"], [dmod, gs_packed], small_lands, 14, _gather8_plan(2), "gather8")

    res = {}

    def finish(tag, after):
        names, halves, theirs = landed(tag, after, "pair_join")
        for n, mine, other in zip(names, halves, theirs):
            if n == "w_in":
                south = ac == 0
                g_t = jnp.concatenate([jnp.where(south, mine, other), jnp.where(south, other, mine)], axis=0).T
                outs = _adamw(w_in[0].T, g_t, m_w_in[0].T, v_w_in[0].T, name="adamw_w_in", by_cols=True)
                res[n] = tuple(a.T for a in (g_t, *outs))
            else:
                res[n] = _adamw_halves(args[n][0], args["m_" + n][0], args["v_" + n][0], mine, other, sel_c,
                                       name="adamw_" + n)
        return res[names[-1]][1]

    join_start("mlp", grad_x)
    join_start("mix", grad_x)
    done = finish("mix", finish("mlp", grad_x))

    _, (dmod_own, gs_own), (dmod_all, gs_all) = landed("small", done, "gather8")
    dmod_all = lax.dynamic_update_slice(dmod_all, dmod_own[None], (dev, 0, 0)).reshape(8 * bsz, 6 * D)
    gs_all = lax.dynamic_update_slice(gs_all, gs_own[None], (dev, 0, 0))
    dmod_mine = lax.dynamic_slice(dmod_all, (0, chip * ADA_SHARD[1]), (8 * bsz, ADA_SHARD[1]))
    g_w_ada = _mm(c_all, dmod_mine, ta=True, a_fn=silu, name="ada_dw")
    wmv = [(args[n], args["m_" + n], args["v_" + n]) for n in names_small]
    wmv.append((gla_w_alpha[0], m_gla_w_alpha[0], v_gla_w_alpha[0]))
    res.update(_small_update(gs_all, dmod_all, sel_chip, wmv, name="small_update"))
    join_start("in", g_w_ada)
    res["w_ada"] = (g_w_ada, *_adamw(w_ada[0], g_w_ada, m_w_ada[0], v_w_ada[0], name="adamw_w_ada"))
    finish("in", res["w_ada"][1])

    order = ["w_ada", "b_ada", "norm1_g", "w_in", "b_merge", "gla_w_alpha", "gla_b_alpha", "gla_out_norm_g", "gla_w_o",
             "mla_q_lat_g", "mla_w_uq", "mla_kv_lat_g", "mla_w_ukv", "mla_qn_g", "mla_kn_g", "mla_w_o", "w_out",
             "norm2_g", "mlp_w1", "mlp_w2"]
    named = lambda k: [res[n][k].reshape(args[n].shape) for n in order]
    return (loss, grad_x, *named(0), *named(1), *named(2), *named(3))
```

```python
import jax
import jax.numpy as jnp
import numpy as np
from jax import lax
from jax.experimental import pallas as pl
from jax.experimental.pallas import tpu as pltpu

F32 = jnp.float32
BF16 = jnp.bfloat16
MESH = pl.DeviceIdType.MESH

D = 1024
CHUNK = 64
EPS = 1e-6
GH, GDK, GDV, GLR, GTAU = 4, 128, 256, 16, 16.0
MH, MQR, MKVR, MNOPE, MROPE, MVD = 16, 256, 128, 64, 32, 64
MQK = MNOPE + MROPE
DFF = 4 * D
ROPE_THETA = 10000.0
IN_WIDTH = 5552
LANE = 128
OFF_Q, OFF_K, OFF_V, OFF_G, OFF_MA, OFF_MB, OFF_CQ, OFF_CKV, OFF_A, OFF_KPE, PW = (
    0, 512, 1024, 2048, 3072, 4096, 5120, 5376, 5504, 5632, 5760)
ADAM_LR, ADAM_B1, ADAM_B2, ADAM_EPS, ADAM_WD, ADAM_STEP = 0.001, 0.9, 0.999, 1e-08, 0.01, 10
VMEM_LIMIT = 48 * 1024 * 1024


def _params(n_axes):
    return pltpu.CompilerParams(dimension_semantics=("arbitrary",) * n_axes, vmem_limit_bytes=VMEM_LIMIT)


def _tile(n, target):
    if n <= target:
        return n
    best = None
    for t in range(LANE, target + 1, LANE):
        if n % t == 0:
            best = t
    assert best is not None, (n, target)
    return best


def _sigmoid(x):
    return 1.0 / (1.0 + jnp.exp(-x))


def _mm(a, b, *, name, ta=False, tb=False, out_dtype=F32, tm=1024, tn=1024, tk=1024,
        epilogue=None, extras=(), extra_specs=(), out_shape=None, out_specs=None, a_fn=None):
    if ta:
        kdim, m = a.shape
    else:
        m, kdim = a.shape
    if tb:
        n, k2 = b.shape
    else:
        k2, n = b.shape
    assert kdim == k2, (a.shape, b.shape)
    tm, tn, tk = _tile(m, tm), _tile(n, tn), _tile(kdim, tk)
    nk = kdim // tk
    a_spec = pl.BlockSpec((tk, tm), lambda i, j, k: (k, i)) if ta else pl.BlockSpec((tm, tk), lambda i, j, k: (i, k))
    b_spec = pl.BlockSpec((tn, tk), lambda i, j, k: (j, k)) if tb else pl.BlockSpec((tk, tn), lambda i, j, k: (k, j))
    dims = (((0 if ta else 1,), (1 if tb else 0,)), ((), ()))
    ne = len(extras)
    if out_shape is None:
        out_shape = jax.ShapeDtypeStruct((m, n), out_dtype)
        out_specs = pl.BlockSpec((tm, tn), lambda i, j, k: (i, j))
    n_out = len(out_shape) if isinstance(out_shape, (list, tuple)) else 1
    in_place = epilogue is None and n_out == 1 and out_shape.dtype == F32
    scratch = [] if (nk == 1 or in_place) else [pltpu.VMEM((tm, tn), F32)]

    def body(a_ref, b_ref, *rest):
        ex, outs = rest[:ne], rest[ne:ne + n_out]
        av = a_ref[...] if a_fn is None else a_fn(a_ref[...])
        prod = lax.dot_general(av.astype(BF16), b_ref[...].astype(BF16), dims, preferred_element_type=F32)

        def finish(val):
            if epilogue is None:
                outs[0][...] = val.astype(outs[0].dtype)
            else:
                epilogue(val, ex, outs)

        if nk == 1:
            finish(prod)
            return
        k = pl.program_id(2)
        acc = outs[0] if in_place else rest[-1]

        @pl.when(k == 0)
        def _():
            acc[...] = prod

        @pl.when(k > 0)
        def _():
            acc[...] += prod

        if not in_place:
            @pl.when(k == nk - 1)
            def _():
                finish(acc[...])

    return pl.pallas_call(
        body, name=name, grid=(m // tm, n // tn, nk),
        in_specs=[a_spec, b_spec, *extra_specs], out_specs=out_specs, out_shape=out_shape,
        scratch_shapes=scratch, compiler_params=_params(3),
    )(a, b, *extras)


def _tile_spec(tm, tn):
    return pl.BlockSpec((tm, tn), lambda i, j, k: (i, j))


def _pieces_dx(pieces, w, after, *, name, tm=256):
    t = pieces[0][0].shape[0]
    tm = _tile(t, tm)
    npc = len(pieces)

    def body(*refs):
        p_refs, w_ref, out_ref = refs[:npc], refs[npc], refs[-1]
        acc = None
        for (arr, off), p_ref in zip(pieces, p_refs):
            part = lax.dot_general(p_ref[...].astype(BF16), w_ref[:, off:off + arr.shape[1]], _NT,
                                   preferred_element_type=F32)
            acc = part if acc is None else acc + part
        out_ref[...] = acc

    return pl.pallas_call(
        body, name=name, grid=(t // tm,),
        in_specs=[pl.BlockSpec((tm, arr.shape[1]), lambda i: (i, 0)) for arr, _ in pieces]
        + [pl.BlockSpec(w.shape, lambda i: (0, 0)), pl.BlockSpec((8, LANE), lambda i: (0, 0))],
        out_specs=pl.BlockSpec((tm, w.shape[0]), lambda i: (i, 0)),
        out_shape=jax.ShapeDtypeStruct((t, w.shape[0]), F32), compiler_params=_params(1),
    )(*[arr for arr, _ in pieces], w, after)


def _pieces_dw(h, pieces, *, name, tk=1024):
    t, d = h.shape
    tk = _tile(t, tk)
    widths = [p.shape[1] for p in pieces]
    starts = [sum(widths[:i]) for i in range(len(pieces))]

    def body(h_ref, *refs):
        p_refs, out_ref = refs[:-1], refs[-1]
        first = pl.program_id(0) == 0
        hv = h_ref[...]
        for p_ref, start, width in zip(p_refs, starts, widths):
            part = lax.dot_general(hv, p_ref[...].astype(BF16), _TN, preferred_element_type=F32)
            cols = slice(start, start + width)

            @pl.when(first)
            def _():
                out_ref[:, cols] = part

            @pl.when(jnp.logical_not(first))
            def _():
                out_ref[:, cols] += part

    return pl.pallas_call(
        body, name=name, grid=(t // tk,),
        in_specs=[pl.BlockSpec((tk, d), lambda k: (k, 0))] + [pl.BlockSpec((tk, wd), lambda k: (k, 0)) for wd in widths],
        out_specs=pl.BlockSpec((d, sum(widths)), lambda k: (0, 0)),
        out_shape=jax.ShapeDtypeStruct((d, sum(widths)), F32), compiler_params=_params(1),
    )(h, *pieces)


def _rms(x, g):
    r = lax.rsqrt(jnp.mean(x * x, axis=-1, keepdims=True) + EPS)
    return x * r, r


def _row_spec(ts, width, col=0):
    return pl.BlockSpec((None, ts, width), lambda b, i: (b, i, col))


def _vec_spec(width):
    return pl.BlockSpec((None, 1, width), lambda b, i: (b, 0, 0))


def _gain_spec(width):
    return pl.BlockSpec((1, width), lambda b, i: (0, 0))


def _norm_mod(x, g, scale, shift, *, name, ts=256):
    bsz, s, d = x.shape
    ts = min(ts, s)

    def body(x_ref, g_ref, sc_ref, sh_ref, h_ref):
        xh, _ = _rms(x_ref[...], None)
        h_ref[...] = ((xh * g_ref[...]) * (1.0 + sc_ref[...]) + sh_ref[...]).astype(BF16)

    return pl.pallas_call(
        body, name=name, grid=(bsz, s // ts),
        in_specs=[_row_spec(ts, d), _gain_spec(d), _vec_spec(d), _vec_spec(d)],
        out_specs=_row_spec(ts, d), out_shape=jax.ShapeDtypeStruct((bsz, s, d), BF16),
        compiler_params=_params(2),
    )(x, g, scale, shift)


def _resid_norm_mod(x, mixed, gate, g, scale, shift, *, name, ts=256):
    bsz, s, d = x.shape
    ts = min(ts, s)

    def body(x_ref, mx_ref, gt_ref, g_ref, sc_ref, sh_ref, x1_ref, h_ref):
        x1 = x_ref[...] + gt_ref[...] * mx_ref[...]
        x1_ref[...] = x1
        xh, _ = _rms(x1, None)
        h_ref[...] = ((xh * g_ref[...]) * (1.0 + sc_ref[...]) + sh_ref[...]).astype(BF16)

    return pl.pallas_call(
        body, name=name, grid=(bsz, s // ts),
        in_specs=[_row_spec(ts, d), _row_spec(ts, d), _vec_spec(d), _gain_spec(d), _vec_spec(d), _vec_spec(d)],
        out_specs=[_row_spec(ts, d), _row_spec(ts, d)],
        out_shape=[jax.ShapeDtypeStruct((bsz, s, d), F32), jax.ShapeDtypeStruct((bsz, s, d), BF16)],
        compiler_params=_params(2),
    )(x, mixed, gate, g, scale, shift)


def _norm_mod_bwd(dh, xin, resid, g, scale, gate=None, mixed=None, *, name, ts=256):
    bsz, s, d = xin.shape
    ts = min(ts, s)
    gated = gate is not None

    def body(*refs):
        if gated:
            dh_ref, x_ref, rs_ref, g_ref, sc_ref, gt_ref, mx_ref, dx_ref, dsc_ref, dsh_ref, dg_ref, dgt_ref, dmx_ref = refs
        else:
            dh_ref, x_ref, rs_ref, g_ref, sc_ref, dx_ref, dsc_ref, dsh_ref, dg_ref = refs
        b, i = pl.program_id(0), pl.program_id(1)

        @pl.when(i == 0)
        def _():
            dsc_ref[...] = jnp.zeros_like(dsc_ref)
            dsh_ref[...] = jnp.zeros_like(dsh_ref)
            if gated:
                dgt_ref[...] = jnp.zeros_like(dgt_ref)

        @pl.when((i == 0) & (b == 0))
        def _():
            dg_ref[...] = jnp.zeros_like(dg_ref)

        dh_v, gv = dh_ref[...], g_ref[...]
        xh, r = _rms(x_ref[...], None)
        dsc_ref[...] += jnp.sum(dh_v * (xh * gv), axis=0, keepdims=True)
        dsh_ref[...] += jnp.sum(dh_v, axis=0, keepdims=True)
        dn = dh_v * (1.0 + sc_ref[...])
        dg_ref[...] += jnp.sum(dn * xh, axis=0, keepdims=True)
        dxh = dn * gv
        dx = rs_ref[...] + r * (dxh - xh * jnp.mean(dxh * xh, axis=-1, keepdims=True))
        dx_ref[...] = dx
        if gated:
            dgt_ref[...] += jnp.sum(dx * mx_ref[...], axis=0, keepdims=True)
            dmx_ref[...] = (dx * gt_ref[...]).astype(BF16)

    ins = [dh, xin, resid, g, scale]
    in_specs = [_row_spec(ts, d), _row_spec(ts, d), _row_spec(ts, d), _gain_spec(d), _vec_spec(d)]
    out_specs = [_row_spec(ts, d), _vec_spec(d), _vec_spec(d), _gain_spec(d)]
    out_shape = [jax.ShapeDtypeStruct((bsz, s, d), F32), jax.ShapeDtypeStruct((bsz, 1, d), F32),
                 jax.ShapeDtypeStruct((bsz, 1, d), F32), jax.ShapeDtypeStruct((1, d), F32)]
    if gated:
        ins += [gate, mixed]
        in_specs += [_vec_spec(d), _row_spec(ts, d)]
        out_specs += [_vec_spec(d), _row_spec(ts, d)]
        out_shape += [jax.ShapeDtypeStruct((bsz, 1, d), F32), jax.ShapeDtypeStruct((bsz, s, d), BF16)]
    return pl.pallas_call(
        body, name=name, grid=(bsz, s // ts), in_specs=in_specs, out_specs=out_specs, out_shape=out_shape,
        compiler_params=_params(2),
    )(*ins)


def _loss_head(x1, ff, gate2, target, *, name, ts=256):
    bsz, s, d = x1.shape
    ts = min(ts, s)

    def body(x1_ref, ff_ref, gt_ref, t_ref, dy_ref, dff_ref, dgt_ref, loss_ref, acc):
        b, i = pl.program_id(0), pl.program_id(1)

        @pl.when(i == 0)
        def _():
            dgt_ref[...] = jnp.zeros_like(dgt_ref)

        @pl.when((i == 0) & (b == 0))
        def _():
            acc[...] = jnp.zeros_like(acc)

        ffv, gt = ff_ref[...], gt_ref[...]
        diff = (x1_ref[...] + gt * ffv) - t_ref[...]
        acc[...] += jnp.sum((diff * diff).reshape(ts // 8, 8, d), axis=0)
        dy = diff * (1.0 / d)
        dy_ref[...] = dy
        dgt_ref[...] += jnp.sum(dy * ffv, axis=0, keepdims=True)
        dff_ref[...] = (dy * gt).astype(BF16)

        @pl.when((i == pl.num_programs(1) - 1) & (b == pl.num_programs(0) - 1))
        def _():
            loss_ref[...] = jnp.full(loss_ref.shape, jnp.sum(acc[...]), F32)

    return pl.pallas_call(
        body, name=name, grid=(bsz, s // ts),
        in_specs=[_row_spec(ts, d), _row_spec(ts, d), _vec_spec(d), _row_spec(ts, d)],
        out_specs=[_row_spec(ts, d), _row_spec(ts, d), _vec_spec(d), pl.BlockSpec((8, LANE), lambda b, i: (0, 0))],
        out_shape=[jax.ShapeDtypeStruct((bsz, s, d), F32), jax.ShapeDtypeStruct((bsz, s, d), BF16),
                   jax.ShapeDtypeStruct((bsz, 1, d), F32), jax.ShapeDtypeStruct((8, LANE), F32)],
        scratch_shapes=[pltpu.VMEM((8, d), F32)], compiler_params=_params(2),
    )(x1, ff, gate2, target)


def _merge_fwd(proj, b_merge, y_a, y_b, *, name, ts=256):
    bsz, s, _ = proj.shape
    ts = min(ts, s)

    def body(la_ref, lb_ref, ba_ref, bb_ref, ya_ref, yb_ref, out_ref):
        ga = _sigmoid(la_ref[...] + ba_ref[...])
        gb = _sigmoid(lb_ref[...] + bb_ref[...])
        out_ref[...] = (ga * ya_ref[...] + gb * yb_ref[...]).astype(BF16)

    return pl.pallas_call(
        body, name=name, grid=(bsz, s // ts),
        in_specs=[_row_spec(ts, D, OFF_MA // D), _row_spec(ts, D, OFF_MB // D),
                  pl.BlockSpec((1, D), lambda b, i: (0, 0)), pl.BlockSpec((1, D), lambda b, i: (0, 1)),
                  _row_spec(ts, D), _row_spec(ts, D)],
        out_specs=_row_spec(ts, D), out_shape=jax.ShapeDtypeStruct((bsz, s, D), BF16),
        compiler_params=_params(2),
    )(proj, proj, b_merge, b_merge, y_a, y_b)


def _merge_bwd(dmi, proj, b_merge, y_a, y_b, *, name, ts=256):
    bsz, s, _ = proj.shape
    ts = min(ts, s)

    def body(d_ref, la_ref, lb_ref, ba_ref, bb_ref, ya_ref, yb_ref, dya_ref, dyb_ref, dla_ref, dlb_ref, dba_ref, dbb_ref):
        @pl.when((pl.program_id(0) == 0) & (pl.program_id(1) == 0))
        def _():
            dba_ref[...] = jnp.zeros_like(dba_ref)
            dbb_ref[...] = jnp.zeros_like(dbb_ref)

        dv = d_ref[...]
        ga = _sigmoid(la_ref[...] + ba_ref[...])
        gb = _sigmoid(lb_ref[...] + bb_ref[...])
        dya_ref[...] = (dv * ga).astype(BF16)
        dyb_ref[...] = (dv * gb).astype(BF16)
        dla = (dv * ya_ref[...]) * (ga * (1.0 - ga))
        dlb = (dv * yb_ref[...]) * (gb * (1.0 - gb))
        dla_ref[...] = dla.astype(BF16)
        dlb_ref[...] = dlb.astype(BF16)
        dba_ref[...] += jnp.sum(dla, axis=0, keepdims=True)
        dbb_ref[...] += jnp.sum(dlb, axis=0, keepdims=True)

    act = jax.ShapeDtypeStruct((bsz, s, D), BF16)
    return pl.pallas_call(
        body, name=name, grid=(bsz, s // ts),
        in_specs=[_row_spec(ts, D), _row_spec(ts, D, OFF_MA // D), _row_spec(ts, D, OFF_MB // D),
                  pl.BlockSpec((1, D), lambda b, i: (0, 0)), pl.BlockSpec((1, D), lambda b, i: (0, 1)),
                  _row_spec(ts, D), _row_spec(ts, D)],
        out_specs=[_row_spec(ts, D)] * 4 + [_gain_spec(D)] * 2,
        out_shape=[act, act, act, act, jax.ShapeDtypeStruct((1, D), F32), jax.ShapeDtypeStruct((1, D), F32)],
        compiler_params=_params(2),
    )(dmi, proj, proj, b_merge, b_merge, y_a, y_b)


def _tri(lower):
    r = lax.broadcasted_iota(jnp.int32, (CHUNK, CHUNK), 0)
    c = lax.broadcasted_iota(jnp.int32, (CHUNK, CHUNK), 1)
    return jnp.where((c <= r) if lower else (c >= r), 1.0, 0.0).astype(F32)


def _gla_logits(a_ref, wal_ref, bal_ref):
    logits = jnp.dot(a_ref[...].astype(BF16), wal_ref[...].astype(BF16), preferred_element_type=F32) + bal_ref[...]
    la = (jnp.minimum(logits, 0.0) - jnp.log(1.0 + jnp.exp(-jnp.abs(logits)))) * (1.0 / GTAU)
    return logits, la


def _chunk_cumsum(la_n, tri):
    cum = jnp.dot(tri, la_n, preferred_element_type=F32, precision=lax.Precision.HIGHEST)
    return cum, jnp.sum(la_n, axis=0, keepdims=True)


def _gla_specs(s, nc):
    def blk(width, off):
        return pl.BlockSpec((None, s, width), lambda h, b: (b, 0, off // width + h))

    proj_specs = [blk(GDK, OFF_Q), blk(GDK, OFF_K), blk(GDV, OFF_V), blk(GDV, OFF_G),
                  pl.BlockSpec((None, s, LANE), lambda h, b: (b, 0, OFF_A // LANE)),
                  pl.BlockSpec((LANE, GDK), lambda h, b: (0, h)), pl.BlockSpec((1, GDK), lambda h, b: (0, h)),
                  pl.BlockSpec((1, GDV), lambda h, b: (0, 0))]
    st_spec = pl.BlockSpec((None, None, nc, GDV, GDK), lambda h, b: (b, h, 0, 0, 0))
    return blk, proj_specs, st_spec


def _gla_fwd(proj, w_alpha_p, b_alpha, out_norm_g, *, name):
    bsz, s, _ = proj.shape
    nc = s // CHUNK
    scale = GDK ** -0.5

    rb = min(512, s)

    def body(q_ref, k_ref, v_ref, g_ref, a_ref, wal_ref, bal_ref, ong_ref, o_ref, og_ref, st_ref):
        _, la = _gla_logits(a_ref, wal_ref, bal_ref)
        tri = _tri(True)
        st = jnp.zeros((GDV, GDK), F32)
        for n in range(nc):
            rows = pl.ds(n * CHUNK, CHUNK)
            cum, cum_end = _chunk_cumsum(la[n * CHUNK:(n + 1) * CHUNK], tri)
            kd = k_ref[rows, :] * jnp.exp(cum_end - cum)
            ut = lax.dot_general(v_ref[rows, :].astype(BF16), kd.astype(BF16), _TN, preferred_element_type=F32)
            st = st * jnp.exp(cum_end) + ut
            st_ref[n] = st
            o_ref[rows, :] = lax.dot_general((q_ref[rows, :] * scale).astype(BF16), st.astype(BF16), _NT,
                                             preferred_element_type=F32)
        for j in range(0, s, rb):
            blk_rows = pl.ds(j, rb)
            oh, _ = _rms(o_ref[blk_rows, :], None)
            gv = g_ref[blk_rows, :]
            og_ref[blk_rows, :] = ((oh * ong_ref[...]) * (gv * _sigmoid(gv))).astype(BF16)

    blk, proj_specs, st_spec = _gla_specs(s, nc)
    return pl.pallas_call(
        body, name=name, grid=(GH, bsz), in_specs=proj_specs, out_specs=[blk(GDV, 0), blk(GDV, 0), st_spec],
        out_shape=[jax.ShapeDtypeStruct((bsz, s, GH * GDV), F32), jax.ShapeDtypeStruct((bsz, s, GH * GDV), BF16),
                   jax.ShapeDtypeStruct((bsz, GH, nc, GDV, GDK), F32)],
        compiler_params=_params(2),
    )(proj, proj, proj, proj, proj, w_alpha_p, b_alpha, out_norm_g)


def _gla_bwd(dog, o, states, proj, w_alpha_p, b_alpha, out_norm_g, *, name):
    bsz, s, _ = proj.shape
    nc = s // CHUNK
    scale = GDK ** -0.5

    def body(dog_ref, o_ref, st_ref, q_ref, k_ref, v_ref, g_ref, a_ref, wal_ref, bal_ref, ong_ref,
             dq_ref, dk_ref, dv_ref, dg_ref, dl_ref, dbal_ref, dong_ref, do_scr, dlog_scr):
        h, b = pl.program_id(0), pl.program_id(1)

        @pl.when(b == 0)
        def _():
            dbal_ref[...] = jnp.zeros_like(dbal_ref)

        @pl.when((b == 0) & (h == 0))
        def _():
            dong_ref[...] = jnp.zeros_like(dong_ref)

        ong = ong_ref[...]
        for j in range(0, s, rb):
            blk_rows = pl.ds(j, rb)
            gv, dogv = g_ref[blk_rows, :], dog_ref[blk_rows, :]
            sg = _sigmoid(gv)
            oh, r = _rms(o_ref[blk_rows, :], None)
            don = dogv * (gv * sg)
            dg_ref[blk_rows, :] = (dogv * (oh * ong) * (sg * (1.0 + gv * (1.0 - sg)))).astype(BF16)
            dong_ref[...] += jnp.sum(don * oh, axis=0, keepdims=True)
            doh = don * ong
            do_scr[blk_rows, :] = (r * (doh - oh * jnp.mean(doh * oh, axis=-1, keepdims=True))).astype(BF16)

        logits, la = _gla_logits(a_ref, wal_ref, bal_ref)
        tri_lo, tri_up = _tri(True), _tri(False)
        carry = jnp.zeros((GDV, GDK), F32)
        for n in range(nc - 1, -1, -1):
            rows = pl.ds(n * CHUNK, CHUNK)
            cum, cum_end = _chunk_cumsum(la[n * CHUNK:(n + 1) * CHUNK], tri_lo)
            decay = jnp.exp(cum_end)
            w = jnp.exp(cum_end - cum)
            kd = k_ref[rows, :] * w
            do_b = do_scr[rows, :]
            qs_b = (q_ref[rows, :] * scale).astype(BF16)
            dq_ref[rows, :] = (jnp.dot(do_b, st_ref[n].astype(BF16), preferred_element_type=F32) * scale).astype(BF16)
            dsn = lax.dot_general(do_b, qs_b, _TN, preferred_element_type=F32) + carry
            carry = dsn * decay
            dsn_b = dsn.astype(BF16)
            dv_ref[rows, :] = lax.dot_general(kd.astype(BF16), dsn_b, _NT, preferred_element_type=F32).astype(BF16)
            dkd = jnp.dot(v_ref[rows, :].astype(BF16), dsn_b, preferred_element_type=F32)
            dk_ref[rows, :] = (dkd * w).astype(BF16)
            e = dkd * kd
            dcum_end = jnp.sum(e, axis=0, keepdims=True)
            if n > 0:
                dcum_end += jnp.sum(dsn * st_ref[n - 1], axis=0, keepdims=True) * decay
            dlog_scr[rows, :] = dcum_end - jnp.dot(tri_up, e, preferred_element_type=F32,
                                                  precision=lax.Precision.HIGHEST)
        dlog = dlog_scr[...] * (1.0 / GTAU) * (1.0 - _sigmoid(logits))
        dl_ref[...] = dlog.astype(BF16)
        dbal_ref[...] += jnp.sum(dlog, axis=0, keepdims=True)

    rb = min(512, s)

    blk, proj_specs, st_spec = _gla_specs(s, nc)
    act = lambda wd: jax.ShapeDtypeStruct((bsz, s, wd), BF16)
    return pl.pallas_call(
        body, name=name, grid=(GH, bsz), in_specs=[blk(GDV, 0), blk(GDV, 0), st_spec, *proj_specs],
        out_specs=[blk(GDK, 0), blk(GDK, 0), blk(GDV, 0), blk(GDV, 0), blk(GDK, 0),
                   pl.BlockSpec((1, GDK), lambda h, b: (0, h)), pl.BlockSpec((1, GDV), lambda h, b: (0, 0))],
        out_shape=[act(GH * GDK), act(GH * GDK), act(GH * GDV), act(GH * GDV), act(GH * GDK),
                   jax.ShapeDtypeStruct((1, GH * GDK), F32), jax.ShapeDtypeStruct((1, GDV), F32)],
        scratch_shapes=[pltpu.VMEM((s, GDV), BF16), pltpu.VMEM((s, GDK), F32)], compiler_params=_params(2),
    )(dog, o, states, proj, proj, proj, proj, proj, w_alpha_p, b_alpha, out_norm_g)


def _lane():
    return lax.broadcasted_iota(jnp.int32, (1, LANE), 1)


def _swap_halves(x):
    lane = _lane()
    half = MROPE // 2
    lo = (lane >= MNOPE) & (lane < MNOPE + half)
    hi = (lane >= MNOPE + half) & (lane < MQK)
    return jnp.where(lo, pltpu.roll(x, LANE - half, 1), jnp.where(hi, pltpu.roll(x, half, 1), 0.0))


def _norm96(x, g):
    r = lax.rsqrt(jnp.sum(x * x, axis=-1, keepdims=True) * (1.0 / MQK) + EPS)
    return x * r, r


def _lat_norm(proj, q_lat_g, kv_lat_g, *, name, ts=512):
    t = proj.shape[0]
    ts = min(ts, t)

    def body(cq_ref, ckv_ref, gq_ref, gk_ref, oq_ref, ok_ref):
        xq, _ = _rms(cq_ref[...], None)
        oq_ref[...] = (xq * gq_ref[...]).astype(BF16)
        xk, _ = _rms(ckv_ref[...], None)
        ok_ref[...] = (xk * gk_ref[...]).astype(BF16)

    return pl.pallas_call(
        body, name=name, grid=(t // ts,),
        in_specs=[pl.BlockSpec((ts, MQR), lambda i: (i, OFF_CQ // MQR)), pl.BlockSpec((ts, MKVR), lambda i: (i, OFF_CKV // MKVR)),
                  pl.BlockSpec((1, MQR), lambda i: (0, 0)), pl.BlockSpec((1, MKVR), lambda i: (0, 0))],
        out_specs=[pl.BlockSpec((ts, MQR), lambda i: (i, 0)), pl.BlockSpec((ts, MKVR), lambda i: (i, 0))],
        out_shape=[jax.ShapeDtypeStruct((t, MQR), BF16), jax.ShapeDtypeStruct((t, MKVR), BF16)],
        compiler_params=_params(1),
    )(proj, proj, q_lat_g, kv_lat_g)


def _lat_norm_bwd(dcqn, dckvn, proj, q_lat_g, kv_lat_g, *, name, ts=512):
    t = proj.shape[0]
    ts = min(ts, t)

    def one(d_ref, x_ref, g_ref, dx_ref, dg_ref):
        xh, r = _rms(x_ref[...], None)
        dn = d_ref[...]
        dg_ref[...] += jnp.sum(dn * xh, axis=0, keepdims=True)
        dxh = dn * g_ref[...]
        dx_ref[...] = (r * (dxh - xh * jnp.mean(dxh * xh, axis=-1, keepdims=True))).astype(BF16)

    def body(dq_ref, dk_ref, cq_ref, ckv_ref, gq_ref, gk_ref, dxq_ref, dxk_ref, dgq_ref, dgk_ref):
        @pl.when(pl.program_id(0) == 0)
        def _():
            dgq_ref[...] = jnp.zeros_like(dgq_ref)
            dgk_ref[...] = jnp.zeros_like(dgk_ref)

        one(dq_ref, cq_ref, gq_ref, dxq_ref, dgq_ref)
        one(dk_ref, ckv_ref, gk_ref, dxk_ref, dgk_ref)

    return pl.pallas_call(
        body, name=name, grid=(t // ts,),
        in_specs=[pl.BlockSpec((ts, MQR), lambda i: (i, 0)), pl.BlockSpec((ts, MKVR), lambda i: (i, 0)),
                  pl.BlockSpec((ts, MQR), lambda i: (i, OFF_CQ // MQR)), pl.BlockSpec((ts, MKVR), lambda i: (i, OFF_CKV // MKVR)),
                  pl.BlockSpec((1, MQR), lambda i: (0, 0)), pl.BlockSpec((1, MKVR), lambda i: (0, 0))],
        out_specs=[pl.BlockSpec((ts, MQR), lambda i: (i, 0)), pl.BlockSpec((ts, MKVR), lambda i: (i, 0)),
                   pl.BlockSpec((1, MQR), lambda i: (0, 0)), pl.BlockSpec((1, MKVR), lambda i: (0, 0))],
        out_shape=[jax.ShapeDtypeStruct((t, MQR), BF16), jax.ShapeDtypeStruct((t, MKVR), BF16),
                   jax.ShapeDtypeStruct((1, MQR), F32), jax.ShapeDtypeStruct((1, MKVR), F32)],
        compiler_params=_params(1),
    )(dcqn, dckvn, proj, proj, q_lat_g, kv_lat_g)


def _qk_prep(q_raw, kv, proj, cos_t, sin_t, gq, gk, *, name, ts=2048):
    t = q_raw.shape[0]
    ts = min(ts, t)

    def body(q_ref, kv_ref, kpe_ref, c_ref, s_ref, gq_ref, gk_ref, qo_ref, ko_ref, vo_ref):
        cs, sn = c_ref[...], s_ref[...]
        nope = _lane() < MNOPE
        qn, _ = _norm96(q_ref[...], None)
        qn = qn * gq_ref[...]
        qo_ref[...] = (qn * cs + _swap_halves(qn) * sn).astype(BF16)
        kvv = kv_ref[...]
        kn, _ = _norm96(jnp.where(nope, kvv, kpe_ref[...]), None)
        kn = kn * gk_ref[...]
        ko_ref[...] = (kn * cs + _swap_halves(kn) * sn).astype(BF16)
        vo_ref[...] = jnp.where(nope, pltpu.roll(kvv, MNOPE, 1), 0.0).astype(BF16)

    hd = pl.BlockSpec((ts, LANE), lambda i, h: (i, h))
    shared = lambda col: pl.BlockSpec((ts, LANE), lambda i, h: (i, col))
    gain = pl.BlockSpec((1, LANE), lambda i, h: (0, 0))
    out = jax.ShapeDtypeStruct((t, MH * LANE), BF16)
    return pl.pallas_call(
        body, name=name, grid=(t // ts, MH),
        in_specs=[hd, hd, shared(OFF_KPE // LANE), shared(0), shared(0), gain, gain],
        out_specs=[hd, hd, hd], out_shape=[out, out, out], compiler_params=_params(2),
    )(q_raw, kv, proj, cos_t, sin_t, gq, gk)


def _qk_prep_bwd(dq, dk, dv, q_raw, kv, proj, cos_t, sin_t, gq, gk, *, name, ts=2048):
    t = q_raw.shape[0]
    ts = min(ts, t)

    def norm_bwd(dy, x, g, dg_ref):
        xh, r = _norm96(x, None)
        dg_ref[...] += jnp.sum(dy * xh, axis=0, keepdims=True)
        dxh = dy * g
        return r * (dxh - xh * (jnp.sum(dxh * xh, axis=-1, keepdims=True) * (1.0 / MQK)))

    def body(dq_ref, dk_ref, dv_ref, q_ref, kv_ref, kpe_ref, c_ref, s_ref, gq_ref, gk_ref,
             dqr_ref, dkv_ref, dkpe_ref, dgq_ref, dgk_ref):
        i, h = pl.program_id(0), pl.program_id(1)

        @pl.when(h == 0)
        def _():
            dkpe_ref[...] = jnp.zeros_like(dkpe_ref)

        @pl.when((h == 0) & (i == 0))
        def _():
            dgq_ref[...] = jnp.zeros_like(dgq_ref)
            dgk_ref[...] = jnp.zeros_like(dgk_ref)

        cs, sn = c_ref[...], s_ref[...]
        lane = _lane()
        nope = lane < MNOPE
        dqv = dq_ref[...]
        dqn = dqv * cs + _swap_halves(dqv * sn)
        dqr_ref[...] = norm_bwd(dqn, q_ref[...], gq_ref[...], dgq_ref).astype(BF16)
        dkv_ = dk_ref[...]
        dkn = dkv_ * cs + _swap_halves(dkv_ * sn)
        kvv = kv_ref[...]
        dkr = norm_bwd(dkn, jnp.where(nope, kvv, kpe_ref[...]), gk_ref[...], dgk_ref)
        dkv_ref[...] = jnp.where(nope, dkr, pltpu.roll(dv_ref[...], MNOPE, 1)).astype(BF16)
        dkpe_ref[...] += jnp.where((lane >= MNOPE) & (lane < MQK), dkr, 0.0)

    hd = pl.BlockSpec((ts, LANE), lambda i, h: (i, h))
    shared = lambda col: pl.BlockSpec((ts, LANE), lambda i, h: (i, col))
    gain = pl.BlockSpec((1, LANE), lambda i, h: (0, 0))
    out = jax.ShapeDtypeStruct((t, MH * LANE), BF16)
    return pl.pallas_call(
        body, name=name, grid=(t // ts, MH),
        in_specs=[hd, hd, hd, hd, hd, shared(OFF_KPE // LANE), shared(0), shared(0), gain, gain],
        out_specs=[hd, hd, shared(0), gain, gain],
        out_shape=[out, out, jax.ShapeDtypeStruct((t, LANE), F32), jax.ShapeDtypeStruct((1, LANE), F32),
                   jax.ShapeDtypeStruct((1, LANE), F32)],
        compiler_params=_params(2),
    )(dq, dk, dv, q_raw, kv, proj, cos_t, sin_t, gq, gk)


_NT = (((1,), (1,)), ((), ()))
_TN = (((0,), (0,)), ((), ()))


SOFTMAX_SCALE = MQK ** -0.5
Q_PRESCALE = SOFTMAX_SCALE * float(np.log2(np.e))


def _attn_weights(q, k_ref, lo, tq):
    row = lax.broadcasted_iota(jnp.int32, (tq, tq), 0) // CHUNK
    col = lax.broadcasted_iota(jnp.int32, (tq, tq), 1) // CHUNK
    sd = lax.dot_general(q, k_ref[pl.ds(lo, tq), :], _NT, preferred_element_type=F32)
    sd = jnp.where(col <= row, sd, -1e30)
    m = jnp.max(sd, axis=-1, keepdims=True)
    if lo:
        so = lax.dot_general(q, k_ref[pl.ds(0, lo), :], _NT, preferred_element_type=F32)
        m = jnp.maximum(m, jnp.max(so, axis=-1, keepdims=True))
        eo = jnp.exp2(so - m)
        ed = jnp.exp2(sd - m)
        return eo, ed, 1.0 / (jnp.sum(eo, axis=-1, keepdims=True) + jnp.sum(ed, axis=-1, keepdims=True))
    ed = jnp.exp2(sd - m)
    return None, ed, 1.0 / jnp.sum(ed, axis=-1, keepdims=True)


def _attn_fwd(q, k, v, *, name, tq=256):
    bsz, s, _ = q.shape
    tq = min(tq, s)

    def body(q_ref, k_ref, v_ref, o_ref):
        for i in range(s // tq):
            lo = i * tq
            eo, ed, inv = _attn_weights(q_ref[pl.ds(lo, tq), :], k_ref, lo, tq)
            o = jnp.dot(ed.astype(BF16), v_ref[pl.ds(lo, tq), :], preferred_element_type=F32)
            if lo:
                o += jnp.dot(eo.astype(BF16), v_ref[pl.ds(0, lo), :], preferred_element_type=F32)
            o_ref[pl.ds(lo, tq), :] = (o * inv).astype(BF16)

    spec = pl.BlockSpec((None, s, LANE), lambda b, h: (b, 0, h))
    return pl.pallas_call(
        body, name=name, grid=(bsz, MH), in_specs=[spec, spec, spec], out_specs=spec,
        out_shape=jax.ShapeDtypeStruct((bsz, s, MH * LANE), BF16), compiler_params=_params(2),
    )(q, k, v)


def _attn_bwd(q, k, v, do, *, name, tq=256):
    bsz, s, _ = q.shape
    tq = min(tq, s)

    def body(q_ref, k_ref, v_ref, do_ref, dq_ref, dk_ref, dv_ref):
        dk_ref[...] = jnp.zeros_like(dk_ref)
        dv_ref[...] = jnp.zeros_like(dv_ref)
        for i in range(s // tq):
            lo = i * tq
            here, before = pl.ds(lo, tq), pl.ds(0, lo)
            qv, dov = q_ref[here, :], do_ref[here, :]
            eo, ed, inv = _attn_weights(qv, k_ref, lo, tq)
            do_n = (dov.astype(F32) * inv).astype(BF16)
            dv_ref[here, :] += lax.dot_general(ed.astype(BF16), do_n, _TN, preferred_element_type=F32)
            dpd = lax.dot_general(dov, v_ref[here, :], _NT, preferred_element_type=F32)
            delta = jnp.sum(dpd * ed, axis=-1, keepdims=True)
            if lo:
                dv_ref[before, :] += lax.dot_general(eo.astype(BF16), do_n, _TN, preferred_element_type=F32)
                dpo = lax.dot_general(dov, v_ref[before, :], _NT, preferred_element_type=F32)
                delta += jnp.sum(dpo * eo, axis=-1, keepdims=True)
            delta = delta * inv
            r = inv * SOFTMAX_SCALE
            dsd = (ed * (dpd - delta) * r).astype(BF16)
            dq = jnp.dot(dsd, k_ref[here, :], preferred_element_type=F32)
            dk_ref[here, :] += lax.dot_general(dsd, qv, _TN, preferred_element_type=F32)
            if lo:
                dso = (eo * (dpo - delta) * r).astype(BF16)
                dq += jnp.dot(dso, k_ref[before, :], preferred_element_type=F32)
                dk_ref[before, :] += lax.dot_general(dso, qv, _TN, preferred_element_type=F32)
            dq_ref[here, :] = dq
        dk_ref[...] = dk_ref[...] * (1.0 / Q_PRESCALE)

    spec = pl.BlockSpec((None, s, LANE), lambda b, h: (b, 0, h))
    out = jax.ShapeDtypeStruct((bsz, s, MH * LANE), F32)
    return pl.pallas_call(
        body, name=name, grid=(bsz, MH), in_specs=[spec] * 4, out_specs=[spec] * 3, out_shape=[out, out, out],
        compiler_params=_params(2),
    )(q, k, v, do)


def _adamw(w, g, m, v, *, name, tr=256, by_cols=False):
    rows, cols = w.shape
    tr = _tile_rows(rows, tr)

    def body(w_ref, g_ref, m_ref, v_ref, d_ref, nm_ref, nv_ref):
        d_ref[...], nm_ref[...], nv_ref[...] = _adamw_update(w_ref[...], g_ref[...], m_ref[...], v_ref[...])

    spec = pl.BlockSpec((rows, LANE), lambda i: (0, i)) if by_cols else pl.BlockSpec((tr, cols), lambda i: (i, 0))
    out = jax.ShapeDtypeStruct((rows, cols), F32)
    return pl.pallas_call(body, name=name, grid=(cols // LANE if by_cols else rows // tr,), in_specs=[spec] * 4,
                          out_specs=[spec] * 3, out_shape=[out, out, out], compiler_params=_params(1))(w, g, m, v)


def _tile_rows(rows, target):
    if rows <= target:
        return rows
    best = 8
    for t in range(8, target + 1, 8):
        if rows % t == 0:
            best = t
    return best


def _adamw_update(w, g, m, v):
    nm = ADAM_B1 * m + (1.0 - ADAM_B1) * g
    nv = ADAM_B2 * v + (1.0 - ADAM_B2) * (g * g)
    m_hat = nm / (1.0 - ADAM_B1 ** ADAM_STEP)
    v_hat = nv / (1.0 - ADAM_B2 ** ADAM_STEP)
    return -ADAM_LR * (m_hat / (jnp.sqrt(v_hat) + ADAM_EPS) + ADAM_WD * w), nm, nv


def _adamw_halves(w, m, v, mine, theirs, sel, *, name, tr=256):
    rows, cols = w.shape
    tr = _tile_rows(rows // 2, tr)
    nh = rows // 2 // tr

    def body(sel_ref, w_ref, m_ref, v_ref, mine_ref, theirs_ref, g_ref, d_ref, nm_ref, nv_ref):
        lower = pl.program_id(0) < nh
        south = sel_ref[0] == 0
        gv = jnp.where(lower == south, mine_ref[...], theirs_ref[...])
        g_ref[...] = gv
        d_ref[...], nm_ref[...], nv_ref[...] = _adamw_update(w_ref[...], gv, m_ref[...], v_ref[...])

    full = pl.BlockSpec((tr, cols), lambda i, sel_ref: (i, 0))
    half = pl.BlockSpec((tr, cols), lambda i, sel_ref: (i % nh, 0))
    out = jax.ShapeDtypeStruct((rows, cols), F32)
    return pl.pallas_call(
        body, name=name, out_shape=[out] * 4, compiler_params=_params(1),
        grid_spec=pltpu.PrefetchScalarGridSpec(num_scalar_prefetch=1, grid=(rows // tr,),
                                               in_specs=[full, full, full, half, half], out_specs=[full] * 4),
    )(sel, w, m, v, mine, theirs)


def _pair_add(x, sib, sel, *, name, tr=256):
    n, _, rows, cols = x.shape
    tr = _tile_rows(rows, tr)

    def body(sel_ref, x_ref, s_ref, o_ref):
        o_ref[...] = (x_ref[...] + s_ref[...]).astype(BF16)

    spec = pl.BlockSpec((None, tr, cols), lambda j, i, sel_ref: (j, i, 0))
    return pl.pallas_call(
        body, name=name, out_shape=jax.ShapeDtypeStruct((n, rows, cols), BF16), compiler_params=_params(2),
        grid_spec=pltpu.PrefetchScalarGridSpec(
            num_scalar_prefetch=1, grid=(n, rows // tr),
            in_specs=[pl.BlockSpec((None, None, tr, cols), lambda j, i, sel_ref: (j, sel_ref[0], i, 0)), spec],
            out_specs=spec),
    )(sel, x, sib)


def _chip_sum(pair, recv, sel, *, name, tr=256):
    _, rows, cols = pair.shape
    tr = _tile_rows(rows, tr)

    def body(sel_ref, p_ref, r_ref, o_ref):
        acc = p_ref[...].astype(F32)
        for k in range(3):
            acc = acc + r_ref[k].astype(F32)
        o_ref[...] = acc

    return pl.pallas_call(
        body, name=name, out_shape=jax.ShapeDtypeStruct((rows, cols), F32), compiler_params=_params(1),
        grid_spec=pltpu.PrefetchScalarGridSpec(
            num_scalar_prefetch=1, grid=(rows // tr,),
            in_specs=[pl.BlockSpec((None, tr, cols), lambda i, sel_ref: (sel_ref[0], i, 0)),
                      pl.BlockSpec((3, tr, cols), lambda i, sel_ref: (0, i, 0))],
            out_specs=pl.BlockSpec((tr, cols), lambda i, sel_ref: (i, 0))),
    )(sel, pair, recv)


def _me():
    return lax.axis_index("x"), lax.axis_index("y"), lax.axis_index("c")


def _flip(pos, bits):
    x, y, c = pos
    return (x ^ bits[0] if bits[0] else x, y ^ bits[1] if bits[1] else y, c ^ bits[2] if bits[2] else c)


ANY = pl.BlockSpec(memory_space=pl.ANY)


def _all_gather8(xs, *, name):
    n = len(xs)
    flips = [((k >> 2) & 1, (k >> 1) & 1, k & 1) for k in range(1, 8)]

    def body(*refs):
        x_refs, out_refs, (send_sems, recv_sems, local_sems) = refs[:n], refs[n:2 * n], refs[2 * n:]
        me = _me()
        slot = lambda p: 4 * p[0] + 2 * p[1] + p[2]
        copies = []
        for i in range(n):
            mine = pltpu.make_async_copy(x_refs[i], out_refs[i].at[slot(me)], local_sems.at[i])
            mine.start()
            copies.append(mine)
            for k, f in enumerate(flips):
                peer = _flip(me, f)
                sems = dict(send_sem=send_sems.at[7 * i + k], recv_sem=recv_sems.at[7 * i + k], device_id=peer,
                            device_id_type=MESH)
                cp = pltpu.make_async_remote_copy(src_ref=x_refs[i], dst_ref=out_refs[i].at[slot(me)], **sems)
                cp.start()
                copies.append(cp)
                copies.append(pltpu.make_async_remote_copy(src_ref=x_refs[i], dst_ref=out_refs[i].at[slot(peer)], **sems))
        for i in range(n):
            base = i * 15
            copies[base].wait()
            for k in range(7):
                copies[base + 1 + 2 * k].wait_send()
                copies[base + 2 + 2 * k].wait_recv()

    outs = pl.pallas_call(
        body, name=name, in_specs=[ANY] * n, out_specs=[ANY] * n,
        out_shape=[jax.ShapeDtypeStruct((8, *x.shape), x.dtype) for x in xs],
        scratch_shapes=[pltpu.SemaphoreType.DMA((7 * n,)), pltpu.SemaphoreType.DMA((7 * n,)),
                        pltpu.SemaphoreType.DMA((n,))])(*xs)
    return list(outs)


CHIP_FLIPS = [(1, 0, 0), (0, 1, 0), (1, 1, 0)]


def _chip():
    return 2 * lax.axis_index("x") + lax.axis_index("y")


HBM = pl.BlockSpec(memory_space=pltpu.HBM)
SEM = pl.BlockSpec(memory_space=pltpu.SEMAPHORE)
EFFECT = pltpu.SideEffectType.DATAFLOW_SIDE_EFFECTING


def _plan_copies(plan, refs, send_sems, recv_sems):
    return [pltpu.make_async_remote_copy(src_ref=src, dst_ref=dst, send_sem=send_sems.at[k], recv_sem=recv_sems.at[k],
                                         device_id=to, device_id_type=MESH) for k, (src, dst, to) in enumerate(plan(refs))]


def _rdma_start(arrays, n_copies, plan, deps, *, name):
    n, nd = len(arrays), len(deps)

    def body(*refs):
        for cp in _plan_copies(plan, refs[:n], refs[n + nd], refs[n + nd + 1]):
            cp.start()
        refs[-1][...] = jnp.zeros_like(refs[-1])

    outs = pl.pallas_call(
        body, name=name,
        out_shape=(pltpu.SemaphoreType.DMA((n_copies,)), pltpu.SemaphoreType.DMA((n_copies,)),
                   *[pltpu.HBM(a.shape, a.dtype) for a in arrays], jax.ShapeDtypeStruct((8, LANE), F32)),
        in_specs=[HBM] * n + [ANY] * nd, out_specs=(SEM, SEM, *[HBM] * n, pl.BlockSpec(memory_space=pltpu.VMEM)),
        input_output_aliases={i: i + 2 for i in range(n)}, compiler_params=pltpu.CompilerParams(has_side_effects=EFFECT),
    )(*[pltpu.with_memory_space_constraint(a, pltpu.HBM) for a in arrays], *deps)
    return outs[0], outs[1], list(outs[2:2 + n]), outs[-1]


def _rdma_wait(send_sems, recv_sems, arrays, plan, after, *, name):
    n = len(arrays)

    def body(*refs):
        for cp in _plan_copies(plan, refs[:n], refs[n], refs[n + 1]):
            cp.wait_send()
            cp.wait_recv()

    return list(pl.pallas_call(
        body, name=name, out_shape=tuple(pltpu.HBM(a.shape, a.dtype) for a in arrays),
        in_specs=[HBM] * n + [SEM, SEM, ANY], out_specs=tuple([HBM] * n), input_output_aliases={i: i for i in range(n)},
        compiler_params=pltpu.CompilerParams(has_side_effects=EFFECT),
    )(*arrays, send_sems, recv_sems, after))


def _gather_plan(n):
    def plan(refs):
        me = _me()
        slot = 2 * me[0] + me[1]
        return [(refs[i].at[me[2]], refs[n + i].at[slot, me[2]], _flip(me, f)) for i in range(n) for f in CHIP_FLIPS]
    return plan


def _scatter_plan(n):
    def plan(refs):
        me = _me()
        out = []
        for i in range(n):
            for k, f in enumerate(CHIP_FLIPS):
                peer = _flip(me, f)
                out.append((refs[i].at[2 * peer[0] + peer[1]], refs[n + i].at[k], peer))
        return out
    return plan


def _sibling_plan(n, src_of):
    def plan(refs):
        me = _me()
        return [(src_of(refs[i], me[2]), refs[n + i], _flip(me, (0, 0, 1))) for i in range(n)]
    return plan


def _gather8_plan(n):
    def plan(refs):
        me = _me()
        slot = 4 * me[0] + 2 * me[1] + me[2]
        return [(refs[i], refs[n + i].at[slot], _flip(me, ((k >> 2) & 1, (k >> 1) & 1, k & 1)))
                for i in range(n) for k in range(1, 8)]
    return plan


def _pair_fill(lands, *, name):
    n = len(lands)

    def body(*refs):
        in_refs, (send_sems, recv_sems) = refs[:n], refs[2 * n:]
        me = _me()
        sib = _flip(me, (0, 0, 1))
        copies = []
        for i in range(n):
            for k, f in enumerate(CHIP_FLIPS):
                peer = _flip(me, f)
                slot = 2 * peer[0] + peer[1]
                mine, theirs = in_refs[i].at[slot, me[2]], in_refs[i].at[slot, 1 - me[2]]
                cp = pltpu.make_async_remote_copy(src_ref=mine, dst_ref=mine, send_sem=send_sems.at[3 * i + k],
                                                  recv_sem=recv_sems.at[3 * i + k], device_id=sib, device_id_type=MESH)
                cp.start()
                copies.append((cp, pltpu.make_async_remote_copy(
                    src_ref=mine, dst_ref=theirs, send_sem=send_sems.at[3 * i + k], recv_sem=recv_sems.at[3 * i + k],
                    device_id=sib, device_id_type=MESH)))
        for cp, arrival in copies:
            arrival.wait_recv()
            cp.wait_send()

    return list(pl.pallas_call(
        body, name=name, in_specs=[ANY] * n, out_specs=[ANY] * n,
        out_shape=[jax.ShapeDtypeStruct(a.shape, a.dtype) for a in lands], input_output_aliases={i: i for i in range(n)},
        scratch_shapes=[pltpu.SemaphoreType.DMA((3 * n,)), pltpu.SemaphoreType.DMA((3 * n,))])(*lands))


def _own_and_landed(lands, xs):
    chip = _chip()
    return [[jnp.where(chip == j, x, o.reshape(4, *x.shape)[j]) for j in range(4)] for o, x in zip(lands, xs)]


BIG = (("w_in", (D, IN_WIDTH // 4), 1), ("gla_w_o", (D // 4, D), 0), ("mla_w_uq", (MQR, MH * MQK // 4), 1),
       ("mla_w_ukv", (MKVR, MH * (MNOPE + MVD) // 4), 1), ("mla_w_o", (D // 4, D), 0), ("w_out", (D // 4, D), 0),
       ("mlp_w1", (D, DFF // 4), 1), ("mlp_w2", (DFF // 4, D), 0))
ADA_SHARD = (D, 6 * D // 4)
SMALL = (("b_ada", 6 * D), ("norm1_g", D), ("b_merge", 2 * D), ("gla_b_alpha", GH * GDK), ("gla_out_norm_g", GDV),
         ("mla_q_lat_g", MQR), ("mla_kv_lat_g", MKVR), ("mla_qn_g", MQK), ("mla_kn_g", MQK), ("norm2_g", D))


W_IN_SEGMENTS = ((0, 3072, OFF_Q), (3072, 3088, OFF_A), (3088, 3344, OFF_CQ), (3344, 3472, OFF_CKV),
                 (3472, 3504, OFF_KPE + MNOPE), (3504, 5552, OFF_MA))
W_IN_SPLIT = OFF_MA
SMALL_ROWS, SMALL_COLS = 32, 2 * D
W_ALPHA_ROW = 16
LOSS_ROW = 15
SMALL_RED = tuple((n, k) for n, k in SMALL if n != "b_ada")


def _pack_small(grads, d_w_alpha, loss_row, *, name):
    def body(*refs):
        g_refs, wa_ref, loss_ref, out_ref = refs[:-3], refs[-3], refs[-2], refs[-1]
        out_ref[...] = jnp.zeros_like(out_ref)
        for i, ((_, k), g_ref) in enumerate(zip(SMALL_RED, g_refs)):
            out_ref[i:i + 1, 0:k] = g_ref[...]
        out_ref[LOSS_ROW:LOSS_ROW + 1, 0:LANE] = loss_ref[...]
        out_ref[W_ALPHA_ROW:W_ALPHA_ROW + GLR, 0:GH * GDK] = wa_ref[...]

    return pl.pallas_call(body, name=name, out_shape=jax.ShapeDtypeStruct((SMALL_ROWS, SMALL_COLS), F32))(
        *grads, d_w_alpha, loss_row)


def _small_update(gathered, dmod_all, sel, wmv, *, name):
    names = [n for n, _ in SMALL] + ["gla_w_alpha"]
    n_par = len(names)

    def body(sel_ref, g_ref, dmod_ref, *refs):
        in_refs, out_refs, loss_ref, acc = refs[:3 * n_par], refs[3 * n_par:-2], refs[-2], refs[-1]
        total = g_ref[0]
        for j in range(1, 8):
            total = total + g_ref[j]
        acc[...] = total
        loss_ref[...] = acc[LOSS_ROW:LOSS_ROW + 1, 0:LANE]
        row = {n: i for i, (n, _) in enumerate(SMALL_RED)}
        for p, name_p in enumerate(names):
            w_ref, m_ref, v_ref = in_refs[3 * p:3 * p + 3]
            if name_p == "b_ada":
                gv = jnp.sum(dmod_ref[...], axis=0, keepdims=True)
            elif name_p == "gla_w_alpha":
                gv = jnp.zeros((GLR, GDK), F32)
                for j in range(4):
                    blk = acc[W_ALPHA_ROW:W_ALPHA_ROW + GLR, j * GDK:(j + 1) * GDK]
                    gv = gv + jnp.where(sel_ref[0] == j, blk, 0.0)
            else:
                gv = acc[row[name_p]:row[name_p] + 1, 0:w_ref.shape[1]]
            o = out_refs[4 * p:4 * p + 4]
            o[0][...] = gv
            o[1][...], o[2][...], o[3][...] = _adamw_update(w_ref[...], gv, m_ref[...], v_ref[...])

    flat = [a for t in wmv for a in t]
    out_shape = [jax.ShapeDtypeStruct(t[0].shape, F32) for t in wmv for _ in range(4)]
    out_shape.append(jax.ShapeDtypeStruct((1, LANE), F32))
    vmem = pl.BlockSpec(memory_space=pltpu.VMEM)
    outs = pl.pallas_call(
        body, name=name, out_shape=out_shape, in_specs=[pl.BlockSpec(memory_space=pltpu.SMEM), vmem, vmem] + [vmem] * len(flat),
        out_specs=[vmem] * len(out_shape), scratch_shapes=[pltpu.VMEM((SMALL_ROWS, SMALL_COLS), F32)],
    )(sel, gathered, dmod_all, *flat)
    return {n: tuple(outs[4 * p:4 * p + 4]) for p, n in enumerate(names)}, outs[-1][0, 0]


def _full_weights(gathered):
    w = {name: jnp.concatenate(gathered[name], axis=axis) for name, _, axis in BIG if name in gathered and name != "w_in"}
    if "w_in" in gathered:
        shards = gathered["w_in"]
        zeros = lambda n: [jnp.zeros((D, n), shards[0].dtype)]

        def cols(a, b):
            width = IN_WIDTH // 4
            return [shards[j][:, max(a, j * width) - j * width:min(b, (j + 1) * width) - j * width]
                    for j in range(4) if max(a, j * width) < min(b, (j + 1) * width)]

        parts = []
        for a, b, at in sorted(W_IN_SEGMENTS, key=lambda seg: seg[2]):
            have = sum(p.shape[1] for p in parts)
            parts += (zeros(at - have) if at > have else []) + cols(a, b)
        w["w_in"] = jnp.concatenate(parts + zeros(PW - sum(p.shape[1] for p in parts)), axis=1)
    if "mla_w_uq" in w:
        w["mla_w_uq"] = jnp.pad(w["mla_w_uq"].reshape(MQR, MH, MQK), ((0, 0), (0, 0), (0, LANE - MQK))).reshape(MQR, MH * LANE)
    if "mla_w_o" in w:
        w["mla_w_o"] = jnp.pad(w["mla_w_o"].reshape(MH, MVD, D), ((0, 0), (0, LANE - MVD), (0, 0))).reshape(MH * LANE, D)
    return w


def _grad_slots(g):
    g = dict(g)
    out = {}
    if "w_in" in g:
        g_lo, g_hi = g.pop("w_in")
        take = lambda at, lo, hi: g_lo[:, at + lo:at + hi] if at < W_IN_SPLIT else g_hi[:, at - W_IN_SPLIT + lo:at - W_IN_SPLIT + hi]
        width = IN_WIDTH // 4
        slots = []
        for j in range(4):
            lo, hi = j * width, (j + 1) * width
            slots.append(jnp.concatenate([take(at, max(lo, a) - a, min(hi, b) - a)
                                          for a, b, at in W_IN_SEGMENTS if max(lo, a) < min(hi, b)], axis=1))
        out["w_in"] = jnp.stack(slots).reshape(4, 2, D // 2, width)
    if "mla_w_uq" in g:
        g["mla_w_uq"] = g["mla_w_uq"].reshape(MQR, MH, LANE)[:, :, :MQK].reshape(MQR, MH * MQK)
    if "mla_w_o" in g:
        g["mla_w_o"] = g["mla_w_o"].reshape(MH, LANE, D)[:, :MVD].reshape(MH * MVD, D)
    for name, (rows, cols), axis in BIG:
        if name not in g:
            continue
        a = g[name]
        a = a.reshape(4, rows, cols) if axis == 0 else jnp.transpose(a.reshape(rows, 4, cols), (1, 0, 2))
        out[name] = a.reshape(4, 2, rows // 2, cols)
    return out


def _rope_tables(positions):
    freqs = ROPE_THETA ** (-jnp.arange(0, MROPE, 2, dtype=F32) / MROPE)
    lane = np.arange(LANE)
    in_rope = (lane >= MNOPE) & (lane < MQK)
    freq_lane = jnp.where(in_rope, freqs[(lane - MNOPE) % (MROPE // 2)], 0.0)
    sign = np.where(in_rope, np.where(lane < MNOPE + MROPE // 2, -1.0, 1.0), 0.0).astype(np.float32)
    ang = positions.astype(F32).reshape(-1, 1) * freq_lane[None, :]
    return jnp.cos(ang), jnp.sin(ang) * sign[None, :]


def _local_step(x, positions, mod, target, w, small, more_weights=None, on_grads=None):
    kept = {}
    if on_grads is None:
        on_grads = lambda tag, grads, after: kept.update(grads)
    bsz, s, _ = x.shape
    t = bsz * s
    tt = _tile(t, 1024)
    shift1, scale1, gate1, shift2, scale2, gate2 = [mod[:, None, i * D:(i + 1) * D] for i in range(6)]
    cos_t, sin_t = _rope_tables(positions)
    w_alpha_p = jnp.pad(small["gla_w_alpha"], ((0, LANE - GLR), (0, 0)))
    gq = jnp.pad(small["mla_qn_g"], ((0, 0), (0, LANE - MQK)))
    gk = jnp.pad(small["mla_kn_g"], ((0, 0), (0, LANE - MQK)))
    flat2 = lambda a: a.reshape(t, a.shape[-1])
    bsd = lambda a: a.reshape(bsz, s, a.shape[-1])

    h = _norm_mod(x, small["norm1_g"], scale1, shift1, name="norm1")
    if callable(w):
        w = w(h)
    proj = _mm(flat2(h), w["w_in"], name="proj", tn=1152)
    proj3 = bsd(proj)
    o, o_gated, states = _gla_fwd(proj3, w_alpha_p, small["gla_b_alpha"], small["gla_out_norm_g"], name="gla_fwd")
    if more_weights is not None:
        w = {**w, **more_weights(o_gated)}
    y_a = _mm(flat2(o_gated), w["gla_w_o"], name="gla_out")
    cq_n, ckv_n = _lat_norm(proj, small["mla_q_lat_g"], small["mla_kv_lat_g"], name="lat_norm")
    q_raw = _mm(cq_n, w["mla_w_uq"], name="mla_uq")
    kv = _mm(ckv_n, w["mla_w_ukv"], name="mla_ukv")
    qf, kf, vf = _qk_prep(q_raw, kv, proj, cos_t, sin_t, gq * Q_PRESCALE, gk, name="qk_prep")
    o_attn = _attn_fwd(bsd(qf), bsd(kf), bsd(vf), name="attn_fwd")
    y_b = _mm(flat2(o_attn), w["mla_w_o"], name="mla_out")
    mixed_in = _merge_fwd(proj3, small["b_merge"], bsd(y_a), bsd(y_b), name="merge_fwd")
    mixed = _mm(flat2(mixed_in), w["w_out"], name="w_out")
    x1, h2 = _resid_norm_mod(x, bsd(mixed), gate1, small["norm2_g"], scale2, shift2, name="norm2")

    def sqrelu(acc, ex, outs):
        r = jnp.maximum(acc, 0.0)
        outs[0][...] = (r * r).astype(BF16)

    r = _mm(flat2(h2), w["mlp_w1"], name="mlp1", epilogue=sqrelu, out_shape=jax.ShapeDtypeStruct((t, DFF), BF16),
            out_specs=_tile_spec(tt, 1024))
    ff = _mm(r, w["mlp_w2"], name="mlp2")
    dy, dff, dgate2, loss_part = _loss_head(x1, bsd(ff), gate2, target, name="loss_head")

    g = {}

    def relu2_bwd(acc, ex, outs):
        outs[0][...] = (acc * (2.0 * jnp.sqrt(ex[0][...].astype(F32)))).astype(BF16)

    dff2 = flat2(dff)
    da1 = _mm(dff2, w["mlp_w2"], tb=True, name="mlp2_dx", epilogue=relu2_bwd, extras=(r,),
              extra_specs=(_tile_spec(tt, 1024),), out_shape=jax.ShapeDtypeStruct((t, DFF), BF16),
              out_specs=_tile_spec(tt, 1024))
    g["mlp_w2"] = _mm(r, dff2, ta=True, name="mlp2_dw")
    dh2 = _mm(da1, w["mlp_w1"], tb=True, name="mlp1_dx")
    g["mlp_w1"] = _mm(flat2(h2), da1, ta=True, name="mlp1_dw")
    token = on_grads("mlp", {n: g.pop(n) for n in ("mlp_w2", "mlp_w1")}, dh2)
    if token is not None:
        gate1 = gate1 + token[0, 0]
    dx1, dscale2, dshift2, dg2, dgate1, dmixed = _norm_mod_bwd(
        bsd(dh2), x1, dy, small["norm2_g"], scale2, gate1, bsd(mixed), name="norm2_bwd")
    dmixed2 = flat2(dmixed)
    dmi = _mm(dmixed2, w["w_out"], tb=True, name="w_out_dx")
    g["w_out"] = _mm(flat2(mixed_in), dmixed2, ta=True, name="w_out_dw")
    dy_a, dy_b, dl_a, dl_b, db_a, db_b = _merge_bwd(bsd(dmi), proj3, small["b_merge"], bsd(y_a), bsd(y_b), name="merge_bwd")
    dy_a2, dy_b2 = flat2(dy_a), flat2(dy_b)
    dog = _mm(dy_a2, w["gla_w_o"], tb=True, name="gla_out_dx")
    g["gla_w_o"] = _mm(flat2(o_gated), dy_a2, ta=True, name="gla_out_dw")
    dq_g, dk_g, dv_g, dg_g, dlog, db_alpha, d_ong = _gla_bwd(
        bsd(dog), o, states, proj3, w_alpha_p, small["gla_b_alpha"], small["gla_out_norm_g"], name="gla_bwd")
    dlog2 = flat2(dlog)
    da_p = _mm(dlog2, w_alpha_p, tb=True, out_dtype=BF16, name="alpha_dx")
    d_w_alpha = _mm(proj[:, OFF_A:OFF_A + LANE], dlog2, ta=True, name="alpha_dw")[:GLR]
    do_attn = _mm(dy_b2, w["mla_w_o"], tb=True, out_dtype=BF16, name="mla_out_dx")
    g["mla_w_o"] = _mm(flat2(o_attn), dy_b2, ta=True, name="mla_out_dw")
    dqf, dkf, dvf = _attn_bwd(bsd(qf), bsd(kf), bsd(vf), bsd(do_attn), name="attn_bwd")
    dq_raw, dkv, dkpe, dgq, dgk = _qk_prep_bwd(flat2(dqf), flat2(dkf), flat2(dvf), q_raw, kv, proj, cos_t, sin_t, gq, gk,
                                                name="qk_prep_bwd")
    dcq_n = _mm(dq_raw, w["mla_w_uq"], tb=True, name="mla_uq_dx")
    g["mla_w_uq"] = _mm(cq_n, dq_raw, ta=True, name="mla_uq_dw")
    dckv_n = _mm(dkv, w["mla_w_ukv"], tb=True, name="mla_ukv_dx")
    g["mla_w_ukv"] = _mm(ckv_n, dkv, ta=True, name="mla_ukv_dw")
    token = on_grads("mix", {n: g.pop(n) for n in ("w_out", "gla_w_o", "mla_w_o", "mla_w_uq", "mla_w_ukv")}, dckv_n)
    q_lat_g = small["mla_q_lat_g"] if token is None else small["mla_q_lat_g"] + token[0:1, 0:1]
    dcq, dckv, dg_qlat, dg_kvlat = _lat_norm_bwd(dcq_n, dckv_n, proj, q_lat_g, small["mla_kv_lat_g"],
                                                  name="lat_norm_bwd")
    pieces = [(flat2(dq_g), OFF_Q), (flat2(dk_g), OFF_K), (flat2(dv_g), OFF_V), (flat2(dg_g), OFF_G),
              (flat2(dl_a), OFF_MA), (flat2(dl_b), OFF_MB), (dcq, OFF_CQ), (dckv, OFF_CKV), (da_p, OFF_A), (dkpe, OFF_KPE)]
    hb = flat2(h)
    g_w_in = (_pieces_dw(hb, [p for p, off in pieces if off < W_IN_SPLIT], name="proj_dw_a"),
              _pieces_dw(hb, [p for p, off in pieces if off >= W_IN_SPLIT], name="proj_dw_b"))
    token = on_grads("in", {"w_in": g_w_in}, g_w_in[1])
    after = jnp.zeros((8, LANE), F32) if token is None else token
    dh = _pieces_dx(pieces, w["w_in"], after, name="proj_dx")
    token = on_grads("dx", {}, dh)
    if token is not None:
        scale1 = scale1 + token[0, 0]
    grad_x, dscale1, dshift1, dg1 = _norm_mod_bwd(bsd(dh), x, dx1, small["norm1_g"], scale1, name="norm1_bwd")

    dmod = jnp.concatenate([dshift1, dscale1, dgate1, dshift2, dscale2, dgate2], axis=-1).reshape(bsz, 6 * D)
    gs = {"norm1_g": dg1, "b_merge": jnp.concatenate([db_a, db_b], axis=1), "gla_b_alpha": db_alpha,
          "gla_out_norm_g": d_ong, "mla_q_lat_g": dg_qlat, "mla_kv_lat_g": dg_kvlat, "mla_qn_g": dgq[:, :MQK],
          "mla_kn_g": dgk[:, :MQK], "norm2_g": dg2}
    return loss_part[0, 0], grad_x, dmod, {**kept, **g}, gs, d_w_alpha


def kernel(x, c, positions, w_ada, b_ada, norm1_g, w_in, b_merge, gla_w_alpha, gla_b_alpha, gla_out_norm_g, gla_w_o, mla_q_lat_g, mla_w_uq, mla_kv_lat_g, mla_w_ukv, mla_qn_g, mla_kn_g, mla_w_o, w_out, norm2_g, mlp_w1, mlp_w2, loss_target, m_w_ada, m_b_ada, m_norm1_g, m_w_in, m_b_merge, m_gla_w_alpha, m_gla_b_alpha, m_gla_out_norm_g, m_gla_w_o, m_mla_q_lat_g, m_mla_w_uq, m_mla_kv_lat_g, m_mla_w_ukv, m_mla_qn_g, m_mla_kn_g, m_mla_w_o, m_w_out, m_norm2_g, m_mlp_w1, m_mlp_w2, v_w_ada, v_b_ada, v_norm1_g, v_w_in, v_b_merge, v_gla_w_alpha, v_gla_b_alpha, v_gla_out_norm_g, v_gla_w_o, v_mla_q_lat_g, v_mla_w_uq, v_mla_kv_lat_g, v_mla_w_ukv, v_mla_qn_g, v_mla_kn_g, v_mla_w_o, v_w_out, v_norm2_g, v_mlp_w1, v_mlp_w2):
    args = dict(locals())
    names_big = [n for n, _, _ in BIG]
    names_small = [n for n, _ in SMALL]
    bsz = x.shape[0]
    ax, ay, ac = lax.axis_index("x"), lax.axis_index("y"), lax.axis_index("c")
    chip = 2 * ax + ay
    dev = 2 * chip + ac

    small = {n: args[n] for n in names_small}
    sel_c = jnp.reshape(ac, (1,)).astype(jnp.int32)
    sel_chip = jnp.reshape(chip, (1,)).astype(jnp.int32)
    c_all, w_alpha_all = _all_gather8([c, gla_w_alpha[0]], name="comm_c_alpha")
    small["gla_w_alpha"] = jnp.concatenate([w_alpha_all[2 * j] for j in range(4)], axis=1)
    c_all = c_all.reshape(8 * bsz, D)

    shards = {n: args[n][0].astype(BF16) for n in names_big}
    halves_of = lambda names: [shards[n].reshape(2, shards[n].shape[0] // 2, shards[n].shape[1]) for n in names]

    def gather_start(names, deps, tag):
        xs = halves_of(names)
        lands = [lax.empty((4, *xh.shape), BF16) for xh in xs]
        plan = _gather_plan(len(names))
        return names, plan, _rdma_start(xs + lands, 3 * len(names), plan, deps, name="comm_weights_start_" + tag)

    def gather_finish(started, after, tag):
        names, plan, sems = started
        arrs = _rdma_wait(sems[0], sems[1], sems[2], plan, after, name="comm_weights_wait_" + tag)
        filled = _pair_fill(arrs[len(names):], name="comm_weights_pair_" + tag)
        own = [a.reshape(shards[n].shape) for n, a in zip(names, arrs)]
        return _full_weights(dict(zip(names, _own_and_landed(filled, own))))


    def add_bias(acc, ex, outs):
        outs[0][...] = acc + ex[0][...]

    silu = lambda v: v * _sigmoid(v)
    b_ada_mine = lax.dynamic_slice(b_ada, (0, chip * ADA_SHARD[1]), (1, ADA_SHARD[1]))
    mod_part = _mm(c_all, w_ada[0], name="ada", tn=512, a_fn=silu, epilogue=add_bias, extras=(b_ada_mine,),
                   extra_specs=(pl.BlockSpec((1, 512), lambda i, j, k: (0, j)),),
                   out_shape=jax.ShapeDtypeStruct((8 * bsz, ADA_SHARD[1]), F32), out_specs=_tile_spec(8 * bsz, 512))
    mod_all = _all_gather8([mod_part], name="comm_mod")[0]
    mod_rows = lax.dynamic_slice(mod_all, (0, dev * bsz, 0), (8, bsz, ADA_SHARD[1]))
    mod = jnp.concatenate([mod_rows[2 * j] for j in range(4)], axis=1)
    first = gather_start(["w_in"], (mod,), "in")
    rest = gather_start([n for n in names_big if n != "w_in"], (mod, first[2][3]), "rest")
    mod = mod + rest[2][3][0, 0]
    w_in_after = lambda after: gather_finish(first, after, "in")
    more_weights = lambda after: gather_finish(rest, after, "rest")

    stage = {}

    def begin(tag, names, arrays, lands, n_copies, plan, what):
        stage[tag] = (names, plan, _rdma_start(arrays + lands, n_copies, plan, (), name=f"comm_{what}_start_{tag}"))
        return stage[tag][2][3]

    def landed(tag, after, what):
        names, plan, sems = stage[tag]
        arrs = _rdma_wait(sems[0], sems[1], sems[2], plan, after, name=f"comm_{what}_wait_{tag}")
        return names, arrs[:len(arrs) // 2], arrs[len(arrs) // 2:]

    def swap_start(tag, grads):
        names = list(grads)
        parts = [_grad_slots(grads)[n] for n in names]
        lands = [lax.empty((4, *p.shape[2:]), F32) for p in parts]
        return begin(tag, names, parts, lands, len(names), _sibling_plan(len(names), lambda r, c: r.at[:, 1 - c]), "pair_sum")

    def scatter_start(tag, after):
        names, parts, sib_halves = landed(tag, after, "pair_sum")
        pairs = [_pair_add(p, s, sel_c, name="pair_add_" + n) for n, p, s in zip(names, parts, sib_halves)]
        recvs = [lax.empty((3, *p.shape[1:]), BF16) for p in pairs]
        return begin(tag, names, pairs, recvs, 3 * len(names), _scatter_plan(len(names)), "scatter")

    def join_start(tag, after):
        names, pairs, recvs = landed(tag, after, "scatter")
        halves = [_chip_sum(p, r, sel_chip, name="chip_sum_" + n) for n, p, r in zip(names, pairs, recvs)]
        lands = [lax.empty(h.shape, F32) for h in halves]
        return begin(tag, names, halves, lands, len(names), _sibling_plan(len(names), lambda r, c: r), "pair_join")

    def reduce_step(tag, grads, after):
        if tag == "mlp":
            return swap_start("mlp", grads)
        if tag == "mix":
            return scatter_start("mlp", after) + swap_start("mix", grads)
        if tag == "in":
            return scatter_start("mix", after) + swap_start("in", grads)
        return scatter_start("in", after)

    loss_part, grad_x, dmod, g, gs, d_w_alpha = _local_step(x, positions, mod, loss_target, w_in_after, small,
                                                            more_weights, reduce_step)

    assert not g, list(g)
    gs_packed = _pack_small([gs[n] for n, _ in SMALL_RED], d_w_alpha, jnp.full((1, LANE), loss_part, F32),
                            name="pack_small")
    small_lands = [lax.empty((8, *a.shape), F32) for a in (dmod, gs_packed)]
    begin("small", ["dmod", "small"], [dmod, gs_packed], small_lands, 14, _gather8_plan(2), "gather8")

    res = {}

    def finish(tag, after):
        names, halves, theirs = landed(tag, after, "pair_join")
        for n, mine, other in zip(names, halves, theirs):
            if n == "w_in":
                south = ac == 0
                g_t = jnp.concatenate([jnp.where(south, mine, other), jnp.where(south, other, mine)], axis=0).T
                outs = _adamw(w_in[0].T, g_t, m_w_in[0].T, v_w_in[0].T, name="adamw_w_in", by_cols=True)
                res[n] = tuple(a.T for a in (g_t, *outs))
            else:
                res[n] = _adamw_halves(args[n][0], args["m_" + n][0], args["v_" + n][0], mine, other, sel_c,
                                       name="adamw_" + n)
        return res[names[-1]][1]

    join_start("mlp", grad_x)
    join_start("mix", grad_x)
    done = finish("mix", finish("mlp", grad_x))

    _, (dmod_own, gs_own), (dmod_all, gs_all) = landed("small", done, "gather8")
    dmod_all = lax.dynamic_update_slice(dmod_all, dmod_own[None], (dev, 0, 0)).reshape(8 * bsz, 6 * D)
    gs_all = lax.dynamic_update_slice(gs_all, gs_own[None], (dev, 0, 0))
    dmod_mine = lax.dynamic_slice(dmod_all, (0, chip * ADA_SHARD[1]), (8 * bsz, ADA_SHARD[1]))
    g_w_ada = _mm(c_all, dmod_mine, ta=True, a_fn=silu, name="ada_dw")
    wmv = [(args[n], args["m_" + n], args["v_" + n]) for n in names_small]
    wmv.append((gla_w_alpha[0], m_gla_w_alpha[0], v_gla_w_alpha[0]))
    res_small, loss_sum = _small_update(gs_all, dmod_all, sel_chip, wmv, name="small_update")
    res.update(res_small)
    loss = loss_sum * (0.5 / D)
    join_start("in", g_w_ada)
    res["w_ada"] = (g_w_ada, *_adamw(w_ada[0], g_w_ada, m_w_ada[0], v_w_ada[0], name="adamw_w_ada"))
    finish("in", res["w_ada"][1])

    order = ["w_ada", "b_ada", "norm1_g", "w_in", "b_merge", "gla_w_alpha", "gla_b_alpha", "gla_out_norm_g", "gla_w_o",
             "mla_q_lat_g", "mla_w_uq", "mla_kv_lat_g", "mla_w_ukv", "mla_qn_g", "mla_kn_g", "mla_w_o", "w_out",
             "norm2_g", "mlp_w1", "mlp_w2"]
    named = lambda k: [res[n][k].reshape(args[n].shape) for n in order]
    return (loss, grad_x, *named(0), *named(1), *named(2), *named(3))
```

```python
import jax
import jax.numpy as jnp
import numpy as np
from jax import lax
from jax.experimental import pallas as pl
from jax.experimental.pallas import tpu as pltpu

F32 = jnp.float32
BF16 = jnp.bfloat16
MESH = pl.DeviceIdType.MESH

D = 1024
CHUNK = 64
EPS = 1e-6
GH, GDK, GDV, GLR, GTAU = 4, 128, 256, 16, 16.0
MH, MQR, MKVR, MNOPE, MROPE, MVD = 16, 256, 128, 64, 32, 64
MQK = MNOPE + MROPE
DFF = 4 * D
ROPE_THETA = 10000.0
IN_WIDTH = 5552
LANE = 128
OFF_Q, OFF_K, OFF_V, OFF_G, OFF_MA, OFF_MB, OFF_CQ, OFF_CKV, OFF_A, OFF_KPE, PW = (
    0, 512, 1024, 2048, 3072, 4096, 5120, 5376, 5504, 5632, 5760)
ADAM_LR, ADAM_B1, ADAM_B2, ADAM_EPS, ADAM_WD, ADAM_STEP = 0.001, 0.9, 0.999, 1e-08, 0.01, 10
VMEM_LIMIT = 48 * 1024 * 1024


def _params(n_axes):
    return pltpu.CompilerParams(dimension_semantics=("arbitrary",) * n_axes, vmem_limit_bytes=VMEM_LIMIT)


def _tile(n, target):
    if n <= target:
        return n
    best = None
    for t in range(LANE, target + 1, LANE):
        if n % t == 0:
            best = t
    assert best is not None, (n, target)
    return best


def _sigmoid(x):
    return 1.0 / (1.0 + jnp.exp(-x))


def _mm(a, b, *, name, ta=False, tb=False, out_dtype=F32, tm=1024, tn=1024, tk=2048,
        epilogue=None, extras=(), extra_specs=(), out_shape=None, out_specs=None, a_fn=None):
    if ta:
        kdim, m = a.shape
    else:
        m, kdim = a.shape
    if tb:
        n, k2 = b.shape
    else:
        k2, n = b.shape
    assert kdim == k2, (a.shape, b.shape)
    tm, tn, tk = _tile(m, tm), _tile(n, tn), _tile(kdim, tk)
    nk = kdim // tk
    a_spec = pl.BlockSpec((tk, tm), lambda i, j, k: (k, i)) if ta else pl.BlockSpec((tm, tk), lambda i, j, k: (i, k))
    b_spec = pl.BlockSpec((tn, tk), lambda i, j, k: (j, k)) if tb else pl.BlockSpec((tk, tn), lambda i, j, k: (k, j))
    dims = (((0 if ta else 1,), (1 if tb else 0,)), ((), ()))
    ne = len(extras)
    if out_shape is None:
        out_shape = jax.ShapeDtypeStruct((m, n), out_dtype)
        out_specs = pl.BlockSpec((tm, tn), lambda i, j, k: (i, j))
    n_out = len(out_shape) if isinstance(out_shape, (list, tuple)) else 1
    in_place = epilogue is None and n_out == 1 and out_shape.dtype == F32
    scratch = [] if (nk == 1 or in_place) else [pltpu.VMEM((tm, tn), F32)]

    def body(a_ref, b_ref, *rest):
        ex, outs = rest[:ne], rest[ne:ne + n_out]
        av = a_ref[...] if a_fn is None else a_fn(a_ref[...])
        prod = lax.dot_general(av.astype(BF16), b_ref[...].astype(BF16), dims, preferred_element_type=F32)

        def finish(val):
            if epilogue is None:
                outs[0][...] = val.astype(outs[0].dtype)
            else:
                epilogue(val, ex, outs)

        if nk == 1:
            finish(prod)
            return
        k = pl.program_id(2)
        acc = outs[0] if in_place else rest[-1]

        @pl.when(k == 0)
        def _():
            acc[...] = prod

        @pl.when(k > 0)
        def _():
            acc[...] += prod

        if not in_place:
            @pl.when(k == nk - 1)
            def _():
                finish(acc[...])

    return pl.pallas_call(
        body, name=name, grid=(m // tm, n // tn, nk),
        in_specs=[a_spec, b_spec, *extra_specs], out_specs=out_specs, out_shape=out_shape,
        scratch_shapes=scratch, compiler_params=_params(3),
    )(a, b, *extras)


def _tile_spec(tm, tn):
    return pl.BlockSpec((tm, tn), lambda i, j, k: (i, j))


def _pieces_dx(pieces, w, after, *, name, tm=256):
    t = pieces[0][0].shape[0]
    tm = _tile(t, tm)
    npc = len(pieces)

    def body(*refs):
        p_refs, w_ref, out_ref = refs[:npc], refs[npc], refs[-1]
        acc = None
        for (arr, off), p_ref in zip(pieces, p_refs):
            part = lax.dot_general(p_ref[...].astype(BF16), w_ref[:, off:off + arr.shape[1]], _NT,
                                   preferred_element_type=F32)
            acc = part if acc is None else acc + part
        out_ref[...] = acc

    return pl.pallas_call(
        body, name=name, grid=(t // tm,),
        in_specs=[pl.BlockSpec((tm, arr.shape[1]), lambda i: (i, 0)) for arr, _ in pieces]
        + [pl.BlockSpec(w.shape, lambda i: (0, 0)), pl.BlockSpec((8, LANE), lambda i: (0, 0))],
        out_specs=pl.BlockSpec((tm, w.shape[0]), lambda i: (i, 0)),
        out_shape=jax.ShapeDtypeStruct((t, w.shape[0]), F32), compiler_params=_params(1),
    )(*[arr for arr, _ in pieces], w, after)


def _pieces_dw(h, pieces, *, name, tk=1024):
    t, d = h.shape
    tk = _tile(t, tk)
    widths = [p.shape[1] for p in pieces]
    starts = [sum(widths[:i]) for i in range(len(pieces))]

    def body(h_ref, *refs):
        p_refs, out_ref = refs[:-1], refs[-1]
        first = pl.program_id(0) == 0
        hv = h_ref[...]
        for p_ref, start, width in zip(p_refs, starts, widths):
            part = lax.dot_general(hv, p_ref[...].astype(BF16), _TN, preferred_element_type=F32)
            cols = slice(start, start + width)

            @pl.when(first)
            def _():
                out_ref[:, cols] = part

            @pl.when(jnp.logical_not(first))
            def _():
                out_ref[:, cols] += part

    return pl.pallas_call(
        body, name=name, grid=(t // tk,),
        in_specs=[pl.BlockSpec((tk, d), lambda k: (k, 0))] + [pl.BlockSpec((tk, wd), lambda k: (k, 0)) for wd in widths],
        out_specs=pl.BlockSpec((d, sum(widths)), lambda k: (0, 0)),
        out_shape=jax.ShapeDtypeStruct((d, sum(widths)), F32), compiler_params=_params(1),
    )(h, *pieces)


def _rms(x, g):
    r = lax.rsqrt(jnp.mean(x * x, axis=-1, keepdims=True) + EPS)
    return x * r, r


def _row_spec(ts, width, col=0):
    return pl.BlockSpec((None, ts, width), lambda b, i: (b, i, col))


def _vec_spec(width):
    return pl.BlockSpec((None, 1, width), lambda b, i: (b, 0, 0))


def _gain_spec(width):
    return pl.BlockSpec((1, width), lambda b, i: (0, 0))


def _norm_mod(x, g, scale, shift, *, name, ts=256):
    bsz, s, d = x.shape
    ts = min(ts, s)

    def body(x_ref, g_ref, sc_ref, sh_ref, h_ref):
        xh, _ = _rms(x_ref[...], None)
        h_ref[...] = ((xh * g_ref[...]) * (1.0 + sc_ref[...]) + sh_ref[...]).astype(BF16)

    return pl.pallas_call(
        body, name=name, grid=(bsz, s // ts),
        in_specs=[_row_spec(ts, d), _gain_spec(d), _vec_spec(d), _vec_spec(d)],
        out_specs=_row_spec(ts, d), out_shape=jax.ShapeDtypeStruct((bsz, s, d), BF16),
        compiler_params=_params(2),
    )(x, g, scale, shift)


def _resid_norm_mod(x, mixed, gate, g, scale, shift, *, name, ts=256):
    bsz, s, d = x.shape
    ts = min(ts, s)

    def body(x_ref, mx_ref, gt_ref, g_ref, sc_ref, sh_ref, x1_ref, h_ref):
        x1 = x_ref[...] + gt_ref[...] * mx_ref[...]
        x1_ref[...] = x1
        xh, _ = _rms(x1, None)
        h_ref[...] = ((xh * g_ref[...]) * (1.0 + sc_ref[...]) + sh_ref[...]).astype(BF16)

    return pl.pallas_call(
        body, name=name, grid=(bsz, s // ts),
        in_specs=[_row_spec(ts, d), _row_spec(ts, d), _vec_spec(d), _gain_spec(d), _vec_spec(d), _vec_spec(d)],
        out_specs=[_row_spec(ts, d), _row_spec(ts, d)],
        out_shape=[jax.ShapeDtypeStruct((bsz, s, d), F32), jax.ShapeDtypeStruct((bsz, s, d), BF16)],
        compiler_params=_params(2),
    )(x, mixed, gate, g, scale, shift)


def _norm_mod_bwd(dh, xin, resid, g, scale, gate=None, mixed=None, *, name, ts=256):
    bsz, s, d = xin.shape
    ts = min(ts, s)
    gated = gate is not None

    def body(*refs):
        if gated:
            dh_ref, x_ref, rs_ref, g_ref, sc_ref, gt_ref, mx_ref, dx_ref, dsc_ref, dsh_ref, dg_ref, dgt_ref, dmx_ref = refs
        else:
            dh_ref, x_ref, rs_ref, g_ref, sc_ref, dx_ref, dsc_ref, dsh_ref, dg_ref = refs
        b, i = pl.program_id(0), pl.program_id(1)

        @pl.when(i == 0)
        def _():
            dsc_ref[...] = jnp.zeros_like(dsc_ref)
            dsh_ref[...] = jnp.zeros_like(dsh_ref)
            if gated:
                dgt_ref[...] = jnp.zeros_like(dgt_ref)

        @pl.when((i == 0) & (b == 0))
        def _():
            dg_ref[...] = jnp.zeros_like(dg_ref)

        dh_v, gv = dh_ref[...], g_ref[...]
        xh, r = _rms(x_ref[...], None)
        dsc_ref[...] += jnp.sum(dh_v * (xh * gv), axis=0, keepdims=True)
        dsh_ref[...] += jnp.sum(dh_v, axis=0, keepdims=True)
        dn = dh_v * (1.0 + sc_ref[...])
        dg_ref[...] += jnp.sum(dn * xh, axis=0, keepdims=True)
        dxh = dn * gv
        dx = rs_ref[...] + r * (dxh - xh * jnp.mean(dxh * xh, axis=-1, keepdims=True))
        dx_ref[...] = dx
        if gated:
            dgt_ref[...] += jnp.sum(dx * mx_ref[...], axis=0, keepdims=True)
            dmx_ref[...] = (dx * gt_ref[...]).astype(BF16)

    ins = [dh, xin, resid, g, scale]
    in_specs = [_row_spec(ts, d), _row_spec(ts, d), _row_spec(ts, d), _gain_spec(d), _vec_spec(d)]
    out_specs = [_row_spec(ts, d), _vec_spec(d), _vec_spec(d), _gain_spec(d)]
    out_shape = [jax.ShapeDtypeStruct((bsz, s, d), F32), jax.ShapeDtypeStruct((bsz, 1, d), F32),
                 jax.ShapeDtypeStruct((bsz, 1, d), F32), jax.ShapeDtypeStruct((1, d), F32)]
    if gated:
        ins += [gate, mixed]
        in_specs += [_vec_spec(d), _row_spec(ts, d)]
        out_specs += [_vec_spec(d), _row_spec(ts, d)]
        out_shape += [jax.ShapeDtypeStruct((bsz, 1, d), F32), jax.ShapeDtypeStruct((bsz, s, d), BF16)]
    return pl.pallas_call(
        body, name=name, grid=(bsz, s // ts), in_specs=in_specs, out_specs=out_specs, out_shape=out_shape,
        compiler_params=_params(2),
    )(*ins)


def _loss_head(x1, ff, gate2, target, *, name, ts=256):
    bsz, s, d = x1.shape
    ts = min(ts, s)

    def body(x1_ref, ff_ref, gt_ref, t_ref, dy_ref, dff_ref, dgt_ref, loss_ref, acc):
        b, i = pl.program_id(0), pl.program_id(1)

        @pl.when(i == 0)
        def _():
            dgt_ref[...] = jnp.zeros_like(dgt_ref)

        @pl.when((i == 0) & (b == 0))
        def _():
            acc[...] = jnp.zeros_like(acc)

        ffv, gt = ff_ref[...], gt_ref[...]
        diff = (x1_ref[...] + gt * ffv) - t_ref[...]
        acc[...] += jnp.sum((diff * diff).reshape(ts // 8, 8, d), axis=0)
        dy = diff * (1.0 / d)
        dy_ref[...] = dy
        dgt_ref[...] += jnp.sum(dy * ffv, axis=0, keepdims=True)
        dff_ref[...] = (dy * gt).astype(BF16)

        @pl.when((i == pl.num_programs(1) - 1) & (b == pl.num_programs(0) - 1))
        def _():
            loss_ref[...] = jnp.full(loss_ref.shape, jnp.sum(acc[...]), F32)

    return pl.pallas_call(
        body, name=name, grid=(bsz, s // ts),
        in_specs=[_row_spec(ts, d), _row_spec(ts, d), _vec_spec(d), _row_spec(ts, d)],
        out_specs=[_row_spec(ts, d), _row_spec(ts, d), _vec_spec(d), pl.BlockSpec((8, LANE), lambda b, i: (0, 0))],
        out_shape=[jax.ShapeDtypeStruct((bsz, s, d), F32), jax.ShapeDtypeStruct((bsz, s, d), BF16),
                   jax.ShapeDtypeStruct((bsz, 1, d), F32), jax.ShapeDtypeStruct((8, LANE), F32)],
        scratch_shapes=[pltpu.VMEM((8, d), F32)], compiler_params=_params(2),
    )(x1, ff, gate2, target)


def _merge_fwd(proj, b_merge, y_a, y_b, *, name, ts=256):
    bsz, s, _ = proj.shape
    ts = min(ts, s)

    def body(la_ref, lb_ref, ba_ref, bb_ref, ya_ref, yb_ref, out_ref):
        ga = _sigmoid(la_ref[...] + ba_ref[...])
        gb = _sigmoid(lb_ref[...] + bb_ref[...])
        out_ref[...] = (ga * ya_ref[...] + gb * yb_ref[...]).astype(BF16)

    return pl.pallas_call(
        body, name=name, grid=(bsz, s // ts),
        in_specs=[_row_spec(ts, D, OFF_MA // D), _row_spec(ts, D, OFF_MB // D),
                  pl.BlockSpec((1, D), lambda b, i: (0, 0)), pl.BlockSpec((1, D), lambda b, i: (0, 1)),
                  _row_spec(ts, D), _row_spec(ts, D)],
        out_specs=_row_spec(ts, D), out_shape=jax.ShapeDtypeStruct((bsz, s, D), BF16),
        compiler_params=_params(2),
    )(proj, proj, b_merge, b_merge, y_a, y_b)


def _merge_bwd(dmi, proj, b_merge, y_a, y_b, *, name, ts=256):
    bsz, s, _ = proj.shape
    ts = min(ts, s)

    def body(d_ref, la_ref, lb_ref, ba_ref, bb_ref, ya_ref, yb_ref, dya_ref, dyb_ref, dla_ref, dlb_ref, dba_ref, dbb_ref):
        @pl.when((pl.program_id(0) == 0) & (pl.program_id(1) == 0))
        def _():
            dba_ref[...] = jnp.zeros_like(dba_ref)
            dbb_ref[...] = jnp.zeros_like(dbb_ref)

        dv = d_ref[...]
        ga = _sigmoid(la_ref[...] + ba_ref[...])
        gb = _sigmoid(lb_ref[...] + bb_ref[...])
        dya_ref[...] = (dv * ga).astype(BF16)
        dyb_ref[...] = (dv * gb).astype(BF16)
        dla = (dv * ya_ref[...]) * (ga * (1.0 - ga))
        dlb = (dv * yb_ref[...]) * (gb * (1.0 - gb))
        dla_ref[...] = dla.astype(BF16)
        dlb_ref[...] = dlb.astype(BF16)
        dba_ref[...] += jnp.sum(dla, axis=0, keepdims=True)
        dbb_ref[...] += jnp.sum(dlb, axis=0, keepdims=True)

    act = jax.ShapeDtypeStruct((bsz, s, D), BF16)
    return pl.pallas_call(
        body, name=name, grid=(bsz, s // ts),
        in_specs=[_row_spec(ts, D), _row_spec(ts, D, OFF_MA // D), _row_spec(ts, D, OFF_MB // D),
                  pl.BlockSpec((1, D), lambda b, i: (0, 0)), pl.BlockSpec((1, D), lambda b, i: (0, 1)),
                  _row_spec(ts, D), _row_spec(ts, D)],
        out_specs=[_row_spec(ts, D)] * 4 + [_gain_spec(D)] * 2,
        out_shape=[act, act, act, act, jax.ShapeDtypeStruct((1, D), F32), jax.ShapeDtypeStruct((1, D), F32)],
        compiler_params=_params(2),
    )(dmi, proj, proj, b_merge, b_merge, y_a, y_b)


def _tri(lower):
    r = lax.broadcasted_iota(jnp.int32, (CHUNK, CHUNK), 0)
    c = lax.broadcasted_iota(jnp.int32, (CHUNK, CHUNK), 1)
    return jnp.where((c <= r) if lower else (c >= r), 1.0, 0.0).astype(F32)


def _gla_logits(a_ref, wal_ref, bal_ref):
    logits = jnp.dot(a_ref[...].astype(BF16), wal_ref[...].astype(BF16), preferred_element_type=F32) + bal_ref[...]
    la = (jnp.minimum(logits, 0.0) - jnp.log(1.0 + jnp.exp(-jnp.abs(logits)))) * (1.0 / GTAU)
    return logits, la


def _chunk_cumsum(la_n, tri):
    cum = jnp.dot(tri, la_n, preferred_element_type=F32, precision=lax.Precision.HIGHEST)
    return cum, jnp.sum(la_n, axis=0, keepdims=True)


def _gla_specs(s, nc):
    def blk(width, off):
        return pl.BlockSpec((None, s, width), lambda h, b: (b, 0, off // width + h))

    proj_specs = [blk(GDK, OFF_Q), blk(GDK, OFF_K), blk(GDV, OFF_V), blk(GDV, OFF_G),
                  pl.BlockSpec((None, s, LANE), lambda h, b: (b, 0, OFF_A // LANE)),
                  pl.BlockSpec((LANE, GDK), lambda h, b: (0, h)), pl.BlockSpec((1, GDK), lambda h, b: (0, h)),
                  pl.BlockSpec((1, GDV), lambda h, b: (0, 0))]
    st_spec = pl.BlockSpec((None, None, nc, GDV, GDK), lambda h, b: (b, h, 0, 0, 0))
    return blk, proj_specs, st_spec


def _gla_fwd(proj, w_alpha_p, b_alpha, out_norm_g, *, name):
    bsz, s, _ = proj.shape
    nc = s // CHUNK
    scale = GDK ** -0.5

    rb = min(512, s)

    def body(q_ref, k_ref, v_ref, g_ref, a_ref, wal_ref, bal_ref, ong_ref, o_ref, og_ref, st_ref):
        _, la = _gla_logits(a_ref, wal_ref, bal_ref)
        tri = _tri(True)
        st = jnp.zeros((GDV, GDK), F32)
        for n in range(nc):
            rows = pl.ds(n * CHUNK, CHUNK)
            cum, cum_end = _chunk_cumsum(la[n * CHUNK:(n + 1) * CHUNK], tri)
            kd = k_ref[rows, :] * jnp.exp(cum_end - cum)
            ut = lax.dot_general(v_ref[rows, :].astype(BF16), kd.astype(BF16), _TN, preferred_element_type=F32)
            st = st * jnp.exp(cum_end) + ut
            st_ref[n] = st
            o_ref[rows, :] = lax.dot_general((q_ref[rows, :] * scale).astype(BF16), st.astype(BF16), _NT,
                                             preferred_element_type=F32)
        for j in range(0, s, rb):
            blk_rows = pl.ds(j, rb)
            oh, _ = _rms(o_ref[blk_rows, :], None)
            gv = g_ref[blk_rows, :]
            og_ref[blk_rows, :] = ((oh * ong_ref[...]) * (gv * _sigmoid(gv))).astype(BF16)

    blk, proj_specs, st_spec = _gla_specs(s, nc)
    return pl.pallas_call(
        body, name=name, grid=(GH, bsz), in_specs=proj_specs, out_specs=[blk(GDV, 0), blk(GDV, 0), st_spec],
        out_shape=[jax.ShapeDtypeStruct((bsz, s, GH * GDV), F32), jax.ShapeDtypeStruct((bsz, s, GH * GDV), BF16),
                   jax.ShapeDtypeStruct((bsz, GH, nc, GDV, GDK), F32)],
        compiler_params=_params(2),
    )(proj, proj, proj, proj, proj, w_alpha_p, b_alpha, out_norm_g)


def _gla_bwd(dog, o, states, proj, w_alpha_p, b_alpha, out_norm_g, *, name):
    bsz, s, _ = proj.shape
    nc = s // CHUNK
    scale = GDK ** -0.5

    def body(dog_ref, o_ref, st_ref, q_ref, k_ref, v_ref, g_ref, a_ref, wal_ref, bal_ref, ong_ref,
             dq_ref, dk_ref, dv_ref, dg_ref, dl_ref, dbal_ref, dong_ref, do_scr, dlog_scr):
        h, b = pl.program_id(0), pl.program_id(1)

        @pl.when(b == 0)
        def _():
            dbal_ref[...] = jnp.zeros_like(dbal_ref)

        @pl.when((b == 0) & (h == 0))
        def _():
            dong_ref[...] = jnp.zeros_like(dong_ref)

        ong = ong_ref[...]
        for j in range(0, s, rb):
            blk_rows = pl.ds(j, rb)
            gv, dogv = g_ref[blk_rows, :], dog_ref[blk_rows, :]
            sg = _sigmoid(gv)
            oh, r = _rms(o_ref[blk_rows, :], None)
            don = dogv * (gv * sg)
            dg_ref[blk_rows, :] = (dogv * (oh * ong) * (sg * (1.0 + gv * (1.0 - sg)))).astype(BF16)
            dong_ref[...] += jnp.sum(don * oh, axis=0, keepdims=True)
            doh = don * ong
            do_scr[blk_rows, :] = (r * (doh - oh * jnp.mean(doh * oh, axis=-1, keepdims=True))).astype(BF16)

        logits, la = _gla_logits(a_ref, wal_ref, bal_ref)
        tri_lo, tri_up = _tri(True), _tri(False)
        carry = jnp.zeros((GDV, GDK), F32)
        for n in range(nc - 1, -1, -1):
            rows = pl.ds(n * CHUNK, CHUNK)
            cum, cum_end = _chunk_cumsum(la[n * CHUNK:(n + 1) * CHUNK], tri_lo)
            decay = jnp.exp(cum_end)
            w = jnp.exp(cum_end - cum)
            kd = k_ref[rows, :] * w
            do_b = do_scr[rows, :]
            qs_b = (q_ref[rows, :] * scale).astype(BF16)
            dq_ref[rows, :] = (jnp.dot(do_b, st_ref[n].astype(BF16), preferred_element_type=F32) * scale).astype(BF16)
            dsn = lax.dot_general(do_b, qs_b, _TN, preferred_element_type=F32) + carry
            carry = dsn * decay
            dsn_b = dsn.astype(BF16)
            dv_ref[rows, :] = lax.dot_general(kd.astype(BF16), dsn_b, _NT, preferred_element_type=F32).astype(BF16)
            dkd = jnp.dot(v_ref[rows, :].astype(BF16), dsn_b, preferred_element_type=F32)
            dk_ref[rows, :] = (dkd * w).astype(BF16)
            e = dkd * kd
            dcum_end = jnp.sum(e, axis=0, keepdims=True)
            if n > 0:
                dcum_end += jnp.sum(dsn * st_ref[n - 1], axis=0, keepdims=True) * decay
            dlog_scr[rows, :] = dcum_end - jnp.dot(tri_up, e, preferred_element_type=F32,
                                                  precision=lax.Precision.HIGHEST)
        dlog = dlog_scr[...] * (1.0 / GTAU) * (1.0 - _sigmoid(logits))
        dl_ref[...] = dlog.astype(BF16)
        dbal_ref[...] += jnp.sum(dlog, axis=0, keepdims=True)

    rb = min(512, s)

    blk, proj_specs, st_spec = _gla_specs(s, nc)
    act = lambda wd: jax.ShapeDtypeStruct((bsz, s, wd), BF16)
    return pl.pallas_call(
        body, name=name, grid=(GH, bsz), in_specs=[blk(GDV, 0), blk(GDV, 0), st_spec, *proj_specs],
        out_specs=[blk(GDK, 0), blk(GDK, 0), blk(GDV, 0), blk(GDV, 0), blk(GDK, 0),
                   pl.BlockSpec((1, GDK), lambda h, b: (0, h)), pl.BlockSpec((1, GDV), lambda h, b: (0, 0))],
        out_shape=[act(GH * GDK), act(GH * GDK), act(GH * GDV), act(GH * GDV), act(GH * GDK),
                   jax.ShapeDtypeStruct((1, GH * GDK), F32), jax.ShapeDtypeStruct((1, GDV), F32)],
        scratch_shapes=[pltpu.VMEM((s, GDV), BF16), pltpu.VMEM((s, GDK), F32)], compiler_params=_params(2),
    )(dog, o, states, proj, proj, proj, proj, proj, w_alpha_p, b_alpha, out_norm_g)


def _lane():
    return lax.broadcasted_iota(jnp.int32, (1, LANE), 1)


def _swap_halves(x):
    lane = _lane()
    half = MROPE // 2
    lo = (lane >= MNOPE) & (lane < MNOPE + half)
    hi = (lane >= MNOPE + half) & (lane < MQK)
    return jnp.where(lo, pltpu.roll(x, LANE - half, 1), jnp.where(hi, pltpu.roll(x, half, 1), 0.0))


def _norm96(x, g):
    r = lax.rsqrt(jnp.sum(x * x, axis=-1, keepdims=True) * (1.0 / MQK) + EPS)
    return x * r, r


def _lat_norm(proj, q_lat_g, kv_lat_g, *, name, ts=512):
    t = proj.shape[0]
    ts = min(ts, t)

    def body(cq_ref, ckv_ref, gq_ref, gk_ref, oq_ref, ok_ref):
        xq, _ = _rms(cq_ref[...], None)
        oq_ref[...] = (xq * gq_ref[...]).astype(BF16)
        xk, _ = _rms(ckv_ref[...], None)
        ok_ref[...] = (xk * gk_ref[...]).astype(BF16)

    return pl.pallas_call(
        body, name=name, grid=(t // ts,),
        in_specs=[pl.BlockSpec((ts, MQR), lambda i: (i, OFF_CQ // MQR)), pl.BlockSpec((ts, MKVR), lambda i: (i, OFF_CKV // MKVR)),
                  pl.BlockSpec((1, MQR), lambda i: (0, 0)), pl.BlockSpec((1, MKVR), lambda i: (0, 0))],
        out_specs=[pl.BlockSpec((ts, MQR), lambda i: (i, 0)), pl.BlockSpec((ts, MKVR), lambda i: (i, 0))],
        out_shape=[jax.ShapeDtypeStruct((t, MQR), BF16), jax.ShapeDtypeStruct((t, MKVR), BF16)],
        compiler_params=_params(1),
    )(proj, proj, q_lat_g, kv_lat_g)


def _lat_norm_bwd(dcqn, dckvn, proj, q_lat_g, kv_lat_g, *, name, ts=512):
    t = proj.shape[0]
    ts = min(ts, t)

    def one(d_ref, x_ref, g_ref, dx_ref, dg_ref):
        xh, r = _rms(x_ref[...], None)
        dn = d_ref[...]
        dg_ref[...] += jnp.sum(dn * xh, axis=0, keepdims=True)
        dxh = dn * g_ref[...]
        dx_ref[...] = (r * (dxh - xh * jnp.mean(dxh * xh, axis=-1, keepdims=True))).astype(BF16)

    def body(dq_ref, dk_ref, cq_ref, ckv_ref, gq_ref, gk_ref, dxq_ref, dxk_ref, dgq_ref, dgk_ref):
        @pl.when(pl.program_id(0) == 0)
        def _():
            dgq_ref[...] = jnp.zeros_like(dgq_ref)
            dgk_ref[...] = jnp.zeros_like(dgk_ref)

        one(dq_ref, cq_ref, gq_ref, dxq_ref, dgq_ref)
        one(dk_ref, ckv_ref, gk_ref, dxk_ref, dgk_ref)

    return pl.pallas_call(
        body, name=name, grid=(t // ts,),
        in_specs=[pl.BlockSpec((ts, MQR), lambda i: (i, 0)), pl.BlockSpec((ts, MKVR), lambda i: (i, 0)),
                  pl.BlockSpec((ts, MQR), lambda i: (i, OFF_CQ // MQR)), pl.BlockSpec((ts, MKVR), lambda i: (i, OFF_CKV // MKVR)),
                  pl.BlockSpec((1, MQR), lambda i: (0, 0)), pl.BlockSpec((1, MKVR), lambda i: (0, 0))],
        out_specs=[pl.BlockSpec((ts, MQR), lambda i: (i, 0)), pl.BlockSpec((ts, MKVR), lambda i: (i, 0)),
                   pl.BlockSpec((1, MQR), lambda i: (0, 0)), pl.BlockSpec((1, MKVR), lambda i: (0, 0))],
        out_shape=[jax.ShapeDtypeStruct((t, MQR), BF16), jax.ShapeDtypeStruct((t, MKVR), BF16),
                   jax.ShapeDtypeStruct((1, MQR), F32), jax.ShapeDtypeStruct((1, MKVR), F32)],
        compiler_params=_params(1),
    )(dcqn, dckvn, proj, proj, q_lat_g, kv_lat_g)


def _qk_prep(q_raw, kv, proj, cos_t, sin_t, gq, gk, *, name, ts=2048):
    t = q_raw.shape[0]
    ts = min(ts, t)

    def body(q_ref, kv_ref, kpe_ref, c_ref, s_ref, gq_ref, gk_ref, qo_ref, ko_ref, vo_ref):
        cs, sn = c_ref[...], s_ref[...]
        nope = _lane() < MNOPE
        qn, _ = _norm96(q_ref[...], None)
        qn = qn * gq_ref[...]
        qo_ref[...] = (qn * cs + _swap_halves(qn) * sn).astype(BF16)
        kvv = kv_ref[...]
        kn, _ = _norm96(jnp.where(nope, kvv, kpe_ref[...]), None)
        kn = kn * gk_ref[...]
        ko_ref[...] = (kn * cs + _swap_halves(kn) * sn).astype(BF16)
        vo_ref[...] = jnp.where(nope, pltpu.roll(kvv, MNOPE, 1), 0.0).astype(BF16)

    hd = pl.BlockSpec((ts, LANE), lambda i, h: (i, h))
    shared = lambda col: pl.BlockSpec((ts, LANE), lambda i, h: (i, col))
    gain = pl.BlockSpec((1, LANE), lambda i, h: (0, 0))
    out = jax.ShapeDtypeStruct((t, MH * LANE), BF16)
    return pl.pallas_call(
        body, name=name, grid=(t // ts, MH),
        in_specs=[hd, hd, shared(OFF_KPE // LANE), shared(0), shared(0), gain, gain],
        out_specs=[hd, hd, hd], out_shape=[out, out, out], compiler_params=_params(2),
    )(q_raw, kv, proj, cos_t, sin_t, gq, gk)


def _qk_prep_bwd(dq, dk, dv, q_raw, kv, proj, cos_t, sin_t, gq, gk, *, name, ts=2048):
    t = q_raw.shape[0]
    ts = min(ts, t)

    def norm_bwd(dy, x, g, dg_ref):
        xh, r = _norm96(x, None)
        dg_ref[...] += jnp.sum(dy * xh, axis=0, keepdims=True)
        dxh = dy * g
        return r * (dxh - xh * (jnp.sum(dxh * xh, axis=-1, keepdims=True) * (1.0 / MQK)))

    def body(dq_ref, dk_ref, dv_ref, q_ref, kv_ref, kpe_ref, c_ref, s_ref, gq_ref, gk_ref,
             dqr_ref, dkv_ref, dkpe_ref, dgq_ref, dgk_ref):
        i, h = pl.program_id(0), pl.program_id(1)

        @pl.when(h == 0)
        def _():
            dkpe_ref[...] = jnp.zeros_like(dkpe_ref)

        @pl.when((h == 0) & (i == 0))
        def _():
            dgq_ref[...] = jnp.zeros_like(dgq_ref)
            dgk_ref[...] = jnp.zeros_like(dgk_ref)

        cs, sn = c_ref[...], s_ref[...]
        lane = _lane()
        nope = lane < MNOPE
        dqv = dq_ref[...]
        dqn = dqv * cs + _swap_halves(dqv * sn)
        dqr_ref[...] = norm_bwd(dqn, q_ref[...], gq_ref[...], dgq_ref).astype(BF16)
        dkv_ = dk_ref[...]
        dkn = dkv_ * cs + _swap_halves(dkv_ * sn)
        kvv = kv_ref[...]
        dkr = norm_bwd(dkn, jnp.where(nope, kvv, kpe_ref[...]), gk_ref[...], dgk_ref)
        dkv_ref[...] = jnp.where(nope, dkr, pltpu.roll(dv_ref[...], MNOPE, 1)).astype(BF16)
        dkpe_ref[...] += jnp.where((lane >= MNOPE) & (lane < MQK), dkr, 0.0)

    hd = pl.BlockSpec((ts, LANE), lambda i, h: (i, h))
    shared = lambda col: pl.BlockSpec((ts, LANE), lambda i, h: (i, col))
    gain = pl.BlockSpec((1, LANE), lambda i, h: (0, 0))
    out = jax.ShapeDtypeStruct((t, MH * LANE), BF16)
    return pl.pallas_call(
        body, name=name, grid=(t // ts, MH),
        in_specs=[hd, hd, hd, hd, hd, shared(OFF_KPE // LANE), shared(0), shared(0), gain, gain],
        out_specs=[hd, hd, shared(0), gain, gain],
        out_shape=[out, out, jax.ShapeDtypeStruct((t, LANE), F32), jax.ShapeDtypeStruct((1, LANE), F32),
                   jax.ShapeDtypeStruct((1, LANE), F32)],
        compiler_params=_params(2),
    )(dq, dk, dv, q_raw, kv, proj, cos_t, sin_t, gq, gk)


_NT = (((1,), (1,)), ((), ()))
_TN = (((0,), (0,)), ((), ()))


SOFTMAX_SCALE = MQK ** -0.5
Q_PRESCALE = SOFTMAX_SCALE * float(np.log2(np.e))


def _attn_weights(q, k_ref, lo, tq):
    row = lax.broadcasted_iota(jnp.int32, (tq, tq), 0) // CHUNK
    col = lax.broadcasted_iota(jnp.int32, (tq, tq), 1) // CHUNK
    sd = lax.dot_general(q, k_ref[pl.ds(lo, tq), :], _NT, preferred_element_type=F32)
    sd = jnp.where(col <= row, sd, -1e30)
    m = jnp.max(sd, axis=-1, keepdims=True)
    if lo:
        so = lax.dot_general(q, k_ref[pl.ds(0, lo), :], _NT, preferred_element_type=F32)
        m = jnp.maximum(m, jnp.max(so, axis=-1, keepdims=True))
        eo = jnp.exp2(so - m)
        ed = jnp.exp2(sd - m)
        return eo, ed, 1.0 / (jnp.sum(eo, axis=-1, keepdims=True) + jnp.sum(ed, axis=-1, keepdims=True))
    ed = jnp.exp2(sd - m)
    return None, ed, 1.0 / jnp.sum(ed, axis=-1, keepdims=True)


def _attn_fwd(q, k, v, *, name, tq=256):
    bsz, s, _ = q.shape
    tq = min(tq, s)

    def body(q_ref, k_ref, v_ref, o_ref):
        for i in range(s // tq):
            lo = i * tq
            eo, ed, inv = _attn_weights(q_ref[pl.ds(lo, tq), :], k_ref, lo, tq)
            o = jnp.dot(ed.astype(BF16), v_ref[pl.ds(lo, tq), :], preferred_element_type=F32)
            if lo:
                o += jnp.dot(eo.astype(BF16), v_ref[pl.ds(0, lo), :], preferred_element_type=F32)
            o_ref[pl.ds(lo, tq), :] = (o * inv).astype(BF16)

    spec = pl.BlockSpec((None, s, LANE), lambda b, h: (b, 0, h))
    return pl.pallas_call(
        body, name=name, grid=(bsz, MH), in_specs=[spec, spec, spec], out_specs=spec,
        out_shape=jax.ShapeDtypeStruct((bsz, s, MH * LANE), BF16), compiler_params=_params(2),
    )(q, k, v)


def _attn_bwd(q, k, v, do, *, name, tq=256):
    bsz, s, _ = q.shape
    tq = min(tq, s)

    def body(q_ref, k_ref, v_ref, do_ref, dq_ref, dk_ref, dv_ref):
        dk_ref[...] = jnp.zeros_like(dk_ref)
        dv_ref[...] = jnp.zeros_like(dv_ref)
        for i in range(s // tq):
            lo = i * tq
            here, before = pl.ds(lo, tq), pl.ds(0, lo)
            qv, dov = q_ref[here, :], do_ref[here, :]
            eo, ed, inv = _attn_weights(qv, k_ref, lo, tq)
            do_n = (dov.astype(F32) * inv).astype(BF16)
            dv_ref[here, :] += lax.dot_general(ed.astype(BF16), do_n, _TN, preferred_element_type=F32)
            dpd = lax.dot_general(dov, v_ref[here, :], _NT, preferred_element_type=F32)
            delta = jnp.sum(dpd * ed, axis=-1, keepdims=True)
            if lo:
                dv_ref[before, :] += lax.dot_general(eo.astype(BF16), do_n, _TN, preferred_element_type=F32)
                dpo = lax.dot_general(dov, v_ref[before, :], _NT, preferred_element_type=F32)
                delta += jnp.sum(dpo * eo, axis=-1, keepdims=True)
            delta = delta * inv
            r = inv * SOFTMAX_SCALE
            dsd = (ed * (dpd - delta) * r).astype(BF16)
            dq = jnp.dot(dsd, k_ref[here, :], preferred_element_type=F32)
            dk_ref[here, :] += lax.dot_general(dsd, qv, _TN, preferred_element_type=F32)
            if lo:
                dso = (eo * (dpo - delta) * r).astype(BF16)
                dq += jnp.dot(dso, k_ref[before, :], preferred_element_type=F32)
                dk_ref[before, :] += lax.dot_general(dso, qv, _TN, preferred_element_type=F32)
            dq_ref[here, :] = dq
        dk_ref[...] = dk_ref[...] * (1.0 / Q_PRESCALE)

    spec = pl.BlockSpec((None, s, LANE), lambda b, h: (b, 0, h))
    out = jax.ShapeDtypeStruct((bsz, s, MH * LANE), F32)
    return pl.pallas_call(
        body, name=name, grid=(bsz, MH), in_specs=[spec] * 4, out_specs=[spec] * 3, out_shape=[out, out, out],
        compiler_params=_params(2),
    )(q, k, v, do)


def _adamw(w, g, m, v, *, name, tr=256, by_cols=False):
    rows, cols = w.shape
    tr = _tile_rows(rows, tr)

    def body(w_ref, g_ref, m_ref, v_ref, d_ref, nm_ref, nv_ref):
        d_ref[...], nm_ref[...], nv_ref[...] = _adamw_update(w_ref[...], g_ref[...], m_ref[...], v_ref[...])

    spec = pl.BlockSpec((rows, LANE), lambda i: (0, i)) if by_cols else pl.BlockSpec((tr, cols), lambda i: (i, 0))
    out = jax.ShapeDtypeStruct((rows, cols), F32)
    return pl.pallas_call(body, name=name, grid=(cols // LANE if by_cols else rows // tr,), in_specs=[spec] * 4,
                          out_specs=[spec] * 3, out_shape=[out, out, out], compiler_params=_params(1))(w, g, m, v)


def _tile_rows(rows, target):
    if rows <= target:
        return rows
    best = 8
    for t in range(8, target + 1, 8):
        if rows % t == 0:
            best = t
    return best


def _adamw_update(w, g, m, v):
    nm = ADAM_B1 * m + (1.0 - ADAM_B1) * g
    nv = ADAM_B2 * v + (1.0 - ADAM_B2) * (g * g)
    m_hat = nm / (1.0 - ADAM_B1 ** ADAM_STEP)
    v_hat = nv / (1.0 - ADAM_B2 ** ADAM_STEP)
    return -ADAM_LR * (m_hat / (jnp.sqrt(v_hat) + ADAM_EPS) + ADAM_WD * w), nm, nv


def _adamw_halves(w, m, v, mine, theirs, sel, *, name, tr=256):
    rows, cols = w.shape
    tr = _tile_rows(rows // 2, tr)
    nh = rows // 2 // tr

    def body(sel_ref, w_ref, m_ref, v_ref, mine_ref, theirs_ref, g_ref, d_ref, nm_ref, nv_ref):
        lower = pl.program_id(0) < nh
        south = sel_ref[0] == 0
        gv = jnp.where(lower == south, mine_ref[...], theirs_ref[...])
        g_ref[...] = gv
        d_ref[...], nm_ref[...], nv_ref[...] = _adamw_update(w_ref[...], gv, m_ref[...], v_ref[...])

    full = pl.BlockSpec((tr, cols), lambda i, sel_ref: (i, 0))
    half = pl.BlockSpec((tr, cols), lambda i, sel_ref: (i % nh, 0))
    out = jax.ShapeDtypeStruct((rows, cols), F32)
    return pl.pallas_call(
        body, name=name, out_shape=[out] * 4, compiler_params=_params(1),
        grid_spec=pltpu.PrefetchScalarGridSpec(num_scalar_prefetch=1, grid=(rows // tr,),
                                               in_specs=[full, full, full, half, half], out_specs=[full] * 4),
    )(sel, w, m, v, mine, theirs)


def _pair_add(x, sib, sel, *, name, tr=256):
    n, _, rows, cols = x.shape
    tr = _tile_rows(rows, tr)

    def body(sel_ref, x_ref, s_ref, o_ref):
        o_ref[...] = (x_ref[...] + s_ref[...]).astype(BF16)

    spec = pl.BlockSpec((None, tr, cols), lambda j, i, sel_ref: (j, i, 0))
    return pl.pallas_call(
        body, name=name, out_shape=jax.ShapeDtypeStruct((n, rows, cols), BF16), compiler_params=_params(2),
        grid_spec=pltpu.PrefetchScalarGridSpec(
            num_scalar_prefetch=1, grid=(n, rows // tr),
            in_specs=[pl.BlockSpec((None, None, tr, cols), lambda j, i, sel_ref: (j, sel_ref[0], i, 0)), spec],
            out_specs=spec),
    )(sel, x, sib)


def _chip_sum(pair, recv, sel, *, name, tr=256):
    _, rows, cols = pair.shape
    tr = _tile_rows(rows, tr)

    def body(sel_ref, p_ref, r_ref, o_ref):
        acc = p_ref[...].astype(F32)
        for k in range(3):
            acc = acc + r_ref[k].astype(F32)
        o_ref[...] = acc

    return pl.pallas_call(
        body, name=name, out_shape=jax.ShapeDtypeStruct((rows, cols), F32), compiler_params=_params(1),
        grid_spec=pltpu.PrefetchScalarGridSpec(
            num_scalar_prefetch=1, grid=(rows // tr,),
            in_specs=[pl.BlockSpec((None, tr, cols), lambda i, sel_ref: (sel_ref[0], i, 0)),
                      pl.BlockSpec((3, tr, cols), lambda i, sel_ref: (0, i, 0))],
            out_specs=pl.BlockSpec((tr, cols), lambda i, sel_ref: (i, 0))),
    )(sel, pair, recv)


def _me():
    return lax.axis_index("x"), lax.axis_index("y"), lax.axis_index("c")


def _flip(pos, bits):
    x, y, c = pos
    return (x ^ bits[0] if bits[0] else x, y ^ bits[1] if bits[1] else y, c ^ bits[2] if bits[2] else c)


ANY = pl.BlockSpec(memory_space=pl.ANY)


def _all_gather8(xs, *, name):
    n = len(xs)
    flips = [((k >> 2) & 1, (k >> 1) & 1, k & 1) for k in range(1, 8)]

    def body(*refs):
        x_refs, out_refs, (send_sems, recv_sems, local_sems) = refs[:n], refs[n:2 * n], refs[2 * n:]
        me = _me()
        slot = lambda p: 4 * p[0] + 2 * p[1] + p[2]
        copies = []
        for i in range(n):
            mine = pltpu.make_async_copy(x_refs[i], out_refs[i].at[slot(me)], local_sems.at[i])
            mine.start()
            copies.append(mine)
            for k, f in enumerate(flips):
                peer = _flip(me, f)
                sems = dict(send_sem=send_sems.at[7 * i + k], recv_sem=recv_sems.at[7 * i + k], device_id=peer,
                            device_id_type=MESH)
                cp = pltpu.make_async_remote_copy(src_ref=x_refs[i], dst_ref=out_refs[i].at[slot(me)], **sems)
                cp.start()
                copies.append(cp)
                copies.append(pltpu.make_async_remote_copy(src_ref=x_refs[i], dst_ref=out_refs[i].at[slot(peer)], **sems))
        for i in range(n):
            base = i * 15
            copies[base].wait()
            for k in range(7):
                copies[base + 1 + 2 * k].wait_send()
                copies[base + 2 + 2 * k].wait_recv()

    outs = pl.pallas_call(
        body, name=name, in_specs=[ANY] * n, out_specs=[ANY] * n,
        out_shape=[jax.ShapeDtypeStruct((8, *x.shape), x.dtype) for x in xs],
        scratch_shapes=[pltpu.SemaphoreType.DMA((7 * n,)), pltpu.SemaphoreType.DMA((7 * n,)),
                        pltpu.SemaphoreType.DMA((n,))])(*xs)
    return list(outs)


CHIP_FLIPS = [(1, 0, 0), (0, 1, 0), (1, 1, 0)]


def _chip():
    return 2 * lax.axis_index("x") + lax.axis_index("y")


HBM = pl.BlockSpec(memory_space=pltpu.HBM)
SEM = pl.BlockSpec(memory_space=pltpu.SEMAPHORE)
EFFECT = pltpu.SideEffectType.DATAFLOW_SIDE_EFFECTING


def _plan_copies(plan, refs, send_sems, recv_sems):
    return [pltpu.make_async_remote_copy(src_ref=src, dst_ref=dst, send_sem=send_sems.at[k], recv_sem=recv_sems.at[k],
                                         device_id=to, device_id_type=MESH) for k, (src, dst, to) in enumerate(plan(refs))]


def _rdma_start(arrays, n_copies, plan, deps, *, name):
    n, nd = len(arrays), len(deps)

    def body(*refs):
        for cp in _plan_copies(plan, refs[:n], refs[n + nd], refs[n + nd + 1]):
            cp.start()
        refs[-1][...] = jnp.zeros_like(refs[-1])

    outs = pl.pallas_call(
        body, name=name,
        out_shape=(pltpu.SemaphoreType.DMA((n_copies,)), pltpu.SemaphoreType.DMA((n_copies,)),
                   *[pltpu.HBM(a.shape, a.dtype) for a in arrays], jax.ShapeDtypeStruct((8, LANE), F32)),
        in_specs=[HBM] * n + [ANY] * nd, out_specs=(SEM, SEM, *[HBM] * n, pl.BlockSpec(memory_space=pltpu.VMEM)),
        input_output_aliases={i: i + 2 for i in range(n)}, compiler_params=pltpu.CompilerParams(has_side_effects=EFFECT),
    )(*[pltpu.with_memory_space_constraint(a, pltpu.HBM) for a in arrays], *deps)
    return outs[0], outs[1], list(outs[2:2 + n]), outs[-1]


def _rdma_wait(send_sems, recv_sems, arrays, plan, after, *, name):
    n = len(arrays)

    def body(*refs):
        for cp in _plan_copies(plan, refs[:n], refs[n], refs[n + 1]):
            cp.wait_send()
            cp.wait_recv()

    return list(pl.pallas_call(
        body, name=name, out_shape=tuple(pltpu.HBM(a.shape, a.dtype) for a in arrays),
        in_specs=[HBM] * n + [SEM, SEM, ANY], out_specs=tuple([HBM] * n), input_output_aliases={i: i for i in range(n)},
        compiler_params=pltpu.CompilerParams(has_side_effects=EFFECT),
    )(*arrays, send_sems, recv_sems, after))


def _gather_plan(n):
    def plan(refs):
        me = _me()
        slot = 2 * me[0] + me[1]
        return [(refs[i].at[me[2]], refs[n + i].at[slot, me[2]], _flip(me, f)) for i in range(n) for f in CHIP_FLIPS]
    return plan


def _scatter_plan(n):
    def plan(refs):
        me = _me()
        out = []
        for i in range(n):
            for k, f in enumerate(CHIP_FLIPS):
                peer = _flip(me, f)
                out.append((refs[i].at[2 * peer[0] + peer[1]], refs[n + i].at[k], peer))
        return out
    return plan


def _sibling_plan(n, src_of):
    def plan(refs):
        me = _me()
        return [(src_of(refs[i], me[2]), refs[n + i], _flip(me, (0, 0, 1))) for i in range(n)]
    return plan


def _gather8_plan(n):
    def plan(refs):
        me = _me()
        slot = 4 * me[0] + 2 * me[1] + me[2]
        return [(refs[i], refs[n + i].at[slot], _flip(me, ((k >> 2) & 1, (k >> 1) & 1, k & 1)))
                for i in range(n) for k in range(1, 8)]
    return plan


def _pair_fill(lands, *, name):
    n = len(lands)

    def body(*refs):
        in_refs, (send_sems, recv_sems) = refs[:n], refs[2 * n:]
        me = _me()
        sib = _flip(me, (0, 0, 1))
        copies = []
        for i in range(n):
            for k, f in enumerate(CHIP_FLIPS):
                peer = _flip(me, f)
                slot = 2 * peer[0] + peer[1]
                mine, theirs = in_refs[i].at[slot, me[2]], in_refs[i].at[slot, 1 - me[2]]
                cp = pltpu.make_async_remote_copy(src_ref=mine, dst_ref=mine, send_sem=send_sems.at[3 * i + k],
                                                  recv_sem=recv_sems.at[3 * i + k], device_id=sib, device_id_type=MESH)
                cp.start()
                copies.append((cp, pltpu.make_async_remote_copy(
                    src_ref=mine, dst_ref=theirs, send_sem=send_sems.at[3 * i + k], recv_sem=recv_sems.at[3 * i + k],
                    device_id=sib, device_id_type=MESH)))
        for cp, arrival in copies:
            arrival.wait_recv()
            cp.wait_send()

    return list(pl.pallas_call(
        body, name=name, in_specs=[ANY] * n, out_specs=[ANY] * n,
        out_shape=[jax.ShapeDtypeStruct(a.shape, a.dtype) for a in lands], input_output_aliases={i: i for i in range(n)},
        scratch_shapes=[pltpu.SemaphoreType.DMA((3 * n,)), pltpu.SemaphoreType.DMA((3 * n,))])(*lands))


def _own_and_landed(lands, xs):
    chip = _chip()
    return [[jnp.where(chip == j, x, o.reshape(4, *x.shape)[j]) for j in range(4)] for o, x in zip(lands, xs)]


BIG = (("w_in", (D, IN_WIDTH // 4), 1), ("gla_w_o", (D // 4, D), 0), ("mla_w_uq", (MQR, MH * MQK // 4), 1),
       ("mla_w_ukv", (MKVR, MH * (MNOPE + MVD) // 4), 1), ("mla_w_o", (D // 4, D), 0), ("w_out", (D // 4, D), 0),
       ("mlp_w1", (D, DFF // 4), 1), ("mlp_w2", (DFF // 4, D), 0))
ADA_SHARD = (D, 6 * D // 4)
SMALL = (("b_ada", 6 * D), ("norm1_g", D), ("b_merge", 2 * D), ("gla_b_alpha", GH * GDK), ("gla_out_norm_g", GDV),
         ("mla_q_lat_g", MQR), ("mla_kv_lat_g", MKVR), ("mla_qn_g", MQK), ("mla_kn_g", MQK), ("norm2_g", D))


W_IN_SEGMENTS = ((0, 3072, OFF_Q), (3072, 3088, OFF_A), (3088, 3344, OFF_CQ), (3344, 3472, OFF_CKV),
                 (3472, 3504, OFF_KPE + MNOPE), (3504, 5552, OFF_MA))
W_IN_SPLIT = OFF_MA
SMALL_ROWS, SMALL_COLS = 32, 2 * D
W_ALPHA_ROW = 16
LOSS_ROW = 15
SMALL_RED = tuple((n, k) for n, k in SMALL if n != "b_ada")


def _pack_small(grads, d_w_alpha, loss_row, *, name):
    def body(*refs):
        g_refs, wa_ref, loss_ref, out_ref = refs[:-3], refs[-3], refs[-2], refs[-1]
        out_ref[...] = jnp.zeros_like(out_ref)
        for i, ((_, k), g_ref) in enumerate(zip(SMALL_RED, g_refs)):
            out_ref[i:i + 1, 0:k] = g_ref[...]
        out_ref[LOSS_ROW:LOSS_ROW + 1, 0:LANE] = loss_ref[...]
        out_ref[W_ALPHA_ROW:W_ALPHA_ROW + GLR, 0:GH * GDK] = wa_ref[...]

    return pl.pallas_call(body, name=name, out_shape=jax.ShapeDtypeStruct((SMALL_ROWS, SMALL_COLS), F32))(
        *grads, d_w_alpha, loss_row)


def _small_update(gathered, dmod_all, sel, wmv, *, name):
    names = [n for n, _ in SMALL] + ["gla_w_alpha"]
    n_par = len(names)

    def body(sel_ref, g_ref, dmod_ref, *refs):
        in_refs, out_refs, loss_ref, acc = refs[:3 * n_par], refs[3 * n_par:-2], refs[-2], refs[-1]
        total = g_ref[0]
        for j in range(1, 8):
            total = total + g_ref[j]
        acc[...] = total
        loss_ref[...] = acc[LOSS_ROW:LOSS_ROW + 1, 0:LANE]
        row = {n: i for i, (n, _) in enumerate(SMALL_RED)}
        for p, name_p in enumerate(names):
            w_ref, m_ref, v_ref = in_refs[3 * p:3 * p + 3]
            if name_p == "b_ada":
                gv = jnp.sum(dmod_ref[...], axis=0, keepdims=True)
            elif name_p == "gla_w_alpha":
                gv = jnp.zeros((GLR, GDK), F32)
                for j in range(4):
                    blk = acc[W_ALPHA_ROW:W_ALPHA_ROW + GLR, j * GDK:(j + 1) * GDK]
                    gv = gv + jnp.where(sel_ref[0] == j, blk, 0.0)
            else:
                gv = acc[row[name_p]:row[name_p] + 1, 0:w_ref.shape[1]]
            o = out_refs[4 * p:4 * p + 4]
            o[0][...] = gv
            o[1][...], o[2][...], o[3][...] = _adamw_update(w_ref[...], gv, m_ref[...], v_ref[...])

    flat = [a for t in wmv for a in t]
    out_shape = [jax.ShapeDtypeStruct(t[0].shape, F32) for t in wmv for _ in range(4)]
    out_shape.append(jax.ShapeDtypeStruct((1, LANE), F32))
    vmem = pl.BlockSpec(memory_space=pltpu.VMEM)
    outs = pl.pallas_call(
        body, name=name, out_shape=out_shape, in_specs=[pl.BlockSpec(memory_space=pltpu.SMEM), vmem, vmem] + [vmem] * len(flat),
        out_specs=[vmem] * len(out_shape), scratch_shapes=[pltpu.VMEM((SMALL_ROWS, SMALL_COLS), F32)],
    )(sel, gathered, dmod_all, *flat)
    return {n: tuple(outs[4 * p:4 * p + 4]) for p, n in enumerate(names)}, outs[-1][0, 0]


def _full_weights(gathered):
    w = {name: jnp.concatenate(gathered[name], axis=axis) for name, _, axis in BIG if name in gathered and name != "w_in"}
    if "w_in" in gathered:
        shards = gathered["w_in"]
        zeros = lambda n: [jnp.zeros((D, n), shards[0].dtype)]

        def cols(a, b):
            width = IN_WIDTH // 4
            return [shards[j][:, max(a, j * width) - j * width:min(b, (j + 1) * width) - j * width]
                    for j in range(4) if max(a, j * width) < min(b, (j + 1) * width)]

        parts = []
        for a, b, at in sorted(W_IN_SEGMENTS, key=lambda seg: seg[2]):
            have = sum(p.shape[1] for p in parts)
            parts += (zeros(at - have) if at > have else []) + cols(a, b)
        w["w_in"] = jnp.concatenate(parts + zeros(PW - sum(p.shape[1] for p in parts)), axis=1)
    if "mla_w_uq" in w:
        w["mla_w_uq"] = jnp.pad(w["mla_w_uq"].reshape(MQR, MH, MQK), ((0, 0), (0, 0), (0, LANE - MQK))).reshape(MQR, MH * LANE)
    if "mla_w_o" in w:
        w["mla_w_o"] = jnp.pad(w["mla_w_o"].reshape(MH, MVD, D), ((0, 0), (0, LANE - MVD), (0, 0))).reshape(MH * LANE, D)
    return w


def _grad_slots(g):
    g = dict(g)
    out = {}
    if "w_in" in g:
        g_lo, g_hi = g.pop("w_in")
        take = lambda at, lo, hi: g_lo[:, at + lo:at + hi] if at < W_IN_SPLIT else g_hi[:, at - W_IN_SPLIT + lo:at - W_IN_SPLIT + hi]
        width = IN_WIDTH // 4
        slots = []
        for j in range(4):
            lo, hi = j * width, (j + 1) * width
            slots.append(jnp.concatenate([take(at, max(lo, a) - a, min(hi, b) - a)
                                          for a, b, at in W_IN_SEGMENTS if max(lo, a) < min(hi, b)], axis=1))
        out["w_in"] = jnp.stack(slots).reshape(4, 2, D // 2, width)
    if "mla_w_uq" in g:
        g["mla_w_uq"] = g["mla_w_uq"].reshape(MQR, MH, LANE)[:, :, :MQK].reshape(MQR, MH * MQK)
    if "mla_w_o" in g:
        g["mla_w_o"] = g["mla_w_o"].reshape(MH, LANE, D)[:, :MVD].reshape(MH * MVD, D)
    for name, (rows, cols), axis in BIG:
        if name not in g:
            continue
        a = g[name]
        a = a.reshape(4, rows, cols) if axis == 0 else jnp.transpose(a.reshape(rows, 4, cols), (1, 0, 2))
        out[name] = a.reshape(4, 2, rows // 2, cols)
    return out


def _rope_tables(positions):
    freqs = ROPE_THETA ** (-jnp.arange(0, MROPE, 2, dtype=F32) / MROPE)
    lane = np.arange(LANE)
    in_rope = (lane >= MNOPE) & (lane < MQK)
    freq_lane = jnp.where(in_rope, freqs[(lane - MNOPE) % (MROPE // 2)], 0.0)
    sign = np.where(in_rope, np.where(lane < MNOPE + MROPE // 2, -1.0, 1.0), 0.0).astype(np.float32)
    ang = positions.astype(F32).reshape(-1, 1) * freq_lane[None, :]
    return jnp.cos(ang), jnp.sin(ang) * sign[None, :]


def _local_step(x, positions, mod, target, w, small, more_weights=None, on_grads=None):
    kept = {}
    if on_grads is None:
        on_grads = lambda tag, grads, after: kept.update(grads)
    bsz, s, _ = x.shape
    t = bsz * s
    tt = _tile(t, 1024)
    shift1, scale1, gate1, shift2, scale2, gate2 = [mod[:, None, i * D:(i + 1) * D] for i in range(6)]
    cos_t, sin_t = _rope_tables(positions)
    w_alpha_p = jnp.pad(small["gla_w_alpha"], ((0, LANE - GLR), (0, 0)))
    gq = jnp.pad(small["mla_qn_g"], ((0, 0), (0, LANE - MQK)))
    gk = jnp.pad(small["mla_kn_g"], ((0, 0), (0, LANE - MQK)))
    flat2 = lambda a: a.reshape(t, a.shape[-1])
    bsd = lambda a: a.reshape(bsz, s, a.shape[-1])

    h = _norm_mod(x, small["norm1_g"], scale1, shift1, name="norm1")
    if callable(w):
        w = w(h)
    proj = _mm(flat2(h), w["w_in"], name="proj", tn=1152)
    proj3 = bsd(proj)
    o, o_gated, states = _gla_fwd(proj3, w_alpha_p, small["gla_b_alpha"], small["gla_out_norm_g"], name="gla_fwd")
    if more_weights is not None:
        w = {**w, **more_weights(o_gated)}
    y_a = _mm(flat2(o_gated), w["gla_w_o"], name="gla_out")
    cq_n, ckv_n = _lat_norm(proj, small["mla_q_lat_g"], small["mla_kv_lat_g"], name="lat_norm")
    q_raw = _mm(cq_n, w["mla_w_uq"], name="mla_uq")
    kv = _mm(ckv_n, w["mla_w_ukv"], name="mla_ukv")
    qf, kf, vf = _qk_prep(q_raw, kv, proj, cos_t, sin_t, gq * Q_PRESCALE, gk, name="qk_prep")
    o_attn = _attn_fwd(bsd(qf), bsd(kf), bsd(vf), name="attn_fwd")
    y_b = _mm(flat2(o_attn), w["mla_w_o"], name="mla_out")
    mixed_in = _merge_fwd(proj3, small["b_merge"], bsd(y_a), bsd(y_b), name="merge_fwd")
    mixed = _mm(flat2(mixed_in), w["w_out"], name="w_out")
    x1, h2 = _resid_norm_mod(x, bsd(mixed), gate1, small["norm2_g"], scale2, shift2, name="norm2")

    def sqrelu(acc, ex, outs):
        r = jnp.maximum(acc, 0.0)
        outs[0][...] = (r * r).astype(BF16)

    r = _mm(flat2(h2), w["mlp_w1"], name="mlp1", epilogue=sqrelu, out_shape=jax.ShapeDtypeStruct((t, DFF), BF16),
            out_specs=_tile_spec(tt, 1024))
    ff = _mm(r, w["mlp_w2"], name="mlp2")
    dy, dff, dgate2, loss_part = _loss_head(x1, bsd(ff), gate2, target, name="loss_head")

    g = {}

    def relu2_bwd(acc, ex, outs):
        outs[0][...] = (acc * (2.0 * jnp.sqrt(ex[0][...].astype(F32)))).astype(BF16)

    dff2 = flat2(dff)
    da1 = _mm(dff2, w["mlp_w2"], tb=True, name="mlp2_dx", epilogue=relu2_bwd, extras=(r,),
              extra_specs=(_tile_spec(tt, 1024),), out_shape=jax.ShapeDtypeStruct((t, DFF), BF16),
              out_specs=_tile_spec(tt, 1024))
    g["mlp_w2"] = _mm(r, dff2, ta=True, name="mlp2_dw")
    dh2 = _mm(da1, w["mlp_w1"], tb=True, name="mlp1_dx")
    g["mlp_w1"] = _mm(flat2(h2), da1, ta=True, name="mlp1_dw")
    token = on_grads("mlp", {n: g.pop(n) for n in ("mlp_w2", "mlp_w1")}, dh2)
    if token is not None:
        gate1 = gate1 + token[0, 0]
    dx1, dscale2, dshift2, dg2, dgate1, dmixed = _norm_mod_bwd(
        bsd(dh2), x1, dy, small["norm2_g"], scale2, gate1, bsd(mixed), name="norm2_bwd")
    dmixed2 = flat2(dmixed)
    dmi = _mm(dmixed2, w["w_out"], tb=True, name="w_out_dx")
    g["w_out"] = _mm(flat2(mixed_in), dmixed2, ta=True, name="w_out_dw")
    dy_a, dy_b, dl_a, dl_b, db_a, db_b = _merge_bwd(bsd(dmi), proj3, small["b_merge"], bsd(y_a), bsd(y_b), name="merge_bwd")
    dy_a2, dy_b2 = flat2(dy_a), flat2(dy_b)
    dog = _mm(dy_a2, w["gla_w_o"], tb=True, name="gla_out_dx")
    g["gla_w_o"] = _mm(flat2(o_gated), dy_a2, ta=True, name="gla_out_dw")
    dq_g, dk_g, dv_g, dg_g, dlog, db_alpha, d_ong = _gla_bwd(
        bsd(dog), o, states, proj3, w_alpha_p, small["gla_b_alpha"], small["gla_out_norm_g"], name="gla_bwd")
    dlog2 = flat2(dlog)
    da_p = _mm(dlog2, w_alpha_p, tb=True, out_dtype=BF16, name="alpha_dx")
    d_w_alpha = _mm(proj[:, OFF_A:OFF_A + LANE], dlog2, ta=True, name="alpha_dw")[:GLR]
    do_attn = _mm(dy_b2, w["mla_w_o"], tb=True, out_dtype=BF16, name="mla_out_dx")
    g["mla_w_o"] = _mm(flat2(o_attn), dy_b2, ta=True, name="mla_out_dw")
    dqf, dkf, dvf = _attn_bwd(bsd(qf), bsd(kf), bsd(vf), bsd(do_attn), name="attn_bwd")
    dq_raw, dkv, dkpe, dgq, dgk = _qk_prep_bwd(flat2(dqf), flat2(dkf), flat2(dvf), q_raw, kv, proj, cos_t, sin_t, gq, gk,
                                                name="qk_prep_bwd")
    dcq_n = _mm(dq_raw, w["mla_w_uq"], tb=True, name="mla_uq_dx")
    g["mla_w_uq"] = _mm(cq_n, dq_raw, ta=True, name="mla_uq_dw")
    dckv_n = _mm(dkv, w["mla_w_ukv"], tb=True, name="mla_ukv_dx")
    g["mla_w_ukv"] = _mm(ckv_n, dkv, ta=True, name="mla_ukv_dw")
    token = on_grads("mix", {n: g.pop(n) for n in ("w_out", "gla_w_o", "mla_w_o", "mla_w_uq", "mla_w_ukv")}, dckv_n)
    q_lat_g = small["mla_q_lat_g"] if token is None else small["mla_q_lat_g"] + token[0:1, 0:1]
    dcq, dckv, dg_qlat, dg_kvlat = _lat_norm_bwd(dcq_n, dckv_n, proj, q_lat_g, small["mla_kv_lat_g"],
                                                  name="lat_norm_bwd")
    pieces = [(flat2(dq_g), OFF_Q), (flat2(dk_g), OFF_K), (flat2(dv_g), OFF_V), (flat2(dg_g), OFF_G),
              (flat2(dl_a), OFF_MA), (flat2(dl_b), OFF_MB), (dcq, OFF_CQ), (dckv, OFF_CKV), (da_p, OFF_A), (dkpe, OFF_KPE)]
    hb = flat2(h)
    g_w_in = (_pieces_dw(hb, [p for p, off in pieces if off < W_IN_SPLIT], name="proj_dw_a"),
              _pieces_dw(hb, [p for p, off in pieces if off >= W_IN_SPLIT], name="proj_dw_b"))
    token = on_grads("in", {"w_in": g_w_in}, g_w_in[1])
    after = jnp.zeros((8, LANE), F32) if token is None else token
    dh = _pieces_dx(pieces, w["w_in"], after, name="proj_dx")
    token = on_grads("dx", {}, dh)
    if token is not None:
        scale1 = scale1 + token[0, 0]
    grad_x, dscale1, dshift1, dg1 = _norm_mod_bwd(bsd(dh), x, dx1, small["norm1_g"], scale1, name="norm1_bwd")

    dmod = jnp.concatenate([dshift1, dscale1, dgate1, dshift2, dscale2, dgate2], axis=-1).reshape(bsz, 6 * D)
    gs = {"norm1_g": dg1, "b_merge": jnp.concatenate([db_a, db_b], axis=1), "gla_b_alpha": db_alpha,
          "gla_out_norm_g": d_ong, "mla_q_lat_g": dg_qlat, "mla_kv_lat_g": dg_kvlat, "mla_qn_g": dgq[:, :MQK],
          "mla_kn_g": dgk[:, :MQK], "norm2_g": dg2}
    return loss_part[0, 0], grad_x, dmod, {**kept, **g}, gs, d_w_alpha


def kernel(x, c, positions, w_ada, b_ada, norm1_g, w_in, b_merge, gla_w_alpha, gla_b_alpha, gla_out_norm_g, gla_w_o, mla_q_lat_g, mla_w_uq, mla_kv_lat_g, mla_w_ukv, mla_qn_g, mla_kn_g, mla_w_o, w_out, norm2_g, mlp_w1, mlp_w2, loss_target, m_w_ada, m_b_ada, m_norm1_g, m_w_in, m_b_merge, m_gla_w_alpha, m_gla_b_alpha, m_gla_out_norm_g, m_gla_w_o, m_mla_q_lat_g, m_mla_w_uq, m_mla_kv_lat_g, m_mla_w_ukv, m_mla_qn_g, m_mla_kn_g, m_mla_w_o, m_w_out, m_norm2_g, m_mlp_w1, m_mlp_w2, v_w_ada, v_b_ada, v_norm1_g, v_w_in, v_b_merge, v_gla_w_alpha, v_gla_b_alpha, v_gla_out_norm_g, v_gla_w_o, v_mla_q_lat_g, v_mla_w_uq, v_mla_kv_lat_g, v_mla_w_ukv, v_mla_qn_g, v_mla_kn_g, v_mla_w_o, v_w_out, v_norm2_g, v_mlp_w1, v_mlp_w2):
    args = dict(locals())
    names_big = [n for n, _, _ in BIG]
    names_small = [n for n, _ in SMALL]
    bsz = x.shape[0]
    ax, ay, ac = lax.axis_index("x"), lax.axis_index("y"), lax.axis_index("c")
    chip = 2 * ax + ay
    dev = 2 * chip + ac

    small = {n: args[n] for n in names_small}
    sel_c = jnp.reshape(ac, (1,)).astype(jnp.int32)
    sel_chip = jnp.reshape(chip, (1,)).astype(jnp.int32)
    c_all, w_alpha_all = _all_gather8([c, gla_w_alpha[0]], name="comm_c_alpha")
    small["gla_w_alpha"] = jnp.concatenate([w_alpha_all[2 * j] for j in range(4)], axis=1)
    c_all = c_all.reshape(8 * bsz, D)

    shards = {n: args[n][0].astype(BF16) for n in names_big}
    halves_of = lambda names: [shards[n].reshape(2, shards[n].shape[0] // 2, shards[n].shape[1]) for n in names]

    def gather_start(names, deps, tag):
        xs = halves_of(names)
        lands = [lax.empty((4, *xh.shape), BF16) for xh in xs]
        plan = _gather_plan(len(names))
        return names, plan, _rdma_start(xs + lands, 3 * len(names), plan, deps, name="comm_weights_start_" + tag)

    def gather_finish(started, after, tag):
        names, plan, sems = started
        arrs = _rdma_wait(sems[0], sems[1], sems[2], plan, after, name="comm_weights_wait_" + tag)
        filled = _pair_fill(arrs[len(names):], name="comm_weights_pair_" + tag)
        own = [a.reshape(shards[n].shape) for n, a in zip(names, arrs)]
        return _full_weights(dict(zip(names, _own_and_landed(filled, own))))


    def add_bias(acc, ex, outs):
        outs[0][...] = acc + ex[0][...]

    silu = lambda v: v * _sigmoid(v)
    b_ada_mine = lax.dynamic_slice(b_ada, (0, chip * ADA_SHARD[1]), (1, ADA_SHARD[1]))
    mod_part = _mm(c_all, w_ada[0], name="ada", tn=512, a_fn=silu, epilogue=add_bias, extras=(b_ada_mine,),
                   extra_specs=(pl.BlockSpec((1, 512), lambda i, j, k: (0, j)),),
                   out_shape=jax.ShapeDtypeStruct((8 * bsz, ADA_SHARD[1]), F32), out_specs=_tile_spec(8 * bsz, 512))
    mod_all = _all_gather8([mod_part], name="comm_mod")[0]
    mod_rows = lax.dynamic_slice(mod_all, (0, dev * bsz, 0), (8, bsz, ADA_SHARD[1]))
    mod = jnp.concatenate([mod_rows[2 * j] for j in range(4)], axis=1)
    first = gather_start(["w_in"], (mod,), "in")
    rest = gather_start([n for n in names_big if n != "w_in"], (mod, first[2][3]), "rest")
    mod = mod + rest[2][3][0, 0]
    w_in_after = lambda after: gather_finish(first, after, "in")
    more_weights = lambda after: gather_finish(rest, after, "rest")

    stage = {}

    def begin(tag, names, arrays, lands, n_copies, plan, what):
        stage[tag] = (names, plan, _rdma_start(arrays + lands, n_copies, plan, (), name=f"comm_{what}_start_{tag}"))
        return stage[tag][2][3]

    def landed(tag, after, what):
        names, plan, sems = stage[tag]
        arrs = _rdma_wait(sems[0], sems[1], sems[2], plan, after, name=f"comm_{what}_wait_{tag}")
        return names, arrs[:len(arrs) // 2], arrs[len(arrs) // 2:]

    def swap_start(tag, grads):
        names = list(grads)
        parts = [_grad_slots(grads)[n] for n in names]
        lands = [lax.empty((4, *p.shape[2:]), F32) for p in parts]
        return begin(tag, names, parts, lands, len(names), _sibling_plan(len(names), lambda r, c: r.at[:, 1 - c]), "pair_sum")

    def scatter_start(tag, after):
        names, parts, sib_halves = landed(tag, after, "pair_sum")
        pairs = [_pair_add(p, s, sel_c, name="pair_add_" + n) for n, p, s in zip(names, parts, sib_halves)]
        recvs = [lax.empty((3, *p.shape[1:]), BF16) for p in pairs]
        return begin(tag, names, pairs, recvs, 3 * len(names), _scatter_plan(len(names)), "scatter")

    def join_start(tag, after):
        names, pairs, recvs = landed(tag, after, "scatter")
        halves = [_chip_sum(p, r, sel_chip, name="chip_sum_" + n) for n, p, r in zip(names, pairs, recvs)]
        lands = [lax.empty(h.shape, F32) for h in halves]
        return begin(tag, names, halves, lands, len(names), _sibling_plan(len(names), lambda r, c: r), "pair_join")

    def reduce_step(tag, grads, after):
        if tag == "mlp":
            return swap_start("mlp", grads)
        if tag == "mix":
            return scatter_start("mlp", after) + swap_start("mix", grads)
        if tag == "in":
            return scatter_start("mix", after) + swap_start("in", grads)
        return scatter_start("in", after)

    loss_part, grad_x, dmod, g, gs, d_w_alpha = _local_step(x, positions, mod, loss_target, w_in_after, small,
                                                            more_weights, reduce_step)

    assert not g, list(g)
    gs_packed = _pack_small([gs[n] for n, _ in SMALL_RED], d_w_alpha, jnp.full((1, LANE), loss_part, F32),
                            name="pack_small")
    small_lands = [lax.empty((8, *a.shape), F32) for a in (dmod, gs_packed)]
    begin("small", ["dmod", "small"], [dmod, gs_packed], small_lands, 14, _gather8_plan(2), "gather8")

    res = {}

    def finish(tag, after):
        names, halves, theirs = landed(tag, after, "pair_join")
        for n, mine, other in zip(names, halves, theirs):
            if n == "w_in":
                south = ac == 0
                g_t = jnp.concatenate([jnp.where(south, mine, other), jnp.where(south, other, mine)], axis=0).T
                outs = _adamw(w_in[0].T, g_t, m_w_in[0].T, v_w_in[0].T, name="adamw_w_in", by_cols=True)
                res[n] = tuple(a.T for a in (g_t, *outs))
            else:
                res[n] = _adamw_halves(args[n][0], args["m_" + n][0], args["v_" + n][0], mine, other, sel_c,
                                       name="adamw_" + n)
        return res[names[-1]][1]

    join_start("mlp", grad_x)
    join_start("mix", grad_x)
    done = finish("mix", finish("mlp", grad_x))

    _, (dmod_own, gs_own), (dmod_all, gs_all) = landed("small", done, "gather8")
    dmod_all = lax.dynamic_update_slice(dmod_all, dmod_own[None], (dev, 0, 0)).reshape(8 * bsz, 6 * D)
    gs_all = lax.dynamic_update_slice(gs_all, gs_own[None], (dev, 0, 0))
    dmod_mine = lax.dynamic_slice(dmod_all, (0, chip * ADA_SHARD[1]), (8 * bsz, ADA_SHARD[1]))
    g_w_ada = _mm(c_all, dmod_mine, ta=True, a_fn=silu, name="ada_dw")
    wmv = [(args[n], args["m_" + n], args["v_" + n]) for n in names_small]
    wmv.append((gla_w_alpha[0], m_gla_w_alpha[0], v_gla_w_alpha[0]))
    res_small, loss_sum = _small_update(gs_all, dmod_all, sel_chip, wmv, name="small_update")
    res.update(res_small)
    loss = loss_sum * (0.5 / D)
    join_start("in", g_w_ada)
    res["w_ada"] = (g_w_ada, *_adamw(w_ada[0], g_w_ada, m_w_ada[0], v_w_ada[0], name="adamw_w_ada"))
    finish("in", res["w_ada"][1])

    order = ["w_ada", "b_ada", "norm1_g", "w_in", "b_merge", "gla_w_alpha", "gla_b_alpha", "gla_out_norm_g", "gla_w_o",
             "mla_q_lat_g", "mla_w_uq", "mla_kv_lat_g", "mla_w_ukv", "mla_qn_g", "mla_kn_g", "mla_w_o", "w_out",
             "norm2_g", "mlp_w1", "mlp_w2"]
    named = lambda k: [res[n][k].reshape(args[n].shape) for n in order]
    return (loss, grad_x, *named(0), *named(1), *named(2), *named(3))
```

```python
import jax
import jax.numpy as jnp
import numpy as np
from jax import lax
from jax.experimental import pallas as pl
from jax.experimental.pallas import tpu as pltpu

F32 = jnp.float32
BF16 = jnp.bfloat16
MESH = pl.DeviceIdType.MESH

D = 1024
CHUNK = 64
EPS = 1e-6
GH, GDK, GDV, GLR, GTAU = 4, 128, 256, 16, 16.0
MH, MQR, MKVR, MNOPE, MROPE, MVD = 16, 256, 128, 64, 32, 64
MQK = MNOPE + MROPE
DFF = 4 * D
ROPE_THETA = 10000.0
IN_WIDTH = 5552
LANE = 128
OFF_Q, OFF_K, OFF_V, OFF_G, OFF_MA, OFF_MB, OFF_CQ, OFF_CKV, OFF_A, OFF_KPE, PW = (
    0, 512, 1024, 2048, 3072, 4096, 5120, 5376, 5504, 5632, 5760)
ADAM_LR, ADAM_B1, ADAM_B2, ADAM_EPS, ADAM_WD, ADAM_STEP = 0.001, 0.9, 0.999, 1e-08, 0.01, 10
VMEM_LIMIT = 48 * 1024 * 1024


def _params(n_axes):
    return pltpu.CompilerParams(dimension_semantics=("arbitrary",) * n_axes, vmem_limit_bytes=VMEM_LIMIT)


def _tile(n, target):
    if n <= target:
        return n
    best = None
    for t in range(LANE, target + 1, LANE):
        if n % t == 0:
            best = t
    assert best is not None, (n, target)
    return best


def _sigmoid(x):
    return 1.0 / (1.0 + jnp.exp(-x))


MM_VMEM_BUDGET = 36 * 1024 * 1024


def _mm(a, b, *, name, ta=False, tb=False, out_dtype=F32, tm=1024, tn=1024, tk=4096,
        epilogue=None, extras=(), extra_specs=(), out_shape=None, out_specs=None, a_fn=None):
    if ta:
        kdim, m = a.shape
    else:
        m, kdim = a.shape
    if tb:
        n, k2 = b.shape
    else:
        k2, n = b.shape
    assert kdim == k2, (a.shape, b.shape)
    tm, tn, tk = _tile(m, tm), _tile(n, tn), _tile(kdim, tk)
    tiles = lambda rows: 2 * (rows * tk * a.dtype.itemsize + tk * tn * b.dtype.itemsize + rows * tn * 4) + rows * tn * 4
    while out_shape is None and tiles(tm) > MM_VMEM_BUDGET and tm % 256 == 0:
        tm //= 2
    nk = kdim // tk
    a_spec = pl.BlockSpec((tk, tm), lambda i, j, k: (k, i)) if ta else pl.BlockSpec((tm, tk), lambda i, j, k: (i, k))
    b_spec = pl.BlockSpec((tn, tk), lambda i, j, k: (j, k)) if tb else pl.BlockSpec((tk, tn), lambda i, j, k: (k, j))
    dims = (((0 if ta else 1,), (1 if tb else 0,)), ((), ()))
    ne = len(extras)
    if out_shape is None:
        out_shape = jax.ShapeDtypeStruct((m, n), out_dtype)
        out_specs = pl.BlockSpec((tm, tn), lambda i, j, k: (i, j))
    n_out = len(out_shape) if isinstance(out_shape, (list, tuple)) else 1
    in_place = epilogue is None and n_out == 1 and out_shape.dtype == F32
    scratch = [] if (nk == 1 or in_place) else [pltpu.VMEM((tm, tn), F32)]

    def body(a_ref, b_ref, *rest):
        ex, outs = rest[:ne], rest[ne:ne + n_out]
        av = a_ref[...] if a_fn is None else a_fn(a_ref[...])
        prod = lax.dot_general(av.astype(BF16), b_ref[...].astype(BF16), dims, preferred_element_type=F32)

        def finish(val):
            if epilogue is None:
                outs[0][...] = val.astype(outs[0].dtype)
            else:
                epilogue(val, ex, outs)

        if nk == 1:
            finish(prod)
            return
        k = pl.program_id(2)
        acc = outs[0] if in_place else rest[-1]

        @pl.when(k == 0)
        def _():
            acc[...] = prod

        @pl.when(k > 0)
        def _():
            acc[...] += prod

        if not in_place:
            @pl.when(k == nk - 1)
            def _():
                finish(acc[...])

    return pl.pallas_call(
        body, name=name, grid=(m // tm, n // tn, nk),
        in_specs=[a_spec, b_spec, *extra_specs], out_specs=out_specs, out_shape=out_shape,
        scratch_shapes=scratch, compiler_params=_params(3),
    )(a, b, *extras)


def _tile_spec(tm, tn):
    return pl.BlockSpec((tm, tn), lambda i, j, k: (i, j))


def _pieces_dx(pieces, w, after, *, name, tm=256):
    t = pieces[0][0].shape[0]
    tm = _tile(t, tm)
    npc = len(pieces)

    def body(*refs):
        p_refs, w_ref, out_ref = refs[:npc], refs[npc], refs[-1]
        acc = None
        for (arr, off), p_ref in zip(pieces, p_refs):
            part = lax.dot_general(p_ref[...].astype(BF16), w_ref[:, off:off + arr.shape[1]], _NT,
                                   preferred_element_type=F32)
            acc = part if acc is None else acc + part
        out_ref[...] = acc

    return pl.pallas_call(
        body, name=name, grid=(t // tm,),
        in_specs=[pl.BlockSpec((tm, arr.shape[1]), lambda i: (i, 0)) for arr, _ in pieces]
        + [pl.BlockSpec(w.shape, lambda i: (0, 0)), pl.BlockSpec((8, LANE), lambda i: (0, 0))],
        out_specs=pl.BlockSpec((tm, w.shape[0]), lambda i: (i, 0)),
        out_shape=jax.ShapeDtypeStruct((t, w.shape[0]), F32), compiler_params=_params(1),
    )(*[arr for arr, _ in pieces], w, after)


def _pieces_dw(h, pieces, *, name, tk=1024):
    t, d = h.shape
    tk = _tile(t, tk)
    widths = [p.shape[1] for p in pieces]
    starts = [sum(widths[:i]) for i in range(len(pieces))]

    def body(h_ref, *refs):
        p_refs, out_ref = refs[:-1], refs[-1]
        first = pl.program_id(0) == 0
        hv = h_ref[...]
        for p_ref, start, width in zip(p_refs, starts, widths):
            part = lax.dot_general(hv, p_ref[...].astype(BF16), _TN, preferred_element_type=F32)
            cols = slice(start, start + width)

            @pl.when(first)
            def _():
                out_ref[:, cols] = part

            @pl.when(jnp.logical_not(first))
            def _():
                out_ref[:, cols] += part

    return pl.pallas_call(
        body, name=name, grid=(t // tk,),
        in_specs=[pl.BlockSpec((tk, d), lambda k: (k, 0))] + [pl.BlockSpec((tk, wd), lambda k: (k, 0)) for wd in widths],
        out_specs=pl.BlockSpec((d, sum(widths)), lambda k: (0, 0)),
        out_shape=jax.ShapeDtypeStruct((d, sum(widths)), F32), compiler_params=_params(1),
    )(h, *pieces)


def _rms(x, g):
    r = lax.rsqrt(jnp.mean(x * x, axis=-1, keepdims=True) + EPS)
    return x * r, r


def _row_spec(ts, width, col=0):
    return pl.BlockSpec((None, ts, width), lambda b, i: (b, i, col))


def _vec_spec(width):
    return pl.BlockSpec((None, 1, width), lambda b, i: (b, 0, 0))


def _gain_spec(width):
    return pl.BlockSpec((1, width), lambda b, i: (0, 0))


def _norm_mod(x, g, scale, shift, *, name, ts=256):
    bsz, s, d = x.shape
    ts = min(ts, s)

    def body(x_ref, g_ref, sc_ref, sh_ref, h_ref):
        xh, _ = _rms(x_ref[...], None)
        h_ref[...] = ((xh * g_ref[...]) * (1.0 + sc_ref[...]) + sh_ref[...]).astype(BF16)

    return pl.pallas_call(
        body, name=name, grid=(bsz, s // ts),
        in_specs=[_row_spec(ts, d), _gain_spec(d), _vec_spec(d), _vec_spec(d)],
        out_specs=_row_spec(ts, d), out_shape=jax.ShapeDtypeStruct((bsz, s, d), BF16),
        compiler_params=_params(2),
    )(x, g, scale, shift)


def _resid_norm_mod(x, mixed, gate, g, scale, shift, *, name, ts=256):
    bsz, s, d = x.shape
    ts = min(ts, s)

    def body(x_ref, mx_ref, gt_ref, g_ref, sc_ref, sh_ref, x1_ref, h_ref):
        x1 = x_ref[...] + gt_ref[...] * mx_ref[...]
        x1_ref[...] = x1
        xh, _ = _rms(x1, None)
        h_ref[...] = ((xh * g_ref[...]) * (1.0 + sc_ref[...]) + sh_ref[...]).astype(BF16)

    return pl.pallas_call(
        body, name=name, grid=(bsz, s // ts),
        in_specs=[_row_spec(ts, d), _row_spec(ts, d), _vec_spec(d), _gain_spec(d), _vec_spec(d), _vec_spec(d)],
        out_specs=[_row_spec(ts, d), _row_spec(ts, d)],
        out_shape=[jax.ShapeDtypeStruct((bsz, s, d), F32), jax.ShapeDtypeStruct((bsz, s, d), BF16)],
        compiler_params=_params(2),
    )(x, mixed, gate, g, scale, shift)


def _norm_mod_bwd(dh, xin, resid, g, scale, gate=None, mixed=None, *, name, ts=256):
    bsz, s, d = xin.shape
    ts = min(ts, s)
    gated = gate is not None

    def body(*refs):
        if gated:
            dh_ref, x_ref, rs_ref, g_ref, sc_ref, gt_ref, mx_ref, dx_ref, dsc_ref, dsh_ref, dg_ref, dgt_ref, dmx_ref = refs
        else:
            dh_ref, x_ref, rs_ref, g_ref, sc_ref, dx_ref, dsc_ref, dsh_ref, dg_ref = refs
        b, i = pl.program_id(0), pl.program_id(1)

        @pl.when(i == 0)
        def _():
            dsc_ref[...] = jnp.zeros_like(dsc_ref)
            dsh_ref[...] = jnp.zeros_like(dsh_ref)
            if gated:
                dgt_ref[...] = jnp.zeros_like(dgt_ref)

        @pl.when((i == 0) & (b == 0))
        def _():
            dg_ref[...] = jnp.zeros_like(dg_ref)

        dh_v, gv = dh_ref[...], g_ref[...]
        xh, r = _rms(x_ref[...], None)
        dsc_ref[...] += jnp.sum(dh_v * (xh * gv), axis=0, keepdims=True)
        dsh_ref[...] += jnp.sum(dh_v, axis=0, keepdims=True)
        dn = dh_v * (1.0 + sc_ref[...])
        dg_ref[...] += jnp.sum(dn * xh, axis=0, keepdims=True)
        dxh = dn * gv
        dx = rs_ref[...] + r * (dxh - xh * jnp.mean(dxh * xh, axis=-1, keepdims=True))
        dx_ref[...] = dx
        if gated:
            dgt_ref[...] += jnp.sum(dx * mx_ref[...], axis=0, keepdims=True)
            dmx_ref[...] = (dx * gt_ref[...]).astype(BF16)

    ins = [dh, xin, resid, g, scale]
    in_specs = [_row_spec(ts, d), _row_spec(ts, d), _row_spec(ts, d), _gain_spec(d), _vec_spec(d)]
    out_specs = [_row_spec(ts, d), _vec_spec(d), _vec_spec(d), _gain_spec(d)]
    out_shape = [jax.ShapeDtypeStruct((bsz, s, d), F32), jax.ShapeDtypeStruct((bsz, 1, d), F32),
                 jax.ShapeDtypeStruct((bsz, 1, d), F32), jax.ShapeDtypeStruct((1, d), F32)]
    if gated:
        ins += [gate, mixed]
        in_specs += [_vec_spec(d), _row_spec(ts, d)]
        out_specs += [_vec_spec(d), _row_spec(ts, d)]
        out_shape += [jax.ShapeDtypeStruct((bsz, 1, d), F32), jax.ShapeDtypeStruct((bsz, s, d), BF16)]
    return pl.pallas_call(
        body, name=name, grid=(bsz, s // ts), in_specs=in_specs, out_specs=out_specs, out_shape=out_shape,
        compiler_params=_params(2),
    )(*ins)


def _loss_head(x1, ff, gate2, target, *, name, ts=256):
    bsz, s, d = x1.shape
    ts = min(ts, s)

    def body(x1_ref, ff_ref, gt_ref, t_ref, dy_ref, dff_ref, dgt_ref, loss_ref, acc):
        b, i = pl.program_id(0), pl.program_id(1)

        @pl.when(i == 0)
        def _():
            dgt_ref[...] = jnp.zeros_like(dgt_ref)

        @pl.when((i == 0) & (b == 0))
        def _():
            acc[...] = jnp.zeros_like(acc)

        ffv, gt = ff_ref[...], gt_ref[...]
        diff = (x1_ref[...] + gt * ffv) - t_ref[...]
        acc[...] += jnp.sum((diff * diff).reshape(ts // 8, 8, d), axis=0)
        dy = diff * (1.0 / d)
        dy_ref[...] = dy
        dgt_ref[...] += jnp.sum(dy * ffv, axis=0, keepdims=True)
        dff_ref[...] = (dy * gt).astype(BF16)

        @pl.when((i == pl.num_programs(1) - 1) & (b == pl.num_programs(0) - 1))
        def _():
            loss_ref[...] = jnp.full(loss_ref.shape, jnp.sum(acc[...]), F32)

    return pl.pallas_call(
        body, name=name, grid=(bsz, s // ts),
        in_specs=[_row_spec(ts, d), _row_spec(ts, d), _vec_spec(d), _row_spec(ts, d)],
        out_specs=[_row_spec(ts, d), _row_spec(ts, d), _vec_spec(d), pl.BlockSpec((8, LANE), lambda b, i: (0, 0))],
        out_shape=[jax.ShapeDtypeStruct((bsz, s, d), F32), jax.ShapeDtypeStruct((bsz, s, d), BF16),
                   jax.ShapeDtypeStruct((bsz, 1, d), F32), jax.ShapeDtypeStruct((8, LANE), F32)],
        scratch_shapes=[pltpu.VMEM((8, d), F32)], compiler_params=_params(2),
    )(x1, ff, gate2, target)


def _merge_fwd(proj, b_merge, y_a, y_b, *, name, ts=256):
    bsz, s, _ = proj.shape
    ts = min(ts, s)

    def body(la_ref, lb_ref, ba_ref, bb_ref, ya_ref, yb_ref, out_ref):
        ga = _sigmoid(la_ref[...] + ba_ref[...])
        gb = _sigmoid(lb_ref[...] + bb_ref[...])
        out_ref[...] = (ga * ya_ref[...] + gb * yb_ref[...]).astype(BF16)

    return pl.pallas_call(
        body, name=name, grid=(bsz, s // ts),
        in_specs=[_row_spec(ts, D, OFF_MA // D), _row_spec(ts, D, OFF_MB // D),
                  pl.BlockSpec((1, D), lambda b, i: (0, 0)), pl.BlockSpec((1, D), lambda b, i: (0, 1)),
                  _row_spec(ts, D), _row_spec(ts, D)],
        out_specs=_row_spec(ts, D), out_shape=jax.ShapeDtypeStruct((bsz, s, D), BF16),
        compiler_params=_params(2),
    )(proj, proj, b_merge, b_merge, y_a, y_b)


def _merge_bwd(dmi, proj, b_merge, y_a, y_b, *, name, ts=256):
    bsz, s, _ = proj.shape
    ts = min(ts, s)

    def body(d_ref, la_ref, lb_ref, ba_ref, bb_ref, ya_ref, yb_ref, dya_ref, dyb_ref, dla_ref, dlb_ref, dba_ref, dbb_ref):
        @pl.when((pl.program_id(0) == 0) & (pl.program_id(1) == 0))
        def _():
            dba_ref[...] = jnp.zeros_like(dba_ref)
            dbb_ref[...] = jnp.zeros_like(dbb_ref)

        dv = d_ref[...]
        ga = _sigmoid(la_ref[...] + ba_ref[...])
        gb = _sigmoid(lb_ref[...] + bb_ref[...])
        dya_ref[...] = (dv * ga).astype(BF16)
        dyb_ref[...] = (dv * gb).astype(BF16)
        dla = (dv * ya_ref[...]) * (ga * (1.0 - ga))
        dlb = (dv * yb_ref[...]) * (gb * (1.0 - gb))
        dla_ref[...] = dla.astype(BF16)
        dlb_ref[...] = dlb.astype(BF16)
        dba_ref[...] += jnp.sum(dla, axis=0, keepdims=True)
        dbb_ref[...] += jnp.sum(dlb, axis=0, keepdims=True)

    act = jax.ShapeDtypeStruct((bsz, s, D), BF16)
    return pl.pallas_call(
        body, name=name, grid=(bsz, s // ts),
        in_specs=[_row_spec(ts, D), _row_spec(ts, D, OFF_MA // D), _row_spec(ts, D, OFF_MB // D),
                  pl.BlockSpec((1, D), lambda b, i: (0, 0)), pl.BlockSpec((1, D), lambda b, i: (0, 1)),
                  _row_spec(ts, D), _row_spec(ts, D)],
        out_specs=[_row_spec(ts, D)] * 4 + [_gain_spec(D)] * 2,
        out_shape=[act, act, act, act, jax.ShapeDtypeStruct((1, D), F32), jax.ShapeDtypeStruct((1, D), F32)],
        compiler_params=_params(2),
    )(dmi, proj, proj, b_merge, b_merge, y_a, y_b)


def _tri(lower):
    r = lax.broadcasted_iota(jnp.int32, (CHUNK, CHUNK), 0)
    c = lax.broadcasted_iota(jnp.int32, (CHUNK, CHUNK), 1)
    return jnp.where((c <= r) if lower else (c >= r), 1.0, 0.0).astype(F32)


def _gla_logits(a_ref, wal_ref, bal_ref):
    logits = jnp.dot(a_ref[...].astype(BF16), wal_ref[...].astype(BF16), preferred_element_type=F32) + bal_ref[...]
    la = (jnp.minimum(logits, 0.0) - jnp.log(1.0 + jnp.exp(-jnp.abs(logits)))) * (1.0 / GTAU)
    return logits, la


def _chunk_cumsum(la_n, tri):
    cum = jnp.dot(tri, la_n, preferred_element_type=F32, precision=lax.Precision.HIGHEST)
    return cum, jnp.sum(la_n, axis=0, keepdims=True)


def _gla_specs(s, nc):
    def blk(width, off):
        return pl.BlockSpec((None, s, width), lambda h, b: (b, 0, off // width + h))

    proj_specs = [blk(GDK, OFF_Q), blk(GDK, OFF_K), blk(GDV, OFF_V), blk(GDV, OFF_G),
                  pl.BlockSpec((None, s, LANE), lambda h, b: (b, 0, OFF_A // LANE)),
                  pl.BlockSpec((LANE, GDK), lambda h, b: (0, h)), pl.BlockSpec((1, GDK), lambda h, b: (0, h)),
                  pl.BlockSpec((1, GDV), lambda h, b: (0, 0))]
    st_spec = pl.BlockSpec((None, None, nc, GDV, GDK), lambda h, b: (b, h, 0, 0, 0))
    return blk, proj_specs, st_spec


def _gla_fwd(proj, w_alpha_p, b_alpha, out_norm_g, *, name):
    bsz, s, _ = proj.shape
    nc = s // CHUNK
    scale = GDK ** -0.5

    rb = min(512, s)

    def body(q_ref, k_ref, v_ref, g_ref, a_ref, wal_ref, bal_ref, ong_ref, o_ref, og_ref, st_ref):
        _, la = _gla_logits(a_ref, wal_ref, bal_ref)
        tri = _tri(True)
        st = jnp.zeros((GDV, GDK), F32)
        for n in range(nc):
            rows = pl.ds(n * CHUNK, CHUNK)
            cum, cum_end = _chunk_cumsum(la[n * CHUNK:(n + 1) * CHUNK], tri)
            kd = k_ref[rows, :] * jnp.exp(cum_end - cum)
            ut = lax.dot_general(v_ref[rows, :].astype(BF16), kd.astype(BF16), _TN, preferred_element_type=F32)
            st = st * jnp.exp(cum_end) + ut
            st_ref[n] = st
            o_ref[rows, :] = lax.dot_general((q_ref[rows, :] * scale).astype(BF16), st.astype(BF16), _NT,
                                             preferred_element_type=F32)
        for j in range(0, s, rb):
            blk_rows = pl.ds(j, rb)
            oh, _ = _rms(o_ref[blk_rows, :], None)
            gv = g_ref[blk_rows, :]
            og_ref[blk_rows, :] = ((oh * ong_ref[...]) * (gv * _sigmoid(gv))).astype(BF16)

    blk, proj_specs, st_spec = _gla_specs(s, nc)
    return pl.pallas_call(
        body, name=name, grid=(GH, bsz), in_specs=proj_specs, out_specs=[blk(GDV, 0), blk(GDV, 0), st_spec],
        out_shape=[jax.ShapeDtypeStruct((bsz, s, GH * GDV), F32), jax.ShapeDtypeStruct((bsz, s, GH * GDV), BF16),
                   jax.ShapeDtypeStruct((bsz, GH, nc, GDV, GDK), F32)],
        compiler_params=_params(2),
    )(proj, proj, proj, proj, proj, w_alpha_p, b_alpha, out_norm_g)


def _gla_bwd(dog, o, states, proj, w_alpha_p, b_alpha, out_norm_g, *, name):
    bsz, s, _ = proj.shape
    nc = s // CHUNK
    scale = GDK ** -0.5

    def body(dog_ref, o_ref, st_ref, q_ref, k_ref, v_ref, g_ref, a_ref, wal_ref, bal_ref, ong_ref,
             dq_ref, dk_ref, dv_ref, dg_ref, dl_ref, dbal_ref, dong_ref, do_scr, dlog_scr):
        h, b = pl.program_id(0), pl.program_id(1)

        @pl.when(b == 0)
        def _():
            dbal_ref[...] = jnp.zeros_like(dbal_ref)

        @pl.when((b == 0) & (h == 0))
        def _():
            dong_ref[...] = jnp.zeros_like(dong_ref)

        ong = ong_ref[...]
        for j in range(0, s, rb):
            blk_rows = pl.ds(j, rb)
            gv, dogv = g_ref[blk_rows, :], dog_ref[blk_rows, :]
            sg = _sigmoid(gv)
            oh, r = _rms(o_ref[blk_rows, :], None)
            don = dogv * (gv * sg)
            dg_ref[blk_rows, :] = (dogv * (oh * ong) * (sg * (1.0 + gv * (1.0 - sg)))).astype(BF16)
            dong_ref[...] += jnp.sum(don * oh, axis=0, keepdims=True)
            doh = don * ong
            do_scr[blk_rows, :] = (r * (doh - oh * jnp.mean(doh * oh, axis=-1, keepdims=True))).astype(BF16)

        logits, la = _gla_logits(a_ref, wal_ref, bal_ref)
        tri_lo, tri_up = _tri(True), _tri(False)
        carry = jnp.zeros((GDV, GDK), F32)
        for n in range(nc - 1, -1, -1):
            rows = pl.ds(n * CHUNK, CHUNK)
            cum, cum_end = _chunk_cumsum(la[n * CHUNK:(n + 1) * CHUNK], tri_lo)
            decay = jnp.exp(cum_end)
            w = jnp.exp(cum_end - cum)
            kd = k_ref[rows, :] * w
            do_b = do_scr[rows, :]
            qs_b = (q_ref[rows, :] * scale).astype(BF16)
            dq_ref[rows, :] = (jnp.dot(do_b, st_ref[n].astype(BF16), preferred_element_type=F32) * scale).astype(BF16)
            dsn = lax.dot_general(do_b, qs_b, _TN, preferred_element_type=F32) + carry
            carry = dsn * decay
            dsn_b = dsn.astype(BF16)
            dv_ref[rows, :] = lax.dot_general(kd.astype(BF16), dsn_b, _NT, preferred_element_type=F32).astype(BF16)
            dkd = jnp.dot(v_ref[rows, :].astype(BF16), dsn_b, preferred_element_type=F32)
            dk_ref[rows, :] = (dkd * w).astype(BF16)
            e = dkd * kd
            dcum_end = jnp.sum(e, axis=0, keepdims=True)
            if n > 0:
                dcum_end += jnp.sum(dsn * st_ref[n - 1], axis=0, keepdims=True) * decay
            dlog_scr[rows, :] = dcum_end - jnp.dot(tri_up, e, preferred_element_type=F32,
                                                  precision=lax.Precision.HIGHEST)
        dlog = dlog_scr[...] * (1.0 / GTAU) * (1.0 - _sigmoid(logits))
        dl_ref[...] = dlog.astype(BF16)
        dbal_ref[...] += jnp.sum(dlog, axis=0, keepdims=True)

    rb = min(512, s)

    blk, proj_specs, st_spec = _gla_specs(s, nc)
    act = lambda wd: jax.ShapeDtypeStruct((bsz, s, wd), BF16)
    return pl.pallas_call(
        body, name=name, grid=(GH, bsz), in_specs=[blk(GDV, 0), blk(GDV, 0), st_spec, *proj_specs],
        out_specs=[blk(GDK, 0), blk(GDK, 0), blk(GDV, 0), blk(GDV, 0), blk(GDK, 0),
                   pl.BlockSpec((1, GDK), lambda h, b: (0, h)), pl.BlockSpec((1, GDV), lambda h, b: (0, 0))],
        out_shape=[act(GH * GDK), act(GH * GDK), act(GH * GDV), act(GH * GDV), act(GH * GDK),
                   jax.ShapeDtypeStruct((1, GH * GDK), F32), jax.ShapeDtypeStruct((1, GDV), F32)],
        scratch_shapes=[pltpu.VMEM((s, GDV), BF16), pltpu.VMEM((s, GDK), F32)], compiler_params=_params(2),
    )(dog, o, states, proj, proj, proj, proj, proj, w_alpha_p, b_alpha, out_norm_g)


def _lane():
    return lax.broadcasted_iota(jnp.int32, (1, LANE), 1)


def _swap_halves(x):
    lane = _lane()
    half = MROPE // 2
    lo = (lane >= MNOPE) & (lane < MNOPE + half)
    hi = (lane >= MNOPE + half) & (lane < MQK)
    return jnp.where(lo, pltpu.roll(x, LANE - half, 1), jnp.where(hi, pltpu.roll(x, half, 1), 0.0))


def _norm96(x, g):
    r = lax.rsqrt(jnp.sum(x * x, axis=-1, keepdims=True) * (1.0 / MQK) + EPS)
    return x * r, r


def _lat_norm(proj, q_lat_g, kv_lat_g, *, name, ts=512):
    t = proj.shape[0]
    ts = min(ts, t)

    def body(cq_ref, ckv_ref, gq_ref, gk_ref, oq_ref, ok_ref):
        xq, _ = _rms(cq_ref[...], None)
        oq_ref[...] = (xq * gq_ref[...]).astype(BF16)
        xk, _ = _rms(ckv_ref[...], None)
        ok_ref[...] = (xk * gk_ref[...]).astype(BF16)

    return pl.pallas_call(
        body, name=name, grid=(t // ts,),
        in_specs=[pl.BlockSpec((ts, MQR), lambda i: (i, OFF_CQ // MQR)), pl.BlockSpec((ts, MKVR), lambda i: (i, OFF_CKV // MKVR)),
                  pl.BlockSpec((1, MQR), lambda i: (0, 0)), pl.BlockSpec((1, MKVR), lambda i: (0, 0))],
        out_specs=[pl.BlockSpec((ts, MQR), lambda i: (i, 0)), pl.BlockSpec((ts, MKVR), lambda i: (i, 0))],
        out_shape=[jax.ShapeDtypeStruct((t, MQR), BF16), jax.ShapeDtypeStruct((t, MKVR), BF16)],
        compiler_params=_params(1),
    )(proj, proj, q_lat_g, kv_lat_g)


def _lat_norm_bwd(dcqn, dckvn, proj, q_lat_g, kv_lat_g, *, name, ts=512):
    t = proj.shape[0]
    ts = min(ts, t)

    def one(d_ref, x_ref, g_ref, dx_ref, dg_ref):
        xh, r = _rms(x_ref[...], None)
        dn = d_ref[...]
        dg_ref[...] += jnp.sum(dn * xh, axis=0, keepdims=True)
        dxh = dn * g_ref[...]
        dx_ref[...] = (r * (dxh - xh * jnp.mean(dxh * xh, axis=-1, keepdims=True))).astype(BF16)

    def body(dq_ref, dk_ref, cq_ref, ckv_ref, gq_ref, gk_ref, dxq_ref, dxk_ref, dgq_ref, dgk_ref):
        @pl.when(pl.program_id(0) == 0)
        def _():
            dgq_ref[...] = jnp.zeros_like(dgq_ref)
            dgk_ref[...] = jnp.zeros_like(dgk_ref)

        one(dq_ref, cq_ref, gq_ref, dxq_ref, dgq_ref)
        one(dk_ref, ckv_ref, gk_ref, dxk_ref, dgk_ref)

    return pl.pallas_call(
        body, name=name, grid=(t // ts,),
        in_specs=[pl.BlockSpec((ts, MQR), lambda i: (i, 0)), pl.BlockSpec((ts, MKVR), lambda i: (i, 0)),
                  pl.BlockSpec((ts, MQR), lambda i: (i, OFF_CQ // MQR)), pl.BlockSpec((ts, MKVR), lambda i: (i, OFF_CKV // MKVR)),
                  pl.BlockSpec((1, MQR), lambda i: (0, 0)), pl.BlockSpec((1, MKVR), lambda i: (0, 0))],
        out_specs=[pl.BlockSpec((ts, MQR), lambda i: (i, 0)), pl.BlockSpec((ts, MKVR), lambda i: (i, 0)),
                   pl.BlockSpec((1, MQR), lambda i: (0, 0)), pl.BlockSpec((1, MKVR), lambda i: (0, 0))],
        out_shape=[jax.ShapeDtypeStruct((t, MQR), BF16), jax.ShapeDtypeStruct((t, MKVR), BF16),
                   jax.ShapeDtypeStruct((1, MQR), F32), jax.ShapeDtypeStruct((1, MKVR), F32)],
        compiler_params=_params(1),
    )(dcqn, dckvn, proj, proj, q_lat_g, kv_lat_g)


def _qk_prep(q_raw, kv, proj, cos_t, sin_t, gq, gk, *, name, ts=2048):
    t = q_raw.shape[0]
    ts = min(ts, t)

    def body(q_ref, kv_ref, kpe_ref, c_ref, s_ref, gq_ref, gk_ref, qo_ref, ko_ref, vo_ref):
        cs, sn = c_ref[...], s_ref[...]
        nope = _lane() < MNOPE
        qn, _ = _norm96(q_ref[...], None)
        qn = qn * gq_ref[...]
        qo_ref[...] = (qn * cs + _swap_halves(qn) * sn).astype(BF16)
        kvv = kv_ref[...]
        kn, _ = _norm96(jnp.where(nope, kvv, kpe_ref[...]), None)
        kn = kn * gk_ref[...]
        ko_ref[...] = (kn * cs + _swap_halves(kn) * sn).astype(BF16)
        vo_ref[...] = jnp.where(nope, pltpu.roll(kvv, MNOPE, 1), 0.0).astype(BF16)

    hd = pl.BlockSpec((ts, LANE), lambda i, h: (i, h))
    shared = lambda col: pl.BlockSpec((ts, LANE), lambda i, h: (i, col))
    gain = pl.BlockSpec((1, LANE), lambda i, h: (0, 0))
    out = jax.ShapeDtypeStruct((t, MH * LANE), BF16)
    return pl.pallas_call(
        body, name=name, grid=(t // ts, MH),
        in_specs=[hd, hd, shared(OFF_KPE // LANE), shared(0), shared(0), gain, gain],
        out_specs=[hd, hd, hd], out_shape=[out, out, out], compiler_params=_params(2),
    )(q_raw, kv, proj, cos_t, sin_t, gq, gk)


def _qk_prep_bwd(dq, dk, dv, q_raw, kv, proj, cos_t, sin_t, gq, gk, *, name, ts=2048):
    t = q_raw.shape[0]
    ts = min(ts, t)

    def norm_bwd(dy, x, g, dg_ref):
        xh, r = _norm96(x, None)
        dg_ref[...] += jnp.sum(dy * xh, axis=0, keepdims=True)
        dxh = dy * g
        return r * (dxh - xh * (jnp.sum(dxh * xh, axis=-1, keepdims=True) * (1.0 / MQK)))

    def body(dq_ref, dk_ref, dv_ref, q_ref, kv_ref, kpe_ref, c_ref, s_ref, gq_ref, gk_ref,
             dqr_ref, dkv_ref, dkpe_ref, dgq_ref, dgk_ref):
        i, h = pl.program_id(0), pl.program_id(1)

        @pl.when(h == 0)
        def _():
            dkpe_ref[...] = jnp.zeros_like(dkpe_ref)

        @pl.when((h == 0) & (i == 0))
        def _():
            dgq_ref[...] = jnp.zeros_like(dgq_ref)
            dgk_ref[...] = jnp.zeros_like(dgk_ref)

        cs, sn = c_ref[...], s_ref[...]
        lane = _lane()
        nope = lane < MNOPE
        dqv = dq_ref[...]
        dqn = dqv * cs + _swap_halves(dqv * sn)
        dqr_ref[...] = norm_bwd(dqn, q_ref[...], gq_ref[...], dgq_ref).astype(BF16)
        dkv_ = dk_ref[...]
        dkn = dkv_ * cs + _swap_halves(dkv_ * sn)
        kvv = kv_ref[...]
        dkr = norm_bwd(dkn, jnp.where(nope, kvv, kpe_ref[...]), gk_ref[...], dgk_ref)
        dkv_ref[...] = jnp.where(nope, dkr, pltpu.roll(dv_ref[...], MNOPE, 1)).astype(BF16)
        dkpe_ref[...] += jnp.where((lane >= MNOPE) & (lane < MQK), dkr, 0.0)

    hd = pl.BlockSpec((ts, LANE), lambda i, h: (i, h))
    shared = lambda col: pl.BlockSpec((ts, LANE), lambda i, h: (i, col))
    gain = pl.BlockSpec((1, LANE), lambda i, h: (0, 0))
    out = jax.ShapeDtypeStruct((t, MH * LANE), BF16)
    return pl.pallas_call(
        body, name=name, grid=(t // ts, MH),
        in_specs=[hd, hd, hd, hd, hd, shared(OFF_KPE // LANE), shared(0), shared(0), gain, gain],
        out_specs=[hd, hd, shared(0), gain, gain],
        out_shape=[out, out, jax.ShapeDtypeStruct((t, LANE), F32), jax.ShapeDtypeStruct((1, LANE), F32),
                   jax.ShapeDtypeStruct((1, LANE), F32)],
        compiler_params=_params(2),
    )(dq, dk, dv, q_raw, kv, proj, cos_t, sin_t, gq, gk)


_NT = (((1,), (1,)), ((), ()))
_TN = (((0,), (0,)), ((), ()))


SOFTMAX_SCALE = MQK ** -0.5
Q_PRESCALE = SOFTMAX_SCALE * float(np.log2(np.e))


def _attn_weights(q, k_ref, lo, tq):
    row = lax.broadcasted_iota(jnp.int32, (tq, tq), 0) // CHUNK
    col = lax.broadcasted_iota(jnp.int32, (tq, tq), 1) // CHUNK
    sd = lax.dot_general(q, k_ref[pl.ds(lo, tq), :], _NT, preferred_element_type=F32)
    sd = jnp.where(col <= row, sd, -1e30)
    m = jnp.max(sd, axis=-1, keepdims=True)
    if lo:
        so = lax.dot_general(q, k_ref[pl.ds(0, lo), :], _NT, preferred_element_type=F32)
        m = jnp.maximum(m, jnp.max(so, axis=-1, keepdims=True))
        eo = jnp.exp2(so - m)
        ed = jnp.exp2(sd - m)
        return eo, ed, 1.0 / (jnp.sum(eo, axis=-1, keepdims=True) + jnp.sum(ed, axis=-1, keepdims=True))
    ed = jnp.exp2(sd - m)
    return None, ed, 1.0 / jnp.sum(ed, axis=-1, keepdims=True)


def _attn_fwd(q, k, v, *, name, tq=256):
    bsz, s, _ = q.shape
    tq = min(tq, s)

    def body(q_ref, k_ref, v_ref, o_ref):
        for i in range(s // tq):
            lo = i * tq
            eo, ed, inv = _attn_weights(q_ref[pl.ds(lo, tq), :], k_ref, lo, tq)
            o = jnp.dot(ed.astype(BF16), v_ref[pl.ds(lo, tq), :], preferred_element_type=F32)
            if lo:
                o += jnp.dot(eo.astype(BF16), v_ref[pl.ds(0, lo), :], preferred_element_type=F32)
            o_ref[pl.ds(lo, tq), :] = (o * inv).astype(BF16)

    spec = pl.BlockSpec((None, s, LANE), lambda b, h: (b, 0, h))
    return pl.pallas_call(
        body, name=name, grid=(bsz, MH), in_specs=[spec, spec, spec], out_specs=spec,
        out_shape=jax.ShapeDtypeStruct((bsz, s, MH * LANE), BF16), compiler_params=_params(2),
    )(q, k, v)


def _attn_bwd(q, k, v, do, *, name, tq=256):
    bsz, s, _ = q.shape
    tq = min(tq, s)

    def body(q_ref, k_ref, v_ref, do_ref, dq_ref, dk_ref, dv_ref):
        dk_ref[...] = jnp.zeros_like(dk_ref)
        dv_ref[...] = jnp.zeros_like(dv_ref)
        for i in range(s // tq):
            lo = i * tq
            here, before = pl.ds(lo, tq), pl.ds(0, lo)
            qv, dov = q_ref[here, :], do_ref[here, :]
            eo, ed, inv = _attn_weights(qv, k_ref, lo, tq)
            do_n = (dov.astype(F32) * inv).astype(BF16)
            dv_ref[here, :] += lax.dot_general(ed.astype(BF16), do_n, _TN, preferred_element_type=F32)
            dpd = lax.dot_general(dov, v_ref[here, :], _NT, preferred_element_type=F32)
            delta = jnp.sum(dpd * ed, axis=-1, keepdims=True)
            if lo:
                dv_ref[before, :] += lax.dot_general(eo.astype(BF16), do_n, _TN, preferred_element_type=F32)
                dpo = lax.dot_general(dov, v_ref[before, :], _NT, preferred_element_type=F32)
                delta += jnp.sum(dpo * eo, axis=-1, keepdims=True)
            delta = delta * inv
            r = inv * SOFTMAX_SCALE
            dsd = (ed * (dpd - delta) * r).astype(BF16)
            dq = jnp.dot(dsd, k_ref[here, :], preferred_element_type=F32)
            dk_ref[here, :] += lax.dot_general(dsd, qv, _TN, preferred_element_type=F32)
            if lo:
                dso = (eo * (dpo - delta) * r).astype(BF16)
                dq += jnp.dot(dso, k_ref[before, :], preferred_element_type=F32)
                dk_ref[before, :] += lax.dot_general(dso, qv, _TN, preferred_element_type=F32)
            dq_ref[here, :] = dq
        dk_ref[...] = dk_ref[...] * (1.0 / Q_PRESCALE)

    spec = pl.BlockSpec((None, s, LANE), lambda b, h: (b, 0, h))
    out = jax.ShapeDtypeStruct((bsz, s, MH * LANE), F32)
    return pl.pallas_call(
        body, name=name, grid=(bsz, MH), in_specs=[spec] * 4, out_specs=[spec] * 3, out_shape=[out, out, out],
        compiler_params=_params(2),
    )(q, k, v, do)


def _adamw(w, g, m, v, *, name, tr=256, by_cols=False):
    rows, cols = w.shape
    tr = _tile_rows(rows, tr)

    def body(w_ref, g_ref, m_ref, v_ref, d_ref, nm_ref, nv_ref):
        d_ref[...], nm_ref[...], nv_ref[...] = _adamw_update(w_ref[...], g_ref[...], m_ref[...], v_ref[...])

    spec = pl.BlockSpec((rows, LANE), lambda i: (0, i)) if by_cols else pl.BlockSpec((tr, cols), lambda i: (i, 0))
    out = jax.ShapeDtypeStruct((rows, cols), F32)
    return pl.pallas_call(body, name=name, grid=(cols // LANE if by_cols else rows // tr,), in_specs=[spec] * 4,
                          out_specs=[spec] * 3, out_shape=[out, out, out], compiler_params=_params(1))(w, g, m, v)


def _tile_rows(rows, target):
    if rows <= target:
        return rows
    best = 8
    for t in range(8, target + 1, 8):
        if rows % t == 0:
            best = t
    return best


def _adamw_update(w, g, m, v):
    nm = ADAM_B1 * m + (1.0 - ADAM_B1) * g
    nv = ADAM_B2 * v + (1.0 - ADAM_B2) * (g * g)
    m_hat = nm / (1.0 - ADAM_B1 ** ADAM_STEP)
    v_hat = nv / (1.0 - ADAM_B2 ** ADAM_STEP)
    return -ADAM_LR * (m_hat / (jnp.sqrt(v_hat) + ADAM_EPS) + ADAM_WD * w), nm, nv


def _adamw_halves(w, m, v, mine, theirs, sel, *, name, tr=256):
    rows, cols = w.shape
    tr = _tile_rows(rows // 2, tr)
    nh = rows // 2 // tr

    def body(sel_ref, w_ref, m_ref, v_ref, mine_ref, theirs_ref, g_ref, d_ref, nm_ref, nv_ref):
        lower = pl.program_id(0) < nh
        south = sel_ref[0] == 0
        gv = jnp.where(lower == south, mine_ref[...], theirs_ref[...])
        g_ref[...] = gv
        d_ref[...], nm_ref[...], nv_ref[...] = _adamw_update(w_ref[...], gv, m_ref[...], v_ref[...])

    full = pl.BlockSpec((tr, cols), lambda i, sel_ref: (i, 0))
    half = pl.BlockSpec((tr, cols), lambda i, sel_ref: (i % nh, 0))
    out = jax.ShapeDtypeStruct((rows, cols), F32)
    return pl.pallas_call(
        body, name=name, out_shape=[out] * 4, compiler_params=_params(1),
        grid_spec=pltpu.PrefetchScalarGridSpec(num_scalar_prefetch=1, grid=(rows // tr,),
                                               in_specs=[full, full, full, half, half], out_specs=[full] * 4),
    )(sel, w, m, v, mine, theirs)


def _pair_add(x, sib, sel, *, name, tr=256):
    n, _, rows, cols = x.shape
    tr = _tile_rows(rows, tr)

    def body(sel_ref, x_ref, s_ref, o_ref):
        o_ref[...] = (x_ref[...] + s_ref[...]).astype(BF16)

    spec = pl.BlockSpec((None, tr, cols), lambda j, i, sel_ref: (j, i, 0))
    return pl.pallas_call(
        body, name=name, out_shape=jax.ShapeDtypeStruct((n, rows, cols), BF16), compiler_params=_params(2),
        grid_spec=pltpu.PrefetchScalarGridSpec(
            num_scalar_prefetch=1, grid=(n, rows // tr),
            in_specs=[pl.BlockSpec((None, None, tr, cols), lambda j, i, sel_ref: (j, sel_ref[0], i, 0)), spec],
            out_specs=spec),
    )(sel, x, sib)


def _chip_sum(pair, recv, sel, *, name, tr=256):
    _, rows, cols = pair.shape
    tr = _tile_rows(rows, tr)

    def body(sel_ref, p_ref, r_ref, o_ref):
        acc = p_ref[...].astype(F32)
        for k in range(3):
            acc = acc + r_ref[k].astype(F32)
        o_ref[...] = acc

    return pl.pallas_call(
        body, name=name, out_shape=jax.ShapeDtypeStruct((rows, cols), F32), compiler_params=_params(1),
        grid_spec=pltpu.PrefetchScalarGridSpec(
            num_scalar_prefetch=1, grid=(rows // tr,),
            in_specs=[pl.BlockSpec((None, tr, cols), lambda i, sel_ref: (sel_ref[0], i, 0)),
                      pl.BlockSpec((3, tr, cols), lambda i, sel_ref: (0, i, 0))],
            out_specs=pl.BlockSpec((tr, cols), lambda i, sel_ref: (i, 0))),
    )(sel, pair, recv)


def _me():
    return lax.axis_index("x"), lax.axis_index("y"), lax.axis_index("c")


def _flip(pos, bits):
    x, y, c = pos
    return (x ^ bits[0] if bits[0] else x, y ^ bits[1] if bits[1] else y, c ^ bits[2] if bits[2] else c)


ANY = pl.BlockSpec(memory_space=pl.ANY)


def _all_gather8(xs, *, name):
    n = len(xs)
    flips = [((k >> 2) & 1, (k >> 1) & 1, k & 1) for k in range(1, 8)]

    def body(*refs):
        x_refs, out_refs, (send_sems, recv_sems, local_sems) = refs[:n], refs[n:2 * n], refs[2 * n:]
        me = _me()
        slot = lambda p: 4 * p[0] + 2 * p[1] + p[2]
        copies = []
        for i in range(n):
            mine = pltpu.make_async_copy(x_refs[i], out_refs[i].at[slot(me)], local_sems.at[i])
            mine.start()
            copies.append(mine)
            for k, f in enumerate(flips):
                peer = _flip(me, f)
                sems = dict(send_sem=send_sems.at[7 * i + k], recv_sem=recv_sems.at[7 * i + k], device_id=peer,
                            device_id_type=MESH)
                cp = pltpu.make_async_remote_copy(src_ref=x_refs[i], dst_ref=out_refs[i].at[slot(me)], **sems)
                cp.start()
                copies.append(cp)
                copies.append(pltpu.make_async_remote_copy(src_ref=x_refs[i], dst_ref=out_refs[i].at[slot(peer)], **sems))
        for i in range(n):
            base = i * 15
            copies[base].wait()
            for k in range(7):
                copies[base + 1 + 2 * k].wait_send()
                copies[base + 2 + 2 * k].wait_recv()

    outs = pl.pallas_call(
        body, name=name, in_specs=[ANY] * n, out_specs=[ANY] * n,
        out_shape=[jax.ShapeDtypeStruct((8, *x.shape), x.dtype) for x in xs],
        scratch_shapes=[pltpu.SemaphoreType.DMA((7 * n,)), pltpu.SemaphoreType.DMA((7 * n,)),
                        pltpu.SemaphoreType.DMA((n,))])(*xs)
    return list(outs)


CHIP_FLIPS = [(1, 0, 0), (0, 1, 0), (1, 1, 0)]


def _chip():
    return 2 * lax.axis_index("x") + lax.axis_index("y")


HBM = pl.BlockSpec(memory_space=pltpu.HBM)
SEM = pl.BlockSpec(memory_space=pltpu.SEMAPHORE)
EFFECT = pltpu.SideEffectType.DATAFLOW_SIDE_EFFECTING


def _plan_copies(plan, refs, send_sems, recv_sems):
    return [pltpu.make_async_remote_copy(src_ref=src, dst_ref=dst, send_sem=send_sems.at[k], recv_sem=recv_sems.at[k],
                                         device_id=to, device_id_type=MESH) for k, (src, dst, to) in enumerate(plan(refs))]


def _rdma_start(arrays, n_copies, plan, deps, *, name):
    n, nd = len(arrays), len(deps)

    def body(*refs):
        for cp in _plan_copies(plan, refs[:n], refs[n + nd], refs[n + nd + 1]):
            cp.start()
        refs[-1][...] = jnp.zeros_like(refs[-1])

    outs = pl.pallas_call(
        body, name=name,
        out_shape=(pltpu.SemaphoreType.DMA((n_copies,)), pltpu.SemaphoreType.DMA((n_copies,)),
                   *[pltpu.HBM(a.shape, a.dtype) for a in arrays], jax.ShapeDtypeStruct((8, LANE), F32)),
        in_specs=[HBM] * n + [ANY] * nd, out_specs=(SEM, SEM, *[HBM] * n, pl.BlockSpec(memory_space=pltpu.VMEM)),
        input_output_aliases={i: i + 2 for i in range(n)}, compiler_params=pltpu.CompilerParams(has_side_effects=EFFECT),
    )(*[pltpu.with_memory_space_constraint(a, pltpu.HBM) for a in arrays], *deps)
    return outs[0], outs[1], list(outs[2:2 + n]), outs[-1]


def _rdma_wait(send_sems, recv_sems, arrays, plan, after, *, name):
    n = len(arrays)

    def body(*refs):
        for cp in _plan_copies(plan, refs[:n], refs[n], refs[n + 1]):
            cp.wait_send()
            cp.wait_recv()

    return list(pl.pallas_call(
        body, name=name, out_shape=tuple(pltpu.HBM(a.shape, a.dtype) for a in arrays),
        in_specs=[HBM] * n + [SEM, SEM, ANY], out_specs=tuple([HBM] * n), input_output_aliases={i: i for i in range(n)},
        compiler_params=pltpu.CompilerParams(has_side_effects=EFFECT),
    )(*arrays, send_sems, recv_sems, after))


def _gather_plan(n):
    def plan(refs):
        me = _me()
        slot = 2 * me[0] + me[1]
        return [(refs[i].at[me[2]], refs[n + i].at[slot, me[2]], _flip(me, f)) for i in range(n) for f in CHIP_FLIPS]
    return plan


def _scatter_plan(n):
    def plan(refs):
        me = _me()
        out = []
        for i in range(n):
            for k, f in enumerate(CHIP_FLIPS):
                peer = _flip(me, f)
                out.append((refs[i].at[2 * peer[0] + peer[1]], refs[n + i].at[k], peer))
        return out
    return plan


def _sibling_plan(n, src_of):
    def plan(refs):
        me = _me()
        return [(src_of(refs[i], me[2]), refs[n + i], _flip(me, (0, 0, 1))) for i in range(n)]
    return plan


def _gather8_plan(n):
    def plan(refs):
        me = _me()
        slot = 4 * me[0] + 2 * me[1] + me[2]
        return [(refs[i], refs[n + i].at[slot], _flip(me, ((k >> 2) & 1, (k >> 1) & 1, k & 1)))
                for i in range(n) for k in range(1, 8)]
    return plan


def _pair_fill(lands, *, name):
    n = len(lands)

    def body(*refs):
        in_refs, (send_sems, recv_sems) = refs[:n], refs[2 * n:]
        me = _me()
        sib = _flip(me, (0, 0, 1))
        copies = []
        for i in range(n):
            for k, f in enumerate(CHIP_FLIPS):
                peer = _flip(me, f)
                slot = 2 * peer[0] + peer[1]
                mine, theirs = in_refs[i].at[slot, me[2]], in_refs[i].at[slot, 1 - me[2]]
                cp = pltpu.make_async_remote_copy(src_ref=mine, dst_ref=mine, send_sem=send_sems.at[3 * i + k],
                                                  recv_sem=recv_sems.at[3 * i + k], device_id=sib, device_id_type=MESH)
                cp.start()
                copies.append((cp, pltpu.make_async_remote_copy(
                    src_ref=mine, dst_ref=theirs, send_sem=send_sems.at[3 * i + k], recv_sem=recv_sems.at[3 * i + k],
                    device_id=sib, device_id_type=MESH)))
        for cp, arrival in copies:
            arrival.wait_recv()
            cp.wait_send()

    return list(pl.pallas_call(
        body, name=name, in_specs=[ANY] * n, out_specs=[ANY] * n,
        out_shape=[jax.ShapeDtypeStruct(a.shape, a.dtype) for a in lands], input_output_aliases={i: i for i in range(n)},
        scratch_shapes=[pltpu.SemaphoreType.DMA((3 * n,)), pltpu.SemaphoreType.DMA((3 * n,))])(*lands))


def _own_and_landed(lands, xs):
    chip = _chip()
    return [[jnp.where(chip == j, x, o.reshape(4, *x.shape)[j]) for j in range(4)] for o, x in zip(lands, xs)]


BIG = (("w_in", (D, IN_WIDTH // 4), 1), ("gla_w_o", (D // 4, D), 0), ("mla_w_uq", (MQR, MH * MQK // 4), 1),
       ("mla_w_ukv", (MKVR, MH * (MNOPE + MVD) // 4), 1), ("mla_w_o", (D // 4, D), 0), ("w_out", (D // 4, D), 0),
       ("mlp_w1", (D, DFF // 4), 1), ("mlp_w2", (DFF // 4, D), 0))
ADA_SHARD = (D, 6 * D // 4)
SMALL = (("b_ada", 6 * D), ("norm1_g", D), ("b_merge", 2 * D), ("gla_b_alpha", GH * GDK), ("gla_out_norm_g", GDV),
         ("mla_q_lat_g", MQR), ("mla_kv_lat_g", MKVR), ("mla_qn_g", MQK), ("mla_kn_g", MQK), ("norm2_g", D))


W_IN_SEGMENTS = ((0, 3072, OFF_Q), (3072, 3088, OFF_A), (3088, 3344, OFF_CQ), (3344, 3472, OFF_CKV),
                 (3472, 3504, OFF_KPE + MNOPE), (3504, 5552, OFF_MA))
W_IN_SPLIT = OFF_MA
SMALL_ROWS, SMALL_COLS = 32, 2 * D
W_ALPHA_ROW = 16
LOSS_ROW = 15
SMALL_RED = tuple((n, k) for n, k in SMALL if n != "b_ada")


def _pack_small(grads, d_w_alpha, loss_row, *, name):
    def body(*refs):
        g_refs, wa_ref, loss_ref, out_ref = refs[:-3], refs[-3], refs[-2], refs[-1]
        out_ref[...] = jnp.zeros_like(out_ref)
        for i, ((_, k), g_ref) in enumerate(zip(SMALL_RED, g_refs)):
            out_ref[i:i + 1, 0:k] = g_ref[...]
        out_ref[LOSS_ROW:LOSS_ROW + 1, 0:LANE] = loss_ref[...]
        out_ref[W_ALPHA_ROW:W_ALPHA_ROW + GLR, 0:GH * GDK] = wa_ref[...]

    return pl.pallas_call(body, name=name, out_shape=jax.ShapeDtypeStruct((SMALL_ROWS, SMALL_COLS), F32))(
        *grads, d_w_alpha, loss_row)


def _small_update(gathered, dmod_all, sel, wmv, *, name):
    names = [n for n, _ in SMALL] + ["gla_w_alpha"]
    n_par = len(names)

    def body(sel_ref, g_ref, dmod_ref, *refs):
        in_refs, out_refs, loss_ref, acc = refs[:3 * n_par], refs[3 * n_par:-2], refs[-2], refs[-1]
        total = g_ref[0]
        for j in range(1, 8):
            total = total + g_ref[j]
        acc[...] = total
        loss_ref[...] = acc[LOSS_ROW:LOSS_ROW + 1, 0:LANE]
        row = {n: i for i, (n, _) in enumerate(SMALL_RED)}
        for p, name_p in enumerate(names):
            w_ref, m_ref, v_ref = in_refs[3 * p:3 * p + 3]
            if name_p == "b_ada":
                gv = jnp.sum(dmod_ref[...], axis=0, keepdims=True)
            elif name_p == "gla_w_alpha":
                gv = jnp.zeros((GLR, GDK), F32)
                for j in range(4):
                    blk = acc[W_ALPHA_ROW:W_ALPHA_ROW + GLR, j * GDK:(j + 1) * GDK]
                    gv = gv + jnp.where(sel_ref[0] == j, blk, 0.0)
            else:
                gv = acc[row[name_p]:row[name_p] + 1, 0:w_ref.shape[1]]
            o = out_refs[4 * p:4 * p + 4]
            o[0][...] = gv
            o[1][...], o[2][...], o[3][...] = _adamw_update(w_ref[...], gv, m_ref[...], v_ref[...])

    flat = [a for t in wmv for a in t]
    out_shape = [jax.ShapeDtypeStruct(t[0].shape, F32) for t in wmv for _ in range(4)]
    out_shape.append(jax.ShapeDtypeStruct((1, LANE), F32))
    vmem = pl.BlockSpec(memory_space=pltpu.VMEM)
    outs = pl.pallas_call(
        body, name=name, out_shape=out_shape, in_specs=[pl.BlockSpec(memory_space=pltpu.SMEM), vmem, vmem] + [vmem] * len(flat),
        out_specs=[vmem] * len(out_shape), scratch_shapes=[pltpu.VMEM((SMALL_ROWS, SMALL_COLS), F32)],
    )(sel, gathered, dmod_all, *flat)
    return {n: tuple(outs[4 * p:4 * p + 4]) for p, n in enumerate(names)}, outs[-1][0, 0]


def _full_weights(gathered):
    w = {name: jnp.concatenate(gathered[name], axis=axis) for name, _, axis in BIG if name in gathered and name != "w_in"}
    if "w_in" in gathered:
        shards = gathered["w_in"]
        zeros = lambda n: [jnp.zeros((D, n), shards[0].dtype)]

        def cols(a, b):
            width = IN_WIDTH // 4
            return [shards[j][:, max(a, j * width) - j * width:min(b, (j + 1) * width) - j * width]
                    for j in range(4) if max(a, j * width) < min(b, (j + 1) * width)]

        parts = []
        for a, b, at in sorted(W_IN_SEGMENTS, key=lambda seg: seg[2]):
            have = sum(p.shape[1] for p in parts)
            parts += (zeros(at - have) if at > have else []) + cols(a, b)
        w["w_in"] = jnp.concatenate(parts + zeros(PW - sum(p.shape[1] for p in parts)), axis=1)
    if "mla_w_uq" in w:
        w["mla_w_uq"] = jnp.pad(w["mla_w_uq"].reshape(MQR, MH, MQK), ((0, 0), (0, 0), (0, LANE - MQK))).reshape(MQR, MH * LANE)
    if "mla_w_o" in w:
        w["mla_w_o"] = jnp.pad(w["mla_w_o"].reshape(MH, MVD, D), ((0, 0), (0, LANE - MVD), (0, 0))).reshape(MH * LANE, D)
    return w


def _grad_slots(g):
    g = dict(g)
    out = {}
    if "w_in" in g:
        g_lo, g_hi = g.pop("w_in")
        take = lambda at, lo, hi: g_lo[:, at + lo:at + hi] if at < W_IN_SPLIT else g_hi[:, at - W_IN_SPLIT + lo:at - W_IN_SPLIT + hi]
        width = IN_WIDTH // 4
        slots = []
        for j in range(4):
            lo, hi = j * width, (j + 1) * width
            slots.append(jnp.concatenate([take(at, max(lo, a) - a, min(hi, b) - a)
                                          for a, b, at in W_IN_SEGMENTS if max(lo, a) < min(hi, b)], axis=1))
        out["w_in"] = jnp.stack(slots).reshape(4, 2, D // 2, width)
    if "mla_w_uq" in g:
        g["mla_w_uq"] = g["mla_w_uq"].reshape(MQR, MH, LANE)[:, :, :MQK].reshape(MQR, MH * MQK)
    if "mla_w_o" in g:
        g["mla_w_o"] = g["mla_w_o"].reshape(MH, LANE, D)[:, :MVD].reshape(MH * MVD, D)
    for name, (rows, cols), axis in BIG:
        if name not in g:
            continue
        a = g[name]
        a = a.reshape(4, rows, cols) if axis == 0 else jnp.transpose(a.reshape(rows, 4, cols), (1, 0, 2))
        out[name] = a.reshape(4, 2, rows // 2, cols)
    return out


def _rope_tables(positions):
    freqs = ROPE_THETA ** (-jnp.arange(0, MROPE, 2, dtype=F32) / MROPE)
    lane = np.arange(LANE)
    in_rope = (lane >= MNOPE) & (lane < MQK)
    freq_lane = jnp.where(in_rope, freqs[(lane - MNOPE) % (MROPE // 2)], 0.0)
    sign = np.where(in_rope, np.where(lane < MNOPE + MROPE // 2, -1.0, 1.0), 0.0).astype(np.float32)
    ang = positions.astype(F32).reshape(-1, 1) * freq_lane[None, :]
    return jnp.cos(ang), jnp.sin(ang) * sign[None, :]


def _local_step(x, positions, mod, target, w, small, more_weights=None, on_grads=None):
    kept = {}
    if on_grads is None:
        on_grads = lambda tag, grads, after: kept.update(grads)
    bsz, s, _ = x.shape
    t = bsz * s
    tt = _tile(t, 1024)
    shift1, scale1, gate1, shift2, scale2, gate2 = [mod[:, None, i * D:(i + 1) * D] for i in range(6)]
    cos_t, sin_t = _rope_tables(positions)
    w_alpha_p = jnp.pad(small["gla_w_alpha"], ((0, LANE - GLR), (0, 0)))
    gq = jnp.pad(small["mla_qn_g"], ((0, 0), (0, LANE - MQK)))
    gk = jnp.pad(small["mla_kn_g"], ((0, 0), (0, LANE - MQK)))
    flat2 = lambda a: a.reshape(t, a.shape[-1])
    bsd = lambda a: a.reshape(bsz, s, a.shape[-1])

    h = _norm_mod(x, small["norm1_g"], scale1, shift1, name="norm1")
    if callable(w):
        w = w(h)
    proj = _mm(flat2(h), w["w_in"], name="proj", tn=1152)
    proj3 = bsd(proj)
    o, o_gated, states = _gla_fwd(proj3, w_alpha_p, small["gla_b_alpha"], small["gla_out_norm_g"], name="gla_fwd")
    if more_weights is not None:
        w = {**w, **more_weights(o_gated)}
    y_a = _mm(flat2(o_gated), w["gla_w_o"], name="gla_out")
    cq_n, ckv_n = _lat_norm(proj, small["mla_q_lat_g"], small["mla_kv_lat_g"], name="lat_norm")
    q_raw = _mm(cq_n, w["mla_w_uq"], name="mla_uq")
    kv = _mm(ckv_n, w["mla_w_ukv"], name="mla_ukv")
    qf, kf, vf = _qk_prep(q_raw, kv, proj, cos_t, sin_t, gq * Q_PRESCALE, gk, name="qk_prep")
    o_attn = _attn_fwd(bsd(qf), bsd(kf), bsd(vf), name="attn_fwd")
    y_b = _mm(flat2(o_attn), w["mla_w_o"], name="mla_out")
    mixed_in = _merge_fwd(proj3, small["b_merge"], bsd(y_a), bsd(y_b), name="merge_fwd")
    mixed = _mm(flat2(mixed_in), w["w_out"], name="w_out")
    x1, h2 = _resid_norm_mod(x, bsd(mixed), gate1, small["norm2_g"], scale2, shift2, name="norm2")

    def sqrelu(acc, ex, outs):
        r = jnp.maximum(acc, 0.0)
        outs[0][...] = (r * r).astype(BF16)

    r = _mm(flat2(h2), w["mlp_w1"], name="mlp1", epilogue=sqrelu, out_shape=jax.ShapeDtypeStruct((t, DFF), BF16),
            out_specs=_tile_spec(tt, 1024))
    ff = _mm(r, w["mlp_w2"], name="mlp2")
    dy, dff, dgate2, loss_part = _loss_head(x1, bsd(ff), gate2, target, name="loss_head")

    g = {}

    def relu2_bwd(acc, ex, outs):
        outs[0][...] = (acc * (2.0 * jnp.sqrt(ex[0][...].astype(F32)))).astype(BF16)

    dff2 = flat2(dff)
    da1 = _mm(dff2, w["mlp_w2"], tb=True, name="mlp2_dx", epilogue=relu2_bwd, extras=(r,),
              extra_specs=(_tile_spec(tt, 1024),), out_shape=jax.ShapeDtypeStruct((t, DFF), BF16),
              out_specs=_tile_spec(tt, 1024))
    g["mlp_w2"] = _mm(r, dff2, ta=True, name="mlp2_dw")
    dh2 = _mm(da1, w["mlp_w1"], tb=True, name="mlp1_dx")
    g["mlp_w1"] = _mm(flat2(h2), da1, ta=True, name="mlp1_dw")
    token = on_grads("mlp", {n: g.pop(n) for n in ("mlp_w2", "mlp_w1")}, dh2)
    if token is not None:
        gate1 = gate1 + token[0, 0]
    dx1, dscale2, dshift2, dg2, dgate1, dmixed = _norm_mod_bwd(
        bsd(dh2), x1, dy, small["norm2_g"], scale2, gate1, bsd(mixed), name="norm2_bwd")
    dmixed2 = flat2(dmixed)
    dmi = _mm(dmixed2, w["w_out"], tb=True, name="w_out_dx")
    g["w_out"] = _mm(flat2(mixed_in), dmixed2, ta=True, name="w_out_dw")
    dy_a, dy_b, dl_a, dl_b, db_a, db_b = _merge_bwd(bsd(dmi), proj3, small["b_merge"], bsd(y_a), bsd(y_b), name="merge_bwd")
    dy_a2, dy_b2 = flat2(dy_a), flat2(dy_b)
    dog = _mm(dy_a2, w["gla_w_o"], tb=True, name="gla_out_dx")
    g["gla_w_o"] = _mm(flat2(o_gated), dy_a2, ta=True, name="gla_out_dw")
    dq_g, dk_g, dv_g, dg_g, dlog, db_alpha, d_ong = _gla_bwd(
        bsd(dog), o, states, proj3, w_alpha_p, small["gla_b_alpha"], small["gla_out_norm_g"], name="gla_bwd")
    dlog2 = flat2(dlog)
    da_p = _mm(dlog2, w_alpha_p, tb=True, out_dtype=BF16, name="alpha_dx")
    d_w_alpha = _mm(proj[:, OFF_A:OFF_A + LANE], dlog2, ta=True, name="alpha_dw")[:GLR]
    do_attn = _mm(dy_b2, w["mla_w_o"], tb=True, out_dtype=BF16, name="mla_out_dx")
    g["mla_w_o"] = _mm(flat2(o_attn), dy_b2, ta=True, name="mla_out_dw")
    dqf, dkf, dvf = _attn_bwd(bsd(qf), bsd(kf), bsd(vf), bsd(do_attn), name="attn_bwd")
    dq_raw, dkv, dkpe, dgq, dgk = _qk_prep_bwd(flat2(dqf), flat2(dkf), flat2(dvf), q_raw, kv, proj, cos_t, sin_t, gq, gk,
                                                name="qk_prep_bwd")
    dcq_n = _mm(dq_raw, w["mla_w_uq"], tb=True, name="mla_uq_dx")
    g["mla_w_uq"] = _mm(cq_n, dq_raw, ta=True, name="mla_uq_dw")
    dckv_n = _mm(dkv, w["mla_w_ukv"], tb=True, name="mla_ukv_dx")
    g["mla_w_ukv"] = _mm(ckv_n, dkv, ta=True, name="mla_ukv_dw")
    token = on_grads("mix", {n: g.pop(n) for n in ("w_out", "gla_w_o", "mla_w_o", "mla_w_uq", "mla_w_ukv")}, dckv_n)
    q_lat_g = small["mla_q_lat_g"] if token is None else small["mla_q_lat_g"] + token[0:1, 0:1]
    dcq, dckv, dg_qlat, dg_kvlat = _lat_norm_bwd(dcq_n, dckv_n, proj, q_lat_g, small["mla_kv_lat_g"],
                                                  name="lat_norm_bwd")
    pieces = [(flat2(dq_g), OFF_Q), (flat2(dk_g), OFF_K), (flat2(dv_g), OFF_V), (flat2(dg_g), OFF_G),
              (flat2(dl_a), OFF_MA), (flat2(dl_b), OFF_MB), (dcq, OFF_CQ), (dckv, OFF_CKV), (da_p, OFF_A), (dkpe, OFF_KPE)]
    hb = flat2(h)
    g_w_in = (_pieces_dw(hb, [p for p, off in pieces if off < W_IN_SPLIT], name="proj_dw_a"),
              _pieces_dw(hb, [p for p, off in pieces if off >= W_IN_SPLIT], name="proj_dw_b"))
    token = on_grads("in", {"w_in": g_w_in}, g_w_in[1])
    after = jnp.zeros((8, LANE), F32) if token is None else token
    dh = _pieces_dx(pieces, w["w_in"], after, name="proj_dx")
    token = on_grads("dx", {}, dh)
    if token is not None:
        scale1 = scale1 + token[0, 0]
    grad_x, dscale1, dshift1, dg1 = _norm_mod_bwd(bsd(dh), x, dx1, small["norm1_g"], scale1, name="norm1_bwd")

    dmod = jnp.concatenate([dshift1, dscale1, dgate1, dshift2, dscale2, dgate2], axis=-1).reshape(bsz, 6 * D)
    gs = {"norm1_g": dg1, "b_merge": jnp.concatenate([db_a, db_b], axis=1), "gla_b_alpha": db_alpha,
          "gla_out_norm_g": d_ong, "mla_q_lat_g": dg_qlat, "mla_kv_lat_g": dg_kvlat, "mla_qn_g": dgq[:, :MQK],
          "mla_kn_g": dgk[:, :MQK], "norm2_g": dg2}
    return loss_part[0, 0], grad_x, dmod, {**kept, **g}, gs, d_w_alpha


def kernel(x, c, positions, w_ada, b_ada, norm1_g, w_in, b_merge, gla_w_alpha, gla_b_alpha, gla_out_norm_g, gla_w_o, mla_q_lat_g, mla_w_uq, mla_kv_lat_g, mla_w_ukv, mla_qn_g, mla_kn_g, mla_w_o, w_out, norm2_g, mlp_w1, mlp_w2, loss_target, m_w_ada, m_b_ada, m_norm1_g, m_w_in, m_b_merge, m_gla_w_alpha, m_gla_b_alpha, m_gla_out_norm_g, m_gla_w_o, m_mla_q_lat_g, m_mla_w_uq, m_mla_kv_lat_g, m_mla_w_ukv, m_mla_qn_g, m_mla_kn_g, m_mla_w_o, m_w_out, m_norm2_g, m_mlp_w1, m_mlp_w2, v_w_ada, v_b_ada, v_norm1_g, v_w_in, v_b_merge, v_gla_w_alpha, v_gla_b_alpha, v_gla_out_norm_g, v_gla_w_o, v_mla_q_lat_g, v_mla_w_uq, v_mla_kv_lat_g, v_mla_w_ukv, v_mla_qn_g, v_mla_kn_g, v_mla_w_o, v_w_out, v_norm2_g, v_mlp_w1, v_mlp_w2):
    args = dict(locals())
    names_big = [n for n, _, _ in BIG]
    names_small = [n for n, _ in SMALL]
    bsz = x.shape[0]
    ax, ay, ac = lax.axis_index("x"), lax.axis_index("y"), lax.axis_index("c")
    chip = 2 * ax + ay
    dev = 2 * chip + ac

    small = {n: args[n] for n in names_small}
    sel_c = jnp.reshape(ac, (1,)).astype(jnp.int32)
    sel_chip = jnp.reshape(chip, (1,)).astype(jnp.int32)
    c_all, w_alpha_all = _all_gather8([c, gla_w_alpha[0]], name="comm_c_alpha")
    small["gla_w_alpha"] = jnp.concatenate([w_alpha_all[2 * j] for j in range(4)], axis=1)
    c_all = c_all.reshape(8 * bsz, D)

    shards = {n: args[n][0].astype(BF16) for n in names_big}
    halves_of = lambda names: [shards[n].reshape(2, shards[n].shape[0] // 2, shards[n].shape[1]) for n in names]

    def gather_start(names, deps, tag):
        xs = halves_of(names)
        lands = [lax.empty((4, *xh.shape), BF16) for xh in xs]
        plan = _gather_plan(len(names))
        return names, plan, _rdma_start(xs + lands, 3 * len(names), plan, deps, name="comm_weights_start_" + tag)

    def gather_finish(started, after, tag):
        names, plan, sems = started
        arrs = _rdma_wait(sems[0], sems[1], sems[2], plan, after, name="comm_weights_wait_" + tag)
        filled = _pair_fill(arrs[len(names):], name="comm_weights_pair_" + tag)
        own = [a.reshape(shards[n].shape) for n, a in zip(names, arrs)]
        return _full_weights(dict(zip(names, _own_and_landed(filled, own))))


    def add_bias(acc, ex, outs):
        outs[0][...] = acc + ex[0][...]

    silu = lambda v: v * _sigmoid(v)
    b_ada_mine = lax.dynamic_slice(b_ada, (0, chip * ADA_SHARD[1]), (1, ADA_SHARD[1]))
    mod_part = _mm(c_all, w_ada[0], name="ada", tn=512, a_fn=silu, epilogue=add_bias, extras=(b_ada_mine,),
                   extra_specs=(pl.BlockSpec((1, 512), lambda i, j, k: (0, j)),),
                   out_shape=jax.ShapeDtypeStruct((8 * bsz, ADA_SHARD[1]), F32), out_specs=_tile_spec(8 * bsz, 512))
    mod_all = _all_gather8([mod_part], name="comm_mod")[0]
    mod_rows = lax.dynamic_slice(mod_all, (0, dev * bsz, 0), (8, bsz, ADA_SHARD[1]))
    mod = jnp.concatenate([mod_rows[2 * j] for j in range(4)], axis=1)
    first = gather_start(["w_in"], (mod,), "in")
    rest = gather_start([n for n in names_big if n != "w_in"], (mod, first[2][3]), "rest")
    mod = mod + rest[2][3][0, 0]
    w_in_after = lambda after: gather_finish(first, after, "in")
    more_weights = lambda after: gather_finish(rest, after, "rest")

    stage = {}

    def begin(tag, names, arrays, lands, n_copies, plan, what):
        stage[tag] = (names, plan, _rdma_start(arrays + lands, n_copies, plan, (), name=f"comm_{what}_start_{tag}"))
        return stage[tag][2][3]

    def landed(tag, after, what):
        names, plan, sems = stage[tag]
        arrs = _rdma_wait(sems[0], sems[1], sems[2], plan, after, name=f"comm_{what}_wait_{tag}")
        return names, arrs[:len(arrs) // 2], arrs[len(arrs) // 2:]

    def swap_start(tag, grads):
        names = list(grads)
        parts = [_grad_slots(grads)[n] for n in names]
        lands = [lax.empty((4, *p.shape[2:]), F32) for p in parts]
        return begin(tag, names, parts, lands, len(names), _sibling_plan(len(names), lambda r, c: r.at[:, 1 - c]), "pair_sum")

    def scatter_start(tag, after):
        names, parts, sib_halves = landed(tag, after, "pair_sum")
        pairs = [_pair_add(p, s, sel_c, name="pair_add_" + n) for n, p, s in zip(names, parts, sib_halves)]
        recvs = [lax.empty((3, *p.shape[1:]), BF16) for p in pairs]
        return begin(tag, names, pairs, recvs, 3 * len(names), _scatter_plan(len(names)), "scatter")

    def join_start(tag, after):
        names, pairs, recvs = landed(tag, after, "scatter")
        halves = [_chip_sum(p, r, sel_chip, name="chip_sum_" + n) for n, p, r in zip(names, pairs, recvs)]
        lands = [lax.empty(h.shape, F32) for h in halves]
        return begin(tag, names, halves, lands, len(names), _sibling_plan(len(names), lambda r, c: r), "pair_join")

    def reduce_step(tag, grads, after):
        if tag == "mlp":
            return swap_start("mlp", grads)
        if tag == "mix":
            return scatter_start("mlp", after) + swap_start("mix", grads)
        if tag == "in":
            return scatter_start("mix", after) + swap_start("in", grads)
        return scatter_start("in", after)

    loss_part, grad_x, dmod, g, gs, d_w_alpha = _local_step(x, positions, mod, loss_target, w_in_after, small,
                                                            more_weights, reduce_step)

    assert not g, list(g)
    gs_packed = _pack_small([gs[n] for n, _ in SMALL_RED], d_w_alpha, jnp.full((1, LANE), loss_part, F32),
                            name="pack_small")
    small_lands = [lax.empty((8, *a.shape), F32) for a in (dmod, gs_packed)]
    begin("small", ["dmod", "small"], [dmod, gs_packed], small_lands, 14, _gather8_plan(2), "gather8")

    res = {}

    def finish(tag, after):
        names, halves, theirs = landed(tag, after, "pair_join")
        for n, mine, other in zip(names, halves, theirs):
            if n == "w_in":
                south = ac == 0
                g_t = jnp.concatenate([jnp.where(south, mine, other), jnp.where(south, other, mine)], axis=0).T
                outs = _adamw(w_in[0].T, g_t, m_w_in[0].T, v_w_in[0].T, name="adamw_w_in", by_cols=True)
                res[n] = tuple(a.T for a in (g_t, *outs))
            else:
                res[n] = _adamw_halves(args[n][0], args["m_" + n][0], args["v_" + n][0], mine, other, sel_c,
                                       name="adamw_" + n)
        return res[names[-1]][1]

    join_start("mlp", grad_x)
    join_start("mix", grad_x)
    done = finish("mix", finish("mlp", grad_x))

    _, (dmod_own, gs_own), (dmod_all, gs_all) = landed("small", done, "gather8")
    dmod_all = lax.dynamic_update_slice(dmod_all, dmod_own[None], (dev, 0, 0)).reshape(8 * bsz, 6 * D)
    gs_all = lax.dynamic_update_slice(gs_all, gs_own[None], (dev, 0, 0))
    dmod_mine = lax.dynamic_slice(dmod_all, (0, chip * ADA_SHARD[1]), (8 * bsz, ADA_SHARD[1]))
    g_w_ada = _mm(c_all, dmod_mine, ta=True, a_fn=silu, name="ada_dw")
    wmv = [(args[n], args["m_" + n], args["v_" + n]) for n in names_small]
    wmv.append((gla_w_alpha[0], m_gla_w_alpha[0], v_gla_w_alpha[0]))
    res_small, loss_sum = _small_update(gs_all, dmod_all, sel_chip, wmv, name="small_update")
    res.update(res_small)
    loss = loss_sum * (0.5 / D)
    join_start("in", g_w_ada)
    res["w_ada"] = (g_w_ada, *_adamw(w_ada[0], g_w_ada, m_w_ada[0], v_w_ada[0], name="adamw_w_ada"))
    finish("in", res["w_ada"][1])

    order = ["w_ada", "b_ada", "norm1_g", "w_in", "b_merge", "gla_w_alpha", "gla_b_alpha", "gla_out_norm_g", "gla_w_o",
             "mla_q_lat_g", "mla_w_uq", "mla_kv_lat_g", "mla_w_ukv", "mla_qn_g", "mla_kn_g", "mla_w_o", "w_out",
             "norm2_g", "mlp_w1", "mlp_w2"]
    named = lambda k: [res[n][k].reshape(args[n].shape) for n in order]
    return (loss, grad_x, *named(0), *named(1), *named(2), *named(3))
```

```python
import jax
import jax.numpy as jnp
import numpy as np
from jax import lax
from jax.experimental import pallas as pl
from jax.experimental.pallas import tpu as pltpu

F32 = jnp.float32
BF16 = jnp.bfloat16
MESH = pl.DeviceIdType.MESH

D = 1024
CHUNK = 64
EPS = 1e-6
GH, GDK, GDV, GLR, GTAU = 4, 128, 256, 16, 16.0
MH, MQR, MKVR, MNOPE, MROPE, MVD = 16, 256, 128, 64, 32, 64
MQK = MNOPE + MROPE
DFF = 4 * D
ROPE_THETA = 10000.0
IN_WIDTH = 5552
LANE = 128
OFF_Q, OFF_K, OFF_V, OFF_G, OFF_MA, OFF_MB, OFF_CQ, OFF_CKV, OFF_A, OFF_KPE, PW = (
    0, 512, 1024, 2048, 3072, 4096, 5120, 5376, 5504, 5632, 5760)
ADAM_LR, ADAM_B1, ADAM_B2, ADAM_EPS, ADAM_WD, ADAM_STEP = 0.001, 0.9, 0.999, 1e-08, 0.01, 10
VMEM_LIMIT = 48 * 1024 * 1024


def _params(n_axes):
    return pltpu.CompilerParams(dimension_semantics=("arbitrary",) * n_axes, vmem_limit_bytes=VMEM_LIMIT)


def _tile(n, target):
    if n <= target:
        return n
    best = None
    for t in range(LANE, target + 1, LANE):
        if n % t == 0:
            best = t
    assert best is not None, (n, target)
    return best


def _sigmoid(x):
    return 1.0 / (1.0 + jnp.exp(-x))


MM_VMEM_BUDGET = 36 * 1024 * 1024


def _mm(a, b, *, name, ta=False, tb=False, out_dtype=F32, tm=1024, tn=1024, tk=4096,
        epilogue=None, extras=(), extra_specs=(), out_shape=None, out_specs=None, a_fn=None):
    if ta:
        kdim, m = a.shape
    else:
        m, kdim = a.shape
    if tb:
        n, k2 = b.shape
    else:
        k2, n = b.shape
    assert kdim == k2, (a.shape, b.shape)
    tm, tn, tk = _tile(m, tm), _tile(n, tn), _tile(kdim, tk)
    tiles = lambda rows: 2 * (rows * tk * a.dtype.itemsize + tk * tn * b.dtype.itemsize + rows * tn * 4) + rows * tn * 4
    while out_shape is None and tiles(tm) > MM_VMEM_BUDGET and tm % 256 == 0:
        tm //= 2
    nk = kdim // tk
    a_spec = pl.BlockSpec((tk, tm), lambda i, j, k: (k, i)) if ta else pl.BlockSpec((tm, tk), lambda i, j, k: (i, k))
    b_spec = pl.BlockSpec((tn, tk), lambda i, j, k: (j, k)) if tb else pl.BlockSpec((tk, tn), lambda i, j, k: (k, j))
    dims = (((0 if ta else 1,), (1 if tb else 0,)), ((), ()))
    ne = len(extras)
    if out_shape is None:
        out_shape = jax.ShapeDtypeStruct((m, n), out_dtype)
        out_specs = pl.BlockSpec((tm, tn), lambda i, j, k: (i, j))
    n_out = len(out_shape) if isinstance(out_shape, (list, tuple)) else 1
    in_place = epilogue is None and n_out == 1 and out_shape.dtype == F32
    scratch = [] if (nk == 1 or in_place) else [pltpu.VMEM((tm, tn), F32)]

    def body(a_ref, b_ref, *rest):
        ex, outs = rest[:ne], rest[ne:ne + n_out]
        av = a_ref[...] if a_fn is None else a_fn(a_ref[...])
        prod = lax.dot_general(av.astype(BF16), b_ref[...].astype(BF16), dims, preferred_element_type=F32)

        def finish(val):
            if epilogue is None:
                outs[0][...] = val.astype(outs[0].dtype)
            else:
                epilogue(val, ex, outs)

        if nk == 1:
            finish(prod)
            return
        k = pl.program_id(2)
        acc = outs[0] if in_place else rest[-1]

        @pl.when(k == 0)
        def _():
            acc[...] = prod

        @pl.when(k > 0)
        def _():
            acc[...] += prod

        if not in_place:
            @pl.when(k == nk - 1)
            def _():
                finish(acc[...])

    return pl.pallas_call(
        body, name=name, grid=(m // tm, n // tn, nk),
        in_specs=[a_spec, b_spec, *extra_specs], out_specs=out_specs, out_shape=out_shape,
        scratch_shapes=scratch, compiler_params=_params(3),
    )(a, b, *extras)


def _tile_spec(tm, tn):
    return pl.BlockSpec((tm, tn), lambda i, j, k: (i, j))


def _pieces_dx(pieces, w, after, *, name, tm=256):
    t = pieces[0][0].shape[0]
    tm = _tile(t, tm)
    npc = len(pieces)

    def body(*refs):
        p_refs, w_ref, out_ref = refs[:npc], refs[npc], refs[-1]
        acc = None
        for (arr, off), p_ref in zip(pieces, p_refs):
            part = lax.dot_general(p_ref[...].astype(BF16), w_ref[:, off:off + arr.shape[1]], _NT,
                                   preferred_element_type=F32)
            acc = part if acc is None else acc + part
        out_ref[...] = acc

    return pl.pallas_call(
        body, name=name, grid=(t // tm,),
        in_specs=[pl.BlockSpec((tm, arr.shape[1]), lambda i: (i, 0)) for arr, _ in pieces]
        + [pl.BlockSpec(w.shape, lambda i: (0, 0)), pl.BlockSpec((8, LANE), lambda i: (0, 0))],
        out_specs=pl.BlockSpec((tm, w.shape[0]), lambda i: (i, 0)),
        out_shape=jax.ShapeDtypeStruct((t, w.shape[0]), F32), compiler_params=_params(1),
    )(*[arr for arr, _ in pieces], w, after)


def _pieces_dw(h, pieces, *, name, tk=1024):
    t, d = h.shape
    tk = _tile(t, tk)
    widths = [p.shape[1] for p in pieces]
    starts = [sum(widths[:i]) for i in range(len(pieces))]

    def body(h_ref, *refs):
        p_refs, out_ref = refs[:-1], refs[-1]
        first = pl.program_id(0) == 0
        hv = h_ref[...]
        for p_ref, start, width in zip(p_refs, starts, widths):
            part = lax.dot_general(hv, p_ref[...].astype(BF16), _TN, preferred_element_type=F32)
            cols = slice(start, start + width)

            @pl.when(first)
            def _():
                out_ref[:, cols] = part

            @pl.when(jnp.logical_not(first))
            def _():
                out_ref[:, cols] += part

    return pl.pallas_call(
        body, name=name, grid=(t // tk,),
        in_specs=[pl.BlockSpec((tk, d), lambda k: (k, 0))] + [pl.BlockSpec((tk, wd), lambda k: (k, 0)) for wd in widths],
        out_specs=pl.BlockSpec((d, sum(widths)), lambda k: (0, 0)),
        out_shape=jax.ShapeDtypeStruct((d, sum(widths)), F32), compiler_params=_params(1),
    )(h, *pieces)


def _rms(x, g):
    r = lax.rsqrt(jnp.mean(x * x, axis=-1, keepdims=True) + EPS)
    return x * r, r


def _row_spec(ts, width, col=0):
    return pl.BlockSpec((None, ts, width), lambda b, i: (b, i, col))


def _vec_spec(width):
    return pl.BlockSpec((None, 1, width), lambda b, i: (b, 0, 0))


def _gain_spec(width):
    return pl.BlockSpec((1, width), lambda b, i: (0, 0))


def _norm_mod(x, g, scale, shift, *, name, ts=512):
    bsz, s, d = x.shape
    ts = min(ts, s)

    def body(x_ref, g_ref, sc_ref, sh_ref, h_ref):
        xh, _ = _rms(x_ref[...], None)
        h_ref[...] = ((xh * g_ref[...]) * (1.0 + sc_ref[...]) + sh_ref[...]).astype(BF16)

    return pl.pallas_call(
        body, name=name, grid=(bsz, s // ts),
        in_specs=[_row_spec(ts, d), _gain_spec(d), _vec_spec(d), _vec_spec(d)],
        out_specs=_row_spec(ts, d), out_shape=jax.ShapeDtypeStruct((bsz, s, d), BF16),
        compiler_params=_params(2),
    )(x, g, scale, shift)


def _resid_norm_mod(x, mixed, gate, g, scale, shift, *, name, ts=512):
    bsz, s, d = x.shape
    ts = min(ts, s)

    def body(x_ref, mx_ref, gt_ref, g_ref, sc_ref, sh_ref, x1_ref, h_ref):
        x1 = x_ref[...] + gt_ref[...] * mx_ref[...]
        x1_ref[...] = x1
        xh, _ = _rms(x1, None)
        h_ref[...] = ((xh * g_ref[...]) * (1.0 + sc_ref[...]) + sh_ref[...]).astype(BF16)

    return pl.pallas_call(
        body, name=name, grid=(bsz, s // ts),
        in_specs=[_row_spec(ts, d), _row_spec(ts, d), _vec_spec(d), _gain_spec(d), _vec_spec(d), _vec_spec(d)],
        out_specs=[_row_spec(ts, d), _row_spec(ts, d)],
        out_shape=[jax.ShapeDtypeStruct((bsz, s, d), F32), jax.ShapeDtypeStruct((bsz, s, d), BF16)],
        compiler_params=_params(2),
    )(x, mixed, gate, g, scale, shift)


def _norm_mod_bwd(dh, xin, resid, g, scale, gate=None, mixed=None, *, name, ts=512):
    bsz, s, d = xin.shape
    ts = min(ts, s)
    gated = gate is not None

    def body(*refs):
        if gated:
            dh_ref, x_ref, rs_ref, g_ref, sc_ref, gt_ref, mx_ref, dx_ref, dsc_ref, dsh_ref, dg_ref, dgt_ref, dmx_ref = refs
        else:
            dh_ref, x_ref, rs_ref, g_ref, sc_ref, dx_ref, dsc_ref, dsh_ref, dg_ref = refs
        b, i = pl.program_id(0), pl.program_id(1)

        @pl.when(i == 0)
        def _():
            dsc_ref[...] = jnp.zeros_like(dsc_ref)
            dsh_ref[...] = jnp.zeros_like(dsh_ref)
            if gated:
                dgt_ref[...] = jnp.zeros_like(dgt_ref)

        @pl.when((i == 0) & (b == 0))
        def _():
            dg_ref[...] = jnp.zeros_like(dg_ref)

        dh_v, gv = dh_ref[...], g_ref[...]
        xh, r = _rms(x_ref[...], None)
        dsc_ref[...] += jnp.sum(dh_v * (xh * gv), axis=0, keepdims=True)
        dsh_ref[...] += jnp.sum(dh_v, axis=0, keepdims=True)
        dn = dh_v * (1.0 + sc_ref[...])
        dg_ref[...] += jnp.sum(dn * xh, axis=0, keepdims=True)
        dxh = dn * gv
        dx = rs_ref[...] + r * (dxh - xh * jnp.mean(dxh * xh, axis=-1, keepdims=True))
        dx_ref[...] = dx
        if gated:
            dgt_ref[...] += jnp.sum(dx * mx_ref[...], axis=0, keepdims=True)
            dmx_ref[...] = (dx * gt_ref[...]).astype(BF16)

    ins = [dh, xin, resid, g, scale]
    in_specs = [_row_spec(ts, d), _row_spec(ts, d), _row_spec(ts, d), _gain_spec(d), _vec_spec(d)]
    out_specs = [_row_spec(ts, d), _vec_spec(d), _vec_spec(d), _gain_spec(d)]
    out_shape = [jax.ShapeDtypeStruct((bsz, s, d), F32), jax.ShapeDtypeStruct((bsz, 1, d), F32),
                 jax.ShapeDtypeStruct((bsz, 1, d), F32), jax.ShapeDtypeStruct((1, d), F32)]
    if gated:
        ins += [gate, mixed]
        in_specs += [_vec_spec(d), _row_spec(ts, d)]
        out_specs += [_vec_spec(d), _row_spec(ts, d)]
        out_shape += [jax.ShapeDtypeStruct((bsz, 1, d), F32), jax.ShapeDtypeStruct((bsz, s, d), BF16)]
    return pl.pallas_call(
        body, name=name, grid=(bsz, s // ts), in_specs=in_specs, out_specs=out_specs, out_shape=out_shape,
        compiler_params=_params(2),
    )(*ins)


def _loss_head(x1, ff, gate2, target, *, name, ts=512):
    bsz, s, d = x1.shape
    ts = min(ts, s)

    def body(x1_ref, ff_ref, gt_ref, t_ref, dy_ref, dff_ref, dgt_ref, loss_ref, acc):
        b, i = pl.program_id(0), pl.program_id(1)

        @pl.when(i == 0)
        def _():
            dgt_ref[...] = jnp.zeros_like(dgt_ref)

        @pl.when((i == 0) & (b == 0))
        def _():
            acc[...] = jnp.zeros_like(acc)

        ffv, gt = ff_ref[...], gt_ref[...]
        diff = (x1_ref[...] + gt * ffv) - t_ref[...]
        acc[...] += jnp.sum((diff * diff).reshape(ts // 8, 8, d), axis=0)
        dy = diff * (1.0 / d)
        dy_ref[...] = dy
        dgt_ref[...] += jnp.sum(dy * ffv, axis=0, keepdims=True)
        dff_ref[...] = (dy * gt).astype(BF16)

        @pl.when((i == pl.num_programs(1) - 1) & (b == pl.num_programs(0) - 1))
        def _():
            loss_ref[...] = jnp.full(loss_ref.shape, jnp.sum(acc[...]), F32)

    return pl.pallas_call(
        body, name=name, grid=(bsz, s // ts),
        in_specs=[_row_spec(ts, d), _row_spec(ts, d), _vec_spec(d), _row_spec(ts, d)],
        out_specs=[_row_spec(ts, d), _row_spec(ts, d), _vec_spec(d), pl.BlockSpec((8, LANE), lambda b, i: (0, 0))],
        out_shape=[jax.ShapeDtypeStruct((bsz, s, d), F32), jax.ShapeDtypeStruct((bsz, s, d), BF16),
                   jax.ShapeDtypeStruct((bsz, 1, d), F32), jax.ShapeDtypeStruct((8, LANE), F32)],
        scratch_shapes=[pltpu.VMEM((8, d), F32)], compiler_params=_params(2),
    )(x1, ff, gate2, target)


def _merge_fwd(proj, b_merge, y_a, y_b, *, name, ts=512):
    bsz, s, _ = proj.shape
    ts = min(ts, s)

    def body(la_ref, lb_ref, ba_ref, bb_ref, ya_ref, yb_ref, out_ref):
        ga = _sigmoid(la_ref[...] + ba_ref[...])
        gb = _sigmoid(lb_ref[...] + bb_ref[...])
        out_ref[...] = (ga * ya_ref[...] + gb * yb_ref[...]).astype(BF16)

    return pl.pallas_call(
        body, name=name, grid=(bsz, s // ts),
        in_specs=[_row_spec(ts, D, OFF_MA // D), _row_spec(ts, D, OFF_MB // D),
                  pl.BlockSpec((1, D), lambda b, i: (0, 0)), pl.BlockSpec((1, D), lambda b, i: (0, 1)),
                  _row_spec(ts, D), _row_spec(ts, D)],
        out_specs=_row_spec(ts, D), out_shape=jax.ShapeDtypeStruct((bsz, s, D), BF16),
        compiler_params=_params(2),
    )(proj, proj, b_merge, b_merge, y_a, y_b)


def _merge_bwd(dmi, proj, b_merge, y_a, y_b, *, name, ts=512):
    bsz, s, _ = proj.shape
    ts = min(ts, s)

    def body(d_ref, la_ref, lb_ref, ba_ref, bb_ref, ya_ref, yb_ref, dya_ref, dyb_ref, dla_ref, dlb_ref, dba_ref, dbb_ref):
        @pl.when((pl.program_id(0) == 0) & (pl.program_id(1) == 0))
        def _():
            dba_ref[...] = jnp.zeros_like(dba_ref)
            dbb_ref[...] = jnp.zeros_like(dbb_ref)

        dv = d_ref[...]
        ga = _sigmoid(la_ref[...] + ba_ref[...])
        gb = _sigmoid(lb_ref[...] + bb_ref[...])
        dya_ref[...] = (dv * ga).astype(BF16)
        dyb_ref[...] = (dv * gb).astype(BF16)
        dla = (dv * ya_ref[...]) * (ga * (1.0 - ga))
        dlb = (dv * yb_ref[...]) * (gb * (1.0 - gb))
        dla_ref[...] = dla.astype(BF16)
        dlb_ref[...] = dlb.astype(BF16)
        dba_ref[...] += jnp.sum(dla, axis=0, keepdims=True)
        dbb_ref[...] += jnp.sum(dlb, axis=0, keepdims=True)

    act = jax.ShapeDtypeStruct((bsz, s, D), BF16)
    return pl.pallas_call(
        body, name=name, grid=(bsz, s // ts),
        in_specs=[_row_spec(ts, D), _row_spec(ts, D, OFF_MA // D), _row_spec(ts, D, OFF_MB // D),
                  pl.BlockSpec((1, D), lambda b, i: (0, 0)), pl.BlockSpec((1, D), lambda b, i: (0, 1)),
                  _row_spec(ts, D), _row_spec(ts, D)],
        out_specs=[_row_spec(ts, D)] * 4 + [_gain_spec(D)] * 2,
        out_shape=[act, act, act, act, jax.ShapeDtypeStruct((1, D), F32), jax.ShapeDtypeStruct((1, D), F32)],
        compiler_params=_params(2),
    )(dmi, proj, proj, b_merge, b_merge, y_a, y_b)


def _tri(lower):
    r = lax.broadcasted_iota(jnp.int32, (CHUNK, CHUNK), 0)
    c = lax.broadcasted_iota(jnp.int32, (CHUNK, CHUNK), 1)
    return jnp.where((c <= r) if lower else (c >= r), 1.0, 0.0).astype(F32)


def _gla_logits(a_ref, wal_ref, bal_ref):
    logits = jnp.dot(a_ref[...].astype(BF16), wal_ref[...].astype(BF16), preferred_element_type=F32) + bal_ref[...]
    la = (jnp.minimum(logits, 0.0) - jnp.log(1.0 + jnp.exp(-jnp.abs(logits)))) * (1.0 / GTAU)
    return logits, la


def _chunk_cumsum(la_n, tri):
    cum = jnp.dot(tri, la_n, preferred_element_type=F32, precision=lax.Precision.HIGHEST)
    return cum, jnp.sum(la_n, axis=0, keepdims=True)


def _gla_specs(s, nc):
    def blk(width, off):
        return pl.BlockSpec((None, s, width), lambda h, b: (b, 0, off // width + h))

    proj_specs = [blk(GDK, OFF_Q), blk(GDK, OFF_K), blk(GDV, OFF_V), blk(GDV, OFF_G),
                  pl.BlockSpec((None, s, LANE), lambda h, b: (b, 0, OFF_A // LANE)),
                  pl.BlockSpec((LANE, GDK), lambda h, b: (0, h)), pl.BlockSpec((1, GDK), lambda h, b: (0, h)),
                  pl.BlockSpec((1, GDV), lambda h, b: (0, 0))]
    st_spec = pl.BlockSpec((None, None, nc, GDV, GDK), lambda h, b: (b, h, 0, 0, 0))
    return blk, proj_specs, st_spec


def _gla_fwd(proj, w_alpha_p, b_alpha, out_norm_g, *, name):
    bsz, s, _ = proj.shape
    nc = s // CHUNK
    scale = GDK ** -0.5

    rb = min(512, s)

    def body(q_ref, k_ref, v_ref, g_ref, a_ref, wal_ref, bal_ref, ong_ref, o_ref, og_ref, st_ref):
        _, la = _gla_logits(a_ref, wal_ref, bal_ref)
        tri = _tri(True)
        st = jnp.zeros((GDV, GDK), F32)
        for n in range(nc):
            rows = pl.ds(n * CHUNK, CHUNK)
            cum, cum_end = _chunk_cumsum(la[n * CHUNK:(n + 1) * CHUNK], tri)
            kd = k_ref[rows, :] * jnp.exp(cum_end - cum)
            ut = lax.dot_general(v_ref[rows, :].astype(BF16), kd.astype(BF16), _TN, preferred_element_type=F32)
            st = st * jnp.exp(cum_end) + ut
            st_ref[n] = st
            o_ref[rows, :] = lax.dot_general((q_ref[rows, :] * scale).astype(BF16), st.astype(BF16), _NT,
                                             preferred_element_type=F32)
        for j in range(0, s, rb):
            blk_rows = pl.ds(j, rb)
            oh, _ = _rms(o_ref[blk_rows, :], None)
            gv = g_ref[blk_rows, :]
            og_ref[blk_rows, :] = ((oh * ong_ref[...]) * (gv * _sigmoid(gv))).astype(BF16)

    blk, proj_specs, st_spec = _gla_specs(s, nc)
    return pl.pallas_call(
        body, name=name, grid=(GH, bsz), in_specs=proj_specs, out_specs=[blk(GDV, 0), blk(GDV, 0), st_spec],
        out_shape=[jax.ShapeDtypeStruct((bsz, s, GH * GDV), F32), jax.ShapeDtypeStruct((bsz, s, GH * GDV), BF16),
                   jax.ShapeDtypeStruct((bsz, GH, nc, GDV, GDK), F32)],
        compiler_params=_params(2),
    )(proj, proj, proj, proj, proj, w_alpha_p, b_alpha, out_norm_g)


def _gla_bwd(dog, o, states, proj, w_alpha_p, b_alpha, out_norm_g, *, name):
    bsz, s, _ = proj.shape
    nc = s // CHUNK
    scale = GDK ** -0.5

    def body(dog_ref, o_ref, st_ref, q_ref, k_ref, v_ref, g_ref, a_ref, wal_ref, bal_ref, ong_ref,
             dq_ref, dk_ref, dv_ref, dg_ref, dl_ref, dbal_ref, dong_ref, do_scr, dlog_scr):
        h, b = pl.program_id(0), pl.program_id(1)

        @pl.when(b == 0)
        def _():
            dbal_ref[...] = jnp.zeros_like(dbal_ref)

        @pl.when((b == 0) & (h == 0))
        def _():
            dong_ref[...] = jnp.zeros_like(dong_ref)

        ong = ong_ref[...]
        for j in range(0, s, rb):
            blk_rows = pl.ds(j, rb)
            gv, dogv = g_ref[blk_rows, :], dog_ref[blk_rows, :]
            sg = _sigmoid(gv)
            oh, r = _rms(o_ref[blk_rows, :], None)
            don = dogv * (gv * sg)
            dg_ref[blk_rows, :] = (dogv * (oh * ong) * (sg * (1.0 + gv * (1.0 - sg)))).astype(BF16)
            dong_ref[...] += jnp.sum(don * oh, axis=0, keepdims=True)
            doh = don * ong
            do_scr[blk_rows, :] = (r * (doh - oh * jnp.mean(doh * oh, axis=-1, keepdims=True))).astype(BF16)

        logits, la = _gla_logits(a_ref, wal_ref, bal_ref)
        tri_lo, tri_up = _tri(True), _tri(False)
        carry = jnp.zeros((GDV, GDK), F32)
        for n in range(nc - 1, -1, -1):
            rows = pl.ds(n * CHUNK, CHUNK)
            cum, cum_end = _chunk_cumsum(la[n * CHUNK:(n + 1) * CHUNK], tri_lo)
            decay = jnp.exp(cum_end)
            w = jnp.exp(cum_end - cum)
            kd = k_ref[rows, :] * w
            do_b = do_scr[rows, :]
            qs_b = (q_ref[rows, :] * scale).astype(BF16)
            dq_ref[rows, :] = (jnp.dot(do_b, st_ref[n].astype(BF16), preferred_element_type=F32) * scale).astype(BF16)
            dsn = lax.dot_general(do_b, qs_b, _TN, preferred_element_type=F32) + carry
            carry = dsn * decay
            dsn_b = dsn.astype(BF16)
            dv_ref[rows, :] = lax.dot_general(kd.astype(BF16), dsn_b, _NT, preferred_element_type=F32).astype(BF16)
            dkd = jnp.dot(v_ref[rows, :].astype(BF16), dsn_b, preferred_element_type=F32)
            dk_ref[rows, :] = (dkd * w).astype(BF16)
            e = dkd * kd
            dcum_end = jnp.sum(e, axis=0, keepdims=True)
            if n > 0:
                dcum_end += jnp.sum(dsn * st_ref[n - 1], axis=0, keepdims=True) * decay
            dlog_scr[rows, :] = dcum_end - jnp.dot(tri_up, e, preferred_element_type=F32,
                                                  precision=lax.Precision.HIGHEST)
        dlog = dlog_scr[...] * (1.0 / GTAU) * (1.0 - _sigmoid(logits))
        dl_ref[...] = dlog.astype(BF16)
        dbal_ref[...] += jnp.sum(dlog, axis=0, keepdims=True)

    rb = min(512, s)

    blk, proj_specs, st_spec = _gla_specs(s, nc)
    act = lambda wd: jax.ShapeDtypeStruct((bsz, s, wd), BF16)
    return pl.pallas_call(
        body, name=name, grid=(GH, bsz), in_specs=[blk(GDV, 0), blk(GDV, 0), st_spec, *proj_specs],
        out_specs=[blk(GDK, 0), blk(GDK, 0), blk(GDV, 0), blk(GDV, 0), blk(GDK, 0),
                   pl.BlockSpec((1, GDK), lambda h, b: (0, h)), pl.BlockSpec((1, GDV), lambda h, b: (0, 0))],
        out_shape=[act(GH * GDK), act(GH * GDK), act(GH * GDV), act(GH * GDV), act(GH * GDK),
                   jax.ShapeDtypeStruct((1, GH * GDK), F32), jax.ShapeDtypeStruct((1, GDV), F32)],
        scratch_shapes=[pltpu.VMEM((s, GDV), BF16), pltpu.VMEM((s, GDK), F32)], compiler_params=_params(2),
    )(dog, o, states, proj, proj, proj, proj, proj, w_alpha_p, b_alpha, out_norm_g)


def _lane():
    return lax.broadcasted_iota(jnp.int32, (1, LANE), 1)


def _swap_halves(x):
    lane = _lane()
    half = MROPE // 2
    lo = (lane >= MNOPE) & (lane < MNOPE + half)
    hi = (lane >= MNOPE + half) & (lane < MQK)
    return jnp.where(lo, pltpu.roll(x, LANE - half, 1), jnp.where(hi, pltpu.roll(x, half, 1), 0.0))


def _norm96(x, g):
    r = lax.rsqrt(jnp.sum(x * x, axis=-1, keepdims=True) * (1.0 / MQK) + EPS)
    return x * r, r


def _lat_norm(proj, q_lat_g, kv_lat_g, *, name, ts=512):
    t = proj.shape[0]
    ts = min(ts, t)

    def body(cq_ref, ckv_ref, gq_ref, gk_ref, oq_ref, ok_ref):
        xq, _ = _rms(cq_ref[...], None)
        oq_ref[...] = (xq * gq_ref[...]).astype(BF16)
        xk, _ = _rms(ckv_ref[...], None)
        ok_ref[...] = (xk * gk_ref[...]).astype(BF16)

    return pl.pallas_call(
        body, name=name, grid=(t // ts,),
        in_specs=[pl.BlockSpec((ts, MQR), lambda i: (i, OFF_CQ // MQR)), pl.BlockSpec((ts, MKVR), lambda i: (i, OFF_CKV // MKVR)),
                  pl.BlockSpec((1, MQR), lambda i: (0, 0)), pl.BlockSpec((1, MKVR), lambda i: (0, 0))],
        out_specs=[pl.BlockSpec((ts, MQR), lambda i: (i, 0)), pl.BlockSpec((ts, MKVR), lambda i: (i, 0))],
        out_shape=[jax.ShapeDtypeStruct((t, MQR), BF16), jax.ShapeDtypeStruct((t, MKVR), BF16)],
        compiler_params=_params(1),
    )(proj, proj, q_lat_g, kv_lat_g)


def _lat_norm_bwd(dcqn, dckvn, proj, q_lat_g, kv_lat_g, *, name, ts=512):
    t = proj.shape[0]
    ts = min(ts, t)

    def one(d_ref, x_ref, g_ref, dx_ref, dg_ref):
        xh, r = _rms(x_ref[...], None)
        dn = d_ref[...]
        dg_ref[...] += jnp.sum(dn * xh, axis=0, keepdims=True)
        dxh = dn * g_ref[...]
        dx_ref[...] = (r * (dxh - xh * jnp.mean(dxh * xh, axis=-1, keepdims=True))).astype(BF16)

    def body(dq_ref, dk_ref, cq_ref, ckv_ref, gq_ref, gk_ref, dxq_ref, dxk_ref, dgq_ref, dgk_ref):
        @pl.when(pl.program_id(0) == 0)
        def _():
            dgq_ref[...] = jnp.zeros_like(dgq_ref)
            dgk_ref[...] = jnp.zeros_like(dgk_ref)

        one(dq_ref, cq_ref, gq_ref, dxq_ref, dgq_ref)
        one(dk_ref, ckv_ref, gk_ref, dxk_ref, dgk_ref)

    return pl.pallas_call(
        body, name=name, grid=(t // ts,),
        in_specs=[pl.BlockSpec((ts, MQR), lambda i: (i, 0)), pl.BlockSpec((ts, MKVR), lambda i: (i, 0)),
                  pl.BlockSpec((ts, MQR), lambda i: (i, OFF_CQ // MQR)), pl.BlockSpec((ts, MKVR), lambda i: (i, OFF_CKV // MKVR)),
                  pl.BlockSpec((1, MQR), lambda i: (0, 0)), pl.BlockSpec((1, MKVR), lambda i: (0, 0))],
        out_specs=[pl.BlockSpec((ts, MQR), lambda i: (i, 0)), pl.BlockSpec((ts, MKVR), lambda i: (i, 0)),
                   pl.BlockSpec((1, MQR), lambda i: (0, 0)), pl.BlockSpec((1, MKVR), lambda i: (0, 0))],
        out_shape=[jax.ShapeDtypeStruct((t, MQR), BF16), jax.ShapeDtypeStruct((t, MKVR), BF16),
                   jax.ShapeDtypeStruct((1, MQR), F32), jax.ShapeDtypeStruct((1, MKVR), F32)],
        compiler_params=_params(1),
    )(dcqn, dckvn, proj, proj, q_lat_g, kv_lat_g)


def _qk_prep(q_raw, kv, proj, cos_t, sin_t, gq, gk, *, name, ts=2048):
    t = q_raw.shape[0]
    ts = min(ts, t)

    def body(q_ref, kv_ref, kpe_ref, c_ref, s_ref, gq_ref, gk_ref, qo_ref, ko_ref, vo_ref):
        cs, sn = c_ref[...], s_ref[...]
        nope = _lane() < MNOPE
        qn, _ = _norm96(q_ref[...], None)
        qn = qn * gq_ref[...]
        qo_ref[...] = (qn * cs + _swap_halves(qn) * sn).astype(BF16)
        kvv = kv_ref[...]
        kn, _ = _norm96(jnp.where(nope, kvv, kpe_ref[...]), None)
        kn = kn * gk_ref[...]
        ko_ref[...] = (kn * cs + _swap_halves(kn) * sn).astype(BF16)
        vo_ref[...] = jnp.where(nope, pltpu.roll(kvv, MNOPE, 1), 0.0).astype(BF16)

    hd = pl.BlockSpec((ts, LANE), lambda i, h: (i, h))
    shared = lambda col: pl.BlockSpec((ts, LANE), lambda i, h: (i, col))
    gain = pl.BlockSpec((1, LANE), lambda i, h: (0, 0))
    out = jax.ShapeDtypeStruct((t, MH * LANE), BF16)
    return pl.pallas_call(
        body, name=name, grid=(t // ts, MH),
        in_specs=[hd, hd, shared(OFF_KPE // LANE), shared(0), shared(0), gain, gain],
        out_specs=[hd, hd, hd], out_shape=[out, out, out], compiler_params=_params(2),
    )(q_raw, kv, proj, cos_t, sin_t, gq, gk)


def _qk_prep_bwd(dq, dk, dv, q_raw, kv, proj, cos_t, sin_t, gq, gk, *, name, ts=2048):
    t = q_raw.shape[0]
    ts = min(ts, t)

    def norm_bwd(dy, x, g, dg_ref):
        xh, r = _norm96(x, None)
        dg_ref[...] += jnp.sum(dy * xh, axis=0, keepdims=True)
        dxh = dy * g
        return r * (dxh - xh * (jnp.sum(dxh * xh, axis=-1, keepdims=True) * (1.0 / MQK)))

    def body(dq_ref, dk_ref, dv_ref, q_ref, kv_ref, kpe_ref, c_ref, s_ref, gq_ref, gk_ref,
             dqr_ref, dkv_ref, dkpe_ref, dgq_ref, dgk_ref):
        i, h = pl.program_id(0), pl.program_id(1)

        @pl.when(h == 0)
        def _():
            dkpe_ref[...] = jnp.zeros_like(dkpe_ref)

        @pl.when((h == 0) & (i == 0))
        def _():
            dgq_ref[...] = jnp.zeros_like(dgq_ref)
            dgk_ref[...] = jnp.zeros_like(dgk_ref)

        cs, sn = c_ref[...], s_ref[...]
        lane = _lane()
        nope = lane < MNOPE
        dqv = dq_ref[...]
        dqn = dqv * cs + _swap_halves(dqv * sn)
        dqr_ref[...] = norm_bwd(dqn, q_ref[...], gq_ref[...], dgq_ref).astype(BF16)
        dkv_ = dk_ref[...]
        dkn = dkv_ * cs + _swap_halves(dkv_ * sn)
        kvv = kv_ref[...]
        dkr = norm_bwd(dkn, jnp.where(nope, kvv, kpe_ref[...]), gk_ref[...], dgk_ref)
        dkv_ref[...] = jnp.where(nope, dkr, pltpu.roll(dv_ref[...], MNOPE, 1)).astype(BF16)
        dkpe_ref[...] += jnp.where((lane >= MNOPE) & (lane < MQK), dkr, 0.0)

    hd = pl.BlockSpec((ts, LANE), lambda i, h: (i, h))
    shared = lambda col: pl.BlockSpec((ts, LANE), lambda i, h: (i, col))
    gain = pl.BlockSpec((1, LANE), lambda i, h: (0, 0))
    out = jax.ShapeDtypeStruct((t, MH * LANE), BF16)
    return pl.pallas_call(
        body, name=name, grid=(t // ts, MH),
        in_specs=[hd, hd, hd, hd, hd, shared(OFF_KPE // LANE), shared(0), shared(0), gain, gain],
        out_specs=[hd, hd, shared(0), gain, gain],
        out_shape=[out, out, jax.ShapeDtypeStruct((t, LANE), F32), jax.ShapeDtypeStruct((1, LANE), F32),
                   jax.ShapeDtypeStruct((1, LANE), F32)],
        compiler_params=_params(2),
    )(dq, dk, dv, q_raw, kv, proj, cos_t, sin_t, gq, gk)


_NT = (((1,), (1,)), ((), ()))
_TN = (((0,), (0,)), ((), ()))


SOFTMAX_SCALE = MQK ** -0.5
Q_PRESCALE = SOFTMAX_SCALE * float(np.log2(np.e))


def _attn_weights(q, k_ref, lo, tq):
    row = lax.broadcasted_iota(jnp.int32, (tq, tq), 0) // CHUNK
    col = lax.broadcasted_iota(jnp.int32, (tq, tq), 1) // CHUNK
    sd = lax.dot_general(q, k_ref[pl.ds(lo, tq), :], _NT, preferred_element_type=F32)
    sd = jnp.where(col <= row, sd, -1e30)
    m = jnp.max(sd, axis=-1, keepdims=True)
    if lo:
        so = lax.dot_general(q, k_ref[pl.ds(0, lo), :], _NT, preferred_element_type=F32)
        m = jnp.maximum(m, jnp.max(so, axis=-1, keepdims=True))
        eo = jnp.exp2(so - m)
        ed = jnp.exp2(sd - m)
        return eo, ed, 1.0 / (jnp.sum(eo, axis=-1, keepdims=True) + jnp.sum(ed, axis=-1, keepdims=True))
    ed = jnp.exp2(sd - m)
    return None, ed, 1.0 / jnp.sum(ed, axis=-1, keepdims=True)


def _attn_fwd(q, k, v, *, name, tq=256):
    bsz, s, _ = q.shape
    tq = min(tq, s)

    def body(q_ref, k_ref, v_ref, o_ref):
        for i in range(s // tq):
            lo = i * tq
            eo, ed, inv = _attn_weights(q_ref[pl.ds(lo, tq), :], k_ref, lo, tq)
            o = jnp.dot(ed.astype(BF16), v_ref[pl.ds(lo, tq), :], preferred_element_type=F32)
            if lo:
                o += jnp.dot(eo.astype(BF16), v_ref[pl.ds(0, lo), :], preferred_element_type=F32)
            o_ref[pl.ds(lo, tq), :] = (o * inv).astype(BF16)

    spec = pl.BlockSpec((None, s, LANE), lambda b, h: (b, 0, h))
    return pl.pallas_call(
        body, name=name, grid=(bsz, MH), in_specs=[spec, spec, spec], out_specs=spec,
        out_shape=jax.ShapeDtypeStruct((bsz, s, MH * LANE), BF16), compiler_params=_params(2),
    )(q, k, v)


def _attn_bwd(q, k, v, do, *, name, tq=256):
    bsz, s, _ = q.shape
    tq = min(tq, s)

    def body(q_ref, k_ref, v_ref, do_ref, dq_ref, dk_ref, dv_ref):
        dk_ref[...] = jnp.zeros_like(dk_ref)
        dv_ref[...] = jnp.zeros_like(dv_ref)
        for i in range(s // tq):
            lo = i * tq
            here, before = pl.ds(lo, tq), pl.ds(0, lo)
            qv, dov = q_ref[here, :], do_ref[here, :]
            eo, ed, inv = _attn_weights(qv, k_ref, lo, tq)
            do_n = (dov.astype(F32) * inv).astype(BF16)
            dv_ref[here, :] += lax.dot_general(ed.astype(BF16), do_n, _TN, preferred_element_type=F32)
            dpd = lax.dot_general(dov, v_ref[here, :], _NT, preferred_element_type=F32)
            delta = jnp.sum(dpd * ed, axis=-1, keepdims=True)
            if lo:
                dv_ref[before, :] += lax.dot_general(eo.astype(BF16), do_n, _TN, preferred_element_type=F32)
                dpo = lax.dot_general(dov, v_ref[before, :], _NT, preferred_element_type=F32)
                delta += jnp.sum(dpo * eo, axis=-1, keepdims=True)
            delta = delta * inv
            r = inv * SOFTMAX_SCALE
            dsd = (ed * (dpd - delta) * r).astype(BF16)
            dq = jnp.dot(dsd, k_ref[here, :], preferred_element_type=F32)
            dk_ref[here, :] += lax.dot_general(dsd, qv, _TN, preferred_element_type=F32)
            if lo:
                dso = (eo * (dpo - delta) * r).astype(BF16)
                dq += jnp.dot(dso, k_ref[before, :], preferred_element_type=F32)
                dk_ref[before, :] += lax.dot_general(dso, qv, _TN, preferred_element_type=F32)
            dq_ref[here, :] = dq
        dk_ref[...] = dk_ref[...] * (1.0 / Q_PRESCALE)

    spec = pl.BlockSpec((None, s, LANE), lambda b, h: (b, 0, h))
    out = jax.ShapeDtypeStruct((bsz, s, MH * LANE), F32)
    return pl.pallas_call(
        body, name=name, grid=(bsz, MH), in_specs=[spec] * 4, out_specs=[spec] * 3, out_shape=[out, out, out],
        compiler_params=_params(2),
    )(q, k, v, do)


def _adamw(w, g, m, v, *, name, tr=256, by_cols=False):
    rows, cols = w.shape
    tr = _tile_rows(rows, tr)

    def body(w_ref, g_ref, m_ref, v_ref, d_ref, nm_ref, nv_ref):
        d_ref[...], nm_ref[...], nv_ref[...] = _adamw_update(w_ref[...], g_ref[...], m_ref[...], v_ref[...])

    spec = pl.BlockSpec((rows, LANE), lambda i: (0, i)) if by_cols else pl.BlockSpec((tr, cols), lambda i: (i, 0))
    out = jax.ShapeDtypeStruct((rows, cols), F32)
    return pl.pallas_call(body, name=name, grid=(cols // LANE if by_cols else rows // tr,), in_specs=[spec] * 4,
                          out_specs=[spec] * 3, out_shape=[out, out, out], compiler_params=_params(1))(w, g, m, v)


def _tile_rows(rows, target):
    if rows <= target:
        return rows
    best = 8
    for t in range(8, target + 1, 8):
        if rows % t == 0:
            best = t
    return best


def _adamw_update(w, g, m, v):
    nm = ADAM_B1 * m + (1.0 - ADAM_B1) * g
    nv = ADAM_B2 * v + (1.0 - ADAM_B2) * (g * g)
    m_hat = nm / (1.0 - ADAM_B1 ** ADAM_STEP)
    v_hat = nv / (1.0 - ADAM_B2 ** ADAM_STEP)
    return -ADAM_LR * (m_hat / (jnp.sqrt(v_hat) + ADAM_EPS) + ADAM_WD * w), nm, nv


def _adamw_halves(w, m, v, mine, theirs, sel, *, name, tr=256):
    rows, cols = w.shape
    tr = _tile_rows(rows // 2, tr)
    nh = rows // 2 // tr

    def body(sel_ref, w_ref, m_ref, v_ref, mine_ref, theirs_ref, g_ref, d_ref, nm_ref, nv_ref):
        lower = pl.program_id(0) < nh
        south = sel_ref[0] == 0
        gv = jnp.where(lower == south, mine_ref[...], theirs_ref[...])
        g_ref[...] = gv
        d_ref[...], nm_ref[...], nv_ref[...] = _adamw_update(w_ref[...], gv, m_ref[...], v_ref[...])

    full = pl.BlockSpec((tr, cols), lambda i, sel_ref: (i, 0))
    half = pl.BlockSpec((tr, cols), lambda i, sel_ref: (i % nh, 0))
    out = jax.ShapeDtypeStruct((rows, cols), F32)
    return pl.pallas_call(
        body, name=name, out_shape=[out] * 4, compiler_params=_params(1),
        grid_spec=pltpu.PrefetchScalarGridSpec(num_scalar_prefetch=1, grid=(rows // tr,),
                                               in_specs=[full, full, full, half, half], out_specs=[full] * 4),
    )(sel, w, m, v, mine, theirs)


def _pair_add(x, sib, sel, *, name, tr=256):
    n, _, rows, cols = x.shape
    tr = _tile_rows(rows, tr)

    def body(sel_ref, x_ref, s_ref, o_ref):
        o_ref[...] = (x_ref[...] + s_ref[...]).astype(BF16)

    spec = pl.BlockSpec((None, tr, cols), lambda j, i, sel_ref: (j, i, 0))
    return pl.pallas_call(
        body, name=name, out_shape=jax.ShapeDtypeStruct((n, rows, cols), BF16), compiler_params=_params(2),
        grid_spec=pltpu.PrefetchScalarGridSpec(
            num_scalar_prefetch=1, grid=(n, rows // tr),
            in_specs=[pl.BlockSpec((None, None, tr, cols), lambda j, i, sel_ref: (j, sel_ref[0], i, 0)), spec],
            out_specs=spec),
    )(sel, x, sib)


def _chip_sum(pair, recv, sel, *, name, tr=256):
    _, rows, cols = pair.shape
    tr = _tile_rows(rows, tr)

    def body(sel_ref, p_ref, r_ref, o_ref):
        acc = p_ref[...].astype(F32)
        for k in range(3):
            acc = acc + r_ref[k].astype(F32)
        o_ref[...] = acc

    return pl.pallas_call(
        body, name=name, out_shape=jax.ShapeDtypeStruct((rows, cols), F32), compiler_params=_params(1),
        grid_spec=pltpu.PrefetchScalarGridSpec(
            num_scalar_prefetch=1, grid=(rows // tr,),
            in_specs=[pl.BlockSpec((None, tr, cols), lambda i, sel_ref: (sel_ref[0], i, 0)),
                      pl.BlockSpec((3, tr, cols), lambda i, sel_ref: (0, i, 0))],
            out_specs=pl.BlockSpec((tr, cols), lambda i, sel_ref: (i, 0))),
    )(sel, pair, recv)


def _me():
    return lax.axis_index("x"), lax.axis_index("y"), lax.axis_index("c")


def _flip(pos, bits):
    x, y, c = pos
    return (x ^ bits[0] if bits[0] else x, y ^ bits[1] if bits[1] else y, c ^ bits[2] if bits[2] else c)


ANY = pl.BlockSpec(memory_space=pl.ANY)


def _all_gather8(xs, *, name):
    n = len(xs)
    flips = [((k >> 2) & 1, (k >> 1) & 1, k & 1) for k in range(1, 8)]

    def body(*refs):
        x_refs, out_refs, (send_sems, recv_sems, local_sems) = refs[:n], refs[n:2 * n], refs[2 * n:]
        me = _me()
        slot = lambda p: 4 * p[0] + 2 * p[1] + p[2]
        copies = []
        for i in range(n):
            mine = pltpu.make_async_copy(x_refs[i], out_refs[i].at[slot(me)], local_sems.at[i])
            mine.start()
            copies.append(mine)
            for k, f in enumerate(flips):
                peer = _flip(me, f)
                sems = dict(send_sem=send_sems.at[7 * i + k], recv_sem=recv_sems.at[7 * i + k], device_id=peer,
                            device_id_type=MESH)
                cp = pltpu.make_async_remote_copy(src_ref=x_refs[i], dst_ref=out_refs[i].at[slot(me)], **sems)
                cp.start()
                copies.append(cp)
                copies.append(pltpu.make_async_remote_copy(src_ref=x_refs[i], dst_ref=out_refs[i].at[slot(peer)], **sems))
        for i in range(n):
            base = i * 15
            copies[base].wait()
            for k in range(7):
                copies[base + 1 + 2 * k].wait_send()
                copies[base + 2 + 2 * k].wait_recv()

    outs = pl.pallas_call(
        body, name=name, in_specs=[ANY] * n, out_specs=[ANY] * n,
        out_shape=[jax.ShapeDtypeStruct((8, *x.shape), x.dtype) for x in xs],
        scratch_shapes=[pltpu.SemaphoreType.DMA((7 * n,)), pltpu.SemaphoreType.DMA((7 * n,)),
                        pltpu.SemaphoreType.DMA((n,))])(*xs)
    return list(outs)


CHIP_FLIPS = [(1, 0, 0), (0, 1, 0), (1, 1, 0)]


def _chip():
    return 2 * lax.axis_index("x") + lax.axis_index("y")


HBM = pl.BlockSpec(memory_space=pltpu.HBM)
SEM = pl.BlockSpec(memory_space=pltpu.SEMAPHORE)
EFFECT = pltpu.SideEffectType.DATAFLOW_SIDE_EFFECTING


def _plan_copies(plan, refs, send_sems, recv_sems):
    return [pltpu.make_async_remote_copy(src_ref=src, dst_ref=dst, send_sem=send_sems.at[k], recv_sem=recv_sems.at[k],
                                         device_id=to, device_id_type=MESH) for k, (src, dst, to) in enumerate(plan(refs))]


def _rdma_start(arrays, n_copies, plan, deps, *, name):
    n, nd = len(arrays), len(deps)

    def body(*refs):
        for cp in _plan_copies(plan, refs[:n], refs[n + nd], refs[n + nd + 1]):
            cp.start()
        refs[-1][...] = jnp.zeros_like(refs[-1])

    outs = pl.pallas_call(
        body, name=name,
        out_shape=(pltpu.SemaphoreType.DMA((n_copies,)), pltpu.SemaphoreType.DMA((n_copies,)),
                   *[pltpu.HBM(a.shape, a.dtype) for a in arrays], jax.ShapeDtypeStruct((8, LANE), F32)),
        in_specs=[HBM] * n + [ANY] * nd, out_specs=(SEM, SEM, *[HBM] * n, pl.BlockSpec(memory_space=pltpu.VMEM)),
        input_output_aliases={i: i + 2 for i in range(n)}, compiler_params=pltpu.CompilerParams(has_side_effects=EFFECT),
    )(*[pltpu.with_memory_space_constraint(a, pltpu.HBM) for a in arrays], *deps)
    return outs[0], outs[1], list(outs[2:2 + n]), outs[-1]


def _rdma_wait(send_sems, recv_sems, arrays, plan, after, *, name):
    n = len(arrays)

    def body(*refs):
        for cp in _plan_copies(plan, refs[:n], refs[n], refs[n + 1]):
            cp.wait_send()
            cp.wait_recv()

    return list(pl.pallas_call(
        body, name=name, out_shape=tuple(pltpu.HBM(a.shape, a.dtype) for a in arrays),
        in_specs=[HBM] * n + [SEM, SEM, ANY], out_specs=tuple([HBM] * n), input_output_aliases={i: i for i in range(n)},
        compiler_params=pltpu.CompilerParams(has_side_effects=EFFECT),
    )(*arrays, send_sems, recv_sems, after))


def _gather_plan(n):
    def plan(refs):
        me = _me()
        slot = 2 * me[0] + me[1]
        return [(refs[i].at[me[2]], refs[n + i].at[slot, me[2]], _flip(me, f)) for i in range(n) for f in CHIP_FLIPS]
    return plan


def _scatter_plan(n):
    def plan(refs):
        me = _me()
        out = []
        for i in range(n):
            for k, f in enumerate(CHIP_FLIPS):
                peer = _flip(me, f)
                out.append((refs[i].at[2 * peer[0] + peer[1]], refs[n + i].at[k], peer))
        return out
    return plan


def _sibling_plan(n, src_of):
    def plan(refs):
        me = _me()
        return [(src_of(refs[i], me[2]), refs[n + i], _flip(me, (0, 0, 1))) for i in range(n)]
    return plan


def _gather8_plan(n):
    def plan(refs):
        me = _me()
        slot = 4 * me[0] + 2 * me[1] + me[2]
        return [(refs[i], refs[n + i].at[slot], _flip(me, ((k >> 2) & 1, (k >> 1) & 1, k & 1)))
                for i in range(n) for k in range(1, 8)]
    return plan


def _pair_fill(lands, *, name):
    n = len(lands)

    def body(*refs):
        in_refs, (send_sems, recv_sems) = refs[:n], refs[2 * n:]
        me = _me()
        sib = _flip(me, (0, 0, 1))
        copies = []
        for i in range(n):
            for k, f in enumerate(CHIP_FLIPS):
                peer = _flip(me, f)
                slot = 2 * peer[0] + peer[1]
                mine, theirs = in_refs[i].at[slot, me[2]], in_refs[i].at[slot, 1 - me[2]]
                cp = pltpu.make_async_remote_copy(src_ref=mine, dst_ref=mine, send_sem=send_sems.at[3 * i + k],
                                                  recv_sem=recv_sems.at[3 * i + k], device_id=sib, device_id_type=MESH)
                cp.start()
                copies.append((cp, pltpu.make_async_remote_copy(
                    src_ref=mine, dst_ref=theirs, send_sem=send_sems.at[3 * i + k], recv_sem=recv_sems.at[3 * i + k],
                    device_id=sib, device_id_type=MESH)))
        for cp, arrival in copies:
            arrival.wait_recv()
            cp.wait_send()

    return list(pl.pallas_call(
        body, name=name, in_specs=[ANY] * n, out_specs=[ANY] * n,
        out_shape=[jax.ShapeDtypeStruct(a.shape, a.dtype) for a in lands], input_output_aliases={i: i for i in range(n)},
        scratch_shapes=[pltpu.SemaphoreType.DMA((3 * n,)), pltpu.SemaphoreType.DMA((3 * n,))])(*lands))


def _own_and_landed(lands, xs):
    chip = _chip()
    return [[jnp.where(chip == j, x, o.reshape(4, *x.shape)[j]) for j in range(4)] for o, x in zip(lands, xs)]


BIG = (("w_in", (D, IN_WIDTH // 4), 1), ("gla_w_o", (D // 4, D), 0), ("mla_w_uq", (MQR, MH * MQK // 4), 1),
       ("mla_w_ukv", (MKVR, MH * (MNOPE + MVD) // 4), 1), ("mla_w_o", (D // 4, D), 0), ("w_out", (D // 4, D), 0),
       ("mlp_w1", (D, DFF // 4), 1), ("mlp_w2", (DFF // 4, D), 0))
ADA_SHARD = (D, 6 * D // 4)
SMALL = (("b_ada", 6 * D), ("norm1_g", D), ("b_merge", 2 * D), ("gla_b_alpha", GH * GDK), ("gla_out_norm_g", GDV),
         ("mla_q_lat_g", MQR), ("mla_kv_lat_g", MKVR), ("mla_qn_g", MQK), ("mla_kn_g", MQK), ("norm2_g", D))


W_IN_SEGMENTS = ((0, 3072, OFF_Q), (3072, 3088, OFF_A), (3088, 3344, OFF_CQ), (3344, 3472, OFF_CKV),
                 (3472, 3504, OFF_KPE + MNOPE), (3504, 5552, OFF_MA))
W_IN_SPLIT = OFF_MA
SMALL_ROWS, SMALL_COLS = 32, 2 * D
W_ALPHA_ROW = 16
LOSS_ROW = 15
SMALL_RED = tuple((n, k) for n, k in SMALL if n != "b_ada")


def _pack_small(grads, d_w_alpha, loss_row, *, name):
    def body(*refs):
        g_refs, wa_ref, loss_ref, out_ref = refs[:-3], refs[-3], refs[-2], refs[-1]
        out_ref[...] = jnp.zeros_like(out_ref)
        for i, ((_, k), g_ref) in enumerate(zip(SMALL_RED, g_refs)):
            out_ref[i:i + 1, 0:k] = g_ref[...]
        out_ref[LOSS_ROW:LOSS_ROW + 1, 0:LANE] = loss_ref[...]
        out_ref[W_ALPHA_ROW:W_ALPHA_ROW + GLR, 0:GH * GDK] = wa_ref[...]

    return pl.pallas_call(body, name=name, out_shape=jax.ShapeDtypeStruct((SMALL_ROWS, SMALL_COLS), F32))(
        *grads, d_w_alpha, loss_row)


def _small_update(gathered, dmod_all, sel, wmv, *, name):
    names = [n for n, _ in SMALL] + ["gla_w_alpha"]
    n_par = len(names)

    def body(sel_ref, g_ref, dmod_ref, *refs):
        in_refs, out_refs, loss_ref, acc = refs[:3 * n_par], refs[3 * n_par:-2], refs[-2], refs[-1]
        total = g_ref[0]
        for j in range(1, 8):
            total = total + g_ref[j]
        acc[...] = total
        loss_ref[...] = acc[LOSS_ROW:LOSS_ROW + 1, 0:LANE]
        row = {n: i for i, (n, _) in enumerate(SMALL_RED)}
        for p, name_p in enumerate(names):
            w_ref, m_ref, v_ref = in_refs[3 * p:3 * p + 3]
            if name_p == "b_ada":
                gv = jnp.sum(dmod_ref[...], axis=0, keepdims=True)
            elif name_p == "gla_w_alpha":
                gv = jnp.zeros((GLR, GDK), F32)
                for j in range(4):
                    blk = acc[W_ALPHA_ROW:W_ALPHA_ROW + GLR, j * GDK:(j + 1) * GDK]
                    gv = gv + jnp.where(sel_ref[0] == j, blk, 0.0)
            else:
                gv = acc[row[name_p]:row[name_p] + 1, 0:w_ref.shape[1]]
            o = out_refs[4 * p:4 * p + 4]
            o[0][...] = gv
            o[1][...], o[2][...], o[3][...] = _adamw_update(w_ref[...], gv, m_ref[...], v_ref[...])

    flat = [a for t in wmv for a in t]
    out_shape = [jax.ShapeDtypeStruct(t[0].shape, F32) for t in wmv for _ in range(4)]
    out_shape.append(jax.ShapeDtypeStruct((1, LANE), F32))
    vmem = pl.BlockSpec(memory_space=pltpu.VMEM)
    outs = pl.pallas_call(
        body, name=name, out_shape=out_shape, in_specs=[pl.BlockSpec(memory_space=pltpu.SMEM), vmem, vmem] + [vmem] * len(flat),
        out_specs=[vmem] * len(out_shape), scratch_shapes=[pltpu.VMEM((SMALL_ROWS, SMALL_COLS), F32)],
    )(sel, gathered, dmod_all, *flat)
    return {n: tuple(outs[4 * p:4 * p + 4]) for p, n in enumerate(names)}, outs[-1][0, 0]


def _full_weights(gathered):
    w = {name: jnp.concatenate(gathered[name], axis=axis) for name, _, axis in BIG if name in gathered and name != "w_in"}
    if "w_in" in gathered:
        shards = gathered["w_in"]
        zeros = lambda n: [jnp.zeros((D, n), shards[0].dtype)]

        def cols(a, b):
            width = IN_WIDTH // 4
            return [shards[j][:, max(a, j * width) - j * width:min(b, (j + 1) * width) - j * width]
                    for j in range(4) if max(a, j * width) < min(b, (j + 1) * width)]

        parts = []
        for a, b, at in sorted(W_IN_SEGMENTS, key=lambda seg: seg[2]):
            have = sum(p.shape[1] for p in parts)
            parts += (zeros(at - have) if at > have else []) + cols(a, b)
        w["w_in"] = jnp.concatenate(parts + zeros(PW - sum(p.shape[1] for p in parts)), axis=1)
    if "mla_w_uq" in w:
        w["mla_w_uq"] = jnp.pad(w["mla_w_uq"].reshape(MQR, MH, MQK), ((0, 0), (0, 0), (0, LANE - MQK))).reshape(MQR, MH * LANE)
    if "mla_w_o" in w:
        w["mla_w_o"] = jnp.pad(w["mla_w_o"].reshape(MH, MVD, D), ((0, 0), (0, LANE - MVD), (0, 0))).reshape(MH * LANE, D)
    return w


def _grad_slots(g):
    g = dict(g)
    out = {}
    if "w_in" in g:
        g_lo, g_hi = g.pop("w_in")
        take = lambda at, lo, hi: g_lo[:, at + lo:at + hi] if at < W_IN_SPLIT else g_hi[:, at - W_IN_SPLIT + lo:at - W_IN_SPLIT + hi]
        width = IN_WIDTH // 4
        slots = []
        for j in range(4):
            lo, hi = j * width, (j + 1) * width
            slots.append(jnp.concatenate([take(at, max(lo, a) - a, min(hi, b) - a)
                                          for a, b, at in W_IN_SEGMENTS if max(lo, a) < min(hi, b)], axis=1))
        out["w_in"] = jnp.stack(slots).reshape(4, 2, D // 2, width)
    if "mla_w_uq" in g:
        g["mla_w_uq"] = g["mla_w_uq"].reshape(MQR, MH, LANE)[:, :, :MQK].reshape(MQR, MH * MQK)
    if "mla_w_o" in g:
        g["mla_w_o"] = g["mla_w_o"].reshape(MH, LANE, D)[:, :MVD].reshape(MH * MVD, D)
    for name, (rows, cols), axis in BIG:
        if name not in g:
            continue
        a = g[name]
        a = a.reshape(4, rows, cols) if axis == 0 else jnp.transpose(a.reshape(rows, 4, cols), (1, 0, 2))
        out[name] = a.reshape(4, 2, rows // 2, cols)
    return out


def _rope_tables(positions):
    freqs = ROPE_THETA ** (-jnp.arange(0, MROPE, 2, dtype=F32) / MROPE)
    lane = np.arange(LANE)
    in_rope = (lane >= MNOPE) & (lane < MQK)
    freq_lane = jnp.where(in_rope, freqs[(lane - MNOPE) % (MROPE // 2)], 0.0)
    sign = np.where(in_rope, np.where(lane < MNOPE + MROPE // 2, -1.0, 1.0), 0.0).astype(np.float32)
    ang = positions.astype(F32).reshape(-1, 1) * freq_lane[None, :]
    return jnp.cos(ang), jnp.sin(ang) * sign[None, :]


def _local_step(x, positions, mod, target, w, small, more_weights=None, on_grads=None):
    kept = {}
    if on_grads is None:
        on_grads = lambda tag, grads, after: kept.update(grads)
    bsz, s, _ = x.shape
    t = bsz * s
    tt = _tile(t, 1024)
    shift1, scale1, gate1, shift2, scale2, gate2 = [mod[:, None, i * D:(i + 1) * D] for i in range(6)]
    cos_t, sin_t = _rope_tables(positions)
    w_alpha_p = jnp.pad(small["gla_w_alpha"], ((0, LANE - GLR), (0, 0)))
    gq = jnp.pad(small["mla_qn_g"], ((0, 0), (0, LANE - MQK)))
    gk = jnp.pad(small["mla_kn_g"], ((0, 0), (0, LANE - MQK)))
    flat2 = lambda a: a.reshape(t, a.shape[-1])
    bsd = lambda a: a.reshape(bsz, s, a.shape[-1])

    h = _norm_mod(x, small["norm1_g"], scale1, shift1, name="norm1")
    if callable(w):
        w = w(h)
    proj = _mm(flat2(h), w["w_in"], name="proj", tn=1152)
    proj3 = bsd(proj)
    o, o_gated, states = _gla_fwd(proj3, w_alpha_p, small["gla_b_alpha"], small["gla_out_norm_g"], name="gla_fwd")
    if more_weights is not None:
        w = {**w, **more_weights(o_gated)}
    y_a = _mm(flat2(o_gated), w["gla_w_o"], name="gla_out")
    cq_n, ckv_n = _lat_norm(proj, small["mla_q_lat_g"], small["mla_kv_lat_g"], name="lat_norm")
    q_raw = _mm(cq_n, w["mla_w_uq"], name="mla_uq")
    kv = _mm(ckv_n, w["mla_w_ukv"], name="mla_ukv")
    qf, kf, vf = _qk_prep(q_raw, kv, proj, cos_t, sin_t, gq * Q_PRESCALE, gk, name="qk_prep")
    o_attn = _attn_fwd(bsd(qf), bsd(kf), bsd(vf), name="attn_fwd")
    y_b = _mm(flat2(o_attn), w["mla_w_o"], name="mla_out")
    mixed_in = _merge_fwd(proj3, small["b_merge"], bsd(y_a), bsd(y_b), name="merge_fwd")
    mixed = _mm(flat2(mixed_in), w["w_out"], name="w_out")
    x1, h2 = _resid_norm_mod(x, bsd(mixed), gate1, small["norm2_g"], scale2, shift2, name="norm2")

    def sqrelu(acc, ex, outs):
        r = jnp.maximum(acc, 0.0)
        outs[0][...] = (r * r).astype(BF16)

    r = _mm(flat2(h2), w["mlp_w1"], name="mlp1", epilogue=sqrelu, out_shape=jax.ShapeDtypeStruct((t, DFF), BF16),
            out_specs=_tile_spec(tt, 1024))
    ff = _mm(r, w["mlp_w2"], name="mlp2")
    dy, dff, dgate2, loss_part = _loss_head(x1, bsd(ff), gate2, target, name="loss_head")

    g = {}

    def relu2_bwd(acc, ex, outs):
        outs[0][...] = (acc * (2.0 * jnp.sqrt(ex[0][...].astype(F32)))).astype(BF16)

    dff2 = flat2(dff)
    da1 = _mm(dff2, w["mlp_w2"], tb=True, name="mlp2_dx", epilogue=relu2_bwd, extras=(r,),
              extra_specs=(_tile_spec(tt, 1024),), out_shape=jax.ShapeDtypeStruct((t, DFF), BF16),
              out_specs=_tile_spec(tt, 1024))
    g["mlp_w2"] = _mm(r, dff2, ta=True, name="mlp2_dw")
    dh2 = _mm(da1, w["mlp_w1"], tb=True, name="mlp1_dx")
    g["mlp_w1"] = _mm(flat2(h2), da1, ta=True, name="mlp1_dw")
    token = on_grads("mlp", {n: g.pop(n) for n in ("mlp_w2", "mlp_w1")}, dh2)
    if token is not None:
        gate1 = gate1 + token[0, 0]
    dx1, dscale2, dshift2, dg2, dgate1, dmixed = _norm_mod_bwd(
        bsd(dh2), x1, dy, small["norm2_g"], scale2, gate1, bsd(mixed), name="norm2_bwd")
    dmixed2 = flat2(dmixed)
    dmi = _mm(dmixed2, w["w_out"], tb=True, name="w_out_dx")
    g["w_out"] = _mm(flat2(mixed_in), dmixed2, ta=True, name="w_out_dw")
    dy_a, dy_b, dl_a, dl_b, db_a, db_b = _merge_bwd(bsd(dmi), proj3, small["b_merge"], bsd(y_a), bsd(y_b), name="merge_bwd")
    dy_a2, dy_b2 = flat2(dy_a), flat2(dy_b)
    dog = _mm(dy_a2, w["gla_w_o"], tb=True, name="gla_out_dx")
    g["gla_w_o"] = _mm(flat2(o_gated), dy_a2, ta=True, name="gla_out_dw")
    dq_g, dk_g, dv_g, dg_g, dlog, db_alpha, d_ong = _gla_bwd(
        bsd(dog), o, states, proj3, w_alpha_p, small["gla_b_alpha"], small["gla_out_norm_g"], name="gla_bwd")
    dlog2 = flat2(dlog)
    da_p = _mm(dlog2, w_alpha_p, tb=True, out_dtype=BF16, name="alpha_dx")
    d_w_alpha = _mm(proj[:, OFF_A:OFF_A + LANE], dlog2, ta=True, name="alpha_dw")[:GLR]
    do_attn = _mm(dy_b2, w["mla_w_o"], tb=True, out_dtype=BF16, name="mla_out_dx")
    g["mla_w_o"] = _mm(flat2(o_attn), dy_b2, ta=True, name="mla_out_dw")
    dqf, dkf, dvf = _attn_bwd(bsd(qf), bsd(kf), bsd(vf), bsd(do_attn), name="attn_bwd")
    dq_raw, dkv, dkpe, dgq, dgk = _qk_prep_bwd(flat2(dqf), flat2(dkf), flat2(dvf), q_raw, kv, proj, cos_t, sin_t, gq, gk,
                                                name="qk_prep_bwd")
    dcq_n = _mm(dq_raw, w["mla_w_uq"], tb=True, name="mla_uq_dx")
    g["mla_w_uq"] = _mm(cq_n, dq_raw, ta=True, name="mla_uq_dw")
    dckv_n = _mm(dkv, w["mla_w_ukv"], tb=True, name="mla_ukv_dx")
    g["mla_w_ukv"] = _mm(ckv_n, dkv, ta=True, name="mla_ukv_dw")
    token = on_grads("mix", {n: g.pop(n) for n in ("w_out", "gla_w_o", "mla_w_o", "mla_w_uq", "mla_w_ukv")}, dckv_n)
    q_lat_g = small["mla_q_lat_g"] if token is None else small["mla_q_lat_g"] + token[0:1, 0:1]
    dcq, dckv, dg_qlat, dg_kvlat = _lat_norm_bwd(dcq_n, dckv_n, proj, q_lat_g, small["mla_kv_lat_g"],
                                                  name="lat_norm_bwd")
    pieces = [(flat2(dq_g), OFF_Q), (flat2(dk_g), OFF_K), (flat2(dv_g), OFF_V), (flat2(dg_g), OFF_G),
              (flat2(dl_a), OFF_MA), (flat2(dl_b), OFF_MB), (dcq, OFF_CQ), (dckv, OFF_CKV), (da_p, OFF_A), (dkpe, OFF_KPE)]
    hb = flat2(h)
    g_w_in = (_pieces_dw(hb, [p for p, off in pieces if off < W_IN_SPLIT], name="proj_dw_a"),
              _pieces_dw(hb, [p for p, off in pieces if off >= W_IN_SPLIT], name="proj_dw_b"))
    token = on_grads("in", {"w_in": g_w_in}, g_w_in[1])
    after = jnp.zeros((8, LANE), F32) if token is None else token
    dh = _pieces_dx(pieces, w["w_in"], after, name="proj_dx")
    token = on_grads("dx", {}, dh)
    if token is not None:
        scale1 = scale1 + token[0, 0]
    grad_x, dscale1, dshift1, dg1 = _norm_mod_bwd(bsd(dh), x, dx1, small["norm1_g"], scale1, name="norm1_bwd")

    dmod = jnp.concatenate([dshift1, dscale1, dgate1, dshift2, dscale2, dgate2], axis=-1).reshape(bsz, 6 * D)
    gs = {"norm1_g": dg1, "b_merge": jnp.concatenate([db_a, db_b], axis=1), "gla_b_alpha": db_alpha,
          "gla_out_norm_g": d_ong, "mla_q_lat_g": dg_qlat, "mla_kv_lat_g": dg_kvlat, "mla_qn_g": dgq[:, :MQK],
          "mla_kn_g": dgk[:, :MQK], "norm2_g": dg2}
    return loss_part[0, 0], grad_x, dmod, {**kept, **g}, gs, d_w_alpha


def kernel(x, c, positions, w_ada, b_ada, norm1_g, w_in, b_merge, gla_w_alpha, gla_b_alpha, gla_out_norm_g, gla_w_o, mla_q_lat_g, mla_w_uq, mla_kv_lat_g, mla_w_ukv, mla_qn_g, mla_kn_g, mla_w_o, w_out, norm2_g, mlp_w1, mlp_w2, loss_target, m_w_ada, m_b_ada, m_norm1_g, m_w_in, m_b_merge, m_gla_w_alpha, m_gla_b_alpha, m_gla_out_norm_g, m_gla_w_o, m_mla_q_lat_g, m_mla_w_uq, m_mla_kv_lat_g, m_mla_w_ukv, m_mla_qn_g, m_mla_kn_g, m_mla_w_o, m_w_out, m_norm2_g, m_mlp_w1, m_mlp_w2, v_w_ada, v_b_ada, v_norm1_g, v_w_in, v_b_merge, v_gla_w_alpha, v_gla_b_alpha, v_gla_out_norm_g, v_gla_w_o, v_mla_q_lat_g, v_mla_w_uq, v_mla_kv_lat_g, v_mla_w_ukv, v_mla_qn_g, v_mla_kn_g, v_mla_w_o, v_w_out, v_norm2_g, v_mlp_w1, v_mlp_w2):
    args = dict(locals())
    names_big = [n for n, _, _ in BIG]
    names_small = [n for n, _ in SMALL]
    bsz = x.shape[0]
    ax, ay, ac = lax.axis_index("x"), lax.axis_index("y"), lax.axis_index("c")
    chip = 2 * ax + ay
    dev = 2 * chip + ac

    small = {n: args[n] for n in names_small}
    sel_c = jnp.reshape(ac, (1,)).astype(jnp.int32)
    sel_chip = jnp.reshape(chip, (1,)).astype(jnp.int32)
    c_all, w_alpha_all = _all_gather8([c, gla_w_alpha[0]], name="comm_c_alpha")
    small["gla_w_alpha"] = jnp.concatenate([w_alpha_all[2 * j] for j in range(4)], axis=1)
    c_all = c_all.reshape(8 * bsz, D)

    shards = {n: args[n][0].astype(BF16) for n in names_big}
    halves_of = lambda names: [shards[n].reshape(2, shards[n].shape[0] // 2, shards[n].shape[1]) for n in names]

    def gather_start(names, deps, tag):
        xs = halves_of(names)
        lands = [lax.empty((4, *xh.shape), BF16) for xh in xs]
        plan = _gather_plan(len(names))
        return names, plan, _rdma_start(xs + lands, 3 * len(names), plan, deps, name="comm_weights_start_" + tag)

    def gather_finish(started, after, tag):
        names, plan, sems = started
        arrs = _rdma_wait(sems[0], sems[1], sems[2], plan, after, name="comm_weights_wait_" + tag)
        filled = _pair_fill(arrs[len(names):], name="comm_weights_pair_" + tag)
        own = [a.reshape(shards[n].shape) for n, a in zip(names, arrs)]
        return _full_weights(dict(zip(names, _own_and_landed(filled, own))))


    def add_bias(acc, ex, outs):
        outs[0][...] = acc + ex[0][...]

    silu = lambda v: v * _sigmoid(v)
    b_ada_mine = lax.dynamic_slice(b_ada, (0, chip * ADA_SHARD[1]), (1, ADA_SHARD[1]))
    mod_part = _mm(c_all, w_ada[0], name="ada", tn=512, a_fn=silu, epilogue=add_bias, extras=(b_ada_mine,),
                   extra_specs=(pl.BlockSpec((1, 512), lambda i, j, k: (0, j)),),
                   out_shape=jax.ShapeDtypeStruct((8 * bsz, ADA_SHARD[1]), F32), out_specs=_tile_spec(8 * bsz, 512))
    mod_all = _all_gather8([mod_part], name="comm_mod")[0]
    mod_rows = lax.dynamic_slice(mod_all, (0, dev * bsz, 0), (8, bsz, ADA_SHARD[1]))
    mod = jnp.concatenate([mod_rows[2 * j] for j in range(4)], axis=1)
    first = gather_start(["w_in"], (mod,), "in")
    rest = gather_start([n for n in names_big if n != "w_in"], (mod, first[2][3]), "rest")
    mod = mod + rest[2][3][0, 0]
    w_in_after = lambda after: gather_finish(first, after, "in")
    more_weights = lambda after: gather_finish(rest, after, "rest")

    stage = {}

    def begin(tag, names, arrays, lands, n_copies, plan, what):
        stage[tag] = (names, plan, _rdma_start(arrays + lands, n_copies, plan, (), name=f"comm_{what}_start_{tag}"))
        return stage[tag][2][3]

    def landed(tag, after, what):
        names, plan, sems = stage[tag]
        arrs = _rdma_wait(sems[0], sems[1], sems[2], plan, after, name=f"comm_{what}_wait_{tag}")
        return names, arrs[:len(arrs) // 2], arrs[len(arrs) // 2:]

    def swap_start(tag, grads):
        names = list(grads)
        parts = [_grad_slots(grads)[n] for n in names]
        lands = [lax.empty((4, *p.shape[2:]), F32) for p in parts]
        return begin(tag, names, parts, lands, len(names), _sibling_plan(len(names), lambda r, c: r.at[:, 1 - c]), "pair_sum")

    def scatter_start(tag, after):
        names, parts, sib_halves = landed(tag, after, "pair_sum")
        pairs = [_pair_add(p, s, sel_c, name="pair_add_" + n) for n, p, s in zip(names, parts, sib_halves)]
        recvs = [lax.empty((3, *p.shape[1:]), BF16) for p in pairs]
        return begin(tag, names, pairs, recvs, 3 * len(names), _scatter_plan(len(names)), "scatter")

    def join_start(tag, after):
        names, pairs, recvs = landed(tag, after, "scatter")
        halves = [_chip_sum(p, r, sel_chip, name="chip_sum_" + n) for n, p, r in zip(names, pairs, recvs)]
        lands = [lax.empty(h.shape, F32) for h in halves]
        return begin(tag, names, halves, lands, len(names), _sibling_plan(len(names), lambda r, c: r), "pair_join")

    def reduce_step(tag, grads, after):
        if tag == "mlp":
            return swap_start("mlp", grads)
        if tag == "mix":
            return scatter_start("mlp", after) + swap_start("mix", grads)
        if tag == "in":
            return scatter_start("mix", after) + swap_start("in", grads)
        return scatter_start("in", after)

    loss_part, grad_x, dmod, g, gs, d_w_alpha = _local_step(x, positions, mod, loss_target, w_in_after, small,
                                                            more_weights, reduce_step)

    assert not g, list(g)
    gs_packed = _pack_small([gs[n] for n, _ in SMALL_RED], d_w_alpha, jnp.full((1, LANE), loss_part, F32),
                            name="pack_small")
    small_lands = [lax.empty((8, *a.shape), F32) for a in (dmod, gs_packed)]
    begin("small", ["dmod", "small"], [dmod, gs_packed], small_lands, 7 * 2, _gather8_plan(2), "gather8")

    res = {}

    def finish(tag, after):
        names, halves, theirs = landed(tag, after, "pair_join")
        for n, mine, other in zip(names, halves, theirs):
            if n == "w_in":
                south = ac == 0
                g_t = jnp.concatenate([jnp.where(south, mine, other), jnp.where(south, other, mine)], axis=0).T
                outs = _adamw(w_in[0].T, g_t, m_w_in[0].T, v_w_in[0].T, name="adamw_w_in", by_cols=True)
                res[n] = tuple(a.T for a in (g_t, *outs))
            else:
                res[n] = _adamw_halves(args[n][0], args["m_" + n][0], args["v_" + n][0], mine, other, sel_c,
                                       name="adamw_" + n)
        return res[names[-1]][1]

    join_start("mlp", grad_x)
    join_start("mix", grad_x)
    done = finish("mix", finish("mlp", grad_x))

    _, (dmod_own, gs_own), (dmod_all, gs_all) = landed("small", done, "gather8")
    dmod_all = lax.dynamic_update_slice(dmod_all, dmod_own[None], (dev, 0, 0)).reshape(8 * bsz, 6 * D)
    gs_all = lax.dynamic_update_slice(gs_all, gs_own[None], (dev, 0, 0))
    dmod_mine = lax.dynamic_slice(dmod_all, (0, chip * ADA_SHARD[1]), (8 * bsz, ADA_SHARD[1]))
    g_w_ada = _mm(c_all, dmod_mine, ta=True, a_fn=silu, name="ada_dw")
    wmv = [(args[n], args["m_" + n], args["v_" + n]) for n in names_small]
    wmv.append((gla_w_alpha[0], m_gla_w_alpha[0], v_gla_w_alpha[0]))
    res_small, loss_sum = _small_update(gs_all, dmod_all, sel_chip, wmv, name="small_update")
    res.update(res_small)
    loss = loss_sum * (0.5 / D)
    join_start("in", g_w_ada)
    res["w_ada"] = (g_w_ada, *_adamw(w_ada[0], g_w_ada, m_w_ada[0], v_w_ada[0], name="adamw_w_ada"))
    finish("in", res["w_ada"][1])

    order = ["w_ada", "b_ada", "norm1_g", "w_in", "b_merge", "gla_w_alpha", "gla_b_alpha", "gla_out_norm_g", "gla_w_o",
             "mla_q_lat_g", "mla_w_uq", "mla_kv_lat_g", "mla_w_ukv", "mla_qn_g", "mla_kn_g", "mla_w_o", "w_out",
             "norm2_g", "mlp_w1", "mlp_w2"]
    named = lambda k: [res[n][k].reshape(args[n].shape) for n in order]
    return (loss, grad_x, *named(0), *named(1), *named(2), *named(3))
```

```python
import jax
import jax.numpy as jnp
import numpy as np
from jax import lax
from jax.experimental import pallas as pl
from jax.experimental.pallas import tpu as pltpu

F32 = jnp.float32
BF16 = jnp.bfloat16
MESH = pl.DeviceIdType.MESH

D = 1024
CHUNK = 64
EPS = 1e-6
GH, GDK, GDV, GLR, GTAU = 4, 128, 256, 16, 16.0
MH, MQR, MKVR, MNOPE, MROPE, MVD = 16, 256, 128, 64, 32, 64
MQK = MNOPE + MROPE
DFF = 4 * D
ROPE_THETA = 10000.0
IN_WIDTH = 5552
LANE = 128
OFF_Q, OFF_K, OFF_V, OFF_G, OFF_MA, OFF_MB, OFF_CQ, OFF_CKV, OFF_A, OFF_KPE, PW = (
    0, 512, 1024, 2048, 3072, 4096, 5120, 5376, 5504, 5632, 5760)
ADAM_LR, ADAM_B1, ADAM_B2, ADAM_EPS, ADAM_WD, ADAM_STEP = 0.001, 0.9, 0.999, 1e-08, 0.01, 10
VMEM_LIMIT = 48 * 1024 * 1024


def _params(n_axes):
    return pltpu.CompilerParams(dimension_semantics=("arbitrary",) * n_axes, vmem_limit_bytes=VMEM_LIMIT)


def _tile(n, target):
    if n <= target:
        return n
    best = None
    for t in range(LANE, target + 1, LANE):
        if n % t == 0:
            best = t
    assert best is not None, (n, target)
    return best


def _sigmoid(x):
    return 1.0 / (1.0 + jnp.exp(-x))


MM_VMEM_BUDGET = 36 * 1024 * 1024


def _mm(a, b, *, name, ta=False, tb=False, out_dtype=F32, tm=1024, tn=1024, tk=4096,
        epilogue=None, extras=(), extra_specs=(), out_shape=None, out_specs=None, a_fn=None):
    if ta:
        kdim, m = a.shape
    else:
        m, kdim = a.shape
    if tb:
        n, k2 = b.shape
    else:
        k2, n = b.shape
    assert kdim == k2, (a.shape, b.shape)
    tm, tn, tk = _tile(m, tm), _tile(n, tn), _tile(kdim, tk)
    tiles = lambda rows: 2 * (rows * tk * a.dtype.itemsize + tk * tn * b.dtype.itemsize + rows * tn * 4) + rows * tn * 4
    while out_shape is None and tiles(tm) > MM_VMEM_BUDGET and tm % 256 == 0:
        tm //= 2
    nk = kdim // tk
    a_spec = pl.BlockSpec((tk, tm), lambda i, j, k: (k, i)) if ta else pl.BlockSpec((tm, tk), lambda i, j, k: (i, k))
    b_spec = pl.BlockSpec((tn, tk), lambda i, j, k: (j, k)) if tb else pl.BlockSpec((tk, tn), lambda i, j, k: (k, j))
    dims = (((0 if ta else 1,), (1 if tb else 0,)), ((), ()))
    ne = len(extras)
    if out_shape is None:
        out_shape = jax.ShapeDtypeStruct((m, n), out_dtype)
        out_specs = pl.BlockSpec((tm, tn), lambda i, j, k: (i, j))
    n_out = len(out_shape) if isinstance(out_shape, (list, tuple)) else 1
    in_place = epilogue is None and n_out == 1 and out_shape.dtype == F32
    scratch = [] if (nk == 1 or in_place) else [pltpu.VMEM((tm, tn), F32)]

    def body(a_ref, b_ref, *rest):
        ex, outs = rest[:ne], rest[ne:ne + n_out]
        av = a_ref[...] if a_fn is None else a_fn(a_ref[...])
        prod = lax.dot_general(av.astype(BF16), b_ref[...].astype(BF16), dims, preferred_element_type=F32)

        def finish(val):
            if epilogue is None:
                outs[0][...] = val.astype(outs[0].dtype)
            else:
                epilogue(val, ex, outs)

        if nk == 1:
            finish(prod)
            return
        k = pl.program_id(2)
        acc = outs[0] if in_place else rest[-1]

        @pl.when(k == 0)
        def _():
            acc[...] = prod

        @pl.when(k > 0)
        def _():
            acc[...] += prod

        if not in_place:
            @pl.when(k == nk - 1)
            def _():
                finish(acc[...])

    return pl.pallas_call(
        body, name=name, grid=(m // tm, n // tn, nk),
        in_specs=[a_spec, b_spec, *extra_specs], out_specs=out_specs, out_shape=out_shape,
        scratch_shapes=scratch, compiler_params=_params(3),
    )(a, b, *extras)


def _tile_spec(tm, tn):
    return pl.BlockSpec((tm, tn), lambda i, j, k: (i, j))


def _pieces_dx(pieces, w, after, *, name, tm=256):
    t = pieces[0][0].shape[0]
    tm = _tile(t, tm)
    npc = len(pieces)

    def body(*refs):
        p_refs, w_ref, out_ref = refs[:npc], refs[npc], refs[-1]
        acc = None
        for (arr, off), p_ref in zip(pieces, p_refs):
            part = lax.dot_general(p_ref[...].astype(BF16), w_ref[:, off:off + arr.shape[1]], _NT,
                                   preferred_element_type=F32)
            acc = part if acc is None else acc + part
        out_ref[...] = acc

    return pl.pallas_call(
        body, name=name, grid=(t // tm,),
        in_specs=[pl.BlockSpec((tm, arr.shape[1]), lambda i: (i, 0)) for arr, _ in pieces]
        + [pl.BlockSpec(w.shape, lambda i: (0, 0)), pl.BlockSpec((8, LANE), lambda i: (0, 0))],
        out_specs=pl.BlockSpec((tm, w.shape[0]), lambda i: (i, 0)),
        out_shape=jax.ShapeDtypeStruct((t, w.shape[0]), F32), compiler_params=_params(1),
    )(*[arr for arr, _ in pieces], w, after)


def _pieces_dw(h, pieces, *, name, tk=1024):
    t, d = h.shape
    tk = _tile(t, tk)
    widths = [p.shape[1] for p in pieces]
    starts = [sum(widths[:i]) for i in range(len(pieces))]

    def body(h_ref, *refs):
        p_refs, out_ref = refs[:-1], refs[-1]
        first = pl.program_id(0) == 0
        hv = h_ref[...]
        for p_ref, start, width in zip(p_refs, starts, widths):
            part = lax.dot_general(hv, p_ref[...].astype(BF16), _TN, preferred_element_type=F32)
            cols = slice(start, start + width)

            @pl.when(first)
            def _():
                out_ref[:, cols] = part

            @pl.when(jnp.logical_not(first))
            def _():
                out_ref[:, cols] += part

    return pl.pallas_call(
        body, name=name, grid=(t // tk,),
        in_specs=[pl.BlockSpec((tk, d), lambda k: (k, 0))] + [pl.BlockSpec((tk, wd), lambda k: (k, 0)) for wd in widths],
        out_specs=pl.BlockSpec((d, sum(widths)), lambda k: (0, 0)),
        out_shape=jax.ShapeDtypeStruct((d, sum(widths)), F32), compiler_params=_params(1),
    )(h, *pieces)


def _rms(x, g):
    r = lax.rsqrt(jnp.mean(x * x, axis=-1, keepdims=True) + EPS)
    return x * r, r


def _row_spec(ts, width, col=0):
    return pl.BlockSpec((None, ts, width), lambda b, i: (b, i, col))


def _vec_spec(width):
    return pl.BlockSpec((None, 1, width), lambda b, i: (b, 0, 0))


def _gain_spec(width):
    return pl.BlockSpec((1, width), lambda b, i: (0, 0))


def _norm_mod(x, g, scale, shift, *, name, ts=512):
    bsz, s, d = x.shape
    ts = min(ts, s)

    def body(x_ref, g_ref, sc_ref, sh_ref, h_ref):
        xh, _ = _rms(x_ref[...], None)
        h_ref[...] = ((xh * g_ref[...]) * (1.0 + sc_ref[...]) + sh_ref[...]).astype(BF16)

    return pl.pallas_call(
        body, name=name, grid=(bsz, s // ts),
        in_specs=[_row_spec(ts, d), _gain_spec(d), _vec_spec(d), _vec_spec(d)],
        out_specs=_row_spec(ts, d), out_shape=jax.ShapeDtypeStruct((bsz, s, d), BF16),
        compiler_params=_params(2),
    )(x, g, scale, shift)


def _resid_norm_mod(x, mixed, gate, g, scale, shift, *, name, ts=512):
    bsz, s, d = x.shape
    ts = min(ts, s)

    def body(x_ref, mx_ref, gt_ref, g_ref, sc_ref, sh_ref, x1_ref, h_ref):
        x1 = x_ref[...] + gt_ref[...] * mx_ref[...]
        x1_ref[...] = x1
        xh, _ = _rms(x1, None)
        h_ref[...] = ((xh * g_ref[...]) * (1.0 + sc_ref[...]) + sh_ref[...]).astype(BF16)

    return pl.pallas_call(
        body, name=name, grid=(bsz, s // ts),
        in_specs=[_row_spec(ts, d), _row_spec(ts, d), _vec_spec(d), _gain_spec(d), _vec_spec(d), _vec_spec(d)],
        out_specs=[_row_spec(ts, d), _row_spec(ts, d)],
        out_shape=[jax.ShapeDtypeStruct((bsz, s, d), F32), jax.ShapeDtypeStruct((bsz, s, d), BF16)],
        compiler_params=_params(2),
    )(x, mixed, gate, g, scale, shift)


def _norm_mod_bwd(dh, xin, resid, g, scale, gate=None, mixed=None, *, name, ts=512):
    bsz, s, d = xin.shape
    ts = min(ts, s)
    gated = gate is not None

    def body(*refs):
        if gated:
            dh_ref, x_ref, rs_ref, g_ref, sc_ref, gt_ref, mx_ref, dx_ref, dsc_ref, dsh_ref, dg_ref, dgt_ref, dmx_ref = refs
        else:
            dh_ref, x_ref, rs_ref, g_ref, sc_ref, dx_ref, dsc_ref, dsh_ref, dg_ref = refs
        b, i = pl.program_id(0), pl.program_id(1)

        @pl.when(i == 0)
        def _():
            dsc_ref[...] = jnp.zeros_like(dsc_ref)
            dsh_ref[...] = jnp.zeros_like(dsh_ref)
            if gated:
                dgt_ref[...] = jnp.zeros_like(dgt_ref)

        @pl.when((i == 0) & (b == 0))
        def _():
            dg_ref[...] = jnp.zeros_like(dg_ref)

        dh_v, gv = dh_ref[...], g_ref[...]
        xh, r = _rms(x_ref[...], None)
        dsc_ref[...] += jnp.sum(dh_v * (xh * gv), axis=0, keepdims=True)
        dsh_ref[...] += jnp.sum(dh_v, axis=0, keepdims=True)
        dn = dh_v * (1.0 + sc_ref[...])
        dg_ref[...] += jnp.sum(dn * xh, axis=0, keepdims=True)
        dxh = dn * gv
        dx = rs_ref[...] + r * (dxh - xh * jnp.mean(dxh * xh, axis=-1, keepdims=True))
        dx_ref[...] = dx
        if gated:
            dgt_ref[...] += jnp.sum(dx * mx_ref[...], axis=0, keepdims=True)
            dmx_ref[...] = (dx * gt_ref[...]).astype(BF16)

    ins = [dh, xin, resid, g, scale]
    in_specs = [_row_spec(ts, d), _row_spec(ts, d), _row_spec(ts, d), _gain_spec(d), _vec_spec(d)]
    out_specs = [_row_spec(ts, d), _vec_spec(d), _vec_spec(d), _gain_spec(d)]
    out_shape = [jax.ShapeDtypeStruct((bsz, s, d), F32), jax.ShapeDtypeStruct((bsz, 1, d), F32),
                 jax.ShapeDtypeStruct((bsz, 1, d), F32), jax.ShapeDtypeStruct((1, d), F32)]
    if gated:
        ins += [gate, mixed]
        in_specs += [_vec_spec(d), _row_spec(ts, d)]
        out_specs += [_vec_spec(d), _row_spec(ts, d)]
        out_shape += [jax.ShapeDtypeStruct((bsz, 1, d), F32), jax.ShapeDtypeStruct((bsz, s, d), BF16)]
    return pl.pallas_call(
        body, name=name, grid=(bsz, s // ts), in_specs=in_specs, out_specs=out_specs, out_shape=out_shape,
        compiler_params=_params(2),
    )(*ins)


def _loss_head(x1, ff, gate2, target, *, name, ts=512):
    bsz, s, d = x1.shape
    ts = min(ts, s)

    def body(x1_ref, ff_ref, gt_ref, t_ref, dy_ref, dff_ref, dgt_ref, loss_ref, acc):
        b, i = pl.program_id(0), pl.program_id(1)

        @pl.when(i == 0)
        def _():
            dgt_ref[...] = jnp.zeros_like(dgt_ref)

        @pl.when((i == 0) & (b == 0))
        def _():
            acc[...] = jnp.zeros_like(acc)

        ffv, gt = ff_ref[...], gt_ref[...]
        diff = (x1_ref[...] + gt * ffv) - t_ref[...]
        acc[...] += jnp.sum((diff * diff).reshape(ts // 8, 8, d), axis=0)
        dy = diff * (1.0 / d)
        dy_ref[...] = dy
        dgt_ref[...] += jnp.sum(dy * ffv, axis=0, keepdims=True)
        dff_ref[...] = (dy * gt).astype(BF16)

        @pl.when((i == pl.num_programs(1) - 1) & (b == pl.num_programs(0) - 1))
        def _():
            loss_ref[...] = jnp.full(loss_ref.shape, jnp.sum(acc[...]), F32)

    return pl.pallas_call(
        body, name=name, grid=(bsz, s // ts),
        in_specs=[_row_spec(ts, d), _row_spec(ts, d), _vec_spec(d), _row_spec(ts, d)],
        out_specs=[_row_spec(ts, d), _row_spec(ts, d), _vec_spec(d), pl.BlockSpec((8, LANE), lambda b, i: (0, 0))],
        out_shape=[jax.ShapeDtypeStruct((bsz, s, d), F32), jax.ShapeDtypeStruct((bsz, s, d), BF16),
                   jax.ShapeDtypeStruct((bsz, 1, d), F32), jax.ShapeDtypeStruct((8, LANE), F32)],
        scratch_shapes=[pltpu.VMEM((8, d), F32)], compiler_params=_params(2),
    )(x1, ff, gate2, target)


def _merge_fwd(proj, b_merge, y_a, y_b, *, name, ts=512):
    bsz, s, _ = proj.shape
    ts = min(ts, s)

    def body(la_ref, lb_ref, ba_ref, bb_ref, ya_ref, yb_ref, out_ref):
        ga = _sigmoid(la_ref[...] + ba_ref[...])
        gb = _sigmoid(lb_ref[...] + bb_ref[...])
        out_ref[...] = (ga * ya_ref[...] + gb * yb_ref[...]).astype(BF16)

    return pl.pallas_call(
        body, name=name, grid=(bsz, s // ts),
        in_specs=[_row_spec(ts, D, OFF_MA // D), _row_spec(ts, D, OFF_MB // D),
                  pl.BlockSpec((1, D), lambda b, i: (0, 0)), pl.BlockSpec((1, D), lambda b, i: (0, 1)),
                  _row_spec(ts, D), _row_spec(ts, D)],
        out_specs=_row_spec(ts, D), out_shape=jax.ShapeDtypeStruct((bsz, s, D), BF16),
        compiler_params=_params(2),
    )(proj, proj, b_merge, b_merge, y_a, y_b)


def _merge_bwd(dmi, proj, b_merge, y_a, y_b, *, name, ts=512):
    bsz, s, _ = proj.shape
    ts = min(ts, s)

    def body(d_ref, la_ref, lb_ref, ba_ref, bb_ref, ya_ref, yb_ref, dya_ref, dyb_ref, dla_ref, dlb_ref, dba_ref, dbb_ref):
        @pl.when((pl.program_id(0) == 0) & (pl.program_id(1) == 0))
        def _():
            dba_ref[...] = jnp.zeros_like(dba_ref)
            dbb_ref[...] = jnp.zeros_like(dbb_ref)

        dv = d_ref[...]
        ga = _sigmoid(la_ref[...] + ba_ref[...])
        gb = _sigmoid(lb_ref[...] + bb_ref[...])
        dya_ref[...] = (dv * ga).astype(BF16)
        dyb_ref[...] = (dv * gb).astype(BF16)
        dla = (dv * ya_ref[...]) * (ga * (1.0 - ga))
        dlb = (dv * yb_ref[...]) * (gb * (1.0 - gb))
        dla_ref[...] = dla.astype(BF16)
        dlb_ref[...] = dlb.astype(BF16)
        dba_ref[...] += jnp.sum(dla, axis=0, keepdims=True)
        dbb_ref[...] += jnp.sum(dlb, axis=0, keepdims=True)

    act = jax.ShapeDtypeStruct((bsz, s, D), BF16)
    return pl.pallas_call(
        body, name=name, grid=(bsz, s // ts),
        in_specs=[_row_spec(ts, D), _row_spec(ts, D, OFF_MA // D), _row_spec(ts, D, OFF_MB // D),
                  pl.BlockSpec((1, D), lambda b, i: (0, 0)), pl.BlockSpec((1, D), lambda b, i: (0, 1)),
                  _row_spec(ts, D), _row_spec(ts, D)],
        out_specs=[_row_spec(ts, D)] * 4 + [_gain_spec(D)] * 2,
        out_shape=[act, act, act, act, jax.ShapeDtypeStruct((1, D), F32), jax.ShapeDtypeStruct((1, D), F32)],
        compiler_params=_params(2),
    )(dmi, proj, proj, b_merge, b_merge, y_a, y_b)


def _tri(lower):
    r = lax.broadcasted_iota(jnp.int32, (CHUNK, CHUNK), 0)
    c = lax.broadcasted_iota(jnp.int32, (CHUNK, CHUNK), 1)
    return jnp.where((c <= r) if lower else (c >= r), 1.0, 0.0).astype(F32)


def _gla_logits(a_ref, wal_ref, bal_ref):
    logits = jnp.dot(a_ref[...].astype(BF16), wal_ref[...].astype(BF16), preferred_element_type=F32) + bal_ref[...]
    la = (jnp.minimum(logits, 0.0) - jnp.log(1.0 + jnp.exp(-jnp.abs(logits)))) * (1.0 / GTAU)
    return logits, la


def _chunk_cumsum(la_n, tri):
    cum = jnp.dot(tri, la_n, preferred_element_type=F32, precision=lax.Precision.HIGHEST)
    return cum, jnp.sum(la_n, axis=0, keepdims=True)


def _gla_specs(s, nc):
    def blk(width, off):
        return pl.BlockSpec((None, s, width), lambda h, b: (b, 0, off // width + h))

    proj_specs = [blk(GDK, OFF_Q), blk(GDK, OFF_K), blk(GDV, OFF_V), blk(GDV, OFF_G),
                  pl.BlockSpec((None, s, LANE), lambda h, b: (b, 0, OFF_A // LANE)),
                  pl.BlockSpec((LANE, GDK), lambda h, b: (0, h)), pl.BlockSpec((1, GDK), lambda h, b: (0, h)),
                  pl.BlockSpec((1, GDV), lambda h, b: (0, 0))]
    st_spec = pl.BlockSpec((None, None, nc, GDV, GDK), lambda h, b: (b, h, 0, 0, 0))
    return blk, proj_specs, st_spec


def _gla_fwd(proj, w_alpha_p, b_alpha, out_norm_g, *, name):
    bsz, s, _ = proj.shape
    nc = s // CHUNK
    scale = GDK ** -0.5

    rb = min(512, s)

    def body(q_ref, k_ref, v_ref, g_ref, a_ref, wal_ref, bal_ref, ong_ref, o_ref, og_ref, st_ref):
        _, la = _gla_logits(a_ref, wal_ref, bal_ref)
        tri = _tri(True)
        st = jnp.zeros((GDV, GDK), F32)
        for n in range(nc):
            rows = pl.ds(n * CHUNK, CHUNK)
            cum, cum_end = _chunk_cumsum(la[n * CHUNK:(n + 1) * CHUNK], tri)
            kd = k_ref[rows, :] * jnp.exp(cum_end - cum)
            ut = lax.dot_general(v_ref[rows, :].astype(BF16), kd.astype(BF16), _TN, preferred_element_type=F32)
            st = st * jnp.exp(cum_end) + ut
            st_ref[n] = st
            o_ref[rows, :] = lax.dot_general((q_ref[rows, :].astype(F32) * scale).astype(BF16), st.astype(BF16), _NT,
                                             preferred_element_type=F32)
        for j in range(0, s, rb):
            blk_rows = pl.ds(j, rb)
            oh, _ = _rms(o_ref[blk_rows, :], None)
            gv = g_ref[blk_rows, :].astype(F32)
            og_ref[blk_rows, :] = ((oh * ong_ref[...]) * (gv * _sigmoid(gv))).astype(BF16)

    blk, proj_specs, st_spec = _gla_specs(s, nc)
    return pl.pallas_call(
        body, name=name, grid=(GH, bsz), in_specs=proj_specs, out_specs=[blk(GDV, 0), blk(GDV, 0), st_spec],
        out_shape=[jax.ShapeDtypeStruct((bsz, s, GH * GDV), F32), jax.ShapeDtypeStruct((bsz, s, GH * GDV), BF16),
                   jax.ShapeDtypeStruct((bsz, GH, nc, GDV, GDK), F32)],
        compiler_params=_params(2),
    )(proj, proj, proj, proj, proj, w_alpha_p, b_alpha, out_norm_g)


def _gla_bwd(dog, o, states, proj, w_alpha_p, b_alpha, out_norm_g, *, name):
    bsz, s, _ = proj.shape
    nc = s // CHUNK
    scale = GDK ** -0.5

    def body(dog_ref, o_ref, st_ref, q_ref, k_ref, v_ref, g_ref, a_ref, wal_ref, bal_ref, ong_ref,
             dq_ref, dk_ref, dv_ref, dg_ref, dl_ref, dbal_ref, dong_ref, do_scr, dlog_scr):
        h, b = pl.program_id(0), pl.program_id(1)

        @pl.when(b == 0)
        def _():
            dbal_ref[...] = jnp.zeros_like(dbal_ref)

        @pl.when((b == 0) & (h == 0))
        def _():
            dong_ref[...] = jnp.zeros_like(dong_ref)

        ong = ong_ref[...]
        for j in range(0, s, rb):
            blk_rows = pl.ds(j, rb)
            gv, dogv = g_ref[blk_rows, :].astype(F32), dog_ref[blk_rows, :]
            sg = _sigmoid(gv)
            oh, r = _rms(o_ref[blk_rows, :], None)
            don = dogv * (gv * sg)
            dg_ref[blk_rows, :] = (dogv * (oh * ong) * (sg * (1.0 + gv * (1.0 - sg)))).astype(BF16)
            dong_ref[...] += jnp.sum(don * oh, axis=0, keepdims=True)
            doh = don * ong
            do_scr[blk_rows, :] = (r * (doh - oh * jnp.mean(doh * oh, axis=-1, keepdims=True))).astype(BF16)

        logits, la = _gla_logits(a_ref, wal_ref, bal_ref)
        tri_lo, tri_up = _tri(True), _tri(False)
        carry = jnp.zeros((GDV, GDK), F32)
        for n in range(nc - 1, -1, -1):
            rows = pl.ds(n * CHUNK, CHUNK)
            cum, cum_end = _chunk_cumsum(la[n * CHUNK:(n + 1) * CHUNK], tri_lo)
            decay = jnp.exp(cum_end)
            w = jnp.exp(cum_end - cum)
            kd = k_ref[rows, :] * w
            do_b = do_scr[rows, :]
            qs_b = (q_ref[rows, :].astype(F32) * scale).astype(BF16)
            dq_ref[rows, :] = (jnp.dot(do_b, st_ref[n].astype(BF16), preferred_element_type=F32) * scale).astype(BF16)
            dsn = lax.dot_general(do_b, qs_b, _TN, preferred_element_type=F32) + carry
            carry = dsn * decay
            dsn_b = dsn.astype(BF16)
            dv_ref[rows, :] = lax.dot_general(kd.astype(BF16), dsn_b, _NT, preferred_element_type=F32).astype(BF16)
            dkd = jnp.dot(v_ref[rows, :].astype(BF16), dsn_b, preferred_element_type=F32)
            dk_ref[rows, :] = (dkd * w).astype(BF16)
            e = dkd * kd
            dcum_end = jnp.sum(e, axis=0, keepdims=True)
            if n > 0:
                dcum_end += jnp.sum(dsn * st_ref[n - 1], axis=0, keepdims=True) * decay
            dlog_scr[rows, :] = dcum_end - jnp.dot(tri_up, e, preferred_element_type=F32,
                                                  precision=lax.Precision.HIGHEST)
        dlog = dlog_scr[...] * (1.0 / GTAU) * (1.0 - _sigmoid(logits))
        dl_ref[...] = dlog.astype(BF16)
        dbal_ref[...] += jnp.sum(dlog, axis=0, keepdims=True)

    rb = min(512, s)

    blk, proj_specs, st_spec = _gla_specs(s, nc)
    act = lambda wd: jax.ShapeDtypeStruct((bsz, s, wd), BF16)
    return pl.pallas_call(
        body, name=name, grid=(GH, bsz), in_specs=[blk(GDV, 0), blk(GDV, 0), st_spec, *proj_specs],
        out_specs=[blk(GDK, 0), blk(GDK, 0), blk(GDV, 0), blk(GDV, 0), blk(GDK, 0),
                   pl.BlockSpec((1, GDK), lambda h, b: (0, h)), pl.BlockSpec((1, GDV), lambda h, b: (0, 0))],
        out_shape=[act(GH * GDK), act(GH * GDK), act(GH * GDV), act(GH * GDV), act(GH * GDK),
                   jax.ShapeDtypeStruct((1, GH * GDK), F32), jax.ShapeDtypeStruct((1, GDV), F32)],
        scratch_shapes=[pltpu.VMEM((s, GDV), BF16), pltpu.VMEM((s, GDK), F32)], compiler_params=_params(2),
    )(dog, o, states, proj, proj, proj, proj, proj, w_alpha_p, b_alpha, out_norm_g)


def _lane():
    return lax.broadcasted_iota(jnp.int32, (1, LANE), 1)


def _swap_halves(x):
    lane = _lane()
    half = MROPE // 2
    lo = (lane >= MNOPE) & (lane < MNOPE + half)
    hi = (lane >= MNOPE + half) & (lane < MQK)
    return jnp.where(lo, pltpu.roll(x, LANE - half, 1), jnp.where(hi, pltpu.roll(x, half, 1), 0.0))


def _norm96(x, g):
    r = lax.rsqrt(jnp.sum(x * x, axis=-1, keepdims=True) * (1.0 / MQK) + EPS)
    return x * r, r


def _lat_norm(proj, q_lat_g, kv_lat_g, *, name, ts=512):
    t = proj.shape[0]
    ts = min(ts, t)

    def body(cq_ref, ckv_ref, gq_ref, gk_ref, oq_ref, ok_ref):
        xq, _ = _rms(cq_ref[...].astype(F32), None)
        oq_ref[...] = (xq * gq_ref[...]).astype(BF16)
        xk, _ = _rms(ckv_ref[...].astype(F32), None)
        ok_ref[...] = (xk * gk_ref[...]).astype(BF16)

    return pl.pallas_call(
        body, name=name, grid=(t // ts,),
        in_specs=[pl.BlockSpec((ts, MQR), lambda i: (i, OFF_CQ // MQR)), pl.BlockSpec((ts, MKVR), lambda i: (i, OFF_CKV // MKVR)),
                  pl.BlockSpec((1, MQR), lambda i: (0, 0)), pl.BlockSpec((1, MKVR), lambda i: (0, 0))],
        out_specs=[pl.BlockSpec((ts, MQR), lambda i: (i, 0)), pl.BlockSpec((ts, MKVR), lambda i: (i, 0))],
        out_shape=[jax.ShapeDtypeStruct((t, MQR), BF16), jax.ShapeDtypeStruct((t, MKVR), BF16)],
        compiler_params=_params(1),
    )(proj, proj, q_lat_g, kv_lat_g)


def _lat_norm_bwd(dcqn, dckvn, proj, q_lat_g, kv_lat_g, *, name, ts=512):
    t = proj.shape[0]
    ts = min(ts, t)

    def one(d_ref, x_ref, g_ref, dx_ref, dg_ref):
        xh, r = _rms(x_ref[...].astype(F32), None)
        dn = d_ref[...]
        dg_ref[...] += jnp.sum(dn * xh, axis=0, keepdims=True)
        dxh = dn * g_ref[...]
        dx_ref[...] = (r * (dxh - xh * jnp.mean(dxh * xh, axis=-1, keepdims=True))).astype(BF16)

    def body(dq_ref, dk_ref, cq_ref, ckv_ref, gq_ref, gk_ref, dxq_ref, dxk_ref, dgq_ref, dgk_ref):
        @pl.when(pl.program_id(0) == 0)
        def _():
            dgq_ref[...] = jnp.zeros_like(dgq_ref)
            dgk_ref[...] = jnp.zeros_like(dgk_ref)

        one(dq_ref, cq_ref, gq_ref, dxq_ref, dgq_ref)
        one(dk_ref, ckv_ref, gk_ref, dxk_ref, dgk_ref)

    return pl.pallas_call(
        body, name=name, grid=(t // ts,),
        in_specs=[pl.BlockSpec((ts, MQR), lambda i: (i, 0)), pl.BlockSpec((ts, MKVR), lambda i: (i, 0)),
                  pl.BlockSpec((ts, MQR), lambda i: (i, OFF_CQ // MQR)), pl.BlockSpec((ts, MKVR), lambda i: (i, OFF_CKV // MKVR)),
                  pl.BlockSpec((1, MQR), lambda i: (0, 0)), pl.BlockSpec((1, MKVR), lambda i: (0, 0))],
        out_specs=[pl.BlockSpec((ts, MQR), lambda i: (i, 0)), pl.BlockSpec((ts, MKVR), lambda i: (i, 0)),
                   pl.BlockSpec((1, MQR), lambda i: (0, 0)), pl.BlockSpec((1, MKVR), lambda i: (0, 0))],
        out_shape=[jax.ShapeDtypeStruct((t, MQR), BF16), jax.ShapeDtypeStruct((t, MKVR), BF16),
                   jax.ShapeDtypeStruct((1, MQR), F32), jax.ShapeDtypeStruct((1, MKVR), F32)],
        compiler_params=_params(1),
    )(dcqn, dckvn, proj, proj, q_lat_g, kv_lat_g)


def _qk_prep(q_raw, kv, proj, cos_t, sin_t, gq, gk, *, name, ts=2048):
    t = q_raw.shape[0]
    ts = min(ts, t)

    def body(q_ref, kv_ref, kpe_ref, c_ref, s_ref, gq_ref, gk_ref, qo_ref, ko_ref, vo_ref):
        cs, sn = c_ref[...], s_ref[...]
        nope = _lane() < MNOPE
        qn, _ = _norm96(q_ref[...], None)
        qn = qn * gq_ref[...]
        qo_ref[...] = (qn * cs + _swap_halves(qn) * sn).astype(BF16)
        kvv = kv_ref[...]
        kn, _ = _norm96(jnp.where(nope, kvv, kpe_ref[...].astype(F32)), None)
        kn = kn * gk_ref[...]
        ko_ref[...] = (kn * cs + _swap_halves(kn) * sn).astype(BF16)
        vo_ref[...] = jnp.where(nope, pltpu.roll(kvv, MNOPE, 1), 0.0).astype(BF16)

    hd = pl.BlockSpec((ts, LANE), lambda i, h: (i, h))
    shared = lambda col: pl.BlockSpec((ts, LANE), lambda i, h: (i, col))
    gain = pl.BlockSpec((1, LANE), lambda i, h: (0, 0))
    out = jax.ShapeDtypeStruct((t, MH * LANE), BF16)
    return pl.pallas_call(
        body, name=name, grid=(t // ts, MH),
        in_specs=[hd, hd, shared(OFF_KPE // LANE), shared(0), shared(0), gain, gain],
        out_specs=[hd, hd, hd], out_shape=[out, out, out], compiler_params=_params(2),
    )(q_raw, kv, proj, cos_t, sin_t, gq, gk)


def _qk_prep_bwd(dq, dk, dv, q_raw, kv, proj, cos_t, sin_t, gq, gk, *, name, ts=2048):
    t = q_raw.shape[0]
    ts = min(ts, t)

    def norm_bwd(dy, x, g, dg_ref):
        xh, r = _norm96(x, None)
        dg_ref[...] += jnp.sum(dy * xh, axis=0, keepdims=True)
        dxh = dy * g
        return r * (dxh - xh * (jnp.sum(dxh * xh, axis=-1, keepdims=True) * (1.0 / MQK)))

    def body(dq_ref, dk_ref, dv_ref, q_ref, kv_ref, kpe_ref, c_ref, s_ref, gq_ref, gk_ref,
             dqr_ref, dkv_ref, dkpe_ref, dgq_ref, dgk_ref):
        i, h = pl.program_id(0), pl.program_id(1)

        @pl.when(h == 0)
        def _():
            dkpe_ref[...] = jnp.zeros_like(dkpe_ref)

        @pl.when((h == 0) & (i == 0))
        def _():
            dgq_ref[...] = jnp.zeros_like(dgq_ref)
            dgk_ref[...] = jnp.zeros_like(dgk_ref)

        cs, sn = c_ref[...], s_ref[...]
        lane = _lane()
        nope = lane < MNOPE
        dqv = dq_ref[...]
        dqn = dqv * cs + _swap_halves(dqv * sn)
        dqr_ref[...] = norm_bwd(dqn, q_ref[...], gq_ref[...], dgq_ref).astype(BF16)
        dkv_ = dk_ref[...]
        dkn = dkv_ * cs + _swap_halves(dkv_ * sn)
        kvv = kv_ref[...]
        dkr = norm_bwd(dkn, jnp.where(nope, kvv, kpe_ref[...].astype(F32)), gk_ref[...], dgk_ref)
        dkv_ref[...] = jnp.where(nope, dkr, pltpu.roll(dv_ref[...], MNOPE, 1)).astype(BF16)
        dkpe_ref[...] += jnp.where((lane >= MNOPE) & (lane < MQK), dkr, 0.0)

    hd = pl.BlockSpec((ts, LANE), lambda i, h: (i, h))
    shared = lambda col: pl.BlockSpec((ts, LANE), lambda i, h: (i, col))
    gain = pl.BlockSpec((1, LANE), lambda i, h: (0, 0))
    out = jax.ShapeDtypeStruct((t, MH * LANE), BF16)
    return pl.pallas_call(
        body, name=name, grid=(t // ts, MH),
        in_specs=[hd, hd, hd, hd, hd, shared(OFF_KPE // LANE), shared(0), shared(0), gain, gain],
        out_specs=[hd, hd, shared(0), gain, gain],
        out_shape=[out, out, jax.ShapeDtypeStruct((t, LANE), F32), jax.ShapeDtypeStruct((1, LANE), F32),
                   jax.ShapeDtypeStruct((1, LANE), F32)],
        compiler_params=_params(2),
    )(dq, dk, dv, q_raw, kv, proj, cos_t, sin_t, gq, gk)


_NT = (((1,), (1,)), ((), ()))
_TN = (((0,), (0,)), ((), ()))


SOFTMAX_SCALE = MQK ** -0.5
Q_PRESCALE = SOFTMAX_SCALE * float(np.log2(np.e))


def _attn_weights(q, k_ref, lo, tq):
    row = lax.broadcasted_iota(jnp.int32, (tq, tq), 0) // CHUNK
    col = lax.broadcasted_iota(jnp.int32, (tq, tq), 1) // CHUNK
    sd = lax.dot_general(q, k_ref[pl.ds(lo, tq), :], _NT, preferred_element_type=F32)
    sd = jnp.where(col <= row, sd, -1e30)
    m = jnp.max(sd, axis=-1, keepdims=True)
    if lo:
        so = lax.dot_general(q, k_ref[pl.ds(0, lo), :], _NT, preferred_element_type=F32)
        m = jnp.maximum(m, jnp.max(so, axis=-1, keepdims=True))
        eo = jnp.exp2(so - m)
        ed = jnp.exp2(sd - m)
        return eo, ed, 1.0 / (jnp.sum(eo, axis=-1, keepdims=True) + jnp.sum(ed, axis=-1, keepdims=True))
    ed = jnp.exp2(sd - m)
    return None, ed, 1.0 / jnp.sum(ed, axis=-1, keepdims=True)


def _attn_fwd(q, k, v, *, name, tq=256):
    bsz, s, _ = q.shape
    tq = min(tq, s)

    def body(q_ref, k_ref, v_ref, o_ref):
        for i in range(s // tq):
            lo = i * tq
            eo, ed, inv = _attn_weights(q_ref[pl.ds(lo, tq), :], k_ref, lo, tq)
            o = jnp.dot(ed.astype(BF16), v_ref[pl.ds(lo, tq), :], preferred_element_type=F32)
            if lo:
                o += jnp.dot(eo.astype(BF16), v_ref[pl.ds(0, lo), :], preferred_element_type=F32)
            o_ref[pl.ds(lo, tq), :] = (o * inv).astype(BF16)

    spec = pl.BlockSpec((None, s, LANE), lambda b, h: (b, 0, h))
    return pl.pallas_call(
        body, name=name, grid=(bsz, MH), in_specs=[spec, spec, spec], out_specs=spec,
        out_shape=jax.ShapeDtypeStruct((bsz, s, MH * LANE), BF16), compiler_params=_params(2),
    )(q, k, v)


def _attn_bwd(q, k, v, do, *, name, tq=256):
    bsz, s, _ = q.shape
    tq = min(tq, s)

    def body(q_ref, k_ref, v_ref, do_ref, dq_ref, dk_ref, dv_ref):
        dk_ref[...] = jnp.zeros_like(dk_ref)
        dv_ref[...] = jnp.zeros_like(dv_ref)
        for i in range(s // tq):
            lo = i * tq
            here, before = pl.ds(lo, tq), pl.ds(0, lo)
            qv, dov = q_ref[here, :], do_ref[here, :]
            eo, ed, inv = _attn_weights(qv, k_ref, lo, tq)
            do_n = (dov.astype(F32) * inv).astype(BF16)
            dv_ref[here, :] += lax.dot_general(ed.astype(BF16), do_n, _TN, preferred_element_type=F32)
            dpd = lax.dot_general(dov, v_ref[here, :], _NT, preferred_element_type=F32)
            delta = jnp.sum(dpd * ed, axis=-1, keepdims=True)
            if lo:
                dv_ref[before, :] += lax.dot_general(eo.astype(BF16), do_n, _TN, preferred_element_type=F32)
                dpo = lax.dot_general(dov, v_ref[before, :], _NT, preferred_element_type=F32)
                delta += jnp.sum(dpo * eo, axis=-1, keepdims=True)
            delta = delta * inv
            r = inv * SOFTMAX_SCALE
            dsd = (ed * (dpd - delta) * r).astype(BF16)
            dq = jnp.dot(dsd, k_ref[here, :], preferred_element_type=F32)
            dk_ref[here, :] += lax.dot_general(dsd, qv, _TN, preferred_element_type=F32)
            if lo:
                dso = (eo * (dpo - delta) * r).astype(BF16)
                dq += jnp.dot(dso, k_ref[before, :], preferred_element_type=F32)
                dk_ref[before, :] += lax.dot_general(dso, qv, _TN, preferred_element_type=F32)
            dq_ref[here, :] = dq
        dk_ref[...] = dk_ref[...] * (1.0 / Q_PRESCALE)

    spec = pl.BlockSpec((None, s, LANE), lambda b, h: (b, 0, h))
    out = jax.ShapeDtypeStruct((bsz, s, MH * LANE), F32)
    return pl.pallas_call(
        body, name=name, grid=(bsz, MH), in_specs=[spec] * 4, out_specs=[spec] * 3, out_shape=[out, out, out],
        compiler_params=_params(2),
    )(q, k, v, do)


def _adamw(w, g, m, v, *, name, tr=256, by_cols=False):
    rows, cols = w.shape
    tr = _tile_rows(rows, tr)

    def body(w_ref, g_ref, m_ref, v_ref, d_ref, nm_ref, nv_ref):
        d_ref[...], nm_ref[...], nv_ref[...] = _adamw_update(w_ref[...], g_ref[...], m_ref[...], v_ref[...])

    spec = pl.BlockSpec((rows, LANE), lambda i: (0, i)) if by_cols else pl.BlockSpec((tr, cols), lambda i: (i, 0))
    out = jax.ShapeDtypeStruct((rows, cols), F32)
    return pl.pallas_call(body, name=name, grid=(cols // LANE if by_cols else rows // tr,), in_specs=[spec] * 4,
                          out_specs=[spec] * 3, out_shape=[out, out, out], compiler_params=_params(1))(w, g, m, v)


def _tile_rows(rows, target):
    if rows <= target:
        return rows
    best = 8
    for t in range(8, target + 1, 8):
        if rows % t == 0:
            best = t
    return best


def _adamw_update(w, g, m, v):
    nm = ADAM_B1 * m + (1.0 - ADAM_B1) * g
    nv = ADAM_B2 * v + (1.0 - ADAM_B2) * (g * g)
    m_hat = nm / (1.0 - ADAM_B1 ** ADAM_STEP)
    v_hat = nv / (1.0 - ADAM_B2 ** ADAM_STEP)
    return -ADAM_LR * (m_hat / (jnp.sqrt(v_hat) + ADAM_EPS) + ADAM_WD * w), nm, nv


def _adamw_halves(w, m, v, mine, theirs, sel, *, name, tr=256):
    rows, cols = w.shape
    tr = _tile_rows(rows // 2, tr)
    nh = rows // 2 // tr

    def body(sel_ref, w_ref, m_ref, v_ref, mine_ref, theirs_ref, g_ref, d_ref, nm_ref, nv_ref):
        lower = pl.program_id(0) < nh
        south = sel_ref[0] == 0
        gv = jnp.where(lower == south, mine_ref[...], theirs_ref[...])
        g_ref[...] = gv
        d_ref[...], nm_ref[...], nv_ref[...] = _adamw_update(w_ref[...], gv, m_ref[...], v_ref[...])

    full = pl.BlockSpec((tr, cols), lambda i, sel_ref: (i, 0))
    half = pl.BlockSpec((tr, cols), lambda i, sel_ref: (i % nh, 0))
    out = jax.ShapeDtypeStruct((rows, cols), F32)
    return pl.pallas_call(
        body, name=name, out_shape=[out] * 4, compiler_params=_params(1),
        grid_spec=pltpu.PrefetchScalarGridSpec(num_scalar_prefetch=1, grid=(rows // tr,),
                                               in_specs=[full, full, full, half, half], out_specs=[full] * 4),
    )(sel, w, m, v, mine, theirs)


def _pair_add(x, sib, sel, *, name, tr=256):
    n, _, rows, cols = x.shape
    tr = _tile_rows(rows, tr)

    def body(sel_ref, x_ref, s_ref, o_ref):
        o_ref[...] = (x_ref[...] + s_ref[...]).astype(BF16)

    spec = pl.BlockSpec((None, tr, cols), lambda j, i, sel_ref: (j, i, 0))
    return pl.pallas_call(
        body, name=name, out_shape=jax.ShapeDtypeStruct((n, rows, cols), BF16), compiler_params=_params(2),
        grid_spec=pltpu.PrefetchScalarGridSpec(
            num_scalar_prefetch=1, grid=(n, rows // tr),
            in_specs=[pl.BlockSpec((None, None, tr, cols), lambda j, i, sel_ref: (j, sel_ref[0], i, 0)), spec],
            out_specs=spec),
    )(sel, x, sib)


def _chip_sum(pair, recv, sel, *, name, tr=256):
    _, rows, cols = pair.shape
    tr = _tile_rows(rows, tr)

    def body(sel_ref, p_ref, r_ref, o_ref):
        acc = p_ref[...].astype(F32)
        for k in range(3):
            acc = acc + r_ref[k].astype(F32)
        o_ref[...] = acc

    return pl.pallas_call(
        body, name=name, out_shape=jax.ShapeDtypeStruct((rows, cols), F32), compiler_params=_params(1),
        grid_spec=pltpu.PrefetchScalarGridSpec(
            num_scalar_prefetch=1, grid=(rows // tr,),
            in_specs=[pl.BlockSpec((None, tr, cols), lambda i, sel_ref: (sel_ref[0], i, 0)),
                      pl.BlockSpec((3, tr, cols), lambda i, sel_ref: (0, i, 0))],
            out_specs=pl.BlockSpec((tr, cols), lambda i, sel_ref: (i, 0))),
    )(sel, pair, recv)


def _me():
    return lax.axis_index("x"), lax.axis_index("y"), lax.axis_index("c")


def _flip(pos, bits):
    x, y, c = pos
    return (x ^ bits[0] if bits[0] else x, y ^ bits[1] if bits[1] else y, c ^ bits[2] if bits[2] else c)


ANY = pl.BlockSpec(memory_space=pl.ANY)


def _all_gather8(xs, *, name):
    n = len(xs)
    flips = [((k >> 2) & 1, (k >> 1) & 1, k & 1) for k in range(1, 8)]

    def body(*refs):
        x_refs, out_refs, (send_sems, recv_sems, local_sems) = refs[:n], refs[n:2 * n], refs[2 * n:]
        me = _me()
        slot = lambda p: 4 * p[0] + 2 * p[1] + p[2]
        copies = []
        for i in range(n):
            mine = pltpu.make_async_copy(x_refs[i], out_refs[i].at[slot(me)], local_sems.at[i])
            mine.start()
            copies.append(mine)
            for k, f in enumerate(flips):
                peer = _flip(me, f)
                sems = dict(send_sem=send_sems.at[7 * i + k], recv_sem=recv_sems.at[7 * i + k], device_id=peer,
                            device_id_type=MESH)
                cp = pltpu.make_async_remote_copy(src_ref=x_refs[i], dst_ref=out_refs[i].at[slot(me)], **sems)
                cp.start()
                copies.append(cp)
                copies.append(pltpu.make_async_remote_copy(src_ref=x_refs[i], dst_ref=out_refs[i].at[slot(peer)], **sems))
        for i in range(n):
            base = i * 15
            copies[base].wait()
            for k in range(7):
                copies[base + 1 + 2 * k].wait_send()
                copies[base + 2 + 2 * k].wait_recv()

    outs = pl.pallas_call(
        body, name=name, in_specs=[ANY] * n, out_specs=[ANY] * n,
        out_shape=[jax.ShapeDtypeStruct((8, *x.shape), x.dtype) for x in xs],
        scratch_shapes=[pltpu.SemaphoreType.DMA((7 * n,)), pltpu.SemaphoreType.DMA((7 * n,)),
                        pltpu.SemaphoreType.DMA((n,))])(*xs)
    return list(outs)


CHIP_FLIPS = [(1, 0, 0), (0, 1, 0), (1, 1, 0)]


def _chip():
    return 2 * lax.axis_index("x") + lax.axis_index("y")


HBM = pl.BlockSpec(memory_space=pltpu.HBM)
SEM = pl.BlockSpec(memory_space=pltpu.SEMAPHORE)
EFFECT = pltpu.SideEffectType.DATAFLOW_SIDE_EFFECTING


def _plan_copies(plan, refs, send_sems, recv_sems):
    return [pltpu.make_async_remote_copy(src_ref=src, dst_ref=dst, send_sem=send_sems.at[k], recv_sem=recv_sems.at[k],
                                         device_id=to, device_id_type=MESH) for k, (src, dst, to) in enumerate(plan(refs))]


def _rdma_start(arrays, n_copies, plan, deps, *, name):
    n, nd = len(arrays), len(deps)

    def body(*refs):
        for cp in _plan_copies(plan, refs[:n], refs[n + nd], refs[n + nd + 1]):
            cp.start()
        refs[-1][...] = jnp.zeros_like(refs[-1])

    outs = pl.pallas_call(
        body, name=name,
        out_shape=(pltpu.SemaphoreType.DMA((n_copies,)), pltpu.SemaphoreType.DMA((n_copies,)),
                   *[pltpu.HBM(a.shape, a.dtype) for a in arrays], jax.ShapeDtypeStruct((8, LANE), F32)),
        in_specs=[HBM] * n + [ANY] * nd, out_specs=(SEM, SEM, *[HBM] * n, pl.BlockSpec(memory_space=pltpu.VMEM)),
        input_output_aliases={i: i + 2 for i in range(n)}, compiler_params=pltpu.CompilerParams(has_side_effects=EFFECT),
    )(*[pltpu.with_memory_space_constraint(a, pltpu.HBM) for a in arrays], *deps)
    return outs[0], outs[1], list(outs[2:2 + n]), outs[-1]


def _rdma_wait(send_sems, recv_sems, arrays, plan, after, *, name):
    n = len(arrays)

    def body(*refs):
        for cp in _plan_copies(plan, refs[:n], refs[n], refs[n + 1]):
            cp.wait_send()
            cp.wait_recv()

    return list(pl.pallas_call(
        body, name=name, out_shape=tuple(pltpu.HBM(a.shape, a.dtype) for a in arrays),
        in_specs=[HBM] * n + [SEM, SEM, ANY], out_specs=tuple([HBM] * n), input_output_aliases={i: i for i in range(n)},
        compiler_params=pltpu.CompilerParams(has_side_effects=EFFECT),
    )(*arrays, send_sems, recv_sems, after))


def _gather_plan(n):
    def plan(refs):
        me = _me()
        slot = 2 * me[0] + me[1]
        return [(refs[i].at[me[2]], refs[n + i].at[slot, me[2]], _flip(me, f)) for i in range(n) for f in CHIP_FLIPS]
    return plan


def _scatter_plan(n):
    def plan(refs):
        me = _me()
        out = []
        for i in range(n):
            for k, f in enumerate(CHIP_FLIPS):
                peer = _flip(me, f)
                out.append((refs[i].at[2 * peer[0] + peer[1]], refs[n + i].at[k], peer))
        return out
    return plan


def _sibling_plan(n, src_of):
    def plan(refs):
        me = _me()
        return [(src_of(refs[i], me[2]), refs[n + i], _flip(me, (0, 0, 1))) for i in range(n)]
    return plan


def _gather8_plan(n):
    def plan(refs):
        me = _me()
        slot = 4 * me[0] + 2 * me[1] + me[2]
        return [(refs[i], refs[n + i].at[slot], _flip(me, ((k >> 2) & 1, (k >> 1) & 1, k & 1)))
                for i in range(n) for k in range(1, 8)]
    return plan


def _pair_fill(lands, *, name):
    n = len(lands)

    def body(*refs):
        in_refs, (send_sems, recv_sems) = refs[:n], refs[2 * n:]
        me = _me()
        sib = _flip(me, (0, 0, 1))
        copies = []
        for i in range(n):
            for k, f in enumerate(CHIP_FLIPS):
                peer = _flip(me, f)
                slot = 2 * peer[0] + peer[1]
                mine, theirs = in_refs[i].at[slot, me[2]], in_refs[i].at[slot, 1 - me[2]]
                cp = pltpu.make_async_remote_copy(src_ref=mine, dst_ref=mine, send_sem=send_sems.at[3 * i + k],
                                                  recv_sem=recv_sems.at[3 * i + k], device_id=sib, device_id_type=MESH)
                cp.start()
                copies.append((cp, pltpu.make_async_remote_copy(
                    src_ref=mine, dst_ref=theirs, send_sem=send_sems.at[3 * i + k], recv_sem=recv_sems.at[3 * i + k],
                    device_id=sib, device_id_type=MESH)))
        for cp, arrival in copies:
            arrival.wait_recv()
            cp.wait_send()

    return list(pl.pallas_call(
        body, name=name, in_specs=[ANY] * n, out_specs=[ANY] * n,
        out_shape=[jax.ShapeDtypeStruct(a.shape, a.dtype) for a in lands], input_output_aliases={i: i for i in range(n)},
        scratch_shapes=[pltpu.SemaphoreType.DMA((3 * n,)), pltpu.SemaphoreType.DMA((3 * n,))])(*lands))


def _own_and_landed(lands, xs):
    chip = _chip()
    return [[jnp.where(chip == j, x, o.reshape(4, *x.shape)[j]) for j in range(4)] for o, x in zip(lands, xs)]


BIG = (("w_in", (D, IN_WIDTH // 4), 1), ("gla_w_o", (D // 4, D), 0), ("mla_w_uq", (MQR, MH * MQK // 4), 1),
       ("mla_w_ukv", (MKVR, MH * (MNOPE + MVD) // 4), 1), ("mla_w_o", (D // 4, D), 0), ("w_out", (D // 4, D), 0),
       ("mlp_w1", (D, DFF // 4), 1), ("mlp_w2", (DFF // 4, D), 0))
ADA_SHARD = (D, 6 * D // 4)
SMALL = (("b_ada", 6 * D), ("norm1_g", D), ("b_merge", 2 * D), ("gla_b_alpha", GH * GDK), ("gla_out_norm_g", GDV),
         ("mla_q_lat_g", MQR), ("mla_kv_lat_g", MKVR), ("mla_qn_g", MQK), ("mla_kn_g", MQK), ("norm2_g", D))


W_IN_SEGMENTS = ((0, 3072, OFF_Q), (3072, 3088, OFF_A), (3088, 3344, OFF_CQ), (3344, 3472, OFF_CKV),
                 (3472, 3504, OFF_KPE + MNOPE), (3504, 5552, OFF_MA))
W_IN_SPLIT = OFF_MA
SMALL_ROWS, SMALL_COLS = 32, 2 * D
W_ALPHA_ROW = 16
LOSS_ROW = 15
SMALL_RED = tuple((n, k) for n, k in SMALL if n != "b_ada")


def _pack_small(grads, d_w_alpha, loss_row, *, name):
    def body(*refs):
        g_refs, wa_ref, loss_ref, out_ref = refs[:-3], refs[-3], refs[-2], refs[-1]
        out_ref[...] = jnp.zeros_like(out_ref)
        for i, ((_, k), g_ref) in enumerate(zip(SMALL_RED, g_refs)):
            out_ref[i:i + 1, 0:k] = g_ref[...]
        out_ref[LOSS_ROW:LOSS_ROW + 1, 0:LANE] = loss_ref[...]
        out_ref[W_ALPHA_ROW:W_ALPHA_ROW + GLR, 0:GH * GDK] = wa_ref[...]

    return pl.pallas_call(body, name=name, out_shape=jax.ShapeDtypeStruct((SMALL_ROWS, SMALL_COLS), F32))(
        *grads, d_w_alpha, loss_row)


def _small_update(gathered, dmod_all, sel, wmv, *, name):
    names = [n for n, _ in SMALL] + ["gla_w_alpha"]
    n_par = len(names)

    def body(sel_ref, g_ref, dmod_ref, *refs):
        in_refs, out_refs, loss_ref, acc = refs[:3 * n_par], refs[3 * n_par:-2], refs[-2], refs[-1]
        total = g_ref[0]
        for j in range(1, 8):
            total = total + g_ref[j]
        acc[...] = total
        loss_ref[...] = acc[LOSS_ROW:LOSS_ROW + 1, 0:LANE]
        row = {n: i for i, (n, _) in enumerate(SMALL_RED)}
        for p, name_p in enumerate(names):
            w_ref, m_ref, v_ref = in_refs[3 * p:3 * p + 3]
            if name_p == "b_ada":
                gv = jnp.sum(dmod_ref[...], axis=0, keepdims=True)
            elif name_p == "gla_w_alpha":
                gv = jnp.zeros((GLR, GDK), F32)
                for j in range(4):
                    blk = acc[W_ALPHA_ROW:W_ALPHA_ROW + GLR, j * GDK:(j + 1) * GDK]
                    gv = gv + jnp.where(sel_ref[0] == j, blk, 0.0)
            else:
                gv = acc[row[name_p]:row[name_p] + 1, 0:w_ref.shape[1]]
            o = out_refs[4 * p:4 * p + 4]
            o[0][...] = gv
            o[1][...], o[2][...], o[3][...] = _adamw_update(w_ref[...], gv, m_ref[...], v_ref[...])

    flat = [a for t in wmv for a in t]
    out_shape = [jax.ShapeDtypeStruct(t[0].shape, F32) for t in wmv for _ in range(4)]
    out_shape.append(jax.ShapeDtypeStruct((1, LANE), F32))
    vmem = pl.BlockSpec(memory_space=pltpu.VMEM)
    outs = pl.pallas_call(
        body, name=name, out_shape=out_shape, in_specs=[pl.BlockSpec(memory_space=pltpu.SMEM), vmem, vmem] + [vmem] * len(flat),
        out_specs=[vmem] * len(out_shape), scratch_shapes=[pltpu.VMEM((SMALL_ROWS, SMALL_COLS), F32)],
    )(sel, gathered, dmod_all, *flat)
    return {n: tuple(outs[4 * p:4 * p + 4]) for p, n in enumerate(names)}, outs[-1][0, 0]


def _full_weights(gathered):
    w = {name: jnp.concatenate(gathered[name], axis=axis) for name, _, axis in BIG if name in gathered and name != "w_in"}
    if "w_in" in gathered:
        shards = gathered["w_in"]
        zeros = lambda n: [jnp.zeros((D, n), shards[0].dtype)]

        def cols(a, b):
            width = IN_WIDTH // 4
            return [shards[j][:, max(a, j * width) - j * width:min(b, (j + 1) * width) - j * width]
                    for j in range(4) if max(a, j * width) < min(b, (j + 1) * width)]

        parts = []
        for a, b, at in sorted(W_IN_SEGMENTS, key=lambda seg: seg[2]):
            have = sum(p.shape[1] for p in parts)
            parts += (zeros(at - have) if at > have else []) + cols(a, b)
        w["w_in"] = jnp.concatenate(parts + zeros(PW - sum(p.shape[1] for p in parts)), axis=1)
    if "mla_w_uq" in w:
        w["mla_w_uq"] = jnp.pad(w["mla_w_uq"].reshape(MQR, MH, MQK), ((0, 0), (0, 0), (0, LANE - MQK))).reshape(MQR, MH * LANE)
    if "mla_w_o" in w:
        w["mla_w_o"] = jnp.pad(w["mla_w_o"].reshape(MH, MVD, D), ((0, 0), (0, LANE - MVD), (0, 0))).reshape(MH * LANE, D)
    return w


def _grad_slots(g):
    g = dict(g)
    out = {}
    if "w_in" in g:
        g_lo, g_hi = g.pop("w_in")
        take = lambda at, lo, hi: g_lo[:, at + lo:at + hi] if at < W_IN_SPLIT else g_hi[:, at - W_IN_SPLIT + lo:at - W_IN_SPLIT + hi]
        width = IN_WIDTH // 4
        slots = []
        for j in range(4):
            lo, hi = j * width, (j + 1) * width
            slots.append(jnp.concatenate([take(at, max(lo, a) - a, min(hi, b) - a)
                                          for a, b, at in W_IN_SEGMENTS if max(lo, a) < min(hi, b)], axis=1))
        out["w_in"] = jnp.stack(slots).reshape(4, 2, D // 2, width)
    if "mla_w_uq" in g:
        g["mla_w_uq"] = g["mla_w_uq"].reshape(MQR, MH, LANE)[:, :, :MQK].reshape(MQR, MH * MQK)
    if "mla_w_o" in g:
        g["mla_w_o"] = g["mla_w_o"].reshape(MH, LANE, D)[:, :MVD].reshape(MH * MVD, D)
    for name, (rows, cols), axis in BIG:
        if name not in g:
            continue
        a = g[name]
        a = a.reshape(4, rows, cols) if axis == 0 else jnp.transpose(a.reshape(rows, 4, cols), (1, 0, 2))
        out[name] = a.reshape(4, 2, rows // 2, cols)
    return out


def _rope_tables(positions):
    freqs = ROPE_THETA ** (-jnp.arange(0, MROPE, 2, dtype=F32) / MROPE)
    lane = np.arange(LANE)
    in_rope = (lane >= MNOPE) & (lane < MQK)
    freq_lane = jnp.where(in_rope, freqs[(lane - MNOPE) % (MROPE // 2)], 0.0)
    sign = np.where(in_rope, np.where(lane < MNOPE + MROPE // 2, -1.0, 1.0), 0.0).astype(np.float32)
    ang = positions.astype(F32).reshape(-1, 1) * freq_lane[None, :]
    return jnp.cos(ang), jnp.sin(ang) * sign[None, :]


def _local_step(x, positions, mod, target, w, small, more_weights=None, on_grads=None):
    kept = {}
    if on_grads is None:
        on_grads = lambda tag, grads, after: kept.update(grads)
    bsz, s, _ = x.shape
    t = bsz * s
    tt = _tile(t, 1024)
    shift1, scale1, gate1, shift2, scale2, gate2 = [mod[:, None, i * D:(i + 1) * D] for i in range(6)]
    cos_t, sin_t = _rope_tables(positions)
    w_alpha_p = jnp.pad(small["gla_w_alpha"], ((0, LANE - GLR), (0, 0)))
    gq = jnp.pad(small["mla_qn_g"], ((0, 0), (0, LANE - MQK)))
    gk = jnp.pad(small["mla_kn_g"], ((0, 0), (0, LANE - MQK)))
    flat2 = lambda a: a.reshape(t, a.shape[-1])
    bsd = lambda a: a.reshape(bsz, s, a.shape[-1])

    h = _norm_mod(x, small["norm1_g"], scale1, shift1, name="norm1")
    if callable(w):
        w = w(h)
    proj = _mm(flat2(h), w["w_in"], name="proj", tn=1152, out_dtype=BF16)
    proj3 = bsd(proj)
    o, o_gated, states = _gla_fwd(proj3, w_alpha_p, small["gla_b_alpha"], small["gla_out_norm_g"], name="gla_fwd")
    if more_weights is not None:
        w = {**w, **more_weights(o_gated)}
    y_a = _mm(flat2(o_gated), w["gla_w_o"], name="gla_out")
    cq_n, ckv_n = _lat_norm(proj, small["mla_q_lat_g"], small["mla_kv_lat_g"], name="lat_norm")
    q_raw = _mm(cq_n, w["mla_w_uq"], name="mla_uq")
    kv = _mm(ckv_n, w["mla_w_ukv"], name="mla_ukv")
    qf, kf, vf = _qk_prep(q_raw, kv, proj, cos_t, sin_t, gq * Q_PRESCALE, gk, name="qk_prep")
    o_attn = _attn_fwd(bsd(qf), bsd(kf), bsd(vf), name="attn_fwd")
    y_b = _mm(flat2(o_attn), w["mla_w_o"], name="mla_out")
    mixed_in = _merge_fwd(proj3, small["b_merge"], bsd(y_a), bsd(y_b), name="merge_fwd")
    mixed = _mm(flat2(mixed_in), w["w_out"], name="w_out")
    x1, h2 = _resid_norm_mod(x, bsd(mixed), gate1, small["norm2_g"], scale2, shift2, name="norm2")

    def sqrelu(acc, ex, outs):
        r = jnp.maximum(acc, 0.0)
        outs[0][...] = (r * r).astype(BF16)

    r = _mm(flat2(h2), w["mlp_w1"], name="mlp1", epilogue=sqrelu, out_shape=jax.ShapeDtypeStruct((t, DFF), BF16),
            out_specs=_tile_spec(tt, 1024))
    ff = _mm(r, w["mlp_w2"], name="mlp2")
    dy, dff, dgate2, loss_part = _loss_head(x1, bsd(ff), gate2, target, name="loss_head")

    g = {}

    def relu2_bwd(acc, ex, outs):
        outs[0][...] = (acc * (2.0 * jnp.sqrt(ex[0][...].astype(F32)))).astype(BF16)

    dff2 = flat2(dff)
    da1 = _mm(dff2, w["mlp_w2"], tb=True, name="mlp2_dx", epilogue=relu2_bwd, extras=(r,),
              extra_specs=(_tile_spec(tt, 1024),), out_shape=jax.ShapeDtypeStruct((t, DFF), BF16),
              out_specs=_tile_spec(tt, 1024))
    g["mlp_w2"] = _mm(r, dff2, ta=True, name="mlp2_dw")
    dh2 = _mm(da1, w["mlp_w1"], tb=True, name="mlp1_dx")
    g["mlp_w1"] = _mm(flat2(h2), da1, ta=True, name="mlp1_dw")
    token = on_grads("mlp", {n: g.pop(n) for n in ("mlp_w2", "mlp_w1")}, dh2)
    if token is not None:
        gate1 = gate1 + token[0, 0]
    dx1, dscale2, dshift2, dg2, dgate1, dmixed = _norm_mod_bwd(
        bsd(dh2), x1, dy, small["norm2_g"], scale2, gate1, bsd(mixed), name="norm2_bwd")
    dmixed2 = flat2(dmixed)
    dmi = _mm(dmixed2, w["w_out"], tb=True, name="w_out_dx")
    g["w_out"] = _mm(flat2(mixed_in), dmixed2, ta=True, name="w_out_dw")
    dy_a, dy_b, dl_a, dl_b, db_a, db_b = _merge_bwd(bsd(dmi), proj3, small["b_merge"], bsd(y_a), bsd(y_b), name="merge_bwd")
    dy_a2, dy_b2 = flat2(dy_a), flat2(dy_b)
    dog = _mm(dy_a2, w["gla_w_o"], tb=True, name="gla_out_dx")
    g["gla_w_o"] = _mm(flat2(o_gated), dy_a2, ta=True, name="gla_out_dw")
    dq_g, dk_g, dv_g, dg_g, dlog, db_alpha, d_ong = _gla_bwd(
        bsd(dog), o, states, proj3, w_alpha_p, small["gla_b_alpha"], small["gla_out_norm_g"], name="gla_bwd")
    dlog2 = flat2(dlog)
    da_p = _mm(dlog2, w_alpha_p, tb=True, out_dtype=BF16, name="alpha_dx")
    d_w_alpha = _mm(proj[:, OFF_A:OFF_A + LANE], dlog2, ta=True, name="alpha_dw")[:GLR]
    do_attn = _mm(dy_b2, w["mla_w_o"], tb=True, out_dtype=BF16, name="mla_out_dx")
    g["mla_w_o"] = _mm(flat2(o_attn), dy_b2, ta=True, name="mla_out_dw")
    dqf, dkf, dvf = _attn_bwd(bsd(qf), bsd(kf), bsd(vf), bsd(do_attn), name="attn_bwd")
    dq_raw, dkv, dkpe, dgq, dgk = _qk_prep_bwd(flat2(dqf), flat2(dkf), flat2(dvf), q_raw, kv, proj, cos_t, sin_t, gq, gk,
                                                name="qk_prep_bwd")
    dcq_n = _mm(dq_raw, w["mla_w_uq"], tb=True, name="mla_uq_dx")
    g["mla_w_uq"] = _mm(cq_n, dq_raw, ta=True, name="mla_uq_dw")
    dckv_n = _mm(dkv, w["mla_w_ukv"], tb=True, name="mla_ukv_dx")
    g["mla_w_ukv"] = _mm(ckv_n, dkv, ta=True, name="mla_ukv_dw")
    token = on_grads("mix", {n: g.pop(n) for n in ("w_out", "gla_w_o", "mla_w_o", "mla_w_uq", "mla_w_ukv")}, dckv_n)
    q_lat_g = small["mla_q_lat_g"] if token is None else small["mla_q_lat_g"] + token[0:1, 0:1]
    dcq, dckv, dg_qlat, dg_kvlat = _lat_norm_bwd(dcq_n, dckv_n, proj, q_lat_g, small["mla_kv_lat_g"],
                                                  name="lat_norm_bwd")
    pieces = [(flat2(dq_g), OFF_Q), (flat2(dk_g), OFF_K), (flat2(dv_g), OFF_V), (flat2(dg_g), OFF_G),
              (flat2(dl_a), OFF_MA), (flat2(dl_b), OFF_MB), (dcq, OFF_CQ), (dckv, OFF_CKV), (da_p, OFF_A), (dkpe, OFF_KPE)]
    hb = flat2(h)
    g_w_in = (_pieces_dw(hb, [p for p, off in pieces if off < W_IN_SPLIT], name="proj_dw_a"),
              _pieces_dw(hb, [p for p, off in pieces if off >= W_IN_SPLIT], name="proj_dw_b"))
    token = on_grads("in", {"w_in": g_w_in}, g_w_in[1])
    after = jnp.zeros((8, LANE), F32) if token is None else token
    dh = _pieces_dx(pieces, w["w_in"], after, name="proj_dx")
    token = on_grads("dx", {}, dh)
    if token is not None:
        scale1 = scale1 + token[0, 0]
    grad_x, dscale1, dshift1, dg1 = _norm_mod_bwd(bsd(dh), x, dx1, small["norm1_g"], scale1, name="norm1_bwd")

    dmod = jnp.concatenate([dshift1, dscale1, dgate1, dshift2, dscale2, dgate2], axis=-1).reshape(bsz, 6 * D)
    gs = {"norm1_g": dg1, "b_merge": jnp.concatenate([db_a, db_b], axis=1), "gla_b_alpha": db_alpha,
          "gla_out_norm_g": d_ong, "mla_q_lat_g": dg_qlat, "mla_kv_lat_g": dg_kvlat, "mla_qn_g": dgq[:, :MQK],
          "mla_kn_g": dgk[:, :MQK], "norm2_g": dg2}
    return loss_part[0, 0], grad_x, dmod, {**kept, **g}, gs, d_w_alpha


def kernel(x, c, positions, w_ada, b_ada, norm1_g, w_in, b_merge, gla_w_alpha, gla_b_alpha, gla_out_norm_g, gla_w_o, mla_q_lat_g, mla_w_uq, mla_kv_lat_g, mla_w_ukv, mla_qn_g, mla_kn_g, mla_w_o, w_out, norm2_g, mlp_w1, mlp_w2, loss_target, m_w_ada, m_b_ada, m_norm1_g, m_w_in, m_b_merge, m_gla_w_alpha, m_gla_b_alpha, m_gla_out_norm_g, m_gla_w_o, m_mla_q_lat_g, m_mla_w_uq, m_mla_kv_lat_g, m_mla_w_ukv, m_mla_qn_g, m_mla_kn_g, m_mla_w_o, m_w_out, m_norm2_g, m_mlp_w1, m_mlp_w2, v_w_ada, v_b_ada, v_norm1_g, v_w_in, v_b_merge, v_gla_w_alpha, v_gla_b_alpha, v_gla_out_norm_g, v_gla_w_o, v_mla_q_lat_g, v_mla_w_uq, v_mla_kv_lat_g, v_mla_w_ukv, v_mla_qn_g, v_mla_kn_g, v_mla_w_o, v_w_out, v_norm2_g, v_mlp_w1, v_mlp_w2):
    args = dict(locals())
    names_big = [n for n, _, _ in BIG]
    names_small = [n for n, _ in SMALL]
    bsz = x.shape[0]
    ax, ay, ac = lax.axis_index("x"), lax.axis_index("y"), lax.axis_index("c")
    chip = 2 * ax + ay
    dev = 2 * chip + ac

    small = {n: args[n] for n in names_small}
    sel_c = jnp.reshape(ac, (1,)).astype(jnp.int32)
    sel_chip = jnp.reshape(chip, (1,)).astype(jnp.int32)
    c_all, w_alpha_all = _all_gather8([c, gla_w_alpha[0]], name="comm_c_alpha")
    small["gla_w_alpha"] = jnp.concatenate([w_alpha_all[2 * j] for j in range(4)], axis=1)
    c_all = c_all.reshape(8 * bsz, D)

    shards = {n: args[n][0].astype(BF16) for n in names_big}
    halves_of = lambda names: [shards[n].reshape(2, shards[n].shape[0] // 2, shards[n].shape[1]) for n in names]

    def gather_start(names, deps, tag):
        xs = halves_of(names)
        lands = [lax.empty((4, *xh.shape), BF16) for xh in xs]
        plan = _gather_plan(len(names))
        return names, plan, _rdma_start(xs + lands, 3 * len(names), plan, deps, name="comm_weights_start_" + tag)

    def gather_finish(started, after, tag):
        names, plan, sems = started
        arrs = _rdma_wait(sems[0], sems[1], sems[2], plan, after, name="comm_weights_wait_" + tag)
        filled = _pair_fill(arrs[len(names):], name="comm_weights_pair_" + tag)
        own = [a.reshape(shards[n].shape) for n, a in zip(names, arrs)]
        return _full_weights(dict(zip(names, _own_and_landed(filled, own))))


    def add_bias(acc, ex, outs):
        outs[0][...] = acc + ex[0][...]

    silu = lambda v: v * _sigmoid(v)
    b_ada_mine = lax.dynamic_slice(b_ada, (0, chip * ADA_SHARD[1]), (1, ADA_SHARD[1]))
    mod_part = _mm(c_all, w_ada[0], name="ada", tn=512, a_fn=silu, epilogue=add_bias, extras=(b_ada_mine,),
                   extra_specs=(pl.BlockSpec((1, 512), lambda i, j, k: (0, j)),),
                   out_shape=jax.ShapeDtypeStruct((8 * bsz, ADA_SHARD[1]), F32), out_specs=_tile_spec(8 * bsz, 512))
    mod_all = _all_gather8([mod_part], name="comm_mod")[0]
    mod_rows = lax.dynamic_slice(mod_all, (0, dev * bsz, 0), (8, bsz, ADA_SHARD[1]))
    mod = jnp.concatenate([mod_rows[2 * j] for j in range(4)], axis=1)
    first = gather_start(["w_in"], (mod,), "in")
    rest = gather_start([n for n in names_big if n != "w_in"], (mod, first[2][3]), "rest")
    mod = mod + rest[2][3][0, 0]
    w_in_after = lambda after: gather_finish(first, after, "in")
    more_weights = lambda after: gather_finish(rest, after, "rest")

    stage = {}

    def begin(tag, names, arrays, lands, n_copies, plan, what):
        stage[tag] = (names, plan, _rdma_start(arrays + lands, n_copies, plan, (), name=f"comm_{what}_start_{tag}"))
        return stage[tag][2][3]

    def landed(tag, after, what):
        names, plan, sems = stage[tag]
        arrs = _rdma_wait(sems[0], sems[1], sems[2], plan, after, name=f"comm_{what}_wait_{tag}")
        return names, arrs[:len(arrs) // 2], arrs[len(arrs) // 2:]

    def swap_start(tag, grads):
        names = list(grads)
        parts = [_grad_slots(grads)[n] for n in names]
        lands = [lax.empty((4, *p.shape[2:]), F32) for p in parts]
        return begin(tag, names, parts, lands, len(names), _sibling_plan(len(names), lambda r, c: r.at[:, 1 - c]), "pair_sum")

    def scatter_start(tag, after):
        names, parts, sib_halves = landed(tag, after, "pair_sum")
        pairs = [_pair_add(p, s, sel_c, name="pair_add_" + n) for n, p, s in zip(names, parts, sib_halves)]
        recvs = [lax.empty((3, *p.shape[1:]), BF16) for p in pairs]
        return begin(tag, names, pairs, recvs, 3 * len(names), _scatter_plan(len(names)), "scatter")

    def join_start(tag, after):
        names, pairs, recvs = landed(tag, after, "scatter")
        halves = [_chip_sum(p, r, sel_chip, name="chip_sum_" + n) for n, p, r in zip(names, pairs, recvs)]
        lands = [lax.empty(h.shape, F32) for h in halves]
        return begin(tag, names, halves, lands, len(names), _sibling_plan(len(names), lambda r, c: r), "pair_join")

    def reduce_step(tag, grads, after):
        if tag == "mlp":
            return swap_start("mlp", grads)
        if tag == "mix":
            return scatter_start("mlp", after) + swap_start("mix", grads)
        if tag == "in":
            return scatter_start("mix", after) + swap_start("in", grads)
        return scatter_start("in", after)

    loss_part, grad_x, dmod, g, gs, d_w_alpha = _local_step(x, positions, mod, loss_target, w_in_after, small,
                                                            more_weights, reduce_step)

    assert not g, list(g)
    gs_packed = _pack_small([gs[n] for n, _ in SMALL_RED], d_w_alpha, jnp.full((1, LANE), loss_part, F32),
                            name="pack_small")
    small_lands = [lax.empty((8, *a.shape), F32) for a in (dmod, gs_packed)]
    begin("small", ["dmod", "small"], [dmod, gs_packed], small_lands, 7 * 2, _gather8_plan(2), "gather8")

    res = {}

    def finish(tag, after):
        names, halves, theirs = landed(tag, after, "pair_join")
        for n, mine, other in zip(names, halves, theirs):
            if n == "w_in":
                south = ac == 0
                g_t = jnp.concatenate([jnp.where(south, mine, other), jnp.where(south, other, mine)], axis=0).T
                outs = _adamw(w_in[0].T, g_t, m_w_in[0].T, v_w_in[0].T, name="adamw_w_in", by_cols=True)
                res[n] = tuple(a.T for a in (g_t, *outs))
            else:
                res[n] = _adamw_halves(args[n][0], args["m_" + n][0], args["v_" + n][0], mine, other, sel_c,
                                       name="adamw_" + n)
        return res[names[-1]][1]

    join_start("mlp", grad_x)
    join_start("mix", grad_x)
    done = finish("mix", finish("mlp", grad_x))

    _, (dmod_own, gs_own), (dmod_all, gs_all) = landed("small", done, "gather8")
    dmod_all = lax.dynamic_update_slice(dmod_all, dmod_own[None], (dev, 0, 0)).reshape(8 * bsz, 6 * D)
    gs_all = lax.dynamic_update_slice(gs_all, gs_own[None], (dev, 0, 0))
    dmod_mine = lax.dynamic_slice(dmod_all, (0, chip * ADA_SHARD[1]), (8 * bsz, ADA_SHARD[1]))
    g_w_ada = _mm(c_all, dmod_mine, ta=True, a_fn=silu, name="ada_dw")
    wmv = [(args[n], args["m_" + n], args["v_" + n]) for n in names_small]
    wmv.append((gla_w_alpha[0], m_gla_w_alpha[0], v_gla_w_alpha[0]))
    res_small, loss_sum = _small_update(gs_all, dmod_all, sel_chip, wmv, name="small_update")
    res.update(res_small)
    loss = loss_sum * (0.5 / D)
    join_start("in", g_w_ada)
    res["w_ada"] = (g_w_ada, *_adamw(w_ada[0], g_w_ada, m_w_ada[0], v_w_ada[0], name="adamw_w_ada"))
    finish("in", res["w_ada"][1])

    order = ["w_ada", "b_ada", "norm1_g", "w_in", "b_merge", "gla_w_alpha", "gla_b_alpha", "gla_out_norm_g", "gla_w_o",
             "mla_q_lat_g", "mla_w_uq", "mla_kv_lat_g", "mla_w_ukv", "mla_qn_g", "mla_kn_g", "mla_w_o", "w_out",
             "norm2_g", "mlp_w1", "mlp_w2"]
    named = lambda k: [res[n][k].reshape(args[n].shape) for n in order]
    return (loss, grad_x, *named(0), *named(1), *named(2), *named(3))
```

```python
import jax
import jax.numpy as jnp
import numpy as np
from jax import lax
from jax.experimental import pallas as pl
from jax.experimental.pallas import tpu as pltpu

F32 = jnp.float32
BF16 = jnp.bfloat16
MESH = pl.DeviceIdType.MESH

D = 1024
CHUNK = 64
EPS = 1e-6
GH, GDK, GDV, GLR, GTAU = 4, 128, 256, 16, 16.0
MH, MQR, MKVR, MNOPE, MROPE, MVD = 16, 256, 128, 64, 32, 64
MQK = MNOPE + MROPE
DFF = 4 * D
ROPE_THETA = 10000.0
IN_WIDTH = 5552
LANE = 128
OFF_Q, OFF_K, OFF_V, OFF_G, OFF_MA, OFF_MB, OFF_CQ, OFF_CKV, OFF_A, OFF_KPE, PW = (
    0, 512, 1024, 2048, 3072, 4096, 5120, 5376, 5504, 5632, 5760)
ADAM_LR, ADAM_B1, ADAM_B2, ADAM_EPS, ADAM_WD, ADAM_STEP = 0.001, 0.9, 0.999, 1e-08, 0.01, 10
VMEM_LIMIT = 48 * 1024 * 1024


def _params(n_axes):
    return pltpu.CompilerParams(dimension_semantics=("arbitrary",) * n_axes, vmem_limit_bytes=VMEM_LIMIT)


def _tile(n, target):
    if n <= target:
        return n
    best = None
    for t in range(LANE, target + 1, LANE):
        if n % t == 0:
            best = t
    assert best is not None, (n, target)
    return best


def _sigmoid(x):
    return 1.0 / (1.0 + jnp.exp(-x))


MM_VMEM_BUDGET = 36 * 1024 * 1024


def _mm(a, b, *, name, ta=False, tb=False, out_dtype=F32, tm=1024, tn=1024, tk=4096,
        epilogue=None, extras=(), extra_specs=(), out_shape=None, out_specs=None, a_fn=None):
    if ta:
        kdim, m = a.shape
    else:
        m, kdim = a.shape
    if tb:
        n, k2 = b.shape
    else:
        k2, n = b.shape
    assert kdim == k2, (a.shape, b.shape)
    tm, tn, tk = _tile(m, tm), _tile(n, tn), _tile(kdim, tk)
    tiles = lambda rows: 2 * (rows * tk * a.dtype.itemsize + tk * tn * b.dtype.itemsize + rows * tn * 4) + rows * tn * 4
    while out_shape is None and tiles(tm) > MM_VMEM_BUDGET and tm % 256 == 0:
        tm //= 2
    nk = kdim // tk
    a_spec = pl.BlockSpec((tk, tm), lambda i, j, k: (k, i)) if ta else pl.BlockSpec((tm, tk), lambda i, j, k: (i, k))
    b_spec = pl.BlockSpec((tn, tk), lambda i, j, k: (j, k)) if tb else pl.BlockSpec((tk, tn), lambda i, j, k: (k, j))
    dims = (((0 if ta else 1,), (1 if tb else 0,)), ((), ()))
    ne = len(extras)
    if out_shape is None:
        out_shape = jax.ShapeDtypeStruct((m, n), out_dtype)
        out_specs = pl.BlockSpec((tm, tn), lambda i, j, k: (i, j))
    n_out = len(out_shape) if isinstance(out_shape, (list, tuple)) else 1
    in_place = epilogue is None and n_out == 1 and out_shape.dtype == F32
    scratch = [] if (nk == 1 or in_place) else [pltpu.VMEM((tm, tn), F32)]

    def body(a_ref, b_ref, *rest):
        ex, outs = rest[:ne], rest[ne:ne + n_out]
        av = a_ref[...] if a_fn is None else a_fn(a_ref[...])
        prod = lax.dot_general(av.astype(BF16), b_ref[...].astype(BF16), dims, preferred_element_type=F32)

        def finish(val):
            if epilogue is None:
                outs[0][...] = val.astype(outs[0].dtype)
            else:
                epilogue(val, ex, outs)

        if nk == 1:
            finish(prod)
            return
        k = pl.program_id(2)
        acc = outs[0] if in_place else rest[-1]

        @pl.when(k == 0)
        def _():
            acc[...] = prod

        @pl.when(k > 0)
        def _():
            acc[...] += prod

        if not in_place:
            @pl.when(k == nk - 1)
            def _():
                finish(acc[...])

    return pl.pallas_call(
        body, name=name, grid=(m // tm, n // tn, nk),
        in_specs=[a_spec, b_spec, *extra_specs], out_specs=out_specs, out_shape=out_shape,
        scratch_shapes=scratch, compiler_params=_params(3),
    )(a, b, *extras)


def _tile_spec(tm, tn):
    return pl.BlockSpec((tm, tn), lambda i, j, k: (i, j))


def _pieces_dx(pieces, w, after, *, name, tm=256):
    t = pieces[0][0].shape[0]
    tm = _tile(t, tm)
    npc = len(pieces)

    def body(*refs):
        p_refs, w_ref, out_ref = refs[:npc], refs[npc], refs[-1]
        acc = None
        for (arr, off), p_ref in zip(pieces, p_refs):
            part = lax.dot_general(p_ref[...].astype(BF16), w_ref[:, off:off + arr.shape[1]], _NT,
                                   preferred_element_type=F32)
            acc = part if acc is None else acc + part
        out_ref[...] = acc

    return pl.pallas_call(
        body, name=name, grid=(t // tm,),
        in_specs=[pl.BlockSpec((tm, arr.shape[1]), lambda i: (i, 0)) for arr, _ in pieces]
        + [pl.BlockSpec(w.shape, lambda i: (0, 0)), pl.BlockSpec((8, LANE), lambda i: (0, 0))],
        out_specs=pl.BlockSpec((tm, w.shape[0]), lambda i: (i, 0)),
        out_shape=jax.ShapeDtypeStruct((t, w.shape[0]), F32), compiler_params=_params(1),
    )(*[arr for arr, _ in pieces], w, after)


def _pieces_dw(h, pieces, *, name, tk=1024):
    t, d = h.shape
    tk = _tile(t, tk)
    widths = [p.shape[1] for p in pieces]
    starts = [sum(widths[:i]) for i in range(len(pieces))]

    def body(h_ref, *refs):
        p_refs, out_ref = refs[:-1], refs[-1]
        first = pl.program_id(0) == 0
        hv = h_ref[...]
        for p_ref, start, width in zip(p_refs, starts, widths):
            part = lax.dot_general(hv, p_ref[...].astype(BF16), _TN, preferred_element_type=F32)
            cols = slice(start, start + width)

            @pl.when(first)
            def _():
                out_ref[:, cols] = part

            @pl.when(jnp.logical_not(first))
            def _():
                out_ref[:, cols] += part

    return pl.pallas_call(
        body, name=name, grid=(t // tk,),
        in_specs=[pl.BlockSpec((tk, d), lambda k: (k, 0))] + [pl.BlockSpec((tk, wd), lambda k: (k, 0)) for wd in widths],
        out_specs=pl.BlockSpec((d, sum(widths)), lambda k: (0, 0)),
        out_shape=jax.ShapeDtypeStruct((d, sum(widths)), F32), compiler_params=_params(1),
    )(h, *pieces)


def _rms(x, g):
    r = lax.rsqrt(jnp.mean(x * x, axis=-1, keepdims=True) + EPS)
    return x * r, r


def _row_spec(ts, width, col=0):
    return pl.BlockSpec((None, ts, width), lambda b, i: (b, i, col))


def _vec_spec(width):
    return pl.BlockSpec((None, 1, width), lambda b, i: (b, 0, 0))


def _gain_spec(width):
    return pl.BlockSpec((1, width), lambda b, i: (0, 0))


def _norm_mod(x, g, scale, shift, *, name, ts=512):
    bsz, s, d = x.shape
    ts = min(ts, s)

    def body(x_ref, g_ref, sc_ref, sh_ref, h_ref):
        xh, _ = _rms(x_ref[...], None)
        h_ref[...] = ((xh * g_ref[...]) * (1.0 + sc_ref[...]) + sh_ref[...]).astype(BF16)

    return pl.pallas_call(
        body, name=name, grid=(bsz, s // ts),
        in_specs=[_row_spec(ts, d), _gain_spec(d), _vec_spec(d), _vec_spec(d)],
        out_specs=_row_spec(ts, d), out_shape=jax.ShapeDtypeStruct((bsz, s, d), BF16),
        compiler_params=_params(2),
    )(x, g, scale, shift)


def _resid_norm_mod(x, mixed, gate, g, scale, shift, *, name, ts=512):
    bsz, s, d = x.shape
    ts = min(ts, s)

    def body(x_ref, mx_ref, gt_ref, g_ref, sc_ref, sh_ref, x1_ref, h_ref):
        x1 = x_ref[...] + gt_ref[...] * mx_ref[...]
        x1_ref[...] = x1
        xh, _ = _rms(x1, None)
        h_ref[...] = ((xh * g_ref[...]) * (1.0 + sc_ref[...]) + sh_ref[...]).astype(BF16)

    return pl.pallas_call(
        body, name=name, grid=(bsz, s // ts),
        in_specs=[_row_spec(ts, d), _row_spec(ts, d), _vec_spec(d), _gain_spec(d), _vec_spec(d), _vec_spec(d)],
        out_specs=[_row_spec(ts, d), _row_spec(ts, d)],
        out_shape=[jax.ShapeDtypeStruct((bsz, s, d), F32), jax.ShapeDtypeStruct((bsz, s, d), BF16)],
        compiler_params=_params(2),
    )(x, mixed, gate, g, scale, shift)


def _norm_mod_bwd(dh, xin, resid, g, scale, gate=None, mixed=None, *, name, ts=512):
    bsz, s, d = xin.shape
    ts = min(ts, s)
    gated = gate is not None

    def body(*refs):
        if gated:
            dh_ref, x_ref, rs_ref, g_ref, sc_ref, gt_ref, mx_ref, dx_ref, dsc_ref, dsh_ref, dg_ref, dgt_ref, dmx_ref = refs
        else:
            dh_ref, x_ref, rs_ref, g_ref, sc_ref, dx_ref, dsc_ref, dsh_ref, dg_ref = refs
        b, i = pl.program_id(0), pl.program_id(1)

        @pl.when(i == 0)
        def _():
            dsc_ref[...] = jnp.zeros_like(dsc_ref)
            dsh_ref[...] = jnp.zeros_like(dsh_ref)
            if gated:
                dgt_ref[...] = jnp.zeros_like(dgt_ref)

        @pl.when((i == 0) & (b == 0))
        def _():
            dg_ref[...] = jnp.zeros_like(dg_ref)

        dh_v, gv = dh_ref[...], g_ref[...]
        xh, r = _rms(x_ref[...], None)
        dsc_ref[...] += jnp.sum(dh_v * (xh * gv), axis=0, keepdims=True)
        dsh_ref[...] += jnp.sum(dh_v, axis=0, keepdims=True)
        dn = dh_v * (1.0 + sc_ref[...])
        dg_ref[...] += jnp.sum(dn * xh, axis=0, keepdims=True)
        dxh = dn * gv
        dx = rs_ref[...] + r * (dxh - xh * jnp.mean(dxh * xh, axis=-1, keepdims=True))
        dx_ref[...] = dx
        if gated:
            dgt_ref[...] += jnp.sum(dx * mx_ref[...], axis=0, keepdims=True)
            dmx_ref[...] = (dx * gt_ref[...]).astype(BF16)

    ins = [dh, xin, resid, g, scale]
    in_specs = [_row_spec(ts, d), _row_spec(ts, d), _row_spec(ts, d), _gain_spec(d), _vec_spec(d)]
    out_specs = [_row_spec(ts, d), _vec_spec(d), _vec_spec(d), _gain_spec(d)]
    out_shape = [jax.ShapeDtypeStruct((bsz, s, d), F32), jax.ShapeDtypeStruct((bsz, 1, d), F32),
                 jax.ShapeDtypeStruct((bsz, 1, d), F32), jax.ShapeDtypeStruct((1, d), F32)]
    if gated:
        ins += [gate, mixed]
        in_specs += [_vec_spec(d), _row_spec(ts, d)]
        out_specs += [_vec_spec(d), _row_spec(ts, d)]
        out_shape += [jax.ShapeDtypeStruct((bsz, 1, d), F32), jax.ShapeDtypeStruct((bsz, s, d), BF16)]
    return pl.pallas_call(
        body, name=name, grid=(bsz, s // ts), in_specs=in_specs, out_specs=out_specs, out_shape=out_shape,
        compiler_params=_params(2),
    )(*ins)


def _loss_head(x1, ff, gate2, target, *, name, ts=512):
    bsz, s, d = x1.shape
    ts = min(ts, s)

    def body(x1_ref, ff_ref, gt_ref, t_ref, dy_ref, dff_ref, dgt_ref, loss_ref, acc):
        b, i = pl.program_id(0), pl.program_id(1)

        @pl.when(i == 0)
        def _():
            dgt_ref[...] = jnp.zeros_like(dgt_ref)

        @pl.when((i == 0) & (b == 0))
        def _():
            acc[...] = jnp.zeros_like(acc)

        ffv, gt = ff_ref[...], gt_ref[...]
        diff = (x1_ref[...] + gt * ffv) - t_ref[...]
        acc[...] += jnp.sum((diff * diff).reshape(ts // 8, 8, d), axis=0)
        dy = diff * (1.0 / d)
        dy_ref[...] = dy
        dgt_ref[...] += jnp.sum(dy * ffv, axis=0, keepdims=True)
        dff_ref[...] = (dy * gt).astype(BF16)

        @pl.when((i == pl.num_programs(1) - 1) & (b == pl.num_programs(0) - 1))
        def _():
            loss_ref[...] = jnp.full(loss_ref.shape, jnp.sum(acc[...]), F32)

    return pl.pallas_call(
        body, name=name, grid=(bsz, s // ts),
        in_specs=[_row_spec(ts, d), _row_spec(ts, d), _vec_spec(d), _row_spec(ts, d)],
        out_specs=[_row_spec(ts, d), _row_spec(ts, d), _vec_spec(d), pl.BlockSpec((8, LANE), lambda b, i: (0, 0))],
        out_shape=[jax.ShapeDtypeStruct((bsz, s, d), F32), jax.ShapeDtypeStruct((bsz, s, d), BF16),
                   jax.ShapeDtypeStruct((bsz, 1, d), F32), jax.ShapeDtypeStruct((8, LANE), F32)],
        scratch_shapes=[pltpu.VMEM((8, d), F32)], compiler_params=_params(2),
    )(x1, ff, gate2, target)


def _merge_fwd(proj, b_merge, y_a, y_b, *, name, ts=512):
    bsz, s, _ = proj.shape
    ts = min(ts, s)

    def body(la_ref, lb_ref, ba_ref, bb_ref, ya_ref, yb_ref, out_ref):
        ga = _sigmoid(la_ref[...] + ba_ref[...])
        gb = _sigmoid(lb_ref[...] + bb_ref[...])
        out_ref[...] = (ga * ya_ref[...] + gb * yb_ref[...]).astype(BF16)

    return pl.pallas_call(
        body, name=name, grid=(bsz, s // ts),
        in_specs=[_row_spec(ts, D, OFF_MA // D), _row_spec(ts, D, OFF_MB // D),
                  pl.BlockSpec((1, D), lambda b, i: (0, 0)), pl.BlockSpec((1, D), lambda b, i: (0, 1)),
                  _row_spec(ts, D), _row_spec(ts, D)],
        out_specs=_row_spec(ts, D), out_shape=jax.ShapeDtypeStruct((bsz, s, D), BF16),
        compiler_params=_params(2),
    )(proj, proj, b_merge, b_merge, y_a, y_b)


def _merge_bwd(dmi, proj, b_merge, y_a, y_b, *, name, ts=512):
    bsz, s, _ = proj.shape
    ts = min(ts, s)

    def body(d_ref, la_ref, lb_ref, ba_ref, bb_ref, ya_ref, yb_ref, dya_ref, dyb_ref, dla_ref, dlb_ref, dba_ref, dbb_ref):
        @pl.when((pl.program_id(0) == 0) & (pl.program_id(1) == 0))
        def _():
            dba_ref[...] = jnp.zeros_like(dba_ref)
            dbb_ref[...] = jnp.zeros_like(dbb_ref)

        dv = d_ref[...].astype(F32)
        ga = _sigmoid(la_ref[...] + ba_ref[...])
        gb = _sigmoid(lb_ref[...] + bb_ref[...])
        dya_ref[...] = (dv * ga).astype(BF16)
        dyb_ref[...] = (dv * gb).astype(BF16)
        dla = (dv * ya_ref[...]) * (ga * (1.0 - ga))
        dlb = (dv * yb_ref[...]) * (gb * (1.0 - gb))
        dla_ref[...] = dla.astype(BF16)
        dlb_ref[...] = dlb.astype(BF16)
        dba_ref[...] += jnp.sum(dla, axis=0, keepdims=True)
        dbb_ref[...] += jnp.sum(dlb, axis=0, keepdims=True)

    act = jax.ShapeDtypeStruct((bsz, s, D), BF16)
    return pl.pallas_call(
        body, name=name, grid=(bsz, s // ts),
        in_specs=[_row_spec(ts, D), _row_spec(ts, D, OFF_MA // D), _row_spec(ts, D, OFF_MB // D),
                  pl.BlockSpec((1, D), lambda b, i: (0, 0)), pl.BlockSpec((1, D), lambda b, i: (0, 1)),
                  _row_spec(ts, D), _row_spec(ts, D)],
        out_specs=[_row_spec(ts, D)] * 4 + [_gain_spec(D)] * 2,
        out_shape=[act, act, act, act, jax.ShapeDtypeStruct((1, D), F32), jax.ShapeDtypeStruct((1, D), F32)],
        compiler_params=_params(2),
    )(dmi, proj, proj, b_merge, b_merge, y_a, y_b)


def _tri(lower):
    r = lax.broadcasted_iota(jnp.int32, (CHUNK, CHUNK), 0)
    c = lax.broadcasted_iota(jnp.int32, (CHUNK, CHUNK), 1)
    return jnp.where((c <= r) if lower else (c >= r), 1.0, 0.0).astype(F32)


def _gla_logits(a_ref, wal_ref, bal_ref):
    logits = jnp.dot(a_ref[...].astype(BF16), wal_ref[...].astype(BF16), preferred_element_type=F32) + bal_ref[...]
    la = (jnp.minimum(logits, 0.0) - jnp.log(1.0 + jnp.exp(-jnp.abs(logits)))) * (1.0 / GTAU)
    return logits, la


def _chunk_cumsum(la_n, tri):
    cum = jnp.dot(tri, la_n, preferred_element_type=F32, precision=lax.Precision.HIGHEST)
    return cum, jnp.sum(la_n, axis=0, keepdims=True)


def _gla_specs(s, nc):
    def blk(width, off):
        return pl.BlockSpec((None, s, width), lambda h, b: (b, 0, off // width + h))

    proj_specs = [blk(GDK, OFF_Q), blk(GDK, OFF_K), blk(GDV, OFF_V), blk(GDV, OFF_G),
                  pl.BlockSpec((None, s, LANE), lambda h, b: (b, 0, OFF_A // LANE)),
                  pl.BlockSpec((LANE, GDK), lambda h, b: (0, h)), pl.BlockSpec((1, GDK), lambda h, b: (0, h)),
                  pl.BlockSpec((1, GDV), lambda h, b: (0, 0))]
    st_spec = pl.BlockSpec((None, None, nc, GDV, GDK), lambda h, b: (b, h, 0, 0, 0))
    return blk, proj_specs, st_spec


def _gla_fwd(proj, w_alpha_p, b_alpha, out_norm_g, *, name):
    bsz, s, _ = proj.shape
    nc = s // CHUNK
    scale = GDK ** -0.5

    rb = min(512, s)

    def body(q_ref, k_ref, v_ref, g_ref, a_ref, wal_ref, bal_ref, ong_ref, o_ref, og_ref, st_ref):
        _, la = _gla_logits(a_ref, wal_ref, bal_ref)
        tri = _tri(True)
        st = jnp.zeros((GDV, GDK), F32)
        for n in range(nc):
            rows = pl.ds(n * CHUNK, CHUNK)
            cum, cum_end = _chunk_cumsum(la[n * CHUNK:(n + 1) * CHUNK], tri)
            kd = k_ref[rows, :] * jnp.exp(cum_end - cum)
            ut = lax.dot_general(v_ref[rows, :].astype(BF16), kd.astype(BF16), _TN, preferred_element_type=F32)
            st = st * jnp.exp(cum_end) + ut
            st_ref[n] = st
            o_ref[rows, :] = lax.dot_general((q_ref[rows, :].astype(F32) * scale).astype(BF16), st.astype(BF16), _NT,
                                             preferred_element_type=F32)
        for j in range(0, s, rb):
            blk_rows = pl.ds(j, rb)
            oh, _ = _rms(o_ref[blk_rows, :], None)
            gv = g_ref[blk_rows, :].astype(F32)
            og_ref[blk_rows, :] = ((oh * ong_ref[...]) * (gv * _sigmoid(gv))).astype(BF16)

    blk, proj_specs, st_spec = _gla_specs(s, nc)
    return pl.pallas_call(
        body, name=name, grid=(GH, bsz), in_specs=proj_specs, out_specs=[blk(GDV, 0), blk(GDV, 0), st_spec],
        out_shape=[jax.ShapeDtypeStruct((bsz, s, GH * GDV), F32), jax.ShapeDtypeStruct((bsz, s, GH * GDV), BF16),
                   jax.ShapeDtypeStruct((bsz, GH, nc, GDV, GDK), F32)],
        compiler_params=_params(2),
    )(proj, proj, proj, proj, proj, w_alpha_p, b_alpha, out_norm_g)


def _gla_bwd(dog, o, states, proj, w_alpha_p, b_alpha, out_norm_g, *, name):
    bsz, s, _ = proj.shape
    nc = s // CHUNK
    scale = GDK ** -0.5

    def body(dog_ref, o_ref, st_ref, q_ref, k_ref, v_ref, g_ref, a_ref, wal_ref, bal_ref, ong_ref,
             dq_ref, dk_ref, dv_ref, dg_ref, dl_ref, dbal_ref, dong_ref, do_scr, dlog_scr):
        h, b = pl.program_id(0), pl.program_id(1)

        @pl.when(b == 0)
        def _():
            dbal_ref[...] = jnp.zeros_like(dbal_ref)

        @pl.when((b == 0) & (h == 0))
        def _():
            dong_ref[...] = jnp.zeros_like(dong_ref)

        ong = ong_ref[...]
        for j in range(0, s, rb):
            blk_rows = pl.ds(j, rb)
            gv, dogv = g_ref[blk_rows, :].astype(F32), dog_ref[blk_rows, :]
            sg = _sigmoid(gv)
            oh, r = _rms(o_ref[blk_rows, :], None)
            don = dogv * (gv * sg)
            dg_ref[blk_rows, :] = (dogv * (oh * ong) * (sg * (1.0 + gv * (1.0 - sg)))).astype(BF16)
            dong_ref[...] += jnp.sum(don * oh, axis=0, keepdims=True)
            doh = don * ong
            do_scr[blk_rows, :] = (r * (doh - oh * jnp.mean(doh * oh, axis=-1, keepdims=True))).astype(BF16)

        logits, la = _gla_logits(a_ref, wal_ref, bal_ref)
        tri_lo, tri_up = _tri(True), _tri(False)
        carry = jnp.zeros((GDV, GDK), F32)
        for n in range(nc - 1, -1, -1):
            rows = pl.ds(n * CHUNK, CHUNK)
            cum, cum_end = _chunk_cumsum(la[n * CHUNK:(n + 1) * CHUNK], tri_lo)
            decay = jnp.exp(cum_end)
            w = jnp.exp(cum_end - cum)
            kd = k_ref[rows, :] * w
            do_b = do_scr[rows, :]
            qs_b = (q_ref[rows, :].astype(F32) * scale).astype(BF16)
            dq_ref[rows, :] = (jnp.dot(do_b, st_ref[n].astype(BF16), preferred_element_type=F32) * scale).astype(BF16)
            dsn = lax.dot_general(do_b, qs_b, _TN, preferred_element_type=F32) + carry
            carry = dsn * decay
            dsn_b = dsn.astype(BF16)
            dv_ref[rows, :] = lax.dot_general(kd.astype(BF16), dsn_b, _NT, preferred_element_type=F32).astype(BF16)
            dkd = jnp.dot(v_ref[rows, :].astype(BF16), dsn_b, preferred_element_type=F32)
            dk_ref[rows, :] = (dkd * w).astype(BF16)
            e = dkd * kd
            dcum_end = jnp.sum(e, axis=0, keepdims=True)
            if n > 0:
                dcum_end += jnp.sum(dsn * st_ref[n - 1], axis=0, keepdims=True) * decay
            dlog_scr[rows, :] = dcum_end - jnp.dot(tri_up, e, preferred_element_type=F32,
                                                  precision=lax.Precision.HIGHEST)
        dlog = dlog_scr[...] * (1.0 / GTAU) * (1.0 - _sigmoid(logits))
        dl_ref[...] = dlog.astype(BF16)
        dbal_ref[...] += jnp.sum(dlog, axis=0, keepdims=True)

    rb = min(512, s)

    blk, proj_specs, st_spec = _gla_specs(s, nc)
    act = lambda wd: jax.ShapeDtypeStruct((bsz, s, wd), BF16)
    return pl.pallas_call(
        body, name=name, grid=(GH, bsz), in_specs=[blk(GDV, 0), blk(GDV, 0), st_spec, *proj_specs],
        out_specs=[blk(GDK, 0), blk(GDK, 0), blk(GDV, 0), blk(GDV, 0), blk(GDK, 0),
                   pl.BlockSpec((1, GDK), lambda h, b: (0, h)), pl.BlockSpec((1, GDV), lambda h, b: (0, 0))],
        out_shape=[act(GH * GDK), act(GH * GDK), act(GH * GDV), act(GH * GDV), act(GH * GDK),
                   jax.ShapeDtypeStruct((1, GH * GDK), F32), jax.ShapeDtypeStruct((1, GDV), F32)],
        scratch_shapes=[pltpu.VMEM((s, GDV), BF16), pltpu.VMEM((s, GDK), F32)], compiler_params=_params(2),
    )(dog, o, states, proj, proj, proj, proj, proj, w_alpha_p, b_alpha, out_norm_g)


def _lane():
    return lax.broadcasted_iota(jnp.int32, (1, LANE), 1)


def _swap_halves(x):
    lane = _lane()
    half = MROPE // 2
    lo = (lane >= MNOPE) & (lane < MNOPE + half)
    hi = (lane >= MNOPE + half) & (lane < MQK)
    return jnp.where(lo, pltpu.roll(x, LANE - half, 1), jnp.where(hi, pltpu.roll(x, half, 1), 0.0))


def _norm96(x, g):
    r = lax.rsqrt(jnp.sum(x * x, axis=-1, keepdims=True) * (1.0 / MQK) + EPS)
    return x * r, r


def _lat_norm(proj, q_lat_g, kv_lat_g, *, name, ts=512):
    t = proj.shape[0]
    ts = min(ts, t)

    def body(cq_ref, ckv_ref, gq_ref, gk_ref, oq_ref, ok_ref):
        xq, _ = _rms(cq_ref[...].astype(F32), None)
        oq_ref[...] = (xq * gq_ref[...]).astype(BF16)
        xk, _ = _rms(ckv_ref[...].astype(F32), None)
        ok_ref[...] = (xk * gk_ref[...]).astype(BF16)

    return pl.pallas_call(
        body, name=name, grid=(t // ts,),
        in_specs=[pl.BlockSpec((ts, MQR), lambda i: (i, OFF_CQ // MQR)), pl.BlockSpec((ts, MKVR), lambda i: (i, OFF_CKV // MKVR)),
                  pl.BlockSpec((1, MQR), lambda i: (0, 0)), pl.BlockSpec((1, MKVR), lambda i: (0, 0))],
        out_specs=[pl.BlockSpec((ts, MQR), lambda i: (i, 0)), pl.BlockSpec((ts, MKVR), lambda i: (i, 0))],
        out_shape=[jax.ShapeDtypeStruct((t, MQR), BF16), jax.ShapeDtypeStruct((t, MKVR), BF16)],
        compiler_params=_params(1),
    )(proj, proj, q_lat_g, kv_lat_g)


def _lat_norm_bwd(dcqn, dckvn, proj, q_lat_g, kv_lat_g, *, name, ts=512):
    t = proj.shape[0]
    ts = min(ts, t)

    def one(d_ref, x_ref, g_ref, dx_ref, dg_ref):
        xh, r = _rms(x_ref[...].astype(F32), None)
        dn = d_ref[...]
        dg_ref[...] += jnp.sum(dn * xh, axis=0, keepdims=True)
        dxh = dn * g_ref[...]
        dx_ref[...] = (r * (dxh - xh * jnp.mean(dxh * xh, axis=-1, keepdims=True))).astype(BF16)

    def body(dq_ref, dk_ref, cq_ref, ckv_ref, gq_ref, gk_ref, dxq_ref, dxk_ref, dgq_ref, dgk_ref):
        @pl.when(pl.program_id(0) == 0)
        def _():
            dgq_ref[...] = jnp.zeros_like(dgq_ref)
            dgk_ref[...] = jnp.zeros_like(dgk_ref)

        one(dq_ref, cq_ref, gq_ref, dxq_ref, dgq_ref)
        one(dk_ref, ckv_ref, gk_ref, dxk_ref, dgk_ref)

    return pl.pallas_call(
        body, name=name, grid=(t // ts,),
        in_specs=[pl.BlockSpec((ts, MQR), lambda i: (i, 0)), pl.BlockSpec((ts, MKVR), lambda i: (i, 0)),
                  pl.BlockSpec((ts, MQR), lambda i: (i, OFF_CQ // MQR)), pl.BlockSpec((ts, MKVR), lambda i: (i, OFF_CKV // MKVR)),
                  pl.BlockSpec((1, MQR), lambda i: (0, 0)), pl.BlockSpec((1, MKVR), lambda i: (0, 0))],
        out_specs=[pl.BlockSpec((ts, MQR), lambda i: (i, 0)), pl.BlockSpec((ts, MKVR), lambda i: (i, 0)),
                   pl.BlockSpec((1, MQR), lambda i: (0, 0)), pl.BlockSpec((1, MKVR), lambda i: (0, 0))],
        out_shape=[jax.ShapeDtypeStruct((t, MQR), BF16), jax.ShapeDtypeStruct((t, MKVR), BF16),
                   jax.ShapeDtypeStruct((1, MQR), F32), jax.ShapeDtypeStruct((1, MKVR), F32)],
        compiler_params=_params(1),
    )(dcqn, dckvn, proj, proj, q_lat_g, kv_lat_g)


def _qk_prep(q_raw, kv, proj, cos_t, sin_t, gq, gk, *, name, ts=2048):
    t = q_raw.shape[0]
    ts = min(ts, t)

    def body(q_ref, kv_ref, kpe_ref, c_ref, s_ref, gq_ref, gk_ref, qo_ref, ko_ref, vo_ref):
        cs, sn = c_ref[...], s_ref[...]
        nope = _lane() < MNOPE
        qn, _ = _norm96(q_ref[...].astype(F32), None)
        qn = qn * gq_ref[...]
        qo_ref[...] = (qn * cs + _swap_halves(qn) * sn).astype(BF16)
        kvv = kv_ref[...].astype(F32)
        kn, _ = _norm96(jnp.where(nope, kvv, kpe_ref[...].astype(F32)), None)
        kn = kn * gk_ref[...]
        ko_ref[...] = (kn * cs + _swap_halves(kn) * sn).astype(BF16)
        vo_ref[...] = jnp.where(nope, pltpu.roll(kvv, MNOPE, 1), 0.0).astype(BF16)

    hd = pl.BlockSpec((ts, LANE), lambda i, h: (i, h))
    shared = lambda col: pl.BlockSpec((ts, LANE), lambda i, h: (i, col))
    gain = pl.BlockSpec((1, LANE), lambda i, h: (0, 0))
    out = jax.ShapeDtypeStruct((t, MH * LANE), BF16)
    return pl.pallas_call(
        body, name=name, grid=(t // ts, MH),
        in_specs=[hd, hd, shared(OFF_KPE // LANE), shared(0), shared(0), gain, gain],
        out_specs=[hd, hd, hd], out_shape=[out, out, out], compiler_params=_params(2),
    )(q_raw, kv, proj, cos_t, sin_t, gq, gk)


def _qk_prep_bwd(dq, dk, dv, q_raw, kv, proj, cos_t, sin_t, gq, gk, *, name, ts=2048):
    t = q_raw.shape[0]
    ts = min(ts, t)

    def norm_bwd(dy, x, g, dg_ref):
        xh, r = _norm96(x, None)
        dg_ref[...] += jnp.sum(dy * xh, axis=0, keepdims=True)
        dxh = dy * g
        return r * (dxh - xh * (jnp.sum(dxh * xh, axis=-1, keepdims=True) * (1.0 / MQK)))

    def body(dq_ref, dk_ref, dv_ref, q_ref, kv_ref, kpe_ref, c_ref, s_ref, gq_ref, gk_ref,
             dqr_ref, dkv_ref, dkpe_ref, dgq_ref, dgk_ref):
        i, h = pl.program_id(0), pl.program_id(1)

        @pl.when(h == 0)
        def _():
            dkpe_ref[...] = jnp.zeros_like(dkpe_ref)

        @pl.when((h == 0) & (i == 0))
        def _():
            dgq_ref[...] = jnp.zeros_like(dgq_ref)
            dgk_ref[...] = jnp.zeros_like(dgk_ref)

        cs, sn = c_ref[...], s_ref[...]
        lane = _lane()
        nope = lane < MNOPE
        dqv = dq_ref[...]
        dqn = dqv * cs + _swap_halves(dqv * sn)
        dqr_ref[...] = norm_bwd(dqn, q_ref[...].astype(F32), gq_ref[...], dgq_ref).astype(BF16)
        dkv_ = dk_ref[...]
        dkn = dkv_ * cs + _swap_halves(dkv_ * sn)
        kvv = kv_ref[...].astype(F32)
        dkr = norm_bwd(dkn, jnp.where(nope, kvv, kpe_ref[...].astype(F32)), gk_ref[...], dgk_ref)
        dkv_ref[...] = jnp.where(nope, dkr, pltpu.roll(dv_ref[...], MNOPE, 1)).astype(BF16)
        dkpe_ref[...] += jnp.where((lane >= MNOPE) & (lane < MQK), dkr, 0.0)

    hd = pl.BlockSpec((ts, LANE), lambda i, h: (i, h))
    shared = lambda col: pl.BlockSpec((ts, LANE), lambda i, h: (i, col))
    gain = pl.BlockSpec((1, LANE), lambda i, h: (0, 0))
    out = jax.ShapeDtypeStruct((t, MH * LANE), BF16)
    return pl.pallas_call(
        body, name=name, grid=(t // ts, MH),
        in_specs=[hd, hd, hd, hd, hd, shared(OFF_KPE // LANE), shared(0), shared(0), gain, gain],
        out_specs=[hd, hd, shared(0), gain, gain],
        out_shape=[out, out, jax.ShapeDtypeStruct((t, LANE), F32), jax.ShapeDtypeStruct((1, LANE), F32),
                   jax.ShapeDtypeStruct((1, LANE), F32)],
        compiler_params=_params(2),
    )(dq, dk, dv, q_raw, kv, proj, cos_t, sin_t, gq, gk)


_NT = (((1,), (1,)), ((), ()))
_TN = (((0,), (0,)), ((), ()))


SOFTMAX_SCALE = MQK ** -0.5
Q_PRESCALE = SOFTMAX_SCALE * float(np.log2(np.e))


def _attn_weights(q, k_ref, lo, tq):
    row = lax.broadcasted_iota(jnp.int32, (tq, tq), 0) // CHUNK
    col = lax.broadcasted_iota(jnp.int32, (tq, tq), 1) // CHUNK
    sd = lax.dot_general(q, k_ref[pl.ds(lo, tq), :], _NT, preferred_element_type=F32)
    sd = jnp.where(col <= row, sd, -1e30)
    m = jnp.max(sd, axis=-1, keepdims=True)
    if lo:
        so = lax.dot_general(q, k_ref[pl.ds(0, lo), :], _NT, preferred_element_type=F32)
        m = jnp.maximum(m, jnp.max(so, axis=-1, keepdims=True))
        eo = jnp.exp2(so - m)
        ed = jnp.exp2(sd - m)
        return eo, ed, 1.0 / (jnp.sum(eo, axis=-1, keepdims=True) + jnp.sum(ed, axis=-1, keepdims=True))
    ed = jnp.exp2(sd - m)
    return None, ed, 1.0 / jnp.sum(ed, axis=-1, keepdims=True)


def _attn_fwd(q, k, v, *, name, tq=256):
    bsz, s, _ = q.shape
    tq = min(tq, s)

    def body(q_ref, k_ref, v_ref, o_ref):
        for i in range(s // tq):
            lo = i * tq
            eo, ed, inv = _attn_weights(q_ref[pl.ds(lo, tq), :], k_ref, lo, tq)
            o = jnp.dot(ed.astype(BF16), v_ref[pl.ds(lo, tq), :], preferred_element_type=F32)
            if lo:
                o += jnp.dot(eo.astype(BF16), v_ref[pl.ds(0, lo), :], preferred_element_type=F32)
            o_ref[pl.ds(lo, tq), :] = (o * inv).astype(BF16)

    spec = pl.BlockSpec((None, s, LANE), lambda b, h: (b, 0, h))
    return pl.pallas_call(
        body, name=name, grid=(bsz, MH), in_specs=[spec, spec, spec], out_specs=spec,
        out_shape=jax.ShapeDtypeStruct((bsz, s, MH * LANE), BF16), compiler_params=_params(2),
    )(q, k, v)


def _attn_bwd(q, k, v, do, *, name, tq=256):
    bsz, s, _ = q.shape
    tq = min(tq, s)

    def body(q_ref, k_ref, v_ref, do_ref, dq_ref, dk_ref, dv_ref):
        dk_ref[...] = jnp.zeros_like(dk_ref)
        dv_ref[...] = jnp.zeros_like(dv_ref)
        for i in range(s // tq):
            lo = i * tq
            here, before = pl.ds(lo, tq), pl.ds(0, lo)
            qv, dov = q_ref[here, :], do_ref[here, :]
            eo, ed, inv = _attn_weights(qv, k_ref, lo, tq)
            do_n = (dov.astype(F32) * inv).astype(BF16)
            dv_ref[here, :] += lax.dot_general(ed.astype(BF16), do_n, _TN, preferred_element_type=F32)
            dpd = lax.dot_general(dov, v_ref[here, :], _NT, preferred_element_type=F32)
            delta = jnp.sum(dpd * ed, axis=-1, keepdims=True)
            if lo:
                dv_ref[before, :] += lax.dot_general(eo.astype(BF16), do_n, _TN, preferred_element_type=F32)
                dpo = lax.dot_general(dov, v_ref[before, :], _NT, preferred_element_type=F32)
                delta += jnp.sum(dpo * eo, axis=-1, keepdims=True)
            delta = delta * inv
            r = inv * SOFTMAX_SCALE
            dsd = (ed * (dpd - delta) * r).astype(BF16)
            dq = jnp.dot(dsd, k_ref[here, :], preferred_element_type=F32)
            dk_ref[here, :] += lax.dot_general(dsd, qv, _TN, preferred_element_type=F32)
            if lo:
                dso = (eo * (dpo - delta) * r).astype(BF16)
                dq += jnp.dot(dso, k_ref[before, :], preferred_element_type=F32)
                dk_ref[before, :] += lax.dot_general(dso, qv, _TN, preferred_element_type=F32)
            dq_ref[here, :] = dq
        dk_ref[...] = dk_ref[...] * (1.0 / Q_PRESCALE)

    spec = pl.BlockSpec((None, s, LANE), lambda b, h: (b, 0, h))
    out = jax.ShapeDtypeStruct((bsz, s, MH * LANE), F32)
    return pl.pallas_call(
        body, name=name, grid=(bsz, MH), in_specs=[spec] * 4, out_specs=[spec] * 3, out_shape=[out, out, out],
        compiler_params=_params(2),
    )(q, k, v, do)


def _adamw(w, g, m, v, *, name, tr=256, by_cols=False):
    rows, cols = w.shape
    tr = _tile_rows(rows, tr)

    def body(w_ref, g_ref, m_ref, v_ref, d_ref, nm_ref, nv_ref):
        d_ref[...], nm_ref[...], nv_ref[...] = _adamw_update(w_ref[...], g_ref[...], m_ref[...], v_ref[...])

    spec = pl.BlockSpec((rows, LANE), lambda i: (0, i)) if by_cols else pl.BlockSpec((tr, cols), lambda i: (i, 0))
    out = jax.ShapeDtypeStruct((rows, cols), F32)
    return pl.pallas_call(body, name=name, grid=(cols // LANE if by_cols else rows // tr,), in_specs=[spec] * 4,
                          out_specs=[spec] * 3, out_shape=[out, out, out], compiler_params=_params(1))(w, g, m, v)


def _tile_rows(rows, target):
    if rows <= target:
        return rows
    best = 8
    for t in range(8, target + 1, 8):
        if rows % t == 0:
            best = t
    return best


def _adamw_update(w, g, m, v):
    nm = ADAM_B1 * m + (1.0 - ADAM_B1) * g
    nv = ADAM_B2 * v + (1.0 - ADAM_B2) * (g * g)
    m_hat = nm / (1.0 - ADAM_B1 ** ADAM_STEP)
    v_hat = nv / (1.0 - ADAM_B2 ** ADAM_STEP)
    return -ADAM_LR * (m_hat / (jnp.sqrt(v_hat) + ADAM_EPS) + ADAM_WD * w), nm, nv


def _adamw_halves(w, m, v, mine, theirs, sel, *, name, tr=256):
    rows, cols = w.shape
    tr = _tile_rows(rows // 2, tr)
    nh = rows // 2 // tr

    def body(sel_ref, w_ref, m_ref, v_ref, mine_ref, theirs_ref, g_ref, d_ref, nm_ref, nv_ref):
        lower = pl.program_id(0) < nh
        south = sel_ref[0] == 0
        gv = jnp.where(lower == south, mine_ref[...], theirs_ref[...])
        g_ref[...] = gv
        d_ref[...], nm_ref[...], nv_ref[...] = _adamw_update(w_ref[...], gv, m_ref[...], v_ref[...])

    full = pl.BlockSpec((tr, cols), lambda i, sel_ref: (i, 0))
    half = pl.BlockSpec((tr, cols), lambda i, sel_ref: (i % nh, 0))
    out = jax.ShapeDtypeStruct((rows, cols), F32)
    return pl.pallas_call(
        body, name=name, out_shape=[out] * 4, compiler_params=_params(1),
        grid_spec=pltpu.PrefetchScalarGridSpec(num_scalar_prefetch=1, grid=(rows // tr,),
                                               in_specs=[full, full, full, half, half], out_specs=[full] * 4),
    )(sel, w, m, v, mine, theirs)


def _pair_add(x, sib, sel, *, name, tr=256):
    n, _, rows, cols = x.shape
    tr = _tile_rows(rows, tr)

    def body(sel_ref, x_ref, s_ref, o_ref):
        o_ref[...] = (x_ref[...] + s_ref[...]).astype(BF16)

    spec = pl.BlockSpec((None, tr, cols), lambda j, i, sel_ref: (j, i, 0))
    return pl.pallas_call(
        body, name=name, out_shape=jax.ShapeDtypeStruct((n, rows, cols), BF16), compiler_params=_params(2),
        grid_spec=pltpu.PrefetchScalarGridSpec(
            num_scalar_prefetch=1, grid=(n, rows // tr),
            in_specs=[pl.BlockSpec((None, None, tr, cols), lambda j, i, sel_ref: (j, sel_ref[0], i, 0)), spec],
            out_specs=spec),
    )(sel, x, sib)


def _chip_sum(pair, recv, sel, *, name, tr=256):
    _, rows, cols = pair.shape
    tr = _tile_rows(rows, tr)

    def body(sel_ref, p_ref, r_ref, o_ref):
        acc = p_ref[...].astype(F32)
        for k in range(3):
            acc = acc + r_ref[k].astype(F32)
        o_ref[...] = acc

    return pl.pallas_call(
        body, name=name, out_shape=jax.ShapeDtypeStruct((rows, cols), F32), compiler_params=_params(1),
        grid_spec=pltpu.PrefetchScalarGridSpec(
            num_scalar_prefetch=1, grid=(rows // tr,),
            in_specs=[pl.BlockSpec((None, tr, cols), lambda i, sel_ref: (sel_ref[0], i, 0)),
                      pl.BlockSpec((3, tr, cols), lambda i, sel_ref: (0, i, 0))],
            out_specs=pl.BlockSpec((tr, cols), lambda i, sel_ref: (i, 0))),
    )(sel, pair, recv)


def _me():
    return lax.axis_index("x"), lax.axis_index("y"), lax.axis_index("c")


def _flip(pos, bits):
    x, y, c = pos
    return (x ^ bits[0] if bits[0] else x, y ^ bits[1] if bits[1] else y, c ^ bits[2] if bits[2] else c)


ANY = pl.BlockSpec(memory_space=pl.ANY)


def _all_gather8(xs, *, name):
    n = len(xs)
    flips = [((k >> 2) & 1, (k >> 1) & 1, k & 1) for k in range(1, 8)]

    def body(*refs):
        x_refs, out_refs, (send_sems, recv_sems, local_sems) = refs[:n], refs[n:2 * n], refs[2 * n:]
        me = _me()
        slot = lambda p: 4 * p[0] + 2 * p[1] + p[2]
        copies = []
        for i in range(n):
            mine = pltpu.make_async_copy(x_refs[i], out_refs[i].at[slot(me)], local_sems.at[i])
            mine.start()
            copies.append(mine)
            for k, f in enumerate(flips):
                peer = _flip(me, f)
                sems = dict(send_sem=send_sems.at[7 * i + k], recv_sem=recv_sems.at[7 * i + k], device_id=peer,
                            device_id_type=MESH)
                cp = pltpu.make_async_remote_copy(src_ref=x_refs[i], dst_ref=out_refs[i].at[slot(me)], **sems)
                cp.start()
                copies.append(cp)
                copies.append(pltpu.make_async_remote_copy(src_ref=x_refs[i], dst_ref=out_refs[i].at[slot(peer)], **sems))
        for i in range(n):
            base = i * 15
            copies[base].wait()
            for k in range(7):
                copies[base + 1 + 2 * k].wait_send()
                copies[base + 2 + 2 * k].wait_recv()

    outs = pl.pallas_call(
        body, name=name, in_specs=[ANY] * n, out_specs=[ANY] * n,
        out_shape=[jax.ShapeDtypeStruct((8, *x.shape), x.dtype) for x in xs],
        scratch_shapes=[pltpu.SemaphoreType.DMA((7 * n,)), pltpu.SemaphoreType.DMA((7 * n,)),
                        pltpu.SemaphoreType.DMA((n,))])(*xs)
    return list(outs)


CHIP_FLIPS = [(1, 0, 0), (0, 1, 0), (1, 1, 0)]


def _chip():
    return 2 * lax.axis_index("x") + lax.axis_index("y")


HBM = pl.BlockSpec(memory_space=pltpu.HBM)
SEM = pl.BlockSpec(memory_space=pltpu.SEMAPHORE)
EFFECT = pltpu.SideEffectType.DATAFLOW_SIDE_EFFECTING


def _plan_copies(plan, refs, send_sems, recv_sems):
    return [pltpu.make_async_remote_copy(src_ref=src, dst_ref=dst, send_sem=send_sems.at[k], recv_sem=recv_sems.at[k],
                                         device_id=to, device_id_type=MESH) for k, (src, dst, to) in enumerate(plan(refs))]


def _rdma_start(arrays, n_copies, plan, deps, *, name):
    n, nd = len(arrays), len(deps)

    def body(*refs):
        for cp in _plan_copies(plan, refs[:n], refs[n + nd], refs[n + nd + 1]):
            cp.start()
        refs[-1][...] = jnp.zeros_like(refs[-1])

    outs = pl.pallas_call(
        body, name=name,
        out_shape=(pltpu.SemaphoreType.DMA((n_copies,)), pltpu.SemaphoreType.DMA((n_copies,)),
                   *[pltpu.HBM(a.shape, a.dtype) for a in arrays], jax.ShapeDtypeStruct((8, LANE), F32)),
        in_specs=[HBM] * n + [ANY] * nd, out_specs=(SEM, SEM, *[HBM] * n, pl.BlockSpec(memory_space=pltpu.VMEM)),
        input_output_aliases={i: i + 2 for i in range(n)}, compiler_params=pltpu.CompilerParams(has_side_effects=EFFECT),
    )(*[pltpu.with_memory_space_constraint(a, pltpu.HBM) for a in arrays], *deps)
    return outs[0], outs[1], list(outs[2:2 + n]), outs[-1]


def _rdma_wait(send_sems, recv_sems, arrays, plan, after, *, name):
    n = len(arrays)

    def body(*refs):
        for cp in _plan_copies(plan, refs[:n], refs[n], refs[n + 1]):
            cp.wait_send()
            cp.wait_recv()

    return list(pl.pallas_call(
        body, name=name, out_shape=tuple(pltpu.HBM(a.shape, a.dtype) for a in arrays),
        in_specs=[HBM] * n + [SEM, SEM, ANY], out_specs=tuple([HBM] * n), input_output_aliases={i: i for i in range(n)},
        compiler_params=pltpu.CompilerParams(has_side_effects=EFFECT),
    )(*arrays, send_sems, recv_sems, after))


def _gather_plan(n):
    def plan(refs):
        me = _me()
        slot = 2 * me[0] + me[1]
        return [(refs[i].at[me[2]], refs[n + i].at[slot, me[2]], _flip(me, f)) for i in range(n) for f in CHIP_FLIPS]
    return plan


def _scatter_plan(n):
    def plan(refs):
        me = _me()
        out = []
        for i in range(n):
            for k, f in enumerate(CHIP_FLIPS):
                peer = _flip(me, f)
                out.append((refs[i].at[2 * peer[0] + peer[1]], refs[n + i].at[k], peer))
        return out
    return plan


def _sibling_plan(n, src_of):
    def plan(refs):
        me = _me()
        return [(src_of(refs[i], me[2]), refs[n + i], _flip(me, (0, 0, 1))) for i in range(n)]
    return plan


def _gather8_plan(n):
    def plan(refs):
        me = _me()
        slot = 4 * me[0] + 2 * me[1] + me[2]
        return [(refs[i], refs[n + i].at[slot], _flip(me, ((k >> 2) & 1, (k >> 1) & 1, k & 1)))
                for i in range(n) for k in range(1, 8)]
    return plan


def _pair_fill(lands, *, name):
    n = len(lands)

    def body(*refs):
        in_refs, (send_sems, recv_sems) = refs[:n], refs[2 * n:]
        me = _me()
        sib = _flip(me, (0, 0, 1))
        copies = []
        for i in range(n):
            for k, f in enumerate(CHIP_FLIPS):
                peer = _flip(me, f)
                slot = 2 * peer[0] + peer[1]
                mine, theirs = in_refs[i].at[slot, me[2]], in_refs[i].at[slot, 1 - me[2]]
                cp = pltpu.make_async_remote_copy(src_ref=mine, dst_ref=mine, send_sem=send_sems.at[3 * i + k],
                                                  recv_sem=recv_sems.at[3 * i + k], device_id=sib, device_id_type=MESH)
                cp.start()
                copies.append((cp, pltpu.make_async_remote_copy(
                    src_ref=mine, dst_ref=theirs, send_sem=send_sems.at[3 * i + k], recv_sem=recv_sems.at[3 * i + k],
                    device_id=sib, device_id_type=MESH)))
        for cp, arrival in copies:
            arrival.wait_recv()
            cp.wait_send()

    return list(pl.pallas_call(
        body, name=name, in_specs=[ANY] * n, out_specs=[ANY] * n,
        out_shape=[jax.ShapeDtypeStruct(a.shape, a.dtype) for a in lands], input_output_aliases={i: i for i in range(n)},
        scratch_shapes=[pltpu.SemaphoreType.DMA((3 * n,)), pltpu.SemaphoreType.DMA((3 * n,))])(*lands))


def _own_and_landed(lands, xs):
    chip = _chip()
    return [[jnp.where(chip == j, x, o.reshape(4, *x.shape)[j]) for j in range(4)] for o, x in zip(lands, xs)]


BIG = (("w_in", (D, IN_WIDTH // 4), 1), ("gla_w_o", (D // 4, D), 0), ("mla_w_uq", (MQR, MH * MQK // 4), 1),
       ("mla_w_ukv", (MKVR, MH * (MNOPE + MVD) // 4), 1), ("mla_w_o", (D // 4, D), 0), ("w_out", (D // 4, D), 0),
       ("mlp_w1", (D, DFF // 4), 1), ("mlp_w2", (DFF // 4, D), 0))
ADA_SHARD = (D, 6 * D // 4)
SMALL = (("b_ada", 6 * D), ("norm1_g", D), ("b_merge", 2 * D), ("gla_b_alpha", GH * GDK), ("gla_out_norm_g", GDV),
         ("mla_q_lat_g", MQR), ("mla_kv_lat_g", MKVR), ("mla_qn_g", MQK), ("mla_kn_g", MQK), ("norm2_g", D))


W_IN_SEGMENTS = ((0, 3072, OFF_Q), (3072, 3088, OFF_A), (3088, 3344, OFF_CQ), (3344, 3472, OFF_CKV),
                 (3472, 3504, OFF_KPE + MNOPE), (3504, 5552, OFF_MA))
W_IN_SPLIT = OFF_MA
SMALL_ROWS, SMALL_COLS = 32, 2 * D
W_ALPHA_ROW = 16
LOSS_ROW = 15
SMALL_RED = tuple((n, k) for n, k in SMALL if n != "b_ada")


def _pack_small(grads, d_w_alpha, loss_row, *, name):
    def body(*refs):
        g_refs, wa_ref, loss_ref, out_ref = refs[:-3], refs[-3], refs[-2], refs[-1]
        out_ref[...] = jnp.zeros_like(out_ref)
        for i, ((_, k), g_ref) in enumerate(zip(SMALL_RED, g_refs)):
            out_ref[i:i + 1, 0:k] = g_ref[...]
        out_ref[LOSS_ROW:LOSS_ROW + 1, 0:LANE] = loss_ref[...]
        out_ref[W_ALPHA_ROW:W_ALPHA_ROW + GLR, 0:GH * GDK] = wa_ref[...]

    return pl.pallas_call(body, name=name, out_shape=jax.ShapeDtypeStruct((SMALL_ROWS, SMALL_COLS), F32))(
        *grads, d_w_alpha, loss_row)


def _small_update(gathered, dmod_all, sel, wmv, *, name):
    names = [n for n, _ in SMALL] + ["gla_w_alpha"]
    n_par = len(names)

    def body(sel_ref, g_ref, dmod_ref, *refs):
        in_refs, out_refs, loss_ref, acc = refs[:3 * n_par], refs[3 * n_par:-2], refs[-2], refs[-1]
        total = g_ref[0]
        for j in range(1, 8):
            total = total + g_ref[j]
        acc[...] = total
        loss_ref[...] = acc[LOSS_ROW:LOSS_ROW + 1, 0:LANE]
        row = {n: i for i, (n, _) in enumerate(SMALL_RED)}
        for p, name_p in enumerate(names):
            w_ref, m_ref, v_ref = in_refs[3 * p:3 * p + 3]
            if name_p == "b_ada":
                gv = jnp.sum(dmod_ref[...], axis=0, keepdims=True)
            elif name_p == "gla_w_alpha":
                gv = jnp.zeros((GLR, GDK), F32)
                for j in range(4):
                    blk = acc[W_ALPHA_ROW:W_ALPHA_ROW + GLR, j * GDK:(j + 1) * GDK]
                    gv = gv + jnp.where(sel_ref[0] == j, blk, 0.0)
            else:
                gv = acc[row[name_p]:row[name_p] + 1, 0:w_ref.shape[1]]
            o = out_refs[4 * p:4 * p + 4]
            o[0][...] = gv
            o[1][...], o[2][...], o[3][...] = _adamw_update(w_ref[...], gv, m_ref[...], v_ref[...])

    flat = [a for t in wmv for a in t]
    out_shape = [jax.ShapeDtypeStruct(t[0].shape, F32) for t in wmv for _ in range(4)]
    out_shape.append(jax.ShapeDtypeStruct((1, LANE), F32))
    vmem = pl.BlockSpec(memory_space=pltpu.VMEM)
    outs = pl.pallas_call(
        body, name=name, out_shape=out_shape, in_specs=[pl.BlockSpec(memory_space=pltpu.SMEM), vmem, vmem] + [vmem] * len(flat),
        out_specs=[vmem] * len(out_shape), scratch_shapes=[pltpu.VMEM((SMALL_ROWS, SMALL_COLS), F32)],
    )(sel, gathered, dmod_all, *flat)
    return {n: tuple(outs[4 * p:4 * p + 4]) for p, n in enumerate(names)}, outs[-1][0, 0]


def _full_weights(gathered):
    w = {name: jnp.concatenate(gathered[name], axis=axis) for name, _, axis in BIG if name in gathered and name != "w_in"}
    if "w_in" in gathered:
        shards = gathered["w_in"]
        zeros = lambda n: [jnp.zeros((D, n), shards[0].dtype)]

        def cols(a, b):
            width = IN_WIDTH // 4
            return [shards[j][:, max(a, j * width) - j * width:min(b, (j + 1) * width) - j * width]
                    for j in range(4) if max(a, j * width) < min(b, (j + 1) * width)]

        parts = []
        for a, b, at in sorted(W_IN_SEGMENTS, key=lambda seg: seg[2]):
            have = sum(p.shape[1] for p in parts)
            parts += (zeros(at - have) if at > have else []) + cols(a, b)
        w["w_in"] = jnp.concatenate(parts + zeros(PW - sum(p.shape[1] for p in parts)), axis=1)
    if "mla_w_uq" in w:
        w["mla_w_uq"] = jnp.pad(w["mla_w_uq"].reshape(MQR, MH, MQK), ((0, 0), (0, 0), (0, LANE - MQK))).reshape(MQR, MH * LANE)
    if "mla_w_o" in w:
        w["mla_w_o"] = jnp.pad(w["mla_w_o"].reshape(MH, MVD, D), ((0, 0), (0, LANE - MVD), (0, 0))).reshape(MH * LANE, D)
    return w


def _grad_slots(g):
    g = dict(g)
    out = {}
    if "w_in" in g:
        g_lo, g_hi = g.pop("w_in")
        take = lambda at, lo, hi: g_lo[:, at + lo:at + hi] if at < W_IN_SPLIT else g_hi[:, at - W_IN_SPLIT + lo:at - W_IN_SPLIT + hi]
        width = IN_WIDTH // 4
        slots = []
        for j in range(4):
            lo, hi = j * width, (j + 1) * width
            slots.append(jnp.concatenate([take(at, max(lo, a) - a, min(hi, b) - a)
                                          for a, b, at in W_IN_SEGMENTS if max(lo, a) < min(hi, b)], axis=1))
        out["w_in"] = jnp.stack(slots).reshape(4, 2, D // 2, width)
    if "mla_w_uq" in g:
        g["mla_w_uq"] = g["mla_w_uq"].reshape(MQR, MH, LANE)[:, :, :MQK].reshape(MQR, MH * MQK)
    if "mla_w_o" in g:
        g["mla_w_o"] = g["mla_w_o"].reshape(MH, LANE, D)[:, :MVD].reshape(MH * MVD, D)
    for name, (rows, cols), axis in BIG:
        if name not in g:
            continue
        a = g[name]
        a = a.reshape(4, rows, cols) if axis == 0 else jnp.transpose(a.reshape(rows, 4, cols), (1, 0, 2))
        out[name] = a.reshape(4, 2, rows // 2, cols)
    return out


def _rope_tables(positions):
    freqs = ROPE_THETA ** (-jnp.arange(0, MROPE, 2, dtype=F32) / MROPE)
    lane = np.arange(LANE)
    in_rope = (lane >= MNOPE) & (lane < MQK)
    freq_lane = jnp.where(in_rope, freqs[(lane - MNOPE) % (MROPE // 2)], 0.0)
    sign = np.where(in_rope, np.where(lane < MNOPE + MROPE // 2, -1.0, 1.0), 0.0).astype(np.float32)
    ang = positions.astype(F32).reshape(-1, 1) * freq_lane[None, :]
    return jnp.cos(ang), jnp.sin(ang) * sign[None, :]


def _local_step(x, positions, mod, target, w, small, more_weights=None, on_grads=None):
    kept = {}
    if on_grads is None:
        on_grads = lambda tag, grads, after: kept.update(grads)
    bsz, s, _ = x.shape
    t = bsz * s
    tt = _tile(t, 1024)
    shift1, scale1, gate1, shift2, scale2, gate2 = [mod[:, None, i * D:(i + 1) * D] for i in range(6)]
    cos_t, sin_t = _rope_tables(positions)
    w_alpha_p = jnp.pad(small["gla_w_alpha"], ((0, LANE - GLR), (0, 0)))
    gq = jnp.pad(small["mla_qn_g"], ((0, 0), (0, LANE - MQK)))
    gk = jnp.pad(small["mla_kn_g"], ((0, 0), (0, LANE - MQK)))
    flat2 = lambda a: a.reshape(t, a.shape[-1])
    bsd = lambda a: a.reshape(bsz, s, a.shape[-1])

    h = _norm_mod(x, small["norm1_g"], scale1, shift1, name="norm1")
    if callable(w):
        w = w(h)
    proj = _mm(flat2(h), w["w_in"], name="proj", tn=1152, out_dtype=BF16)
    proj3 = bsd(proj)
    o, o_gated, states = _gla_fwd(proj3, w_alpha_p, small["gla_b_alpha"], small["gla_out_norm_g"], name="gla_fwd")
    if more_weights is not None:
        w = {**w, **more_weights(o_gated)}
    y_a = _mm(flat2(o_gated), w["gla_w_o"], name="gla_out", out_dtype=BF16)
    cq_n, ckv_n = _lat_norm(proj, small["mla_q_lat_g"], small["mla_kv_lat_g"], name="lat_norm")
    q_raw = _mm(cq_n, w["mla_w_uq"], name="mla_uq", out_dtype=BF16)
    kv = _mm(ckv_n, w["mla_w_ukv"], name="mla_ukv", out_dtype=BF16)
    qf, kf, vf = _qk_prep(q_raw, kv, proj, cos_t, sin_t, gq * Q_PRESCALE, gk, name="qk_prep")
    o_attn = _attn_fwd(bsd(qf), bsd(kf), bsd(vf), name="attn_fwd")
    y_b = _mm(flat2(o_attn), w["mla_w_o"], name="mla_out", out_dtype=BF16)
    mixed_in = _merge_fwd(proj3, small["b_merge"], bsd(y_a), bsd(y_b), name="merge_fwd")
    mixed = _mm(flat2(mixed_in), w["w_out"], name="w_out")
    x1, h2 = _resid_norm_mod(x, bsd(mixed), gate1, small["norm2_g"], scale2, shift2, name="norm2")

    def sqrelu(acc, ex, outs):
        r = jnp.maximum(acc, 0.0)
        outs[0][...] = (r * r).astype(BF16)

    r = _mm(flat2(h2), w["mlp_w1"], name="mlp1", epilogue=sqrelu, out_shape=jax.ShapeDtypeStruct((t, DFF), BF16),
            out_specs=_tile_spec(tt, 1024))
    ff = _mm(r, w["mlp_w2"], name="mlp2")
    dy, dff, dgate2, loss_part = _loss_head(x1, bsd(ff), gate2, target, name="loss_head")

    g = {}

    def relu2_bwd(acc, ex, outs):
        outs[0][...] = (acc * (2.0 * jnp.sqrt(ex[0][...].astype(F32)))).astype(BF16)

    dff2 = flat2(dff)
    da1 = _mm(dff2, w["mlp_w2"], tb=True, name="mlp2_dx", epilogue=relu2_bwd, extras=(r,),
              extra_specs=(_tile_spec(tt, 1024),), out_shape=jax.ShapeDtypeStruct((t, DFF), BF16),
              out_specs=_tile_spec(tt, 1024))
    g["mlp_w2"] = _mm(r, dff2, ta=True, name="mlp2_dw")
    dh2 = _mm(da1, w["mlp_w1"], tb=True, name="mlp1_dx")
    g["mlp_w1"] = _mm(flat2(h2), da1, ta=True, name="mlp1_dw")
    token = on_grads("mlp", {n: g.pop(n) for n in ("mlp_w2", "mlp_w1")}, dh2)
    if token is not None:
        gate1 = gate1 + token[0, 0]
    dx1, dscale2, dshift2, dg2, dgate1, dmixed = _norm_mod_bwd(
        bsd(dh2), x1, dy, small["norm2_g"], scale2, gate1, bsd(mixed), name="norm2_bwd")
    dmixed2 = flat2(dmixed)
    dmi = _mm(dmixed2, w["w_out"], tb=True, name="w_out_dx", out_dtype=BF16)
    g["w_out"] = _mm(flat2(mixed_in), dmixed2, ta=True, name="w_out_dw")
    dy_a, dy_b, dl_a, dl_b, db_a, db_b = _merge_bwd(bsd(dmi), proj3, small["b_merge"], bsd(y_a), bsd(y_b), name="merge_bwd")
    dy_a2, dy_b2 = flat2(dy_a), flat2(dy_b)
    dog = _mm(dy_a2, w["gla_w_o"], tb=True, name="gla_out_dx")
    g["gla_w_o"] = _mm(flat2(o_gated), dy_a2, ta=True, name="gla_out_dw")
    dq_g, dk_g, dv_g, dg_g, dlog, db_alpha, d_ong = _gla_bwd(
        bsd(dog), o, states, proj3, w_alpha_p, small["gla_b_alpha"], small["gla_out_norm_g"], name="gla_bwd")
    dlog2 = flat2(dlog)
    da_p = _mm(dlog2, w_alpha_p, tb=True, out_dtype=BF16, name="alpha_dx")
    d_w_alpha = _mm(proj[:, OFF_A:OFF_A + LANE], dlog2, ta=True, name="alpha_dw")[:GLR]
    do_attn = _mm(dy_b2, w["mla_w_o"], tb=True, out_dtype=BF16, name="mla_out_dx")
    g["mla_w_o"] = _mm(flat2(o_attn), dy_b2, ta=True, name="mla_out_dw")
    dqf, dkf, dvf = _attn_bwd(bsd(qf), bsd(kf), bsd(vf), bsd(do_attn), name="attn_bwd")
    dq_raw, dkv, dkpe, dgq, dgk = _qk_prep_bwd(flat2(dqf), flat2(dkf), flat2(dvf), q_raw, kv, proj, cos_t, sin_t, gq, gk,
                                                name="qk_prep_bwd")
    dcq_n = _mm(dq_raw, w["mla_w_uq"], tb=True, name="mla_uq_dx")
    g["mla_w_uq"] = _mm(cq_n, dq_raw, ta=True, name="mla_uq_dw")
    dckv_n = _mm(dkv, w["mla_w_ukv"], tb=True, name="mla_ukv_dx")
    g["mla_w_ukv"] = _mm(ckv_n, dkv, ta=True, name="mla_ukv_dw")
    token = on_grads("mix", {n: g.pop(n) for n in ("w_out", "gla_w_o", "mla_w_o", "mla_w_uq", "mla_w_ukv")}, dckv_n)
    q_lat_g = small["mla_q_lat_g"] if token is None else small["mla_q_lat_g"] + token[0:1, 0:1]
    dcq, dckv, dg_qlat, dg_kvlat = _lat_norm_bwd(dcq_n, dckv_n, proj, q_lat_g, small["mla_kv_lat_g"],
                                                  name="lat_norm_bwd")
    pieces = [(flat2(dq_g), OFF_Q), (flat2(dk_g), OFF_K), (flat2(dv_g), OFF_V), (flat2(dg_g), OFF_G),
              (flat2(dl_a), OFF_MA), (flat2(dl_b), OFF_MB), (dcq, OFF_CQ), (dckv, OFF_CKV), (da_p, OFF_A), (dkpe, OFF_KPE)]
    hb = flat2(h)
    g_w_in = (_pieces_dw(hb, [p for p, off in pieces if off < W_IN_SPLIT], name="proj_dw_a"),
              _pieces_dw(hb, [p for p, off in pieces if off >= W_IN_SPLIT], name="proj_dw_b"))
    token = on_grads("in", {"w_in": g_w_in}, g_w_in[1])
    after = jnp.zeros((8, LANE), F32) if token is None else token
    dh = _pieces_dx(pieces, w["w_in"], after, name="proj_dx")
    token = on_grads("dx", {}, dh)
    if token is not None:
        scale1 = scale1 + token[0, 0]
    grad_x, dscale1, dshift1, dg1 = _norm_mod_bwd(bsd(dh), x, dx1, small["norm1_g"], scale1, name="norm1_bwd")

    dmod = jnp.concatenate([dshift1, dscale1, dgate1, dshift2, dscale2, dgate2], axis=-1).reshape(bsz, 6 * D)
    gs = {"norm1_g": dg1, "b_merge": jnp.concatenate([db_a, db_b], axis=1), "gla_b_alpha": db_alpha,
          "gla_out_norm_g": d_ong, "mla_q_lat_g": dg_qlat, "mla_kv_lat_g": dg_kvlat, "mla_qn_g": dgq[:, :MQK],
          "mla_kn_g": dgk[:, :MQK], "norm2_g": dg2}
    return loss_part[0, 0], grad_x, dmod, {**kept, **g}, gs, d_w_alpha


def kernel(x, c, positions, w_ada, b_ada, norm1_g, w_in, b_merge, gla_w_alpha, gla_b_alpha, gla_out_norm_g, gla_w_o, mla_q_lat_g, mla_w_uq, mla_kv_lat_g, mla_w_ukv, mla_qn_g, mla_kn_g, mla_w_o, w_out, norm2_g, mlp_w1, mlp_w2, loss_target, m_w_ada, m_b_ada, m_norm1_g, m_w_in, m_b_merge, m_gla_w_alpha, m_gla_b_alpha, m_gla_out_norm_g, m_gla_w_o, m_mla_q_lat_g, m_mla_w_uq, m_mla_kv_lat_g, m_mla_w_ukv, m_mla_qn_g, m_mla_kn_g, m_mla_w_o, m_w_out, m_norm2_g, m_mlp_w1, m_mlp_w2, v_w_ada, v_b_ada, v_norm1_g, v_w_in, v_b_merge, v_gla_w_alpha, v_gla_b_alpha, v_gla_out_norm_g, v_gla_w_o, v_mla_q_lat_g, v_mla_w_uq, v_mla_kv_lat_g, v_mla_w_ukv, v_mla_qn_g, v_mla_kn_g, v_mla_w_o, v_w_out, v_norm2_g, v_mlp_w1, v_mlp_w2):
    args = dict(locals())
    names_big = [n for n, _, _ in BIG]
    names_small = [n for n, _ in SMALL]
    bsz = x.shape[0]
    ax, ay, ac = lax.axis_index("x"), lax.axis_index("y"), lax.axis_index("c")
    chip = 2 * ax + ay
    dev = 2 * chip + ac

    small = {n: args[n] for n in names_small}
    sel_c = jnp.reshape(ac, (1,)).astype(jnp.int32)
    sel_chip = jnp.reshape(chip, (1,)).astype(jnp.int32)
    c_all, w_alpha_all = _all_gather8([c, gla_w_alpha[0]], name="comm_c_alpha")
    small["gla_w_alpha"] = jnp.concatenate([w_alpha_all[2 * j] for j in range(4)], axis=1)
    c_all = c_all.reshape(8 * bsz, D)

    shards = {n: args[n][0].astype(BF16) for n in names_big}
    halves_of = lambda names: [shards[n].reshape(2, shards[n].shape[0] // 2, shards[n].shape[1]) for n in names]

    def gather_start(names, deps, tag):
        xs = halves_of(names)
        lands = [lax.empty((4, *xh.shape), BF16) for xh in xs]
        plan = _gather_plan(len(names))
        return names, plan, _rdma_start(xs + lands, 3 * len(names), plan, deps, name="comm_weights_start_" + tag)

    def gather_finish(started, after, tag):
        names, plan, sems = started
        arrs = _rdma_wait(sems[0], sems[1], sems[2], plan, after, name="comm_weights_wait_" + tag)
        filled = _pair_fill(arrs[len(names):], name="comm_weights_pair_" + tag)
        own = [a.reshape(shards[n].shape) for n, a in zip(names, arrs)]
        return _full_weights(dict(zip(names, _own_and_landed(filled, own))))


    def add_bias(acc, ex, outs):
        outs[0][...] = acc + ex[0][...]

    silu = lambda v: v * _sigmoid(v)
    b_ada_mine = lax.dynamic_slice(b_ada, (0, chip * ADA_SHARD[1]), (1, ADA_SHARD[1]))
    mod_part = _mm(c_all, w_ada[0], name="ada", tn=512, a_fn=silu, epilogue=add_bias, extras=(b_ada_mine,),
                   extra_specs=(pl.BlockSpec((1, 512), lambda i, j, k: (0, j)),),
                   out_shape=jax.ShapeDtypeStruct((8 * bsz, ADA_SHARD[1]), F32), out_specs=_tile_spec(8 * bsz, 512))
    mod_all = _all_gather8([mod_part], name="comm_mod")[0]
    mod_rows = lax.dynamic_slice(mod_all, (0, dev * bsz, 0), (8, bsz, ADA_SHARD[1]))
    mod = jnp.concatenate([mod_rows[2 * j] for j in range(4)], axis=1)
    first = gather_start(["w_in"], (mod,), "in")
    rest = gather_start([n for n in names_big if n != "w_in"], (mod, first[2][3]), "rest")
    mod = mod + rest[2][3][0, 0]
    w_in_after = lambda after: gather_finish(first, after, "in")
    more_weights = lambda after: gather_finish(rest, after, "rest")

    stage = {}

    def begin(tag, names, arrays, lands, n_copies, plan, what):
        stage[tag] = (names, plan, _rdma_start(arrays + lands, n_copies, plan, (), name=f"comm_{what}_start_{tag}"))
        return stage[tag][2][3]

    def landed(tag, after, what):
        names, plan, sems = stage[tag]
        arrs = _rdma_wait(sems[0], sems[1], sems[2], plan, after, name=f"comm_{what}_wait_{tag}")
        return names, arrs[:len(arrs) // 2], arrs[len(arrs) // 2:]

    def swap_start(tag, grads):
        names = list(grads)
        parts = [_grad_slots(grads)[n] for n in names]
        lands = [lax.empty((4, *p.shape[2:]), F32) for p in parts]
        return begin(tag, names, parts, lands, len(names), _sibling_plan(len(names), lambda r, c: r.at[:, 1 - c]), "pair_sum")

    def scatter_start(tag, after):
        names, parts, sib_halves = landed(tag, after, "pair_sum")
        pairs = [_pair_add(p, s, sel_c, name="pair_add_" + n) for n, p, s in zip(names, parts, sib_halves)]
        recvs = [lax.empty((3, *p.shape[1:]), BF16) for p in pairs]
        return begin(tag, names, pairs, recvs, 3 * len(names), _scatter_plan(len(names)), "scatter")

    def join_start(tag, after):
        names, pairs, recvs = landed(tag, after, "scatter")
        halves = [_chip_sum(p, r, sel_chip, name="chip_sum_" + n) for n, p, r in zip(names, pairs, recvs)]
        lands = [lax.empty(h.shape, F32) for h in halves]
        return begin(tag, names, halves, lands, len(names), _sibling_plan(len(names), lambda r, c: r), "pair_join")

    def reduce_step(tag, grads, after):
        if tag == "mlp":
            return swap_start("mlp", grads)
        if tag == "mix":
            return scatter_start("mlp", after) + swap_start("mix", grads)
        if tag == "in":
            return scatter_start("mix", after) + swap_start("in", grads)
        return scatter_start("in", after)

    loss_part, grad_x, dmod, g, gs, d_w_alpha = _local_step(x, positions, mod, loss_target, w_in_after, small,
                                                            more_weights, reduce_step)

    assert not g, list(g)
    gs_packed = _pack_small([gs[n] for n, _ in SMALL_RED], d_w_alpha, jnp.full((1, LANE), loss_part, F32),
                            name="pack_small")
    small_lands = [lax.empty((8, *a.shape), F32) for a in (dmod, gs_packed)]
    begin("small", ["dmod", "small"], [dmod, gs_packed], small_lands, 7 * 2, _gather8_plan(2), "gather8")

    res = {}

    def finish(tag, after):
        names, halves, theirs = landed(tag, after, "pair_join")
        for n, mine, other in zip(names, halves, theirs):
            if n == "w_in":
                south = ac == 0
                g_t = jnp.concatenate([jnp.where(south, mine, other), jnp.where(south, other, mine)], axis=0).T
                outs = _adamw(w_in[0].T, g_t, m_w_in[0].T, v_w_in[0].T, name="adamw_w_in", by_cols=True)
                res[n] = tuple(a.T for a in (g_t, *outs))
            else:
                res[n] = _adamw_halves(args[n][0], args["m_" + n][0], args["v_" + n][0], mine, other, sel_c,
                                       name="adamw_" + n)
        return res[names[-1]][1]

    join_start("mlp", grad_x)
    join_start("mix", grad_x)
    done = finish("mix", finish("mlp", grad_x))

    _, (dmod_own, gs_own), (dmod_all, gs_all) = landed("small", done, "gather8")
    dmod_all = lax.dynamic_update_slice(dmod_all, dmod_own[None], (dev, 0, 0)).reshape(8 * bsz, 6 * D)
    gs_all = lax.dynamic_update_slice(gs_all, gs_own[None], (dev, 0, 0))
    dmod_mine = lax.dynamic_slice(dmod_all, (0, chip * ADA_SHARD[1]), (8 * bsz, ADA_SHARD[1]))
    g_w_ada = _mm(c_all, dmod_mine, ta=True, a_fn=silu, name="ada_dw")
    wmv = [(args[n], args["m_" + n], args["v_" + n]) for n in names_small]
    wmv.append((gla_w_alpha[0], m_gla_w_alpha[0], v_gla_w_alpha[0]))
    res_small, loss_sum = _small_update(gs_all, dmod_all, sel_chip, wmv, name="small_update")
    res.update(res_small)
    loss = loss_sum * (0.5 / D)
    join_start("in", g_w_ada)
    res["w_ada"] = (g_w_ada, *_adamw(w_ada[0], g_w_ada, m_w_ada[0], v_w_ada[0], name="adamw_w_ada"))
    finish("in", res["w_ada"][1])

    order = ["w_ada", "b_ada", "norm1_g", "w_in", "b_merge", "gla_w_alpha", "gla_b_alpha", "gla_out_norm_g", "gla_w_o",
             "mla_q_lat_g", "mla_w_uq", "mla_kv_lat_g", "mla_w_ukv", "mla_qn_g", "mla_kn_g", "mla_w_o", "w_out",
             "norm2_g", "mlp_w1", "mlp_w2"]
    named = lambda k: [res[n][k].reshape(args[n].shape) for n in order]
    return (loss, grad_x, *named(0), *named(1), *named(2), *named(3))
```

```python
import jax
import jax.numpy as jnp
import numpy as np
from jax import lax
from jax.experimental import pallas as pl
from jax.experimental.pallas import tpu as pltpu

F32 = jnp.float32
BF16 = jnp.bfloat16
MESH = pl.DeviceIdType.MESH

D = 1024
CHUNK = 64
EPS = 1e-6
GH, GDK, GDV, GLR, GTAU = 4, 128, 256, 16, 16.0
MH, MQR, MKVR, MNOPE, MROPE, MVD = 16, 256, 128, 64, 32, 64
MQK = MNOPE + MROPE
DFF = 4 * D
ROPE_THETA = 10000.0
IN_WIDTH = 5552
LANE = 128
OFF_Q, OFF_K, OFF_V, OFF_G, OFF_MA, OFF_MB, OFF_CQ, OFF_CKV, OFF_A, OFF_KPE, PW = (
    0, 512, 1024, 2048, 3072, 4096, 5120, 5376, 5504, 5632, 5760)
ADAM_LR, ADAM_B1, ADAM_B2, ADAM_EPS, ADAM_WD, ADAM_STEP = 0.001, 0.9, 0.999, 1e-08, 0.01, 10
VMEM_LIMIT = 48 * 1024 * 1024


def _params(n_axes):
    return pltpu.CompilerParams(dimension_semantics=("arbitrary",) * n_axes, vmem_limit_bytes=VMEM_LIMIT)


def _tile(n, target):
    if n <= target:
        return n
    best = None
    for t in range(LANE, target + 1, LANE):
        if n % t == 0:
            best = t
    assert best is not None, (n, target)
    return best


def _sigmoid(x):
    return 1.0 / (1.0 + jnp.exp(-x))


MM_VMEM_BUDGET = 36 * 1024 * 1024


def _mm(a, b, *, name, ta=False, tb=False, out_dtype=F32, tm=1024, tn=1024, tk=4096,
        epilogue=None, extras=(), extra_specs=(), out_shape=None, out_specs=None, a_fn=None):
    if ta:
        kdim, m = a.shape
    else:
        m, kdim = a.shape
    if tb:
        n, k2 = b.shape
    else:
        k2, n = b.shape
    assert kdim == k2, (a.shape, b.shape)
    tm, tn, tk = _tile(m, tm), _tile(n, tn), _tile(kdim, tk)
    tiles = lambda rows: 2 * (rows * tk * a.dtype.itemsize + tk * tn * b.dtype.itemsize + rows * tn * 4) + rows * tn * 4
    while out_shape is None and tiles(tm) > MM_VMEM_BUDGET and tm % 256 == 0:
        tm //= 2
    nk = kdim // tk
    a_spec = pl.BlockSpec((tk, tm), lambda i, j, k: (k, i)) if ta else pl.BlockSpec((tm, tk), lambda i, j, k: (i, k))
    b_spec = pl.BlockSpec((tn, tk), lambda i, j, k: (j, k)) if tb else pl.BlockSpec((tk, tn), lambda i, j, k: (k, j))
    dims = (((0 if ta else 1,), (1 if tb else 0,)), ((), ()))
    ne = len(extras)
    if out_shape is None:
        out_shape = jax.ShapeDtypeStruct((m, n), out_dtype)
        out_specs = pl.BlockSpec((tm, tn), lambda i, j, k: (i, j))
    n_out = len(out_shape) if isinstance(out_shape, (list, tuple)) else 1
    in_place = epilogue is None and n_out == 1 and out_shape.dtype == F32
    scratch = [] if (nk == 1 or in_place) else [pltpu.VMEM((tm, tn), F32)]

    def body(a_ref, b_ref, *rest):
        ex, outs = rest[:ne], rest[ne:ne + n_out]
        av = a_ref[...] if a_fn is None else a_fn(a_ref[...])
        prod = lax.dot_general(av.astype(BF16), b_ref[...].astype(BF16), dims, preferred_element_type=F32)

        def finish(val):
            if epilogue is None:
                outs[0][...] = val.astype(outs[0].dtype)
            else:
                epilogue(val, ex, outs)

        if nk == 1:
            finish(prod)
            return
        k = pl.program_id(2)
        acc = outs[0] if in_place else rest[-1]

        @pl.when(k == 0)
        def _():
            acc[...] = prod

        @pl.when(k > 0)
        def _():
            acc[...] += prod

        if not in_place:
            @pl.when(k == nk - 1)
            def _():
                finish(acc[...])

    return pl.pallas_call(
        body, name=name, grid=(m // tm, n // tn, nk),
        in_specs=[a_spec, b_spec, *extra_specs], out_specs=out_specs, out_shape=out_shape,
        scratch_shapes=scratch, compiler_params=_params(3),
    )(a, b, *extras)


def _tile_spec(tm, tn):
    return pl.BlockSpec((tm, tn), lambda i, j, k: (i, j))


def _pieces_dx(pieces, w, after, *, name, tm=256):
    t = pieces[0][0].shape[0]
    tm = _tile(t, tm)
    npc = len(pieces)

    def body(*refs):
        p_refs, w_ref, out_ref = refs[:npc], refs[npc], refs[-1]
        acc = None
        for (arr, off), p_ref in zip(pieces, p_refs):
            part = lax.dot_general(p_ref[...].astype(BF16), w_ref[:, off:off + arr.shape[1]], _NT,
                                   preferred_element_type=F32)
            acc = part if acc is None else acc + part
        out_ref[...] = acc

    return pl.pallas_call(
        body, name=name, grid=(t // tm,),
        in_specs=[pl.BlockSpec((tm, arr.shape[1]), lambda i: (i, 0)) for arr, _ in pieces]
        + [pl.BlockSpec(w.shape, lambda i: (0, 0)), pl.BlockSpec((8, LANE), lambda i: (0, 0))],
        out_specs=pl.BlockSpec((tm, w.shape[0]), lambda i: (i, 0)),
        out_shape=jax.ShapeDtypeStruct((t, w.shape[0]), F32), compiler_params=_params(1),
    )(*[arr for arr, _ in pieces], w, after)


def _pieces_dw(h, pieces, *, name, tk=1024):
    t, d = h.shape
    tk = _tile(t, tk)
    widths = [p.shape[1] for p in pieces]
    starts = [sum(widths[:i]) for i in range(len(pieces))]

    def body(h_ref, *refs):
        p_refs, out_ref = refs[:-1], refs[-1]
        first = pl.program_id(0) == 0
        hv = h_ref[...]
        for p_ref, start, width in zip(p_refs, starts, widths):
            part = lax.dot_general(hv, p_ref[...].astype(BF16), _TN, preferred_element_type=F32)
            cols = slice(start, start + width)

            @pl.when(first)
            def _():
                out_ref[:, cols] = part

            @pl.when(jnp.logical_not(first))
            def _():
                out_ref[:, cols] += part

    return pl.pallas_call(
        body, name=name, grid=(t // tk,),
        in_specs=[pl.BlockSpec((tk, d), lambda k: (k, 0))] + [pl.BlockSpec((tk, wd), lambda k: (k, 0)) for wd in widths],
        out_specs=pl.BlockSpec((d, sum(widths)), lambda k: (0, 0)),
        out_shape=jax.ShapeDtypeStruct((d, sum(widths)), F32), compiler_params=_params(1),
    )(h, *pieces)


def _rms(x, g):
    r = lax.rsqrt(jnp.mean(x * x, axis=-1, keepdims=True) + EPS)
    return x * r, r


def _row_spec(ts, width, col=0):
    return pl.BlockSpec((None, ts, width), lambda b, i: (b, i, col))


def _vec_spec(width):
    return pl.BlockSpec((None, 1, width), lambda b, i: (b, 0, 0))


def _gain_spec(width):
    return pl.BlockSpec((1, width), lambda b, i: (0, 0))


def _norm_mod(x, g, scale, shift, *, name, ts=512):
    bsz, s, d = x.shape
    ts = min(ts, s)

    def body(x_ref, g_ref, sc_ref, sh_ref, h_ref):
        xh, _ = _rms(x_ref[...], None)
        h_ref[...] = ((xh * g_ref[...]) * (1.0 + sc_ref[...]) + sh_ref[...]).astype(BF16)

    return pl.pallas_call(
        body, name=name, grid=(bsz, s // ts),
        in_specs=[_row_spec(ts, d), _gain_spec(d), _vec_spec(d), _vec_spec(d)],
        out_specs=_row_spec(ts, d), out_shape=jax.ShapeDtypeStruct((bsz, s, d), BF16),
        compiler_params=_params(2),
    )(x, g, scale, shift)


def _resid_norm_mod(x, mixed, gate, g, scale, shift, *, name, ts=512):
    bsz, s, d = x.shape
    ts = min(ts, s)

    def body(x_ref, mx_ref, gt_ref, g_ref, sc_ref, sh_ref, x1_ref, h_ref):
        x1 = x_ref[...] + gt_ref[...] * mx_ref[...]
        x1_ref[...] = x1
        xh, _ = _rms(x1, None)
        h_ref[...] = ((xh * g_ref[...]) * (1.0 + sc_ref[...]) + sh_ref[...]).astype(BF16)

    return pl.pallas_call(
        body, name=name, grid=(bsz, s // ts),
        in_specs=[_row_spec(ts, d), _row_spec(ts, d), _vec_spec(d), _gain_spec(d), _vec_spec(d), _vec_spec(d)],
        out_specs=[_row_spec(ts, d), _row_spec(ts, d)],
        out_shape=[jax.ShapeDtypeStruct((bsz, s, d), F32), jax.ShapeDtypeStruct((bsz, s, d), BF16)],
        compiler_params=_params(2),
    )(x, mixed, gate, g, scale, shift)


def _norm_mod_bwd(dh, xin, resid, g, scale, gate=None, mixed=None, *, name, ts=512):
    bsz, s, d = xin.shape
    ts = min(ts, s)
    gated = gate is not None

    def body(*refs):
        if gated:
            dh_ref, x_ref, rs_ref, g_ref, sc_ref, gt_ref, mx_ref, dx_ref, dsc_ref, dsh_ref, dg_ref, dgt_ref, dmx_ref = refs
        else:
            dh_ref, x_ref, rs_ref, g_ref, sc_ref, dx_ref, dsc_ref, dsh_ref, dg_ref = refs
        b, i = pl.program_id(0), pl.program_id(1)

        @pl.when(i == 0)
        def _():
            dsc_ref[...] = jnp.zeros_like(dsc_ref)
            dsh_ref[...] = jnp.zeros_like(dsh_ref)
            if gated:
                dgt_ref[...] = jnp.zeros_like(dgt_ref)

        @pl.when((i == 0) & (b == 0))
        def _():
            dg_ref[...] = jnp.zeros_like(dg_ref)

        dh_v, gv = dh_ref[...], g_ref[...]
        xh, r = _rms(x_ref[...], None)
        dsc_ref[...] += jnp.sum(dh_v * (xh * gv), axis=0, keepdims=True)
        dsh_ref[...] += jnp.sum(dh_v, axis=0, keepdims=True)
        dn = dh_v * (1.0 + sc_ref[...])
        dg_ref[...] += jnp.sum(dn * xh, axis=0, keepdims=True)
        dxh = dn * gv
        dx = rs_ref[...] + r * (dxh - xh * jnp.mean(dxh * xh, axis=-1, keepdims=True))
        dx_ref[...] = dx
        if gated:
            dgt_ref[...] += jnp.sum(dx * mx_ref[...], axis=0, keepdims=True)
            dmx_ref[...] = (dx * gt_ref[...]).astype(BF16)

    ins = [dh, xin, resid, g, scale]
    in_specs = [_row_spec(ts, d), _row_spec(ts, d), _row_spec(ts, d), _gain_spec(d), _vec_spec(d)]
    out_specs = [_row_spec(ts, d), _vec_spec(d), _vec_spec(d), _gain_spec(d)]
    out_shape = [jax.ShapeDtypeStruct((bsz, s, d), F32), jax.ShapeDtypeStruct((bsz, 1, d), F32),
                 jax.ShapeDtypeStruct((bsz, 1, d), F32), jax.ShapeDtypeStruct((1, d), F32)]
    if gated:
        ins += [gate, mixed]
        in_specs += [_vec_spec(d), _row_spec(ts, d)]
        out_specs += [_vec_spec(d), _row_spec(ts, d)]
        out_shape += [jax.ShapeDtypeStruct((bsz, 1, d), F32), jax.ShapeDtypeStruct((bsz, s, d), BF16)]
    return pl.pallas_call(
        body, name=name, grid=(bsz, s // ts), in_specs=in_specs, out_specs=out_specs, out_shape=out_shape,
        compiler_params=_params(2),
    )(*ins)


def _loss_head(x1, ff, gate2, target, *, name, ts=512):
    bsz, s, d = x1.shape
    ts = min(ts, s)

    def body(x1_ref, ff_ref, gt_ref, t_ref, dy_ref, dff_ref, dgt_ref, loss_ref, acc):
        b, i = pl.program_id(0), pl.program_id(1)

        @pl.when(i == 0)
        def _():
            dgt_ref[...] = jnp.zeros_like(dgt_ref)

        @pl.when((i == 0) & (b == 0))
        def _():
            acc[...] = jnp.zeros_like(acc)

        ffv, gt = ff_ref[...], gt_ref[...]
        diff = (x1_ref[...] + gt * ffv) - t_ref[...]
        acc[...] += jnp.sum((diff * diff).reshape(ts // 8, 8, d), axis=0)
        dy = diff * (1.0 / d)
        dy_ref[...] = dy
        dgt_ref[...] += jnp.sum(dy * ffv, axis=0, keepdims=True)
        dff_ref[...] = (dy * gt).astype(BF16)

        @pl.when((i == pl.num_programs(1) - 1) & (b == pl.num_programs(0) - 1))
        def _():
            loss_ref[...] = jnp.full(loss_ref.shape, jnp.sum(acc[...]), F32)

    return pl.pallas_call(
        body, name=name, grid=(bsz, s // ts),
        in_specs=[_row_spec(ts, d), _row_spec(ts, d), _vec_spec(d), _row_spec(ts, d)],
        out_specs=[_row_spec(ts, d), _row_spec(ts, d), _vec_spec(d), pl.BlockSpec((8, LANE), lambda b, i: (0, 0))],
        out_shape=[jax.ShapeDtypeStruct((bsz, s, d), F32), jax.ShapeDtypeStruct((bsz, s, d), BF16),
                   jax.ShapeDtypeStruct((bsz, 1, d), F32), jax.ShapeDtypeStruct((8, LANE), F32)],
        scratch_shapes=[pltpu.VMEM((8, d), F32)], compiler_params=_params(2),
    )(x1, ff, gate2, target)


def _merge_fwd(proj, b_merge, y_a, y_b, *, name, ts=512):
    bsz, s, _ = proj.shape
    ts = min(ts, s)

    def body(la_ref, lb_ref, ba_ref, bb_ref, ya_ref, yb_ref, out_ref):
        ga = _sigmoid(la_ref[...] + ba_ref[...])
        gb = _sigmoid(lb_ref[...] + bb_ref[...])
        out_ref[...] = (ga * ya_ref[...] + gb * yb_ref[...]).astype(BF16)

    return pl.pallas_call(
        body, name=name, grid=(bsz, s // ts),
        in_specs=[_row_spec(ts, D, OFF_MA // D), _row_spec(ts, D, OFF_MB // D),
                  pl.BlockSpec((1, D), lambda b, i: (0, 0)), pl.BlockSpec((1, D), lambda b, i: (0, 1)),
                  _row_spec(ts, D), _row_spec(ts, D)],
        out_specs=_row_spec(ts, D), out_shape=jax.ShapeDtypeStruct((bsz, s, D), BF16),
        compiler_params=_params(2),
    )(proj, proj, b_merge, b_merge, y_a, y_b)


def _merge_bwd(dmi, proj, b_merge, y_a, y_b, *, name, ts=512):
    bsz, s, _ = proj.shape
    ts = min(ts, s)

    def body(d_ref, la_ref, lb_ref, ba_ref, bb_ref, ya_ref, yb_ref, dya_ref, dyb_ref, dla_ref, dlb_ref, dba_ref, dbb_ref):
        @pl.when((pl.program_id(0) == 0) & (pl.program_id(1) == 0))
        def _():
            dba_ref[...] = jnp.zeros_like(dba_ref)
            dbb_ref[...] = jnp.zeros_like(dbb_ref)

        dv = d_ref[...].astype(F32)
        ga = _sigmoid(la_ref[...] + ba_ref[...])
        gb = _sigmoid(lb_ref[...] + bb_ref[...])
        dya_ref[...] = (dv * ga).astype(BF16)
        dyb_ref[...] = (dv * gb).astype(BF16)
        dla = (dv * ya_ref[...]) * (ga * (1.0 - ga))
        dlb = (dv * yb_ref[...]) * (gb * (1.0 - gb))
        dla_ref[...] = dla.astype(BF16)
        dlb_ref[...] = dlb.astype(BF16)
        dba_ref[...] += jnp.sum(dla, axis=0, keepdims=True)
        dbb_ref[...] += jnp.sum(dlb, axis=0, keepdims=True)

    act = jax.ShapeDtypeStruct((bsz, s, D), BF16)
    return pl.pallas_call(
        body, name=name, grid=(bsz, s // ts),
        in_specs=[_row_spec(ts, D), _row_spec(ts, D, OFF_MA // D), _row_spec(ts, D, OFF_MB // D),
                  pl.BlockSpec((1, D), lambda b, i: (0, 0)), pl.BlockSpec((1, D), lambda b, i: (0, 1)),
                  _row_spec(ts, D), _row_spec(ts, D)],
        out_specs=[_row_spec(ts, D)] * 4 + [_gain_spec(D)] * 2,
        out_shape=[act, act, act, act, jax.ShapeDtypeStruct((1, D), F32), jax.ShapeDtypeStruct((1, D), F32)],
        compiler_params=_params(2),
    )(dmi, proj, proj, b_merge, b_merge, y_a, y_b)


def _tri(lower):
    r = lax.broadcasted_iota(jnp.int32, (CHUNK, CHUNK), 0)
    c = lax.broadcasted_iota(jnp.int32, (CHUNK, CHUNK), 1)
    return jnp.where((c <= r) if lower else (c >= r), 1.0, 0.0).astype(F32)


def _gla_logits(a_ref, wal_ref, bal_ref):
    logits = jnp.dot(a_ref[...].astype(BF16), wal_ref[...].astype(BF16), preferred_element_type=F32) + bal_ref[...]
    la = (jnp.minimum(logits, 0.0) - jnp.log(1.0 + jnp.exp(-jnp.abs(logits)))) * (1.0 / GTAU)
    return logits, la


def _chunk_cumsum(la_n, tri):
    cum = jnp.dot(tri, la_n, preferred_element_type=F32, precision=lax.Precision.HIGHEST)
    return cum, jnp.sum(la_n, axis=0, keepdims=True)


def _gla_specs(s, nc):
    def blk(width, off):
        return pl.BlockSpec((None, s, width), lambda h, b: (b, 0, off // width + h))

    proj_specs = [blk(GDK, OFF_Q), blk(GDK, OFF_K), blk(GDV, OFF_V), blk(GDV, OFF_G),
                  pl.BlockSpec((None, s, LANE), lambda h, b: (b, 0, OFF_A // LANE)),
                  pl.BlockSpec((LANE, GDK), lambda h, b: (0, h)), pl.BlockSpec((1, GDK), lambda h, b: (0, h)),
                  pl.BlockSpec((1, GDV), lambda h, b: (0, 0))]
    st_spec = pl.BlockSpec((None, None, nc, GDV, GDK), lambda h, b: (b, h, 0, 0, 0))
    return blk, proj_specs, st_spec


def _gla_fwd(proj, w_alpha_p, b_alpha, out_norm_g, *, name):
    bsz, s, _ = proj.shape
    nc = s // CHUNK
    scale = GDK ** -0.5

    rb = min(512, s)

    def body(q_ref, k_ref, v_ref, g_ref, a_ref, wal_ref, bal_ref, ong_ref, o_ref, og_ref, st_ref):
        _, la = _gla_logits(a_ref, wal_ref, bal_ref)
        tri = _tri(True)
        st = jnp.zeros((GDV, GDK), F32)
        for n in range(nc):
            rows = pl.ds(n * CHUNK, CHUNK)
            cum, cum_end = _chunk_cumsum(la[n * CHUNK:(n + 1) * CHUNK], tri)
            kd = k_ref[rows, :] * jnp.exp(cum_end - cum)
            ut = lax.dot_general(v_ref[rows, :].astype(BF16), kd.astype(BF16), _TN, preferred_element_type=F32)
            st = st * jnp.exp(cum_end) + ut
            st_ref[n] = st.astype(BF16)
            o_ref[rows, :] = lax.dot_general((q_ref[rows, :].astype(F32) * scale).astype(BF16), st.astype(BF16), _NT,
                                             preferred_element_type=F32)
        for j in range(0, s, rb):
            blk_rows = pl.ds(j, rb)
            oh, _ = _rms(o_ref[blk_rows, :], None)
            gv = g_ref[blk_rows, :].astype(F32)
            og_ref[blk_rows, :] = ((oh * ong_ref[...]) * (gv * _sigmoid(gv))).astype(BF16)

    blk, proj_specs, st_spec = _gla_specs(s, nc)
    return pl.pallas_call(
        body, name=name, grid=(GH, bsz), in_specs=proj_specs, out_specs=[blk(GDV, 0), blk(GDV, 0), st_spec],
        out_shape=[jax.ShapeDtypeStruct((bsz, s, GH * GDV), F32), jax.ShapeDtypeStruct((bsz, s, GH * GDV), BF16),
                   jax.ShapeDtypeStruct((bsz, GH, nc, GDV, GDK), BF16)],
        compiler_params=_params(2),
    )(proj, proj, proj, proj, proj, w_alpha_p, b_alpha, out_norm_g)


def _gla_bwd(dog, o, states, proj, w_alpha_p, b_alpha, out_norm_g, *, name):
    bsz, s, _ = proj.shape
    nc = s // CHUNK
    scale = GDK ** -0.5

    def body(dog_ref, o_ref, st_ref, q_ref, k_ref, v_ref, g_ref, a_ref, wal_ref, bal_ref, ong_ref,
             dq_ref, dk_ref, dv_ref, dg_ref, dl_ref, dbal_ref, dong_ref, do_scr, dlog_scr):
        h, b = pl.program_id(0), pl.program_id(1)

        @pl.when(b == 0)
        def _():
            dbal_ref[...] = jnp.zeros_like(dbal_ref)

        @pl.when((b == 0) & (h == 0))
        def _():
            dong_ref[...] = jnp.zeros_like(dong_ref)

        ong = ong_ref[...]
        for j in range(0, s, rb):
            blk_rows = pl.ds(j, rb)
            gv, dogv = g_ref[blk_rows, :].astype(F32), dog_ref[blk_rows, :]
            sg = _sigmoid(gv)
            oh, r = _rms(o_ref[blk_rows, :], None)
            don = dogv * (gv * sg)
            dg_ref[blk_rows, :] = (dogv * (oh * ong) * (sg * (1.0 + gv * (1.0 - sg)))).astype(BF16)
            dong_ref[...] += jnp.sum(don * oh, axis=0, keepdims=True)
            doh = don * ong
            do_scr[blk_rows, :] = (r * (doh - oh * jnp.mean(doh * oh, axis=-1, keepdims=True))).astype(BF16)

        logits, la = _gla_logits(a_ref, wal_ref, bal_ref)
        tri_lo, tri_up = _tri(True), _tri(False)
        carry = jnp.zeros((GDV, GDK), F32)
        for n in range(nc - 1, -1, -1):
            rows = pl.ds(n * CHUNK, CHUNK)
            cum, cum_end = _chunk_cumsum(la[n * CHUNK:(n + 1) * CHUNK], tri_lo)
            decay = jnp.exp(cum_end)
            w = jnp.exp(cum_end - cum)
            kd = k_ref[rows, :] * w
            do_b = do_scr[rows, :]
            qs_b = (q_ref[rows, :].astype(F32) * scale).astype(BF16)
            dq_ref[rows, :] = (jnp.dot(do_b, st_ref[n], preferred_element_type=F32) * scale).astype(BF16)
            dsn = lax.dot_general(do_b, qs_b, _TN, preferred_element_type=F32) + carry
            carry = dsn * decay
            dsn_b = dsn.astype(BF16)
            dv_ref[rows, :] = lax.dot_general(kd.astype(BF16), dsn_b, _NT, preferred_element_type=F32).astype(BF16)
            dkd = jnp.dot(v_ref[rows, :].astype(BF16), dsn_b, preferred_element_type=F32)
            dk_ref[rows, :] = (dkd * w).astype(BF16)
            e = dkd * kd
            dcum_end = jnp.sum(e, axis=0, keepdims=True)
            if n > 0:
                dcum_end += jnp.sum(dsn * st_ref[n - 1].astype(F32), axis=0, keepdims=True) * decay
            dlog_scr[rows, :] = dcum_end - jnp.dot(tri_up, e, preferred_element_type=F32,
                                                  precision=lax.Precision.HIGHEST)
        dlog = dlog_scr[...] * (1.0 / GTAU) * (1.0 - _sigmoid(logits))
        dl_ref[...] = dlog.astype(BF16)
        dbal_ref[...] += jnp.sum(dlog, axis=0, keepdims=True)

    rb = min(512, s)

    blk, proj_specs, st_spec = _gla_specs(s, nc)
    act = lambda wd: jax.ShapeDtypeStruct((bsz, s, wd), BF16)
    return pl.pallas_call(
        body, name=name, grid=(GH, bsz), in_specs=[blk(GDV, 0), blk(GDV, 0), st_spec, *proj_specs],
        out_specs=[blk(GDK, 0), blk(GDK, 0), blk(GDV, 0), blk(GDV, 0), blk(GDK, 0),
                   pl.BlockSpec((1, GDK), lambda h, b: (0, h)), pl.BlockSpec((1, GDV), lambda h, b: (0, 0))],
        out_shape=[act(GH * GDK), act(GH * GDK), act(GH * GDV), act(GH * GDV), act(GH * GDK),
                   jax.ShapeDtypeStruct((1, GH * GDK), F32), jax.ShapeDtypeStruct((1, GDV), F32)],
        scratch_shapes=[pltpu.VMEM((s, GDV), BF16), pltpu.VMEM((s, GDK), F32)], compiler_params=_params(2),
    )(dog, o, states, proj, proj, proj, proj, proj, w_alpha_p, b_alpha, out_norm_g)


def _lane():
    return lax.broadcasted_iota(jnp.int32, (1, LANE), 1)


def _swap_halves(x):
    lane = _lane()
    half = MROPE // 2
    lo = (lane >= MNOPE) & (lane < MNOPE + half)
    hi = (lane >= MNOPE + half) & (lane < MQK)
    return jnp.where(lo, pltpu.roll(x, LANE - half, 1), jnp.where(hi, pltpu.roll(x, half, 1), 0.0))


def _norm96(x, g):
    r = lax.rsqrt(jnp.sum(x * x, axis=-1, keepdims=True) * (1.0 / MQK) + EPS)
    return x * r, r


def _lat_norm(proj, q_lat_g, kv_lat_g, *, name, ts=512):
    t = proj.shape[0]
    ts = min(ts, t)

    def body(cq_ref, ckv_ref, gq_ref, gk_ref, oq_ref, ok_ref):
        xq, _ = _rms(cq_ref[...].astype(F32), None)
        oq_ref[...] = (xq * gq_ref[...]).astype(BF16)
        xk, _ = _rms(ckv_ref[...].astype(F32), None)
        ok_ref[...] = (xk * gk_ref[...]).astype(BF16)

    return pl.pallas_call(
        body, name=name, grid=(t // ts,),
        in_specs=[pl.BlockSpec((ts, MQR), lambda i: (i, OFF_CQ // MQR)), pl.BlockSpec((ts, MKVR), lambda i: (i, OFF_CKV // MKVR)),
                  pl.BlockSpec((1, MQR), lambda i: (0, 0)), pl.BlockSpec((1, MKVR), lambda i: (0, 0))],
        out_specs=[pl.BlockSpec((ts, MQR), lambda i: (i, 0)), pl.BlockSpec((ts, MKVR), lambda i: (i, 0))],
        out_shape=[jax.ShapeDtypeStruct((t, MQR), BF16), jax.ShapeDtypeStruct((t, MKVR), BF16)],
        compiler_params=_params(1),
    )(proj, proj, q_lat_g, kv_lat_g)


def _lat_norm_bwd(dcqn, dckvn, proj, q_lat_g, kv_lat_g, *, name, ts=512):
    t = proj.shape[0]
    ts = min(ts, t)

    def one(d_ref, x_ref, g_ref, dx_ref, dg_ref):
        xh, r = _rms(x_ref[...].astype(F32), None)
        dn = d_ref[...]
        dg_ref[...] += jnp.sum(dn * xh, axis=0, keepdims=True)
        dxh = dn * g_ref[...]
        dx_ref[...] = (r * (dxh - xh * jnp.mean(dxh * xh, axis=-1, keepdims=True))).astype(BF16)

    def body(dq_ref, dk_ref, cq_ref, ckv_ref, gq_ref, gk_ref, dxq_ref, dxk_ref, dgq_ref, dgk_ref):
        @pl.when(pl.program_id(0) == 0)
        def _():
            dgq_ref[...] = jnp.zeros_like(dgq_ref)
            dgk_ref[...] = jnp.zeros_like(dgk_ref)

        one(dq_ref, cq_ref, gq_ref, dxq_ref, dgq_ref)
        one(dk_ref, ckv_ref, gk_ref, dxk_ref, dgk_ref)

    return pl.pallas_call(
        body, name=name, grid=(t // ts,),
        in_specs=[pl.BlockSpec((ts, MQR), lambda i: (i, 0)), pl.BlockSpec((ts, MKVR), lambda i: (i, 0)),
                  pl.BlockSpec((ts, MQR), lambda i: (i, OFF_CQ // MQR)), pl.BlockSpec((ts, MKVR), lambda i: (i, OFF_CKV // MKVR)),
                  pl.BlockSpec((1, MQR), lambda i: (0, 0)), pl.BlockSpec((1, MKVR), lambda i: (0, 0))],
        out_specs=[pl.BlockSpec((ts, MQR), lambda i: (i, 0)), pl.BlockSpec((ts, MKVR), lambda i: (i, 0)),
                   pl.BlockSpec((1, MQR), lambda i: (0, 0)), pl.BlockSpec((1, MKVR), lambda i: (0, 0))],
        out_shape=[jax.ShapeDtypeStruct((t, MQR), BF16), jax.ShapeDtypeStruct((t, MKVR), BF16),
                   jax.ShapeDtypeStruct((1, MQR), F32), jax.ShapeDtypeStruct((1, MKVR), F32)],
        compiler_params=_params(1),
    )(dcqn, dckvn, proj, proj, q_lat_g, kv_lat_g)


def _qk_prep(q_raw, kv, proj, cos_t, sin_t, gq, gk, *, name, ts=2048):
    t = q_raw.shape[0]
    ts = min(ts, t)

    def body(q_ref, kv_ref, kpe_ref, c_ref, s_ref, gq_ref, gk_ref, qo_ref, ko_ref, vo_ref):
        cs, sn = c_ref[...], s_ref[...]
        nope = _lane() < MNOPE
        qn, _ = _norm96(q_ref[...].astype(F32), None)
        qn = qn * gq_ref[...]
        qo_ref[...] = (qn * cs + _swap_halves(qn) * sn).astype(BF16)
        kvv = kv_ref[...].astype(F32)
        kn, _ = _norm96(jnp.where(nope, kvv, kpe_ref[...].astype(F32)), None)
        kn = kn * gk_ref[...]
        ko_ref[...] = (kn * cs + _swap_halves(kn) * sn).astype(BF16)
        vo_ref[...] = jnp.where(nope, pltpu.roll(kvv, MNOPE, 1), 0.0).astype(BF16)

    hd = pl.BlockSpec((ts, LANE), lambda i, h: (i, h))
    shared = lambda col: pl.BlockSpec((ts, LANE), lambda i, h: (i, col))
    gain = pl.BlockSpec((1, LANE), lambda i, h: (0, 0))
    out = jax.ShapeDtypeStruct((t, MH * LANE), BF16)
    return pl.pallas_call(
        body, name=name, grid=(t // ts, MH),
        in_specs=[hd, hd, shared(OFF_KPE // LANE), shared(0), shared(0), gain, gain],
        out_specs=[hd, hd, hd], out_shape=[out, out, out], compiler_params=_params(2),
    )(q_raw, kv, proj, cos_t, sin_t, gq, gk)


def _qk_prep_bwd(dq, dk, dv, q_raw, kv, proj, cos_t, sin_t, gq, gk, *, name, ts=2048):
    t = q_raw.shape[0]
    ts = min(ts, t)

    def norm_bwd(dy, x, g, dg_ref):
        xh, r = _norm96(x, None)
        dg_ref[...] += jnp.sum(dy * xh, axis=0, keepdims=True)
        dxh = dy * g
        return r * (dxh - xh * (jnp.sum(dxh * xh, axis=-1, keepdims=True) * (1.0 / MQK)))

    def body(dq_ref, dk_ref, dv_ref, q_ref, kv_ref, kpe_ref, c_ref, s_ref, gq_ref, gk_ref,
             dqr_ref, dkv_ref, dkpe_ref, dgq_ref, dgk_ref):
        i, h = pl.program_id(0), pl.program_id(1)

        @pl.when(h == 0)
        def _():
            dkpe_ref[...] = jnp.zeros_like(dkpe_ref)

        @pl.when((h == 0) & (i == 0))
        def _():
            dgq_ref[...] = jnp.zeros_like(dgq_ref)
            dgk_ref[...] = jnp.zeros_like(dgk_ref)

        cs, sn = c_ref[...], s_ref[...]
        lane = _lane()
        nope = lane < MNOPE
        dqv = dq_ref[...]
        dqn = dqv * cs + _swap_halves(dqv * sn)
        dqr_ref[...] = norm_bwd(dqn, q_ref[...].astype(F32), gq_ref[...], dgq_ref).astype(BF16)
        dkv_ = dk_ref[...]
        dkn = dkv_ * cs + _swap_halves(dkv_ * sn)
        kvv = kv_ref[...].astype(F32)
        dkr = norm_bwd(dkn, jnp.where(nope, kvv, kpe_ref[...].astype(F32)), gk_ref[...], dgk_ref)
        dkv_ref[...] = jnp.where(nope, dkr, pltpu.roll(dv_ref[...].astype(F32), MNOPE, 1)).astype(BF16)
        dkpe_ref[...] += jnp.where((lane >= MNOPE) & (lane < MQK), dkr, 0.0)

    hd = pl.BlockSpec((ts, LANE), lambda i, h: (i, h))
    shared = lambda col: pl.BlockSpec((ts, LANE), lambda i, h: (i, col))
    gain = pl.BlockSpec((1, LANE), lambda i, h: (0, 0))
    out = jax.ShapeDtypeStruct((t, MH * LANE), BF16)
    return pl.pallas_call(
        body, name=name, grid=(t // ts, MH),
        in_specs=[hd, hd, hd, hd, hd, shared(OFF_KPE // LANE), shared(0), shared(0), gain, gain],
        out_specs=[hd, hd, shared(0), gain, gain],
        out_shape=[out, out, jax.ShapeDtypeStruct((t, LANE), F32), jax.ShapeDtypeStruct((1, LANE), F32),
                   jax.ShapeDtypeStruct((1, LANE), F32)],
        compiler_params=_params(2),
    )(dq, dk, dv, q_raw, kv, proj, cos_t, sin_t, gq, gk)


_NT = (((1,), (1,)), ((), ()))
_TN = (((0,), (0,)), ((), ()))


SOFTMAX_SCALE = MQK ** -0.5
Q_PRESCALE = SOFTMAX_SCALE * float(np.log2(np.e))


def _attn_weights(q, k_ref, lo, tq):
    row = lax.broadcasted_iota(jnp.int32, (tq, tq), 0) // CHUNK
    col = lax.broadcasted_iota(jnp.int32, (tq, tq), 1) // CHUNK
    sd = lax.dot_general(q, k_ref[pl.ds(lo, tq), :], _NT, preferred_element_type=F32)
    sd = jnp.where(col <= row, sd, -1e30)
    m = jnp.max(sd, axis=-1, keepdims=True)
    if lo:
        so = lax.dot_general(q, k_ref[pl.ds(0, lo), :], _NT, preferred_element_type=F32)
        m = jnp.maximum(m, jnp.max(so, axis=-1, keepdims=True))
        eo = jnp.exp2(so - m)
        ed = jnp.exp2(sd - m)
        return eo, ed, 1.0 / (jnp.sum(eo, axis=-1, keepdims=True) + jnp.sum(ed, axis=-1, keepdims=True))
    ed = jnp.exp2(sd - m)
    return None, ed, 1.0 / jnp.sum(ed, axis=-1, keepdims=True)


def _attn_fwd(q, k, v, *, name, tq=256):
    bsz, s, _ = q.shape
    tq = min(tq, s)

    def body(q_ref, k_ref, v_ref, o_ref):
        for i in range(s // tq):
            lo = i * tq
            eo, ed, inv = _attn_weights(q_ref[pl.ds(lo, tq), :], k_ref, lo, tq)
            o = jnp.dot(ed.astype(BF16), v_ref[pl.ds(lo, tq), :], preferred_element_type=F32)
            if lo:
                o += jnp.dot(eo.astype(BF16), v_ref[pl.ds(0, lo), :], preferred_element_type=F32)
            o_ref[pl.ds(lo, tq), :] = (o * inv).astype(BF16)

    spec = pl.BlockSpec((None, s, LANE), lambda b, h: (b, 0, h))
    return pl.pallas_call(
        body, name=name, grid=(bsz, MH), in_specs=[spec, spec, spec], out_specs=spec,
        out_shape=jax.ShapeDtypeStruct((bsz, s, MH * LANE), BF16), compiler_params=_params(2),
    )(q, k, v)


def _attn_bwd(q, k, v, do, *, name, tq=256):
    bsz, s, _ = q.shape
    tq = min(tq, s)

    def body(q_ref, k_ref, v_ref, do_ref, dq_ref, dk_ref, dv_out_ref, dv_ref):
        dk_ref[...] = jnp.zeros_like(dk_ref)
        dv_ref[...] = jnp.zeros_like(dv_ref)
        for i in range(s // tq):
            lo = i * tq
            here, before = pl.ds(lo, tq), pl.ds(0, lo)
            qv, dov = q_ref[here, :], do_ref[here, :]
            eo, ed, inv = _attn_weights(qv, k_ref, lo, tq)
            do_n = (dov.astype(F32) * inv).astype(BF16)
            dv_ref[here, :] += lax.dot_general(ed.astype(BF16), do_n, _TN, preferred_element_type=F32)
            dpd = lax.dot_general(dov, v_ref[here, :], _NT, preferred_element_type=F32)
            delta = jnp.sum(dpd * ed, axis=-1, keepdims=True)
            if lo:
                dv_ref[before, :] += lax.dot_general(eo.astype(BF16), do_n, _TN, preferred_element_type=F32)
                dpo = lax.dot_general(dov, v_ref[before, :], _NT, preferred_element_type=F32)
                delta += jnp.sum(dpo * eo, axis=-1, keepdims=True)
            delta = delta * inv
            r = inv * SOFTMAX_SCALE
            dsd = (ed * (dpd - delta) * r).astype(BF16)
            dq = jnp.dot(dsd, k_ref[here, :], preferred_element_type=F32)
            dk_ref[here, :] += lax.dot_general(dsd, qv, _TN, preferred_element_type=F32)
            if lo:
                dso = (eo * (dpo - delta) * r).astype(BF16)
                dq += jnp.dot(dso, k_ref[before, :], preferred_element_type=F32)
                dk_ref[before, :] += lax.dot_general(dso, qv, _TN, preferred_element_type=F32)
            dq_ref[here, :] = dq
        dk_ref[...] = dk_ref[...] * (1.0 / Q_PRESCALE)
        dv_out_ref[...] = dv_ref[...].astype(BF16)

    spec = pl.BlockSpec((None, s, LANE), lambda b, h: (b, 0, h))
    out = jax.ShapeDtypeStruct((bsz, s, MH * LANE), F32)
    return pl.pallas_call(
        body, name=name, grid=(bsz, MH), in_specs=[spec] * 4, out_specs=[spec] * 3,
        out_shape=[out, out, jax.ShapeDtypeStruct((bsz, s, MH * LANE), BF16)],
        scratch_shapes=[pltpu.VMEM((s, LANE), F32)], compiler_params=_params(2),
    )(q, k, v, do)


def _adamw(w, g, m, v, *, name, tr=256, by_cols=False):
    rows, cols = w.shape
    tr = _tile_rows(rows, tr)

    def body(w_ref, g_ref, m_ref, v_ref, d_ref, nm_ref, nv_ref):
        d_ref[...], nm_ref[...], nv_ref[...] = _adamw_update(w_ref[...], g_ref[...], m_ref[...], v_ref[...])

    spec = pl.BlockSpec((rows, LANE), lambda i: (0, i)) if by_cols else pl.BlockSpec((tr, cols), lambda i: (i, 0))
    out = jax.ShapeDtypeStruct((rows, cols), F32)
    return pl.pallas_call(body, name=name, grid=(cols // LANE if by_cols else rows // tr,), in_specs=[spec] * 4,
                          out_specs=[spec] * 3, out_shape=[out, out, out], compiler_params=_params(1))(w, g, m, v)


def _tile_rows(rows, target):
    if rows <= target:
        return rows
    best = 8
    for t in range(8, target + 1, 8):
        if rows % t == 0:
            best = t
    return best


def _adamw_update(w, g, m, v):
    nm = ADAM_B1 * m + (1.0 - ADAM_B1) * g
    nv = ADAM_B2 * v + (1.0 - ADAM_B2) * (g * g)
    m_hat = nm / (1.0 - ADAM_B1 ** ADAM_STEP)
    v_hat = nv / (1.0 - ADAM_B2 ** ADAM_STEP)
    return -ADAM_LR * (m_hat / (jnp.sqrt(v_hat) + ADAM_EPS) + ADAM_WD * w), nm, nv


def _adamw_halves(w, m, v, mine, theirs, sel, *, name, tr=256):
    rows, cols = w.shape
    tr = _tile_rows(rows // 2, tr)
    nh = rows // 2 // tr

    def body(sel_ref, w_ref, m_ref, v_ref, mine_ref, theirs_ref, g_ref, d_ref, nm_ref, nv_ref):
        lower = pl.program_id(0) < nh
        south = sel_ref[0] == 0
        gv = jnp.where(lower == south, mine_ref[...], theirs_ref[...])
        g_ref[...] = gv
        d_ref[...], nm_ref[...], nv_ref[...] = _adamw_update(w_ref[...], gv, m_ref[...], v_ref[...])

    full = pl.BlockSpec((tr, cols), lambda i, sel_ref: (i, 0))
    half = pl.BlockSpec((tr, cols), lambda i, sel_ref: (i % nh, 0))
    out = jax.ShapeDtypeStruct((rows, cols), F32)
    return pl.pallas_call(
        body, name=name, out_shape=[out] * 4, compiler_params=_params(1),
        grid_spec=pltpu.PrefetchScalarGridSpec(num_scalar_prefetch=1, grid=(rows // tr,),
                                               in_specs=[full, full, full, half, half], out_specs=[full] * 4),
    )(sel, w, m, v, mine, theirs)


def _pair_add(x, sib, sel, *, name, tr=256):
    n, _, rows, cols = x.shape
    tr = _tile_rows(rows, tr)

    def body(sel_ref, x_ref, s_ref, o_ref):
        o_ref[...] = (x_ref[...] + s_ref[...]).astype(BF16)

    spec = pl.BlockSpec((None, tr, cols), lambda j, i, sel_ref: (j, i, 0))
    return pl.pallas_call(
        body, name=name, out_shape=jax.ShapeDtypeStruct((n, rows, cols), BF16), compiler_params=_params(2),
        grid_spec=pltpu.PrefetchScalarGridSpec(
            num_scalar_prefetch=1, grid=(n, rows // tr),
            in_specs=[pl.BlockSpec((None, None, tr, cols), lambda j, i, sel_ref: (j, sel_ref[0], i, 0)), spec],
            out_specs=spec),
    )(sel, x, sib)


def _chip_sum(pair, recv, sel, *, name, tr=256):
    _, rows, cols = pair.shape
    tr = _tile_rows(rows, tr)

    def body(sel_ref, p_ref, r_ref, o_ref):
        acc = p_ref[...].astype(F32)
        for k in range(3):
            acc = acc + r_ref[k].astype(F32)
        o_ref[...] = acc

    return pl.pallas_call(
        body, name=name, out_shape=jax.ShapeDtypeStruct((rows, cols), F32), compiler_params=_params(1),
        grid_spec=pltpu.PrefetchScalarGridSpec(
            num_scalar_prefetch=1, grid=(rows // tr,),
            in_specs=[pl.BlockSpec((None, tr, cols), lambda i, sel_ref: (sel_ref[0], i, 0)),
                      pl.BlockSpec((3, tr, cols), lambda i, sel_ref: (0, i, 0))],
            out_specs=pl.BlockSpec((tr, cols), lambda i, sel_ref: (i, 0))),
    )(sel, pair, recv)


def _me():
    return lax.axis_index("x"), lax.axis_index("y"), lax.axis_index("c")


def _flip(pos, bits):
    x, y, c = pos
    return (x ^ bits[0] if bits[0] else x, y ^ bits[1] if bits[1] else y, c ^ bits[2] if bits[2] else c)


ANY = pl.BlockSpec(memory_space=pl.ANY)


def _all_gather8(xs, *, name):
    n = len(xs)
    flips = [((k >> 2) & 1, (k >> 1) & 1, k & 1) for k in range(1, 8)]

    def body(*refs):
        x_refs, out_refs, (send_sems, recv_sems, local_sems) = refs[:n], refs[n:2 * n], refs[2 * n:]
        me = _me()
        slot = lambda p: 4 * p[0] + 2 * p[1] + p[2]
        copies = []
        for i in range(n):
            mine = pltpu.make_async_copy(x_refs[i], out_refs[i].at[slot(me)], local_sems.at[i])
            mine.start()
            copies.append(mine)
            for k, f in enumerate(flips):
                peer = _flip(me, f)
                sems = dict(send_sem=send_sems.at[7 * i + k], recv_sem=recv_sems.at[7 * i + k], device_id=peer,
                            device_id_type=MESH)
                cp = pltpu.make_async_remote_copy(src_ref=x_refs[i], dst_ref=out_refs[i].at[slot(me)], **sems)
                cp.start()
                copies.append(cp)
                copies.append(pltpu.make_async_remote_copy(src_ref=x_refs[i], dst_ref=out_refs[i].at[slot(peer)], **sems))
        for i in range(n):
            base = i * 15
            copies[base].wait()
            for k in range(7):
                copies[base + 1 + 2 * k].wait_send()
                copies[base + 2 + 2 * k].wait_recv()

    outs = pl.pallas_call(
        body, name=name, in_specs=[ANY] * n, out_specs=[ANY] * n,
        out_shape=[jax.ShapeDtypeStruct((8, *x.shape), x.dtype) for x in xs],
        scratch_shapes=[pltpu.SemaphoreType.DMA((7 * n,)), pltpu.SemaphoreType.DMA((7 * n,)),
                        pltpu.SemaphoreType.DMA((n,))])(*xs)
    return list(outs)


CHIP_FLIPS = [(1, 0, 0), (0, 1, 0), (1, 1, 0)]


def _chip():
    return 2 * lax.axis_index("x") + lax.axis_index("y")


HBM = pl.BlockSpec(memory_space=pltpu.HBM)
SEM = pl.BlockSpec(memory_space=pltpu.SEMAPHORE)
EFFECT = pltpu.SideEffectType.DATAFLOW_SIDE_EFFECTING


def _plan_copies(plan, refs, send_sems, recv_sems):
    return [pltpu.make_async_remote_copy(src_ref=src, dst_ref=dst, send_sem=send_sems.at[k], recv_sem=recv_sems.at[k],
                                         device_id=to, device_id_type=MESH) for k, (src, dst, to) in enumerate(plan(refs))]


def _rdma_start(arrays, n_copies, plan, deps, *, name):
    n, nd = len(arrays), len(deps)

    def body(*refs):
        for cp in _plan_copies(plan, refs[:n], refs[n + nd], refs[n + nd + 1]):
            cp.start()
        refs[-1][...] = jnp.zeros_like(refs[-1])

    outs = pl.pallas_call(
        body, name=name,
        out_shape=(pltpu.SemaphoreType.DMA((n_copies,)), pltpu.SemaphoreType.DMA((n_copies,)),
                   *[pltpu.HBM(a.shape, a.dtype) for a in arrays], jax.ShapeDtypeStruct((8, LANE), F32)),
        in_specs=[HBM] * n + [ANY] * nd, out_specs=(SEM, SEM, *[HBM] * n, pl.BlockSpec(memory_space=pltpu.VMEM)),
        input_output_aliases={i: i + 2 for i in range(n)}, compiler_params=pltpu.CompilerParams(has_side_effects=EFFECT),
    )(*[pltpu.with_memory_space_constraint(a, pltpu.HBM) for a in arrays], *deps)
    return outs[0], outs[1], list(outs[2:2 + n]), outs[-1]


def _rdma_wait(send_sems, recv_sems, arrays, plan, after, *, name):
    n = len(arrays)

    def body(*refs):
        for cp in _plan_copies(plan, refs[:n], refs[n], refs[n + 1]):
            cp.wait_send()
            cp.wait_recv()

    return list(pl.pallas_call(
        body, name=name, out_shape=tuple(pltpu.HBM(a.shape, a.dtype) for a in arrays),
        in_specs=[HBM] * n + [SEM, SEM, ANY], out_specs=tuple([HBM] * n), input_output_aliases={i: i for i in range(n)},
        compiler_params=pltpu.CompilerParams(has_side_effects=EFFECT),
    )(*arrays, send_sems, recv_sems, after))


def _gather_plan(n):
    def plan(refs):
        me = _me()
        slot = 2 * me[0] + me[1]
        return [(refs[i].at[me[2]], refs[n + i].at[slot, me[2]], _flip(me, f)) for i in range(n) for f in CHIP_FLIPS]
    return plan


def _scatter_plan(n):
    def plan(refs):
        me = _me()
        out = []
        for i in range(n):
            for k, f in enumerate(CHIP_FLIPS):
                peer = _flip(me, f)
                out.append((refs[i].at[2 * peer[0] + peer[1]], refs[n + i].at[k], peer))
        return out
    return plan


def _sibling_plan(n, src_of):
    def plan(refs):
        me = _me()
        return [(src_of(refs[i], me[2]), refs[n + i], _flip(me, (0, 0, 1))) for i in range(n)]
    return plan


def _gather8_plan(n):
    def plan(refs):
        me = _me()
        slot = 4 * me[0] + 2 * me[1] + me[2]
        return [(refs[i], refs[n + i].at[slot], _flip(me, ((k >> 2) & 1, (k >> 1) & 1, k & 1)))
                for i in range(n) for k in range(1, 8)]
    return plan


def _pair_fill(lands, *, name):
    n = len(lands)

    def body(*refs):
        in_refs, (send_sems, recv_sems) = refs[:n], refs[2 * n:]
        me = _me()
        sib = _flip(me, (0, 0, 1))
        copies = []
        for i in range(n):
            for k, f in enumerate(CHIP_FLIPS):
                peer = _flip(me, f)
                slot = 2 * peer[0] + peer[1]
                mine, theirs = in_refs[i].at[slot, me[2]], in_refs[i].at[slot, 1 - me[2]]
                cp = pltpu.make_async_remote_copy(src_ref=mine, dst_ref=mine, send_sem=send_sems.at[3 * i + k],
                                                  recv_sem=recv_sems.at[3 * i + k], device_id=sib, device_id_type=MESH)
                cp.start()
                copies.append((cp, pltpu.make_async_remote_copy(
                    src_ref=mine, dst_ref=theirs, send_sem=send_sems.at[3 * i + k], recv_sem=recv_sems.at[3 * i + k],
                    device_id=sib, device_id_type=MESH)))
        for cp, arrival in copies:
            arrival.wait_recv()
            cp.wait_send()

    return list(pl.pallas_call(
        body, name=name, in_specs=[ANY] * n, out_specs=[ANY] * n,
        out_shape=[jax.ShapeDtypeStruct(a.shape, a.dtype) for a in lands], input_output_aliases={i: i for i in range(n)},
        scratch_shapes=[pltpu.SemaphoreType.DMA((3 * n,)), pltpu.SemaphoreType.DMA((3 * n,))])(*lands))


def _own_and_landed(lands, xs):
    chip = _chip()
    return [[jnp.where(chip == j, x, o.reshape(4, *x.shape)[j]) for j in range(4)] for o, x in zip(lands, xs)]


BIG = (("w_in", (D, IN_WIDTH // 4), 1), ("gla_w_o", (D // 4, D), 0), ("mla_w_uq", (MQR, MH * MQK // 4), 1),
       ("mla_w_ukv", (MKVR, MH * (MNOPE + MVD) // 4), 1), ("mla_w_o", (D // 4, D), 0), ("w_out", (D // 4, D), 0),
       ("mlp_w1", (D, DFF // 4), 1), ("mlp_w2", (DFF // 4, D), 0))
ADA_SHARD = (D, 6 * D // 4)
SMALL = (("b_ada", 6 * D), ("norm1_g", D), ("b_merge", 2 * D), ("gla_b_alpha", GH * GDK), ("gla_out_norm_g", GDV),
         ("mla_q_lat_g", MQR), ("mla_kv_lat_g", MKVR), ("mla_qn_g", MQK), ("mla_kn_g", MQK), ("norm2_g", D))


W_IN_SEGMENTS = ((0, 3072, OFF_Q), (3072, 3088, OFF_A), (3088, 3344, OFF_CQ), (3344, 3472, OFF_CKV),
                 (3472, 3504, OFF_KPE + MNOPE), (3504, 5552, OFF_MA))
W_IN_SPLIT = OFF_MA
SMALL_ROWS, SMALL_COLS = 32, 2 * D
W_ALPHA_ROW = 16
LOSS_ROW = 15
SMALL_RED = tuple((n, k) for n, k in SMALL if n != "b_ada")


def _pack_small(grads, d_w_alpha, loss_row, *, name):
    def body(*refs):
        g_refs, wa_ref, loss_ref, out_ref = refs[:-3], refs[-3], refs[-2], refs[-1]
        out_ref[...] = jnp.zeros_like(out_ref)
        for i, ((_, k), g_ref) in enumerate(zip(SMALL_RED, g_refs)):
            out_ref[i:i + 1, 0:k] = g_ref[...]
        out_ref[LOSS_ROW:LOSS_ROW + 1, 0:LANE] = loss_ref[...]
        out_ref[W_ALPHA_ROW:W_ALPHA_ROW + GLR, 0:GH * GDK] = wa_ref[...]

    return pl.pallas_call(body, name=name, out_shape=jax.ShapeDtypeStruct((SMALL_ROWS, SMALL_COLS), F32))(
        *grads, d_w_alpha, loss_row)


def _small_update(gathered, dmod_all, sel, wmv, *, name):
    names = [n for n, _ in SMALL] + ["gla_w_alpha"]
    n_par = len(names)

    def body(sel_ref, g_ref, dmod_ref, *refs):
        in_refs, out_refs, loss_ref, acc = refs[:3 * n_par], refs[3 * n_par:-2], refs[-2], refs[-1]
        total = g_ref[0]
        for j in range(1, 8):
            total = total + g_ref[j]
        acc[...] = total
        loss_ref[...] = acc[LOSS_ROW:LOSS_ROW + 1, 0:LANE]
        row = {n: i for i, (n, _) in enumerate(SMALL_RED)}
        for p, name_p in enumerate(names):
            w_ref, m_ref, v_ref = in_refs[3 * p:3 * p + 3]
            if name_p == "b_ada":
                gv = jnp.sum(dmod_ref[...], axis=0, keepdims=True)
            elif name_p == "gla_w_alpha":
                gv = jnp.zeros((GLR, GDK), F32)
                for j in range(4):
                    blk = acc[W_ALPHA_ROW:W_ALPHA_ROW + GLR, j * GDK:(j + 1) * GDK]
                    gv = gv + jnp.where(sel_ref[0] == j, blk, 0.0)
            else:
                gv = acc[row[name_p]:row[name_p] + 1, 0:w_ref.shape[1]]
            o = out_refs[4 * p:4 * p + 4]
            o[0][...] = gv
            o[1][...], o[2][...], o[3][...] = _adamw_update(w_ref[...], gv, m_ref[...], v_ref[...])

    flat = [a for t in wmv for a in t]
    out_shape = [jax.ShapeDtypeStruct(t[0].shape, F32) for t in wmv for _ in range(4)]
    out_shape.append(jax.ShapeDtypeStruct((1, LANE), F32))
    vmem = pl.BlockSpec(memory_space=pltpu.VMEM)
    outs = pl.pallas_call(
        body, name=name, out_shape=out_shape, in_specs=[pl.BlockSpec(memory_space=pltpu.SMEM), vmem, vmem] + [vmem] * len(flat),
        out_specs=[vmem] * len(out_shape), scratch_shapes=[pltpu.VMEM((SMALL_ROWS, SMALL_COLS), F32)],
    )(sel, gathered, dmod_all, *flat)
    return {n: tuple(outs[4 * p:4 * p + 4]) for p, n in enumerate(names)}, outs[-1][0, 0]


def _full_weights(gathered):
    w = {name: jnp.concatenate(gathered[name], axis=axis) for name, _, axis in BIG if name in gathered and name != "w_in"}
    if "w_in" in gathered:
        shards = gathered["w_in"]
        zeros = lambda n: [jnp.zeros((D, n), shards[0].dtype)]

        def cols(a, b):
            width = IN_WIDTH // 4
            return [shards[j][:, max(a, j * width) - j * width:min(b, (j + 1) * width) - j * width]
                    for j in range(4) if max(a, j * width) < min(b, (j + 1) * width)]

        parts = []
        for a, b, at in sorted(W_IN_SEGMENTS, key=lambda seg: seg[2]):
            have = sum(p.shape[1] for p in parts)
            parts += (zeros(at - have) if at > have else []) + cols(a, b)
        w["w_in"] = jnp.concatenate(parts + zeros(PW - sum(p.shape[1] for p in parts)), axis=1)
    if "mla_w_uq" in w:
        w["mla_w_uq"] = jnp.pad(w["mla_w_uq"].reshape(MQR, MH, MQK), ((0, 0), (0, 0), (0, LANE - MQK))).reshape(MQR, MH * LANE)
    if "mla_w_o" in w:
        w["mla_w_o"] = jnp.pad(w["mla_w_o"].reshape(MH, MVD, D), ((0, 0), (0, LANE - MVD), (0, 0))).reshape(MH * LANE, D)
    return w


def _grad_slots(g):
    g = dict(g)
    out = {}
    if "w_in" in g:
        g_lo, g_hi = g.pop("w_in")
        take = lambda at, lo, hi: g_lo[:, at + lo:at + hi] if at < W_IN_SPLIT else g_hi[:, at - W_IN_SPLIT + lo:at - W_IN_SPLIT + hi]
        width = IN_WIDTH // 4
        slots = []
        for j in range(4):
            lo, hi = j * width, (j + 1) * width
            slots.append(jnp.concatenate([take(at, max(lo, a) - a, min(hi, b) - a)
                                          for a, b, at in W_IN_SEGMENTS if max(lo, a) < min(hi, b)], axis=1))
        out["w_in"] = jnp.stack(slots).reshape(4, 2, D // 2, width)
    if "mla_w_uq" in g:
        g["mla_w_uq"] = g["mla_w_uq"].reshape(MQR, MH, LANE)[:, :, :MQK].reshape(MQR, MH * MQK)
    if "mla_w_o" in g:
        g["mla_w_o"] = g["mla_w_o"].reshape(MH, LANE, D)[:, :MVD].reshape(MH * MVD, D)
    for name, (rows, cols), axis in BIG:
        if name not in g:
            continue
        a = g[name]
        a = a.reshape(4, rows, cols) if axis == 0 else jnp.transpose(a.reshape(rows, 4, cols), (1, 0, 2))
        out[name] = a.reshape(4, 2, rows // 2, cols)
    return out


def _rope_tables(positions):
    freqs = ROPE_THETA ** (-jnp.arange(0, MROPE, 2, dtype=F32) / MROPE)
    lane = np.arange(LANE)
    in_rope = (lane >= MNOPE) & (lane < MQK)
    freq_lane = jnp.where(in_rope, freqs[(lane - MNOPE) % (MROPE // 2)], 0.0)
    sign = np.where(in_rope, np.where(lane < MNOPE + MROPE // 2, -1.0, 1.0), 0.0).astype(np.float32)
    ang = positions.astype(F32).reshape(-1, 1) * freq_lane[None, :]
    return jnp.cos(ang), jnp.sin(ang) * sign[None, :]


def _local_step(x, positions, mod, target, w, small, more_weights=None, on_grads=None):
    kept = {}
    if on_grads is None:
        on_grads = lambda tag, grads, after: kept.update(grads)
    bsz, s, _ = x.shape
    t = bsz * s
    tt = _tile(t, 1024)
    shift1, scale1, gate1, shift2, scale2, gate2 = [mod[:, None, i * D:(i + 1) * D] for i in range(6)]
    cos_t, sin_t = _rope_tables(positions)
    w_alpha_p = jnp.pad(small["gla_w_alpha"], ((0, LANE - GLR), (0, 0)))
    gq = jnp.pad(small["mla_qn_g"], ((0, 0), (0, LANE - MQK)))
    gk = jnp.pad(small["mla_kn_g"], ((0, 0), (0, LANE - MQK)))
    flat2 = lambda a: a.reshape(t, a.shape[-1])
    bsd = lambda a: a.reshape(bsz, s, a.shape[-1])

    h = _norm_mod(x, small["norm1_g"], scale1, shift1, name="norm1")
    if callable(w):
        w = w(h)
    proj = _mm(flat2(h), w["w_in"], name="proj", tn=1152, out_dtype=BF16)
    proj3 = bsd(proj)
    o, o_gated, states = _gla_fwd(proj3, w_alpha_p, small["gla_b_alpha"], small["gla_out_norm_g"], name="gla_fwd")
    if more_weights is not None:
        w = {**w, **more_weights(o_gated)}
    y_a = _mm(flat2(o_gated), w["gla_w_o"], name="gla_out", out_dtype=BF16)
    cq_n, ckv_n = _lat_norm(proj, small["mla_q_lat_g"], small["mla_kv_lat_g"], name="lat_norm")
    q_raw = _mm(cq_n, w["mla_w_uq"], name="mla_uq", out_dtype=BF16)
    kv = _mm(ckv_n, w["mla_w_ukv"], name="mla_ukv", out_dtype=BF16)
    qf, kf, vf = _qk_prep(q_raw, kv, proj, cos_t, sin_t, gq * Q_PRESCALE, gk, name="qk_prep")
    o_attn = _attn_fwd(bsd(qf), bsd(kf), bsd(vf), name="attn_fwd")
    y_b = _mm(flat2(o_attn), w["mla_w_o"], name="mla_out", out_dtype=BF16)
    mixed_in = _merge_fwd(proj3, small["b_merge"], bsd(y_a), bsd(y_b), name="merge_fwd")
    mixed = _mm(flat2(mixed_in), w["w_out"], name="w_out")
    x1, h2 = _resid_norm_mod(x, bsd(mixed), gate1, small["norm2_g"], scale2, shift2, name="norm2")

    def sqrelu(acc, ex, outs):
        r = jnp.maximum(acc, 0.0)
        outs[0][...] = (r * r).astype(BF16)

    r = _mm(flat2(h2), w["mlp_w1"], name="mlp1", epilogue=sqrelu, out_shape=jax.ShapeDtypeStruct((t, DFF), BF16),
            out_specs=_tile_spec(tt, 1024))
    ff = _mm(r, w["mlp_w2"], name="mlp2")
    dy, dff, dgate2, loss_part = _loss_head(x1, bsd(ff), gate2, target, name="loss_head")

    g = {}

    def relu2_bwd(acc, ex, outs):
        outs[0][...] = (acc * (2.0 * jnp.sqrt(ex[0][...].astype(F32)))).astype(BF16)

    dff2 = flat2(dff)
    da1 = _mm(dff2, w["mlp_w2"], tb=True, name="mlp2_dx", epilogue=relu2_bwd, extras=(r,),
              extra_specs=(_tile_spec(tt, 1024),), out_shape=jax.ShapeDtypeStruct((t, DFF), BF16),
              out_specs=_tile_spec(tt, 1024))
    g["mlp_w2"] = _mm(r, dff2, ta=True, name="mlp2_dw")
    dh2 = _mm(da1, w["mlp_w1"], tb=True, name="mlp1_dx")
    g["mlp_w1"] = _mm(flat2(h2), da1, ta=True, name="mlp1_dw")
    token = on_grads("mlp", {n: g.pop(n) for n in ("mlp_w2", "mlp_w1")}, dh2)
    if token is not None:
        gate1 = gate1 + token[0, 0]
    dx1, dscale2, dshift2, dg2, dgate1, dmixed = _norm_mod_bwd(
        bsd(dh2), x1, dy, small["norm2_g"], scale2, gate1, bsd(mixed), name="norm2_bwd")
    dmixed2 = flat2(dmixed)
    dmi = _mm(dmixed2, w["w_out"], tb=True, name="w_out_dx", out_dtype=BF16)
    g["w_out"] = _mm(flat2(mixed_in), dmixed2, ta=True, name="w_out_dw")
    dy_a, dy_b, dl_a, dl_b, db_a, db_b = _merge_bwd(bsd(dmi), proj3, small["b_merge"], bsd(y_a), bsd(y_b), name="merge_bwd")
    dy_a2, dy_b2 = flat2(dy_a), flat2(dy_b)
    dog = _mm(dy_a2, w["gla_w_o"], tb=True, name="gla_out_dx")
    g["gla_w_o"] = _mm(flat2(o_gated), dy_a2, ta=True, name="gla_out_dw")
    dq_g, dk_g, dv_g, dg_g, dlog, db_alpha, d_ong = _gla_bwd(
        bsd(dog), o, states, proj3, w_alpha_p, small["gla_b_alpha"], small["gla_out_norm_g"], name="gla_bwd")
    dlog2 = flat2(dlog)
    da_p = _mm(dlog2, w_alpha_p, tb=True, out_dtype=BF16, name="alpha_dx")
    d_w_alpha = _mm(proj[:, OFF_A:OFF_A + LANE], dlog2, ta=True, name="alpha_dw")[:GLR]
    do_attn = _mm(dy_b2, w["mla_w_o"], tb=True, out_dtype=BF16, name="mla_out_dx")
    g["mla_w_o"] = _mm(flat2(o_attn), dy_b2, ta=True, name="mla_out_dw")
    dqf, dkf, dvf = _attn_bwd(bsd(qf), bsd(kf), bsd(vf), bsd(do_attn), name="attn_bwd")
    dq_raw, dkv, dkpe, dgq, dgk = _qk_prep_bwd(flat2(dqf), flat2(dkf), flat2(dvf), q_raw, kv, proj, cos_t, sin_t, gq, gk,
                                                name="qk_prep_bwd")
    dcq_n = _mm(dq_raw, w["mla_w_uq"], tb=True, name="mla_uq_dx")
    g["mla_w_uq"] = _mm(cq_n, dq_raw, ta=True, name="mla_uq_dw")
    dckv_n = _mm(dkv, w["mla_w_ukv"], tb=True, name="mla_ukv_dx")
    g["mla_w_ukv"] = _mm(ckv_n, dkv, ta=True, name="mla_ukv_dw")
    token = on_grads("mix", {n: g.pop(n) for n in ("w_out", "gla_w_o", "mla_w_o", "mla_w_uq", "mla_w_ukv")}, dckv_n)
    q_lat_g = small["mla_q_lat_g"] if token is None else small["mla_q_lat_g"] + token[0:1, 0:1]
    dcq, dckv, dg_qlat, dg_kvlat = _lat_norm_bwd(dcq_n, dckv_n, proj, q_lat_g, small["mla_kv_lat_g"],
                                                  name="lat_norm_bwd")
    pieces = [(flat2(dq_g), OFF_Q), (flat2(dk_g), OFF_K), (flat2(dv_g), OFF_V), (flat2(dg_g), OFF_G),
              (flat2(dl_a), OFF_MA), (flat2(dl_b), OFF_MB), (dcq, OFF_CQ), (dckv, OFF_CKV), (da_p, OFF_A), (dkpe, OFF_KPE)]
    hb = flat2(h)
    g_w_in = (_pieces_dw(hb, [p for p, off in pieces if off < W_IN_SPLIT], name="proj_dw_a"),
              _pieces_dw(hb, [p for p, off in pieces if off >= W_IN_SPLIT], name="proj_dw_b"))
    token = on_grads("in", {"w_in": g_w_in}, g_w_in[1])
    after = jnp.zeros((8, LANE), F32) if token is None else token
    dh = _pieces_dx(pieces, w["w_in"], after, name="proj_dx")
    token = on_grads("dx", {}, dh)
    if token is not None:
        scale1 = scale1 + token[0, 0]
    grad_x, dscale1, dshift1, dg1 = _norm_mod_bwd(bsd(dh), x, dx1, small["norm1_g"], scale1, name="norm1_bwd")

    dmod = jnp.concatenate([dshift1, dscale1, dgate1, dshift2, dscale2, dgate2], axis=-1).reshape(bsz, 6 * D)
    gs = {"norm1_g": dg1, "b_merge": jnp.concatenate([db_a, db_b], axis=1), "gla_b_alpha": db_alpha,
          "gla_out_norm_g": d_ong, "mla_q_lat_g": dg_qlat, "mla_kv_lat_g": dg_kvlat, "mla_qn_g": dgq[:, :MQK],
          "mla_kn_g": dgk[:, :MQK], "norm2_g": dg2}
    return loss_part[0, 0], grad_x, dmod, {**kept, **g}, gs, d_w_alpha


def kernel(x, c, positions, w_ada, b_ada, norm1_g, w_in, b_merge, gla_w_alpha, gla_b_alpha, gla_out_norm_g, gla_w_o, mla_q_lat_g, mla_w_uq, mla_kv_lat_g, mla_w_ukv, mla_qn_g, mla_kn_g, mla_w_o, w_out, norm2_g, mlp_w1, mlp_w2, loss_target, m_w_ada, m_b_ada, m_norm1_g, m_w_in, m_b_merge, m_gla_w_alpha, m_gla_b_alpha, m_gla_out_norm_g, m_gla_w_o, m_mla_q_lat_g, m_mla_w_uq, m_mla_kv_lat_g, m_mla_w_ukv, m_mla_qn_g, m_mla_kn_g, m_mla_w_o, m_w_out, m_norm2_g, m_mlp_w1, m_mlp_w2, v_w_ada, v_b_ada, v_norm1_g, v_w_in, v_b_merge, v_gla_w_alpha, v_gla_b_alpha, v_gla_out_norm_g, v_gla_w_o, v_mla_q_lat_g, v_mla_w_uq, v_mla_kv_lat_g, v_mla_w_ukv, v_mla_qn_g, v_mla_kn_g, v_mla_w_o, v_w_out, v_norm2_g, v_mlp_w1, v_mlp_w2):
    args = dict(locals())
    names_big = [n for n, _, _ in BIG]
    names_small = [n for n, _ in SMALL]
    bsz = x.shape[0]
    ax, ay, ac = lax.axis_index("x"), lax.axis_index("y"), lax.axis_index("c")
    chip = 2 * ax + ay
    dev = 2 * chip + ac

    small = {n: args[n] for n in names_small}
    sel_c = jnp.reshape(ac, (1,)).astype(jnp.int32)
    sel_chip = jnp.reshape(chip, (1,)).astype(jnp.int32)
    c_all, w_alpha_all = _all_gather8([c, gla_w_alpha[0]], name="comm_c_alpha")
    small["gla_w_alpha"] = jnp.concatenate([w_alpha_all[2 * j] for j in range(4)], axis=1)
    c_all = c_all.reshape(8 * bsz, D)

    shards = {n: args[n][0].astype(BF16) for n in names_big}
    halves_of = lambda names: [shards[n].reshape(2, shards[n].shape[0] // 2, shards[n].shape[1]) for n in names]

    def gather_start(names, deps, tag):
        xs = halves_of(names)
        lands = [lax.empty((4, *xh.shape), BF16) for xh in xs]
        plan = _gather_plan(len(names))
        return names, plan, _rdma_start(xs + lands, 3 * len(names), plan, deps, name="comm_weights_start_" + tag)

    def gather_finish(started, after, tag):
        names, plan, sems = started
        arrs = _rdma_wait(sems[0], sems[1], sems[2], plan, after, name="comm_weights_wait_" + tag)
        filled = _pair_fill(arrs[len(names):], name="comm_weights_pair_" + tag)
        own = [a.reshape(shards[n].shape) for n, a in zip(names, arrs)]
        return _full_weights(dict(zip(names, _own_and_landed(filled, own))))


    def add_bias(acc, ex, outs):
        outs[0][...] = acc + ex[0][...]

    silu = lambda v: v * _sigmoid(v)
    b_ada_mine = lax.dynamic_slice(b_ada, (0, chip * ADA_SHARD[1]), (1, ADA_SHARD[1]))
    mod_part = _mm(c_all, w_ada[0], name="ada", tn=512, a_fn=silu, epilogue=add_bias, extras=(b_ada_mine,),
                   extra_specs=(pl.BlockSpec((1, 512), lambda i, j, k: (0, j)),),
                   out_shape=jax.ShapeDtypeStruct((8 * bsz, ADA_SHARD[1]), F32), out_specs=_tile_spec(8 * bsz, 512))
    mod_all = _all_gather8([mod_part], name="comm_mod")[0]
    mod_rows = lax.dynamic_slice(mod_all, (0, dev * bsz, 0), (8, bsz, ADA_SHARD[1]))
    mod = jnp.concatenate([mod_rows[2 * j] for j in range(4)], axis=1)
    first = gather_start(["w_in"], (mod,), "in")
    rest = gather_start([n for n in names_big if n != "w_in"], (mod, first[2][3]), "rest")
    mod = mod + rest[2][3][0, 0]
    w_in_after = lambda after: gather_finish(first, after, "in")
    more_weights = lambda after: gather_finish(rest, after, "rest")

    stage = {}

    def begin(tag, names, arrays, lands, n_copies, plan, what):
        stage[tag] = (names, plan, _rdma_start(arrays + lands, n_copies, plan, (), name=f"comm_{what}_start_{tag}"))
        return stage[tag][2][3]

    def landed(tag, after, what):
        names, plan, sems = stage[tag]
        arrs = _rdma_wait(sems[0], sems[1], sems[2], plan, after, name=f"comm_{what}_wait_{tag}")
        return names, arrs[:len(arrs) // 2], arrs[len(arrs) // 2:]

    def swap_start(tag, grads):
        names = list(grads)
        parts = [_grad_slots(grads)[n] for n in names]
        lands = [lax.empty((4, *p.shape[2:]), F32) for p in parts]
        return begin(tag, names, parts, lands, len(names), _sibling_plan(len(names), lambda r, c: r.at[:, 1 - c]), "pair_sum")

    def scatter_start(tag, after):
        names, parts, sib_halves = landed(tag, after, "pair_sum")
        pairs = [_pair_add(p, s, sel_c, name="pair_add_" + n) for n, p, s in zip(names, parts, sib_halves)]
        recvs = [lax.empty((3, *p.shape[1:]), BF16) for p in pairs]
        return begin(tag, names, pairs, recvs, 3 * len(names), _scatter_plan(len(names)), "scatter")

    def join_start(tag, after):
        names, pairs, recvs = landed(tag, after, "scatter")
        halves = [_chip_sum(p, r, sel_chip, name="chip_sum_" + n) for n, p, r in zip(names, pairs, recvs)]
        lands = [lax.empty(h.shape, F32) for h in halves]
        return begin(tag, names, halves, lands, len(names), _sibling_plan(len(names), lambda r, c: r), "pair_join")

    def reduce_step(tag, grads, after):
        if tag == "mlp":
            return swap_start("mlp", grads)
        if tag == "mix":
            return scatter_start("mlp", after) + swap_start("mix", grads)
        if tag == "in":
            return scatter_start("mix", after) + swap_start("in", grads)
        return scatter_start("in", after)

    loss_part, grad_x, dmod, g, gs, d_w_alpha = _local_step(x, positions, mod, loss_target, w_in_after, small,
                                                            more_weights, reduce_step)

    assert not g, list(g)
    gs_packed = _pack_small([gs[n] for n, _ in SMALL_RED], d_w_alpha, jnp.full((1, LANE), loss_part, F32),
                            name="pack_small")
    small_lands = [lax.empty((8, *a.shape), F32) for a in (dmod, gs_packed)]
    begin("small", ["dmod", "small"], [dmod, gs_packed], small_lands, 7 * 2, _gather8_plan(2), "gather8")

    res = {}

    def finish(tag, after):
        names, halves, theirs = landed(tag, after, "pair_join")
        for n, mine, other in zip(names, halves, theirs):
            if n == "w_in":
                south = ac == 0
                g_t = jnp.concatenate([jnp.where(south, mine, other), jnp.where(south, other, mine)], axis=0).T
                outs = _adamw(w_in[0].T, g_t, m_w_in[0].T, v_w_in[0].T, name="adamw_w_in", by_cols=True)
                res[n] = tuple(a.T for a in (g_t, *outs))
            else:
                res[n] = _adamw_halves(args[n][0], args["m_" + n][0], args["v_" + n][0], mine, other, sel_c,
                                       name="adamw_" + n)
        return res[names[-1]][1]

    join_start("mlp", grad_x)
    join_start("mix", grad_x)
    done = finish("mix", finish("mlp", grad_x))

    _, (dmod_own, gs_own), (dmod_all, gs_all) = landed("small", done, "gather8")
    dmod_all = lax.dynamic_update_slice(dmod_all, dmod_own[None], (dev, 0, 0)).reshape(8 * bsz, 6 * D)
    gs_all = lax.dynamic_update_slice(gs_all, gs_own[None], (dev, 0, 0))
    dmod_mine = lax.dynamic_slice(dmod_all, (0, chip * ADA_SHARD[1]), (8 * bsz, ADA_SHARD[1]))
    g_w_ada = _mm(c_all, dmod_mine, ta=True, a_fn=silu, name="ada_dw")
    wmv = [(args[n], args["m_" + n], args["v_" + n]) for n in names_small]
    wmv.append((gla_w_alpha[0], m_gla_w_alpha[0], v_gla_w_alpha[0]))
    res_small, loss_sum = _small_update(gs_all, dmod_all, sel_chip, wmv, name="small_update")
    res.update(res_small)
    loss = loss_sum * (0.5 / D)
    join_start("in", g_w_ada)
    res["w_ada"] = (g_w_ada, *_adamw(w_ada[0], g_w_ada, m_w_ada[0], v_w_ada[0], name="adamw_w_ada"))
    finish("in", res["w_ada"][1])

    order = ["w_ada", "b_ada", "norm1_g", "w_in", "b_merge", "gla_w_alpha", "gla_b_alpha", "gla_out_norm_g", "gla_w_o",
             "mla_q_lat_g", "mla_w_uq", "mla_kv_lat_g", "mla_w_ukv", "mla_qn_g", "mla_kn_g", "mla_w_o", "w_out",
             "norm2_g", "mlp_w1", "mlp_w2"]
    named = lambda k: [res[n][k].reshape(args[n].shape) for n in order]
    return (loss, grad_x, *named(0), *named(1), *named(2), *named(3))
```

```python
import jax
import jax.numpy as jnp
import numpy as np
from jax import lax
from jax.experimental import pallas as pl
from jax.experimental.pallas import tpu as pltpu

F32 = jnp.float32
BF16 = jnp.bfloat16
MESH = pl.DeviceIdType.MESH

D = 1024
CHUNK = 64
EPS = 1e-6
GH, GDK, GDV, GLR, GTAU = 4, 128, 256, 16, 16.0
MH, MQR, MKVR, MNOPE, MROPE, MVD = 16, 256, 128, 64, 32, 64
MQK = MNOPE + MROPE
DFF = 4 * D
ROPE_THETA = 10000.0
IN_WIDTH = 5552
LANE = 128
OFF_Q, OFF_K, OFF_V, OFF_G, OFF_MA, OFF_MB, OFF_CQ, OFF_CKV, OFF_A, OFF_KPE, PW = (
    0, 512, 1024, 2048, 3072, 4096, 5120, 5376, 5504, 5632, 5760)
ADAM_LR, ADAM_B1, ADAM_B2, ADAM_EPS, ADAM_WD, ADAM_STEP = 0.001, 0.9, 0.999, 1e-08, 0.01, 10
VMEM_LIMIT = 48 * 1024 * 1024


def _params(n_axes):
    return pltpu.CompilerParams(dimension_semantics=("arbitrary",) * n_axes, vmem_limit_bytes=VMEM_LIMIT)


def _tile(n, target):
    if n <= target:
        return n
    best = None
    for t in range(LANE, target + 1, LANE):
        if n % t == 0:
            best = t
    assert best is not None, (n, target)
    return best


def _sigmoid(x):
    return 1.0 / (1.0 + jnp.exp(-x))


MM_VMEM_BUDGET = 36 * 1024 * 1024


def _mm(a, b, *, name, ta=False, tb=False, out_dtype=F32, tm=1024, tn=1024, tk=4096,
        epilogue=None, extras=(), extra_specs=(), out_shape=None, out_specs=None, a_fn=None):
    if ta:
        kdim, m = a.shape
    else:
        m, kdim = a.shape
    if tb:
        n, k2 = b.shape
    else:
        k2, n = b.shape
    assert kdim == k2, (a.shape, b.shape)
    tm, tn, tk = _tile(m, tm), _tile(n, tn), _tile(kdim, tk)
    tiles = lambda rows: 2 * (rows * tk * a.dtype.itemsize + tk * tn * b.dtype.itemsize + rows * tn * 4) + rows * tn * 4
    while out_shape is None and tiles(tm) > MM_VMEM_BUDGET and tm % 256 == 0:
        tm //= 2
    nk = kdim // tk
    a_spec = pl.BlockSpec((tk, tm), lambda i, j, k: (k, i)) if ta else pl.BlockSpec((tm, tk), lambda i, j, k: (i, k))
    b_spec = pl.BlockSpec((tn, tk), lambda i, j, k: (j, k)) if tb else pl.BlockSpec((tk, tn), lambda i, j, k: (k, j))
    dims = (((0 if ta else 1,), (1 if tb else 0,)), ((), ()))
    ne = len(extras)
    if out_shape is None:
        out_shape = jax.ShapeDtypeStruct((m, n), out_dtype)
        out_specs = pl.BlockSpec((tm, tn), lambda i, j, k: (i, j))
    n_out = len(out_shape) if isinstance(out_shape, (list, tuple)) else 1
    in_place = epilogue is None and n_out == 1 and out_shape.dtype == F32
    scratch = [] if (nk == 1 or in_place) else [pltpu.VMEM((tm, tn), F32)]

    def body(a_ref, b_ref, *rest):
        ex, outs = rest[:ne], rest[ne:ne + n_out]
        av = a_ref[...] if a_fn is None else a_fn(a_ref[...])
        prod = lax.dot_general(av.astype(BF16), b_ref[...].astype(BF16), dims, preferred_element_type=F32)

        def finish(val):
            if epilogue is None:
                outs[0][...] = val.astype(outs[0].dtype)
            else:
                epilogue(val, ex, outs)

        if nk == 1:
            finish(prod)
            return
        k = pl.program_id(2)
        acc = outs[0] if in_place else rest[-1]

        @pl.when(k == 0)
        def _():
            acc[...] = prod

        @pl.when(k > 0)
        def _():
            acc[...] += prod

        if not in_place:
            @pl.when(k == nk - 1)
            def _():
                finish(acc[...])

    return pl.pallas_call(
        body, name=name, grid=(m // tm, n // tn, nk),
        in_specs=[a_spec, b_spec, *extra_specs], out_specs=out_specs, out_shape=out_shape,
        scratch_shapes=scratch, compiler_params=_params(3),
    )(a, b, *extras)


def _tile_spec(tm, tn):
    return pl.BlockSpec((tm, tn), lambda i, j, k: (i, j))


def _pieces_dx(pieces, w, after, *, name, tm=256):
    t = pieces[0][0].shape[0]
    tm = _tile(t, tm)
    npc = len(pieces)

    def body(*refs):
        p_refs, w_ref, out_ref = refs[:npc], refs[npc], refs[-1]
        acc = None
        for (arr, off), p_ref in zip(pieces, p_refs):
            part = lax.dot_general(p_ref[...].astype(BF16), w_ref[:, off:off + arr.shape[1]], _NT,
                                   preferred_element_type=F32)
            acc = part if acc is None else acc + part
        out_ref[...] = acc

    return pl.pallas_call(
        body, name=name, grid=(t // tm,),
        in_specs=[pl.BlockSpec((tm, arr.shape[1]), lambda i: (i, 0)) for arr, _ in pieces]
        + [pl.BlockSpec(w.shape, lambda i: (0, 0)), pl.BlockSpec((8, LANE), lambda i: (0, 0))],
        out_specs=pl.BlockSpec((tm, w.shape[0]), lambda i: (i, 0)),
        out_shape=jax.ShapeDtypeStruct((t, w.shape[0]), F32), compiler_params=_params(1),
    )(*[arr for arr, _ in pieces], w, after)


def _pieces_dw(h, pieces, *, name, tk=1024):
    t, d = h.shape
    tk = _tile(t, tk)
    widths = [p.shape[1] for p in pieces]
    starts = [sum(widths[:i]) for i in range(len(pieces))]

    def body(h_ref, *refs):
        p_refs, out_ref = refs[:-1], refs[-1]
        first = pl.program_id(0) == 0
        hv = h_ref[...]
        for p_ref, start, width in zip(p_refs, starts, widths):
            part = lax.dot_general(hv, p_ref[...].astype(BF16), _TN, preferred_element_type=F32)
            cols = slice(start, start + width)

            @pl.when(first)
            def _():
                out_ref[:, cols] = part

            @pl.when(jnp.logical_not(first))
            def _():
                out_ref[:, cols] += part

    return pl.pallas_call(
        body, name=name, grid=(t // tk,),
        in_specs=[pl.BlockSpec((tk, d), lambda k: (k, 0))] + [pl.BlockSpec((tk, wd), lambda k: (k, 0)) for wd in widths],
        out_specs=pl.BlockSpec((d, sum(widths)), lambda k: (0, 0)),
        out_shape=jax.ShapeDtypeStruct((d, sum(widths)), F32), compiler_params=_params(1),
    )(h, *pieces)


def _rms(x, g):
    r = lax.rsqrt(jnp.mean(x * x, axis=-1, keepdims=True) + EPS)
    return x * r, r


def _row_spec(ts, width, col=0):
    return pl.BlockSpec((None, ts, width), lambda b, i: (b, i, col))


def _vec_spec(width):
    return pl.BlockSpec((None, 1, width), lambda b, i: (b, 0, 0))


def _gain_spec(width):
    return pl.BlockSpec((1, width), lambda b, i: (0, 0))


def _norm_mod(x, g, scale, shift, *, name, ts=512):
    bsz, s, d = x.shape
    ts = min(ts, s)

    def body(x_ref, g_ref, sc_ref, sh_ref, h_ref):
        xh, _ = _rms(x_ref[...], None)
        h_ref[...] = ((xh * g_ref[...]) * (1.0 + sc_ref[...]) + sh_ref[...]).astype(BF16)

    return pl.pallas_call(
        body, name=name, grid=(bsz, s // ts),
        in_specs=[_row_spec(ts, d), _gain_spec(d), _vec_spec(d), _vec_spec(d)],
        out_specs=_row_spec(ts, d), out_shape=jax.ShapeDtypeStruct((bsz, s, d), BF16),
        compiler_params=_params(2),
    )(x, g, scale, shift)


def _resid_norm_mod(x, mixed, gate, g, scale, shift, *, name, ts=512):
    bsz, s, d = x.shape
    ts = min(ts, s)

    def body(x_ref, mx_ref, gt_ref, g_ref, sc_ref, sh_ref, x1_ref, h_ref):
        x1 = x_ref[...] + gt_ref[...] * mx_ref[...]
        x1_ref[...] = x1
        xh, _ = _rms(x1, None)
        h_ref[...] = ((xh * g_ref[...]) * (1.0 + sc_ref[...]) + sh_ref[...]).astype(BF16)

    return pl.pallas_call(
        body, name=name, grid=(bsz, s // ts),
        in_specs=[_row_spec(ts, d), _row_spec(ts, d), _vec_spec(d), _gain_spec(d), _vec_spec(d), _vec_spec(d)],
        out_specs=[_row_spec(ts, d), _row_spec(ts, d)],
        out_shape=[jax.ShapeDtypeStruct((bsz, s, d), F32), jax.ShapeDtypeStruct((bsz, s, d), BF16)],
        compiler_params=_params(2),
    )(x, mixed, gate, g, scale, shift)


def _norm_mod_bwd(dh, xin, resid, g, scale, gate=None, mixed=None, *, name, ts=512):
    bsz, s, d = xin.shape
    ts = min(ts, s)
    gated = gate is not None

    def body(*refs):
        if gated:
            dh_ref, x_ref, rs_ref, g_ref, sc_ref, gt_ref, mx_ref, dx_ref, dsc_ref, dsh_ref, dg_ref, dgt_ref, dmx_ref = refs
        else:
            dh_ref, x_ref, rs_ref, g_ref, sc_ref, dx_ref, dsc_ref, dsh_ref, dg_ref = refs
        b, i = pl.program_id(0), pl.program_id(1)

        @pl.when(i == 0)
        def _():
            dsc_ref[...] = jnp.zeros_like(dsc_ref)
            dsh_ref[...] = jnp.zeros_like(dsh_ref)
            if gated:
                dgt_ref[...] = jnp.zeros_like(dgt_ref)

        @pl.when((i == 0) & (b == 0))
        def _():
            dg_ref[...] = jnp.zeros_like(dg_ref)

        dh_v, gv = dh_ref[...], g_ref[...]
        xh, r = _rms(x_ref[...], None)
        dsc_ref[...] += jnp.sum(dh_v * (xh * gv), axis=0, keepdims=True)
        dsh_ref[...] += jnp.sum(dh_v, axis=0, keepdims=True)
        dn = dh_v * (1.0 + sc_ref[...])
        dg_ref[...] += jnp.sum(dn * xh, axis=0, keepdims=True)
        dxh = dn * gv
        dx = rs_ref[...] + r * (dxh - xh * jnp.mean(dxh * xh, axis=-1, keepdims=True))
        dx_ref[...] = dx
        if gated:
            dgt_ref[...] += jnp.sum(dx * mx_ref[...], axis=0, keepdims=True)
            dmx_ref[...] = (dx * gt_ref[...]).astype(BF16)

    ins = [dh, xin, resid, g, scale]
    in_specs = [_row_spec(ts, d), _row_spec(ts, d), _row_spec(ts, d), _gain_spec(d), _vec_spec(d)]
    out_specs = [_row_spec(ts, d), _vec_spec(d), _vec_spec(d), _gain_spec(d)]
    out_shape = [jax.ShapeDtypeStruct((bsz, s, d), F32), jax.ShapeDtypeStruct((bsz, 1, d), F32),
                 jax.ShapeDtypeStruct((bsz, 1, d), F32), jax.ShapeDtypeStruct((1, d), F32)]
    if gated:
        ins += [gate, mixed]
        in_specs += [_vec_spec(d), _row_spec(ts, d)]
        out_specs += [_vec_spec(d), _row_spec(ts, d)]
        out_shape += [jax.ShapeDtypeStruct((bsz, 1, d), F32), jax.ShapeDtypeStruct((bsz, s, d), BF16)]
    return pl.pallas_call(
        body, name=name, grid=(bsz, s // ts), in_specs=in_specs, out_specs=out_specs, out_shape=out_shape,
        compiler_params=_params(2),
    )(*ins)


def _loss_head(x1, ff, gate2, target, *, name, ts=512):
    bsz, s, d = x1.shape
    ts = min(ts, s)

    def body(x1_ref, ff_ref, gt_ref, t_ref, dy_ref, dff_ref, dgt_ref, loss_ref, acc):
        b, i = pl.program_id(0), pl.program_id(1)

        @pl.when(i == 0)
        def _():
            dgt_ref[...] = jnp.zeros_like(dgt_ref)

        @pl.when((i == 0) & (b == 0))
        def _():
            acc[...] = jnp.zeros_like(acc)

        ffv, gt = ff_ref[...], gt_ref[...]
        diff = (x1_ref[...] + gt * ffv) - t_ref[...]
        acc[...] += jnp.sum((diff * diff).reshape(ts // 8, 8, d), axis=0)
        dy = diff * (1.0 / d)
        dy_ref[...] = dy
        dgt_ref[...] += jnp.sum(dy * ffv, axis=0, keepdims=True)
        dff_ref[...] = (dy * gt).astype(BF16)

        @pl.when((i == pl.num_programs(1) - 1) & (b == pl.num_programs(0) - 1))
        def _():
            loss_ref[...] = jnp.full(loss_ref.shape, jnp.sum(acc[...]), F32)

    return pl.pallas_call(
        body, name=name, grid=(bsz, s // ts),
        in_specs=[_row_spec(ts, d), _row_spec(ts, d), _vec_spec(d), _row_spec(ts, d)],
        out_specs=[_row_spec(ts, d), _row_spec(ts, d), _vec_spec(d), pl.BlockSpec((8, LANE), lambda b, i: (0, 0))],
        out_shape=[jax.ShapeDtypeStruct((bsz, s, d), F32), jax.ShapeDtypeStruct((bsz, s, d), BF16),
                   jax.ShapeDtypeStruct((bsz, 1, d), F32), jax.ShapeDtypeStruct((8, LANE), F32)],
        scratch_shapes=[pltpu.VMEM((8, d), F32)], compiler_params=_params(2),
    )(x1, ff, gate2, target)


def _merge_fwd(proj, b_merge, y_a, y_b, *, name, ts=512):
    bsz, s, _ = proj.shape
    ts = min(ts, s)

    def body(la_ref, lb_ref, ba_ref, bb_ref, ya_ref, yb_ref, out_ref):
        ga = _sigmoid(la_ref[...] + ba_ref[...])
        gb = _sigmoid(lb_ref[...] + bb_ref[...])
        out_ref[...] = (ga * ya_ref[...] + gb * yb_ref[...]).astype(BF16)

    return pl.pallas_call(
        body, name=name, grid=(bsz, s // ts),
        in_specs=[_row_spec(ts, D, OFF_MA // D), _row_spec(ts, D, OFF_MB // D),
                  pl.BlockSpec((1, D), lambda b, i: (0, 0)), pl.BlockSpec((1, D), lambda b, i: (0, 1)),
                  _row_spec(ts, D), _row_spec(ts, D)],
        out_specs=_row_spec(ts, D), out_shape=jax.ShapeDtypeStruct((bsz, s, D), BF16),
        compiler_params=_params(2),
    )(proj, proj, b_merge, b_merge, y_a, y_b)


def _merge_bwd(dmi, proj, b_merge, y_a, y_b, *, name, ts=512):
    bsz, s, _ = proj.shape
    ts = min(ts, s)

    def body(d_ref, la_ref, lb_ref, ba_ref, bb_ref, ya_ref, yb_ref, dya_ref, dyb_ref, dla_ref, dlb_ref, dba_ref, dbb_ref):
        @pl.when((pl.program_id(0) == 0) & (pl.program_id(1) == 0))
        def _():
            dba_ref[...] = jnp.zeros_like(dba_ref)
            dbb_ref[...] = jnp.zeros_like(dbb_ref)

        dv = d_ref[...].astype(F32)
        ga = _sigmoid(la_ref[...] + ba_ref[...])
        gb = _sigmoid(lb_ref[...] + bb_ref[...])
        dya_ref[...] = (dv * ga).astype(BF16)
        dyb_ref[...] = (dv * gb).astype(BF16)
        dla = (dv * ya_ref[...]) * (ga * (1.0 - ga))
        dlb = (dv * yb_ref[...]) * (gb * (1.0 - gb))
        dla_ref[...] = dla.astype(BF16)
        dlb_ref[...] = dlb.astype(BF16)
        dba_ref[...] += jnp.sum(dla, axis=0, keepdims=True)
        dbb_ref[...] += jnp.sum(dlb, axis=0, keepdims=True)

    act = jax.ShapeDtypeStruct((bsz, s, D), BF16)
    return pl.pallas_call(
        body, name=name, grid=(bsz, s // ts),
        in_specs=[_row_spec(ts, D), _row_spec(ts, D, OFF_MA // D), _row_spec(ts, D, OFF_MB // D),
                  pl.BlockSpec((1, D), lambda b, i: (0, 0)), pl.BlockSpec((1, D), lambda b, i: (0, 1)),
                  _row_spec(ts, D), _row_spec(ts, D)],
        out_specs=[_row_spec(ts, D)] * 4 + [_gain_spec(D)] * 2,
        out_shape=[act, act, act, act, jax.ShapeDtypeStruct((1, D), F32), jax.ShapeDtypeStruct((1, D), F32)],
        compiler_params=_params(2),
    )(dmi, proj, proj, b_merge, b_merge, y_a, y_b)


def _tri(lower):
    r = lax.broadcasted_iota(jnp.int32, (CHUNK, CHUNK), 0)
    c = lax.broadcasted_iota(jnp.int32, (CHUNK, CHUNK), 1)
    return jnp.where((c <= r) if lower else (c >= r), 1.0, 0.0).astype(F32)


def _gla_logits(a_ref, wal_ref, bal_ref):
    logits = jnp.dot(a_ref[...].astype(BF16), wal_ref[...].astype(BF16), preferred_element_type=F32) + bal_ref[...]
    la = (jnp.minimum(logits, 0.0) - jnp.log(1.0 + jnp.exp(-jnp.abs(logits)))) * (1.0 / GTAU)
    return logits, la


def _chunk_cumsum(la_n, tri):
    cum = jnp.dot(tri, la_n, preferred_element_type=F32, precision=lax.Precision.HIGHEST)
    return cum, jnp.sum(la_n, axis=0, keepdims=True)


def _gla_specs(s, nc):
    def blk(width, off):
        return pl.BlockSpec((None, s, width), lambda h, b: (b, 0, off // width + h))

    proj_specs = [blk(GDK, OFF_Q), blk(GDK, OFF_K), blk(GDV, OFF_V), blk(GDV, OFF_G),
                  pl.BlockSpec((None, s, LANE), lambda h, b: (b, 0, OFF_A // LANE)),
                  pl.BlockSpec((LANE, GDK), lambda h, b: (0, h)), pl.BlockSpec((1, GDK), lambda h, b: (0, h)),
                  pl.BlockSpec((1, GDV), lambda h, b: (0, 0))]
    st_spec = pl.BlockSpec((None, None, nc, GDV, GDK), lambda h, b: (b, h, 0, 0, 0))
    return blk, proj_specs, st_spec


def _gla_fwd(proj, w_alpha_p, b_alpha, out_norm_g, *, name):
    bsz, s, _ = proj.shape
    nc = s // CHUNK
    scale = GDK ** -0.5

    rb = min(512, s)

    def body(q_ref, k_ref, v_ref, g_ref, a_ref, wal_ref, bal_ref, ong_ref, o_ref, og_ref, st_ref):
        _, la = _gla_logits(a_ref, wal_ref, bal_ref)
        tri = _tri(True)
        st = jnp.zeros((GDV, GDK), F32)
        for n in range(nc):
            rows = pl.ds(n * CHUNK, CHUNK)
            cum, cum_end = _chunk_cumsum(la[n * CHUNK:(n + 1) * CHUNK], tri)
            kd = k_ref[rows, :] * jnp.exp(cum_end - cum)
            ut = lax.dot_general(v_ref[rows, :].astype(BF16), kd.astype(BF16), _TN, preferred_element_type=F32)
            st = st * jnp.exp(cum_end) + ut
            st_ref[n] = st.astype(BF16)
            o_ref[rows, :] = lax.dot_general((q_ref[rows, :].astype(F32) * scale).astype(BF16), st.astype(BF16), _NT,
                                             preferred_element_type=F32)
        for j in range(0, s, rb):
            blk_rows = pl.ds(j, rb)
            oh, _ = _rms(o_ref[blk_rows, :], None)
            gv = g_ref[blk_rows, :].astype(F32)
            og_ref[blk_rows, :] = ((oh * ong_ref[...]) * (gv * _sigmoid(gv))).astype(BF16)

    blk, proj_specs, st_spec = _gla_specs(s, nc)
    return pl.pallas_call(
        body, name=name, grid=(GH, bsz), in_specs=proj_specs, out_specs=[blk(GDV, 0), blk(GDV, 0), st_spec],
        out_shape=[jax.ShapeDtypeStruct((bsz, s, GH * GDV), F32), jax.ShapeDtypeStruct((bsz, s, GH * GDV), BF16),
                   jax.ShapeDtypeStruct((bsz, GH, nc, GDV, GDK), BF16)],
        compiler_params=_params(2),
    )(proj, proj, proj, proj, proj, w_alpha_p, b_alpha, out_norm_g)


def _gla_bwd(dog, o, states, proj, w_alpha_p, b_alpha, out_norm_g, *, name):
    bsz, s, _ = proj.shape
    nc = s // CHUNK
    scale = GDK ** -0.5

    def body(dog_ref, o_ref, st_ref, q_ref, k_ref, v_ref, g_ref, a_ref, wal_ref, bal_ref, ong_ref,
             dq_ref, dk_ref, dv_ref, dg_ref, dl_ref, dbal_ref, dong_ref, do_scr, dlog_scr):
        h, b = pl.program_id(0), pl.program_id(1)

        @pl.when(b == 0)
        def _():
            dbal_ref[...] = jnp.zeros_like(dbal_ref)

        @pl.when((b == 0) & (h == 0))
        def _():
            dong_ref[...] = jnp.zeros_like(dong_ref)

        ong = ong_ref[...]
        for j in range(0, s, rb):
            blk_rows = pl.ds(j, rb)
            gv, dogv = g_ref[blk_rows, :].astype(F32), dog_ref[blk_rows, :].astype(F32)
            sg = _sigmoid(gv)
            oh, r = _rms(o_ref[blk_rows, :], None)
            don = dogv * (gv * sg)
            dg_ref[blk_rows, :] = (dogv * (oh * ong) * (sg * (1.0 + gv * (1.0 - sg)))).astype(BF16)
            dong_ref[...] += jnp.sum(don * oh, axis=0, keepdims=True)
            doh = don * ong
            do_scr[blk_rows, :] = (r * (doh - oh * jnp.mean(doh * oh, axis=-1, keepdims=True))).astype(BF16)

        logits, la = _gla_logits(a_ref, wal_ref, bal_ref)
        tri_lo, tri_up = _tri(True), _tri(False)
        carry = jnp.zeros((GDV, GDK), F32)
        for n in range(nc - 1, -1, -1):
            rows = pl.ds(n * CHUNK, CHUNK)
            cum, cum_end = _chunk_cumsum(la[n * CHUNK:(n + 1) * CHUNK], tri_lo)
            decay = jnp.exp(cum_end)
            w = jnp.exp(cum_end - cum)
            kd = k_ref[rows, :] * w
            do_b = do_scr[rows, :]
            qs_b = (q_ref[rows, :].astype(F32) * scale).astype(BF16)
            dq_ref[rows, :] = (jnp.dot(do_b, st_ref[n], preferred_element_type=F32) * scale).astype(BF16)
            dsn = lax.dot_general(do_b, qs_b, _TN, preferred_element_type=F32) + carry
            carry = dsn * decay
            dsn_b = dsn.astype(BF16)
            dv_ref[rows, :] = lax.dot_general(kd.astype(BF16), dsn_b, _NT, preferred_element_type=F32).astype(BF16)
            dkd = jnp.dot(v_ref[rows, :].astype(BF16), dsn_b, preferred_element_type=F32)
            dk_ref[rows, :] = (dkd * w).astype(BF16)
            e = dkd * kd
            dcum_end = jnp.sum(e, axis=0, keepdims=True)
            if n > 0:
                dcum_end += jnp.sum(dsn * st_ref[n - 1].astype(F32), axis=0, keepdims=True) * decay
            dlog_scr[rows, :] = dcum_end - jnp.dot(tri_up, e, preferred_element_type=F32,
                                                  precision=lax.Precision.HIGHEST)
        dlog = dlog_scr[...] * (1.0 / GTAU) * (1.0 - _sigmoid(logits))
        dl_ref[...] = dlog.astype(BF16)
        dbal_ref[...] += jnp.sum(dlog, axis=0, keepdims=True)

    rb = min(512, s)

    blk, proj_specs, st_spec = _gla_specs(s, nc)
    act = lambda wd: jax.ShapeDtypeStruct((bsz, s, wd), BF16)
    return pl.pallas_call(
        body, name=name, grid=(GH, bsz), in_specs=[blk(GDV, 0), blk(GDV, 0), st_spec, *proj_specs],
        out_specs=[blk(GDK, 0), blk(GDK, 0), blk(GDV, 0), blk(GDV, 0), blk(GDK, 0),
                   pl.BlockSpec((1, GDK), lambda h, b: (0, h)), pl.BlockSpec((1, GDV), lambda h, b: (0, 0))],
        out_shape=[act(GH * GDK), act(GH * GDK), act(GH * GDV), act(GH * GDV), act(GH * GDK),
                   jax.ShapeDtypeStruct((1, GH * GDK), F32), jax.ShapeDtypeStruct((1, GDV), F32)],
        scratch_shapes=[pltpu.VMEM((s, GDV), BF16), pltpu.VMEM((s, GDK), F32)], compiler_params=_params(2),
    )(dog, o, states, proj, proj, proj, proj, proj, w_alpha_p, b_alpha, out_norm_g)


def _lane():
    return lax.broadcasted_iota(jnp.int32, (1, LANE), 1)


def _swap_halves(x):
    lane = _lane()
    half = MROPE // 2
    lo = (lane >= MNOPE) & (lane < MNOPE + half)
    hi = (lane >= MNOPE + half) & (lane < MQK)
    return jnp.where(lo, pltpu.roll(x, LANE - half, 1), jnp.where(hi, pltpu.roll(x, half, 1), 0.0))


def _norm96(x, g):
    r = lax.rsqrt(jnp.sum(x * x, axis=-1, keepdims=True) * (1.0 / MQK) + EPS)
    return x * r, r


def _lat_norm(proj, q_lat_g, kv_lat_g, *, name, ts=512):
    t = proj.shape[0]
    ts = min(ts, t)

    def body(cq_ref, ckv_ref, gq_ref, gk_ref, oq_ref, ok_ref):
        xq, _ = _rms(cq_ref[...].astype(F32), None)
        oq_ref[...] = (xq * gq_ref[...]).astype(BF16)
        xk, _ = _rms(ckv_ref[...].astype(F32), None)
        ok_ref[...] = (xk * gk_ref[...]).astype(BF16)

    return pl.pallas_call(
        body, name=name, grid=(t // ts,),
        in_specs=[pl.BlockSpec((ts, MQR), lambda i: (i, OFF_CQ // MQR)), pl.BlockSpec((ts, MKVR), lambda i: (i, OFF_CKV // MKVR)),
                  pl.BlockSpec((1, MQR), lambda i: (0, 0)), pl.BlockSpec((1, MKVR), lambda i: (0, 0))],
        out_specs=[pl.BlockSpec((ts, MQR), lambda i: (i, 0)), pl.BlockSpec((ts, MKVR), lambda i: (i, 0))],
        out_shape=[jax.ShapeDtypeStruct((t, MQR), BF16), jax.ShapeDtypeStruct((t, MKVR), BF16)],
        compiler_params=_params(1),
    )(proj, proj, q_lat_g, kv_lat_g)


def _lat_norm_bwd(dcqn, dckvn, proj, q_lat_g, kv_lat_g, *, name, ts=512):
    t = proj.shape[0]
    ts = min(ts, t)

    def one(d_ref, x_ref, g_ref, dx_ref, dg_ref):
        xh, r = _rms(x_ref[...].astype(F32), None)
        dn = d_ref[...]
        dg_ref[...] += jnp.sum(dn * xh, axis=0, keepdims=True)
        dxh = dn * g_ref[...]
        dx_ref[...] = (r * (dxh - xh * jnp.mean(dxh * xh, axis=-1, keepdims=True))).astype(BF16)

    def body(dq_ref, dk_ref, cq_ref, ckv_ref, gq_ref, gk_ref, dxq_ref, dxk_ref, dgq_ref, dgk_ref):
        @pl.when(pl.program_id(0) == 0)
        def _():
            dgq_ref[...] = jnp.zeros_like(dgq_ref)
            dgk_ref[...] = jnp.zeros_like(dgk_ref)

        one(dq_ref, cq_ref, gq_ref, dxq_ref, dgq_ref)
        one(dk_ref, ckv_ref, gk_ref, dxk_ref, dgk_ref)

    return pl.pallas_call(
        body, name=name, grid=(t // ts,),
        in_specs=[pl.BlockSpec((ts, MQR), lambda i: (i, 0)), pl.BlockSpec((ts, MKVR), lambda i: (i, 0)),
                  pl.BlockSpec((ts, MQR), lambda i: (i, OFF_CQ // MQR)), pl.BlockSpec((ts, MKVR), lambda i: (i, OFF_CKV // MKVR)),
                  pl.BlockSpec((1, MQR), lambda i: (0, 0)), pl.BlockSpec((1, MKVR), lambda i: (0, 0))],
        out_specs=[pl.BlockSpec((ts, MQR), lambda i: (i, 0)), pl.BlockSpec((ts, MKVR), lambda i: (i, 0)),
                   pl.BlockSpec((1, MQR), lambda i: (0, 0)), pl.BlockSpec((1, MKVR), lambda i: (0, 0))],
        out_shape=[jax.ShapeDtypeStruct((t, MQR), BF16), jax.ShapeDtypeStruct((t, MKVR), BF16),
                   jax.ShapeDtypeStruct((1, MQR), F32), jax.ShapeDtypeStruct((1, MKVR), F32)],
        compiler_params=_params(1),
    )(dcqn, dckvn, proj, proj, q_lat_g, kv_lat_g)


def _qk_prep(q_raw, kv, proj, cos_t, sin_t, gq, gk, *, name, ts=2048):
    t = q_raw.shape[0]
    ts = min(ts, t)

    def body(q_ref, kv_ref, kpe_ref, c_ref, s_ref, gq_ref, gk_ref, qo_ref, ko_ref, vo_ref):
        cs, sn = c_ref[...], s_ref[...]
        nope = _lane() < MNOPE
        qn, _ = _norm96(q_ref[...].astype(F32), None)
        qn = qn * gq_ref[...]
        qo_ref[...] = (qn * cs + _swap_halves(qn) * sn).astype(BF16)
        kvv = kv_ref[...].astype(F32)
        kn, _ = _norm96(jnp.where(nope, kvv, kpe_ref[...].astype(F32)), None)
        kn = kn * gk_ref[...]
        ko_ref[...] = (kn * cs + _swap_halves(kn) * sn).astype(BF16)
        vo_ref[...] = jnp.where(nope, pltpu.roll(kvv, MNOPE, 1), 0.0).astype(BF16)

    hd = pl.BlockSpec((ts, LANE), lambda i, h: (i, h))
    shared = lambda col: pl.BlockSpec((ts, LANE), lambda i, h: (i, col))
    gain = pl.BlockSpec((1, LANE), lambda i, h: (0, 0))
    out = jax.ShapeDtypeStruct((t, MH * LANE), BF16)
    return pl.pallas_call(
        body, name=name, grid=(t // ts, MH),
        in_specs=[hd, hd, shared(OFF_KPE // LANE), shared(0), shared(0), gain, gain],
        out_specs=[hd, hd, hd], out_shape=[out, out, out], compiler_params=_params(2),
    )(q_raw, kv, proj, cos_t, sin_t, gq, gk)


def _qk_prep_bwd(dq, dk, dv, q_raw, kv, proj, cos_t, sin_t, gq, gk, *, name, ts=2048):
    t = q_raw.shape[0]
    ts = min(ts, t)

    def norm_bwd(dy, x, g, dg_ref):
        xh, r = _norm96(x, None)
        dg_ref[...] += jnp.sum(dy * xh, axis=0, keepdims=True)
        dxh = dy * g
        return r * (dxh - xh * (jnp.sum(dxh * xh, axis=-1, keepdims=True) * (1.0 / MQK)))

    def body(dq_ref, dk_ref, dv_ref, q_ref, kv_ref, kpe_ref, c_ref, s_ref, gq_ref, gk_ref,
             dqr_ref, dkv_ref, dkpe_ref, dgq_ref, dgk_ref):
        i, h = pl.program_id(0), pl.program_id(1)

        @pl.when(h == 0)
        def _():
            dkpe_ref[...] = jnp.zeros_like(dkpe_ref)

        @pl.when((h == 0) & (i == 0))
        def _():
            dgq_ref[...] = jnp.zeros_like(dgq_ref)
            dgk_ref[...] = jnp.zeros_like(dgk_ref)

        cs, sn = c_ref[...], s_ref[...]
        lane = _lane()
        nope = lane < MNOPE
        dqv = dq_ref[...].astype(F32)
        dqn = dqv * cs + _swap_halves(dqv * sn)
        dqr_ref[...] = norm_bwd(dqn, q_ref[...].astype(F32), gq_ref[...], dgq_ref).astype(BF16)
        dkv_ = dk_ref[...].astype(F32)
        dkn = dkv_ * cs + _swap_halves(dkv_ * sn)
        kvv = kv_ref[...].astype(F32)
        dkr = norm_bwd(dkn, jnp.where(nope, kvv, kpe_ref[...].astype(F32)), gk_ref[...], dgk_ref)
        dkv_ref[...] = jnp.where(nope, dkr, pltpu.roll(dv_ref[...].astype(F32), MNOPE, 1)).astype(BF16)
        dkpe_ref[...] += jnp.where((lane >= MNOPE) & (lane < MQK), dkr, 0.0)

    hd = pl.BlockSpec((ts, LANE), lambda i, h: (i, h))
    shared = lambda col: pl.BlockSpec((ts, LANE), lambda i, h: (i, col))
    gain = pl.BlockSpec((1, LANE), lambda i, h: (0, 0))
    out = jax.ShapeDtypeStruct((t, MH * LANE), BF16)
    return pl.pallas_call(
        body, name=name, grid=(t // ts, MH),
        in_specs=[hd, hd, hd, hd, hd, shared(OFF_KPE // LANE), shared(0), shared(0), gain, gain],
        out_specs=[hd, hd, shared(0), gain, gain],
        out_shape=[out, out, jax.ShapeDtypeStruct((t, LANE), F32), jax.ShapeDtypeStruct((1, LANE), F32),
                   jax.ShapeDtypeStruct((1, LANE), F32)],
        compiler_params=_params(2),
    )(dq, dk, dv, q_raw, kv, proj, cos_t, sin_t, gq, gk)


_NT = (((1,), (1,)), ((), ()))
_TN = (((0,), (0,)), ((), ()))


SOFTMAX_SCALE = MQK ** -0.5
Q_PRESCALE = SOFTMAX_SCALE * float(np.log2(np.e))


def _attn_weights(q, k_ref, lo, tq):
    row = lax.broadcasted_iota(jnp.int32, (tq, tq), 0) // CHUNK
    col = lax.broadcasted_iota(jnp.int32, (tq, tq), 1) // CHUNK
    sd = lax.dot_general(q, k_ref[pl.ds(lo, tq), :], _NT, preferred_element_type=F32)
    sd = jnp.where(col <= row, sd, -1e30)
    m = jnp.max(sd, axis=-1, keepdims=True)
    if lo:
        so = lax.dot_general(q, k_ref[pl.ds(0, lo), :], _NT, preferred_element_type=F32)
        m = jnp.maximum(m, jnp.max(so, axis=-1, keepdims=True))
        eo = jnp.exp2(so - m)
        ed = jnp.exp2(sd - m)
        return eo, ed, 1.0 / (jnp.sum(eo, axis=-1, keepdims=True) + jnp.sum(ed, axis=-1, keepdims=True))
    ed = jnp.exp2(sd - m)
    return None, ed, 1.0 / jnp.sum(ed, axis=-1, keepdims=True)


def _attn_fwd(q, k, v, *, name, tq=256):
    bsz, s, _ = q.shape
    tq = min(tq, s)

    def body(q_ref, k_ref, v_ref, o_ref):
        for i in range(s // tq):
            lo = i * tq
            eo, ed, inv = _attn_weights(q_ref[pl.ds(lo, tq), :], k_ref, lo, tq)
            o = jnp.dot(ed.astype(BF16), v_ref[pl.ds(lo, tq), :], preferred_element_type=F32)
            if lo:
                o += jnp.dot(eo.astype(BF16), v_ref[pl.ds(0, lo), :], preferred_element_type=F32)
            o_ref[pl.ds(lo, tq), :] = (o * inv).astype(BF16)

    spec = pl.BlockSpec((None, s, LANE), lambda b, h: (b, 0, h))
    return pl.pallas_call(
        body, name=name, grid=(bsz, MH), in_specs=[spec, spec, spec], out_specs=spec,
        out_shape=jax.ShapeDtypeStruct((bsz, s, MH * LANE), BF16), compiler_params=_params(2),
    )(q, k, v)


def _attn_bwd(q, k, v, do, *, name, tq=256):
    bsz, s, _ = q.shape
    tq = min(tq, s)

    def body(q_ref, k_ref, v_ref, do_ref, dq_ref, dk_out_ref, dv_out_ref, dk_ref, dv_ref):
        dk_ref[...] = jnp.zeros_like(dk_ref)
        dv_ref[...] = jnp.zeros_like(dv_ref)
        for i in range(s // tq):
            lo = i * tq
            here, before = pl.ds(lo, tq), pl.ds(0, lo)
            qv, dov = q_ref[here, :], do_ref[here, :]
            eo, ed, inv = _attn_weights(qv, k_ref, lo, tq)
            do_n = (dov.astype(F32) * inv).astype(BF16)
            dv_ref[here, :] += lax.dot_general(ed.astype(BF16), do_n, _TN, preferred_element_type=F32)
            dpd = lax.dot_general(dov, v_ref[here, :], _NT, preferred_element_type=F32)
            delta = jnp.sum(dpd * ed, axis=-1, keepdims=True)
            if lo:
                dv_ref[before, :] += lax.dot_general(eo.astype(BF16), do_n, _TN, preferred_element_type=F32)
                dpo = lax.dot_general(dov, v_ref[before, :], _NT, preferred_element_type=F32)
                delta += jnp.sum(dpo * eo, axis=-1, keepdims=True)
            delta = delta * inv
            r = inv * SOFTMAX_SCALE
            dsd = (ed * (dpd - delta) * r).astype(BF16)
            dq = jnp.dot(dsd, k_ref[here, :], preferred_element_type=F32)
            dk_ref[here, :] += lax.dot_general(dsd, qv, _TN, preferred_element_type=F32)
            if lo:
                dso = (eo * (dpo - delta) * r).astype(BF16)
                dq += jnp.dot(dso, k_ref[before, :], preferred_element_type=F32)
                dk_ref[before, :] += lax.dot_general(dso, qv, _TN, preferred_element_type=F32)
            dq_ref[here, :] = dq.astype(BF16)
        dk_out_ref[...] = (dk_ref[...] * (1.0 / Q_PRESCALE)).astype(BF16)
        dv_out_ref[...] = dv_ref[...].astype(BF16)

    spec = pl.BlockSpec((None, s, LANE), lambda b, h: (b, 0, h))
    out = jax.ShapeDtypeStruct((bsz, s, MH * LANE), BF16)
    return pl.pallas_call(
        body, name=name, grid=(bsz, MH), in_specs=[spec] * 4, out_specs=[spec] * 3, out_shape=[out, out, out],
        scratch_shapes=[pltpu.VMEM((s, LANE), F32), pltpu.VMEM((s, LANE), F32)], compiler_params=_params(2),
    )(q, k, v, do)


def _adamw(w, g, m, v, *, name, tr=256, by_cols=False):
    rows, cols = w.shape
    tr = _tile_rows(rows, tr)

    def body(w_ref, g_ref, m_ref, v_ref, d_ref, nm_ref, nv_ref):
        d_ref[...], nm_ref[...], nv_ref[...] = _adamw_update(w_ref[...], g_ref[...], m_ref[...], v_ref[...])

    spec = pl.BlockSpec((rows, LANE), lambda i: (0, i)) if by_cols else pl.BlockSpec((tr, cols), lambda i: (i, 0))
    out = jax.ShapeDtypeStruct((rows, cols), F32)
    return pl.pallas_call(body, name=name, grid=(cols // LANE if by_cols else rows // tr,), in_specs=[spec] * 4,
                          out_specs=[spec] * 3, out_shape=[out, out, out], compiler_params=_params(1))(w, g, m, v)


def _tile_rows(rows, target):
    if rows <= target:
        return rows
    best = 8
    for t in range(8, target + 1, 8):
        if rows % t == 0:
            best = t
    return best


def _adamw_update(w, g, m, v):
    nm = ADAM_B1 * m + (1.0 - ADAM_B1) * g
    nv = ADAM_B2 * v + (1.0 - ADAM_B2) * (g * g)
    m_hat = nm / (1.0 - ADAM_B1 ** ADAM_STEP)
    v_hat = nv / (1.0 - ADAM_B2 ** ADAM_STEP)
    return -ADAM_LR * (m_hat / (jnp.sqrt(v_hat) + ADAM_EPS) + ADAM_WD * w), nm, nv


def _adamw_halves(w, m, v, mine, theirs, sel, *, name, tr=256):
    rows, cols = w.shape
    tr = _tile_rows(rows // 2, tr)
    nh = rows // 2 // tr

    def body(sel_ref, w_ref, m_ref, v_ref, mine_ref, theirs_ref, g_ref, d_ref, nm_ref, nv_ref):
        lower = pl.program_id(0) < nh
        south = sel_ref[0] == 0
        gv = jnp.where(lower == south, mine_ref[...], theirs_ref[...])
        g_ref[...] = gv
        d_ref[...], nm_ref[...], nv_ref[...] = _adamw_update(w_ref[...], gv, m_ref[...], v_ref[...])

    full = pl.BlockSpec((tr, cols), lambda i, sel_ref: (i, 0))
    half = pl.BlockSpec((tr, cols), lambda i, sel_ref: (i % nh, 0))
    out = jax.ShapeDtypeStruct((rows, cols), F32)
    return pl.pallas_call(
        body, name=name, out_shape=[out] * 4, compiler_params=_params(1),
        grid_spec=pltpu.PrefetchScalarGridSpec(num_scalar_prefetch=1, grid=(rows // tr,),
                                               in_specs=[full, full, full, half, half], out_specs=[full] * 4),
    )(sel, w, m, v, mine, theirs)


def _pair_add(x, sib, sel, *, name, tr=256):
    n, _, rows, cols = x.shape
    tr = _tile_rows(rows, tr)

    def body(sel_ref, x_ref, s_ref, o_ref):
        o_ref[...] = (x_ref[...] + s_ref[...]).astype(BF16)

    spec = pl.BlockSpec((None, tr, cols), lambda j, i, sel_ref: (j, i, 0))
    return pl.pallas_call(
        body, name=name, out_shape=jax.ShapeDtypeStruct((n, rows, cols), BF16), compiler_params=_params(2),
        grid_spec=pltpu.PrefetchScalarGridSpec(
            num_scalar_prefetch=1, grid=(n, rows // tr),
            in_specs=[pl.BlockSpec((None, None, tr, cols), lambda j, i, sel_ref: (j, sel_ref[0], i, 0)), spec],
            out_specs=spec),
    )(sel, x, sib)


def _chip_sum(pair, recv, sel, *, name, tr=256):
    _, rows, cols = pair.shape
    tr = _tile_rows(rows, tr)

    def body(sel_ref, p_ref, r_ref, o_ref):
        acc = p_ref[...].astype(F32)
        for k in range(3):
            acc = acc + r_ref[k].astype(F32)
        o_ref[...] = acc

    return pl.pallas_call(
        body, name=name, out_shape=jax.ShapeDtypeStruct((rows, cols), F32), compiler_params=_params(1),
        grid_spec=pltpu.PrefetchScalarGridSpec(
            num_scalar_prefetch=1, grid=(rows // tr,),
            in_specs=[pl.BlockSpec((None, tr, cols), lambda i, sel_ref: (sel_ref[0], i, 0)),
                      pl.BlockSpec((3, tr, cols), lambda i, sel_ref: (0, i, 0))],
            out_specs=pl.BlockSpec((tr, cols), lambda i, sel_ref: (i, 0))),
    )(sel, pair, recv)


def _me():
    return lax.axis_index("x"), lax.axis_index("y"), lax.axis_index("c")


def _flip(pos, bits):
    x, y, c = pos
    return (x ^ bits[0] if bits[0] else x, y ^ bits[1] if bits[1] else y, c ^ bits[2] if bits[2] else c)


ANY = pl.BlockSpec(memory_space=pl.ANY)


def _all_gather8(xs, *, name):
    n = len(xs)
    flips = [((k >> 2) & 1, (k >> 1) & 1, k & 1) for k in range(1, 8)]

    def body(*refs):
        x_refs, out_refs, (send_sems, recv_sems, local_sems) = refs[:n], refs[n:2 * n], refs[2 * n:]
        me = _me()
        slot = lambda p: 4 * p[0] + 2 * p[1] + p[2]
        copies = []
        for i in range(n):
            mine = pltpu.make_async_copy(x_refs[i], out_refs[i].at[slot(me)], local_sems.at[i])
            mine.start()
            copies.append(mine)
            for k, f in enumerate(flips):
                peer = _flip(me, f)
                sems = dict(send_sem=send_sems.at[7 * i + k], recv_sem=recv_sems.at[7 * i + k], device_id=peer,
                            device_id_type=MESH)
                cp = pltpu.make_async_remote_copy(src_ref=x_refs[i], dst_ref=out_refs[i].at[slot(me)], **sems)
                cp.start()
                copies.append(cp)
                copies.append(pltpu.make_async_remote_copy(src_ref=x_refs[i], dst_ref=out_refs[i].at[slot(peer)], **sems))
        for i in range(n):
            base = i * 15
            copies[base].wait()
            for k in range(7):
                copies[base + 1 + 2 * k].wait_send()
                copies[base + 2 + 2 * k].wait_recv()

    outs = pl.pallas_call(
        body, name=name, in_specs=[ANY] * n, out_specs=[ANY] * n,
        out_shape=[jax.ShapeDtypeStruct((8, *x.shape), x.dtype) for x in xs],
        scratch_shapes=[pltpu.SemaphoreType.DMA((7 * n,)), pltpu.SemaphoreType.DMA((7 * n,)),
                        pltpu.SemaphoreType.DMA((n,))])(*xs)
    return list(outs)


CHIP_FLIPS = [(1, 0, 0), (0, 1, 0), (1, 1, 0)]


def _chip():
    return 2 * lax.axis_index("x") + lax.axis_index("y")


HBM = pl.BlockSpec(memory_space=pltpu.HBM)
SEM = pl.BlockSpec(memory_space=pltpu.SEMAPHORE)
EFFECT = pltpu.SideEffectType.DATAFLOW_SIDE_EFFECTING


def _plan_copies(plan, refs, send_sems, recv_sems):
    return [pltpu.make_async_remote_copy(src_ref=src, dst_ref=dst, send_sem=send_sems.at[k], recv_sem=recv_sems.at[k],
                                         device_id=to, device_id_type=MESH) for k, (src, dst, to) in enumerate(plan(refs))]


def _rdma_start(arrays, n_copies, plan, deps, *, name):
    n, nd = len(arrays), len(deps)

    def body(*refs):
        for cp in _plan_copies(plan, refs[:n], refs[n + nd], refs[n + nd + 1]):
            cp.start()
        refs[-1][...] = jnp.zeros_like(refs[-1])

    outs = pl.pallas_call(
        body, name=name,
        out_shape=(pltpu.SemaphoreType.DMA((n_copies,)), pltpu.SemaphoreType.DMA((n_copies,)),
                   *[pltpu.HBM(a.shape, a.dtype) for a in arrays], jax.ShapeDtypeStruct((8, LANE), F32)),
        in_specs=[HBM] * n + [ANY] * nd, out_specs=(SEM, SEM, *[HBM] * n, pl.BlockSpec(memory_space=pltpu.VMEM)),
        input_output_aliases={i: i + 2 for i in range(n)}, compiler_params=pltpu.CompilerParams(has_side_effects=EFFECT),
    )(*[pltpu.with_memory_space_constraint(a, pltpu.HBM) for a in arrays], *deps)
    return outs[0], outs[1], list(outs[2:2 + n]), outs[-1]


def _rdma_wait(send_sems, recv_sems, arrays, plan, after, *, name):
    n = len(arrays)

    def body(*refs):
        for cp in _plan_copies(plan, refs[:n], refs[n], refs[n + 1]):
            cp.wait_send()
            cp.wait_recv()

    return list(pl.pallas_call(
        body, name=name, out_shape=tuple(pltpu.HBM(a.shape, a.dtype) for a in arrays),
        in_specs=[HBM] * n + [SEM, SEM, ANY], out_specs=tuple([HBM] * n), input_output_aliases={i: i for i in range(n)},
        compiler_params=pltpu.CompilerParams(has_side_effects=EFFECT),
    )(*arrays, send_sems, recv_sems, after))


def _gather_plan(n):
    def plan(refs):
        me = _me()
        slot = 2 * me[0] + me[1]
        return [(refs[i].at[me[2]], refs[n + i].at[slot, me[2]], _flip(me, f)) for i in range(n) for f in CHIP_FLIPS]
    return plan


def _scatter_plan(n):
    def plan(refs):
        me = _me()
        out = []
        for i in range(n):
            for k, f in enumerate(CHIP_FLIPS):
                peer = _flip(me, f)
                out.append((refs[i].at[2 * peer[0] + peer[1]], refs[n + i].at[k], peer))
        return out
    return plan


def _sibling_plan(n, src_of):
    def plan(refs):
        me = _me()
        return [(src_of(refs[i], me[2]), refs[n + i], _flip(me, (0, 0, 1))) for i in range(n)]
    return plan


def _gather8_plan(n):
    def plan(refs):
        me = _me()
        slot = 4 * me[0] + 2 * me[1] + me[2]
        return [(refs[i], refs[n + i].at[slot], _flip(me, ((k >> 2) & 1, (k >> 1) & 1, k & 1)))
                for i in range(n) for k in range(1, 8)]
    return plan


def _pair_fill(lands, *, name):
    n = len(lands)

    def body(*refs):
        in_refs, (send_sems, recv_sems) = refs[:n], refs[2 * n:]
        me = _me()
        sib = _flip(me, (0, 0, 1))
        copies = []
        for i in range(n):
            for k, f in enumerate(CHIP_FLIPS):
                peer = _flip(me, f)
                slot = 2 * peer[0] + peer[1]
                mine, theirs = in_refs[i].at[slot, me[2]], in_refs[i].at[slot, 1 - me[2]]
                cp = pltpu.make_async_remote_copy(src_ref=mine, dst_ref=mine, send_sem=send_sems.at[3 * i + k],
                                                  recv_sem=recv_sems.at[3 * i + k], device_id=sib, device_id_type=MESH)
                cp.start()
                copies.append((cp, pltpu.make_async_remote_copy(
                    src_ref=mine, dst_ref=theirs, send_sem=send_sems.at[3 * i + k], recv_sem=recv_sems.at[3 * i + k],
                    device_id=sib, device_id_type=MESH)))
        for cp, arrival in copies:
            arrival.wait_recv()
            cp.wait_send()

    return list(pl.pallas_call(
        body, name=name, in_specs=[ANY] * n, out_specs=[ANY] * n,
        out_shape=[jax.ShapeDtypeStruct(a.shape, a.dtype) for a in lands], input_output_aliases={i: i for i in range(n)},
        scratch_shapes=[pltpu.SemaphoreType.DMA((3 * n,)), pltpu.SemaphoreType.DMA((3 * n,))])(*lands))


def _own_and_landed(lands, xs):
    chip = _chip()
    return [[jnp.where(chip == j, x, o.reshape(4, *x.shape)[j]) for j in range(4)] for o, x in zip(lands, xs)]


BIG = (("w_in", (D, IN_WIDTH // 4), 1), ("gla_w_o", (D // 4, D), 0), ("mla_w_uq", (MQR, MH * MQK // 4), 1),
       ("mla_w_ukv", (MKVR, MH * (MNOPE + MVD) // 4), 1), ("mla_w_o", (D // 4, D), 0), ("w_out", (D // 4, D), 0),
       ("mlp_w1", (D, DFF // 4), 1), ("mlp_w2", (DFF // 4, D), 0))
ADA_SHARD = (D, 6 * D // 4)
SMALL = (("b_ada", 6 * D), ("norm1_g", D), ("b_merge", 2 * D), ("gla_b_alpha", GH * GDK), ("gla_out_norm_g", GDV),
         ("mla_q_lat_g", MQR), ("mla_kv_lat_g", MKVR), ("mla_qn_g", MQK), ("mla_kn_g", MQK), ("norm2_g", D))


W_IN_SEGMENTS = ((0, 3072, OFF_Q), (3072, 3088, OFF_A), (3088, 3344, OFF_CQ), (3344, 3472, OFF_CKV),
                 (3472, 3504, OFF_KPE + MNOPE), (3504, 5552, OFF_MA))
W_IN_SPLIT = OFF_MA
SMALL_ROWS, SMALL_COLS = 32, 2 * D
W_ALPHA_ROW = 16
LOSS_ROW = 15
SMALL_RED = tuple((n, k) for n, k in SMALL if n != "b_ada")


def _pack_small(grads, d_w_alpha, loss_row, *, name):
    def body(*refs):
        g_refs, wa_ref, loss_ref, out_ref = refs[:-3], refs[-3], refs[-2], refs[-1]
        out_ref[...] = jnp.zeros_like(out_ref)
        for i, ((_, k), g_ref) in enumerate(zip(SMALL_RED, g_refs)):
            out_ref[i:i + 1, 0:k] = g_ref[...]
        out_ref[LOSS_ROW:LOSS_ROW + 1, 0:LANE] = loss_ref[...]
        out_ref[W_ALPHA_ROW:W_ALPHA_ROW + GLR, 0:GH * GDK] = wa_ref[...]

    return pl.pallas_call(body, name=name, out_shape=jax.ShapeDtypeStruct((SMALL_ROWS, SMALL_COLS), F32))(
        *grads, d_w_alpha, loss_row)


def _small_update(gathered, dmod_all, sel, wmv, *, name):
    names = [n for n, _ in SMALL] + ["gla_w_alpha"]
    n_par = len(names)

    def body(sel_ref, g_ref, dmod_ref, *refs):
        in_refs, out_refs, loss_ref, acc = refs[:3 * n_par], refs[3 * n_par:-2], refs[-2], refs[-1]
        total = g_ref[0]
        for j in range(1, 8):
            total = total + g_ref[j]
        acc[...] = total
        loss_ref[...] = acc[LOSS_ROW:LOSS_ROW + 1, 0:LANE]
        row = {n: i for i, (n, _) in enumerate(SMALL_RED)}
        for p, name_p in enumerate(names):
            w_ref, m_ref, v_ref = in_refs[3 * p:3 * p + 3]
            if name_p == "b_ada":
                gv = jnp.sum(dmod_ref[...], axis=0, keepdims=True)
            elif name_p == "gla_w_alpha":
                gv = jnp.zeros((GLR, GDK), F32)
                for j in range(4):
                    blk = acc[W_ALPHA_ROW:W_ALPHA_ROW + GLR, j * GDK:(j + 1) * GDK]
                    gv = gv + jnp.where(sel_ref[0] == j, blk, 0.0)
            else:
                gv = acc[row[name_p]:row[name_p] + 1, 0:w_ref.shape[1]]
            o = out_refs[4 * p:4 * p + 4]
            o[0][...] = gv
            o[1][...], o[2][...], o[3][...] = _adamw_update(w_ref[...], gv, m_ref[...], v_ref[...])

    flat = [a for t in wmv for a in t]
    out_shape = [jax.ShapeDtypeStruct(t[0].shape, F32) for t in wmv for _ in range(4)]
    out_shape.append(jax.ShapeDtypeStruct((1, LANE), F32))
    vmem = pl.BlockSpec(memory_space=pltpu.VMEM)
    outs = pl.pallas_call(
        body, name=name, out_shape=out_shape, in_specs=[pl.BlockSpec(memory_space=pltpu.SMEM), vmem, vmem] + [vmem] * len(flat),
        out_specs=[vmem] * len(out_shape), scratch_shapes=[pltpu.VMEM((SMALL_ROWS, SMALL_COLS), F32)],
    )(sel, gathered, dmod_all, *flat)
    return {n: tuple(outs[4 * p:4 * p + 4]) for p, n in enumerate(names)}, outs[-1][0, 0]


def _full_weights(gathered):
    w = {name: jnp.concatenate(gathered[name], axis=axis) for name, _, axis in BIG if name in gathered and name != "w_in"}
    if "w_in" in gathered:
        shards = gathered["w_in"]
        zeros = lambda n: [jnp.zeros((D, n), shards[0].dtype)]

        def cols(a, b):
            width = IN_WIDTH // 4
            return [shards[j][:, max(a, j * width) - j * width:min(b, (j + 1) * width) - j * width]
                    for j in range(4) if max(a, j * width) < min(b, (j + 1) * width)]

        parts = []
        for a, b, at in sorted(W_IN_SEGMENTS, key=lambda seg: seg[2]):
            have = sum(p.shape[1] for p in parts)
            parts += (zeros(at - have) if at > have else []) + cols(a, b)
        w["w_in"] = jnp.concatenate(parts + zeros(PW - sum(p.shape[1] for p in parts)), axis=1)
    if "mla_w_uq" in w:
        w["mla_w_uq"] = jnp.pad(w["mla_w_uq"].reshape(MQR, MH, MQK), ((0, 0), (0, 0), (0, LANE - MQK))).reshape(MQR, MH * LANE)
    if "mla_w_o" in w:
        w["mla_w_o"] = jnp.pad(w["mla_w_o"].reshape(MH, MVD, D), ((0, 0), (0, LANE - MVD), (0, 0))).reshape(MH * LANE, D)
    return w


def _grad_slots(g):
    g = dict(g)
    out = {}
    if "w_in" in g:
        g_lo, g_hi = g.pop("w_in")
        take = lambda at, lo, hi: g_lo[:, at + lo:at + hi] if at < W_IN_SPLIT else g_hi[:, at - W_IN_SPLIT + lo:at - W_IN_SPLIT + hi]
        width = IN_WIDTH // 4
        slots = []
        for j in range(4):
            lo, hi = j * width, (j + 1) * width
            slots.append(jnp.concatenate([take(at, max(lo, a) - a, min(hi, b) - a)
                                          for a, b, at in W_IN_SEGMENTS if max(lo, a) < min(hi, b)], axis=1))
        out["w_in"] = jnp.stack(slots).reshape(4, 2, D // 2, width)
    if "mla_w_uq" in g:
        g["mla_w_uq"] = g["mla_w_uq"].reshape(MQR, MH, LANE)[:, :, :MQK].reshape(MQR, MH * MQK)
    if "mla_w_o" in g:
        g["mla_w_o"] = g["mla_w_o"].reshape(MH, LANE, D)[:, :MVD].reshape(MH * MVD, D)
    for name, (rows, cols), axis in BIG:
        if name not in g:
            continue
        a = g[name]
        a = a.reshape(4, rows, cols) if axis == 0 else jnp.transpose(a.reshape(rows, 4, cols), (1, 0, 2))
        out[name] = a.reshape(4, 2, rows // 2, cols)
    return out


def _rope_tables(positions):
    freqs = ROPE_THETA ** (-jnp.arange(0, MROPE, 2, dtype=F32) / MROPE)
    lane = np.arange(LANE)
    in_rope = (lane >= MNOPE) & (lane < MQK)
    freq_lane = jnp.where(in_rope, freqs[(lane - MNOPE) % (MROPE // 2)], 0.0)
    sign = np.where(in_rope, np.where(lane < MNOPE + MROPE // 2, -1.0, 1.0), 0.0).astype(np.float32)
    ang = positions.astype(F32).reshape(-1, 1) * freq_lane[None, :]
    return jnp.cos(ang), jnp.sin(ang) * sign[None, :]


def _local_step(x, positions, mod, target, w, small, more_weights=None, on_grads=None):
    kept = {}
    if on_grads is None:
        on_grads = lambda tag, grads, after: kept.update(grads)
    bsz, s, _ = x.shape
    t = bsz * s
    tt = _tile(t, 1024)
    shift1, scale1, gate1, shift2, scale2, gate2 = [mod[:, None, i * D:(i + 1) * D] for i in range(6)]
    cos_t, sin_t = _rope_tables(positions)
    w_alpha_p = jnp.pad(small["gla_w_alpha"], ((0, LANE - GLR), (0, 0)))
    gq = jnp.pad(small["mla_qn_g"], ((0, 0), (0, LANE - MQK)))
    gk = jnp.pad(small["mla_kn_g"], ((0, 0), (0, LANE - MQK)))
    flat2 = lambda a: a.reshape(t, a.shape[-1])
    bsd = lambda a: a.reshape(bsz, s, a.shape[-1])

    h = _norm_mod(x, small["norm1_g"], scale1, shift1, name="norm1")
    if callable(w):
        w = w(h)
    proj = _mm(flat2(h), w["w_in"], name="proj", tn=1152, out_dtype=BF16)
    proj3 = bsd(proj)
    o, o_gated, states = _gla_fwd(proj3, w_alpha_p, small["gla_b_alpha"], small["gla_out_norm_g"], name="gla_fwd")
    if more_weights is not None:
        w = {**w, **more_weights(o_gated)}
    y_a = _mm(flat2(o_gated), w["gla_w_o"], name="gla_out", out_dtype=BF16)
    cq_n, ckv_n = _lat_norm(proj, small["mla_q_lat_g"], small["mla_kv_lat_g"], name="lat_norm")
    q_raw = _mm(cq_n, w["mla_w_uq"], name="mla_uq", out_dtype=BF16)
    kv = _mm(ckv_n, w["mla_w_ukv"], name="mla_ukv", out_dtype=BF16)
    qf, kf, vf = _qk_prep(q_raw, kv, proj, cos_t, sin_t, gq * Q_PRESCALE, gk, name="qk_prep")
    o_attn = _attn_fwd(bsd(qf), bsd(kf), bsd(vf), name="attn_fwd")
    y_b = _mm(flat2(o_attn), w["mla_w_o"], name="mla_out", out_dtype=BF16)
    mixed_in = _merge_fwd(proj3, small["b_merge"], bsd(y_a), bsd(y_b), name="merge_fwd")
    mixed = _mm(flat2(mixed_in), w["w_out"], name="w_out")
    x1, h2 = _resid_norm_mod(x, bsd(mixed), gate1, small["norm2_g"], scale2, shift2, name="norm2")

    def sqrelu(acc, ex, outs):
        r = jnp.maximum(acc, 0.0)
        outs[0][...] = (r * r).astype(BF16)

    r = _mm(flat2(h2), w["mlp_w1"], name="mlp1", epilogue=sqrelu, out_shape=jax.ShapeDtypeStruct((t, DFF), BF16),
            out_specs=_tile_spec(tt, 1024))
    ff = _mm(r, w["mlp_w2"], name="mlp2")
    dy, dff, dgate2, loss_part = _loss_head(x1, bsd(ff), gate2, target, name="loss_head")

    g = {}

    def relu2_bwd(acc, ex, outs):
        outs[0][...] = (acc * (2.0 * jnp.sqrt(ex[0][...].astype(F32)))).astype(BF16)

    dff2 = flat2(dff)
    da1 = _mm(dff2, w["mlp_w2"], tb=True, name="mlp2_dx", epilogue=relu2_bwd, extras=(r,),
              extra_specs=(_tile_spec(tt, 1024),), out_shape=jax.ShapeDtypeStruct((t, DFF), BF16),
              out_specs=_tile_spec(tt, 1024))
    g["mlp_w2"] = _mm(r, dff2, ta=True, name="mlp2_dw")
    dh2 = _mm(da1, w["mlp_w1"], tb=True, name="mlp1_dx")
    g["mlp_w1"] = _mm(flat2(h2), da1, ta=True, name="mlp1_dw")
    token = on_grads("mlp", {n: g.pop(n) for n in ("mlp_w2", "mlp_w1")}, dh2)
    if token is not None:
        gate1 = gate1 + token[0, 0]
    dx1, dscale2, dshift2, dg2, dgate1, dmixed = _norm_mod_bwd(
        bsd(dh2), x1, dy, small["norm2_g"], scale2, gate1, bsd(mixed), name="norm2_bwd")
    dmixed2 = flat2(dmixed)
    dmi = _mm(dmixed2, w["w_out"], tb=True, name="w_out_dx", out_dtype=BF16)
    g["w_out"] = _mm(flat2(mixed_in), dmixed2, ta=True, name="w_out_dw")
    dy_a, dy_b, dl_a, dl_b, db_a, db_b = _merge_bwd(bsd(dmi), proj3, small["b_merge"], bsd(y_a), bsd(y_b), name="merge_bwd")
    dy_a2, dy_b2 = flat2(dy_a), flat2(dy_b)
    dog = _mm(dy_a2, w["gla_w_o"], tb=True, name="gla_out_dx", out_dtype=BF16)
    g["gla_w_o"] = _mm(flat2(o_gated), dy_a2, ta=True, name="gla_out_dw")
    dq_g, dk_g, dv_g, dg_g, dlog, db_alpha, d_ong = _gla_bwd(
        bsd(dog), o, states, proj3, w_alpha_p, small["gla_b_alpha"], small["gla_out_norm_g"], name="gla_bwd")
    dlog2 = flat2(dlog)
    da_p = _mm(dlog2, w_alpha_p, tb=True, out_dtype=BF16, name="alpha_dx")
    d_w_alpha = _mm(proj[:, OFF_A:OFF_A + LANE], dlog2, ta=True, name="alpha_dw")[:GLR]
    do_attn = _mm(dy_b2, w["mla_w_o"], tb=True, out_dtype=BF16, name="mla_out_dx")
    g["mla_w_o"] = _mm(flat2(o_attn), dy_b2, ta=True, name="mla_out_dw")
    dqf, dkf, dvf = _attn_bwd(bsd(qf), bsd(kf), bsd(vf), bsd(do_attn), name="attn_bwd")
    dq_raw, dkv, dkpe, dgq, dgk = _qk_prep_bwd(flat2(dqf), flat2(dkf), flat2(dvf), q_raw, kv, proj, cos_t, sin_t, gq, gk,
                                                name="qk_prep_bwd")
    dcq_n = _mm(dq_raw, w["mla_w_uq"], tb=True, name="mla_uq_dx")
    g["mla_w_uq"] = _mm(cq_n, dq_raw, ta=True, name="mla_uq_dw")
    dckv_n = _mm(dkv, w["mla_w_ukv"], tb=True, name="mla_ukv_dx")
    g["mla_w_ukv"] = _mm(ckv_n, dkv, ta=True, name="mla_ukv_dw")
    token = on_grads("mix", {n: g.pop(n) for n in ("w_out", "gla_w_o", "mla_w_o", "mla_w_uq", "mla_w_ukv")}, dckv_n)
    q_lat_g = small["mla_q_lat_g"] if token is None else small["mla_q_lat_g"] + token[0:1, 0:1]
    dcq, dckv, dg_qlat, dg_kvlat = _lat_norm_bwd(dcq_n, dckv_n, proj, q_lat_g, small["mla_kv_lat_g"],
                                                  name="lat_norm_bwd")
    pieces = [(flat2(dq_g), OFF_Q), (flat2(dk_g), OFF_K), (flat2(dv_g), OFF_V), (flat2(dg_g), OFF_G),
              (flat2(dl_a), OFF_MA), (flat2(dl_b), OFF_MB), (dcq, OFF_CQ), (dckv, OFF_CKV), (da_p, OFF_A), (dkpe, OFF_KPE)]
    hb = flat2(h)
    g_w_in = (_pieces_dw(hb, [p for p, off in pieces if off < W_IN_SPLIT], name="proj_dw_a"),
              _pieces_dw(hb, [p for p, off in pieces if off >= W_IN_SPLIT], name="proj_dw_b"))
    token = on_grads("in", {"w_in": g_w_in}, g_w_in[1])
    after = jnp.zeros((8, LANE), F32) if token is None else token
    dh = _pieces_dx(pieces, w["w_in"], after, name="proj_dx")
    token = on_grads("dx", {}, dh)
    if token is not None:
        scale1 = scale1 + token[0, 0]
    grad_x, dscale1, dshift1, dg1 = _norm_mod_bwd(bsd(dh), x, dx1, small["norm1_g"], scale1, name="norm1_bwd")

    dmod = jnp.concatenate([dshift1, dscale1, dgate1, dshift2, dscale2, dgate2], axis=-1).reshape(bsz, 6 * D)
    gs = {"norm1_g": dg1, "b_merge": jnp.concatenate([db_a, db_b], axis=1), "gla_b_alpha": db_alpha,
          "gla_out_norm_g": d_ong, "mla_q_lat_g": dg_qlat, "mla_kv_lat_g": dg_kvlat, "mla_qn_g": dgq[:, :MQK],
          "mla_kn_g": dgk[:, :MQK], "norm2_g": dg2}
    return loss_part[0, 0], grad_x, dmod, {**kept, **g}, gs, d_w_alpha


def kernel(x, c, positions, w_ada, b_ada, norm1_g, w_in, b_merge, gla_w_alpha, gla_b_alpha, gla_out_norm_g, gla_w_o, mla_q_lat_g, mla_w_uq, mla_kv_lat_g, mla_w_ukv, mla_qn_g, mla_kn_g, mla_w_o, w_out, norm2_g, mlp_w1, mlp_w2, loss_target, m_w_ada, m_b_ada, m_norm1_g, m_w_in, m_b_merge, m_gla_w_alpha, m_gla_b_alpha, m_gla_out_norm_g, m_gla_w_o, m_mla_q_lat_g, m_mla_w_uq, m_mla_kv_lat_g, m_mla_w_ukv, m_mla_qn_g, m_mla_kn_g, m_mla_w_o, m_w_out, m_norm2_g, m_mlp_w1, m_mlp_w2, v_w_ada, v_b_ada, v_norm1_g, v_w_in, v_b_merge, v_gla_w_alpha, v_gla_b_alpha, v_gla_out_norm_g, v_gla_w_o, v_mla_q_lat_g, v_mla_w_uq, v_mla_kv_lat_g, v_mla_w_ukv, v_mla_qn_g, v_mla_kn_g, v_mla_w_o, v_w_out, v_norm2_g, v_mlp_w1, v_mlp_w2):
    args = dict(locals())
    names_big = [n for n, _, _ in BIG]
    names_small = [n for n, _ in SMALL]
    bsz = x.shape[0]
    ax, ay, ac = lax.axis_index("x"), lax.axis_index("y"), lax.axis_index("c")
    chip = 2 * ax + ay
    dev = 2 * chip + ac

    small = {n: args[n] for n in names_small}
    sel_c = jnp.reshape(ac, (1,)).astype(jnp.int32)
    sel_chip = jnp.reshape(chip, (1,)).astype(jnp.int32)
    c_all, w_alpha_all = _all_gather8([c, gla_w_alpha[0]], name="comm_c_alpha")
    small["gla_w_alpha"] = jnp.concatenate([w_alpha_all[2 * j] for j in range(4)], axis=1)
    c_all = c_all.reshape(8 * bsz, D)

    shards = {n: args[n][0].astype(BF16) for n in names_big}
    halves_of = lambda names: [shards[n].reshape(2, shards[n].shape[0] // 2, shards[n].shape[1]) for n in names]

    def gather_start(names, deps, tag):
        xs = halves_of(names)
        lands = [lax.empty((4, *xh.shape), BF16) for xh in xs]
        plan = _gather_plan(len(names))
        return names, plan, _rdma_start(xs + lands, 3 * len(names), plan, deps, name="comm_weights_start_" + tag)

    def gather_finish(started, after, tag):
        names, plan, sems = started
        arrs = _rdma_wait(sems[0], sems[1], sems[2], plan, after, name="comm_weights_wait_" + tag)
        filled = _pair_fill(arrs[len(names):], name="comm_weights_pair_" + tag)
        own = [a.reshape(shards[n].shape) for n, a in zip(names, arrs)]
        return _full_weights(dict(zip(names, _own_and_landed(filled, own))))


    def add_bias(acc, ex, outs):
        outs[0][...] = acc + ex[0][...]

    silu = lambda v: v * _sigmoid(v)
    b_ada_mine = lax.dynamic_slice(b_ada, (0, chip * ADA_SHARD[1]), (1, ADA_SHARD[1]))
    mod_part = _mm(c_all, w_ada[0], name="ada", tn=512, a_fn=silu, epilogue=add_bias, extras=(b_ada_mine,),
                   extra_specs=(pl.BlockSpec((1, 512), lambda i, j, k: (0, j)),),
                   out_shape=jax.ShapeDtypeStruct((8 * bsz, ADA_SHARD[1]), F32), out_specs=_tile_spec(8 * bsz, 512))
    mod_all = _all_gather8([mod_part], name="comm_mod")[0]
    mod_rows = lax.dynamic_slice(mod_all, (0, dev * bsz, 0), (8, bsz, ADA_SHARD[1]))
    mod = jnp.concatenate([mod_rows[2 * j] for j in range(4)], axis=1)
    first = gather_start(["w_in"], (mod,), "in")
    rest = gather_start([n for n in names_big if n != "w_in"], (mod, first[2][3]), "rest")
    mod = mod + rest[2][3][0, 0]
    w_in_after = lambda after: gather_finish(first, after, "in")
    more_weights = lambda after: gather_finish(rest, after, "rest")

    stage = {}

    def begin(tag, names, arrays, lands, n_copies, plan, what):
        stage[tag] = (names, plan, _rdma_start(arrays + lands, n_copies, plan, (), name=f"comm_{what}_start_{tag}"))
        return stage[tag][2][3]

    def landed(tag, after, what):
        names, plan, sems = stage[tag]
        arrs = _rdma_wait(sems[0], sems[1], sems[2], plan, after, name=f"comm_{what}_wait_{tag}")
        return names, arrs[:len(arrs) // 2], arrs[len(arrs) // 2:]

    def swap_start(tag, grads):
        names = list(grads)
        parts = [_grad_slots(grads)[n] for n in names]
        lands = [lax.empty((4, *p.shape[2:]), F32) for p in parts]
        return begin(tag, names, parts, lands, len(names), _sibling_plan(len(names), lambda r, c: r.at[:, 1 - c]), "pair_sum")

    def scatter_start(tag, after):
        names, parts, sib_halves = landed(tag, after, "pair_sum")
        pairs = [_pair_add(p, s, sel_c, name="pair_add_" + n) for n, p, s in zip(names, parts, sib_halves)]
        recvs = [lax.empty((3, *p.shape[1:]), BF16) for p in pairs]
        return begin(tag, names, pairs, recvs, 3 * len(names), _scatter_plan(len(names)), "scatter")

    def join_start(tag, after):
        names, pairs, recvs = landed(tag, after, "scatter")
        halves = [_chip_sum(p, r, sel_chip, name="chip_sum_" + n) for n, p, r in zip(names, pairs, recvs)]
        lands = [lax.empty(h.shape, F32) for h in halves]
        return begin(tag, names, halves, lands, len(names), _sibling_plan(len(names), lambda r, c: r), "pair_join")

    def reduce_step(tag, grads, after):
        if tag == "mlp":
            return swap_start("mlp", grads)
        if tag == "mix":
            return scatter_start("mlp", after) + swap_start("mix", grads)
        if tag == "in":
            return scatter_start("mix", after) + swap_start("in", grads)
        return scatter_start("in", after)

    loss_part, grad_x, dmod, g, gs, d_w_alpha = _local_step(x, positions, mod, loss_target, w_in_after, small,
                                                            more_weights, reduce_step)

    assert not g, list(g)
    gs_packed = _pack_small([gs[n] for n, _ in SMALL_RED], d_w_alpha, jnp.full((1, LANE), loss_part, F32),
                            name="pack_small")
    small_lands = [lax.empty((8, *a.shape), F32) for a in (dmod, gs_packed)]
    begin("small", ["dmod", "small"], [dmod, gs_packed], small_lands, 7 * 2, _gather8_plan(2), "gather8")

    res = {}

    def finish(tag, after):
        names, halves, theirs = landed(tag, after, "pair_join")
        for n, mine, other in zip(names, halves, theirs):
            if n == "w_in":
                south = ac == 0
                g_t = jnp.concatenate([jnp.where(south, mine, other), jnp.where(south, other, mine)], axis=0).T
                outs = _adamw(w_in[0].T, g_t, m_w_in[0].T, v_w_in[0].T, name="adamw_w_in", by_cols=True)
                res[n] = tuple(a.T for a in (g_t, *outs))
            else:
                res[n] = _adamw_halves(args[n][0], args["m_" + n][0], args["v_" + n][0], mine, other, sel_c,
                                       name="adamw_" + n)
        return res[names[-1]][1]

    join_start("mlp", grad_x)
    join_start("mix", grad_x)
    done = finish("mix", finish("mlp", grad_x))

    _, (dmod_own, gs_own), (dmod_all, gs_all) = landed("small", done, "gather8")
    dmod_all = lax.dynamic_update_slice(dmod_all, dmod_own[None], (dev, 0, 0)).reshape(8 * bsz, 6 * D)
    gs_all = lax.dynamic_update_slice(gs_all, gs_own[None], (dev, 0, 0))
    dmod_mine = lax.dynamic_slice(dmod_all, (0, chip * ADA_SHARD[1]), (8 * bsz, ADA_SHARD[1]))
    g_w_ada = _mm(c_all, dmod_mine, ta=True, a_fn=silu, name="ada_dw")
    wmv = [(args[n], args["m_" + n], args["v_" + n]) for n in names_small]
    wmv.append((gla_w_alpha[0], m_gla_w_alpha[0], v_gla_w_alpha[0]))
    res_small, loss_sum = _small_update(gs_all, dmod_all, sel_chip, wmv, name="small_update")
    res.update(res_small)
    loss = loss_sum * (0.5 / D)
    join_start("in", g_w_ada)
    res["w_ada"] = (g_w_ada, *_adamw(w_ada[0], g_w_ada, m_w_ada[0], v_w_ada[0], name="adamw_w_ada"))
    finish("in", res["w_ada"][1])

    order = ["w_ada", "b_ada", "norm1_g", "w_in", "b_merge", "gla_w_alpha", "gla_b_alpha", "gla_out_norm_g", "gla_w_o",
             "mla_q_lat_g", "mla_w_uq", "mla_kv_lat_g", "mla_w_ukv", "mla_qn_g", "mla_kn_g", "mla_w_o", "w_out",
             "norm2_g", "mlp_w1", "mlp_w2"]
    named = lambda k: [res[n][k].reshape(args[n].shape) for n in order]
    return (loss, grad_x, *named(0), *named(1), *named(2), *named(3))
```

```python
import jax
import jax.numpy as jnp
import numpy as np
from jax import lax
from jax.experimental import pallas as pl
from jax.experimental.pallas import tpu as pltpu

F32 = jnp.float32
BF16 = jnp.bfloat16
MESH = pl.DeviceIdType.MESH

D = 1024
CHUNK = 64
EPS = 1e-6
GH, GDK, GDV, GLR, GTAU = 4, 128, 256, 16, 16.0
MH, MQR, MKVR, MNOPE, MROPE, MVD = 16, 256, 128, 64, 32, 64
MQK = MNOPE + MROPE
DFF = 4 * D
ROPE_THETA = 10000.0
IN_WIDTH = 5552
LANE = 128
OFF_Q, OFF_K, OFF_V, OFF_G, OFF_MA, OFF_MB, OFF_CQ, OFF_CKV, OFF_A, OFF_KPE, PW = (
    0, 512, 1024, 2048, 3072, 4096, 5120, 5376, 5504, 5632, 5760)
ADAM_LR, ADAM_B1, ADAM_B2, ADAM_EPS, ADAM_WD, ADAM_STEP = 0.001, 0.9, 0.999, 1e-08, 0.01, 10
VMEM_LIMIT = 48 * 1024 * 1024


def _params(n_axes):
    return pltpu.CompilerParams(dimension_semantics=("arbitrary",) * n_axes, vmem_limit_bytes=VMEM_LIMIT)


def _tile(n, target):
    if n <= target:
        return n
    best = None
    for t in range(LANE, target + 1, LANE):
        if n % t == 0:
            best = t
    assert best is not None, (n, target)
    return best


def _sigmoid(x):
    return 1.0 / (1.0 + jnp.exp(-x))


MM_VMEM_BUDGET = 36 * 1024 * 1024


def _mm(a, b, *, name, ta=False, tb=False, out_dtype=F32, tm=1024, tn=1024, tk=4096,
        epilogue=None, extras=(), extra_specs=(), out_shape=None, out_specs=None, a_fn=None, tile_index=False):
    if ta:
        kdim, m = a.shape
    else:
        m, kdim = a.shape
    if tb:
        n, k2 = b.shape
    else:
        k2, n = b.shape
    assert kdim == k2, (a.shape, b.shape)
    tm, tn, tk = _tile(m, tm), _tile(n, tn), _tile(kdim, tk)
    tiles = lambda rows: 2 * (rows * tk * a.dtype.itemsize + tk * tn * b.dtype.itemsize + rows * tn * 4) + rows * tn * 4
    while out_shape is None and tiles(tm) > MM_VMEM_BUDGET and tm % 256 == 0:
        tm //= 2
    nk = kdim // tk
    a_spec = pl.BlockSpec((tk, tm), lambda i, j, k: (k, i)) if ta else pl.BlockSpec((tm, tk), lambda i, j, k: (i, k))
    b_spec = pl.BlockSpec((tn, tk), lambda i, j, k: (j, k)) if tb else pl.BlockSpec((tk, tn), lambda i, j, k: (k, j))
    dims = (((0 if ta else 1,), (1 if tb else 0,)), ((), ()))
    ne = len(extras)
    if out_shape is None:
        out_shape = jax.ShapeDtypeStruct((m, n), out_dtype)
        out_specs = pl.BlockSpec((tm, tn), lambda i, j, k: (i, j))
    n_out = len(out_shape) if isinstance(out_shape, (list, tuple)) else 1
    in_place = epilogue is None and n_out == 1 and out_shape.dtype == F32
    scratch = [] if (nk == 1 or in_place) else [pltpu.VMEM((tm, tn), F32)]

    def body(a_ref, b_ref, *rest):
        ex, outs = rest[:ne], rest[ne:ne + n_out]
        av = a_ref[...] if a_fn is None else a_fn(a_ref[...])
        prod = lax.dot_general(av.astype(BF16), b_ref[...].astype(BF16), dims, preferred_element_type=F32)
        row_tile = pl.program_id(0) if tile_index else None

        def finish(val):
            if epilogue is None:
                outs[0][...] = val.astype(outs[0].dtype)
            elif tile_index:
                epilogue(val, ex, outs, row_tile)
            else:
                epilogue(val, ex, outs)

        if nk == 1:
            finish(prod)
            return
        k = pl.program_id(2)
        acc = outs[0] if in_place else rest[-1]

        @pl.when(k == 0)
        def _():
            acc[...] = prod

        @pl.when(k > 0)
        def _():
            acc[...] += prod

        if not in_place:
            @pl.when(k == nk - 1)
            def _():
                finish(acc[...])

    return pl.pallas_call(
        body, name=name, grid=(m // tm, n // tn, nk),
        in_specs=[a_spec, b_spec, *extra_specs], out_specs=out_specs, out_shape=out_shape,
        scratch_shapes=scratch, compiler_params=_params(3),
    )(a, b, *extras)


def _tile_spec(tm, tn):
    return pl.BlockSpec((tm, tn), lambda i, j, k: (i, j))


def _pieces_dx(pieces, w, after, *, name, tm=512):
    t = pieces[0][0].shape[0]
    tm = _tile(t, tm)
    npc = len(pieces)

    def body(*refs):
        p_refs, w_ref, out_ref = refs[:npc], refs[npc], refs[-1]
        acc = None
        for (arr, off), p_ref in zip(pieces, p_refs):
            part = lax.dot_general(p_ref[...].astype(BF16), w_ref[:, off:off + arr.shape[1]], _NT,
                                   preferred_element_type=F32)
            acc = part if acc is None else acc + part
        out_ref[...] = acc

    return pl.pallas_call(
        body, name=name, grid=(t // tm,),
        in_specs=[pl.BlockSpec((tm, arr.shape[1]), lambda i: (i, 0)) for arr, _ in pieces]
        + [pl.BlockSpec(w.shape, lambda i: (0, 0)), pl.BlockSpec((8, LANE), lambda i: (0, 0))],
        out_specs=pl.BlockSpec((tm, w.shape[0]), lambda i: (i, 0)),
        out_shape=jax.ShapeDtypeStruct((t, w.shape[0]), F32), compiler_params=_params(1),
    )(*[arr for arr, _ in pieces], w, after)


def _pieces_dw(h, pieces, *, name, tk=1024):
    t, d = h.shape
    tk = _tile(t, tk)
    widths = [p.shape[1] for p in pieces]
    starts = [sum(widths[:i]) for i in range(len(pieces))]

    def body(h_ref, *refs):
        p_refs, out_ref = refs[:-1], refs[-1]
        first = pl.program_id(0) == 0
        hv = h_ref[...]
        for p_ref, start, width in zip(p_refs, starts, widths):
            part = lax.dot_general(hv, p_ref[...].astype(BF16), _TN, preferred_element_type=F32)
            cols = slice(start, start + width)

            @pl.when(first)
            def _():
                out_ref[:, cols] = part

            @pl.when(jnp.logical_not(first))
            def _():
                out_ref[:, cols] += part

    return pl.pallas_call(
        body, name=name, grid=(t // tk,),
        in_specs=[pl.BlockSpec((tk, d), lambda k: (k, 0))] + [pl.BlockSpec((tk, wd), lambda k: (k, 0)) for wd in widths],
        out_specs=pl.BlockSpec((d, sum(widths)), lambda k: (0, 0)),
        out_shape=jax.ShapeDtypeStruct((d, sum(widths)), F32), compiler_params=_params(1),
    )(h, *pieces)


def _rms(x, g):
    r = lax.rsqrt(jnp.mean(x * x, axis=-1, keepdims=True) + EPS)
    return x * r, r


def _row_spec(ts, width, col=0):
    return pl.BlockSpec((None, ts, width), lambda b, i: (b, i, col))


def _vec_spec(width):
    return pl.BlockSpec((None, 1, width), lambda b, i: (b, 0, 0))


def _gain_spec(width):
    return pl.BlockSpec((1, width), lambda b, i: (0, 0))


def _norm_mod(x, g, scale, shift, *, name, ts=512):
    bsz, s, d = x.shape
    ts = min(ts, s)

    def body(x_ref, g_ref, sc_ref, sh_ref, h_ref):
        xh, _ = _rms(x_ref[...], None)
        h_ref[...] = ((xh * g_ref[...]) * (1.0 + sc_ref[...]) + sh_ref[...]).astype(BF16)

    return pl.pallas_call(
        body, name=name, grid=(bsz, s // ts),
        in_specs=[_row_spec(ts, d), _gain_spec(d), _vec_spec(d), _vec_spec(d)],
        out_specs=_row_spec(ts, d), out_shape=jax.ShapeDtypeStruct((bsz, s, d), BF16),
        compiler_params=_params(2),
    )(x, g, scale, shift)


def _resid_norm_mod(x, mixed, gate, g, scale, shift, *, name, ts=512):
    bsz, s, d = x.shape
    ts = min(ts, s)

    def body(x_ref, mx_ref, gt_ref, g_ref, sc_ref, sh_ref, x1_ref, h_ref):
        x1 = x_ref[...] + gt_ref[...] * mx_ref[...]
        x1_ref[...] = x1
        xh, _ = _rms(x1, None)
        h_ref[...] = ((xh * g_ref[...]) * (1.0 + sc_ref[...]) + sh_ref[...]).astype(BF16)

    return pl.pallas_call(
        body, name=name, grid=(bsz, s // ts),
        in_specs=[_row_spec(ts, d), _row_spec(ts, d), _vec_spec(d), _gain_spec(d), _vec_spec(d), _vec_spec(d)],
        out_specs=[_row_spec(ts, d), _row_spec(ts, d)],
        out_shape=[jax.ShapeDtypeStruct((bsz, s, d), F32), jax.ShapeDtypeStruct((bsz, s, d), BF16)],
        compiler_params=_params(2),
    )(x, mixed, gate, g, scale, shift)


def _norm_mod_bwd(dh, xin, resid, g, scale, gate=None, mixed=None, *, name, ts=512):
    bsz, s, d = xin.shape
    ts = min(ts, s)
    gated = gate is not None

    def body(*refs):
        if gated:
            dh_ref, x_ref, rs_ref, g_ref, sc_ref, gt_ref, mx_ref, dx_ref, dsc_ref, dsh_ref, dg_ref, dgt_ref, dmx_ref = refs
        else:
            dh_ref, x_ref, rs_ref, g_ref, sc_ref, dx_ref, dsc_ref, dsh_ref, dg_ref = refs
        b, i = pl.program_id(0), pl.program_id(1)

        @pl.when(i == 0)
        def _():
            dsc_ref[...] = jnp.zeros_like(dsc_ref)
            dsh_ref[...] = jnp.zeros_like(dsh_ref)
            if gated:
                dgt_ref[...] = jnp.zeros_like(dgt_ref)

        @pl.when((i == 0) & (b == 0))
        def _():
            dg_ref[...] = jnp.zeros_like(dg_ref)

        dh_v, gv = dh_ref[...], g_ref[...]
        xh, r = _rms(x_ref[...], None)
        dsc_ref[...] += jnp.sum(dh_v * (xh * gv), axis=0, keepdims=True)
        dsh_ref[...] += jnp.sum(dh_v, axis=0, keepdims=True)
        dn = dh_v * (1.0 + sc_ref[...])
        dg_ref[...] += jnp.sum(dn * xh, axis=0, keepdims=True)
        dxh = dn * gv
        dx = rs_ref[...] + r * (dxh - xh * jnp.mean(dxh * xh, axis=-1, keepdims=True))
        dx_ref[...] = dx
        if gated:
            dgt_ref[...] += jnp.sum(dx * mx_ref[...], axis=0, keepdims=True)
            dmx_ref[...] = (dx * gt_ref[...]).astype(BF16)

    ins = [dh, xin, resid, g, scale]
    in_specs = [_row_spec(ts, d), _row_spec(ts, d), _row_spec(ts, d), _gain_spec(d), _vec_spec(d)]
    out_specs = [_row_spec(ts, d), _vec_spec(d), _vec_spec(d), _gain_spec(d)]
    out_shape = [jax.ShapeDtypeStruct((bsz, s, d), F32), jax.ShapeDtypeStruct((bsz, 1, d), F32),
                 jax.ShapeDtypeStruct((bsz, 1, d), F32), jax.ShapeDtypeStruct((1, d), F32)]
    if gated:
        ins += [gate, mixed]
        in_specs += [_vec_spec(d), _row_spec(ts, d)]
        out_specs += [_vec_spec(d), _row_spec(ts, d)]
        out_shape += [jax.ShapeDtypeStruct((bsz, 1, d), F32), jax.ShapeDtypeStruct((bsz, s, d), BF16)]
    return pl.pallas_call(
        body, name=name, grid=(bsz, s // ts), in_specs=in_specs, out_specs=out_specs, out_shape=out_shape,
        compiler_params=_params(2),
    )(*ins)


def _merge_fwd(proj, b_merge, y_a, y_b, *, name, ts=512):
    bsz, s, _ = proj.shape
    ts = min(ts, s)

    def body(la_ref, lb_ref, ba_ref, bb_ref, ya_ref, yb_ref, out_ref):
        ga = _sigmoid(la_ref[...] + ba_ref[...])
        gb = _sigmoid(lb_ref[...] + bb_ref[...])
        out_ref[...] = (ga * ya_ref[...] + gb * yb_ref[...]).astype(BF16)

    return pl.pallas_call(
        body, name=name, grid=(bsz, s // ts),
        in_specs=[_row_spec(ts, D, OFF_MA // D), _row_spec(ts, D, OFF_MB // D),
                  pl.BlockSpec((1, D), lambda b, i: (0, 0)), pl.BlockSpec((1, D), lambda b, i: (0, 1)),
                  _row_spec(ts, D), _row_spec(ts, D)],
        out_specs=_row_spec(ts, D), out_shape=jax.ShapeDtypeStruct((bsz, s, D), BF16),
        compiler_params=_params(2),
    )(proj, proj, b_merge, b_merge, y_a, y_b)


def _merge_bwd(dmi, proj, b_merge, y_a, y_b, *, name, ts=512):
    bsz, s, _ = proj.shape
    ts = min(ts, s)

    def body(d_ref, la_ref, lb_ref, ba_ref, bb_ref, ya_ref, yb_ref, dya_ref, dyb_ref, dla_ref, dlb_ref, dba_ref, dbb_ref):
        @pl.when((pl.program_id(0) == 0) & (pl.program_id(1) == 0))
        def _():
            dba_ref[...] = jnp.zeros_like(dba_ref)
            dbb_ref[...] = jnp.zeros_like(dbb_ref)

        dv = d_ref[...].astype(F32)
        ga = _sigmoid(la_ref[...] + ba_ref[...])
        gb = _sigmoid(lb_ref[...] + bb_ref[...])
        dya_ref[...] = (dv * ga).astype(BF16)
        dyb_ref[...] = (dv * gb).astype(BF16)
        dla = (dv * ya_ref[...]) * (ga * (1.0 - ga))
        dlb = (dv * yb_ref[...]) * (gb * (1.0 - gb))
        dla_ref[...] = dla.astype(BF16)
        dlb_ref[...] = dlb.astype(BF16)
        dba_ref[...] += jnp.sum(dla, axis=0, keepdims=True)
        dbb_ref[...] += jnp.sum(dlb, axis=0, keepdims=True)

    act = jax.ShapeDtypeStruct((bsz, s, D), BF16)
    return pl.pallas_call(
        body, name=name, grid=(bsz, s // ts),
        in_specs=[_row_spec(ts, D), _row_spec(ts, D, OFF_MA // D), _row_spec(ts, D, OFF_MB // D),
                  pl.BlockSpec((1, D), lambda b, i: (0, 0)), pl.BlockSpec((1, D), lambda b, i: (0, 1)),
                  _row_spec(ts, D), _row_spec(ts, D)],
        out_specs=[_row_spec(ts, D)] * 4 + [_gain_spec(D)] * 2,
        out_shape=[act, act, act, act, jax.ShapeDtypeStruct((1, D), F32), jax.ShapeDtypeStruct((1, D), F32)],
        compiler_params=_params(2),
    )(dmi, proj, proj, b_merge, b_merge, y_a, y_b)


def _tri(lower):
    r = lax.broadcasted_iota(jnp.int32, (CHUNK, CHUNK), 0)
    c = lax.broadcasted_iota(jnp.int32, (CHUNK, CHUNK), 1)
    return jnp.where((c <= r) if lower else (c >= r), 1.0, 0.0).astype(F32)


def _gla_logits(a_ref, wal_ref, bal_ref):
    logits = jnp.dot(a_ref[...].astype(BF16), wal_ref[...].astype(BF16), preferred_element_type=F32) + bal_ref[...]
    la = (jnp.minimum(logits, 0.0) - jnp.log(1.0 + jnp.exp(-jnp.abs(logits)))) * (1.0 / GTAU)
    return logits, la


def _chunk_cumsum(la_n, tri):
    cum = jnp.dot(tri, la_n, preferred_element_type=F32, precision=lax.Precision.HIGHEST)
    return cum, jnp.sum(la_n, axis=0, keepdims=True)


def _gla_specs(s, nc):
    def blk(width, off):
        return pl.BlockSpec((None, s, width), lambda h, b: (b, 0, off // width + h))

    proj_specs = [blk(GDK, OFF_Q), blk(GDK, OFF_K), blk(GDV, OFF_V), blk(GDV, OFF_G),
                  pl.BlockSpec((None, s, LANE), lambda h, b: (b, 0, OFF_A // LANE)),
                  pl.BlockSpec((LANE, GDK), lambda h, b: (0, h)), pl.BlockSpec((1, GDK), lambda h, b: (0, h)),
                  pl.BlockSpec((1, GDV), lambda h, b: (0, 0))]
    st_spec = pl.BlockSpec((None, None, nc, GDV, GDK), lambda h, b: (b, h, 0, 0, 0))
    return blk, proj_specs, st_spec


def _gla_fwd(proj, w_alpha_p, b_alpha, out_norm_g, *, name):
    bsz, s, _ = proj.shape
    nc = s // CHUNK
    scale = GDK ** -0.5

    rb = min(512, s)

    def body(q_ref, k_ref, v_ref, g_ref, a_ref, wal_ref, bal_ref, ong_ref, o_ref, og_ref, st_ref):
        _, la = _gla_logits(a_ref, wal_ref, bal_ref)
        tri = _tri(True)
        st = jnp.zeros((GDV, GDK), F32)
        for n in range(nc):
            rows = pl.ds(n * CHUNK, CHUNK)
            cum, cum_end = _chunk_cumsum(la[n * CHUNK:(n + 1) * CHUNK], tri)
            kd = k_ref[rows, :] * jnp.exp(cum_end - cum)
            ut = lax.dot_general(v_ref[rows, :].astype(BF16), kd.astype(BF16), _TN, preferred_element_type=F32)
            st = st * jnp.exp(cum_end) + ut
            st_ref[n] = st
            o_ref[rows, :] = lax.dot_general((q_ref[rows, :].astype(F32) * scale).astype(BF16), st.astype(BF16), _NT,
                                             preferred_element_type=F32)
        for j in range(0, s, rb):
            blk_rows = pl.ds(j, rb)
            oh, _ = _rms(o_ref[blk_rows, :], None)
            gv = g_ref[blk_rows, :].astype(F32)
            og_ref[blk_rows, :] = ((oh * ong_ref[...]) * (gv * _sigmoid(gv))).astype(BF16)

    blk, proj_specs, st_spec = _gla_specs(s, nc)
    return pl.pallas_call(
        body, name=name, grid=(GH, bsz), in_specs=proj_specs, out_specs=[blk(GDV, 0), blk(GDV, 0), st_spec],
        out_shape=[jax.ShapeDtypeStruct((bsz, s, GH * GDV), F32), jax.ShapeDtypeStruct((bsz, s, GH * GDV), BF16),
                   jax.ShapeDtypeStruct((bsz, GH, nc, GDV, GDK), F32)],
        compiler_params=_params(2),
    )(proj, proj, proj, proj, proj, w_alpha_p, b_alpha, out_norm_g)


def _gla_bwd(dog, o, states, proj, w_alpha_p, b_alpha, out_norm_g, *, name):
    bsz, s, _ = proj.shape
    nc = s // CHUNK
    scale = GDK ** -0.5

    def body(dog_ref, o_ref, st_ref, q_ref, k_ref, v_ref, g_ref, a_ref, wal_ref, bal_ref, ong_ref,
             dq_ref, dk_ref, dv_ref, dg_ref, dl_ref, dbal_ref, dong_ref, do_scr, dlog_scr):
        h, b = pl.program_id(0), pl.program_id(1)

        @pl.when(b == 0)
        def _():
            dbal_ref[...] = jnp.zeros_like(dbal_ref)

        @pl.when((b == 0) & (h == 0))
        def _():
            dong_ref[...] = jnp.zeros_like(dong_ref)

        ong = ong_ref[...]
        for j in range(0, s, rb):
            blk_rows = pl.ds(j, rb)
            gv, dogv = g_ref[blk_rows, :].astype(F32), dog_ref[blk_rows, :]
            sg = _sigmoid(gv)
            oh, r = _rms(o_ref[blk_rows, :], None)
            don = dogv * (gv * sg)
            dg_ref[blk_rows, :] = (dogv * (oh * ong) * (sg * (1.0 + gv * (1.0 - sg)))).astype(BF16)
            dong_ref[...] += jnp.sum(don * oh, axis=0, keepdims=True)
            doh = don * ong
            do_scr[blk_rows, :] = (r * (doh - oh * jnp.mean(doh * oh, axis=-1, keepdims=True))).astype(BF16)

        logits, la = _gla_logits(a_ref, wal_ref, bal_ref)
        tri_lo, tri_up = _tri(True), _tri(False)
        carry = jnp.zeros((GDV, GDK), F32)
        for n in range(nc - 1, -1, -1):
            rows = pl.ds(n * CHUNK, CHUNK)
            cum, cum_end = _chunk_cumsum(la[n * CHUNK:(n + 1) * CHUNK], tri_lo)
            decay = jnp.exp(cum_end)
            w = jnp.exp(cum_end - cum)
            kd = k_ref[rows, :] * w
            do_b = do_scr[rows, :]
            qs_b = (q_ref[rows, :].astype(F32) * scale).astype(BF16)
            dq_ref[rows, :] = (jnp.dot(do_b, st_ref[n].astype(BF16), preferred_element_type=F32) * scale).astype(BF16)
            dsn = lax.dot_general(do_b, qs_b, _TN, preferred_element_type=F32) + carry
            carry = dsn * decay
            dsn_b = dsn.astype(BF16)
            dv_ref[rows, :] = lax.dot_general(kd.astype(BF16), dsn_b, _NT, preferred_element_type=F32).astype(BF16)
            dkd = jnp.dot(v_ref[rows, :].astype(BF16), dsn_b, preferred_element_type=F32)
            dk_ref[rows, :] = (dkd * w).astype(BF16)
            e = dkd * kd
            dcum_end = jnp.sum(e, axis=0, keepdims=True)
            if n > 0:
                dcum_end += jnp.sum(dsn * st_ref[n - 1], axis=0, keepdims=True) * decay
            dlog_scr[rows, :] = dcum_end - jnp.dot(tri_up, e, preferred_element_type=F32,
                                                  precision=lax.Precision.HIGHEST)
        dlog = dlog_scr[...] * (1.0 / GTAU) * (1.0 - _sigmoid(logits))
        dl_ref[...] = dlog.astype(BF16)
        dbal_ref[...] += jnp.sum(dlog, axis=0, keepdims=True)

    rb = min(512, s)

    blk, proj_specs, st_spec = _gla_specs(s, nc)
    act = lambda wd: jax.ShapeDtypeStruct((bsz, s, wd), BF16)
    return pl.pallas_call(
        body, name=name, grid=(GH, bsz), in_specs=[blk(GDV, 0), blk(GDV, 0), st_spec, *proj_specs],
        out_specs=[blk(GDK, 0), blk(GDK, 0), blk(GDV, 0), blk(GDV, 0), blk(GDK, 0),
                   pl.BlockSpec((1, GDK), lambda h, b: (0, h)), pl.BlockSpec((1, GDV), lambda h, b: (0, 0))],
        out_shape=[act(GH * GDK), act(GH * GDK), act(GH * GDV), act(GH * GDV), act(GH * GDK),
                   jax.ShapeDtypeStruct((1, GH * GDK), F32), jax.ShapeDtypeStruct((1, GDV), F32)],
        scratch_shapes=[pltpu.VMEM((s, GDV), BF16), pltpu.VMEM((s, GDK), F32)], compiler_params=_params(2),
    )(dog, o, states, proj, proj, proj, proj, proj, w_alpha_p, b_alpha, out_norm_g)


def _lane():
    return lax.broadcasted_iota(jnp.int32, (1, LANE), 1)


def _swap_halves(x):
    lane = _lane()
    half = MROPE // 2
    lo = (lane >= MNOPE) & (lane < MNOPE + half)
    hi = (lane >= MNOPE + half) & (lane < MQK)
    return jnp.where(lo, pltpu.roll(x, LANE - half, 1), jnp.where(hi, pltpu.roll(x, half, 1), 0.0))


def _norm96(x, g):
    r = lax.rsqrt(jnp.sum(x * x, axis=-1, keepdims=True) * (1.0 / MQK) + EPS)
    return x * r, r


def _lat_norm(proj, q_lat_g, kv_lat_g, *, name, ts=512):
    t = proj.shape[0]
    ts = min(ts, t)

    def body(cq_ref, ckv_ref, gq_ref, gk_ref, oq_ref, ok_ref):
        xq, _ = _rms(cq_ref[...].astype(F32), None)
        oq_ref[...] = (xq * gq_ref[...]).astype(BF16)
        xk, _ = _rms(ckv_ref[...].astype(F32), None)
        ok_ref[...] = (xk * gk_ref[...]).astype(BF16)

    return pl.pallas_call(
        body, name=name, grid=(t // ts,),
        in_specs=[pl.BlockSpec((ts, MQR), lambda i: (i, OFF_CQ // MQR)), pl.BlockSpec((ts, MKVR), lambda i: (i, OFF_CKV // MKVR)),
                  pl.BlockSpec((1, MQR), lambda i: (0, 0)), pl.BlockSpec((1, MKVR), lambda i: (0, 0))],
        out_specs=[pl.BlockSpec((ts, MQR), lambda i: (i, 0)), pl.BlockSpec((ts, MKVR), lambda i: (i, 0))],
        out_shape=[jax.ShapeDtypeStruct((t, MQR), BF16), jax.ShapeDtypeStruct((t, MKVR), BF16)],
        compiler_params=_params(1),
    )(proj, proj, q_lat_g, kv_lat_g)


def _lat_norm_bwd(dcqn, dckvn, proj, q_lat_g, kv_lat_g, *, name, ts=512):
    t = proj.shape[0]
    ts = min(ts, t)

    def one(d_ref, x_ref, g_ref, dx_ref, dg_ref):
        xh, r = _rms(x_ref[...].astype(F32), None)
        dn = d_ref[...]
        dg_ref[...] += jnp.sum(dn * xh, axis=0, keepdims=True)
        dxh = dn * g_ref[...]
        dx_ref[...] = (r * (dxh - xh * jnp.mean(dxh * xh, axis=-1, keepdims=True))).astype(BF16)

    def body(dq_ref, dk_ref, cq_ref, ckv_ref, gq_ref, gk_ref, dxq_ref, dxk_ref, dgq_ref, dgk_ref):
        @pl.when(pl.program_id(0) == 0)
        def _():
            dgq_ref[...] = jnp.zeros_like(dgq_ref)
            dgk_ref[...] = jnp.zeros_like(dgk_ref)

        one(dq_ref, cq_ref, gq_ref, dxq_ref, dgq_ref)
        one(dk_ref, ckv_ref, gk_ref, dxk_ref, dgk_ref)

    return pl.pallas_call(
        body, name=name, grid=(t // ts,),
        in_specs=[pl.BlockSpec((ts, MQR), lambda i: (i, 0)), pl.BlockSpec((ts, MKVR), lambda i: (i, 0)),
                  pl.BlockSpec((ts, MQR), lambda i: (i, OFF_CQ // MQR)), pl.BlockSpec((ts, MKVR), lambda i: (i, OFF_CKV // MKVR)),
                  pl.BlockSpec((1, MQR), lambda i: (0, 0)), pl.BlockSpec((1, MKVR), lambda i: (0, 0))],
        out_specs=[pl.BlockSpec((ts, MQR), lambda i: (i, 0)), pl.BlockSpec((ts, MKVR), lambda i: (i, 0)),
                   pl.BlockSpec((1, MQR), lambda i: (0, 0)), pl.BlockSpec((1, MKVR), lambda i: (0, 0))],
        out_shape=[jax.ShapeDtypeStruct((t, MQR), BF16), jax.ShapeDtypeStruct((t, MKVR), BF16),
                   jax.ShapeDtypeStruct((1, MQR), F32), jax.ShapeDtypeStruct((1, MKVR), F32)],
        compiler_params=_params(1),
    )(dcqn, dckvn, proj, proj, q_lat_g, kv_lat_g)


def _qk_prep(q_raw, kv, proj, cos_t, sin_t, gq, gk, *, name, ts=2048):
    t = q_raw.shape[0]
    ts = min(ts, t)

    def body(q_ref, kv_ref, kpe_ref, c_ref, s_ref, gq_ref, gk_ref, qo_ref, ko_ref, vo_ref):
        cs, sn = c_ref[...], s_ref[...]
        nope = _lane() < MNOPE
        qn, _ = _norm96(q_ref[...].astype(F32), None)
        qn = qn * gq_ref[...]
        qo_ref[...] = (qn * cs + _swap_halves(qn) * sn).astype(BF16)
        kvv = kv_ref[...].astype(F32)
        kn, _ = _norm96(jnp.where(nope, kvv, kpe_ref[...].astype(F32)), None)
        kn = kn * gk_ref[...]
        ko_ref[...] = (kn * cs + _swap_halves(kn) * sn).astype(BF16)
        vo_ref[...] = jnp.where(nope, pltpu.roll(kvv, MNOPE, 1), 0.0).astype(BF16)

    hd = pl.BlockSpec((ts, LANE), lambda i, h: (i, h))
    shared = lambda col: pl.BlockSpec((ts, LANE), lambda i, h: (i, col))
    gain = pl.BlockSpec((1, LANE), lambda i, h: (0, 0))
    out = jax.ShapeDtypeStruct((t, MH * LANE), BF16)
    return pl.pallas_call(
        body, name=name, grid=(t // ts, MH),
        in_specs=[hd, hd, shared(OFF_KPE // LANE), shared(0), shared(0), gain, gain],
        out_specs=[hd, hd, hd], out_shape=[out, out, out], compiler_params=_params(2),
    )(q_raw, kv, proj, cos_t, sin_t, gq, gk)


def _qk_prep_bwd(dq, dk, dv, q_raw, kv, proj, cos_t, sin_t, gq, gk, *, name, ts=2048):
    t = q_raw.shape[0]
    ts = min(ts, t)

    def norm_bwd(dy, x, g, dg_ref):
        xh, r = _norm96(x, None)
        dg_ref[...] += jnp.sum(dy * xh, axis=0, keepdims=True)
        dxh = dy * g
        return r * (dxh - xh * (jnp.sum(dxh * xh, axis=-1, keepdims=True) * (1.0 / MQK)))

    def body(dq_ref, dk_ref, dv_ref, q_ref, kv_ref, kpe_ref, c_ref, s_ref, gq_ref, gk_ref,
             dqr_ref, dkv_ref, dkpe_ref, dgq_ref, dgk_ref):
        i, h = pl.program_id(0), pl.program_id(1)

        @pl.when(h == 0)
        def _():
            dkpe_ref[...] = jnp.zeros_like(dkpe_ref)

        @pl.when((h == 0) & (i == 0))
        def _():
            dgq_ref[...] = jnp.zeros_like(dgq_ref)
            dgk_ref[...] = jnp.zeros_like(dgk_ref)

        cs, sn = c_ref[...], s_ref[...]
        lane = _lane()
        nope = lane < MNOPE
        dqv = dq_ref[...]
        dqn = dqv * cs + _swap_halves(dqv * sn)
        dqr_ref[...] = norm_bwd(dqn, q_ref[...].astype(F32), gq_ref[...], dgq_ref).astype(BF16)
        dkv_ = dk_ref[...]
        dkn = dkv_ * cs + _swap_halves(dkv_ * sn)
        kvv = kv_ref[...].astype(F32)
        dkr = norm_bwd(dkn, jnp.where(nope, kvv, kpe_ref[...].astype(F32)), gk_ref[...], dgk_ref)
        dkv_ref[...] = jnp.where(nope, dkr, pltpu.roll(dv_ref[...], MNOPE, 1)).astype(BF16)
        dkpe_ref[...] += jnp.where((lane >= MNOPE) & (lane < MQK), dkr, 0.0)

    hd = pl.BlockSpec((ts, LANE), lambda i, h: (i, h))
    shared = lambda col: pl.BlockSpec((ts, LANE), lambda i, h: (i, col))
    gain = pl.BlockSpec((1, LANE), lambda i, h: (0, 0))
    out = jax.ShapeDtypeStruct((t, MH * LANE), BF16)
    return pl.pallas_call(
        body, name=name, grid=(t // ts, MH),
        in_specs=[hd, hd, hd, hd, hd, shared(OFF_KPE // LANE), shared(0), shared(0), gain, gain],
        out_specs=[hd, hd, shared(0), gain, gain],
        out_shape=[out, out, jax.ShapeDtypeStruct((t, LANE), F32), jax.ShapeDtypeStruct((1, LANE), F32),
                   jax.ShapeDtypeStruct((1, LANE), F32)],
        compiler_params=_params(2),
    )(dq, dk, dv, q_raw, kv, proj, cos_t, sin_t, gq, gk)


_NT = (((1,), (1,)), ((), ()))
_TN = (((0,), (0,)), ((), ()))


SOFTMAX_SCALE = MQK ** -0.5
Q_PRESCALE = SOFTMAX_SCALE * float(np.log2(np.e))


def _attn_weights(q, k_ref, lo, tq):
    row = lax.broadcasted_iota(jnp.int32, (tq, tq), 0) // CHUNK
    col = lax.broadcasted_iota(jnp.int32, (tq, tq), 1) // CHUNK
    sd = lax.dot_general(q, k_ref[pl.ds(lo, tq), :], _NT, preferred_element_type=F32)
    sd = jnp.where(col <= row, sd, -1e30)
    m = jnp.max(sd, axis=-1, keepdims=True)
    if lo:
        so = lax.dot_general(q, k_ref[pl.ds(0, lo), :], _NT, preferred_element_type=F32)
        m = jnp.maximum(m, jnp.max(so, axis=-1, keepdims=True))
        eo = jnp.exp2(so - m)
        ed = jnp.exp2(sd - m)
        return eo, ed, 1.0 / (jnp.sum(eo, axis=-1, keepdims=True) + jnp.sum(ed, axis=-1, keepdims=True))
    ed = jnp.exp2(sd - m)
    return None, ed, 1.0 / jnp.sum(ed, axis=-1, keepdims=True)


def _attn_fwd(q, k, v, *, name, tq=256):
    bsz, s, _ = q.shape
    tq = min(tq, s)

    def body(q_ref, k_ref, v_ref, o_ref):
        for i in range(s // tq):
            lo = i * tq
            eo, ed, inv = _attn_weights(q_ref[pl.ds(lo, tq), :], k_ref, lo, tq)
            o = jnp.dot(ed.astype(BF16), v_ref[pl.ds(lo, tq), :], preferred_element_type=F32)
            if lo:
                o += jnp.dot(eo.astype(BF16), v_ref[pl.ds(0, lo), :], preferred_element_type=F32)
            o_ref[pl.ds(lo, tq), :] = (o * inv).astype(BF16)

    spec = pl.BlockSpec((None, s, LANE), lambda b, h: (b, 0, h))
    return pl.pallas_call(
        body, name=name, grid=(bsz, MH), in_specs=[spec, spec, spec], out_specs=spec,
        out_shape=jax.ShapeDtypeStruct((bsz, s, MH * LANE), BF16), compiler_params=_params(2),
    )(q, k, v)


def _attn_bwd(q, k, v, do, *, name, tq=256):
    bsz, s, _ = q.shape
    tq = min(tq, s)

    def body(q_ref, k_ref, v_ref, do_ref, dq_ref, dk_ref, dv_ref):
        dk_ref[...] = jnp.zeros_like(dk_ref)
        dv_ref[...] = jnp.zeros_like(dv_ref)
        for i in range(s // tq):
            lo = i * tq
            here, before = pl.ds(lo, tq), pl.ds(0, lo)
            qv, dov = q_ref[here, :], do_ref[here, :]
            eo, ed, inv = _attn_weights(qv, k_ref, lo, tq)
            do_n = (dov.astype(F32) * inv).astype(BF16)
            dv_ref[here, :] += lax.dot_general(ed.astype(BF16), do_n, _TN, preferred_element_type=F32)
            dpd = lax.dot_general(dov, v_ref[here, :], _NT, preferred_element_type=F32)
            delta = jnp.sum(dpd * ed, axis=-1, keepdims=True)
            if lo:
                dv_ref[before, :] += lax.dot_general(eo.astype(BF16), do_n, _TN, preferred_element_type=F32)
                dpo = lax.dot_general(dov, v_ref[before, :], _NT, preferred_element_type=F32)
                delta += jnp.sum(dpo * eo, axis=-1, keepdims=True)
            delta = delta * inv
            r = inv * SOFTMAX_SCALE
            dsd = (ed * (dpd - delta) * r).astype(BF16)
            dq = jnp.dot(dsd, k_ref[here, :], preferred_element_type=F32)
            dk_ref[here, :] += lax.dot_general(dsd, qv, _TN, preferred_element_type=F32)
            if lo:
                dso = (eo * (dpo - delta) * r).astype(BF16)
                dq += jnp.dot(dso, k_ref[before, :], preferred_element_type=F32)
                dk_ref[before, :] += lax.dot_general(dso, qv, _TN, preferred_element_type=F32)
            dq_ref[here, :] = dq
        dk_ref[...] = dk_ref[...] * (1.0 / Q_PRESCALE)

    spec = pl.BlockSpec((None, s, LANE), lambda b, h: (b, 0, h))
    out = jax.ShapeDtypeStruct((bsz, s, MH * LANE), F32)
    return pl.pallas_call(
        body, name=name, grid=(bsz, MH), in_specs=[spec] * 4, out_specs=[spec] * 3, out_shape=[out, out, out],
        compiler_params=_params(2),
    )(q, k, v, do)


def _adamw(w, g, m, v, *, name, tr=256, by_cols=False):
    rows, cols = w.shape
    tr = _tile_rows(rows, tr)

    def body(w_ref, g_ref, m_ref, v_ref, d_ref, nm_ref, nv_ref):
        d_ref[...], nm_ref[...], nv_ref[...] = _adamw_update(w_ref[...], g_ref[...], m_ref[...], v_ref[...])

    spec = pl.BlockSpec((rows, LANE), lambda i: (0, i)) if by_cols else pl.BlockSpec((tr, cols), lambda i: (i, 0))
    out = jax.ShapeDtypeStruct((rows, cols), F32)
    return pl.pallas_call(body, name=name, grid=(cols // LANE if by_cols else rows // tr,), in_specs=[spec] * 4,
                          out_specs=[spec] * 3, out_shape=[out, out, out], compiler_params=_params(1))(w, g, m, v)


def _tile_rows(rows, target):
    if rows <= target:
        return rows
    best = 8
    for t in range(8, target + 1, 8):
        if rows % t == 0:
            best = t
    return best


def _adamw_update(w, g, m, v):
    nm = ADAM_B1 * m + (1.0 - ADAM_B1) * g
    nv = ADAM_B2 * v + (1.0 - ADAM_B2) * (g * g)
    m_hat = nm / (1.0 - ADAM_B1 ** ADAM_STEP)
    v_hat = nv / (1.0 - ADAM_B2 ** ADAM_STEP)
    return -ADAM_LR * (m_hat / (jnp.sqrt(v_hat) + ADAM_EPS) + ADAM_WD * w), nm, nv


def _adamw_halves(w, m, v, mine, theirs, sel, *, name, tr=256):
    rows, cols = w.shape
    tr = _tile_rows(rows // 2, tr)
    nh = rows // 2 // tr

    def body(sel_ref, w_ref, m_ref, v_ref, mine_ref, theirs_ref, g_ref, d_ref, nm_ref, nv_ref):
        lower = pl.program_id(0) < nh
        south = sel_ref[0] == 0
        gv = jnp.where(lower == south, mine_ref[...], theirs_ref[...])
        g_ref[...] = gv
        d_ref[...], nm_ref[...], nv_ref[...] = _adamw_update(w_ref[...], gv, m_ref[...], v_ref[...])

    full = pl.BlockSpec((tr, cols), lambda i, sel_ref: (i, 0))
    half = pl.BlockSpec((tr, cols), lambda i, sel_ref: (i % nh, 0))
    out = jax.ShapeDtypeStruct((rows, cols), F32)
    return pl.pallas_call(
        body, name=name, out_shape=[out] * 4, compiler_params=_params(1),
        grid_spec=pltpu.PrefetchScalarGridSpec(num_scalar_prefetch=1, grid=(rows // tr,),
                                               in_specs=[full, full, full, half, half], out_specs=[full] * 4),
    )(sel, w, m, v, mine, theirs)


def _pair_add(x, sib, sel, *, name, tr=256):
    n, _, rows, cols = x.shape
    tr = _tile_rows(rows, tr)

    def body(sel_ref, x_ref, s_ref, o_ref):
        o_ref[...] = (x_ref[...] + s_ref[...]).astype(BF16)

    spec = pl.BlockSpec((None, tr, cols), lambda j, i, sel_ref: (j, i, 0))
    return pl.pallas_call(
        body, name=name, out_shape=jax.ShapeDtypeStruct((n, rows, cols), BF16), compiler_params=_params(2),
        grid_spec=pltpu.PrefetchScalarGridSpec(
            num_scalar_prefetch=1, grid=(n, rows // tr),
            in_specs=[pl.BlockSpec((None, None, tr, cols), lambda j, i, sel_ref: (j, sel_ref[0], i, 0)), spec],
            out_specs=spec),
    )(sel, x, sib)


def _chip_sum(pair, recv, sel, *, name, tr=256):
    _, rows, cols = pair.shape
    tr = _tile_rows(rows, tr)

    def body(sel_ref, p_ref, r_ref, o_ref):
        acc = p_ref[...].astype(F32)
        for k in range(3):
            acc = acc + r_ref[k].astype(F32)
        o_ref[...] = acc

    return pl.pallas_call(
        body, name=name, out_shape=jax.ShapeDtypeStruct((rows, cols), F32), compiler_params=_params(1),
        grid_spec=pltpu.PrefetchScalarGridSpec(
            num_scalar_prefetch=1, grid=(rows // tr,),
            in_specs=[pl.BlockSpec((None, tr, cols), lambda i, sel_ref: (sel_ref[0], i, 0)),
                      pl.BlockSpec((3, tr, cols), lambda i, sel_ref: (0, i, 0))],
            out_specs=pl.BlockSpec((tr, cols), lambda i, sel_ref: (i, 0))),
    )(sel, pair, recv)


def _me():
    return lax.axis_index("x"), lax.axis_index("y"), lax.axis_index("c")


def _flip(pos, bits):
    x, y, c = pos
    return (x ^ bits[0] if bits[0] else x, y ^ bits[1] if bits[1] else y, c ^ bits[2] if bits[2] else c)


ANY = pl.BlockSpec(memory_space=pl.ANY)


def _all_gather8(xs, *, name):
    n = len(xs)
    flips = [((k >> 2) & 1, (k >> 1) & 1, k & 1) for k in range(1, 8)]

    def body(*refs):
        x_refs, out_refs, (send_sems, recv_sems, local_sems) = refs[:n], refs[n:2 * n], refs[2 * n:]
        me = _me()
        slot = lambda p: 4 * p[0] + 2 * p[1] + p[2]
        copies = []
        for i in range(n):
            mine = pltpu.make_async_copy(x_refs[i], out_refs[i].at[slot(me)], local_sems.at[i])
            mine.start()
            copies.append(mine)
            for k, f in enumerate(flips):
                peer = _flip(me, f)
                sems = dict(send_sem=send_sems.at[7 * i + k], recv_sem=recv_sems.at[7 * i + k], device_id=peer,
                            device_id_type=MESH)
                cp = pltpu.make_async_remote_copy(src_ref=x_refs[i], dst_ref=out_refs[i].at[slot(me)], **sems)
                cp.start()
                copies.append(cp)
                copies.append(pltpu.make_async_remote_copy(src_ref=x_refs[i], dst_ref=out_refs[i].at[slot(peer)], **sems))
        for i in range(n):
            base = i * 15
            copies[base].wait()
            for k in range(7):
                copies[base + 1 + 2 * k].wait_send()
                copies[base + 2 + 2 * k].wait_recv()

    outs = pl.pallas_call(
        body, name=name, in_specs=[ANY] * n, out_specs=[ANY] * n,
        out_shape=[jax.ShapeDtypeStruct((8, *x.shape), x.dtype) for x in xs],
        scratch_shapes=[pltpu.SemaphoreType.DMA((7 * n,)), pltpu.SemaphoreType.DMA((7 * n,)),
                        pltpu.SemaphoreType.DMA((n,))])(*xs)
    return list(outs)


CHIP_FLIPS = [(1, 0, 0), (0, 1, 0), (1, 1, 0)]


def _chip():
    return 2 * lax.axis_index("x") + lax.axis_index("y")


HBM = pl.BlockSpec(memory_space=pltpu.HBM)
SEM = pl.BlockSpec(memory_space=pltpu.SEMAPHORE)
EFFECT = pltpu.SideEffectType.DATAFLOW_SIDE_EFFECTING


def _plan_copies(plan, refs, send_sems, recv_sems):
    return [pltpu.make_async_remote_copy(src_ref=src, dst_ref=dst, send_sem=send_sems.at[k], recv_sem=recv_sems.at[k],
                                         device_id=to, device_id_type=MESH) for k, (src, dst, to) in enumerate(plan(refs))]


def _rdma_start(arrays, n_copies, plan, deps, *, name):
    n, nd = len(arrays), len(deps)

    def body(*refs):
        for cp in _plan_copies(plan, refs[:n], refs[n + nd], refs[n + nd + 1]):
            cp.start()
        refs[-1][...] = jnp.zeros_like(refs[-1])

    outs = pl.pallas_call(
        body, name=name,
        out_shape=(pltpu.SemaphoreType.DMA((n_copies,)), pltpu.SemaphoreType.DMA((n_copies,)),
                   *[pltpu.HBM(a.shape, a.dtype) for a in arrays], jax.ShapeDtypeStruct((8, LANE), F32)),
        in_specs=[HBM] * n + [ANY] * nd, out_specs=(SEM, SEM, *[HBM] * n, pl.BlockSpec(memory_space=pltpu.VMEM)),
        input_output_aliases={i: i + 2 for i in range(n)}, compiler_params=pltpu.CompilerParams(has_side_effects=EFFECT),
    )(*[pltpu.with_memory_space_constraint(a, pltpu.HBM) for a in arrays], *deps)
    return outs[0], outs[1], list(outs[2:2 + n]), outs[-1]


def _rdma_wait(send_sems, recv_sems, arrays, plan, after, *, name):
    n = len(arrays)

    def body(*refs):
        for cp in _plan_copies(plan, refs[:n], refs[n], refs[n + 1]):
            cp.wait_send()
            cp.wait_recv()

    return list(pl.pallas_call(
        body, name=name, out_shape=tuple(pltpu.HBM(a.shape, a.dtype) for a in arrays),
        in_specs=[HBM] * n + [SEM, SEM, ANY], out_specs=tuple([HBM] * n), input_output_aliases={i: i for i in range(n)},
        compiler_params=pltpu.CompilerParams(has_side_effects=EFFECT),
    )(*arrays, send_sems, recv_sems, after))


def _gather_plan(n):
    def plan(refs):
        me = _me()
        slot = 2 * me[0] + me[1]
        return [(refs[i].at[me[2]], refs[n + i].at[slot, me[2]], _flip(me, f)) for i in range(n) for f in CHIP_FLIPS]
    return plan


def _scatter_plan(n):
    def plan(refs):
        me = _me()
        out = []
        for i in range(n):
            for k, f in enumerate(CHIP_FLIPS):
                peer = _flip(me, f)
                out.append((refs[i].at[2 * peer[0] + peer[1]], refs[n + i].at[k], peer))
        return out
    return plan


def _sibling_plan(n, src_of):
    def plan(refs):
        me = _me()
        return [(src_of(refs[i], me[2]), refs[n + i], _flip(me, (0, 0, 1))) for i in range(n)]
    return plan


def _gather8_plan(n):
    def plan(refs):
        me = _me()
        slot = 4 * me[0] + 2 * me[1] + me[2]
        return [(refs[i], refs[n + i].at[slot], _flip(me, ((k >> 2) & 1, (k >> 1) & 1, k & 1)))
                for i in range(n) for k in range(1, 8)]
    return plan


def _pair_fill(lands, *, name):
    n = len(lands)

    def body(*refs):
        in_refs, (send_sems, recv_sems) = refs[:n], refs[2 * n:]
        me = _me()
        sib = _flip(me, (0, 0, 1))
        copies = []
        for i in range(n):
            for k, f in enumerate(CHIP_FLIPS):
                peer = _flip(me, f)
                slot = 2 * peer[0] + peer[1]
                mine, theirs = in_refs[i].at[slot, me[2]], in_refs[i].at[slot, 1 - me[2]]
                cp = pltpu.make_async_remote_copy(src_ref=mine, dst_ref=mine, send_sem=send_sems.at[3 * i + k],
                                                  recv_sem=recv_sems.at[3 * i + k], device_id=sib, device_id_type=MESH)
                cp.start()
                copies.append((cp, pltpu.make_async_remote_copy(
                    src_ref=mine, dst_ref=theirs, send_sem=send_sems.at[3 * i + k], recv_sem=recv_sems.at[3 * i + k],
                    device_id=sib, device_id_type=MESH)))
        for cp, arrival in copies:
            arrival.wait_recv()
            cp.wait_send()

    return list(pl.pallas_call(
        body, name=name, in_specs=[ANY] * n, out_specs=[ANY] * n,
        out_shape=[jax.ShapeDtypeStruct(a.shape, a.dtype) for a in lands], input_output_aliases={i: i for i in range(n)},
        scratch_shapes=[pltpu.SemaphoreType.DMA((3 * n,)), pltpu.SemaphoreType.DMA((3 * n,))])(*lands))


def _own_and_landed(lands, xs):
    chip = _chip()
    return [[jnp.where(chip == j, x, o.reshape(4, *x.shape)[j]) for j in range(4)] for o, x in zip(lands, xs)]


BIG = (("w_in", (D, IN_WIDTH // 4), 1), ("gla_w_o", (D // 4, D), 0), ("mla_w_uq", (MQR, MH * MQK // 4), 1),
       ("mla_w_ukv", (MKVR, MH * (MNOPE + MVD) // 4), 1), ("mla_w_o", (D // 4, D), 0), ("w_out", (D // 4, D), 0),
       ("mlp_w1", (D, DFF // 4), 1), ("mlp_w2", (DFF // 4, D), 0))
ADA_SHARD = (D, 6 * D // 4)
SMALL = (("b_ada", 6 * D), ("norm1_g", D), ("b_merge", 2 * D), ("gla_b_alpha", GH * GDK), ("gla_out_norm_g", GDV),
         ("mla_q_lat_g", MQR), ("mla_kv_lat_g", MKVR), ("mla_qn_g", MQK), ("mla_kn_g", MQK), ("norm2_g", D))


W_IN_SEGMENTS = ((0, 3072, OFF_Q), (3072, 3088, OFF_A), (3088, 3344, OFF_CQ), (3344, 3472, OFF_CKV),
                 (3472, 3504, OFF_KPE + MNOPE), (3504, 5552, OFF_MA))
W_IN_SPLIT = OFF_MA
SMALL_ROWS, SMALL_COLS = 32, 2 * D
W_ALPHA_ROW = 16
LOSS_ROW = 15
SMALL_RED = tuple((n, k) for n, k in SMALL if n != "b_ada")


def _pack_small(grads, d_w_alpha, loss_row, *, name):
    def body(*refs):
        g_refs, wa_ref, loss_ref, out_ref = refs[:-3], refs[-3], refs[-2], refs[-1]
        out_ref[...] = jnp.zeros_like(out_ref)
        for i, ((_, k), g_ref) in enumerate(zip(SMALL_RED, g_refs)):
            out_ref[i:i + 1, 0:k] = g_ref[...]
        out_ref[LOSS_ROW:LOSS_ROW + 1, 0:LANE] = loss_ref[...]
        out_ref[W_ALPHA_ROW:W_ALPHA_ROW + GLR, 0:GH * GDK] = wa_ref[...]

    return pl.pallas_call(body, name=name, out_shape=jax.ShapeDtypeStruct((SMALL_ROWS, SMALL_COLS), F32))(
        *grads, d_w_alpha, loss_row)


def _small_update(gathered, dmod_all, sel, wmv, *, name):
    names = [n for n, _ in SMALL] + ["gla_w_alpha"]
    n_par = len(names)

    def body(sel_ref, g_ref, dmod_ref, *refs):
        in_refs, out_refs, loss_ref, acc = refs[:3 * n_par], refs[3 * n_par:-2], refs[-2], refs[-1]
        total = g_ref[0]
        for j in range(1, 8):
            total = total + g_ref[j]
        acc[...] = total
        loss_ref[...] = acc[LOSS_ROW:LOSS_ROW + 1, 0:LANE]
        row = {n: i for i, (n, _) in enumerate(SMALL_RED)}
        for p, name_p in enumerate(names):
            w_ref, m_ref, v_ref = in_refs[3 * p:3 * p + 3]
            if name_p == "b_ada":
                gv = jnp.sum(dmod_ref[...], axis=0, keepdims=True)
            elif name_p == "gla_w_alpha":
                gv = jnp.zeros((GLR, GDK), F32)
                for j in range(4):
                    blk = acc[W_ALPHA_ROW:W_ALPHA_ROW + GLR, j * GDK:(j + 1) * GDK]
                    gv = gv + jnp.where(sel_ref[0] == j, blk, 0.0)
            else:
                gv = acc[row[name_p]:row[name_p] + 1, 0:w_ref.shape[1]]
            o = out_refs[4 * p:4 * p + 4]
            o[0][...] = gv
            o[1][...], o[2][...], o[3][...] = _adamw_update(w_ref[...], gv, m_ref[...], v_ref[...])

    flat = [a for t in wmv for a in t]
    out_shape = [jax.ShapeDtypeStruct(t[0].shape, F32) for t in wmv for _ in range(4)]
    out_shape.append(jax.ShapeDtypeStruct((1, LANE), F32))
    vmem = pl.BlockSpec(memory_space=pltpu.VMEM)
    outs = pl.pallas_call(
        body, name=name, out_shape=out_shape, in_specs=[pl.BlockSpec(memory_space=pltpu.SMEM), vmem, vmem] + [vmem] * len(flat),
        out_specs=[vmem] * len(out_shape), scratch_shapes=[pltpu.VMEM((SMALL_ROWS, SMALL_COLS), F32)],
    )(sel, gathered, dmod_all, *flat)
    return {n: tuple(outs[4 * p:4 * p + 4]) for p, n in enumerate(names)}, outs[-1][0, 0]


def _full_weights(gathered):
    w = {name: jnp.concatenate(gathered[name], axis=axis) for name, _, axis in BIG if name in gathered and name != "w_in"}
    if "w_in" in gathered:
        shards = gathered["w_in"]
        zeros = lambda n: [jnp.zeros((D, n), shards[0].dtype)]

        def cols(a, b):
            width = IN_WIDTH // 4
            return [shards[j][:, max(a, j * width) - j * width:min(b, (j + 1) * width) - j * width]
                    for j in range(4) if max(a, j * width) < min(b, (j + 1) * width)]

        parts = []
        for a, b, at in sorted(W_IN_SEGMENTS, key=lambda seg: seg[2]):
            have = sum(p.shape[1] for p in parts)
            parts += (zeros(at - have) if at > have else []) + cols(a, b)
        w["w_in"] = jnp.concatenate(parts + zeros(PW - sum(p.shape[1] for p in parts)), axis=1)
    if "mla_w_uq" in w:
        w["mla_w_uq"] = jnp.pad(w["mla_w_uq"].reshape(MQR, MH, MQK), ((0, 0), (0, 0), (0, LANE - MQK))).reshape(MQR, MH * LANE)
    if "mla_w_o" in w:
        w["mla_w_o"] = jnp.pad(w["mla_w_o"].reshape(MH, MVD, D), ((0, 0), (0, LANE - MVD), (0, 0))).reshape(MH * LANE, D)
    return w


def _grad_slots(g):
    g = dict(g)
    out = {}
    if "w_in" in g:
        g_lo, g_hi = g.pop("w_in")
        take = lambda at, lo, hi: g_lo[:, at + lo:at + hi] if at < W_IN_SPLIT else g_hi[:, at - W_IN_SPLIT + lo:at - W_IN_SPLIT + hi]
        width = IN_WIDTH // 4
        slots = []
        for j in range(4):
            lo, hi = j * width, (j + 1) * width
            slots.append(jnp.concatenate([take(at, max(lo, a) - a, min(hi, b) - a)
                                          for a, b, at in W_IN_SEGMENTS if max(lo, a) < min(hi, b)], axis=1))
        out["w_in"] = jnp.stack(slots).reshape(4, 2, D // 2, width)
    if "mla_w_uq" in g:
        g["mla_w_uq"] = g["mla_w_uq"].reshape(MQR, MH, LANE)[:, :, :MQK].reshape(MQR, MH * MQK)
    if "mla_w_o" in g:
        g["mla_w_o"] = g["mla_w_o"].reshape(MH, LANE, D)[:, :MVD].reshape(MH * MVD, D)
    for name, (rows, cols), axis in BIG:
        if name not in g:
            continue
        a = g[name]
        a = a.reshape(4, rows, cols) if axis == 0 else jnp.transpose(a.reshape(rows, 4, cols), (1, 0, 2))
        out[name] = a.reshape(4, 2, rows // 2, cols)
    return out


def _rope_tables(positions):
    freqs = ROPE_THETA ** (-jnp.arange(0, MROPE, 2, dtype=F32) / MROPE)
    lane = np.arange(LANE)
    in_rope = (lane >= MNOPE) & (lane < MQK)
    freq_lane = jnp.where(in_rope, freqs[(lane - MNOPE) % (MROPE // 2)], 0.0)
    sign = np.where(in_rope, np.where(lane < MNOPE + MROPE // 2, -1.0, 1.0), 0.0).astype(np.float32)
    ang = positions.astype(F32).reshape(-1, 1) * freq_lane[None, :]
    return jnp.cos(ang), jnp.sin(ang) * sign[None, :]


def _local_step(x, positions, mod, target, w, small, more_weights=None, on_grads=None):
    kept = {}
    if on_grads is None:
        on_grads = lambda tag, grads, after: kept.update(grads)
    bsz, s, _ = x.shape
    t = bsz * s
    tt = _tile(t, 1024)
    shift1, scale1, gate1, shift2, scale2, gate2 = [mod[:, None, i * D:(i + 1) * D] for i in range(6)]
    cos_t, sin_t = _rope_tables(positions)
    w_alpha_p = jnp.pad(small["gla_w_alpha"], ((0, LANE - GLR), (0, 0)))
    gq = jnp.pad(small["mla_qn_g"], ((0, 0), (0, LANE - MQK)))
    gk = jnp.pad(small["mla_kn_g"], ((0, 0), (0, LANE - MQK)))
    flat2 = lambda a: a.reshape(t, a.shape[-1])
    bsd = lambda a: a.reshape(bsz, s, a.shape[-1])

    h = _norm_mod(x, small["norm1_g"], scale1, shift1, name="norm1")
    if callable(w):
        w = w(h)
    proj = _mm(flat2(h), w["w_in"], name="proj", tn=1152, out_dtype=BF16)
    proj3 = bsd(proj)
    o, o_gated, states = _gla_fwd(proj3, w_alpha_p, small["gla_b_alpha"], small["gla_out_norm_g"], name="gla_fwd")
    if more_weights is not None:
        w = {**w, **more_weights(o_gated)}
    y_a = _mm(flat2(o_gated), w["gla_w_o"], name="gla_out", out_dtype=BF16)
    cq_n, ckv_n = _lat_norm(proj, small["mla_q_lat_g"], small["mla_kv_lat_g"], name="lat_norm")
    q_raw = _mm(cq_n, w["mla_w_uq"], name="mla_uq", out_dtype=BF16)
    kv = _mm(ckv_n, w["mla_w_ukv"], name="mla_ukv", out_dtype=BF16)
    qf, kf, vf = _qk_prep(q_raw, kv, proj, cos_t, sin_t, gq * Q_PRESCALE, gk, name="qk_prep")
    o_attn = _attn_fwd(bsd(qf), bsd(kf), bsd(vf), name="attn_fwd")
    y_b = _mm(flat2(o_attn), w["mla_w_o"], name="mla_out", out_dtype=BF16)
    mixed_in = _merge_fwd(proj3, small["b_merge"], bsd(y_a), bsd(y_b), name="merge_fwd")
    mixed = _mm(flat2(mixed_in), w["w_out"], name="w_out")
    x1, h2 = _resid_norm_mod(x, bsd(mixed), gate1, small["norm2_g"], scale2, shift2, name="norm2")

    def sqrelu(acc, ex, outs):
        r = jnp.maximum(acc, 0.0)
        outs[0][...] = (r * r).astype(BF16)

    r = _mm(flat2(h2), w["mlp_w1"], name="mlp1", epilogue=sqrelu, out_shape=jax.ShapeDtypeStruct((t, DFF), BF16),
            out_specs=_tile_spec(tt, 1024))
    tl = _tile(s, 512)
    per_row = s // tl

    def loss_head(ffv, ex, outs, i):
        x1_ref, t_ref, gt_ref = ex
        dy_ref, dff_ref, dgt_ref, loss_ref = outs
        gt = gt_ref[...]
        diff = (x1_ref[...] + gt * ffv) - t_ref[...]
        part = jnp.sum(diff * diff)
        dy = diff * (1.0 / D)
        dy_ref[...] = dy
        dff_ref[...] = (dy * gt).astype(BF16)
        dg = jnp.sum(dy * ffv, axis=0, keepdims=True)
        first = i % per_row == 0

        @pl.when(first)
        def _():
            dgt_ref[...] = dg

        @pl.when(jnp.logical_not(first))
        def _():
            dgt_ref[...] += dg

        @pl.when(i == 0)
        def _():
            loss_ref[...] = jnp.full(loss_ref.shape, part, F32)

        @pl.when(i > 0)
        def _():
            loss_ref[...] += part

    rows = pl.BlockSpec((tl, D), lambda i, j, k: (i, 0))
    per_batch = pl.BlockSpec((None, 1, D), lambda i, j, k: (i // per_row, 0, 0))
    dy, dff, dgate2, loss_part = _mm(
        r, w["mlp_w2"], name="mlp2_loss", tm=tl, tk=2048, epilogue=loss_head, tile_index=True,
        extras=(flat2(x1), flat2(target), gate2),
        extra_specs=(rows, rows, per_batch), out_specs=[rows, rows, per_batch, pl.BlockSpec((8, LANE), lambda i, j, k: (0, 0))],
        out_shape=[jax.ShapeDtypeStruct((t, D), F32), jax.ShapeDtypeStruct((t, D), BF16),
                   jax.ShapeDtypeStruct((bsz, 1, D), F32), jax.ShapeDtypeStruct((8, LANE), F32)])

    g = {}

    def relu2_bwd(acc, ex, outs):
        outs[0][...] = (acc * (2.0 * jnp.sqrt(ex[0][...].astype(F32)))).astype(BF16)

    dff2 = flat2(dff)
    da1 = _mm(dff2, w["mlp_w2"], tb=True, name="mlp2_dx", epilogue=relu2_bwd, extras=(r,),
              extra_specs=(_tile_spec(tt, 1024),), out_shape=jax.ShapeDtypeStruct((t, DFF), BF16),
              out_specs=_tile_spec(tt, 1024))
    g["mlp_w2"] = _mm(r, dff2, ta=True, name="mlp2_dw")
    dh2 = _mm(da1, w["mlp_w1"], tb=True, name="mlp1_dx")
    g["mlp_w1"] = _mm(flat2(h2), da1, ta=True, name="mlp1_dw")
    token = on_grads("mlp", {n: g.pop(n) for n in ("mlp_w2", "mlp_w1")}, dh2)
    if token is not None:
        gate1 = gate1 + token[0, 0]
    dx1, dscale2, dshift2, dg2, dgate1, dmixed = _norm_mod_bwd(
        bsd(dh2), x1, bsd(dy), small["norm2_g"], scale2, gate1, bsd(mixed), name="norm2_bwd")
    dmixed2 = flat2(dmixed)
    dmi = _mm(dmixed2, w["w_out"], tb=True, name="w_out_dx", out_dtype=BF16)
    g["w_out"] = _mm(flat2(mixed_in), dmixed2, ta=True, name="w_out_dw")
    dy_a, dy_b, dl_a, dl_b, db_a, db_b = _merge_bwd(bsd(dmi), proj3, small["b_merge"], bsd(y_a), bsd(y_b), name="merge_bwd")
    dy_a2, dy_b2 = flat2(dy_a), flat2(dy_b)
    dog = _mm(dy_a2, w["gla_w_o"], tb=True, name="gla_out_dx")
    g["gla_w_o"] = _mm(flat2(o_gated), dy_a2, ta=True, name="gla_out_dw")
    dq_g, dk_g, dv_g, dg_g, dlog, db_alpha, d_ong = _gla_bwd(
        bsd(dog), o, states, proj3, w_alpha_p, small["gla_b_alpha"], small["gla_out_norm_g"], name="gla_bwd")
    dlog2 = flat2(dlog)
    da_p = _mm(dlog2, w_alpha_p, tb=True, out_dtype=BF16, name="alpha_dx")
    d_w_alpha = _mm(proj[:, OFF_A:OFF_A + LANE], dlog2, ta=True, name="alpha_dw")[:GLR]
    do_attn = _mm(dy_b2, w["mla_w_o"], tb=True, out_dtype=BF16, name="mla_out_dx")
    g["mla_w_o"] = _mm(flat2(o_attn), dy_b2, ta=True, name="mla_out_dw")
    dqf, dkf, dvf = _attn_bwd(bsd(qf), bsd(kf), bsd(vf), bsd(do_attn), name="attn_bwd")
    dq_raw, dkv, dkpe, dgq, dgk = _qk_prep_bwd(flat2(dqf), flat2(dkf), flat2(dvf), q_raw, kv, proj, cos_t, sin_t, gq, gk,
                                                name="qk_prep_bwd")
    dcq_n = _mm(dq_raw, w["mla_w_uq"], tb=True, name="mla_uq_dx")
    g["mla_w_uq"] = _mm(cq_n, dq_raw, ta=True, name="mla_uq_dw")
    dckv_n = _mm(dkv, w["mla_w_ukv"], tb=True, name="mla_ukv_dx")
    g["mla_w_ukv"] = _mm(ckv_n, dkv, ta=True, name="mla_ukv_dw")
    token = on_grads("mix", {n: g.pop(n) for n in ("w_out", "gla_w_o", "mla_w_o", "mla_w_uq", "mla_w_ukv")}, dckv_n)
    q_lat_g = small["mla_q_lat_g"] if token is None else small["mla_q_lat_g"] + token[0:1, 0:1]
    dcq, dckv, dg_qlat, dg_kvlat = _lat_norm_bwd(dcq_n, dckv_n, proj, q_lat_g, small["mla_kv_lat_g"],
                                                  name="lat_norm_bwd")
    pieces = [(flat2(dq_g), OFF_Q), (flat2(dk_g), OFF_K), (flat2(dv_g), OFF_V), (flat2(dg_g), OFF_G),
              (flat2(dl_a), OFF_MA), (flat2(dl_b), OFF_MB), (dcq, OFF_CQ), (dckv, OFF_CKV), (da_p, OFF_A), (dkpe, OFF_KPE)]
    hb = flat2(h)
    g_w_in = (_pieces_dw(hb, [p for p, off in pieces if off < W_IN_SPLIT], name="proj_dw_a"),
              _pieces_dw(hb, [p for p, off in pieces if off >= W_IN_SPLIT], name="proj_dw_b"))
    token = on_grads("in", {"w_in": g_w_in}, g_w_in[1])
    after = jnp.zeros((8, LANE), F32) if token is None else token
    dh = _pieces_dx(pieces, w["w_in"], after, name="proj_dx")
    token = on_grads("dx", {}, dh)
    if token is not None:
        scale1 = scale1 + token[0, 0]
    grad_x, dscale1, dshift1, dg1 = _norm_mod_bwd(bsd(dh), x, dx1, small["norm1_g"], scale1, name="norm1_bwd")

    dmod = jnp.concatenate([dshift1, dscale1, dgate1, dshift2, dscale2, dgate2], axis=-1).reshape(bsz, 6 * D)
    gs = {"norm1_g": dg1, "b_merge": jnp.concatenate([db_a, db_b], axis=1), "gla_b_alpha": db_alpha,
          "gla_out_norm_g": d_ong, "mla_q_lat_g": dg_qlat, "mla_kv_lat_g": dg_kvlat, "mla_qn_g": dgq[:, :MQK],
          "mla_kn_g": dgk[:, :MQK], "norm2_g": dg2}
    return loss_part[0, 0], grad_x, dmod, {**kept, **g}, gs, d_w_alpha


def kernel(x, c, positions, w_ada, b_ada, norm1_g, w_in, b_merge, gla_w_alpha, gla_b_alpha, gla_out_norm_g, gla_w_o, mla_q_lat_g, mla_w_uq, mla_kv_lat_g, mla_w_ukv, mla_qn_g, mla_kn_g, mla_w_o, w_out, norm2_g, mlp_w1, mlp_w2, loss_target, m_w_ada, m_b_ada, m_norm1_g, m_w_in, m_b_merge, m_gla_w_alpha, m_gla_b_alpha, m_gla_out_norm_g, m_gla_w_o, m_mla_q_lat_g, m_mla_w_uq, m_mla_kv_lat_g, m_mla_w_ukv, m_mla_qn_g, m_mla_kn_g, m_mla_w_o, m_w_out, m_norm2_g, m_mlp_w1, m_mlp_w2, v_w_ada, v_b_ada, v_norm1_g, v_w_in, v_b_merge, v_gla_w_alpha, v_gla_b_alpha, v_gla_out_norm_g, v_gla_w_o, v_mla_q_lat_g, v_mla_w_uq, v_mla_kv_lat_g, v_mla_w_ukv, v_mla_qn_g, v_mla_kn_g, v_mla_w_o, v_w_out, v_norm2_g, v_mlp_w1, v_mlp_w2):
    args = dict(locals())
    names_big = [n for n, _, _ in BIG]
    names_small = [n for n, _ in SMALL]
    bsz = x.shape[0]
    ax, ay, ac = lax.axis_index("x"), lax.axis_index("y"), lax.axis_index("c")
    chip = 2 * ax + ay
    dev = 2 * chip + ac

    small = {n: args[n] for n in names_small}
    sel_c = jnp.reshape(ac, (1,)).astype(jnp.int32)
    sel_chip = jnp.reshape(chip, (1,)).astype(jnp.int32)
    c_all, w_alpha_all = _all_gather8([c, gla_w_alpha[0]], name="comm_c_alpha")
    small["gla_w_alpha"] = jnp.concatenate([w_alpha_all[2 * j] for j in range(4)], axis=1)
    c_all = c_all.reshape(8 * bsz, D)

    shards = {n: args[n][0].astype(BF16) for n in names_big}
    halves_of = lambda names: [shards[n].reshape(2, shards[n].shape[0] // 2, shards[n].shape[1]) for n in names]

    def gather_start(names, deps, tag):
        xs = halves_of(names)
        lands = [lax.empty((4, *xh.shape), BF16) for xh in xs]
        plan = _gather_plan(len(names))
        return names, plan, _rdma_start(xs + lands, 3 * len(names), plan, deps, name="comm_weights_start_" + tag)

    def gather_finish(started, after, tag):
        names, plan, sems = started
        arrs = _rdma_wait(sems[0], sems[1], sems[2], plan, after, name="comm_weights_wait_" + tag)
        filled = _pair_fill(arrs[len(names):], name="comm_weights_pair_" + tag)
        own = [a.reshape(shards[n].shape) for n, a in zip(names, arrs)]
        return _full_weights(dict(zip(names, _own_and_landed(filled, own))))


    def add_bias(acc, ex, outs):
        outs[0][...] = acc + ex[0][...]

    silu = lambda v: v * _sigmoid(v)
    b_ada_mine = lax.dynamic_slice(b_ada, (0, chip * ADA_SHARD[1]), (1, ADA_SHARD[1]))
    mod_part = _mm(c_all, w_ada[0], name="ada", tn=512, a_fn=silu, epilogue=add_bias, extras=(b_ada_mine,),
                   extra_specs=(pl.BlockSpec((1, 512), lambda i, j, k: (0, j)),),
                   out_shape=jax.ShapeDtypeStruct((8 * bsz, ADA_SHARD[1]), F32), out_specs=_tile_spec(8 * bsz, 512))
    mod_all = _all_gather8([mod_part], name="comm_mod")[0]
    mod_rows = lax.dynamic_slice(mod_all, (0, dev * bsz, 0), (8, bsz, ADA_SHARD[1]))
    mod = jnp.concatenate([mod_rows[2 * j] for j in range(4)], axis=1)
    first = gather_start(["w_in"], (mod,), "in")
    rest = gather_start([n for n in names_big if n != "w_in"], (mod, first[2][3]), "rest")
    mod = mod + rest[2][3][0, 0]
    w_in_after = lambda after: gather_finish(first, after, "in")
    more_weights = lambda after: gather_finish(rest, after, "rest")

    stage = {}

    def begin(tag, names, arrays, lands, n_copies, plan, what):
        stage[tag] = (names, plan, _rdma_start(arrays + lands, n_copies, plan, (), name=f"comm_{what}_start_{tag}"))
        return stage[tag][2][3]

    def landed(tag, after, what):
        names, plan, sems = stage[tag]
        arrs = _rdma_wait(sems[0], sems[1], sems[2], plan, after, name=f"comm_{what}_wait_{tag}")
        return names, arrs[:len(arrs) // 2], arrs[len(arrs) // 2:]

    def swap_start(tag, grads):
        names = list(grads)
        parts = [_grad_slots(grads)[n] for n in names]
        lands = [lax.empty((4, *p.shape[2:]), F32) for p in parts]
        return begin(tag, names, parts, lands, len(names), _sibling_plan(len(names), lambda r, c: r.at[:, 1 - c]), "pair_sum")

    def scatter_start(tag, after):
        names, parts, sib_halves = landed(tag, after, "pair_sum")
        pairs = [_pair_add(p, s, sel_c, name="pair_add_" + n) for n, p, s in zip(names, parts, sib_halves)]
        recvs = [lax.empty((3, *p.shape[1:]), BF16) for p in pairs]
        return begin(tag, names, pairs, recvs, 3 * len(names), _scatter_plan(len(names)), "scatter")

    def join_start(tag, after):
        names, pairs, recvs = landed(tag, after, "scatter")
        halves = [_chip_sum(p, r, sel_chip, name="chip_sum_" + n) for n, p, r in zip(names, pairs, recvs)]
        lands = [lax.empty(h.shape, F32) for h in halves]
        return begin(tag, names, halves, lands, len(names), _sibling_plan(len(names), lambda r, c: r), "pair_join")

    def reduce_step(tag, grads, after):
        if tag == "mlp":
            return swap_start("mlp", grads)
        if tag == "mix":
            return scatter_start("mlp", after) + swap_start("mix", grads)
        if tag == "in":
            return scatter_start("mix", after) + swap_start("in", grads)
        return scatter_start("in", after)

    loss_part, grad_x, dmod, g, gs, d_w_alpha = _local_step(x, positions, mod, loss_target, w_in_after, small,
                                                            more_weights, reduce_step)

    assert not g, list(g)
    gs_packed = _pack_small([gs[n] for n, _ in SMALL_RED], d_w_alpha, jnp.full((1, LANE), loss_part, F32),
                            name="pack_small")
    small_lands = [lax.empty((8, *a.shape), F32) for a in (dmod, gs_packed)]
    begin("small", ["dmod", "small"], [dmod, gs_packed], small_lands, 7 * 2, _gather8_plan(2), "gather8")

    res = {}

    def finish(tag, after):
        names, halves, theirs = landed(tag, after, "pair_join")
        for n, mine, other in zip(names, halves, theirs):
            if n == "w_in":
                south = ac == 0
                g_t = jnp.concatenate([jnp.where(south, mine, other), jnp.where(south, other, mine)], axis=0).T
                outs = _adamw(w_in[0].T, g_t, m_w_in[0].T, v_w_in[0].T, name="adamw_w_in", by_cols=True)
                res[n] = tuple(a.T for a in (g_t, *outs))
            else:
                res[n] = _adamw_halves(args[n][0], args["m_" + n][0], args["v_" + n][0], mine, other, sel_c,
                                       name="adamw_" + n)
        return res[names[-1]][1]

    join_start("mlp", grad_x)
    join_start("mix", grad_x)
    done = finish("mix", finish("mlp", grad_x))

    _, (dmod_own, gs_own), (dmod_all, gs_all) = landed("small", done, "gather8")
    dmod_all = lax.dynamic_update_slice(dmod_all, dmod_own[None], (dev, 0, 0)).reshape(8 * bsz, 6 * D)
    gs_all = lax.dynamic_update_slice(gs_all, gs_own[None], (dev, 0, 0))
    dmod_mine = lax.dynamic_slice(dmod_all, (0, chip * ADA_SHARD[1]), (8 * bsz, ADA_SHARD[1]))
    g_w_ada = _mm(c_all, dmod_mine, ta=True, a_fn=silu, name="ada_dw")
    wmv = [(args[n], args["m_" + n], args["v_" + n]) for n in names_small]
    wmv.append((gla_w_alpha[0], m_gla_w_alpha[0], v_gla_w_alpha[0]))
    res_small, loss_sum = _small_update(gs_all, dmod_all, sel_chip, wmv, name="small_update")
    res.update(res_small)
    loss = loss_sum * (0.5 / D)
    join_start("in", g_w_ada)
    res["w_ada"] = (g_w_ada, *_adamw(w_ada[0], g_w_ada, m_w_ada[0], v_w_ada[0], name="adamw_w_ada"))
    finish("in", res["w_ada"][1])

    order = ["w_ada", "b_ada", "norm1_g", "w_in", "b_merge", "gla_w_alpha", "gla_b_alpha", "gla_out_norm_g", "gla_w_o",
             "mla_q_lat_g", "mla_w_uq", "mla_kv_lat_g", "mla_w_ukv", "mla_qn_g", "mla_kn_g", "mla_w_o", "w_out",
             "norm2_g", "mlp_w1", "mlp_w2"]
    named = lambda k: [res[n][k].reshape(args[n].shape) for n in order]
    return (loss, grad_x, *named(0), *named(1), *named(2), *named(3))
```

```python
import jax
import jax.numpy as jnp
import numpy as np
from jax import lax
from jax.experimental import pallas as pl
from jax.experimental.pallas import tpu as pltpu

F32 = jnp.float32
BF16 = jnp.bfloat16
MESH = pl.DeviceIdType.MESH

D = 1024
CHUNK = 64
EPS = 1e-6
GH, GDK, GDV, GLR, GTAU = 4, 128, 256, 16, 16.0
MH, MQR, MKVR, MNOPE, MROPE, MVD = 16, 256, 128, 64, 32, 64
MQK = MNOPE + MROPE
DFF = 4 * D
ROPE_THETA = 10000.0
IN_WIDTH = 5552
LANE = 128
OFF_Q, OFF_K, OFF_V, OFF_G, OFF_MA, OFF_MB, OFF_CQ, OFF_CKV, OFF_A, OFF_KPE, PW = (
    0, 512, 1024, 2048, 3072, 4096, 5120, 5376, 5504, 5632, 5760)
ADAM_LR, ADAM_B1, ADAM_B2, ADAM_EPS, ADAM_WD, ADAM_STEP = 0.001, 0.9, 0.999, 1e-08, 0.01, 10
VMEM_LIMIT = 48 * 1024 * 1024


def _params(n_axes):
    return pltpu.CompilerParams(dimension_semantics=("arbitrary",) * n_axes, vmem_limit_bytes=VMEM_LIMIT)


def _tile(n, target):
    if n <= target:
        return n
    best = None
    for t in range(LANE, target + 1, LANE):
        if n % t == 0:
            best = t
    assert best is not None, (n, target)
    return best


def _sigmoid(x):
    return 1.0 / (1.0 + jnp.exp(-x))


MM_VMEM_BUDGET = 36 * 1024 * 1024


def _mm(a, b, *, name, ta=False, tb=False, out_dtype=F32, tm=1024, tn=1024, tk=4096,
        epilogue=None, extras=(), extra_specs=(), out_shape=None, out_specs=None, a_fn=None):
    if ta:
        kdim, m = a.shape
    else:
        m, kdim = a.shape
    if tb:
        n, k2 = b.shape
    else:
        k2, n = b.shape
    assert kdim == k2, (a.shape, b.shape)
    tm, tn, tk = _tile(m, tm), _tile(n, tn), _tile(kdim, tk)
    tiles = lambda rows: 2 * (rows * tk * a.dtype.itemsize + tk * tn * b.dtype.itemsize + rows * tn * 4) + rows * tn * 4
    while out_shape is None and tiles(tm) > MM_VMEM_BUDGET and tm % 256 == 0:
        tm //= 2
    nk = kdim // tk
    a_spec = pl.BlockSpec((tk, tm), lambda i, j, k: (k, i)) if ta else pl.BlockSpec((tm, tk), lambda i, j, k: (i, k))
    b_spec = pl.BlockSpec((tn, tk), lambda i, j, k: (j, k)) if tb else pl.BlockSpec((tk, tn), lambda i, j, k: (k, j))
    dims = (((0 if ta else 1,), (1 if tb else 0,)), ((), ()))
    ne = len(extras)
    if out_shape is None:
        out_shape = jax.ShapeDtypeStruct((m, n), out_dtype)
        out_specs = pl.BlockSpec((tm, tn), lambda i, j, k: (i, j))
    n_out = len(out_shape) if isinstance(out_shape, (list, tuple)) else 1
    in_place = epilogue is None and n_out == 1 and out_shape.dtype == F32
    scratch = [] if (nk == 1 or in_place) else [pltpu.VMEM((tm, tn), F32)]

    def body(a_ref, b_ref, *rest):
        ex, outs = rest[:ne], rest[ne:ne + n_out]
        av = a_ref[...] if a_fn is None else a_fn(a_ref[...])
        prod = lax.dot_general(av.astype(BF16), b_ref[...].astype(BF16), dims, preferred_element_type=F32)

        def finish(val):
            if epilogue is None:
                outs[0][...] = val.astype(outs[0].dtype)
            else:
                epilogue(val, ex, outs)

        if nk == 1:
            finish(prod)
            return
        k = pl.program_id(2)
        acc = outs[0] if in_place else rest[-1]

        @pl.when(k == 0)
        def _():
            acc[...] = prod

        @pl.when(k > 0)
        def _():
            acc[...] += prod

        if not in_place:
            @pl.when(k == nk - 1)
            def _():
                finish(acc[...])

    return pl.pallas_call(
        body, name=name, grid=(m // tm, n // tn, nk),
        in_specs=[a_spec, b_spec, *extra_specs], out_specs=out_specs, out_shape=out_shape,
        scratch_shapes=scratch, compiler_params=_params(3),
    )(a, b, *extras)


def _tile_spec(tm, tn):
    return pl.BlockSpec((tm, tn), lambda i, j, k: (i, j))


def _pieces_dx(pieces, w, after, *, name, tm=256):
    t = pieces[0][0].shape[0]
    tm = _tile(t, tm)
    npc = len(pieces)

    def body(*refs):
        p_refs, w_ref, out_ref = refs[:npc], refs[npc], refs[-1]
        acc = None
        for (arr, off), p_ref in zip(pieces, p_refs):
            part = lax.dot_general(p_ref[...].astype(BF16), w_ref[:, off:off + arr.shape[1]], _NT,
                                   preferred_element_type=F32)
            acc = part if acc is None else acc + part
        out_ref[...] = acc

    return pl.pallas_call(
        body, name=name, grid=(t // tm,),
        in_specs=[pl.BlockSpec((tm, arr.shape[1]), lambda i: (i, 0)) for arr, _ in pieces]
        + [pl.BlockSpec(w.shape, lambda i: (0, 0)), pl.BlockSpec((8, LANE), lambda i: (0, 0))],
        out_specs=pl.BlockSpec((tm, w.shape[0]), lambda i: (i, 0)),
        out_shape=jax.ShapeDtypeStruct((t, w.shape[0]), F32), compiler_params=_params(1),
    )(*[arr for arr, _ in pieces], w, after)


def _pieces_dw(h, pieces, *, name, tk=1024):
    t, d = h.shape
    tk = _tile(t, tk)
    widths = [p.shape[1] for p in pieces]
    starts = [sum(widths[:i]) for i in range(len(pieces))]

    def body(h_ref, *refs):
        p_refs, out_ref = refs[:-1], refs[-1]
        first = pl.program_id(0) == 0
        hv = h_ref[...]
        for p_ref, start, width in zip(p_refs, starts, widths):
            part = lax.dot_general(hv, p_ref[...].astype(BF16), _TN, preferred_element_type=F32)
            cols = slice(start, start + width)

            @pl.when(first)
            def _():
                out_ref[:, cols] = part

            @pl.when(jnp.logical_not(first))
            def _():
                out_ref[:, cols] += part

    return pl.pallas_call(
        body, name=name, grid=(t // tk,),
        in_specs=[pl.BlockSpec((tk, d), lambda k: (k, 0))] + [pl.BlockSpec((tk, wd), lambda k: (k, 0)) for wd in widths],
        out_specs=pl.BlockSpec((d, sum(widths)), lambda k: (0, 0)),
        out_shape=jax.ShapeDtypeStruct((d, sum(widths)), F32), compiler_params=_params(1),
    )(h, *pieces)


def _rms(x, g):
    r = lax.rsqrt(jnp.mean(x * x, axis=-1, keepdims=True) + EPS)
    return x * r, r


def _row_spec(ts, width, col=0):
    return pl.BlockSpec((None, ts, width), lambda b, i: (b, i, col))


def _vec_spec(width):
    return pl.BlockSpec((None, 1, width), lambda b, i: (b, 0, 0))


def _gain_spec(width):
    return pl.BlockSpec((1, width), lambda b, i: (0, 0))


def _norm_mod(x, g, scale, shift, *, name, ts=512):
    bsz, s, d = x.shape
    ts = min(ts, s)

    def body(x_ref, g_ref, sc_ref, sh_ref, h_ref):
        xh, _ = _rms(x_ref[...], None)
        h_ref[...] = ((xh * g_ref[...]) * (1.0 + sc_ref[...]) + sh_ref[...]).astype(BF16)

    return pl.pallas_call(
        body, name=name, grid=(bsz, s // ts),
        in_specs=[_row_spec(ts, d), _gain_spec(d), _vec_spec(d), _vec_spec(d)],
        out_specs=_row_spec(ts, d), out_shape=jax.ShapeDtypeStruct((bsz, s, d), BF16),
        compiler_params=_params(2),
    )(x, g, scale, shift)


def _resid_norm_mod(x, mixed, gate, g, scale, shift, *, name, ts=512):
    bsz, s, d = x.shape
    ts = min(ts, s)

    def body(x_ref, mx_ref, gt_ref, g_ref, sc_ref, sh_ref, x1_ref, h_ref):
        x1 = x_ref[...] + gt_ref[...] * mx_ref[...]
        x1_ref[...] = x1
        xh, _ = _rms(x1, None)
        h_ref[...] = ((xh * g_ref[...]) * (1.0 + sc_ref[...]) + sh_ref[...]).astype(BF16)

    return pl.pallas_call(
        body, name=name, grid=(bsz, s // ts),
        in_specs=[_row_spec(ts, d), _row_spec(ts, d), _vec_spec(d), _gain_spec(d), _vec_spec(d), _vec_spec(d)],
        out_specs=[_row_spec(ts, d), _row_spec(ts, d)],
        out_shape=[jax.ShapeDtypeStruct((bsz, s, d), F32), jax.ShapeDtypeStruct((bsz, s, d), BF16)],
        compiler_params=_params(2),
    )(x, mixed, gate, g, scale, shift)


def _norm_mod_bwd(dh, xin, resid, g, scale, gate=None, mixed=None, *, name, ts=512):
    bsz, s, d = xin.shape
    ts = min(ts, s)
    gated = gate is not None

    def body(*refs):
        if gated:
            dh_ref, x_ref, rs_ref, g_ref, sc_ref, gt_ref, mx_ref, dx_ref, dsc_ref, dsh_ref, dg_ref, dgt_ref, dmx_ref = refs
        else:
            dh_ref, x_ref, rs_ref, g_ref, sc_ref, dx_ref, dsc_ref, dsh_ref, dg_ref = refs
        b, i = pl.program_id(0), pl.program_id(1)

        @pl.when(i == 0)
        def _():
            dsc_ref[...] = jnp.zeros_like(dsc_ref)
            dsh_ref[...] = jnp.zeros_like(dsh_ref)
            if gated:
                dgt_ref[...] = jnp.zeros_like(dgt_ref)

        @pl.when((i == 0) & (b == 0))
        def _():
            dg_ref[...] = jnp.zeros_like(dg_ref)

        dh_v, gv = dh_ref[...], g_ref[...]
        xh, r = _rms(x_ref[...], None)
        dsc_ref[...] += jnp.sum(dh_v * (xh * gv), axis=0, keepdims=True)
        dsh_ref[...] += jnp.sum(dh_v, axis=0, keepdims=True)
        dn = dh_v * (1.0 + sc_ref[...])
        dg_ref[...] += jnp.sum(dn * xh, axis=0, keepdims=True)
        dxh = dn * gv
        dx = rs_ref[...] + r * (dxh - xh * jnp.mean(dxh * xh, axis=-1, keepdims=True))
        dx_ref[...] = dx
        if gated:
            dgt_ref[...] += jnp.sum(dx * mx_ref[...], axis=0, keepdims=True)
            dmx_ref[...] = (dx * gt_ref[...]).astype(BF16)

    ins = [dh, xin, resid, g, scale]
    in_specs = [_row_spec(ts, d), _row_spec(ts, d), _row_spec(ts, d), _gain_spec(d), _vec_spec(d)]
    out_specs = [_row_spec(ts, d), _vec_spec(d), _vec_spec(d), _gain_spec(d)]
    out_shape = [jax.ShapeDtypeStruct((bsz, s, d), F32), jax.ShapeDtypeStruct((bsz, 1, d), F32),
                 jax.ShapeDtypeStruct((bsz, 1, d), F32), jax.ShapeDtypeStruct((1, d), F32)]
    if gated:
        ins += [gate, mixed]
        in_specs += [_vec_spec(d), _row_spec(ts, d)]
        out_specs += [_vec_spec(d), _row_spec(ts, d)]
        out_shape += [jax.ShapeDtypeStruct((bsz, 1, d), F32), jax.ShapeDtypeStruct((bsz, s, d), BF16)]
    return pl.pallas_call(
        body, name=name, grid=(bsz, s // ts), in_specs=in_specs, out_specs=out_specs, out_shape=out_shape,
        compiler_params=_params(2),
    )(*ins)


def _loss_head(x1, ff, gate2, target, *, name, ts=512):
    bsz, s, d = x1.shape
    ts = min(ts, s)

    def body(x1_ref, ff_ref, gt_ref, t_ref, dy_ref, dff_ref, dgt_ref, loss_ref, acc):
        b, i = pl.program_id(0), pl.program_id(1)

        @pl.when(i == 0)
        def _():
            dgt_ref[...] = jnp.zeros_like(dgt_ref)

        @pl.when((i == 0) & (b == 0))
        def _():
            acc[...] = jnp.zeros_like(acc)

        ffv, gt = ff_ref[...], gt_ref[...]
        diff = (x1_ref[...] + gt * ffv) - t_ref[...]
        acc[...] += jnp.sum((diff * diff).reshape(ts // 8, 8, d), axis=0)
        dy = diff * (1.0 / d)
        dy_ref[...] = dy
        dgt_ref[...] += jnp.sum(dy * ffv, axis=0, keepdims=True)
        dff_ref[...] = (dy * gt).astype(BF16)

        @pl.when((i == pl.num_programs(1) - 1) & (b == pl.num_programs(0) - 1))
        def _():
            loss_ref[...] = jnp.full(loss_ref.shape, jnp.sum(acc[...]), F32)

    return pl.pallas_call(
        body, name=name, grid=(bsz, s // ts),
        in_specs=[_row_spec(ts, d), _row_spec(ts, d), _vec_spec(d), _row_spec(ts, d)],
        out_specs=[_row_spec(ts, d), _row_spec(ts, d), _vec_spec(d), pl.BlockSpec((8, LANE), lambda b, i: (0, 0))],
        out_shape=[jax.ShapeDtypeStruct((bsz, s, d), F32), jax.ShapeDtypeStruct((bsz, s, d), BF16),
                   jax.ShapeDtypeStruct((bsz, 1, d), F32), jax.ShapeDtypeStruct((8, LANE), F32)],
        scratch_shapes=[pltpu.VMEM((8, d), F32)], compiler_params=_params(2),
    )(x1, ff, gate2, target)


def _merge_fwd(proj, b_merge, y_a, y_b, *, name, ts=512):
    bsz, s, _ = proj.shape
    ts = min(ts, s)

    def body(la_ref, lb_ref, ba_ref, bb_ref, ya_ref, yb_ref, out_ref):
        ga = _sigmoid(la_ref[...] + ba_ref[...])
        gb = _sigmoid(lb_ref[...] + bb_ref[...])
        out_ref[...] = (ga * ya_ref[...] + gb * yb_ref[...]).astype(BF16)

    return pl.pallas_call(
        body, name=name, grid=(bsz, s // ts),
        in_specs=[_row_spec(ts, D, OFF_MA // D), _row_spec(ts, D, OFF_MB // D),
                  pl.BlockSpec((1, D), lambda b, i: (0, 0)), pl.BlockSpec((1, D), lambda b, i: (0, 1)),
                  _row_spec(ts, D), _row_spec(ts, D)],
        out_specs=_row_spec(ts, D), out_shape=jax.ShapeDtypeStruct((bsz, s, D), BF16),
        compiler_params=_params(2),
    )(proj, proj, b_merge, b_merge, y_a, y_b)


def _merge_bwd(dmi, proj, b_merge, y_a, y_b, *, name, ts=512):
    bsz, s, _ = proj.shape
    ts = min(ts, s)

    def body(d_ref, la_ref, lb_ref, ba_ref, bb_ref, ya_ref, yb_ref, dya_ref, dyb_ref, dla_ref, dlb_ref, dba_ref, dbb_ref):
        @pl.when((pl.program_id(0) == 0) & (pl.program_id(1) == 0))
        def _():
            dba_ref[...] = jnp.zeros_like(dba_ref)
            dbb_ref[...] = jnp.zeros_like(dbb_ref)

        dv = d_ref[...].astype(F32)
        ga = _sigmoid(la_ref[...] + ba_ref[...])
        gb = _sigmoid(lb_ref[...] + bb_ref[...])
        dya_ref[...] = (dv * ga).astype(BF16)
        dyb_ref[...] = (dv * gb).astype(BF16)
        dla = (dv * ya_ref[...]) * (ga * (1.0 - ga))
        dlb = (dv * yb_ref[...]) * (gb * (1.0 - gb))
        dla_ref[...] = dla.astype(BF16)
        dlb_ref[...] = dlb.astype(BF16)
        dba_ref[...] += jnp.sum(dla, axis=0, keepdims=True)
        dbb_ref[...] += jnp.sum(dlb, axis=0, keepdims=True)

    act = jax.ShapeDtypeStruct((bsz, s, D), BF16)
    return pl.pallas_call(
        body, name=name, grid=(bsz, s // ts),
        in_specs=[_row_spec(ts, D), _row_spec(ts, D, OFF_MA // D), _row_spec(ts, D, OFF_MB // D),
                  pl.BlockSpec((1, D), lambda b, i: (0, 0)), pl.BlockSpec((1, D), lambda b, i: (0, 1)),
                  _row_spec(ts, D), _row_spec(ts, D)],
        out_specs=[_row_spec(ts, D)] * 4 + [_gain_spec(D)] * 2,
        out_shape=[act, act, act, act, jax.ShapeDtypeStruct((1, D), F32), jax.ShapeDtypeStruct((1, D), F32)],
        compiler_params=_params(2),
    )(dmi, proj, proj, b_merge, b_merge, y_a, y_b)


def _tri(lower):
    r = lax.broadcasted_iota(jnp.int32, (CHUNK, CHUNK), 0)
    c = lax.broadcasted_iota(jnp.int32, (CHUNK, CHUNK), 1)
    return jnp.where((c <= r) if lower else (c >= r), 1.0, 0.0).astype(F32)


def _gla_logits(a_ref, wal_ref, bal_ref):
    logits = jnp.dot(a_ref[...].astype(BF16), wal_ref[...].astype(BF16), preferred_element_type=F32) + bal_ref[...]
    la = (jnp.minimum(logits, 0.0) - jnp.log(1.0 + jnp.exp(-jnp.abs(logits)))) * (1.0 / GTAU)
    return logits, la


def _chunk_cumsum(la_n, tri, precision=lax.Precision.HIGHEST):
    cum = jnp.dot(tri, la_n, preferred_element_type=F32, precision=precision)
    return cum, jnp.sum(la_n, axis=0, keepdims=True)


def _gla_specs(s, nc):
    def blk(width, off):
        return pl.BlockSpec((None, s, width), lambda h, b: (b, 0, off // width + h))

    proj_specs = [blk(GDK, OFF_Q), blk(GDK, OFF_K), blk(GDV, OFF_V), blk(GDV, OFF_G),
                  pl.BlockSpec((None, s, LANE), lambda h, b: (b, 0, OFF_A // LANE)),
                  pl.BlockSpec((LANE, GDK), lambda h, b: (0, h)), pl.BlockSpec((1, GDK), lambda h, b: (0, h)),
                  pl.BlockSpec((1, GDV), lambda h, b: (0, 0))]
    st_spec = pl.BlockSpec((None, None, nc, GDV, GDK), lambda h, b: (b, h, 0, 0, 0))
    return blk, proj_specs, st_spec


def _gla_fwd(proj, w_alpha_p, b_alpha, out_norm_g, *, name):
    bsz, s, _ = proj.shape
    nc = s // CHUNK
    scale = GDK ** -0.5

    rb = min(512, s)

    def body(q_ref, k_ref, v_ref, g_ref, a_ref, wal_ref, bal_ref, ong_ref, o_ref, og_ref, st_ref):
        _, la = _gla_logits(a_ref, wal_ref, bal_ref)
        tri = _tri(True)
        st = jnp.zeros((GDV, GDK), F32)
        for n in range(nc):
            rows = pl.ds(n * CHUNK, CHUNK)
            cum, cum_end = _chunk_cumsum(la[n * CHUNK:(n + 1) * CHUNK], tri, lax.Precision.HIGH)
            kd = k_ref[rows, :] * jnp.exp(cum_end - cum)
            ut = lax.dot_general(v_ref[rows, :].astype(BF16), kd.astype(BF16), _TN, preferred_element_type=F32)
            st = st * jnp.exp(cum_end) + ut
            st_ref[n] = st
            o_ref[rows, :] = lax.dot_general((q_ref[rows, :].astype(F32) * scale).astype(BF16), st.astype(BF16), _NT,
                                             preferred_element_type=F32)
        for j in range(0, s, rb):
            blk_rows = pl.ds(j, rb)
            oh, _ = _rms(o_ref[blk_rows, :], None)
            gv = g_ref[blk_rows, :].astype(F32)
            og_ref[blk_rows, :] = ((oh * ong_ref[...]) * (gv * _sigmoid(gv))).astype(BF16)

    blk, proj_specs, st_spec = _gla_specs(s, nc)
    return pl.pallas_call(
        body, name=name, grid=(GH, bsz), in_specs=proj_specs, out_specs=[blk(GDV, 0), blk(GDV, 0), st_spec],
        out_shape=[jax.ShapeDtypeStruct((bsz, s, GH * GDV), F32), jax.ShapeDtypeStruct((bsz, s, GH * GDV), BF16),
                   jax.ShapeDtypeStruct((bsz, GH, nc, GDV, GDK), F32)],
        compiler_params=_params(2),
    )(proj, proj, proj, proj, proj, w_alpha_p, b_alpha, out_norm_g)


def _gla_bwd(dog, o, states, proj, w_alpha_p, b_alpha, out_norm_g, *, name):
    bsz, s, _ = proj.shape
    nc = s // CHUNK
    scale = GDK ** -0.5

    def body(dog_ref, o_ref, st_ref, q_ref, k_ref, v_ref, g_ref, a_ref, wal_ref, bal_ref, ong_ref,
             dq_ref, dk_ref, dv_ref, dg_ref, dl_ref, dbal_ref, dong_ref, do_scr, dlog_scr):
        h, b = pl.program_id(0), pl.program_id(1)

        @pl.when(b == 0)
        def _():
            dbal_ref[...] = jnp.zeros_like(dbal_ref)

        @pl.when((b == 0) & (h == 0))
        def _():
            dong_ref[...] = jnp.zeros_like(dong_ref)

        ong = ong_ref[...]
        for j in range(0, s, rb):
            blk_rows = pl.ds(j, rb)
            gv, dogv = g_ref[blk_rows, :].astype(F32), dog_ref[blk_rows, :]
            sg = _sigmoid(gv)
            oh, r = _rms(o_ref[blk_rows, :], None)
            don = dogv * (gv * sg)
            dg_ref[blk_rows, :] = (dogv * (oh * ong) * (sg * (1.0 + gv * (1.0 - sg)))).astype(BF16)
            dong_ref[...] += jnp.sum(don * oh, axis=0, keepdims=True)
            doh = don * ong
            do_scr[blk_rows, :] = (r * (doh - oh * jnp.mean(doh * oh, axis=-1, keepdims=True))).astype(BF16)

        logits, la = _gla_logits(a_ref, wal_ref, bal_ref)
        tri_lo, tri_up = _tri(True), _tri(False)
        carry = jnp.zeros((GDV, GDK), F32)
        for n in range(nc - 1, -1, -1):
            rows = pl.ds(n * CHUNK, CHUNK)
            cum, cum_end = _chunk_cumsum(la[n * CHUNK:(n + 1) * CHUNK], tri_lo)
            decay = jnp.exp(cum_end)
            w = jnp.exp(cum_end - cum)
            kd = k_ref[rows, :] * w
            do_b = do_scr[rows, :]
            qs_b = (q_ref[rows, :].astype(F32) * scale).astype(BF16)
            dq_ref[rows, :] = (jnp.dot(do_b, st_ref[n].astype(BF16), preferred_element_type=F32) * scale).astype(BF16)
            dsn = lax.dot_general(do_b, qs_b, _TN, preferred_element_type=F32) + carry
            carry = dsn * decay
            dsn_b = dsn.astype(BF16)
            dv_ref[rows, :] = lax.dot_general(kd.astype(BF16), dsn_b, _NT, preferred_element_type=F32).astype(BF16)
            dkd = jnp.dot(v_ref[rows, :].astype(BF16), dsn_b, preferred_element_type=F32)
            dk_ref[rows, :] = (dkd * w).astype(BF16)
            e = dkd * kd
            dcum_end = jnp.sum(e, axis=0, keepdims=True)
            if n > 0:
                dcum_end += jnp.sum(dsn * st_ref[n - 1], axis=0, keepdims=True) * decay
            dlog_scr[rows, :] = dcum_end - jnp.dot(tri_up, e, preferred_element_type=F32,
                                                  precision=lax.Precision.HIGHEST)
        dlog = dlog_scr[...] * (1.0 / GTAU) * (1.0 - _sigmoid(logits))
        dl_ref[...] = dlog.astype(BF16)
        dbal_ref[...] += jnp.sum(dlog, axis=0, keepdims=True)

    rb = min(512, s)

    blk, proj_specs, st_spec = _gla_specs(s, nc)
    act = lambda wd: jax.ShapeDtypeStruct((bsz, s, wd), BF16)
    return pl.pallas_call(
        body, name=name, grid=(GH, bsz), in_specs=[blk(GDV, 0), blk(GDV, 0), st_spec, *proj_specs],
        out_specs=[blk(GDK, 0), blk(GDK, 0), blk(GDV, 0), blk(GDV, 0), blk(GDK, 0),
                   pl.BlockSpec((1, GDK), lambda h, b: (0, h)), pl.BlockSpec((1, GDV), lambda h, b: (0, 0))],
        out_shape=[act(GH * GDK), act(GH * GDK), act(GH * GDV), act(GH * GDV), act(GH * GDK),
                   jax.ShapeDtypeStruct((1, GH * GDK), F32), jax.ShapeDtypeStruct((1, GDV), F32)],
        scratch_shapes=[pltpu.VMEM((s, GDV), BF16), pltpu.VMEM((s, GDK), F32)], compiler_params=_params(2),
    )(dog, o, states, proj, proj, proj, proj, proj, w_alpha_p, b_alpha, out_norm_g)


def _lane():
    return lax.broadcasted_iota(jnp.int32, (1, LANE), 1)


def _swap_halves(x):
    lane = _lane()
    half = MROPE // 2
    lo = (lane >= MNOPE) & (lane < MNOPE + half)
    hi = (lane >= MNOPE + half) & (lane < MQK)
    return jnp.where(lo, pltpu.roll(x, LANE - half, 1), jnp.where(hi, pltpu.roll(x, half, 1), 0.0))


def _norm96(x, g):
    r = lax.rsqrt(jnp.sum(x * x, axis=-1, keepdims=True) * (1.0 / MQK) + EPS)
    return x * r, r


def _lat_norm(proj, q_lat_g, kv_lat_g, *, name, ts=512):
    t = proj.shape[0]
    ts = min(ts, t)

    def body(cq_ref, ckv_ref, gq_ref, gk_ref, oq_ref, ok_ref):
        xq, _ = _rms(cq_ref[...].astype(F32), None)
        oq_ref[...] = (xq * gq_ref[...]).astype(BF16)
        xk, _ = _rms(ckv_ref[...].astype(F32), None)
        ok_ref[...] = (xk * gk_ref[...]).astype(BF16)

    return pl.pallas_call(
        body, name=name, grid=(t // ts,),
        in_specs=[pl.BlockSpec((ts, MQR), lambda i: (i, OFF_CQ // MQR)), pl.BlockSpec((ts, MKVR), lambda i: (i, OFF_CKV // MKVR)),
                  pl.BlockSpec((1, MQR), lambda i: (0, 0)), pl.BlockSpec((1, MKVR), lambda i: (0, 0))],
        out_specs=[pl.BlockSpec((ts, MQR), lambda i: (i, 0)), pl.BlockSpec((ts, MKVR), lambda i: (i, 0))],
        out_shape=[jax.ShapeDtypeStruct((t, MQR), BF16), jax.ShapeDtypeStruct((t, MKVR), BF16)],
        compiler_params=_params(1),
    )(proj, proj, q_lat_g, kv_lat_g)


def _lat_norm_bwd(dcqn, dckvn, proj, q_lat_g, kv_lat_g, *, name, ts=512):
    t = proj.shape[0]
    ts = min(ts, t)

    def one(d_ref, x_ref, g_ref, dx_ref, dg_ref):
        xh, r = _rms(x_ref[...].astype(F32), None)
        dn = d_ref[...]
        dg_ref[...] += jnp.sum(dn * xh, axis=0, keepdims=True)
        dxh = dn * g_ref[...]
        dx_ref[...] = (r * (dxh - xh * jnp.mean(dxh * xh, axis=-1, keepdims=True))).astype(BF16)

    def body(dq_ref, dk_ref, cq_ref, ckv_ref, gq_ref, gk_ref, dxq_ref, dxk_ref, dgq_ref, dgk_ref):
        @pl.when(pl.program_id(0) == 0)
        def _():
            dgq_ref[...] = jnp.zeros_like(dgq_ref)
            dgk_ref[...] = jnp.zeros_like(dgk_ref)

        one(dq_ref, cq_ref, gq_ref, dxq_ref, dgq_ref)
        one(dk_ref, ckv_ref, gk_ref, dxk_ref, dgk_ref)

    return pl.pallas_call(
        body, name=name, grid=(t // ts,),
        in_specs=[pl.BlockSpec((ts, MQR), lambda i: (i, 0)), pl.BlockSpec((ts, MKVR), lambda i: (i, 0)),
                  pl.BlockSpec((ts, MQR), lambda i: (i, OFF_CQ // MQR)), pl.BlockSpec((ts, MKVR), lambda i: (i, OFF_CKV // MKVR)),
                  pl.BlockSpec((1, MQR), lambda i: (0, 0)), pl.BlockSpec((1, MKVR), lambda i: (0, 0))],
        out_specs=[pl.BlockSpec((ts, MQR), lambda i: (i, 0)), pl.BlockSpec((ts, MKVR), lambda i: (i, 0)),
                   pl.BlockSpec((1, MQR), lambda i: (0, 0)), pl.BlockSpec((1, MKVR), lambda i: (0, 0))],
        out_shape=[jax.ShapeDtypeStruct((t, MQR), BF16), jax.ShapeDtypeStruct((t, MKVR), BF16),
                   jax.ShapeDtypeStruct((1, MQR), F32), jax.ShapeDtypeStruct((1, MKVR), F32)],
        compiler_params=_params(1),
    )(dcqn, dckvn, proj, proj, q_lat_g, kv_lat_g)


def _qk_prep(q_raw, kv, proj, cos_t, sin_t, gq, gk, *, name, ts=2048):
    t = q_raw.shape[0]
    ts = min(ts, t)

    def body(q_ref, kv_ref, kpe_ref, c_ref, s_ref, gq_ref, gk_ref, qo_ref, ko_ref, vo_ref):
        cs, sn = c_ref[...], s_ref[...]
        nope = _lane() < MNOPE
        qn, _ = _norm96(q_ref[...].astype(F32), None)
        qn = qn * gq_ref[...]
        qo_ref[...] = (qn * cs + _swap_halves(qn) * sn).astype(BF16)
        kvv = kv_ref[...].astype(F32)
        kn, _ = _norm96(jnp.where(nope, kvv, kpe_ref[...].astype(F32)), None)
        kn = kn * gk_ref[...]
        ko_ref[...] = (kn * cs + _swap_halves(kn) * sn).astype(BF16)
        vo_ref[...] = jnp.where(nope, pltpu.roll(kvv, MNOPE, 1), 0.0).astype(BF16)

    hd = pl.BlockSpec((ts, LANE), lambda i, h: (i, h))
    shared = lambda col: pl.BlockSpec((ts, LANE), lambda i, h: (i, col))
    gain = pl.BlockSpec((1, LANE), lambda i, h: (0, 0))
    out = jax.ShapeDtypeStruct((t, MH * LANE), BF16)
    return pl.pallas_call(
        body, name=name, grid=(t // ts, MH),
        in_specs=[hd, hd, shared(OFF_KPE // LANE), shared(0), shared(0), gain, gain],
        out_specs=[hd, hd, hd], out_shape=[out, out, out], compiler_params=_params(2),
    )(q_raw, kv, proj, cos_t, sin_t, gq, gk)


def _qk_prep_bwd(dq, dk, dv, q_raw, kv, proj, cos_t, sin_t, gq, gk, *, name, ts=2048):
    t = q_raw.shape[0]
    ts = min(ts, t)

    def norm_bwd(dy, x, g, dg_ref):
        xh, r = _norm96(x, None)
        dg_ref[...] += jnp.sum(dy * xh, axis=0, keepdims=True)
        dxh = dy * g
        return r * (dxh - xh * (jnp.sum(dxh * xh, axis=-1, keepdims=True) * (1.0 / MQK)))

    def body(dq_ref, dk_ref, dv_ref, q_ref, kv_ref, kpe_ref, c_ref, s_ref, gq_ref, gk_ref,
             dqr_ref, dkv_ref, dkpe_ref, dgq_ref, dgk_ref):
        i, h = pl.program_id(0), pl.program_id(1)

        @pl.when(h == 0)
        def _():
            dkpe_ref[...] = jnp.zeros_like(dkpe_ref)

        @pl.when((h == 0) & (i == 0))
        def _():
            dgq_ref[...] = jnp.zeros_like(dgq_ref)
            dgk_ref[...] = jnp.zeros_like(dgk_ref)

        cs, sn = c_ref[...], s_ref[...]
        lane = _lane()
        nope = lane < MNOPE
        dqv = dq_ref[...]
        dqn = dqv * cs + _swap_halves(dqv * sn)
        dqr_ref[...] = norm_bwd(dqn, q_ref[...].astype(F32), gq_ref[...], dgq_ref).astype(BF16)
        dkv_ = dk_ref[...]
        dkn = dkv_ * cs + _swap_halves(dkv_ * sn)
        kvv = kv_ref[...].astype(F32)
        dkr = norm_bwd(dkn, jnp.where(nope, kvv, kpe_ref[...].astype(F32)), gk_ref[...], dgk_ref)
        dkv_ref[...] = jnp.where(nope, dkr, pltpu.roll(dv_ref[...], MNOPE, 1)).astype(BF16)
        dkpe_ref[...] += jnp.where((lane >= MNOPE) & (lane < MQK), dkr, 0.0)

    hd = pl.BlockSpec((ts, LANE), lambda i, h: (i, h))
    shared = lambda col: pl.BlockSpec((ts, LANE), lambda i, h: (i, col))
    gain = pl.BlockSpec((1, LANE), lambda i, h: (0, 0))
    out = jax.ShapeDtypeStruct((t, MH * LANE), BF16)
    return pl.pallas_call(
        body, name=name, grid=(t // ts, MH),
        in_specs=[hd, hd, hd, hd, hd, shared(OFF_KPE // LANE), shared(0), shared(0), gain, gain],
        out_specs=[hd, hd, shared(0), gain, gain],
        out_shape=[out, out, jax.ShapeDtypeStruct((t, LANE), F32), jax.ShapeDtypeStruct((1, LANE), F32),
                   jax.ShapeDtypeStruct((1, LANE), F32)],
        compiler_params=_params(2),
    )(dq, dk, dv, q_raw, kv, proj, cos_t, sin_t, gq, gk)


_NT = (((1,), (1,)), ((), ()))
_TN = (((0,), (0,)), ((), ()))


SOFTMAX_SCALE = MQK ** -0.5
Q_PRESCALE = SOFTMAX_SCALE * float(np.log2(np.e))


def _attn_weights(q, k_ref, lo, tq):
    row = lax.broadcasted_iota(jnp.int32, (tq, tq), 0) // CHUNK
    col = lax.broadcasted_iota(jnp.int32, (tq, tq), 1) // CHUNK
    sd = lax.dot_general(q, k_ref[pl.ds(lo, tq), :], _NT, preferred_element_type=F32)
    sd = jnp.where(col <= row, sd, -1e30)
    m = jnp.max(sd, axis=-1, keepdims=True)
    if lo:
        so = lax.dot_general(q, k_ref[pl.ds(0, lo), :], _NT, preferred_element_type=F32)
        m = jnp.maximum(m, jnp.max(so, axis=-1, keepdims=True))
        eo = jnp.exp2(so - m)
        ed = jnp.exp2(sd - m)
        return eo, ed, 1.0 / (jnp.sum(eo, axis=-1, keepdims=True) + jnp.sum(ed, axis=-1, keepdims=True))
    ed = jnp.exp2(sd - m)
    return None, ed, 1.0 / jnp.sum(ed, axis=-1, keepdims=True)


def _attn_fwd(q, k, v, *, name, tq=256):
    bsz, s, _ = q.shape
    tq = min(tq, s)

    def body(q_ref, k_ref, v_ref, o_ref):
        for i in range(s // tq):
            lo = i * tq
            eo, ed, inv = _attn_weights(q_ref[pl.ds(lo, tq), :], k_ref, lo, tq)
            o = jnp.dot(ed.astype(BF16), v_ref[pl.ds(lo, tq), :], preferred_element_type=F32)
            if lo:
                o += jnp.dot(eo.astype(BF16), v_ref[pl.ds(0, lo), :], preferred_element_type=F32)
            o_ref[pl.ds(lo, tq), :] = (o * inv).astype(BF16)

    spec = pl.BlockSpec((None, s, LANE), lambda b, h: (b, 0, h))
    return pl.pallas_call(
        body, name=name, grid=(bsz, MH), in_specs=[spec, spec, spec], out_specs=spec,
        out_shape=jax.ShapeDtypeStruct((bsz, s, MH * LANE), BF16), compiler_params=_params(2),
    )(q, k, v)


def _attn_bwd(q, k, v, do, *, name, tq=256):
    bsz, s, _ = q.shape
    tq = min(tq, s)

    def body(q_ref, k_ref, v_ref, do_ref, dq_ref, dk_ref, dv_ref):
        dk_ref[...] = jnp.zeros_like(dk_ref)
        dv_ref[...] = jnp.zeros_like(dv_ref)
        for i in range(s // tq):
            lo = i * tq
            here, before = pl.ds(lo, tq), pl.ds(0, lo)
            qv, dov = q_ref[here, :], do_ref[here, :]
            eo, ed, inv = _attn_weights(qv, k_ref, lo, tq)
            do_n = (dov.astype(F32) * inv).astype(BF16)
            dv_ref[here, :] += lax.dot_general(ed.astype(BF16), do_n, _TN, preferred_element_type=F32)
            dpd = lax.dot_general(dov, v_ref[here, :], _NT, preferred_element_type=F32)
            delta = jnp.sum(dpd * ed, axis=-1, keepdims=True)
            if lo:
                dv_ref[before, :] += lax.dot_general(eo.astype(BF16), do_n, _TN, preferred_element_type=F32)
                dpo = lax.dot_general(dov, v_ref[before, :], _NT, preferred_element_type=F32)
                delta += jnp.sum(dpo * eo, axis=-1, keepdims=True)
            delta = delta * inv
            r = inv * SOFTMAX_SCALE
            dsd = (ed * (dpd - delta) * r).astype(BF16)
            dq = jnp.dot(dsd, k_ref[here, :], preferred_element_type=F32)
            dk_ref[here, :] += lax.dot_general(dsd, qv, _TN, preferred_element_type=F32)
            if lo:
                dso = (eo * (dpo - delta) * r).astype(BF16)
                dq += jnp.dot(dso, k_ref[before, :], preferred_element_type=F32)
                dk_ref[before, :] += lax.dot_general(dso, qv, _TN, preferred_element_type=F32)
            dq_ref[here, :] = dq
        dk_ref[...] = dk_ref[...] * (1.0 / Q_PRESCALE)

    spec = pl.BlockSpec((None, s, LANE), lambda b, h: (b, 0, h))
    out = jax.ShapeDtypeStruct((bsz, s, MH * LANE), F32)
    return pl.pallas_call(
        body, name=name, grid=(bsz, MH), in_specs=[spec] * 4, out_specs=[spec] * 3, out_shape=[out, out, out],
        compiler_params=_params(2),
    )(q, k, v, do)


def _adamw(w, g, m, v, *, name, tr=256, by_cols=False):
    rows, cols = w.shape
    tr = _tile_rows(rows, tr)

    def body(w_ref, g_ref, m_ref, v_ref, d_ref, nm_ref, nv_ref):
        d_ref[...], nm_ref[...], nv_ref[...] = _adamw_update(w_ref[...], g_ref[...], m_ref[...], v_ref[...])

    spec = pl.BlockSpec((rows, LANE), lambda i: (0, i)) if by_cols else pl.BlockSpec((tr, cols), lambda i: (i, 0))
    out = jax.ShapeDtypeStruct((rows, cols), F32)
    return pl.pallas_call(body, name=name, grid=(cols // LANE if by_cols else rows // tr,), in_specs=[spec] * 4,
                          out_specs=[spec] * 3, out_shape=[out, out, out], compiler_params=_params(1))(w, g, m, v)


def _tile_rows(rows, target):
    if rows <= target:
        return rows
    best = 8
    for t in range(8, target + 1, 8):
        if rows % t == 0:
            best = t
    return best


def _adamw_update(w, g, m, v):
    nm = ADAM_B1 * m + (1.0 - ADAM_B1) * g
    nv = ADAM_B2 * v + (1.0 - ADAM_B2) * (g * g)
    m_hat = nm / (1.0 - ADAM_B1 ** ADAM_STEP)
    v_hat = nv / (1.0 - ADAM_B2 ** ADAM_STEP)
    return -ADAM_LR * (m_hat / (jnp.sqrt(v_hat) + ADAM_EPS) + ADAM_WD * w), nm, nv


def _adamw_halves(w, m, v, mine, theirs, sel, *, name, tr=256):
    rows, cols = w.shape
    tr = _tile_rows(rows // 2, tr)
    nh = rows // 2 // tr

    def body(sel_ref, w_ref, m_ref, v_ref, mine_ref, theirs_ref, g_ref, d_ref, nm_ref, nv_ref):
        lower = pl.program_id(0) < nh
        south = sel_ref[0] == 0
        gv = jnp.where(lower == south, mine_ref[...], theirs_ref[...])
        g_ref[...] = gv
        d_ref[...], nm_ref[...], nv_ref[...] = _adamw_update(w_ref[...], gv, m_ref[...], v_ref[...])

    full = pl.BlockSpec((tr, cols), lambda i, sel_ref: (i, 0))
    half = pl.BlockSpec((tr, cols), lambda i, sel_ref: (i % nh, 0))
    out = jax.ShapeDtypeStruct((rows, cols), F32)
    return pl.pallas_call(
        body, name=name, out_shape=[out] * 4, compiler_params=_params(1),
        grid_spec=pltpu.PrefetchScalarGridSpec(num_scalar_prefetch=1, grid=(rows // tr,),
                                               in_specs=[full, full, full, half, half], out_specs=[full] * 4),
    )(sel, w, m, v, mine, theirs)


def _pair_add(x, sib, sel, *, name, tr=256):
    n, _, rows, cols = x.shape
    tr = _tile_rows(rows, tr)

    def body(sel_ref, x_ref, s_ref, o_ref):
        o_ref[...] = (x_ref[...] + s_ref[...]).astype(BF16)

    spec = pl.BlockSpec((None, tr, cols), lambda j, i, sel_ref: (j, i, 0))
    return pl.pallas_call(
        body, name=name, out_shape=jax.ShapeDtypeStruct((n, rows, cols), BF16), compiler_params=_params(2),
        grid_spec=pltpu.PrefetchScalarGridSpec(
            num_scalar_prefetch=1, grid=(n, rows // tr),
            in_specs=[pl.BlockSpec((None, None, tr, cols), lambda j, i, sel_ref: (j, sel_ref[0], i, 0)), spec],
            out_specs=spec),
    )(sel, x, sib)


def _chip_sum(pair, recv, sel, *, name, tr=256):
    _, rows, cols = pair.shape
    tr = _tile_rows(rows, tr)

    def body(sel_ref, p_ref, r_ref, o_ref):
        acc = p_ref[...].astype(F32)
        for k in range(3):
            acc = acc + r_ref[k].astype(F32)
        o_ref[...] = acc

    return pl.pallas_call(
        body, name=name, out_shape=jax.ShapeDtypeStruct((rows, cols), F32), compiler_params=_params(1),
        grid_spec=pltpu.PrefetchScalarGridSpec(
            num_scalar_prefetch=1, grid=(rows // tr,),
            in_specs=[pl.BlockSpec((None, tr, cols), lambda i, sel_ref: (sel_ref[0], i, 0)),
                      pl.BlockSpec((3, tr, cols), lambda i, sel_ref: (0, i, 0))],
            out_specs=pl.BlockSpec((tr, cols), lambda i, sel_ref: (i, 0))),
    )(sel, pair, recv)


def _me():
    return lax.axis_index("x"), lax.axis_index("y"), lax.axis_index("c")


def _flip(pos, bits):
    x, y, c = pos
    return (x ^ bits[0] if bits[0] else x, y ^ bits[1] if bits[1] else y, c ^ bits[2] if bits[2] else c)


ANY = pl.BlockSpec(memory_space=pl.ANY)


def _all_gather8(xs, *, name):
    n = len(xs)
    flips = [((k >> 2) & 1, (k >> 1) & 1, k & 1) for k in range(1, 8)]

    def body(*refs):
        x_refs, out_refs, (send_sems, recv_sems, local_sems) = refs[:n], refs[n:2 * n], refs[2 * n:]
        me = _me()
        slot = lambda p: 4 * p[0] + 2 * p[1] + p[2]
        copies = []
        for i in range(n):
            mine = pltpu.make_async_copy(x_refs[i], out_refs[i].at[slot(me)], local_sems.at[i])
            mine.start()
            copies.append(mine)
            for k, f in enumerate(flips):
                peer = _flip(me, f)
                sems = dict(send_sem=send_sems.at[7 * i + k], recv_sem=recv_sems.at[7 * i + k], device_id=peer,
                            device_id_type=MESH)
                cp = pltpu.make_async_remote_copy(src_ref=x_refs[i], dst_ref=out_refs[i].at[slot(me)], **sems)
                cp.start()
                copies.append(cp)
                copies.append(pltpu.make_async_remote_copy(src_ref=x_refs[i], dst_ref=out_refs[i].at[slot(peer)], **sems))
        for i in range(n):
            base = i * 15
            copies[base].wait()
            for k in range(7):
                copies[base + 1 + 2 * k].wait_send()
                copies[base + 2 + 2 * k].wait_recv()

    outs = pl.pallas_call(
        body, name=name, in_specs=[ANY] * n, out_specs=[ANY] * n,
        out_shape=[jax.ShapeDtypeStruct((8, *x.shape), x.dtype) for x in xs],
        scratch_shapes=[pltpu.SemaphoreType.DMA((7 * n,)), pltpu.SemaphoreType.DMA((7 * n,)),
                        pltpu.SemaphoreType.DMA((n,))])(*xs)
    return list(outs)


CHIP_FLIPS = [(1, 0, 0), (0, 1, 0), (1, 1, 0)]


def _chip():
    return 2 * lax.axis_index("x") + lax.axis_index("y")


HBM = pl.BlockSpec(memory_space=pltpu.HBM)
SEM = pl.BlockSpec(memory_space=pltpu.SEMAPHORE)
EFFECT = pltpu.SideEffectType.DATAFLOW_SIDE_EFFECTING


def _plan_copies(plan, refs, send_sems, recv_sems):
    return [pltpu.make_async_remote_copy(src_ref=src, dst_ref=dst, send_sem=send_sems.at[k], recv_sem=recv_sems.at[k],
                                         device_id=to, device_id_type=MESH) for k, (src, dst, to) in enumerate(plan(refs))]


def _rdma_start(arrays, n_copies, plan, deps, *, name):
    n, nd = len(arrays), len(deps)

    def body(*refs):
        for cp in _plan_copies(plan, refs[:n], refs[n + nd], refs[n + nd + 1]):
            cp.start()
        refs[-1][...] = jnp.zeros_like(refs[-1])

    outs = pl.pallas_call(
        body, name=name,
        out_shape=(pltpu.SemaphoreType.DMA((n_copies,)), pltpu.SemaphoreType.DMA((n_copies,)),
                   *[pltpu.HBM(a.shape, a.dtype) for a in arrays], jax.ShapeDtypeStruct((8, LANE), F32)),
        in_specs=[HBM] * n + [ANY] * nd, out_specs=(SEM, SEM, *[HBM] * n, pl.BlockSpec(memory_space=pltpu.VMEM)),
        input_output_aliases={i: i + 2 for i in range(n)}, compiler_params=pltpu.CompilerParams(has_side_effects=EFFECT),
    )(*[pltpu.with_memory_space_constraint(a, pltpu.HBM) for a in arrays], *deps)
    return outs[0], outs[1], list(outs[2:2 + n]), outs[-1]


def _rdma_wait(send_sems, recv_sems, arrays, plan, after, *, name):
    n = len(arrays)

    def body(*refs):
        for cp in _plan_copies(plan, refs[:n], refs[n], refs[n + 1]):
            cp.wait_send()
            cp.wait_recv()

    return list(pl.pallas_call(
        body, name=name, out_shape=tuple(pltpu.HBM(a.shape, a.dtype) for a in arrays),
        in_specs=[HBM] * n + [SEM, SEM, ANY], out_specs=tuple([HBM] * n), input_output_aliases={i: i for i in range(n)},
        compiler_params=pltpu.CompilerParams(has_side_effects=EFFECT),
    )(*arrays, send_sems, recv_sems, after))


def _gather_plan(n):
    def plan(refs):
        me = _me()
        slot = 2 * me[0] + me[1]
        return [(refs[i].at[me[2]], refs[n + i].at[slot, me[2]], _flip(me, f)) for i in range(n) for f in CHIP_FLIPS]
    return plan


def _scatter_plan(n):
    def plan(refs):
        me = _me()
        out = []
        for i in range(n):
            for k, f in enumerate(CHIP_FLIPS):
                peer = _flip(me, f)
                out.append((refs[i].at[2 * peer[0] + peer[1]], refs[n + i].at[k], peer))
        return out
    return plan


def _sibling_plan(n, src_of):
    def plan(refs):
        me = _me()
        return [(src_of(refs[i], me[2]), refs[n + i], _flip(me, (0, 0, 1))) for i in range(n)]
    return plan


def _gather8_plan(n):
    def plan(refs):
        me = _me()
        slot = 4 * me[0] + 2 * me[1] + me[2]
        return [(refs[i], refs[n + i].at[slot], _flip(me, ((k >> 2) & 1, (k >> 1) & 1, k & 1)))
                for i in range(n) for k in range(1, 8)]
    return plan


def _pair_fill(lands, *, name):
    n = len(lands)

    def body(*refs):
        in_refs, (send_sems, recv_sems) = refs[:n], refs[2 * n:]
        me = _me()
        sib = _flip(me, (0, 0, 1))
        copies = []
        for i in range(n):
            for k, f in enumerate(CHIP_FLIPS):
                peer = _flip(me, f)
                slot = 2 * peer[0] + peer[1]
                mine, theirs = in_refs[i].at[slot, me[2]], in_refs[i].at[slot, 1 - me[2]]
                cp = pltpu.make_async_remote_copy(src_ref=mine, dst_ref=mine, send_sem=send_sems.at[3 * i + k],
                                                  recv_sem=recv_sems.at[3 * i + k], device_id=sib, device_id_type=MESH)
                cp.start()
                copies.append((cp, pltpu.make_async_remote_copy(
                    src_ref=mine, dst_ref=theirs, send_sem=send_sems.at[3 * i + k], recv_sem=recv_sems.at[3 * i + k],
                    device_id=sib, device_id_type=MESH)))
        for cp, arrival in copies:
            arrival.wait_recv()
            cp.wait_send()

    return list(pl.pallas_call(
        body, name=name, in_specs=[ANY] * n, out_specs=[ANY] * n,
        out_shape=[jax.ShapeDtypeStruct(a.shape, a.dtype) for a in lands], input_output_aliases={i: i for i in range(n)},
        scratch_shapes=[pltpu.SemaphoreType.DMA((3 * n,)), pltpu.SemaphoreType.DMA((3 * n,))])(*lands))


def _own_and_landed(lands, xs):
    chip = _chip()
    return [[jnp.where(chip == j, x, o.reshape(4, *x.shape)[j]) for j in range(4)] for o, x in zip(lands, xs)]


BIG = (("w_in", (D, IN_WIDTH // 4), 1), ("gla_w_o", (D // 4, D), 0), ("mla_w_uq", (MQR, MH * MQK // 4), 1),
       ("mla_w_ukv", (MKVR, MH * (MNOPE + MVD) // 4), 1), ("mla_w_o", (D // 4, D), 0), ("w_out", (D // 4, D), 0),
       ("mlp_w1", (D, DFF // 4), 1), ("mlp_w2", (DFF // 4, D), 0))
ADA_SHARD = (D, 6 * D // 4)
SMALL = (("b_ada", 6 * D), ("norm1_g", D), ("b_merge", 2 * D), ("gla_b_alpha", GH * GDK), ("gla_out_norm_g", GDV),
         ("mla_q_lat_g", MQR), ("mla_kv_lat_g", MKVR), ("mla_qn_g", MQK), ("mla_kn_g", MQK), ("norm2_g", D))


W_IN_SEGMENTS = ((0, 3072, OFF_Q), (3072, 3088, OFF_A), (3088, 3344, OFF_CQ), (3344, 3472, OFF_CKV),
                 (3472, 3504, OFF_KPE + MNOPE), (3504, 5552, OFF_MA))
W_IN_SPLIT = OFF_MA
SMALL_ROWS, SMALL_COLS = 32, 2 * D
W_ALPHA_ROW = 16
LOSS_ROW = 15
SMALL_RED = tuple((n, k) for n, k in SMALL if n != "b_ada")


def _pack_small(grads, d_w_alpha, loss_row, *, name):
    def body(*refs):
        g_refs, wa_ref, loss_ref, out_ref = refs[:-3], refs[-3], refs[-2], refs[-1]
        out_ref[...] = jnp.zeros_like(out_ref)
        for i, ((_, k), g_ref) in enumerate(zip(SMALL_RED, g_refs)):
            out_ref[i:i + 1, 0:k] = g_ref[...]
        out_ref[LOSS_ROW:LOSS_ROW + 1, 0:LANE] = loss_ref[...]
        out_ref[W_ALPHA_ROW:W_ALPHA_ROW + GLR, 0:GH * GDK] = wa_ref[...]

    return pl.pallas_call(body, name=name, out_shape=jax.ShapeDtypeStruct((SMALL_ROWS, SMALL_COLS), F32))(
        *grads, d_w_alpha, loss_row)


def _small_update(gathered, dmod_all, sel, wmv, *, name):
    names = [n for n, _ in SMALL] + ["gla_w_alpha"]
    n_par = len(names)

    def body(sel_ref, g_ref, dmod_ref, *refs):
        in_refs, out_refs, loss_ref, acc = refs[:3 * n_par], refs[3 * n_par:-2], refs[-2], refs[-1]
        total = g_ref[0]
        for j in range(1, 8):
            total = total + g_ref[j]
        acc[...] = total
        loss_ref[...] = acc[LOSS_ROW:LOSS_ROW + 1, 0:LANE]
        row = {n: i for i, (n, _) in enumerate(SMALL_RED)}
        for p, name_p in enumerate(names):
            w_ref, m_ref, v_ref = in_refs[3 * p:3 * p + 3]
            if name_p == "b_ada":
                gv = jnp.sum(dmod_ref[...], axis=0, keepdims=True)
            elif name_p == "gla_w_alpha":
                gv = jnp.zeros((GLR, GDK), F32)
                for j in range(4):
                    blk = acc[W_ALPHA_ROW:W_ALPHA_ROW + GLR, j * GDK:(j + 1) * GDK]
                    gv = gv + jnp.where(sel_ref[0] == j, blk, 0.0)
            else:
                gv = acc[row[name_p]:row[name_p] + 1, 0:w_ref.shape[1]]
            o = out_refs[4 * p:4 * p + 4]
            o[0][...] = gv
            o[1][...], o[2][...], o[3][...] = _adamw_update(w_ref[...], gv, m_ref[...], v_ref[...])

    flat = [a for t in wmv for a in t]
    out_shape = [jax.ShapeDtypeStruct(t[0].shape, F32) for t in wmv for _ in range(4)]
    out_shape.append(jax.ShapeDtypeStruct((1, LANE), F32))
    vmem = pl.BlockSpec(memory_space=pltpu.VMEM)
    outs = pl.pallas_call(
        body, name=name, out_shape=out_shape, in_specs=[pl.BlockSpec(memory_space=pltpu.SMEM), vmem, vmem] + [vmem] * len(flat),
        out_specs=[vmem] * len(out_shape), scratch_shapes=[pltpu.VMEM((SMALL_ROWS, SMALL_COLS), F32)],
    )(sel, gathered, dmod_all, *flat)
    return {n: tuple(outs[4 * p:4 * p + 4]) for p, n in enumerate(names)}, outs[-1][0, 0]


def _full_weights(gathered):
    w = {name: jnp.concatenate(gathered[name], axis=axis) for name, _, axis in BIG if name in gathered and name != "w_in"}
    if "w_in" in gathered:
        shards = gathered["w_in"]
        zeros = lambda n: [jnp.zeros((D, n), shards[0].dtype)]

        def cols(a, b):
            width = IN_WIDTH // 4
            return [shards[j][:, max(a, j * width) - j * width:min(b, (j + 1) * width) - j * width]
                    for j in range(4) if max(a, j * width) < min(b, (j + 1) * width)]

        parts = []
        for a, b, at in sorted(W_IN_SEGMENTS, key=lambda seg: seg[2]):
            have = sum(p.shape[1] for p in parts)
            parts += (zeros(at - have) if at > have else []) + cols(a, b)
        w["w_in"] = jnp.concatenate(parts + zeros(PW - sum(p.shape[1] for p in parts)), axis=1)
    if "mla_w_uq" in w:
        w["mla_w_uq"] = jnp.pad(w["mla_w_uq"].reshape(MQR, MH, MQK), ((0, 0), (0, 0), (0, LANE - MQK))).reshape(MQR, MH * LANE)
    if "mla_w_o" in w:
        w["mla_w_o"] = jnp.pad(w["mla_w_o"].reshape(MH, MVD, D), ((0, 0), (0, LANE - MVD), (0, 0))).reshape(MH * LANE, D)
    return w


def _grad_slots(g):
    g = dict(g)
    out = {}
    if "w_in" in g:
        g_lo, g_hi = g.pop("w_in")
        take = lambda at, lo, hi: g_lo[:, at + lo:at + hi] if at < W_IN_SPLIT else g_hi[:, at - W_IN_SPLIT + lo:at - W_IN_SPLIT + hi]
        width = IN_WIDTH // 4
        slots = []
        for j in range(4):
            lo, hi = j * width, (j + 1) * width
            slots.append(jnp.concatenate([take(at, max(lo, a) - a, min(hi, b) - a)
                                          for a, b, at in W_IN_SEGMENTS if max(lo, a) < min(hi, b)], axis=1))
        out["w_in"] = jnp.stack(slots).reshape(4, 2, D // 2, width)
    if "mla_w_uq" in g:
        g["mla_w_uq"] = g["mla_w_uq"].reshape(MQR, MH, LANE)[:, :, :MQK].reshape(MQR, MH * MQK)
    if "mla_w_o" in g:
        g["mla_w_o"] = g["mla_w_o"].reshape(MH, LANE, D)[:, :MVD].reshape(MH * MVD, D)
    for name, (rows, cols), axis in BIG:
        if name not in g:
            continue
        a = g[name]
        a = a.reshape(4, rows, cols) if axis == 0 else jnp.transpose(a.reshape(rows, 4, cols), (1, 0, 2))
        out[name] = a.reshape(4, 2, rows // 2, cols)
    return out


def _rope_tables(positions):
    freqs = ROPE_THETA ** (-jnp.arange(0, MROPE, 2, dtype=F32) / MROPE)
    lane = np.arange(LANE)
    in_rope = (lane >= MNOPE) & (lane < MQK)
    freq_lane = jnp.where(in_rope, freqs[(lane - MNOPE) % (MROPE // 2)], 0.0)
    sign = np.where(in_rope, np.where(lane < MNOPE + MROPE // 2, -1.0, 1.0), 0.0).astype(np.float32)
    ang = positions.astype(F32).reshape(-1, 1) * freq_lane[None, :]
    return jnp.cos(ang), jnp.sin(ang) * sign[None, :]


def _local_step(x, positions, mod, target, w, small, more_weights=None, on_grads=None):
    kept = {}
    if on_grads is None:
        on_grads = lambda tag, grads, after: kept.update(grads)
    bsz, s, _ = x.shape
    t = bsz * s
    tt = _tile(t, 1024)
    shift1, scale1, gate1, shift2, scale2, gate2 = [mod[:, None, i * D:(i + 1) * D] for i in range(6)]
    cos_t, sin_t = _rope_tables(positions)
    w_alpha_p = jnp.pad(small["gla_w_alpha"], ((0, LANE - GLR), (0, 0)))
    gq = jnp.pad(small["mla_qn_g"], ((0, 0), (0, LANE - MQK)))
    gk = jnp.pad(small["mla_kn_g"], ((0, 0), (0, LANE - MQK)))
    flat2 = lambda a: a.reshape(t, a.shape[-1])
    bsd = lambda a: a.reshape(bsz, s, a.shape[-1])

    h = _norm_mod(x, small["norm1_g"], scale1, shift1, name="norm1")
    if callable(w):
        w = w(h)
    proj = _mm(flat2(h), w["w_in"], name="proj", tn=1152, out_dtype=BF16)
    proj3 = bsd(proj)
    o, o_gated, states = _gla_fwd(proj3, w_alpha_p, small["gla_b_alpha"], small["gla_out_norm_g"], name="gla_fwd")
    if more_weights is not None:
        w = {**w, **more_weights(o_gated)}
    y_a = _mm(flat2(o_gated), w["gla_w_o"], name="gla_out", out_dtype=BF16)
    cq_n, ckv_n = _lat_norm(proj, small["mla_q_lat_g"], small["mla_kv_lat_g"], name="lat_norm")
    q_raw = _mm(cq_n, w["mla_w_uq"], name="mla_uq", out_dtype=BF16)
    kv = _mm(ckv_n, w["mla_w_ukv"], name="mla_ukv", out_dtype=BF16)
    qf, kf, vf = _qk_prep(q_raw, kv, proj, cos_t, sin_t, gq * Q_PRESCALE, gk, name="qk_prep")
    o_attn = _attn_fwd(bsd(qf), bsd(kf), bsd(vf), name="attn_fwd")
    y_b = _mm(flat2(o_attn), w["mla_w_o"], name="mla_out", out_dtype=BF16)
    mixed_in = _merge_fwd(proj3, small["b_merge"], bsd(y_a), bsd(y_b), name="merge_fwd")
    mixed = _mm(flat2(mixed_in), w["w_out"], name="w_out")
    x1, h2 = _resid_norm_mod(x, bsd(mixed), gate1, small["norm2_g"], scale2, shift2, name="norm2")

    def sqrelu(acc, ex, outs):
        r = jnp.maximum(acc, 0.0)
        outs[0][...] = (r * r).astype(BF16)

    r = _mm(flat2(h2), w["mlp_w1"], name="mlp1", epilogue=sqrelu, out_shape=jax.ShapeDtypeStruct((t, DFF), BF16),
            out_specs=_tile_spec(tt, 1024))
    ff = _mm(r, w["mlp_w2"], name="mlp2")
    dy, dff, dgate2, loss_part = _loss_head(x1, bsd(ff), gate2, target, name="loss_head")

    g = {}

    def relu2_bwd(acc, ex, outs):
        outs[0][...] = (acc * (2.0 * jnp.sqrt(ex[0][...].astype(F32)))).astype(BF16)

    dff2 = flat2(dff)
    da1 = _mm(dff2, w["mlp_w2"], tb=True, name="mlp2_dx", epilogue=relu2_bwd, extras=(r,),
              extra_specs=(_tile_spec(tt, 1024),), out_shape=jax.ShapeDtypeStruct((t, DFF), BF16),
              out_specs=_tile_spec(tt, 1024))
    g["mlp_w2"] = _mm(r, dff2, ta=True, name="mlp2_dw")
    dh2 = _mm(da1, w["mlp_w1"], tb=True, name="mlp1_dx")
    g["mlp_w1"] = _mm(flat2(h2), da1, ta=True, name="mlp1_dw")
    token = on_grads("mlp", {n: g.pop(n) for n in ("mlp_w2", "mlp_w1")}, dh2)
    if token is not None:
        gate1 = gate1 + token[0, 0]
    dx1, dscale2, dshift2, dg2, dgate1, dmixed = _norm_mod_bwd(
        bsd(dh2), x1, dy, small["norm2_g"], scale2, gate1, bsd(mixed), name="norm2_bwd")
    dmixed2 = flat2(dmixed)
    dmi = _mm(dmixed2, w["w_out"], tb=True, name="w_out_dx", out_dtype=BF16)
    g["w_out"] = _mm(flat2(mixed_in), dmixed2, ta=True, name="w_out_dw")
    dy_a, dy_b, dl_a, dl_b, db_a, db_b = _merge_bwd(bsd(dmi), proj3, small["b_merge"], bsd(y_a), bsd(y_b), name="merge_bwd")
    dy_a2, dy_b2 = flat2(dy_a), flat2(dy_b)
    dog = _mm(dy_a2, w["gla_w_o"], tb=True, name="gla_out_dx")
    g["gla_w_o"] = _mm(flat2(o_gated), dy_a2, ta=True, name="gla_out_dw")
    dq_g, dk_g, dv_g, dg_g, dlog, db_alpha, d_ong = _gla_bwd(
        bsd(dog), o, states, proj3, w_alpha_p, small["gla_b_alpha"], small["gla_out_norm_g"], name="gla_bwd")
    dlog2 = flat2(dlog)
    da_p = _mm(dlog2, w_alpha_p, tb=True, out_dtype=BF16, name="alpha_dx")
    d_w_alpha = _mm(proj[:, OFF_A:OFF_A + LANE], dlog2, ta=True, name="alpha_dw")[:GLR]
    do_attn = _mm(dy_b2, w["mla_w_o"], tb=True, out_dtype=BF16, name="mla_out_dx")
    g["mla_w_o"] = _mm(flat2(o_attn), dy_b2, ta=True, name="mla_out_dw")
    dqf, dkf, dvf = _attn_bwd(bsd(qf), bsd(kf), bsd(vf), bsd(do_attn), name="attn_bwd")
    dq_raw, dkv, dkpe, dgq, dgk = _qk_prep_bwd(flat2(dqf), flat2(dkf), flat2(dvf), q_raw, kv, proj, cos_t, sin_t, gq, gk,
                                                name="qk_prep_bwd")
    dcq_n = _mm(dq_raw, w["mla_w_uq"], tb=True, name="mla_uq_dx")
    g["mla_w_uq"] = _mm(cq_n, dq_raw, ta=True, name="mla_uq_dw")
    dckv_n = _mm(dkv, w["mla_w_ukv"], tb=True, name="mla_ukv_dx")
    g["mla_w_ukv"] = _mm(ckv_n, dkv, ta=True, name="mla_ukv_dw")
    token = on_grads("mix", {n: g.pop(n) for n in ("w_out", "gla_w_o", "mla_w_o", "mla_w_uq", "mla_w_ukv")}, dckv_n)
    q_lat_g = small["mla_q_lat_g"] if token is None else small["mla_q_lat_g"] + token[0:1, 0:1]
    dcq, dckv, dg_qlat, dg_kvlat = _lat_norm_bwd(dcq_n, dckv_n, proj, q_lat_g, small["mla_kv_lat_g"],
                                                  name="lat_norm_bwd")
    pieces = [(flat2(dq_g), OFF_Q), (flat2(dk_g), OFF_K), (flat2(dv_g), OFF_V), (flat2(dg_g), OFF_G),
              (flat2(dl_a), OFF_MA), (flat2(dl_b), OFF_MB), (dcq, OFF_CQ), (dckv, OFF_CKV), (da_p, OFF_A), (dkpe, OFF_KPE)]
    hb = flat2(h)
    g_w_in = (_pieces_dw(hb, [p for p, off in pieces if off < W_IN_SPLIT], name="proj_dw_a"),
              _pieces_dw(hb, [p for p, off in pieces if off >= W_IN_SPLIT], name="proj_dw_b"))
    token = on_grads("in", {"w_in": g_w_in}, g_w_in[1])
    after = jnp.zeros((8, LANE), F32) if token is None else token
    dh = _pieces_dx(pieces, w["w_in"], after, name="proj_dx")
    token = on_grads("dx", {}, dh)
    if token is not None:
        scale1 = scale1 + token[0, 0]
    grad_x, dscale1, dshift1, dg1 = _norm_mod_bwd(bsd(dh), x, dx1, small["norm1_g"], scale1, name="norm1_bwd")

    dmod = jnp.concatenate([dshift1, dscale1, dgate1, dshift2, dscale2, dgate2], axis=-1).reshape(bsz, 6 * D)
    gs = {"norm1_g": dg1, "b_merge": jnp.concatenate([db_a, db_b], axis=1), "gla_b_alpha": db_alpha,
          "gla_out_norm_g": d_ong, "mla_q_lat_g": dg_qlat, "mla_kv_lat_g": dg_kvlat, "mla_qn_g": dgq[:, :MQK],
          "mla_kn_g": dgk[:, :MQK], "norm2_g": dg2}
    return loss_part[0, 0], grad_x, dmod, {**kept, **g}, gs, d_w_alpha


def kernel(x, c, positions, w_ada, b_ada, norm1_g, w_in, b_merge, gla_w_alpha, gla_b_alpha, gla_out_norm_g, gla_w_o, mla_q_lat_g, mla_w_uq, mla_kv_lat_g, mla_w_ukv, mla_qn_g, mla_kn_g, mla_w_o, w_out, norm2_g, mlp_w1, mlp_w2, loss_target, m_w_ada, m_b_ada, m_norm1_g, m_w_in, m_b_merge, m_gla_w_alpha, m_gla_b_alpha, m_gla_out_norm_g, m_gla_w_o, m_mla_q_lat_g, m_mla_w_uq, m_mla_kv_lat_g, m_mla_w_ukv, m_mla_qn_g, m_mla_kn_g, m_mla_w_o, m_w_out, m_norm2_g, m_mlp_w1, m_mlp_w2, v_w_ada, v_b_ada, v_norm1_g, v_w_in, v_b_merge, v_gla_w_alpha, v_gla_b_alpha, v_gla_out_norm_g, v_gla_w_o, v_mla_q_lat_g, v_mla_w_uq, v_mla_kv_lat_g, v_mla_w_ukv, v_mla_qn_g, v_mla_kn_g, v_mla_w_o, v_w_out, v_norm2_g, v_mlp_w1, v_mlp_w2):
    args = dict(locals())
    names_big = [n for n, _, _ in BIG]
    names_small = [n for n, _ in SMALL]
    bsz = x.shape[0]
    ax, ay, ac = lax.axis_index("x"), lax.axis_index("y"), lax.axis_index("c")
    chip = 2 * ax + ay
    dev = 2 * chip + ac

    small = {n: args[n] for n in names_small}
    sel_c = jnp.reshape(ac, (1,)).astype(jnp.int32)
    sel_chip = jnp.reshape(chip, (1,)).astype(jnp.int32)
    c_all, w_alpha_all = _all_gather8([c, gla_w_alpha[0]], name="comm_c_alpha")
    small["gla_w_alpha"] = jnp.concatenate([w_alpha_all[2 * j] for j in range(4)], axis=1)
    c_all = c_all.reshape(8 * bsz, D)

    shards = {n: args[n][0].astype(BF16) for n in names_big}
    halves_of = lambda names: [shards[n].reshape(2, shards[n].shape[0] // 2, shards[n].shape[1]) for n in names]

    def gather_start(names, deps, tag):
        xs = halves_of(names)
        lands = [lax.empty((4, *xh.shape), BF16) for xh in xs]
        plan = _gather_plan(len(names))
        return names, plan, _rdma_start(xs + lands, 3 * len(names), plan, deps, name="comm_weights_start_" + tag)

    def gather_finish(started, after, tag):
        names, plan, sems = started
        arrs = _rdma_wait(sems[0], sems[1], sems[2], plan, after, name="comm_weights_wait_" + tag)
        filled = _pair_fill(arrs[len(names):], name="comm_weights_pair_" + tag)
        own = [a.reshape(shards[n].shape) for n, a in zip(names, arrs)]
        return _full_weights(dict(zip(names, _own_and_landed(filled, own))))


    def add_bias(acc, ex, outs):
        outs[0][...] = acc + ex[0][...]

    silu = lambda v: v * _sigmoid(v)
    b_ada_mine = lax.dynamic_slice(b_ada, (0, chip * ADA_SHARD[1]), (1, ADA_SHARD[1]))
    mod_part = _mm(c_all, w_ada[0], name="ada", tn=512, a_fn=silu, epilogue=add_bias, extras=(b_ada_mine,),
                   extra_specs=(pl.BlockSpec((1, 512), lambda i, j, k: (0, j)),),
                   out_shape=jax.ShapeDtypeStruct((8 * bsz, ADA_SHARD[1]), F32), out_specs=_tile_spec(8 * bsz, 512))
    mod_all = _all_gather8([mod_part], name="comm_mod")[0]
    mod_rows = lax.dynamic_slice(mod_all, (0, dev * bsz, 0), (8, bsz, ADA_SHARD[1]))
    mod = jnp.concatenate([mod_rows[2 * j] for j in range(4)], axis=1)
    first = gather_start(["w_in"], (mod,), "in")
    rest = gather_start([n for n in names_big if n != "w_in"], (mod, first[2][3]), "rest")
    mod = mod + rest[2][3][0, 0]
    w_in_after = lambda after: gather_finish(first, after, "in")
    more_weights = lambda after: gather_finish(rest, after, "rest")

    stage = {}

    def begin(tag, names, arrays, lands, n_copies, plan, what):
        stage[tag] = (names, plan, _rdma_start(arrays + lands, n_copies, plan, (), name=f"comm_{what}_start_{tag}"))
        return stage[tag][2][3]

    def landed(tag, after, what):
        names, plan, sems = stage[tag]
        arrs = _rdma_wait(sems[0], sems[1], sems[2], plan, after, name=f"comm_{what}_wait_{tag}")
        return names, arrs[:len(arrs) // 2], arrs[len(arrs) // 2:]

    def swap_start(tag, grads):
        names = list(grads)
        parts = [_grad_slots(grads)[n] for n in names]
        lands = [lax.empty((4, *p.shape[2:]), F32) for p in parts]
        return begin(tag, names, parts, lands, len(names), _sibling_plan(len(names), lambda r, c: r.at[:, 1 - c]), "pair_sum")

    def scatter_start(tag, after):
        names, parts, sib_halves = landed(tag, after, "pair_sum")
        pairs = [_pair_add(p, s, sel_c, name="pair_add_" + n) for n, p, s in zip(names, parts, sib_halves)]
        recvs = [lax.empty((3, *p.shape[1:]), BF16) for p in pairs]
        return begin(tag, names, pairs, recvs, 3 * len(names), _scatter_plan(len(names)), "scatter")

    def join_start(tag, after):
        names, pairs, recvs = landed(tag, after, "scatter")
        halves = [_chip_sum(p, r, sel_chip, name="chip_sum_" + n) for n, p, r in zip(names, pairs, recvs)]
        lands = [lax.empty(h.shape, F32) for h in halves]
        return begin(tag, names, halves, lands, len(names), _sibling_plan(len(names), lambda r, c: r), "pair_join")

    def reduce_step(tag, grads, after):
        if tag == "mlp":
            return swap_start("mlp", grads)
        if tag == "mix":
            return scatter_start("mlp", after) + swap_start("mix", grads)
        if tag == "in":
            return scatter_start("mix", after) + swap_start("in", grads)
        return scatter_start("in", after)

    loss_part, grad_x, dmod, g, gs, d_w_alpha = _local_step(x, positions, mod, loss_target, w_in_after, small,
                                                            more_weights, reduce_step)

    assert not g, list(g)
    gs_packed = _pack_small([gs[n] for n, _ in SMALL_RED], d_w_alpha, jnp.full((1, LANE), loss_part, F32),
                            name="pack_small")
    small_lands = [lax.empty((8, *a.shape), F32) for a in (dmod, gs_packed)]
    begin("small", ["dmod", "small"], [dmod, gs_packed], small_lands, 7 * 2, _gather8_plan(2), "gather8")

    res = {}

    def finish(tag, after):
        names, halves, theirs = landed(tag, after, "pair_join")
        for n, mine, other in zip(names, halves, theirs):
            if n == "w_in":
                south = ac == 0
                g_t = jnp.concatenate([jnp.where(south, mine, other), jnp.where(south, other, mine)], axis=0).T
                outs = _adamw(w_in[0].T, g_t, m_w_in[0].T, v_w_in[0].T, name="adamw_w_in", by_cols=True)
                res[n] = tuple(a.T for a in (g_t, *outs))
            else:
                res[n] = _adamw_halves(args[n][0], args["m_" + n][0], args["v_" + n][0], mine, other, sel_c,
                                       name="adamw_" + n)
        return res[names[-1]][1]

    join_start("mlp", grad_x)
    join_start("mix", grad_x)
    done = finish("mix", finish("mlp", grad_x))

    _, (dmod_own, gs_own), (dmod_all, gs_all) = landed("small", done, "gather8")
    dmod_all = lax.dynamic_update_slice(dmod_all, dmod_own[None], (dev, 0, 0)).reshape(8 * bsz, 6 * D)
    gs_all = lax.dynamic_update_slice(gs_all, gs_own[None], (dev, 0, 0))
    dmod_mine = lax.dynamic_slice(dmod_all, (0, chip * ADA_SHARD[1]), (8 * bsz, ADA_SHARD[1]))
    g_w_ada = _mm(c_all, dmod_mine, ta=True, a_fn=silu, name="ada_dw")
    wmv = [(args[n], args["m_" + n], args["v_" + n]) for n in names_small]
    wmv.append((gla_w_alpha[0], m_gla_w_alpha[0], v_gla_w_alpha[0]))
    res_small, loss_sum = _small_update(gs_all, dmod_all, sel_chip, wmv, name="small_update")
    res.update(res_small)
    loss = loss_sum * (0.5 / D)
    join_start("in", g_w_ada)
    res["w_ada"] = (g_w_ada, *_adamw(w_ada[0], g_w_ada, m_w_ada[0], v_w_ada[0], name="adamw_w_ada"))
    finish("in", res["w_ada"][1])

    order = ["w_ada", "b_ada", "norm1_g", "w_in", "b_merge", "gla_w_alpha", "gla_b_alpha", "gla_out_norm_g", "gla_w_o",
             "mla_q_lat_g", "mla_w_uq", "mla_kv_lat_g", "mla_w_ukv", "mla_qn_g", "mla_kn_g", "mla_w_o", "w_out",
             "norm2_g", "mlp_w1", "mlp_w2"]
    named = lambda k: [res[n][k].reshape(args[n].shape) for n in order]
    return (loss, grad_x, *named(0), *named(1), *named(2), *named(3))
```

```python
import jax
import jax.numpy as jnp
import numpy as np
from jax import lax
from jax.experimental import pallas as pl
from jax.experimental.pallas import tpu as pltpu

F32 = jnp.float32
BF16 = jnp.bfloat16
MESH = pl.DeviceIdType.MESH

D = 1024
CHUNK = 64
EPS = 1e-6
GH, GDK, GDV, GLR, GTAU = 4, 128, 256, 16, 16.0
MH, MQR, MKVR, MNOPE, MROPE, MVD = 16, 256, 128, 64, 32, 64
MQK = MNOPE + MROPE
DFF = 4 * D
ROPE_THETA = 10000.0
IN_WIDTH = 5552
LANE = 128
OFF_Q, OFF_K, OFF_V, OFF_G, OFF_MA, OFF_MB, OFF_CQ, OFF_CKV, OFF_A, OFF_KPE, PW = (
    0, 512, 1024, 2048, 3072, 4096, 5120, 5376, 5504, 5632, 5760)
ADAM_LR, ADAM_B1, ADAM_B2, ADAM_EPS, ADAM_WD, ADAM_STEP = 0.001, 0.9, 0.999, 1e-08, 0.01, 10
VMEM_LIMIT = 48 * 1024 * 1024


def _params(n_axes):
    return pltpu.CompilerParams(dimension_semantics=("arbitrary",) * n_axes, vmem_limit_bytes=VMEM_LIMIT)


def _tile(n, target):
    if n <= target:
        return n
    best = None
    for t in range(LANE, target + 1, LANE):
        if n % t == 0:
            best = t
    assert best is not None, (n, target)
    return best


def _sigmoid(x):
    return 1.0 / (1.0 + jnp.exp(-x))


MM_VMEM_BUDGET = 36 * 1024 * 1024


def _mm(a, b, *, name, ta=False, tb=False, out_dtype=F32, tm=1024, tn=1024, tk=4096,
        epilogue=None, extras=(), extra_specs=(), out_shape=None, out_specs=None, a_fn=None):
    if ta:
        kdim, m = a.shape
    else:
        m, kdim = a.shape
    if tb:
        n, k2 = b.shape
    else:
        k2, n = b.shape
    assert kdim == k2, (a.shape, b.shape)
    tm, tn, tk = _tile(m, tm), _tile(n, tn), _tile(kdim, tk)
    tiles = lambda rows: 2 * (rows * tk * a.dtype.itemsize + tk * tn * b.dtype.itemsize + rows * tn * 4) + rows * tn * 4
    while out_shape is None and tiles(tm) > MM_VMEM_BUDGET and tm % 256 == 0:
        tm //= 2
    nk = kdim // tk
    a_spec = pl.BlockSpec((tk, tm), lambda i, j, k: (k, i)) if ta else pl.BlockSpec((tm, tk), lambda i, j, k: (i, k))
    b_spec = pl.BlockSpec((tn, tk), lambda i, j, k: (j, k)) if tb else pl.BlockSpec((tk, tn), lambda i, j, k: (k, j))
    dims = (((0 if ta else 1,), (1 if tb else 0,)), ((), ()))
    ne = len(extras)
    if out_shape is None:
        out_shape = jax.ShapeDtypeStruct((m, n), out_dtype)
        out_specs = pl.BlockSpec((tm, tn), lambda i, j, k: (i, j))
    n_out = len(out_shape) if isinstance(out_shape, (list, tuple)) else 1
    in_place = epilogue is None and n_out == 1 and out_shape.dtype == F32
    scratch = [] if (nk == 1 or in_place) else [pltpu.VMEM((tm, tn), F32)]

    def body(a_ref, b_ref, *rest):
        ex, outs = rest[:ne], rest[ne:ne + n_out]
        av = a_ref[...] if a_fn is None else a_fn(a_ref[...])
        prod = lax.dot_general(av.astype(BF16), b_ref[...].astype(BF16), dims, preferred_element_type=F32)

        def finish(val):
            if epilogue is None:
                outs[0][...] = val.astype(outs[0].dtype)
            else:
                epilogue(val, ex, outs)

        if nk == 1:
            finish(prod)
            return
        k = pl.program_id(2)
        acc = outs[0] if in_place else rest[-1]

        @pl.when(k == 0)
        def _():
            acc[...] = prod

        @pl.when(k > 0)
        def _():
            acc[...] += prod

        if not in_place:
            @pl.when(k == nk - 1)
            def _():
                finish(acc[...])

    return pl.pallas_call(
        body, name=name, grid=(m // tm, n // tn, nk),
        in_specs=[a_spec, b_spec, *extra_specs], out_specs=out_specs, out_shape=out_shape,
        scratch_shapes=scratch, compiler_params=_params(3),
    )(a, b, *extras)


def _tile_spec(tm, tn):
    return pl.BlockSpec((tm, tn), lambda i, j, k: (i, j))


def _pieces_dx(pieces, w, after, *, name, tm=256):
    t = pieces[0][0].shape[0]
    tm = _tile(t, tm)
    npc = len(pieces)

    def body(*refs):
        p_refs, w_ref, out_ref = refs[:npc], refs[npc], refs[-1]
        acc = None
        for (arr, off), p_ref in zip(pieces, p_refs):
            part = lax.dot_general(p_ref[...].astype(BF16), w_ref[:, off:off + arr.shape[1]], _NT,
                                   preferred_element_type=F32)
            acc = part if acc is None else acc + part
        out_ref[...] = acc

    return pl.pallas_call(
        body, name=name, grid=(t // tm,),
        in_specs=[pl.BlockSpec((tm, arr.shape[1]), lambda i: (i, 0)) for arr, _ in pieces]
        + [pl.BlockSpec(w.shape, lambda i: (0, 0)), pl.BlockSpec((8, LANE), lambda i: (0, 0))],
        out_specs=pl.BlockSpec((tm, w.shape[0]), lambda i: (i, 0)),
        out_shape=jax.ShapeDtypeStruct((t, w.shape[0]), F32), compiler_params=_params(1),
    )(*[arr for arr, _ in pieces], w, after)


def _pieces_dw(h, pieces, *, name, tk=1024):
    t, d = h.shape
    tk = _tile(t, tk)
    widths = [p.shape[1] for p in pieces]
    starts = [sum(widths[:i]) for i in range(len(pieces))]

    def body(h_ref, *refs):
        p_refs, out_ref = refs[:-1], refs[-1]
        first = pl.program_id(0) == 0
        hv = h_ref[...]
        for p_ref, start, width in zip(p_refs, starts, widths):
            part = lax.dot_general(hv, p_ref[...].astype(BF16), _TN, preferred_element_type=F32)
            cols = slice(start, start + width)

            @pl.when(first)
            def _():
                out_ref[:, cols] = part

            @pl.when(jnp.logical_not(first))
            def _():
                out_ref[:, cols] += part

    return pl.pallas_call(
        body, name=name, grid=(t // tk,),
        in_specs=[pl.BlockSpec((tk, d), lambda k: (k, 0))] + [pl.BlockSpec((tk, wd), lambda k: (k, 0)) for wd in widths],
        out_specs=pl.BlockSpec((d, sum(widths)), lambda k: (0, 0)),
        out_shape=jax.ShapeDtypeStruct((d, sum(widths)), F32), compiler_params=_params(1),
    )(h, *pieces)


def _rms(x, g):
    r = lax.rsqrt(jnp.mean(x * x, axis=-1, keepdims=True) + EPS)
    return x * r, r


def _row_spec(ts, width, col=0):
    return pl.BlockSpec((None, ts, width), lambda b, i: (b, i, col))


def _vec_spec(width):
    return pl.BlockSpec((None, 1, width), lambda b, i: (b, 0, 0))


def _gain_spec(width):
    return pl.BlockSpec((1, width), lambda b, i: (0, 0))


def _norm_mod(x, g, scale, shift, *, name, ts=512):
    bsz, s, d = x.shape
    ts = min(ts, s)

    def body(x_ref, g_ref, sc_ref, sh_ref, h_ref):
        xh, _ = _rms(x_ref[...], None)
        h_ref[...] = ((xh * g_ref[...]) * (1.0 + sc_ref[...]) + sh_ref[...]).astype(BF16)

    return pl.pallas_call(
        body, name=name, grid=(bsz, s // ts),
        in_specs=[_row_spec(ts, d), _gain_spec(d), _vec_spec(d), _vec_spec(d)],
        out_specs=_row_spec(ts, d), out_shape=jax.ShapeDtypeStruct((bsz, s, d), BF16),
        compiler_params=_params(2),
    )(x, g, scale, shift)


def _resid_norm_mod(x, mixed, gate, g, scale, shift, *, name, ts=512):
    bsz, s, d = x.shape
    ts = min(ts, s)

    def body(x_ref, mx_ref, gt_ref, g_ref, sc_ref, sh_ref, x1_ref, h_ref):
        x1 = x_ref[...] + gt_ref[...] * mx_ref[...]
        x1_ref[...] = x1
        xh, _ = _rms(x1, None)
        h_ref[...] = ((xh * g_ref[...]) * (1.0 + sc_ref[...]) + sh_ref[...]).astype(BF16)

    return pl.pallas_call(
        body, name=name, grid=(bsz, s // ts),
        in_specs=[_row_spec(ts, d), _row_spec(ts, d), _vec_spec(d), _gain_spec(d), _vec_spec(d), _vec_spec(d)],
        out_specs=[_row_spec(ts, d), _row_spec(ts, d)],
        out_shape=[jax.ShapeDtypeStruct((bsz, s, d), F32), jax.ShapeDtypeStruct((bsz, s, d), BF16)],
        compiler_params=_params(2),
    )(x, mixed, gate, g, scale, shift)


def _norm_mod_bwd(dh, xin, resid, g, scale, gate=None, mixed=None, *, name, ts=512):
    bsz, s, d = xin.shape
    ts = min(ts, s)
    gated = gate is not None

    def body(*refs):
        if gated:
            dh_ref, x_ref, rs_ref, g_ref, sc_ref, gt_ref, mx_ref, dx_ref, dsc_ref, dsh_ref, dg_ref, dgt_ref, dmx_ref = refs
        else:
            dh_ref, x_ref, rs_ref, g_ref, sc_ref, dx_ref, dsc_ref, dsh_ref, dg_ref = refs
        b, i = pl.program_id(0), pl.program_id(1)

        @pl.when(i == 0)
        def _():
            dsc_ref[...] = jnp.zeros_like(dsc_ref)
            dsh_ref[...] = jnp.zeros_like(dsh_ref)
            if gated:
                dgt_ref[...] = jnp.zeros_like(dgt_ref)

        @pl.when((i == 0) & (b == 0))
        def _():
            dg_ref[...] = jnp.zeros_like(dg_ref)

        dh_v, gv = dh_ref[...], g_ref[...]
        xh, r = _rms(x_ref[...], None)
        dsc_ref[...] += jnp.sum(dh_v * (xh * gv), axis=0, keepdims=True)
        dsh_ref[...] += jnp.sum(dh_v, axis=0, keepdims=True)
        dn = dh_v * (1.0 + sc_ref[...])
        dg_ref[...] += jnp.sum(dn * xh, axis=0, keepdims=True)
        dxh = dn * gv
        dx = rs_ref[...] + r * (dxh - xh * jnp.mean(dxh * xh, axis=-1, keepdims=True))
        dx_ref[...] = dx
        if gated:
            dgt_ref[...] += jnp.sum(dx * mx_ref[...], axis=0, keepdims=True)
            dmx_ref[...] = (dx * gt_ref[...]).astype(BF16)

    ins = [dh, xin, resid, g, scale]
    in_specs = [_row_spec(ts, d), _row_spec(ts, d), _row_spec(ts, d), _gain_spec(d), _vec_spec(d)]
    out_specs = [_row_spec(ts, d), _vec_spec(d), _vec_spec(d), _gain_spec(d)]
    out_shape = [jax.ShapeDtypeStruct((bsz, s, d), F32), jax.ShapeDtypeStruct((bsz, 1, d), F32),
                 jax.ShapeDtypeStruct((bsz, 1, d), F32), jax.ShapeDtypeStruct((1, d), F32)]
    if gated:
        ins += [gate, mixed]
        in_specs += [_vec_spec(d), _row_spec(ts, d)]
        out_specs += [_vec_spec(d), _row_spec(ts, d)]
        out_shape += [jax.ShapeDtypeStruct((bsz, 1, d), F32), jax.ShapeDtypeStruct((bsz, s, d), BF16)]
    return pl.pallas_call(
        body, name=name, grid=(bsz, s // ts), in_specs=in_specs, out_specs=out_specs, out_shape=out_shape,
        compiler_params=_params(2),
    )(*ins)


def _loss_head(x1, ff, gate2, target, *, name, ts=512):
    bsz, s, d = x1.shape
    ts = min(ts, s)

    def body(x1_ref, ff_ref, gt_ref, t_ref, dy_ref, dff_ref, dgt_ref, loss_ref, acc):
        b, i = pl.program_id(0), pl.program_id(1)

        @pl.when(i == 0)
        def _():
            dgt_ref[...] = jnp.zeros_like(dgt_ref)

        @pl.when((i == 0) & (b == 0))
        def _():
            acc[...] = jnp.zeros_like(acc)

        ffv, gt = ff_ref[...], gt_ref[...]
        diff = (x1_ref[...] + gt * ffv) - t_ref[...]
        acc[...] += jnp.sum((diff * diff).reshape(ts // 8, 8, d), axis=0)
        dy = diff * (1.0 / d)
        dy_ref[...] = dy
        dgt_ref[...] += jnp.sum(dy * ffv, axis=0, keepdims=True)
        dff_ref[...] = (dy * gt).astype(BF16)

        @pl.when((i == pl.num_programs(1) - 1) & (b == pl.num_programs(0) - 1))
        def _():
            loss_ref[...] = jnp.full(loss_ref.shape, jnp.sum(acc[...]), F32)

    return pl.pallas_call(
        body, name=name, grid=(bsz, s // ts),
        in_specs=[_row_spec(ts, d), _row_spec(ts, d), _vec_spec(d), _row_spec(ts, d)],
        out_specs=[_row_spec(ts, d), _row_spec(ts, d), _vec_spec(d), pl.BlockSpec((8, LANE), lambda b, i: (0, 0))],
        out_shape=[jax.ShapeDtypeStruct((bsz, s, d), F32), jax.ShapeDtypeStruct((bsz, s, d), BF16),
                   jax.ShapeDtypeStruct((bsz, 1, d), F32), jax.ShapeDtypeStruct((8, LANE), F32)],
        scratch_shapes=[pltpu.VMEM((8, d), F32)], compiler_params=_params(2),
    )(x1, ff, gate2, target)


def _merge_fwd(proj, b_merge, y_a, y_b, *, name, ts=512):
    bsz, s, _ = proj.shape
    ts = min(ts, s)

    def body(la_ref, lb_ref, ba_ref, bb_ref, ya_ref, yb_ref, out_ref):
        ga = _sigmoid(la_ref[...] + ba_ref[...])
        gb = _sigmoid(lb_ref[...] + bb_ref[...])
        out_ref[...] = (ga * ya_ref[...] + gb * yb_ref[...]).astype(BF16)

    return pl.pallas_call(
        body, name=name, grid=(bsz, s // ts),
        in_specs=[_row_spec(ts, D, OFF_MA // D), _row_spec(ts, D, OFF_MB // D),
                  pl.BlockSpec((1, D), lambda b, i: (0, 0)), pl.BlockSpec((1, D), lambda b, i: (0, 1)),
                  _row_spec(ts, D), _row_spec(ts, D)],
        out_specs=_row_spec(ts, D), out_shape=jax.ShapeDtypeStruct((bsz, s, D), BF16),
        compiler_params=_params(2),
    )(proj, proj, b_merge, b_merge, y_a, y_b)


def _merge_bwd(dmi, proj, b_merge, y_a, y_b, *, name, ts=512):
    bsz, s, _ = proj.shape
    ts = min(ts, s)

    def body(d_ref, la_ref, lb_ref, ba_ref, bb_ref, ya_ref, yb_ref, dya_ref, dyb_ref, dla_ref, dlb_ref, dba_ref, dbb_ref):
        @pl.when((pl.program_id(0) == 0) & (pl.program_id(1) == 0))
        def _():
            dba_ref[...] = jnp.zeros_like(dba_ref)
            dbb_ref[...] = jnp.zeros_like(dbb_ref)

        dv = d_ref[...].astype(F32)
        ga = _sigmoid(la_ref[...] + ba_ref[...])
        gb = _sigmoid(lb_ref[...] + bb_ref[...])
        dya_ref[...] = (dv * ga).astype(BF16)
        dyb_ref[...] = (dv * gb).astype(BF16)
        dla = (dv * ya_ref[...]) * (ga * (1.0 - ga))
        dlb = (dv * yb_ref[...]) * (gb * (1.0 - gb))
        dla_ref[...] = dla.astype(BF16)
        dlb_ref[...] = dlb.astype(BF16)
        dba_ref[...] += jnp.sum(dla, axis=0, keepdims=True)
        dbb_ref[...] += jnp.sum(dlb, axis=0, keepdims=True)

    act = jax.ShapeDtypeStruct((bsz, s, D), BF16)
    return pl.pallas_call(
        body, name=name, grid=(bsz, s // ts),
        in_specs=[_row_spec(ts, D), _row_spec(ts, D, OFF_MA // D), _row_spec(ts, D, OFF_MB // D),
                  pl.BlockSpec((1, D), lambda b, i: (0, 0)), pl.BlockSpec((1, D), lambda b, i: (0, 1)),
                  _row_spec(ts, D), _row_spec(ts, D)],
        out_specs=[_row_spec(ts, D)] * 4 + [_gain_spec(D)] * 2,
        out_shape=[act, act, act, act, jax.ShapeDtypeStruct((1, D), F32), jax.ShapeDtypeStruct((1, D), F32)],
        compiler_params=_params(2),
    )(dmi, proj, proj, b_merge, b_merge, y_a, y_b)


def _tri(lower):
    r = lax.broadcasted_iota(jnp.int32, (CHUNK, CHUNK), 0)
    c = lax.broadcasted_iota(jnp.int32, (CHUNK, CHUNK), 1)
    return jnp.where((c <= r) if lower else (c >= r), 1.0, 0.0).astype(F32)


def _gla_logits(a_ref, wal_ref, bal_ref):
    logits = jnp.dot(a_ref[...].astype(BF16), wal_ref[...].astype(BF16), preferred_element_type=F32) + bal_ref[...]
    la = (jnp.minimum(logits, 0.0) - jnp.log(1.0 + jnp.exp(-jnp.abs(logits)))) * (1.0 / GTAU)
    return logits, la


def _chunk_cumsum(la_n, tri, precision=lax.Precision.HIGHEST):
    cum = jnp.dot(tri, la_n, preferred_element_type=F32, precision=precision)
    return cum, jnp.sum(la_n, axis=0, keepdims=True)


def _gla_specs(s, nc):
    def blk(width, off):
        return pl.BlockSpec((None, s, width), lambda h, b: (b, 0, off // width + h))

    proj_specs = [blk(GDK, OFF_Q), blk(GDK, OFF_K), blk(GDV, OFF_V), blk(GDV, OFF_G),
                  pl.BlockSpec((None, s, LANE), lambda h, b: (b, 0, OFF_A // LANE)),
                  pl.BlockSpec((LANE, GDK), lambda h, b: (0, h)), pl.BlockSpec((1, GDK), lambda h, b: (0, h)),
                  pl.BlockSpec((1, GDV), lambda h, b: (0, 0))]
    st_spec = pl.BlockSpec((None, None, nc, GDV, GDK), lambda h, b: (b, h, 0, 0, 0))
    return blk, proj_specs, st_spec


def _gla_fwd(proj, w_alpha_p, b_alpha, out_norm_g, *, name):
    bsz, s, _ = proj.shape
    nc = s // CHUNK
    scale = GDK ** -0.5

    rb = min(512, s)

    def body(q_ref, k_ref, v_ref, g_ref, a_ref, wal_ref, bal_ref, ong_ref, o_ref, og_ref, st_ref):
        _, la = _gla_logits(a_ref, wal_ref, bal_ref)
        tri = _tri(True)
        st = jnp.zeros((GDV, GDK), F32)
        for n in range(nc):
            rows = pl.ds(n * CHUNK, CHUNK)
            cum, cum_end = _chunk_cumsum(la[n * CHUNK:(n + 1) * CHUNK], tri, lax.Precision.HIGH)
            kd = k_ref[rows, :] * jnp.exp(cum_end - cum)
            ut = lax.dot_general(v_ref[rows, :].astype(BF16), kd.astype(BF16), _TN, preferred_element_type=F32)
            st = st * jnp.exp(cum_end) + ut
            st_ref[n] = st
            o_ref[rows, :] = lax.dot_general((q_ref[rows, :].astype(F32) * scale).astype(BF16), st.astype(BF16), _NT,
                                             preferred_element_type=F32)
        for j in range(0, s, rb):
            blk_rows = pl.ds(j, rb)
            oh, _ = _rms(o_ref[blk_rows, :], None)
            gv = g_ref[blk_rows, :].astype(F32)
            og_ref[blk_rows, :] = ((oh * ong_ref[...]) * (gv * _sigmoid(gv))).astype(BF16)

    blk, proj_specs, st_spec = _gla_specs(s, nc)
    return pl.pallas_call(
        body, name=name, grid=(GH, bsz), in_specs=proj_specs, out_specs=[blk(GDV, 0), blk(GDV, 0), st_spec],
        out_shape=[jax.ShapeDtypeStruct((bsz, s, GH * GDV), F32), jax.ShapeDtypeStruct((bsz, s, GH * GDV), BF16),
                   jax.ShapeDtypeStruct((bsz, GH, nc, GDV, GDK), F32)],
        compiler_params=_params(2),
    )(proj, proj, proj, proj, proj, w_alpha_p, b_alpha, out_norm_g)


def _gla_bwd(dog, o, states, proj, w_alpha_p, b_alpha, out_norm_g, *, name):
    bsz, s, _ = proj.shape
    nc = s // CHUNK
    scale = GDK ** -0.5

    def body(dog_ref, o_ref, st_ref, q_ref, k_ref, v_ref, g_ref, a_ref, wal_ref, bal_ref, ong_ref,
             dq_ref, dk_ref, dv_ref, dg_ref, dl_ref, dbal_ref, dong_ref, do_scr, dlog_scr):
        h, b = pl.program_id(0), pl.program_id(1)

        @pl.when(b == 0)
        def _():
            dbal_ref[...] = jnp.zeros_like(dbal_ref)

        @pl.when((b == 0) & (h == 0))
        def _():
            dong_ref[...] = jnp.zeros_like(dong_ref)

        ong = ong_ref[...]
        for j in range(0, s, rb):
            blk_rows = pl.ds(j, rb)
            gv, dogv = g_ref[blk_rows, :].astype(F32), dog_ref[blk_rows, :]
            sg = _sigmoid(gv)
            oh, r = _rms(o_ref[blk_rows, :], None)
            don = dogv * (gv * sg)
            dg_ref[blk_rows, :] = (dogv * (oh * ong) * (sg * (1.0 + gv * (1.0 - sg)))).astype(BF16)
            dong_ref[...] += jnp.sum(don * oh, axis=0, keepdims=True)
            doh = don * ong
            do_scr[blk_rows, :] = (r * (doh - oh * jnp.mean(doh * oh, axis=-1, keepdims=True))).astype(BF16)

        logits, la = _gla_logits(a_ref, wal_ref, bal_ref)
        tri_lo, tri_up = _tri(True), _tri(False)
        carry = jnp.zeros((GDV, GDK), F32)
        for n in range(nc - 1, -1, -1):
            rows = pl.ds(n * CHUNK, CHUNK)
            cum, cum_end = _chunk_cumsum(la[n * CHUNK:(n + 1) * CHUNK], tri_lo, lax.Precision.HIGH)
            decay = jnp.exp(cum_end)
            w = jnp.exp(cum_end - cum)
            kd = k_ref[rows, :] * w
            do_b = do_scr[rows, :]
            qs_b = (q_ref[rows, :].astype(F32) * scale).astype(BF16)
            dq_ref[rows, :] = (jnp.dot(do_b, st_ref[n].astype(BF16), preferred_element_type=F32) * scale).astype(BF16)
            dsn = lax.dot_general(do_b, qs_b, _TN, preferred_element_type=F32) + carry
            carry = dsn * decay
            dsn_b = dsn.astype(BF16)
            dv_ref[rows, :] = lax.dot_general(kd.astype(BF16), dsn_b, _NT, preferred_element_type=F32).astype(BF16)
            dkd = jnp.dot(v_ref[rows, :].astype(BF16), dsn_b, preferred_element_type=F32)
            dk_ref[rows, :] = (dkd * w).astype(BF16)
            e = dkd * kd
            dcum_end = jnp.sum(e, axis=0, keepdims=True)
            if n > 0:
                dcum_end += jnp.sum(dsn * st_ref[n - 1], axis=0, keepdims=True) * decay
            dlog_scr[rows, :] = dcum_end - jnp.dot(tri_up, e, preferred_element_type=F32,
                                                  precision=lax.Precision.HIGH)
        dlog = dlog_scr[...] * (1.0 / GTAU) * (1.0 - _sigmoid(logits))
        dl_ref[...] = dlog.astype(BF16)
        dbal_ref[...] += jnp.sum(dlog, axis=0, keepdims=True)

    rb = min(512, s)

    blk, proj_specs, st_spec = _gla_specs(s, nc)
    act = lambda wd: jax.ShapeDtypeStruct((bsz, s, wd), BF16)
    return pl.pallas_call(
        body, name=name, grid=(GH, bsz), in_specs=[blk(GDV, 0), blk(GDV, 0), st_spec, *proj_specs],
        out_specs=[blk(GDK, 0), blk(GDK, 0), blk(GDV, 0), blk(GDV, 0), blk(GDK, 0),
                   pl.BlockSpec((1, GDK), lambda h, b: (0, h)), pl.BlockSpec((1, GDV), lambda h, b: (0, 0))],
        out_shape=[act(GH * GDK), act(GH * GDK), act(GH * GDV), act(GH * GDV), act(GH * GDK),
                   jax.ShapeDtypeStruct((1, GH * GDK), F32), jax.ShapeDtypeStruct((1, GDV), F32)],
        scratch_shapes=[pltpu.VMEM((s, GDV), BF16), pltpu.VMEM((s, GDK), F32)], compiler_params=_params(2),
    )(dog, o, states, proj, proj, proj, proj, proj, w_alpha_p, b_alpha, out_norm_g)


def _lane():
    return lax.broadcasted_iota(jnp.int32, (1, LANE), 1)


def _swap_halves(x):
    lane = _lane()
    half = MROPE // 2
    lo = (lane >= MNOPE) & (lane < MNOPE + half)
    hi = (lane >= MNOPE + half) & (lane < MQK)
    return jnp.where(lo, pltpu.roll(x, LANE - half, 1), jnp.where(hi, pltpu.roll(x, half, 1), 0.0))


def _norm96(x, g):
    r = lax.rsqrt(jnp.sum(x * x, axis=-1, keepdims=True) * (1.0 / MQK) + EPS)
    return x * r, r


def _lat_norm(proj, q_lat_g, kv_lat_g, *, name, ts=512):
    t = proj.shape[0]
    ts = min(ts, t)

    def body(cq_ref, ckv_ref, gq_ref, gk_ref, oq_ref, ok_ref):
        xq, _ = _rms(cq_ref[...].astype(F32), None)
        oq_ref[...] = (xq * gq_ref[...]).astype(BF16)
        xk, _ = _rms(ckv_ref[...].astype(F32), None)
        ok_ref[...] = (xk * gk_ref[...]).astype(BF16)

    return pl.pallas_call(
        body, name=name, grid=(t // ts,),
        in_specs=[pl.BlockSpec((ts, MQR), lambda i: (i, OFF_CQ // MQR)), pl.BlockSpec((ts, MKVR), lambda i: (i, OFF_CKV // MKVR)),
                  pl.BlockSpec((1, MQR), lambda i: (0, 0)), pl.BlockSpec((1, MKVR), lambda i: (0, 0))],
        out_specs=[pl.BlockSpec((ts, MQR), lambda i: (i, 0)), pl.BlockSpec((ts, MKVR), lambda i: (i, 0))],
        out_shape=[jax.ShapeDtypeStruct((t, MQR), BF16), jax.ShapeDtypeStruct((t, MKVR), BF16)],
        compiler_params=_params(1),
    )(proj, proj, q_lat_g, kv_lat_g)


def _lat_norm_bwd(dcqn, dckvn, proj, q_lat_g, kv_lat_g, *, name, ts=512):
    t = proj.shape[0]
    ts = min(ts, t)

    def one(d_ref, x_ref, g_ref, dx_ref, dg_ref):
        xh, r = _rms(x_ref[...].astype(F32), None)
        dn = d_ref[...]
        dg_ref[...] += jnp.sum(dn * xh, axis=0, keepdims=True)
        dxh = dn * g_ref[...]
        dx_ref[...] = (r * (dxh - xh * jnp.mean(dxh * xh, axis=-1, keepdims=True))).astype(BF16)

    def body(dq_ref, dk_ref, cq_ref, ckv_ref, gq_ref, gk_ref, dxq_ref, dxk_ref, dgq_ref, dgk_ref):
        @pl.when(pl.program_id(0) == 0)
        def _():
            dgq_ref[...] = jnp.zeros_like(dgq_ref)
            dgk_ref[...] = jnp.zeros_like(dgk_ref)

        one(dq_ref, cq_ref, gq_ref, dxq_ref, dgq_ref)
        one(dk_ref, ckv_ref, gk_ref, dxk_ref, dgk_ref)

    return pl.pallas_call(
        body, name=name, grid=(t // ts,),
        in_specs=[pl.BlockSpec((ts, MQR), lambda i: (i, 0)), pl.BlockSpec((ts, MKVR), lambda i: (i, 0)),
                  pl.BlockSpec((ts, MQR), lambda i: (i, OFF_CQ // MQR)), pl.BlockSpec((ts, MKVR), lambda i: (i, OFF_CKV // MKVR)),
                  pl.BlockSpec((1, MQR), lambda i: (0, 0)), pl.BlockSpec((1, MKVR), lambda i: (0, 0))],
        out_specs=[pl.BlockSpec((ts, MQR), lambda i: (i, 0)), pl.BlockSpec((ts, MKVR), lambda i: (i, 0)),
                   pl.BlockSpec((1, MQR), lambda i: (0, 0)), pl.BlockSpec((1, MKVR), lambda i: (0, 0))],
        out_shape=[jax.ShapeDtypeStruct((t, MQR), BF16), jax.ShapeDtypeStruct((t, MKVR), BF16),
                   jax.ShapeDtypeStruct((1, MQR), F32), jax.ShapeDtypeStruct((1, MKVR), F32)],
        compiler_params=_params(1),
    )(dcqn, dckvn, proj, proj, q_lat_g, kv_lat_g)


def _qk_prep(q_raw, kv, proj, cos_t, sin_t, gq, gk, *, name, ts=2048):
    t = q_raw.shape[0]
    ts = min(ts, t)

    def body(q_ref, kv_ref, kpe_ref, c_ref, s_ref, gq_ref, gk_ref, qo_ref, ko_ref, vo_ref):
        cs, sn = c_ref[...], s_ref[...]
        nope = _lane() < MNOPE
        qn, _ = _norm96(q_ref[...].astype(F32), None)
        qn = qn * gq_ref[...]
        qo_ref[...] = (qn * cs + _swap_halves(qn) * sn).astype(BF16)
        kvv = kv_ref[...].astype(F32)
        kn, _ = _norm96(jnp.where(nope, kvv, kpe_ref[...].astype(F32)), None)
        kn = kn * gk_ref[...]
        ko_ref[...] = (kn * cs + _swap_halves(kn) * sn).astype(BF16)
        vo_ref[...] = jnp.where(nope, pltpu.roll(kvv, MNOPE, 1), 0.0).astype(BF16)

    hd = pl.BlockSpec((ts, LANE), lambda i, h: (i, h))
    shared = lambda col: pl.BlockSpec((ts, LANE), lambda i, h: (i, col))
    gain = pl.BlockSpec((1, LANE), lambda i, h: (0, 0))
    out = jax.ShapeDtypeStruct((t, MH * LANE), BF16)
    return pl.pallas_call(
        body, name=name, grid=(t // ts, MH),
        in_specs=[hd, hd, shared(OFF_KPE // LANE), shared(0), shared(0), gain, gain],
        out_specs=[hd, hd, hd], out_shape=[out, out, out], compiler_params=_params(2),
    )(q_raw, kv, proj, cos_t, sin_t, gq, gk)


def _qk_prep_bwd(dq, dk, dv, q_raw, kv, proj, cos_t, sin_t, gq, gk, *, name, ts=2048):
    t = q_raw.shape[0]
    ts = min(ts, t)

    def norm_bwd(dy, x, g, dg_ref):
        xh, r = _norm96(x, None)
        dg_ref[...] += jnp.sum(dy * xh, axis=0, keepdims=True)
        dxh = dy * g
        return r * (dxh - xh * (jnp.sum(dxh * xh, axis=-1, keepdims=True) * (1.0 / MQK)))

    def body(dq_ref, dk_ref, dv_ref, q_ref, kv_ref, kpe_ref, c_ref, s_ref, gq_ref, gk_ref,
             dqr_ref, dkv_ref, dkpe_ref, dgq_ref, dgk_ref):
        i, h = pl.program_id(0), pl.program_id(1)

        @pl.when(h == 0)
        def _():
            dkpe_ref[...] = jnp.zeros_like(dkpe_ref)

        @pl.when((h == 0) & (i == 0))
        def _():
            dgq_ref[...] = jnp.zeros_like(dgq_ref)
            dgk_ref[...] = jnp.zeros_like(dgk_ref)

        cs, sn = c_ref[...], s_ref[...]
        lane = _lane()
        nope = lane < MNOPE
        dqv = dq_ref[...]
        dqn = dqv * cs + _swap_halves(dqv * sn)
        dqr_ref[...] = norm_bwd(dqn, q_ref[...].astype(F32), gq_ref[...], dgq_ref).astype(BF16)
        dkv_ = dk_ref[...]
        dkn = dkv_ * cs + _swap_halves(dkv_ * sn)
        kvv = kv_ref[...].astype(F32)
        dkr = norm_bwd(dkn, jnp.where(nope, kvv, kpe_ref[...].astype(F32)), gk_ref[...], dgk_ref)
        dkv_ref[...] = jnp.where(nope, dkr, pltpu.roll(dv_ref[...], MNOPE, 1)).astype(BF16)
        dkpe_ref[...] += jnp.where((lane >= MNOPE) & (lane < MQK), dkr, 0.0)

    hd = pl.BlockSpec((ts, LANE), lambda i, h: (i, h))
    shared = lambda col: pl.BlockSpec((ts, LANE), lambda i, h: (i, col))
    gain = pl.BlockSpec((1, LANE), lambda i, h: (0, 0))
    out = jax.ShapeDtypeStruct((t, MH * LANE), BF16)
    return pl.pallas_call(
        body, name=name, grid=(t // ts, MH),
        in_specs=[hd, hd, hd, hd, hd, shared(OFF_KPE // LANE), shared(0), shared(0), gain, gain],
        out_specs=[hd, hd, shared(0), gain, gain],
        out_shape=[out, out, jax.ShapeDtypeStruct((t, LANE), F32), jax.ShapeDtypeStruct((1, LANE), F32),
                   jax.ShapeDtypeStruct((1, LANE), F32)],
        compiler_params=_params(2),
    )(dq, dk, dv, q_raw, kv, proj, cos_t, sin_t, gq, gk)


_NT = (((1,), (1,)), ((), ()))
_TN = (((0,), (0,)), ((), ()))


SOFTMAX_SCALE = MQK ** -0.5
Q_PRESCALE = SOFTMAX_SCALE * float(np.log2(np.e))


def _attn_weights(q, k_ref, lo, tq):
    row = lax.broadcasted_iota(jnp.int32, (tq, tq), 0) // CHUNK
    col = lax.broadcasted_iota(jnp.int32, (tq, tq), 1) // CHUNK
    sd = lax.dot_general(q, k_ref[pl.ds(lo, tq), :], _NT, preferred_element_type=F32)
    sd = jnp.where(col <= row, sd, -1e30)
    m = jnp.max(sd, axis=-1, keepdims=True)
    if lo:
        so = lax.dot_general(q, k_ref[pl.ds(0, lo), :], _NT, preferred_element_type=F32)
        m = jnp.maximum(m, jnp.max(so, axis=-1, keepdims=True))
        eo = jnp.exp2(so - m)
        ed = jnp.exp2(sd - m)
        return eo, ed, 1.0 / (jnp.sum(eo, axis=-1, keepdims=True) + jnp.sum(ed, axis=-1, keepdims=True))
    ed = jnp.exp2(sd - m)
    return None, ed, 1.0 / jnp.sum(ed, axis=-1, keepdims=True)


def _attn_fwd(q, k, v, *, name, tq=256):
    bsz, s, _ = q.shape
    tq = min(tq, s)

    def body(q_ref, k_ref, v_ref, o_ref):
        for i in range(s // tq):
            lo = i * tq
            eo, ed, inv = _attn_weights(q_ref[pl.ds(lo, tq), :], k_ref, lo, tq)
            o = jnp.dot(ed.astype(BF16), v_ref[pl.ds(lo, tq), :], preferred_element_type=F32)
            if lo:
                o += jnp.dot(eo.astype(BF16), v_ref[pl.ds(0, lo), :], preferred_element_type=F32)
            o_ref[pl.ds(lo, tq), :] = (o * inv).astype(BF16)

    spec = pl.BlockSpec((None, s, LANE), lambda b, h: (b, 0, h))
    return pl.pallas_call(
        body, name=name, grid=(bsz, MH), in_specs=[spec, spec, spec], out_specs=spec,
        out_shape=jax.ShapeDtypeStruct((bsz, s, MH * LANE), BF16), compiler_params=_params(2),
    )(q, k, v)


def _attn_bwd(q, k, v, do, *, name, tq=256):
    bsz, s, _ = q.shape
    tq = min(tq, s)

    def body(q_ref, k_ref, v_ref, do_ref, dq_ref, dk_ref, dv_ref):
        dk_ref[...] = jnp.zeros_like(dk_ref)
        dv_ref[...] = jnp.zeros_like(dv_ref)
        for i in range(s // tq):
            lo = i * tq
            here, before = pl.ds(lo, tq), pl.ds(0, lo)
            qv, dov = q_ref[here, :], do_ref[here, :]
            eo, ed, inv = _attn_weights(qv, k_ref, lo, tq)
            do_n = (dov.astype(F32) * inv).astype(BF16)
            dv_ref[here, :] += lax.dot_general(ed.astype(BF16), do_n, _TN, preferred_element_type=F32)
            dpd = lax.dot_general(dov, v_ref[here, :], _NT, preferred_element_type=F32)
            delta = jnp.sum(dpd * ed, axis=-1, keepdims=True)
            if lo:
                dv_ref[before, :] += lax.dot_general(eo.astype(BF16), do_n, _TN, preferred_element_type=F32)
                dpo = lax.dot_general(dov, v_ref[before, :], _NT, preferred_element_type=F32)
                delta += jnp.sum(dpo * eo, axis=-1, keepdims=True)
            delta = delta * inv
            r = inv * SOFTMAX_SCALE
            dsd = (ed * (dpd - delta) * r).astype(BF16)
            dq = jnp.dot(dsd, k_ref[here, :], preferred_element_type=F32)
            dk_ref[here, :] += lax.dot_general(dsd, qv, _TN, preferred_element_type=F32)
            if lo:
                dso = (eo * (dpo - delta) * r).astype(BF16)
                dq += jnp.dot(dso, k_ref[before, :], preferred_element_type=F32)
                dk_ref[before, :] += lax.dot_general(dso, qv, _TN, preferred_element_type=F32)
            dq_ref[here, :] = dq
        dk_ref[...] = dk_ref[...] * (1.0 / Q_PRESCALE)

    spec = pl.BlockSpec((None, s, LANE), lambda b, h: (b, 0, h))
    out = jax.ShapeDtypeStruct((bsz, s, MH * LANE), F32)
    return pl.pallas_call(
        body, name=name, grid=(bsz, MH), in_specs=[spec] * 4, out_specs=[spec] * 3, out_shape=[out, out, out],
        compiler_params=_params(2),
    )(q, k, v, do)


def _adamw(w, g, m, v, *, name, tr=256, by_cols=False):
    rows, cols = w.shape
    tr = _tile_rows(rows, tr)

    def body(w_ref, g_ref, m_ref, v_ref, d_ref, nm_ref, nv_ref):
        d_ref[...], nm_ref[...], nv_ref[...] = _adamw_update(w_ref[...], g_ref[...], m_ref[...], v_ref[...])

    spec = pl.BlockSpec((rows, LANE), lambda i: (0, i)) if by_cols else pl.BlockSpec((tr, cols), lambda i: (i, 0))
    out = jax.ShapeDtypeStruct((rows, cols), F32)
    return pl.pallas_call(body, name=name, grid=(cols // LANE if by_cols else rows // tr,), in_specs=[spec] * 4,
                          out_specs=[spec] * 3, out_shape=[out, out, out], compiler_params=_params(1))(w, g, m, v)


def _tile_rows(rows, target):
    if rows <= target:
        return rows
    best = 8
    for t in range(8, target + 1, 8):
        if rows % t == 0:
            best = t
    return best


def _adamw_update(w, g, m, v):
    nm = ADAM_B1 * m + (1.0 - ADAM_B1) * g
    nv = ADAM_B2 * v + (1.0 - ADAM_B2) * (g * g)
    m_hat = nm / (1.0 - ADAM_B1 ** ADAM_STEP)
    v_hat = nv / (1.0 - ADAM_B2 ** ADAM_STEP)
    return -ADAM_LR * (m_hat / (jnp.sqrt(v_hat) + ADAM_EPS) + ADAM_WD * w), nm, nv


def _adamw_halves(w, m, v, mine, theirs, sel, *, name, tr=256):
    rows, cols = w.shape
    tr = _tile_rows(rows // 2, tr)
    nh = rows // 2 // tr

    def body(sel_ref, w_ref, m_ref, v_ref, mine_ref, theirs_ref, g_ref, d_ref, nm_ref, nv_ref):
        lower = pl.program_id(0) < nh
        south = sel_ref[0] == 0
        gv = jnp.where(lower == south, mine_ref[...], theirs_ref[...])
        g_ref[...] = gv
        d_ref[...], nm_ref[...], nv_ref[...] = _adamw_update(w_ref[...], gv, m_ref[...], v_ref[...])

    full = pl.BlockSpec((tr, cols), lambda i, sel_ref: (i, 0))
    half = pl.BlockSpec((tr, cols), lambda i, sel_ref: (i % nh, 0))
    out = jax.ShapeDtypeStruct((rows, cols), F32)
    return pl.pallas_call(
        body, name=name, out_shape=[out] * 4, compiler_params=_params(1),
        grid_spec=pltpu.PrefetchScalarGridSpec(num_scalar_prefetch=1, grid=(rows // tr,),
                                               in_specs=[full, full, full, half, half], out_specs=[full] * 4),
    )(sel, w, m, v, mine, theirs)


def _pair_add(x, sib, sel, *, name, tr=256):
    n, _, rows, cols = x.shape
    tr = _tile_rows(rows, tr)

    def body(sel_ref, x_ref, s_ref, o_ref):
        o_ref[...] = (x_ref[...] + s_ref[...]).astype(BF16)

    spec = pl.BlockSpec((None, tr, cols), lambda j, i, sel_ref: (j, i, 0))
    return pl.pallas_call(
        body, name=name, out_shape=jax.ShapeDtypeStruct((n, rows, cols), BF16), compiler_params=_params(2),
        grid_spec=pltpu.PrefetchScalarGridSpec(
            num_scalar_prefetch=1, grid=(n, rows // tr),
            in_specs=[pl.BlockSpec((None, None, tr, cols), lambda j, i, sel_ref: (j, sel_ref[0], i, 0)), spec],
            out_specs=spec),
    )(sel, x, sib)


def _chip_sum(pair, recv, sel, *, name, tr=256):
    _, rows, cols = pair.shape
    tr = _tile_rows(rows, tr)

    def body(sel_ref, p_ref, r_ref, o_ref):
        acc = p_ref[...].astype(F32)
        for k in range(3):
            acc = acc + r_ref[k].astype(F32)
        o_ref[...] = acc

    return pl.pallas_call(
        body, name=name, out_shape=jax.ShapeDtypeStruct((rows, cols), F32), compiler_params=_params(1),
        grid_spec=pltpu.PrefetchScalarGridSpec(
            num_scalar_prefetch=1, grid=(rows // tr,),
            in_specs=[pl.BlockSpec((None, tr, cols), lambda i, sel_ref: (sel_ref[0], i, 0)),
                      pl.BlockSpec((3, tr, cols), lambda i, sel_ref: (0, i, 0))],
            out_specs=pl.BlockSpec((tr, cols), lambda i, sel_ref: (i, 0))),
    )(sel, pair, recv)


def _me():
    return lax.axis_index("x"), lax.axis_index("y"), lax.axis_index("c")


def _flip(pos, bits):
    x, y, c = pos
    return (x ^ bits[0] if bits[0] else x, y ^ bits[1] if bits[1] else y, c ^ bits[2] if bits[2] else c)


ANY = pl.BlockSpec(memory_space=pl.ANY)


def _all_gather8(xs, *, name):
    n = len(xs)
    flips = [((k >> 2) & 1, (k >> 1) & 1, k & 1) for k in range(1, 8)]

    def body(*refs):
        x_refs, out_refs, (send_sems, recv_sems, local_sems) = refs[:n], refs[n:2 * n], refs[2 * n:]
        me = _me()
        slot = lambda p: 4 * p[0] + 2 * p[1] + p[2]
        copies = []
        for i in range(n):
            mine = pltpu.make_async_copy(x_refs[i], out_refs[i].at[slot(me)], local_sems.at[i])
            mine.start()
            copies.append(mine)
            for k, f in enumerate(flips):
                peer = _flip(me, f)
                sems = dict(send_sem=send_sems.at[7 * i + k], recv_sem=recv_sems.at[7 * i + k], device_id=peer,
                            device_id_type=MESH)
                cp = pltpu.make_async_remote_copy(src_ref=x_refs[i], dst_ref=out_refs[i].at[slot(me)], **sems)
                cp.start()
                copies.append(cp)
                copies.append(pltpu.make_async_remote_copy(src_ref=x_refs[i], dst_ref=out_refs[i].at[slot(peer)], **sems))
        for i in range(n):
            base = i * 15
            copies[base].wait()
            for k in range(7):
                copies[base + 1 + 2 * k].wait_send()
                copies[base + 2 + 2 * k].wait_recv()

    outs = pl.pallas_call(
        body, name=name, in_specs=[ANY] * n, out_specs=[ANY] * n,
        out_shape=[jax.ShapeDtypeStruct((8, *x.shape), x.dtype) for x in xs],
        scratch_shapes=[pltpu.SemaphoreType.DMA((7 * n,)), pltpu.SemaphoreType.DMA((7 * n,)),
                        pltpu.SemaphoreType.DMA((n,))])(*xs)
    return list(outs)


CHIP_FLIPS = [(1, 0, 0), (0, 1, 0), (1, 1, 0)]


def _chip():
    return 2 * lax.axis_index("x") + lax.axis_index("y")


HBM = pl.BlockSpec(memory_space=pltpu.HBM)
SEM = pl.BlockSpec(memory_space=pltpu.SEMAPHORE)
EFFECT = pltpu.SideEffectType.DATAFLOW_SIDE_EFFECTING


def _plan_copies(plan, refs, send_sems, recv_sems):
    return [pltpu.make_async_remote_copy(src_ref=src, dst_ref=dst, send_sem=send_sems.at[k], recv_sem=recv_sems.at[k],
                                         device_id=to, device_id_type=MESH) for k, (src, dst, to) in enumerate(plan(refs))]


def _rdma_start(arrays, n_copies, plan, deps, *, name):
    n, nd = len(arrays), len(deps)

    def body(*refs):
        for cp in _plan_copies(plan, refs[:n], refs[n + nd], refs[n + nd + 1]):
            cp.start()
        refs[-1][...] = jnp.zeros_like(refs[-1])

    outs = pl.pallas_call(
        body, name=name,
        out_shape=(pltpu.SemaphoreType.DMA((n_copies,)), pltpu.SemaphoreType.DMA((n_copies,)),
                   *[pltpu.HBM(a.shape, a.dtype) for a in arrays], jax.ShapeDtypeStruct((8, LANE), F32)),
        in_specs=[HBM] * n + [ANY] * nd, out_specs=(SEM, SEM, *[HBM] * n, pl.BlockSpec(memory_space=pltpu.VMEM)),
        input_output_aliases={i: i + 2 for i in range(n)}, compiler_params=pltpu.CompilerParams(has_side_effects=EFFECT),
    )(*[pltpu.with_memory_space_constraint(a, pltpu.HBM) for a in arrays], *deps)
    return outs[0], outs[1], list(outs[2:2 + n]), outs[-1]


def _rdma_wait(send_sems, recv_sems, arrays, plan, after, *, name):
    n = len(arrays)

    def body(*refs):
        for cp in _plan_copies(plan, refs[:n], refs[n], refs[n + 1]):
            cp.wait_send()
            cp.wait_recv()

    return list(pl.pallas_call(
        body, name=name, out_shape=tuple(pltpu.HBM(a.shape, a.dtype) for a in arrays),
        in_specs=[HBM] * n + [SEM, SEM, ANY], out_specs=tuple([HBM] * n), input_output_aliases={i: i for i in range(n)},
        compiler_params=pltpu.CompilerParams(has_side_effects=EFFECT),
    )(*arrays, send_sems, recv_sems, after))


def _gather_plan(n):
    def plan(refs):
        me = _me()
        slot = 2 * me[0] + me[1]
        return [(refs[i].at[me[2]], refs[n + i].at[slot, me[2]], _flip(me, f)) for i in range(n) for f in CHIP_FLIPS]
    return plan


def _scatter_plan(n):
    def plan(refs):
        me = _me()
        out = []
        for i in range(n):
            for k, f in enumerate(CHIP_FLIPS):
                peer = _flip(me, f)
                out.append((refs[i].at[2 * peer[0] + peer[1]], refs[n + i].at[k], peer))
        return out
    return plan


def _sibling_plan(n, src_of):
    def plan(refs):
        me = _me()
        return [(src_of(refs[i], me[2]), refs[n + i], _flip(me, (0, 0, 1))) for i in range(n)]
    return plan


def _gather8_plan(n):
    def plan(refs):
        me = _me()
        slot = 4 * me[0] + 2 * me[1] + me[2]
        return [(refs[i], refs[n + i].at[slot], _flip(me, ((k >> 2) & 1, (k >> 1) & 1, k & 1)))
                for i in range(n) for k in range(1, 8)]
    return plan


def _pair_fill(lands, *, name):
    n = len(lands)

    def body(*refs):
        in_refs, (send_sems, recv_sems) = refs[:n], refs[2 * n:]
        me = _me()
        sib = _flip(me, (0, 0, 1))
        copies = []
        for i in range(n):
            for k, f in enumerate(CHIP_FLIPS):
                peer = _flip(me, f)
                slot = 2 * peer[0] + peer[1]
                mine, theirs = in_refs[i].at[slot, me[2]], in_refs[i].at[slot, 1 - me[2]]
                cp = pltpu.make_async_remote_copy(src_ref=mine, dst_ref=mine, send_sem=send_sems.at[3 * i + k],
                                                  recv_sem=recv_sems.at[3 * i + k], device_id=sib, device_id_type=MESH)
                cp.start()
                copies.append((cp, pltpu.make_async_remote_copy(
                    src_ref=mine, dst_ref=theirs, send_sem=send_sems.at[3 * i + k], recv_sem=recv_sems.at[3 * i + k],
                    device_id=sib, device_id_type=MESH)))
        for cp, arrival in copies:
            arrival.wait_recv()
            cp.wait_send()

    return list(pl.pallas_call(
        body, name=name, in_specs=[ANY] * n, out_specs=[ANY] * n,
        out_shape=[jax.ShapeDtypeStruct(a.shape, a.dtype) for a in lands], input_output_aliases={i: i for i in range(n)},
        scratch_shapes=[pltpu.SemaphoreType.DMA((3 * n,)), pltpu.SemaphoreType.DMA((3 * n,))])(*lands))


def _own_and_landed(lands, xs):
    chip = _chip()
    return [[jnp.where(chip == j, x, o.reshape(4, *x.shape)[j]) for j in range(4)] for o, x in zip(lands, xs)]


BIG = (("w_in", (D, IN_WIDTH // 4), 1), ("gla_w_o", (D // 4, D), 0), ("mla_w_uq", (MQR, MH * MQK // 4), 1),
       ("mla_w_ukv", (MKVR, MH * (MNOPE + MVD) // 4), 1), ("mla_w_o", (D // 4, D), 0), ("w_out", (D // 4, D), 0),
       ("mlp_w1", (D, DFF // 4), 1), ("mlp_w2", (DFF // 4, D), 0))
ADA_SHARD = (D, 6 * D // 4)
SMALL = (("b_ada", 6 * D), ("norm1_g", D), ("b_merge", 2 * D), ("gla_b_alpha", GH * GDK), ("gla_out_norm_g", GDV),
         ("mla_q_lat_g", MQR), ("mla_kv_lat_g", MKVR), ("mla_qn_g", MQK), ("mla_kn_g", MQK), ("norm2_g", D))


W_IN_SEGMENTS = ((0, 3072, OFF_Q), (3072, 3088, OFF_A), (3088, 3344, OFF_CQ), (3344, 3472, OFF_CKV),
                 (3472, 3504, OFF_KPE + MNOPE), (3504, 5552, OFF_MA))
W_IN_SPLIT = OFF_MA
SMALL_ROWS, SMALL_COLS = 32, 2 * D
W_ALPHA_ROW = 16
LOSS_ROW = 15
SMALL_RED = tuple((n, k) for n, k in SMALL if n != "b_ada")


def _pack_small(grads, d_w_alpha, loss_row, *, name):
    def body(*refs):
        g_refs, wa_ref, loss_ref, out_ref = refs[:-3], refs[-3], refs[-2], refs[-1]
        out_ref[...] = jnp.zeros_like(out_ref)
        for i, ((_, k), g_ref) in enumerate(zip(SMALL_RED, g_refs)):
            out_ref[i:i + 1, 0:k] = g_ref[...]
        out_ref[LOSS_ROW:LOSS_ROW + 1, 0:LANE] = loss_ref[...]
        out_ref[W_ALPHA_ROW:W_ALPHA_ROW + GLR, 0:GH * GDK] = wa_ref[...]

    return pl.pallas_call(body, name=name, out_shape=jax.ShapeDtypeStruct((SMALL_ROWS, SMALL_COLS), F32))(
        *grads, d_w_alpha, loss_row)


def _small_update(gathered, dmod_all, sel, wmv, *, name):
    names = [n for n, _ in SMALL] + ["gla_w_alpha"]
    n_par = len(names)

    def body(sel_ref, g_ref, dmod_ref, *refs):
        in_refs, out_refs, loss_ref, acc = refs[:3 * n_par], refs[3 * n_par:-2], refs[-2], refs[-1]
        total = g_ref[0]
        for j in range(1, 8):
            total = total + g_ref[j]
        acc[...] = total
        loss_ref[...] = acc[LOSS_ROW:LOSS_ROW + 1, 0:LANE]
        row = {n: i for i, (n, _) in enumerate(SMALL_RED)}
        for p, name_p in enumerate(names):
            w_ref, m_ref, v_ref = in_refs[3 * p:3 * p + 3]
            if name_p == "b_ada":
                gv = jnp.sum(dmod_ref[...], axis=0, keepdims=True)
            elif name_p == "gla_w_alpha":
                gv = jnp.zeros((GLR, GDK), F32)
                for j in range(4):
                    blk = acc[W_ALPHA_ROW:W_ALPHA_ROW + GLR, j * GDK:(j + 1) * GDK]
                    gv = gv + jnp.where(sel_ref[0] == j, blk, 0.0)
            else:
                gv = acc[row[name_p]:row[name_p] + 1, 0:w_ref.shape[1]]
            o = out_refs[4 * p:4 * p + 4]
            o[0][...] = gv
            o[1][...], o[2][...], o[3][...] = _adamw_update(w_ref[...], gv, m_ref[...], v_ref[...])

    flat = [a for t in wmv for a in t]
    out_shape = [jax.ShapeDtypeStruct(t[0].shape, F32) for t in wmv for _ in range(4)]
    out_shape.append(jax.ShapeDtypeStruct((1, LANE), F32))
    vmem = pl.BlockSpec(memory_space=pltpu.VMEM)
    outs = pl.pallas_call(
        body, name=name, out_shape=out_shape, in_specs=[pl.BlockSpec(memory_space=pltpu.SMEM), vmem, vmem] + [vmem] * len(flat),
        out_specs=[vmem] * len(out_shape), scratch_shapes=[pltpu.VMEM((SMALL_ROWS, SMALL_COLS), F32)],
    )(sel, gathered, dmod_all, *flat)
    return {n: tuple(outs[4 * p:4 * p + 4]) for p, n in enumerate(names)}, outs[-1][0, 0]


def _full_weights(gathered):
    w = {name: jnp.concatenate(gathered[name], axis=axis) for name, _, axis in BIG if name in gathered and name != "w_in"}
    if "w_in" in gathered:
        shards = gathered["w_in"]
        zeros = lambda n: [jnp.zeros((D, n), shards[0].dtype)]

        def cols(a, b):
            width = IN_WIDTH // 4
            return [shards[j][:, max(a, j * width) - j * width:min(b, (j + 1) * width) - j * width]
                    for j in range(4) if max(a, j * width) < min(b, (j + 1) * width)]

        parts = []
        for a, b, at in sorted(W_IN_SEGMENTS, key=lambda seg: seg[2]):
            have = sum(p.shape[1] for p in parts)
            parts += (zeros(at - have) if at > have else []) + cols(a, b)
        w["w_in"] = jnp.concatenate(parts + zeros(PW - sum(p.shape[1] for p in parts)), axis=1)
    if "mla_w_uq" in w:
        w["mla_w_uq"] = jnp.pad(w["mla_w_uq"].reshape(MQR, MH, MQK), ((0, 0), (0, 0), (0, LANE - MQK))).reshape(MQR, MH * LANE)
    if "mla_w_o" in w:
        w["mla_w_o"] = jnp.pad(w["mla_w_o"].reshape(MH, MVD, D), ((0, 0), (0, LANE - MVD), (0, 0))).reshape(MH * LANE, D)
    return w


def _grad_slots(g):
    g = dict(g)
    out = {}
    if "w_in" in g:
        g_lo, g_hi = g.pop("w_in")
        take = lambda at, lo, hi: g_lo[:, at + lo:at + hi] if at < W_IN_SPLIT else g_hi[:, at - W_IN_SPLIT + lo:at - W_IN_SPLIT + hi]
        width = IN_WIDTH // 4
        slots = []
        for j in range(4):
            lo, hi = j * width, (j + 1) * width
            slots.append(jnp.concatenate([take(at, max(lo, a) - a, min(hi, b) - a)
                                          for a, b, at in W_IN_SEGMENTS if max(lo, a) < min(hi, b)], axis=1))
        out["w_in"] = jnp.stack(slots).reshape(4, 2, D // 2, width)
    if "mla_w_uq" in g:
        g["mla_w_uq"] = g["mla_w_uq"].reshape(MQR, MH, LANE)[:, :, :MQK].reshape(MQR, MH * MQK)
    if "mla_w_o" in g:
        g["mla_w_o"] = g["mla_w_o"].reshape(MH, LANE, D)[:, :MVD].reshape(MH * MVD, D)
    for name, (rows, cols), axis in BIG:
        if name not in g:
            continue
        a = g[name]
        a = a.reshape(4, rows, cols) if axis == 0 else jnp.transpose(a.reshape(rows, 4, cols), (1, 0, 2))
        out[name] = a.reshape(4, 2, rows // 2, cols)
    return out


def _rope_tables(positions):
    freqs = ROPE_THETA ** (-jnp.arange(0, MROPE, 2, dtype=F32) / MROPE)
    lane = np.arange(LANE)
    in_rope = (lane >= MNOPE) & (lane < MQK)
    freq_lane = jnp.where(in_rope, freqs[(lane - MNOPE) % (MROPE // 2)], 0.0)
    sign = np.where(in_rope, np.where(lane < MNOPE + MROPE // 2, -1.0, 1.0), 0.0).astype(np.float32)
    ang = positions.astype(F32).reshape(-1, 1) * freq_lane[None, :]
    return jnp.cos(ang), jnp.sin(ang) * sign[None, :]


def _local_step(x, positions, mod, target, w, small, more_weights=None, on_grads=None):
    kept = {}
    if on_grads is None:
        on_grads = lambda tag, grads, after: kept.update(grads)
    bsz, s, _ = x.shape
    t = bsz * s
    tt = _tile(t, 1024)
    shift1, scale1, gate1, shift2, scale2, gate2 = [mod[:, None, i * D:(i + 1) * D] for i in range(6)]
    cos_t, sin_t = _rope_tables(positions)
    w_alpha_p = jnp.pad(small["gla_w_alpha"], ((0, LANE - GLR), (0, 0)))
    gq = jnp.pad(small["mla_qn_g"], ((0, 0), (0, LANE - MQK)))
    gk = jnp.pad(small["mla_kn_g"], ((0, 0), (0, LANE - MQK)))
    flat2 = lambda a: a.reshape(t, a.shape[-1])
    bsd = lambda a: a.reshape(bsz, s, a.shape[-1])

    h = _norm_mod(x, small["norm1_g"], scale1, shift1, name="norm1")
    if callable(w):
        w = w(h)
    proj = _mm(flat2(h), w["w_in"], name="proj", tn=1152, out_dtype=BF16)
    proj3 = bsd(proj)
    o, o_gated, states = _gla_fwd(proj3, w_alpha_p, small["gla_b_alpha"], small["gla_out_norm_g"], name="gla_fwd")
    if more_weights is not None:
        w = {**w, **more_weights(o_gated)}
    y_a = _mm(flat2(o_gated), w["gla_w_o"], name="gla_out", out_dtype=BF16)
    cq_n, ckv_n = _lat_norm(proj, small["mla_q_lat_g"], small["mla_kv_lat_g"], name="lat_norm")
    q_raw = _mm(cq_n, w["mla_w_uq"], name="mla_uq", out_dtype=BF16)
    kv = _mm(ckv_n, w["mla_w_ukv"], name="mla_ukv", out_dtype=BF16)
    qf, kf, vf = _qk_prep(q_raw, kv, proj, cos_t, sin_t, gq * Q_PRESCALE, gk, name="qk_prep")
    o_attn = _attn_fwd(bsd(qf), bsd(kf), bsd(vf), name="attn_fwd")
    y_b = _mm(flat2(o_attn), w["mla_w_o"], name="mla_out", out_dtype=BF16)
    mixed_in = _merge_fwd(proj3, small["b_merge"], bsd(y_a), bsd(y_b), name="merge_fwd")
    mixed = _mm(flat2(mixed_in), w["w_out"], name="w_out")
    x1, h2 = _resid_norm_mod(x, bsd(mixed), gate1, small["norm2_g"], scale2, shift2, name="norm2")

    def sqrelu(acc, ex, outs):
        r = jnp.maximum(acc, 0.0)
        outs[0][...] = (r * r).astype(BF16)

    r = _mm(flat2(h2), w["mlp_w1"], name="mlp1", epilogue=sqrelu, out_shape=jax.ShapeDtypeStruct((t, DFF), BF16),
            out_specs=_tile_spec(tt, 1024))
    ff = _mm(r, w["mlp_w2"], name="mlp2")
    dy, dff, dgate2, loss_part = _loss_head(x1, bsd(ff), gate2, target, name="loss_head")

    g = {}

    def relu2_bwd(acc, ex, outs):
        outs[0][...] = (acc * (2.0 * jnp.sqrt(ex[0][...].astype(F32)))).astype(BF16)

    dff2 = flat2(dff)
    da1 = _mm(dff2, w["mlp_w2"], tb=True, name="mlp2_dx", epilogue=relu2_bwd, extras=(r,),
              extra_specs=(_tile_spec(tt, 1024),), out_shape=jax.ShapeDtypeStruct((t, DFF), BF16),
              out_specs=_tile_spec(tt, 1024))
    g["mlp_w2"] = _mm(r, dff2, ta=True, name="mlp2_dw")
    dh2 = _mm(da1, w["mlp_w1"], tb=True, name="mlp1_dx")
    g["mlp_w1"] = _mm(flat2(h2), da1, ta=True, name="mlp1_dw")
    token = on_grads("mlp", {n: g.pop(n) for n in ("mlp_w2", "mlp_w1")}, dh2)
    if token is not None:
        gate1 = gate1 + token[0, 0]
    dx1, dscale2, dshift2, dg2, dgate1, dmixed = _norm_mod_bwd(
        bsd(dh2), x1, dy, small["norm2_g"], scale2, gate1, bsd(mixed), name="norm2_bwd")
    dmixed2 = flat2(dmixed)
    dmi = _mm(dmixed2, w["w_out"], tb=True, name="w_out_dx", out_dtype=BF16)
    g["w_out"] = _mm(flat2(mixed_in), dmixed2, ta=True, name="w_out_dw")
    dy_a, dy_b, dl_a, dl_b, db_a, db_b = _merge_bwd(bsd(dmi), proj3, small["b_merge"], bsd(y_a), bsd(y_b), name="merge_bwd")
    dy_a2, dy_b2 = flat2(dy_a), flat2(dy_b)
    dog = _mm(dy_a2, w["gla_w_o"], tb=True, name="gla_out_dx")
    g["gla_w_o"] = _mm(flat2(o_gated), dy_a2, ta=True, name="gla_out_dw")
    dq_g, dk_g, dv_g, dg_g, dlog, db_alpha, d_ong = _gla_bwd(
        bsd(dog), o, states, proj3, w_alpha_p, small["gla_b_alpha"], small["gla_out_norm_g"], name="gla_bwd")
    dlog2 = flat2(dlog)
    da_p = _mm(dlog2, w_alpha_p, tb=True, out_dtype=BF16, name="alpha_dx")
    d_w_alpha = _mm(proj[:, OFF_A:OFF_A + LANE], dlog2, ta=True, name="alpha_dw")[:GLR]
    do_attn = _mm(dy_b2, w["mla_w_o"], tb=True, out_dtype=BF16, name="mla_out_dx")
    g["mla_w_o"] = _mm(flat2(o_attn), dy_b2, ta=True, name="mla_out_dw")
    dqf, dkf, dvf = _attn_bwd(bsd(qf), bsd(kf), bsd(vf), bsd(do_attn), name="attn_bwd")
    dq_raw, dkv, dkpe, dgq, dgk = _qk_prep_bwd(flat2(dqf), flat2(dkf), flat2(dvf), q_raw, kv, proj, cos_t, sin_t, gq, gk,
                                                name="qk_prep_bwd")
    dcq_n = _mm(dq_raw, w["mla_w_uq"], tb=True, name="mla_uq_dx")
    g["mla_w_uq"] = _mm(cq_n, dq_raw, ta=True, name="mla_uq_dw")
    dckv_n = _mm(dkv, w["mla_w_ukv"], tb=True, name="mla_ukv_dx")
    g["mla_w_ukv"] = _mm(ckv_n, dkv, ta=True, name="mla_ukv_dw")
    token = on_grads("mix", {n: g.pop(n) for n in ("w_out", "gla_w_o", "mla_w_o", "mla_w_uq", "mla_w_ukv")}, dckv_n)
    q_lat_g = small["mla_q_lat_g"] if token is None else small["mla_q_lat_g"] + token[0:1, 0:1]
    dcq, dckv, dg_qlat, dg_kvlat = _lat_norm_bwd(dcq_n, dckv_n, proj, q_lat_g, small["mla_kv_lat_g"],
                                                  name="lat_norm_bwd")
    pieces = [(flat2(dq_g), OFF_Q), (flat2(dk_g), OFF_K), (flat2(dv_g), OFF_V), (flat2(dg_g), OFF_G),
              (flat2(dl_a), OFF_MA), (flat2(dl_b), OFF_MB), (dcq, OFF_CQ), (dckv, OFF_CKV), (da_p, OFF_A), (dkpe, OFF_KPE)]
    hb = flat2(h)
    g_w_in = (_pieces_dw(hb, [p for p, off in pieces if off < W_IN_SPLIT], name="proj_dw_a"),
              _pieces_dw(hb, [p for p, off in pieces if off >= W_IN_SPLIT], name="proj_dw_b"))
    token = on_grads("in", {"w_in": g_w_in}, g_w_in[1])
    after = jnp.zeros((8, LANE), F32) if token is None else token
    dh = _pieces_dx(pieces, w["w_in"], after, name="proj_dx")
    token = on_grads("dx", {}, dh)
    if token is not None:
        scale1 = scale1 + token[0, 0]
    grad_x, dscale1, dshift1, dg1 = _norm_mod_bwd(bsd(dh), x, dx1, small["norm1_g"], scale1, name="norm1_bwd")

    dmod = jnp.concatenate([dshift1, dscale1, dgate1, dshift2, dscale2, dgate2], axis=-1).reshape(bsz, 6 * D)
    gs = {"norm1_g": dg1, "b_merge": jnp.concatenate([db_a, db_b], axis=1), "gla_b_alpha": db_alpha,
          "gla_out_norm_g": d_ong, "mla_q_lat_g": dg_qlat, "mla_kv_lat_g": dg_kvlat, "mla_qn_g": dgq[:, :MQK],
          "mla_kn_g": dgk[:, :MQK], "norm2_g": dg2}
    return loss_part[0, 0], grad_x, dmod, {**kept, **g}, gs, d_w_alpha


def kernel(x, c, positions, w_ada, b_ada, norm1_g, w_in, b_merge, gla_w_alpha, gla_b_alpha, gla_out_norm_g, gla_w_o, mla_q_lat_g, mla_w_uq, mla_kv_lat_g, mla_w_ukv, mla_qn_g, mla_kn_g, mla_w_o, w_out, norm2_g, mlp_w1, mlp_w2, loss_target, m_w_ada, m_b_ada, m_norm1_g, m_w_in, m_b_merge, m_gla_w_alpha, m_gla_b_alpha, m_gla_out_norm_g, m_gla_w_o, m_mla_q_lat_g, m_mla_w_uq, m_mla_kv_lat_g, m_mla_w_ukv, m_mla_qn_g, m_mla_kn_g, m_mla_w_o, m_w_out, m_norm2_g, m_mlp_w1, m_mlp_w2, v_w_ada, v_b_ada, v_norm1_g, v_w_in, v_b_merge, v_gla_w_alpha, v_gla_b_alpha, v_gla_out_norm_g, v_gla_w_o, v_mla_q_lat_g, v_mla_w_uq, v_mla_kv_lat_g, v_mla_w_ukv, v_mla_qn_g, v_mla_kn_g, v_mla_w_o, v_w_out, v_norm2_g, v_mlp_w1, v_mlp_w2):
    args = dict(locals())
    names_big = [n for n, _, _ in BIG]
    names_small = [n for n, _ in SMALL]
    bsz = x.shape[0]
    ax, ay, ac = lax.axis_index("x"), lax.axis_index("y"), lax.axis_index("c")
    chip = 2 * ax + ay
    dev = 2 * chip + ac

    small = {n: args[n] for n in names_small}
    sel_c = jnp.reshape(ac, (1,)).astype(jnp.int32)
    sel_chip = jnp.reshape(chip, (1,)).astype(jnp.int32)
    c_all, w_alpha_all = _all_gather8([c, gla_w_alpha[0]], name="comm_c_alpha")
    small["gla_w_alpha"] = jnp.concatenate([w_alpha_all[2 * j] for j in range(4)], axis=1)
    c_all = c_all.reshape(8 * bsz, D)

    shards = {n: args[n][0].astype(BF16) for n in names_big}
    halves_of = lambda names: [shards[n].reshape(2, shards[n].shape[0] // 2, shards[n].shape[1]) for n in names]

    def gather_start(names, deps, tag):
        xs = halves_of(names)
        lands = [lax.empty((4, *xh.shape), BF16) for xh in xs]
        plan = _gather_plan(len(names))
        return names, plan, _rdma_start(xs + lands, 3 * len(names), plan, deps, name="comm_weights_start_" + tag)

    def gather_finish(started, after, tag):
        names, plan, sems = started
        arrs = _rdma_wait(sems[0], sems[1], sems[2], plan, after, name="comm_weights_wait_" + tag)
        filled = _pair_fill(arrs[len(names):], name="comm_weights_pair_" + tag)
        own = [a.reshape(shards[n].shape) for n, a in zip(names, arrs)]
        return _full_weights(dict(zip(names, _own_and_landed(filled, own))))


    def add_bias(acc, ex, outs):
        outs[0][...] = acc + ex[0][...]

    silu = lambda v: v * _sigmoid(v)
    b_ada_mine = lax.dynamic_slice(b_ada, (0, chip * ADA_SHARD[1]), (1, ADA_SHARD[1]))
    mod_part = _mm(c_all, w_ada[0], name="ada", tn=512, a_fn=silu, epilogue=add_bias, extras=(b_ada_mine,),
                   extra_specs=(pl.BlockSpec((1, 512), lambda i, j, k: (0, j)),),
                   out_shape=jax.ShapeDtypeStruct((8 * bsz, ADA_SHARD[1]), F32), out_specs=_tile_spec(8 * bsz, 512))
    mod_all = _all_gather8([mod_part], name="comm_mod")[0]
    mod_rows = lax.dynamic_slice(mod_all, (0, dev * bsz, 0), (8, bsz, ADA_SHARD[1]))
    mod = jnp.concatenate([mod_rows[2 * j] for j in range(4)], axis=1)
    first = gather_start(["w_in"], (mod,), "in")
    rest = gather_start([n for n in names_big if n != "w_in"], (mod, first[2][3]), "rest")
    mod = mod + rest[2][3][0, 0]
    w_in_after = lambda after: gather_finish(first, after, "in")
    more_weights = lambda after: gather_finish(rest, after, "rest")

    stage = {}

    def begin(tag, names, arrays, lands, n_copies, plan, what):
        stage[tag] = (names, plan, _rdma_start(arrays + lands, n_copies, plan, (), name=f"comm_{what}_start_{tag}"))
        return stage[tag][2][3]

    def landed(tag, after, what):
        names, plan, sems = stage[tag]
        arrs = _rdma_wait(sems[0], sems[1], sems[2], plan, after, name=f"comm_{what}_wait_{tag}")
        return names, arrs[:len(arrs) // 2], arrs[len(arrs) // 2:]

    def swap_start(tag, grads):
        names = list(grads)
        parts = [_grad_slots(grads)[n] for n in names]
        lands = [lax.empty((4, *p.shape[2:]), F32) for p in parts]
        return begin(tag, names, parts, lands, len(names), _sibling_plan(len(names), lambda r, c: r.at[:, 1 - c]), "pair_sum")

    def scatter_start(tag, after):
        names, parts, sib_halves = landed(tag, after, "pair_sum")
        pairs = [_pair_add(p, s, sel_c, name="pair_add_" + n) for n, p, s in zip(names, parts, sib_halves)]
        recvs = [lax.empty((3, *p.shape[1:]), BF16) for p in pairs]
        return begin(tag, names, pairs, recvs, 3 * len(names), _scatter_plan(len(names)), "scatter")

    def join_start(tag, after):
        names, pairs, recvs = landed(tag, after, "scatter")
        halves = [_chip_sum(p, r, sel_chip, name="chip_sum_" + n) for n, p, r in zip(names, pairs, recvs)]
        lands = [lax.empty(h.shape, F32) for h in halves]
        return begin(tag, names, halves, lands, len(names), _sibling_plan(len(names), lambda r, c: r), "pair_join")

    def reduce_step(tag, grads, after):
        if tag == "mlp":
            return swap_start("mlp", grads)
        if tag == "mix":
            return scatter_start("mlp", after) + swap_start("mix", grads)
        if tag == "in":
            return scatter_start("mix", after) + swap_start("in", grads)
        return scatter_start("in", after)

    loss_part, grad_x, dmod, g, gs, d_w_alpha = _local_step(x, positions, mod, loss_target, w_in_after, small,
                                                            more_weights, reduce_step)

    assert not g, list(g)
    gs_packed = _pack_small([gs[n] for n, _ in SMALL_RED], d_w_alpha, jnp.full((1, LANE), loss_part, F32),
                            name="pack_small")
    small_lands = [lax.empty((8, *a.shape), F32) for a in (dmod, gs_packed)]
    begin("small", ["dmod", "small"], [dmod, gs_packed], small_lands, 7 * 2, _gather8_plan(2), "gather8")

    res = {}

    def finish(tag, after):
        names, halves, theirs = landed(tag, after, "pair_join")
        for n, mine, other in zip(names, halves, theirs):
            if n == "w_in":
                south = ac == 0
                g_t = jnp.concatenate([jnp.where(south, mine, other), jnp.where(south, other, mine)], axis=0).T
                outs = _adamw(w_in[0].T, g_t, m_w_in[0].T, v_w_in[0].T, name="adamw_w_in", by_cols=True)
                res[n] = tuple(a.T for a in (g_t, *outs))
            else:
                res[n] = _adamw_halves(args[n][0], args["m_" + n][0], args["v_" + n][0], mine, other, sel_c,
                                       name="adamw_" + n)
        return res[names[-1]][1]

    join_start("mlp", grad_x)
    join_start("mix", grad_x)
    done = finish("mix", finish("mlp", grad_x))

    _, (dmod_own, gs_own), (dmod_all, gs_all) = landed("small", done, "gather8")
    dmod_all = lax.dynamic_update_slice(dmod_all, dmod_own[None], (dev, 0, 0)).reshape(8 * bsz, 6 * D)
    gs_all = lax.dynamic_update_slice(gs_all, gs_own[None], (dev, 0, 0))
    dmod_mine = lax.dynamic_slice(dmod_all, (0, chip * ADA_SHARD[1]), (8 * bsz, ADA_SHARD[1]))
    g_w_ada = _mm(c_all, dmod_mine, ta=True, a_fn=silu, name="ada_dw")
    wmv = [(args[n], args["m_" + n], args["v_" + n]) for n in names_small]
    wmv.append((gla_w_alpha[0], m_gla_w_alpha[0], v_gla_w_alpha[0]))
    res_small, loss_sum = _small_update(gs_all, dmod_all, sel_chip, wmv, name="small_update")
    res.update(res_small)
    loss = loss_sum * (0.5 / D)
    join_start("in", g_w_ada)
    res["w_ada"] = (g_w_ada, *_adamw(w_ada[0], g_w_ada, m_w_ada[0], v_w_ada[0], name="adamw_w_ada"))
    finish("in", res["w_ada"][1])

    order = ["w_ada", "b_ada", "norm1_g", "w_in", "b_merge", "gla_w_alpha", "gla_b_alpha", "gla_out_norm_g", "gla_w_o",
             "mla_q_lat_g", "mla_w_uq", "mla_kv_lat_g", "mla_w_ukv", "mla_qn_g", "mla_kn_g", "mla_w_o", "w_out",
             "norm2_g", "mlp_w1", "mlp_w2"]
    named = lambda k: [res[n][k].reshape(args[n].shape) for n in order]
    return (loss, grad_x, *named(0), *named(1), *named(2), *named(3))
```

```python
import jax
import jax.numpy as jnp
import numpy as np
from jax import lax
from jax.experimental import pallas as pl
from jax.experimental.pallas import tpu as pltpu

F32 = jnp.float32
BF16 = jnp.bfloat16
MESH = pl.DeviceIdType.MESH

D = 1024
CHUNK = 64
EPS = 1e-6
GH, GDK, GDV, GLR, GTAU = 4, 128, 256, 16, 16.0
MH, MQR, MKVR, MNOPE, MROPE, MVD = 16, 256, 128, 64, 32, 64
MQK = MNOPE + MROPE
DFF = 4 * D
ROPE_THETA = 10000.0
IN_WIDTH = 5552
LANE = 128
OFF_Q, OFF_K, OFF_V, OFF_G, OFF_MA, OFF_MB, OFF_CQ, OFF_CKV, OFF_A, OFF_KPE, PW = (
    0, 512, 1024, 2048, 3072, 4096, 5120, 5376, 5504, 5632, 5760)
ADAM_LR, ADAM_B1, ADAM_B2, ADAM_EPS, ADAM_WD, ADAM_STEP = 0.001, 0.9, 0.999, 1e-08, 0.01, 10
VMEM_LIMIT = 48 * 1024 * 1024


def _params(n_axes):
    return pltpu.CompilerParams(dimension_semantics=("arbitrary",) * n_axes, vmem_limit_bytes=VMEM_LIMIT)


def _tile(n, target):
    if n <= target:
        return n
    best = None
    for t in range(LANE, target + 1, LANE):
        if n % t == 0:
            best = t
    assert best is not None, (n, target)
    return best


def _sigmoid(x):
    return 1.0 / (1.0 + jnp.exp(-x))


MM_VMEM_BUDGET = 36 * 1024 * 1024


def _mm(a, b, *, name, ta=False, tb=False, out_dtype=F32, tm=1024, tn=1024, tk=4096,
        epilogue=None, extras=(), extra_specs=(), out_shape=None, out_specs=None, a_fn=None):
    if ta:
        kdim, m = a.shape
    else:
        m, kdim = a.shape
    if tb:
        n, k2 = b.shape
    else:
        k2, n = b.shape
    assert kdim == k2, (a.shape, b.shape)
    tm, tn, tk = _tile(m, tm), _tile(n, tn), _tile(kdim, tk)
    tiles = lambda rows: 2 * (rows * tk * a.dtype.itemsize + tk * tn * b.dtype.itemsize + rows * tn * 4) + rows * tn * 4
    while out_shape is None and tiles(tm) > MM_VMEM_BUDGET and tm % 256 == 0:
        tm //= 2
    nk = kdim // tk
    a_spec = pl.BlockSpec((tk, tm), lambda i, j, k: (k, i)) if ta else pl.BlockSpec((tm, tk), lambda i, j, k: (i, k))
    b_spec = pl.BlockSpec((tn, tk), lambda i, j, k: (j, k)) if tb else pl.BlockSpec((tk, tn), lambda i, j, k: (k, j))
    dims = (((0 if ta else 1,), (1 if tb else 0,)), ((), ()))
    ne = len(extras)
    if out_shape is None:
        out_shape = jax.ShapeDtypeStruct((m, n), out_dtype)
        out_specs = pl.BlockSpec((tm, tn), lambda i, j, k: (i, j))
    n_out = len(out_shape) if isinstance(out_shape, (list, tuple)) else 1
    in_place = epilogue is None and n_out == 1 and out_shape.dtype == F32
    scratch = [] if (nk == 1 or in_place) else [pltpu.VMEM((tm, tn), F32)]

    def body(a_ref, b_ref, *rest):
        ex, outs = rest[:ne], rest[ne:ne + n_out]
        av = a_ref[...] if a_fn is None else a_fn(a_ref[...])
        prod = lax.dot_general(av.astype(BF16), b_ref[...].astype(BF16), dims, preferred_element_type=F32)

        def finish(val):
            if epilogue is None:
                outs[0][...] = val.astype(outs[0].dtype)
            else:
                epilogue(val, ex, outs)

        if nk == 1:
            finish(prod)
            return
        k = pl.program_id(2)
        acc = outs[0] if in_place else rest[-1]

        @pl.when(k == 0)
        def _():
            acc[...] = prod

        @pl.when(k > 0)
        def _():
            acc[...] += prod

        if not in_place:
            @pl.when(k == nk - 1)
            def _():
                finish(acc[...])

    return pl.pallas_call(
        body, name=name, grid=(m // tm, n // tn, nk),
        in_specs=[a_spec, b_spec, *extra_specs], out_specs=out_specs, out_shape=out_shape,
        scratch_shapes=scratch, compiler_params=_params(3),
    )(a, b, *extras)


def _tile_spec(tm, tn):
    return pl.BlockSpec((tm, tn), lambda i, j, k: (i, j))


def _pieces_dx(pieces, w, after, *, name, tm=256):
    t = pieces[0][0].shape[0]
    tm = _tile(t, tm)
    npc = len(pieces)

    def body(*refs):
        p_refs, w_ref, out_ref = refs[:npc], refs[npc], refs[-1]
        acc = None
        for (arr, off), p_ref in zip(pieces, p_refs):
            part = lax.dot_general(p_ref[...].astype(BF16), w_ref[:, off:off + arr.shape[1]], _NT,
                                   preferred_element_type=F32)
            acc = part if acc is None else acc + part
        out_ref[...] = acc

    return pl.pallas_call(
        body, name=name, grid=(t // tm,),
        in_specs=[pl.BlockSpec((tm, arr.shape[1]), lambda i: (i, 0)) for arr, _ in pieces]
        + [pl.BlockSpec(w.shape, lambda i: (0, 0)), pl.BlockSpec((8, LANE), lambda i: (0, 0))],
        out_specs=pl.BlockSpec((tm, w.shape[0]), lambda i: (i, 0)),
        out_shape=jax.ShapeDtypeStruct((t, w.shape[0]), F32), compiler_params=_params(1),
    )(*[arr for arr, _ in pieces], w, after)


def _pieces_dw(h, pieces, *, name, tk=1024):
    t, d = h.shape
    tk = _tile(t, tk)
    widths = [p.shape[1] for p in pieces]
    starts = [sum(widths[:i]) for i in range(len(pieces))]

    def body(h_ref, *refs):
        p_refs, out_ref = refs[:-1], refs[-1]
        first = pl.program_id(0) == 0
        hv = h_ref[...]
        for p_ref, start, width in zip(p_refs, starts, widths):
            part = lax.dot_general(hv, p_ref[...].astype(BF16), _TN, preferred_element_type=F32)
            cols = slice(start, start + width)

            @pl.when(first)
            def _():
                out_ref[:, cols] = part

            @pl.when(jnp.logical_not(first))
            def _():
                out_ref[:, cols] += part

    return pl.pallas_call(
        body, name=name, grid=(t // tk,),
        in_specs=[pl.BlockSpec((tk, d), lambda k: (k, 0))] + [pl.BlockSpec((tk, wd), lambda k: (k, 0)) for wd in widths],
        out_specs=pl.BlockSpec((d, sum(widths)), lambda k: (0, 0)),
        out_shape=jax.ShapeDtypeStruct((d, sum(widths)), F32), compiler_params=_params(1),
    )(h, *pieces)


def _rms(x, g):
    r = lax.rsqrt(jnp.mean(x * x, axis=-1, keepdims=True) + EPS)
    return x * r, r


def _row_spec(ts, width, col=0):
    return pl.BlockSpec((None, ts, width), lambda b, i: (b, i, col))


def _vec_spec(width):
    return pl.BlockSpec((None, 1, width), lambda b, i: (b, 0, 0))


def _gain_spec(width):
    return pl.BlockSpec((1, width), lambda b, i: (0, 0))


def _norm_mod(x, g, scale, shift, *, name, ts=512):
    bsz, s, d = x.shape
    ts = min(ts, s)

    def body(x_ref, g_ref, sc_ref, sh_ref, h_ref):
        xh, _ = _rms(x_ref[...], None)
        h_ref[...] = ((xh * g_ref[...]) * (1.0 + sc_ref[...]) + sh_ref[...]).astype(BF16)

    return pl.pallas_call(
        body, name=name, grid=(bsz, s // ts),
        in_specs=[_row_spec(ts, d), _gain_spec(d), _vec_spec(d), _vec_spec(d)],
        out_specs=_row_spec(ts, d), out_shape=jax.ShapeDtypeStruct((bsz, s, d), BF16),
        compiler_params=_params(2),
    )(x, g, scale, shift)


def _resid_norm_mod(x, mixed, gate, g, scale, shift, *, name, ts=512):
    bsz, s, d = x.shape
    ts = min(ts, s)

    def body(x_ref, mx_ref, gt_ref, g_ref, sc_ref, sh_ref, x1_ref, h_ref):
        x1 = x_ref[...] + gt_ref[...] * mx_ref[...]
        x1_ref[...] = x1
        xh, _ = _rms(x1, None)
        h_ref[...] = ((xh * g_ref[...]) * (1.0 + sc_ref[...]) + sh_ref[...]).astype(BF16)

    return pl.pallas_call(
        body, name=name, grid=(bsz, s // ts),
        in_specs=[_row_spec(ts, d), _row_spec(ts, d), _vec_spec(d), _gain_spec(d), _vec_spec(d), _vec_spec(d)],
        out_specs=[_row_spec(ts, d), _row_spec(ts, d)],
        out_shape=[jax.ShapeDtypeStruct((bsz, s, d), F32), jax.ShapeDtypeStruct((bsz, s, d), BF16)],
        compiler_params=_params(2),
    )(x, mixed, gate, g, scale, shift)


def _norm_mod_bwd(dh, xin, resid, g, scale, gate=None, mixed=None, *, name, ts=512):
    bsz, s, d = xin.shape
    ts = min(ts, s)
    gated = gate is not None

    def body(*refs):
        if gated:
            dh_ref, x_ref, rs_ref, g_ref, sc_ref, gt_ref, mx_ref, dx_ref, dsc_ref, dsh_ref, dg_ref, dgt_ref, dmx_ref = refs
        else:
            dh_ref, x_ref, rs_ref, g_ref, sc_ref, dx_ref, dsc_ref, dsh_ref, dg_ref = refs
        b, i = pl.program_id(0), pl.program_id(1)

        @pl.when(i == 0)
        def _():
            dsc_ref[...] = jnp.zeros_like(dsc_ref)
            dsh_ref[...] = jnp.zeros_like(dsh_ref)
            if gated:
                dgt_ref[...] = jnp.zeros_like(dgt_ref)

        @pl.when((i == 0) & (b == 0))
        def _():
            dg_ref[...] = jnp.zeros_like(dg_ref)

        dh_v, gv = dh_ref[...], g_ref[...]
        xh, r = _rms(x_ref[...], None)
        dsc_ref[...] += jnp.sum(dh_v * (xh * gv), axis=0, keepdims=True)
        dsh_ref[...] += jnp.sum(dh_v, axis=0, keepdims=True)
        dn = dh_v * (1.0 + sc_ref[...])
        dg_ref[...] += jnp.sum(dn * xh, axis=0, keepdims=True)
        dxh = dn * gv
        dx = rs_ref[...] + r * (dxh - xh * jnp.mean(dxh * xh, axis=-1, keepdims=True))
        dx_ref[...] = dx
        if gated:
            dgt_ref[...] += jnp.sum(dx * mx_ref[...], axis=0, keepdims=True)
            dmx_ref[...] = (dx * gt_ref[...]).astype(BF16)

    ins = [dh, xin, resid, g, scale]
    in_specs = [_row_spec(ts, d), _row_spec(ts, d), _row_spec(ts, d), _gain_spec(d), _vec_spec(d)]
    out_specs = [_row_spec(ts, d), _vec_spec(d), _vec_spec(d), _gain_spec(d)]
    out_shape = [jax.ShapeDtypeStruct((bsz, s, d), F32), jax.ShapeDtypeStruct((bsz, 1, d), F32),
                 jax.ShapeDtypeStruct((bsz, 1, d), F32), jax.ShapeDtypeStruct((1, d), F32)]
    if gated:
        ins += [gate, mixed]
        in_specs += [_vec_spec(d), _row_spec(ts, d)]
        out_specs += [_vec_spec(d), _row_spec(ts, d)]
        out_shape += [jax.ShapeDtypeStruct((bsz, 1, d), F32), jax.ShapeDtypeStruct((bsz, s, d), BF16)]
    return pl.pallas_call(
        body, name=name, grid=(bsz, s // ts), in_specs=in_specs, out_specs=out_specs, out_shape=out_shape,
        compiler_params=_params(2),
    )(*ins)


def _loss_head(x1, ff, gate2, target, *, name, ts=512):
    bsz, s, d = x1.shape
    ts = min(ts, s)

    def body(x1_ref, ff_ref, gt_ref, t_ref, dy_ref, dff_ref, dgt_ref, loss_ref, acc):
        b, i = pl.program_id(0), pl.program_id(1)

        @pl.when(i == 0)
        def _():
            dgt_ref[...] = jnp.zeros_like(dgt_ref)

        @pl.when((i == 0) & (b == 0))
        def _():
            acc[...] = jnp.zeros_like(acc)

        ffv, gt = ff_ref[...], gt_ref[...]
        diff = (x1_ref[...] + gt * ffv) - t_ref[...]
        acc[...] += jnp.sum((diff * diff).reshape(ts // 8, 8, d), axis=0)
        dy = diff * (1.0 / d)
        dy_ref[...] = dy
        dgt_ref[...] += jnp.sum(dy * ffv, axis=0, keepdims=True)
        dff_ref[...] = (dy * gt).astype(BF16)

        @pl.when((i == pl.num_programs(1) - 1) & (b == pl.num_programs(0) - 1))
        def _():
            loss_ref[...] = jnp.full(loss_ref.shape, jnp.sum(acc[...]), F32)

    return pl.pallas_call(
        body, name=name, grid=(bsz, s // ts),
        in_specs=[_row_spec(ts, d), _row_spec(ts, d), _vec_spec(d), _row_spec(ts, d)],
        out_specs=[_row_spec(ts, d), _row_spec(ts, d), _vec_spec(d), pl.BlockSpec((8, LANE), lambda b, i: (0, 0))],
        out_shape=[jax.ShapeDtypeStruct((bsz, s, d), F32), jax.ShapeDtypeStruct((bsz, s, d), BF16),
                   jax.ShapeDtypeStruct((bsz, 1, d), F32), jax.ShapeDtypeStruct((8, LANE), F32)],
        scratch_shapes=[pltpu.VMEM((8, d), F32)], compiler_params=_params(2),
    )(x1, ff, gate2, target)


def _merge_fwd(proj, b_merge, y_a, y_b, *, name, ts=512):
    bsz, s, _ = proj.shape
    ts = min(ts, s)

    def body(la_ref, lb_ref, ba_ref, bb_ref, ya_ref, yb_ref, out_ref):
        ga = _sigmoid(la_ref[...] + ba_ref[...])
        gb = _sigmoid(lb_ref[...] + bb_ref[...])
        out_ref[...] = (ga * ya_ref[...] + gb * yb_ref[...]).astype(BF16)

    return pl.pallas_call(
        body, name=name, grid=(bsz, s // ts),
        in_specs=[_row_spec(ts, D, OFF_MA // D), _row_spec(ts, D, OFF_MB // D),
                  pl.BlockSpec((1, D), lambda b, i: (0, 0)), pl.BlockSpec((1, D), lambda b, i: (0, 1)),
                  _row_spec(ts, D), _row_spec(ts, D)],
        out_specs=_row_spec(ts, D), out_shape=jax.ShapeDtypeStruct((bsz, s, D), BF16),
        compiler_params=_params(2),
    )(proj, proj, b_merge, b_merge, y_a, y_b)


def _merge_bwd(dmi, proj, b_merge, y_a, y_b, *, name, ts=512):
    bsz, s, _ = proj.shape
    ts = min(ts, s)

    def body(d_ref, la_ref, lb_ref, ba_ref, bb_ref, ya_ref, yb_ref, dya_ref, dyb_ref, dla_ref, dlb_ref, dba_ref, dbb_ref):
        @pl.when((pl.program_id(0) == 0) & (pl.program_id(1) == 0))
        def _():
            dba_ref[...] = jnp.zeros_like(dba_ref)
            dbb_ref[...] = jnp.zeros_like(dbb_ref)

        dv = d_ref[...].astype(F32)
        ga = _sigmoid(la_ref[...] + ba_ref[...])
        gb = _sigmoid(lb_ref[...] + bb_ref[...])
        dya_ref[...] = (dv * ga).astype(BF16)
        dyb_ref[...] = (dv * gb).astype(BF16)
        dla = (dv * ya_ref[...]) * (ga * (1.0 - ga))
        dlb = (dv * yb_ref[...]) * (gb * (1.0 - gb))
        dla_ref[...] = dla.astype(BF16)
        dlb_ref[...] = dlb.astype(BF16)
        dba_ref[...] += jnp.sum(dla, axis=0, keepdims=True)
        dbb_ref[...] += jnp.sum(dlb, axis=0, keepdims=True)

    act = jax.ShapeDtypeStruct((bsz, s, D), BF16)
    return pl.pallas_call(
        body, name=name, grid=(bsz, s // ts),
        in_specs=[_row_spec(ts, D), _row_spec(ts, D, OFF_MA // D), _row_spec(ts, D, OFF_MB // D),
                  pl.BlockSpec((1, D), lambda b, i: (0, 0)), pl.BlockSpec((1, D), lambda b, i: (0, 1)),
                  _row_spec(ts, D), _row_spec(ts, D)],
        out_specs=[_row_spec(ts, D)] * 4 + [_gain_spec(D)] * 2,
        out_shape=[act, act, act, act, jax.ShapeDtypeStruct((1, D), F32), jax.ShapeDtypeStruct((1, D), F32)],
        compiler_params=_params(2),
    )(dmi, proj, proj, b_merge, b_merge, y_a, y_b)


def _tri(lower):
    r = lax.broadcasted_iota(jnp.int32, (CHUNK, CHUNK), 0)
    c = lax.broadcasted_iota(jnp.int32, (CHUNK, CHUNK), 1)
    return jnp.where((c <= r) if lower else (c >= r), 1.0, 0.0).astype(F32)


def _gla_logits(a_ref, wal_ref, bal_ref):
    logits = jnp.dot(a_ref[...].astype(BF16), wal_ref[...].astype(BF16), preferred_element_type=F32) + bal_ref[...]
    la = (jnp.minimum(logits, 0.0) - jnp.log(1.0 + jnp.exp(-jnp.abs(logits)))) * (1.0 / GTAU)
    return logits, la


def _chunk_cumsum(la_n, tri, precision=lax.Precision.HIGHEST):
    cum = jnp.dot(tri, la_n, preferred_element_type=F32, precision=precision)
    return cum, jnp.sum(la_n, axis=0, keepdims=True)


def _gla_specs(s, nc):
    def blk(width, off):
        return pl.BlockSpec((None, s, width), lambda h, b: (b, 0, off // width + h))

    proj_specs = [blk(GDK, OFF_Q), blk(GDK, OFF_K), blk(GDV, OFF_V), blk(GDV, OFF_G),
                  pl.BlockSpec((None, s, LANE), lambda h, b: (b, 0, OFF_A // LANE)),
                  pl.BlockSpec((LANE, GDK), lambda h, b: (0, h)), pl.BlockSpec((1, GDK), lambda h, b: (0, h)),
                  pl.BlockSpec((1, GDV), lambda h, b: (0, 0))]
    st_spec = pl.BlockSpec((None, None, nc, GDV, GDK), lambda h, b: (b, h, 0, 0, 0))
    return blk, proj_specs, st_spec


def _gla_fwd(proj, w_alpha_p, b_alpha, out_norm_g, *, name):
    bsz, s, _ = proj.shape
    nc = s // CHUNK
    scale = GDK ** -0.5

    rb = min(512, s)

    def body(q_ref, k_ref, v_ref, g_ref, a_ref, wal_ref, bal_ref, ong_ref, o_ref, og_ref, st_ref):
        _, la = _gla_logits(a_ref, wal_ref, bal_ref)
        tri = _tri(True)
        st = jnp.zeros((GDV, GDK), F32)
        for n in range(nc):
            rows = pl.ds(n * CHUNK, CHUNK)
            cum, cum_end = _chunk_cumsum(la[n * CHUNK:(n + 1) * CHUNK], tri, lax.Precision.HIGH)
            kd = k_ref[rows, :] * jnp.exp(cum_end - cum)
            ut = lax.dot_general(v_ref[rows, :].astype(BF16), kd.astype(BF16), _TN, preferred_element_type=F32)
            st = st * jnp.exp(cum_end) + ut
            st_ref[n] = st
            o_ref[rows, :] = lax.dot_general((q_ref[rows, :].astype(F32) * scale).astype(BF16), st.astype(BF16), _NT,
                                             preferred_element_type=F32)
        for j in range(0, s, rb):
            blk_rows = pl.ds(j, rb)
            oh, _ = _rms(o_ref[blk_rows, :], None)
            gv = g_ref[blk_rows, :].astype(F32)
            og_ref[blk_rows, :] = ((oh * ong_ref[...]) * (gv * _sigmoid(gv))).astype(BF16)

    blk, proj_specs, st_spec = _gla_specs(s, nc)
    return pl.pallas_call(
        body, name=name, grid=(GH, bsz), in_specs=proj_specs, out_specs=[blk(GDV, 0), blk(GDV, 0), st_spec],
        out_shape=[jax.ShapeDtypeStruct((bsz, s, GH * GDV), F32), jax.ShapeDtypeStruct((bsz, s, GH * GDV), BF16),
                   jax.ShapeDtypeStruct((bsz, GH, nc, GDV, GDK), F32)],
        compiler_params=_params(2),
    )(proj, proj, proj, proj, proj, w_alpha_p, b_alpha, out_norm_g)


def _gla_bwd(dog, o, states, proj, w_alpha_p, b_alpha, out_norm_g, *, name):
    bsz, s, _ = proj.shape
    nc = s // CHUNK
    scale = GDK ** -0.5

    def body(dog_ref, o_ref, st_ref, q_ref, k_ref, v_ref, g_ref, a_ref, wal_ref, bal_ref, ong_ref,
             dq_ref, dk_ref, dv_ref, dg_ref, dl_ref, dbal_ref, dong_ref, do_scr, dlog_scr):
        h, b = pl.program_id(0), pl.program_id(1)

        @pl.when(b == 0)
        def _():
            dbal_ref[...] = jnp.zeros_like(dbal_ref)

        @pl.when((b == 0) & (h == 0))
        def _():
            dong_ref[...] = jnp.zeros_like(dong_ref)

        ong = ong_ref[...]
        for j in range(0, s, rb):
            blk_rows = pl.ds(j, rb)
            gv, dogv = g_ref[blk_rows, :].astype(F32), dog_ref[blk_rows, :]
            sg = _sigmoid(gv)
            oh, r = _rms(o_ref[blk_rows, :], None)
            don = dogv * (gv * sg)
            dg_ref[blk_rows, :] = (dogv * (oh * ong) * (sg * (1.0 + gv * (1.0 - sg)))).astype(BF16)
            dong_ref[...] += jnp.sum(don * oh, axis=0, keepdims=True)
            doh = don * ong
            do_scr[blk_rows, :] = (r * (doh - oh * jnp.mean(doh * oh, axis=-1, keepdims=True))).astype(BF16)

        logits, la = _gla_logits(a_ref, wal_ref, bal_ref)
        tri_lo, tri_up = _tri(True), _tri(False)
        carry = jnp.zeros((GDV, GDK), F32)
        for n in range(nc - 1, -1, -1):
            rows = pl.ds(n * CHUNK, CHUNK)
            cum, cum_end = _chunk_cumsum(la[n * CHUNK:(n + 1) * CHUNK], tri_lo, lax.Precision.HIGH)
            decay = jnp.exp(cum_end)
            w = jnp.exp(cum_end - cum)
            kd = k_ref[rows, :] * w
            do_b = do_scr[rows, :]
            qs_b = (q_ref[rows, :].astype(F32) * scale).astype(BF16)
            dq_ref[rows, :] = (jnp.dot(do_b, st_ref[n].astype(BF16), preferred_element_type=F32) * scale).astype(BF16)
            dsn = lax.dot_general(do_b, qs_b, _TN, preferred_element_type=F32) + carry
            carry = dsn * decay
            dsn_b = dsn.astype(BF16)
            dv_ref[rows, :] = lax.dot_general(kd.astype(BF16), dsn_b, _NT, preferred_element_type=F32).astype(BF16)
            dkd = jnp.dot(v_ref[rows, :].astype(BF16), dsn_b, preferred_element_type=F32)
            dk_ref[rows, :] = (dkd * w).astype(BF16)
            e = dkd * kd
            dcum_end = jnp.sum(e, axis=0, keepdims=True)
            if n > 0:
                dcum_end += jnp.sum(dsn * st_ref[n - 1], axis=0, keepdims=True) * decay
            dlog_scr[rows, :] = dcum_end - jnp.dot(tri_up, e, preferred_element_type=F32,
                                                  precision=lax.Precision.HIGH)
        dlog = dlog_scr[...] * (1.0 / GTAU) * (1.0 - _sigmoid(logits))
        dl_ref[...] = dlog.astype(BF16)
        dbal_ref[...] += jnp.sum(dlog, axis=0, keepdims=True)

    rb = min(512, s)

    blk, proj_specs, st_spec = _gla_specs(s, nc)
    act = lambda wd: jax.ShapeDtypeStruct((bsz, s, wd), BF16)
    return pl.pallas_call(
        body, name=name, grid=(GH, bsz), in_specs=[blk(GDV, 0), blk(GDV, 0), st_spec, *proj_specs],
        out_specs=[blk(GDK, 0), blk(GDK, 0), blk(GDV, 0), blk(GDV, 0), blk(GDK, 0),
                   pl.BlockSpec((1, GDK), lambda h, b: (0, h)), pl.BlockSpec((1, GDV), lambda h, b: (0, 0))],
        out_shape=[act(GH * GDK), act(GH * GDK), act(GH * GDV), act(GH * GDV), act(GH * GDK),
                   jax.ShapeDtypeStruct((1, GH * GDK), F32), jax.ShapeDtypeStruct((1, GDV), F32)],
        scratch_shapes=[pltpu.VMEM((s, GDV), BF16), pltpu.VMEM((s, GDK), F32)], compiler_params=_params(2),
    )(dog, o, states, proj, proj, proj, proj, proj, w_alpha_p, b_alpha, out_norm_g)


def _lane():
    return lax.broadcasted_iota(jnp.int32, (1, LANE), 1)


def _swap_halves(x):
    lane = _lane()
    half = MROPE // 2
    lo = (lane >= MNOPE) & (lane < MNOPE + half)
    hi = (lane >= MNOPE + half) & (lane < MQK)
    return jnp.where(lo, pltpu.roll(x, LANE - half, 1), jnp.where(hi, pltpu.roll(x, half, 1), 0.0))


def _norm96(x, g):
    r = lax.rsqrt(jnp.sum(x * x, axis=-1, keepdims=True) * (1.0 / MQK) + EPS)
    return x * r, r


def _lat_norm(proj, q_lat_g, kv_lat_g, *, name, ts=512):
    t = proj.shape[0]
    ts = min(ts, t)

    def body(cq_ref, ckv_ref, gq_ref, gk_ref, oq_ref, ok_ref):
        xq, _ = _rms(cq_ref[...].astype(F32), None)
        oq_ref[...] = (xq * gq_ref[...]).astype(BF16)
        xk, _ = _rms(ckv_ref[...].astype(F32), None)
        ok_ref[...] = (xk * gk_ref[...]).astype(BF16)

    return pl.pallas_call(
        body, name=name, grid=(t // ts,),
        in_specs=[pl.BlockSpec((ts, MQR), lambda i: (i, OFF_CQ // MQR)), pl.BlockSpec((ts, MKVR), lambda i: (i, OFF_CKV // MKVR)),
                  pl.BlockSpec((1, MQR), lambda i: (0, 0)), pl.BlockSpec((1, MKVR), lambda i: (0, 0))],
        out_specs=[pl.BlockSpec((ts, MQR), lambda i: (i, 0)), pl.BlockSpec((ts, MKVR), lambda i: (i, 0))],
        out_shape=[jax.ShapeDtypeStruct((t, MQR), BF16), jax.ShapeDtypeStruct((t, MKVR), BF16)],
        compiler_params=_params(1),
    )(proj, proj, q_lat_g, kv_lat_g)


def _lat_norm_bwd(dcqn, dckvn, proj, q_lat_g, kv_lat_g, *, name, ts=512):
    t = proj.shape[0]
    ts = min(ts, t)

    def one(d_ref, x_ref, g_ref, dx_ref, dg_ref):
        xh, r = _rms(x_ref[...].astype(F32), None)
        dn = d_ref[...]
        dg_ref[...] += jnp.sum(dn * xh, axis=0, keepdims=True)
        dxh = dn * g_ref[...]
        dx_ref[...] = (r * (dxh - xh * jnp.mean(dxh * xh, axis=-1, keepdims=True))).astype(BF16)

    def body(dq_ref, dk_ref, cq_ref, ckv_ref, gq_ref, gk_ref, dxq_ref, dxk_ref, dgq_ref, dgk_ref):
        @pl.when(pl.program_id(0) == 0)
        def _():
            dgq_ref[...] = jnp.zeros_like(dgq_ref)
            dgk_ref[...] = jnp.zeros_like(dgk_ref)

        one(dq_ref, cq_ref, gq_ref, dxq_ref, dgq_ref)
        one(dk_ref, ckv_ref, gk_ref, dxk_ref, dgk_ref)

    return pl.pallas_call(
        body, name=name, grid=(t // ts,),
        in_specs=[pl.BlockSpec((ts, MQR), lambda i: (i, 0)), pl.BlockSpec((ts, MKVR), lambda i: (i, 0)),
                  pl.BlockSpec((ts, MQR), lambda i: (i, OFF_CQ // MQR)), pl.BlockSpec((ts, MKVR), lambda i: (i, OFF_CKV // MKVR)),
                  pl.BlockSpec((1, MQR), lambda i: (0, 0)), pl.BlockSpec((1, MKVR), lambda i: (0, 0))],
        out_specs=[pl.BlockSpec((ts, MQR), lambda i: (i, 0)), pl.BlockSpec((ts, MKVR), lambda i: (i, 0)),
                   pl.BlockSpec((1, MQR), lambda i: (0, 0)), pl.BlockSpec((1, MKVR), lambda i: (0, 0))],
        out_shape=[jax.ShapeDtypeStruct((t, MQR), BF16), jax.ShapeDtypeStruct((t, MKVR), BF16),
                   jax.ShapeDtypeStruct((1, MQR), F32), jax.ShapeDtypeStruct((1, MKVR), F32)],
        compiler_params=_params(1),
    )(dcqn, dckvn, proj, proj, q_lat_g, kv_lat_g)


def _qk_prep(q_raw, kv, proj, cos_t, sin_t, gq, gk, *, name, ts=4096):
    t = q_raw.shape[0]
    ts = min(ts, t)

    def body(q_ref, kv_ref, kpe_ref, c_ref, s_ref, gq_ref, gk_ref, qo_ref, ko_ref, vo_ref):
        cs, sn = c_ref[...], s_ref[...]
        nope = _lane() < MNOPE
        qn, _ = _norm96(q_ref[...].astype(F32), None)
        qn = qn * gq_ref[...]
        qo_ref[...] = (qn * cs + _swap_halves(qn) * sn).astype(BF16)
        kvv = kv_ref[...].astype(F32)
        kn, _ = _norm96(jnp.where(nope, kvv, kpe_ref[...].astype(F32)), None)
        kn = kn * gk_ref[...]
        ko_ref[...] = (kn * cs + _swap_halves(kn) * sn).astype(BF16)
        vo_ref[...] = jnp.where(nope, pltpu.roll(kvv, MNOPE, 1), 0.0).astype(BF16)

    hd = pl.BlockSpec((ts, LANE), lambda i, h: (i, h))
    shared = lambda col: pl.BlockSpec((ts, LANE), lambda i, h: (i, col))
    gain = pl.BlockSpec((1, LANE), lambda i, h: (0, 0))
    out = jax.ShapeDtypeStruct((t, MH * LANE), BF16)
    return pl.pallas_call(
        body, name=name, grid=(t // ts, MH),
        in_specs=[hd, hd, shared(OFF_KPE // LANE), shared(0), shared(0), gain, gain],
        out_specs=[hd, hd, hd], out_shape=[out, out, out], compiler_params=_params(2),
    )(q_raw, kv, proj, cos_t, sin_t, gq, gk)


def _qk_prep_bwd(dq, dk, dv, q_raw, kv, proj, cos_t, sin_t, gq, gk, *, name, ts=2048):
    t = q_raw.shape[0]
    ts = min(ts, t)

    def norm_bwd(dy, x, g, dg_ref):
        xh, r = _norm96(x, None)
        dg_ref[...] += jnp.sum(dy * xh, axis=0, keepdims=True)
        dxh = dy * g
        return r * (dxh - xh * (jnp.sum(dxh * xh, axis=-1, keepdims=True) * (1.0 / MQK)))

    def body(dq_ref, dk_ref, dv_ref, q_ref, kv_ref, kpe_ref, c_ref, s_ref, gq_ref, gk_ref,
             dqr_ref, dkv_ref, dkpe_ref, dgq_ref, dgk_ref):
        i, h = pl.program_id(0), pl.program_id(1)

        @pl.when(h == 0)
        def _():
            dkpe_ref[...] = jnp.zeros_like(dkpe_ref)

        @pl.when((h == 0) & (i == 0))
        def _():
            dgq_ref[...] = jnp.zeros_like(dgq_ref)
            dgk_ref[...] = jnp.zeros_like(dgk_ref)

        cs, sn = c_ref[...], s_ref[...]
        lane = _lane()
        nope = lane < MNOPE
        dqv = dq_ref[...]
        dqn = dqv * cs + _swap_halves(dqv * sn)
        dqr_ref[...] = norm_bwd(dqn, q_ref[...].astype(F32), gq_ref[...], dgq_ref).astype(BF16)
        dkv_ = dk_ref[...]
        dkn = dkv_ * cs + _swap_halves(dkv_ * sn)
        kvv = kv_ref[...].astype(F32)
        dkr = norm_bwd(dkn, jnp.where(nope, kvv, kpe_ref[...].astype(F32)), gk_ref[...], dgk_ref)
        dkv_ref[...] = jnp.where(nope, dkr, pltpu.roll(dv_ref[...], MNOPE, 1)).astype(BF16)
        dkpe_ref[...] += jnp.where((lane >= MNOPE) & (lane < MQK), dkr, 0.0)

    hd = pl.BlockSpec((ts, LANE), lambda i, h: (i, h))
    shared = lambda col: pl.BlockSpec((ts, LANE), lambda i, h: (i, col))
    gain = pl.BlockSpec((1, LANE), lambda i, h: (0, 0))
    out = jax.ShapeDtypeStruct((t, MH * LANE), BF16)
    return pl.pallas_call(
        body, name=name, grid=(t // ts, MH),
        in_specs=[hd, hd, hd, hd, hd, shared(OFF_KPE // LANE), shared(0), shared(0), gain, gain],
        out_specs=[hd, hd, shared(0), gain, gain],
        out_shape=[out, out, jax.ShapeDtypeStruct((t, LANE), F32), jax.ShapeDtypeStruct((1, LANE), F32),
                   jax.ShapeDtypeStruct((1, LANE), F32)],
        compiler_params=_params(2),
    )(dq, dk, dv, q_raw, kv, proj, cos_t, sin_t, gq, gk)


_NT = (((1,), (1,)), ((), ()))
_TN = (((0,), (0,)), ((), ()))


SOFTMAX_SCALE = MQK ** -0.5
Q_PRESCALE = SOFTMAX_SCALE * float(np.log2(np.e))


def _attn_weights(q, k_ref, lo, tq):
    row = lax.broadcasted_iota(jnp.int32, (tq, tq), 0) // CHUNK
    col = lax.broadcasted_iota(jnp.int32, (tq, tq), 1) // CHUNK
    sd = lax.dot_general(q, k_ref[pl.ds(lo, tq), :], _NT, preferred_element_type=F32)
    sd = jnp.where(col <= row, sd, -1e30)
    m = jnp.max(sd, axis=-1, keepdims=True)
    if lo:
        so = lax.dot_general(q, k_ref[pl.ds(0, lo), :], _NT, preferred_element_type=F32)
        m = jnp.maximum(m, jnp.max(so, axis=-1, keepdims=True))
        eo = jnp.exp2(so - m)
        ed = jnp.exp2(sd - m)
        return eo, ed, 1.0 / (jnp.sum(eo, axis=-1, keepdims=True) + jnp.sum(ed, axis=-1, keepdims=True))
    ed = jnp.exp2(sd - m)
    return None, ed, 1.0 / jnp.sum(ed, axis=-1, keepdims=True)


def _attn_fwd(q, k, v, *, name, tq=256):
    bsz, s, _ = q.shape
    tq = min(tq, s)

    def body(q_ref, k_ref, v_ref, o_ref):
        for i in range(s // tq):
            lo = i * tq
            eo, ed, inv = _attn_weights(q_ref[pl.ds(lo, tq), :], k_ref, lo, tq)
            o = jnp.dot(ed.astype(BF16), v_ref[pl.ds(lo, tq), :], preferred_element_type=F32)
            if lo:
                o += jnp.dot(eo.astype(BF16), v_ref[pl.ds(0, lo), :], preferred_element_type=F32)
            o_ref[pl.ds(lo, tq), :] = (o * inv).astype(BF16)

    spec = pl.BlockSpec((None, s, LANE), lambda b, h: (b, 0, h))
    return pl.pallas_call(
        body, name=name, grid=(bsz, MH), in_specs=[spec, spec, spec], out_specs=spec,
        out_shape=jax.ShapeDtypeStruct((bsz, s, MH * LANE), BF16), compiler_params=_params(2),
    )(q, k, v)


def _attn_bwd(q, k, v, do, *, name, tq=256):
    bsz, s, _ = q.shape
    tq = min(tq, s)

    def body(q_ref, k_ref, v_ref, do_ref, dq_ref, dk_ref, dv_ref):
        dk_ref[...] = jnp.zeros_like(dk_ref)
        dv_ref[...] = jnp.zeros_like(dv_ref)
        for i in range(s // tq):
            lo = i * tq
            here, before = pl.ds(lo, tq), pl.ds(0, lo)
            qv, dov = q_ref[here, :], do_ref[here, :]
            eo, ed, inv = _attn_weights(qv, k_ref, lo, tq)
            do_n = (dov.astype(F32) * inv).astype(BF16)
            dv_ref[here, :] += lax.dot_general(ed.astype(BF16), do_n, _TN, preferred_element_type=F32)
            dpd = lax.dot_general(dov, v_ref[here, :], _NT, preferred_element_type=F32)
            delta = jnp.sum(dpd * ed, axis=-1, keepdims=True)
            if lo:
                dv_ref[before, :] += lax.dot_general(eo.astype(BF16), do_n, _TN, preferred_element_type=F32)
                dpo = lax.dot_general(dov, v_ref[before, :], _NT, preferred_element_type=F32)
                delta += jnp.sum(dpo * eo, axis=-1, keepdims=True)
            delta = delta * inv
            r = inv * SOFTMAX_SCALE
            dsd = (ed * (dpd - delta) * r).astype(BF16)
            dq = jnp.dot(dsd, k_ref[here, :], preferred_element_type=F32)
            dk_ref[here, :] += lax.dot_general(dsd, qv, _TN, preferred_element_type=F32)
            if lo:
                dso = (eo * (dpo - delta) * r).astype(BF16)
                dq += jnp.dot(dso, k_ref[before, :], preferred_element_type=F32)
                dk_ref[before, :] += lax.dot_general(dso, qv, _TN, preferred_element_type=F32)
            dq_ref[here, :] = dq
        dk_ref[...] = dk_ref[...] * (1.0 / Q_PRESCALE)

    spec = pl.BlockSpec((None, s, LANE), lambda b, h: (b, 0, h))
    out = jax.ShapeDtypeStruct((bsz, s, MH * LANE), F32)
    return pl.pallas_call(
        body, name=name, grid=(bsz, MH), in_specs=[spec] * 4, out_specs=[spec] * 3, out_shape=[out, out, out],
        compiler_params=_params(2),
    )(q, k, v, do)


def _adamw(w, g, m, v, *, name, tr=256, by_cols=False):
    rows, cols = w.shape
    tr = _tile_rows(rows, tr)

    def body(w_ref, g_ref, m_ref, v_ref, d_ref, nm_ref, nv_ref):
        d_ref[...], nm_ref[...], nv_ref[...] = _adamw_update(w_ref[...], g_ref[...], m_ref[...], v_ref[...])

    spec = pl.BlockSpec((rows, LANE), lambda i: (0, i)) if by_cols else pl.BlockSpec((tr, cols), lambda i: (i, 0))
    out = jax.ShapeDtypeStruct((rows, cols), F32)
    return pl.pallas_call(body, name=name, grid=(cols // LANE if by_cols else rows // tr,), in_specs=[spec] * 4,
                          out_specs=[spec] * 3, out_shape=[out, out, out], compiler_params=_params(1))(w, g, m, v)


def _tile_rows(rows, target):
    if rows <= target:
        return rows
    best = 8
    for t in range(8, target + 1, 8):
        if rows % t == 0:
            best = t
    return best


def _adamw_update(w, g, m, v):
    nm = ADAM_B1 * m + (1.0 - ADAM_B1) * g
    nv = ADAM_B2 * v + (1.0 - ADAM_B2) * (g * g)
    m_hat = nm / (1.0 - ADAM_B1 ** ADAM_STEP)
    v_hat = nv / (1.0 - ADAM_B2 ** ADAM_STEP)
    return -ADAM_LR * (m_hat / (jnp.sqrt(v_hat) + ADAM_EPS) + ADAM_WD * w), nm, nv


def _adamw_halves(w, m, v, mine, theirs, sel, *, name, tr=256):
    rows, cols = w.shape
    tr = _tile_rows(rows // 2, tr)
    nh = rows // 2 // tr

    def body(sel_ref, w_ref, m_ref, v_ref, mine_ref, theirs_ref, g_ref, d_ref, nm_ref, nv_ref):
        lower = pl.program_id(0) < nh
        south = sel_ref[0] == 0
        gv = jnp.where(lower == south, mine_ref[...], theirs_ref[...])
        g_ref[...] = gv
        d_ref[...], nm_ref[...], nv_ref[...] = _adamw_update(w_ref[...], gv, m_ref[...], v_ref[...])

    full = pl.BlockSpec((tr, cols), lambda i, sel_ref: (i, 0))
    half = pl.BlockSpec((tr, cols), lambda i, sel_ref: (i % nh, 0))
    out = jax.ShapeDtypeStruct((rows, cols), F32)
    return pl.pallas_call(
        body, name=name, out_shape=[out] * 4, compiler_params=_params(1),
        grid_spec=pltpu.PrefetchScalarGridSpec(num_scalar_prefetch=1, grid=(rows // tr,),
                                               in_specs=[full, full, full, half, half], out_specs=[full] * 4),
    )(sel, w, m, v, mine, theirs)


def _pair_add(x, sib, sel, *, name, tr=256):
    n, _, rows, cols = x.shape
    tr = _tile_rows(rows, tr)

    def body(sel_ref, x_ref, s_ref, o_ref):
        o_ref[...] = (x_ref[...] + s_ref[...]).astype(BF16)

    spec = pl.BlockSpec((None, tr, cols), lambda j, i, sel_ref: (j, i, 0))
    return pl.pallas_call(
        body, name=name, out_shape=jax.ShapeDtypeStruct((n, rows, cols), BF16), compiler_params=_params(2),
        grid_spec=pltpu.PrefetchScalarGridSpec(
            num_scalar_prefetch=1, grid=(n, rows // tr),
            in_specs=[pl.BlockSpec((None, None, tr, cols), lambda j, i, sel_ref: (j, sel_ref[0], i, 0)), spec],
            out_specs=spec),
    )(sel, x, sib)


def _chip_sum(pair, recv, sel, *, name, tr=256):
    _, rows, cols = pair.shape
    tr = _tile_rows(rows, tr)

    def body(sel_ref, p_ref, r_ref, o_ref):
        acc = p_ref[...].astype(F32)
        for k in range(3):
            acc = acc + r_ref[k].astype(F32)
        o_ref[...] = acc

    return pl.pallas_call(
        body, name=name, out_shape=jax.ShapeDtypeStruct((rows, cols), F32), compiler_params=_params(1),
        grid_spec=pltpu.PrefetchScalarGridSpec(
            num_scalar_prefetch=1, grid=(rows // tr,),
            in_specs=[pl.BlockSpec((None, tr, cols), lambda i, sel_ref: (sel_ref[0], i, 0)),
                      pl.BlockSpec((3, tr, cols), lambda i, sel_ref: (0, i, 0))],
            out_specs=pl.BlockSpec((tr, cols), lambda i, sel_ref: (i, 0))),
    )(sel, pair, recv)


def _me():
    return lax.axis_index("x"), lax.axis_index("y"), lax.axis_index("c")


def _flip(pos, bits):
    x, y, c = pos
    return (x ^ bits[0] if bits[0] else x, y ^ bits[1] if bits[1] else y, c ^ bits[2] if bits[2] else c)


ANY = pl.BlockSpec(memory_space=pl.ANY)


def _all_gather8(xs, *, name):
    n = len(xs)
    flips = [((k >> 2) & 1, (k >> 1) & 1, k & 1) for k in range(1, 8)]

    def body(*refs):
        x_refs, out_refs, (send_sems, recv_sems, local_sems) = refs[:n], refs[n:2 * n], refs[2 * n:]
        me = _me()
        slot = lambda p: 4 * p[0] + 2 * p[1] + p[2]
        copies = []
        for i in range(n):
            mine = pltpu.make_async_copy(x_refs[i], out_refs[i].at[slot(me)], local_sems.at[i])
            mine.start()
            copies.append(mine)
            for k, f in enumerate(flips):
                peer = _flip(me, f)
                sems = dict(send_sem=send_sems.at[7 * i + k], recv_sem=recv_sems.at[7 * i + k], device_id=peer,
                            device_id_type=MESH)
                cp = pltpu.make_async_remote_copy(src_ref=x_refs[i], dst_ref=out_refs[i].at[slot(me)], **sems)
                cp.start()
                copies.append(cp)
                copies.append(pltpu.make_async_remote_copy(src_ref=x_refs[i], dst_ref=out_refs[i].at[slot(peer)], **sems))
        for i in range(n):
            base = i * 15
            copies[base].wait()
            for k in range(7):
                copies[base + 1 + 2 * k].wait_send()
                copies[base + 2 + 2 * k].wait_recv()

    outs = pl.pallas_call(
        body, name=name, in_specs=[ANY] * n, out_specs=[ANY] * n,
        out_shape=[jax.ShapeDtypeStruct((8, *x.shape), x.dtype) for x in xs],
        scratch_shapes=[pltpu.SemaphoreType.DMA((7 * n,)), pltpu.SemaphoreType.DMA((7 * n,)),
                        pltpu.SemaphoreType.DMA((n,))])(*xs)
    return list(outs)


CHIP_FLIPS = [(1, 0, 0), (0, 1, 0), (1, 1, 0)]


def _chip():
    return 2 * lax.axis_index("x") + lax.axis_index("y")


HBM = pl.BlockSpec(memory_space=pltpu.HBM)
SEM = pl.BlockSpec(memory_space=pltpu.SEMAPHORE)
EFFECT = pltpu.SideEffectType.DATAFLOW_SIDE_EFFECTING


def _plan_copies(plan, refs, send_sems, recv_sems):
    return [pltpu.make_async_remote_copy(src_ref=src, dst_ref=dst, send_sem=send_sems.at[k], recv_sem=recv_sems.at[k],
                                         device_id=to, device_id_type=MESH) for k, (src, dst, to) in enumerate(plan(refs))]


def _rdma_start(arrays, n_copies, plan, deps, *, name):
    n, nd = len(arrays), len(deps)

    def body(*refs):
        for cp in _plan_copies(plan, refs[:n], refs[n + nd], refs[n + nd + 1]):
            cp.start()
        refs[-1][...] = jnp.zeros_like(refs[-1])

    outs = pl.pallas_call(
        body, name=name,
        out_shape=(pltpu.SemaphoreType.DMA((n_copies,)), pltpu.SemaphoreType.DMA((n_copies,)),
                   *[pltpu.HBM(a.shape, a.dtype) for a in arrays], jax.ShapeDtypeStruct((8, LANE), F32)),
        in_specs=[HBM] * n + [ANY] * nd, out_specs=(SEM, SEM, *[HBM] * n, pl.BlockSpec(memory_space=pltpu.VMEM)),
        input_output_aliases={i: i + 2 for i in range(n)}, compiler_params=pltpu.CompilerParams(has_side_effects=EFFECT),
    )(*[pltpu.with_memory_space_constraint(a, pltpu.HBM) for a in arrays], *deps)
    return outs[0], outs[1], list(outs[2:2 + n]), outs[-1]


def _rdma_wait(send_sems, recv_sems, arrays, plan, after, *, name):
    n = len(arrays)

    def body(*refs):
        for cp in _plan_copies(plan, refs[:n], refs[n], refs[n + 1]):
            cp.wait_send()
            cp.wait_recv()

    return list(pl.pallas_call(
        body, name=name, out_shape=tuple(pltpu.HBM(a.shape, a.dtype) for a in arrays),
        in_specs=[HBM] * n + [SEM, SEM, ANY], out_specs=tuple([HBM] * n), input_output_aliases={i: i for i in range(n)},
        compiler_params=pltpu.CompilerParams(has_side_effects=EFFECT),
    )(*arrays, send_sems, recv_sems, after))


def _gather_plan(n):
    def plan(refs):
        me = _me()
        slot = 2 * me[0] + me[1]
        return [(refs[i].at[me[2]], refs[n + i].at[slot, me[2]], _flip(me, f)) for i in range(n) for f in CHIP_FLIPS]
    return plan


def _scatter_plan(n):
    def plan(refs):
        me = _me()
        out = []
        for i in range(n):
            for k, f in enumerate(CHIP_FLIPS):
                peer = _flip(me, f)
                out.append((refs[i].at[2 * peer[0] + peer[1]], refs[n + i].at[k], peer))
        return out
    return plan


def _sibling_plan(n, src_of):
    def plan(refs):
        me = _me()
        return [(src_of(refs[i], me[2]), refs[n + i], _flip(me, (0, 0, 1))) for i in range(n)]
    return plan


def _gather8_plan(n):
    def plan(refs):
        me = _me()
        slot = 4 * me[0] + 2 * me[1] + me[2]
        return [(refs[i], refs[n + i].at[slot], _flip(me, ((k >> 2) & 1, (k >> 1) & 1, k & 1)))
                for i in range(n) for k in range(1, 8)]
    return plan


def _pair_fill(lands, *, name):
    n = len(lands)

    def body(*refs):
        in_refs, (send_sems, recv_sems) = refs[:n], refs[2 * n:]
        me = _me()
        sib = _flip(me, (0, 0, 1))
        copies = []
        for i in range(n):
            for k, f in enumerate(CHIP_FLIPS):
                peer = _flip(me, f)
                slot = 2 * peer[0] + peer[1]
                mine, theirs = in_refs[i].at[slot, me[2]], in_refs[i].at[slot, 1 - me[2]]
                cp = pltpu.make_async_remote_copy(src_ref=mine, dst_ref=mine, send_sem=send_sems.at[3 * i + k],
                                                  recv_sem=recv_sems.at[3 * i + k], device_id=sib, device_id_type=MESH)
                cp.start()
                copies.append((cp, pltpu.make_async_remote_copy(
                    src_ref=mine, dst_ref=theirs, send_sem=send_sems.at[3 * i + k], recv_sem=recv_sems.at[3 * i + k],
                    device_id=sib, device_id_type=MESH)))
        for cp, arrival in copies:
            arrival.wait_recv()
            cp.wait_send()

    return list(pl.pallas_call(
        body, name=name, in_specs=[ANY] * n, out_specs=[ANY] * n,
        out_shape=[jax.ShapeDtypeStruct(a.shape, a.dtype) for a in lands], input_output_aliases={i: i for i in range(n)},
        scratch_shapes=[pltpu.SemaphoreType.DMA((3 * n,)), pltpu.SemaphoreType.DMA((3 * n,))])(*lands))


def _own_and_landed(lands, xs):
    chip = _chip()
    return [[jnp.where(chip == j, x, o.reshape(4, *x.shape)[j]) for j in range(4)] for o, x in zip(lands, xs)]


BIG = (("w_in", (D, IN_WIDTH // 4), 1), ("gla_w_o", (D // 4, D), 0), ("mla_w_uq", (MQR, MH * MQK // 4), 1),
       ("mla_w_ukv", (MKVR, MH * (MNOPE + MVD) // 4), 1), ("mla_w_o", (D // 4, D), 0), ("w_out", (D // 4, D), 0),
       ("mlp_w1", (D, DFF // 4), 1), ("mlp_w2", (DFF // 4, D), 0))
ADA_SHARD = (D, 6 * D // 4)
SMALL = (("b_ada", 6 * D), ("norm1_g", D), ("b_merge", 2 * D), ("gla_b_alpha", GH * GDK), ("gla_out_norm_g", GDV),
         ("mla_q_lat_g", MQR), ("mla_kv_lat_g", MKVR), ("mla_qn_g", MQK), ("mla_kn_g", MQK), ("norm2_g", D))


W_IN_SEGMENTS = ((0, 3072, OFF_Q), (3072, 3088, OFF_A), (3088, 3344, OFF_CQ), (3344, 3472, OFF_CKV),
                 (3472, 3504, OFF_KPE + MNOPE), (3504, 5552, OFF_MA))
W_IN_SPLIT = OFF_MA
SMALL_ROWS, SMALL_COLS = 32, 2 * D
W_ALPHA_ROW = 16
LOSS_ROW = 15
SMALL_RED = tuple((n, k) for n, k in SMALL if n != "b_ada")


def _pack_small(grads, d_w_alpha, loss_row, *, name):
    def body(*refs):
        g_refs, wa_ref, loss_ref, out_ref = refs[:-3], refs[-3], refs[-2], refs[-1]
        out_ref[...] = jnp.zeros_like(out_ref)
        for i, ((_, k), g_ref) in enumerate(zip(SMALL_RED, g_refs)):
            out_ref[i:i + 1, 0:k] = g_ref[...]
        out_ref[LOSS_ROW:LOSS_ROW + 1, 0:LANE] = loss_ref[...]
        out_ref[W_ALPHA_ROW:W_ALPHA_ROW + GLR, 0:GH * GDK] = wa_ref[...]

    return pl.pallas_call(body, name=name, out_shape=jax.ShapeDtypeStruct((SMALL_ROWS, SMALL_COLS), F32))(
        *grads, d_w_alpha, loss_row)


def _small_update(gathered, dmod_all, sel, wmv, *, name):
    names = [n for n, _ in SMALL] + ["gla_w_alpha"]
    n_par = len(names)

    def body(sel_ref, g_ref, dmod_ref, *refs):
        in_refs, out_refs, loss_ref, acc = refs[:3 * n_par], refs[3 * n_par:-2], refs[-2], refs[-1]
        total = g_ref[0]
        for j in range(1, 8):
            total = total + g_ref[j]
        acc[...] = total
        loss_ref[...] = acc[LOSS_ROW:LOSS_ROW + 1, 0:LANE]
        row = {n: i for i, (n, _) in enumerate(SMALL_RED)}
        for p, name_p in enumerate(names):
            w_ref, m_ref, v_ref = in_refs[3 * p:3 * p + 3]
            if name_p == "b_ada":
                gv = jnp.sum(dmod_ref[...], axis=0, keepdims=True)
            elif name_p == "gla_w_alpha":
                gv = jnp.zeros((GLR, GDK), F32)
                for j in range(4):
                    blk = acc[W_ALPHA_ROW:W_ALPHA_ROW + GLR, j * GDK:(j + 1) * GDK]
                    gv = gv + jnp.where(sel_ref[0] == j, blk, 0.0)
            else:
                gv = acc[row[name_p]:row[name_p] + 1, 0:w_ref.shape[1]]
            o = out_refs[4 * p:4 * p + 4]
            o[0][...] = gv
            o[1][...], o[2][...], o[3][...] = _adamw_update(w_ref[...], gv, m_ref[...], v_ref[...])

    flat = [a for t in wmv for a in t]
    out_shape = [jax.ShapeDtypeStruct(t[0].shape, F32) for t in wmv for _ in range(4)]
    out_shape.append(jax.ShapeDtypeStruct((1, LANE), F32))
    vmem = pl.BlockSpec(memory_space=pltpu.VMEM)
    outs = pl.pallas_call(
        body, name=name, out_shape=out_shape, in_specs=[pl.BlockSpec(memory_space=pltpu.SMEM), vmem, vmem] + [vmem] * len(flat),
        out_specs=[vmem] * len(out_shape), scratch_shapes=[pltpu.VMEM((SMALL_ROWS, SMALL_COLS), F32)],
    )(sel, gathered, dmod_all, *flat)
    return {n: tuple(outs[4 * p:4 * p + 4]) for p, n in enumerate(names)}, outs[-1][0, 0]


def _full_weights(gathered):
    w = {name: jnp.concatenate(gathered[name], axis=axis) for name, _, axis in BIG if name in gathered and name != "w_in"}
    if "w_in" in gathered:
        shards = gathered["w_in"]
        zeros = lambda n: [jnp.zeros((D, n), shards[0].dtype)]

        def cols(a, b):
            width = IN_WIDTH // 4
            return [shards[j][:, max(a, j * width) - j * width:min(b, (j + 1) * width) - j * width]
                    for j in range(4) if max(a, j * width) < min(b, (j + 1) * width)]

        parts = []
        for a, b, at in sorted(W_IN_SEGMENTS, key=lambda seg: seg[2]):
            have = sum(p.shape[1] for p in parts)
            parts += (zeros(at - have) if at > have else []) + cols(a, b)
        w["w_in"] = jnp.concatenate(parts + zeros(PW - sum(p.shape[1] for p in parts)), axis=1)
    if "mla_w_uq" in w:
        w["mla_w_uq"] = jnp.pad(w["mla_w_uq"].reshape(MQR, MH, MQK), ((0, 0), (0, 0), (0, LANE - MQK))).reshape(MQR, MH * LANE)
    if "mla_w_o" in w:
        w["mla_w_o"] = jnp.pad(w["mla_w_o"].reshape(MH, MVD, D), ((0, 0), (0, LANE - MVD), (0, 0))).reshape(MH * LANE, D)
    return w


def _grad_slots(g):
    g = dict(g)
    out = {}
    if "w_in" in g:
        g_lo, g_hi = g.pop("w_in")
        take = lambda at, lo, hi: g_lo[:, at + lo:at + hi] if at < W_IN_SPLIT else g_hi[:, at - W_IN_SPLIT + lo:at - W_IN_SPLIT + hi]
        width = IN_WIDTH // 4
        slots = []
        for j in range(4):
            lo, hi = j * width, (j + 1) * width
            slots.append(jnp.concatenate([take(at, max(lo, a) - a, min(hi, b) - a)
                                          for a, b, at in W_IN_SEGMENTS if max(lo, a) < min(hi, b)], axis=1))
        out["w_in"] = jnp.stack(slots).reshape(4, 2, D // 2, width)
    if "mla_w_uq" in g:
        g["mla_w_uq"] = g["mla_w_uq"].reshape(MQR, MH, LANE)[:, :, :MQK].reshape(MQR, MH * MQK)
    if "mla_w_o" in g:
        g["mla_w_o"] = g["mla_w_o"].reshape(MH, LANE, D)[:, :MVD].reshape(MH * MVD, D)
    for name, (rows, cols), axis in BIG:
        if name not in g:
            continue
        a = g[name]
        a = a.reshape(4, rows, cols) if axis == 0 else jnp.transpose(a.reshape(rows, 4, cols), (1, 0, 2))
        out[name] = a.reshape(4, 2, rows // 2, cols)
    return out


def _rope_tables(positions):
    freqs = ROPE_THETA ** (-jnp.arange(0, MROPE, 2, dtype=F32) / MROPE)
    lane = np.arange(LANE)
    in_rope = (lane >= MNOPE) & (lane < MQK)
    freq_lane = jnp.where(in_rope, freqs[(lane - MNOPE) % (MROPE // 2)], 0.0)
    sign = np.where(in_rope, np.where(lane < MNOPE + MROPE // 2, -1.0, 1.0), 0.0).astype(np.float32)
    ang = positions.astype(F32).reshape(-1, 1) * freq_lane[None, :]
    return jnp.cos(ang), jnp.sin(ang) * sign[None, :]


def _local_step(x, positions, mod, target, w, small, more_weights=None, on_grads=None):
    kept = {}
    if on_grads is None:
        on_grads = lambda tag, grads, after: kept.update(grads)
    bsz, s, _ = x.shape
    t = bsz * s
    tt = _tile(t, 1024)
    shift1, scale1, gate1, shift2, scale2, gate2 = [mod[:, None, i * D:(i + 1) * D] for i in range(6)]
    cos_t, sin_t = _rope_tables(positions)
    w_alpha_p = jnp.pad(small["gla_w_alpha"], ((0, LANE - GLR), (0, 0)))
    gq = jnp.pad(small["mla_qn_g"], ((0, 0), (0, LANE - MQK)))
    gk = jnp.pad(small["mla_kn_g"], ((0, 0), (0, LANE - MQK)))
    flat2 = lambda a: a.reshape(t, a.shape[-1])
    bsd = lambda a: a.reshape(bsz, s, a.shape[-1])

    h = _norm_mod(x, small["norm1_g"], scale1, shift1, name="norm1")
    if callable(w):
        w = w(h)
    proj = _mm(flat2(h), w["w_in"], name="proj", tn=1152, out_dtype=BF16)
    proj3 = bsd(proj)
    o, o_gated, states = _gla_fwd(proj3, w_alpha_p, small["gla_b_alpha"], small["gla_out_norm_g"], name="gla_fwd")
    if more_weights is not None:
        w = {**w, **more_weights(o_gated)}
    y_a = _mm(flat2(o_gated), w["gla_w_o"], name="gla_out", out_dtype=BF16)
    cq_n, ckv_n = _lat_norm(proj, small["mla_q_lat_g"], small["mla_kv_lat_g"], name="lat_norm")
    q_raw = _mm(cq_n, w["mla_w_uq"], name="mla_uq", out_dtype=BF16)
    kv = _mm(ckv_n, w["mla_w_ukv"], name="mla_ukv", out_dtype=BF16)
    qf, kf, vf = _qk_prep(q_raw, kv, proj, cos_t, sin_t, gq * Q_PRESCALE, gk, name="qk_prep")
    o_attn = _attn_fwd(bsd(qf), bsd(kf), bsd(vf), name="attn_fwd")
    y_b = _mm(flat2(o_attn), w["mla_w_o"], name="mla_out", out_dtype=BF16)
    mixed_in = _merge_fwd(proj3, small["b_merge"], bsd(y_a), bsd(y_b), name="merge_fwd")
    mixed = _mm(flat2(mixed_in), w["w_out"], name="w_out")
    x1, h2 = _resid_norm_mod(x, bsd(mixed), gate1, small["norm2_g"], scale2, shift2, name="norm2")

    def sqrelu(acc, ex, outs):
        r = jnp.maximum(acc, 0.0)
        outs[0][...] = (r * r).astype(BF16)

    r = _mm(flat2(h2), w["mlp_w1"], name="mlp1", epilogue=sqrelu, out_shape=jax.ShapeDtypeStruct((t, DFF), BF16),
            out_specs=_tile_spec(tt, 1024))
    ff = _mm(r, w["mlp_w2"], name="mlp2")
    dy, dff, dgate2, loss_part = _loss_head(x1, bsd(ff), gate2, target, name="loss_head")

    g = {}

    def relu2_bwd(acc, ex, outs):
        outs[0][...] = (acc * (2.0 * jnp.sqrt(ex[0][...].astype(F32)))).astype(BF16)

    dff2 = flat2(dff)
    da1 = _mm(dff2, w["mlp_w2"], tb=True, name="mlp2_dx", epilogue=relu2_bwd, extras=(r,),
              extra_specs=(_tile_spec(tt, 1024),), out_shape=jax.ShapeDtypeStruct((t, DFF), BF16),
              out_specs=_tile_spec(tt, 1024))
    g["mlp_w2"] = _mm(r, dff2, ta=True, name="mlp2_dw")
    dh2 = _mm(da1, w["mlp_w1"], tb=True, name="mlp1_dx")
    g["mlp_w1"] = _mm(flat2(h2), da1, ta=True, name="mlp1_dw")
    token = on_grads("mlp", {n: g.pop(n) for n in ("mlp_w2", "mlp_w1")}, dh2)
    if token is not None:
        gate1 = gate1 + token[0, 0]
    dx1, dscale2, dshift2, dg2, dgate1, dmixed = _norm_mod_bwd(
        bsd(dh2), x1, dy, small["norm2_g"], scale2, gate1, bsd(mixed), name="norm2_bwd")
    dmixed2 = flat2(dmixed)
    dmi = _mm(dmixed2, w["w_out"], tb=True, name="w_out_dx", out_dtype=BF16)
    g["w_out"] = _mm(flat2(mixed_in), dmixed2, ta=True, name="w_out_dw")
    dy_a, dy_b, dl_a, dl_b, db_a, db_b = _merge_bwd(bsd(dmi), proj3, small["b_merge"], bsd(y_a), bsd(y_b), name="merge_bwd")
    dy_a2, dy_b2 = flat2(dy_a), flat2(dy_b)
    dog = _mm(dy_a2, w["gla_w_o"], tb=True, name="gla_out_dx")
    g["gla_w_o"] = _mm(flat2(o_gated), dy_a2, ta=True, name="gla_out_dw")
    dq_g, dk_g, dv_g, dg_g, dlog, db_alpha, d_ong = _gla_bwd(
        bsd(dog), o, states, proj3, w_alpha_p, small["gla_b_alpha"], small["gla_out_norm_g"], name="gla_bwd")
    dlog2 = flat2(dlog)
    da_p = _mm(dlog2, w_alpha_p, tb=True, out_dtype=BF16, name="alpha_dx")
    d_w_alpha = _mm(proj[:, OFF_A:OFF_A + LANE], dlog2, ta=True, name="alpha_dw")[:GLR]
    do_attn = _mm(dy_b2, w["mla_w_o"], tb=True, out_dtype=BF16, name="mla_out_dx")
    g["mla_w_o"] = _mm(flat2(o_attn), dy_b2, ta=True, name="mla_out_dw")
    dqf, dkf, dvf = _attn_bwd(bsd(qf), bsd(kf), bsd(vf), bsd(do_attn), name="attn_bwd")
    dq_raw, dkv, dkpe, dgq, dgk = _qk_prep_bwd(flat2(dqf), flat2(dkf), flat2(dvf), q_raw, kv, proj, cos_t, sin_t, gq, gk,
                                                name="qk_prep_bwd")
    dcq_n = _mm(dq_raw, w["mla_w_uq"], tb=True, name="mla_uq_dx")
    g["mla_w_uq"] = _mm(cq_n, dq_raw, ta=True, name="mla_uq_dw")
    dckv_n = _mm(dkv, w["mla_w_ukv"], tb=True, name="mla_ukv_dx")
    g["mla_w_ukv"] = _mm(ckv_n, dkv, ta=True, name="mla_ukv_dw")
    token = on_grads("mix", {n: g.pop(n) for n in ("w_out", "gla_w_o", "mla_w_o", "mla_w_uq", "mla_w_ukv")}, dckv_n)
    q_lat_g = small["mla_q_lat_g"] if token is None else small["mla_q_lat_g"] + token[0:1, 0:1]
    dcq, dckv, dg_qlat, dg_kvlat = _lat_norm_bwd(dcq_n, dckv_n, proj, q_lat_g, small["mla_kv_lat_g"],
                                                  name="lat_norm_bwd")
    pieces = [(flat2(dq_g), OFF_Q), (flat2(dk_g), OFF_K), (flat2(dv_g), OFF_V), (flat2(dg_g), OFF_G),
              (flat2(dl_a), OFF_MA), (flat2(dl_b), OFF_MB), (dcq, OFF_CQ), (dckv, OFF_CKV), (da_p, OFF_A), (dkpe, OFF_KPE)]
    hb = flat2(h)
    g_w_in = (_pieces_dw(hb, [p for p, off in pieces if off < W_IN_SPLIT], name="proj_dw_a"),
              _pieces_dw(hb, [p for p, off in pieces if off >= W_IN_SPLIT], name="proj_dw_b"))
    token = on_grads("in", {"w_in": g_w_in}, g_w_in[1])
    after = jnp.zeros((8, LANE), F32) if token is None else token
    dh = _pieces_dx(pieces, w["w_in"], after, name="proj_dx")
    token = on_grads("dx", {}, dh)
    if token is not None:
        scale1 = scale1 + token[0, 0]
    grad_x, dscale1, dshift1, dg1 = _norm_mod_bwd(bsd(dh), x, dx1, small["norm1_g"], scale1, name="norm1_bwd")

    dmod = jnp.concatenate([dshift1, dscale1, dgate1, dshift2, dscale2, dgate2], axis=-1).reshape(bsz, 6 * D)
    gs = {"norm1_g": dg1, "b_merge": jnp.concatenate([db_a, db_b], axis=1), "gla_b_alpha": db_alpha,
          "gla_out_norm_g": d_ong, "mla_q_lat_g": dg_qlat, "mla_kv_lat_g": dg_kvlat, "mla_qn_g": dgq[:, :MQK],
          "mla_kn_g": dgk[:, :MQK], "norm2_g": dg2}
    return loss_part[0, 0], grad_x, dmod, {**kept, **g}, gs, d_w_alpha


def kernel(x, c, positions, w_ada, b_ada, norm1_g, w_in, b_merge, gla_w_alpha, gla_b_alpha, gla_out_norm_g, gla_w_o, mla_q_lat_g, mla_w_uq, mla_kv_lat_g, mla_w_ukv, mla_qn_g, mla_kn_g, mla_w_o, w_out, norm2_g, mlp_w1, mlp_w2, loss_target, m_w_ada, m_b_ada, m_norm1_g, m_w_in, m_b_merge, m_gla_w_alpha, m_gla_b_alpha, m_gla_out_norm_g, m_gla_w_o, m_mla_q_lat_g, m_mla_w_uq, m_mla_kv_lat_g, m_mla_w_ukv, m_mla_qn_g, m_mla_kn_g, m_mla_w_o, m_w_out, m_norm2_g, m_mlp_w1, m_mlp_w2, v_w_ada, v_b_ada, v_norm1_g, v_w_in, v_b_merge, v_gla_w_alpha, v_gla_b_alpha, v_gla_out_norm_g, v_gla_w_o, v_mla_q_lat_g, v_mla_w_uq, v_mla_kv_lat_g, v_mla_w_ukv, v_mla_qn_g, v_mla_kn_g, v_mla_w_o, v_w_out, v_norm2_g, v_mlp_w1, v_mlp_w2):
    args = dict(locals())
    names_big = [n for n, _, _ in BIG]
    names_small = [n for n, _ in SMALL]
    bsz = x.shape[0]
    ax, ay, ac = lax.axis_index("x"), lax.axis_index("y"), lax.axis_index("c")
    chip = 2 * ax + ay
    dev = 2 * chip + ac

    small = {n: args[n] for n in names_small}
    sel_c = jnp.reshape(ac, (1,)).astype(jnp.int32)
    sel_chip = jnp.reshape(chip, (1,)).astype(jnp.int32)
    c_all, w_alpha_all = _all_gather8([c, gla_w_alpha[0]], name="comm_c_alpha")
    small["gla_w_alpha"] = jnp.concatenate([w_alpha_all[2 * j] for j in range(4)], axis=1)
    c_all = c_all.reshape(8 * bsz, D)

    shards = {n: args[n][0].astype(BF16) for n in names_big}
    halves_of = lambda names: [shards[n].reshape(2, shards[n].shape[0] // 2, shards[n].shape[1]) for n in names]

    def gather_start(names, deps, tag):
        xs = halves_of(names)
        lands = [lax.empty((4, *xh.shape), BF16) for xh in xs]
        plan = _gather_plan(len(names))
        return names, plan, _rdma_start(xs + lands, 3 * len(names), plan, deps, name="comm_weights_start_" + tag)

    def gather_finish(started, after, tag):
        names, plan, sems = started
        arrs = _rdma_wait(sems[0], sems[1], sems[2], plan, after, name="comm_weights_wait_" + tag)
        filled = _pair_fill(arrs[len(names):], name="comm_weights_pair_" + tag)
        own = [a.reshape(shards[n].shape) for n, a in zip(names, arrs)]
        return _full_weights(dict(zip(names, _own_and_landed(filled, own))))


    def add_bias(acc, ex, outs):
        outs[0][...] = acc + ex[0][...]

    silu = lambda v: v * _sigmoid(v)
    b_ada_mine = lax.dynamic_slice(b_ada, (0, chip * ADA_SHARD[1]), (1, ADA_SHARD[1]))
    mod_part = _mm(c_all, w_ada[0], name="ada", tn=512, a_fn=silu, epilogue=add_bias, extras=(b_ada_mine,),
                   extra_specs=(pl.BlockSpec((1, 512), lambda i, j, k: (0, j)),),
                   out_shape=jax.ShapeDtypeStruct((8 * bsz, ADA_SHARD[1]), F32), out_specs=_tile_spec(8 * bsz, 512))
    mod_all = _all_gather8([mod_part], name="comm_mod")[0]
    mod_rows = lax.dynamic_slice(mod_all, (0, dev * bsz, 0), (8, bsz, ADA_SHARD[1]))
    mod = jnp.concatenate([mod_rows[2 * j] for j in range(4)], axis=1)
    first = gather_start(["w_in"], (mod,), "in")
    rest = gather_start([n for n in names_big if n != "w_in"], (mod, first[2][3]), "rest")
    mod = mod + rest[2][3][0, 0]
    w_in_after = lambda after: gather_finish(first, after, "in")
    more_weights = lambda after: gather_finish(rest, after, "rest")

    stage = {}

    def begin(tag, names, arrays, lands, n_copies, plan, what):
        stage[tag] = (names, plan, _rdma_start(arrays + lands, n_copies, plan, (), name=f"comm_{what}_start_{tag}"))
        return stage[tag][2][3]

    def landed(tag, after, what):
        names, plan, sems = stage[tag]
        arrs = _rdma_wait(sems[0], sems[1], sems[2], plan, after, name=f"comm_{what}_wait_{tag}")
        return names, arrs[:len(arrs) // 2], arrs[len(arrs) // 2:]

    def swap_start(tag, grads):
        names = list(grads)
        parts = [_grad_slots(grads)[n] for n in names]
        lands = [lax.empty((4, *p.shape[2:]), F32) for p in parts]
        return begin(tag, names, parts, lands, len(names), _sibling_plan(len(names), lambda r, c: r.at[:, 1 - c]), "pair_sum")

    def scatter_start(tag, after):
        names, parts, sib_halves = landed(tag, after, "pair_sum")
        pairs = [_pair_add(p, s, sel_c, name="pair_add_" + n) for n, p, s in zip(names, parts, sib_halves)]
        recvs = [lax.empty((3, *p.shape[1:]), BF16) for p in pairs]
        return begin(tag, names, pairs, recvs, 3 * len(names), _scatter_plan(len(names)), "scatter")

    def join_start(tag, after):
        names, pairs, recvs = landed(tag, after, "scatter")
        halves = [_chip_sum(p, r, sel_chip, name="chip_sum_" + n) for n, p, r in zip(names, pairs, recvs)]
        lands = [lax.empty(h.shape, F32) for h in halves]
        return begin(tag, names, halves, lands, len(names), _sibling_plan(len(names), lambda r, c: r), "pair_join")

    def reduce_step(tag, grads, after):
        if tag == "mlp":
            return swap_start("mlp", grads)
        if tag == "mix":
            return scatter_start("mlp", after) + swap_start("mix", grads)
        if tag == "in":
            return scatter_start("mix", after) + swap_start("in", grads)
        return scatter_start("in", after)

    loss_part, grad_x, dmod, g, gs, d_w_alpha = _local_step(x, positions, mod, loss_target, w_in_after, small,
                                                            more_weights, reduce_step)

    assert not g, list(g)
    gs_packed = _pack_small([gs[n] for n, _ in SMALL_RED], d_w_alpha, jnp.full((1, LANE), loss_part, F32),
                            name="pack_small")
    small_lands = [lax.empty((8, *a.shape), F32) for a in (dmod, gs_packed)]
    begin("small", ["dmod", "small"], [dmod, gs_packed], small_lands, 7 * 2, _gather8_plan(2), "gather8")

    res = {}

    def finish(tag, after):
        names, halves, theirs = landed(tag, after, "pair_join")
        for n, mine, other in zip(names, halves, theirs):
            if n == "w_in":
                south = ac == 0
                g_t = jnp.concatenate([jnp.where(south, mine, other), jnp.where(south, other, mine)], axis=0).T
                outs = _adamw(w_in[0].T, g_t, m_w_in[0].T, v_w_in[0].T, name="adamw_w_in", by_cols=True)
                res[n] = tuple(a.T for a in (g_t, *outs))
            else:
                res[n] = _adamw_halves(args[n][0], args["m_" + n][0], args["v_" + n][0], mine, other, sel_c,
                                       name="adamw_" + n)
        return res[names[-1]][1]

    join_start("mlp", grad_x)
    join_start("mix", grad_x)
    done = finish("mix", finish("mlp", grad_x))

    _, (dmod_own, gs_own), (dmod_all, gs_all) = landed("small", done, "gather8")
    dmod_all = lax.dynamic_update_slice(dmod_all, dmod_own[None], (dev, 0, 0)).reshape(8 * bsz, 6 * D)
    gs_all = lax.dynamic_update_slice(gs_all, gs_own[None], (dev, 0, 0))
    dmod_mine = lax.dynamic_slice(dmod_all, (0, chip * ADA_SHARD[1]), (8 * bsz, ADA_SHARD[1]))
    g_w_ada = _mm(c_all, dmod_mine, ta=True, a_fn=silu, name="ada_dw")
    wmv = [(args[n], args["m_" + n], args["v_" + n]) for n in names_small]
    wmv.append((gla_w_alpha[0], m_gla_w_alpha[0], v_gla_w_alpha[0]))
    res_small, loss_sum = _small_update(gs_all, dmod_all, sel_chip, wmv, name="small_update")
    res.update(res_small)
    loss = loss_sum * (0.5 / D)
    join_start("in", g_w_ada)
    res["w_ada"] = (g_w_ada, *_adamw(w_ada[0], g_w_ada, m_w_ada[0], v_w_ada[0], name="adamw_w_ada"))
    finish("in", res["w_ada"][1])

    order = ["w_ada", "b_ada", "norm1_g", "w_in", "b_merge", "gla_w_alpha", "gla_b_alpha", "gla_out_norm_g", "gla_w_o",
             "mla_q_lat_g", "mla_w_uq", "mla_kv_lat_g", "mla_w_ukv", "mla_qn_g", "mla_kn_g", "mla_w_o", "w_out",
             "norm2_g", "mlp_w1", "mlp_w2"]
    named = lambda k: [res[n][k].reshape(args[n].shape) for n in order]
    return (loss, grad_x, *named(0), *named(1), *named(2), *named(3))
```

```python
import jax
import jax.numpy as jnp
import numpy as np
from jax import lax
from jax.experimental import pallas as pl
from jax.experimental.pallas import tpu as pltpu

F32 = jnp.float32
BF16 = jnp.bfloat16
MESH = pl.DeviceIdType.MESH

D = 1024
CHUNK = 64
EPS = 1e-6
GH, GDK, GDV, GLR, GTAU = 4, 128, 256, 16, 16.0
MH, MQR, MKVR, MNOPE, MROPE, MVD = 16, 256, 128, 64, 32, 64
MQK = MNOPE + MROPE
DFF = 4 * D
ROPE_THETA = 10000.0
IN_WIDTH = 5552
LANE = 128
OFF_Q, OFF_K, OFF_V, OFF_G, OFF_MA, OFF_MB, OFF_CQ, OFF_CKV, OFF_A, OFF_KPE, PW = (
    0, 512, 1024, 2048, 3072, 4096, 5120, 5376, 5504, 5632, 5760)
ADAM_LR, ADAM_B1, ADAM_B2, ADAM_EPS, ADAM_WD, ADAM_STEP = 0.001, 0.9, 0.999, 1e-08, 0.01, 10
VMEM_LIMIT = 48 * 1024 * 1024


def _params(n_axes):
    return pltpu.CompilerParams(dimension_semantics=("arbitrary",) * n_axes, vmem_limit_bytes=VMEM_LIMIT)


def _tile(n, target):
    if n <= target:
        return n
    best = None
    for t in range(LANE, target + 1, LANE):
        if n % t == 0:
            best = t
    assert best is not None, (n, target)
    return best


def _sigmoid(x):
    return 1.0 / (1.0 + jnp.exp(-x))


MM_VMEM_BUDGET = 36 * 1024 * 1024


def _mm(a, b, *, name, ta=False, tb=False, out_dtype=F32, tm=1024, tn=1024, tk=4096,
        epilogue=None, extras=(), extra_specs=(), out_shape=None, out_specs=None, a_fn=None):
    if ta:
        kdim, m = a.shape
    else:
        m, kdim = a.shape
    if tb:
        n, k2 = b.shape
    else:
        k2, n = b.shape
    assert kdim == k2, (a.shape, b.shape)
    tm, tn, tk = _tile(m, tm), _tile(n, tn), _tile(kdim, tk)
    tiles = lambda rows: 2 * (rows * tk * a.dtype.itemsize + tk * tn * b.dtype.itemsize + rows * tn * 4) + rows * tn * 4
    while out_shape is None and tiles(tm) > MM_VMEM_BUDGET and tm % 256 == 0:
        tm //= 2
    nk = kdim // tk
    a_spec = pl.BlockSpec((tk, tm), lambda i, j, k: (k, i)) if ta else pl.BlockSpec((tm, tk), lambda i, j, k: (i, k))
    b_spec = pl.BlockSpec((tn, tk), lambda i, j, k: (j, k)) if tb else pl.BlockSpec((tk, tn), lambda i, j, k: (k, j))
    dims = (((0 if ta else 1,), (1 if tb else 0,)), ((), ()))
    ne = len(extras)
    if out_shape is None:
        out_shape = jax.ShapeDtypeStruct((m, n), out_dtype)
        out_specs = pl.BlockSpec((tm, tn), lambda i, j, k: (i, j))
    n_out = len(out_shape) if isinstance(out_shape, (list, tuple)) else 1
    in_place = epilogue is None and n_out == 1 and out_shape.dtype == F32
    scratch = [] if (nk == 1 or in_place) else [pltpu.VMEM((tm, tn), F32)]

    def body(a_ref, b_ref, *rest):
        ex, outs = rest[:ne], rest[ne:ne + n_out]
        av = a_ref[...] if a_fn is None else a_fn(a_ref[...])
        prod = lax.dot_general(av.astype(BF16), b_ref[...].astype(BF16), dims, preferred_element_type=F32)

        def finish(val):
            if epilogue is None:
                outs[0][...] = val.astype(outs[0].dtype)
            else:
                epilogue(val, ex, outs)

        if nk == 1:
            finish(prod)
            return
        k = pl.program_id(2)
        acc = outs[0] if in_place else rest[-1]

        @pl.when(k == 0)
        def _():
            acc[...] = prod

        @pl.when(k > 0)
        def _():
            acc[...] += prod

        if not in_place:
            @pl.when(k == nk - 1)
            def _():
                finish(acc[...])

    return pl.pallas_call(
        body, name=name, grid=(m // tm, n // tn, nk),
        in_specs=[a_spec, b_spec, *extra_specs], out_specs=out_specs, out_shape=out_shape,
        scratch_shapes=scratch, compiler_params=_params(3),
    )(a, b, *extras)


def _tile_spec(tm, tn):
    return pl.BlockSpec((tm, tn), lambda i, j, k: (i, j))


def _pieces_dx(pieces, w, after, *, name, tm=256):
    t = pieces[0][0].shape[0]
    tm = _tile(t, tm)
    npc = len(pieces)

    def body(*refs):
        p_refs, w_ref, out_ref = refs[:npc], refs[npc], refs[-1]
        acc = None
        for (arr, off), p_ref in zip(pieces, p_refs):
            part = lax.dot_general(p_ref[...].astype(BF16), w_ref[:, off:off + arr.shape[1]], _NT,
                                   preferred_element_type=F32)
            acc = part if acc is None else acc + part
        out_ref[...] = acc

    return pl.pallas_call(
        body, name=name, grid=(t // tm,),
        in_specs=[pl.BlockSpec((tm, arr.shape[1]), lambda i: (i, 0)) for arr, _ in pieces]
        + [pl.BlockSpec(w.shape, lambda i: (0, 0)), pl.BlockSpec((8, LANE), lambda i: (0, 0))],
        out_specs=pl.BlockSpec((tm, w.shape[0]), lambda i: (i, 0)),
        out_shape=jax.ShapeDtypeStruct((t, w.shape[0]), F32), compiler_params=_params(1),
    )(*[arr for arr, _ in pieces], w, after)


def _pieces_dw(h, pieces, *, name, tk=1024):
    t, d = h.shape
    tk = _tile(t, tk)
    widths = [p.shape[1] for p in pieces]
    starts = [sum(widths[:i]) for i in range(len(pieces))]

    def body(h_ref, *refs):
        p_refs, out_ref = refs[:-1], refs[-1]
        first = pl.program_id(0) == 0
        hv = h_ref[...]
        for p_ref, start, width in zip(p_refs, starts, widths):
            part = lax.dot_general(hv, p_ref[...].astype(BF16), _TN, preferred_element_type=F32)
            cols = slice(start, start + width)

            @pl.when(first)
            def _():
                out_ref[:, cols] = part

            @pl.when(jnp.logical_not(first))
            def _():
                out_ref[:, cols] += part

    return pl.pallas_call(
        body, name=name, grid=(t // tk,),
        in_specs=[pl.BlockSpec((tk, d), lambda k: (k, 0))] + [pl.BlockSpec((tk, wd), lambda k: (k, 0)) for wd in widths],
        out_specs=pl.BlockSpec((d, sum(widths)), lambda k: (0, 0)),
        out_shape=jax.ShapeDtypeStruct((d, sum(widths)), F32), compiler_params=_params(1),
    )(h, *pieces)


def _rms(x, g):
    r = lax.rsqrt(jnp.mean(x * x, axis=-1, keepdims=True) + EPS)
    return x * r, r


def _row_spec(ts, width, col=0):
    return pl.BlockSpec((None, ts, width), lambda b, i: (b, i, col))


def _vec_spec(width):
    return pl.BlockSpec((None, 1, width), lambda b, i: (b, 0, 0))


def _gain_spec(width):
    return pl.BlockSpec((1, width), lambda b, i: (0, 0))


def _norm_mod(x, g, scale, shift, *, name, ts=512):
    bsz, s, d = x.shape
    ts = min(ts, s)

    def body(x_ref, g_ref, sc_ref, sh_ref, h_ref):
        xh, _ = _rms(x_ref[...], None)
        h_ref[...] = ((xh * g_ref[...]) * (1.0 + sc_ref[...]) + sh_ref[...]).astype(BF16)

    return pl.pallas_call(
        body, name=name, grid=(bsz, s // ts),
        in_specs=[_row_spec(ts, d), _gain_spec(d), _vec_spec(d), _vec_spec(d)],
        out_specs=_row_spec(ts, d), out_shape=jax.ShapeDtypeStruct((bsz, s, d), BF16),
        compiler_params=_params(2),
    )(x, g, scale, shift)


def _resid_norm_mod(x, mixed, gate, g, scale, shift, *, name, ts=512):
    bsz, s, d = x.shape
    ts = min(ts, s)

    def body(x_ref, mx_ref, gt_ref, g_ref, sc_ref, sh_ref, x1_ref, h_ref):
        x1 = x_ref[...] + gt_ref[...] * mx_ref[...]
        x1_ref[...] = x1
        xh, _ = _rms(x1, None)
        h_ref[...] = ((xh * g_ref[...]) * (1.0 + sc_ref[...]) + sh_ref[...]).astype(BF16)

    return pl.pallas_call(
        body, name=name, grid=(bsz, s // ts),
        in_specs=[_row_spec(ts, d), _row_spec(ts, d), _vec_spec(d), _gain_spec(d), _vec_spec(d), _vec_spec(d)],
        out_specs=[_row_spec(ts, d), _row_spec(ts, d)],
        out_shape=[jax.ShapeDtypeStruct((bsz, s, d), F32), jax.ShapeDtypeStruct((bsz, s, d), BF16)],
        compiler_params=_params(2),
    )(x, mixed, gate, g, scale, shift)


def _norm_mod_bwd(dh, xin, resid, g, scale, gate=None, mixed=None, *, name, ts=512):
    bsz, s, d = xin.shape
    ts = min(ts, s)
    gated = gate is not None

    def body(*refs):
        if gated:
            dh_ref, x_ref, rs_ref, g_ref, sc_ref, gt_ref, mx_ref, dx_ref, dsc_ref, dsh_ref, dg_ref, dgt_ref, dmx_ref = refs
        else:
            dh_ref, x_ref, rs_ref, g_ref, sc_ref, dx_ref, dsc_ref, dsh_ref, dg_ref = refs
        b, i = pl.program_id(0), pl.program_id(1)

        @pl.when(i == 0)
        def _():
            dsc_ref[...] = jnp.zeros_like(dsc_ref)
            dsh_ref[...] = jnp.zeros_like(dsh_ref)
            if gated:
                dgt_ref[...] = jnp.zeros_like(dgt_ref)

        @pl.when((i == 0) & (b == 0))
        def _():
            dg_ref[...] = jnp.zeros_like(dg_ref)

        dh_v, gv = dh_ref[...], g_ref[...]
        xh, r = _rms(x_ref[...], None)
        dsc_ref[...] += jnp.sum(dh_v * (xh * gv), axis=0, keepdims=True)
        dsh_ref[...] += jnp.sum(dh_v, axis=0, keepdims=True)
        dn = dh_v * (1.0 + sc_ref[...])
        dg_ref[...] += jnp.sum(dn * xh, axis=0, keepdims=True)
        dxh = dn * gv
        dx = rs_ref[...] + r * (dxh - xh * jnp.mean(dxh * xh, axis=-1, keepdims=True))
        dx_ref[...] = dx
        if gated:
            dgt_ref[...] += jnp.sum(dx * mx_ref[...], axis=0, keepdims=True)
            dmx_ref[...] = (dx * gt_ref[...]).astype(BF16)

    ins = [dh, xin, resid, g, scale]
    in_specs = [_row_spec(ts, d), _row_spec(ts, d), _row_spec(ts, d), _gain_spec(d), _vec_spec(d)]
    out_specs = [_row_spec(ts, d), _vec_spec(d), _vec_spec(d), _gain_spec(d)]
    out_shape = [jax.ShapeDtypeStruct((bsz, s, d), F32), jax.ShapeDtypeStruct((bsz, 1, d), F32),
                 jax.ShapeDtypeStruct((bsz, 1, d), F32), jax.ShapeDtypeStruct((1, d), F32)]
    if gated:
        ins += [gate, mixed]
        in_specs += [_vec_spec(d), _row_spec(ts, d)]
        out_specs += [_vec_spec(d), _row_spec(ts, d)]
        out_shape += [jax.ShapeDtypeStruct((bsz, 1, d), F32), jax.ShapeDtypeStruct((bsz, s, d), BF16)]
    return pl.pallas_call(
        body, name=name, grid=(bsz, s // ts), in_specs=in_specs, out_specs=out_specs, out_shape=out_shape,
        compiler_params=_params(2),
    )(*ins)


def _loss_head(x1, ff, gate2, target, *, name, ts=512):
    bsz, s, d = x1.shape
    ts = min(ts, s)

    def body(x1_ref, ff_ref, gt_ref, t_ref, dy_ref, dff_ref, dgt_ref, loss_ref, acc):
        b, i = pl.program_id(0), pl.program_id(1)

        @pl.when(i == 0)
        def _():
            dgt_ref[...] = jnp.zeros_like(dgt_ref)

        @pl.when((i == 0) & (b == 0))
        def _():
            acc[...] = jnp.zeros_like(acc)

        ffv, gt = ff_ref[...], gt_ref[...]
        diff = (x1_ref[...] + gt * ffv) - t_ref[...]
        acc[...] += jnp.sum((diff * diff).reshape(ts // 8, 8, d), axis=0)
        dy = diff * (1.0 / d)
        dy_ref[...] = dy
        dgt_ref[...] += jnp.sum(dy * ffv, axis=0, keepdims=True)
        dff_ref[...] = (dy * gt).astype(BF16)

        @pl.when((i == pl.num_programs(1) - 1) & (b == pl.num_programs(0) - 1))
        def _():
            loss_ref[...] = jnp.full(loss_ref.shape, jnp.sum(acc[...]), F32)

    return pl.pallas_call(
        body, name=name, grid=(bsz, s // ts),
        in_specs=[_row_spec(ts, d), _row_spec(ts, d), _vec_spec(d), _row_spec(ts, d)],
        out_specs=[_row_spec(ts, d), _row_spec(ts, d), _vec_spec(d), pl.BlockSpec((8, LANE), lambda b, i: (0, 0))],
        out_shape=[jax.ShapeDtypeStruct((bsz, s, d), F32), jax.ShapeDtypeStruct((bsz, s, d), BF16),
                   jax.ShapeDtypeStruct((bsz, 1, d), F32), jax.ShapeDtypeStruct((8, LANE), F32)],
        scratch_shapes=[pltpu.VMEM((8, d), F32)], compiler_params=_params(2),
    )(x1, ff, gate2, target)


def _merge_fwd(proj, b_merge, y_a, y_b, *, name, ts=512):
    bsz, s, _ = proj.shape
    ts = min(ts, s)

    def body(la_ref, lb_ref, ba_ref, bb_ref, ya_ref, yb_ref, out_ref):
        ga = _sigmoid(la_ref[...] + ba_ref[...])
        gb = _sigmoid(lb_ref[...] + bb_ref[...])
        out_ref[...] = (ga * ya_ref[...] + gb * yb_ref[...]).astype(BF16)

    return pl.pallas_call(
        body, name=name, grid=(bsz, s // ts),
        in_specs=[_row_spec(ts, D, OFF_MA // D), _row_spec(ts, D, OFF_MB // D),
                  pl.BlockSpec((1, D), lambda b, i: (0, 0)), pl.BlockSpec((1, D), lambda b, i: (0, 1)),
                  _row_spec(ts, D), _row_spec(ts, D)],
        out_specs=_row_spec(ts, D), out_shape=jax.ShapeDtypeStruct((bsz, s, D), BF16),
        compiler_params=_params(2),
    )(proj, proj, b_merge, b_merge, y_a, y_b)


def _merge_bwd(dmi, proj, b_merge, y_a, y_b, *, name, ts=512):
    bsz, s, _ = proj.shape
    ts = min(ts, s)

    def body(d_ref, la_ref, lb_ref, ba_ref, bb_ref, ya_ref, yb_ref, dya_ref, dyb_ref, dla_ref, dlb_ref, dba_ref, dbb_ref):
        @pl.when((pl.program_id(0) == 0) & (pl.program_id(1) == 0))
        def _():
            dba_ref[...] = jnp.zeros_like(dba_ref)
            dbb_ref[...] = jnp.zeros_like(dbb_ref)

        dv = d_ref[...].astype(F32)
        ga = _sigmoid(la_ref[...] + ba_ref[...])
        gb = _sigmoid(lb_ref[...] + bb_ref[...])
        dya_ref[...] = (dv * ga).astype(BF16)
        dyb_ref[...] = (dv * gb).astype(BF16)
        dla = (dv * ya_ref[...]) * (ga * (1.0 - ga))
        dlb = (dv * yb_ref[...]) * (gb * (1.0 - gb))
        dla_ref[...] = dla.astype(BF16)
        dlb_ref[...] = dlb.astype(BF16)
        dba_ref[...] += jnp.sum(dla, axis=0, keepdims=True)
        dbb_ref[...] += jnp.sum(dlb, axis=0, keepdims=True)

    act = jax.ShapeDtypeStruct((bsz, s, D), BF16)
    return pl.pallas_call(
        body, name=name, grid=(bsz, s // ts),
        in_specs=[_row_spec(ts, D), _row_spec(ts, D, OFF_MA // D), _row_spec(ts, D, OFF_MB // D),
                  pl.BlockSpec((1, D), lambda b, i: (0, 0)), pl.BlockSpec((1, D), lambda b, i: (0, 1)),
                  _row_spec(ts, D), _row_spec(ts, D)],
        out_specs=[_row_spec(ts, D)] * 4 + [_gain_spec(D)] * 2,
        out_shape=[act, act, act, act, jax.ShapeDtypeStruct((1, D), F32), jax.ShapeDtypeStruct((1, D), F32)],
        compiler_params=_params(2),
    )(dmi, proj, proj, b_merge, b_merge, y_a, y_b)


def _tri(lower):
    r = lax.broadcasted_iota(jnp.int32, (CHUNK, CHUNK), 0)
    c = lax.broadcasted_iota(jnp.int32, (CHUNK, CHUNK), 1)
    return jnp.where((c <= r) if lower else (c >= r), 1.0, 0.0).astype(F32)


def _gla_logits(a_ref, wal_ref, bal_ref):
    logits = jnp.dot(a_ref[...].astype(BF16), wal_ref[...].astype(BF16), preferred_element_type=F32) + bal_ref[...]
    la = (jnp.minimum(logits, 0.0) - jnp.log(1.0 + jnp.exp(-jnp.abs(logits)))) * (1.0 / GTAU)
    return logits, la


def _chunk_cumsum(la_n, tri, precision=lax.Precision.HIGHEST):
    cum = jnp.dot(tri, la_n, preferred_element_type=F32, precision=precision)
    return cum, jnp.sum(la_n, axis=0, keepdims=True)


def _gla_specs(s, nc):
    def blk(width, off):
        return pl.BlockSpec((None, s, width), lambda h, b: (b, 0, off // width + h))

    proj_specs = [blk(GDK, OFF_Q), blk(GDK, OFF_K), blk(GDV, OFF_V), blk(GDV, OFF_G),
                  pl.BlockSpec((None, s, LANE), lambda h, b: (b, 0, OFF_A // LANE)),
                  pl.BlockSpec((LANE, GDK), lambda h, b: (0, h)), pl.BlockSpec((1, GDK), lambda h, b: (0, h)),
                  pl.BlockSpec((1, GDV), lambda h, b: (0, 0))]
    st_spec = pl.BlockSpec((None, None, nc, GDV, GDK), lambda h, b: (b, h, 0, 0, 0))
    return blk, proj_specs, st_spec


def _gla_fwd(proj, w_alpha_p, b_alpha, out_norm_g, *, name):
    bsz, s, _ = proj.shape
    nc = s // CHUNK
    scale = GDK ** -0.5

    rb = min(512, s)

    def body(q_ref, k_ref, v_ref, g_ref, a_ref, wal_ref, bal_ref, ong_ref, o_ref, og_ref, st_ref):
        _, la = _gla_logits(a_ref, wal_ref, bal_ref)
        tri = _tri(True)
        st = jnp.zeros((GDV, GDK), F32)
        for n in range(nc):
            rows = pl.ds(n * CHUNK, CHUNK)
            cum, cum_end = _chunk_cumsum(la[n * CHUNK:(n + 1) * CHUNK], tri, lax.Precision.HIGH)
            kd = k_ref[rows, :] * jnp.exp(cum_end - cum)
            ut = lax.dot_general(v_ref[rows, :].astype(BF16), kd.astype(BF16), _TN, preferred_element_type=F32)
            st = st * jnp.exp(cum_end) + ut
            st_ref[n] = st
            o_ref[rows, :] = lax.dot_general((q_ref[rows, :].astype(F32) * scale).astype(BF16), st.astype(BF16), _NT,
                                             preferred_element_type=F32)
        for j in range(0, s, rb):
            blk_rows = pl.ds(j, rb)
            oh, _ = _rms(o_ref[blk_rows, :], None)
            gv = g_ref[blk_rows, :].astype(F32)
            og_ref[blk_rows, :] = ((oh * ong_ref[...]) * (gv * _sigmoid(gv))).astype(BF16)

    blk, proj_specs, st_spec = _gla_specs(s, nc)
    return pl.pallas_call(
        body, name=name, grid=(GH, bsz), in_specs=proj_specs, out_specs=[blk(GDV, 0), blk(GDV, 0), st_spec],
        out_shape=[jax.ShapeDtypeStruct((bsz, s, GH * GDV), F32), jax.ShapeDtypeStruct((bsz, s, GH * GDV), BF16),
                   jax.ShapeDtypeStruct((bsz, GH, nc, GDV, GDK), F32)],
        compiler_params=_params(2),
    )(proj, proj, proj, proj, proj, w_alpha_p, b_alpha, out_norm_g)


def _gla_bwd(dog, o, states, proj, w_alpha_p, b_alpha, out_norm_g, *, name):
    bsz, s, _ = proj.shape
    nc = s // CHUNK
    scale = GDK ** -0.5

    def body(dog_ref, o_ref, st_ref, q_ref, k_ref, v_ref, g_ref, a_ref, wal_ref, bal_ref, ong_ref,
             dq_ref, dk_ref, dv_ref, dg_ref, dl_ref, dbal_ref, dong_ref, do_scr, dlog_scr):
        h, b = pl.program_id(0), pl.program_id(1)

        @pl.when(b == 0)
        def _():
            dbal_ref[...] = jnp.zeros_like(dbal_ref)

        @pl.when((b == 0) & (h == 0))
        def _():
            dong_ref[...] = jnp.zeros_like(dong_ref)

        ong = ong_ref[...]
        for j in range(0, s, rb):
            blk_rows = pl.ds(j, rb)
            gv, dogv = g_ref[blk_rows, :].astype(F32), dog_ref[blk_rows, :]
            sg = _sigmoid(gv)
            oh, r = _rms(o_ref[blk_rows, :], None)
            don = dogv * (gv * sg)
            dg_ref[blk_rows, :] = (dogv * (oh * ong) * (sg * (1.0 + gv * (1.0 - sg)))).astype(BF16)
            dong_ref[...] += jnp.sum(don * oh, axis=0, keepdims=True)
            doh = don * ong
            do_scr[blk_rows, :] = (r * (doh - oh * jnp.mean(doh * oh, axis=-1, keepdims=True))).astype(BF16)

        logits, la = _gla_logits(a_ref, wal_ref, bal_ref)
        tri_lo, tri_up = _tri(True), _tri(False)
        carry = jnp.zeros((GDV, GDK), F32)
        for n in range(nc - 1, -1, -1):
            rows = pl.ds(n * CHUNK, CHUNK)
            cum, cum_end = _chunk_cumsum(la[n * CHUNK:(n + 1) * CHUNK], tri_lo, lax.Precision.HIGH)
            decay = jnp.exp(cum_end)
            w = jnp.exp(cum_end - cum)
            kd = k_ref[rows, :] * w
            do_b = do_scr[rows, :]
            qs_b = (q_ref[rows, :].astype(F32) * scale).astype(BF16)
            dq_ref[rows, :] = (jnp.dot(do_b, st_ref[n].astype(BF16), preferred_element_type=F32) * scale).astype(BF16)
            dsn = lax.dot_general(do_b, qs_b, _TN, preferred_element_type=F32) + carry
            carry = dsn * decay
            dsn_b = dsn.astype(BF16)
            dv_ref[rows, :] = lax.dot_general(kd.astype(BF16), dsn_b, _NT, preferred_element_type=F32).astype(BF16)
            dkd = jnp.dot(v_ref[rows, :].astype(BF16), dsn_b, preferred_element_type=F32)
            dk_ref[rows, :] = (dkd * w).astype(BF16)
            e = dkd * kd
            dcum_end = jnp.sum(e, axis=0, keepdims=True)
            if n > 0:
                dcum_end += jnp.sum(dsn * st_ref[n - 1], axis=0, keepdims=True) * decay
            dlog_scr[rows, :] = dcum_end - jnp.dot(tri_up, e, preferred_element_type=F32,
                                                  precision=lax.Precision.HIGH)
        dlog = dlog_scr[...] * (1.0 / GTAU) * (1.0 - _sigmoid(logits))
        dl_ref[...] = dlog.astype(BF16)
        dbal_ref[...] += jnp.sum(dlog, axis=0, keepdims=True)

    rb = min(512, s)

    blk, proj_specs, st_spec = _gla_specs(s, nc)
    act = lambda wd: jax.ShapeDtypeStruct((bsz, s, wd), BF16)
    return pl.pallas_call(
        body, name=name, grid=(GH, bsz), in_specs=[blk(GDV, 0), blk(GDV, 0), st_spec, *proj_specs],
        out_specs=[blk(GDK, 0), blk(GDK, 0), blk(GDV, 0), blk(GDV, 0), blk(GDK, 0),
                   pl.BlockSpec((1, GDK), lambda h, b: (0, h)), pl.BlockSpec((1, GDV), lambda h, b: (0, 0))],
        out_shape=[act(GH * GDK), act(GH * GDK), act(GH * GDV), act(GH * GDV), act(GH * GDK),
                   jax.ShapeDtypeStruct((1, GH * GDK), F32), jax.ShapeDtypeStruct((1, GDV), F32)],
        scratch_shapes=[pltpu.VMEM((s, GDV), BF16), pltpu.VMEM((s, GDK), F32)], compiler_params=_params(2),
    )(dog, o, states, proj, proj, proj, proj, proj, w_alpha_p, b_alpha, out_norm_g)


def _lane():
    return lax.broadcasted_iota(jnp.int32, (1, LANE), 1)


def _swap_halves(x):
    lane = _lane()
    half = MROPE // 2
    lo = (lane >= MNOPE) & (lane < MNOPE + half)
    hi = (lane >= MNOPE + half) & (lane < MQK)
    return jnp.where(lo, pltpu.roll(x, LANE - half, 1), jnp.where(hi, pltpu.roll(x, half, 1), 0.0))


def _norm96(x, g):
    r = lax.rsqrt(jnp.sum(x * x, axis=-1, keepdims=True) * (1.0 / MQK) + EPS)
    return x * r, r


def _lat_norm(proj, q_lat_g, kv_lat_g, *, name, ts=512):
    t = proj.shape[0]
    ts = min(ts, t)

    def body(cq_ref, ckv_ref, gq_ref, gk_ref, oq_ref, ok_ref):
        xq, _ = _rms(cq_ref[...].astype(F32), None)
        oq_ref[...] = (xq * gq_ref[...]).astype(BF16)
        xk, _ = _rms(ckv_ref[...].astype(F32), None)
        ok_ref[...] = (xk * gk_ref[...]).astype(BF16)

    return pl.pallas_call(
        body, name=name, grid=(t // ts,),
        in_specs=[pl.BlockSpec((ts, MQR), lambda i: (i, OFF_CQ // MQR)), pl.BlockSpec((ts, MKVR), lambda i: (i, OFF_CKV // MKVR)),
                  pl.BlockSpec((1, MQR), lambda i: (0, 0)), pl.BlockSpec((1, MKVR), lambda i: (0, 0))],
        out_specs=[pl.BlockSpec((ts, MQR), lambda i: (i, 0)), pl.BlockSpec((ts, MKVR), lambda i: (i, 0))],
        out_shape=[jax.ShapeDtypeStruct((t, MQR), BF16), jax.ShapeDtypeStruct((t, MKVR), BF16)],
        compiler_params=_params(1),
    )(proj, proj, q_lat_g, kv_lat_g)


def _lat_norm_bwd(dcqn, dckvn, proj, q_lat_g, kv_lat_g, *, name, ts=512):
    t = proj.shape[0]
    ts = min(ts, t)

    def one(d_ref, x_ref, g_ref, dx_ref, dg_ref):
        xh, r = _rms(x_ref[...].astype(F32), None)
        dn = d_ref[...]
        dg_ref[...] += jnp.sum(dn * xh, axis=0, keepdims=True)
        dxh = dn * g_ref[...]
        dx_ref[...] = (r * (dxh - xh * jnp.mean(dxh * xh, axis=-1, keepdims=True))).astype(BF16)

    def body(dq_ref, dk_ref, cq_ref, ckv_ref, gq_ref, gk_ref, dxq_ref, dxk_ref, dgq_ref, dgk_ref):
        @pl.when(pl.program_id(0) == 0)
        def _():
            dgq_ref[...] = jnp.zeros_like(dgq_ref)
            dgk_ref[...] = jnp.zeros_like(dgk_ref)

        one(dq_ref, cq_ref, gq_ref, dxq_ref, dgq_ref)
        one(dk_ref, ckv_ref, gk_ref, dxk_ref, dgk_ref)

    return pl.pallas_call(
        body, name=name, grid=(t // ts,),
        in_specs=[pl.BlockSpec((ts, MQR), lambda i: (i, 0)), pl.BlockSpec((ts, MKVR), lambda i: (i, 0)),
                  pl.BlockSpec((ts, MQR), lambda i: (i, OFF_CQ // MQR)), pl.BlockSpec((ts, MKVR), lambda i: (i, OFF_CKV // MKVR)),
                  pl.BlockSpec((1, MQR), lambda i: (0, 0)), pl.BlockSpec((1, MKVR), lambda i: (0, 0))],
        out_specs=[pl.BlockSpec((ts, MQR), lambda i: (i, 0)), pl.BlockSpec((ts, MKVR), lambda i: (i, 0)),
                   pl.BlockSpec((1, MQR), lambda i: (0, 0)), pl.BlockSpec((1, MKVR), lambda i: (0, 0))],
        out_shape=[jax.ShapeDtypeStruct((t, MQR), BF16), jax.ShapeDtypeStruct((t, MKVR), BF16),
                   jax.ShapeDtypeStruct((1, MQR), F32), jax.ShapeDtypeStruct((1, MKVR), F32)],
        compiler_params=_params(1),
    )(dcqn, dckvn, proj, proj, q_lat_g, kv_lat_g)


def _qk_prep(q_raw, kv, proj, cos_t, sin_t, gq, gk, *, name, ts=4096):
    t = q_raw.shape[0]
    ts = min(ts, t)

    def body(q_ref, kv_ref, kpe_ref, c_ref, s_ref, gq_ref, gk_ref, qo_ref, ko_ref, vo_ref):
        cs, sn = c_ref[...], s_ref[...]
        nope = _lane() < MNOPE
        qn, _ = _norm96(q_ref[...].astype(F32), None)
        qn = qn * gq_ref[...]
        qo_ref[...] = (qn * cs + _swap_halves(qn) * sn).astype(BF16)
        kvv = kv_ref[...].astype(F32)
        kn, _ = _norm96(jnp.where(nope, kvv, kpe_ref[...].astype(F32)), None)
        kn = kn * gk_ref[...]
        ko_ref[...] = (kn * cs + _swap_halves(kn) * sn).astype(BF16)
        vo_ref[...] = jnp.where(nope, pltpu.roll(kvv, MNOPE, 1), 0.0).astype(BF16)

    hd = pl.BlockSpec((ts, LANE), lambda i, h: (i, h))
    shared = lambda col: pl.BlockSpec((ts, LANE), lambda i, h: (i, col))
    gain = pl.BlockSpec((1, LANE), lambda i, h: (0, 0))
    out = jax.ShapeDtypeStruct((t, MH * LANE), BF16)
    return pl.pallas_call(
        body, name=name, grid=(t // ts, MH),
        in_specs=[hd, hd, shared(OFF_KPE // LANE), shared(0), shared(0), gain, gain],
        out_specs=[hd, hd, hd], out_shape=[out, out, out], compiler_params=_params(2),
    )(q_raw, kv, proj, cos_t, sin_t, gq, gk)


def _qk_prep_bwd(dq, dk, dv, q_raw, kv, proj, cos_t, sin_t, gq, gk, *, name, ts=2048):
    t = q_raw.shape[0]
    ts = min(ts, t)

    def norm_bwd(dy, x, g, dg_ref):
        xh, r = _norm96(x, None)
        dg_ref[...] += jnp.sum(dy * xh, axis=0, keepdims=True)
        dxh = dy * g
        return r * (dxh - xh * (jnp.sum(dxh * xh, axis=-1, keepdims=True) * (1.0 / MQK)))

    def body(dq_ref, dk_ref, dv_ref, q_ref, kv_ref, kpe_ref, c_ref, s_ref, gq_ref, gk_ref,
             dqr_ref, dkv_ref, dkpe_ref, dgq_ref, dgk_ref):
        i, h = pl.program_id(0), pl.program_id(1)

        @pl.when(h == 0)
        def _():
            dkpe_ref[...] = jnp.zeros_like(dkpe_ref)

        @pl.when((h == 0) & (i == 0))
        def _():
            dgq_ref[...] = jnp.zeros_like(dgq_ref)
            dgk_ref[...] = jnp.zeros_like(dgk_ref)

        cs, sn = c_ref[...], s_ref[...]
        lane = _lane()
        nope = lane < MNOPE
        dqv = dq_ref[...]
        dqn = dqv * cs + _swap_halves(dqv * sn)
        dqr_ref[...] = norm_bwd(dqn, q_ref[...].astype(F32), gq_ref[...], dgq_ref).astype(BF16)
        dkv_ = dk_ref[...]
        dkn = dkv_ * cs + _swap_halves(dkv_ * sn)
        kvv = kv_ref[...].astype(F32)
        dkr = norm_bwd(dkn, jnp.where(nope, kvv, kpe_ref[...].astype(F32)), gk_ref[...], dgk_ref)
        dkv_ref[...] = jnp.where(nope, dkr, pltpu.roll(dv_ref[...], MNOPE, 1)).astype(BF16)
        dkpe_ref[...] += jnp.where((lane >= MNOPE) & (lane < MQK), dkr, 0.0)

    hd = pl.BlockSpec((ts, LANE), lambda i, h: (i, h))
    shared = lambda col: pl.BlockSpec((ts, LANE), lambda i, h: (i, col))
    gain = pl.BlockSpec((1, LANE), lambda i, h: (0, 0))
    out = jax.ShapeDtypeStruct((t, MH * LANE), BF16)
    return pl.pallas_call(
        body, name=name, grid=(t // ts, MH),
        in_specs=[hd, hd, hd, hd, hd, shared(OFF_KPE // LANE), shared(0), shared(0), gain, gain],
        out_specs=[hd, hd, shared(0), gain, gain],
        out_shape=[out, out, jax.ShapeDtypeStruct((t, LANE), F32), jax.ShapeDtypeStruct((1, LANE), F32),
                   jax.ShapeDtypeStruct((1, LANE), F32)],
        compiler_params=_params(2),
    )(dq, dk, dv, q_raw, kv, proj, cos_t, sin_t, gq, gk)


_NT = (((1,), (1,)), ((), ()))
_TN = (((0,), (0,)), ((), ()))


SOFTMAX_SCALE = MQK ** -0.5
Q_PRESCALE = SOFTMAX_SCALE * float(np.log2(np.e))


def _attn_weights(q, k_ref, lo, tq):
    row = lax.broadcasted_iota(jnp.int32, (tq, tq), 0) // CHUNK
    col = lax.broadcasted_iota(jnp.int32, (tq, tq), 1) // CHUNK
    sd = lax.dot_general(q, k_ref[pl.ds(lo, tq), :], _NT, preferred_element_type=F32)
    sd = jnp.where(col <= row, sd, -1e30)
    m = jnp.max(sd, axis=-1, keepdims=True)
    if lo:
        so = lax.dot_general(q, k_ref[pl.ds(0, lo), :], _NT, preferred_element_type=F32)
        m = jnp.maximum(m, jnp.max(so, axis=-1, keepdims=True))
        eo = jnp.exp2(so - m)
        ed = jnp.exp2(sd - m)
        return eo, ed, 1.0 / (jnp.sum(eo, axis=-1, keepdims=True) + jnp.sum(ed, axis=-1, keepdims=True))
    ed = jnp.exp2(sd - m)
    return None, ed, 1.0 / jnp.sum(ed, axis=-1, keepdims=True)


def _attn_fwd(q, k, v, *, name, tq=256):
    bsz, s, _ = q.shape
    tq = min(tq, s)

    def body(q_ref, k_ref, v_ref, o_ref):
        for i in range(s // tq):
            lo = i * tq
            eo, ed, inv = _attn_weights(q_ref[pl.ds(lo, tq), :], k_ref, lo, tq)
            o = jnp.dot(ed.astype(BF16), v_ref[pl.ds(lo, tq), :], preferred_element_type=F32)
            if lo:
                o += jnp.dot(eo.astype(BF16), v_ref[pl.ds(0, lo), :], preferred_element_type=F32)
            o_ref[pl.ds(lo, tq), :] = (o * inv).astype(BF16)

    spec = pl.BlockSpec((None, s, LANE), lambda b, h: (b, 0, h))
    return pl.pallas_call(
        body, name=name, grid=(bsz, MH), in_specs=[spec, spec, spec], out_specs=spec,
        out_shape=jax.ShapeDtypeStruct((bsz, s, MH * LANE), BF16), compiler_params=_params(2),
    )(q, k, v)


def _attn_bwd(q, k, v, do, *, name, tq=256):
    bsz, s, _ = q.shape
    tq = min(tq, s)

    def body(q_ref, k_ref, v_ref, do_ref, dq_ref, dk_ref, dv_ref):
        dk_ref[...] = jnp.zeros_like(dk_ref)
        dv_ref[...] = jnp.zeros_like(dv_ref)
        for i in range(s // tq):
            lo = i * tq
            here, before = pl.ds(lo, tq), pl.ds(0, lo)
            qv, dov = q_ref[here, :], do_ref[here, :]
            eo, ed, inv = _attn_weights(qv, k_ref, lo, tq)
            do_n = (dov.astype(F32) * inv).astype(BF16)
            dv_ref[here, :] += lax.dot_general(ed.astype(BF16), do_n, _TN, preferred_element_type=F32)
            dpd = lax.dot_general(dov, v_ref[here, :], _NT, preferred_element_type=F32)
            delta = jnp.sum(dpd * ed, axis=-1, keepdims=True)
            if lo:
                dv_ref[before, :] += lax.dot_general(eo.astype(BF16), do_n, _TN, preferred_element_type=F32)
                dpo = lax.dot_general(dov, v_ref[before, :], _NT, preferred_element_type=F32)
                delta += jnp.sum(dpo * eo, axis=-1, keepdims=True)
            delta = delta * inv
            r = inv * SOFTMAX_SCALE
            dsd = (ed * (dpd - delta) * r).astype(BF16)
            dq = jnp.dot(dsd, k_ref[here, :], preferred_element_type=F32)
            dk_ref[here, :] += lax.dot_general(dsd, qv, _TN, preferred_element_type=F32)
            if lo:
                dso = (eo * (dpo - delta) * r).astype(BF16)
                dq += jnp.dot(dso, k_ref[before, :], preferred_element_type=F32)
                dk_ref[before, :] += lax.dot_general(dso, qv, _TN, preferred_element_type=F32)
            dq_ref[here, :] = dq
        dk_ref[...] = dk_ref[...] * (1.0 / Q_PRESCALE)

    spec = pl.BlockSpec((None, s, LANE), lambda b, h: (b, 0, h))
    out = jax.ShapeDtypeStruct((bsz, s, MH * LANE), F32)
    return pl.pallas_call(
        body, name=name, grid=(bsz, MH), in_specs=[spec] * 4, out_specs=[spec] * 3, out_shape=[out, out, out],
        compiler_params=_params(2),
    )(q, k, v, do)


def _adamw(w, g, m, v, *, name, tr=256, by_cols=False):
    rows, cols = w.shape
    tr = _tile_rows(rows, tr)

    def body(w_ref, g_ref, m_ref, v_ref, d_ref, nm_ref, nv_ref):
        d_ref[...], nm_ref[...], nv_ref[...] = _adamw_update(w_ref[...], g_ref[...], m_ref[...], v_ref[...])

    spec = pl.BlockSpec((rows, LANE), lambda i: (0, i)) if by_cols else pl.BlockSpec((tr, cols), lambda i: (i, 0))
    out = jax.ShapeDtypeStruct((rows, cols), F32)
    return pl.pallas_call(body, name=name, grid=(cols // LANE if by_cols else rows // tr,), in_specs=[spec] * 4,
                          out_specs=[spec] * 3, out_shape=[out, out, out], compiler_params=_params(1))(w, g, m, v)


def _tile_rows(rows, target):
    if rows <= target:
        return rows
    best = 8
    for t in range(8, target + 1, 8):
        if rows % t == 0:
            best = t
    return best


def _adamw_update(w, g, m, v):
    nm = ADAM_B1 * m + (1.0 - ADAM_B1) * g
    nv = ADAM_B2 * v + (1.0 - ADAM_B2) * (g * g)
    m_hat = nm / (1.0 - ADAM_B1 ** ADAM_STEP)
    v_hat = nv / (1.0 - ADAM_B2 ** ADAM_STEP)
    return -ADAM_LR * (m_hat / (jnp.sqrt(v_hat) + ADAM_EPS) + ADAM_WD * w), nm, nv


def _adamw_halves(w, m, v, mine, theirs, sel, *, name, tr=256):
    rows, cols = w.shape
    tr = _tile_rows(rows // 2, tr)
    nh = rows // 2 // tr

    def body(sel_ref, w_ref, m_ref, v_ref, mine_ref, theirs_ref, g_ref, d_ref, nm_ref, nv_ref):
        lower = pl.program_id(0) < nh
        south = sel_ref[0] == 0
        gv = jnp.where(lower == south, mine_ref[...], theirs_ref[...])
        g_ref[...] = gv
        d_ref[...], nm_ref[...], nv_ref[...] = _adamw_update(w_ref[...], gv, m_ref[...], v_ref[...])

    full = pl.BlockSpec((tr, cols), lambda i, sel_ref: (i, 0))
    half = pl.BlockSpec((tr, cols), lambda i, sel_ref: (i % nh, 0))
    out = jax.ShapeDtypeStruct((rows, cols), F32)
    return pl.pallas_call(
        body, name=name, out_shape=[out] * 4, compiler_params=_params(1),
        grid_spec=pltpu.PrefetchScalarGridSpec(num_scalar_prefetch=1, grid=(rows // tr,),
                                               in_specs=[full, full, full, half, half], out_specs=[full] * 4),
    )(sel, w, m, v, mine, theirs)


def _pair_add(x, sib, sel, *, name, tr=256):
    n, _, rows, cols = x.shape
    tr = _tile_rows(rows, tr)

    def body(sel_ref, x_ref, s_ref, o_ref):
        o_ref[...] = (x_ref[...] + s_ref[...]).astype(BF16)

    spec = pl.BlockSpec((None, tr, cols), lambda j, i, sel_ref: (j, i, 0))
    return pl.pallas_call(
        body, name=name, out_shape=jax.ShapeDtypeStruct((n, rows, cols), BF16), compiler_params=_params(2),
        grid_spec=pltpu.PrefetchScalarGridSpec(
            num_scalar_prefetch=1, grid=(n, rows // tr),
            in_specs=[pl.BlockSpec((None, None, tr, cols), lambda j, i, sel_ref: (j, sel_ref[0], i, 0)), spec],
            out_specs=spec),
    )(sel, x, sib)


def _chip_sum(pair, recv, sel, *, name, tr=256):
    _, rows, cols = pair.shape
    tr = _tile_rows(rows, tr)

    def body(sel_ref, p_ref, r_ref, o_ref):
        acc = p_ref[...].astype(F32)
        for k in range(3):
            acc = acc + r_ref[k].astype(F32)
        o_ref[...] = acc

    return pl.pallas_call(
        body, name=name, out_shape=jax.ShapeDtypeStruct((rows, cols), F32), compiler_params=_params(1),
        grid_spec=pltpu.PrefetchScalarGridSpec(
            num_scalar_prefetch=1, grid=(rows // tr,),
            in_specs=[pl.BlockSpec((None, tr, cols), lambda i, sel_ref: (sel_ref[0], i, 0)),
                      pl.BlockSpec((3, tr, cols), lambda i, sel_ref: (0, i, 0))],
            out_specs=pl.BlockSpec((tr, cols), lambda i, sel_ref: (i, 0))),
    )(sel, pair, recv)


def _me():
    return lax.axis_index("x"), lax.axis_index("y"), lax.axis_index("c")


def _flip(pos, bits):
    x, y, c = pos
    return (x ^ bits[0] if bits[0] else x, y ^ bits[1] if bits[1] else y, c ^ bits[2] if bits[2] else c)


ANY = pl.BlockSpec(memory_space=pl.ANY)


def _all_gather8(xs, *, name):
    n = len(xs)

    def body(*refs):
        x_refs, out_refs, (send_sems, recv_sems, local_sems) = refs[:n], refs[n:2 * n], refs[2 * n:]
        me = _me()
        sib = _flip(me, (0, 0, 1))
        slot = lambda p: 4 * p[0] + 2 * p[1] + p[2]
        chips = [_flip(me, f) for f in CHIP_FLIPS]

        def copy(i, k, src, block, to):
            return pltpu.make_async_remote_copy(src_ref=src, dst_ref=out_refs[i].at[slot(block)],
                                                send_sem=send_sems.at[7 * i + k], recv_sem=recv_sems.at[7 * i + k],
                                                device_id=to, device_id_type=MESH)

        locals_, sends = [], []
        for i in range(n):
            mine = pltpu.make_async_copy(x_refs[i], out_refs[i].at[slot(me)], local_sems.at[i])
            mine.start()
            locals_.append(mine)
            sends += [copy(i, 0, x_refs[i], me, sib)] + [copy(i, 1 + j, x_refs[i], me, p) for j, p in enumerate(chips)]
        for cp in sends:
            cp.start()
        for i in range(n):
            for j, p in enumerate(chips):
                copy(i, 1 + j, x_refs[i], p, me).wait_recv()
                passed = copy(i, 4 + j, out_refs[i].at[slot(p)], p, sib)
                passed.start()
                sends.append(passed)
        for i in range(n):
            copy(i, 0, x_refs[i], sib, me).wait_recv()
            for j, p in enumerate(chips):
                copy(i, 4 + j, x_refs[i], _flip(p, (0, 0, 1)), me).wait_recv()
        for cp in sends:
            cp.wait_send()
        for mine in locals_:
            mine.wait()

    outs = pl.pallas_call(
        body, name=name, in_specs=[ANY] * n, out_specs=[ANY] * n,
        out_shape=[jax.ShapeDtypeStruct((8, *x.shape), x.dtype) for x in xs],
        scratch_shapes=[pltpu.SemaphoreType.DMA((7 * n,)), pltpu.SemaphoreType.DMA((7 * n,)),
                        pltpu.SemaphoreType.DMA((n,))])(*xs)
    return list(outs)


CHIP_FLIPS = [(1, 0, 0), (0, 1, 0), (1, 1, 0)]


def _chip():
    return 2 * lax.axis_index("x") + lax.axis_index("y")


HBM = pl.BlockSpec(memory_space=pltpu.HBM)
SEM = pl.BlockSpec(memory_space=pltpu.SEMAPHORE)
EFFECT = pltpu.SideEffectType.DATAFLOW_SIDE_EFFECTING


def _plan_copies(plan, refs, send_sems, recv_sems):
    return [pltpu.make_async_remote_copy(src_ref=src, dst_ref=dst, send_sem=send_sems.at[k], recv_sem=recv_sems.at[k],
                                         device_id=to, device_id_type=MESH) for k, (src, dst, to) in enumerate(plan(refs))]


def _rdma_start(arrays, n_copies, plan, deps, *, name):
    n, nd = len(arrays), len(deps)

    def body(*refs):
        for cp in _plan_copies(plan, refs[:n], refs[n + nd], refs[n + nd + 1]):
            cp.start()
        refs[-1][...] = jnp.zeros_like(refs[-1])

    outs = pl.pallas_call(
        body, name=name,
        out_shape=(pltpu.SemaphoreType.DMA((n_copies,)), pltpu.SemaphoreType.DMA((n_copies,)),
                   *[pltpu.HBM(a.shape, a.dtype) for a in arrays], jax.ShapeDtypeStruct((8, LANE), F32)),
        in_specs=[HBM] * n + [ANY] * nd, out_specs=(SEM, SEM, *[HBM] * n, pl.BlockSpec(memory_space=pltpu.VMEM)),
        input_output_aliases={i: i + 2 for i in range(n)}, compiler_params=pltpu.CompilerParams(has_side_effects=EFFECT),
    )(*[pltpu.with_memory_space_constraint(a, pltpu.HBM) for a in arrays], *deps)
    return outs[0], outs[1], list(outs[2:2 + n]), outs[-1]


def _rdma_wait(send_sems, recv_sems, arrays, plan, after, *, name):
    n = len(arrays)

    def body(*refs):
        for cp in _plan_copies(plan, refs[:n], refs[n], refs[n + 1]):
            cp.wait_send()
            cp.wait_recv()

    return list(pl.pallas_call(
        body, name=name, out_shape=tuple(pltpu.HBM(a.shape, a.dtype) for a in arrays),
        in_specs=[HBM] * n + [SEM, SEM, ANY], out_specs=tuple([HBM] * n), input_output_aliases={i: i for i in range(n)},
        compiler_params=pltpu.CompilerParams(has_side_effects=EFFECT),
    )(*arrays, send_sems, recv_sems, after))


def _gather_plan(n):
    def plan(refs):
        me = _me()
        slot = 2 * me[0] + me[1]
        return [(refs[i].at[me[2]], refs[n + i].at[slot, me[2]], _flip(me, f)) for i in range(n) for f in CHIP_FLIPS]
    return plan


def _scatter_plan(n):
    def plan(refs):
        me = _me()
        out = []
        for i in range(n):
            for k, f in enumerate(CHIP_FLIPS):
                peer = _flip(me, f)
                out.append((refs[i].at[2 * peer[0] + peer[1]], refs[n + i].at[k], peer))
        return out
    return plan


def _sibling_plan(n, src_of):
    def plan(refs):
        me = _me()
        return [(src_of(refs[i], me[2]), refs[n + i], _flip(me, (0, 0, 1))) for i in range(n)]
    return plan


def _gather8_plan(n):
    def plan(refs):
        me = _me()
        slot = 4 * me[0] + 2 * me[1] + me[2]
        return [(refs[i], refs[n + i].at[slot], _flip(me, ((k >> 2) & 1, (k >> 1) & 1, k & 1)))
                for i in range(n) for k in range(1, 8)]
    return plan


def _pair_fill(lands, *, name):
    n = len(lands)

    def body(*refs):
        in_refs, (send_sems, recv_sems) = refs[:n], refs[2 * n:]
        me = _me()
        sib = _flip(me, (0, 0, 1))
        copies = []
        for i in range(n):
            for k, f in enumerate(CHIP_FLIPS):
                peer = _flip(me, f)
                slot = 2 * peer[0] + peer[1]
                mine, theirs = in_refs[i].at[slot, me[2]], in_refs[i].at[slot, 1 - me[2]]
                cp = pltpu.make_async_remote_copy(src_ref=mine, dst_ref=mine, send_sem=send_sems.at[3 * i + k],
                                                  recv_sem=recv_sems.at[3 * i + k], device_id=sib, device_id_type=MESH)
                cp.start()
                copies.append((cp, pltpu.make_async_remote_copy(
                    src_ref=mine, dst_ref=theirs, send_sem=send_sems.at[3 * i + k], recv_sem=recv_sems.at[3 * i + k],
                    device_id=sib, device_id_type=MESH)))
        for cp, arrival in copies:
            arrival.wait_recv()
            cp.wait_send()

    return list(pl.pallas_call(
        body, name=name, in_specs=[ANY] * n, out_specs=[ANY] * n,
        out_shape=[jax.ShapeDtypeStruct(a.shape, a.dtype) for a in lands], input_output_aliases={i: i for i in range(n)},
        scratch_shapes=[pltpu.SemaphoreType.DMA((3 * n,)), pltpu.SemaphoreType.DMA((3 * n,))])(*lands))


def _own_and_landed(lands, xs):
    chip = _chip()
    return [[jnp.where(chip == j, x, o.reshape(4, *x.shape)[j]) for j in range(4)] for o, x in zip(lands, xs)]


BIG = (("w_in", (D, IN_WIDTH // 4), 1), ("gla_w_o", (D // 4, D), 0), ("mla_w_uq", (MQR, MH * MQK // 4), 1),
       ("mla_w_ukv", (MKVR, MH * (MNOPE + MVD) // 4), 1), ("mla_w_o", (D // 4, D), 0), ("w_out", (D // 4, D), 0),
       ("mlp_w1", (D, DFF // 4), 1), ("mlp_w2", (DFF // 4, D), 0))
ADA_SHARD = (D, 6 * D // 4)
SMALL = (("b_ada", 6 * D), ("norm1_g", D), ("b_merge", 2 * D), ("gla_b_alpha", GH * GDK), ("gla_out_norm_g", GDV),
         ("mla_q_lat_g", MQR), ("mla_kv_lat_g", MKVR), ("mla_qn_g", MQK), ("mla_kn_g", MQK), ("norm2_g", D))


W_IN_SEGMENTS = ((0, 3072, OFF_Q), (3072, 3088, OFF_A), (3088, 3344, OFF_CQ), (3344, 3472, OFF_CKV),
                 (3472, 3504, OFF_KPE + MNOPE), (3504, 5552, OFF_MA))
W_IN_SPLIT = OFF_MA
SMALL_ROWS, SMALL_COLS = 32, 2 * D
W_ALPHA_ROW = 16
LOSS_ROW = 15
SMALL_RED = tuple((n, k) for n, k in SMALL if n != "b_ada")


def _pack_small(grads, d_w_alpha, loss_row, *, name):
    def body(*refs):
        g_refs, wa_ref, loss_ref, out_ref = refs[:-3], refs[-3], refs[-2], refs[-1]
        out_ref[...] = jnp.zeros_like(out_ref)
        for i, ((_, k), g_ref) in enumerate(zip(SMALL_RED, g_refs)):
            out_ref[i:i + 1, 0:k] = g_ref[...]
        out_ref[LOSS_ROW:LOSS_ROW + 1, 0:LANE] = loss_ref[...]
        out_ref[W_ALPHA_ROW:W_ALPHA_ROW + GLR, 0:GH * GDK] = wa_ref[...]

    return pl.pallas_call(body, name=name, out_shape=jax.ShapeDtypeStruct((SMALL_ROWS, SMALL_COLS), F32))(
        *grads, d_w_alpha, loss_row)


def _small_update(gathered, dmod_all, sel, wmv, *, name):
    names = [n for n, _ in SMALL] + ["gla_w_alpha"]
    n_par = len(names)

    def body(sel_ref, g_ref, dmod_ref, *refs):
        in_refs, out_refs, loss_ref, acc = refs[:3 * n_par], refs[3 * n_par:-2], refs[-2], refs[-1]
        total = g_ref[0]
        for j in range(1, 8):
            total = total + g_ref[j]
        acc[...] = total
        loss_ref[...] = acc[LOSS_ROW:LOSS_ROW + 1, 0:LANE]
        row = {n: i for i, (n, _) in enumerate(SMALL_RED)}
        for p, name_p in enumerate(names):
            w_ref, m_ref, v_ref = in_refs[3 * p:3 * p + 3]
            if name_p == "b_ada":
                gv = jnp.sum(dmod_ref[...], axis=0, keepdims=True)
            elif name_p == "gla_w_alpha":
                gv = jnp.zeros((GLR, GDK), F32)
                for j in range(4):
                    blk = acc[W_ALPHA_ROW:W_ALPHA_ROW + GLR, j * GDK:(j + 1) * GDK]
                    gv = gv + jnp.where(sel_ref[0] == j, blk, 0.0)
            else:
                gv = acc[row[name_p]:row[name_p] + 1, 0:w_ref.shape[1]]
            o = out_refs[4 * p:4 * p + 4]
            o[0][...] = gv
            o[1][...], o[2][...], o[3][...] = _adamw_update(w_ref[...], gv, m_ref[...], v_ref[...])

    flat = [a for t in wmv for a in t]
    out_shape = [jax.ShapeDtypeStruct(t[0].shape, F32) for t in wmv for _ in range(4)]
    out_shape.append(jax.ShapeDtypeStruct((1, LANE), F32))
    vmem = pl.BlockSpec(memory_space=pltpu.VMEM)
    outs = pl.pallas_call(
        body, name=name, out_shape=out_shape, in_specs=[pl.BlockSpec(memory_space=pltpu.SMEM), vmem, vmem] + [vmem] * len(flat),
        out_specs=[vmem] * len(out_shape), scratch_shapes=[pltpu.VMEM((SMALL_ROWS, SMALL_COLS), F32)],
    )(sel, gathered, dmod_all, *flat)
    return {n: tuple(outs[4 * p:4 * p + 4]) for p, n in enumerate(names)}, outs[-1][0, 0]


def _full_weights(gathered):
    w = {name: jnp.concatenate(gathered[name], axis=axis) for name, _, axis in BIG if name in gathered and name != "w_in"}
    if "w_in" in gathered:
        shards = gathered["w_in"]
        zeros = lambda n: [jnp.zeros((D, n), shards[0].dtype)]

        def cols(a, b):
            width = IN_WIDTH // 4
            return [shards[j][:, max(a, j * width) - j * width:min(b, (j + 1) * width) - j * width]
                    for j in range(4) if max(a, j * width) < min(b, (j + 1) * width)]

        parts = []
        for a, b, at in sorted(W_IN_SEGMENTS, key=lambda seg: seg[2]):
            have = sum(p.shape[1] for p in parts)
            parts += (zeros(at - have) if at > have else []) + cols(a, b)
        w["w_in"] = jnp.concatenate(parts + zeros(PW - sum(p.shape[1] for p in parts)), axis=1)
    if "mla_w_uq" in w:
        w["mla_w_uq"] = jnp.pad(w["mla_w_uq"].reshape(MQR, MH, MQK), ((0, 0), (0, 0), (0, LANE - MQK))).reshape(MQR, MH * LANE)
    if "mla_w_o" in w:
        w["mla_w_o"] = jnp.pad(w["mla_w_o"].reshape(MH, MVD, D), ((0, 0), (0, LANE - MVD), (0, 0))).reshape(MH * LANE, D)
    return w


def _grad_slots(g):
    g = dict(g)
    out = {}
    if "w_in" in g:
        g_lo, g_hi = g.pop("w_in")
        take = lambda at, lo, hi: g_lo[:, at + lo:at + hi] if at < W_IN_SPLIT else g_hi[:, at - W_IN_SPLIT + lo:at - W_IN_SPLIT + hi]
        width = IN_WIDTH // 4
        slots = []
        for j in range(4):
            lo, hi = j * width, (j + 1) * width
            slots.append(jnp.concatenate([take(at, max(lo, a) - a, min(hi, b) - a)
                                          for a, b, at in W_IN_SEGMENTS if max(lo, a) < min(hi, b)], axis=1))
        out["w_in"] = jnp.stack(slots).reshape(4, 2, D // 2, width)
    if "mla_w_uq" in g:
        g["mla_w_uq"] = g["mla_w_uq"].reshape(MQR, MH, LANE)[:, :, :MQK].reshape(MQR, MH * MQK)
    if "mla_w_o" in g:
        g["mla_w_o"] = g["mla_w_o"].reshape(MH, LANE, D)[:, :MVD].reshape(MH * MVD, D)
    for name, (rows, cols), axis in BIG:
        if name not in g:
            continue
        a = g[name]
        a = a.reshape(4, rows, cols) if axis == 0 else jnp.transpose(a.reshape(rows, 4, cols), (1, 0, 2))
        out[name] = a.reshape(4, 2, rows // 2, cols)
    return out


def _rope_tables(positions):
    freqs = ROPE_THETA ** (-jnp.arange(0, MROPE, 2, dtype=F32) / MROPE)
    lane = np.arange(LANE)
    in_rope = (lane >= MNOPE) & (lane < MQK)
    freq_lane = jnp.where(in_rope, freqs[(lane - MNOPE) % (MROPE // 2)], 0.0)
    sign = np.where(in_rope, np.where(lane < MNOPE + MROPE // 2, -1.0, 1.0), 0.0).astype(np.float32)
    ang = positions.astype(F32).reshape(-1, 1) * freq_lane[None, :]
    return jnp.cos(ang), jnp.sin(ang) * sign[None, :]


def _local_step(x, positions, mod, target, w, small, more_weights=None, on_grads=None):
    kept = {}
    if on_grads is None:
        on_grads = lambda tag, grads, after: kept.update(grads)
    bsz, s, _ = x.shape
    t = bsz * s
    tt = _tile(t, 1024)
    shift1, scale1, gate1, shift2, scale2, gate2 = [mod[:, None, i * D:(i + 1) * D] for i in range(6)]
    cos_t, sin_t = _rope_tables(positions)
    w_alpha_p = jnp.pad(small["gla_w_alpha"], ((0, LANE - GLR), (0, 0)))
    gq = jnp.pad(small["mla_qn_g"], ((0, 0), (0, LANE - MQK)))
    gk = jnp.pad(small["mla_kn_g"], ((0, 0), (0, LANE - MQK)))
    flat2 = lambda a: a.reshape(t, a.shape[-1])
    bsd = lambda a: a.reshape(bsz, s, a.shape[-1])

    h = _norm_mod(x, small["norm1_g"], scale1, shift1, name="norm1")
    if callable(w):
        w = w(h)
    proj = _mm(flat2(h), w["w_in"], name="proj", tn=1152, out_dtype=BF16)
    proj3 = bsd(proj)
    o, o_gated, states = _gla_fwd(proj3, w_alpha_p, small["gla_b_alpha"], small["gla_out_norm_g"], name="gla_fwd")
    if more_weights is not None:
        w = {**w, **more_weights(o_gated)}
    y_a = _mm(flat2(o_gated), w["gla_w_o"], name="gla_out", out_dtype=BF16)
    cq_n, ckv_n = _lat_norm(proj, small["mla_q_lat_g"], small["mla_kv_lat_g"], name="lat_norm")
    q_raw = _mm(cq_n, w["mla_w_uq"], name="mla_uq", out_dtype=BF16)
    kv = _mm(ckv_n, w["mla_w_ukv"], name="mla_ukv", out_dtype=BF16)
    qf, kf, vf = _qk_prep(q_raw, kv, proj, cos_t, sin_t, gq * Q_PRESCALE, gk, name="qk_prep")
    o_attn = _attn_fwd(bsd(qf), bsd(kf), bsd(vf), name="attn_fwd")
    y_b = _mm(flat2(o_attn), w["mla_w_o"], name="mla_out", out_dtype=BF16)
    mixed_in = _merge_fwd(proj3, small["b_merge"], bsd(y_a), bsd(y_b), name="merge_fwd")
    mixed = _mm(flat2(mixed_in), w["w_out"], name="w_out")
    x1, h2 = _resid_norm_mod(x, bsd(mixed), gate1, small["norm2_g"], scale2, shift2, name="norm2")

    def sqrelu(acc, ex, outs):
        r = jnp.maximum(acc, 0.0)
        outs[0][...] = (r * r).astype(BF16)

    r = _mm(flat2(h2), w["mlp_w1"], name="mlp1", epilogue=sqrelu, out_shape=jax.ShapeDtypeStruct((t, DFF), BF16),
            out_specs=_tile_spec(tt, 1024))
    ff = _mm(r, w["mlp_w2"], name="mlp2")
    dy, dff, dgate2, loss_part = _loss_head(x1, bsd(ff), gate2, target, name="loss_head")

    g = {}

    def relu2_bwd(acc, ex, outs):
        outs[0][...] = (acc * (2.0 * jnp.sqrt(ex[0][...].astype(F32)))).astype(BF16)

    dff2 = flat2(dff)
    da1 = _mm(dff2, w["mlp_w2"], tb=True, name="mlp2_dx", epilogue=relu2_bwd, extras=(r,),
              extra_specs=(_tile_spec(tt, 1024),), out_shape=jax.ShapeDtypeStruct((t, DFF), BF16),
              out_specs=_tile_spec(tt, 1024))
    g["mlp_w2"] = _mm(r, dff2, ta=True, name="mlp2_dw")
    dh2 = _mm(da1, w["mlp_w1"], tb=True, name="mlp1_dx")
    g["mlp_w1"] = _mm(flat2(h2), da1, ta=True, name="mlp1_dw")
    token = on_grads("mlp", {n: g.pop(n) for n in ("mlp_w2", "mlp_w1")}, dh2)
    if token is not None:
        gate1 = gate1 + token[0, 0]
    dx1, dscale2, dshift2, dg2, dgate1, dmixed = _norm_mod_bwd(
        bsd(dh2), x1, dy, small["norm2_g"], scale2, gate1, bsd(mixed), name="norm2_bwd")
    dmixed2 = flat2(dmixed)
    dmi = _mm(dmixed2, w["w_out"], tb=True, name="w_out_dx", out_dtype=BF16)
    g["w_out"] = _mm(flat2(mixed_in), dmixed2, ta=True, name="w_out_dw")
    dy_a, dy_b, dl_a, dl_b, db_a, db_b = _merge_bwd(bsd(dmi), proj3, small["b_merge"], bsd(y_a), bsd(y_b), name="merge_bwd")
    dy_a2, dy_b2 = flat2(dy_a), flat2(dy_b)
    dog = _mm(dy_a2, w["gla_w_o"], tb=True, name="gla_out_dx")
    g["gla_w_o"] = _mm(flat2(o_gated), dy_a2, ta=True, name="gla_out_dw")
    dq_g, dk_g, dv_g, dg_g, dlog, db_alpha, d_ong = _gla_bwd(
        bsd(dog), o, states, proj3, w_alpha_p, small["gla_b_alpha"], small["gla_out_norm_g"], name="gla_bwd")
    dlog2 = flat2(dlog)
    da_p = _mm(dlog2, w_alpha_p, tb=True, out_dtype=BF16, name="alpha_dx")
    d_w_alpha = _mm(proj[:, OFF_A:OFF_A + LANE], dlog2, ta=True, name="alpha_dw")[:GLR]
    do_attn = _mm(dy_b2, w["mla_w_o"], tb=True, out_dtype=BF16, name="mla_out_dx")
    g["mla_w_o"] = _mm(flat2(o_attn), dy_b2, ta=True, name="mla_out_dw")
    dqf, dkf, dvf = _attn_bwd(bsd(qf), bsd(kf), bsd(vf), bsd(do_attn), name="attn_bwd")
    dq_raw, dkv, dkpe, dgq, dgk = _qk_prep_bwd(flat2(dqf), flat2(dkf), flat2(dvf), q_raw, kv, proj, cos_t, sin_t, gq, gk,
                                                name="qk_prep_bwd")
    dcq_n = _mm(dq_raw, w["mla_w_uq"], tb=True, name="mla_uq_dx")
    g["mla_w_uq"] = _mm(cq_n, dq_raw, ta=True, name="mla_uq_dw")
    dckv_n = _mm(dkv, w["mla_w_ukv"], tb=True, name="mla_ukv_dx")
    g["mla_w_ukv"] = _mm(ckv_n, dkv, ta=True, name="mla_ukv_dw")
    token = on_grads("mix", {n: g.pop(n) for n in ("w_out", "gla_w_o", "mla_w_o", "mla_w_uq", "mla_w_ukv")}, dckv_n)
    q_lat_g = small["mla_q_lat_g"] if token is None else small["mla_q_lat_g"] + token[0:1, 0:1]
    dcq, dckv, dg_qlat, dg_kvlat = _lat_norm_bwd(dcq_n, dckv_n, proj, q_lat_g, small["mla_kv_lat_g"],
                                                  name="lat_norm_bwd")
    pieces = [(flat2(dq_g), OFF_Q), (flat2(dk_g), OFF_K), (flat2(dv_g), OFF_V), (flat2(dg_g), OFF_G),
              (flat2(dl_a), OFF_MA), (flat2(dl_b), OFF_MB), (dcq, OFF_CQ), (dckv, OFF_CKV), (da_p, OFF_A), (dkpe, OFF_KPE)]
    hb = flat2(h)
    g_w_in = (_pieces_dw(hb, [p for p, off in pieces if off < W_IN_SPLIT], name="proj_dw_a"),
              _pieces_dw(hb, [p for p, off in pieces if off >= W_IN_SPLIT], name="proj_dw_b"))
    token = on_grads("in", {"w_in": g_w_in}, g_w_in[1])
    after = jnp.zeros((8, LANE), F32) if token is None else token
    dh = _pieces_dx(pieces, w["w_in"], after, name="proj_dx")
    token = on_grads("dx", {}, dh)
    if token is not None:
        scale1 = scale1 + token[0, 0]
    grad_x, dscale1, dshift1, dg1 = _norm_mod_bwd(bsd(dh), x, dx1, small["norm1_g"], scale1, name="norm1_bwd")

    dmod = jnp.concatenate([dshift1, dscale1, dgate1, dshift2, dscale2, dgate2], axis=-1).reshape(bsz, 6 * D)
    gs = {"norm1_g": dg1, "b_merge": jnp.concatenate([db_a, db_b], axis=1), "gla_b_alpha": db_alpha,
          "gla_out_norm_g": d_ong, "mla_q_lat_g": dg_qlat, "mla_kv_lat_g": dg_kvlat, "mla_qn_g": dgq[:, :MQK],
          "mla_kn_g": dgk[:, :MQK], "norm2_g": dg2}
    return loss_part[0, 0], grad_x, dmod, {**kept, **g}, gs, d_w_alpha


def kernel(x, c, positions, w_ada, b_ada, norm1_g, w_in, b_merge, gla_w_alpha, gla_b_alpha, gla_out_norm_g, gla_w_o, mla_q_lat_g, mla_w_uq, mla_kv_lat_g, mla_w_ukv, mla_qn_g, mla_kn_g, mla_w_o, w_out, norm2_g, mlp_w1, mlp_w2, loss_target, m_w_ada, m_b_ada, m_norm1_g, m_w_in, m_b_merge, m_gla_w_alpha, m_gla_b_alpha, m_gla_out_norm_g, m_gla_w_o, m_mla_q_lat_g, m_mla_w_uq, m_mla_kv_lat_g, m_mla_w_ukv, m_mla_qn_g, m_mla_kn_g, m_mla_w_o, m_w_out, m_norm2_g, m_mlp_w1, m_mlp_w2, v_w_ada, v_b_ada, v_norm1_g, v_w_in, v_b_merge, v_gla_w_alpha, v_gla_b_alpha, v_gla_out_norm_g, v_gla_w_o, v_mla_q_lat_g, v_mla_w_uq, v_mla_kv_lat_g, v_mla_w_ukv, v_mla_qn_g, v_mla_kn_g, v_mla_w_o, v_w_out, v_norm2_g, v_mlp_w1, v_mlp_w2):
    args = dict(locals())
    names_big = [n for n, _, _ in BIG]
    names_small = [n for n, _ in SMALL]
    bsz = x.shape[0]
    ax, ay, ac = lax.axis_index("x"), lax.axis_index("y"), lax.axis_index("c")
    chip = 2 * ax + ay
    dev = 2 * chip + ac

    small = {n: args[n] for n in names_small}
    sel_c = jnp.reshape(ac, (1,)).astype(jnp.int32)
    sel_chip = jnp.reshape(chip, (1,)).astype(jnp.int32)
    c_all, w_alpha_all = _all_gather8([c, gla_w_alpha[0]], name="comm_c_alpha")
    small["gla_w_alpha"] = jnp.concatenate([w_alpha_all[2 * j] for j in range(4)], axis=1)
    c_all = c_all.reshape(8 * bsz, D)

    shards = {n: args[n][0].astype(BF16) for n in names_big}
    halves_of = lambda names: [shards[n].reshape(2, shards[n].shape[0] // 2, shards[n].shape[1]) for n in names]

    def gather_start(names, deps, tag):
        xs = halves_of(names)
        lands = [lax.empty((4, *xh.shape), BF16) for xh in xs]
        plan = _gather_plan(len(names))
        return names, plan, _rdma_start(xs + lands, 3 * len(names), plan, deps, name="comm_weights_start_" + tag)

    def gather_finish(started, after, tag):
        names, plan, sems = started
        arrs = _rdma_wait(sems[0], sems[1], sems[2], plan, after, name="comm_weights_wait_" + tag)
        filled = _pair_fill(arrs[len(names):], name="comm_weights_pair_" + tag)
        own = [a.reshape(shards[n].shape) for n, a in zip(names, arrs)]
        return _full_weights(dict(zip(names, _own_and_landed(filled, own))))


    def add_bias(acc, ex, outs):
        outs[0][...] = acc + ex[0][...]

    silu = lambda v: v * _sigmoid(v)
    b_ada_mine = lax.dynamic_slice(b_ada, (0, chip * ADA_SHARD[1]), (1, ADA_SHARD[1]))
    mod_part = _mm(c_all, w_ada[0], name="ada", tn=512, a_fn=silu, epilogue=add_bias, extras=(b_ada_mine,),
                   extra_specs=(pl.BlockSpec((1, 512), lambda i, j, k: (0, j)),),
                   out_shape=jax.ShapeDtypeStruct((8 * bsz, ADA_SHARD[1]), F32), out_specs=_tile_spec(8 * bsz, 512))
    mod_all = _all_gather8([mod_part], name="comm_mod")[0]
    mod_rows = lax.dynamic_slice(mod_all, (0, dev * bsz, 0), (8, bsz, ADA_SHARD[1]))
    mod = jnp.concatenate([mod_rows[2 * j] for j in range(4)], axis=1)
    first = gather_start(["w_in"], (mod,), "in")
    rest = gather_start([n for n in names_big if n != "w_in"], (mod, first[2][3]), "rest")
    mod = mod + rest[2][3][0, 0]
    w_in_after = lambda after: gather_finish(first, after, "in")
    more_weights = lambda after: gather_finish(rest, after, "rest")

    stage = {}

    def begin(tag, names, arrays, lands, n_copies, plan, what):
        stage[tag] = (names, plan, _rdma_start(arrays + lands, n_copies, plan, (), name=f"comm_{what}_start_{tag}"))
        return stage[tag][2][3]

    def landed(tag, after, what):
        names, plan, sems = stage[tag]
        arrs = _rdma_wait(sems[0], sems[1], sems[2], plan, after, name=f"comm_{what}_wait_{tag}")
        return names, arrs[:len(arrs) // 2], arrs[len(arrs) // 2:]

    def swap_start(tag, grads):
        names = list(grads)
        parts = [_grad_slots(grads)[n] for n in names]
        lands = [lax.empty((4, *p.shape[2:]), F32) for p in parts]
        return begin(tag, names, parts, lands, len(names), _sibling_plan(len(names), lambda r, c: r.at[:, 1 - c]), "pair_sum")

    def scatter_start(tag, after):
        names, parts, sib_halves = landed(tag, after, "pair_sum")
        pairs = [_pair_add(p, s, sel_c, name="pair_add_" + n) for n, p, s in zip(names, parts, sib_halves)]
        recvs = [lax.empty((3, *p.shape[1:]), BF16) for p in pairs]
        return begin(tag, names, pairs, recvs, 3 * len(names), _scatter_plan(len(names)), "scatter")

    def join_start(tag, after):
        names, pairs, recvs = landed(tag, after, "scatter")
        halves = [_chip_sum(p, r, sel_chip, name="chip_sum_" + n) for n, p, r in zip(names, pairs, recvs)]
        lands = [lax.empty(h.shape, F32) for h in halves]
        return begin(tag, names, halves, lands, len(names), _sibling_plan(len(names), lambda r, c: r), "pair_join")

    def reduce_step(tag, grads, after):
        if tag == "mlp":
            return swap_start("mlp", grads)
        if tag == "mix":
            return scatter_start("mlp", after) + swap_start("mix", grads)
        if tag == "in":
            return scatter_start("mix", after) + swap_start("in", grads)
        return scatter_start("in", after)

    loss_part, grad_x, dmod, g, gs, d_w_alpha = _local_step(x, positions, mod, loss_target, w_in_after, small,
                                                            more_weights, reduce_step)

    assert not g, list(g)
    gs_packed = _pack_small([gs[n] for n, _ in SMALL_RED], d_w_alpha, jnp.full((1, LANE), loss_part, F32),
                            name="pack_small")
    small_lands = [lax.empty((8, *a.shape), F32) for a in (dmod, gs_packed)]
    begin("small", ["dmod", "small"], [dmod, gs_packed], small_lands, 7 * 2, _gather8_plan(2), "gather8")

    res = {}

    def finish(tag, after):
        names, halves, theirs = landed(tag, after, "pair_join")
        for n, mine, other in zip(names, halves, theirs):
            if n == "w_in":
                south = ac == 0
                g_t = jnp.concatenate([jnp.where(south, mine, other), jnp.where(south, other, mine)], axis=0).T
                outs = _adamw(w_in[0].T, g_t, m_w_in[0].T, v_w_in[0].T, name="adamw_w_in", by_cols=True)
                res[n] = tuple(a.T for a in (g_t, *outs))
            else:
                res[n] = _adamw_halves(args[n][0], args["m_" + n][0], args["v_" + n][0], mine, other, sel_c,
                                       name="adamw_" + n)
        return res[names[-1]][1]

    join_start("mlp", grad_x)
    join_start("mix", grad_x)
    done = finish("mix", finish("mlp", grad_x))

    _, (dmod_own, gs_own), (dmod_all, gs_all) = landed("small", done, "gather8")
    dmod_all = lax.dynamic_update_slice(dmod_all, dmod_own[None], (dev, 0, 0)).reshape(8 * bsz, 6 * D)
    gs_all = lax.dynamic_update_slice(gs_all, gs_own[None], (dev, 0, 0))
    dmod_mine = lax.dynamic_slice(dmod_all, (0, chip * ADA_SHARD[1]), (8 * bsz, ADA_SHARD[1]))
    g_w_ada = _mm(c_all, dmod_mine, ta=True, a_fn=silu, name="ada_dw")
    wmv = [(args[n], args["m_" + n], args["v_" + n]) for n in names_small]
    wmv.append((gla_w_alpha[0], m_gla_w_alpha[0], v_gla_w_alpha[0]))
    res_small, loss_sum = _small_update(gs_all, dmod_all, sel_chip, wmv, name="small_update")
    res.update(res_small)
    loss = loss_sum * (0.5 / D)
    join_start("in", g_w_ada)
    res["w_ada"] = (g_w_ada, *_adamw(w_ada[0], g_w_ada, m_w_ada[0], v_w_ada[0], name="adamw_w_ada"))
    finish("in", res["w_ada"][1])

    order = ["w_ada", "b_ada", "norm1_g", "w_in", "b_merge", "gla_w_alpha", "gla_b_alpha", "gla_out_norm_g", "gla_w_o",
             "mla_q_lat_g", "mla_w_uq", "mla_kv_lat_g", "mla_w_ukv", "mla_qn_g", "mla_kn_g", "mla_w_o", "w_out",
             "norm2_g", "mlp_w1", "mlp_w2"]
    named = lambda k: [res[n][k].reshape(args[n].shape) for n in order]
    return (loss, grad_x, *named(0), *named(1), *named(2), *named(3))
```

```python
import jax
import jax.numpy as jnp
import numpy as np
from jax import lax
from jax.experimental import pallas as pl
from jax.experimental.pallas import tpu as pltpu

F32 = jnp.float32
BF16 = jnp.bfloat16
MESH = pl.DeviceIdType.MESH

D = 1024
CHUNK = 64
EPS = 1e-6
GH, GDK, GDV, GLR, GTAU = 4, 128, 256, 16, 16.0
MH, MQR, MKVR, MNOPE, MROPE, MVD = 16, 256, 128, 64, 32, 64
MQK = MNOPE + MROPE
DFF = 4 * D
ROPE_THETA = 10000.0
IN_WIDTH = 5552
LANE = 128
OFF_Q, OFF_K, OFF_V, OFF_G, OFF_MA, OFF_MB, OFF_CQ, OFF_CKV, OFF_A, OFF_KPE, PW = (
    0, 512, 1024, 2048, 3072, 4096, 5120, 5376, 5504, 5632, 5760)
ADAM_LR, ADAM_B1, ADAM_B2, ADAM_EPS, ADAM_WD, ADAM_STEP = 0.001, 0.9, 0.999, 1e-08, 0.01, 10
VMEM_LIMIT = 48 * 1024 * 1024


def _params(n_axes):
    return pltpu.CompilerParams(dimension_semantics=("arbitrary",) * n_axes, vmem_limit_bytes=VMEM_LIMIT)


def _tile(n, target):
    if n <= target:
        return n
    best = None
    for t in range(LANE, target + 1, LANE):
        if n % t == 0:
            best = t
    assert best is not None, (n, target)
    return best


def _sigmoid(x):
    return 1.0 / (1.0 + jnp.exp(-x))


MM_VMEM_BUDGET = 36 * 1024 * 1024


def _mm(a, b, *, name, ta=False, tb=False, out_dtype=F32, tm=1024, tn=1024, tk=4096,
        epilogue=None, extras=(), extra_specs=(), out_shape=None, out_specs=None, a_fn=None):
    if ta:
        kdim, m = a.shape
    else:
        m, kdim = a.shape
    if tb:
        n, k2 = b.shape
    else:
        k2, n = b.shape
    assert kdim == k2, (a.shape, b.shape)
    tm, tn, tk = _tile(m, tm), _tile(n, tn), _tile(kdim, tk)
    tiles = lambda rows: 2 * (rows * tk * a.dtype.itemsize + tk * tn * b.dtype.itemsize + rows * tn * 4) + rows * tn * 4
    while out_shape is None and tiles(tm) > MM_VMEM_BUDGET and tm % 256 == 0:
        tm //= 2
    nk = kdim // tk
    a_spec = pl.BlockSpec((tk, tm), lambda i, j, k: (k, i)) if ta else pl.BlockSpec((tm, tk), lambda i, j, k: (i, k))
    b_spec = pl.BlockSpec((tn, tk), lambda i, j, k: (j, k)) if tb else pl.BlockSpec((tk, tn), lambda i, j, k: (k, j))
    dims = (((0 if ta else 1,), (1 if tb else 0,)), ((), ()))
    ne = len(extras)
    if out_shape is None:
        out_shape = jax.ShapeDtypeStruct((m, n), out_dtype)
        out_specs = pl.BlockSpec((tm, tn), lambda i, j, k: (i, j))
    n_out = len(out_shape) if isinstance(out_shape, (list, tuple)) else 1
    in_place = epilogue is None and n_out == 1 and out_shape.dtype == F32
    scratch = [] if (nk == 1 or in_place) else [pltpu.VMEM((tm, tn), F32)]

    def body(a_ref, b_ref, *rest):
        ex, outs = rest[:ne], rest[ne:ne + n_out]
        av = a_ref[...] if a_fn is None else a_fn(a_ref[...])
        prod = lax.dot_general(av.astype(BF16), b_ref[...].astype(BF16), dims, preferred_element_type=F32)

        def finish(val):
            if epilogue is None:
                outs[0][...] = val.astype(outs[0].dtype)
            else:
                epilogue(val, ex, outs)

        if nk == 1:
            finish(prod)
            return
        k = pl.program_id(2)
        acc = outs[0] if in_place else rest[-1]

        @pl.when(k == 0)
        def _():
            acc[...] = prod

        @pl.when(k > 0)
        def _():
            acc[...] += prod

        if not in_place:
            @pl.when(k == nk - 1)
            def _():
                finish(acc[...])

    return pl.pallas_call(
        body, name=name, grid=(m // tm, n // tn, nk),
        in_specs=[a_spec, b_spec, *extra_specs], out_specs=out_specs, out_shape=out_shape,
        scratch_shapes=scratch, compiler_params=_params(3),
    )(a, b, *extras)


def _tile_spec(tm, tn):
    return pl.BlockSpec((tm, tn), lambda i, j, k: (i, j))


def _pieces_dx(pieces, w, after, *, name, tm=512):
    t = pieces[0][0].shape[0]
    tm = _tile(t, tm)
    npc = len(pieces)

    def body(*refs):
        p_refs, w_ref, out_ref = refs[:npc], refs[npc], refs[-1]
        acc = None
        for (arr, off), p_ref in zip(pieces, p_refs):
            part = lax.dot_general(p_ref[...].astype(BF16), w_ref[:, off:off + arr.shape[1]], _NT,
                                   preferred_element_type=F32)
            acc = part if acc is None else acc + part
        out_ref[...] = acc

    return pl.pallas_call(
        body, name=name, grid=(t // tm,),
        in_specs=[pl.BlockSpec((tm, arr.shape[1]), lambda i: (i, 0)) for arr, _ in pieces]
        + [pl.BlockSpec(w.shape, lambda i: (0, 0), pipeline_mode=pl.Buffered(1)),
           pl.BlockSpec((8, LANE), lambda i: (0, 0))],
        out_specs=pl.BlockSpec((tm, w.shape[0]), lambda i: (i, 0)),
        out_shape=jax.ShapeDtypeStruct((t, w.shape[0]), F32), compiler_params=_params(1),
    )(*[arr for arr, _ in pieces], w, after)


def _pieces_dw(h, pieces, *, name, tk=1024):
    t, d = h.shape
    tk = _tile(t, tk)
    widths = [p.shape[1] for p in pieces]
    starts = [sum(widths[:i]) for i in range(len(pieces))]

    def body(h_ref, *refs):
        p_refs, out_ref = refs[:-1], refs[-1]
        first = pl.program_id(0) == 0
        hv = h_ref[...]
        for p_ref, start, width in zip(p_refs, starts, widths):
            part = lax.dot_general(hv, p_ref[...].astype(BF16), _TN, preferred_element_type=F32)
            cols = slice(start, start + width)

            @pl.when(first)
            def _():
                out_ref[:, cols] = part

            @pl.when(jnp.logical_not(first))
            def _():
                out_ref[:, cols] += part

    return pl.pallas_call(
        body, name=name, grid=(t // tk,),
        in_specs=[pl.BlockSpec((tk, d), lambda k: (k, 0))] + [pl.BlockSpec((tk, wd), lambda k: (k, 0)) for wd in widths],
        out_specs=pl.BlockSpec((d, sum(widths)), lambda k: (0, 0)),
        out_shape=jax.ShapeDtypeStruct((d, sum(widths)), F32), compiler_params=_params(1),
    )(h, *pieces)


def _rms(x, g):
    r = lax.rsqrt(jnp.mean(x * x, axis=-1, keepdims=True) + EPS)
    return x * r, r


def _row_spec(ts, width, col=0):
    return pl.BlockSpec((None, ts, width), lambda b, i: (b, i, col))


def _vec_spec(width):
    return pl.BlockSpec((None, 1, width), lambda b, i: (b, 0, 0))


def _gain_spec(width):
    return pl.BlockSpec((1, width), lambda b, i: (0, 0))


def _norm_mod(x, g, scale, shift, *, name, ts=512):
    bsz, s, d = x.shape
    ts = min(ts, s)

    def body(x_ref, g_ref, sc_ref, sh_ref, h_ref):
        xh, _ = _rms(x_ref[...], None)
        h_ref[...] = ((xh * g_ref[...]) * (1.0 + sc_ref[...]) + sh_ref[...]).astype(BF16)

    return pl.pallas_call(
        body, name=name, grid=(bsz, s // ts),
        in_specs=[_row_spec(ts, d), _gain_spec(d), _vec_spec(d), _vec_spec(d)],
        out_specs=_row_spec(ts, d), out_shape=jax.ShapeDtypeStruct((bsz, s, d), BF16),
        compiler_params=_params(2),
    )(x, g, scale, shift)


def _resid_norm_mod(x, mixed, gate, g, scale, shift, *, name, ts=512):
    bsz, s, d = x.shape
    ts = min(ts, s)

    def body(x_ref, mx_ref, gt_ref, g_ref, sc_ref, sh_ref, x1_ref, h_ref):
        x1 = x_ref[...] + gt_ref[...] * mx_ref[...]
        x1_ref[...] = x1
        xh, _ = _rms(x1, None)
        h_ref[...] = ((xh * g_ref[...]) * (1.0 + sc_ref[...]) + sh_ref[...]).astype(BF16)

    return pl.pallas_call(
        body, name=name, grid=(bsz, s // ts),
        in_specs=[_row_spec(ts, d), _row_spec(ts, d), _vec_spec(d), _gain_spec(d), _vec_spec(d), _vec_spec(d)],
        out_specs=[_row_spec(ts, d), _row_spec(ts, d)],
        out_shape=[jax.ShapeDtypeStruct((bsz, s, d), F32), jax.ShapeDtypeStruct((bsz, s, d), BF16)],
        compiler_params=_params(2),
    )(x, mixed, gate, g, scale, shift)


def _norm_mod_bwd(dh, xin, resid, g, scale, gate=None, mixed=None, *, name, ts=512):
    bsz, s, d = xin.shape
    ts = min(ts, s)
    gated = gate is not None

    def body(*refs):
        if gated:
            dh_ref, x_ref, rs_ref, g_ref, sc_ref, gt_ref, mx_ref, dx_ref, dsc_ref, dsh_ref, dg_ref, dgt_ref, dmx_ref = refs
        else:
            dh_ref, x_ref, rs_ref, g_ref, sc_ref, dx_ref, dsc_ref, dsh_ref, dg_ref = refs
        b, i = pl.program_id(0), pl.program_id(1)

        @pl.when(i == 0)
        def _():
            dsc_ref[...] = jnp.zeros_like(dsc_ref)
            dsh_ref[...] = jnp.zeros_like(dsh_ref)
            if gated:
                dgt_ref[...] = jnp.zeros_like(dgt_ref)

        @pl.when((i == 0) & (b == 0))
        def _():
            dg_ref[...] = jnp.zeros_like(dg_ref)

        dh_v, gv = dh_ref[...], g_ref[...]
        xh, r = _rms(x_ref[...], None)
        dsc_ref[...] += jnp.sum(dh_v * (xh * gv), axis=0, keepdims=True)
        dsh_ref[...] += jnp.sum(dh_v, axis=0, keepdims=True)
        dn = dh_v * (1.0 + sc_ref[...])
        dg_ref[...] += jnp.sum(dn * xh, axis=0, keepdims=True)
        dxh = dn * gv
        dx = rs_ref[...] + r * (dxh - xh * jnp.mean(dxh * xh, axis=-1, keepdims=True))
        dx_ref[...] = dx
        if gated:
            dgt_ref[...] += jnp.sum(dx * mx_ref[...], axis=0, keepdims=True)
            dmx_ref[...] = (dx * gt_ref[...]).astype(BF16)

    ins = [dh, xin, resid, g, scale]
    in_specs = [_row_spec(ts, d), _row_spec(ts, d), _row_spec(ts, d), _gain_spec(d), _vec_spec(d)]
    out_specs = [_row_spec(ts, d), _vec_spec(d), _vec_spec(d), _gain_spec(d)]
    out_shape = [jax.ShapeDtypeStruct((bsz, s, d), F32), jax.ShapeDtypeStruct((bsz, 1, d), F32),
                 jax.ShapeDtypeStruct((bsz, 1, d), F32), jax.ShapeDtypeStruct((1, d), F32)]
    if gated:
        ins += [gate, mixed]
        in_specs += [_vec_spec(d), _row_spec(ts, d)]
        out_specs += [_vec_spec(d), _row_spec(ts, d)]
        out_shape += [jax.ShapeDtypeStruct((bsz, 1, d), F32), jax.ShapeDtypeStruct((bsz, s, d), BF16)]
    return pl.pallas_call(
        body, name=name, grid=(bsz, s // ts), in_specs=in_specs, out_specs=out_specs, out_shape=out_shape,
        compiler_params=_params(2),
    )(*ins)


def _loss_head(x1, ff, gate2, target, *, name, ts=512):
    bsz, s, d = x1.shape
    ts = min(ts, s)

    def body(x1_ref, ff_ref, gt_ref, t_ref, dy_ref, dff_ref, dgt_ref, loss_ref, acc):
        b, i = pl.program_id(0), pl.program_id(1)

        @pl.when(i == 0)
        def _():
            dgt_ref[...] = jnp.zeros_like(dgt_ref)

        @pl.when((i == 0) & (b == 0))
        def _():
            acc[...] = jnp.zeros_like(acc)

        ffv, gt = ff_ref[...], gt_ref[...]
        diff = (x1_ref[...] + gt * ffv) - t_ref[...]
        acc[...] += jnp.sum((diff * diff).reshape(ts // 8, 8, d), axis=0)
        dy = diff * (1.0 / d)
        dy_ref[...] = dy
        dgt_ref[...] += jnp.sum(dy * ffv, axis=0, keepdims=True)
        dff_ref[...] = (dy * gt).astype(BF16)

        @pl.when((i == pl.num_programs(1) - 1) & (b == pl.num_programs(0) - 1))
        def _():
            loss_ref[...] = jnp.full(loss_ref.shape, jnp.sum(acc[...]), F32)

    return pl.pallas_call(
        body, name=name, grid=(bsz, s // ts),
        in_specs=[_row_spec(ts, d), _row_spec(ts, d), _vec_spec(d), _row_spec(ts, d)],
        out_specs=[_row_spec(ts, d), _row_spec(ts, d), _vec_spec(d), pl.BlockSpec((8, LANE), lambda b, i: (0, 0))],
        out_shape=[jax.ShapeDtypeStruct((bsz, s, d), F32), jax.ShapeDtypeStruct((bsz, s, d), BF16),
                   jax.ShapeDtypeStruct((bsz, 1, d), F32), jax.ShapeDtypeStruct((8, LANE), F32)],
        scratch_shapes=[pltpu.VMEM((8, d), F32)], compiler_params=_params(2),
    )(x1, ff, gate2, target)


def _merge_fwd(proj, b_merge, y_a, y_b, *, name, ts=512):
    bsz, s, _ = proj.shape
    ts = min(ts, s)

    def body(la_ref, lb_ref, ba_ref, bb_ref, ya_ref, yb_ref, out_ref):
        ga = _sigmoid(la_ref[...] + ba_ref[...])
        gb = _sigmoid(lb_ref[...] + bb_ref[...])
        out_ref[...] = (ga * ya_ref[...] + gb * yb_ref[...]).astype(BF16)

    return pl.pallas_call(
        body, name=name, grid=(bsz, s // ts),
        in_specs=[_row_spec(ts, D, OFF_MA // D), _row_spec(ts, D, OFF_MB // D),
                  pl.BlockSpec((1, D), lambda b, i: (0, 0)), pl.BlockSpec((1, D), lambda b, i: (0, 1)),
                  _row_spec(ts, D), _row_spec(ts, D)],
        out_specs=_row_spec(ts, D), out_shape=jax.ShapeDtypeStruct((bsz, s, D), BF16),
        compiler_params=_params(2),
    )(proj, proj, b_merge, b_merge, y_a, y_b)


def _merge_bwd(dmi, proj, b_merge, y_a, y_b, *, name, ts=512):
    bsz, s, _ = proj.shape
    ts = min(ts, s)

    def body(d_ref, la_ref, lb_ref, ba_ref, bb_ref, ya_ref, yb_ref, dya_ref, dyb_ref, dla_ref, dlb_ref, dba_ref, dbb_ref):
        @pl.when((pl.program_id(0) == 0) & (pl.program_id(1) == 0))
        def _():
            dba_ref[...] = jnp.zeros_like(dba_ref)
            dbb_ref[...] = jnp.zeros_like(dbb_ref)

        dv = d_ref[...].astype(F32)
        ga = _sigmoid(la_ref[...] + ba_ref[...])
        gb = _sigmoid(lb_ref[...] + bb_ref[...])
        dya_ref[...] = (dv * ga).astype(BF16)
        dyb_ref[...] = (dv * gb).astype(BF16)
        dla = (dv * ya_ref[...]) * (ga * (1.0 - ga))
        dlb = (dv * yb_ref[...]) * (gb * (1.0 - gb))
        dla_ref[...] = dla.astype(BF16)
        dlb_ref[...] = dlb.astype(BF16)
        dba_ref[...] += jnp.sum(dla, axis=0, keepdims=True)
        dbb_ref[...] += jnp.sum(dlb, axis=0, keepdims=True)

    act = jax.ShapeDtypeStruct((bsz, s, D), BF16)
    return pl.pallas_call(
        body, name=name, grid=(bsz, s // ts),
        in_specs=[_row_spec(ts, D), _row_spec(ts, D, OFF_MA // D), _row_spec(ts, D, OFF_MB // D),
                  pl.BlockSpec((1, D), lambda b, i: (0, 0)), pl.BlockSpec((1, D), lambda b, i: (0, 1)),
                  _row_spec(ts, D), _row_spec(ts, D)],
        out_specs=[_row_spec(ts, D)] * 4 + [_gain_spec(D)] * 2,
        out_shape=[act, act, act, act, jax.ShapeDtypeStruct((1, D), F32), jax.ShapeDtypeStruct((1, D), F32)],
        compiler_params=_params(2),
    )(dmi, proj, proj, b_merge, b_merge, y_a, y_b)


def _tri(lower):
    r = lax.broadcasted_iota(jnp.int32, (CHUNK, CHUNK), 0)
    c = lax.broadcasted_iota(jnp.int32, (CHUNK, CHUNK), 1)
    return jnp.where((c <= r) if lower else (c >= r), 1.0, 0.0).astype(F32)


def _gla_logits(a_ref, wal_ref, bal_ref):
    logits = jnp.dot(a_ref[...].astype(BF16), wal_ref[...].astype(BF16), preferred_element_type=F32) + bal_ref[...]
    la = (jnp.minimum(logits, 0.0) - jnp.log(1.0 + jnp.exp(-jnp.abs(logits)))) * (1.0 / GTAU)
    return logits, la


def _chunk_cumsum(la_n, tri, precision=lax.Precision.HIGHEST):
    cum = jnp.dot(tri, la_n, preferred_element_type=F32, precision=precision)
    return cum, jnp.sum(la_n, axis=0, keepdims=True)


def _gla_specs(s, nc):
    def blk(width, off):
        return pl.BlockSpec((None, s, width), lambda h, b: (b, 0, off // width + h))

    proj_specs = [blk(GDK, OFF_Q), blk(GDK, OFF_K), blk(GDV, OFF_V), blk(GDV, OFF_G),
                  pl.BlockSpec((None, s, LANE), lambda h, b: (b, 0, OFF_A // LANE)),
                  pl.BlockSpec((LANE, GDK), lambda h, b: (0, h)), pl.BlockSpec((1, GDK), lambda h, b: (0, h)),
                  pl.BlockSpec((1, GDV), lambda h, b: (0, 0))]
    st_spec = pl.BlockSpec((None, None, nc, GDV, GDK), lambda h, b: (b, h, 0, 0, 0))
    return blk, proj_specs, st_spec


def _gla_fwd(proj, w_alpha_p, b_alpha, out_norm_g, *, name):
    bsz, s, _ = proj.shape
    nc = s // CHUNK
    scale = GDK ** -0.5

    rb = min(512, s)

    def body(q_ref, k_ref, v_ref, g_ref, a_ref, wal_ref, bal_ref, ong_ref, o_ref, og_ref, st_ref):
        _, la = _gla_logits(a_ref, wal_ref, bal_ref)
        tri = _tri(True)
        st = jnp.zeros((GDV, GDK), F32)
        for n in range(nc):
            rows = pl.ds(n * CHUNK, CHUNK)
            cum, cum_end = _chunk_cumsum(la[n * CHUNK:(n + 1) * CHUNK], tri, lax.Precision.HIGH)
            kd = k_ref[rows, :] * jnp.exp(cum_end - cum)
            ut = lax.dot_general(v_ref[rows, :].astype(BF16), kd.astype(BF16), _TN, preferred_element_type=F32)
            st = st * jnp.exp(cum_end) + ut
            st_ref[n] = st
            o_ref[rows, :] = lax.dot_general((q_ref[rows, :].astype(F32) * scale).astype(BF16), st.astype(BF16), _NT,
                                             preferred_element_type=F32)
        for j in range(0, s, rb):
            blk_rows = pl.ds(j, rb)
            oh, _ = _rms(o_ref[blk_rows, :], None)
            gv = g_ref[blk_rows, :].astype(F32)
            og_ref[blk_rows, :] = ((oh * ong_ref[...]) * (gv * _sigmoid(gv))).astype(BF16)

    blk, proj_specs, st_spec = _gla_specs(s, nc)
    return pl.pallas_call(
        body, name=name, grid=(GH, bsz), in_specs=proj_specs, out_specs=[blk(GDV, 0), blk(GDV, 0), st_spec],
        out_shape=[jax.ShapeDtypeStruct((bsz, s, GH * GDV), F32), jax.ShapeDtypeStruct((bsz, s, GH * GDV), BF16),
                   jax.ShapeDtypeStruct((bsz, GH, nc, GDV, GDK), F32)],
        compiler_params=_params(2),
    )(proj, proj, proj, proj, proj, w_alpha_p, b_alpha, out_norm_g)


def _gla_bwd(dog, o, states, proj, w_alpha_p, b_alpha, out_norm_g, *, name):
    bsz, s, _ = proj.shape
    nc = s // CHUNK
    scale = GDK ** -0.5

    def body(dog_ref, o_ref, st_ref, q_ref, k_ref, v_ref, g_ref, a_ref, wal_ref, bal_ref, ong_ref,
             dq_ref, dk_ref, dv_ref, dg_ref, dl_ref, dbal_ref, dong_ref, do_scr, dlog_scr):
        h, b = pl.program_id(0), pl.program_id(1)

        @pl.when(b == 0)
        def _():
            dbal_ref[...] = jnp.zeros_like(dbal_ref)

        @pl.when((b == 0) & (h == 0))
        def _():
            dong_ref[...] = jnp.zeros_like(dong_ref)

        ong = ong_ref[...]
        for j in range(0, s, rb):
            blk_rows = pl.ds(j, rb)
            gv, dogv = g_ref[blk_rows, :].astype(F32), dog_ref[blk_rows, :]
            sg = _sigmoid(gv)
            oh, r = _rms(o_ref[blk_rows, :], None)
            don = dogv * (gv * sg)
            dg_ref[blk_rows, :] = (dogv * (oh * ong) * (sg * (1.0 + gv * (1.0 - sg)))).astype(BF16)
            dong_ref[...] += jnp.sum(don * oh, axis=0, keepdims=True)
            doh = don * ong
            do_scr[blk_rows, :] = (r * (doh - oh * jnp.mean(doh * oh, axis=-1, keepdims=True))).astype(BF16)

        logits, la = _gla_logits(a_ref, wal_ref, bal_ref)
        tri_lo, tri_up = _tri(True), _tri(False)
        carry = jnp.zeros((GDV, GDK), F32)
        for n in range(nc - 1, -1, -1):
            rows = pl.ds(n * CHUNK, CHUNK)
            cum, cum_end = _chunk_cumsum(la[n * CHUNK:(n + 1) * CHUNK], tri_lo, lax.Precision.HIGH)
            decay = jnp.exp(cum_end)
            w = jnp.exp(cum_end - cum)
            kd = k_ref[rows, :] * w
            do_b = do_scr[rows, :]
            qs_b = (q_ref[rows, :].astype(F32) * scale).astype(BF16)
            dq_ref[rows, :] = (jnp.dot(do_b, st_ref[n].astype(BF16), preferred_element_type=F32) * scale).astype(BF16)
            dsn = lax.dot_general(do_b, qs_b, _TN, preferred_element_type=F32) + carry
            carry = dsn * decay
            dsn_b = dsn.astype(BF16)
            dv_ref[rows, :] = lax.dot_general(kd.astype(BF16), dsn_b, _NT, preferred_element_type=F32).astype(BF16)
            dkd = jnp.dot(v_ref[rows, :].astype(BF16), dsn_b, preferred_element_type=F32)
            dk_ref[rows, :] = (dkd * w).astype(BF16)
            e = dkd * kd
            dcum_end = jnp.sum(e, axis=0, keepdims=True)
            if n > 0:
                dcum_end += jnp.sum(dsn * st_ref[n - 1], axis=0, keepdims=True) * decay
            dlog_scr[rows, :] = dcum_end - jnp.dot(tri_up, e, preferred_element_type=F32,
                                                  precision=lax.Precision.HIGH)
        dlog = dlog_scr[...] * (1.0 / GTAU) * (1.0 - _sigmoid(logits))
        dl_ref[...] = dlog.astype(BF16)
        dbal_ref[...] += jnp.sum(dlog, axis=0, keepdims=True)

    rb = min(512, s)

    blk, proj_specs, st_spec = _gla_specs(s, nc)
    act = lambda wd: jax.ShapeDtypeStruct((bsz, s, wd), BF16)
    return pl.pallas_call(
        body, name=name, grid=(GH, bsz), in_specs=[blk(GDV, 0), blk(GDV, 0), st_spec, *proj_specs],
        out_specs=[blk(GDK, 0), blk(GDK, 0), blk(GDV, 0), blk(GDV, 0), blk(GDK, 0),
                   pl.BlockSpec((1, GDK), lambda h, b: (0, h)), pl.BlockSpec((1, GDV), lambda h, b: (0, 0))],
        out_shape=[act(GH * GDK), act(GH * GDK), act(GH * GDV), act(GH * GDV), act(GH * GDK),
                   jax.ShapeDtypeStruct((1, GH * GDK), F32), jax.ShapeDtypeStruct((1, GDV), F32)],
        scratch_shapes=[pltpu.VMEM((s, GDV), BF16), pltpu.VMEM((s, GDK), F32)], compiler_params=_params(2),
    )(dog, o, states, proj, proj, proj, proj, proj, w_alpha_p, b_alpha, out_norm_g)


def _lane():
    return lax.broadcasted_iota(jnp.int32, (1, LANE), 1)


def _swap_halves(x):
    lane = _lane()
    half = MROPE // 2
    lo = (lane >= MNOPE) & (lane < MNOPE + half)
    hi = (lane >= MNOPE + half) & (lane < MQK)
    return jnp.where(lo, pltpu.roll(x, LANE - half, 1), jnp.where(hi, pltpu.roll(x, half, 1), 0.0))


def _norm96(x, g):
    r = lax.rsqrt(jnp.sum(x * x, axis=-1, keepdims=True) * (1.0 / MQK) + EPS)
    return x * r, r


def _lat_norm(proj, q_lat_g, kv_lat_g, *, name, ts=512):
    t = proj.shape[0]
    ts = min(ts, t)

    def body(cq_ref, ckv_ref, gq_ref, gk_ref, oq_ref, ok_ref):
        xq, _ = _rms(cq_ref[...].astype(F32), None)
        oq_ref[...] = (xq * gq_ref[...]).astype(BF16)
        xk, _ = _rms(ckv_ref[...].astype(F32), None)
        ok_ref[...] = (xk * gk_ref[...]).astype(BF16)

    return pl.pallas_call(
        body, name=name, grid=(t // ts,),
        in_specs=[pl.BlockSpec((ts, MQR), lambda i: (i, OFF_CQ // MQR)), pl.BlockSpec((ts, MKVR), lambda i: (i, OFF_CKV // MKVR)),
                  pl.BlockSpec((1, MQR), lambda i: (0, 0)), pl.BlockSpec((1, MKVR), lambda i: (0, 0))],
        out_specs=[pl.BlockSpec((ts, MQR), lambda i: (i, 0)), pl.BlockSpec((ts, MKVR), lambda i: (i, 0))],
        out_shape=[jax.ShapeDtypeStruct((t, MQR), BF16), jax.ShapeDtypeStruct((t, MKVR), BF16)],
        compiler_params=_params(1),
    )(proj, proj, q_lat_g, kv_lat_g)


def _lat_norm_bwd(dcqn, dckvn, proj, q_lat_g, kv_lat_g, *, name, ts=512):
    t = proj.shape[0]
    ts = min(ts, t)

    def one(d_ref, x_ref, g_ref, dx_ref, dg_ref):
        xh, r = _rms(x_ref[...].astype(F32), None)
        dn = d_ref[...]
        dg_ref[...] += jnp.sum(dn * xh, axis=0, keepdims=True)
        dxh = dn * g_ref[...]
        dx_ref[...] = (r * (dxh - xh * jnp.mean(dxh * xh, axis=-1, keepdims=True))).astype(BF16)

    def body(dq_ref, dk_ref, cq_ref, ckv_ref, gq_ref, gk_ref, dxq_ref, dxk_ref, dgq_ref, dgk_ref):
        @pl.when(pl.program_id(0) == 0)
        def _():
            dgq_ref[...] = jnp.zeros_like(dgq_ref)
            dgk_ref[...] = jnp.zeros_like(dgk_ref)

        one(dq_ref, cq_ref, gq_ref, dxq_ref, dgq_ref)
        one(dk_ref, ckv_ref, gk_ref, dxk_ref, dgk_ref)

    return pl.pallas_call(
        body, name=name, grid=(t // ts,),
        in_specs=[pl.BlockSpec((ts, MQR), lambda i: (i, 0)), pl.BlockSpec((ts, MKVR), lambda i: (i, 0)),
                  pl.BlockSpec((ts, MQR), lambda i: (i, OFF_CQ // MQR)), pl.BlockSpec((ts, MKVR), lambda i: (i, OFF_CKV // MKVR)),
                  pl.BlockSpec((1, MQR), lambda i: (0, 0)), pl.BlockSpec((1, MKVR), lambda i: (0, 0))],
        out_specs=[pl.BlockSpec((ts, MQR), lambda i: (i, 0)), pl.BlockSpec((ts, MKVR), lambda i: (i, 0)),
                   pl.BlockSpec((1, MQR), lambda i: (0, 0)), pl.BlockSpec((1, MKVR), lambda i: (0, 0))],
        out_shape=[jax.ShapeDtypeStruct((t, MQR), BF16), jax.ShapeDtypeStruct((t, MKVR), BF16),
                   jax.ShapeDtypeStruct((1, MQR), F32), jax.ShapeDtypeStruct((1, MKVR), F32)],
        compiler_params=_params(1),
    )(dcqn, dckvn, proj, proj, q_lat_g, kv_lat_g)


def _qk_prep(q_raw, kv, proj, cos_t, sin_t, gq, gk, *, name, ts=4096):
    t = q_raw.shape[0]
    ts = min(ts, t)

    def body(q_ref, kv_ref, kpe_ref, c_ref, s_ref, gq_ref, gk_ref, qo_ref, ko_ref, vo_ref):
        cs, sn = c_ref[...], s_ref[...]
        nope = _lane() < MNOPE
        qn, _ = _norm96(q_ref[...].astype(F32), None)
        qn = qn * gq_ref[...]
        qo_ref[...] = (qn * cs + _swap_halves(qn) * sn).astype(BF16)
        kvv = kv_ref[...].astype(F32)
        kn, _ = _norm96(jnp.where(nope, kvv, kpe_ref[...].astype(F32)), None)
        kn = kn * gk_ref[...]
        ko_ref[...] = (kn * cs + _swap_halves(kn) * sn).astype(BF16)
        vo_ref[...] = jnp.where(nope, pltpu.roll(kvv, MNOPE, 1), 0.0).astype(BF16)

    hd = pl.BlockSpec((ts, LANE), lambda i, h: (i, h))
    shared = lambda col: pl.BlockSpec((ts, LANE), lambda i, h: (i, col))
    gain = pl.BlockSpec((1, LANE), lambda i, h: (0, 0))
    out = jax.ShapeDtypeStruct((t, MH * LANE), BF16)
    return pl.pallas_call(
        body, name=name, grid=(t // ts, MH),
        in_specs=[hd, hd, shared(OFF_KPE // LANE), shared(0), shared(0), gain, gain],
        out_specs=[hd, hd, hd], out_shape=[out, out, out], compiler_params=_params(2),
    )(q_raw, kv, proj, cos_t, sin_t, gq, gk)


def _qk_prep_bwd(dq, dk, dv, q_raw, kv, proj, cos_t, sin_t, gq, gk, *, name, ts=2048):
    t = q_raw.shape[0]
    ts = min(ts, t)

    def norm_bwd(dy, x, g, dg_ref):
        xh, r = _norm96(x, None)
        dg_ref[...] += jnp.sum(dy * xh, axis=0, keepdims=True)
        dxh = dy * g
        return r * (dxh - xh * (jnp.sum(dxh * xh, axis=-1, keepdims=True) * (1.0 / MQK)))

    def body(dq_ref, dk_ref, dv_ref, q_ref, kv_ref, kpe_ref, c_ref, s_ref, gq_ref, gk_ref,
             dqr_ref, dkv_ref, dkpe_ref, dgq_ref, dgk_ref):
        i, h = pl.program_id(0), pl.program_id(1)

        @pl.when(h == 0)
        def _():
            dkpe_ref[...] = jnp.zeros_like(dkpe_ref)

        @pl.when((h == 0) & (i == 0))
        def _():
            dgq_ref[...] = jnp.zeros_like(dgq_ref)
            dgk_ref[...] = jnp.zeros_like(dgk_ref)

        cs, sn = c_ref[...], s_ref[...]
        lane = _lane()
        nope = lane < MNOPE
        dqv = dq_ref[...]
        dqn = dqv * cs + _swap_halves(dqv * sn)
        dqr_ref[...] = norm_bwd(dqn, q_ref[...].astype(F32), gq_ref[...], dgq_ref).astype(BF16)
        dkv_ = dk_ref[...]
        dkn = dkv_ * cs + _swap_halves(dkv_ * sn)
        kvv = kv_ref[...].astype(F32)
        dkr = norm_bwd(dkn, jnp.where(nope, kvv, kpe_ref[...].astype(F32)), gk_ref[...], dgk_ref)
        dkv_ref[...] = jnp.where(nope, dkr, pltpu.roll(dv_ref[...], MNOPE, 1)).astype(BF16)
        dkpe_ref[...] += jnp.where((lane >= MNOPE) & (lane < MQK), dkr, 0.0)

    hd = pl.BlockSpec((ts, LANE), lambda i, h: (i, h))
    shared = lambda col: pl.BlockSpec((ts, LANE), lambda i, h: (i, col))
    gain = pl.BlockSpec((1, LANE), lambda i, h: (0, 0))
    out = jax.ShapeDtypeStruct((t, MH * LANE), BF16)
    return pl.pallas_call(
        body, name=name, grid=(t // ts, MH),
        in_specs=[hd, hd, hd, hd, hd, shared(OFF_KPE // LANE), shared(0), shared(0), gain, gain],
        out_specs=[hd, hd, shared(0), gain, gain],
        out_shape=[out, out, jax.ShapeDtypeStruct((t, LANE), F32), jax.ShapeDtypeStruct((1, LANE), F32),
                   jax.ShapeDtypeStruct((1, LANE), F32)],
        compiler_params=_params(2),
    )(dq, dk, dv, q_raw, kv, proj, cos_t, sin_t, gq, gk)


_NT = (((1,), (1,)), ((), ()))
_TN = (((0,), (0,)), ((), ()))


SOFTMAX_SCALE = MQK ** -0.5
Q_PRESCALE = SOFTMAX_SCALE * float(np.log2(np.e))


def _attn_weights(q, k_ref, lo, tq):
    row = lax.broadcasted_iota(jnp.int32, (tq, tq), 0) // CHUNK
    col = lax.broadcasted_iota(jnp.int32, (tq, tq), 1) // CHUNK
    sd = lax.dot_general(q, k_ref[pl.ds(lo, tq), :], _NT, preferred_element_type=F32)
    sd = jnp.where(col <= row, sd, -1e30)
    m = jnp.max(sd, axis=-1, keepdims=True)
    if lo:
        so = lax.dot_general(q, k_ref[pl.ds(0, lo), :], _NT, preferred_element_type=F32)
        m = jnp.maximum(m, jnp.max(so, axis=-1, keepdims=True))
        eo = jnp.exp2(so - m)
        ed = jnp.exp2(sd - m)
        return eo, ed, 1.0 / (jnp.sum(eo, axis=-1, keepdims=True) + jnp.sum(ed, axis=-1, keepdims=True))
    ed = jnp.exp2(sd - m)
    return None, ed, 1.0 / jnp.sum(ed, axis=-1, keepdims=True)


def _attn_fwd(q, k, v, *, name, tq=256):
    bsz, s, _ = q.shape
    tq = min(tq, s)

    def body(q_ref, k_ref, v_ref, o_ref):
        for i in range(s // tq):
            lo = i * tq
            eo, ed, inv = _attn_weights(q_ref[pl.ds(lo, tq), :], k_ref, lo, tq)
            o = jnp.dot(ed.astype(BF16), v_ref[pl.ds(lo, tq), :], preferred_element_type=F32)
            if lo:
                o += jnp.dot(eo.astype(BF16), v_ref[pl.ds(0, lo), :], preferred_element_type=F32)
            o_ref[pl.ds(lo, tq), :] = (o * inv).astype(BF16)

    spec = pl.BlockSpec((None, s, LANE), lambda b, h: (b, 0, h))
    return pl.pallas_call(
        body, name=name, grid=(bsz, MH), in_specs=[spec, spec, spec], out_specs=spec,
        out_shape=jax.ShapeDtypeStruct((bsz, s, MH * LANE), BF16), compiler_params=_params(2),
    )(q, k, v)


def _attn_bwd(q, k, v, do, *, name, tq=256):
    bsz, s, _ = q.shape
    tq = min(tq, s)

    def body(q_ref, k_ref, v_ref, do_ref, dq_ref, dk_ref, dv_ref):
        dk_ref[...] = jnp.zeros_like(dk_ref)
        dv_ref[...] = jnp.zeros_like(dv_ref)
        for i in range(s // tq):
            lo = i * tq
            here, before = pl.ds(lo, tq), pl.ds(0, lo)
            qv, dov = q_ref[here, :], do_ref[here, :]
            eo, ed, inv = _attn_weights(qv, k_ref, lo, tq)
            do_n = (dov.astype(F32) * inv).astype(BF16)
            dv_ref[here, :] += lax.dot_general(ed.astype(BF16), do_n, _TN, preferred_element_type=F32)
            dpd = lax.dot_general(dov, v_ref[here, :], _NT, preferred_element_type=F32)
            delta = jnp.sum(dpd * ed, axis=-1, keepdims=True)
            if lo:
                dv_ref[before, :] += lax.dot_general(eo.astype(BF16), do_n, _TN, preferred_element_type=F32)
                dpo = lax.dot_general(dov, v_ref[before, :], _NT, preferred_element_type=F32)
                delta += jnp.sum(dpo * eo, axis=-1, keepdims=True)
            delta = delta * inv
            r = inv * SOFTMAX_SCALE
            dsd = (ed * (dpd - delta) * r).astype(BF16)
            dq = jnp.dot(dsd, k_ref[here, :], preferred_element_type=F32)
            dk_ref[here, :] += lax.dot_general(dsd, qv, _TN, preferred_element_type=F32)
            if lo:
                dso = (eo * (dpo - delta) * r).astype(BF16)
                dq += jnp.dot(dso, k_ref[before, :], preferred_element_type=F32)
                dk_ref[before, :] += lax.dot_general(dso, qv, _TN, preferred_element_type=F32)
            dq_ref[here, :] = dq
        dk_ref[...] = dk_ref[...] * (1.0 / Q_PRESCALE)

    spec = pl.BlockSpec((None, s, LANE), lambda b, h: (b, 0, h))
    out = jax.ShapeDtypeStruct((bsz, s, MH * LANE), F32)
    return pl.pallas_call(
        body, name=name, grid=(bsz, MH), in_specs=[spec] * 4, out_specs=[spec] * 3, out_shape=[out, out, out],
        compiler_params=_params(2),
    )(q, k, v, do)


def _adamw(w, g, m, v, *, name, tr=256, by_cols=False):
    rows, cols = w.shape
    tr = _tile_rows(rows, tr)

    def body(w_ref, g_ref, m_ref, v_ref, d_ref, nm_ref, nv_ref):
        d_ref[...], nm_ref[...], nv_ref[...] = _adamw_update(w_ref[...], g_ref[...], m_ref[...], v_ref[...])

    spec = pl.BlockSpec((rows, LANE), lambda i: (0, i)) if by_cols else pl.BlockSpec((tr, cols), lambda i: (i, 0))
    out = jax.ShapeDtypeStruct((rows, cols), F32)
    return pl.pallas_call(body, name=name, grid=(cols // LANE if by_cols else rows // tr,), in_specs=[spec] * 4,
                          out_specs=[spec] * 3, out_shape=[out, out, out], compiler_params=_params(1))(w, g, m, v)


def _tile_rows(rows, target):
    if rows <= target:
        return rows
    best = 8
    for t in range(8, target + 1, 8):
        if rows % t == 0:
            best = t
    return best


def _adamw_update(w, g, m, v):
    nm = ADAM_B1 * m + (1.0 - ADAM_B1) * g
    nv = ADAM_B2 * v + (1.0 - ADAM_B2) * (g * g)
    m_hat = nm / (1.0 - ADAM_B1 ** ADAM_STEP)
    v_hat = nv / (1.0 - ADAM_B2 ** ADAM_STEP)
    return -ADAM_LR * (m_hat / (jnp.sqrt(v_hat) + ADAM_EPS) + ADAM_WD * w), nm, nv


def _adamw_halves(w, m, v, mine, theirs, sel, *, name, tr=256):
    rows, cols = w.shape
    tr = _tile_rows(rows // 2, tr)
    nh = rows // 2 // tr

    def body(sel_ref, w_ref, m_ref, v_ref, mine_ref, theirs_ref, g_ref, d_ref, nm_ref, nv_ref):
        lower = pl.program_id(0) < nh
        south = sel_ref[0] == 0
        gv = jnp.where(lower == south, mine_ref[...], theirs_ref[...])
        g_ref[...] = gv
        d_ref[...], nm_ref[...], nv_ref[...] = _adamw_update(w_ref[...], gv, m_ref[...], v_ref[...])

    full = pl.BlockSpec((tr, cols), lambda i, sel_ref: (i, 0))
    half = pl.BlockSpec((tr, cols), lambda i, sel_ref: (i % nh, 0))
    out = jax.ShapeDtypeStruct((rows, cols), F32)
    return pl.pallas_call(
        body, name=name, out_shape=[out] * 4, compiler_params=_params(1),
        grid_spec=pltpu.PrefetchScalarGridSpec(num_scalar_prefetch=1, grid=(rows // tr,),
                                               in_specs=[full, full, full, half, half], out_specs=[full] * 4),
    )(sel, w, m, v, mine, theirs)


def _pair_add(x, sib, sel, *, name, tr=256):
    n, _, rows, cols = x.shape
    tr = _tile_rows(rows, tr)

    def body(sel_ref, x_ref, s_ref, o_ref):
        o_ref[...] = (x_ref[...] + s_ref[...]).astype(BF16)

    spec = pl.BlockSpec((None, tr, cols), lambda j, i, sel_ref: (j, i, 0))
    return pl.pallas_call(
        body, name=name, out_shape=jax.ShapeDtypeStruct((n, rows, cols), BF16), compiler_params=_params(2),
        grid_spec=pltpu.PrefetchScalarGridSpec(
            num_scalar_prefetch=1, grid=(n, rows // tr),
            in_specs=[pl.BlockSpec((None, None, tr, cols), lambda j, i, sel_ref: (j, sel_ref[0], i, 0)), spec],
            out_specs=spec),
    )(sel, x, sib)


def _chip_sum(pair, recv, sel, *, name, tr=256):
    _, rows, cols = pair.shape
    tr = _tile_rows(rows, tr)

    def body(sel_ref, p_ref, r_ref, o_ref):
        acc = p_ref[...].astype(F32)
        for k in range(3):
            acc = acc + r_ref[k].astype(F32)
        o_ref[...] = acc

    return pl.pallas_call(
        body, name=name, out_shape=jax.ShapeDtypeStruct((rows, cols), F32), compiler_params=_params(1),
        grid_spec=pltpu.PrefetchScalarGridSpec(
            num_scalar_prefetch=1, grid=(rows // tr,),
            in_specs=[pl.BlockSpec((None, tr, cols), lambda i, sel_ref: (sel_ref[0], i, 0)),
                      pl.BlockSpec((3, tr, cols), lambda i, sel_ref: (0, i, 0))],
            out_specs=pl.BlockSpec((tr, cols), lambda i, sel_ref: (i, 0))),
    )(sel, pair, recv)


def _me():
    return lax.axis_index("x"), lax.axis_index("y"), lax.axis_index("c")


def _flip(pos, bits):
    x, y, c = pos
    return (x ^ bits[0] if bits[0] else x, y ^ bits[1] if bits[1] else y, c ^ bits[2] if bits[2] else c)


ANY = pl.BlockSpec(memory_space=pl.ANY)


def _all_gather8(xs, *, name):
    n = len(xs)
    flips = [((k >> 2) & 1, (k >> 1) & 1, k & 1) for k in range(1, 8)]

    def body(*refs):
        x_refs, out_refs, (send_sems, recv_sems, local_sems) = refs[:n], refs[n:2 * n], refs[2 * n:]
        me = _me()
        slot = lambda p: 4 * p[0] + 2 * p[1] + p[2]
        copies = []
        for i in range(n):
            mine = pltpu.make_async_copy(x_refs[i], out_refs[i].at[slot(me)], local_sems.at[i])
            mine.start()
            copies.append(mine)
            for k, f in enumerate(flips):
                peer = _flip(me, f)
                sems = dict(send_sem=send_sems.at[7 * i + k], recv_sem=recv_sems.at[7 * i + k], device_id=peer,
                            device_id_type=MESH)
                cp = pltpu.make_async_remote_copy(src_ref=x_refs[i], dst_ref=out_refs[i].at[slot(me)], **sems)
                cp.start()
                copies.append(cp)
                copies.append(pltpu.make_async_remote_copy(src_ref=x_refs[i], dst_ref=out_refs[i].at[slot(peer)], **sems))
        for i in range(n):
            base = i * 15
            copies[base].wait()
            for k in range(7):
                copies[base + 1 + 2 * k].wait_send()
                copies[base + 2 + 2 * k].wait_recv()

    outs = pl.pallas_call(
        body, name=name, in_specs=[ANY] * n, out_specs=[ANY] * n,
        out_shape=[jax.ShapeDtypeStruct((8, *x.shape), x.dtype) for x in xs],
        scratch_shapes=[pltpu.SemaphoreType.DMA((7 * n,)), pltpu.SemaphoreType.DMA((7 * n,)),
                        pltpu.SemaphoreType.DMA((n,))])(*xs)
    return list(outs)


CHIP_FLIPS = [(1, 0, 0), (0, 1, 0), (1, 1, 0)]


def _chip():
    return 2 * lax.axis_index("x") + lax.axis_index("y")


HBM = pl.BlockSpec(memory_space=pltpu.HBM)
SEM = pl.BlockSpec(memory_space=pltpu.SEMAPHORE)
EFFECT = pltpu.SideEffectType.DATAFLOW_SIDE_EFFECTING


def _plan_copies(plan, refs, send_sems, recv_sems):
    return [pltpu.make_async_remote_copy(src_ref=src, dst_ref=dst, send_sem=send_sems.at[k], recv_sem=recv_sems.at[k],
                                         device_id=to, device_id_type=MESH) for k, (src, dst, to) in enumerate(plan(refs))]


def _rdma_start(arrays, n_copies, plan, deps, *, name):
    n, nd = len(arrays), len(deps)

    def body(*refs):
        for cp in _plan_copies(plan, refs[:n], refs[n + nd], refs[n + nd + 1]):
            cp.start()
        refs[-1][...] = jnp.zeros_like(refs[-1])

    outs = pl.pallas_call(
        body, name=name,
        out_shape=(pltpu.SemaphoreType.DMA((n_copies,)), pltpu.SemaphoreType.DMA((n_copies,)),
                   *[pltpu.HBM(a.shape, a.dtype) for a in arrays], jax.ShapeDtypeStruct((8, LANE), F32)),
        in_specs=[HBM] * n + [ANY] * nd, out_specs=(SEM, SEM, *[HBM] * n, pl.BlockSpec(memory_space=pltpu.VMEM)),
        input_output_aliases={i: i + 2 for i in range(n)}, compiler_params=pltpu.CompilerParams(has_side_effects=EFFECT),
    )(*[pltpu.with_memory_space_constraint(a, pltpu.HBM) for a in arrays], *deps)
    return outs[0], outs[1], list(outs[2:2 + n]), outs[-1]


def _rdma_wait(send_sems, recv_sems, arrays, plan, after, *, name):
    n = len(arrays)

    def body(*refs):
        for cp in _plan_copies(plan, refs[:n], refs[n], refs[n + 1]):
            cp.wait_send()
            cp.wait_recv()

    return list(pl.pallas_call(
        body, name=name, out_shape=tuple(pltpu.HBM(a.shape, a.dtype) for a in arrays),
        in_specs=[HBM] * n + [SEM, SEM, ANY], out_specs=tuple([HBM] * n), input_output_aliases={i: i for i in range(n)},
        compiler_params=pltpu.CompilerParams(has_side_effects=EFFECT),
    )(*arrays, send_sems, recv_sems, after))


def _gather_plan(n):
    def plan(refs):
        me = _me()
        slot = 2 * me[0] + me[1]
        return [(refs[i].at[me[2]], refs[n + i].at[slot, me[2]], _flip(me, f)) for i in range(n) for f in CHIP_FLIPS]
    return plan


def _scatter_plan(n):
    def plan(refs):
        me = _me()
        out = []
        for i in range(n):
            for k, f in enumerate(CHIP_FLIPS):
                peer = _flip(me, f)
                out.append((refs[i].at[2 * peer[0] + peer[1]], refs[n + i].at[k], peer))
        return out
    return plan


def _sibling_plan(n, src_of):
    def plan(refs):
        me = _me()
        return [(src_of(refs[i], me[2]), refs[n + i], _flip(me, (0, 0, 1))) for i in range(n)]
    return plan


def _gather8_plan(n):
    def plan(refs):
        me = _me()
        slot = 4 * me[0] + 2 * me[1] + me[2]
        return [(refs[i], refs[n + i].at[slot], _flip(me, ((k >> 2) & 1, (k >> 1) & 1, k & 1)))
                for i in range(n) for k in range(1, 8)]
    return plan


def _pair_fill(lands, *, name):
    n = len(lands)

    def body(*refs):
        in_refs, (send_sems, recv_sems) = refs[:n], refs[2 * n:]
        me = _me()
        sib = _flip(me, (0, 0, 1))
        copies = []
        for i in range(n):
            for k, f in enumerate(CHIP_FLIPS):
                peer = _flip(me, f)
                slot = 2 * peer[0] + peer[1]
                mine, theirs = in_refs[i].at[slot, me[2]], in_refs[i].at[slot, 1 - me[2]]
                cp = pltpu.make_async_remote_copy(src_ref=mine, dst_ref=mine, send_sem=send_sems.at[3 * i + k],
                                                  recv_sem=recv_sems.at[3 * i + k], device_id=sib, device_id_type=MESH)
                cp.start()
                copies.append((cp, pltpu.make_async_remote_copy(
                    src_ref=mine, dst_ref=theirs, send_sem=send_sems.at[3 * i + k], recv_sem=recv_sems.at[3 * i + k],
                    device_id=sib, device_id_type=MESH)))
        for cp, arrival in copies:
            arrival.wait_recv()
            cp.wait_send()

    return list(pl.pallas_call(
        body, name=name, in_specs=[ANY] * n, out_specs=[ANY] * n,
        out_shape=[jax.ShapeDtypeStruct(a.shape, a.dtype) for a in lands], input_output_aliases={i: i for i in range(n)},
        scratch_shapes=[pltpu.SemaphoreType.DMA((3 * n,)), pltpu.SemaphoreType.DMA((3 * n,))])(*lands))


def _own_and_landed(lands, xs):
    chip = _chip()
    return [[jnp.where(chip == j, x, o.reshape(4, *x.shape)[j]) for j in range(4)] for o, x in zip(lands, xs)]


BIG = (("w_in", (D, IN_WIDTH // 4), 1), ("gla_w_o", (D // 4, D), 0), ("mla_w_uq", (MQR, MH * MQK // 4), 1),
       ("mla_w_ukv", (MKVR, MH * (MNOPE + MVD) // 4), 1), ("mla_w_o", (D // 4, D), 0), ("w_out", (D // 4, D), 0),
       ("mlp_w1", (D, DFF // 4), 1), ("mlp_w2", (DFF // 4, D), 0))
ADA_SHARD = (D, 6 * D // 4)
SMALL = (("b_ada", 6 * D), ("norm1_g", D), ("b_merge", 2 * D), ("gla_b_alpha", GH * GDK), ("gla_out_norm_g", GDV),
         ("mla_q_lat_g", MQR), ("mla_kv_lat_g", MKVR), ("mla_qn_g", MQK), ("mla_kn_g", MQK), ("norm2_g", D))


W_IN_SEGMENTS = ((0, 3072, OFF_Q), (3072, 3088, OFF_A), (3088, 3344, OFF_CQ), (3344, 3472, OFF_CKV),
                 (3472, 3504, OFF_KPE + MNOPE), (3504, 5552, OFF_MA))
W_IN_SPLIT = OFF_MA
SMALL_ROWS, SMALL_COLS = 32, 2 * D
W_ALPHA_ROW = 16
LOSS_ROW = 15
SMALL_RED = tuple((n, k) for n, k in SMALL if n != "b_ada")


def _pack_small(grads, d_w_alpha, loss_row, *, name):
    def body(*refs):
        g_refs, wa_ref, loss_ref, out_ref = refs[:-3], refs[-3], refs[-2], refs[-1]
        out_ref[...] = jnp.zeros_like(out_ref)
        for i, ((_, k), g_ref) in enumerate(zip(SMALL_RED, g_refs)):
            out_ref[i:i + 1, 0:k] = g_ref[...]
        out_ref[LOSS_ROW:LOSS_ROW + 1, 0:LANE] = loss_ref[...]
        out_ref[W_ALPHA_ROW:W_ALPHA_ROW + GLR, 0:GH * GDK] = wa_ref[...]

    return pl.pallas_call(body, name=name, out_shape=jax.ShapeDtypeStruct((SMALL_ROWS, SMALL_COLS), F32))(
        *grads, d_w_alpha, loss_row)


def _small_update(gathered, dmod_all, sel, wmv, *, name):
    names = [n for n, _ in SMALL] + ["gla_w_alpha"]
    n_par = len(names)

    def body(sel_ref, g_ref, dmod_ref, *refs):
        in_refs, out_refs, loss_ref, acc = refs[:3 * n_par], refs[3 * n_par:-2], refs[-2], refs[-1]
        total = g_ref[0]
        for j in range(1, 8):
            total = total + g_ref[j]
        acc[...] = total
        loss_ref[...] = acc[LOSS_ROW:LOSS_ROW + 1, 0:LANE]
        row = {n: i for i, (n, _) in enumerate(SMALL_RED)}
        for p, name_p in enumerate(names):
            w_ref, m_ref, v_ref = in_refs[3 * p:3 * p + 3]
            if name_p == "b_ada":
                gv = jnp.sum(dmod_ref[...], axis=0, keepdims=True)
            elif name_p == "gla_w_alpha":
                gv = jnp.zeros((GLR, GDK), F32)
                for j in range(4):
                    blk = acc[W_ALPHA_ROW:W_ALPHA_ROW + GLR, j * GDK:(j + 1) * GDK]
                    gv = gv + jnp.where(sel_ref[0] == j, blk, 0.0)
            else:
                gv = acc[row[name_p]:row[name_p] + 1, 0:w_ref.shape[1]]
            o = out_refs[4 * p:4 * p + 4]
            o[0][...] = gv
            o[1][...], o[2][...], o[3][...] = _adamw_update(w_ref[...], gv, m_ref[...], v_ref[...])

    flat = [a for t in wmv for a in t]
    out_shape = [jax.ShapeDtypeStruct(t[0].shape, F32) for t in wmv for _ in range(4)]
    out_shape.append(jax.ShapeDtypeStruct((1, LANE), F32))
    vmem = pl.BlockSpec(memory_space=pltpu.VMEM)
    outs = pl.pallas_call(
        body, name=name, out_shape=out_shape, in_specs=[pl.BlockSpec(memory_space=pltpu.SMEM), vmem, vmem] + [vmem] * len(flat),
        out_specs=[vmem] * len(out_shape), scratch_shapes=[pltpu.VMEM((SMALL_ROWS, SMALL_COLS), F32)],
    )(sel, gathered, dmod_all, *flat)
    return {n: tuple(outs[4 * p:4 * p + 4]) for p, n in enumerate(names)}, outs[-1][0, 0]


def _full_weights(gathered):
    w = {name: jnp.concatenate(gathered[name], axis=axis) for name, _, axis in BIG if name in gathered and name != "w_in"}
    if "w_in" in gathered:
        shards = gathered["w_in"]
        zeros = lambda n: [jnp.zeros((D, n), shards[0].dtype)]

        def cols(a, b):
            width = IN_WIDTH // 4
            return [shards[j][:, max(a, j * width) - j * width:min(b, (j + 1) * width) - j * width]
                    for j in range(4) if max(a, j * width) < min(b, (j + 1) * width)]

        parts = []
        for a, b, at in sorted(W_IN_SEGMENTS, key=lambda seg: seg[2]):
            have = sum(p.shape[1] for p in parts)
            parts += (zeros(at - have) if at > have else []) + cols(a, b)
        w["w_in"] = jnp.concatenate(parts + zeros(PW - sum(p.shape[1] for p in parts)), axis=1)
    if "mla_w_uq" in w:
        w["mla_w_uq"] = jnp.pad(w["mla_w_uq"].reshape(MQR, MH, MQK), ((0, 0), (0, 0), (0, LANE - MQK))).reshape(MQR, MH * LANE)
    if "mla_w_o" in w:
        w["mla_w_o"] = jnp.pad(w["mla_w_o"].reshape(MH, MVD, D), ((0, 0), (0, LANE - MVD), (0, 0))).reshape(MH * LANE, D)
    return w


def _grad_slots(g):
    g = dict(g)
    out = {}
    if "w_in" in g:
        g_lo, g_hi = g.pop("w_in")
        take = lambda at, lo, hi: g_lo[:, at + lo:at + hi] if at < W_IN_SPLIT else g_hi[:, at - W_IN_SPLIT + lo:at - W_IN_SPLIT + hi]
        width = IN_WIDTH // 4
        slots = []
        for j in range(4):
            lo, hi = j * width, (j + 1) * width
            slots.append(jnp.concatenate([take(at, max(lo, a) - a, min(hi, b) - a)
                                          for a, b, at in W_IN_SEGMENTS if max(lo, a) < min(hi, b)], axis=1))
        out["w_in"] = jnp.stack(slots).reshape(4, 2, D // 2, width)
    if "mla_w_uq" in g:
        g["mla_w_uq"] = g["mla_w_uq"].reshape(MQR, MH, LANE)[:, :, :MQK].reshape(MQR, MH * MQK)
    if "mla_w_o" in g:
        g["mla_w_o"] = g["mla_w_o"].reshape(MH, LANE, D)[:, :MVD].reshape(MH * MVD, D)
    for name, (rows, cols), axis in BIG:
        if name not in g:
            continue
        a = g[name]
        a = a.reshape(4, rows, cols) if axis == 0 else jnp.transpose(a.reshape(rows, 4, cols), (1, 0, 2))
        out[name] = a.reshape(4, 2, rows // 2, cols)
    return out


def _rope_tables(positions):
    freqs = ROPE_THETA ** (-jnp.arange(0, MROPE, 2, dtype=F32) / MROPE)
    lane = np.arange(LANE)
    in_rope = (lane >= MNOPE) & (lane < MQK)
    freq_lane = jnp.where(in_rope, freqs[(lane - MNOPE) % (MROPE // 2)], 0.0)
    sign = np.where(in_rope, np.where(lane < MNOPE + MROPE // 2, -1.0, 1.0), 0.0).astype(np.float32)
    ang = positions.astype(F32).reshape(-1, 1) * freq_lane[None, :]
    return jnp.cos(ang), jnp.sin(ang) * sign[None, :]


def _local_step(x, positions, mod, target, w, small, more_weights=None, on_grads=None):
    kept = {}
    if on_grads is None:
        on_grads = lambda tag, grads, after: kept.update(grads)
    bsz, s, _ = x.shape
    t = bsz * s
    tt = _tile(t, 1024)
    shift1, scale1, gate1, shift2, scale2, gate2 = [mod[:, None, i * D:(i + 1) * D] for i in range(6)]
    cos_t, sin_t = _rope_tables(positions)
    w_alpha_p = jnp.pad(small["gla_w_alpha"], ((0, LANE - GLR), (0, 0)))
    gq = jnp.pad(small["mla_qn_g"], ((0, 0), (0, LANE - MQK)))
    gk = jnp.pad(small["mla_kn_g"], ((0, 0), (0, LANE - MQK)))
    flat2 = lambda a: a.reshape(t, a.shape[-1])
    bsd = lambda a: a.reshape(bsz, s, a.shape[-1])

    h = _norm_mod(x, small["norm1_g"], scale1, shift1, name="norm1")
    if callable(w):
        w = w(h)
    proj = _mm(flat2(h), w["w_in"], name="proj", tn=1152, out_dtype=BF16)
    proj3 = bsd(proj)
    o, o_gated, states = _gla_fwd(proj3, w_alpha_p, small["gla_b_alpha"], small["gla_out_norm_g"], name="gla_fwd")
    if more_weights is not None:
        w = {**w, **more_weights(o_gated)}
    y_a = _mm(flat2(o_gated), w["gla_w_o"], name="gla_out", out_dtype=BF16)
    cq_n, ckv_n = _lat_norm(proj, small["mla_q_lat_g"], small["mla_kv_lat_g"], name="lat_norm")
    q_raw = _mm(cq_n, w["mla_w_uq"], name="mla_uq", out_dtype=BF16)
    kv = _mm(ckv_n, w["mla_w_ukv"], name="mla_ukv", out_dtype=BF16)
    qf, kf, vf = _qk_prep(q_raw, kv, proj, cos_t, sin_t, gq * Q_PRESCALE, gk, name="qk_prep")
    o_attn = _attn_fwd(bsd(qf), bsd(kf), bsd(vf), name="attn_fwd")
    y_b = _mm(flat2(o_attn), w["mla_w_o"], name="mla_out", out_dtype=BF16)
    mixed_in = _merge_fwd(proj3, small["b_merge"], bsd(y_a), bsd(y_b), name="merge_fwd")
    mixed = _mm(flat2(mixed_in), w["w_out"], name="w_out")
    x1, h2 = _resid_norm_mod(x, bsd(mixed), gate1, small["norm2_g"], scale2, shift2, name="norm2")

    def sqrelu(acc, ex, outs):
        r = jnp.maximum(acc, 0.0)
        outs[0][...] = (r * r).astype(BF16)

    r = _mm(flat2(h2), w["mlp_w1"], name="mlp1", epilogue=sqrelu, out_shape=jax.ShapeDtypeStruct((t, DFF), BF16),
            out_specs=_tile_spec(tt, 1024))
    ff = _mm(r, w["mlp_w2"], name="mlp2")
    dy, dff, dgate2, loss_part = _loss_head(x1, bsd(ff), gate2, target, name="loss_head")

    g = {}

    def relu2_bwd(acc, ex, outs):
        outs[0][...] = (acc * (2.0 * jnp.sqrt(ex[0][...].astype(F32)))).astype(BF16)

    dff2 = flat2(dff)
    da1 = _mm(dff2, w["mlp_w2"], tb=True, name="mlp2_dx", epilogue=relu2_bwd, extras=(r,),
              extra_specs=(_tile_spec(tt, 1024),), out_shape=jax.ShapeDtypeStruct((t, DFF), BF16),
              out_specs=_tile_spec(tt, 1024))
    g["mlp_w2"] = _mm(r, dff2, ta=True, name="mlp2_dw")
    dh2 = _mm(da1, w["mlp_w1"], tb=True, name="mlp1_dx")
    g["mlp_w1"] = _mm(flat2(h2), da1, ta=True, name="mlp1_dw")
    token = on_grads("mlp", {n: g.pop(n) for n in ("mlp_w2", "mlp_w1")}, dh2)
    if token is not None:
        gate1 = gate1 + token[0, 0]
    dx1, dscale2, dshift2, dg2, dgate1, dmixed = _norm_mod_bwd(
        bsd(dh2), x1, dy, small["norm2_g"], scale2, gate1, bsd(mixed), name="norm2_bwd")
    dmixed2 = flat2(dmixed)
    dmi = _mm(dmixed2, w["w_out"], tb=True, name="w_out_dx", out_dtype=BF16)
    g["w_out"] = _mm(flat2(mixed_in), dmixed2, ta=True, name="w_out_dw")
    dy_a, dy_b, dl_a, dl_b, db_a, db_b = _merge_bwd(bsd(dmi), proj3, small["b_merge"], bsd(y_a), bsd(y_b), name="merge_bwd")
    dy_a2, dy_b2 = flat2(dy_a), flat2(dy_b)
    dog = _mm(dy_a2, w["gla_w_o"], tb=True, name="gla_out_dx")
    g["gla_w_o"] = _mm(flat2(o_gated), dy_a2, ta=True, name="gla_out_dw")
    dq_g, dk_g, dv_g, dg_g, dlog, db_alpha, d_ong = _gla_bwd(
        bsd(dog), o, states, proj3, w_alpha_p, small["gla_b_alpha"], small["gla_out_norm_g"], name="gla_bwd")
    dlog2 = flat2(dlog)
    da_p = _mm(dlog2, w_alpha_p, tb=True, out_dtype=BF16, name="alpha_dx")
    d_w_alpha = _mm(proj[:, OFF_A:OFF_A + LANE], dlog2, ta=True, name="alpha_dw")[:GLR]
    do_attn = _mm(dy_b2, w["mla_w_o"], tb=True, out_dtype=BF16, name="mla_out_dx")
    g["mla_w_o"] = _mm(flat2(o_attn), dy_b2, ta=True, name="mla_out_dw")
    dqf, dkf, dvf = _attn_bwd(bsd(qf), bsd(kf), bsd(vf), bsd(do_attn), name="attn_bwd")
    dq_raw, dkv, dkpe, dgq, dgk = _qk_prep_bwd(flat2(dqf), flat2(dkf), flat2(dvf), q_raw, kv, proj, cos_t, sin_t, gq, gk,
                                                name="qk_prep_bwd")
    dcq_n = _mm(dq_raw, w["mla_w_uq"], tb=True, name="mla_uq_dx")
    g["mla_w_uq"] = _mm(cq_n, dq_raw, ta=True, name="mla_uq_dw")
    dckv_n = _mm(dkv, w["mla_w_ukv"], tb=True, name="mla_ukv_dx")
    g["mla_w_ukv"] = _mm(ckv_n, dkv, ta=True, name="mla_ukv_dw")
    token = on_grads("mix", {n: g.pop(n) for n in ("w_out", "gla_w_o", "mla_w_o", "mla_w_uq", "mla_w_ukv")}, dckv_n)
    q_lat_g = small["mla_q_lat_g"] if token is None else small["mla_q_lat_g"] + token[0:1, 0:1]
    dcq, dckv, dg_qlat, dg_kvlat = _lat_norm_bwd(dcq_n, dckv_n, proj, q_lat_g, small["mla_kv_lat_g"],
                                                  name="lat_norm_bwd")
    pieces = [(flat2(dq_g), OFF_Q), (flat2(dk_g), OFF_K), (flat2(dv_g), OFF_V), (flat2(dg_g), OFF_G),
              (flat2(dl_a), OFF_MA), (flat2(dl_b), OFF_MB), (dcq, OFF_CQ), (dckv, OFF_CKV), (da_p, OFF_A), (dkpe, OFF_KPE)]
    hb = flat2(h)
    g_w_in = (_pieces_dw(hb, [p for p, off in pieces if off < W_IN_SPLIT], name="proj_dw_a"),
              _pieces_dw(hb, [p for p, off in pieces if off >= W_IN_SPLIT], name="proj_dw_b"))
    token = on_grads("in", {"w_in": g_w_in}, g_w_in[1])
    after = jnp.zeros((8, LANE), F32) if token is None else token
    dh = _pieces_dx(pieces, w["w_in"], after, name="proj_dx")
    token = on_grads("dx", {}, dh)
    if token is not None:
        scale1 = scale1 + token[0, 0]
    grad_x, dscale1, dshift1, dg1 = _norm_mod_bwd(bsd(dh), x, dx1, small["norm1_g"], scale1, name="norm1_bwd")

    dmod = jnp.concatenate([dshift1, dscale1, dgate1, dshift2, dscale2, dgate2], axis=-1).reshape(bsz, 6 * D)
    gs = {"norm1_g": dg1, "b_merge": jnp.concatenate([db_a, db_b], axis=1), "gla_b_alpha": db_alpha,
          "gla_out_norm_g": d_ong, "mla_q_lat_g": dg_qlat, "mla_kv_lat_g": dg_kvlat, "mla_qn_g": dgq[:, :MQK],
          "mla_kn_g": dgk[:, :MQK], "norm2_g": dg2}
    return loss_part[0, 0], grad_x, dmod, {**kept, **g}, gs, d_w_alpha


def kernel(x, c, positions, w_ada, b_ada, norm1_g, w_in, b_merge, gla_w_alpha, gla_b_alpha, gla_out_norm_g, gla_w_o, mla_q_lat_g, mla_w_uq, mla_kv_lat_g, mla_w_ukv, mla_qn_g, mla_kn_g, mla_w_o, w_out, norm2_g, mlp_w1, mlp_w2, loss_target, m_w_ada, m_b_ada, m_norm1_g, m_w_in, m_b_merge, m_gla_w_alpha, m_gla_b_alpha, m_gla_out_norm_g, m_gla_w_o, m_mla_q_lat_g, m_mla_w_uq, m_mla_kv_lat_g, m_mla_w_ukv, m_mla_qn_g, m_mla_kn_g, m_mla_w_o, m_w_out, m_norm2_g, m_mlp_w1, m_mlp_w2, v_w_ada, v_b_ada, v_norm1_g, v_w_in, v_b_merge, v_gla_w_alpha, v_gla_b_alpha, v_gla_out_norm_g, v_gla_w_o, v_mla_q_lat_g, v_mla_w_uq, v_mla_kv_lat_g, v_mla_w_ukv, v_mla_qn_g, v_mla_kn_g, v_mla_w_o, v_w_out, v_norm2_g, v_mlp_w1, v_mlp_w2):
    args = dict(locals())
    names_big = [n for n, _, _ in BIG]
    names_small = [n for n, _ in SMALL]
    bsz = x.shape[0]
    ax, ay, ac = lax.axis_index("x"), lax.axis_index("y"), lax.axis_index("c")
    chip = 2 * ax + ay
    dev = 2 * chip + ac

    small = {n: args[n] for n in names_small}
    sel_c = jnp.reshape(ac, (1,)).astype(jnp.int32)
    sel_chip = jnp.reshape(chip, (1,)).astype(jnp.int32)
    c_all, w_alpha_all = _all_gather8([c, gla_w_alpha[0]], name="comm_c_alpha")
    small["gla_w_alpha"] = jnp.concatenate([w_alpha_all[2 * j] for j in range(4)], axis=1)
    c_all = c_all.reshape(8 * bsz, D)

    shards = {n: args[n][0].astype(BF16) for n in names_big}
    halves_of = lambda names: [shards[n].reshape(2, shards[n].shape[0] // 2, shards[n].shape[1]) for n in names]

    def gather_start(names, deps, tag):
        xs = halves_of(names)
        lands = [lax.empty((4, *xh.shape), BF16) for xh in xs]
        plan = _gather_plan(len(names))
        return names, plan, _rdma_start(xs + lands, 3 * len(names), plan, deps, name="comm_weights_start_" + tag)

    def gather_finish(started, after, tag):
        names, plan, sems = started
        arrs = _rdma_wait(sems[0], sems[1], sems[2], plan, after, name="comm_weights_wait_" + tag)
        filled = _pair_fill(arrs[len(names):], name="comm_weights_pair_" + tag)
        own = [a.reshape(shards[n].shape) for n, a in zip(names, arrs)]
        return _full_weights(dict(zip(names, _own_and_landed(filled, own))))


    def add_bias(acc, ex, outs):
        outs[0][...] = acc + ex[0][...]

    silu = lambda v: v * _sigmoid(v)
    b_ada_mine = lax.dynamic_slice(b_ada, (0, chip * ADA_SHARD[1]), (1, ADA_SHARD[1]))
    mod_part = _mm(c_all, w_ada[0], name="ada", tn=512, a_fn=silu, epilogue=add_bias, extras=(b_ada_mine,),
                   extra_specs=(pl.BlockSpec((1, 512), lambda i, j, k: (0, j)),),
                   out_shape=jax.ShapeDtypeStruct((8 * bsz, ADA_SHARD[1]), F32), out_specs=_tile_spec(8 * bsz, 512))
    mod_all = _all_gather8([mod_part], name="comm_mod")[0]
    mod_rows = lax.dynamic_slice(mod_all, (0, dev * bsz, 0), (8, bsz, ADA_SHARD[1]))
    mod = jnp.concatenate([mod_rows[2 * j] for j in range(4)], axis=1)
    first = gather_start(["w_in"], (mod,), "in")
    rest = gather_start([n for n in names_big if n != "w_in"], (mod, first[2][3]), "rest")
    mod = mod + rest[2][3][0, 0]
    w_in_after = lambda after: gather_finish(first, after, "in")
    more_weights = lambda after: gather_finish(rest, after, "rest")

    stage = {}

    def begin(tag, names, arrays, lands, n_copies, plan, what):
        stage[tag] = (names, plan, _rdma_start(arrays + lands, n_copies, plan, (), name=f"comm_{what}_start_{tag}"))
        return stage[tag][2][3]

    def landed(tag, after, what):
        names, plan, sems = stage[tag]
        arrs = _rdma_wait(sems[0], sems[1], sems[2], plan, after, name=f"comm_{what}_wait_{tag}")
        return names, arrs[:len(arrs) // 2], arrs[len(arrs) // 2:]

    def swap_start(tag, grads):
        names = list(grads)
        parts = [_grad_slots(grads)[n] for n in names]
        lands = [lax.empty((4, *p.shape[2:]), F32) for p in parts]
        return begin(tag, names, parts, lands, len(names), _sibling_plan(len(names), lambda r, c: r.at[:, 1 - c]), "pair_sum")

    def scatter_start(tag, after):
        names, parts, sib_halves = landed(tag, after, "pair_sum")
        pairs = [_pair_add(p, s, sel_c, name="pair_add_" + n) for n, p, s in zip(names, parts, sib_halves)]
        recvs = [lax.empty((3, *p.shape[1:]), BF16) for p in pairs]
        return begin(tag, names, pairs, recvs, 3 * len(names), _scatter_plan(len(names)), "scatter")

    def join_start(tag, after):
        names, pairs, recvs = landed(tag, after, "scatter")
        halves = [_chip_sum(p, r, sel_chip, name="chip_sum_" + n) for n, p, r in zip(names, pairs, recvs)]
        lands = [lax.empty(h.shape, F32) for h in halves]
        return begin(tag, names, halves, lands, len(names), _sibling_plan(len(names), lambda r, c: r), "pair_join")

    def reduce_step(tag, grads, after):
        if tag == "mlp":
            return swap_start("mlp", grads)
        if tag == "mix":
            return scatter_start("mlp", after) + swap_start("mix", grads)
        if tag == "in":
            return scatter_start("mix", after) + swap_start("in", grads)
        return scatter_start("in", after)

    loss_part, grad_x, dmod, g, gs, d_w_alpha = _local_step(x, positions, mod, loss_target, w_in_after, small,
                                                            more_weights, reduce_step)

    assert not g, list(g)
    gs_packed = _pack_small([gs[n] for n, _ in SMALL_RED], d_w_alpha, jnp.full((1, LANE), loss_part, F32),
                            name="pack_small")
    small_lands = [lax.empty((8, *a.shape), F32) for a in (dmod, gs_packed)]
    begin("small", ["dmod", "small"], [dmod, gs_packed], small_lands, 7 * 2, _gather8_plan(2), "gather8")

    res = {}

    def finish(tag, after):
        names, halves, theirs = landed(tag, after, "pair_join")
        for n, mine, other in zip(names, halves, theirs):
            if n == "w_in":
                south = ac == 0
                g_t = jnp.concatenate([jnp.where(south, mine, other), jnp.where(south, other, mine)], axis=0).T
                outs = _adamw(w_in[0].T, g_t, m_w_in[0].T, v_w_in[0].T, name="adamw_w_in", by_cols=True)
                res[n] = tuple(a.T for a in (g_t, *outs))
            else:
                res[n] = _adamw_halves(args[n][0], args["m_" + n][0], args["v_" + n][0], mine, other, sel_c,
                                       name="adamw_" + n)
        return res[names[-1]][1]

    join_start("mlp", grad_x)
    join_start("mix", grad_x)
    done = finish("mix", finish("mlp", grad_x))

    _, (dmod_own, gs_own), (dmod_all, gs_all) = landed("small", done, "gather8")
    dmod_all = lax.dynamic_update_slice(dmod_all, dmod_own[None], (dev, 0, 0)).reshape(8 * bsz, 6 * D)
    gs_all = lax.dynamic_update_slice(gs_all, gs_own[None], (dev, 0, 0))
    dmod_mine = lax.dynamic_slice(dmod_all, (0, chip * ADA_SHARD[1]), (8 * bsz, ADA_SHARD[1]))
    g_w_ada = _mm(c_all, dmod_mine, ta=True, a_fn=silu, name="ada_dw")
    wmv = [(args[n], args["m_" + n], args["v_" + n]) for n in names_small]
    wmv.append((gla_w_alpha[0], m_gla_w_alpha[0], v_gla_w_alpha[0]))
    res_small, loss_sum = _small_update(gs_all, dmod_all, sel_chip, wmv, name="small_update")
    res.update(res_small)
    loss = loss_sum * (0.5 / D)
    join_start("in", g_w_ada)
    res["w_ada"] = (g_w_ada, *_adamw(w_ada[0], g_w_ada, m_w_ada[0], v_w_ada[0], name="adamw_w_ada"))
    finish("in", res["w_ada"][1])

    order = ["w_ada", "b_ada", "norm1_g", "w_in", "b_merge", "gla_w_alpha", "gla_b_alpha", "gla_out_norm_g", "gla_w_o",
             "mla_q_lat_g", "mla_w_uq", "mla_kv_lat_g", "mla_w_ukv", "mla_qn_g", "mla_kn_g", "mla_w_o", "w_out",
             "norm2_g", "mlp_w1", "mlp_w2"]
    named = lambda k: [res[n][k].reshape(args[n].shape) for n in order]
    return (loss, grad_x, *named(0), *named(1), *named(2), *named(3))
```
